```python
import jax, jax.numpy as jnp
from jax import lax
import numpy as np

D_MODEL = 1024
BATCH = 16
SEQ = 2048
DEPTH = 4

N_MIXERS = 2
N_MLA_LAYERS = (DEPTH + 1) // 2
N_GMLP_LAYERS = DEPTH // 2
MLA_HEADS = 8
QK_NOPE_DIM = 128
QK_ROPE_DIM = 64
V_HEAD_DIM = 128
Q_LORA_RANK = 384
KV_LORA_RANK = 256
ROPE_BASE = 10000.0
Q_BLOCK = 128
GMLP_CHUNK = 128
GMLP_HALF = 2 * D_MODEL
GMLP_GROUPS = 8
GMLP_GROUP_DIM = GMLP_HALF // GMLP_GROUPS
D_FF = 4 * D_MODEL
PLE_DIM = 256
NORM_EPS = 1e-6
MAX_POS_OFFSET = 4096

kernel_name = "hybrid_mla_chunked_gmlp_trunk"


def rms_norm(x, g):
    xf = x.astype(jnp.float32)
    y = xf * lax.rsqrt(jnp.mean(xf * xf, axis=-1, keepdims=True) + NORM_EPS)
    return (y * g.astype(jnp.float32)).astype(x.dtype)


def layer_norm(x, g, b):
    xf = x.astype(jnp.float32)
    mu = jnp.mean(xf, axis=-1, keepdims=True)
    xc = xf - mu
    y = xc * lax.rsqrt(jnp.mean(xc * xc, axis=-1, keepdims=True) + NORM_EPS)
    return (y * g.astype(jnp.float32) + b.astype(jnp.float32)).astype(x.dtype)


def rope_cos_sin(positions):
    inv_freq = ROPE_BASE ** (-(jnp.arange(0, QK_ROPE_DIM, 2, dtype=jnp.float32) / QK_ROPE_DIM))
    ang = positions.astype(jnp.float32)[..., None] * inv_freq
    return jnp.cos(ang), jnp.sin(ang)


def apply_rope(x, cos, sin):
    x1, x2 = jnp.split(x.astype(jnp.float32), 2, axis=-1)
    return jnp.concatenate([x1 * cos - x2 * sin, x2 * cos + x1 * sin], axis=-1).astype(x.dtype)


def mla_mixer(hn, cos, sin, w_down, q_lora_g, kv_lora_g, w_uq, w_ukv,
              q_nope_g, q_rope_g, k_nope_g, k_rope_g, w_out):
    B, S, _ = hn.shape
    H = MLA_HEADS
    lat = hn @ w_down
    c_q, c_kv, k_rope = jnp.split(lat, [Q_LORA_RANK, Q_LORA_RANK + KV_LORA_RANK], axis=-1)
    c_q = rms_norm(c_q, q_lora_g)
    c_kv = rms_norm(c_kv, kv_lora_g)
    q = (c_q @ w_uq).reshape(B, S, H, QK_NOPE_DIM + QK_ROPE_DIM)
    q_nope, q_rope = jnp.split(q, [QK_NOPE_DIM], axis=-1)
    kv = (c_kv @ w_ukv).reshape(B, S, H, QK_NOPE_DIM + V_HEAD_DIM)
    k_nope, v = jnp.split(kv, [QK_NOPE_DIM], axis=-1)
    q_nope = rms_norm(q_nope, q_nope_g)
    q_rope = apply_rope(rms_norm(q_rope, q_rope_g), cos[:, :, None], sin[:, :, None])
    k_nope = rms_norm(k_nope, k_nope_g)
    k_rope = apply_rope(rms_norm(k_rope, k_rope_g), cos, sin)
    scale = (QK_NOPE_DIM + QK_ROPE_DIM) ** -0.5
    outs = []
    for j in range(S // Q_BLOCK):
        q0 = j * Q_BLOCK
        kend = q0 + Q_BLOCK
        s = (jnp.einsum('bqhd,bkhd->bhqk', q_nope[:, q0:kend], k_nope[:, :kend])
             + jnp.einsum('bqhr,bkr->bhqk', q_rope[:, q0:kend], k_rope[:, :kend]))
        s = s.astype(jnp.float32) * scale
        causal = jnp.arange(kend)[None, :] <= (q0 + jnp.arange(Q_BLOCK))[:, None]
        s = jnp.where(causal, s, -jnp.inf)
        pr = jax.nn.softmax(s, axis=-1).astype(v.dtype)
        outs.append(jnp.einsum('bhqk,bkhd->bqhd', pr, v[:, :kend]))
    o = jnp.concatenate(outs, axis=1).reshape(B, S, H * V_HEAD_DIM)
    return o @ w_out


def gmlp_mixer(hn, w_in, ln_g, ln_b, w_s, b_s, w_out):
    B, S, _ = hn.shape
    z = jax.nn.gelu(hn @ w_in)
    u, v = jnp.split(z, 2, axis=-1)
    v = layer_norm(v, ln_g, ln_b)
    v = v.reshape(B, S // GMLP_CHUNK, GMLP_CHUNK, GMLP_GROUPS, GMLP_GROUP_DIM)
    mask = jnp.tril(jnp.ones((GMLP_CHUNK, GMLP_CHUNK), dtype=w_s.dtype))
    ws = w_s * mask
    sv = jnp.einsum('gts,bnsgd->bntgd', ws, v) + b_s.T[None, None, :, :, None]
    y = u * sv.reshape(B, S, GMLP_HALF)
    return y @ w_out


def _fwd_setup_inputs(seed: int = 0) -> dict:
    key = jax.random.key(seed)
    ks = iter(jax.random.split(key, 40))

    def nrm(shape, scale):
        return jax.random.normal(next(ks), shape, dtype=jnp.float32) * scale

    def gain(shape):
        return 1.0 + nrm(shape, 0.02)

    nA, nB, D = N_MLA_LAYERS, N_GMLP_LAYERS, D_MODEL
    x = nrm((BATCH, SEQ, D), 1.0)
    p = nrm((DEPTH, BATCH, SEQ, PLE_DIM), 1.0)
    offs = jax.random.randint(next(ks), (BATCH, 1), 0, MAX_POS_OFFSET, dtype=jnp.int32)
    positions = offs + jnp.arange(SEQ, dtype=jnp.int32)[None, :]
    down_w = Q_LORA_RANK + KV_LORA_RANK + QK_ROPE_DIM
    return {
        "x": x,
        "p": p,
        "positions": positions,
        "norm_mix": gain((DEPTH, D)),
        "norm_ffn": gain((DEPTH, D)),
        "norm_ple": gain((DEPTH, D)),
        "mla_w_down": nrm((nA, D, down_w), D ** -0.5),
        "mla_q_lora_g": gain((nA, Q_LORA_RANK)),
        "mla_kv_lora_g": gain((nA, KV_LORA_RANK)),
        "mla_w_uq": nrm((nA, Q_LORA_RANK, MLA_HEADS * (QK_NOPE_DIM + QK_ROPE_DIM)), Q_LORA_RANK ** -0.5),
        "mla_w_ukv": nrm((nA, KV_LORA_RANK, MLA_HEADS * (QK_NOPE_DIM + V_HEAD_DIM)), KV_LORA_RANK ** -0.5),
        "mla_q_nope_g": gain((nA, QK_NOPE_DIM)),
        "mla_q_rope_g": gain((nA, QK_ROPE_DIM)),
        "mla_k_nope_g": gain((nA, QK_NOPE_DIM)),
        "mla_k_rope_g": gain((nA, QK_ROPE_DIM)),
        "mla_w_out": nrm((nA, MLA_HEADS * V_HEAD_DIM, D), 0.5 * (MLA_HEADS * V_HEAD_DIM) ** -0.5),
        "gmlp_w_in": nrm((nB, D, 2 * GMLP_HALF), D ** -0.5),
        "gmlp_ln_g": gain((nB, GMLP_HALF)),
        "gmlp_ln_b": nrm((nB, GMLP_HALF), 0.02),
        "gmlp_w_s": nrm((nB, GMLP_GROUPS, GMLP_CHUNK, GMLP_CHUNK), GMLP_CHUNK ** -0.5),
        "gmlp_b_s": 1.0 + nrm((nB, GMLP_GROUPS, GMLP_CHUNK), 0.1),
        "gmlp_w_out": nrm((nB, GMLP_HALF, D), 0.5 * GMLP_HALF ** -0.5),
        "ffn_w_up": nrm((DEPTH, D, D_FF), D ** -0.5),
        "ffn_w_down": nrm((DEPTH, D_FF, D), 0.5 * D_FF ** -0.5),
        "ple_w_gate": nrm((DEPTH, D, D), D ** -0.5),
        "ple_w_proj": nrm((DEPTH, PLE_DIM, D), PLE_DIM ** -0.5),
    }


def _fwd_reference(x, p, positions, norm_mix, norm_ffn, norm_ple,
              mla_w_down, mla_q_lora_g, mla_kv_lora_g, mla_w_uq, mla_w_ukv,
              mla_q_nope_g, mla_q_rope_g, mla_k_nope_g, mla_k_rope_g, mla_w_out,
              gmlp_w_in, gmlp_ln_g, gmlp_ln_b, gmlp_w_s, gmlp_b_s, gmlp_w_out,
              ffn_w_up, ffn_w_down, ple_w_gate, ple_w_proj):
    cos, sin = rope_cos_sin(positions)
    h = x
    for i in range(DEPTH):
        hn = rms_norm(h, norm_mix[i])
        j = i // N_MIXERS
        if i % N_MIXERS == 0:
            mix = mla_mixer(hn, cos, sin, mla_w_down[j], mla_q_lora_g[j], mla_kv_lora_g[j],
                            mla_w_uq[j], mla_w_ukv[j], mla_q_nope_g[j], mla_q_rope_g[j],
                            mla_k_nope_g[j], mla_k_rope_g[j], mla_w_out[j])
        else:
            mix = gmlp_mixer(hn, gmlp_w_in[j], gmlp_ln_g[j], gmlp_ln_b[j],
                             gmlp_w_s[j], gmlp_b_s[j], gmlp_w_out[j])
        h = h + mix
        hn = rms_norm(h, norm_ffn[i])
        h = h + jnp.square(jax.nn.relu(hn @ ffn_w_up[i])) @ ffn_w_down[i]
        hn = rms_norm(h, norm_ple[i])
        h = h + jax.nn.sigmoid(hn @ ple_w_gate[i]) * (p[i] @ ple_w_proj[i])
    return h


import jax as _jax
import jax.numpy as _jnp

TWIN_FORMAT = 'train_step'
FWD_PARAMS = ['x', 'p', 'positions', 'norm_mix', 'norm_ffn', 'norm_ple', 'mla_w_down', 'mla_q_lora_g', 'mla_kv_lora_g', 'mla_w_uq', 'mla_w_ukv', 'mla_q_nope_g', 'mla_q_rope_g', 'mla_k_nope_g', 'mla_k_rope_g', 'mla_w_out', 'gmlp_w_in', 'gmlp_ln_g', 'gmlp_ln_b', 'gmlp_w_s', 'gmlp_b_s', 'gmlp_w_out', 'ffn_w_up', 'ffn_w_down', 'ple_w_gate', 'ple_w_proj']
TWIN_WEIGHTS = ['norm_mix', 'norm_ffn', 'norm_ple', 'mla_w_down', 'mla_q_lora_g', 'mla_kv_lora_g', 'mla_w_uq', 'mla_w_ukv', 'mla_q_nope_g', 'mla_q_rope_g', 'mla_k_nope_g', 'mla_k_rope_g', 'mla_w_out', 'gmlp_w_in', 'gmlp_ln_g', 'gmlp_ln_b', 'gmlp_w_s', 'gmlp_b_s', 'gmlp_w_out', 'ffn_w_up', 'ffn_w_down', 'ple_w_gate', 'ple_w_proj']
TWIN_DIFF_INPUT = 'x'
TWIN_INPUTS = ['x', 'p', 'positions', 'norm_mix', 'norm_ffn', 'norm_ple', 'mla_w_down', 'mla_q_lora_g', 'mla_kv_lora_g', 'mla_w_uq', 'mla_w_ukv', 'mla_q_nope_g', 'mla_q_rope_g', 'mla_k_nope_g', 'mla_k_rope_g', 'mla_w_out', 'gmlp_w_in', 'gmlp_ln_g', 'gmlp_ln_b', 'gmlp_w_s', 'gmlp_b_s', 'gmlp_w_out', 'ffn_w_up', 'ffn_w_down', 'ple_w_gate', 'ple_w_proj', 'loss_target', 'm_norm_mix', 'm_norm_ffn', 'm_norm_ple', 'm_mla_w_down', 'm_mla_q_lora_g', 'm_mla_kv_lora_g', 'm_mla_w_uq', 'm_mla_w_ukv', 'm_mla_q_nope_g', 'm_mla_q_rope_g', 'm_mla_k_nope_g', 'm_mla_k_rope_g', 'm_mla_w_out', 'm_gmlp_w_in', 'm_gmlp_ln_g', 'm_gmlp_ln_b', 'm_gmlp_w_s', 'm_gmlp_b_s', 'm_gmlp_w_out', 'm_ffn_w_up', 'm_ffn_w_down', 'm_ple_w_gate', 'm_ple_w_proj', 'v_norm_mix', 'v_norm_ffn', 'v_norm_ple', 'v_mla_w_down', 'v_mla_q_lora_g', 'v_mla_kv_lora_g', 'v_mla_w_uq', 'v_mla_w_ukv', 'v_mla_q_nope_g', 'v_mla_q_rope_g', 'v_mla_k_nope_g', 'v_mla_k_rope_g', 'v_mla_w_out', 'v_gmlp_w_in', 'v_gmlp_ln_g', 'v_gmlp_ln_b', 'v_gmlp_w_s', 'v_gmlp_b_s', 'v_gmlp_w_out', 'v_ffn_w_up', 'v_ffn_w_down', 'v_ple_w_gate', 'v_ple_w_proj']
TWIN_OUTPUTS = ['loss', 'grad_x', 'grad_norm_mix', 'grad_norm_ffn', 'grad_norm_ple', 'grad_mla_w_down', 'grad_mla_q_lora_g', 'grad_mla_kv_lora_g', 'grad_mla_w_uq', 'grad_mla_w_ukv', 'grad_mla_q_nope_g', 'grad_mla_q_rope_g', 'grad_mla_k_nope_g', 'grad_mla_k_rope_g', 'grad_mla_w_out', 'grad_gmlp_w_in', 'grad_gmlp_ln_g', 'grad_gmlp_ln_b', 'grad_gmlp_w_s', 'grad_gmlp_b_s', 'grad_gmlp_w_out', 'grad_ffn_w_up', 'grad_ffn_w_down', 'grad_ple_w_gate', 'grad_ple_w_proj', 'delta_norm_mix', 'delta_norm_ffn', 'delta_norm_ple', 'delta_mla_w_down', 'delta_mla_q_lora_g', 'delta_mla_kv_lora_g', 'delta_mla_w_uq', 'delta_mla_w_ukv', 'delta_mla_q_nope_g', 'delta_mla_q_rope_g', 'delta_mla_k_nope_g', 'delta_mla_k_rope_g', 'delta_mla_w_out', 'delta_gmlp_w_in', 'delta_gmlp_ln_g', 'delta_gmlp_ln_b', 'delta_gmlp_w_s', 'delta_gmlp_b_s', 'delta_gmlp_w_out', 'delta_ffn_w_up', 'delta_ffn_w_down', 'delta_ple_w_gate', 'delta_ple_w_proj', 'new_m_norm_mix', 'new_m_norm_ffn', 'new_m_norm_ple', 'new_m_mla_w_down', 'new_m_mla_q_lora_g', 'new_m_mla_kv_lora_g', 'new_m_mla_w_uq', 'new_m_mla_w_ukv', 'new_m_mla_q_nope_g', 'new_m_mla_q_rope_g', 'new_m_mla_k_nope_g', 'new_m_mla_k_rope_g', 'new_m_mla_w_out', 'new_m_gmlp_w_in', 'new_m_gmlp_ln_g', 'new_m_gmlp_ln_b', 'new_m_gmlp_w_s', 'new_m_gmlp_b_s', 'new_m_gmlp_w_out', 'new_m_ffn_w_up', 'new_m_ffn_w_down', 'new_m_ple_w_gate', 'new_m_ple_w_proj', 'new_v_norm_mix', 'new_v_norm_ffn', 'new_v_norm_ple', 'new_v_mla_w_down', 'new_v_mla_q_lora_g', 'new_v_mla_kv_lora_g', 'new_v_mla_w_uq', 'new_v_mla_w_ukv', 'new_v_mla_q_nope_g', 'new_v_mla_q_rope_g', 'new_v_mla_k_nope_g', 'new_v_mla_k_rope_g', 'new_v_mla_w_out', 'new_v_gmlp_w_in', 'new_v_gmlp_ln_g', 'new_v_gmlp_ln_b', 'new_v_gmlp_w_s', 'new_v_gmlp_b_s', 'new_v_gmlp_w_out', 'new_v_ffn_w_up', 'new_v_ffn_w_down', 'new_v_ple_w_gate', 'new_v_ple_w_proj']
TWIN_LEAF_KINDS = {'loss': 'loss', 'grad_x': 'grad_x', 'grad_norm_mix': 'grad_w', 'grad_norm_ffn': 'grad_w', 'grad_norm_ple': 'grad_w', 'grad_mla_w_down': 'grad_w', 'grad_mla_q_lora_g': 'grad_w', 'grad_mla_kv_lora_g': 'grad_w', 'grad_mla_w_uq': 'grad_w', 'grad_mla_w_ukv': 'grad_w', 'grad_mla_q_nope_g': 'grad_w', 'grad_mla_q_rope_g': 'grad_w', 'grad_mla_k_nope_g': 'grad_w', 'grad_mla_k_rope_g': 'grad_w', 'grad_mla_w_out': 'grad_w', 'grad_gmlp_w_in': 'grad_w', 'grad_gmlp_ln_g': 'grad_w', 'grad_gmlp_ln_b': 'grad_w', 'grad_gmlp_w_s': 'grad_w', 'grad_gmlp_b_s': 'grad_w', 'grad_gmlp_w_out': 'grad_w', 'grad_ffn_w_up': 'grad_w', 'grad_ffn_w_down': 'grad_w', 'grad_ple_w_gate': 'grad_w', 'grad_ple_w_proj': 'grad_w', 'delta_norm_mix': 'delta_w', 'delta_norm_ffn': 'delta_w', 'delta_norm_ple': 'delta_w', 'delta_mla_w_down': 'delta_w', 'delta_mla_q_lora_g': 'delta_w', 'delta_mla_kv_lora_g': 'delta_w', 'delta_mla_w_uq': 'delta_w', 'delta_mla_w_ukv': 'delta_w', 'delta_mla_q_nope_g': 'delta_w', 'delta_mla_q_rope_g': 'delta_w', 'delta_mla_k_nope_g': 'delta_w', 'delta_mla_k_rope_g': 'delta_w', 'delta_mla_w_out': 'delta_w', 'delta_gmlp_w_in': 'delta_w', 'delta_gmlp_ln_g': 'delta_w', 'delta_gmlp_ln_b': 'delta_w', 'delta_gmlp_w_s': 'delta_w', 'delta_gmlp_b_s': 'delta_w', 'delta_gmlp_w_out': 'delta_w', 'delta_ffn_w_up': 'delta_w', 'delta_ffn_w_down': 'delta_w', 'delta_ple_w_gate': 'delta_w', 'delta_ple_w_proj': 'delta_w', 'new_m_norm_mix': 'new_m', 'new_m_norm_ffn': 'new_m', 'new_m_norm_ple': 'new_m', 'new_m_mla_w_down': 'new_m', 'new_m_mla_q_lora_g': 'new_m', 'new_m_mla_kv_lora_g': 'new_m', 'new_m_mla_w_uq': 'new_m', 'new_m_mla_w_ukv': 'new_m', 'new_m_mla_q_nope_g': 'new_m', 'new_m_mla_q_rope_g': 'new_m', 'new_m_mla_k_nope_g': 'new_m', 'new_m_mla_k_rope_g': 'new_m', 'new_m_mla_w_out': 'new_m', 'new_m_gmlp_w_in': 'new_m', 'new_m_gmlp_ln_g': 'new_m', 'new_m_gmlp_ln_b': 'new_m', 'new_m_gmlp_w_s': 'new_m', 'new_m_gmlp_b_s': 'new_m', 'new_m_gmlp_w_out': 'new_m', 'new_m_ffn_w_up': 'new_m', 'new_m_ffn_w_down': 'new_m', 'new_m_ple_w_gate': 'new_m', 'new_m_ple_w_proj': 'new_m', 'new_v_norm_mix': 'new_v', 'new_v_norm_ffn': 'new_v', 'new_v_norm_ple': 'new_v', 'new_v_mla_w_down': 'new_v', 'new_v_mla_q_lora_g': 'new_v', 'new_v_mla_kv_lora_g': 'new_v', 'new_v_mla_w_uq': 'new_v', 'new_v_mla_w_ukv': 'new_v', 'new_v_mla_q_nope_g': 'new_v', 'new_v_mla_q_rope_g': 'new_v', 'new_v_mla_k_nope_g': 'new_v', 'new_v_mla_k_rope_g': 'new_v', 'new_v_mla_w_out': 'new_v', 'new_v_gmlp_w_in': 'new_v', 'new_v_gmlp_ln_g': 'new_v', 'new_v_gmlp_ln_b': 'new_v', 'new_v_gmlp_w_s': 'new_v', 'new_v_gmlp_b_s': 'new_v', 'new_v_gmlp_w_out': 'new_v', 'new_v_ffn_w_up': 'new_v', 'new_v_ffn_w_down': 'new_v', 'new_v_ple_w_gate': 'new_v', 'new_v_ple_w_proj': 'new_v'}


def _forward(args):
    return _fwd_reference(*[args[k] for k in FWD_PARAMS])


def _output_shape():
    out = _jax.eval_shape(lambda: _forward(_fwd_setup_inputs(0)))
    return out.shape, out.dtype

N_MICROBATCH = 1
ADAM_LR = 0.001
ADAM_B1 = 0.9
ADAM_B2 = 0.999
ADAM_EPS = 1e-08
ADAM_WD = 0.01
ADAM_STEP = 10
PER_EXAMPLE_BATCH_AXIS = {'x': 0, 'p': 1, 'positions': 0, 'loss_target': 0}
SHARED_INPUTS = []
_WEIGHT_DTYPES = {'norm_mix': _jnp.float32, 'norm_ffn': _jnp.float32, 'norm_ple': _jnp.float32, 'mla_w_down': _jnp.float32, 'mla_q_lora_g': _jnp.float32, 'mla_kv_lora_g': _jnp.float32, 'mla_w_uq': _jnp.float32, 'mla_w_ukv': _jnp.float32, 'mla_q_nope_g': _jnp.float32, 'mla_q_rope_g': _jnp.float32, 'mla_k_nope_g': _jnp.float32, 'mla_k_rope_g': _jnp.float32, 'mla_w_out': _jnp.float32, 'gmlp_w_in': _jnp.float32, 'gmlp_ln_g': _jnp.float32, 'gmlp_ln_b': _jnp.float32, 'gmlp_w_s': _jnp.float32, 'gmlp_b_s': _jnp.float32, 'gmlp_w_out': _jnp.float32, 'ffn_w_up': _jnp.float32, 'ffn_w_down': _jnp.float32, 'ple_w_gate': _jnp.float32, 'ple_w_proj': _jnp.float32}
MOMENT_SCALE = {'norm_mix': 4.566780e+00, 'norm_ffn': 2.442181e+01, 'norm_ple': 1.289850e+00, 'mla_w_down': 2.268637e+00, 'mla_q_lora_g': 1.092135e-01, 'mla_kv_lora_g': 4.195047e+00, 'mla_w_uq': 5.338171e-02, 'mla_w_ukv': 1.299232e+00, 'mla_q_nope_g': 2.559886e-01, 'mla_q_rope_g': 2.323989e-01, 'mla_k_nope_g': 2.583110e-01, 'mla_k_rope_g': 2.387208e-01, 'mla_w_out': 3.846177e+00, 'gmlp_w_in': 6.851402e-01, 'gmlp_ln_g': 8.927933e-01, 'gmlp_ln_b': 2.310070e-01, 'gmlp_w_s': 1.871370e-01, 'gmlp_b_s': 3.420653e+00, 'gmlp_w_out': 6.928502e+00, 'ffn_w_up': 1.133194e+00, 'ffn_w_down': 1.173402e+01, 'ple_w_gate': 7.805450e-01, 'ple_w_proj': 4.852651e-01}


def _to_microbatches(a, axis):
    t = _jnp.moveaxis(a, axis, 0)
    t = t.reshape((N_MICROBATCH, t.shape[0] // N_MICROBATCH) + t.shape[1:])
    return _jnp.moveaxis(t, 1, axis + 1)


def setup_inputs(seed: int = 0) -> dict:
    inp = _fwd_setup_inputs(seed)
    key = _jax.random.fold_in(_jax.random.key(seed), 7919)
    shape, _ = _output_shape()
    out = dict(inp)
    out["loss_target"] = _jax.random.normal(_jax.random.fold_in(key, 0), shape, _jnp.float32)
    for i, name in enumerate(TWIN_WEIGHTS):
        w = inp[name].astype(_jnp.float32)
        if MOMENT_SCALE is None:
            s = _jnp.sqrt(_jnp.mean(_jnp.square(w)) + 1e-30)
        else:
            s = MOMENT_SCALE[name]
        km, kv = _jax.random.split(_jax.random.fold_in(key, i + 1))
        out[name] = w
        out["m_" + name] = s * _jax.random.normal(km, w.shape, _jnp.float32)
        out["v_" + name] = (s * s) * _jax.random.uniform(kv, w.shape, _jnp.float32, 0.5, 1.5)
    if N_MICROBATCH > 1:
        for name, axis in PER_EXAMPLE_BATCH_AXIS.items():
            out[name] = _to_microbatches(out[name], axis)
    return {'x': out['x'], 'p': out['p'], 'positions': out['positions'], 'norm_mix': out['norm_mix'], 'norm_ffn': out['norm_ffn'], 'norm_ple': out['norm_ple'], 'mla_w_down': out['mla_w_down'], 'mla_q_lora_g': out['mla_q_lora_g'], 'mla_kv_lora_g': out['mla_kv_lora_g'], 'mla_w_uq': out['mla_w_uq'], 'mla_w_ukv': out['mla_w_ukv'], 'mla_q_nope_g': out['mla_q_nope_g'], 'mla_q_rope_g': out['mla_q_rope_g'], 'mla_k_nope_g': out['mla_k_nope_g'], 'mla_k_rope_g': out['mla_k_rope_g'], 'mla_w_out': out['mla_w_out'], 'gmlp_w_in': out['gmlp_w_in'], 'gmlp_ln_g': out['gmlp_ln_g'], 'gmlp_ln_b': out['gmlp_ln_b'], 'gmlp_w_s': out['gmlp_w_s'], 'gmlp_b_s': out['gmlp_b_s'], 'gmlp_w_out': out['gmlp_w_out'], 'ffn_w_up': out['ffn_w_up'], 'ffn_w_down': out['ffn_w_down'], 'ple_w_gate': out['ple_w_gate'], 'ple_w_proj': out['ple_w_proj'], 'loss_target': out['loss_target'], 'm_norm_mix': out['m_norm_mix'], 'm_norm_ffn': out['m_norm_ffn'], 'm_norm_ple': out['m_norm_ple'], 'm_mla_w_down': out['m_mla_w_down'], 'm_mla_q_lora_g': out['m_mla_q_lora_g'], 'm_mla_kv_lora_g': out['m_mla_kv_lora_g'], 'm_mla_w_uq': out['m_mla_w_uq'], 'm_mla_w_ukv': out['m_mla_w_ukv'], 'm_mla_q_nope_g': out['m_mla_q_nope_g'], 'm_mla_q_rope_g': out['m_mla_q_rope_g'], 'm_mla_k_nope_g': out['m_mla_k_nope_g'], 'm_mla_k_rope_g': out['m_mla_k_rope_g'], 'm_mla_w_out': out['m_mla_w_out'], 'm_gmlp_w_in': out['m_gmlp_w_in'], 'm_gmlp_ln_g': out['m_gmlp_ln_g'], 'm_gmlp_ln_b': out['m_gmlp_ln_b'], 'm_gmlp_w_s': out['m_gmlp_w_s'], 'm_gmlp_b_s': out['m_gmlp_b_s'], 'm_gmlp_w_out': out['m_gmlp_w_out'], 'm_ffn_w_up': out['m_ffn_w_up'], 'm_ffn_w_down': out['m_ffn_w_down'], 'm_ple_w_gate': out['m_ple_w_gate'], 'm_ple_w_proj': out['m_ple_w_proj'], 'v_norm_mix': out['v_norm_mix'], 'v_norm_ffn': out['v_norm_ffn'], 'v_norm_ple': out['v_norm_ple'], 'v_mla_w_down': out['v_mla_w_down'], 'v_mla_q_lora_g': out['v_mla_q_lora_g'], 'v_mla_kv_lora_g': out['v_mla_kv_lora_g'], 'v_mla_w_uq': out['v_mla_w_uq'], 'v_mla_w_ukv': out['v_mla_w_ukv'], 'v_mla_q_nope_g': out['v_mla_q_nope_g'], 'v_mla_q_rope_g': out['v_mla_q_rope_g'], 'v_mla_k_nope_g': out['v_mla_k_nope_g'], 'v_mla_k_rope_g': out['v_mla_k_rope_g'], 'v_mla_w_out': out['v_mla_w_out'], 'v_gmlp_w_in': out['v_gmlp_w_in'], 'v_gmlp_ln_g': out['v_gmlp_ln_g'], 'v_gmlp_ln_b': out['v_gmlp_ln_b'], 'v_gmlp_w_s': out['v_gmlp_w_s'], 'v_gmlp_b_s': out['v_gmlp_b_s'], 'v_gmlp_w_out': out['v_gmlp_w_out'], 'v_ffn_w_up': out['v_ffn_w_up'], 'v_ffn_w_down': out['v_ffn_w_down'], 'v_ple_w_gate': out['v_ple_w_gate'], 'v_ple_w_proj': out['v_ple_w_proj']}


def _loss(weights, diff, rest, loss_target):
    with _jax.named_scope("forward"):
        args = {**rest, TWIN_DIFF_INPUT: diff, **{k: w.astype(_WEIGHT_DTYPES[k]) for k, w in weights.items()}}
        y = _forward(args)
    with _jax.named_scope("loss_head"):
        err = _jnp.square(y.astype(_jnp.float32) - loss_target)
        return 0.5 * _jnp.sum(_jnp.mean(err, axis=-1)) if err.ndim else 0.5 * err


def _adamw(w, g, m, v):
    m = ADAM_B1 * m + (1.0 - ADAM_B1) * g
    v = ADAM_B2 * v + (1.0 - ADAM_B2) * _jnp.square(g)
    m_hat = m / (1.0 - ADAM_B1 ** ADAM_STEP)
    v_hat = v / (1.0 - ADAM_B2 ** ADAM_STEP)
    delta = -ADAM_LR * (m_hat / (_jnp.sqrt(v_hat) + ADAM_EPS) + ADAM_WD * w)
    return delta, m, v


def reference(x, p, positions, norm_mix, norm_ffn, norm_ple, mla_w_down, mla_q_lora_g, mla_kv_lora_g, mla_w_uq, mla_w_ukv, mla_q_nope_g, mla_q_rope_g, mla_k_nope_g, mla_k_rope_g, mla_w_out, gmlp_w_in, gmlp_ln_g, gmlp_ln_b, gmlp_w_s, gmlp_b_s, gmlp_w_out, ffn_w_up, ffn_w_down, ple_w_gate, ple_w_proj, loss_target, m_norm_mix, m_norm_ffn, m_norm_ple, m_mla_w_down, m_mla_q_lora_g, m_mla_kv_lora_g, m_mla_w_uq, m_mla_w_ukv, m_mla_q_nope_g, m_mla_q_rope_g, m_mla_k_nope_g, m_mla_k_rope_g, m_mla_w_out, m_gmlp_w_in, m_gmlp_ln_g, m_gmlp_ln_b, m_gmlp_w_s, m_gmlp_b_s, m_gmlp_w_out, m_ffn_w_up, m_ffn_w_down, m_ple_w_gate, m_ple_w_proj, v_norm_mix, v_norm_ffn, v_norm_ple, v_mla_w_down, v_mla_q_lora_g, v_mla_kv_lora_g, v_mla_w_uq, v_mla_w_ukv, v_mla_q_nope_g, v_mla_q_rope_g, v_mla_k_nope_g, v_mla_k_rope_g, v_mla_w_out, v_gmlp_w_in, v_gmlp_ln_g, v_gmlp_ln_b, v_gmlp_w_s, v_gmlp_b_s, v_gmlp_w_out, v_ffn_w_up, v_ffn_w_down, v_ple_w_gate, v_ple_w_proj):
    given = dict(x=x, p=p, positions=positions, norm_mix=norm_mix, norm_ffn=norm_ffn, norm_ple=norm_ple, mla_w_down=mla_w_down, mla_q_lora_g=mla_q_lora_g, mla_kv_lora_g=mla_kv_lora_g, mla_w_uq=mla_w_uq, mla_w_ukv=mla_w_ukv, mla_q_nope_g=mla_q_nope_g, mla_q_rope_g=mla_q_rope_g, mla_k_nope_g=mla_k_nope_g, mla_k_rope_g=mla_k_rope_g, mla_w_out=mla_w_out, gmlp_w_in=gmlp_w_in, gmlp_ln_g=gmlp_ln_g, gmlp_ln_b=gmlp_ln_b, gmlp_w_s=gmlp_w_s, gmlp_b_s=gmlp_b_s, gmlp_w_out=gmlp_w_out, ffn_w_up=ffn_w_up, ffn_w_down=ffn_w_down, ple_w_gate=ple_w_gate, ple_w_proj=ple_w_proj, loss_target=loss_target, m_norm_mix=m_norm_mix, m_norm_ffn=m_norm_ffn, m_norm_ple=m_norm_ple, m_mla_w_down=m_mla_w_down, m_mla_q_lora_g=m_mla_q_lora_g, m_mla_kv_lora_g=m_mla_kv_lora_g, m_mla_w_uq=m_mla_w_uq, m_mla_w_ukv=m_mla_w_ukv, m_mla_q_nope_g=m_mla_q_nope_g, m_mla_q_rope_g=m_mla_q_rope_g, m_mla_k_nope_g=m_mla_k_nope_g, m_mla_k_rope_g=m_mla_k_rope_g, m_mla_w_out=m_mla_w_out, m_gmlp_w_in=m_gmlp_w_in, m_gmlp_ln_g=m_gmlp_ln_g, m_gmlp_ln_b=m_gmlp_ln_b, m_gmlp_w_s=m_gmlp_w_s, m_gmlp_b_s=m_gmlp_b_s, m_gmlp_w_out=m_gmlp_w_out, m_ffn_w_up=m_ffn_w_up, m_ffn_w_down=m_ffn_w_down, m_ple_w_gate=m_ple_w_gate, m_ple_w_proj=m_ple_w_proj, v_norm_mix=v_norm_mix, v_norm_ffn=v_norm_ffn, v_norm_ple=v_norm_ple, v_mla_w_down=v_mla_w_down, v_mla_q_lora_g=v_mla_q_lora_g, v_mla_kv_lora_g=v_mla_kv_lora_g, v_mla_w_uq=v_mla_w_uq, v_mla_w_ukv=v_mla_w_ukv, v_mla_q_nope_g=v_mla_q_nope_g, v_mla_q_rope_g=v_mla_q_rope_g, v_mla_k_nope_g=v_mla_k_nope_g, v_mla_k_rope_g=v_mla_k_rope_g, v_mla_w_out=v_mla_w_out, v_gmlp_w_in=v_gmlp_w_in, v_gmlp_ln_g=v_gmlp_ln_g, v_gmlp_ln_b=v_gmlp_ln_b, v_gmlp_w_s=v_gmlp_w_s, v_gmlp_b_s=v_gmlp_b_s, v_gmlp_w_out=v_gmlp_w_out, v_ffn_w_up=v_ffn_w_up, v_ffn_w_down=v_ffn_w_down, v_ple_w_gate=v_ple_w_gate, v_ple_w_proj=v_ple_w_proj)
    weights = {n: given[n] for n in TWIN_WEIGHTS}
    shared = {n: given[n] for n in SHARED_INPUTS}
    per_example = {n: given[n] for n in ['x', 'p', 'positions']}
    grad_fn = _jax.value_and_grad(_loss, argnums=(0, 1))

    def one_microbatch(ex, loss_target):
        ex = dict(ex)
        diff = ex.pop(TWIN_DIFF_INPUT)
        return grad_fn(weights, diff, {**shared, **ex}, loss_target)

    if N_MICROBATCH == 1:
        loss, (grad_w, grad_x) = one_microbatch(per_example, given["loss_target"])
    else:
        def body(carry, xs):
            loss_sum, grad_sum = carry
            l_k, (gw_k, gx_k) = one_microbatch(xs[0], xs[1])
            with _jax.named_scope("update"):
                return (loss_sum + l_k, _jax.tree.map(_jnp.add, grad_sum, gw_k)), gx_k

        init = (_jnp.zeros((), _jnp.float32), _jax.tree.map(_jnp.zeros_like, weights))
        (loss, grad_w), grad_x = _jax.lax.scan(body, init, (per_example, given["loss_target"]))
    with _jax.named_scope("update"):
        delta_w, new_m, new_v = {}, {}, {}
        for n in TWIN_WEIGHTS:
            delta_w[n], new_m[n], new_v[n] = _adamw(weights[n], grad_w[n], given["m_" + n], given["v_" + n])
    return (loss, grad_x, *[grad_w[n] for n in TWIN_WEIGHTS], *[delta_w[n] for n in TWIN_WEIGHTS],
            *[new_m[n] for n in TWIN_WEIGHTS], *[new_v[n] for n in TWIN_WEIGHTS])
```

```python
import functools

import jax
import jax.numpy as jnp
from jax import lax
from jax.experimental import pallas as pl
from jax.experimental.pallas import tpu as pltpu

F32 = jnp.float32
BF16 = jnp.bfloat16
MESH = pl.DeviceIdType.MESH

D = 1024
HEADS = 8
DN = 128
DR = 64
QL = 384
KVL = 256
LAT = 704
LATP = 768
DFF = 4096
GH = 2048
GC = 128
GG = 8
GD = 256
PLE = 256
EPS = 1e-6
ROPE_BASE = 10000.0
SM_SCALE = (DN + DR) ** -0.5
N_CHIPS = 4
LANES = 128

ADAM_LR = 0.001
ADAM_B1 = 0.9
ADAM_B2 = 0.999
ADAM_EPS = 1e-08
ADAM_WD = 0.01
ADAM_STEP = 10

TM = 256
TQ = 256
SUM_ROWS = 320
VMEM_LIMIT = 56 * 1024 * 1024


def _cp(*sem):
    return pltpu.CompilerParams(dimension_semantics=sem, vmem_limit_bytes=VMEM_LIMIT)


def _dot(a, b):
    return jnp.dot(a, b, preferred_element_type=F32)


def _dot_nt(a, b):
    return lax.dot_general(a, b, (((1,), (1,)), ((), ())), preferred_element_type=F32)


def _dot_tn(a, b):
    return lax.dot_general(a, b, (((0,), (0,)), ((), ())), preferred_element_type=F32)


def _rms(x, g, n):
    r = lax.rsqrt(jnp.sum(x * x, axis=-1, keepdims=True) * (1.0 / n) + EPS)
    xhat = x * r
    return xhat * g, xhat, r


def _rms_bwd(dy, g, xhat, r, n):
    dxhat = dy * g
    return r * (dxhat - xhat * (jnp.sum(dxhat * xhat, axis=-1, keepdims=True) * (1.0 / n)))


def _rope(x, c, s):
    return x * c + (pltpu.roll(x, 32, 1) - pltpu.roll(x, 96, 1)) * s


def _rope_t(dy, c, s):
    w = dy * s
    return dy * c + pltpu.roll(w, 96, 1) - pltpu.roll(w, 32, 1)


def _sigmoid(x):
    return 1.0 / (1.0 + jnp.exp(-x))


_GELU_K = 0.7978845608028654
_GELU_C = 0.044715


def _gelu(x):
    return 0.5 * x * (1.0 + jnp.tanh(_GELU_K * (x + _GELU_C * x * x * x)))


def _gelu_grad(x):
    t = jnp.tanh(_GELU_K * (x + _GELU_C * x * x * x))
    return 0.5 * (1.0 + t) + 0.5 * x * (1.0 - t * t) * (_GELU_K * (1.0 + 3.0 * _GELU_C * x * x))


def _acc_rows(ref, val):
    ref[...] += jnp.broadcast_to(jnp.sum(val, axis=0, keepdims=True), ref.shape)


def _row(tm, c):
    return pl.BlockSpec((tm, c), lambda i: (i, 0))


def _const(shape):
    nd = len(shape)
    return pl.BlockSpec(shape, lambda i: (0,) * nd, pipeline_mode=pl.Buffered(1))


def _sds(shape, dtype):
    return jax.ShapeDtypeStruct(shape, dtype)


def mixffn_fwd(h, y, wo, g2, wu, wd):
    t, k = y.shape

    def body(h_ref, y_ref, wo_ref, g_ref, wu_ref, wd_ref, h1_ref, h2_ref, hn_ref, r_ref):
        h1 = h_ref[...] + _dot(y_ref[...], wo_ref[...])
        h1_ref[...] = h1
        yn, _, _ = _rms(h1, g_ref[...], D)
        hn = yn.astype(BF16)
        hn_ref[...] = hn
        f = jnp.zeros((TM, D), F32)
        for c in range(DFF // D):
            cs = slice(c * D, (c + 1) * D)
            r = jnp.maximum(_dot(hn, wu_ref[:, cs]), 0.0)
            r_ref[:, cs] = r.astype(BF16)
            f = f + _dot((r * r).astype(BF16), wd_ref[cs, :])
        h2_ref[...] = h1 + f

    return pl.pallas_call(
        body, name="mixffn_fwd", grid=(t // TM,),
        in_specs=[_row(TM, D), _row(TM, k), _const((k, D)), _const((1, D)), _const((D, DFF)), _const((DFF, D))],
        out_specs=[_row(TM, D), _row(TM, D), _row(TM, D), _row(TM, DFF)],
        out_shape=[_sds((t, D), F32), _sds((t, D), F32), _sds((t, D), BF16), _sds((t, DFF), BF16)],
        compiler_params=_cp("parallel"),
    )(h, y, wo, g2, wu, wd)


def ple_fwd(h2, p, g3, wg, wp):
    t = h2.shape[0]

    def body(h_ref, p_ref, g_ref, wg_ref, wp_ref, h3_ref, hn_ref):
        x = h_ref[...]
        yn, _, _ = _rms(x, g_ref[...], D)
        hn = yn.astype(BF16)
        hn_ref[...] = hn
        gt = _dot(hn, wg_ref[...])
        pp = _dot(p_ref[...].astype(BF16), wp_ref[...])
        h3_ref[...] = x + _sigmoid(gt) * pp

    return pl.pallas_call(
        body, name="ple_fwd", grid=(t // TM,),
        in_specs=[_row(TM, D), _row(TM, PLE), _const((1, D)), _const((D, D)), _const((PLE, D))],
        out_specs=[_row(TM, D), _row(TM, D)],
        out_shape=[_sds((t, D), F32), _sds((t, D), BF16)],
        compiler_params=_cp("parallel"),
    )(h2, p, g3, wg, wp)


def _mla_project(h_ref, g1_ref, wdn_ref, gq_ref, gkv_ref, wuq_ref, wukv_ref):
    x = h_ref[...]
    yn, xhat, rx = _rms(x, g1_ref[...], D)
    hn = yn.astype(BF16)
    lat = _dot(hn, wdn_ref[...])
    cq, cqhat, rq = _rms(lat[:, :QL], gq_ref[...], QL)
    ckv, ckvhat, rkv = _rms(lat[:, QL:QL + KVL], gkv_ref[...], KVL)
    kr_raw = lat[:, QL + KVL:]
    cqb = cq.astype(BF16)
    ckvb = ckv.astype(BF16)
    qp = _dot(cqb, wuq_ref[...])
    kvp = _dot(ckvb, wukv_ref[...])
    return dict(xhat=xhat, rx=rx, hn=hn, cqhat=cqhat, rq=rq, ckvhat=ckvhat, rkv=rkv, kr_raw=kr_raw,
                cqb=cqb, ckvb=ckvb, qp=qp, kvp=kvp)


def mla_pre_fwd(h, g1, wdn, gq, gkv, wuq, wukv, gqn, gqr, gkn, gkr, cos, sin):
    t = h.shape[0]

    def body(h_ref, g1_ref, wdn_ref, gq_ref, gkv_ref, wuq_ref, wukv_ref, gqn_ref, gqr_ref, gkn_ref, gkr_ref,
             c_ref, s_ref, q_ref, k_ref, v_ref):
        m = _mla_project(h_ref, g1_ref, wdn_ref, gq_ref, gkv_ref, wuq_ref, wukv_ref)
        c = c_ref[...]
        s = s_ref[...]
        kr, _, _ = _rms(m["kr_raw"], gkr_ref[...], DR)
        krb = _rope(kr, c, s).astype(BF16)
        for hd in range(HEADS):
            qn, _, _ = _rms(m["qp"][:, hd * DN:(hd + 1) * DN], gqn_ref[...], DN)
            qr, _, _ = _rms(m["qp"][:, D + hd * LANES:D + (hd + 1) * LANES], gqr_ref[...], DR)
            q_ref[hd, :, 0:DN] = qn.astype(BF16)
            q_ref[hd, :, DN:2 * DN] = _rope(qr, c, s).astype(BF16)
            kn, _, _ = _rms(m["kvp"][:, hd * 2 * DN:hd * 2 * DN + DN], gkn_ref[...], DN)
            k_ref[hd, :, 0:DN] = kn.astype(BF16)
            k_ref[hd, :, DN:2 * DN] = krb
            v_ref[hd] = m["kvp"][:, hd * 2 * DN + DN:(hd + 1) * 2 * DN].astype(BF16)

    hb = lambda w: pl.BlockSpec((HEADS, TM, w), lambda i: (0, i, 0))
    return pl.pallas_call(
        body, name="mla_pre_fwd", grid=(t // TM,),
        in_specs=[_row(TM, D), _const((1, D)), _const((D, LATP)), _const((1, QL)), _const((1, KVL)),
                  _const((QL, 2 * D)), _const((KVL, 2 * D)), _const((1, LANES)), _const((1, LANES)),
                  _const((1, LANES)), _const((1, LANES)), _row(TM, LANES), _row(TM, LANES)],
        out_specs=[hb(2 * DN), hb(2 * DN), hb(DN)],
        out_shape=[_sds((HEADS, t, 2 * DN), BF16), _sds((HEADS, t, 2 * DN), BF16), _sds((HEADS, t, DN), BF16)],
        compiler_params=_cp("parallel"),
    )(h, g1, wdn, gq, gkv, wuq, wukv, gqn, gqr, gkn, gkr, cos, sin)


def _causal_mask(qi, kj):
    row = qi * TQ + lax.broadcasted_iota(jnp.int32, (TQ, TQ), 0)
    col = kj * TQ + lax.broadcasted_iota(jnp.int32, (TQ, TQ), 1)
    return col <= row


def flash_fwd(q, k, v, seq):
    t = q.shape[1]
    nb = t // seq
    nq = seq // TQ

    def body(q_ref, k_ref, v_ref, o_ref, lse_ref):
        qi = pl.program_id(2)
        qv = q_ref[0]

        def step(j, carry):
            m, l, acc = carry
            rows = pl.ds(pl.multiple_of(j * TQ, TQ), TQ)
            s = _dot_nt(qv, k_ref[0, rows, :]) * SM_SCALE
            s = jnp.where(_causal_mask(qi, j), s, -1e30)
            m_new = jnp.maximum(m, jnp.max(s, axis=-1, keepdims=True))
            p = jnp.exp(s - m_new)
            alpha = jnp.exp(m - m_new)
            l = alpha * l + jnp.sum(p, axis=-1, keepdims=True)
            acc = alpha * acc + _dot(p.astype(BF16), v_ref[0, rows, :])
            return m_new, l, acc

        init = (jnp.full((TQ, 1), -1e30, F32), jnp.zeros((TQ, 1), F32), jnp.zeros((TQ, DN), F32))
        m, l, acc = lax.fori_loop(0, qi + 1, step, init)
        o_ref[...] = (acc / l).astype(BF16)
        lse_ref[0] = m + jnp.log(l)

    return pl.pallas_call(
        body, name="flash_fwd", grid=(nb, HEADS, nq),
        in_specs=[pl.BlockSpec((1, TQ, 2 * DN), lambda b, h, i: (h, b * nq + i, 0)),
                  pl.BlockSpec((1, seq, 2 * DN), lambda b, h, i: (h, b, 0)),
                  pl.BlockSpec((1, seq, DN), lambda b, h, i: (h, b, 0))],
        out_specs=[pl.BlockSpec((TQ, DN), lambda b, h, i: (b * nq + i, h)),
                   pl.BlockSpec((1, TQ, 1), lambda b, h, i: (h, b * nq + i, 0))],
        out_shape=[_sds((t, HEADS * DN), BF16), _sds((HEADS, t, 1), F32)],
        compiler_params=_cp("parallel", "parallel", "arbitrary"),
    )(q, k, v)


def gmlp_fwd(h, g1, win, lng, lnb, wm, bfull):
    t = h.shape[0]

    def body(h_ref, g1_ref, win_ref, lng_ref, lnb_ref, wm_ref, b_ref, y_ref, pre_ref):
        yn, _, _ = _rms(h_ref[...], g1_ref[...], D)
        hn = yn.astype(BF16)
        pre_u = _dot(hn, win_ref[:, :GH])
        pre_v = _dot(hn, win_ref[:, GH:])
        pre_ref[:, :GH] = pre_u.astype(BF16)
        pre_ref[:, GH:] = pre_v.astype(BF16)
        u = _gelu(pre_u)
        v = _gelu(pre_v)
        xc = v - jnp.mean(v, axis=-1, keepdims=True)
        rs = lax.rsqrt(jnp.mean(xc * xc, axis=-1, keepdims=True) + EPS)
        vnb = (xc * rs * lng_ref[...] + lnb_ref[...]).astype(BF16)
        for ch in range(TM // GC):
            rows = slice(ch * GC, (ch + 1) * GC)
            for g in range(GG):
                cols = slice(g * GD, (g + 1) * GD)
                sv = _dot(wm_ref[g], vnb[rows, cols]) + b_ref[:, cols]
                y_ref[rows, cols] = (u[rows, cols] * sv).astype(BF16)

    return pl.pallas_call(
        body, name="gmlp_fwd", grid=(t // TM,),
        in_specs=[_row(TM, D), _const((1, D)), _const((D, 2 * GH)), _const((1, GH)), _const((1, GH)),
                  _const((GG, GC, GC)), _const((GC, GH))],
        out_specs=[_row(TM, GH), _row(TM, 2 * GH)],
        out_shape=[_sds((t, GH), BF16), _sds((t, 2 * GH), BF16)],
        compiler_params=_cp("parallel"),
    )(h, g1, win, lng, lnb, wm, bfull)


def loss_head(h, tgt):
    t = h.shape[0]

    def body(h_ref, t_ref, dh_ref, loss_ref):
        @pl.when(pl.program_id(0) == 0)
        def _():
            loss_ref[...] = jnp.zeros_like(loss_ref)

        e = h_ref[...] - t_ref[...]
        dh_ref[...] = e * (1.0 / D)
        part = jnp.sum(jnp.sum(e * e, axis=-1, keepdims=True), axis=0, keepdims=True) * (0.5 / D)
        loss_ref[...] += jnp.broadcast_to(part, loss_ref.shape)

    return pl.pallas_call(
        body, name="loss_head", grid=(t // TM,),
        in_specs=[_row(TM, D), _row(TM, D)],
        out_specs=[_row(TM, D), _const((8, LANES))],
        out_shape=[_sds((t, D), F32), _sds((8, LANES), F32)],
        compiler_params=_cp("arbitrary"),
    )(h, tgt)


def _zero_at_first_step(*refs):
    @pl.when(pl.program_id(0) == 0)
    def _():
        for r in refs:
            r[...] = jnp.zeros_like(r)


def ple_bwd(dh3, h2, p, g3, wg, wp):
    t = h2.shape[0]

    def body(dh_ref, h_ref, p_ref, g_ref, wg_ref, wp_ref, dh2_ref, dh2b_ref, dgt_ref, dpp_ref, dg_ref):
        _zero_at_first_step(dg_ref)
        dh3v = dh_ref[...]
        x = h_ref[...]
        g = g_ref[...]
        yn, xhat, r = _rms(x, g, D)
        gt = _dot(yn.astype(BF16), wg_ref[...])
        pp = _dot(p_ref[...].astype(BF16), wp_ref[...])
        sg = _sigmoid(gt)
        dgt = (dh3v * pp * sg * (1.0 - sg)).astype(BF16)
        dgt_ref[...] = dgt
        dpp_ref[...] = (dh3v * sg).astype(BF16)
        dhn = _dot_nt(dgt, wg_ref[...])
        _acc_rows(dg_ref, dhn * xhat)
        dh2 = dh3v + _rms_bwd(dhn, g, xhat, r, D)
        dh2_ref[...] = dh2
        dh2b_ref[...] = dh2.astype(BF16)

    return pl.pallas_call(
        body, name="ple_bwd", grid=(t // TM,),
        in_specs=[_row(TM, D), _row(TM, D), _row(TM, PLE), _const((1, D)), _const((D, D)), _const((PLE, D))],
        out_specs=[_row(TM, D), _row(TM, D), _row(TM, D), _row(TM, D), _const((8, D))],
        out_shape=[_sds((t, D), F32), _sds((t, D), BF16), _sds((t, D), BF16), _sds((t, D), BF16), _sds((8, D), F32)],
        compiler_params=_cp("arbitrary"),
    )(dh3, h2, p, g3, wg, wp)


def ffn_bwd(dh2, dh2b, h1, r, g2, wu, wd):
    t = h1.shape[0]

    def body(dh_ref, dhb_ref, h_ref, r_ref, g_ref, wu_ref, wd_ref, dh1_ref, dh1b_ref, du_ref, a_ref, dg_ref):
        _zero_at_first_step(dg_ref)
        dhb = dhb_ref[...]
        g = g_ref[...]
        _, xhat, rr = _rms(h_ref[...], g, D)
        dhn = jnp.zeros((TM, D), F32)
        for c in range(DFF // D):
            cs = slice(c * D, (c + 1) * D)
            rc = r_ref[:, cs].astype(F32)
            a_ref[:, cs] = (rc * rc).astype(BF16)
            da = _dot_nt(dhb, wd_ref[cs, :])
            du = (da * (2.0 * rc)).astype(BF16)
            du_ref[:, cs] = du
            dhn = dhn + _dot_nt(du, wu_ref[:, cs])
        _acc_rows(dg_ref, dhn * xhat)
        dh1 = dh_ref[...] + _rms_bwd(dhn, g, xhat, rr, D)
        dh1_ref[...] = dh1
        dh1b_ref[...] = dh1.astype(BF16)

    return pl.pallas_call(
        body, name="ffn_bwd", grid=(t // TM,),
        in_specs=[_row(TM, D), _row(TM, D), _row(TM, D), _row(TM, DFF), _const((1, D)), _const((D, DFF)),
                  _const((DFF, D))],
        out_specs=[_row(TM, D), _row(TM, D), _row(TM, DFF), _row(TM, DFF), _const((8, D))],
        out_shape=[_sds((t, D), F32), _sds((t, D), BF16), _sds((t, DFF), BF16), _sds((t, DFF), BF16),
                   _sds((8, D), F32)],
        compiler_params=_cp("arbitrary"),
    )(dh2, dh2b, h1, r, g2, wu, wd)


def linear_nt(a, w):
    t, n = a.shape
    k = w.shape[0]

    def body(a_ref, w_ref, o_ref):
        o_ref[...] = _dot_nt(a_ref[...], w_ref[...]).astype(BF16)

    return pl.pallas_call(
        body, name="linear_nt", grid=(t // TM,),
        in_specs=[_row(TM, n), _const((k, n))],
        out_specs=_row(TM, k),
        out_shape=_sds((t, k), BF16),
        compiler_params=_cp("parallel"),
    )(a, w)


def flash_bwd(q, k, v, o, do, lse, seq):
    t = q.shape[1]
    nb = t // seq
    nq = seq // TQ

    def body(q_ref, k_ref, v_ref, o_ref, do_ref, lse_ref, dq_ref, dk_ref, dv_ref):
        kj = pl.program_id(2)

        @pl.when(kj == 0)
        def _():
            dq_ref[...] = jnp.zeros_like(dq_ref)

        kv = k_ref[0]
        vv = v_ref[0]

        def step(i, carry):
            dk, dv = carry
            rows = pl.ds(pl.multiple_of(i * TQ, TQ), TQ)
            qv = q_ref[0, rows, :]
            dov = do_ref[rows, :]
            delta = jnp.sum(dov.astype(F32) * o_ref[rows, :].astype(F32), axis=-1, keepdims=True)
            s = _dot_nt(qv, kv) * SM_SCALE
            s = jnp.where(_causal_mask(i, kj), s, -1e30)
            p = jnp.exp(s - lse_ref[0, rows, :])
            dp = _dot_nt(dov, vv)
            ds = (p * (dp - delta) * SM_SCALE).astype(BF16)
            dv = dv + _dot_tn(p.astype(BF16), dov)
            dk = dk + _dot_tn(ds, qv)
            dq_ref[0, rows, :] += _dot(ds, kv)
            return dk, dv

        init = (jnp.zeros((TQ, 2 * DN), F32), jnp.zeros((TQ, DN), F32))
        dk, dv = lax.fori_loop(kj, nq, step, init)
        dk_ref[0] = dk
        dv_ref[0] = dv

    return pl.pallas_call(
        body, name="flash_bwd", grid=(nb, HEADS, nq),
        in_specs=[pl.BlockSpec((1, seq, 2 * DN), lambda b, h, j: (h, b, 0)),
                  pl.BlockSpec((1, TQ, 2 * DN), lambda b, h, j: (h, b * nq + j, 0)),
                  pl.BlockSpec((1, TQ, DN), lambda b, h, j: (h, b * nq + j, 0)),
                  pl.BlockSpec((seq, DN), lambda b, h, j: (b, h)),
                  pl.BlockSpec((seq, DN), lambda b, h, j: (b, h)),
                  pl.BlockSpec((1, seq, 1), lambda b, h, j: (h, b, 0))],
        out_specs=[pl.BlockSpec((1, seq, 2 * DN), lambda b, h, j: (h, b, 0)),
                   pl.BlockSpec((1, TQ, 2 * DN), lambda b, h, j: (h, b * nq + j, 0)),
                   pl.BlockSpec((1, TQ, DN), lambda b, h, j: (h, b * nq + j, 0))],
        out_shape=[_sds((HEADS, t, 2 * DN), F32), _sds((HEADS, t, 2 * DN), F32), _sds((HEADS, t, DN), F32)],
        compiler_params=_cp("parallel", "parallel", "arbitrary"),
    )(q, k, v, o, do, lse)


def mla_pre_bwd(dq, dk, dv, dh1, h, g1, wdn, gq, gkv, wuq, wukv, gqn, gqr, gkn, gkr, cos, sin):
    t = h.shape[0]

    def body(dq_ref, dk_ref, dv_ref, dh1_ref, h_ref, g1_ref, wdn_ref, gq_ref, gkv_ref, wuq_ref, wukv_ref,
             gqn_ref, gqr_ref, gkn_ref, gkr_ref, c_ref, s_ref,
             dh_ref, hn_ref, cq_ref, ckv_ref, dqp_ref, dkvp_ref, dlat_ref,
             dg1_ref, dgq_ref, dgkv_ref, dgqn_ref, dgqr_ref, dgkn_ref, dgkr_ref):
        _zero_at_first_step(dg1_ref, dgq_ref, dgkv_ref, dgqn_ref, dgqr_ref, dgkn_ref, dgkr_ref)
        m = _mla_project(h_ref, g1_ref, wdn_ref, gq_ref, gkv_ref, wuq_ref, wukv_ref)
        hn_ref[...] = m["hn"]
        cq_ref[...] = m["cqb"]
        ckv_ref[...] = m["ckvb"]
        c = c_ref[...]
        s = s_ref[...]
        gqn = gqn_ref[...]
        gqr = gqr_ref[...]
        gkn = gkn_ref[...]
        gkr = gkr_ref[...]

        dkr = dk_ref[0, :, DN:2 * DN]
        for hd in range(1, HEADS):
            dkr = dkr + dk_ref[hd, :, DN:2 * DN]
        dkr = _rope_t(dkr, c, s)
        _, krhat, rkr = _rms(m["kr_raw"], gkr, DR)
        _acc_rows(dgkr_ref, dkr * krhat)
        dkr_raw = _rms_bwd(dkr, gkr, krhat, rkr, DR)

        for hd in range(HEADS):
            ncols = slice(hd * DN, (hd + 1) * DN)
            _, xh, r = _rms(m["qp"][:, ncols], gqn, DN)
            dqn = dq_ref[hd, :, 0:DN]
            _acc_rows(dgqn_ref, dqn * xh)
            dqp_ref[:, ncols] = _rms_bwd(dqn, gqn, xh, r, DN).astype(BF16)

            rcols = slice(D + hd * LANES, D + (hd + 1) * LANES)
            _, xh, r = _rms(m["qp"][:, rcols], gqr, DR)
            dqr = _rope_t(dq_ref[hd, :, DN:2 * DN], c, s)
            _acc_rows(dgqr_ref, dqr * xh)
            dqp_ref[:, rcols] = _rms_bwd(dqr, gqr, xh, r, DR).astype(BF16)

            kcols = slice(hd * 2 * DN, hd * 2 * DN + DN)
            _, xh, r = _rms(m["kvp"][:, kcols], gkn, DN)
            dkn = dk_ref[hd, :, 0:DN]
            _acc_rows(dgkn_ref, dkn * xh)
            dkvp_ref[:, kcols] = _rms_bwd(dkn, gkn, xh, r, DN).astype(BF16)
            dkvp_ref[:, hd * 2 * DN + DN:(hd + 1) * 2 * DN] = dv_ref[hd].astype(BF16)

        dcq = _dot_nt(dqp_ref[...], wuq_ref[...])
        _acc_rows(dgq_ref, dcq * m["cqhat"])
        dlat_q = _rms_bwd(dcq, gq_ref[...], m["cqhat"], m["rq"], QL)
        dckv = _dot_nt(dkvp_ref[...], wukv_ref[...])
        _acc_rows(dgkv_ref, dckv * m["ckvhat"])
        dlat_kv = _rms_bwd(dckv, gkv_ref[...], m["ckvhat"], m["rkv"], KVL)
        dlat = jnp.concatenate([dlat_q, dlat_kv, dkr_raw], axis=1).astype(BF16)
        dlat_ref[...] = dlat
        dhn = _dot_nt(dlat, wdn_ref[...])
        _acc_rows(dg1_ref, dhn * m["xhat"])
        dh_ref[...] = dh1_ref[...] + _rms_bwd(dhn, g1_ref[...], m["xhat"], m["rx"], D)

    hb = lambda w: pl.BlockSpec((HEADS, TM, w), lambda i: (0, i, 0))
    return pl.pallas_call(
        body, name="mla_pre_bwd", grid=(t // TM,),
        in_specs=[hb(2 * DN), hb(2 * DN), hb(DN), _row(TM, D), _row(TM, D), _const((1, D)), _const((D, LATP)),
                  _const((1, QL)), _const((1, KVL)), _const((QL, 2 * D)), _const((KVL, 2 * D)),
                  _const((1, LANES)), _const((1, LANES)), _const((1, LANES)), _const((1, LANES)),
                  _row(TM, LANES), _row(TM, LANES)],
        out_specs=[_row(TM, D), _row(TM, D), _row(TM, QL), _row(TM, KVL), _row(TM, 2 * D), _row(TM, 2 * D),
                   _row(TM, LATP), _const((8, D)), _const((8, QL)), _const((8, KVL)), _const((8, LANES)),
                   _const((8, LANES)), _const((8, LANES)), _const((8, LANES))],
        out_shape=[_sds((t, D), F32), _sds((t, D), BF16), _sds((t, QL), BF16), _sds((t, KVL), BF16),
                   _sds((t, 2 * D), BF16), _sds((t, 2 * D), BF16), _sds((t, LATP), BF16),
                   _sds((8, D), F32), _sds((8, QL), F32), _sds((8, KVL), F32), _sds((8, LANES), F32),
                   _sds((8, LANES), F32), _sds((8, LANES), F32), _sds((8, LANES), F32)],
        compiler_params=_cp("arbitrary"),
    )(dq, dk, dv, dh1, h, g1, wdn, gq, gkv, wuq, wukv, gqn, gqr, gkn, gkr, cos, sin)


def gmlp_bwd(dh1, dh1b, h, pre, g1, win, lng, lnb, wm, wmt, bfull, wout, tril):
    t = h.shape[0]

    def body(dh1_ref, dh1b_ref, h_ref, pre_ref, g1_ref, win_ref, lng_ref, lnb_ref, wm_ref, wmt_ref, b_ref,
             wout_ref, tril_ref, dh_ref, hn_ref, dpre_ref, dws_ref, dbs_ref, dlng_ref, dlnb_ref, dg1_ref,
             dvn_s):
        _zero_at_first_step(dws_ref, dbs_ref, dlng_ref, dlnb_ref, dg1_ref)
        g1 = g1_ref[...]
        yn, xhat, rx = _rms(h_ref[...], g1, D)
        hn_ref[...] = yn.astype(BF16)
        dy = _dot_nt(dh1b_ref[...], wout_ref[...])
        pre_u = pre_ref[:, :GH].astype(F32)
        pre_v = pre_ref[:, GH:].astype(F32)
        u = _gelu(pre_u)
        v = _gelu(pre_v)
        xc = v - jnp.mean(v, axis=-1, keepdims=True)
        rs = lax.rsqrt(jnp.mean(xc * xc, axis=-1, keepdims=True) + EPS)
        vhat = xc * rs
        lng = lng_ref[...]
        vnb = (vhat * lng + lnb_ref[...]).astype(BF16)
        dsv = dy * u
        dsvb = dsv.astype(BF16)
        tril_m = tril_ref[...]
        gg_u = _gelu_grad(pre_u)
        for ch in range(TM // GC):
            rows = slice(ch * GC, (ch + 1) * GC)
            dbs_ref[...] += dsv[rows, :]
            for g in range(GG):
                cols = slice(g * GD, (g + 1) * GD)
                sv = _dot(wm_ref[g], vnb[rows, cols]) + b_ref[:, cols]
                dpre_ref[rows, cols] = (dy[rows, cols] * sv * gg_u[rows, cols]).astype(BF16)
                dvn_s[rows, cols] = _dot(wmt_ref[g], dsvb[rows, cols])
                dws_ref[g] += _dot_nt(dsvb[rows, cols], vnb[rows, cols]) * tril_m
        dvn = dvn_s[...]
        _acc_rows(dlng_ref, dvn * vhat)
        _acc_rows(dlnb_ref, dvn)
        dvhat = dvn * lng
        dv = rs * (dvhat - jnp.mean(dvhat, axis=-1, keepdims=True)
                   - vhat * jnp.mean(dvhat * vhat, axis=-1, keepdims=True))
        dpre_v = (dv * _gelu_grad(pre_v)).astype(BF16)
        dpre_ref[:, GH:] = dpre_v
        dhn = _dot_nt(dpre_ref[:, :GH], win_ref[:, :GH]) + _dot_nt(dpre_v, win_ref[:, GH:])
        _acc_rows(dg1_ref, dhn * xhat)
        dh_ref[...] = dh1_ref[...] + _rms_bwd(dhn, g1, xhat, rx, D)

    return pl.pallas_call(
        body, name="gmlp_bwd", grid=(t // TM,),
        in_specs=[_row(TM, D), _row(TM, D), _row(TM, D), _row(TM, 2 * GH), _const((1, D)), _const((D, 2 * GH)),
                  _const((1, GH)), _const((1, GH)), _const((GG, GC, GC)), _const((GG, GC, GC)), _const((GC, GH)),
                  _const((GH, D)), _const((GC, GC))],
        out_specs=[_row(TM, D), _row(TM, D), _row(TM, 2 * GH), _const((GG, GC, GC)), _const((GC, GH)),
                   _const((8, GH)), _const((8, GH)), _const((8, D))],
        out_shape=[_sds((t, D), F32), _sds((t, D), BF16), _sds((t, 2 * GH), BF16), _sds((GG, GC, GC), F32),
                   _sds((GC, GH), F32), _sds((8, GH), F32), _sds((8, GH), F32), _sds((8, D), F32)],
        scratch_shapes=[pltpu.VMEM((TM, GH), F32)],
        compiler_params=_cp("arbitrary"),
    )(dh1, dh1b, h, pre, g1, win, lng, lnb, wm, wmt, bfull, wout, tril)


def mm_tn(a, b, col_shards=1):
    t, k = a.shape
    n = b.shape[1]
    tk = min(k, 1024)
    ns = n // col_shards
    tn = min(ns, 1024)
    tt = 512
    nbn = ns // tn

    def body(a_ref, b_ref, o_ref):
        @pl.when(pl.program_id(2) == 0)
        def _():
            o_ref[...] = jnp.zeros_like(o_ref)

        o_ref[...] += _dot_tn(a_ref[...].astype(BF16), b_ref[...].astype(BF16))

    if col_shards == 1:
        out_spec = pl.BlockSpec((tk, tn), lambda i, j, s: (i, j))
        out_shape = _sds((k, n), F32)
    else:
        out_spec = pl.BlockSpec((None, tk, tn), lambda i, j, s: (j // nbn, i, j % nbn))
        out_shape = _sds((col_shards, k, ns), F32)
    return pl.pallas_call(
        body, name="mm_tn", grid=(k // tk, n // tn, t // tt),
        in_specs=[pl.BlockSpec((tt, tk), lambda i, j, s: (s, i)), pl.BlockSpec((tt, tn), lambda i, j, s: (s, j))],
        out_specs=out_spec, out_shape=out_shape,
        compiler_params=_cp("parallel", "parallel", "arbitrary"),
    )(a, b)


def adamw(w, g, m, v):
    rows, cols = w.shape
    tr = rows if rows <= 512 else next(r for r in (512, 384, 256, 128) if rows % r == 0)
    c1 = 1.0 - ADAM_B1 ** ADAM_STEP
    c2 = 1.0 - ADAM_B2 ** ADAM_STEP

    def body(w_ref, g_ref, m_ref, v_ref, d_ref, mo_ref, vo_ref):
        gv = g_ref[...]
        mn = ADAM_B1 * m_ref[...] + (1.0 - ADAM_B1) * gv
        vn = ADAM_B2 * v_ref[...] + (1.0 - ADAM_B2) * (gv * gv)
        mo_ref[...] = mn
        vo_ref[...] = vn
        d_ref[...] = -ADAM_LR * ((mn / c1) / (jnp.sqrt(vn / c2) + ADAM_EPS) + ADAM_WD * w_ref[...])

    spec = pl.BlockSpec((tr, cols), lambda i: (i, 0))
    return pl.pallas_call(
        body, name="adamw", grid=(rows // tr,),
        in_specs=[spec] * 4, out_specs=[spec] * 3, out_shape=[_sds((rows, cols), F32)] * 3,
        compiler_params=_cp("parallel"),
    )(w, g, m, v)


def _place():
    return lax.axis_index("x"), lax.axis_index("y"), lax.axis_index("c")


def _other_chips(x, y):
    return [(1 - x, y), (x, 1 - y), (1 - x, 1 - y)]


_ANY = pl.BlockSpec(memory_space=pl.ANY)


def gather_shards(mine):
    rr, cc = mine.shape
    hh = rr // 2
    assert rr % 32 == 0, rr

    def body(x_ref, o_ref, send_sems, recv_sems, local_sem):
        x, y, c = _place()
        k = 2 * x + y
        chips = _other_chips(x, y)
        mine_half = pl.ds(pl.multiple_of(c * hh, 16), hh)
        other_half = pl.ds(pl.multiple_of((1 - c) * hh, 16), hh)

        def copy(j, src, dst, to):
            return pltpu.make_async_remote_copy(src_ref=src, dst_ref=dst, send_sem=send_sems.at[j],
                                                recv_sem=recv_sems.at[j], device_id=to, device_id_type=MESH)

        local = pltpu.make_async_copy(x_ref, o_ref.at[k], local_sem)
        local.start()
        sends = [copy(j, x_ref.at[mine_half], o_ref.at[k, mine_half], (cx, cy, c))
                 for j, (cx, cy) in enumerate(chips)]
        for cp in sends:
            cp.start()
        passed = []
        for j, (cx, cy) in enumerate(chips):
            kk = 2 * cx + cy
            copy(j, x_ref.at[mine_half], o_ref.at[kk, mine_half], (cx, cy, c)).wait_recv()
            cp = copy(3 + j, o_ref.at[kk, mine_half], o_ref.at[kk, mine_half], (x, y, 1 - c))
            cp.start()
            passed.append(cp)
        for j, (cx, cy) in enumerate(chips):
            kk = 2 * cx + cy
            copy(3 + j, o_ref.at[kk, other_half], o_ref.at[kk, other_half], (x, y, 1 - c)).wait_recv()
        for cp in sends + passed:
            cp.wait_send()
        local.wait()

    return pl.pallas_call(
        body, name="gather_shards", in_specs=[_ANY], out_specs=_ANY,
        out_shape=_sds((N_CHIPS, rr, cc), mine.dtype),
        scratch_shapes=[pltpu.SemaphoreType.DMA((6,)), pltpu.SemaphoreType.DMA((6,)), pltpu.SemaphoreType.DMA],
    )(mine)


def swap_halves(g):
    _, rr, cc = g.shape
    hh = rr // 2

    def body(g_ref, o_ref, send_sems, recv_sems):
        x, y, c = _place()
        other_half = pl.ds(pl.multiple_of((1 - c) * hh, 16), hh)
        copies = [pltpu.make_async_remote_copy(src_ref=g_ref.at[k, other_half], dst_ref=o_ref.at[k],
                                               send_sem=send_sems.at[k], recv_sem=recv_sems.at[k],
                                               device_id=(x, y, 1 - c), device_id_type=MESH)
                  for k in range(N_CHIPS)]
        for cp in copies:
            cp.start()
        for cp in copies:
            cp.wait()

    return pl.pallas_call(
        body, name="swap_halves", in_specs=[_ANY], out_specs=_ANY,
        out_shape=_sds((N_CHIPS, hh, cc), g.dtype),
        scratch_shapes=[pltpu.SemaphoreType.DMA((N_CHIPS,)), pltpu.SemaphoreType.DMA((N_CHIPS,))],
    )(g)


def chip_sum(place, g32, got):
    _, rr, cc = g32.shape
    hh = rr // 2
    tr = SUM_ROWS
    assert rr % 2 == 0 and hh % tr == 0, (rr, tr)
    nb = hh // tr

    def body(place_ref, g_ref, got_ref, own_ref, all_ref):
        s = g_ref[...] + got_ref[...].astype(F32)
        all_ref[...] = s.astype(BF16)
        own_ref[...] = g_ref[place_ref[1]] + got_ref[place_ref[1]].astype(F32)

    return pl.pallas_call(
        body, name="chip_sum",
        grid_spec=pltpu.PrefetchScalarGridSpec(
            num_scalar_prefetch=1, grid=(nb,),
            in_specs=[pl.BlockSpec((N_CHIPS, tr, cc), lambda i, pr: (0, pr[0] * nb + i, 0)),
                      pl.BlockSpec((N_CHIPS, tr, cc), lambda i, pr: (0, i, 0))],
            out_specs=[pl.BlockSpec((tr, cc), lambda i, pr: (i, 0)),
                       pl.BlockSpec((N_CHIPS, tr, cc), lambda i, pr: (0, i, 0))]),
        out_shape=[_sds((hh, cc), F32), _sds((N_CHIPS, hh, cc), BF16)],
        compiler_params=_cp("parallel"),
    )(place, g32, got)


def scatter_to_chips(s):
    _, hh, cc = s.shape

    def body(s_ref, o_ref, send_sems, recv_sems):
        x, y, c = _place()
        copies = [pltpu.make_async_remote_copy(src_ref=s_ref.at[2 * cx + cy], dst_ref=o_ref.at[j],
                                               send_sem=send_sems.at[j], recv_sem=recv_sems.at[j],
                                               device_id=(cx, cy, c), device_id_type=MESH)
                  for j, (cx, cy) in enumerate(_other_chips(x, y))]
        for cp in copies:
            cp.start()
        for cp in copies:
            cp.wait()

    return pl.pallas_call(
        body, name="scatter_to_chips", in_specs=[_ANY], out_specs=_ANY,
        out_shape=_sds((3, hh, cc), s.dtype),
        scratch_shapes=[pltpu.SemaphoreType.DMA((3,)), pltpu.SemaphoreType.DMA((3,))],
    )(s)


def final_sum(own, got):
    hh, cc = own.shape
    tr = SUM_ROWS
    assert hh % tr == 0, (hh, tr)

    def body(own_ref, got_ref, o_ref):
        o_ref[...] = ((own_ref[...] + got_ref[0].astype(F32)) + got_ref[1].astype(F32)) + got_ref[2].astype(F32)

    return pl.pallas_call(
        body, name="final_sum", grid=(hh // tr,),
        in_specs=[pl.BlockSpec((tr, cc), lambda i: (i, 0)), pl.BlockSpec((3, tr, cc), lambda i: (0, i, 0))],
        out_specs=pl.BlockSpec((tr, cc), lambda i: (i, 0)),
        out_shape=_sds((hh, cc), F32),
        compiler_params=_cp("parallel"),
    )(own, got)


def share_with_sibling(f):
    hh, cc = f.shape

    def body(f_ref, o_ref, send_sem, recv_sem, local_sem):
        x, y, c = _place()
        mine_half = pl.ds(pl.multiple_of(c * hh, 8), hh)
        local = pltpu.make_async_copy(f_ref, o_ref.at[mine_half], local_sem)
        local.start()
        cp = pltpu.make_async_remote_copy(src_ref=f_ref, dst_ref=o_ref.at[mine_half], send_sem=send_sem,
                                          recv_sem=recv_sem, device_id=(x, y, 1 - c), device_id_type=MESH)
        cp.start()
        cp.wait()
        local.wait()

    return pl.pallas_call(
        body, name="share_with_sibling", in_specs=[_ANY], out_specs=_ANY,
        out_shape=_sds((2 * hh, cc), f.dtype),
        scratch_shapes=[pltpu.SemaphoreType.DMA, pltpu.SemaphoreType.DMA, pltpu.SemaphoreType.DMA],
    )(f)


def allreduce_small(part):
    rr, cc = part.shape
    n_dev = 8

    def body(x_ref, all_ref, sum_ref, send_sems, recv_sems, local_sem):
        x, y, c = _place()
        me, sibling = (x, y, c), (x, y, 1 - c)
        chips = _other_chips(x, y)

        def rows(px, py, pc):
            return all_ref.at[pl.ds(pl.multiple_of((4 * px + 2 * py + pc) * rr, 8), rr), :]

        def copy(j, block, to, src=None):
            return pltpu.make_async_remote_copy(src_ref=rows(*block) if src is None else src, dst_ref=rows(*block),
                                                send_sem=send_sems.at[j], recv_sem=recv_sems.at[j],
                                                device_id=to, device_id_type=MESH)

        mine = pltpu.make_async_copy(x_ref, rows(*me), local_sem)
        mine.start()
        first = [copy(0, me, sibling, src=x_ref)]
        first += [copy(1 + j, me, (*chip, c), src=x_ref) for j, chip in enumerate(chips)]
        for cp in first:
            cp.start()
        passed = [copy(4 + j, (*chip, c), sibling) for j, chip in enumerate(chips)]
        for j, chip in enumerate(chips):
            copy(1 + j, (*chip, c), me).wait_recv()
            passed[j].start()
        copy(0, sibling, me).wait_recv()
        for j, chip in enumerate(chips):
            copy(4 + j, (*chip, 1 - c), me).wait_recv()
        for cp in first + passed:
            cp.wait_send()
        mine.wait()
        acc = all_ref[0:rr, :]
        for d in range(1, n_dev):
            acc = acc + all_ref[d * rr:(d + 1) * rr, :]
        sum_ref[...] = acc

    vm = pl.BlockSpec(memory_space=pltpu.VMEM)
    return pl.pallas_call(
        body, name="allreduce_small", in_specs=[vm], out_specs=[vm, vm],
        out_shape=[_sds((n_dev * rr, cc), F32), _sds((rr, cc), F32)],
        scratch_shapes=[pltpu.SemaphoreType.DMA((7,)), pltpu.SemaphoreType.DMA((7,)), pltpu.SemaphoreType.DMA],
        compiler_params=pltpu.CompilerParams(vmem_limit_bytes=VMEM_LIMIT),
    )(part)[1]


_BIG = [
    ("mla_w_down", (256, LAT), 0), ("mla_w_uq", (QL, 384), 1), ("mla_w_ukv", (KVL, 512), 1),
    ("mla_w_out", (256, D), 0), ("gmlp_w_in", (D, D), 1), ("gmlp_w_out", (512, D), 0),
    ("ffn_w_up", (D, D), 1), ("ffn_w_down", (D, D), 0), ("ple_w_gate", (256, D), 0), ("ple_w_proj", (PLE, 256), 1),
]
_SMALL = ["norm_mix", "norm_ffn", "norm_ple", "mla_q_lora_g", "mla_kv_lora_g", "mla_q_nope_g", "mla_q_rope_g",
          "mla_k_nope_g", "mla_k_rope_g", "gmlp_ln_g", "gmlp_ln_b", "gmlp_w_s", "gmlp_b_s"]
_LN_ROWS = 128


def _rows_of(a):
    return a.size // D


def _pack_rows(parts, dtype, pad_to=None):
    flat = jnp.concatenate([p.reshape(-1).astype(dtype) for p in parts])
    if pad_to is not None:
        flat = jnp.pad(flat, (0, pad_to * D - flat.size))
    return flat.reshape(-1, D)


def _unshard(seg, shape, dim):
    a, b = shape
    s = seg.reshape(N_CHIPS, -1, a, b)
    if dim == 0:
        return jnp.transpose(s, (1, 0, 2, 3)).reshape(-1, N_CHIPS * a, b)
    return jnp.transpose(s, (1, 2, 0, 3)).reshape(-1, a, N_CHIPS * b)


def _to_shards(full, dim):
    l, aa, bb = full.shape
    if dim == 0:
        s = jnp.transpose(full.reshape(l, N_CHIPS, aa // N_CHIPS, bb), (1, 0, 2, 3))
    else:
        s = jnp.transpose(full.reshape(l, aa, N_CHIPS, bb // N_CHIPS), (2, 0, 1, 3))
    return s.reshape(N_CHIPS, -1, D)


def _pad_lanes(g):
    return jnp.pad(g, ((0, 0), (0, LANES - g.shape[1])))


def _split_uq(wuq):
    l = wuq.shape[0]
    w = wuq.reshape(l, QL, HEADS, DN + DR)
    nope = w[..., :DN].reshape(l, QL, HEADS * DN)
    rope = jnp.pad(w[..., DN:], ((0, 0), (0, 0), (0, 0), (0, LANES - DR))).reshape(l, QL, HEADS * LANES)
    return jnp.concatenate([nope, rope], axis=-1)


def _merge_uq(d):
    nope = d[:, :HEADS * DN].reshape(QL, HEADS, DN)
    rope = d[:, HEADS * DN:].reshape(QL, HEADS, LANES)[..., :DR]
    return jnp.concatenate([nope, rope], axis=-1).reshape(QL, HEADS * (DN + DR))


def _rope_tables(positions):
    inv_freq = ROPE_BASE ** (-(jnp.arange(0, DR, 2, dtype=F32) / DR))
    ang = positions.reshape(-1).astype(F32)[:, None] * inv_freq
    z = jnp.zeros((ang.shape[0], LANES - DR), F32)
    return (jnp.concatenate([jnp.cos(ang), jnp.cos(ang), z], axis=1),
            jnp.concatenate([jnp.sin(ang), jnp.sin(ang), z], axis=1))


def kernel(x, p, positions, norm_mix, norm_ffn, norm_ple, mla_w_down, mla_q_lora_g, mla_kv_lora_g, mla_w_uq, mla_w_ukv, mla_q_nope_g, mla_q_rope_g, mla_k_nope_g, mla_k_rope_g, mla_w_out, gmlp_w_in, gmlp_ln_g, gmlp_ln_b, gmlp_w_s, gmlp_b_s, gmlp_w_out, ffn_w_up, ffn_w_down, ple_w_gate, ple_w_proj, loss_target, m_norm_mix, m_norm_ffn, m_norm_ple, m_mla_w_down, m_mla_q_lora_g, m_mla_kv_lora_g, m_mla_w_uq, m_mla_w_ukv, m_mla_q_nope_g, m_mla_q_rope_g, m_mla_k_nope_g, m_mla_k_rope_g, m_mla_w_out, m_gmlp_w_in, m_gmlp_ln_g, m_gmlp_ln_b, m_gmlp_w_s, m_gmlp_b_s, m_gmlp_w_out, m_ffn_w_up, m_ffn_w_down, m_ple_w_gate, m_ple_w_proj, v_norm_mix, v_norm_ffn, v_norm_ple, v_mla_w_down, v_mla_q_lora_g, v_mla_kv_lora_g, v_mla_w_uq, v_mla_w_ukv, v_mla_q_nope_g, v_mla_q_rope_g, v_mla_k_nope_g, v_mla_k_rope_g, v_mla_w_out, v_gmlp_w_in, v_gmlp_ln_g, v_gmlp_ln_b, v_gmlp_w_s, v_gmlp_b_s, v_gmlp_w_out, v_ffn_w_up, v_ffn_w_down, v_ple_w_gate, v_ple_w_proj):
    args = dict(locals())
    weights = {n: args[n] for n, _, _ in _BIG}
    weights.update({n: args[n] for n in _SMALL})
    depth = norm_mix.shape[0]
    nb, seq, _ = x.shape
    t = nb * seq
    assert seq % TQ == 0 and seq % TM == 0 and t % 512 == 0, (nb, seq)
    cx = lax.axis_index("x")
    cy = lax.axis_index("y")
    cc = lax.axis_index("c")
    chip = 2 * cx + cy

    ln = jnp.stack([gmlp_ln_g, gmlp_ln_b]).astype(F32)
    ln_rows = _pack_rows([lax.bitcast_convert_type(ln, BF16)], BF16, pad_to=_LN_ROWS)
    mine = jnp.concatenate([_pack_rows([weights[n] for n, _, _ in _BIG], BF16), ln_rows])
    allw = gather_shards(mine)
    full = {}
    r0 = 0
    for n, shape, dim in _BIG:
        nrows = _rows_of(weights[n])
        full[n] = _unshard(allw[:, r0:r0 + nrows], shape, dim)
        r0 += nrows
    ln_all = lax.bitcast_convert_type(allw[:, r0:r0 + _LN_ROWS].reshape(N_CHIPS, -1)[:, :ln.size * 2]
                                      .reshape(N_CHIPS, 2, ln.shape[1], ln.shape[2], 2), F32)
    ln_all = jnp.transpose(ln_all, (1, 2, 0, 3)).reshape(2, ln.shape[1], GH)

    wdn = jnp.pad(full["mla_w_down"], ((0, 0), (0, 0), (0, LATP - LAT)))
    wuq = _split_uq(full["mla_w_uq"])
    tril = jnp.tril(jnp.ones((GC, GC), F32))
    wm = (gmlp_w_s * tril).astype(BF16)
    wmt = jnp.swapaxes(wm, -1, -2)
    bfull = jnp.repeat(jnp.swapaxes(gmlp_b_s, -1, -2), GD, axis=-1)
    cos, sin = _rope_tables(positions)
    row = lambda g: g.reshape(1, -1)
    gqr = _pad_lanes(mla_q_rope_g)
    gkr = _pad_lanes(mla_k_rope_g)

    h = x.reshape(t, D)
    pt = p.reshape(depth, t, PLE)
    saved = []
    for i in range(depth):
        j = i // 2
        s = dict(h=h)
        if i % 2 == 0:
            mla_args = (row(norm_mix[i]), wdn[j], row(mla_q_lora_g[j]), row(mla_kv_lora_g[j]), wuq[j],
                        full["mla_w_ukv"][j], row(mla_q_nope_g[j]), gqr[j:j + 1], row(mla_k_nope_g[j]), gkr[j:j + 1],
                        cos, sin)
            q, k, v = mla_pre_fwd(h, *mla_args)
            o, lse = flash_fwd(q, k, v, seq)
            s.update(q=q, k=k, v=v, o=o, lse=lse, mla_args=mla_args)
            y, wo = o, full["mla_w_out"][j]
        else:
            y, pre = gmlp_fwd(h, row(norm_mix[i]), full["gmlp_w_in"][j], row(ln_all[0, j]), row(ln_all[1, j]),
                              wm[j], bfull[j])
            s.update(pre=pre)
            wo = full["gmlp_w_out"][j]
        h1, h2, hn2, r = mixffn_fwd(h, y, wo, row(norm_ffn[i]), full["ffn_w_up"][i], full["ffn_w_down"][i])
        h, hn3 = ple_fwd(h2, pt[i], row(norm_ple[i]), full["ple_w_gate"][i], full["ple_w_proj"][i])
        s.update(y=y, wo=wo, h1=h1, h2=h2, hn2=hn2, r=r, hn3=hn3)
        saved.append(s)

    dh, loss_part = loss_head(h, loss_target.reshape(t, D))
    loss = lax.psum(loss_part[0, 0], ("x", "y", "c"))

    gw = {n: [None] * weights[n].shape[0] for n, _, _ in _BIG}
    gs = {n: [None] * weights[n].shape[0] for n in _SMALL}
    for i in reversed(range(depth)):
        j = i // 2
        s = saved[i]
        dh2, dh2b, dgt, dpp, dg3 = ple_bwd(dh, s["h2"], pt[i], row(norm_ple[i]), full["ple_w_gate"][i],
                                           full["ple_w_proj"][i])
        gs["norm_ple"][i] = dg3[0]
        gw["ple_w_gate"][i] = mm_tn(s["hn3"], dgt).reshape(N_CHIPS, -1, D)
        gw["ple_w_proj"][i] = mm_tn(pt[i], dpp, col_shards=N_CHIPS).reshape(N_CHIPS, -1, D)
        dh1, dh1b, du, a, dg2 = ffn_bwd(dh2, dh2b, s["h1"], s["r"], row(norm_ffn[i]), full["ffn_w_up"][i],
                                        full["ffn_w_down"][i])
        gs["norm_ffn"][i] = dg2[0]
        gw["ffn_w_down"][i] = mm_tn(a, dh2b).reshape(N_CHIPS, -1, D)
        gw["ffn_w_up"][i] = mm_tn(s["hn2"], du, col_shards=N_CHIPS).reshape(N_CHIPS, -1, D)
        dwo = mm_tn(s["y"], dh1b)
        if i % 2 == 0:
            gw["mla_w_out"][j] = dwo.reshape(N_CHIPS, -1, D)
            do = linear_nt(dh1b, s["wo"])
            dq, dk, dv = flash_bwd(s["q"], s["k"], s["v"], s["o"], do, s["lse"], seq)
            (dh, hn1, cq, ckv, dqp, dkvp, dlat, dg1, dgq, dgkv, dgqn, dgqr, dgkn, dgkr) = mla_pre_bwd(
                dq, dk, dv, dh1, s["h"], *s["mla_args"])
            gs["norm_mix"][i] = dg1[0]
            gs["mla_q_lora_g"][j] = dgq[0]
            gs["mla_kv_lora_g"][j] = dgkv[0]
            gs["mla_q_nope_g"][j] = dgqn[0]
            gs["mla_q_rope_g"][j] = dgqr[0, :DR]
            gs["mla_k_nope_g"][j] = dgkn[0]
            gs["mla_k_rope_g"][j] = dgkr[0, :DR]
            gw["mla_w_down"][j] = _to_shards(mm_tn(hn1, dlat)[None, :, :LAT], 0)[:, :, :]
            gw["mla_w_uq"][j] = _to_shards(_merge_uq(mm_tn(cq, dqp))[None], 1)
            gw["mla_w_ukv"][j] = mm_tn(ckv, dkvp, col_shards=N_CHIPS).reshape(N_CHIPS, -1, D)
        else:
            gw["gmlp_w_out"][j] = dwo.reshape(N_CHIPS, -1, D)
            dh, hn1, dpre, dws, dbs, dlng, dlnb, dg1 = gmlp_bwd(
                dh1, dh1b, s["h"], s["pre"], row(norm_mix[i]), full["gmlp_w_in"][j], row(ln_all[0, j]),
                row(ln_all[1, j]), wm[j], wmt[j], bfull[j], s["wo"], tril)
            gs["norm_mix"][i] = dg1[0]
            gs["gmlp_ln_g"][j] = dlng[0]
            gs["gmlp_ln_b"][j] = dlnb[0]
            gs["gmlp_w_s"][j] = dws
            gs["gmlp_b_s"][j] = jnp.sum(dbs.reshape(GC, GG, GD), axis=-1).T
            gw["gmlp_w_in"][j] = mm_tn(hn1, dpre, col_shards=N_CHIPS).reshape(N_CHIPS, -1, D)
    grad_x = dh.reshape(x.shape)

    g32 = jnp.concatenate(
        [jnp.concatenate(gw[n], axis=1) for n, _, _ in _BIG] + [jnp.zeros((N_CHIPS, _LN_ROWS, D), F32)], axis=1)
    place = jnp.stack([cc, chip]).astype(jnp.int32)
    got = swap_halves(g32.astype(BF16))
    own, sums = chip_sum(place, g32, got)
    reduced = share_with_sibling(final_sum(own, scatter_to_chips(sums)))
    grads = {}
    r0 = 0
    for n, shape, dim in _BIG:
        nrows = _rows_of(weights[n])
        grads[n] = reduced[r0:r0 + nrows].reshape(weights[n].shape)
        r0 += nrows

    small_sizes = [weights[n].size if n not in ("gmlp_ln_g", "gmlp_ln_b") else weights[n].shape[0] * GH
                   for n in _SMALL]
    small_rows = -(-sum(small_sizes) // (8 * D)) * 8
    part = _pack_rows([jnp.stack(gs[n]) for n in _SMALL], F32, pad_to=small_rows)
    tot = allreduce_small(part).reshape(-1)
    off = 0
    for n, sz in zip(_SMALL, small_sizes):
        gsum = tot[off:off + sz]
        off += sz
        if n in ("gmlp_ln_g", "gmlp_ln_b"):
            gsum = lax.dynamic_slice_in_dim(gsum.reshape(-1, GH), chip * (GH // N_CHIPS), GH // N_CHIPS, axis=1)
        grads[n] = gsum.reshape(weights[n].shape)

    delta, new_m, new_v = {}, {}, {}
    for n, _, _ in _BIG:
        w2 = weights[n].reshape(-1, weights[n].shape[-1])
        d, mn, vn = adamw(w2, grads[n].reshape(w2.shape), args["m_" + n].reshape(w2.shape),
                          args["v_" + n].reshape(w2.shape))
        delta[n], new_m[n], new_v[n] = (a.reshape(weights[n].shape) for a in (d, mn, vn))
    own_sizes = [weights[n].size for n in _SMALL]
    own_rows = -(-sum(own_sizes) // (8 * D)) * 8
    packed = [_pack_rows([src[n] for n in _SMALL], F32, pad_to=own_rows)
              for src in (weights, grads, {n: args["m_" + n] for n in _SMALL}, {n: args["v_" + n] for n in _SMALL})]
    outs = adamw(*packed)
    off = 0
    for n, sz in zip(_SMALL, own_sizes):
        for dst, o in zip((delta, new_m, new_v), outs):
            dst[n] = o.reshape(-1)[off:off + sz].reshape(weights[n].shape)
        off += sz

    order = ["norm_mix", "norm_ffn", "norm_ple", "mla_w_down", "mla_q_lora_g", "mla_kv_lora_g", "mla_w_uq",
             "mla_w_ukv", "mla_q_nope_g", "mla_q_rope_g", "mla_k_nope_g", "mla_k_rope_g", "mla_w_out", "gmlp_w_in",
             "gmlp_ln_g", "gmlp_ln_b", "gmlp_w_s", "gmlp_b_s", "gmlp_w_out", "ffn_w_up", "ffn_w_down", "ple_w_gate",
             "ple_w_proj"]
    return (loss, grad_x, *[grads[n] for n in order], *[delta[n] for n in order], *[new_m[n] for n in order],
            *[new_v[n] for n in order])
```

```python
import functools

import jax
import jax.numpy as jnp
from jax import lax
from jax.experimental import pallas as pl
from jax.experimental.pallas import tpu as pltpu

F32 = jnp.float32
BF16 = jnp.bfloat16
MESH = pl.DeviceIdType.MESH

D = 1024
HEADS = 8
DN = 128
DR = 64
QL = 384
KVL = 256
LAT = 704
LATP = 768
DFF = 4096
GH = 2048
GC = 128
GG = 8
GD = 256
PLE = 256
EPS = 1e-6
ROPE_BASE = 10000.0
SM_SCALE = (DN + DR) ** -0.5
N_CHIPS = 4
LANES = 128

ADAM_LR = 0.001
ADAM_B1 = 0.9
ADAM_B2 = 0.999
ADAM_EPS = 1e-08
ADAM_WD = 0.01
ADAM_STEP = 10

TM = 256
TQ = 256
SUM_ROWS = 320
VMEM_LIMIT = 56 * 1024 * 1024


def _cp(*sem):
    return pltpu.CompilerParams(dimension_semantics=sem, vmem_limit_bytes=VMEM_LIMIT)


def _dot(a, b):
    return jnp.dot(a, b, preferred_element_type=F32)


def _dot_nt(a, b):
    return lax.dot_general(a, b, (((1,), (1,)), ((), ())), preferred_element_type=F32)


def _dot_tn(a, b):
    return lax.dot_general(a, b, (((0,), (0,)), ((), ())), preferred_element_type=F32)


def _rms(x, g, n):
    r = lax.rsqrt(jnp.sum(x * x, axis=-1, keepdims=True) * (1.0 / n) + EPS)
    xhat = x * r
    return xhat * g, xhat, r


def _rms_bwd(dy, g, xhat, r, n):
    dxhat = dy * g
    return r * (dxhat - xhat * (jnp.sum(dxhat * xhat, axis=-1, keepdims=True) * (1.0 / n)))


def _rope(x, c, s):
    return x * c + (pltpu.roll(x, 32, 1) - pltpu.roll(x, 96, 1)) * s


def _rope_t(dy, c, s):
    w = dy * s
    return dy * c + pltpu.roll(w, 96, 1) - pltpu.roll(w, 32, 1)


def _sigmoid(x):
    return 1.0 / (1.0 + jnp.exp(-x))


_GELU_K = 0.7978845608028654
_GELU_C = 0.044715


def _gelu(x):
    return 0.5 * x * (1.0 + jnp.tanh(_GELU_K * (x + _GELU_C * x * x * x)))


def _gelu_grad(x):
    t = jnp.tanh(_GELU_K * (x + _GELU_C * x * x * x))
    return 0.5 * (1.0 + t) + 0.5 * x * (1.0 - t * t) * (_GELU_K * (1.0 + 3.0 * _GELU_C * x * x))


def _acc_rows(ref, val):
    ref[...] += jnp.broadcast_to(jnp.sum(val, axis=0, keepdims=True), ref.shape)


def _row(tm, c):
    return pl.BlockSpec((tm, c), lambda i: (i, 0))


def _const(shape):
    nd = len(shape)
    return pl.BlockSpec(shape, lambda i: (0,) * nd, pipeline_mode=pl.Buffered(1))


def _sds(shape, dtype):
    return jax.ShapeDtypeStruct(shape, dtype)


def mixffn_fwd(h, y, wo, g2, wu, wd):
    t, k = y.shape

    def body(h_ref, y_ref, wo_ref, g_ref, wu_ref, wd_ref, h1_ref, h2_ref, hn_ref, r_ref):
        h1 = h_ref[...] + _dot(y_ref[...], wo_ref[...])
        h1_ref[...] = h1
        yn, _, _ = _rms(h1, g_ref[...], D)
        hn = yn.astype(BF16)
        hn_ref[...] = hn
        f = jnp.zeros((TM, D), F32)
        for c in range(DFF // D):
            cs = slice(c * D, (c + 1) * D)
            r = jnp.maximum(_dot(hn, wu_ref[:, cs]), 0.0)
            r_ref[:, cs] = r.astype(BF16)
            f = f + _dot((r * r).astype(BF16), wd_ref[cs, :])
        h2_ref[...] = h1 + f

    return pl.pallas_call(
        body, name="mixffn_fwd", grid=(t // TM,),
        in_specs=[_row(TM, D), _row(TM, k), _const((k, D)), _const((1, D)), _const((D, DFF)), _const((DFF, D))],
        out_specs=[_row(TM, D), _row(TM, D), _row(TM, D), _row(TM, DFF)],
        out_shape=[_sds((t, D), F32), _sds((t, D), F32), _sds((t, D), BF16), _sds((t, DFF), BF16)],
        compiler_params=_cp("parallel"),
    )(h, y, wo, g2, wu, wd)


def ple_fwd(h2, p, g3, wg, wp):
    t = h2.shape[0]

    def body(h_ref, p_ref, g_ref, wg_ref, wp_ref, h3_ref, hn_ref):
        x = h_ref[...]
        yn, _, _ = _rms(x, g_ref[...], D)
        hn = yn.astype(BF16)
        hn_ref[...] = hn
        gt = _dot(hn, wg_ref[...])
        pp = _dot(p_ref[...].astype(BF16), wp_ref[...])
        h3_ref[...] = x + _sigmoid(gt) * pp

    return pl.pallas_call(
        body, name="ple_fwd", grid=(t // TM,),
        in_specs=[_row(TM, D), _row(TM, PLE), _const((1, D)), _const((D, D)), _const((PLE, D))],
        out_specs=[_row(TM, D), _row(TM, D)],
        out_shape=[_sds((t, D), F32), _sds((t, D), BF16)],
        compiler_params=_cp("parallel"),
    )(h2, p, g3, wg, wp)


def _mla_project(h_ref, g1_ref, wdn_ref, gq_ref, gkv_ref, wuq_ref, wukv_ref):
    x = h_ref[...]
    yn, xhat, rx = _rms(x, g1_ref[...], D)
    hn = yn.astype(BF16)
    lat = _dot(hn, wdn_ref[...])
    cq, cqhat, rq = _rms(lat[:, :QL], gq_ref[...], QL)
    ckv, ckvhat, rkv = _rms(lat[:, QL:QL + KVL], gkv_ref[...], KVL)
    kr_raw = lat[:, QL + KVL:]
    cqb = cq.astype(BF16)
    ckvb = ckv.astype(BF16)
    qp = _dot(cqb, wuq_ref[...])
    kvp = _dot(ckvb, wukv_ref[...])
    return dict(xhat=xhat, rx=rx, hn=hn, cqhat=cqhat, rq=rq, ckvhat=ckvhat, rkv=rkv, kr_raw=kr_raw,
                cqb=cqb, ckvb=ckvb, qp=qp, kvp=kvp)


def mla_pre_fwd(h, g1, wdn, gq, gkv, wuq, wukv, gqn, gqr, gkn, gkr, cos, sin):
    t = h.shape[0]

    def body(h_ref, g1_ref, wdn_ref, gq_ref, gkv_ref, wuq_ref, wukv_ref, gqn_ref, gqr_ref, gkn_ref, gkr_ref,
             c_ref, s_ref, q_ref, k_ref, v_ref):
        m = _mla_project(h_ref, g1_ref, wdn_ref, gq_ref, gkv_ref, wuq_ref, wukv_ref)
        c = c_ref[...]
        s = s_ref[...]
        kr, _, _ = _rms(m["kr_raw"], gkr_ref[...], DR)
        krb = _rope(kr, c, s).astype(BF16)
        for hd in range(HEADS):
            qn, _, _ = _rms(m["qp"][:, hd * DN:(hd + 1) * DN], gqn_ref[...], DN)
            qr, _, _ = _rms(m["qp"][:, D + hd * LANES:D + (hd + 1) * LANES], gqr_ref[...], DR)
            q_ref[hd, :, 0:DN] = (qn * SM_SCALE).astype(BF16)
            q_ref[hd, :, DN:2 * DN] = (_rope(qr, c, s) * SM_SCALE).astype(BF16)
            kn, _, _ = _rms(m["kvp"][:, hd * 2 * DN:hd * 2 * DN + DN], gkn_ref[...], DN)
            k_ref[hd, :, 0:DN] = kn.astype(BF16)
            k_ref[hd, :, DN:2 * DN] = krb
            v_ref[hd] = m["kvp"][:, hd * 2 * DN + DN:(hd + 1) * 2 * DN].astype(BF16)

    hb = lambda w: pl.BlockSpec((HEADS, TM, w), lambda i: (0, i, 0))
    return pl.pallas_call(
        body, name="mla_pre_fwd", grid=(t // TM,),
        in_specs=[_row(TM, D), _const((1, D)), _const((D, LATP)), _const((1, QL)), _const((1, KVL)),
                  _const((QL, 2 * D)), _const((KVL, 2 * D)), _const((1, LANES)), _const((1, LANES)),
                  _const((1, LANES)), _const((1, LANES)), _row(TM, LANES), _row(TM, LANES)],
        out_specs=[hb(2 * DN), hb(2 * DN), hb(DN)],
        out_shape=[_sds((HEADS, t, 2 * DN), BF16), _sds((HEADS, t, 2 * DN), BF16), _sds((HEADS, t, DN), BF16)],
        compiler_params=_cp("parallel"),
    )(h, g1, wdn, gq, gkv, wuq, wukv, gqn, gqr, gkn, gkr, cos, sin)


def _diagonal_mask():
    return lax.broadcasted_iota(jnp.int32, (TQ, TQ), 1) <= lax.broadcasted_iota(jnp.int32, (TQ, TQ), 0)


def flash_fwd(q, k, v, seq):
    t = q.shape[1]
    nb = t // seq
    nq = seq // TQ

    def body(q_ref, k_ref, v_ref, o_ref, lse_ref):
        qi = pl.program_id(2)
        qv = q_ref[0]

        def step(j, carry, diagonal=False):
            m, l, acc = carry
            rows = pl.ds(pl.multiple_of(j * TQ, TQ), TQ)
            s = _dot_nt(qv, k_ref[0, rows, :])
            if diagonal:
                s = jnp.where(_diagonal_mask(), s, -1e30)
            m_new = jnp.maximum(m, jnp.max(s, axis=-1, keepdims=True))
            p = jnp.exp(s - m_new)
            alpha = jnp.exp(m - m_new)
            l = alpha * l + jnp.sum(p, axis=-1, keepdims=True)
            acc = alpha * acc + _dot(p.astype(BF16), v_ref[0, rows, :])
            return m_new, l, acc

        init = (jnp.full((TQ, 1), -1e30, F32), jnp.zeros((TQ, 1), F32), jnp.zeros((TQ, DN), F32))
        m, l, acc = step(qi, lax.fori_loop(0, qi, step, init), diagonal=True)
        o_ref[...] = (acc / l).astype(BF16)
        lse_ref[0] = m + jnp.log(l)

    return pl.pallas_call(
        body, name="flash_fwd", grid=(nb, HEADS, nq),
        in_specs=[pl.BlockSpec((1, TQ, 2 * DN), lambda b, h, i: (h, b * nq + i, 0)),
                  pl.BlockSpec((1, seq, 2 * DN), lambda b, h, i: (h, b, 0)),
                  pl.BlockSpec((1, seq, DN), lambda b, h, i: (h, b, 0))],
        out_specs=[pl.BlockSpec((TQ, DN), lambda b, h, i: (b * nq + i, h)),
                   pl.BlockSpec((1, TQ, 1), lambda b, h, i: (h, b * nq + i, 0))],
        out_shape=[_sds((t, HEADS * DN), BF16), _sds((HEADS, t, 1), F32)],
        compiler_params=_cp("parallel", "parallel", "arbitrary"),
    )(q, k, v)


def gmlp_fwd(h, g1, win, lng, lnb, wm, bfull):
    t = h.shape[0]

    def body(h_ref, g1_ref, win_ref, lng_ref, lnb_ref, wm_ref, b_ref, y_ref, pre_ref):
        yn, _, _ = _rms(h_ref[...], g1_ref[...], D)
        hn = yn.astype(BF16)
        pre_u = _dot(hn, win_ref[:, :GH])
        pre_v = _dot(hn, win_ref[:, GH:])
        pre_ref[:, :GH] = pre_u.astype(BF16)
        pre_ref[:, GH:] = pre_v.astype(BF16)
        u = _gelu(pre_u)
        v = _gelu(pre_v)
        xc = v - jnp.mean(v, axis=-1, keepdims=True)
        rs = lax.rsqrt(jnp.mean(xc * xc, axis=-1, keepdims=True) + EPS)
        vnb = (xc * rs * lng_ref[...] + lnb_ref[...]).astype(BF16)
        for ch in range(TM // GC):
            rows = slice(ch * GC, (ch + 1) * GC)
            for g in range(GG):
                cols = slice(g * GD, (g + 1) * GD)
                sv = _dot(wm_ref[g], vnb[rows, cols]) + b_ref[:, cols]
                y_ref[rows, cols] = (u[rows, cols] * sv).astype(BF16)

    return pl.pallas_call(
        body, name="gmlp_fwd", grid=(t // TM,),
        in_specs=[_row(TM, D), _const((1, D)), _const((D, 2 * GH)), _const((1, GH)), _const((1, GH)),
                  _const((GG, GC, GC)), _const((GC, GH))],
        out_specs=[_row(TM, GH), _row(TM, 2 * GH)],
        out_shape=[_sds((t, GH), BF16), _sds((t, 2 * GH), BF16)],
        compiler_params=_cp("parallel"),
    )(h, g1, win, lng, lnb, wm, bfull)


def loss_head(h, tgt):
    t = h.shape[0]

    def body(h_ref, t_ref, dh_ref, loss_ref):
        @pl.when(pl.program_id(0) == 0)
        def _():
            loss_ref[...] = jnp.zeros_like(loss_ref)

        e = h_ref[...] - t_ref[...]
        dh_ref[...] = e * (1.0 / D)
        part = jnp.sum(jnp.sum(e * e, axis=-1, keepdims=True), axis=0, keepdims=True) * (0.5 / D)
        loss_ref[...] += jnp.broadcast_to(part, loss_ref.shape)

    return pl.pallas_call(
        body, name="loss_head", grid=(t // TM,),
        in_specs=[_row(TM, D), _row(TM, D)],
        out_specs=[_row(TM, D), _const((8, LANES))],
        out_shape=[_sds((t, D), F32), _sds((8, LANES), F32)],
        compiler_params=_cp("arbitrary"),
    )(h, tgt)


def _zero_at_first_step(*refs):
    @pl.when(pl.program_id(0) == 0)
    def _():
        for r in refs:
            r[...] = jnp.zeros_like(r)


def ple_bwd(dh3, h2, p, g3, wg, wp):
    t = h2.shape[0]

    def body(dh_ref, h_ref, p_ref, g_ref, wg_ref, wp_ref, dh2_ref, dh2b_ref, dgt_ref, dpp_ref, dg_ref):
        _zero_at_first_step(dg_ref)
        dh3v = dh_ref[...]
        x = h_ref[...]
        g = g_ref[...]
        yn, xhat, r = _rms(x, g, D)
        gt = _dot(yn.astype(BF16), wg_ref[...])
        pp = _dot(p_ref[...].astype(BF16), wp_ref[...])
        sg = _sigmoid(gt)
        dgt = (dh3v * pp * sg * (1.0 - sg)).astype(BF16)
        dgt_ref[...] = dgt
        dpp_ref[...] = (dh3v * sg).astype(BF16)
        dhn = _dot_nt(dgt, wg_ref[...])
        _acc_rows(dg_ref, dhn * xhat)
        dh2 = dh3v + _rms_bwd(dhn, g, xhat, r, D)
        dh2_ref[...] = dh2
        dh2b_ref[...] = dh2.astype(BF16)

    return pl.pallas_call(
        body, name="ple_bwd", grid=(t // TM,),
        in_specs=[_row(TM, D), _row(TM, D), _row(TM, PLE), _const((1, D)), _const((D, D)), _const((PLE, D))],
        out_specs=[_row(TM, D), _row(TM, D), _row(TM, D), _row(TM, D), _const((8, D))],
        out_shape=[_sds((t, D), F32), _sds((t, D), BF16), _sds((t, D), BF16), _sds((t, D), BF16), _sds((8, D), F32)],
        compiler_params=_cp("arbitrary"),
    )(dh3, h2, p, g3, wg, wp)


def ffn_bwd(dh2, dh2b, h1, r, g2, wu, wd):
    t = h1.shape[0]

    def body(dh_ref, dhb_ref, h_ref, r_ref, g_ref, wu_ref, wd_ref, dh1_ref, dh1b_ref, du_ref, a_ref, dg_ref):
        _zero_at_first_step(dg_ref)
        dhb = dhb_ref[...]
        g = g_ref[...]
        _, xhat, rr = _rms(h_ref[...], g, D)
        dhn = jnp.zeros((TM, D), F32)
        for c in range(DFF // D):
            cs = slice(c * D, (c + 1) * D)
            rc = r_ref[:, cs].astype(F32)
            a_ref[:, cs] = (rc * rc).astype(BF16)
            da = _dot_nt(dhb, wd_ref[cs, :])
            du = (da * (2.0 * rc)).astype(BF16)
            du_ref[:, cs] = du
            dhn = dhn + _dot_nt(du, wu_ref[:, cs])
        _acc_rows(dg_ref, dhn * xhat)
        dh1 = dh_ref[...] + _rms_bwd(dhn, g, xhat, rr, D)
        dh1_ref[...] = dh1
        dh1b_ref[...] = dh1.astype(BF16)

    return pl.pallas_call(
        body, name="ffn_bwd", grid=(t // TM,),
        in_specs=[_row(TM, D), _row(TM, D), _row(TM, D), _row(TM, DFF), _const((1, D)), _const((D, DFF)),
                  _const((DFF, D))],
        out_specs=[_row(TM, D), _row(TM, D), _row(TM, DFF), _row(TM, DFF), _const((8, D))],
        out_shape=[_sds((t, D), F32), _sds((t, D), BF16), _sds((t, DFF), BF16), _sds((t, DFF), BF16),
                   _sds((8, D), F32)],
        compiler_params=_cp("arbitrary"),
    )(dh2, dh2b, h1, r, g2, wu, wd)


def linear_nt(a, w):
    t, n = a.shape
    k = w.shape[0]

    def body(a_ref, w_ref, o_ref):
        o_ref[...] = _dot_nt(a_ref[...], w_ref[...]).astype(BF16)

    return pl.pallas_call(
        body, name="linear_nt", grid=(t // TM,),
        in_specs=[_row(TM, n), _const((k, n))],
        out_specs=_row(TM, k),
        out_shape=_sds((t, k), BF16),
        compiler_params=_cp("parallel"),
    )(a, w)


def flash_bwd(q, k, v, o, do, lse, seq):
    t = q.shape[1]
    nb = t // seq
    nq = seq // TQ

    def body(q_ref, k_ref, v_ref, o_ref, do_ref, lse_ref, dq_ref, dk_ref, dv_ref):
        kj = pl.program_id(2)

        @pl.when(kj == 0)
        def _():
            dq_ref[...] = jnp.zeros_like(dq_ref)

        kv = k_ref[0]
        vv = v_ref[0]

        def step(i, carry, diagonal=False):
            dk, dv = carry
            rows = pl.ds(pl.multiple_of(i * TQ, TQ), TQ)
            qv = q_ref[0, rows, :]
            dov = do_ref[rows, :]
            delta = jnp.sum(dov.astype(F32) * o_ref[rows, :].astype(F32), axis=-1, keepdims=True)
            s = _dot_nt(qv, kv)
            if diagonal:
                s = jnp.where(_diagonal_mask(), s, -1e30)
            p = jnp.exp(s - lse_ref[0, rows, :])
            dp = _dot_nt(dov, vv)
            ds = (p * (dp - delta)).astype(BF16)
            dv = dv + _dot_tn(p.astype(BF16), dov)
            dk = dk + _dot_tn(ds, qv)
            dq_ref[0, rows, :] += _dot(ds, kv)
            return dk, dv

        init = (jnp.zeros((TQ, 2 * DN), F32), jnp.zeros((TQ, DN), F32))
        dk, dv = lax.fori_loop(kj + 1, nq, step, step(kj, init, diagonal=True))
        dk_ref[0] = dk
        dv_ref[0] = dv

    return pl.pallas_call(
        body, name="flash_bwd", grid=(nb, HEADS, nq),
        in_specs=[pl.BlockSpec((1, seq, 2 * DN), lambda b, h, j: (h, b, 0)),
                  pl.BlockSpec((1, TQ, 2 * DN), lambda b, h, j: (h, b * nq + j, 0)),
                  pl.BlockSpec((1, TQ, DN), lambda b, h, j: (h, b * nq + j, 0)),
                  pl.BlockSpec((seq, DN), lambda b, h, j: (b, h)),
                  pl.BlockSpec((seq, DN), lambda b, h, j: (b, h)),
                  pl.BlockSpec((1, seq, 1), lambda b, h, j: (h, b, 0))],
        out_specs=[pl.BlockSpec((1, seq, 2 * DN), lambda b, h, j: (h, b, 0)),
                   pl.BlockSpec((1, TQ, 2 * DN), lambda b, h, j: (h, b * nq + j, 0)),
                   pl.BlockSpec((1, TQ, DN), lambda b, h, j: (h, b * nq + j, 0))],
        out_shape=[_sds((HEADS, t, 2 * DN), F32), _sds((HEADS, t, 2 * DN), F32), _sds((HEADS, t, DN), F32)],
        compiler_params=_cp("parallel", "parallel", "arbitrary"),
    )(q, k, v, o, do, lse)


def mla_pre_bwd(dq, dk, dv, dh1, h, g1, wdn, gq, gkv, wuq, wukv, gqn, gqr, gkn, gkr, cos, sin):
    t = h.shape[0]

    def body(dq_ref, dk_ref, dv_ref, dh1_ref, h_ref, g1_ref, wdn_ref, gq_ref, gkv_ref, wuq_ref, wukv_ref,
             gqn_ref, gqr_ref, gkn_ref, gkr_ref, c_ref, s_ref,
             dh_ref, hn_ref, cq_ref, ckv_ref, dqp_ref, dkvp_ref, dlat_ref,
             dg1_ref, dgq_ref, dgkv_ref, dgqn_ref, dgqr_ref, dgkn_ref, dgkr_ref):
        _zero_at_first_step(dg1_ref, dgq_ref, dgkv_ref, dgqn_ref, dgqr_ref, dgkn_ref, dgkr_ref)
        m = _mla_project(h_ref, g1_ref, wdn_ref, gq_ref, gkv_ref, wuq_ref, wukv_ref)
        hn_ref[...] = m["hn"]
        cq_ref[...] = m["cqb"]
        ckv_ref[...] = m["ckvb"]
        c = c_ref[...]
        s = s_ref[...]
        gqn = gqn_ref[...]
        gqr = gqr_ref[...]
        gkn = gkn_ref[...]
        gkr = gkr_ref[...]

        dkr = dk_ref[0, :, DN:2 * DN]
        for hd in range(1, HEADS):
            dkr = dkr + dk_ref[hd, :, DN:2 * DN]
        dkr = _rope_t(dkr, c, s)
        _, krhat, rkr = _rms(m["kr_raw"], gkr, DR)
        _acc_rows(dgkr_ref, dkr * krhat)
        dkr_raw = _rms_bwd(dkr, gkr, krhat, rkr, DR)

        for hd in range(HEADS):
            ncols = slice(hd * DN, (hd + 1) * DN)
            _, xh, r = _rms(m["qp"][:, ncols], gqn, DN)
            dqn = dq_ref[hd, :, 0:DN] * SM_SCALE
            _acc_rows(dgqn_ref, dqn * xh)
            dqp_ref[:, ncols] = _rms_bwd(dqn, gqn, xh, r, DN).astype(BF16)

            rcols = slice(D + hd * LANES, D + (hd + 1) * LANES)
            _, xh, r = _rms(m["qp"][:, rcols], gqr, DR)
            dqr = _rope_t(dq_ref[hd, :, DN:2 * DN] * SM_SCALE, c, s)
            _acc_rows(dgqr_ref, dqr * xh)
            dqp_ref[:, rcols] = _rms_bwd(dqr, gqr, xh, r, DR).astype(BF16)

            kcols = slice(hd * 2 * DN, hd * 2 * DN + DN)
            _, xh, r = _rms(m["kvp"][:, kcols], gkn, DN)
            dkn = dk_ref[hd, :, 0:DN]
            _acc_rows(dgkn_ref, dkn * xh)
            dkvp_ref[:, kcols] = _rms_bwd(dkn, gkn, xh, r, DN).astype(BF16)
            dkvp_ref[:, hd * 2 * DN + DN:(hd + 1) * 2 * DN] = dv_ref[hd].astype(BF16)

        dcq = _dot_nt(dqp_ref[...], wuq_ref[...])
        _acc_rows(dgq_ref, dcq * m["cqhat"])
        dlat_q = _rms_bwd(dcq, gq_ref[...], m["cqhat"], m["rq"], QL)
        dckv = _dot_nt(dkvp_ref[...], wukv_ref[...])
        _acc_rows(dgkv_ref, dckv * m["ckvhat"])
        dlat_kv = _rms_bwd(dckv, gkv_ref[...], m["ckvhat"], m["rkv"], KVL)
        dlat = jnp.concatenate([dlat_q, dlat_kv, dkr_raw], axis=1).astype(BF16)
        dlat_ref[...] = dlat
        dhn = _dot_nt(dlat, wdn_ref[...])
        _acc_rows(dg1_ref, dhn * m["xhat"])
        dh_ref[...] = dh1_ref[...] + _rms_bwd(dhn, g1_ref[...], m["xhat"], m["rx"], D)

    hb = lambda w: pl.BlockSpec((HEADS, TM, w), lambda i: (0, i, 0))
    return pl.pallas_call(
        body, name="mla_pre_bwd", grid=(t // TM,),
        in_specs=[hb(2 * DN), hb(2 * DN), hb(DN), _row(TM, D), _row(TM, D), _const((1, D)), _const((D, LATP)),
                  _const((1, QL)), _const((1, KVL)), _const((QL, 2 * D)), _const((KVL, 2 * D)),
                  _const((1, LANES)), _const((1, LANES)), _const((1, LANES)), _const((1, LANES)),
                  _row(TM, LANES), _row(TM, LANES)],
        out_specs=[_row(TM, D), _row(TM, D), _row(TM, QL), _row(TM, KVL), _row(TM, 2 * D), _row(TM, 2 * D),
                   _row(TM, LATP), _const((8, D)), _const((8, QL)), _const((8, KVL)), _const((8, LANES)),
                   _const((8, LANES)), _const((8, LANES)), _const((8, LANES))],
        out_shape=[_sds((t, D), F32), _sds((t, D), BF16), _sds((t, QL), BF16), _sds((t, KVL), BF16),
                   _sds((t, 2 * D), BF16), _sds((t, 2 * D), BF16), _sds((t, LATP), BF16),
                   _sds((8, D), F32), _sds((8, QL), F32), _sds((8, KVL), F32), _sds((8, LANES), F32),
                   _sds((8, LANES), F32), _sds((8, LANES), F32), _sds((8, LANES), F32)],
        compiler_params=_cp("arbitrary"),
    )(dq, dk, dv, dh1, h, g1, wdn, gq, gkv, wuq, wukv, gqn, gqr, gkn, gkr, cos, sin)


def gmlp_bwd(dh1, dh1b, h, pre, g1, win, lng, lnb, wm, wmt, bfull, wout, tril):
    t = h.shape[0]

    def body(dh1_ref, dh1b_ref, h_ref, pre_ref, g1_ref, win_ref, lng_ref, lnb_ref, wm_ref, wmt_ref, b_ref,
             wout_ref, tril_ref, dh_ref, hn_ref, dpre_ref, dws_ref, dbs_ref, dlng_ref, dlnb_ref, dg1_ref,
             dvn_s):
        _zero_at_first_step(dws_ref, dbs_ref, dlng_ref, dlnb_ref, dg1_ref)
        g1 = g1_ref[...]
        yn, xhat, rx = _rms(h_ref[...], g1, D)
        hn_ref[...] = yn.astype(BF16)
        dy = _dot_nt(dh1b_ref[...], wout_ref[...])
        pre_u = pre_ref[:, :GH].astype(F32)
        pre_v = pre_ref[:, GH:].astype(F32)
        u = _gelu(pre_u)
        v = _gelu(pre_v)
        xc = v - jnp.mean(v, axis=-1, keepdims=True)
        rs = lax.rsqrt(jnp.mean(xc * xc, axis=-1, keepdims=True) + EPS)
        vhat = xc * rs
        lng = lng_ref[...]
        vnb = (vhat * lng + lnb_ref[...]).astype(BF16)
        dsv = dy * u
        dsvb = dsv.astype(BF16)
        tril_m = tril_ref[...]
        gg_u = _gelu_grad(pre_u)
        for ch in range(TM // GC):
            rows = slice(ch * GC, (ch + 1) * GC)
            dbs_ref[...] += dsv[rows, :]
            for g in range(GG):
                cols = slice(g * GD, (g + 1) * GD)
                sv = _dot(wm_ref[g], vnb[rows, cols]) + b_ref[:, cols]
                dpre_ref[rows, cols] = (dy[rows, cols] * sv * gg_u[rows, cols]).astype(BF16)
                dvn_s[rows, cols] = _dot(wmt_ref[g], dsvb[rows, cols])
                dws_ref[g] += _dot_nt(dsvb[rows, cols], vnb[rows, cols]) * tril_m
        dvn = dvn_s[...]
        _acc_rows(dlng_ref, dvn * vhat)
        _acc_rows(dlnb_ref, dvn)
        dvhat = dvn * lng
        dv = rs * (dvhat - jnp.mean(dvhat, axis=-1, keepdims=True)
                   - vhat * jnp.mean(dvhat * vhat, axis=-1, keepdims=True))
        dpre_v = (dv * _gelu_grad(pre_v)).astype(BF16)
        dpre_ref[:, GH:] = dpre_v
        dhn = _dot_nt(dpre_ref[:, :GH], win_ref[:, :GH]) + _dot_nt(dpre_v, win_ref[:, GH:])
        _acc_rows(dg1_ref, dhn * xhat)
        dh_ref[...] = dh1_ref[...] + _rms_bwd(dhn, g1, xhat, rx, D)

    return pl.pallas_call(
        body, name="gmlp_bwd", grid=(t // TM,),
        in_specs=[_row(TM, D), _row(TM, D), _row(TM, D), _row(TM, 2 * GH), _const((1, D)), _const((D, 2 * GH)),
                  _const((1, GH)), _const((1, GH)), _const((GG, GC, GC)), _const((GG, GC, GC)), _const((GC, GH)),
                  _const((GH, D)), _const((GC, GC))],
        out_specs=[_row(TM, D), _row(TM, D), _row(TM, 2 * GH), _const((GG, GC, GC)), _const((GC, GH)),
                   _const((8, GH)), _const((8, GH)), _const((8, D))],
        out_shape=[_sds((t, D), F32), _sds((t, D), BF16), _sds((t, 2 * GH), BF16), _sds((GG, GC, GC), F32),
                   _sds((GC, GH), F32), _sds((8, GH), F32), _sds((8, GH), F32), _sds((8, D), F32)],
        scratch_shapes=[pltpu.VMEM((TM, GH), F32)],
        compiler_params=_cp("arbitrary"),
    )(dh1, dh1b, h, pre, g1, win, lng, lnb, wm, wmt, bfull, wout, tril)


def mm_tn(a, b, col_shards=1):
    t, k = a.shape
    n = b.shape[1]
    tk = min(k, 1024)
    ns = n // col_shards
    tn = min(ns, 1024)
    tt = 1024 if t % 1024 == 0 else 512
    nbn = ns // tn

    def body(a_ref, b_ref, o_ref):
        @pl.when(pl.program_id(2) == 0)
        def _():
            o_ref[...] = jnp.zeros_like(o_ref)

        o_ref[...] += _dot_tn(a_ref[...].astype(BF16), b_ref[...].astype(BF16))

    if col_shards == 1:
        out_spec = pl.BlockSpec((tk, tn), lambda i, j, s: (i, j))
        out_shape = _sds((k, n), F32)
    else:
        out_spec = pl.BlockSpec((None, tk, tn), lambda i, j, s: (j // nbn, i, j % nbn))
        out_shape = _sds((col_shards, k, ns), F32)
    return pl.pallas_call(
        body, name="mm_tn", grid=(k // tk, n // tn, t // tt),
        in_specs=[pl.BlockSpec((tt, tk), lambda i, j, s: (s, i)), pl.BlockSpec((tt, tn), lambda i, j, s: (s, j))],
        out_specs=out_spec, out_shape=out_shape,
        compiler_params=_cp("parallel", "parallel", "arbitrary"),
    )(a, b)


def adamw(w, g, m, v):
    rows, cols = w.shape
    tr = rows if rows <= 512 else next(r for r in (512, 384, 256, 128) if rows % r == 0)
    c1 = 1.0 - ADAM_B1 ** ADAM_STEP
    c2 = 1.0 - ADAM_B2 ** ADAM_STEP

    def body(w_ref, g_ref, m_ref, v_ref, d_ref, mo_ref, vo_ref):
        gv = g_ref[...]
        mn = ADAM_B1 * m_ref[...] + (1.0 - ADAM_B1) * gv
        vn = ADAM_B2 * v_ref[...] + (1.0 - ADAM_B2) * (gv * gv)
        mo_ref[...] = mn
        vo_ref[...] = vn
        d_ref[...] = -ADAM_LR * ((mn / c1) / (jnp.sqrt(vn / c2) + ADAM_EPS) + ADAM_WD * w_ref[...])

    spec = pl.BlockSpec((tr, cols), lambda i: (i, 0))
    return pl.pallas_call(
        body, name="adamw", grid=(rows // tr,),
        in_specs=[spec] * 4, out_specs=[spec] * 3, out_shape=[_sds((rows, cols), F32)] * 3,
        compiler_params=_cp("parallel"),
    )(w, g, m, v)


def _place():
    return lax.axis_index("x"), lax.axis_index("y"), lax.axis_index("c")


def _other_chips(x, y):
    return [(1 - x, y), (x, 1 - y), (1 - x, 1 - y)]


_ANY = pl.BlockSpec(memory_space=pl.ANY)


def gather_shards(mine):
    rr, cc = mine.shape
    hh = rr // 2
    assert rr % 32 == 0, rr

    def body(x_ref, o_ref, send_sems, recv_sems):
        x, y, c = _place()
        k = 2 * x + y
        chips = _other_chips(x, y)
        mine_half = pl.ds(pl.multiple_of(c * hh, 16), hh)
        other_half = pl.ds(pl.multiple_of((1 - c) * hh, 16), hh)

        def copy(j, src, dst, to):
            return pltpu.make_async_remote_copy(src_ref=src, dst_ref=dst, send_sem=send_sems.at[j],
                                                recv_sem=recv_sems.at[j], device_id=to, device_id_type=MESH)

        sends = [copy(j, x_ref.at[mine_half], o_ref.at[k, mine_half], (cx, cy, c))
                 for j, (cx, cy) in enumerate(chips)]
        for cp in sends:
            cp.start()
        passed = []
        for j, (cx, cy) in enumerate(chips):
            kk = 2 * cx + cy
            copy(j, x_ref.at[mine_half], o_ref.at[kk, mine_half], (cx, cy, c)).wait_recv()
            cp = copy(3 + j, o_ref.at[kk, mine_half], o_ref.at[kk, mine_half], (x, y, 1 - c))
            cp.start()
            passed.append(cp)
        for j, (cx, cy) in enumerate(chips):
            kk = 2 * cx + cy
            copy(3 + j, o_ref.at[kk, other_half], o_ref.at[kk, other_half], (x, y, 1 - c)).wait_recv()
        for cp in sends + passed:
            cp.wait_send()

    return pl.pallas_call(
        body, name="gather_shards", in_specs=[_ANY], out_specs=_ANY,
        out_shape=_sds((N_CHIPS, rr, cc), mine.dtype),
        scratch_shapes=[pltpu.SemaphoreType.DMA((6,)), pltpu.SemaphoreType.DMA((6,))],
    )(mine)


def swap_halves(g):
    _, rr, cc = g.shape
    hh = rr // 2

    def body(g_ref, o_ref, send_sems, recv_sems):
        x, y, c = _place()
        other_half = pl.ds(pl.multiple_of((1 - c) * hh, 16), hh)
        copies = [pltpu.make_async_remote_copy(src_ref=g_ref.at[k, other_half], dst_ref=o_ref.at[k],
                                               send_sem=send_sems.at[k], recv_sem=recv_sems.at[k],
                                               device_id=(x, y, 1 - c), device_id_type=MESH)
                  for k in range(N_CHIPS)]
        for cp in copies:
            cp.start()
        for cp in copies:
            cp.wait()

    return pl.pallas_call(
        body, name="swap_halves", in_specs=[_ANY], out_specs=_ANY,
        out_shape=_sds((N_CHIPS, hh, cc), g.dtype),
        scratch_shapes=[pltpu.SemaphoreType.DMA((N_CHIPS,)), pltpu.SemaphoreType.DMA((N_CHIPS,))],
    )(g)


def chip_sum(place, g32, got):
    _, rr, cc = g32.shape
    hh = rr // 2
    tr = SUM_ROWS
    assert rr % 2 == 0 and hh % tr == 0, (rr, tr)
    nb = hh // tr

    def body(place_ref, g_ref, got_ref, own_ref, all_ref):
        s = g_ref[...] + got_ref[...].astype(F32)
        all_ref[...] = s.astype(BF16)
        own_ref[...] = g_ref[place_ref[1]] + got_ref[place_ref[1]].astype(F32)

    return pl.pallas_call(
        body, name="chip_sum",
        grid_spec=pltpu.PrefetchScalarGridSpec(
            num_scalar_prefetch=1, grid=(nb,),
            in_specs=[pl.BlockSpec((N_CHIPS, tr, cc), lambda i, pr: (0, pr[0] * nb + i, 0)),
                      pl.BlockSpec((N_CHIPS, tr, cc), lambda i, pr: (0, i, 0))],
            out_specs=[pl.BlockSpec((tr, cc), lambda i, pr: (i, 0)),
                       pl.BlockSpec((N_CHIPS, tr, cc), lambda i, pr: (0, i, 0))]),
        out_shape=[_sds((hh, cc), F32), _sds((N_CHIPS, hh, cc), BF16)],
        compiler_params=_cp("parallel"),
    )(place, g32, got)


def scatter_to_chips(s):
    _, hh, cc = s.shape

    def body(s_ref, o_ref, send_sems, recv_sems):
        x, y, c = _place()
        copies = [pltpu.make_async_remote_copy(src_ref=s_ref.at[2 * cx + cy], dst_ref=o_ref.at[j],
                                               send_sem=send_sems.at[j], recv_sem=recv_sems.at[j],
                                               device_id=(cx, cy, c), device_id_type=MESH)
                  for j, (cx, cy) in enumerate(_other_chips(x, y))]
        for cp in copies:
            cp.start()
        for cp in copies:
            cp.wait()

    return pl.pallas_call(
        body, name="scatter_to_chips", in_specs=[_ANY], out_specs=_ANY,
        out_shape=_sds((3, hh, cc), s.dtype),
        scratch_shapes=[pltpu.SemaphoreType.DMA((3,)), pltpu.SemaphoreType.DMA((3,))],
    )(s)


def final_sum(own, got):
    hh, cc = own.shape
    tr = SUM_ROWS
    assert hh % tr == 0, (hh, tr)

    def body(own_ref, got_ref, o_ref):
        o_ref[...] = ((own_ref[...] + got_ref[0].astype(F32)) + got_ref[1].astype(F32)) + got_ref[2].astype(F32)

    return pl.pallas_call(
        body, name="final_sum", grid=(hh // tr,),
        in_specs=[pl.BlockSpec((tr, cc), lambda i: (i, 0)), pl.BlockSpec((3, tr, cc), lambda i: (0, i, 0))],
        out_specs=pl.BlockSpec((tr, cc), lambda i: (i, 0)),
        out_shape=_sds((hh, cc), F32),
        compiler_params=_cp("parallel"),
    )(own, got)


def share_with_sibling(f):
    hh, cc = f.shape

    def body(f_ref, o_ref, send_sem, recv_sem):
        x, y, c = _place()
        mine_half = pl.ds(pl.multiple_of(c * hh, 8), hh)
        cp = pltpu.make_async_remote_copy(src_ref=f_ref, dst_ref=o_ref.at[mine_half], send_sem=send_sem,
                                          recv_sem=recv_sem, device_id=(x, y, 1 - c), device_id_type=MESH)
        cp.start()
        cp.wait()

    return pl.pallas_call(
        body, name="share_with_sibling", in_specs=[_ANY], out_specs=_ANY,
        out_shape=_sds((2 * hh, cc), f.dtype),
        scratch_shapes=[pltpu.SemaphoreType.DMA, pltpu.SemaphoreType.DMA],
    )(f)


def allreduce_small(part):
    rr, cc = part.shape
    n_dev = 8

    def body(x_ref, all_ref, sum_ref, send_sems, recv_sems, local_sem):
        x, y, c = _place()
        me, sibling = (x, y, c), (x, y, 1 - c)
        chips = _other_chips(x, y)

        def rows(px, py, pc):
            return all_ref.at[pl.ds(pl.multiple_of((4 * px + 2 * py + pc) * rr, 8), rr), :]

        def copy(j, block, to, src=None):
            return pltpu.make_async_remote_copy(src_ref=rows(*block) if src is None else src, dst_ref=rows(*block),
                                                send_sem=send_sems.at[j], recv_sem=recv_sems.at[j],
                                                device_id=to, device_id_type=MESH)

        mine = pltpu.make_async_copy(x_ref, rows(*me), local_sem)
        mine.start()
        first = [copy(0, me, sibling, src=x_ref)]
        first += [copy(1 + j, me, (*chip, c), src=x_ref) for j, chip in enumerate(chips)]
        for cp in first:
            cp.start()
        passed = [copy(4 + j, (*chip, c), sibling) for j, chip in enumerate(chips)]
        for j, chip in enumerate(chips):
            copy(1 + j, (*chip, c), me).wait_recv()
            passed[j].start()
        copy(0, sibling, me).wait_recv()
        for j, chip in enumerate(chips):
            copy(4 + j, (*chip, 1 - c), me).wait_recv()
        for cp in first + passed:
            cp.wait_send()
        mine.wait()
        acc = all_ref[0:rr, :]
        for d in range(1, n_dev):
            acc = acc + all_ref[d * rr:(d + 1) * rr, :]
        sum_ref[...] = acc

    vm = pl.BlockSpec(memory_space=pltpu.VMEM)
    return pl.pallas_call(
        body, name="allreduce_small", in_specs=[vm], out_specs=[vm, vm],
        out_shape=[_sds((n_dev * rr, cc), F32), _sds((rr, cc), F32)],
        scratch_shapes=[pltpu.SemaphoreType.DMA((7,)), pltpu.SemaphoreType.DMA((7,)), pltpu.SemaphoreType.DMA],
        compiler_params=pltpu.CompilerParams(vmem_limit_bytes=VMEM_LIMIT),
    )(part)[1]


_BIG = [
    ("mla_w_down", (256, LAT), 0), ("mla_w_uq", (QL, 384), 1), ("mla_w_ukv", (KVL, 512), 1),
    ("mla_w_out", (256, D), 0), ("gmlp_w_in", (D, D), 1), ("gmlp_w_out", (512, D), 0),
    ("ffn_w_up", (D, D), 1), ("ffn_w_down", (D, D), 0), ("ple_w_gate", (256, D), 0), ("ple_w_proj", (PLE, 256), 1),
]
_SMALL = ["norm_mix", "norm_ffn", "norm_ple", "mla_q_lora_g", "mla_kv_lora_g", "mla_q_nope_g", "mla_q_rope_g",
          "mla_k_nope_g", "mla_k_rope_g", "gmlp_ln_g", "gmlp_ln_b", "gmlp_w_s", "gmlp_b_s"]
_LN_ROWS = 128


def _rows_of(a):
    return a.size // D


def _pack_rows(parts, dtype, pad_to=None):
    flat = jnp.concatenate([p.reshape(-1).astype(dtype) for p in parts])
    if pad_to is not None:
        flat = jnp.pad(flat, (0, pad_to * D - flat.size))
    return flat.reshape(-1, D)


def _unshard(seg, shape, dim):
    a, b = shape
    s = seg.reshape(N_CHIPS, -1, a, b)
    if dim == 0:
        return jnp.transpose(s, (1, 0, 2, 3)).reshape(-1, N_CHIPS * a, b)
    return jnp.transpose(s, (1, 2, 0, 3)).reshape(-1, a, N_CHIPS * b)


def _to_shards(full, dim):
    l, aa, bb = full.shape
    if dim == 0:
        s = jnp.transpose(full.reshape(l, N_CHIPS, aa // N_CHIPS, bb), (1, 0, 2, 3))
    else:
        s = jnp.transpose(full.reshape(l, aa, N_CHIPS, bb // N_CHIPS), (2, 0, 1, 3))
    return s.reshape(N_CHIPS, -1, D)


def _pad_lanes(g):
    return jnp.pad(g, ((0, 0), (0, LANES - g.shape[1])))


def _split_uq(wuq):
    l = wuq.shape[0]
    w = wuq.reshape(l, QL, HEADS, DN + DR)
    nope = w[..., :DN].reshape(l, QL, HEADS * DN)
    rope = jnp.pad(w[..., DN:], ((0, 0), (0, 0), (0, 0), (0, LANES - DR))).reshape(l, QL, HEADS * LANES)
    return jnp.concatenate([nope, rope], axis=-1)


def _merge_uq(d):
    nope = d[:, :HEADS * DN].reshape(QL, HEADS, DN)
    rope = d[:, HEADS * DN:].reshape(QL, HEADS, LANES)[..., :DR]
    return jnp.concatenate([nope, rope], axis=-1).reshape(QL, HEADS * (DN + DR))


def _rope_tables(positions):
    inv_freq = ROPE_BASE ** (-(jnp.arange(0, DR, 2, dtype=F32) / DR))
    ang = positions.reshape(-1).astype(F32)[:, None] * inv_freq
    z = jnp.zeros((ang.shape[0], LANES - DR), F32)
    return (jnp.concatenate([jnp.cos(ang), jnp.cos(ang), z], axis=1),
            jnp.concatenate([jnp.sin(ang), jnp.sin(ang), z], axis=1))


def kernel(x, p, positions, norm_mix, norm_ffn, norm_ple, mla_w_down, mla_q_lora_g, mla_kv_lora_g, mla_w_uq, mla_w_ukv, mla_q_nope_g, mla_q_rope_g, mla_k_nope_g, mla_k_rope_g, mla_w_out, gmlp_w_in, gmlp_ln_g, gmlp_ln_b, gmlp_w_s, gmlp_b_s, gmlp_w_out, ffn_w_up, ffn_w_down, ple_w_gate, ple_w_proj, loss_target, m_norm_mix, m_norm_ffn, m_norm_ple, m_mla_w_down, m_mla_q_lora_g, m_mla_kv_lora_g, m_mla_w_uq, m_mla_w_ukv, m_mla_q_nope_g, m_mla_q_rope_g, m_mla_k_nope_g, m_mla_k_rope_g, m_mla_w_out, m_gmlp_w_in, m_gmlp_ln_g, m_gmlp_ln_b, m_gmlp_w_s, m_gmlp_b_s, m_gmlp_w_out, m_ffn_w_up, m_ffn_w_down, m_ple_w_gate, m_ple_w_proj, v_norm_mix, v_norm_ffn, v_norm_ple, v_mla_w_down, v_mla_q_lora_g, v_mla_kv_lora_g, v_mla_w_uq, v_mla_w_ukv, v_mla_q_nope_g, v_mla_q_rope_g, v_mla_k_nope_g, v_mla_k_rope_g, v_mla_w_out, v_gmlp_w_in, v_gmlp_ln_g, v_gmlp_ln_b, v_gmlp_w_s, v_gmlp_b_s, v_gmlp_w_out, v_ffn_w_up, v_ffn_w_down, v_ple_w_gate, v_ple_w_proj):
    args = dict(locals())
    weights = {n: args[n] for n, _, _ in _BIG}
    weights.update({n: args[n] for n in _SMALL})
    depth = norm_mix.shape[0]
    nb, seq, _ = x.shape
    t = nb * seq
    assert seq % TQ == 0 and seq % TM == 0 and t % 512 == 0, (nb, seq)
    cx = lax.axis_index("x")
    cy = lax.axis_index("y")
    cc = lax.axis_index("c")
    chip = 2 * cx + cy

    ln = jnp.stack([gmlp_ln_g, gmlp_ln_b]).astype(F32)
    ln_rows = _pack_rows([lax.bitcast_convert_type(ln, BF16)], BF16, pad_to=_LN_ROWS)
    mine = jnp.concatenate([_pack_rows([weights[n] for n, _, _ in _BIG], BF16), ln_rows])
    allw = lax.dynamic_update_slice(gather_shards(mine), mine[None], (chip, 0, 0))
    full = {}
    r0 = 0
    for n, shape, dim in _BIG:
        nrows = _rows_of(weights[n])
        full[n] = _unshard(allw[:, r0:r0 + nrows], shape, dim)
        r0 += nrows
    ln_all = lax.bitcast_convert_type(allw[:, r0:r0 + _LN_ROWS].reshape(N_CHIPS, -1)[:, :ln.size * 2]
                                      .reshape(N_CHIPS, 2, ln.shape[1], ln.shape[2], 2), F32)
    ln_all = jnp.transpose(ln_all, (1, 2, 0, 3)).reshape(2, ln.shape[1], GH)

    wdn = jnp.pad(full["mla_w_down"], ((0, 0), (0, 0), (0, LATP - LAT)))
    wuq = _split_uq(full["mla_w_uq"])
    tril = jnp.tril(jnp.ones((GC, GC), F32))
    wm = (gmlp_w_s * tril).astype(BF16)
    wmt = jnp.swapaxes(wm, -1, -2)
    bfull = jnp.repeat(jnp.swapaxes(gmlp_b_s, -1, -2), GD, axis=-1)
    cos, sin = _rope_tables(positions)
    row = lambda g: g.reshape(1, -1)
    gqr = _pad_lanes(mla_q_rope_g)
    gkr = _pad_lanes(mla_k_rope_g)

    h = x.reshape(t, D)
    pt = p.reshape(depth, t, PLE)
    saved = []
    for i in range(depth):
        j = i // 2
        s = dict(h=h)
        if i % 2 == 0:
            mla_args = (row(norm_mix[i]), wdn[j], row(mla_q_lora_g[j]), row(mla_kv_lora_g[j]), wuq[j],
                        full["mla_w_ukv"][j], row(mla_q_nope_g[j]), gqr[j:j + 1], row(mla_k_nope_g[j]), gkr[j:j + 1],
                        cos, sin)
            q, k, v = mla_pre_fwd(h, *mla_args)
            o, lse = flash_fwd(q, k, v, seq)
            s.update(q=q, k=k, v=v, o=o, lse=lse, mla_args=mla_args)
            y, wo = o, full["mla_w_out"][j]
        else:
            y, pre = gmlp_fwd(h, row(norm_mix[i]), full["gmlp_w_in"][j], row(ln_all[0, j]), row(ln_all[1, j]),
                              wm[j], bfull[j])
            s.update(pre=pre)
            wo = full["gmlp_w_out"][j]
        h1, h2, hn2, r = mixffn_fwd(h, y, wo, row(norm_ffn[i]), full["ffn_w_up"][i], full["ffn_w_down"][i])
        h, hn3 = ple_fwd(h2, pt[i], row(norm_ple[i]), full["ple_w_gate"][i], full["ple_w_proj"][i])
        s.update(y=y, wo=wo, h1=h1, h2=h2, hn2=hn2, r=r, hn3=hn3)
        saved.append(s)

    dh, loss_part = loss_head(h, loss_target.reshape(t, D))
    loss = lax.psum(loss_part[0, 0], ("x", "y", "c"))

    gw = {n: [None] * weights[n].shape[0] for n, _, _ in _BIG}
    gs = {n: [None] * weights[n].shape[0] for n in _SMALL}
    for i in reversed(range(depth)):
        j = i // 2
        s = saved[i]
        dh2, dh2b, dgt, dpp, dg3 = ple_bwd(dh, s["h2"], pt[i], row(norm_ple[i]), full["ple_w_gate"][i],
                                           full["ple_w_proj"][i])
        gs["norm_ple"][i] = dg3[0]
        gw["ple_w_gate"][i] = mm_tn(s["hn3"], dgt).reshape(N_CHIPS, -1, D)
        gw["ple_w_proj"][i] = mm_tn(pt[i], dpp, col_shards=N_CHIPS).reshape(N_CHIPS, -1, D)
        dh1, dh1b, du, a, dg2 = ffn_bwd(dh2, dh2b, s["h1"], s["r"], row(norm_ffn[i]), full["ffn_w_up"][i],
                                        full["ffn_w_down"][i])
        gs["norm_ffn"][i] = dg2[0]
        gw["ffn_w_down"][i] = mm_tn(a, dh2b).reshape(N_CHIPS, -1, D)
        gw["ffn_w_up"][i] = mm_tn(s["hn2"], du, col_shards=N_CHIPS).reshape(N_CHIPS, -1, D)
        dwo = mm_tn(s["y"], dh1b)
        if i % 2 == 0:
            gw["mla_w_out"][j] = dwo.reshape(N_CHIPS, -1, D)
            do = linear_nt(dh1b, s["wo"])
            dq, dk, dv = flash_bwd(s["q"], s["k"], s["v"], s["o"], do, s["lse"], seq)
            (dh, hn1, cq, ckv, dqp, dkvp, dlat, dg1, dgq, dgkv, dgqn, dgqr, dgkn, dgkr) = mla_pre_bwd(
                dq, dk, dv, dh1, s["h"], *s["mla_args"])
            gs["norm_mix"][i] = dg1[0]
            gs["mla_q_lora_g"][j] = dgq[0]
            gs["mla_kv_lora_g"][j] = dgkv[0]
            gs["mla_q_nope_g"][j] = dgqn[0]
            gs["mla_q_rope_g"][j] = dgqr[0, :DR]
            gs["mla_k_nope_g"][j] = dgkn[0]
            gs["mla_k_rope_g"][j] = dgkr[0, :DR]
            gw["mla_w_down"][j] = _to_shards(mm_tn(hn1, dlat)[None, :, :LAT], 0)[:, :, :]
            gw["mla_w_uq"][j] = _to_shards(_merge_uq(mm_tn(cq, dqp))[None], 1)
            gw["mla_w_ukv"][j] = mm_tn(ckv, dkvp, col_shards=N_CHIPS).reshape(N_CHIPS, -1, D)
        else:
            gw["gmlp_w_out"][j] = dwo.reshape(N_CHIPS, -1, D)
            dh, hn1, dpre, dws, dbs, dlng, dlnb, dg1 = gmlp_bwd(
                dh1, dh1b, s["h"], s["pre"], row(norm_mix[i]), full["gmlp_w_in"][j], row(ln_all[0, j]),
                row(ln_all[1, j]), wm[j], wmt[j], bfull[j], s["wo"], tril)
            gs["norm_mix"][i] = dg1[0]
            gs["gmlp_ln_g"][j] = dlng[0]
            gs["gmlp_ln_b"][j] = dlnb[0]
            gs["gmlp_w_s"][j] = dws
            gs["gmlp_b_s"][j] = jnp.sum(dbs.reshape(GC, GG, GD), axis=-1).T
            gw["gmlp_w_in"][j] = mm_tn(hn1, dpre, col_shards=N_CHIPS).reshape(N_CHIPS, -1, D)
    grad_x = dh.reshape(x.shape)

    g32 = jnp.concatenate(
        [jnp.concatenate(gw[n], axis=1) for n, _, _ in _BIG] + [jnp.zeros((N_CHIPS, _LN_ROWS, D), F32)], axis=1)
    place = jnp.stack([cc, chip]).astype(jnp.int32)
    got = swap_halves(g32.astype(BF16))
    own, sums = chip_sum(place, g32, got)
    half = final_sum(own, scatter_to_chips(sums))
    reduced = lax.dynamic_update_slice(share_with_sibling(half), half, (cc * half.shape[0], 0))
    grads = {}
    r0 = 0
    for n, shape, dim in _BIG:
        nrows = _rows_of(weights[n])
        grads[n] = reduced[r0:r0 + nrows].reshape(weights[n].shape)
        r0 += nrows

    small_sizes = [weights[n].size if n not in ("gmlp_ln_g", "gmlp_ln_b") else weights[n].shape[0] * GH
                   for n in _SMALL]
    small_rows = -(-sum(small_sizes) // (8 * D)) * 8
    part = _pack_rows([jnp.stack(gs[n]) for n in _SMALL], F32, pad_to=small_rows)
    tot = allreduce_small(part).reshape(-1)
    off = 0
    for n, sz in zip(_SMALL, small_sizes):
        gsum = tot[off:off + sz]
        off += sz
        if n in ("gmlp_ln_g", "gmlp_ln_b"):
            gsum = lax.dynamic_slice_in_dim(gsum.reshape(-1, GH), chip * (GH // N_CHIPS), GH // N_CHIPS, axis=1)
        grads[n] = gsum.reshape(weights[n].shape)

    delta, new_m, new_v = {}, {}, {}
    for n, _, _ in _BIG:
        w2 = weights[n].reshape(-1, weights[n].shape[-1])
        d, mn, vn = adamw(w2, grads[n].reshape(w2.shape), args["m_" + n].reshape(w2.shape),
                          args["v_" + n].reshape(w2.shape))
        delta[n], new_m[n], new_v[n] = (a.reshape(weights[n].shape) for a in (d, mn, vn))
    own_sizes = [weights[n].size for n in _SMALL]
    own_rows = -(-sum(own_sizes) // (8 * D)) * 8
    packed = [_pack_rows([src[n] for n in _SMALL], F32, pad_to=own_rows)
              for src in (weights, grads, {n: args["m_" + n] for n in _SMALL}, {n: args["v_" + n] for n in _SMALL})]
    outs = adamw(*packed)
    off = 0
    for n, sz in zip(_SMALL, own_sizes):
        for dst, o in zip((delta, new_m, new_v), outs):
            dst[n] = o.reshape(-1)[off:off + sz].reshape(weights[n].shape)
        off += sz

    order = ["norm_mix", "norm_ffn", "norm_ple", "mla_w_down", "mla_q_lora_g", "mla_kv_lora_g", "mla_w_uq",
             "mla_w_ukv", "mla_q_nope_g", "mla_q_rope_g", "mla_k_nope_g", "mla_k_rope_g", "mla_w_out", "gmlp_w_in",
             "gmlp_ln_g", "gmlp_ln_b", "gmlp_w_s", "gmlp_b_s", "gmlp_w_out", "ffn_w_up", "ffn_w_down", "ple_w_gate",
             "ple_w_proj"]
    return (loss, grad_x, *[grads[n] for n in order], *[delta[n] for n in order], *[new_m[n] for n in order],
            *[new_v[n] for n in order])
```

```python
import functools

import jax
import jax.numpy as jnp
from jax import lax
from jax.experimental import pallas as pl
from jax.experimental.pallas import tpu as pltpu

F32 = jnp.float32
BF16 = jnp.bfloat16
MESH = pl.DeviceIdType.MESH

D = 1024
HEADS = 8
DN = 128
DR = 64
QL = 384
KVL = 256
LAT = 704
LATP = 768
DFF = 4096
GH = 2048
GC = 128
GG = 8
GD = 256
PLE = 256
EPS = 1e-6
ROPE_BASE = 10000.0
SM_SCALE = (DN + DR) ** -0.5
N_CHIPS = 4
LANES = 128

ADAM_LR = 0.001
ADAM_B1 = 0.9
ADAM_B2 = 0.999
ADAM_EPS = 1e-08
ADAM_WD = 0.01
ADAM_STEP = 10

TM = 256
TQ = 256
SUM_ROWS = 256
VMEM_LIMIT = 56 * 1024 * 1024


def _cp(*sem):
    return pltpu.CompilerParams(dimension_semantics=sem, vmem_limit_bytes=VMEM_LIMIT)


def _dot(a, b):
    return jnp.dot(a, b, preferred_element_type=F32)


def _dot_nt(a, b):
    return lax.dot_general(a, b, (((1,), (1,)), ((), ())), preferred_element_type=F32)


def _dot_tn(a, b):
    return lax.dot_general(a, b, (((0,), (0,)), ((), ())), preferred_element_type=F32)


def _rms(x, g, n):
    r = lax.rsqrt(jnp.sum(x * x, axis=-1, keepdims=True) * (1.0 / n) + EPS)
    xhat = x * r
    return xhat * g, xhat, r


def _rms_bwd(dy, g, xhat, r, n):
    dxhat = dy * g
    return r * (dxhat - xhat * (jnp.sum(dxhat * xhat, axis=-1, keepdims=True) * (1.0 / n)))


def _rope(x, c, s):
    return x * c + (pltpu.roll(x, 32, 1) - pltpu.roll(x, 96, 1)) * s


def _rope_t(dy, c, s):
    w = dy * s
    return dy * c + pltpu.roll(w, 96, 1) - pltpu.roll(w, 32, 1)


def _sigmoid(x):
    return 1.0 / (1.0 + jnp.exp(-x))


_GELU_K = 0.7978845608028654
_GELU_C = 0.044715


def _gelu(x):
    return 0.5 * x * (1.0 + jnp.tanh(_GELU_K * (x + _GELU_C * x * x * x)))


def _gelu_grad(x):
    t = jnp.tanh(_GELU_K * (x + _GELU_C * x * x * x))
    return 0.5 * (1.0 + t) + 0.5 * x * (1.0 - t * t) * (_GELU_K * (1.0 + 3.0 * _GELU_C * x * x))


def _acc_rows(ref, val):
    ref[...] += jnp.broadcast_to(jnp.sum(val, axis=0, keepdims=True), ref.shape)


def _row(tm, c):
    return pl.BlockSpec((tm, c), lambda i: (i, 0))


def _const(shape):
    nd = len(shape)
    return pl.BlockSpec(shape, lambda i: (0,) * nd, pipeline_mode=pl.Buffered(1))


def _wblk(rows, row0):
    assert row0 % rows == 0, (rows, row0)
    return pl.BlockSpec((N_CHIPS, rows, D), lambda i: (0, row0 // rows, 0), pipeline_mode=pl.Buffered(1))


def _rows_joined(w_ref):
    return w_ref[...].reshape(N_CHIPS * w_ref.shape[1], D)


def _sds(shape, dtype):
    return jax.ShapeDtypeStruct(shape, dtype)


def mixffn_fwd(h, y, allw, lay, g2):
    t, k = y.shape

    def body(h_ref, y_ref, wo_ref, g_ref, wu_ref, wd_ref, h1_ref, h2_ref, hn_ref, r_ref):
        h1 = h_ref[...] + _dot(y_ref[...], _rows_joined(wo_ref))
        h1_ref[...] = h1
        yn, _, _ = _rms(h1, g_ref[...], D)
        hn = yn.astype(BF16)
        hn_ref[...] = hn
        f = jnp.zeros((TM, D), F32)
        for c in range(N_CHIPS):
            r = jnp.maximum(_dot(hn, wu_ref[c]), 0.0)
            r_ref[:, c * D:(c + 1) * D] = r.astype(BF16)
            f = f + _dot((r * r).astype(BF16), wd_ref[c])
        h2_ref[...] = h1 + f

    return pl.pallas_call(
        body, name="mixffn_fwd", grid=(t // TM,),
        in_specs=[_row(TM, D), _row(TM, k), _wblk(k // N_CHIPS, lay["out"]), _const((1, D)), _wblk(D, lay["up"]),
                  _wblk(D, lay["down"])],
        out_specs=[_row(TM, D), _row(TM, D), _row(TM, D), _row(TM, DFF)],
        out_shape=[_sds((t, D), F32), _sds((t, D), F32), _sds((t, D), BF16), _sds((t, DFF), BF16)],
        compiler_params=_cp("parallel"),
    )(h, y, allw, g2, allw, allw)


def ple_fwd(h2, p, g3, allw, lay, wp):
    t = h2.shape[0]

    def body(h_ref, p_ref, g_ref, wg_ref, wp_ref, h3_ref, hn_ref):
        x = h_ref[...]
        yn, _, _ = _rms(x, g_ref[...], D)
        hn = yn.astype(BF16)
        hn_ref[...] = hn
        gt = _dot(hn, _rows_joined(wg_ref))
        pp = _dot(p_ref[...].astype(BF16), wp_ref[...])
        h3_ref[...] = x + _sigmoid(gt) * pp

    return pl.pallas_call(
        body, name="ple_fwd", grid=(t // TM,),
        in_specs=[_row(TM, D), _row(TM, PLE), _const((1, D)), _wblk(D // N_CHIPS, lay["gate"]), _const((PLE, D))],
        out_specs=[_row(TM, D), _row(TM, D)],
        out_shape=[_sds((t, D), F32), _sds((t, D), BF16)],
        compiler_params=_cp("parallel"),
    )(h2, p, g3, allw, wp)


def _mla_project(h_ref, g1_ref, wdn_ref, gq_ref, gkv_ref, wuq_ref, wukv_ref):
    x = h_ref[...]
    yn, xhat, rx = _rms(x, g1_ref[...], D)
    hn = yn.astype(BF16)
    lat = _dot(hn, wdn_ref[...])
    cq, cqhat, rq = _rms(lat[:, :QL], gq_ref[...], QL)
    ckv, ckvhat, rkv = _rms(lat[:, QL:QL + KVL], gkv_ref[...], KVL)
    kr_raw = lat[:, QL + KVL:]
    cqb = cq.astype(BF16)
    ckvb = ckv.astype(BF16)
    qp = _dot(cqb, wuq_ref[...])
    kvp = _dot(ckvb, wukv_ref[...])
    return dict(xhat=xhat, rx=rx, hn=hn, cqhat=cqhat, rq=rq, ckvhat=ckvhat, rkv=rkv, kr_raw=kr_raw,
                cqb=cqb, ckvb=ckvb, qp=qp, kvp=kvp)


def mla_pre_fwd(h, g1, wdn, gq, gkv, wuq, wukv, gqn, gqr, gkn, gkr, cos, sin):
    t = h.shape[0]

    def body(h_ref, g1_ref, wdn_ref, gq_ref, gkv_ref, wuq_ref, wukv_ref, gqn_ref, gqr_ref, gkn_ref, gkr_ref,
             c_ref, s_ref, q_ref, k_ref, v_ref):
        m = _mla_project(h_ref, g1_ref, wdn_ref, gq_ref, gkv_ref, wuq_ref, wukv_ref)
        c = c_ref[...]
        s = s_ref[...]
        kr, _, _ = _rms(m["kr_raw"], gkr_ref[...], DR)
        krb = _rope(kr, c, s).astype(BF16)
        for hd in range(HEADS):
            qn, _, _ = _rms(m["qp"][:, hd * DN:(hd + 1) * DN], gqn_ref[...], DN)
            qr, _, _ = _rms(m["qp"][:, D + hd * LANES:D + (hd + 1) * LANES], gqr_ref[...], DR)
            q_ref[hd, :, 0:DN] = (qn * SM_SCALE).astype(BF16)
            q_ref[hd, :, DN:2 * DN] = (_rope(qr, c, s) * SM_SCALE).astype(BF16)
            kn, _, _ = _rms(m["kvp"][:, hd * 2 * DN:hd * 2 * DN + DN], gkn_ref[...], DN)
            k_ref[hd, :, 0:DN] = kn.astype(BF16)
            k_ref[hd, :, DN:2 * DN] = krb
            v_ref[hd] = m["kvp"][:, hd * 2 * DN + DN:(hd + 1) * 2 * DN].astype(BF16)

    hb = lambda w: pl.BlockSpec((HEADS, TM, w), lambda i: (0, i, 0))
    return pl.pallas_call(
        body, name="mla_pre_fwd", grid=(t // TM,),
        in_specs=[_row(TM, D), _const((1, D)), _const((D, LATP)), _const((1, QL)), _const((1, KVL)),
                  _const((QL, 2 * D)), _const((KVL, 2 * D)), _const((1, LANES)), _const((1, LANES)),
                  _const((1, LANES)), _const((1, LANES)), _row(TM, LANES), _row(TM, LANES)],
        out_specs=[hb(2 * DN), hb(2 * DN), hb(DN)],
        out_shape=[_sds((HEADS, t, 2 * DN), BF16), _sds((HEADS, t, 2 * DN), BF16), _sds((HEADS, t, DN), BF16)],
        compiler_params=_cp("parallel"),
    )(h, g1, wdn, gq, gkv, wuq, wukv, gqn, gqr, gkn, gkr, cos, sin)


def _diagonal_mask():
    return lax.broadcasted_iota(jnp.int32, (TQ, TQ), 1) <= lax.broadcasted_iota(jnp.int32, (TQ, TQ), 0)


def flash_fwd(q, k, v, seq):
    t = q.shape[1]
    nb = t // seq
    nq = seq // TQ

    def body(q_ref, k_ref, v_ref, o_ref, lse_ref):
        qi = pl.program_id(2)
        qv = q_ref[0]

        def step(j, carry, diagonal=False):
            m, l, acc = carry
            rows = pl.ds(pl.multiple_of(j * TQ, TQ), TQ)
            s = _dot_nt(qv, k_ref[0, rows, :])
            if diagonal:
                s = jnp.where(_diagonal_mask(), s, -1e30)
            m_new = jnp.maximum(m, jnp.max(s, axis=-1, keepdims=True))
            p = jnp.exp(s - m_new)
            alpha = jnp.exp(m - m_new)
            l = alpha * l + jnp.sum(p, axis=-1, keepdims=True)
            acc = alpha * acc + _dot(p.astype(BF16), v_ref[0, rows, :])
            return m_new, l, acc

        init = (jnp.full((TQ, 1), -1e30, F32), jnp.zeros((TQ, 1), F32), jnp.zeros((TQ, DN), F32))
        m, l, acc = step(qi, lax.fori_loop(0, qi, step, init), diagonal=True)
        o_ref[...] = (acc / l).astype(BF16)
        lse_ref[0] = m + jnp.log(l)

    return pl.pallas_call(
        body, name="flash_fwd", grid=(nb, HEADS, nq),
        in_specs=[pl.BlockSpec((1, TQ, 2 * DN), lambda b, h, i: (h, b * nq + i, 0)),
                  pl.BlockSpec((1, seq, 2 * DN), lambda b, h, i: (h, b, 0)),
                  pl.BlockSpec((1, seq, DN), lambda b, h, i: (h, b, 0))],
        out_specs=[pl.BlockSpec((TQ, DN), lambda b, h, i: (b * nq + i, h)),
                   pl.BlockSpec((1, TQ, 1), lambda b, h, i: (h, b * nq + i, 0))],
        out_shape=[_sds((t, HEADS * DN), BF16), _sds((HEADS, t, 1), F32)],
        compiler_params=_cp("parallel", "parallel", "arbitrary"),
    )(q, k, v)


def _gmlp_in(hn, win_ref):
    pre = [_dot(hn, win_ref[c]) for c in range(N_CHIPS)]
    return jnp.concatenate(pre[:2], axis=1), jnp.concatenate(pre[2:], axis=1)


def gmlp_fwd(h, g1, allw, lay, lng, lnb, wm, bfull):
    t = h.shape[0]

    def body(h_ref, g1_ref, win_ref, lng_ref, lnb_ref, wm_ref, b_ref, y_ref, pre_ref):
        yn, _, _ = _rms(h_ref[...], g1_ref[...], D)
        pre_u, pre_v = _gmlp_in(yn.astype(BF16), win_ref)
        pre_ref[:, :GH] = pre_u.astype(BF16)
        pre_ref[:, GH:] = pre_v.astype(BF16)
        u = _gelu(pre_u)
        v = _gelu(pre_v)
        xc = v - jnp.mean(v, axis=-1, keepdims=True)
        rs = lax.rsqrt(jnp.mean(xc * xc, axis=-1, keepdims=True) + EPS)
        vnb = (xc * rs * lng_ref[...] + lnb_ref[...]).astype(BF16)
        for ch in range(TM // GC):
            rows = slice(ch * GC, (ch + 1) * GC)
            for g in range(GG):
                cols = slice(g * GD, (g + 1) * GD)
                sv = _dot(wm_ref[g], vnb[rows, cols]) + b_ref[:, cols]
                y_ref[rows, cols] = (u[rows, cols] * sv).astype(BF16)

    return pl.pallas_call(
        body, name="gmlp_fwd", grid=(t // TM,),
        in_specs=[_row(TM, D), _const((1, D)), _wblk(D, lay["in"]), _const((1, GH)), _const((1, GH)),
                  _const((GG, GC, GC)), _const((GC, GH))],
        out_specs=[_row(TM, GH), _row(TM, 2 * GH)],
        out_shape=[_sds((t, GH), BF16), _sds((t, 2 * GH), BF16)],
        compiler_params=_cp("parallel"),
    )(h, g1, allw, lng, lnb, wm, bfull)


def loss_head(h, tgt):
    t = h.shape[0]

    def body(h_ref, t_ref, dh_ref, loss_ref):
        @pl.when(pl.program_id(0) == 0)
        def _():
            loss_ref[...] = jnp.zeros_like(loss_ref)

        e = h_ref[...] - t_ref[...]
        dh_ref[...] = e * (1.0 / D)
        part = jnp.sum(jnp.sum(e * e, axis=-1, keepdims=True), axis=0, keepdims=True) * (0.5 / D)
        loss_ref[...] += jnp.broadcast_to(part, loss_ref.shape)

    return pl.pallas_call(
        body, name="loss_head", grid=(t // TM,),
        in_specs=[_row(TM, D), _row(TM, D)],
        out_specs=[_row(TM, D), _const((8, LANES))],
        out_shape=[_sds((t, D), F32), _sds((8, LANES), F32)],
        compiler_params=_cp("arbitrary"),
    )(h, tgt)


def _zero_at_first_step(*refs):
    @pl.when(pl.program_id(0) == 0)
    def _():
        for r in refs:
            r[...] = jnp.zeros_like(r)


def ple_bwd(dh3, h2, p, g3, allw, lay, wp):
    t = h2.shape[0]

    def body(dh_ref, h_ref, p_ref, g_ref, wg_ref, wp_ref, dh2_ref, dh2b_ref, dgt_ref, dpp_ref, dg_ref):
        _zero_at_first_step(dg_ref)
        dh3v = dh_ref[...]
        x = h_ref[...]
        g = g_ref[...]
        wg = _rows_joined(wg_ref)
        yn, xhat, r = _rms(x, g, D)
        gt = _dot(yn.astype(BF16), wg)
        pp = _dot(p_ref[...].astype(BF16), wp_ref[...])
        sg = _sigmoid(gt)
        dgt = (dh3v * pp * sg * (1.0 - sg)).astype(BF16)
        dgt_ref[...] = dgt
        dpp_ref[...] = (dh3v * sg).astype(BF16)
        dhn = _dot_nt(dgt, wg)
        _acc_rows(dg_ref, dhn * xhat)
        dh2 = dh3v + _rms_bwd(dhn, g, xhat, r, D)
        dh2_ref[...] = dh2
        dh2b_ref[...] = dh2.astype(BF16)

    return pl.pallas_call(
        body, name="ple_bwd", grid=(t // TM,),
        in_specs=[_row(TM, D), _row(TM, D), _row(TM, PLE), _const((1, D)), _wblk(D // N_CHIPS, lay["gate"]),
                  _const((PLE, D))],
        out_specs=[_row(TM, D), _row(TM, D), _row(TM, D), _row(TM, D), _const((8, D))],
        out_shape=[_sds((t, D), F32), _sds((t, D), BF16), _sds((t, D), BF16), _sds((t, D), BF16), _sds((8, D), F32)],
        compiler_params=_cp("arbitrary"),
    )(dh3, h2, p, g3, allw, wp)


def ffn_bwd(dh2, dh2b, h1, r, g2, allw, lay):
    t = h1.shape[0]

    def body(dh_ref, dhb_ref, h_ref, r_ref, g_ref, wu_ref, wd_ref, dh1_ref, dh1b_ref, du_ref, a_ref, dg_ref):
        _zero_at_first_step(dg_ref)
        dhb = dhb_ref[...]
        g = g_ref[...]
        _, xhat, rr = _rms(h_ref[...], g, D)
        dhn = jnp.zeros((TM, D), F32)
        for c in range(N_CHIPS):
            cs = slice(c * D, (c + 1) * D)
            rc = r_ref[:, cs].astype(F32)
            a_ref[:, cs] = (rc * rc).astype(BF16)
            da = _dot_nt(dhb, wd_ref[c])
            du = (da * (2.0 * rc)).astype(BF16)
            du_ref[:, cs] = du
            dhn = dhn + _dot_nt(du, wu_ref[c])
        _acc_rows(dg_ref, dhn * xhat)
        dh1 = dh_ref[...] + _rms_bwd(dhn, g, xhat, rr, D)
        dh1_ref[...] = dh1
        dh1b_ref[...] = dh1.astype(BF16)

    return pl.pallas_call(
        body, name="ffn_bwd", grid=(t // TM,),
        in_specs=[_row(TM, D), _row(TM, D), _row(TM, D), _row(TM, DFF), _const((1, D)), _wblk(D, lay["up"]),
                  _wblk(D, lay["down"])],
        out_specs=[_row(TM, D), _row(TM, D), _row(TM, DFF), _row(TM, DFF), _const((8, D))],
        out_shape=[_sds((t, D), F32), _sds((t, D), BF16), _sds((t, DFF), BF16), _sds((t, DFF), BF16),
                   _sds((8, D), F32)],
        compiler_params=_cp("arbitrary"),
    )(dh2, dh2b, h1, r, g2, allw, allw)


def linear_nt(a, allw, rows, row0):
    t = a.shape[0]
    k = N_CHIPS * rows

    def body(a_ref, w_ref, o_ref):
        o_ref[...] = _dot_nt(a_ref[...], _rows_joined(w_ref)).astype(BF16)

    return pl.pallas_call(
        body, name="linear_nt", grid=(t // TM,),
        in_specs=[_row(TM, D), _wblk(rows, row0)],
        out_specs=_row(TM, k),
        out_shape=_sds((t, k), BF16),
        compiler_params=_cp("parallel"),
    )(a, allw)


def flash_bwd(q, k, v, o, do, lse, seq):
    t = q.shape[1]
    nb = t // seq
    nq = seq // TQ

    def body(q_ref, k_ref, v_ref, o_ref, do_ref, lse_ref, dq_ref, dk_ref, dv_ref):
        kj = pl.program_id(2)

        @pl.when(kj == 0)
        def _():
            dq_ref[...] = jnp.zeros_like(dq_ref)

        kv = k_ref[0]
        vv = v_ref[0]

        def step(i, carry, diagonal=False):
            dk, dv = carry
            rows = pl.ds(pl.multiple_of(i * TQ, TQ), TQ)
            qv = q_ref[0, rows, :]
            dov = do_ref[rows, :]
            delta = jnp.sum(dov.astype(F32) * o_ref[rows, :].astype(F32), axis=-1, keepdims=True)
            s = _dot_nt(qv, kv)
            if diagonal:
                s = jnp.where(_diagonal_mask(), s, -1e30)
            p = jnp.exp(s - lse_ref[0, rows, :])
            dp = _dot_nt(dov, vv)
            ds = (p * (dp - delta)).astype(BF16)
            dv = dv + _dot_tn(p.astype(BF16), dov)
            dk = dk + _dot_tn(ds, qv)
            dq_ref[0, rows, :] += _dot(ds, kv)
            return dk, dv

        init = (jnp.zeros((TQ, 2 * DN), F32), jnp.zeros((TQ, DN), F32))
        dk, dv = lax.fori_loop(kj + 1, nq, step, step(kj, init, diagonal=True))
        dk_ref[0] = dk
        dv_ref[0] = dv

    return pl.pallas_call(
        body, name="flash_bwd", grid=(nb, HEADS, nq),
        in_specs=[pl.BlockSpec((1, seq, 2 * DN), lambda b, h, j: (h, b, 0)),
                  pl.BlockSpec((1, TQ, 2 * DN), lambda b, h, j: (h, b * nq + j, 0)),
                  pl.BlockSpec((1, TQ, DN), lambda b, h, j: (h, b * nq + j, 0)),
                  pl.BlockSpec((seq, DN), lambda b, h, j: (b, h)),
                  pl.BlockSpec((seq, DN), lambda b, h, j: (b, h)),
                  pl.BlockSpec((1, seq, 1), lambda b, h, j: (h, b, 0))],
        out_specs=[pl.BlockSpec((1, seq, 2 * DN), lambda b, h, j: (h, b, 0)),
                   pl.BlockSpec((1, TQ, 2 * DN), lambda b, h, j: (h, b * nq + j, 0)),
                   pl.BlockSpec((1, TQ, DN), lambda b, h, j: (h, b * nq + j, 0))],
        out_shape=[_sds((HEADS, t, 2 * DN), F32), _sds((HEADS, t, 2 * DN), F32), _sds((HEADS, t, DN), F32)],
        compiler_params=_cp("parallel", "parallel", "arbitrary"),
    )(q, k, v, o, do, lse)


def mla_pre_bwd(dq, dk, dv, dh1, h, g1, wdn, gq, gkv, wuq, wukv, gqn, gqr, gkn, gkr, cos, sin):
    t = h.shape[0]

    def body(dq_ref, dk_ref, dv_ref, dh1_ref, h_ref, g1_ref, wdn_ref, gq_ref, gkv_ref, wuq_ref, wukv_ref,
             gqn_ref, gqr_ref, gkn_ref, gkr_ref, c_ref, s_ref,
             dh_ref, hn_ref, cq_ref, ckv_ref, dqp_ref, dkvp_ref, dlat_ref,
             dg1_ref, dgq_ref, dgkv_ref, dgqn_ref, dgqr_ref, dgkn_ref, dgkr_ref):
        _zero_at_first_step(dg1_ref, dgq_ref, dgkv_ref, dgqn_ref, dgqr_ref, dgkn_ref, dgkr_ref)
        m = _mla_project(h_ref, g1_ref, wdn_ref, gq_ref, gkv_ref, wuq_ref, wukv_ref)
        hn_ref[...] = m["hn"]
        cq_ref[...] = m["cqb"]
        ckv_ref[...] = m["ckvb"]
        c = c_ref[...]
        s = s_ref[...]
        gqn = gqn_ref[...]
        gqr = gqr_ref[...]
        gkn = gkn_ref[...]
        gkr = gkr_ref[...]

        dkr = dk_ref[0, :, DN:2 * DN]
        for hd in range(1, HEADS):
            dkr = dkr + dk_ref[hd, :, DN:2 * DN]
        dkr = _rope_t(dkr, c, s)
        _, krhat, rkr = _rms(m["kr_raw"], gkr, DR)
        _acc_rows(dgkr_ref, dkr * krhat)
        dkr_raw = _rms_bwd(dkr, gkr, krhat, rkr, DR)

        for hd in range(HEADS):
            ncols = slice(hd * DN, (hd + 1) * DN)
            _, xh, r = _rms(m["qp"][:, ncols], gqn, DN)
            dqn = dq_ref[hd, :, 0:DN] * SM_SCALE
            _acc_rows(dgqn_ref, dqn * xh)
            dqp_ref[:, ncols] = _rms_bwd(dqn, gqn, xh, r, DN).astype(BF16)

            rcols = slice(D + hd * LANES, D + (hd + 1) * LANES)
            _, xh, r = _rms(m["qp"][:, rcols], gqr, DR)
            dqr = _rope_t(dq_ref[hd, :, DN:2 * DN] * SM_SCALE, c, s)
            _acc_rows(dgqr_ref, dqr * xh)
            dqp_ref[:, rcols] = _rms_bwd(dqr, gqr, xh, r, DR).astype(BF16)

            kcols = slice(hd * 2 * DN, hd * 2 * DN + DN)
            _, xh, r = _rms(m["kvp"][:, kcols], gkn, DN)
            dkn = dk_ref[hd, :, 0:DN]
            _acc_rows(dgkn_ref, dkn * xh)
            dkvp_ref[:, kcols] = _rms_bwd(dkn, gkn, xh, r, DN).astype(BF16)
            dkvp_ref[:, hd * 2 * DN + DN:(hd + 1) * 2 * DN] = dv_ref[hd].astype(BF16)

        dcq = _dot_nt(dqp_ref[...], wuq_ref[...])
        _acc_rows(dgq_ref, dcq * m["cqhat"])
        dlat_q = _rms_bwd(dcq, gq_ref[...], m["cqhat"], m["rq"], QL)
        dckv = _dot_nt(dkvp_ref[...], wukv_ref[...])
        _acc_rows(dgkv_ref, dckv * m["ckvhat"])
        dlat_kv = _rms_bwd(dckv, gkv_ref[...], m["ckvhat"], m["rkv"], KVL)
        dlat = jnp.concatenate([dlat_q, dlat_kv, dkr_raw], axis=1).astype(BF16)
        dlat_ref[...] = dlat
        dhn = _dot_nt(dlat, wdn_ref[...])
        _acc_rows(dg1_ref, dhn * m["xhat"])
        dh_ref[...] = dh1_ref[...] + _rms_bwd(dhn, g1_ref[...], m["xhat"], m["rx"], D)

    hb = lambda w: pl.BlockSpec((HEADS, TM, w), lambda i: (0, i, 0))
    return pl.pallas_call(
        body, name="mla_pre_bwd", grid=(t // TM,),
        in_specs=[hb(2 * DN), hb(2 * DN), hb(DN), _row(TM, D), _row(TM, D), _const((1, D)), _const((D, LATP)),
                  _const((1, QL)), _const((1, KVL)), _const((QL, 2 * D)), _const((KVL, 2 * D)),
                  _const((1, LANES)), _const((1, LANES)), _const((1, LANES)), _const((1, LANES)),
                  _row(TM, LANES), _row(TM, LANES)],
        out_specs=[_row(TM, D), _row(TM, D), _row(TM, QL), _row(TM, KVL), _row(TM, 2 * D), _row(TM, 2 * D),
                   _row(TM, LATP), _const((8, D)), _const((8, QL)), _const((8, KVL)), _const((8, LANES)),
                   _const((8, LANES)), _const((8, LANES)), _const((8, LANES))],
        out_shape=[_sds((t, D), F32), _sds((t, D), BF16), _sds((t, QL), BF16), _sds((t, KVL), BF16),
                   _sds((t, 2 * D), BF16), _sds((t, 2 * D), BF16), _sds((t, LATP), BF16),
                   _sds((8, D), F32), _sds((8, QL), F32), _sds((8, KVL), F32), _sds((8, LANES), F32),
                   _sds((8, LANES), F32), _sds((8, LANES), F32), _sds((8, LANES), F32)],
        compiler_params=_cp("arbitrary"),
    )(dq, dk, dv, dh1, h, g1, wdn, gq, gkv, wuq, wukv, gqn, gqr, gkn, gkr, cos, sin)


def gmlp_bwd(dh1, dh1b, h, pre, g1, allw, lay, lng, lnb, wm, wmt, bfull, tril):
    t = h.shape[0]

    def body(dh1_ref, dh1b_ref, h_ref, pre_ref, g1_ref, win_ref, lng_ref, lnb_ref, wm_ref, wmt_ref, b_ref,
             wout_ref, tril_ref, dh_ref, hn_ref, dpre_ref, dws_ref, dbs_ref, dlng_ref, dlnb_ref, dg1_ref,
             dvn_s):
        _zero_at_first_step(dws_ref, dbs_ref, dlng_ref, dlnb_ref, dg1_ref)
        g1 = g1_ref[...]
        yn, xhat, rx = _rms(h_ref[...], g1, D)
        hn_ref[...] = yn.astype(BF16)
        dy = _dot_nt(dh1b_ref[...], _rows_joined(wout_ref))
        pre_u = pre_ref[:, :GH].astype(F32)
        pre_v = pre_ref[:, GH:].astype(F32)
        u = _gelu(pre_u)
        v = _gelu(pre_v)
        xc = v - jnp.mean(v, axis=-1, keepdims=True)
        rs = lax.rsqrt(jnp.mean(xc * xc, axis=-1, keepdims=True) + EPS)
        vhat = xc * rs
        lng = lng_ref[...]
        vnb = (vhat * lng + lnb_ref[...]).astype(BF16)
        dsv = dy * u
        dsvb = dsv.astype(BF16)
        tril_m = tril_ref[...]
        gg_u = _gelu_grad(pre_u)
        for ch in range(TM // GC):
            rows = slice(ch * GC, (ch + 1) * GC)
            dbs_ref[...] += dsv[rows, :]
            for g in range(GG):
                cols = slice(g * GD, (g + 1) * GD)
                sv = _dot(wm_ref[g], vnb[rows, cols]) + b_ref[:, cols]
                dpre_ref[rows, cols] = (dy[rows, cols] * sv * gg_u[rows, cols]).astype(BF16)
                dvn_s[rows, cols] = _dot(wmt_ref[g], dsvb[rows, cols])
                dws_ref[g] += _dot_nt(dsvb[rows, cols], vnb[rows, cols]) * tril_m
        dvn = dvn_s[...]
        _acc_rows(dlng_ref, dvn * vhat)
        _acc_rows(dlnb_ref, dvn)
        dvhat = dvn * lng
        dv = rs * (dvhat - jnp.mean(dvhat, axis=-1, keepdims=True)
                   - vhat * jnp.mean(dvhat * vhat, axis=-1, keepdims=True))
        dpre_v = (dv * _gelu_grad(pre_v)).astype(BF16)
        dpre_ref[:, GH:] = dpre_v
        dhn = _dot_nt(dpre_ref[:, 0:D], win_ref[0])
        for c in range(1, N_CHIPS):
            dhn = dhn + _dot_nt(dpre_ref[:, c * D:(c + 1) * D], win_ref[c])
        _acc_rows(dg1_ref, dhn * xhat)
        dh_ref[...] = dh1_ref[...] + _rms_bwd(dhn, g1, xhat, rx, D)

    return pl.pallas_call(
        body, name="gmlp_bwd", grid=(t // TM,),
        in_specs=[_row(TM, D), _row(TM, D), _row(TM, D), _row(TM, 2 * GH), _const((1, D)), _wblk(D, lay["in"]),
                  _const((1, GH)), _const((1, GH)), _const((GG, GC, GC)), _const((GG, GC, GC)), _const((GC, GH)),
                  _wblk(GH // N_CHIPS, lay["out"]), _const((GC, GC))],
        out_specs=[_row(TM, D), _row(TM, D), _row(TM, 2 * GH), _const((GG, GC, GC)), _const((GC, GH)),
                   _const((8, GH)), _const((8, GH)), _const((8, D))],
        out_shape=[_sds((t, D), F32), _sds((t, D), BF16), _sds((t, 2 * GH), BF16), _sds((GG, GC, GC), F32),
                   _sds((GC, GH), F32), _sds((8, GH), F32), _sds((8, GH), F32), _sds((8, D), F32)],
        scratch_shapes=[pltpu.VMEM((TM, GH), F32)],
        compiler_params=_cp("arbitrary"),
    )(dh1, dh1b, h, pre, g1, allw, lng, lnb, wm, wmt, bfull, allw, tril)


def _token_step(t):
    return 1024 if t % 1024 == 0 else 512


def mm_tn(a, b):
    t, k = a.shape
    n = b.shape[1]
    tk = min(k, 1024)
    tn = min(n, 1024)
    tt = _token_step(t)

    def body(a_ref, b_ref, o_ref):
        @pl.when(pl.program_id(2) == 0)
        def _():
            o_ref[...] = jnp.zeros_like(o_ref)

        o_ref[...] += _dot_tn(a_ref[...].astype(BF16), b_ref[...].astype(BF16))

    return pl.pallas_call(
        body, name="mm_tn", grid=(k // tk, n // tn, t // tt),
        in_specs=[pl.BlockSpec((tt, tk), lambda i, j, s: (s, i)), pl.BlockSpec((tt, tn), lambda i, j, s: (s, j))],
        out_specs=pl.BlockSpec((tk, tn), lambda i, j, s: (i, j)), out_shape=_sds((k, n), F32),
        compiler_params=_cp("parallel", "parallel", "arbitrary"),
    )(a, b)


def mm_tn_into(buf, a, b, rows, row0, col_sharded):
    t = a.shape[0]
    tt = _token_step(t)
    assert row0 % rows == 0 and a.shape[1] == (rows if col_sharded else N_CHIPS * rows), (rows, row0, a.shape)
    assert b.shape[1] == (N_CHIPS * D if col_sharded else D), b.shape
    grid = (1, N_CHIPS, t // tt) if col_sharded else (N_CHIPS, 1, t // tt)

    def body(buf_ref, a_ref, b_ref, o_ref):
        del buf_ref

        @pl.when(pl.program_id(2) == 0)
        def _():
            o_ref[...] = jnp.zeros_like(o_ref)

        o_ref[...] += _dot_tn(a_ref[...].astype(BF16), b_ref[...].astype(BF16))

    return pl.pallas_call(
        body, name="mm_tn_into", grid=grid,
        in_specs=[_ANY, pl.BlockSpec((tt, rows), lambda i, j, s: (s, i)), pl.BlockSpec((tt, D), lambda i, j, s: (s, j))],
        out_specs=pl.BlockSpec((None, rows, D), lambda i, j, s: (i + j, row0 // rows, 0)),
        out_shape=_sds(buf.shape, F32), input_output_aliases={0: 0},
        compiler_params=_cp("parallel", "parallel", "arbitrary"),
    )(buf, a, b)


def adamw(w, g, m, v):
    rows, cols = w.shape
    tr = rows if rows <= 512 else next(r for r in (512, 384, 256, 128) if rows % r == 0)
    c1 = 1.0 - ADAM_B1 ** ADAM_STEP
    c2 = 1.0 - ADAM_B2 ** ADAM_STEP

    def body(w_ref, g_ref, m_ref, v_ref, d_ref, mo_ref, vo_ref):
        gv = g_ref[...]
        mn = ADAM_B1 * m_ref[...] + (1.0 - ADAM_B1) * gv
        vn = ADAM_B2 * v_ref[...] + (1.0 - ADAM_B2) * (gv * gv)
        mo_ref[...] = mn
        vo_ref[...] = vn
        d_ref[...] = -ADAM_LR * ((mn / c1) / (jnp.sqrt(vn / c2) + ADAM_EPS) + ADAM_WD * w_ref[...])

    spec = pl.BlockSpec((tr, cols), lambda i: (i, 0))
    return pl.pallas_call(
        body, name="adamw", grid=(rows // tr,),
        in_specs=[spec] * 4, out_specs=[spec] * 3, out_shape=[_sds((rows, cols), F32)] * 3,
        compiler_params=_cp("parallel"),
    )(w, g, m, v)


def _place():
    return lax.axis_index("x"), lax.axis_index("y"), lax.axis_index("c")


def _other_chips(x, y):
    return [(1 - x, y), (x, 1 - y), (1 - x, 1 - y)]


_ANY = pl.BlockSpec(memory_space=pl.ANY)


def gather_shards(mine):
    rr, cc = mine.shape
    hh = rr // 2
    assert rr % 32 == 0, rr

    def body(x_ref, o_ref, send_sems, recv_sems):
        x, y, c = _place()
        k = 2 * x + y
        chips = _other_chips(x, y)
        mine_half = pl.ds(pl.multiple_of(c * hh, 16), hh)
        other_half = pl.ds(pl.multiple_of((1 - c) * hh, 16), hh)

        def copy(j, src, dst, to):
            return pltpu.make_async_remote_copy(src_ref=src, dst_ref=dst, send_sem=send_sems.at[j],
                                                recv_sem=recv_sems.at[j], device_id=to, device_id_type=MESH)

        sends = [copy(j, x_ref.at[mine_half], o_ref.at[k, mine_half], (cx, cy, c))
                 for j, (cx, cy) in enumerate(chips)]
        for cp in sends:
            cp.start()
        passed = []
        for j, (cx, cy) in enumerate(chips):
            kk = 2 * cx + cy
            copy(j, x_ref.at[mine_half], o_ref.at[kk, mine_half], (cx, cy, c)).wait_recv()
            cp = copy(3 + j, o_ref.at[kk, mine_half], o_ref.at[kk, mine_half], (x, y, 1 - c))
            cp.start()
            passed.append(cp)
        for j, (cx, cy) in enumerate(chips):
            kk = 2 * cx + cy
            copy(3 + j, o_ref.at[kk, other_half], o_ref.at[kk, other_half], (x, y, 1 - c)).wait_recv()
        for cp in sends + passed:
            cp.wait_send()

    return pl.pallas_call(
        body, name="gather_shards", in_specs=[_ANY], out_specs=_ANY,
        out_shape=_sds((N_CHIPS, rr, cc), mine.dtype),
        scratch_shapes=[pltpu.SemaphoreType.DMA((6,)), pltpu.SemaphoreType.DMA((6,))],
    )(mine)


def swap_halves(g):
    _, rr, cc = g.shape
    hh = rr // 2

    def body(g_ref, o_ref, send_sems, recv_sems):
        x, y, c = _place()
        other_half = pl.ds(pl.multiple_of((1 - c) * hh, 16), hh)
        copies = [pltpu.make_async_remote_copy(src_ref=g_ref.at[k, other_half], dst_ref=o_ref.at[k],
                                               send_sem=send_sems.at[k], recv_sem=recv_sems.at[k],
                                               device_id=(x, y, 1 - c), device_id_type=MESH)
                  for k in range(N_CHIPS)]
        for cp in copies:
            cp.start()
        for cp in copies:
            cp.wait()

    return pl.pallas_call(
        body, name="swap_halves", in_specs=[_ANY], out_specs=_ANY,
        out_shape=_sds((N_CHIPS, hh, cc), g.dtype),
        scratch_shapes=[pltpu.SemaphoreType.DMA((N_CHIPS,)), pltpu.SemaphoreType.DMA((N_CHIPS,))],
    )(g)


def chip_sum(place, g32, got):
    _, rr, cc = g32.shape
    hh = rr // 2
    tr = SUM_ROWS
    assert rr % 2 == 0 and hh % tr == 0, (rr, tr)
    nb = hh // tr

    def body(place_ref, g_ref, got_ref, own_ref, all_ref):
        s = g_ref[...] + got_ref[...].astype(F32)
        all_ref[...] = s.astype(BF16)
        own_ref[...] = g_ref[place_ref[1]] + got_ref[place_ref[1]].astype(F32)

    return pl.pallas_call(
        body, name="chip_sum",
        grid_spec=pltpu.PrefetchScalarGridSpec(
            num_scalar_prefetch=1, grid=(nb,),
            in_specs=[pl.BlockSpec((N_CHIPS, tr, cc), lambda i, pr: (0, pr[0] * nb + i, 0)),
                      pl.BlockSpec((N_CHIPS, tr, cc), lambda i, pr: (0, i, 0))],
            out_specs=[pl.BlockSpec((tr, cc), lambda i, pr: (i, 0)),
                       pl.BlockSpec((N_CHIPS, tr, cc), lambda i, pr: (0, i, 0))]),
        out_shape=[_sds((hh, cc), F32), _sds((N_CHIPS, hh, cc), BF16)],
        compiler_params=_cp("parallel"),
    )(place, g32, got)


def scatter_to_chips(s):
    _, hh, cc = s.shape

    def body(s_ref, o_ref, send_sems, recv_sems):
        x, y, c = _place()
        copies = [pltpu.make_async_remote_copy(src_ref=s_ref.at[2 * cx + cy], dst_ref=o_ref.at[j],
                                               send_sem=send_sems.at[j], recv_sem=recv_sems.at[j],
                                               device_id=(cx, cy, c), device_id_type=MESH)
                  for j, (cx, cy) in enumerate(_other_chips(x, y))]
        for cp in copies:
            cp.start()
        for cp in copies:
            cp.wait()

    return pl.pallas_call(
        body, name="scatter_to_chips", in_specs=[_ANY], out_specs=_ANY,
        out_shape=_sds((3, hh, cc), s.dtype),
        scratch_shapes=[pltpu.SemaphoreType.DMA((3,)), pltpu.SemaphoreType.DMA((3,))],
    )(s)


def final_sum(own, got):
    hh, cc = own.shape
    tr = SUM_ROWS
    assert hh % tr == 0, (hh, tr)

    def body(own_ref, got_ref, o_ref):
        o_ref[...] = ((own_ref[...] + got_ref[0].astype(F32)) + got_ref[1].astype(F32)) + got_ref[2].astype(F32)

    return pl.pallas_call(
        body, name="final_sum", grid=(hh // tr,),
        in_specs=[pl.BlockSpec((tr, cc), lambda i: (i, 0)), pl.BlockSpec((3, tr, cc), lambda i: (0, i, 0))],
        out_specs=pl.BlockSpec((tr, cc), lambda i: (i, 0)),
        out_shape=_sds((hh, cc), F32),
        compiler_params=_cp("parallel"),
    )(own, got)


def share_with_sibling(f):
    hh, cc = f.shape

    def body(f_ref, o_ref, send_sem, recv_sem):
        x, y, c = _place()
        mine_half = pl.ds(pl.multiple_of(c * hh, 8), hh)
        cp = pltpu.make_async_remote_copy(src_ref=f_ref, dst_ref=o_ref.at[mine_half], send_sem=send_sem,
                                          recv_sem=recv_sem, device_id=(x, y, 1 - c), device_id_type=MESH)
        cp.start()
        cp.wait()

    return pl.pallas_call(
        body, name="share_with_sibling", in_specs=[_ANY], out_specs=_ANY,
        out_shape=_sds((2 * hh, cc), f.dtype),
        scratch_shapes=[pltpu.SemaphoreType.DMA, pltpu.SemaphoreType.DMA],
    )(f)


def allreduce_small(part):
    rr, cc = part.shape
    n_dev = 8

    def body(x_ref, all_ref, sum_ref, send_sems, recv_sems, local_sem):
        x, y, c = _place()
        me, sibling = (x, y, c), (x, y, 1 - c)
        chips = _other_chips(x, y)

        def rows(px, py, pc):
            return all_ref.at[pl.ds(pl.multiple_of((4 * px + 2 * py + pc) * rr, 8), rr), :]

        def copy(j, block, to, src=None):
            return pltpu.make_async_remote_copy(src_ref=rows(*block) if src is None else src, dst_ref=rows(*block),
                                                send_sem=send_sems.at[j], recv_sem=recv_sems.at[j],
                                                device_id=to, device_id_type=MESH)

        mine = pltpu.make_async_copy(x_ref, rows(*me), local_sem)
        mine.start()
        first = [copy(0, me, sibling, src=x_ref)]
        first += [copy(1 + j, me, (*chip, c), src=x_ref) for j, chip in enumerate(chips)]
        for cp in first:
            cp.start()
        passed = [copy(4 + j, (*chip, c), sibling) for j, chip in enumerate(chips)]
        for j, chip in enumerate(chips):
            copy(1 + j, (*chip, c), me).wait_recv()
            passed[j].start()
        copy(0, sibling, me).wait_recv()
        for j, chip in enumerate(chips):
            copy(4 + j, (*chip, 1 - c), me).wait_recv()
        for cp in first + passed:
            cp.wait_send()
        mine.wait()
        acc = all_ref[0:rr, :]
        for d in range(1, n_dev):
            acc = acc + all_ref[d * rr:(d + 1) * rr, :]
        sum_ref[...] = acc

    vm = pl.BlockSpec(memory_space=pltpu.VMEM)
    return pl.pallas_call(
        body, name="allreduce_small", in_specs=[vm], out_specs=[vm, vm],
        out_shape=[_sds((n_dev * rr, cc), F32), _sds((rr, cc), F32)],
        scratch_shapes=[pltpu.SemaphoreType.DMA((7,)), pltpu.SemaphoreType.DMA((7,)), pltpu.SemaphoreType.DMA],
        compiler_params=pltpu.CompilerParams(vmem_limit_bytes=VMEM_LIMIT),
    )(part)[1]


_BIG = ["mla_w_down", "mla_w_uq", "mla_w_ukv", "mla_w_out", "gmlp_w_in", "gmlp_w_out", "ffn_w_up", "ffn_w_down",
        "ple_w_gate", "ple_w_proj"]
_SMALL = ["norm_mix", "norm_ffn", "norm_ple", "mla_q_lora_g", "mla_kv_lora_g", "mla_q_nope_g", "mla_q_rope_g",
          "mla_k_nope_g", "mla_k_rope_g", "gmlp_ln_g", "gmlp_ln_b", "gmlp_w_s", "gmlp_b_s"]

_LAY_MLA = dict(up=0, down=1024, out=2048, gate=2304, wdn=2560, wuq=2736, wukv=2880, proj=3008, rows=3072)
_LAY_GMLP = {"up": 0, "down": 1024, "in": 2048, "out": 3072, "gate": 3584, "proj": 3840, "ln": 3904, "rows": 4096}


def _layer_parts(i):
    j = i // 2
    if i % 2 == 0:
        lay = _LAY_MLA
        return lay, [("ffn_w_up", i, lay["up"]), ("ffn_w_down", i, lay["down"]), ("mla_w_out", j, lay["out"]),
                     ("ple_w_gate", i, lay["gate"]), ("mla_w_down", j, lay["wdn"]), ("mla_w_uq", j, lay["wuq"]),
                     ("mla_w_ukv", j, lay["wukv"]), ("ple_w_proj", i, lay["proj"])]
    lay = _LAY_GMLP
    return lay, [("ffn_w_up", i, lay["up"]), ("ffn_w_down", i, lay["down"]), ("gmlp_w_in", j, lay["in"]),
                 ("gmlp_w_out", j, lay["out"]), ("ple_w_gate", i, lay["gate"]), ("ple_w_proj", i, lay["proj"])]


def _pack_rows(parts, dtype, pad_to=None):
    flat = jnp.concatenate([p.reshape(-1).astype(dtype) for p in parts])
    if pad_to is not None:
        flat = jnp.pad(flat, (0, pad_to * D - flat.size))
    return flat.reshape(-1, D)


def _odd(allw, row0, a, b):
    return allw[:, row0:row0 + a * b // D].reshape(N_CHIPS, a, b)


def _cols_joined(s):
    return jnp.transpose(s, (1, 0, 2)).reshape(s.shape[1], N_CHIPS * s.shape[2])


def _col_shards(full):
    a, bb = full.shape
    return jnp.transpose(full.reshape(a, N_CHIPS, bb // N_CHIPS), (1, 0, 2)).reshape(N_CHIPS, -1, D)


def _pad_lanes(g):
    return jnp.pad(g, ((0, 0), (0, LANES - g.shape[1])))


def _split_uq(wuq):
    l = wuq.shape[0]
    w = wuq.reshape(l, QL, HEADS, DN + DR)
    nope = w[..., :DN].reshape(l, QL, HEADS * DN)
    rope = jnp.pad(w[..., DN:], ((0, 0), (0, 0), (0, 0), (0, LANES - DR))).reshape(l, QL, HEADS * LANES)
    return jnp.concatenate([nope, rope], axis=-1)


def _merge_uq(d):
    nope = d[:, :HEADS * DN].reshape(QL, HEADS, DN)
    rope = d[:, HEADS * DN:].reshape(QL, HEADS, LANES)[..., :DR]
    return jnp.concatenate([nope, rope], axis=-1).reshape(QL, HEADS * (DN + DR))


def _rope_tables(positions):
    inv_freq = ROPE_BASE ** (-(jnp.arange(0, DR, 2, dtype=F32) / DR))
    ang = positions.reshape(-1).astype(F32)[:, None] * inv_freq
    z = jnp.zeros((ang.shape[0], LANES - DR), F32)
    return (jnp.concatenate([jnp.cos(ang), jnp.cos(ang), z], axis=1),
            jnp.concatenate([jnp.sin(ang), jnp.sin(ang), z], axis=1))


def kernel(x, p, positions, norm_mix, norm_ffn, norm_ple, mla_w_down, mla_q_lora_g, mla_kv_lora_g, mla_w_uq, mla_w_ukv, mla_q_nope_g, mla_q_rope_g, mla_k_nope_g, mla_k_rope_g, mla_w_out, gmlp_w_in, gmlp_ln_g, gmlp_ln_b, gmlp_w_s, gmlp_b_s, gmlp_w_out, ffn_w_up, ffn_w_down, ple_w_gate, ple_w_proj, loss_target, m_norm_mix, m_norm_ffn, m_norm_ple, m_mla_w_down, m_mla_q_lora_g, m_mla_kv_lora_g, m_mla_w_uq, m_mla_w_ukv, m_mla_q_nope_g, m_mla_q_rope_g, m_mla_k_nope_g, m_mla_k_rope_g, m_mla_w_out, m_gmlp_w_in, m_gmlp_ln_g, m_gmlp_ln_b, m_gmlp_w_s, m_gmlp_b_s, m_gmlp_w_out, m_ffn_w_up, m_ffn_w_down, m_ple_w_gate, m_ple_w_proj, v_norm_mix, v_norm_ffn, v_norm_ple, v_mla_w_down, v_mla_q_lora_g, v_mla_kv_lora_g, v_mla_w_uq, v_mla_w_ukv, v_mla_q_nope_g, v_mla_q_rope_g, v_mla_k_nope_g, v_mla_k_rope_g, v_mla_w_out, v_gmlp_w_in, v_gmlp_ln_g, v_gmlp_ln_b, v_gmlp_w_s, v_gmlp_b_s, v_gmlp_w_out, v_ffn_w_up, v_ffn_w_down, v_ple_w_gate, v_ple_w_proj):
    args = dict(locals())
    weights = {n: args[n] for n in _BIG + _SMALL}
    depth = norm_mix.shape[0]
    nb, seq, _ = x.shape
    t = nb * seq
    assert seq % TQ == 0 and seq % TM == 0 and t % 512 == 0, (nb, seq)
    cx = lax.axis_index("x")
    cy = lax.axis_index("y")
    cc = lax.axis_index("c")
    chip = 2 * cx + cy

    allw = []
    for i in range(depth):
        lay, parts = _layer_parts(i)
        rows = [weights[n][l] for n, l, _ in parts]
        if i % 2 == 1:
            ln = jnp.stack([gmlp_ln_g[i // 2], gmlp_ln_b[i // 2]]).astype(F32)
            rows.append(lax.bitcast_convert_type(ln, BF16))
        mine = _pack_rows(rows, BF16, pad_to=lay["rows"])
        allw.append(lax.dynamic_update_slice(gather_shards(mine), mine[None], (chip, 0, 0)))

    tril = jnp.tril(jnp.ones((GC, GC), F32))
    wm = (gmlp_w_s * tril).astype(BF16)
    wmt = jnp.swapaxes(wm, -1, -2)
    bfull = jnp.repeat(jnp.swapaxes(gmlp_b_s, -1, -2), GD, axis=-1)
    cos, sin = _rope_tables(positions)
    row = lambda g: g.reshape(1, -1)
    gqr = _pad_lanes(mla_q_rope_g)
    gkr = _pad_lanes(mla_k_rope_g)

    h = x.reshape(t, D)
    pt = p.reshape(depth, t, PLE)
    saved = []
    for i in range(depth):
        j = i // 2
        lay, _ = _layer_parts(i)
        aw = allw[i]
        s = dict(h=h)
        if i % 2 == 0:
            wdn = jnp.pad(_odd(aw, lay["wdn"], D // N_CHIPS, LAT).reshape(D, LAT), ((0, 0), (0, LATP - LAT)))
            wuq = _split_uq(_cols_joined(_odd(aw, lay["wuq"], QL, 384))[None])[0]
            wukv = _cols_joined(_odd(aw, lay["wukv"], KVL, 512))
            mla_args = (row(norm_mix[i]), wdn, row(mla_q_lora_g[j]), row(mla_kv_lora_g[j]), wuq, wukv,
                        row(mla_q_nope_g[j]), gqr[j:j + 1], row(mla_k_nope_g[j]), gkr[j:j + 1], cos, sin)
            q, k, v = mla_pre_fwd(h, *mla_args)
            y, lse = flash_fwd(q, k, v, seq)
            s.update(q=q, k=k, v=v, lse=lse, mla_args=mla_args)
        else:
            ln = lax.bitcast_convert_type(aw[:, lay["ln"]:lay["ln"] + 2].reshape(N_CHIPS, 2, GH // N_CHIPS, 2), F32)
            ln = jnp.transpose(ln, (1, 0, 2)).reshape(2, 1, GH)
            y, pre = gmlp_fwd(h, row(norm_mix[i]), aw, lay, ln[0], ln[1], wm[j], bfull[j])
            s.update(pre=pre, ln=ln)
        wp = _cols_joined(_odd(aw, lay["proj"], PLE, 256))
        h1, h2, hn2, r = mixffn_fwd(h, y, aw, lay, row(norm_ffn[i]))
        h, hn3 = ple_fwd(h2, pt[i], row(norm_ple[i]), aw, lay, wp)
        s.update(y=y, wp=wp, h1=h1, h2=h2, hn2=hn2, r=r, hn3=hn3)
        saved.append(s)

    dh, loss_part = loss_head(h, loss_target.reshape(t, D))
    loss = lax.psum(loss_part[0, 0], ("x", "y", "c"))

    gs = {n: [None] * weights[n].shape[0] for n in _SMALL}
    gw = {n: [None] * weights[n].shape[0] for n in _BIG}
    place = jnp.stack([cc, chip]).astype(jnp.int32)
    for i in reversed(range(depth)):
        j = i // 2
        lay, parts = _layer_parts(i)
        aw = allw[i]
        s = saved[i]
        buf = jnp.zeros((N_CHIPS, lay["rows"], D), F32)

        def put(b, row0, shards):
            return lax.dynamic_update_slice(b, shards.reshape(N_CHIPS, -1, D), (0, row0, 0))

        dh2, dh2b, dgt, dpp, dg3 = ple_bwd(dh, s["h2"], pt[i], row(norm_ple[i]), aw, lay, s["wp"])
        gs["norm_ple"][i] = dg3[0]
        buf = mm_tn_into(buf, s["hn3"], dgt, D // N_CHIPS, lay["gate"], False)
        buf = put(buf, lay["proj"], _col_shards(mm_tn(pt[i], dpp)))
        dh1, dh1b, du, a, dg2 = ffn_bwd(dh2, dh2b, s["h1"], s["r"], row(norm_ffn[i]), aw, lay)
        gs["norm_ffn"][i] = dg2[0]
        buf = mm_tn_into(buf, a, dh2b, D, lay["down"], False)
        buf = mm_tn_into(buf, s["hn2"], du, D, lay["up"], True)
        buf = mm_tn_into(buf, s["y"], dh1b, s["y"].shape[1] // N_CHIPS, lay["out"], False)
        if i % 2 == 0:
            do = linear_nt(dh1b, aw, D // N_CHIPS, lay["out"])
            dq, dk, dv = flash_bwd(s["q"], s["k"], s["v"], s["y"], do, s["lse"], seq)
            (dh, hn1, cq, ckv, dqp, dkvp, dlat, dg1, dgq, dgkv, dgqn, dgqr, dgkn, dgkr) = mla_pre_bwd(
                dq, dk, dv, dh1, s["h"], *s["mla_args"])
            gs["norm_mix"][i] = dg1[0]
            gs["mla_q_lora_g"][j] = dgq[0]
            gs["mla_kv_lora_g"][j] = dgkv[0]
            gs["mla_q_nope_g"][j] = dgqn[0]
            gs["mla_q_rope_g"][j] = dgqr[0, :DR]
            gs["mla_k_nope_g"][j] = dgkn[0]
            gs["mla_k_rope_g"][j] = dgkr[0, :DR]
            buf = put(buf, lay["wdn"], mm_tn(hn1, dlat)[:, :LAT])
            buf = put(buf, lay["wuq"], _col_shards(_merge_uq(mm_tn(cq, dqp))))
            buf = put(buf, lay["wukv"], _col_shards(mm_tn(ckv, dkvp)))
        else:
            dh, hn1, dpre, dws, dbs, dlng, dlnb, dg1 = gmlp_bwd(
                dh1, dh1b, s["h"], s["pre"], row(norm_mix[i]), aw, lay, s["ln"][0], s["ln"][1], wm[j], wmt[j],
                bfull[j], tril)
            gs["norm_mix"][i] = dg1[0]
            gs["gmlp_ln_g"][j] = dlng[0]
            gs["gmlp_ln_b"][j] = dlnb[0]
            gs["gmlp_w_s"][j] = dws
            gs["gmlp_b_s"][j] = jnp.sum(dbs.reshape(GC, GG, GD), axis=-1).T
            buf = mm_tn_into(buf, hn1, dpre, D, lay["in"], True)

        own, sums = chip_sum(place, buf, swap_halves(buf))
        half = final_sum(own, scatter_to_chips(sums))
        reduced = lax.dynamic_update_slice(share_with_sibling(half), half, (cc * half.shape[0], 0))
        for n, l, row0 in parts:
            gw[n][l] = reduced[row0:row0 + weights[n][l].size // D].reshape(weights[n].shape[1:])
    grad_x = dh.reshape(x.shape)
    grads = {n: jnp.stack(gw[n]) for n in _BIG}

    small_sizes = [weights[n].size if n not in ("gmlp_ln_g", "gmlp_ln_b") else weights[n].shape[0] * GH
                   for n in _SMALL]
    small_rows = -(-sum(small_sizes) // (8 * D)) * 8
    part = _pack_rows([jnp.stack(gs[n]) for n in _SMALL], F32, pad_to=small_rows)
    tot = allreduce_small(part).reshape(-1)
    off = 0
    for n, sz in zip(_SMALL, small_sizes):
        gsum = tot[off:off + sz]
        off += sz
        if n in ("gmlp_ln_g", "gmlp_ln_b"):
            gsum = lax.dynamic_slice_in_dim(gsum.reshape(-1, GH), chip * (GH // N_CHIPS), GH // N_CHIPS, axis=1)
        grads[n] = gsum.reshape(weights[n].shape)

    delta, new_m, new_v = {}, {}, {}
    for n in _BIG:
        w2 = weights[n].reshape(-1, weights[n].shape[-1])
        d, mn, vn = adamw(w2, grads[n].reshape(w2.shape), args["m_" + n].reshape(w2.shape),
                          args["v_" + n].reshape(w2.shape))
        delta[n], new_m[n], new_v[n] = (a.reshape(weights[n].shape) for a in (d, mn, vn))
    own_sizes = [weights[n].size for n in _SMALL]
    own_rows = -(-sum(own_sizes) // (8 * D)) * 8
    packed = [_pack_rows([src[n] for n in _SMALL], F32, pad_to=own_rows)
              for src in (weights, grads, {n: args["m_" + n] for n in _SMALL}, {n: args["v_" + n] for n in _SMALL})]
    outs = adamw(*packed)
    off = 0
    for n, sz in zip(_SMALL, own_sizes):
        for dst, o in zip((delta, new_m, new_v), outs):
            dst[n] = o.reshape(-1)[off:off + sz].reshape(weights[n].shape)
        off += sz

    order = ["norm_mix", "norm_ffn", "norm_ple", "mla_w_down", "mla_q_lora_g", "mla_kv_lora_g", "mla_w_uq",
             "mla_w_ukv", "mla_q_nope_g", "mla_q_rope_g", "mla_k_nope_g", "mla_k_rope_g", "mla_w_out", "gmlp_w_in",
             "gmlp_ln_g", "gmlp_ln_b", "gmlp_w_s", "gmlp_b_s", "gmlp_w_out", "ffn_w_up", "ffn_w_down", "ple_w_gate",
             "ple_w_proj"]
    return (loss, grad_x, *[grads[n] for n in order], *[delta[n] for n in order], *[new_m[n] for n in order],
            *[new_v[n] for n in order])
```

```python
import functools

import jax
import jax.numpy as jnp
from jax import lax
from jax.experimental import pallas as pl
from jax.experimental.pallas import tpu as pltpu

F32 = jnp.float32
BF16 = jnp.bfloat16
MESH = pl.DeviceIdType.MESH

D = 1024
HEADS = 8
DN = 128
DR = 64
QL = 384
KVL = 256
LAT = 704
LATP = 768
DFF = 4096
GH = 2048
GC = 128
GG = 8
GD = 256
PLE = 256
EPS = 1e-6
ROPE_BASE = 10000.0
SM_SCALE = (DN + DR) ** -0.5
N_CHIPS = 4
LANES = 128

ADAM_LR = 0.001
ADAM_B1 = 0.9
ADAM_B2 = 0.999
ADAM_EPS = 1e-08
ADAM_WD = 0.01
ADAM_STEP = 10

TM = 256
TQ = 256
SUM_ROWS = 256
VMEM_LIMIT = 56 * 1024 * 1024


def _cp(*sem):
    return pltpu.CompilerParams(dimension_semantics=sem, vmem_limit_bytes=VMEM_LIMIT)


def _dot(a, b):
    return jnp.dot(a, b, preferred_element_type=F32)


def _dot_nt(a, b):
    return lax.dot_general(a, b, (((1,), (1,)), ((), ())), preferred_element_type=F32)


def _dot_tn(a, b):
    return lax.dot_general(a, b, (((0,), (0,)), ((), ())), preferred_element_type=F32)


def _rms(x, g, n):
    r = lax.rsqrt(jnp.sum(x * x, axis=-1, keepdims=True) * (1.0 / n) + EPS)
    xhat = x * r
    return xhat * g, xhat, r


def _rms_bwd(dy, g, xhat, r, n):
    dxhat = dy * g
    return r * (dxhat - xhat * (jnp.sum(dxhat * xhat, axis=-1, keepdims=True) * (1.0 / n)))


def _rope(x, c, s):
    return x * c + (pltpu.roll(x, 32, 1) - pltpu.roll(x, 96, 1)) * s


def _rope_t(dy, c, s):
    w = dy * s
    return dy * c + pltpu.roll(w, 96, 1) - pltpu.roll(w, 32, 1)


def _sigmoid(x):
    return 1.0 / (1.0 + jnp.exp(-x))


_GELU_K = 0.7978845608028654
_GELU_C = 0.044715


def _gelu(x):
    return 0.5 * x * (1.0 + jnp.tanh(_GELU_K * (x + _GELU_C * x * x * x)))


def _gelu_grad(x):
    t = jnp.tanh(_GELU_K * (x + _GELU_C * x * x * x))
    return 0.5 * (1.0 + t) + 0.5 * x * (1.0 - t * t) * (_GELU_K * (1.0 + 3.0 * _GELU_C * x * x))


def _acc_rows(ref, val):
    ref[...] += jnp.broadcast_to(jnp.sum(val, axis=0, keepdims=True), ref.shape)


def _row(tm, c):
    return pl.BlockSpec((tm, c), lambda i: (i, 0))


def _const(shape):
    nd = len(shape)
    return pl.BlockSpec(shape, lambda i: (0,) * nd, pipeline_mode=pl.Buffered(1))


def _wblk(rows, row0):
    assert row0 % rows == 0, (rows, row0)
    return pl.BlockSpec((N_CHIPS, rows, D), lambda i: (0, row0 // rows, 0), pipeline_mode=pl.Buffered(1))


def _rows_joined(w_ref):
    return w_ref[...].reshape(N_CHIPS * w_ref.shape[1], D)


def _sds(shape, dtype):
    return jax.ShapeDtypeStruct(shape, dtype)


def mixffn_fwd(h, y, allw, lay, g2):
    t, k = y.shape

    def body(h_ref, y_ref, wo_ref, g_ref, wu_ref, wd_ref, h1_ref, h2_ref, hn_ref, r_ref):
        h1 = h_ref[...] + _dot(y_ref[...], _rows_joined(wo_ref))
        h1_ref[...] = h1
        yn, _, _ = _rms(h1, g_ref[...], D)
        hn = yn.astype(BF16)
        hn_ref[...] = hn
        f = jnp.zeros((TM, D), F32)
        for c in range(N_CHIPS):
            r = jnp.maximum(_dot(hn, wu_ref[c]), 0.0)
            r_ref[:, c * D:(c + 1) * D] = r.astype(BF16)
            f = f + _dot((r * r).astype(BF16), wd_ref[c])
        h2_ref[...] = h1 + f

    return pl.pallas_call(
        body, name="mixffn_fwd", grid=(t // TM,),
        in_specs=[_row(TM, D), _row(TM, k), _wblk(k // N_CHIPS, lay["out"]), _const((1, D)), _wblk(D, lay["up"]),
                  _wblk(D, lay["down"])],
        out_specs=[_row(TM, D), _row(TM, D), _row(TM, D), _row(TM, DFF)],
        out_shape=[_sds((t, D), F32), _sds((t, D), F32), _sds((t, D), BF16), _sds((t, DFF), BF16)],
        compiler_params=_cp("parallel"),
    )(h, y, allw, g2, allw, allw)


def ple_fwd(h2, p, g3, allw, lay, wp):
    t = h2.shape[0]

    def body(h_ref, p_ref, g_ref, wg_ref, wp_ref, h3_ref, hn_ref):
        x = h_ref[...]
        yn, _, _ = _rms(x, g_ref[...], D)
        hn = yn.astype(BF16)
        hn_ref[...] = hn
        gt = _dot(hn, _rows_joined(wg_ref))
        pp = _dot(p_ref[...].astype(BF16), wp_ref[...])
        h3_ref[...] = x + _sigmoid(gt) * pp

    return pl.pallas_call(
        body, name="ple_fwd", grid=(t // TM,),
        in_specs=[_row(TM, D), _row(TM, PLE), _const((1, D)), _wblk(D // N_CHIPS, lay["gate"]), _const((PLE, D))],
        out_specs=[_row(TM, D), _row(TM, D)],
        out_shape=[_sds((t, D), F32), _sds((t, D), BF16)],
        compiler_params=_cp("parallel"),
    )(h2, p, g3, allw, wp)


def _mla_project(h_ref, g1_ref, wdn_ref, gq_ref, gkv_ref, wuq_ref, wukv_ref):
    x = h_ref[...]
    yn, xhat, rx = _rms(x, g1_ref[...], D)
    hn = yn.astype(BF16)
    lat = _dot(hn, wdn_ref[...])
    cq, cqhat, rq = _rms(lat[:, :QL], gq_ref[...], QL)
    ckv, ckvhat, rkv = _rms(lat[:, QL:QL + KVL], gkv_ref[...], KVL)
    kr_raw = lat[:, QL + KVL:]
    cqb = cq.astype(BF16)
    ckvb = ckv.astype(BF16)
    qp = _dot(cqb, wuq_ref[...])
    kvp = _dot(ckvb, wukv_ref[...])
    return dict(xhat=xhat, rx=rx, hn=hn, cqhat=cqhat, rq=rq, ckvhat=ckvhat, rkv=rkv, kr_raw=kr_raw,
                cqb=cqb, ckvb=ckvb, qp=qp, kvp=kvp)


def mla_pre_fwd(h, g1, wdn, gq, gkv, wuq, wukv, gqn, gqr, gkn, gkr, cos, sin):
    t = h.shape[0]

    def body(h_ref, g1_ref, wdn_ref, gq_ref, gkv_ref, wuq_ref, wukv_ref, gqn_ref, gqr_ref, gkn_ref, gkr_ref,
             c_ref, s_ref, q_ref, k_ref, v_ref):
        m = _mla_project(h_ref, g1_ref, wdn_ref, gq_ref, gkv_ref, wuq_ref, wukv_ref)
        c = c_ref[...]
        s = s_ref[...]
        kr, _, _ = _rms(m["kr_raw"], gkr_ref[...], DR)
        krb = _rope(kr, c, s).astype(BF16)
        for hd in range(HEADS):
            qn, _, _ = _rms(m["qp"][:, hd * DN:(hd + 1) * DN], gqn_ref[...], DN)
            qr, _, _ = _rms(m["qp"][:, D + hd * LANES:D + (hd + 1) * LANES], gqr_ref[...], DR)
            q_ref[hd, :, 0:DN] = (qn * SM_SCALE).astype(BF16)
            q_ref[hd, :, DN:2 * DN] = (_rope(qr, c, s) * SM_SCALE).astype(BF16)
            kn, _, _ = _rms(m["kvp"][:, hd * 2 * DN:hd * 2 * DN + DN], gkn_ref[...], DN)
            k_ref[hd, :, 0:DN] = kn.astype(BF16)
            k_ref[hd, :, DN:2 * DN] = krb
            v_ref[hd] = m["kvp"][:, hd * 2 * DN + DN:(hd + 1) * 2 * DN].astype(BF16)

    hb = lambda w: pl.BlockSpec((HEADS, TM, w), lambda i: (0, i, 0))
    return pl.pallas_call(
        body, name="mla_pre_fwd", grid=(t // TM,),
        in_specs=[_row(TM, D), _const((1, D)), _const((D, LATP)), _const((1, QL)), _const((1, KVL)),
                  _const((QL, 2 * D)), _const((KVL, 2 * D)), _const((1, LANES)), _const((1, LANES)),
                  _const((1, LANES)), _const((1, LANES)), _row(TM, LANES), _row(TM, LANES)],
        out_specs=[hb(2 * DN), hb(2 * DN), hb(DN)],
        out_shape=[_sds((HEADS, t, 2 * DN), BF16), _sds((HEADS, t, 2 * DN), BF16), _sds((HEADS, t, DN), BF16)],
        compiler_params=_cp("parallel"),
    )(h, g1, wdn, gq, gkv, wuq, wukv, gqn, gqr, gkn, gkr, cos, sin)


def _diagonal_mask():
    return lax.broadcasted_iota(jnp.int32, (TQ, TQ), 1) <= lax.broadcasted_iota(jnp.int32, (TQ, TQ), 0)


def flash_fwd(q, k, v, seq):
    t = q.shape[1]
    nb = t // seq
    nq = seq // TQ

    def body(q_ref, k_ref, v_ref, o_ref, lse_ref):
        qi = pl.program_id(2)
        qv = q_ref[0]

        def step(j, carry, diagonal=False):
            m, l, acc = carry
            rows = pl.ds(pl.multiple_of(j * TQ, TQ), TQ)
            s = _dot_nt(qv, k_ref[0, rows, :])
            if diagonal:
                s = jnp.where(_diagonal_mask(), s, -1e30)
            m_new = jnp.maximum(m, jnp.max(s, axis=-1, keepdims=True))
            p = jnp.exp(s - m_new)
            alpha = jnp.exp(m - m_new)
            l = alpha * l + jnp.sum(p, axis=-1, keepdims=True)
            acc = alpha * acc + _dot(p.astype(BF16), v_ref[0, rows, :])
            return m_new, l, acc

        init = (jnp.full((TQ, 1), -1e30, F32), jnp.zeros((TQ, 1), F32), jnp.zeros((TQ, DN), F32))
        m, l, acc = step(qi, lax.fori_loop(0, qi, step, init), diagonal=True)
        o_ref[...] = (acc / l).astype(BF16)
        lse_ref[0] = m + jnp.log(l)

    return pl.pallas_call(
        body, name="flash_fwd", grid=(nb, HEADS, nq),
        in_specs=[pl.BlockSpec((1, TQ, 2 * DN), lambda b, h, i: (h, b * nq + i, 0)),
                  pl.BlockSpec((1, seq, 2 * DN), lambda b, h, i: (h, b, 0)),
                  pl.BlockSpec((1, seq, DN), lambda b, h, i: (h, b, 0))],
        out_specs=[pl.BlockSpec((TQ, DN), lambda b, h, i: (b * nq + i, h)),
                   pl.BlockSpec((1, TQ, 1), lambda b, h, i: (h, b * nq + i, 0))],
        out_shape=[_sds((t, HEADS * DN), BF16), _sds((HEADS, t, 1), F32)],
        compiler_params=_cp("parallel", "parallel", "arbitrary"),
    )(q, k, v)


def _gmlp_in(hn, win_ref):
    pre = [_dot(hn, win_ref[c]) for c in range(N_CHIPS)]
    return jnp.concatenate(pre[:2], axis=1), jnp.concatenate(pre[2:], axis=1)


def gmlp_fwd(h, g1, allw, lay, lng, lnb, wm, bfull):
    t = h.shape[0]

    def body(h_ref, g1_ref, win_ref, lng_ref, lnb_ref, wm_ref, b_ref, y_ref, pre_ref):
        yn, _, _ = _rms(h_ref[...], g1_ref[...], D)
        pre_u, pre_v = _gmlp_in(yn.astype(BF16), win_ref)
        pre_ref[:, :GH] = pre_u.astype(BF16)
        pre_ref[:, GH:] = pre_v.astype(BF16)
        u = _gelu(pre_u)
        v = _gelu(pre_v)
        xc = v - jnp.mean(v, axis=-1, keepdims=True)
        rs = lax.rsqrt(jnp.mean(xc * xc, axis=-1, keepdims=True) + EPS)
        vnb = (xc * rs * lng_ref[...] + lnb_ref[...]).astype(BF16)
        for ch in range(TM // GC):
            rows = slice(ch * GC, (ch + 1) * GC)
            for g in range(GG):
                cols = slice(g * GD, (g + 1) * GD)
                sv = _dot(wm_ref[g], vnb[rows, cols]) + b_ref[:, cols]
                y_ref[rows, cols] = (u[rows, cols] * sv).astype(BF16)

    return pl.pallas_call(
        body, name="gmlp_fwd", grid=(t // TM,),
        in_specs=[_row(TM, D), _const((1, D)), _wblk(D, lay["in"]), _const((1, GH)), _const((1, GH)),
                  _const((GG, GC, GC)), _const((GC, GH))],
        out_specs=[_row(TM, GH), _row(TM, 2 * GH)],
        out_shape=[_sds((t, GH), BF16), _sds((t, 2 * GH), BF16)],
        compiler_params=_cp("parallel"),
    )(h, g1, allw, lng, lnb, wm, bfull)


def loss_head(h, tgt):
    t = h.shape[0]

    def body(h_ref, t_ref, dh_ref, loss_ref):
        @pl.when(pl.program_id(0) == 0)
        def _():
            loss_ref[...] = jnp.zeros_like(loss_ref)

        e = h_ref[...] - t_ref[...]
        dh_ref[...] = e * (1.0 / D)
        part = jnp.sum(jnp.sum(e * e, axis=-1, keepdims=True), axis=0, keepdims=True) * (0.5 / D)
        loss_ref[...] += jnp.broadcast_to(part, loss_ref.shape)

    return pl.pallas_call(
        body, name="loss_head", grid=(t // TM,),
        in_specs=[_row(TM, D), _row(TM, D)],
        out_specs=[_row(TM, D), _const((8, LANES))],
        out_shape=[_sds((t, D), F32), _sds((8, LANES), F32)],
        compiler_params=_cp("arbitrary"),
    )(h, tgt)


def _zero_at_first_step(*refs):
    @pl.when(pl.program_id(0) == 0)
    def _():
        for r in refs:
            r[...] = jnp.zeros_like(r)


def ple_bwd(dh3, h2, p, g3, allw, lay, wp):
    t = h2.shape[0]

    def body(dh_ref, h_ref, p_ref, g_ref, wg_ref, wp_ref, dh2_ref, dh2b_ref, dgt_ref, dpp_ref, dg_ref):
        _zero_at_first_step(dg_ref)
        dh3v = dh_ref[...]
        x = h_ref[...]
        g = g_ref[...]
        wg = _rows_joined(wg_ref)
        yn, xhat, r = _rms(x, g, D)
        gt = _dot(yn.astype(BF16), wg)
        pp = _dot(p_ref[...].astype(BF16), wp_ref[...])
        sg = _sigmoid(gt)
        dgt = (dh3v * pp * sg * (1.0 - sg)).astype(BF16)
        dgt_ref[...] = dgt
        dpp_ref[...] = (dh3v * sg).astype(BF16)
        dhn = _dot_nt(dgt, wg)
        _acc_rows(dg_ref, dhn * xhat)
        dh2 = dh3v + _rms_bwd(dhn, g, xhat, r, D)
        dh2_ref[...] = dh2
        dh2b_ref[...] = dh2.astype(BF16)

    return pl.pallas_call(
        body, name="ple_bwd", grid=(t // TM,),
        in_specs=[_row(TM, D), _row(TM, D), _row(TM, PLE), _const((1, D)), _wblk(D // N_CHIPS, lay["gate"]),
                  _const((PLE, D))],
        out_specs=[_row(TM, D), _row(TM, D), _row(TM, D), _row(TM, D), _const((8, D))],
        out_shape=[_sds((t, D), F32), _sds((t, D), BF16), _sds((t, D), BF16), _sds((t, D), BF16), _sds((8, D), F32)],
        compiler_params=_cp("arbitrary"),
    )(dh3, h2, p, g3, allw, wp)


def ffn_bwd(dh2, dh2b, h1, r, g2, allw, lay):
    t = h1.shape[0]

    def body(dh_ref, dhb_ref, h_ref, r_ref, g_ref, wu_ref, wd_ref, dh1_ref, dh1b_ref, du_ref, a_ref, dg_ref):
        _zero_at_first_step(dg_ref)
        dhb = dhb_ref[...]
        g = g_ref[...]
        _, xhat, rr = _rms(h_ref[...], g, D)
        dhn = jnp.zeros((TM, D), F32)
        for c in range(N_CHIPS):
            cs = slice(c * D, (c + 1) * D)
            rc = r_ref[:, cs].astype(F32)
            a_ref[:, cs] = (rc * rc).astype(BF16)
            da = _dot_nt(dhb, wd_ref[c])
            du = (da * (2.0 * rc)).astype(BF16)
            du_ref[:, cs] = du
            dhn = dhn + _dot_nt(du, wu_ref[c])
        _acc_rows(dg_ref, dhn * xhat)
        dh1 = dh_ref[...] + _rms_bwd(dhn, g, xhat, rr, D)
        dh1_ref[...] = dh1
        dh1b_ref[...] = dh1.astype(BF16)

    return pl.pallas_call(
        body, name="ffn_bwd", grid=(t // TM,),
        in_specs=[_row(TM, D), _row(TM, D), _row(TM, D), _row(TM, DFF), _const((1, D)), _wblk(D, lay["up"]),
                  _wblk(D, lay["down"])],
        out_specs=[_row(TM, D), _row(TM, D), _row(TM, DFF), _row(TM, DFF), _const((8, D))],
        out_shape=[_sds((t, D), F32), _sds((t, D), BF16), _sds((t, DFF), BF16), _sds((t, DFF), BF16),
                   _sds((8, D), F32)],
        compiler_params=_cp("arbitrary"),
    )(dh2, dh2b, h1, r, g2, allw, allw)


def linear_nt(a, allw, rows, row0):
    t = a.shape[0]
    k = N_CHIPS * rows

    def body(a_ref, w_ref, o_ref):
        o_ref[...] = _dot_nt(a_ref[...], _rows_joined(w_ref)).astype(BF16)

    return pl.pallas_call(
        body, name="linear_nt", grid=(t // TM,),
        in_specs=[_row(TM, D), _wblk(rows, row0)],
        out_specs=_row(TM, k),
        out_shape=_sds((t, k), BF16),
        compiler_params=_cp("parallel"),
    )(a, allw)


def flash_bwd(q, k, v, o, do, lse, seq):
    t = q.shape[1]
    nb = t // seq
    nq = seq // TQ

    def body(q_ref, k_ref, v_ref, o_ref, do_ref, lse_ref, dq_ref, dk_ref, dv_ref):
        kj = pl.program_id(2)

        @pl.when(kj == 0)
        def _():
            dq_ref[...] = jnp.zeros_like(dq_ref)

        kv = k_ref[0]
        vv = v_ref[0]

        def step(i, carry, diagonal=False):
            dk, dv = carry
            rows = pl.ds(pl.multiple_of(i * TQ, TQ), TQ)
            qv = q_ref[0, rows, :]
            dov = do_ref[rows, :]
            delta = jnp.sum(dov.astype(F32) * o_ref[rows, :].astype(F32), axis=-1, keepdims=True)
            s = _dot_nt(qv, kv)
            if diagonal:
                s = jnp.where(_diagonal_mask(), s, -1e30)
            p = jnp.exp(s - lse_ref[0, rows, :])
            dp = _dot_nt(dov, vv)
            ds = (p * (dp - delta)).astype(BF16)
            dv = dv + _dot_tn(p.astype(BF16), dov)
            dk = dk + _dot_tn(ds, qv)
            dq_ref[0, rows, :] += _dot(ds, kv)
            return dk, dv

        init = (jnp.zeros((TQ, 2 * DN), F32), jnp.zeros((TQ, DN), F32))
        dk, dv = lax.fori_loop(kj + 1, nq, step, step(kj, init, diagonal=True))
        dk_ref[0] = dk
        dv_ref[0] = dv

    return pl.pallas_call(
        body, name="flash_bwd", grid=(nb, HEADS, nq),
        in_specs=[pl.BlockSpec((1, seq, 2 * DN), lambda b, h, j: (h, b, 0)),
                  pl.BlockSpec((1, TQ, 2 * DN), lambda b, h, j: (h, b * nq + j, 0)),
                  pl.BlockSpec((1, TQ, DN), lambda b, h, j: (h, b * nq + j, 0)),
                  pl.BlockSpec((seq, DN), lambda b, h, j: (b, h)),
                  pl.BlockSpec((seq, DN), lambda b, h, j: (b, h)),
                  pl.BlockSpec((1, seq, 1), lambda b, h, j: (h, b, 0))],
        out_specs=[pl.BlockSpec((1, seq, 2 * DN), lambda b, h, j: (h, b, 0)),
                   pl.BlockSpec((1, TQ, 2 * DN), lambda b, h, j: (h, b * nq + j, 0)),
                   pl.BlockSpec((1, TQ, DN), lambda b, h, j: (h, b * nq + j, 0))],
        out_shape=[_sds((HEADS, t, 2 * DN), F32), _sds((HEADS, t, 2 * DN), F32), _sds((HEADS, t, DN), F32)],
        compiler_params=_cp("parallel", "parallel", "arbitrary"),
    )(q, k, v, o, do, lse)


def mla_pre_bwd(dq, dk, dv, dh1, h, g1, wdn, gq, gkv, wuq, wukv, gqn, gqr, gkn, gkr, cos, sin):
    t = h.shape[0]

    def body(dq_ref, dk_ref, dv_ref, dh1_ref, h_ref, g1_ref, wdn_ref, gq_ref, gkv_ref, wuq_ref, wukv_ref,
             gqn_ref, gqr_ref, gkn_ref, gkr_ref, c_ref, s_ref,
             dh_ref, hn_ref, cq_ref, ckv_ref, dqp_ref, dkvp_ref, dlat_ref,
             dg1_ref, dgq_ref, dgkv_ref, dgqn_ref, dgqr_ref, dgkn_ref, dgkr_ref):
        _zero_at_first_step(dg1_ref, dgq_ref, dgkv_ref, dgqn_ref, dgqr_ref, dgkn_ref, dgkr_ref)
        m = _mla_project(h_ref, g1_ref, wdn_ref, gq_ref, gkv_ref, wuq_ref, wukv_ref)
        hn_ref[...] = m["hn"]
        cq_ref[...] = m["cqb"]
        ckv_ref[...] = m["ckvb"]
        c = c_ref[...]
        s = s_ref[...]
        gqn = gqn_ref[...]
        gqr = gqr_ref[...]
        gkn = gkn_ref[...]
        gkr = gkr_ref[...]

        dkr = dk_ref[0, :, DN:2 * DN]
        for hd in range(1, HEADS):
            dkr = dkr + dk_ref[hd, :, DN:2 * DN]
        dkr = _rope_t(dkr, c, s)
        _, krhat, rkr = _rms(m["kr_raw"], gkr, DR)
        _acc_rows(dgkr_ref, dkr * krhat)
        dkr_raw = _rms_bwd(dkr, gkr, krhat, rkr, DR)

        for hd in range(HEADS):
            ncols = slice(hd * DN, (hd + 1) * DN)
            _, xh, r = _rms(m["qp"][:, ncols], gqn, DN)
            dqn = dq_ref[hd, :, 0:DN] * SM_SCALE
            _acc_rows(dgqn_ref, dqn * xh)
            dqp_ref[:, ncols] = _rms_bwd(dqn, gqn, xh, r, DN).astype(BF16)

            rcols = slice(D + hd * LANES, D + (hd + 1) * LANES)
            _, xh, r = _rms(m["qp"][:, rcols], gqr, DR)
            dqr = _rope_t(dq_ref[hd, :, DN:2 * DN] * SM_SCALE, c, s)
            _acc_rows(dgqr_ref, dqr * xh)
            dqp_ref[:, rcols] = _rms_bwd(dqr, gqr, xh, r, DR).astype(BF16)

            kcols = slice(hd * 2 * DN, hd * 2 * DN + DN)
            _, xh, r = _rms(m["kvp"][:, kcols], gkn, DN)
            dkn = dk_ref[hd, :, 0:DN]
            _acc_rows(dgkn_ref, dkn * xh)
            dkvp_ref[:, kcols] = _rms_bwd(dkn, gkn, xh, r, DN).astype(BF16)
            dkvp_ref[:, hd * 2 * DN + DN:(hd + 1) * 2 * DN] = dv_ref[hd].astype(BF16)

        dcq = _dot_nt(dqp_ref[...], wuq_ref[...])
        _acc_rows(dgq_ref, dcq * m["cqhat"])
        dlat_q = _rms_bwd(dcq, gq_ref[...], m["cqhat"], m["rq"], QL)
        dckv = _dot_nt(dkvp_ref[...], wukv_ref[...])
        _acc_rows(dgkv_ref, dckv * m["ckvhat"])
        dlat_kv = _rms_bwd(dckv, gkv_ref[...], m["ckvhat"], m["rkv"], KVL)
        dlat = jnp.concatenate([dlat_q, dlat_kv, dkr_raw], axis=1).astype(BF16)
        dlat_ref[...] = dlat
        dhn = _dot_nt(dlat, wdn_ref[...])
        _acc_rows(dg1_ref, dhn * m["xhat"])
        dh_ref[...] = dh1_ref[...] + _rms_bwd(dhn, g1_ref[...], m["xhat"], m["rx"], D)

    hb = lambda w: pl.BlockSpec((HEADS, TM, w), lambda i: (0, i, 0))
    return pl.pallas_call(
        body, name="mla_pre_bwd", grid=(t // TM,),
        in_specs=[hb(2 * DN), hb(2 * DN), hb(DN), _row(TM, D), _row(TM, D), _const((1, D)), _const((D, LATP)),
                  _const((1, QL)), _const((1, KVL)), _const((QL, 2 * D)), _const((KVL, 2 * D)),
                  _const((1, LANES)), _const((1, LANES)), _const((1, LANES)), _const((1, LANES)),
                  _row(TM, LANES), _row(TM, LANES)],
        out_specs=[_row(TM, D), _row(TM, D), _row(TM, QL), _row(TM, KVL), _row(TM, 2 * D), _row(TM, 2 * D),
                   _row(TM, LATP), _const((8, D)), _const((8, QL)), _const((8, KVL)), _const((8, LANES)),
                   _const((8, LANES)), _const((8, LANES)), _const((8, LANES))],
        out_shape=[_sds((t, D), F32), _sds((t, D), BF16), _sds((t, QL), BF16), _sds((t, KVL), BF16),
                   _sds((t, 2 * D), BF16), _sds((t, 2 * D), BF16), _sds((t, LATP), BF16),
                   _sds((8, D), F32), _sds((8, QL), F32), _sds((8, KVL), F32), _sds((8, LANES), F32),
                   _sds((8, LANES), F32), _sds((8, LANES), F32), _sds((8, LANES), F32)],
        compiler_params=_cp("arbitrary"),
    )(dq, dk, dv, dh1, h, g1, wdn, gq, gkv, wuq, wukv, gqn, gqr, gkn, gkr, cos, sin)


def gmlp_bwd(dh1, dh1b, h, pre, g1, allw, lay, lng, lnb, wm, wmt, bfull, tril):
    t = h.shape[0]

    def body(dh1_ref, dh1b_ref, h_ref, pre_ref, g1_ref, win_ref, lng_ref, lnb_ref, wm_ref, wmt_ref, b_ref,
             wout_ref, tril_ref, dh_ref, hn_ref, dpre_ref, dws_ref, dbs_ref, dlng_ref, dlnb_ref, dg1_ref,
             dvn_s):
        _zero_at_first_step(dws_ref, dbs_ref, dlng_ref, dlnb_ref, dg1_ref)
        g1 = g1_ref[...]
        yn, xhat, rx = _rms(h_ref[...], g1, D)
        hn_ref[...] = yn.astype(BF16)
        dy = _dot_nt(dh1b_ref[...], _rows_joined(wout_ref))
        pre_u = pre_ref[:, :GH].astype(F32)
        pre_v = pre_ref[:, GH:].astype(F32)
        u = _gelu(pre_u)
        v = _gelu(pre_v)
        xc = v - jnp.mean(v, axis=-1, keepdims=True)
        rs = lax.rsqrt(jnp.mean(xc * xc, axis=-1, keepdims=True) + EPS)
        vhat = xc * rs
        lng = lng_ref[...]
        vnb = (vhat * lng + lnb_ref[...]).astype(BF16)
        dsv = dy * u
        dsvb = dsv.astype(BF16)
        tril_m = tril_ref[...]
        gg_u = _gelu_grad(pre_u)
        for ch in range(TM // GC):
            rows = slice(ch * GC, (ch + 1) * GC)
            dbs_ref[...] += dsv[rows, :]
            for g in range(GG):
                cols = slice(g * GD, (g + 1) * GD)
                sv = _dot(wm_ref[g], vnb[rows, cols]) + b_ref[:, cols]
                dpre_ref[rows, cols] = (dy[rows, cols] * sv * gg_u[rows, cols]).astype(BF16)
                dvn_s[rows, cols] = _dot(wmt_ref[g], dsvb[rows, cols])
                dws_ref[g] += _dot_nt(dsvb[rows, cols], vnb[rows, cols]) * tril_m
        dvn = dvn_s[...]
        _acc_rows(dlng_ref, dvn * vhat)
        _acc_rows(dlnb_ref, dvn)
        dvhat = dvn * lng
        dv = rs * (dvhat - jnp.mean(dvhat, axis=-1, keepdims=True)
                   - vhat * jnp.mean(dvhat * vhat, axis=-1, keepdims=True))
        dpre_v = (dv * _gelu_grad(pre_v)).astype(BF16)
        dpre_ref[:, GH:] = dpre_v
        dhn = _dot_nt(dpre_ref[:, 0:D], win_ref[0])
        for c in range(1, N_CHIPS):
            dhn = dhn + _dot_nt(dpre_ref[:, c * D:(c + 1) * D], win_ref[c])
        _acc_rows(dg1_ref, dhn * xhat)
        dh_ref[...] = dh1_ref[...] + _rms_bwd(dhn, g1, xhat, rx, D)

    return pl.pallas_call(
        body, name="gmlp_bwd", grid=(t // TM,),
        in_specs=[_row(TM, D), _row(TM, D), _row(TM, D), _row(TM, 2 * GH), _const((1, D)), _wblk(D, lay["in"]),
                  _const((1, GH)), _const((1, GH)), _const((GG, GC, GC)), _const((GG, GC, GC)), _const((GC, GH)),
                  _wblk(GH // N_CHIPS, lay["out"]), _const((GC, GC))],
        out_specs=[_row(TM, D), _row(TM, D), _row(TM, 2 * GH), _const((GG, GC, GC)), _const((GC, GH)),
                   _const((8, GH)), _const((8, GH)), _const((8, D))],
        out_shape=[_sds((t, D), F32), _sds((t, D), BF16), _sds((t, 2 * GH), BF16), _sds((GG, GC, GC), F32),
                   _sds((GC, GH), F32), _sds((8, GH), F32), _sds((8, GH), F32), _sds((8, D), F32)],
        scratch_shapes=[pltpu.VMEM((TM, GH), F32)],
        compiler_params=_cp("arbitrary"),
    )(dh1, dh1b, h, pre, g1, allw, lng, lnb, wm, wmt, bfull, allw, tril)


def _token_step(t):
    return 1024 if t % 1024 == 0 else 512


def mm_tn(a, b):
    t, k = a.shape
    n = b.shape[1]
    tk = min(k, 1024)
    tn = min(n, 1024)
    tt = _token_step(t)

    def body(a_ref, b_ref, o_ref):
        @pl.when(pl.program_id(2) == 0)
        def _():
            o_ref[...] = jnp.zeros_like(o_ref)

        o_ref[...] += _dot_tn(a_ref[...].astype(BF16), b_ref[...].astype(BF16))

    return pl.pallas_call(
        body, name="mm_tn", grid=(k // tk, n // tn, t // tt),
        in_specs=[pl.BlockSpec((tt, tk), lambda i, j, s: (s, i)), pl.BlockSpec((tt, tn), lambda i, j, s: (s, j))],
        out_specs=pl.BlockSpec((tk, tn), lambda i, j, s: (i, j)), out_shape=_sds((k, n), F32),
        compiler_params=_cp("parallel", "parallel", "arbitrary"),
    )(a, b)


def mm_tn_into(buf, a, b, rows, row0, col_sharded):
    t = a.shape[0]
    tt = _token_step(t)
    assert row0 % rows == 0 and a.shape[1] == (rows if col_sharded else N_CHIPS * rows), (rows, row0, a.shape)
    assert b.shape[1] == (N_CHIPS * D if col_sharded else D), b.shape
    grid = (1, N_CHIPS, t // tt) if col_sharded else (N_CHIPS, 1, t // tt)

    def body(buf_ref, a_ref, b_ref, o_ref):
        del buf_ref

        @pl.when(pl.program_id(2) == 0)
        def _():
            o_ref[...] = jnp.zeros_like(o_ref)

        o_ref[...] += _dot_tn(a_ref[...].astype(BF16), b_ref[...].astype(BF16))

    return pl.pallas_call(
        body, name="mm_tn_into", grid=grid,
        in_specs=[_ANY, pl.BlockSpec((tt, rows), lambda i, j, s: (s, i)), pl.BlockSpec((tt, D), lambda i, j, s: (s, j))],
        out_specs=pl.BlockSpec((None, rows, D), lambda i, j, s: (i + j, row0 // rows, 0)),
        out_shape=_sds(buf.shape, F32), input_output_aliases={0: 0},
        compiler_params=_cp("parallel", "parallel", "arbitrary"),
    )(buf, a, b)


def adamw(w, g, m, v):
    rows, cols = w.shape
    tr = rows if rows <= 512 else next(r for r in (512, 384, 256, 128) if rows % r == 0)
    c1 = 1.0 - ADAM_B1 ** ADAM_STEP
    c2 = 1.0 - ADAM_B2 ** ADAM_STEP

    def body(w_ref, g_ref, m_ref, v_ref, d_ref, mo_ref, vo_ref):
        gv = g_ref[...]
        mn = ADAM_B1 * m_ref[...] + (1.0 - ADAM_B1) * gv
        vn = ADAM_B2 * v_ref[...] + (1.0 - ADAM_B2) * (gv * gv)
        mo_ref[...] = mn
        vo_ref[...] = vn
        d_ref[...] = -ADAM_LR * ((mn / c1) / (jnp.sqrt(vn / c2) + ADAM_EPS) + ADAM_WD * w_ref[...])

    spec = pl.BlockSpec((tr, cols), lambda i: (i, 0))
    return pl.pallas_call(
        body, name="adamw", grid=(rows // tr,),
        in_specs=[spec] * 4, out_specs=[spec] * 3, out_shape=[_sds((rows, cols), F32)] * 3,
        compiler_params=_cp("parallel"),
    )(w, g, m, v)


def _place():
    return lax.axis_index("x"), lax.axis_index("y"), lax.axis_index("c")


def _other_chips(x, y):
    return [(1 - x, y), (x, 1 - y), (1 - x, 1 - y)]


_ANY = pl.BlockSpec(memory_space=pl.ANY)


_HBM = pl.BlockSpec(memory_space=pltpu.HBM)
_SEM = pl.BlockSpec(memory_space=pltpu.SEMAPHORE)
_EFFECT = pltpu.SideEffectType.DATAFLOW_SIDE_EFFECTING
N_ICI = 3


def _exchange_start(name, src, land_shape, copies):
    def body(src_ref, land_ref, *outs):
        sems, token = outs[:2 * N_ICI], outs[-1]
        for j, (s, d, to) in enumerate(copies(src_ref, land_ref, _place())):
            pltpu.make_async_remote_copy(src_ref=s, dst_ref=d, send_sem=sems[j], recv_sem=sems[N_ICI + j],
                                         device_id=to, device_id_type=MESH).start()
        token[...] = jnp.zeros_like(token)

    sem = pltpu.SemaphoreType.DMA(())
    outs = pl.pallas_call(
        body, name=name,
        out_shape=(sem,) * (2 * N_ICI) + (pltpu.HBM(src.shape, src.dtype), pltpu.HBM(land_shape, src.dtype),
                                          _sds((8, LANES), F32)),
        in_specs=(_HBM, _HBM),
        out_specs=(_SEM,) * (2 * N_ICI) + (_HBM, _HBM, pl.BlockSpec(memory_space=pltpu.VMEM)),
        input_output_aliases={0: 2 * N_ICI, 1: 2 * N_ICI + 1},
        compiler_params=pltpu.CompilerParams(has_side_effects=_EFFECT),
    )(pltpu.with_memory_space_constraint(src, pltpu.HBM),
      pltpu.with_memory_space_constraint(lax.empty(land_shape, src.dtype), pltpu.HBM))
    return outs[:2 * N_ICI], outs[2 * N_ICI], outs[2 * N_ICI + 1], outs[-1]


def _exchange_wait(name, sems, src, land, after, arrivals):
    def body(src_ref, land_ref, *rest):
        sems = rest[:2 * N_ICI]
        for j, (s, d) in enumerate(arrivals(src_ref, land_ref, _place())):
            cp = pltpu.make_async_remote_copy(src_ref=s, dst_ref=d, send_sem=sems[j], recv_sem=sems[N_ICI + j],
                                              device_id=_place(), device_id_type=MESH)
            cp.wait_send()
            cp.wait_recv()

    return pl.pallas_call(
        body, name=name, out_shape=(pltpu.HBM(src.shape, src.dtype), pltpu.HBM(land.shape, land.dtype)),
        in_specs=(_HBM, _HBM) + (_SEM,) * (2 * N_ICI) + (_ANY,), out_specs=(_HBM, _HBM),
        input_output_aliases={0: 0, 1: 1},
        compiler_params=pltpu.CompilerParams(has_side_effects=_EFFECT),
    )(src, land, *sems, after)


def _gather_views(hh):
    def copies(x_ref, land_ref, place):
        x, y, c = place
        half = pl.ds(pl.multiple_of(c * hh, 16), hh)
        return [(x_ref.at[half], land_ref.at[2 * x + y, half], (cx, cy, c)) for cx, cy in _other_chips(x, y)]

    def arrivals(x_ref, land_ref, place):
        x, y, c = place
        half = pl.ds(pl.multiple_of(c * hh, 16), hh)
        return [(x_ref.at[half], land_ref.at[2 * cx + cy, half]) for cx, cy in _other_chips(x, y)]

    return copies, arrivals


def gather_start(mine, tag):
    rr, cc = mine.shape
    assert rr % 32 == 0, rr
    return _exchange_start(f"gather_start_{tag}", mine, (N_CHIPS, rr, cc), _gather_views(rr // 2)[0])


def gather_wait(sems, mine, land, after, tag):
    return _exchange_wait(f"gather_wait_{tag}", sems, mine, land, after, _gather_views(mine.shape[0] // 2)[1])


def pass_to_sibling(land):
    _, rr, cc = land.shape
    hh = rr // 2

    def body(land_ref, o_ref, send_sems, recv_sems):
        del o_ref
        x, y, c = _place()
        half = pl.ds(pl.multiple_of(c * hh, 16), hh)
        copies = [pltpu.make_async_remote_copy(src_ref=land_ref.at[2 * cx + cy, half],
                                               dst_ref=land_ref.at[2 * cx + cy, half],
                                               send_sem=send_sems.at[j], recv_sem=recv_sems.at[j],
                                               device_id=(x, y, 1 - c), device_id_type=MESH)
                  for j, (cx, cy) in enumerate(_other_chips(x, y))]
        for cp in copies:
            cp.start()
        for cp in copies:
            cp.wait()

    return pl.pallas_call(
        body, name="pass_to_sibling", in_specs=[_ANY], out_specs=_ANY, out_shape=_sds(land.shape, land.dtype),
        input_output_aliases={0: 0},
        scratch_shapes=[pltpu.SemaphoreType.DMA((N_ICI,)), pltpu.SemaphoreType.DMA((N_ICI,))],
    )(land)


def swap_halves(g):
    _, rr, cc = g.shape
    hh = rr // 2

    def body(g_ref, o_ref, send_sems, recv_sems):
        x, y, c = _place()
        other_half = pl.ds(pl.multiple_of((1 - c) * hh, 16), hh)
        copies = [pltpu.make_async_remote_copy(src_ref=g_ref.at[k, other_half], dst_ref=o_ref.at[k],
                                               send_sem=send_sems.at[k], recv_sem=recv_sems.at[k],
                                               device_id=(x, y, 1 - c), device_id_type=MESH)
                  for k in range(N_CHIPS)]
        for cp in copies:
            cp.start()
        for cp in copies:
            cp.wait()

    return pl.pallas_call(
        body, name="swap_halves", in_specs=[_ANY], out_specs=_ANY,
        out_shape=_sds((N_CHIPS, hh, cc), g.dtype),
        scratch_shapes=[pltpu.SemaphoreType.DMA((N_CHIPS,)), pltpu.SemaphoreType.DMA((N_CHIPS,))],
    )(g)


def chip_sum(place, g32, got):
    _, rr, cc = g32.shape
    hh = rr // 2
    tr = SUM_ROWS
    assert rr % 2 == 0 and hh % tr == 0, (rr, tr)
    nb = hh // tr

    def body(place_ref, g_ref, got_ref, own_ref, all_ref):
        s = g_ref[...] + got_ref[...].astype(F32)
        all_ref[...] = s.astype(BF16)
        own_ref[...] = g_ref[place_ref[1]] + got_ref[place_ref[1]].astype(F32)

    return pl.pallas_call(
        body, name="chip_sum",
        grid_spec=pltpu.PrefetchScalarGridSpec(
            num_scalar_prefetch=1, grid=(nb,),
            in_specs=[pl.BlockSpec((N_CHIPS, tr, cc), lambda i, pr: (0, pr[0] * nb + i, 0)),
                      pl.BlockSpec((N_CHIPS, tr, cc), lambda i, pr: (0, i, 0))],
            out_specs=[pl.BlockSpec((tr, cc), lambda i, pr: (i, 0)),
                       pl.BlockSpec((N_CHIPS, tr, cc), lambda i, pr: (0, i, 0))]),
        out_shape=[_sds((hh, cc), F32), _sds((N_CHIPS, hh, cc), BF16)],
        compiler_params=_cp("parallel"),
    )(place, g32, got)


def _scatter_copies(s_ref, land_ref, place):
    x, y, c = place
    return [(s_ref.at[2 * cx + cy], land_ref.at[j], (cx, cy, c)) for j, (cx, cy) in enumerate(_other_chips(x, y))]


def scatter_start(s, tag):
    return _exchange_start(f"scatter_start_{tag}", s, (N_ICI,) + s.shape[1:], _scatter_copies)


def scatter_wait(sems, s, land, after, tag):
    return _exchange_wait(f"scatter_wait_{tag}", sems, s, land, after,
                          lambda s_ref, land_ref, place: [(a, b) for a, b, _ in _scatter_copies(s_ref, land_ref, place)])


def final_sum(own, got):
    hh, cc = own.shape
    tr = SUM_ROWS
    assert hh % tr == 0, (hh, tr)

    def body(own_ref, got_ref, o_ref):
        o_ref[...] = ((own_ref[...] + got_ref[0].astype(F32)) + got_ref[1].astype(F32)) + got_ref[2].astype(F32)

    return pl.pallas_call(
        body, name="final_sum", grid=(hh // tr,),
        in_specs=[pl.BlockSpec((tr, cc), lambda i: (i, 0)), pl.BlockSpec((3, tr, cc), lambda i: (0, i, 0))],
        out_specs=pl.BlockSpec((tr, cc), lambda i: (i, 0)),
        out_shape=_sds((hh, cc), F32),
        compiler_params=_cp("parallel"),
    )(own, got)


def share_with_sibling(f):
    hh, cc = f.shape

    def body(f_ref, o_ref, send_sem, recv_sem):
        x, y, c = _place()
        mine_half = pl.ds(pl.multiple_of(c * hh, 8), hh)
        cp = pltpu.make_async_remote_copy(src_ref=f_ref, dst_ref=o_ref.at[mine_half], send_sem=send_sem,
                                          recv_sem=recv_sem, device_id=(x, y, 1 - c), device_id_type=MESH)
        cp.start()
        cp.wait()

    return pl.pallas_call(
        body, name="share_with_sibling", in_specs=[_ANY], out_specs=_ANY,
        out_shape=_sds((2 * hh, cc), f.dtype),
        scratch_shapes=[pltpu.SemaphoreType.DMA, pltpu.SemaphoreType.DMA],
    )(f)


def allreduce_small(part):
    rr, cc = part.shape
    n_dev = 8

    def body(x_ref, all_ref, sum_ref, send_sems, recv_sems, local_sem):
        x, y, c = _place()
        me, sibling = (x, y, c), (x, y, 1 - c)
        chips = _other_chips(x, y)

        def rows(px, py, pc):
            return all_ref.at[pl.ds(pl.multiple_of((4 * px + 2 * py + pc) * rr, 8), rr), :]

        def copy(j, block, to, src=None):
            return pltpu.make_async_remote_copy(src_ref=rows(*block) if src is None else src, dst_ref=rows(*block),
                                                send_sem=send_sems.at[j], recv_sem=recv_sems.at[j],
                                                device_id=to, device_id_type=MESH)

        mine = pltpu.make_async_copy(x_ref, rows(*me), local_sem)
        mine.start()
        first = [copy(0, me, sibling, src=x_ref)]
        first += [copy(1 + j, me, (*chip, c), src=x_ref) for j, chip in enumerate(chips)]
        for cp in first:
            cp.start()
        passed = [copy(4 + j, (*chip, c), sibling) for j, chip in enumerate(chips)]
        for j, chip in enumerate(chips):
            copy(1 + j, (*chip, c), me).wait_recv()
            passed[j].start()
        copy(0, sibling, me).wait_recv()
        for j, chip in enumerate(chips):
            copy(4 + j, (*chip, 1 - c), me).wait_recv()
        for cp in first + passed:
            cp.wait_send()
        mine.wait()
        acc = all_ref[0:rr, :]
        for d in range(1, n_dev):
            acc = acc + all_ref[d * rr:(d + 1) * rr, :]
        sum_ref[...] = acc

    vm = pl.BlockSpec(memory_space=pltpu.VMEM)
    return pl.pallas_call(
        body, name="allreduce_small", in_specs=[vm], out_specs=[vm, vm],
        out_shape=[_sds((n_dev * rr, cc), F32), _sds((rr, cc), F32)],
        scratch_shapes=[pltpu.SemaphoreType.DMA((7,)), pltpu.SemaphoreType.DMA((7,)), pltpu.SemaphoreType.DMA],
        compiler_params=pltpu.CompilerParams(vmem_limit_bytes=VMEM_LIMIT),
    )(part)[1]


_BIG = ["mla_w_down", "mla_w_uq", "mla_w_ukv", "mla_w_out", "gmlp_w_in", "gmlp_w_out", "ffn_w_up", "ffn_w_down",
        "ple_w_gate", "ple_w_proj"]
_SMALL = ["norm_mix", "norm_ffn", "norm_ple", "mla_q_lora_g", "mla_kv_lora_g", "mla_q_nope_g", "mla_q_rope_g",
          "mla_k_nope_g", "mla_k_rope_g", "gmlp_ln_g", "gmlp_ln_b", "gmlp_w_s", "gmlp_b_s"]

_LAY_MLA = dict(up=0, down=1024, out=2048, gate=2304, wdn=2560, wuq=2736, wukv=2880, proj=3008, rows=3072)
_LAY_GMLP = {"up": 0, "down": 1024, "in": 2048, "out": 3072, "gate": 3584, "proj": 3840, "ln": 3904, "rows": 4096}


def _layer_parts(i):
    j = i // 2
    if i % 2 == 0:
        lay = _LAY_MLA
        return lay, [("ffn_w_up", i, lay["up"]), ("ffn_w_down", i, lay["down"]), ("mla_w_out", j, lay["out"]),
                     ("ple_w_gate", i, lay["gate"]), ("mla_w_down", j, lay["wdn"]), ("mla_w_uq", j, lay["wuq"]),
                     ("mla_w_ukv", j, lay["wukv"]), ("ple_w_proj", i, lay["proj"])]
    lay = _LAY_GMLP
    return lay, [("ffn_w_up", i, lay["up"]), ("ffn_w_down", i, lay["down"]), ("gmlp_w_in", j, lay["in"]),
                 ("gmlp_w_out", j, lay["out"]), ("ple_w_gate", i, lay["gate"]), ("ple_w_proj", i, lay["proj"])]


def _pack_rows(parts, dtype, pad_to=None):
    flat = jnp.concatenate([p.reshape(-1).astype(dtype) for p in parts])
    if pad_to is not None:
        flat = jnp.pad(flat, (0, pad_to * D - flat.size))
    return flat.reshape(-1, D)


def _odd(allw, row0, a, b):
    return allw[:, row0:row0 + a * b // D].reshape(N_CHIPS, a, b)


def _cols_joined(s):
    return jnp.transpose(s, (1, 0, 2)).reshape(s.shape[1], N_CHIPS * s.shape[2])


def _col_shards(full):
    a, bb = full.shape
    return jnp.transpose(full.reshape(a, N_CHIPS, bb // N_CHIPS), (1, 0, 2)).reshape(N_CHIPS, -1, D)


def _pad_lanes(g):
    return jnp.pad(g, ((0, 0), (0, LANES - g.shape[1])))


def _split_uq(wuq):
    l = wuq.shape[0]
    w = wuq.reshape(l, QL, HEADS, DN + DR)
    nope = w[..., :DN].reshape(l, QL, HEADS * DN)
    rope = jnp.pad(w[..., DN:], ((0, 0), (0, 0), (0, 0), (0, LANES - DR))).reshape(l, QL, HEADS * LANES)
    return jnp.concatenate([nope, rope], axis=-1)


def _merge_uq(d):
    nope = d[:, :HEADS * DN].reshape(QL, HEADS, DN)
    rope = d[:, HEADS * DN:].reshape(QL, HEADS, LANES)[..., :DR]
    return jnp.concatenate([nope, rope], axis=-1).reshape(QL, HEADS * (DN + DR))


def _rope_tables(positions):
    inv_freq = ROPE_BASE ** (-(jnp.arange(0, DR, 2, dtype=F32) / DR))
    ang = positions.reshape(-1).astype(F32)[:, None] * inv_freq
    z = jnp.zeros((ang.shape[0], LANES - DR), F32)
    return (jnp.concatenate([jnp.cos(ang), jnp.cos(ang), z], axis=1),
            jnp.concatenate([jnp.sin(ang), jnp.sin(ang), z], axis=1))


def kernel(x, p, positions, norm_mix, norm_ffn, norm_ple, mla_w_down, mla_q_lora_g, mla_kv_lora_g, mla_w_uq, mla_w_ukv, mla_q_nope_g, mla_q_rope_g, mla_k_nope_g, mla_k_rope_g, mla_w_out, gmlp_w_in, gmlp_ln_g, gmlp_ln_b, gmlp_w_s, gmlp_b_s, gmlp_w_out, ffn_w_up, ffn_w_down, ple_w_gate, ple_w_proj, loss_target, m_norm_mix, m_norm_ffn, m_norm_ple, m_mla_w_down, m_mla_q_lora_g, m_mla_kv_lora_g, m_mla_w_uq, m_mla_w_ukv, m_mla_q_nope_g, m_mla_q_rope_g, m_mla_k_nope_g, m_mla_k_rope_g, m_mla_w_out, m_gmlp_w_in, m_gmlp_ln_g, m_gmlp_ln_b, m_gmlp_w_s, m_gmlp_b_s, m_gmlp_w_out, m_ffn_w_up, m_ffn_w_down, m_ple_w_gate, m_ple_w_proj, v_norm_mix, v_norm_ffn, v_norm_ple, v_mla_w_down, v_mla_q_lora_g, v_mla_kv_lora_g, v_mla_w_uq, v_mla_w_ukv, v_mla_q_nope_g, v_mla_q_rope_g, v_mla_k_nope_g, v_mla_k_rope_g, v_mla_w_out, v_gmlp_w_in, v_gmlp_ln_g, v_gmlp_ln_b, v_gmlp_w_s, v_gmlp_b_s, v_gmlp_w_out, v_ffn_w_up, v_ffn_w_down, v_ple_w_gate, v_ple_w_proj):
    args = dict(locals())
    weights = {n: args[n] for n in _BIG + _SMALL}
    depth = norm_mix.shape[0]
    nb, seq, _ = x.shape
    t = nb * seq
    assert seq % TQ == 0 and seq % TM == 0 and t % 512 == 0, (nb, seq)
    cx = lax.axis_index("x")
    cy = lax.axis_index("y")
    cc = lax.axis_index("c")
    chip = 2 * cx + cy

    gathers = []
    token = None
    for i in range(depth):
        lay, parts = _layer_parts(i)
        rows = [weights[n][l] for n, l, _ in parts]
        if token is not None:
            rows[0] = rows[0] + token[0, 0]
        if i % 2 == 1:
            ln = jnp.stack([gmlp_ln_g[i // 2], gmlp_ln_b[i // 2]]).astype(F32)
            rows.append(lax.bitcast_convert_type(ln, BF16))
        mine = _pack_rows(rows, BF16, pad_to=lay["rows"])
        sems, mine, land, token = gather_start(mine, i)
        gathers.append((sems, mine, land))
    allw = [None] * depth

    tril = jnp.tril(jnp.ones((GC, GC), F32))
    wm = (gmlp_w_s * tril).astype(BF16)
    wmt = jnp.swapaxes(wm, -1, -2)
    bfull = jnp.repeat(jnp.swapaxes(gmlp_b_s, -1, -2), GD, axis=-1)
    cos, sin = _rope_tables(positions)
    row = lambda g: g.reshape(1, -1)
    gqr = _pad_lanes(mla_q_rope_g)
    gkr = _pad_lanes(mla_k_rope_g)

    h = x.reshape(t, D)
    pt = p.reshape(depth, t, PLE)
    saved = []
    for i in range(depth):
        j = i // 2
        lay, _ = _layer_parts(i)
        sems, mine, land = gathers[i]
        mine, land = gather_wait(sems, mine, land, token if i == 0 else h, i)
        aw = allw[i] = lax.dynamic_update_slice(pass_to_sibling(land), mine[None], (chip, 0, 0))
        s = dict(h=h)
        if i % 2 == 0:
            wdn = jnp.pad(_odd(aw, lay["wdn"], D // N_CHIPS, LAT).reshape(D, LAT), ((0, 0), (0, LATP - LAT)))
            wuq = _split_uq(_cols_joined(_odd(aw, lay["wuq"], QL, 384))[None])[0]
            wukv = _cols_joined(_odd(aw, lay["wukv"], KVL, 512))
            mla_args = (row(norm_mix[i]), wdn, row(mla_q_lora_g[j]), row(mla_kv_lora_g[j]), wuq, wukv,
                        row(mla_q_nope_g[j]), gqr[j:j + 1], row(mla_k_nope_g[j]), gkr[j:j + 1], cos, sin)
            q, k, v = mla_pre_fwd(h, *mla_args)
            y, lse = flash_fwd(q, k, v, seq)
            s.update(q=q, k=k, v=v, lse=lse, mla_args=mla_args)
        else:
            ln = lax.bitcast_convert_type(aw[:, lay["ln"]:lay["ln"] + 2].reshape(N_CHIPS, 2, GH // N_CHIPS, 2), F32)
            ln = jnp.transpose(ln, (1, 0, 2)).reshape(2, 1, GH)
            y, pre = gmlp_fwd(h, row(norm_mix[i]), aw, lay, ln[0], ln[1], wm[j], bfull[j])
            s.update(pre=pre, ln=ln)
        wp = _cols_joined(_odd(aw, lay["proj"], PLE, 256))
        h1, h2, hn2, r = mixffn_fwd(h, y, aw, lay, row(norm_ffn[i]))
        h, hn3 = ple_fwd(h2, pt[i], row(norm_ple[i]), aw, lay, wp)
        s.update(y=y, wp=wp, h1=h1, h2=h2, hn2=hn2, r=r, hn3=hn3)
        saved.append(s)

    dh, loss_part = loss_head(h, loss_target.reshape(t, D))
    loss = lax.psum(loss_part[0, 0], ("x", "y", "c"))

    gs = {n: [None] * weights[n].shape[0] for n in _SMALL}
    gw = {n: [None] * weights[n].shape[0] for n in _BIG}
    place = jnp.stack([cc, chip]).astype(jnp.int32)
    scatters = []
    token = None
    for i in reversed(range(depth)):
        j = i // 2
        lay, parts = _layer_parts(i)
        aw = allw[i]
        s = saved[i]

        def put(b, row0, shards):
            return lax.dynamic_update_slice(b, shards.reshape(N_CHIPS, -1, D), (0, row0, 0))

        buf = lax.empty((N_CHIPS, lay["rows"], D), F32)
        tail = lay.get("ln", lay["rows"])
        if tail < lay["rows"]:
            buf = put(buf, tail, jnp.zeros((N_CHIPS, lay["rows"] - tail, D), F32))
        g3 = row(norm_ple[i])
        if token is not None:
            g3 = g3 + token[0:1, 0:1]
        dh2, dh2b, dgt, dpp, dg3 = ple_bwd(dh, s["h2"], pt[i], g3, aw, lay, s["wp"])
        gs["norm_ple"][i] = dg3[0]
        buf = mm_tn_into(buf, s["hn3"], dgt, D // N_CHIPS, lay["gate"], False)
        buf = put(buf, lay["proj"], _col_shards(mm_tn(pt[i], dpp)))
        dh1, dh1b, du, a, dg2 = ffn_bwd(dh2, dh2b, s["h1"], s["r"], row(norm_ffn[i]), aw, lay)
        gs["norm_ffn"][i] = dg2[0]
        buf = mm_tn_into(buf, a, dh2b, D, lay["down"], False)
        buf = mm_tn_into(buf, s["hn2"], du, D, lay["up"], True)
        buf = mm_tn_into(buf, s["y"], dh1b, s["y"].shape[1] // N_CHIPS, lay["out"], False)
        if i % 2 == 0:
            do = linear_nt(dh1b, aw, D // N_CHIPS, lay["out"])
            dq, dk, dv = flash_bwd(s["q"], s["k"], s["v"], s["y"], do, s["lse"], seq)
            (dh, hn1, cq, ckv, dqp, dkvp, dlat, dg1, dgq, dgkv, dgqn, dgqr, dgkn, dgkr) = mla_pre_bwd(
                dq, dk, dv, dh1, s["h"], *s["mla_args"])
            gs["norm_mix"][i] = dg1[0]
            gs["mla_q_lora_g"][j] = dgq[0]
            gs["mla_kv_lora_g"][j] = dgkv[0]
            gs["mla_q_nope_g"][j] = dgqn[0]
            gs["mla_q_rope_g"][j] = dgqr[0, :DR]
            gs["mla_k_nope_g"][j] = dgkn[0]
            gs["mla_k_rope_g"][j] = dgkr[0, :DR]
            buf = put(buf, lay["wdn"], mm_tn(hn1, dlat)[:, :LAT])
            buf = put(buf, lay["wuq"], _col_shards(_merge_uq(mm_tn(cq, dqp))))
            buf = put(buf, lay["wukv"], _col_shards(mm_tn(ckv, dkvp)))
        else:
            dh, hn1, dpre, dws, dbs, dlng, dlnb, dg1 = gmlp_bwd(
                dh1, dh1b, s["h"], s["pre"], row(norm_mix[i]), aw, lay, s["ln"][0], s["ln"][1], wm[j], wmt[j],
                bfull[j], tril)
            gs["norm_mix"][i] = dg1[0]
            gs["gmlp_ln_g"][j] = dlng[0]
            gs["gmlp_ln_b"][j] = dlnb[0]
            gs["gmlp_w_s"][j] = dws
            gs["gmlp_b_s"][j] = jnp.sum(dbs.reshape(GC, GG, GD), axis=-1).T
            buf = mm_tn_into(buf, hn1, dpre, D, lay["in"], True)

        own, sums = chip_sum(place, buf, swap_halves(buf))
        sems, sums, land, token = scatter_start(sums, i)
        scatters.append((i, own, sems, sums, land))
    grad_x = dh.reshape(x.shape)

    after = dh
    for i, own, sems, sums, land in scatters:
        _, got = scatter_wait(sems, sums, land, after, i)
        half = final_sum(own, got)
        after = reduced = lax.dynamic_update_slice(share_with_sibling(half), half, (cc * half.shape[0], 0))
        for n, l, row0 in _layer_parts(i)[1]:
            gw[n][l] = reduced[row0:row0 + weights[n][l].size // D].reshape(weights[n].shape[1:])
    grads = {n: jnp.stack(gw[n]) for n in _BIG}

    small_sizes = [weights[n].size if n not in ("gmlp_ln_g", "gmlp_ln_b") else weights[n].shape[0] * GH
                   for n in _SMALL]
    small_rows = -(-sum(small_sizes) // (8 * D)) * 8
    part = _pack_rows([jnp.stack(gs[n]) for n in _SMALL], F32, pad_to=small_rows)
    tot = allreduce_small(part).reshape(-1)
    off = 0
    for n, sz in zip(_SMALL, small_sizes):
        gsum = tot[off:off + sz]
        off += sz
        if n in ("gmlp_ln_g", "gmlp_ln_b"):
            gsum = lax.dynamic_slice_in_dim(gsum.reshape(-1, GH), chip * (GH // N_CHIPS), GH // N_CHIPS, axis=1)
        grads[n] = gsum.reshape(weights[n].shape)

    delta, new_m, new_v = {}, {}, {}
    for n in _BIG:
        w2 = weights[n].reshape(-1, weights[n].shape[-1])
        d, mn, vn = adamw(w2, grads[n].reshape(w2.shape), args["m_" + n].reshape(w2.shape),
                          args["v_" + n].reshape(w2.shape))
        delta[n], new_m[n], new_v[n] = (a.reshape(weights[n].shape) for a in (d, mn, vn))
    own_sizes = [weights[n].size for n in _SMALL]
    own_rows = -(-sum(own_sizes) // (8 * D)) * 8
    packed = [_pack_rows([src[n] for n in _SMALL], F32, pad_to=own_rows)
              for src in (weights, grads, {n: args["m_" + n] for n in _SMALL}, {n: args["v_" + n] for n in _SMALL})]
    outs = adamw(*packed)
    off = 0
    for n, sz in zip(_SMALL, own_sizes):
        for dst, o in zip((delta, new_m, new_v), outs):
            dst[n] = o.reshape(-1)[off:off + sz].reshape(weights[n].shape)
        off += sz

    order = ["norm_mix", "norm_ffn", "norm_ple", "mla_w_down", "mla_q_lora_g", "mla_kv_lora_g", "mla_w_uq",
             "mla_w_ukv", "mla_q_nope_g", "mla_q_rope_g", "mla_k_nope_g", "mla_k_rope_g", "mla_w_out", "gmlp_w_in",
             "gmlp_ln_g", "gmlp_ln_b", "gmlp_w_s", "gmlp_b_s", "gmlp_w_out", "ffn_w_up", "ffn_w_down", "ple_w_gate",
             "ple_w_proj"]
    return (loss, grad_x, *[grads[n] for n in order], *[delta[n] for n in order], *[new_m[n] for n in order],
            *[new_v[n] for n in order])
```

```python
import functools

import jax
import jax.numpy as jnp
from jax import lax
from jax.experimental import pallas as pl
from jax.experimental.pallas import tpu as pltpu

F32 = jnp.float32
BF16 = jnp.bfloat16
MESH = pl.DeviceIdType.MESH

D = 1024
HEADS = 8
DN = 128
DR = 64
QL = 384
KVL = 256
LAT = 704
LATP = 768
DFF = 4096
GH = 2048
GC = 128
GG = 8
GD = 256
PLE = 256
EPS = 1e-6
ROPE_BASE = 10000.0
SM_SCALE = (DN + DR) ** -0.5
N_CHIPS = 4
LANES = 128

ADAM_LR = 0.001
ADAM_B1 = 0.9
ADAM_B2 = 0.999
ADAM_EPS = 1e-08
ADAM_WD = 0.01
ADAM_STEP = 10

TM = 256
TMB = 512
TQ = 512
TQ_FWD = 512
FWD_HEADS = 1
SUM_ROWS = 256
VMEM_LIMIT = 56 * 1024 * 1024


def _cp(*sem):
    return pltpu.CompilerParams(dimension_semantics=sem, vmem_limit_bytes=VMEM_LIMIT)


def _dot(a, b):
    return jnp.dot(a, b, preferred_element_type=F32)


def _dot_nt(a, b):
    return lax.dot_general(a, b, (((1,), (1,)), ((), ())), preferred_element_type=F32)


def _dot_tn(a, b):
    return lax.dot_general(a, b, (((0,), (0,)), ((), ())), preferred_element_type=F32)


def _rms(x, g, n):
    r = lax.rsqrt(jnp.sum(x * x, axis=-1, keepdims=True) * (1.0 / n) + EPS)
    xhat = x * r
    return xhat * g, xhat, r


def _rms_bwd(dy, g, xhat, r, n):
    dxhat = dy * g
    return r * (dxhat - xhat * (jnp.sum(dxhat * xhat, axis=-1, keepdims=True) * (1.0 / n)))


def _rope(x, c, s):
    return x * c + (pltpu.roll(x, 32, 1) - pltpu.roll(x, 96, 1)) * s


def _rope_t(dy, c, s):
    w = dy * s
    return dy * c + pltpu.roll(w, 96, 1) - pltpu.roll(w, 32, 1)


def _sigmoid(x):
    return 1.0 / (1.0 + jnp.exp(-x))


_GELU_K = 0.7978845608028654
_GELU_C = 0.044715


def _gelu(x):
    return 0.5 * x * (1.0 + jnp.tanh(_GELU_K * (x + _GELU_C * x * x * x)))


def _gelu_grad(x):
    t = jnp.tanh(_GELU_K * (x + _GELU_C * x * x * x))
    return 0.5 * (1.0 + t) + 0.5 * x * (1.0 - t * t) * (_GELU_K * (1.0 + 3.0 * _GELU_C * x * x))


def _acc_rows(ref, val):
    ref[...] += jnp.broadcast_to(jnp.sum(val, axis=0, keepdims=True), ref.shape)


def _row(tm, c):
    return pl.BlockSpec((tm, c), lambda i: (i, 0))


def _const(shape):
    nd = len(shape)
    return pl.BlockSpec(shape, lambda i: (0,) * nd, pipeline_mode=pl.Buffered(1))


def _wblk(rows, row0):
    assert row0 % rows == 0, (rows, row0)
    return pl.BlockSpec((N_CHIPS, rows, D), lambda i: (0, row0 // rows, 0), pipeline_mode=pl.Buffered(1))


def _rows_joined(w_ref):
    return w_ref[...].reshape(N_CHIPS * w_ref.shape[1], D)


def _sds(shape, dtype):
    return jax.ShapeDtypeStruct(shape, dtype)


def mixffn_fwd(h, y, allw, lay, g2):
    t, k = y.shape

    def body(h_ref, y_ref, wo_ref, g_ref, wu_ref, wd_ref, h1_ref, h2_ref, hn_ref, r_ref):
        h1 = h_ref[...] + _dot(y_ref[...], _rows_joined(wo_ref))
        h1_ref[...] = h1
        yn, _, _ = _rms(h1, g_ref[...], D)
        hn = yn.astype(BF16)
        hn_ref[...] = hn
        f = jnp.zeros((TMB, D), F32)
        for c in range(N_CHIPS):
            r = jnp.maximum(_dot(hn, wu_ref[c]), 0.0)
            r_ref[:, c * D:(c + 1) * D] = r.astype(BF16)
            f = f + _dot((r * r).astype(BF16), wd_ref[c])
        h2_ref[...] = h1 + f

    return pl.pallas_call(
        body, name="mixffn_fwd", grid=(t // TMB,),
        in_specs=[_row(TMB, D), _row(TMB, k), _wblk(k // N_CHIPS, lay["out"]), _const((1, D)), _wblk(D, lay["up"]),
                  _wblk(D, lay["down"])],
        out_specs=[_row(TMB, D), _row(TMB, D), _row(TMB, D), _row(TMB, DFF)],
        out_shape=[_sds((t, D), F32), _sds((t, D), F32), _sds((t, D), BF16), _sds((t, DFF), BF16)],
        compiler_params=_cp("parallel"),
    )(h, y, allw, g2, allw, allw)


def ple_fwd(h2, p, g3, allw, lay, wp):
    t = h2.shape[0]

    def body(h_ref, p_ref, g_ref, wg_ref, wp_ref, h3_ref, hn_ref):
        x = h_ref[...]
        yn, _, _ = _rms(x, g_ref[...], D)
        hn = yn.astype(BF16)
        hn_ref[...] = hn
        gt = _dot(hn, _rows_joined(wg_ref))
        pp = _dot(p_ref[...].astype(BF16), wp_ref[...])
        h3_ref[...] = x + _sigmoid(gt) * pp

    return pl.pallas_call(
        body, name="ple_fwd", grid=(t // TMB,),
        in_specs=[_row(TMB, D), _row(TMB, PLE), _const((1, D)), _wblk(D // N_CHIPS, lay["gate"]), _const((PLE, D))],
        out_specs=[_row(TMB, D), _row(TMB, D)],
        out_shape=[_sds((t, D), F32), _sds((t, D), BF16)],
        compiler_params=_cp("parallel"),
    )(h2, p, g3, allw, wp)


def _mla_project(h_ref, g1_ref, wdn_ref, gq_ref, gkv_ref, wuq_ref, wukv_ref):
    x = h_ref[...]
    yn, xhat, rx = _rms(x, g1_ref[...], D)
    hn = yn.astype(BF16)
    lat = _dot(hn, wdn_ref[...])
    cq, cqhat, rq = _rms(lat[:, :QL], gq_ref[...], QL)
    ckv, ckvhat, rkv = _rms(lat[:, QL:QL + KVL], gkv_ref[...], KVL)
    kr_raw = lat[:, QL + KVL:]
    cqb = cq.astype(BF16)
    ckvb = ckv.astype(BF16)
    qp = _dot(cqb, wuq_ref[...])
    kvp = _dot(ckvb, wukv_ref[...])
    return dict(xhat=xhat, rx=rx, hn=hn, cqhat=cqhat, rq=rq, ckvhat=ckvhat, rkv=rkv, kr_raw=kr_raw,
                cqb=cqb, ckvb=ckvb, qp=qp, kvp=kvp)


def mla_pre_fwd(h, g1, wdn, gq, gkv, wuq, wukv, gqn, gqr, gkn, gkr, cos, sin):
    t = h.shape[0]

    def body(h_ref, g1_ref, wdn_ref, gq_ref, gkv_ref, wuq_ref, wukv_ref, gqn_ref, gqr_ref, gkn_ref, gkr_ref,
             c_ref, s_ref, q_ref, k_ref, v_ref):
        m = _mla_project(h_ref, g1_ref, wdn_ref, gq_ref, gkv_ref, wuq_ref, wukv_ref)
        c = c_ref[...]
        s = s_ref[...]
        kr, _, _ = _rms(m["kr_raw"], gkr_ref[...], DR)
        krb = _rope(kr, c, s).astype(BF16)
        for hd in range(HEADS):
            qn, _, _ = _rms(m["qp"][:, hd * DN:(hd + 1) * DN], gqn_ref[...], DN)
            qr, _, _ = _rms(m["qp"][:, D + hd * LANES:D + (hd + 1) * LANES], gqr_ref[...], DR)
            q_ref[hd, :, 0:DN] = (qn * SM_SCALE).astype(BF16)
            q_ref[hd, :, DN:2 * DN] = (_rope(qr, c, s) * SM_SCALE).astype(BF16)
            kn, _, _ = _rms(m["kvp"][:, hd * 2 * DN:hd * 2 * DN + DN], gkn_ref[...], DN)
            k_ref[hd, :, 0:DN] = kn.astype(BF16)
            k_ref[hd, :, DN:2 * DN] = krb
            v_ref[hd] = m["kvp"][:, hd * 2 * DN + DN:(hd + 1) * 2 * DN].astype(BF16)

    hb = lambda w: pl.BlockSpec((HEADS, TM, w), lambda i: (0, i, 0))
    return pl.pallas_call(
        body, name="mla_pre_fwd", grid=(t // TM,),
        in_specs=[_row(TM, D), _const((1, D)), _const((D, LATP)), _const((1, QL)), _const((1, KVL)),
                  _const((QL, 2 * D)), _const((KVL, 2 * D)), _const((1, LANES)), _const((1, LANES)),
                  _const((1, LANES)), _const((1, LANES)), _row(TM, LANES), _row(TM, LANES)],
        out_specs=[hb(2 * DN), hb(2 * DN), hb(DN)],
        out_shape=[_sds((HEADS, t, 2 * DN), BF16), _sds((HEADS, t, 2 * DN), BF16), _sds((HEADS, t, DN), BF16)],
        compiler_params=_cp("parallel"),
    )(h, g1, wdn, gq, gkv, wuq, wukv, gqn, gqr, gkn, gkr, cos, sin)


def _diagonal_mask(n=TQ):
    return lax.broadcasted_iota(jnp.int32, (n, n), 1) <= lax.broadcasted_iota(jnp.int32, (n, n), 0)


def flash_fwd(q, k, v, seq):
    t = q.shape[1]
    nb = t // seq
    tq = TQ_FWD
    nq = seq // tq
    hp = FWD_HEADS

    def body(q_ref, k_ref, v_ref, o_ref, lse_ref):
        qi = pl.program_id(2)
        qs = [q_ref[a] for a in range(hp)]

        def step(j, carry, diagonal=False):
            rows = pl.ds(pl.multiple_of(j * tq, tq), tq)
            out = []
            for a in range(hp):
                m, l, acc = carry[a]
                s = _dot_nt(qs[a], k_ref[a, rows, :])
                if diagonal:
                    s = jnp.where(_diagonal_mask(tq), s, -1e30)
                m_new = jnp.maximum(m, jnp.max(s, axis=-1, keepdims=True))
                p = jnp.exp(s - m_new)
                alpha = jnp.exp(m - m_new)
                l = alpha * l + jnp.sum(p, axis=-1, keepdims=True)
                acc = alpha * acc + _dot(p.astype(BF16), v_ref[a, rows, :])
                out.append((m_new, l, acc))
            return tuple(out)

        one = (jnp.full((tq, 1), -1e30, F32), jnp.zeros((tq, 1), F32), jnp.zeros((tq, DN), F32))
        done = step(qi, lax.fori_loop(0, qi, step, (one,) * hp), diagonal=True)
        for a, (m, l, acc) in enumerate(done):
            o_ref[:, a * DN:(a + 1) * DN] = (acc / l).astype(BF16)
            lse_ref[a] = m + jnp.log(l)

    return pl.pallas_call(
        body, name="flash_fwd", grid=(nb, HEADS // hp, nq),
        in_specs=[pl.BlockSpec((hp, tq, 2 * DN), lambda b, h, i: (h, b * nq + i, 0)),
                  pl.BlockSpec((hp, seq, 2 * DN), lambda b, h, i: (h, b, 0)),
                  pl.BlockSpec((hp, seq, DN), lambda b, h, i: (h, b, 0))],
        out_specs=[pl.BlockSpec((tq, hp * DN), lambda b, h, i: (b * nq + i, h)),
                   pl.BlockSpec((hp, tq, 1), lambda b, h, i: (h, b * nq + i, 0))],
        out_shape=[_sds((t, HEADS * DN), BF16), _sds((HEADS, t, 1), F32)],
        compiler_params=_cp("parallel", "parallel", "arbitrary"),
    )(q, k, v)


def _gmlp_in(hn, win_ref):
    pre = [_dot(hn, win_ref[c]) for c in range(N_CHIPS)]
    return jnp.concatenate(pre[:2], axis=1), jnp.concatenate(pre[2:], axis=1)


def gmlp_fwd(h, g1, allw, lay, lng, lnb, wm, bfull):
    t = h.shape[0]

    def body(h_ref, g1_ref, win_ref, lng_ref, lnb_ref, wm_ref, b_ref, y_ref, pre_ref):
        yn, _, _ = _rms(h_ref[...], g1_ref[...], D)
        pre_u, pre_v = _gmlp_in(yn.astype(BF16), win_ref)
        pre_ref[:, :GH] = pre_u.astype(BF16)
        pre_ref[:, GH:] = pre_v.astype(BF16)
        u = _gelu(pre_u)
        v = _gelu(pre_v)
        xc = v - jnp.mean(v, axis=-1, keepdims=True)
        rs = lax.rsqrt(jnp.mean(xc * xc, axis=-1, keepdims=True) + EPS)
        vnb = (xc * rs * lng_ref[...] + lnb_ref[...]).astype(BF16)
        for ch in range(TM // GC):
            rows = slice(ch * GC, (ch + 1) * GC)
            for g in range(GG):
                cols = slice(g * GD, (g + 1) * GD)
                sv = _dot(wm_ref[g], vnb[rows, cols]) + b_ref[:, cols]
                y_ref[rows, cols] = (u[rows, cols] * sv).astype(BF16)

    return pl.pallas_call(
        body, name="gmlp_fwd", grid=(t // TM,),
        in_specs=[_row(TM, D), _const((1, D)), _wblk(D, lay["in"]), _const((1, GH)), _const((1, GH)),
                  _const((GG, GC, GC)), _const((GC, GH))],
        out_specs=[_row(TM, GH), _row(TM, 2 * GH)],
        out_shape=[_sds((t, GH), BF16), _sds((t, 2 * GH), BF16)],
        compiler_params=_cp("parallel"),
    )(h, g1, allw, lng, lnb, wm, bfull)


def loss_head(h, tgt):
    t = h.shape[0]

    def body(h_ref, t_ref, dh_ref, loss_ref):
        @pl.when(pl.program_id(0) == 0)
        def _():
            loss_ref[...] = jnp.zeros_like(loss_ref)

        e = h_ref[...] - t_ref[...]
        dh_ref[...] = e * (1.0 / D)
        part = jnp.sum(jnp.sum(e * e, axis=-1, keepdims=True), axis=0, keepdims=True) * (0.5 / D)
        loss_ref[...] += jnp.broadcast_to(part, loss_ref.shape)

    return pl.pallas_call(
        body, name="loss_head", grid=(t // TMB,),
        in_specs=[_row(TMB, D), _row(TMB, D)],
        out_specs=[_row(TMB, D), _const((8, LANES))],
        out_shape=[_sds((t, D), F32), _sds((8, LANES), F32)],
        compiler_params=_cp("arbitrary"),
    )(h, tgt)


def _zero_at_first_step(*refs):
    @pl.when(pl.program_id(0) == 0)
    def _():
        for r in refs:
            r[...] = jnp.zeros_like(r)


def ple_bwd(dh3, h2, p, g3, allw, lay, wp):
    t = h2.shape[0]

    def body(dh_ref, h_ref, p_ref, g_ref, wg_ref, wp_ref, dh2_ref, dh2b_ref, dgt_ref, dpp_ref, dg_ref):
        _zero_at_first_step(dg_ref)
        dh3v = dh_ref[...]
        x = h_ref[...]
        g = g_ref[...]
        wg = _rows_joined(wg_ref)
        yn, xhat, r = _rms(x, g, D)
        gt = _dot(yn.astype(BF16), wg)
        pp = _dot(p_ref[...].astype(BF16), wp_ref[...])
        sg = _sigmoid(gt)
        dgt = (dh3v * pp * sg * (1.0 - sg)).astype(BF16)
        dgt_ref[...] = dgt
        dpp_ref[...] = (dh3v * sg).astype(BF16)
        dhn = _dot_nt(dgt, wg)
        _acc_rows(dg_ref, dhn * xhat)
        dh2 = dh3v + _rms_bwd(dhn, g, xhat, r, D)
        dh2_ref[...] = dh2
        dh2b_ref[...] = dh2.astype(BF16)

    return pl.pallas_call(
        body, name="ple_bwd", grid=(t // TMB,),
        in_specs=[_row(TMB, D), _row(TMB, D), _row(TMB, PLE), _const((1, D)), _wblk(D // N_CHIPS, lay["gate"]),
                  _const((PLE, D))],
        out_specs=[_row(TMB, D), _row(TMB, D), _row(TMB, D), _row(TMB, D), _const((8, D))],
        out_shape=[_sds((t, D), F32), _sds((t, D), BF16), _sds((t, D), BF16), _sds((t, D), BF16), _sds((8, D), F32)],
        compiler_params=_cp("arbitrary"),
    )(dh3, h2, p, g3, allw, wp)


def ffn_bwd(dh2, dh2b, h1, r, g2, allw, lay):
    t = h1.shape[0]

    def body(dh_ref, dhb_ref, h_ref, r_ref, g_ref, wu_ref, wd_ref, dh1_ref, dh1b_ref, du_ref, a_ref, dg_ref):
        _zero_at_first_step(dg_ref)
        dhb = dhb_ref[...]
        g = g_ref[...]
        _, xhat, rr = _rms(h_ref[...], g, D)
        dhn = jnp.zeros((TM, D), F32)
        for c in range(N_CHIPS):
            cs = slice(c * D, (c + 1) * D)
            rc = r_ref[:, cs].astype(F32)
            a_ref[:, cs] = (rc * rc).astype(BF16)
            da = _dot_nt(dhb, wd_ref[c])
            du = (da * (2.0 * rc)).astype(BF16)
            du_ref[:, cs] = du
            dhn = dhn + _dot_nt(du, wu_ref[c])
        _acc_rows(dg_ref, dhn * xhat)
        dh1 = dh_ref[...] + _rms_bwd(dhn, g, xhat, rr, D)
        dh1_ref[...] = dh1
        dh1b_ref[...] = dh1.astype(BF16)

    return pl.pallas_call(
        body, name="ffn_bwd", grid=(t // TM,),
        in_specs=[_row(TM, D), _row(TM, D), _row(TM, D), _row(TM, DFF), _const((1, D)), _wblk(D, lay["up"]),
                  _wblk(D, lay["down"])],
        out_specs=[_row(TM, D), _row(TM, D), _row(TM, DFF), _row(TM, DFF), _const((8, D))],
        out_shape=[_sds((t, D), F32), _sds((t, D), BF16), _sds((t, DFF), BF16), _sds((t, DFF), BF16),
                   _sds((8, D), F32)],
        compiler_params=_cp("arbitrary"),
    )(dh2, dh2b, h1, r, g2, allw, allw)


def linear_nt(a, allw, rows, row0):
    t = a.shape[0]
    k = N_CHIPS * rows

    def body(a_ref, w_ref, o_ref):
        o_ref[...] = _dot_nt(a_ref[...], _rows_joined(w_ref)).astype(BF16)

    return pl.pallas_call(
        body, name="linear_nt", grid=(t // TMB,),
        in_specs=[_row(TMB, D), _wblk(rows, row0)],
        out_specs=_row(TMB, k),
        out_shape=_sds((t, k), BF16),
        compiler_params=_cp("parallel"),
    )(a, allw)


def flash_bwd(q, k, v, o, do, lse, seq):
    t = q.shape[1]
    nb = t // seq
    nq = seq // TQ

    def body(q_ref, k_ref, v_ref, o_ref, do_ref, lse_ref, dq_ref, dk_ref, dv_ref):
        kj = pl.program_id(2)

        @pl.when(kj == 0)
        def _():
            dq_ref[...] = jnp.zeros_like(dq_ref)

        kv = k_ref[0]
        vv = v_ref[0]

        def step(i, carry, diagonal=False):
            dk, dv = carry
            rows = pl.ds(pl.multiple_of(i * TQ, TQ), TQ)
            qv = q_ref[0, rows, :]
            dov = do_ref[rows, :]
            delta = jnp.sum(dov.astype(F32) * o_ref[rows, :].astype(F32), axis=-1, keepdims=True)
            s = _dot_nt(qv, kv)
            if diagonal:
                s = jnp.where(_diagonal_mask(), s, -1e30)
            p = jnp.exp(s - lse_ref[0, rows, :])
            dp = _dot_nt(dov, vv)
            ds = (p * (dp - delta)).astype(BF16)
            dv = dv + _dot_tn(p.astype(BF16), dov)
            dk = dk + _dot_tn(ds, qv)
            dq_ref[0, rows, :] += _dot(ds, kv)
            return dk, dv

        init = (jnp.zeros((TQ, 2 * DN), F32), jnp.zeros((TQ, DN), F32))
        dk, dv = lax.fori_loop(kj + 1, nq, step, step(kj, init, diagonal=True))
        dk_ref[0] = dk
        dv_ref[0] = dv

    return pl.pallas_call(
        body, name="flash_bwd", grid=(nb, HEADS, nq),
        in_specs=[pl.BlockSpec((1, seq, 2 * DN), lambda b, h, j: (h, b, 0)),
                  pl.BlockSpec((1, TQ, 2 * DN), lambda b, h, j: (h, b * nq + j, 0)),
                  pl.BlockSpec((1, TQ, DN), lambda b, h, j: (h, b * nq + j, 0)),
                  pl.BlockSpec((seq, DN), lambda b, h, j: (b, h)),
                  pl.BlockSpec((seq, DN), lambda b, h, j: (b, h)),
                  pl.BlockSpec((1, seq, 1), lambda b, h, j: (h, b, 0))],
        out_specs=[pl.BlockSpec((1, seq, 2 * DN), lambda b, h, j: (h, b, 0)),
                   pl.BlockSpec((1, TQ, 2 * DN), lambda b, h, j: (h, b * nq + j, 0)),
                   pl.BlockSpec((1, TQ, DN), lambda b, h, j: (h, b * nq + j, 0))],
        out_shape=[_sds((HEADS, t, 2 * DN), F32), _sds((HEADS, t, 2 * DN), F32), _sds((HEADS, t, DN), F32)],
        compiler_params=_cp("parallel", "parallel", "arbitrary"),
    )(q, k, v, o, do, lse)


def mla_pre_bwd(dq, dk, dv, dh1, h, g1, wdn, gq, gkv, wuq, wukv, gqn, gqr, gkn, gkr, cos, sin):
    t = h.shape[0]

    def body(dq_ref, dk_ref, dv_ref, dh1_ref, h_ref, g1_ref, wdn_ref, gq_ref, gkv_ref, wuq_ref, wukv_ref,
             gqn_ref, gqr_ref, gkn_ref, gkr_ref, c_ref, s_ref,
             dh_ref, hn_ref, cq_ref, ckv_ref, dqp_ref, dkvp_ref, dlat_ref,
             dg1_ref, dgq_ref, dgkv_ref, dgqn_ref, dgqr_ref, dgkn_ref, dgkr_ref):
        _zero_at_first_step(dg1_ref, dgq_ref, dgkv_ref, dgqn_ref, dgqr_ref, dgkn_ref, dgkr_ref)
        m = _mla_project(h_ref, g1_ref, wdn_ref, gq_ref, gkv_ref, wuq_ref, wukv_ref)
        hn_ref[...] = m["hn"]
        cq_ref[...] = m["cqb"]
        ckv_ref[...] = m["ckvb"]
        c = c_ref[...]
        s = s_ref[...]
        gqn = gqn_ref[...]
        gqr = gqr_ref[...]
        gkn = gkn_ref[...]
        gkr = gkr_ref[...]

        dkr = dk_ref[0, :, DN:2 * DN]
        for hd in range(1, HEADS):
            dkr = dkr + dk_ref[hd, :, DN:2 * DN]
        dkr = _rope_t(dkr, c, s)
        _, krhat, rkr = _rms(m["kr_raw"], gkr, DR)
        _acc_rows(dgkr_ref, dkr * krhat)
        dkr_raw = _rms_bwd(dkr, gkr, krhat, rkr, DR)

        for hd in range(HEADS):
            ncols = slice(hd * DN, (hd + 1) * DN)
            _, xh, r = _rms(m["qp"][:, ncols], gqn, DN)
            dqn = dq_ref[hd, :, 0:DN] * SM_SCALE
            _acc_rows(dgqn_ref, dqn * xh)
            dqp_ref[:, ncols] = _rms_bwd(dqn, gqn, xh, r, DN).astype(BF16)

            rcols = slice(D + hd * LANES, D + (hd + 1) * LANES)
            _, xh, r = _rms(m["qp"][:, rcols], gqr, DR)
            dqr = _rope_t(dq_ref[hd, :, DN:2 * DN] * SM_SCALE, c, s)
            _acc_rows(dgqr_ref, dqr * xh)
            dqp_ref[:, rcols] = _rms_bwd(dqr, gqr, xh, r, DR).astype(BF16)

            kcols = slice(hd * 2 * DN, hd * 2 * DN + DN)
            _, xh, r = _rms(m["kvp"][:, kcols], gkn, DN)
            dkn = dk_ref[hd, :, 0:DN]
            _acc_rows(dgkn_ref, dkn * xh)
            dkvp_ref[:, kcols] = _rms_bwd(dkn, gkn, xh, r, DN).astype(BF16)
            dkvp_ref[:, hd * 2 * DN + DN:(hd + 1) * 2 * DN] = dv_ref[hd].astype(BF16)

        dcq = _dot_nt(dqp_ref[...], wuq_ref[...])
        _acc_rows(dgq_ref, dcq * m["cqhat"])
        dlat_q = _rms_bwd(dcq, gq_ref[...], m["cqhat"], m["rq"], QL)
        dckv = _dot_nt(dkvp_ref[...], wukv_ref[...])
        _acc_rows(dgkv_ref, dckv * m["ckvhat"])
        dlat_kv = _rms_bwd(dckv, gkv_ref[...], m["ckvhat"], m["rkv"], KVL)
        dlat = jnp.concatenate([dlat_q, dlat_kv, dkr_raw], axis=1).astype(BF16)
        dlat_ref[...] = dlat
        dhn = _dot_nt(dlat, wdn_ref[...])
        _acc_rows(dg1_ref, dhn * m["xhat"])
        dh_ref[...] = dh1_ref[...] + _rms_bwd(dhn, g1_ref[...], m["xhat"], m["rx"], D)

    hb = lambda w: pl.BlockSpec((HEADS, TM, w), lambda i: (0, i, 0))
    return pl.pallas_call(
        body, name="mla_pre_bwd", grid=(t // TM,),
        in_specs=[hb(2 * DN), hb(2 * DN), hb(DN), _row(TM, D), _row(TM, D), _const((1, D)), _const((D, LATP)),
                  _const((1, QL)), _const((1, KVL)), _const((QL, 2 * D)), _const((KVL, 2 * D)),
                  _const((1, LANES)), _const((1, LANES)), _const((1, LANES)), _const((1, LANES)),
                  _row(TM, LANES), _row(TM, LANES)],
        out_specs=[_row(TM, D), _row(TM, D), _row(TM, QL), _row(TM, KVL), _row(TM, 2 * D), _row(TM, 2 * D),
                   _row(TM, LATP), _const((8, D)), _const((8, QL)), _const((8, KVL)), _const((8, LANES)),
                   _const((8, LANES)), _const((8, LANES)), _const((8, LANES))],
        out_shape=[_sds((t, D), F32), _sds((t, D), BF16), _sds((t, QL), BF16), _sds((t, KVL), BF16),
                   _sds((t, 2 * D), BF16), _sds((t, 2 * D), BF16), _sds((t, LATP), BF16),
                   _sds((8, D), F32), _sds((8, QL), F32), _sds((8, KVL), F32), _sds((8, LANES), F32),
                   _sds((8, LANES), F32), _sds((8, LANES), F32), _sds((8, LANES), F32)],
        compiler_params=_cp("arbitrary"),
    )(dq, dk, dv, dh1, h, g1, wdn, gq, gkv, wuq, wukv, gqn, gqr, gkn, gkr, cos, sin)


def gmlp_bwd(dh1, dh1b, h, pre, g1, allw, lay, lng, lnb, wm, wmt, bfull, tril):
    t = h.shape[0]

    def body(dh1_ref, dh1b_ref, h_ref, pre_ref, g1_ref, win_ref, lng_ref, lnb_ref, wm_ref, wmt_ref, b_ref,
             wout_ref, tril_ref, dh_ref, hn_ref, dpre_ref, dws_ref, dbs_ref, dlng_ref, dlnb_ref, dg1_ref,
             dvn_s):
        _zero_at_first_step(dws_ref, dbs_ref, dlng_ref, dlnb_ref, dg1_ref)
        g1 = g1_ref[...]
        yn, xhat, rx = _rms(h_ref[...], g1, D)
        hn_ref[...] = yn.astype(BF16)
        dy = _dot_nt(dh1b_ref[...], _rows_joined(wout_ref))
        pre_u = pre_ref[:, :GH].astype(F32)
        pre_v = pre_ref[:, GH:].astype(F32)
        u = _gelu(pre_u)
        v = _gelu(pre_v)
        xc = v - jnp.mean(v, axis=-1, keepdims=True)
        rs = lax.rsqrt(jnp.mean(xc * xc, axis=-1, keepdims=True) + EPS)
        vhat = xc * rs
        lng = lng_ref[...]
        vnb = (vhat * lng + lnb_ref[...]).astype(BF16)
        dsv = dy * u
        dsvb = dsv.astype(BF16)
        tril_m = tril_ref[...]
        gg_u = _gelu_grad(pre_u)
        for ch in range(TM // GC):
            rows = slice(ch * GC, (ch + 1) * GC)
            dbs_ref[...] += dsv[rows, :]
            for g in range(GG):
                cols = slice(g * GD, (g + 1) * GD)
                sv = _dot(wm_ref[g], vnb[rows, cols]) + b_ref[:, cols]
                dpre_ref[rows, cols] = (dy[rows, cols] * sv * gg_u[rows, cols]).astype(BF16)
                dvn_s[rows, cols] = _dot(wmt_ref[g], dsvb[rows, cols])
                dws_ref[g] += _dot_nt(dsvb[rows, cols], vnb[rows, cols]) * tril_m
        dvn = dvn_s[...]
        _acc_rows(dlng_ref, dvn * vhat)
        _acc_rows(dlnb_ref, dvn)
        dvhat = dvn * lng
        dv = rs * (dvhat - jnp.mean(dvhat, axis=-1, keepdims=True)
                   - vhat * jnp.mean(dvhat * vhat, axis=-1, keepdims=True))
        dpre_v = (dv * _gelu_grad(pre_v)).astype(BF16)
        dpre_ref[:, GH:] = dpre_v
        dhn = _dot_nt(dpre_ref[:, 0:D], win_ref[0])
        for c in range(1, N_CHIPS):
            dhn = dhn + _dot_nt(dpre_ref[:, c * D:(c + 1) * D], win_ref[c])
        _acc_rows(dg1_ref, dhn * xhat)
        dh_ref[...] = dh1_ref[...] + _rms_bwd(dhn, g1, xhat, rx, D)

    return pl.pallas_call(
        body, name="gmlp_bwd", grid=(t // TM,),
        in_specs=[_row(TM, D), _row(TM, D), _row(TM, D), _row(TM, 2 * GH), _const((1, D)), _wblk(D, lay["in"]),
                  _const((1, GH)), _const((1, GH)), _const((GG, GC, GC)), _const((GG, GC, GC)), _const((GC, GH)),
                  _wblk(GH // N_CHIPS, lay["out"]), _const((GC, GC))],
        out_specs=[_row(TM, D), _row(TM, D), _row(TM, 2 * GH), _const((GG, GC, GC)), _const((GC, GH)),
                   _const((8, GH)), _const((8, GH)), _const((8, D))],
        out_shape=[_sds((t, D), F32), _sds((t, D), BF16), _sds((t, 2 * GH), BF16), _sds((GG, GC, GC), F32),
                   _sds((GC, GH), F32), _sds((8, GH), F32), _sds((8, GH), F32), _sds((8, D), F32)],
        scratch_shapes=[pltpu.VMEM((TM, GH), F32)],
        compiler_params=_cp("arbitrary"),
    )(dh1, dh1b, h, pre, g1, allw, lng, lnb, wm, wmt, bfull, allw, tril)


def _token_step(t):
    return 1024 if t % 1024 == 0 else 512


def mm_tn(a, b):
    t, k = a.shape
    n = b.shape[1]
    tk = min(k, 1024)
    tn = min(n, 1024)
    tt = _token_step(t)

    def body(a_ref, b_ref, o_ref):
        @pl.when(pl.program_id(2) == 0)
        def _():
            o_ref[...] = jnp.zeros_like(o_ref)

        o_ref[...] += _dot_tn(a_ref[...].astype(BF16), b_ref[...].astype(BF16))

    return pl.pallas_call(
        body, name="mm_tn", grid=(k // tk, n // tn, t // tt),
        in_specs=[pl.BlockSpec((tt, tk), lambda i, j, s: (s, i)), pl.BlockSpec((tt, tn), lambda i, j, s: (s, j))],
        out_specs=pl.BlockSpec((tk, tn), lambda i, j, s: (i, j)), out_shape=_sds((k, n), F32),
        compiler_params=_cp("parallel", "parallel", "arbitrary"),
    )(a, b)


def mm_tn_into(buf, a, b, rows, row0, col_sharded):
    t = a.shape[0]
    tt = _token_step(t)
    assert row0 % rows == 0 and a.shape[1] == (rows if col_sharded else N_CHIPS * rows), (rows, row0, a.shape)
    assert b.shape[1] == (N_CHIPS * D if col_sharded else D), b.shape
    grid = (1, N_CHIPS, t // tt) if col_sharded else (N_CHIPS, 1, t // tt)

    def body(buf_ref, a_ref, b_ref, o_ref):
        del buf_ref

        @pl.when(pl.program_id(2) == 0)
        def _():
            o_ref[...] = jnp.zeros_like(o_ref)

        o_ref[...] += _dot_tn(a_ref[...].astype(BF16), b_ref[...].astype(BF16))

    return pl.pallas_call(
        body, name="mm_tn_into", grid=grid,
        in_specs=[_ANY, pl.BlockSpec((tt, rows), lambda i, j, s: (s, i)), pl.BlockSpec((tt, D), lambda i, j, s: (s, j))],
        out_specs=pl.BlockSpec((None, rows, D), lambda i, j, s: (i + j, row0 // rows, 0)),
        out_shape=_sds(buf.shape, F32), input_output_aliases={0: 0},
        compiler_params=_cp("parallel", "parallel", "arbitrary"),
    )(buf, a, b)


def adamw(w, g, m, v):
    rows, cols = w.shape
    tr = rows if rows <= 512 else next(r for r in (512, 384, 256, 128) if rows % r == 0)
    c1 = 1.0 - ADAM_B1 ** ADAM_STEP
    c2 = 1.0 - ADAM_B2 ** ADAM_STEP

    def body(w_ref, g_ref, m_ref, v_ref, d_ref, mo_ref, vo_ref):
        gv = g_ref[...]
        mn = ADAM_B1 * m_ref[...] + (1.0 - ADAM_B1) * gv
        vn = ADAM_B2 * v_ref[...] + (1.0 - ADAM_B2) * (gv * gv)
        mo_ref[...] = mn
        vo_ref[...] = vn
        d_ref[...] = -ADAM_LR * ((mn / c1) / (jnp.sqrt(vn / c2) + ADAM_EPS) + ADAM_WD * w_ref[...])

    spec = pl.BlockSpec((tr, cols), lambda i: (i, 0))
    return pl.pallas_call(
        body, name="adamw", grid=(rows // tr,),
        in_specs=[spec] * 4, out_specs=[spec] * 3, out_shape=[_sds((rows, cols), F32)] * 3,
        compiler_params=_cp("parallel"),
    )(w, g, m, v)


def _place():
    return lax.axis_index("x"), lax.axis_index("y"), lax.axis_index("c")


def _other_chips(x, y):
    return [(1 - x, y), (x, 1 - y), (1 - x, 1 - y)]


_ANY = pl.BlockSpec(memory_space=pl.ANY)


_HBM = pl.BlockSpec(memory_space=pltpu.HBM)
_SEM = pl.BlockSpec(memory_space=pltpu.SEMAPHORE)
_EFFECT = pltpu.SideEffectType.DATAFLOW_SIDE_EFFECTING
N_ICI = 3


def _exchange_start(name, src, land_shape, copies):
    def body(src_ref, land_ref, *outs):
        sems, token = outs[:2 * N_ICI], outs[-1]
        for j, (s, d, to) in enumerate(copies(src_ref, land_ref, _place())):
            pltpu.make_async_remote_copy(src_ref=s, dst_ref=d, send_sem=sems[j], recv_sem=sems[N_ICI + j],
                                         device_id=to, device_id_type=MESH).start()
        token[...] = jnp.zeros_like(token)

    sem = pltpu.SemaphoreType.DMA(())
    outs = pl.pallas_call(
        body, name=name,
        out_shape=(sem,) * (2 * N_ICI) + (pltpu.HBM(src.shape, src.dtype), pltpu.HBM(land_shape, src.dtype),
                                          _sds((8, LANES), F32)),
        in_specs=(_HBM, _HBM),
        out_specs=(_SEM,) * (2 * N_ICI) + (_HBM, _HBM, pl.BlockSpec(memory_space=pltpu.VMEM)),
        input_output_aliases={0: 2 * N_ICI, 1: 2 * N_ICI + 1},
        compiler_params=pltpu.CompilerParams(has_side_effects=_EFFECT),
    )(pltpu.with_memory_space_constraint(src, pltpu.HBM),
      pltpu.with_memory_space_constraint(lax.empty(land_shape, src.dtype), pltpu.HBM))
    return outs[:2 * N_ICI], outs[2 * N_ICI], outs[2 * N_ICI + 1], outs[-1]


def _exchange_wait(name, sems, src, land, after, arrivals):
    def body(src_ref, land_ref, *rest):
        sems = rest[:2 * N_ICI]
        for j, (s, d) in enumerate(arrivals(src_ref, land_ref, _place())):
            cp = pltpu.make_async_remote_copy(src_ref=s, dst_ref=d, send_sem=sems[j], recv_sem=sems[N_ICI + j],
                                              device_id=_place(), device_id_type=MESH)
            cp.wait_send()
            cp.wait_recv()

    return pl.pallas_call(
        body, name=name, out_shape=(pltpu.HBM(src.shape, src.dtype), pltpu.HBM(land.shape, land.dtype)),
        in_specs=(_HBM, _HBM) + (_SEM,) * (2 * N_ICI) + (_ANY,), out_specs=(_HBM, _HBM),
        input_output_aliases={0: 0, 1: 1},
        compiler_params=pltpu.CompilerParams(has_side_effects=_EFFECT),
    )(src, land, *sems, after)


def _gather_views(hh):
    def copies(x_ref, land_ref, place):
        x, y, c = place
        half = pl.ds(pl.multiple_of(c * hh, 16), hh)
        return [(x_ref.at[half], land_ref.at[2 * x + y, half], (cx, cy, c)) for cx, cy in _other_chips(x, y)]

    def arrivals(x_ref, land_ref, place):
        x, y, c = place
        half = pl.ds(pl.multiple_of(c * hh, 16), hh)
        return [(x_ref.at[half], land_ref.at[2 * cx + cy, half]) for cx, cy in _other_chips(x, y)]

    return copies, arrivals


def gather_start(mine, tag):
    rr, cc = mine.shape
    assert rr % 32 == 0, rr
    return _exchange_start(f"gather_start_{tag}", mine, (N_CHIPS, rr, cc), _gather_views(rr // 2)[0])


def gather_wait(sems, mine, land, after, tag):
    return _exchange_wait(f"gather_wait_{tag}", sems, mine, land, after, _gather_views(mine.shape[0] // 2)[1])


def pass_to_sibling(land):
    _, rr, cc = land.shape
    hh = rr // 2

    def body(land_ref, o_ref, send_sems, recv_sems):
        del o_ref
        x, y, c = _place()
        half = pl.ds(pl.multiple_of(c * hh, 16), hh)
        copies = [pltpu.make_async_remote_copy(src_ref=land_ref.at[2 * cx + cy, half],
                                               dst_ref=land_ref.at[2 * cx + cy, half],
                                               send_sem=send_sems.at[j], recv_sem=recv_sems.at[j],
                                               device_id=(x, y, 1 - c), device_id_type=MESH)
                  for j, (cx, cy) in enumerate(_other_chips(x, y))]
        for cp in copies:
            cp.start()
        for cp in copies:
            cp.wait()

    return pl.pallas_call(
        body, name="pass_to_sibling", in_specs=[_ANY], out_specs=_ANY, out_shape=_sds(land.shape, land.dtype),
        input_output_aliases={0: 0},
        scratch_shapes=[pltpu.SemaphoreType.DMA((N_ICI,)), pltpu.SemaphoreType.DMA((N_ICI,))],
    )(land)


def swap_halves(g):
    _, rr, cc = g.shape
    hh = rr // 2

    def body(g_ref, o_ref, send_sems, recv_sems):
        x, y, c = _place()
        other_half = pl.ds(pl.multiple_of((1 - c) * hh, 16), hh)
        copies = [pltpu.make_async_remote_copy(src_ref=g_ref.at[k, other_half], dst_ref=o_ref.at[k],
                                               send_sem=send_sems.at[k], recv_sem=recv_sems.at[k],
                                               device_id=(x, y, 1 - c), device_id_type=MESH)
                  for k in range(N_CHIPS)]
        for cp in copies:
            cp.start()
        for cp in copies:
            cp.wait()

    return pl.pallas_call(
        body, name="swap_halves", in_specs=[_ANY], out_specs=_ANY,
        out_shape=_sds((N_CHIPS, hh, cc), g.dtype),
        scratch_shapes=[pltpu.SemaphoreType.DMA((N_CHIPS,)), pltpu.SemaphoreType.DMA((N_CHIPS,))],
    )(g)


def chip_sum(place, g32, got):
    _, rr, cc = g32.shape
    hh = rr // 2
    tr = SUM_ROWS
    assert rr % 2 == 0 and hh % tr == 0, (rr, tr)
    nb = hh // tr

    def body(place_ref, g_ref, got_ref, own_ref, all_ref):
        s = g_ref[...] + got_ref[...].astype(F32)
        all_ref[...] = s.astype(BF16)
        own_ref[...] = g_ref[place_ref[1]] + got_ref[place_ref[1]].astype(F32)

    return pl.pallas_call(
        body, name="chip_sum",
        grid_spec=pltpu.PrefetchScalarGridSpec(
            num_scalar_prefetch=1, grid=(nb,),
            in_specs=[pl.BlockSpec((N_CHIPS, tr, cc), lambda i, pr: (0, pr[0] * nb + i, 0)),
                      pl.BlockSpec((N_CHIPS, tr, cc), lambda i, pr: (0, i, 0))],
            out_specs=[pl.BlockSpec((tr, cc), lambda i, pr: (i, 0)),
                       pl.BlockSpec((N_CHIPS, tr, cc), lambda i, pr: (0, i, 0))]),
        out_shape=[_sds((hh, cc), F32), _sds((N_CHIPS, hh, cc), BF16)],
        compiler_params=_cp("parallel"),
    )(place, g32, got)


def _scatter_copies(s_ref, land_ref, place):
    x, y, c = place
    return [(s_ref.at[2 * cx + cy], land_ref.at[j], (cx, cy, c)) for j, (cx, cy) in enumerate(_other_chips(x, y))]


def scatter_start(s, tag):
    return _exchange_start(f"scatter_start_{tag}", s, (N_ICI,) + s.shape[1:], _scatter_copies)


def scatter_wait(sems, s, land, after, tag):
    return _exchange_wait(f"scatter_wait_{tag}", sems, s, land, after,
                          lambda s_ref, land_ref, place: [(a, b) for a, b, _ in _scatter_copies(s_ref, land_ref, place)])


def final_sum(own, got):
    hh, cc = own.shape
    tr = SUM_ROWS
    assert hh % tr == 0, (hh, tr)

    def body(own_ref, got_ref, o_ref):
        o_ref[...] = ((own_ref[...] + got_ref[0].astype(F32)) + got_ref[1].astype(F32)) + got_ref[2].astype(F32)

    return pl.pallas_call(
        body, name="final_sum", grid=(hh // tr,),
        in_specs=[pl.BlockSpec((tr, cc), lambda i: (i, 0)), pl.BlockSpec((3, tr, cc), lambda i: (0, i, 0))],
        out_specs=pl.BlockSpec((tr, cc), lambda i: (i, 0)),
        out_shape=_sds((hh, cc), F32),
        compiler_params=_cp("parallel"),
    )(own, got)


def share_with_sibling(f):
    hh, cc = f.shape

    def body(f_ref, o_ref, send_sem, recv_sem):
        x, y, c = _place()
        mine_half = pl.ds(pl.multiple_of(c * hh, 8), hh)
        cp = pltpu.make_async_remote_copy(src_ref=f_ref, dst_ref=o_ref.at[mine_half], send_sem=send_sem,
                                          recv_sem=recv_sem, device_id=(x, y, 1 - c), device_id_type=MESH)
        cp.start()
        cp.wait()

    return pl.pallas_call(
        body, name="share_with_sibling", in_specs=[_ANY], out_specs=_ANY,
        out_shape=_sds((2 * hh, cc), f.dtype),
        scratch_shapes=[pltpu.SemaphoreType.DMA, pltpu.SemaphoreType.DMA],
    )(f)


def allreduce_small(part):
    rr, cc = part.shape
    n_dev = 8

    def body(x_ref, all_ref, sum_ref, send_sems, recv_sems, local_sem):
        x, y, c = _place()
        me, sibling = (x, y, c), (x, y, 1 - c)
        chips = _other_chips(x, y)

        def rows(px, py, pc):
            return all_ref.at[pl.ds(pl.multiple_of((4 * px + 2 * py + pc) * rr, 8), rr), :]

        def copy(j, block, to, src=None):
            return pltpu.make_async_remote_copy(src_ref=rows(*block) if src is None else src, dst_ref=rows(*block),
                                                send_sem=send_sems.at[j], recv_sem=recv_sems.at[j],
                                                device_id=to, device_id_type=MESH)

        mine = pltpu.make_async_copy(x_ref, rows(*me), local_sem)
        mine.start()
        first = [copy(0, me, sibling, src=x_ref)]
        first += [copy(1 + j, me, (*chip, c), src=x_ref) for j, chip in enumerate(chips)]
        for cp in first:
            cp.start()
        passed = [copy(4 + j, (*chip, c), sibling) for j, chip in enumerate(chips)]
        for j, chip in enumerate(chips):
            copy(1 + j, (*chip, c), me).wait_recv()
            passed[j].start()
        copy(0, sibling, me).wait_recv()
        for j, chip in enumerate(chips):
            copy(4 + j, (*chip, 1 - c), me).wait_recv()
        for cp in first + passed:
            cp.wait_send()
        mine.wait()
        acc = all_ref[0:rr, :]
        for d in range(1, n_dev):
            acc = acc + all_ref[d * rr:(d + 1) * rr, :]
        sum_ref[...] = acc

    vm = pl.BlockSpec(memory_space=pltpu.VMEM)
    return pl.pallas_call(
        body, name="allreduce_small", in_specs=[vm], out_specs=[vm, vm],
        out_shape=[_sds((n_dev * rr, cc), F32), _sds((rr, cc), F32)],
        scratch_shapes=[pltpu.SemaphoreType.DMA((7,)), pltpu.SemaphoreType.DMA((7,)), pltpu.SemaphoreType.DMA],
        compiler_params=pltpu.CompilerParams(vmem_limit_bytes=VMEM_LIMIT),
    )(part)[1]


_BIG = ["mla_w_down", "mla_w_uq", "mla_w_ukv", "mla_w_out", "gmlp_w_in", "gmlp_w_out", "ffn_w_up", "ffn_w_down",
        "ple_w_gate", "ple_w_proj"]
_SMALL = ["norm_mix", "norm_ffn", "norm_ple", "mla_q_lora_g", "mla_kv_lora_g", "mla_q_nope_g", "mla_q_rope_g",
          "mla_k_nope_g", "mla_k_rope_g", "gmlp_ln_g", "gmlp_ln_b", "gmlp_w_s", "gmlp_b_s"]

_LAY_MLA = dict(up=0, down=1024, out=2048, gate=2304, wdn=2560, wuq=2736, wukv=2880, proj=3008, rows=3072)
_LAY_GMLP = {"up": 0, "down": 1024, "in": 2048, "out": 3072, "gate": 3584, "proj": 3840, "ln": 3904, "rows": 4096}


def _layer_parts(i):
    j = i // 2
    if i % 2 == 0:
        lay = _LAY_MLA
        return lay, [("ffn_w_up", i, lay["up"]), ("ffn_w_down", i, lay["down"]), ("mla_w_out", j, lay["out"]),
                     ("ple_w_gate", i, lay["gate"]), ("mla_w_down", j, lay["wdn"]), ("mla_w_uq", j, lay["wuq"]),
                     ("mla_w_ukv", j, lay["wukv"]), ("ple_w_proj", i, lay["proj"])]
    lay = _LAY_GMLP
    return lay, [("ffn_w_up", i, lay["up"]), ("ffn_w_down", i, lay["down"]), ("gmlp_w_in", j, lay["in"]),
                 ("gmlp_w_out", j, lay["out"]), ("ple_w_gate", i, lay["gate"]), ("ple_w_proj", i, lay["proj"])]


def _pack_rows(parts, dtype, pad_to=None):
    flat = jnp.concatenate([p.reshape(-1).astype(dtype) for p in parts])
    if pad_to is not None:
        flat = jnp.pad(flat, (0, pad_to * D - flat.size))
    return flat.reshape(-1, D)


def _odd(allw, row0, a, b):
    return allw[:, row0:row0 + a * b // D].reshape(N_CHIPS, a, b)


def _cols_joined(s):
    return jnp.transpose(s, (1, 0, 2)).reshape(s.shape[1], N_CHIPS * s.shape[2])


def _col_shards(full):
    a, bb = full.shape
    return jnp.transpose(full.reshape(a, N_CHIPS, bb // N_CHIPS), (1, 0, 2)).reshape(N_CHIPS, -1, D)


def _pad_lanes(g):
    return jnp.pad(g, ((0, 0), (0, LANES - g.shape[1])))


def _split_uq(wuq):
    l = wuq.shape[0]
    w = wuq.reshape(l, QL, HEADS, DN + DR)
    nope = w[..., :DN].reshape(l, QL, HEADS * DN)
    rope = jnp.pad(w[..., DN:], ((0, 0), (0, 0), (0, 0), (0, LANES - DR))).reshape(l, QL, HEADS * LANES)
    return jnp.concatenate([nope, rope], axis=-1)


def _merge_uq(d):
    nope = d[:, :HEADS * DN].reshape(QL, HEADS, DN)
    rope = d[:, HEADS * DN:].reshape(QL, HEADS, LANES)[..., :DR]
    return jnp.concatenate([nope, rope], axis=-1).reshape(QL, HEADS * (DN + DR))


def _rope_tables(positions):
    inv_freq = ROPE_BASE ** (-(jnp.arange(0, DR, 2, dtype=F32) / DR))
    ang = positions.reshape(-1).astype(F32)[:, None] * inv_freq
    z = jnp.zeros((ang.shape[0], LANES - DR), F32)
    return (jnp.concatenate([jnp.cos(ang), jnp.cos(ang), z], axis=1),
            jnp.concatenate([jnp.sin(ang), jnp.sin(ang), z], axis=1))


def kernel(x, p, positions, norm_mix, norm_ffn, norm_ple, mla_w_down, mla_q_lora_g, mla_kv_lora_g, mla_w_uq, mla_w_ukv, mla_q_nope_g, mla_q_rope_g, mla_k_nope_g, mla_k_rope_g, mla_w_out, gmlp_w_in, gmlp_ln_g, gmlp_ln_b, gmlp_w_s, gmlp_b_s, gmlp_w_out, ffn_w_up, ffn_w_down, ple_w_gate, ple_w_proj, loss_target, m_norm_mix, m_norm_ffn, m_norm_ple, m_mla_w_down, m_mla_q_lora_g, m_mla_kv_lora_g, m_mla_w_uq, m_mla_w_ukv, m_mla_q_nope_g, m_mla_q_rope_g, m_mla_k_nope_g, m_mla_k_rope_g, m_mla_w_out, m_gmlp_w_in, m_gmlp_ln_g, m_gmlp_ln_b, m_gmlp_w_s, m_gmlp_b_s, m_gmlp_w_out, m_ffn_w_up, m_ffn_w_down, m_ple_w_gate, m_ple_w_proj, v_norm_mix, v_norm_ffn, v_norm_ple, v_mla_w_down, v_mla_q_lora_g, v_mla_kv_lora_g, v_mla_w_uq, v_mla_w_ukv, v_mla_q_nope_g, v_mla_q_rope_g, v_mla_k_nope_g, v_mla_k_rope_g, v_mla_w_out, v_gmlp_w_in, v_gmlp_ln_g, v_gmlp_ln_b, v_gmlp_w_s, v_gmlp_b_s, v_gmlp_w_out, v_ffn_w_up, v_ffn_w_down, v_ple_w_gate, v_ple_w_proj):
    args = dict(locals())
    weights = {n: args[n] for n in _BIG + _SMALL}
    depth = norm_mix.shape[0]
    nb, seq, _ = x.shape
    t = nb * seq
    assert seq % TQ == 0 and seq % TM == 0 and t % 512 == 0, (nb, seq)
    cx = lax.axis_index("x")
    cy = lax.axis_index("y")
    cc = lax.axis_index("c")
    chip = 2 * cx + cy

    gathers = []
    token = None
    for i in range(depth):
        lay, parts = _layer_parts(i)
        rows = [weights[n][l] for n, l, _ in parts]
        if token is not None:
            rows[0] = rows[0] + token[0, 0]
        if i % 2 == 1:
            ln = jnp.stack([gmlp_ln_g[i // 2], gmlp_ln_b[i // 2]]).astype(F32)
            rows.append(lax.bitcast_convert_type(ln, BF16))
        mine = _pack_rows(rows, BF16, pad_to=lay["rows"])
        sems, mine, land, token = gather_start(mine, i)
        gathers.append((sems, mine, land))
    allw = [None] * depth

    tril = jnp.tril(jnp.ones((GC, GC), F32))
    wm = (gmlp_w_s * tril).astype(BF16)
    wmt = jnp.swapaxes(wm, -1, -2)
    bfull = jnp.repeat(jnp.swapaxes(gmlp_b_s, -1, -2), GD, axis=-1)
    cos, sin = _rope_tables(positions)
    row = lambda g: g.reshape(1, -1)
    gqr = _pad_lanes(mla_q_rope_g)
    gkr = _pad_lanes(mla_k_rope_g)

    h = x.reshape(t, D)
    pt = p.reshape(depth, t, PLE)
    saved = []
    for i in range(depth):
        j = i // 2
        lay, _ = _layer_parts(i)
        sems, mine, land = gathers[i]
        mine, land = gather_wait(sems, mine, land, token if i == 0 else h, i)
        aw = allw[i] = lax.dynamic_update_slice(pass_to_sibling(land), mine[None], (chip, 0, 0))
        s = dict(h=h)
        if i % 2 == 0:
            wdn = jnp.pad(_odd(aw, lay["wdn"], D // N_CHIPS, LAT).reshape(D, LAT), ((0, 0), (0, LATP - LAT)))
            wuq = _split_uq(_cols_joined(_odd(aw, lay["wuq"], QL, 384))[None])[0]
            wukv = _cols_joined(_odd(aw, lay["wukv"], KVL, 512))
            mla_args = (row(norm_mix[i]), wdn, row(mla_q_lora_g[j]), row(mla_kv_lora_g[j]), wuq, wukv,
                        row(mla_q_nope_g[j]), gqr[j:j + 1], row(mla_k_nope_g[j]), gkr[j:j + 1], cos, sin)
            q, k, v = mla_pre_fwd(h, *mla_args)
            y, lse = flash_fwd(q, k, v, seq)
            s.update(q=q, k=k, v=v, lse=lse, mla_args=mla_args)
        else:
            ln = lax.bitcast_convert_type(aw[:, lay["ln"]:lay["ln"] + 2].reshape(N_CHIPS, 2, GH // N_CHIPS, 2), F32)
            ln = jnp.transpose(ln, (1, 0, 2)).reshape(2, 1, GH)
            y, pre = gmlp_fwd(h, row(norm_mix[i]), aw, lay, ln[0], ln[1], wm[j], bfull[j])
            s.update(pre=pre, ln=ln)
        wp = _cols_joined(_odd(aw, lay["proj"], PLE, 256))
        h1, h2, hn2, r = mixffn_fwd(h, y, aw, lay, row(norm_ffn[i]))
        h, hn3 = ple_fwd(h2, pt[i], row(norm_ple[i]), aw, lay, wp)
        s.update(y=y, wp=wp, h1=h1, h2=h2, hn2=hn2, r=r, hn3=hn3)
        saved.append(s)

    dh, loss_part = loss_head(h, loss_target.reshape(t, D))
    loss = lax.psum(loss_part[0, 0], ("x", "y", "c"))

    gs = {n: [None] * weights[n].shape[0] for n in _SMALL}
    gw = {n: [None] * weights[n].shape[0] for n in _BIG}
    place = jnp.stack([cc, chip]).astype(jnp.int32)
    scatters = []
    token = None
    for i in reversed(range(depth)):
        j = i // 2
        lay, parts = _layer_parts(i)
        aw = allw[i]
        s = saved[i]

        def put(b, row0, shards):
            return lax.dynamic_update_slice(b, shards.reshape(N_CHIPS, -1, D), (0, row0, 0))

        buf = lax.empty((N_CHIPS, lay["rows"], D), F32)
        tail = lay.get("ln", lay["rows"])
        if tail < lay["rows"]:
            buf = put(buf, tail, jnp.zeros((N_CHIPS, lay["rows"] - tail, D), F32))
        g3 = row(norm_ple[i])
        if token is not None:
            g3 = g3 + token[0:1, 0:1]
        dh2, dh2b, dgt, dpp, dg3 = ple_bwd(dh, s["h2"], pt[i], g3, aw, lay, s["wp"])
        gs["norm_ple"][i] = dg3[0]
        buf = mm_tn_into(buf, s["hn3"], dgt, D // N_CHIPS, lay["gate"], False)
        buf = put(buf, lay["proj"], _col_shards(mm_tn(pt[i], dpp)))
        dh1, dh1b, du, a, dg2 = ffn_bwd(dh2, dh2b, s["h1"], s["r"], row(norm_ffn[i]), aw, lay)
        gs["norm_ffn"][i] = dg2[0]
        buf = mm_tn_into(buf, a, dh2b, D, lay["down"], False)
        buf = mm_tn_into(buf, s["hn2"], du, D, lay["up"], True)
        buf = mm_tn_into(buf, s["y"], dh1b, s["y"].shape[1] // N_CHIPS, lay["out"], False)
        if i % 2 == 0:
            do = linear_nt(dh1b, aw, D // N_CHIPS, lay["out"])
            dq, dk, dv = flash_bwd(s["q"], s["k"], s["v"], s["y"], do, s["lse"], seq)
            (dh, hn1, cq, ckv, dqp, dkvp, dlat, dg1, dgq, dgkv, dgqn, dgqr, dgkn, dgkr) = mla_pre_bwd(
                dq, dk, dv, dh1, s["h"], *s["mla_args"])
            gs["norm_mix"][i] = dg1[0]
            gs["mla_q_lora_g"][j] = dgq[0]
            gs["mla_kv_lora_g"][j] = dgkv[0]
            gs["mla_q_nope_g"][j] = dgqn[0]
            gs["mla_q_rope_g"][j] = dgqr[0, :DR]
            gs["mla_k_nope_g"][j] = dgkn[0]
            gs["mla_k_rope_g"][j] = dgkr[0, :DR]
            buf = put(buf, lay["wdn"], mm_tn(hn1, dlat)[:, :LAT])
            buf = put(buf, lay["wuq"], _col_shards(_merge_uq(mm_tn(cq, dqp))))
            buf = put(buf, lay["wukv"], _col_shards(mm_tn(ckv, dkvp)))
        else:
            dh, hn1, dpre, dws, dbs, dlng, dlnb, dg1 = gmlp_bwd(
                dh1, dh1b, s["h"], s["pre"], row(norm_mix[i]), aw, lay, s["ln"][0], s["ln"][1], wm[j], wmt[j],
                bfull[j], tril)
            gs["norm_mix"][i] = dg1[0]
            gs["gmlp_ln_g"][j] = dlng[0]
            gs["gmlp_ln_b"][j] = dlnb[0]
            gs["gmlp_w_s"][j] = dws
            gs["gmlp_b_s"][j] = jnp.sum(dbs.reshape(GC, GG, GD), axis=-1).T
            buf = mm_tn_into(buf, hn1, dpre, D, lay["in"], True)

        own, sums = chip_sum(place, buf, swap_halves(buf))
        sems, sums, land, token = scatter_start(sums, i)
        scatters.append((i, own, sems, sums, land))
    grad_x = dh.reshape(x.shape)

    after = dh
    for i, own, sems, sums, land in scatters:
        _, got = scatter_wait(sems, sums, land, after, i)
        half = final_sum(own, got)
        after = reduced = lax.dynamic_update_slice(share_with_sibling(half), half, (cc * half.shape[0], 0))
        for n, l, row0 in _layer_parts(i)[1]:
            gw[n][l] = reduced[row0:row0 + weights[n][l].size // D].reshape(weights[n].shape[1:])
    grads = {n: jnp.stack(gw[n]) for n in _BIG}

    small_sizes = [weights[n].size if n not in ("gmlp_ln_g", "gmlp_ln_b") else weights[n].shape[0] * GH
                   for n in _SMALL]
    small_rows = -(-sum(small_sizes) // (8 * D)) * 8
    part = _pack_rows([jnp.stack(gs[n]) for n in _SMALL], F32, pad_to=small_rows)
    tot = allreduce_small(part).reshape(-1)
    off = 0
    for n, sz in zip(_SMALL, small_sizes):
        gsum = tot[off:off + sz]
        off += sz
        if n in ("gmlp_ln_g", "gmlp_ln_b"):
            gsum = lax.dynamic_slice_in_dim(gsum.reshape(-1, GH), chip * (GH // N_CHIPS), GH // N_CHIPS, axis=1)
        grads[n] = gsum.reshape(weights[n].shape)

    delta, new_m, new_v = {}, {}, {}
    for n in _BIG:
        w2 = weights[n].reshape(-1, weights[n].shape[-1])
        d, mn, vn = adamw(w2, grads[n].reshape(w2.shape), args["m_" + n].reshape(w2.shape),
                          args["v_" + n].reshape(w2.shape))
        delta[n], new_m[n], new_v[n] = (a.reshape(weights[n].shape) for a in (d, mn, vn))
    own_sizes = [weights[n].size for n in _SMALL]
    own_rows = -(-sum(own_sizes) // (8 * D)) * 8
    packed = [_pack_rows([src[n] for n in _SMALL], F32, pad_to=own_rows)
              for src in (weights, grads, {n: args["m_" + n] for n in _SMALL}, {n: args["v_" + n] for n in _SMALL})]
    outs = adamw(*packed)
    off = 0
    for n, sz in zip(_SMALL, own_sizes):
        for dst, o in zip((delta, new_m, new_v), outs):
            dst[n] = o.reshape(-1)[off:off + sz].reshape(weights[n].shape)
        off += sz

    order = ["norm_mix", "norm_ffn", "norm_ple", "mla_w_down", "mla_q_lora_g", "mla_kv_lora_g", "mla_w_uq",
             "mla_w_ukv", "mla_q_nope_g", "mla_q_rope_g", "mla_k_nope_g", "mla_k_rope_g", "mla_w_out", "gmlp_w_in",
             "gmlp_ln_g", "gmlp_ln_b", "gmlp_w_s", "gmlp_b_s", "gmlp_w_out", "ffn_w_up", "ffn_w_down", "ple_w_gate",
             "ple_w_proj"]
    return (loss, grad_x, *[grads[n] for n in order], *[delta[n] for n in order], *[new_m[n] for n in order],
            *[new_v[n] for n in order])
```

```python
import functools

import jax
import jax.numpy as jnp
from jax import lax
from jax.experimental import pallas as pl
from jax.experimental.pallas import tpu as pltpu

F32 = jnp.float32
BF16 = jnp.bfloat16
MESH = pl.DeviceIdType.MESH

D = 1024
HEADS = 8
DN = 128
DR = 64
QL = 384
KVL = 256
LAT = 704
LATP = 768
DFF = 4096
GH = 2048
GC = 128
GG = 8
GD = 256
PLE = 256
EPS = 1e-6
ROPE_BASE = 10000.0
SM_SCALE = (DN + DR) ** -0.5
N_CHIPS = 4
LANES = 128

ADAM_LR = 0.001
ADAM_B1 = 0.9
ADAM_B2 = 0.999
ADAM_EPS = 1e-08
ADAM_WD = 0.01
ADAM_STEP = 10

TM = 256
TMB = 512
TQ = 512
TQ_FWD = 512
FWD_HEADS = 1
SUM_ROWS = 256
VMEM_LIMIT = 56 * 1024 * 1024


def _cp(*sem):
    return pltpu.CompilerParams(dimension_semantics=sem, vmem_limit_bytes=VMEM_LIMIT)


def _dot(a, b):
    return jnp.dot(a, b, preferred_element_type=F32)


def _dot_nt(a, b):
    return lax.dot_general(a, b, (((1,), (1,)), ((), ())), preferred_element_type=F32)


def _dot_tn(a, b):
    return lax.dot_general(a, b, (((0,), (0,)), ((), ())), preferred_element_type=F32)


def _rms(x, g, n):
    r = lax.rsqrt(jnp.sum(x * x, axis=-1, keepdims=True) * (1.0 / n) + EPS)
    xhat = x * r
    return xhat * g, xhat, r


def _rms_bwd(dy, g, xhat, r, n):
    dxhat = dy * g
    return r * (dxhat - xhat * (jnp.sum(dxhat * xhat, axis=-1, keepdims=True) * (1.0 / n)))


def _rope(x, c, s):
    return x * c + (pltpu.roll(x, 32, 1) - pltpu.roll(x, 96, 1)) * s


def _rope_t(dy, c, s):
    w = dy * s
    return dy * c + pltpu.roll(w, 96, 1) - pltpu.roll(w, 32, 1)


def _sigmoid(x):
    return 1.0 / (1.0 + jnp.exp(-x))


_GELU_K = 0.7978845608028654
_GELU_C = 0.044715


def _gelu(x):
    return 0.5 * x * (1.0 + jnp.tanh(_GELU_K * (x + _GELU_C * x * x * x)))


def _gelu_and_grad(x):
    x2 = x * x
    t = jnp.tanh(_GELU_K * (x + _GELU_C * x2 * x))
    half = 0.5 * (1.0 + t)
    return x * half, half + 0.5 * x * (1.0 - t * t) * (_GELU_K * (1.0 + 3.0 * _GELU_C * x2))


def _acc_rows(ref, val):
    ref[...] += jnp.broadcast_to(jnp.sum(val, axis=0, keepdims=True), ref.shape)


def _row(tm, c):
    return pl.BlockSpec((tm, c), lambda i: (i, 0))


def _const(shape):
    nd = len(shape)
    return pl.BlockSpec(shape, lambda i: (0,) * nd, pipeline_mode=pl.Buffered(1))


def _wblk(rows, row0):
    assert row0 % rows == 0, (rows, row0)
    return pl.BlockSpec((N_CHIPS, rows, D), lambda i: (0, row0 // rows, 0), pipeline_mode=pl.Buffered(1))


def _rows_joined(w_ref):
    return w_ref[...].reshape(N_CHIPS * w_ref.shape[1], D)


def _sds(shape, dtype):
    return jax.ShapeDtypeStruct(shape, dtype)


def mixffn_fwd(h, y, allw, lay, g2):
    t, k = y.shape

    def body(h_ref, y_ref, wo_ref, g_ref, wu_ref, wd_ref, h1_ref, h2_ref, hn_ref, r_ref):
        h1 = h_ref[...] + _dot(y_ref[...], _rows_joined(wo_ref))
        h1_ref[...] = h1
        yn, _, _ = _rms(h1, g_ref[...], D)
        hn = yn.astype(BF16)
        hn_ref[...] = hn
        f = jnp.zeros((TMB, D), F32)
        for c in range(N_CHIPS):
            r = jnp.maximum(_dot(hn, wu_ref[c]), 0.0)
            r_ref[:, c * D:(c + 1) * D] = r.astype(BF16)
            f = f + _dot((r * r).astype(BF16), wd_ref[c])
        h2_ref[...] = h1 + f

    return pl.pallas_call(
        body, name="mixffn_fwd", grid=(t // TMB,),
        in_specs=[_row(TMB, D), _row(TMB, k), _wblk(k // N_CHIPS, lay["out"]), _const((1, D)), _wblk(D, lay["up"]),
                  _wblk(D, lay["down"])],
        out_specs=[_row(TMB, D), _row(TMB, D), _row(TMB, D), _row(TMB, DFF)],
        out_shape=[_sds((t, D), F32), _sds((t, D), F32), _sds((t, D), BF16), _sds((t, DFF), BF16)],
        compiler_params=_cp("parallel"),
    )(h, y, allw, g2, allw, allw)


def ple_fwd(h2, p, g3, allw, lay, wp):
    t = h2.shape[0]

    def body(h_ref, p_ref, g_ref, wg_ref, wp_ref, h3_ref, hn_ref):
        x = h_ref[...]
        yn, _, _ = _rms(x, g_ref[...], D)
        hn = yn.astype(BF16)
        hn_ref[...] = hn
        gt = _dot(hn, _rows_joined(wg_ref))
        pp = _dot(p_ref[...].astype(BF16), wp_ref[...])
        h3_ref[...] = x + _sigmoid(gt) * pp

    return pl.pallas_call(
        body, name="ple_fwd", grid=(t // TMB,),
        in_specs=[_row(TMB, D), _row(TMB, PLE), _const((1, D)), _wblk(D // N_CHIPS, lay["gate"]), _const((PLE, D))],
        out_specs=[_row(TMB, D), _row(TMB, D)],
        out_shape=[_sds((t, D), F32), _sds((t, D), BF16)],
        compiler_params=_cp("parallel"),
    )(h2, p, g3, allw, wp)


def _mla_project(h_ref, g1_ref, wdn_ref, gq_ref, gkv_ref, wuq_ref, wukv_ref):
    x = h_ref[...]
    yn, xhat, rx = _rms(x, g1_ref[...], D)
    hn = yn.astype(BF16)
    lat = _dot(hn, wdn_ref[...])
    cq, cqhat, rq = _rms(lat[:, :QL], gq_ref[...], QL)
    ckv, ckvhat, rkv = _rms(lat[:, QL:QL + KVL], gkv_ref[...], KVL)
    kr_raw = lat[:, QL + KVL:]
    cqb = cq.astype(BF16)
    ckvb = ckv.astype(BF16)
    qp = _dot(cqb, wuq_ref[...])
    kvp = _dot(ckvb, wukv_ref[...])
    return dict(xhat=xhat, rx=rx, hn=hn, cqhat=cqhat, rq=rq, ckvhat=ckvhat, rkv=rkv, kr_raw=kr_raw,
                cqb=cqb, ckvb=ckvb, qp=qp, kvp=kvp)


def mla_pre_fwd(h, g1, wdn, gq, gkv, wuq, wukv, gqn, gqr, gkn, gkr, cos, sin):
    t = h.shape[0]

    def body(h_ref, g1_ref, wdn_ref, gq_ref, gkv_ref, wuq_ref, wukv_ref, gqn_ref, gqr_ref, gkn_ref, gkr_ref,
             c_ref, s_ref, q_ref, k_ref, v_ref):
        m = _mla_project(h_ref, g1_ref, wdn_ref, gq_ref, gkv_ref, wuq_ref, wukv_ref)
        c = c_ref[...]
        s = s_ref[...]
        kr, _, _ = _rms(m["kr_raw"], gkr_ref[...], DR)
        krb = _rope(kr, c, s).astype(BF16)
        for hd in range(HEADS):
            qn, _, _ = _rms(m["qp"][:, hd * DN:(hd + 1) * DN], gqn_ref[...], DN)
            qr, _, _ = _rms(m["qp"][:, D + hd * LANES:D + (hd + 1) * LANES], gqr_ref[...], DR)
            q_ref[hd, :, 0:DN] = (qn * SM_SCALE).astype(BF16)
            q_ref[hd, :, DN:2 * DN] = (_rope(qr, c, s) * SM_SCALE).astype(BF16)
            kn, _, _ = _rms(m["kvp"][:, hd * 2 * DN:hd * 2 * DN + DN], gkn_ref[...], DN)
            k_ref[hd, :, 0:DN] = kn.astype(BF16)
            k_ref[hd, :, DN:2 * DN] = krb
            v_ref[hd] = m["kvp"][:, hd * 2 * DN + DN:(hd + 1) * 2 * DN].astype(BF16)

    hb = lambda w: pl.BlockSpec((HEADS, TM, w), lambda i: (0, i, 0))
    return pl.pallas_call(
        body, name="mla_pre_fwd", grid=(t // TM,),
        in_specs=[_row(TM, D), _const((1, D)), _const((D, LATP)), _const((1, QL)), _const((1, KVL)),
                  _const((QL, 2 * D)), _const((KVL, 2 * D)), _const((1, LANES)), _const((1, LANES)),
                  _const((1, LANES)), _const((1, LANES)), _row(TM, LANES), _row(TM, LANES)],
        out_specs=[hb(2 * DN), hb(2 * DN), hb(DN)],
        out_shape=[_sds((HEADS, t, 2 * DN), BF16), _sds((HEADS, t, 2 * DN), BF16), _sds((HEADS, t, DN), BF16)],
        compiler_params=_cp("parallel"),
    )(h, g1, wdn, gq, gkv, wuq, wukv, gqn, gqr, gkn, gkr, cos, sin)


def _diagonal_mask(n=TQ):
    return lax.broadcasted_iota(jnp.int32, (n, n), 1) <= lax.broadcasted_iota(jnp.int32, (n, n), 0)


def flash_fwd(q, k, v, seq):
    t = q.shape[1]
    nb = t // seq
    tq = TQ_FWD
    nq = seq // tq
    hp = FWD_HEADS

    def body(q_ref, k_ref, v_ref, o_ref, lse_ref):
        qi = pl.program_id(2)
        qs = [q_ref[a] for a in range(hp)]

        def step(j, carry, diagonal=False):
            rows = pl.ds(pl.multiple_of(j * tq, tq), tq)
            out = []
            for a in range(hp):
                m, l, acc = carry[a]
                s = _dot_nt(qs[a], k_ref[a, rows, :])
                if diagonal:
                    s = jnp.where(_diagonal_mask(tq), s, -1e30)
                m_new = jnp.maximum(m, jnp.max(s, axis=-1, keepdims=True))
                p = jnp.exp(s - m_new)
                alpha = jnp.exp(m - m_new)
                l = alpha * l + jnp.sum(p, axis=-1, keepdims=True)
                acc = alpha * acc + _dot(p.astype(BF16), v_ref[a, rows, :])
                out.append((m_new, l, acc))
            return tuple(out)

        one = (jnp.full((tq, 1), -1e30, F32), jnp.zeros((tq, 1), F32), jnp.zeros((tq, DN), F32))
        done = step(qi, lax.fori_loop(0, qi, step, (one,) * hp), diagonal=True)
        for a, (m, l, acc) in enumerate(done):
            o_ref[:, a * DN:(a + 1) * DN] = (acc / l).astype(BF16)
            lse_ref[a] = m + jnp.log(l)

    return pl.pallas_call(
        body, name="flash_fwd", grid=(nb, HEADS // hp, nq),
        in_specs=[pl.BlockSpec((hp, tq, 2 * DN), lambda b, h, i: (h, b * nq + i, 0)),
                  pl.BlockSpec((hp, seq, 2 * DN), lambda b, h, i: (h, b, 0)),
                  pl.BlockSpec((hp, seq, DN), lambda b, h, i: (h, b, 0))],
        out_specs=[pl.BlockSpec((tq, hp * DN), lambda b, h, i: (b * nq + i, h)),
                   pl.BlockSpec((hp, tq, 1), lambda b, h, i: (h, b * nq + i, 0))],
        out_shape=[_sds((t, HEADS * DN), BF16), _sds((HEADS, t, 1), F32)],
        compiler_params=_cp("parallel", "parallel", "arbitrary"),
    )(q, k, v)


def _gmlp_in(hn, win_ref):
    pre = [_dot(hn, win_ref[c]) for c in range(N_CHIPS)]
    return jnp.concatenate(pre[:2], axis=1), jnp.concatenate(pre[2:], axis=1)


def gmlp_fwd(h, g1, allw, lay, lng, lnb, wm, bfull):
    t = h.shape[0]

    def body(h_ref, g1_ref, win_ref, lng_ref, lnb_ref, wm_ref, b_ref, y_ref, pre_ref):
        yn, _, _ = _rms(h_ref[...], g1_ref[...], D)
        pre_u, pre_v = _gmlp_in(yn.astype(BF16), win_ref)
        pre_ref[:, :GH] = pre_u.astype(BF16)
        pre_ref[:, GH:] = pre_v.astype(BF16)
        u = _gelu(pre_u)
        v = _gelu(pre_v)
        xc = v - jnp.mean(v, axis=-1, keepdims=True)
        rs = lax.rsqrt(jnp.mean(xc * xc, axis=-1, keepdims=True) + EPS)
        vnb = (xc * rs * lng_ref[...] + lnb_ref[...]).astype(BF16)
        for ch in range(TM // GC):
            rows = slice(ch * GC, (ch + 1) * GC)
            for g in range(GG):
                cols = slice(g * GD, (g + 1) * GD)
                sv = _dot(wm_ref[g], vnb[rows, cols]) + b_ref[:, cols]
                y_ref[rows, cols] = (u[rows, cols] * sv).astype(BF16)

    return pl.pallas_call(
        body, name="gmlp_fwd", grid=(t // TM,),
        in_specs=[_row(TM, D), _const((1, D)), _wblk(D, lay["in"]), _const((1, GH)), _const((1, GH)),
                  _const((GG, GC, GC)), _const((GC, GH))],
        out_specs=[_row(TM, GH), _row(TM, 2 * GH)],
        out_shape=[_sds((t, GH), BF16), _sds((t, 2 * GH), BF16)],
        compiler_params=_cp("parallel"),
    )(h, g1, allw, lng, lnb, wm, bfull)


def loss_head(h, tgt):
    t = h.shape[0]

    def body(h_ref, t_ref, dh_ref, loss_ref):
        @pl.when(pl.program_id(0) == 0)
        def _():
            loss_ref[...] = jnp.zeros_like(loss_ref)

        e = h_ref[...] - t_ref[...]
        dh_ref[...] = e * (1.0 / D)
        part = jnp.sum(jnp.sum(e * e, axis=-1, keepdims=True), axis=0, keepdims=True) * (0.5 / D)
        loss_ref[...] += jnp.broadcast_to(part, loss_ref.shape)

    return pl.pallas_call(
        body, name="loss_head", grid=(t // TMB,),
        in_specs=[_row(TMB, D), _row(TMB, D)],
        out_specs=[_row(TMB, D), _const((8, LANES))],
        out_shape=[_sds((t, D), F32), _sds((8, LANES), F32)],
        compiler_params=_cp("arbitrary"),
    )(h, tgt)


def _zero_at_first_step(*refs):
    @pl.when(pl.program_id(0) == 0)
    def _():
        for r in refs:
            r[...] = jnp.zeros_like(r)


def ple_bwd(dh3, h2, p, g3, allw, lay, wp):
    t = h2.shape[0]

    def body(dh_ref, h_ref, p_ref, g_ref, wg_ref, wp_ref, dh2_ref, dh2b_ref, dgt_ref, dpp_ref, dg_ref):
        _zero_at_first_step(dg_ref)
        dh3v = dh_ref[...]
        x = h_ref[...]
        g = g_ref[...]
        wg = _rows_joined(wg_ref)
        yn, xhat, r = _rms(x, g, D)
        gt = _dot(yn.astype(BF16), wg)
        pp = _dot(p_ref[...].astype(BF16), wp_ref[...])
        sg = _sigmoid(gt)
        dgt = (dh3v * pp * sg * (1.0 - sg)).astype(BF16)
        dgt_ref[...] = dgt
        dpp_ref[...] = (dh3v * sg).astype(BF16)
        dhn = _dot_nt(dgt, wg)
        _acc_rows(dg_ref, dhn * xhat)
        dh2 = dh3v + _rms_bwd(dhn, g, xhat, r, D)
        dh2_ref[...] = dh2
        dh2b_ref[...] = dh2.astype(BF16)

    return pl.pallas_call(
        body, name="ple_bwd", grid=(t // TMB,),
        in_specs=[_row(TMB, D), _row(TMB, D), _row(TMB, PLE), _const((1, D)), _wblk(D // N_CHIPS, lay["gate"]),
                  _const((PLE, D))],
        out_specs=[_row(TMB, D), _row(TMB, D), _row(TMB, D), _row(TMB, D), _const((8, D))],
        out_shape=[_sds((t, D), F32), _sds((t, D), BF16), _sds((t, D), BF16), _sds((t, D), BF16), _sds((8, D), F32)],
        compiler_params=_cp("arbitrary"),
    )(dh3, h2, p, g3, allw, wp)


def ffn_bwd(dh2, dh2b, h1, r, g2, allw, lay):
    t = h1.shape[0]

    def body(dh_ref, dhb_ref, h_ref, r_ref, g_ref, wu_ref, wd_ref, dh1_ref, dh1b_ref, du_ref, a_ref, dg_ref):
        _zero_at_first_step(dg_ref)
        dhb = dhb_ref[...]
        g = g_ref[...]
        _, xhat, rr = _rms(h_ref[...], g, D)
        dhn = jnp.zeros((TM, D), F32)
        for c in range(N_CHIPS):
            cs = slice(c * D, (c + 1) * D)
            rc = r_ref[:, cs].astype(F32)
            a_ref[:, cs] = (rc * rc).astype(BF16)
            da = _dot_nt(dhb, wd_ref[c])
            du = (da * (2.0 * rc)).astype(BF16)
            du_ref[:, cs] = du
            dhn = dhn + _dot_nt(du, wu_ref[c])
        _acc_rows(dg_ref, dhn * xhat)
        dh1 = dh_ref[...] + _rms_bwd(dhn, g, xhat, rr, D)
        dh1_ref[...] = dh1
        dh1b_ref[...] = dh1.astype(BF16)

    return pl.pallas_call(
        body, name="ffn_bwd", grid=(t // TM,),
        in_specs=[_row(TM, D), _row(TM, D), _row(TM, D), _row(TM, DFF), _const((1, D)), _wblk(D, lay["up"]),
                  _wblk(D, lay["down"])],
        out_specs=[_row(TM, D), _row(TM, D), _row(TM, DFF), _row(TM, DFF), _const((8, D))],
        out_shape=[_sds((t, D), F32), _sds((t, D), BF16), _sds((t, DFF), BF16), _sds((t, DFF), BF16),
                   _sds((8, D), F32)],
        compiler_params=_cp("arbitrary"),
    )(dh2, dh2b, h1, r, g2, allw, allw)


def linear_nt(a, allw, rows, row0):
    t = a.shape[0]
    k = N_CHIPS * rows

    def body(a_ref, w_ref, o_ref):
        o_ref[...] = _dot_nt(a_ref[...], _rows_joined(w_ref)).astype(BF16)

    return pl.pallas_call(
        body, name="linear_nt", grid=(t // TMB,),
        in_specs=[_row(TMB, D), _wblk(rows, row0)],
        out_specs=_row(TMB, k),
        out_shape=_sds((t, k), BF16),
        compiler_params=_cp("parallel"),
    )(a, allw)


def flash_bwd(q, k, v, o, do, lse, seq):
    t = q.shape[1]
    nb = t // seq
    nq = seq // TQ

    def body(q_ref, k_ref, v_ref, o_ref, do_ref, lse_ref, dq_ref, dk_ref, dv_ref):
        kj = pl.program_id(2)

        @pl.when(kj == 0)
        def _():
            dq_ref[...] = jnp.zeros_like(dq_ref)

        kv = k_ref[0]
        vv = v_ref[0]

        def step(i, carry, diagonal=False):
            dk, dv = carry
            rows = pl.ds(pl.multiple_of(i * TQ, TQ), TQ)
            qv = q_ref[0, rows, :]
            dov = do_ref[rows, :]
            delta = jnp.sum(dov.astype(F32) * o_ref[rows, :].astype(F32), axis=-1, keepdims=True)
            s = _dot_nt(qv, kv)
            if diagonal:
                s = jnp.where(_diagonal_mask(), s, -1e30)
            p = jnp.exp(s - lse_ref[0, rows, :])
            dp = _dot_nt(dov, vv)
            ds = (p * (dp - delta)).astype(BF16)
            dv = dv + _dot_tn(p.astype(BF16), dov)
            dk = dk + _dot_tn(ds, qv)
            dq_ref[0, rows, :] += _dot(ds, kv)
            return dk, dv

        init = (jnp.zeros((TQ, 2 * DN), F32), jnp.zeros((TQ, DN), F32))
        dk, dv = lax.fori_loop(kj + 1, nq, step, step(kj, init, diagonal=True))
        dk_ref[0] = dk
        dv_ref[0] = dv

    return pl.pallas_call(
        body, name="flash_bwd", grid=(nb, HEADS, nq),
        in_specs=[pl.BlockSpec((1, seq, 2 * DN), lambda b, h, j: (h, b, 0)),
                  pl.BlockSpec((1, TQ, 2 * DN), lambda b, h, j: (h, b * nq + j, 0)),
                  pl.BlockSpec((1, TQ, DN), lambda b, h, j: (h, b * nq + j, 0)),
                  pl.BlockSpec((seq, DN), lambda b, h, j: (b, h)),
                  pl.BlockSpec((seq, DN), lambda b, h, j: (b, h)),
                  pl.BlockSpec((1, seq, 1), lambda b, h, j: (h, b, 0))],
        out_specs=[pl.BlockSpec((1, seq, 2 * DN), lambda b, h, j: (h, b, 0)),
                   pl.BlockSpec((1, TQ, 2 * DN), lambda b, h, j: (h, b * nq + j, 0)),
                   pl.BlockSpec((1, TQ, DN), lambda b, h, j: (h, b * nq + j, 0))],
        out_shape=[_sds((HEADS, t, 2 * DN), F32), _sds((HEADS, t, 2 * DN), F32), _sds((HEADS, t, DN), F32)],
        compiler_params=_cp("parallel", "parallel", "arbitrary"),
    )(q, k, v, o, do, lse)


def mla_pre_bwd(dq, dk, dv, dh1, h, g1, wdn, gq, gkv, wuq, wukv, gqn, gqr, gkn, gkr, cos, sin):
    t = h.shape[0]

    def body(dq_ref, dk_ref, dv_ref, dh1_ref, h_ref, g1_ref, wdn_ref, gq_ref, gkv_ref, wuq_ref, wukv_ref,
             gqn_ref, gqr_ref, gkn_ref, gkr_ref, c_ref, s_ref,
             dh_ref, hn_ref, cq_ref, ckv_ref, dqp_ref, dkvp_ref, dlat_ref,
             dg1_ref, dgq_ref, dgkv_ref, dgqn_ref, dgqr_ref, dgkn_ref, dgkr_ref):
        _zero_at_first_step(dg1_ref, dgq_ref, dgkv_ref, dgqn_ref, dgqr_ref, dgkn_ref, dgkr_ref)
        m = _mla_project(h_ref, g1_ref, wdn_ref, gq_ref, gkv_ref, wuq_ref, wukv_ref)
        hn_ref[...] = m["hn"]
        cq_ref[...] = m["cqb"]
        ckv_ref[...] = m["ckvb"]
        c = c_ref[...]
        s = s_ref[...]
        gqn = gqn_ref[...]
        gqr = gqr_ref[...]
        gkn = gkn_ref[...]
        gkr = gkr_ref[...]

        dkr = dk_ref[0, :, DN:2 * DN]
        for hd in range(1, HEADS):
            dkr = dkr + dk_ref[hd, :, DN:2 * DN]
        dkr = _rope_t(dkr, c, s)
        _, krhat, rkr = _rms(m["kr_raw"], gkr, DR)
        _acc_rows(dgkr_ref, dkr * krhat)
        dkr_raw = _rms_bwd(dkr, gkr, krhat, rkr, DR)

        for hd in range(HEADS):
            ncols = slice(hd * DN, (hd + 1) * DN)
            _, xh, r = _rms(m["qp"][:, ncols], gqn, DN)
            dqn = dq_ref[hd, :, 0:DN] * SM_SCALE
            _acc_rows(dgqn_ref, dqn * xh)
            dqp_ref[:, ncols] = _rms_bwd(dqn, gqn, xh, r, DN).astype(BF16)

            rcols = slice(D + hd * LANES, D + (hd + 1) * LANES)
            _, xh, r = _rms(m["qp"][:, rcols], gqr, DR)
            dqr = _rope_t(dq_ref[hd, :, DN:2 * DN] * SM_SCALE, c, s)
            _acc_rows(dgqr_ref, dqr * xh)
            dqp_ref[:, rcols] = _rms_bwd(dqr, gqr, xh, r, DR).astype(BF16)

            kcols = slice(hd * 2 * DN, hd * 2 * DN + DN)
            _, xh, r = _rms(m["kvp"][:, kcols], gkn, DN)
            dkn = dk_ref[hd, :, 0:DN]
            _acc_rows(dgkn_ref, dkn * xh)
            dkvp_ref[:, kcols] = _rms_bwd(dkn, gkn, xh, r, DN).astype(BF16)
            dkvp_ref[:, hd * 2 * DN + DN:(hd + 1) * 2 * DN] = dv_ref[hd].astype(BF16)

        dcq = _dot_nt(dqp_ref[...], wuq_ref[...])
        _acc_rows(dgq_ref, dcq * m["cqhat"])
        dlat_q = _rms_bwd(dcq, gq_ref[...], m["cqhat"], m["rq"], QL)
        dckv = _dot_nt(dkvp_ref[...], wukv_ref[...])
        _acc_rows(dgkv_ref, dckv * m["ckvhat"])
        dlat_kv = _rms_bwd(dckv, gkv_ref[...], m["ckvhat"], m["rkv"], KVL)
        dlat = jnp.concatenate([dlat_q, dlat_kv, dkr_raw], axis=1).astype(BF16)
        dlat_ref[...] = dlat
        dhn = _dot_nt(dlat, wdn_ref[...])
        _acc_rows(dg1_ref, dhn * m["xhat"])
        dh_ref[...] = dh1_ref[...] + _rms_bwd(dhn, g1_ref[...], m["xhat"], m["rx"], D)

    hb = lambda w: pl.BlockSpec((HEADS, TM, w), lambda i: (0, i, 0))
    return pl.pallas_call(
        body, name="mla_pre_bwd", grid=(t // TM,),
        in_specs=[hb(2 * DN), hb(2 * DN), hb(DN), _row(TM, D), _row(TM, D), _const((1, D)), _const((D, LATP)),
                  _const((1, QL)), _const((1, KVL)), _const((QL, 2 * D)), _const((KVL, 2 * D)),
                  _const((1, LANES)), _const((1, LANES)), _const((1, LANES)), _const((1, LANES)),
                  _row(TM, LANES), _row(TM, LANES)],
        out_specs=[_row(TM, D), _row(TM, D), _row(TM, QL), _row(TM, KVL), _row(TM, 2 * D), _row(TM, 2 * D),
                   _row(TM, LATP), _const((8, D)), _const((8, QL)), _const((8, KVL)), _const((8, LANES)),
                   _const((8, LANES)), _const((8, LANES)), _const((8, LANES))],
        out_shape=[_sds((t, D), F32), _sds((t, D), BF16), _sds((t, QL), BF16), _sds((t, KVL), BF16),
                   _sds((t, 2 * D), BF16), _sds((t, 2 * D), BF16), _sds((t, LATP), BF16),
                   _sds((8, D), F32), _sds((8, QL), F32), _sds((8, KVL), F32), _sds((8, LANES), F32),
                   _sds((8, LANES), F32), _sds((8, LANES), F32), _sds((8, LANES), F32)],
        compiler_params=_cp("arbitrary"),
    )(dq, dk, dv, dh1, h, g1, wdn, gq, gkv, wuq, wukv, gqn, gqr, gkn, gkr, cos, sin)


def gmlp_bwd(dh1, dh1b, h, pre, g1, allw, lay, lng, lnb, wm, wmt, bfull, tril):
    t = h.shape[0]

    def body(dh1_ref, dh1b_ref, h_ref, pre_ref, g1_ref, win_ref, lng_ref, lnb_ref, wm_ref, wmt_ref, b_ref,
             wout_ref, tril_ref, dh_ref, hn_ref, dpre_ref, dws_ref, dbs_ref, dlng_ref, dlnb_ref, dg1_ref,
             dvn_s):
        _zero_at_first_step(dws_ref, dbs_ref, dlng_ref, dlnb_ref, dg1_ref)
        g1 = g1_ref[...]
        yn, xhat, rx = _rms(h_ref[...], g1, D)
        hn_ref[...] = yn.astype(BF16)
        dy = _dot_nt(dh1b_ref[...], _rows_joined(wout_ref))
        pre_u = pre_ref[:, :GH].astype(F32)
        pre_v = pre_ref[:, GH:].astype(F32)
        u, gg_u = _gelu_and_grad(pre_u)
        v, gg_v = _gelu_and_grad(pre_v)
        xc = v - jnp.mean(v, axis=-1, keepdims=True)
        rs = lax.rsqrt(jnp.mean(xc * xc, axis=-1, keepdims=True) + EPS)
        vhat = xc * rs
        lng = lng_ref[...]
        vnb = (vhat * lng + lnb_ref[...]).astype(BF16)
        dsv = dy * u
        dsvb = dsv.astype(BF16)
        tril_m = tril_ref[...]
        for ch in range(TM // GC):
            rows = slice(ch * GC, (ch + 1) * GC)
            dbs_ref[...] += dsv[rows, :]
            for g in range(GG):
                cols = slice(g * GD, (g + 1) * GD)
                sv = _dot(wm_ref[g], vnb[rows, cols]) + b_ref[:, cols]
                dpre_ref[rows, cols] = (dy[rows, cols] * sv * gg_u[rows, cols]).astype(BF16)
                dvn_s[rows, cols] = _dot(wmt_ref[g], dsvb[rows, cols])
                dws_ref[g] += _dot_nt(dsvb[rows, cols], vnb[rows, cols]) * tril_m
        dvn = dvn_s[...]
        _acc_rows(dlng_ref, dvn * vhat)
        _acc_rows(dlnb_ref, dvn)
        dvhat = dvn * lng
        dv = rs * (dvhat - jnp.mean(dvhat, axis=-1, keepdims=True)
                   - vhat * jnp.mean(dvhat * vhat, axis=-1, keepdims=True))
        dpre_v = (dv * gg_v).astype(BF16)
        dpre_ref[:, GH:] = dpre_v
        dhn = _dot_nt(dpre_ref[:, 0:D], win_ref[0])
        for c in range(1, N_CHIPS):
            dhn = dhn + _dot_nt(dpre_ref[:, c * D:(c + 1) * D], win_ref[c])
        _acc_rows(dg1_ref, dhn * xhat)
        dh_ref[...] = dh1_ref[...] + _rms_bwd(dhn, g1, xhat, rx, D)

    return pl.pallas_call(
        body, name="gmlp_bwd", grid=(t // TM,),
        in_specs=[_row(TM, D), _row(TM, D), _row(TM, D), _row(TM, 2 * GH), _const((1, D)), _wblk(D, lay["in"]),
                  _const((1, GH)), _const((1, GH)), _const((GG, GC, GC)), _const((GG, GC, GC)), _const((GC, GH)),
                  _wblk(GH // N_CHIPS, lay["out"]), _const((GC, GC))],
        out_specs=[_row(TM, D), _row(TM, D), _row(TM, 2 * GH), _const((GG, GC, GC)), _const((GC, GH)),
                   _const((8, GH)), _const((8, GH)), _const((8, D))],
        out_shape=[_sds((t, D), F32), _sds((t, D), BF16), _sds((t, 2 * GH), BF16), _sds((GG, GC, GC), F32),
                   _sds((GC, GH), F32), _sds((8, GH), F32), _sds((8, GH), F32), _sds((8, D), F32)],
        scratch_shapes=[pltpu.VMEM((TM, GH), F32)],
        compiler_params=_cp("arbitrary"),
    )(dh1, dh1b, h, pre, g1, allw, lng, lnb, wm, wmt, bfull, allw, tril)


def _token_step(t):
    return 1024 if t % 1024 == 0 else 512


def mm_tn(a, b):
    t, k = a.shape
    n = b.shape[1]
    tk = min(k, 1024)
    tn = min(n, 1024)
    tt = _token_step(t)

    def body(a_ref, b_ref, o_ref):
        @pl.when(pl.program_id(2) == 0)
        def _():
            o_ref[...] = jnp.zeros_like(o_ref)

        o_ref[...] += _dot_tn(a_ref[...].astype(BF16), b_ref[...].astype(BF16))

    return pl.pallas_call(
        body, name="mm_tn", grid=(k // tk, n // tn, t // tt),
        in_specs=[pl.BlockSpec((tt, tk), lambda i, j, s: (s, i)), pl.BlockSpec((tt, tn), lambda i, j, s: (s, j))],
        out_specs=pl.BlockSpec((tk, tn), lambda i, j, s: (i, j)), out_shape=_sds((k, n), F32),
        compiler_params=_cp("parallel", "parallel", "arbitrary"),
    )(a, b)


def mm_tn_into(buf, a, b, rows, row0, col_sharded):
    t = a.shape[0]
    tt = _token_step(t)
    assert row0 % rows == 0 and a.shape[1] == (rows if col_sharded else N_CHIPS * rows), (rows, row0, a.shape)
    assert b.shape[1] == (N_CHIPS * D if col_sharded else D), b.shape
    grid = (1, N_CHIPS, t // tt) if col_sharded else (N_CHIPS, 1, t // tt)

    def body(buf_ref, a_ref, b_ref, o_ref):
        del buf_ref

        @pl.when(pl.program_id(2) == 0)
        def _():
            o_ref[...] = jnp.zeros_like(o_ref)

        o_ref[...] += _dot_tn(a_ref[...].astype(BF16), b_ref[...].astype(BF16))

    return pl.pallas_call(
        body, name="mm_tn_into", grid=grid,
        in_specs=[_ANY, pl.BlockSpec((tt, rows), lambda i, j, s: (s, i)), pl.BlockSpec((tt, D), lambda i, j, s: (s, j))],
        out_specs=pl.BlockSpec((None, rows, D), lambda i, j, s: (i + j, row0 // rows, 0)),
        out_shape=_sds(buf.shape, F32), input_output_aliases={0: 0},
        compiler_params=_cp("parallel", "parallel", "arbitrary"),
    )(buf, a, b)


def adamw(w, g, m, v):
    rows, cols = w.shape
    tr = rows if rows <= 512 else next(r for r in (512, 384, 256, 128) if rows % r == 0)
    c1 = 1.0 - ADAM_B1 ** ADAM_STEP
    c2 = 1.0 - ADAM_B2 ** ADAM_STEP

    def body(w_ref, g_ref, m_ref, v_ref, d_ref, mo_ref, vo_ref):
        gv = g_ref[...]
        mn = ADAM_B1 * m_ref[...] + (1.0 - ADAM_B1) * gv
        vn = ADAM_B2 * v_ref[...] + (1.0 - ADAM_B2) * (gv * gv)
        mo_ref[...] = mn
        vo_ref[...] = vn
        d_ref[...] = -ADAM_LR * ((mn / c1) / (jnp.sqrt(vn / c2) + ADAM_EPS) + ADAM_WD * w_ref[...])

    spec = pl.BlockSpec((tr, cols), lambda i: (i, 0))
    return pl.pallas_call(
        body, name="adamw", grid=(rows // tr,),
        in_specs=[spec] * 4, out_specs=[spec] * 3, out_shape=[_sds((rows, cols), F32)] * 3,
        compiler_params=_cp("parallel"),
    )(w, g, m, v)


def _place():
    return lax.axis_index("x"), lax.axis_index("y"), lax.axis_index("c")


def _other_chips(x, y):
    return [(1 - x, y), (x, 1 - y), (1 - x, 1 - y)]


_ANY = pl.BlockSpec(memory_space=pl.ANY)


_HBM = pl.BlockSpec(memory_space=pltpu.HBM)
_SEM = pl.BlockSpec(memory_space=pltpu.SEMAPHORE)
_EFFECT = pltpu.SideEffectType.DATAFLOW_SIDE_EFFECTING
N_ICI = 3


def _exchange_start(name, src, land, copies, n):
    def body(src_ref, land_ref, *outs):
        sems, token = outs[:2 * n], outs[-1]
        for j, (s, d, to) in enumerate(copies(src_ref, land_ref, _place())):
            pltpu.make_async_remote_copy(src_ref=s, dst_ref=d, send_sem=sems[j], recv_sem=sems[n + j],
                                         device_id=to, device_id_type=MESH).start()
        token[...] = jnp.zeros_like(token)

    sem = pltpu.SemaphoreType.DMA(())
    outs = pl.pallas_call(
        body, name=name,
        out_shape=(sem,) * (2 * n) + (pltpu.HBM(src.shape, src.dtype), pltpu.HBM(land.shape, land.dtype),
                                      _sds((8, LANES), F32)),
        in_specs=(_HBM, _HBM),
        out_specs=(_SEM,) * (2 * n) + (_HBM, _HBM, pl.BlockSpec(memory_space=pltpu.VMEM)),
        input_output_aliases={0: 2 * n, 1: 2 * n + 1},
        compiler_params=pltpu.CompilerParams(has_side_effects=_EFFECT),
    )(pltpu.with_memory_space_constraint(src, pltpu.HBM), pltpu.with_memory_space_constraint(land, pltpu.HBM))
    return outs[:2 * n], outs[2 * n], outs[2 * n + 1], outs[-1]


def _exchange_wait(name, sems, src, land, after, arrivals):
    n = len(sems) // 2

    def body(src_ref, land_ref, *rest):
        sems = rest[:2 * n]
        for j, (s, d) in enumerate(arrivals(src_ref, land_ref, _place())):
            cp = pltpu.make_async_remote_copy(src_ref=s, dst_ref=d, send_sem=sems[j], recv_sem=sems[n + j],
                                              device_id=_place(), device_id_type=MESH)
            cp.wait_send()
            cp.wait_recv()

    return pl.pallas_call(
        body, name=name, out_shape=(pltpu.HBM(src.shape, src.dtype), pltpu.HBM(land.shape, land.dtype)),
        in_specs=(_HBM, _HBM) + (_SEM,) * (2 * n) + (_ANY,), out_specs=(_HBM, _HBM),
        input_output_aliases={0: 0, 1: 1},
        compiler_params=pltpu.CompilerParams(has_side_effects=_EFFECT),
    )(src, land, *sems, after)


def _halves(c, hh):
    return pl.ds(pl.multiple_of(c * hh, 16), hh), pl.ds(pl.multiple_of((1 - c) * hh, 16), hh)


def gather_start(land, tag):
    _, rr, _ = land.shape
    assert rr % 32 == 0, rr

    def copies(_, land_ref, place):
        x, y, c = place
        mine = land_ref.at[2 * x + y, _halves(c, rr // 2)[0]]
        return [(mine, mine, (cx, cy, c)) for cx, cy in _other_chips(x, y)]

    return _exchange_start(f"gather_start_{tag}", jnp.zeros((8, LANES), F32), land, copies, N_ICI)


def gather_wait(sems, src, land, after, tag):
    def arrivals(_, land_ref, place):
        x, y, c = place
        half = _halves(c, land.shape[1] // 2)[0]
        return [(land_ref.at[2 * x + y, half], land_ref.at[2 * cx + cy, half]) for cx, cy in _other_chips(x, y)]

    return _exchange_wait(f"gather_wait_{tag}", sems, src, land, after, arrivals)


def pass_start(land, tag):
    def copies(_, land_ref, place):
        x, y, c = place
        half = _halves(c, land.shape[1] // 2)[0]
        return [(land_ref.at[2 * cx + cy, half], land_ref.at[2 * cx + cy, half], (x, y, 1 - c))
                for cx, cy in _other_chips(x, y)]

    return _exchange_start(f"pass_start_{tag}", jnp.zeros((8, LANES), F32), land, copies, N_ICI)


def pass_wait(sems, src, land, after, tag):
    def arrivals(_, land_ref, place):
        x, y, c = place
        mine, other = _halves(c, land.shape[1] // 2)
        return [(land_ref.at[2 * cx + cy, mine], land_ref.at[2 * cx + cy, other]) for cx, cy in _other_chips(x, y)]

    return _exchange_wait(f"pass_wait_{tag}", sems, src, land, after, arrivals)


def swap_start(g, tag):
    _, rr, cc = g.shape

    def copies(g_ref, got_ref, place):
        x, y, c = place
        other = _halves(c, rr // 2)[1]
        return [(g_ref.at[k, other], got_ref.at[k], (x, y, 1 - c)) for k in range(N_CHIPS)]

    return _exchange_start(f"swap_start_{tag}", g, lax.empty((N_CHIPS, rr // 2, cc), g.dtype), copies, N_CHIPS)


def swap_wait(sems, g, got, after, tag):
    def arrivals(g_ref, got_ref, place):
        other = _halves(place[2], g.shape[1] // 2)[1]
        return [(g_ref.at[k, other], got_ref.at[k]) for k in range(N_CHIPS)]

    return _exchange_wait(f"swap_wait_{tag}", sems, g, got, after, arrivals)


def chip_sum(place, g32, got):
    _, rr, cc = g32.shape
    hh = rr // 2
    tr = SUM_ROWS
    assert rr % 2 == 0 and hh % tr == 0, (rr, tr)
    nb = hh // tr

    def body(place_ref, g_ref, got_ref, own_ref, all_ref):
        s = g_ref[...] + got_ref[...].astype(F32)
        all_ref[...] = s.astype(BF16)
        own_ref[...] = g_ref[place_ref[1]] + got_ref[place_ref[1]].astype(F32)

    return pl.pallas_call(
        body, name="chip_sum",
        grid_spec=pltpu.PrefetchScalarGridSpec(
            num_scalar_prefetch=1, grid=(nb,),
            in_specs=[pl.BlockSpec((N_CHIPS, tr, cc), lambda i, pr: (0, pr[0] * nb + i, 0)),
                      pl.BlockSpec((N_CHIPS, tr, cc), lambda i, pr: (0, i, 0))],
            out_specs=[pl.BlockSpec((tr, cc), lambda i, pr: (i, 0)),
                       pl.BlockSpec((N_CHIPS, tr, cc), lambda i, pr: (0, i, 0))]),
        out_shape=[_sds((hh, cc), F32), _sds((N_CHIPS, hh, cc), BF16)],
        compiler_params=_cp("parallel"),
    )(place, g32, got)


def _scatter_copies(s_ref, land_ref, place):
    x, y, c = place
    return [(s_ref.at[2 * cx + cy], land_ref.at[j], (cx, cy, c)) for j, (cx, cy) in enumerate(_other_chips(x, y))]


def scatter_start(s, tag):
    return _exchange_start(f"scatter_start_{tag}", s, lax.empty((N_ICI,) + s.shape[1:], s.dtype), _scatter_copies, N_ICI)


def scatter_wait(sems, s, land, after, tag):
    return _exchange_wait(f"scatter_wait_{tag}", sems, s, land, after,
                          lambda s_ref, land_ref, place: [(a, b) for a, b, _ in _scatter_copies(s_ref, land_ref, place)])


def final_sum(own, got):
    hh, cc = own.shape
    tr = SUM_ROWS
    assert hh % tr == 0, (hh, tr)

    def body(own_ref, got_ref, o_ref):
        o_ref[...] = ((own_ref[...] + got_ref[0].astype(F32)) + got_ref[1].astype(F32)) + got_ref[2].astype(F32)

    return pl.pallas_call(
        body, name="final_sum", grid=(hh // tr,),
        in_specs=[pl.BlockSpec((tr, cc), lambda i: (i, 0)), pl.BlockSpec((3, tr, cc), lambda i: (0, i, 0))],
        out_specs=pl.BlockSpec((tr, cc), lambda i: (i, 0)),
        out_shape=_sds((hh, cc), F32),
        compiler_params=_cp("parallel"),
    )(own, got)


def share_with_sibling(f):
    hh, cc = f.shape

    def body(f_ref, o_ref, send_sem, recv_sem):
        x, y, c = _place()
        mine_half = pl.ds(pl.multiple_of(c * hh, 8), hh)
        cp = pltpu.make_async_remote_copy(src_ref=f_ref, dst_ref=o_ref.at[mine_half], send_sem=send_sem,
                                          recv_sem=recv_sem, device_id=(x, y, 1 - c), device_id_type=MESH)
        cp.start()
        cp.wait()

    return pl.pallas_call(
        body, name="share_with_sibling", in_specs=[_ANY], out_specs=_ANY,
        out_shape=_sds((2 * hh, cc), f.dtype),
        scratch_shapes=[pltpu.SemaphoreType.DMA, pltpu.SemaphoreType.DMA],
    )(f)


def allreduce_small(part):
    rr, cc = part.shape
    n_dev = 8

    def body(x_ref, all_ref, sum_ref, send_sems, recv_sems, local_sem):
        x, y, c = _place()
        me, sibling = (x, y, c), (x, y, 1 - c)
        chips = _other_chips(x, y)

        def rows(px, py, pc):
            return all_ref.at[pl.ds(pl.multiple_of((4 * px + 2 * py + pc) * rr, 8), rr), :]

        def copy(j, block, to, src=None):
            return pltpu.make_async_remote_copy(src_ref=rows(*block) if src is None else src, dst_ref=rows(*block),
                                                send_sem=send_sems.at[j], recv_sem=recv_sems.at[j],
                                                device_id=to, device_id_type=MESH)

        mine = pltpu.make_async_copy(x_ref, rows(*me), local_sem)
        mine.start()
        first = [copy(0, me, sibling, src=x_ref)]
        first += [copy(1 + j, me, (*chip, c), src=x_ref) for j, chip in enumerate(chips)]
        for cp in first:
            cp.start()
        passed = [copy(4 + j, (*chip, c), sibling) for j, chip in enumerate(chips)]
        for j, chip in enumerate(chips):
            copy(1 + j, (*chip, c), me).wait_recv()
            passed[j].start()
        copy(0, sibling, me).wait_recv()
        for j, chip in enumerate(chips):
            copy(4 + j, (*chip, 1 - c), me).wait_recv()
        for cp in first + passed:
            cp.wait_send()
        mine.wait()
        acc = all_ref[0:rr, :]
        for d in range(1, n_dev):
            acc = acc + all_ref[d * rr:(d + 1) * rr, :]
        sum_ref[...] = acc

    vm = pl.BlockSpec(memory_space=pltpu.VMEM)
    return pl.pallas_call(
        body, name="allreduce_small", in_specs=[vm], out_specs=[vm, vm],
        out_shape=[_sds((n_dev * rr, cc), F32), _sds((rr, cc), F32)],
        scratch_shapes=[pltpu.SemaphoreType.DMA((7,)), pltpu.SemaphoreType.DMA((7,)), pltpu.SemaphoreType.DMA],
        compiler_params=pltpu.CompilerParams(vmem_limit_bytes=VMEM_LIMIT),
    )(part)[1]


_BIG = ["mla_w_down", "mla_w_uq", "mla_w_ukv", "mla_w_out", "gmlp_w_in", "gmlp_w_out", "ffn_w_up", "ffn_w_down",
        "ple_w_gate", "ple_w_proj"]
_SMALL = ["norm_mix", "norm_ffn", "norm_ple", "mla_q_lora_g", "mla_kv_lora_g", "mla_q_nope_g", "mla_q_rope_g",
          "mla_k_nope_g", "mla_k_rope_g", "gmlp_ln_g", "gmlp_ln_b", "gmlp_w_s", "gmlp_b_s"]

_LAY_MLA = dict(up=0, down=1024, out=2048, gate=2304, wdn=2560, wuq=2736, wukv=2880, proj=3008, rows=3072)
_LAY_GMLP = {"up": 0, "down": 1024, "in": 2048, "out": 3072, "gate": 3584, "proj": 3840, "ln": 3904, "rows": 4096}


def _layer_parts(i):
    j = i // 2
    if i % 2 == 0:
        lay = _LAY_MLA
        return lay, [("ffn_w_up", i, lay["up"]), ("ffn_w_down", i, lay["down"]), ("mla_w_out", j, lay["out"]),
                     ("ple_w_gate", i, lay["gate"]), ("mla_w_down", j, lay["wdn"]), ("mla_w_uq", j, lay["wuq"]),
                     ("mla_w_ukv", j, lay["wukv"]), ("ple_w_proj", i, lay["proj"])]
    lay = _LAY_GMLP
    return lay, [("ffn_w_up", i, lay["up"]), ("ffn_w_down", i, lay["down"]), ("gmlp_w_in", j, lay["in"]),
                 ("gmlp_w_out", j, lay["out"]), ("ple_w_gate", i, lay["gate"]), ("ple_w_proj", i, lay["proj"])]


def _pack_rows(parts, dtype, pad_to=None):
    flat = jnp.concatenate([p.reshape(-1).astype(dtype) for p in parts])
    if pad_to is not None:
        flat = jnp.pad(flat, (0, pad_to * D - flat.size))
    return flat.reshape(-1, D)


def _odd(allw, row0, a, b):
    return allw[:, row0:row0 + a * b // D].reshape(N_CHIPS, a, b)


def _cols_joined(s):
    return jnp.transpose(s, (1, 0, 2)).reshape(s.shape[1], N_CHIPS * s.shape[2])


def _col_shards(full):
    a, bb = full.shape
    return jnp.transpose(full.reshape(a, N_CHIPS, bb // N_CHIPS), (1, 0, 2)).reshape(N_CHIPS, -1, D)


def _pad_lanes(g):
    return jnp.pad(g, ((0, 0), (0, LANES - g.shape[1])))


def _split_uq(wuq):
    l = wuq.shape[0]
    w = wuq.reshape(l, QL, HEADS, DN + DR)
    nope = w[..., :DN].reshape(l, QL, HEADS * DN)
    rope = jnp.pad(w[..., DN:], ((0, 0), (0, 0), (0, 0), (0, LANES - DR))).reshape(l, QL, HEADS * LANES)
    return jnp.concatenate([nope, rope], axis=-1)


def _merge_uq(d):
    nope = d[:, :HEADS * DN].reshape(QL, HEADS, DN)
    rope = d[:, HEADS * DN:].reshape(QL, HEADS, LANES)[..., :DR]
    return jnp.concatenate([nope, rope], axis=-1).reshape(QL, HEADS * (DN + DR))


def _rope_tables(positions):
    inv_freq = ROPE_BASE ** (-(jnp.arange(0, DR, 2, dtype=F32) / DR))
    ang = positions.reshape(-1).astype(F32)[:, None] * inv_freq
    z = jnp.zeros((ang.shape[0], LANES - DR), F32)
    return (jnp.concatenate([jnp.cos(ang), jnp.cos(ang), z], axis=1),
            jnp.concatenate([jnp.sin(ang), jnp.sin(ang), z], axis=1))


def kernel(x, p, positions, norm_mix, norm_ffn, norm_ple, mla_w_down, mla_q_lora_g, mla_kv_lora_g, mla_w_uq, mla_w_ukv, mla_q_nope_g, mla_q_rope_g, mla_k_nope_g, mla_k_rope_g, mla_w_out, gmlp_w_in, gmlp_ln_g, gmlp_ln_b, gmlp_w_s, gmlp_b_s, gmlp_w_out, ffn_w_up, ffn_w_down, ple_w_gate, ple_w_proj, loss_target, m_norm_mix, m_norm_ffn, m_norm_ple, m_mla_w_down, m_mla_q_lora_g, m_mla_kv_lora_g, m_mla_w_uq, m_mla_w_ukv, m_mla_q_nope_g, m_mla_q_rope_g, m_mla_k_nope_g, m_mla_k_rope_g, m_mla_w_out, m_gmlp_w_in, m_gmlp_ln_g, m_gmlp_ln_b, m_gmlp_w_s, m_gmlp_b_s, m_gmlp_w_out, m_ffn_w_up, m_ffn_w_down, m_ple_w_gate, m_ple_w_proj, v_norm_mix, v_norm_ffn, v_norm_ple, v_mla_w_down, v_mla_q_lora_g, v_mla_kv_lora_g, v_mla_w_uq, v_mla_w_ukv, v_mla_q_nope_g, v_mla_q_rope_g, v_mla_k_nope_g, v_mla_k_rope_g, v_mla_w_out, v_gmlp_w_in, v_gmlp_ln_g, v_gmlp_ln_b, v_gmlp_w_s, v_gmlp_b_s, v_gmlp_w_out, v_ffn_w_up, v_ffn_w_down, v_ple_w_gate, v_ple_w_proj):
    args = dict(locals())
    weights = {n: args[n] for n in _BIG + _SMALL}
    depth = norm_mix.shape[0]
    nb, seq, _ = x.shape
    t = nb * seq
    assert seq % TQ == 0 and seq % TM == 0 and t % 512 == 0, (nb, seq)
    cx = lax.axis_index("x")
    cy = lax.axis_index("y")
    cc = lax.axis_index("c")
    chip = 2 * cx + cy

    gathers = []
    token = None
    for i in range(depth):
        lay, parts = _layer_parts(i)
        rows = [weights[n][l] for n, l, _ in parts]
        if token is not None:
            rows[0] = rows[0] + token[0, 0]
        if i % 2 == 1:
            ln = jnp.stack([gmlp_ln_g[i // 2], gmlp_ln_b[i // 2]]).astype(F32)
            rows.append(lax.bitcast_convert_type(ln, BF16))
        mine = _pack_rows(rows, BF16, pad_to=lay["rows"])
        land = lax.dynamic_update_slice(lax.empty((N_CHIPS, lay["rows"], D), BF16), mine[None], (chip, 0, 0))
        sems, src, land, token = gather_start(land, i)
        gathers.append((sems, src, land))
    allw = [None] * depth

    tril = jnp.tril(jnp.ones((GC, GC), F32))
    wm = (gmlp_w_s * tril).astype(BF16)
    wmt = jnp.swapaxes(wm, -1, -2)
    bfull = jnp.repeat(jnp.swapaxes(gmlp_b_s, -1, -2), GD, axis=-1)
    cos, sin = _rope_tables(positions)
    row = lambda g: g.reshape(1, -1)
    gqr = _pad_lanes(mla_q_rope_g)
    gkr = _pad_lanes(mla_k_rope_g)

    h = x.reshape(t, D)
    pt = p.reshape(depth, t, PLE)
    saved = []

    def arrive(i, after):
        sems, src, land = gathers[i]
        _, land = gather_wait(sems, src, land, after, i)
        return pass_start(land, i)

    passing = arrive(0, token)
    for i in range(depth):
        j = i // 2
        lay, _ = _layer_parts(i)
        sems, src, land, token = passing
        _, aw = pass_wait(sems, src, land, token if i == 0 else h, i)
        allw[i] = aw
        s = dict(h=h)
        if i % 2 == 0:
            wdn = jnp.pad(_odd(aw, lay["wdn"], D // N_CHIPS, LAT).reshape(D, LAT), ((0, 0), (0, LATP - LAT)))
            wuq = _split_uq(_cols_joined(_odd(aw, lay["wuq"], QL, 384))[None])[0]
            wukv = _cols_joined(_odd(aw, lay["wukv"], KVL, 512))
            mla_args = (row(norm_mix[i]), wdn, row(mla_q_lora_g[j]), row(mla_kv_lora_g[j]), wuq, wukv,
                        row(mla_q_nope_g[j]), gqr[j:j + 1], row(mla_k_nope_g[j]), gkr[j:j + 1], cos, sin)
            q, k, v = mla_pre_fwd(h, *mla_args)
            y, lse = flash_fwd(q, k, v, seq)
            s.update(q=q, k=k, v=v, lse=lse, mla_args=mla_args)
        else:
            ln = lax.bitcast_convert_type(aw[:, lay["ln"]:lay["ln"] + 2].reshape(N_CHIPS, 2, GH // N_CHIPS, 2), F32)
            ln = jnp.transpose(ln, (1, 0, 2)).reshape(2, 1, GH)
            y, pre = gmlp_fwd(h, row(norm_mix[i]), aw, lay, ln[0], ln[1], wm[j], bfull[j])
            s.update(pre=pre, ln=ln)
        wp = _cols_joined(_odd(aw, lay["proj"], PLE, 256))
        g2 = row(norm_ffn[i])
        if i + 1 < depth:
            passing = arrive(i + 1, y)
            g2 = g2 + passing[3][0:1, 0:1]
        h1, h2, hn2, r = mixffn_fwd(h, y, aw, lay, g2)
        h, hn3 = ple_fwd(h2, pt[i], row(norm_ple[i]), aw, lay, wp)
        s.update(y=y, wp=wp, h1=h1, h2=h2, hn2=hn2, r=r, hn3=hn3)
        saved.append(s)

    dh, loss_part = loss_head(h, loss_target.reshape(t, D))
    loss = lax.psum(loss_part[0, 0], ("x", "y", "c"))

    gs = {n: [None] * weights[n].shape[0] for n in _SMALL}
    gw = {n: [None] * weights[n].shape[0] for n in _BIG}
    place = jnp.stack([cc, chip]).astype(jnp.int32)
    scatters = []
    swapping = None
    token = None

    def put(b, row0, shards):
        return lax.dynamic_update_slice(b, shards.reshape(N_CHIPS, -1, D), (0, row0, 0))

    def swapped(after):
        ii, sems, g, got = swapping
        g, got = swap_wait(sems, g, got, after, ii)
        own, sums = chip_sum(place, g, got)
        sems, sums, land, tok = scatter_start(sums, ii)
        scatters.append((ii, own, sems, sums, land))
        return tok

    for i in reversed(range(depth)):
        j = i // 2
        lay, parts = _layer_parts(i)
        aw = allw[i]
        s = saved[i]

        buf = lax.empty((N_CHIPS, lay["rows"], D), F32)
        tail = lay.get("ln", lay["rows"])
        if tail < lay["rows"]:
            buf = put(buf, tail, jnp.zeros((N_CHIPS, lay["rows"] - tail, D), F32))
        g3 = row(norm_ple[i])
        if token is not None:
            g3 = g3 + token[0:1, 0:1]
        dh2, dh2b, dgt, dpp, dg3 = ple_bwd(dh, s["h2"], pt[i], g3, aw, lay, s["wp"])
        gs["norm_ple"][i] = dg3[0]
        buf = mm_tn_into(buf, s["hn3"], dgt, D // N_CHIPS, lay["gate"], False)
        buf = put(buf, lay["proj"], _col_shards(mm_tn(pt[i], dpp)))
        dh1, dh1b, du, a, dg2 = ffn_bwd(dh2, dh2b, s["h1"], s["r"], row(norm_ffn[i]), aw, lay)
        gs["norm_ffn"][i] = dg2[0]
        buf = mm_tn_into(buf, a, dh2b, D, lay["down"], False)
        buf = mm_tn_into(buf, s["hn2"], du, D, lay["up"], True)
        buf = mm_tn_into(buf, s["y"], dh1b, s["y"].shape[1] // N_CHIPS, lay["out"], False)
        g1 = row(norm_mix[i])
        if swapping is not None:
            g1 = g1 + swapped(dh1)[0:1, 0:1]
        if i % 2 == 0:
            do = linear_nt(dh1b, aw, D // N_CHIPS, lay["out"])
            dq, dk, dv = flash_bwd(s["q"], s["k"], s["v"], s["y"], do, s["lse"], seq)
            (dh, hn1, cq, ckv, dqp, dkvp, dlat, dg1, dgq, dgkv, dgqn, dgqr, dgkn, dgkr) = mla_pre_bwd(
                dq, dk, dv, dh1, s["h"], g1, *s["mla_args"][1:])
            gs["norm_mix"][i] = dg1[0]
            gs["mla_q_lora_g"][j] = dgq[0]
            gs["mla_kv_lora_g"][j] = dgkv[0]
            gs["mla_q_nope_g"][j] = dgqn[0]
            gs["mla_q_rope_g"][j] = dgqr[0, :DR]
            gs["mla_k_nope_g"][j] = dgkn[0]
            gs["mla_k_rope_g"][j] = dgkr[0, :DR]
            buf = put(buf, lay["wdn"], mm_tn(hn1, dlat)[:, :LAT])
            buf = put(buf, lay["wuq"], _col_shards(_merge_uq(mm_tn(cq, dqp))))
            buf = put(buf, lay["wukv"], _col_shards(mm_tn(ckv, dkvp)))
        else:
            dh, hn1, dpre, dws, dbs, dlng, dlnb, dg1 = gmlp_bwd(
                dh1, dh1b, s["h"], s["pre"], g1, aw, lay, s["ln"][0], s["ln"][1], wm[j], wmt[j], bfull[j], tril)
            gs["norm_mix"][i] = dg1[0]
            gs["gmlp_ln_g"][j] = dlng[0]
            gs["gmlp_ln_b"][j] = dlnb[0]
            gs["gmlp_w_s"][j] = dws
            gs["gmlp_b_s"][j] = jnp.sum(dbs.reshape(GC, GG, GD), axis=-1).T
            buf = mm_tn_into(buf, hn1, dpre, D, lay["in"], True)

        sems, buf, got, token = swap_start(buf, i)
        swapping = (i, sems, buf, got)
    swapped(dh)
    grad_x = dh.reshape(x.shape)

    after = dh
    for i, own, sems, sums, land in scatters:
        _, got = scatter_wait(sems, sums, land, after, i)
        half = final_sum(own, got)
        after = reduced = lax.dynamic_update_slice(share_with_sibling(half), half, (cc * half.shape[0], 0))
        for n, l, row0 in _layer_parts(i)[1]:
            gw[n][l] = reduced[row0:row0 + weights[n][l].size // D].reshape(weights[n].shape[1:])
    grads = {n: jnp.stack(gw[n]) for n in _BIG}

    small_sizes = [weights[n].size if n not in ("gmlp_ln_g", "gmlp_ln_b") else weights[n].shape[0] * GH
                   for n in _SMALL]
    small_rows = -(-sum(small_sizes) // (8 * D)) * 8
    part = _pack_rows([jnp.stack(gs[n]) for n in _SMALL], F32, pad_to=small_rows)
    tot = allreduce_small(part).reshape(-1)
    off = 0
    for n, sz in zip(_SMALL, small_sizes):
        gsum = tot[off:off + sz]
        off += sz
        if n in ("gmlp_ln_g", "gmlp_ln_b"):
            gsum = lax.dynamic_slice_in_dim(gsum.reshape(-1, GH), chip * (GH // N_CHIPS), GH // N_CHIPS, axis=1)
        grads[n] = gsum.reshape(weights[n].shape)

    delta, new_m, new_v = {}, {}, {}
    for n in _BIG:
        w2 = weights[n].reshape(-1, weights[n].shape[-1])
        d, mn, vn = adamw(w2, grads[n].reshape(w2.shape), args["m_" + n].reshape(w2.shape),
                          args["v_" + n].reshape(w2.shape))
        delta[n], new_m[n], new_v[n] = (a.reshape(weights[n].shape) for a in (d, mn, vn))
    own_sizes = [weights[n].size for n in _SMALL]
    own_rows = -(-sum(own_sizes) // (8 * D)) * 8
    packed = [_pack_rows([src[n] for n in _SMALL], F32, pad_to=own_rows)
              for src in (weights, grads, {n: args["m_" + n] for n in _SMALL}, {n: args["v_" + n] for n in _SMALL})]
    outs = adamw(*packed)
    off = 0
    for n, sz in zip(_SMALL, own_sizes):
        for dst, o in zip((delta, new_m, new_v), outs):
            dst[n] = o.reshape(-1)[off:off + sz].reshape(weights[n].shape)
        off += sz

    order = ["norm_mix", "norm_ffn", "norm_ple", "mla_w_down", "mla_q_lora_g", "mla_kv_lora_g", "mla_w_uq",
             "mla_w_ukv", "mla_q_nope_g", "mla_q_rope_g", "mla_k_nope_g", "mla_k_rope_g", "mla_w_out", "gmlp_w_in",
             "gmlp_ln_g", "gmlp_ln_b", "gmlp_w_s", "gmlp_b_s", "gmlp_w_out", "ffn_w_up", "ffn_w_down", "ple_w_gate",
             "ple_w_proj"]
    return (loss, grad_x, *[grads[n] for n in order], *[delta[n] for n in order], *[new_m[n] for n in order],
            *[new_v[n] for n in order])
```

```python
import functools

import jax
import jax.numpy as jnp
from jax import lax
from jax.experimental import pallas as pl
from jax.experimental.pallas import tpu as pltpu

F32 = jnp.float32
BF16 = jnp.bfloat16
MESH = pl.DeviceIdType.MESH

D = 1024
HEADS = 8
DN = 128
DR = 64
QL = 384
KVL = 256
LAT = 704
LATP = 768
DFF = 4096
GH = 2048
GC = 128
GG = 8
GD = 256
PLE = 256
EPS = 1e-6
ROPE_BASE = 10000.0
SM_SCALE = (DN + DR) ** -0.5
N_CHIPS = 4
LANES = 128

ADAM_LR = 0.001
ADAM_B1 = 0.9
ADAM_B2 = 0.999
ADAM_EPS = 1e-08
ADAM_WD = 0.01
ADAM_STEP = 10

TM = 256
TMB = 512
TQ = 512
TQ_FWD = 512
FWD_HEADS = 1
SUM_ROWS = 256
VMEM_LIMIT = 56 * 1024 * 1024


def _cp(*sem):
    return pltpu.CompilerParams(dimension_semantics=sem, vmem_limit_bytes=VMEM_LIMIT)


def _dot(a, b):
    return jnp.dot(a, b, preferred_element_type=F32)


def _dot_nt(a, b):
    return lax.dot_general(a, b, (((1,), (1,)), ((), ())), preferred_element_type=F32)


def _dot_tn(a, b):
    return lax.dot_general(a, b, (((0,), (0,)), ((), ())), preferred_element_type=F32)


def _rms(x, g, n):
    r = lax.rsqrt(jnp.sum(x * x, axis=-1, keepdims=True) * (1.0 / n) + EPS)
    xhat = x * r
    return xhat * g, xhat, r


def _rms_bwd(dy, g, xhat, r, n):
    dxhat = dy * g
    return r * (dxhat - xhat * (jnp.sum(dxhat * xhat, axis=-1, keepdims=True) * (1.0 / n)))


def _rope(x, c, s):
    return x * c + (pltpu.roll(x, 32, 1) - pltpu.roll(x, 96, 1)) * s


def _rope_t(dy, c, s):
    w = dy * s
    return dy * c + pltpu.roll(w, 96, 1) - pltpu.roll(w, 32, 1)


def _sigmoid(x):
    return 1.0 / (1.0 + jnp.exp(-x))


_GELU_K = 0.7978845608028654
_GELU_C = 0.044715


def _gelu(x):
    return 0.5 * x * (1.0 + jnp.tanh(_GELU_K * (x + _GELU_C * x * x * x)))


def _gelu_and_grad(x):
    x2 = x * x
    t = jnp.tanh(_GELU_K * (x + _GELU_C * x2 * x))
    half = 0.5 * (1.0 + t)
    return x * half, half + 0.5 * x * (1.0 - t * t) * (_GELU_K * (1.0 + 3.0 * _GELU_C * x2))


def _acc_rows(ref, val):
    ref[...] += jnp.broadcast_to(jnp.sum(val, axis=0, keepdims=True), ref.shape)


def _row(tm, c):
    return pl.BlockSpec((tm, c), lambda i: (i, 0))


def _const(shape):
    nd = len(shape)
    return pl.BlockSpec(shape, lambda i: (0,) * nd, pipeline_mode=pl.Buffered(1))


def _wblk(rows, row0):
    assert row0 % rows == 0, (rows, row0)
    return pl.BlockSpec((N_CHIPS, rows, D), lambda i: (0, row0 // rows, 0), pipeline_mode=pl.Buffered(1))


def _rows_joined(w_ref):
    return w_ref[...].reshape(N_CHIPS * w_ref.shape[1], D)


def _sds(shape, dtype):
    return jax.ShapeDtypeStruct(shape, dtype)


def mixffn_fwd(h, y, allw, lay, g2):
    t, k = y.shape

    def body(h_ref, y_ref, wo_ref, g_ref, wu_ref, wd_ref, h1_ref, h2_ref, hn_ref, r_ref):
        h1 = h_ref[...] + _dot(y_ref[...], _rows_joined(wo_ref))
        h1_ref[...] = h1
        yn, _, _ = _rms(h1, g_ref[...], D)
        hn = yn.astype(BF16)
        hn_ref[...] = hn
        f = jnp.zeros((TMB, D), F32)
        for c in range(N_CHIPS):
            r = jnp.maximum(_dot(hn, wu_ref[c]), 0.0)
            r_ref[:, c * D:(c + 1) * D] = r.astype(BF16)
            f = f + _dot((r * r).astype(BF16), wd_ref[c])
        h2_ref[...] = h1 + f

    return pl.pallas_call(
        body, name="mixffn_fwd", grid=(t // TMB,),
        in_specs=[_row(TMB, D), _row(TMB, k), _wblk(k // N_CHIPS, lay["out"]), _const((1, D)), _wblk(D, lay["up"]),
                  _wblk(D, lay["down"])],
        out_specs=[_row(TMB, D), _row(TMB, D), _row(TMB, D), _row(TMB, DFF)],
        out_shape=[_sds((t, D), F32), _sds((t, D), F32), _sds((t, D), BF16), _sds((t, DFF), BF16)],
        compiler_params=_cp("parallel"),
    )(h, y, allw, g2, allw, allw)


def ple_fwd(h2, p, g3, allw, lay, wp):
    t = h2.shape[0]

    def body(h_ref, p_ref, g_ref, wg_ref, wp_ref, h3_ref, hn_ref):
        x = h_ref[...]
        yn, _, _ = _rms(x, g_ref[...], D)
        hn = yn.astype(BF16)
        hn_ref[...] = hn
        gt = _dot(hn, _rows_joined(wg_ref))
        pp = _dot(p_ref[...].astype(BF16), wp_ref[...])
        h3_ref[...] = x + _sigmoid(gt) * pp

    return pl.pallas_call(
        body, name="ple_fwd", grid=(t // TMB,),
        in_specs=[_row(TMB, D), _row(TMB, PLE), _const((1, D)), _wblk(D // N_CHIPS, lay["gate"]), _const((PLE, D))],
        out_specs=[_row(TMB, D), _row(TMB, D)],
        out_shape=[_sds((t, D), F32), _sds((t, D), BF16)],
        compiler_params=_cp("parallel"),
    )(h2, p, g3, allw, wp)


def _mla_project(h_ref, g1_ref, wdn_ref, gq_ref, gkv_ref, wuq_ref, wukv_ref):
    x = h_ref[...]
    yn, xhat, rx = _rms(x, g1_ref[...], D)
    hn = yn.astype(BF16)
    lat = _dot(hn, wdn_ref[...])
    cq, cqhat, rq = _rms(lat[:, :QL], gq_ref[...], QL)
    ckv, ckvhat, rkv = _rms(lat[:, QL:QL + KVL], gkv_ref[...], KVL)
    kr_raw = lat[:, QL + KVL:]
    cqb = cq.astype(BF16)
    ckvb = ckv.astype(BF16)
    qp = _dot(cqb, wuq_ref[...])
    kvp = _dot(ckvb, wukv_ref[...])
    return dict(xhat=xhat, rx=rx, hn=hn, cqhat=cqhat, rq=rq, ckvhat=ckvhat, rkv=rkv, kr_raw=kr_raw,
                cqb=cqb, ckvb=ckvb, qp=qp, kvp=kvp)


def mla_pre_fwd(h, g1, wdn, gq, gkv, wuq, wukv, gqn, gqr, gkn, gkr, cos, sin):
    t = h.shape[0]

    def body(h_ref, g1_ref, wdn_ref, gq_ref, gkv_ref, wuq_ref, wukv_ref, gqn_ref, gqr_ref, gkn_ref, gkr_ref,
             c_ref, s_ref, q_ref, k_ref, v_ref):
        m = _mla_project(h_ref, g1_ref, wdn_ref, gq_ref, gkv_ref, wuq_ref, wukv_ref)
        c = c_ref[...]
        s = s_ref[...]
        kr, _, _ = _rms(m["kr_raw"], gkr_ref[...], DR)
        krb = _rope(kr, c, s).astype(BF16)
        for hd in range(HEADS):
            qn, _, _ = _rms(m["qp"][:, hd * DN:(hd + 1) * DN], gqn_ref[...], DN)
            qr, _, _ = _rms(m["qp"][:, D + hd * LANES:D + (hd + 1) * LANES], gqr_ref[...], DR)
            q_ref[hd, :, 0:DN] = (qn * SM_SCALE).astype(BF16)
            q_ref[hd, :, DN:2 * DN] = (_rope(qr, c, s) * SM_SCALE).astype(BF16)
            kn, _, _ = _rms(m["kvp"][:, hd * 2 * DN:hd * 2 * DN + DN], gkn_ref[...], DN)
            k_ref[hd, :, 0:DN] = kn.astype(BF16)
            k_ref[hd, :, DN:2 * DN] = krb
            v_ref[hd] = m["kvp"][:, hd * 2 * DN + DN:(hd + 1) * 2 * DN].astype(BF16)

    hb = lambda w: pl.BlockSpec((HEADS, TM, w), lambda i: (0, i, 0))
    return pl.pallas_call(
        body, name="mla_pre_fwd", grid=(t // TM,),
        in_specs=[_row(TM, D), _const((1, D)), _const((D, LATP)), _const((1, QL)), _const((1, KVL)),
                  _const((QL, 2 * D)), _const((KVL, 2 * D)), _const((1, LANES)), _const((1, LANES)),
                  _const((1, LANES)), _const((1, LANES)), _row(TM, LANES), _row(TM, LANES)],
        out_specs=[hb(2 * DN), hb(2 * DN), hb(DN)],
        out_shape=[_sds((HEADS, t, 2 * DN), BF16), _sds((HEADS, t, 2 * DN), BF16), _sds((HEADS, t, DN), BF16)],
        compiler_params=_cp("parallel"),
    )(h, g1, wdn, gq, gkv, wuq, wukv, gqn, gqr, gkn, gkr, cos, sin)


def _diagonal_mask(n=TQ):
    return lax.broadcasted_iota(jnp.int32, (n, n), 1) <= lax.broadcasted_iota(jnp.int32, (n, n), 0)


def flash_fwd(q, k, v, seq):
    t = q.shape[1]
    nb = t // seq
    tq = TQ_FWD
    nq = seq // tq
    hp = FWD_HEADS

    def body(q_ref, k_ref, v_ref, o_ref, lse_ref):
        qi = pl.program_id(2)
        qs = [q_ref[a] for a in range(hp)]

        def step(j, carry, diagonal=False):
            rows = pl.ds(pl.multiple_of(j * tq, tq), tq)
            out = []
            for a in range(hp):
                m, l, acc = carry[a]
                s = _dot_nt(qs[a], k_ref[a, rows, :])
                if diagonal:
                    s = jnp.where(_diagonal_mask(tq), s, -1e30)
                m_new = jnp.maximum(m, jnp.max(s, axis=-1, keepdims=True))
                p = jnp.exp(s - m_new)
                alpha = jnp.exp(m - m_new)
                l = alpha * l + jnp.sum(p, axis=-1, keepdims=True)
                acc = alpha * acc + _dot(p.astype(BF16), v_ref[a, rows, :])
                out.append((m_new, l, acc))
            return tuple(out)

        one = (jnp.full((tq, 1), -1e30, F32), jnp.zeros((tq, 1), F32), jnp.zeros((tq, DN), F32))
        done = step(qi, lax.fori_loop(0, qi, step, (one,) * hp), diagonal=True)
        for a, (m, l, acc) in enumerate(done):
            o_ref[:, a * DN:(a + 1) * DN] = (acc / l).astype(BF16)
            lse_ref[a] = m + jnp.log(l)

    return pl.pallas_call(
        body, name="flash_fwd", grid=(nb, HEADS // hp, nq),
        in_specs=[pl.BlockSpec((hp, tq, 2 * DN), lambda b, h, i: (h, b * nq + i, 0)),
                  pl.BlockSpec((hp, seq, 2 * DN), lambda b, h, i: (h, b, 0)),
                  pl.BlockSpec((hp, seq, DN), lambda b, h, i: (h, b, 0))],
        out_specs=[pl.BlockSpec((tq, hp * DN), lambda b, h, i: (b * nq + i, h)),
                   pl.BlockSpec((hp, tq, 1), lambda b, h, i: (h, b * nq + i, 0))],
        out_shape=[_sds((t, HEADS * DN), BF16), _sds((HEADS, t, 1), F32)],
        compiler_params=_cp("parallel", "parallel", "arbitrary"),
    )(q, k, v)


def _gmlp_in(hn, win_ref):
    pre = [_dot(hn, win_ref[c]) for c in range(N_CHIPS)]
    return jnp.concatenate(pre[:2], axis=1), jnp.concatenate(pre[2:], axis=1)


def gmlp_fwd(h, g1, allw, lay, lng, lnb, wm, bfull):
    t = h.shape[0]

    def body(h_ref, g1_ref, win_ref, lng_ref, lnb_ref, wm_ref, b_ref, y_ref, pre_ref):
        yn, _, _ = _rms(h_ref[...], g1_ref[...], D)
        pre_u, pre_v = _gmlp_in(yn.astype(BF16), win_ref)
        pre_ref[:, :GH] = pre_u.astype(BF16)
        pre_ref[:, GH:] = pre_v.astype(BF16)
        u = _gelu(pre_u)
        v = _gelu(pre_v)
        xc = v - jnp.mean(v, axis=-1, keepdims=True)
        rs = lax.rsqrt(jnp.mean(xc * xc, axis=-1, keepdims=True) + EPS)
        vnb = (xc * rs * lng_ref[...] + lnb_ref[...]).astype(BF16)
        for ch in range(TM // GC):
            rows = slice(ch * GC, (ch + 1) * GC)
            for g in range(GG):
                cols = slice(g * GD, (g + 1) * GD)
                sv = _dot(wm_ref[g], vnb[rows, cols]) + b_ref[:, cols]
                y_ref[rows, cols] = (u[rows, cols] * sv).astype(BF16)

    return pl.pallas_call(
        body, name="gmlp_fwd", grid=(t // TM,),
        in_specs=[_row(TM, D), _const((1, D)), _wblk(D, lay["in"]), _const((1, GH)), _const((1, GH)),
                  _const((GG, GC, GC)), _const((GC, GH))],
        out_specs=[_row(TM, GH), _row(TM, 2 * GH)],
        out_shape=[_sds((t, GH), BF16), _sds((t, 2 * GH), BF16)],
        compiler_params=_cp("parallel"),
    )(h, g1, allw, lng, lnb, wm, bfull)


def loss_head(h, tgt):
    t = h.shape[0]

    def body(h_ref, t_ref, dh_ref, loss_ref):
        @pl.when(pl.program_id(0) == 0)
        def _():
            loss_ref[...] = jnp.zeros_like(loss_ref)

        e = h_ref[...] - t_ref[...]
        dh_ref[...] = e * (1.0 / D)
        part = jnp.sum(jnp.sum(e * e, axis=-1, keepdims=True), axis=0, keepdims=True) * (0.5 / D)
        loss_ref[...] += jnp.broadcast_to(part, loss_ref.shape)

    return pl.pallas_call(
        body, name="loss_head", grid=(t // TMB,),
        in_specs=[_row(TMB, D), _row(TMB, D)],
        out_specs=[_row(TMB, D), _const((8, LANES))],
        out_shape=[_sds((t, D), F32), _sds((8, LANES), F32)],
        compiler_params=_cp("arbitrary"),
    )(h, tgt)


def _zero_at_first_step(*refs):
    @pl.when(pl.program_id(0) == 0)
    def _():
        for r in refs:
            r[...] = jnp.zeros_like(r)


def ple_bwd(dh3, h2, p, g3, allw, lay, wp):
    t = h2.shape[0]

    def body(dh_ref, h_ref, p_ref, g_ref, wg_ref, wp_ref, dh2_ref, dh2b_ref, dgt_ref, dpp_ref, dg_ref):
        _zero_at_first_step(dg_ref)
        dh3v = dh_ref[...]
        x = h_ref[...]
        g = g_ref[...]
        wg = _rows_joined(wg_ref)
        yn, xhat, r = _rms(x, g, D)
        gt = _dot(yn.astype(BF16), wg)
        pp = _dot(p_ref[...].astype(BF16), wp_ref[...])
        sg = _sigmoid(gt)
        dgt = (dh3v * pp * sg * (1.0 - sg)).astype(BF16)
        dgt_ref[...] = dgt
        dpp_ref[...] = (dh3v * sg).astype(BF16)
        dhn = _dot_nt(dgt, wg)
        _acc_rows(dg_ref, dhn * xhat)
        dh2 = dh3v + _rms_bwd(dhn, g, xhat, r, D)
        dh2_ref[...] = dh2
        dh2b_ref[...] = dh2.astype(BF16)

    return pl.pallas_call(
        body, name="ple_bwd", grid=(t // TMB,),
        in_specs=[_row(TMB, D), _row(TMB, D), _row(TMB, PLE), _const((1, D)), _wblk(D // N_CHIPS, lay["gate"]),
                  _const((PLE, D))],
        out_specs=[_row(TMB, D), _row(TMB, D), _row(TMB, D), _row(TMB, D), _const((8, D))],
        out_shape=[_sds((t, D), F32), _sds((t, D), BF16), _sds((t, D), BF16), _sds((t, D), BF16), _sds((8, D), F32)],
        compiler_params=_cp("arbitrary"),
    )(dh3, h2, p, g3, allw, wp)


def ffn_bwd(dh2, dh2b, h1, r, g2, allw, lay):
    t = h1.shape[0]

    def body(dh_ref, dhb_ref, h_ref, r_ref, g_ref, wu_ref, wd_ref, dh1_ref, dh1b_ref, du_ref, a_ref, dg_ref):
        _zero_at_first_step(dg_ref)
        dhb = dhb_ref[...]
        g = g_ref[...]
        _, xhat, rr = _rms(h_ref[...], g, D)
        dhn = jnp.zeros((TM, D), F32)
        for c in range(N_CHIPS):
            cs = slice(c * D, (c + 1) * D)
            rc = r_ref[:, cs].astype(F32)
            a_ref[:, cs] = (rc * rc).astype(BF16)
            da = _dot_nt(dhb, wd_ref[c])
            du = (da * (2.0 * rc)).astype(BF16)
            du_ref[:, cs] = du
            dhn = dhn + _dot_nt(du, wu_ref[c])
        _acc_rows(dg_ref, dhn * xhat)
        dh1 = dh_ref[...] + _rms_bwd(dhn, g, xhat, rr, D)
        dh1_ref[...] = dh1
        dh1b_ref[...] = dh1.astype(BF16)

    return pl.pallas_call(
        body, name="ffn_bwd", grid=(t // TM,),
        in_specs=[_row(TM, D), _row(TM, D), _row(TM, D), _row(TM, DFF), _const((1, D)), _wblk(D, lay["up"]),
                  _wblk(D, lay["down"])],
        out_specs=[_row(TM, D), _row(TM, D), _row(TM, DFF), _row(TM, DFF), _const((8, D))],
        out_shape=[_sds((t, D), F32), _sds((t, D), BF16), _sds((t, DFF), BF16), _sds((t, DFF), BF16),
                   _sds((8, D), F32)],
        compiler_params=_cp("arbitrary"),
    )(dh2, dh2b, h1, r, g2, allw, allw)


def linear_nt(a, allw, rows, row0):
    t = a.shape[0]
    k = N_CHIPS * rows

    def body(a_ref, w_ref, o_ref):
        o_ref[...] = _dot_nt(a_ref[...], _rows_joined(w_ref)).astype(BF16)

    return pl.pallas_call(
        body, name="linear_nt", grid=(t // TMB,),
        in_specs=[_row(TMB, D), _wblk(rows, row0)],
        out_specs=_row(TMB, k),
        out_shape=_sds((t, k), BF16),
        compiler_params=_cp("parallel"),
    )(a, allw)


def flash_bwd(q, k, v, o, do, lse, seq):
    t = q.shape[1]
    nb = t // seq
    nq = seq // TQ

    def body(q_ref, k_ref, v_ref, o_ref, do_ref, lse_ref, dq_ref, dk_ref, dv_ref):
        kj = pl.program_id(2)

        @pl.when(kj == 0)
        def _():
            dq_ref[...] = jnp.zeros_like(dq_ref)

        kv = k_ref[0]
        vv = v_ref[0]

        def step(i, carry, diagonal=False):
            dk, dv = carry
            rows = pl.ds(pl.multiple_of(i * TQ, TQ), TQ)
            qv = q_ref[0, rows, :]
            dov = do_ref[rows, :]
            delta = jnp.sum(dov.astype(F32) * o_ref[rows, :].astype(F32), axis=-1, keepdims=True)
            s = _dot_nt(qv, kv)
            if diagonal:
                s = jnp.where(_diagonal_mask(), s, -1e30)
            p = jnp.exp(s - lse_ref[0, rows, :])
            dp = _dot_nt(dov, vv)
            ds = (p * (dp - delta)).astype(BF16)
            dv = dv + _dot_tn(p.astype(BF16), dov)
            dk = dk + _dot_tn(ds, qv)
            dq_ref[0, rows, :] += _dot(ds, kv)
            return dk, dv

        init = (jnp.zeros((TQ, 2 * DN), F32), jnp.zeros((TQ, DN), F32))
        dk, dv = lax.fori_loop(kj + 1, nq, step, step(kj, init, diagonal=True))
        dk_ref[0] = dk
        dv_ref[0] = dv

    return pl.pallas_call(
        body, name="flash_bwd", grid=(nb, HEADS, nq),
        in_specs=[pl.BlockSpec((1, seq, 2 * DN), lambda b, h, j: (h, b, 0)),
                  pl.BlockSpec((1, TQ, 2 * DN), lambda b, h, j: (h, b * nq + j, 0)),
                  pl.BlockSpec((1, TQ, DN), lambda b, h, j: (h, b * nq + j, 0)),
                  pl.BlockSpec((seq, DN), lambda b, h, j: (b, h)),
                  pl.BlockSpec((seq, DN), lambda b, h, j: (b, h)),
                  pl.BlockSpec((1, seq, 1), lambda b, h, j: (h, b, 0))],
        out_specs=[pl.BlockSpec((1, seq, 2 * DN), lambda b, h, j: (h, b, 0)),
                   pl.BlockSpec((1, TQ, 2 * DN), lambda b, h, j: (h, b * nq + j, 0)),
                   pl.BlockSpec((1, TQ, DN), lambda b, h, j: (h, b * nq + j, 0))],
        out_shape=[_sds((HEADS, t, 2 * DN), F32), _sds((HEADS, t, 2 * DN), F32), _sds((HEADS, t, DN), F32)],
        compiler_params=_cp("parallel", "parallel", "arbitrary"),
    )(q, k, v, o, do, lse)


def mla_pre_bwd(dq, dk, dv, dh1, h, g1, wdn, gq, gkv, wuq, wukv, gqn, gqr, gkn, gkr, cos, sin):
    t = h.shape[0]

    def body(dq_ref, dk_ref, dv_ref, dh1_ref, h_ref, g1_ref, wdn_ref, gq_ref, gkv_ref, wuq_ref, wukv_ref,
             gqn_ref, gqr_ref, gkn_ref, gkr_ref, c_ref, s_ref,
             dh_ref, hn_ref, cq_ref, ckv_ref, dqp_ref, dkvp_ref, dlat_ref,
             dg1_ref, dgq_ref, dgkv_ref, dgqn_ref, dgqr_ref, dgkn_ref, dgkr_ref):
        _zero_at_first_step(dg1_ref, dgq_ref, dgkv_ref, dgqn_ref, dgqr_ref, dgkn_ref, dgkr_ref)
        m = _mla_project(h_ref, g1_ref, wdn_ref, gq_ref, gkv_ref, wuq_ref, wukv_ref)
        hn_ref[...] = m["hn"]
        cq_ref[...] = m["cqb"]
        ckv_ref[...] = m["ckvb"]
        c = c_ref[...]
        s = s_ref[...]
        gqn = gqn_ref[...]
        gqr = gqr_ref[...]
        gkn = gkn_ref[...]
        gkr = gkr_ref[...]

        dkr = dk_ref[0, :, DN:2 * DN]
        for hd in range(1, HEADS):
            dkr = dkr + dk_ref[hd, :, DN:2 * DN]
        dkr = _rope_t(dkr, c, s)
        _, krhat, rkr = _rms(m["kr_raw"], gkr, DR)
        _acc_rows(dgkr_ref, dkr * krhat)
        dkr_raw = _rms_bwd(dkr, gkr, krhat, rkr, DR)

        for hd in range(HEADS):
            ncols = slice(hd * DN, (hd + 1) * DN)
            _, xh, r = _rms(m["qp"][:, ncols], gqn, DN)
            dqn = dq_ref[hd, :, 0:DN] * SM_SCALE
            _acc_rows(dgqn_ref, dqn * xh)
            dqp_ref[:, ncols] = _rms_bwd(dqn, gqn, xh, r, DN).astype(BF16)

            rcols = slice(D + hd * LANES, D + (hd + 1) * LANES)
            _, xh, r = _rms(m["qp"][:, rcols], gqr, DR)
            dqr = _rope_t(dq_ref[hd, :, DN:2 * DN] * SM_SCALE, c, s)
            _acc_rows(dgqr_ref, dqr * xh)
            dqp_ref[:, rcols] = _rms_bwd(dqr, gqr, xh, r, DR).astype(BF16)

            kcols = slice(hd * 2 * DN, hd * 2 * DN + DN)
            _, xh, r = _rms(m["kvp"][:, kcols], gkn, DN)
            dkn = dk_ref[hd, :, 0:DN]
            _acc_rows(dgkn_ref, dkn * xh)
            dkvp_ref[:, kcols] = _rms_bwd(dkn, gkn, xh, r, DN).astype(BF16)
            dkvp_ref[:, hd * 2 * DN + DN:(hd + 1) * 2 * DN] = dv_ref[hd].astype(BF16)

        dcq = _dot_nt(dqp_ref[...], wuq_ref[...])
        _acc_rows(dgq_ref, dcq * m["cqhat"])
        dlat_q = _rms_bwd(dcq, gq_ref[...], m["cqhat"], m["rq"], QL)
        dckv = _dot_nt(dkvp_ref[...], wukv_ref[...])
        _acc_rows(dgkv_ref, dckv * m["ckvhat"])
        dlat_kv = _rms_bwd(dckv, gkv_ref[...], m["ckvhat"], m["rkv"], KVL)
        dlat = jnp.concatenate([dlat_q, dlat_kv, dkr_raw], axis=1).astype(BF16)
        dlat_ref[...] = dlat
        dhn = _dot_nt(dlat, wdn_ref[...])
        _acc_rows(dg1_ref, dhn * m["xhat"])
        dh_ref[...] = dh1_ref[...] + _rms_bwd(dhn, g1_ref[...], m["xhat"], m["rx"], D)

    hb = lambda w: pl.BlockSpec((HEADS, TM, w), lambda i: (0, i, 0))
    return pl.pallas_call(
        body, name="mla_pre_bwd", grid=(t // TM,),
        in_specs=[hb(2 * DN), hb(2 * DN), hb(DN), _row(TM, D), _row(TM, D), _const((1, D)), _const((D, LATP)),
                  _const((1, QL)), _const((1, KVL)), _const((QL, 2 * D)), _const((KVL, 2 * D)),
                  _const((1, LANES)), _const((1, LANES)), _const((1, LANES)), _const((1, LANES)),
                  _row(TM, LANES), _row(TM, LANES)],
        out_specs=[_row(TM, D), _row(TM, D), _row(TM, QL), _row(TM, KVL), _row(TM, 2 * D), _row(TM, 2 * D),
                   _row(TM, LATP), _const((8, D)), _const((8, QL)), _const((8, KVL)), _const((8, LANES)),
                   _const((8, LANES)), _const((8, LANES)), _const((8, LANES))],
        out_shape=[_sds((t, D), F32), _sds((t, D), BF16), _sds((t, QL), BF16), _sds((t, KVL), BF16),
                   _sds((t, 2 * D), BF16), _sds((t, 2 * D), BF16), _sds((t, LATP), BF16),
                   _sds((8, D), F32), _sds((8, QL), F32), _sds((8, KVL), F32), _sds((8, LANES), F32),
                   _sds((8, LANES), F32), _sds((8, LANES), F32), _sds((8, LANES), F32)],
        compiler_params=_cp("arbitrary"),
    )(dq, dk, dv, dh1, h, g1, wdn, gq, gkv, wuq, wukv, gqn, gqr, gkn, gkr, cos, sin)


def gmlp_bwd(dh1, dh1b, h, pre, g1, allw, lay, lng, lnb, wm, wmt, bfull, tril):
    t = h.shape[0]

    def body(dh1_ref, dh1b_ref, h_ref, pre_ref, g1_ref, win_ref, lng_ref, lnb_ref, wm_ref, wmt_ref, b_ref,
             wout_ref, tril_ref, dh_ref, hn_ref, dpre_ref, dws_ref, dbs_ref, dlng_ref, dlnb_ref, dg1_ref,
             dvn_s):
        _zero_at_first_step(dws_ref, dbs_ref, dlng_ref, dlnb_ref, dg1_ref)
        g1 = g1_ref[...]
        yn, xhat, rx = _rms(h_ref[...], g1, D)
        hn_ref[...] = yn.astype(BF16)
        dy = _dot_nt(dh1b_ref[...], _rows_joined(wout_ref))
        pre_u = pre_ref[:, :GH].astype(F32)
        pre_v = pre_ref[:, GH:].astype(F32)
        u, gg_u = _gelu_and_grad(pre_u)
        v, gg_v = _gelu_and_grad(pre_v)
        xc = v - jnp.mean(v, axis=-1, keepdims=True)
        rs = lax.rsqrt(jnp.mean(xc * xc, axis=-1, keepdims=True) + EPS)
        vhat = xc * rs
        lng = lng_ref[...]
        vnb = (vhat * lng + lnb_ref[...]).astype(BF16)
        dsv = dy * u
        dsvb = dsv.astype(BF16)
        tril_m = tril_ref[...]
        for ch in range(TM // GC):
            rows = slice(ch * GC, (ch + 1) * GC)
            dbs_ref[...] += dsv[rows, :]
            for g in range(GG):
                cols = slice(g * GD, (g + 1) * GD)
                sv = _dot(wm_ref[g], vnb[rows, cols]) + b_ref[:, cols]
                dpre_ref[rows, cols] = (dy[rows, cols] * sv * gg_u[rows, cols]).astype(BF16)
                dvn_s[rows, cols] = _dot(wmt_ref[g], dsvb[rows, cols])
                dws_ref[g] += _dot_nt(dsvb[rows, cols], vnb[rows, cols]) * tril_m
        dvn = dvn_s[...]
        _acc_rows(dlng_ref, dvn * vhat)
        _acc_rows(dlnb_ref, dvn)
        dvhat = dvn * lng
        dv = rs * (dvhat - jnp.mean(dvhat, axis=-1, keepdims=True)
                   - vhat * jnp.mean(dvhat * vhat, axis=-1, keepdims=True))
        dpre_v = (dv * gg_v).astype(BF16)
        dpre_ref[:, GH:] = dpre_v
        dhn = _dot_nt(dpre_ref[:, 0:D], win_ref[0])
        for c in range(1, N_CHIPS):
            dhn = dhn + _dot_nt(dpre_ref[:, c * D:(c + 1) * D], win_ref[c])
        _acc_rows(dg1_ref, dhn * xhat)
        dh_ref[...] = dh1_ref[...] + _rms_bwd(dhn, g1, xhat, rx, D)

    return pl.pallas_call(
        body, name="gmlp_bwd", grid=(t // TM,),
        in_specs=[_row(TM, D), _row(TM, D), _row(TM, D), _row(TM, 2 * GH), _const((1, D)), _wblk(D, lay["in"]),
                  _const((1, GH)), _const((1, GH)), _const((GG, GC, GC)), _const((GG, GC, GC)), _const((GC, GH)),
                  _wblk(GH // N_CHIPS, lay["out"]), _const((GC, GC))],
        out_specs=[_row(TM, D), _row(TM, D), _row(TM, 2 * GH), _const((GG, GC, GC)), _const((GC, GH)),
                   _const((8, GH)), _const((8, GH)), _const((8, D))],
        out_shape=[_sds((t, D), F32), _sds((t, D), BF16), _sds((t, 2 * GH), BF16), _sds((GG, GC, GC), F32),
                   _sds((GC, GH), F32), _sds((8, GH), F32), _sds((8, GH), F32), _sds((8, D), F32)],
        scratch_shapes=[pltpu.VMEM((TM, GH), F32)],
        compiler_params=_cp("arbitrary"),
    )(dh1, dh1b, h, pre, g1, allw, lng, lnb, wm, wmt, bfull, allw, tril)


def _token_step(t):
    return 1024 if t % 1024 == 0 else 512


def mm_tn(a, b):
    t, k = a.shape
    n = b.shape[1]
    tk = min(k, 1024)
    tn = min(n, 1024)
    tt = _token_step(t)

    def body(a_ref, b_ref, o_ref):
        @pl.when(pl.program_id(2) == 0)
        def _():
            o_ref[...] = jnp.zeros_like(o_ref)

        o_ref[...] += _dot_tn(a_ref[...].astype(BF16), b_ref[...].astype(BF16))

    return pl.pallas_call(
        body, name="mm_tn", grid=(k // tk, n // tn, t // tt),
        in_specs=[pl.BlockSpec((tt, tk), lambda i, j, s: (s, i)), pl.BlockSpec((tt, tn), lambda i, j, s: (s, j))],
        out_specs=pl.BlockSpec((tk, tn), lambda i, j, s: (i, j)), out_shape=_sds((k, n), F32),
        compiler_params=_cp("parallel", "parallel", "arbitrary"),
    )(a, b)


def mm_tn_into(buf, a, b, rows, row0, col_sharded):
    t = a.shape[0]
    tt = _token_step(t)
    assert row0 % rows == 0 and a.shape[1] == (rows if col_sharded else N_CHIPS * rows), (rows, row0, a.shape)
    assert b.shape[1] == (N_CHIPS * D if col_sharded else D), b.shape
    grid = (1, N_CHIPS, t // tt) if col_sharded else (N_CHIPS, 1, t // tt)
    fresh = isinstance(buf, int)

    def body(*refs):
        a_ref, b_ref, o_ref = refs[-3:]

        @pl.when(pl.program_id(2) == 0)
        def _():
            o_ref[...] = jnp.zeros_like(o_ref)

        o_ref[...] += _dot_tn(a_ref[...].astype(BF16), b_ref[...].astype(BF16))

    specs = [pl.BlockSpec((tt, rows), lambda i, j, s: (s, i)), pl.BlockSpec((tt, D), lambda i, j, s: (s, j))]
    return pl.pallas_call(
        body, name="mm_tn_into", grid=grid,
        in_specs=specs if fresh else [_ANY] + specs,
        out_specs=pl.BlockSpec((None, rows, D), lambda i, j, s: (i + j, row0 // rows, 0)),
        out_shape=_sds((N_CHIPS, buf, D) if fresh else buf.shape, F32),
        input_output_aliases={} if fresh else {0: 0},
        compiler_params=_cp("parallel", "parallel", "arbitrary"),
    )(*((a, b) if fresh else (buf, a, b)))


def adamw(w, g, m, v):
    rows, cols = w.shape
    tr = rows if rows <= 512 else next(r for r in (512, 384, 256, 128) if rows % r == 0)
    c1 = 1.0 - ADAM_B1 ** ADAM_STEP
    c2 = 1.0 - ADAM_B2 ** ADAM_STEP

    def body(w_ref, g_ref, m_ref, v_ref, d_ref, mo_ref, vo_ref):
        gv = g_ref[...]
        mn = ADAM_B1 * m_ref[...] + (1.0 - ADAM_B1) * gv
        vn = ADAM_B2 * v_ref[...] + (1.0 - ADAM_B2) * (gv * gv)
        mo_ref[...] = mn
        vo_ref[...] = vn
        d_ref[...] = -ADAM_LR * ((mn / c1) / (jnp.sqrt(vn / c2) + ADAM_EPS) + ADAM_WD * w_ref[...])

    spec = pl.BlockSpec((tr, cols), lambda i: (i, 0))
    return pl.pallas_call(
        body, name="adamw", grid=(rows // tr,),
        in_specs=[spec] * 4, out_specs=[spec] * 3, out_shape=[_sds((rows, cols), F32)] * 3,
        compiler_params=_cp("parallel"),
    )(w, g, m, v)


def _place():
    return lax.axis_index("x"), lax.axis_index("y"), lax.axis_index("c")


def _other_chips(x, y):
    return [(1 - x, y), (x, 1 - y), (1 - x, 1 - y)]


_ANY = pl.BlockSpec(memory_space=pl.ANY)


_HBM = pl.BlockSpec(memory_space=pltpu.HBM)
_SEM = pl.BlockSpec(memory_space=pltpu.SEMAPHORE)
_EFFECT = pltpu.SideEffectType.DATAFLOW_SIDE_EFFECTING
N_ICI = 3


def _exchange_start(name, src, land, copies, n):
    def body(src_ref, land_ref, *outs):
        sems, token = outs[:2 * n], outs[-1]
        for j, (s, d, to) in enumerate(copies(src_ref, land_ref, _place())):
            pltpu.make_async_remote_copy(src_ref=s, dst_ref=d, send_sem=sems[j], recv_sem=sems[n + j],
                                         device_id=to, device_id_type=MESH).start()
        token[...] = jnp.zeros_like(token)

    sem = pltpu.SemaphoreType.DMA(())
    outs = pl.pallas_call(
        body, name=name,
        out_shape=(sem,) * (2 * n) + (pltpu.HBM(src.shape, src.dtype), pltpu.HBM(land.shape, land.dtype),
                                      _sds((8, LANES), F32)),
        in_specs=(_HBM, _HBM),
        out_specs=(_SEM,) * (2 * n) + (_HBM, _HBM, pl.BlockSpec(memory_space=pltpu.VMEM)),
        input_output_aliases={0: 2 * n, 1: 2 * n + 1},
        compiler_params=pltpu.CompilerParams(has_side_effects=_EFFECT),
    )(pltpu.with_memory_space_constraint(src, pltpu.HBM), pltpu.with_memory_space_constraint(land, pltpu.HBM))
    return outs[:2 * n], outs[2 * n], outs[2 * n + 1], outs[-1]


def _exchange_wait(name, sems, src, land, after, arrivals):
    n = len(sems) // 2

    def body(src_ref, land_ref, *rest):
        sems = rest[:2 * n]
        for j, (s, d) in enumerate(arrivals(src_ref, land_ref, _place())):
            cp = pltpu.make_async_remote_copy(src_ref=s, dst_ref=d, send_sem=sems[j], recv_sem=sems[n + j],
                                              device_id=_place(), device_id_type=MESH)
            cp.wait_send()
            cp.wait_recv()

    return pl.pallas_call(
        body, name=name, out_shape=(pltpu.HBM(src.shape, src.dtype), pltpu.HBM(land.shape, land.dtype)),
        in_specs=(_HBM, _HBM) + (_SEM,) * (2 * n) + (_ANY,), out_specs=(_HBM, _HBM),
        input_output_aliases={0: 0, 1: 1},
        compiler_params=pltpu.CompilerParams(has_side_effects=_EFFECT),
    )(src, land, *sems, after)


def _halves(c, hh):
    return pl.ds(pl.multiple_of(c * hh, 16), hh), pl.ds(pl.multiple_of((1 - c) * hh, 16), hh)


def gather_start(land, tag):
    _, rr, _ = land.shape
    assert rr % 32 == 0, rr

    def copies(_, land_ref, place):
        x, y, c = place
        mine = land_ref.at[2 * x + y, _halves(c, rr // 2)[0]]
        return [(mine, mine, (cx, cy, c)) for cx, cy in _other_chips(x, y)]

    return _exchange_start(f"gather_start_{tag}", jnp.zeros((8, LANES), F32), land, copies, N_ICI)


def gather_wait(sems, src, land, after, tag):
    def arrivals(_, land_ref, place):
        x, y, c = place
        half = _halves(c, land.shape[1] // 2)[0]
        return [(land_ref.at[2 * x + y, half], land_ref.at[2 * cx + cy, half]) for cx, cy in _other_chips(x, y)]

    return _exchange_wait(f"gather_wait_{tag}", sems, src, land, after, arrivals)


def pass_start(land, tag):
    def copies(_, land_ref, place):
        x, y, c = place
        half = _halves(c, land.shape[1] // 2)[0]
        return [(land_ref.at[2 * cx + cy, half], land_ref.at[2 * cx + cy, half], (x, y, 1 - c))
                for cx, cy in _other_chips(x, y)]

    return _exchange_start(f"pass_start_{tag}", jnp.zeros((8, LANES), F32), land, copies, N_ICI)


def pass_wait(sems, src, land, after, tag):
    def arrivals(_, land_ref, place):
        x, y, c = place
        mine, other = _halves(c, land.shape[1] // 2)
        return [(land_ref.at[2 * cx + cy, mine], land_ref.at[2 * cx + cy, other]) for cx, cy in _other_chips(x, y)]

    return _exchange_wait(f"pass_wait_{tag}", sems, src, land, after, arrivals)


def swap_start(g, tag):
    _, rr, cc = g.shape

    def copies(g_ref, got_ref, place):
        x, y, c = place
        other = _halves(c, rr // 2)[1]
        return [(g_ref.at[k, other], got_ref.at[k], (x, y, 1 - c)) for k in range(N_CHIPS)]

    return _exchange_start(f"swap_start_{tag}", g, lax.empty((N_CHIPS, rr // 2, cc), g.dtype), copies, N_CHIPS)


def swap_wait(sems, g, got, after, tag):
    def arrivals(g_ref, got_ref, place):
        other = _halves(place[2], g.shape[1] // 2)[1]
        return [(g_ref.at[k, other], got_ref.at[k]) for k in range(N_CHIPS)]

    return _exchange_wait(f"swap_wait_{tag}", sems, g, got, after, arrivals)


def chip_sum(place, g32, got):
    _, rr, cc = g32.shape
    hh = rr // 2
    tr = SUM_ROWS
    assert rr % 2 == 0 and hh % tr == 0, (rr, tr)
    nb = hh // tr

    def body(place_ref, g_ref, got_ref, own_ref, all_ref):
        s = g_ref[...] + got_ref[...].astype(F32)
        all_ref[...] = s.astype(BF16)
        own_ref[...] = g_ref[place_ref[1]] + got_ref[place_ref[1]].astype(F32)

    return pl.pallas_call(
        body, name="chip_sum",
        grid_spec=pltpu.PrefetchScalarGridSpec(
            num_scalar_prefetch=1, grid=(nb,),
            in_specs=[pl.BlockSpec((N_CHIPS, tr, cc), lambda i, pr: (0, pr[0] * nb + i, 0)),
                      pl.BlockSpec((N_CHIPS, tr, cc), lambda i, pr: (0, i, 0))],
            out_specs=[pl.BlockSpec((tr, cc), lambda i, pr: (i, 0)),
                       pl.BlockSpec((N_CHIPS, tr, cc), lambda i, pr: (0, i, 0))]),
        out_shape=[_sds((hh, cc), F32), _sds((N_CHIPS, hh, cc), BF16)],
        compiler_params=_cp("parallel"),
    )(place, g32, got)


def _scatter_copies(s_ref, land_ref, place):
    x, y, c = place
    return [(s_ref.at[2 * cx + cy], land_ref.at[j], (cx, cy, c)) for j, (cx, cy) in enumerate(_other_chips(x, y))]


def scatter_start(s, tag):
    return _exchange_start(f"scatter_start_{tag}", s, lax.empty((N_ICI,) + s.shape[1:], s.dtype), _scatter_copies, N_ICI)


def scatter_wait(sems, s, land, after, tag):
    return _exchange_wait(f"scatter_wait_{tag}", sems, s, land, after,
                          lambda s_ref, land_ref, place: [(a, b) for a, b, _ in _scatter_copies(s_ref, land_ref, place)])


def final_sum(place, own, got):
    hh, cc = own.shape
    tr = SUM_ROWS
    assert hh % tr == 0, (hh, tr)
    nb = hh // tr

    def body(place_ref, own_ref, got_ref, o_ref):
        del place_ref
        o_ref[...] = ((own_ref[...] + got_ref[0].astype(F32)) + got_ref[1].astype(F32)) + got_ref[2].astype(F32)

    return pl.pallas_call(
        body, name="final_sum",
        grid_spec=pltpu.PrefetchScalarGridSpec(
            num_scalar_prefetch=1, grid=(nb,),
            in_specs=[pl.BlockSpec((tr, cc), lambda i, pr: (i, 0)), pl.BlockSpec((3, tr, cc), lambda i, pr: (0, i, 0))],
            out_specs=pl.BlockSpec((tr, cc), lambda i, pr: (pr[0] * nb + i, 0))),
        out_shape=_sds((2 * hh, cc), F32),
        compiler_params=_cp("parallel"),
    )(place, own, got)


def share_with_sibling(f):
    rr, cc = f.shape
    hh = rr // 2

    def body(f_ref, o_ref, send_sem, recv_sem):
        del o_ref
        x, y, c = _place()
        mine_half = f_ref.at[pl.ds(pl.multiple_of(c * hh, 8), hh)]
        cp = pltpu.make_async_remote_copy(src_ref=mine_half, dst_ref=mine_half, send_sem=send_sem,
                                          recv_sem=recv_sem, device_id=(x, y, 1 - c), device_id_type=MESH)
        cp.start()
        cp.wait()

    return pl.pallas_call(
        body, name="share_with_sibling", in_specs=[_ANY], out_specs=_ANY, out_shape=_sds(f.shape, f.dtype),
        input_output_aliases={0: 0},
        scratch_shapes=[pltpu.SemaphoreType.DMA, pltpu.SemaphoreType.DMA],
    )(f)


N_DEV = 8


def _peers(place):
    x, y, c = place
    return [((1 - x) if r & 4 else x, (1 - y) if r & 2 else y, (1 - c) if r & 1 else c) for r in range(1, N_DEV)]


def _device_index(place):
    x, y, c = place
    return 4 * x + 2 * y + c


def small_start(land):
    def copies(_, land_ref, place):
        mine = land_ref.at[_device_index(place)]
        return [(mine, mine, to) for to in _peers(place)]

    return _exchange_start("small_start", jnp.zeros((8, LANES), F32), land, copies, N_DEV - 1)


def small_wait(sems, src, land, after):
    def arrivals(_, land_ref, place):
        return [(land_ref.at[_device_index(place)], land_ref.at[_device_index(peer)]) for peer in _peers(place)]

    return _exchange_wait("small_wait", sems, src, land, after, arrivals)


def sum_devices(land):
    _, rr, cc = land.shape
    tr = 56
    assert rr % tr == 0, rr

    def body(l_ref, o_ref):
        acc = l_ref[0]
        for d in range(1, N_DEV):
            acc = acc + l_ref[d]
        o_ref[...] = acc

    return pl.pallas_call(
        body, name="sum_devices", grid=(rr // tr,),
        in_specs=[pl.BlockSpec((N_DEV, tr, cc), lambda i: (0, i, 0))],
        out_specs=pl.BlockSpec((tr, cc), lambda i: (i, 0)), out_shape=_sds((rr, cc), F32),
        compiler_params=_cp("parallel"),
    )(land)


_BIG = ["mla_w_down", "mla_w_uq", "mla_w_ukv", "mla_w_out", "gmlp_w_in", "gmlp_w_out", "ffn_w_up", "ffn_w_down",
        "ple_w_gate", "ple_w_proj"]
_SMALL = ["norm_mix", "norm_ffn", "norm_ple", "mla_q_lora_g", "mla_kv_lora_g", "mla_q_nope_g", "mla_q_rope_g",
          "mla_k_nope_g", "mla_k_rope_g", "gmlp_ln_g", "gmlp_ln_b", "gmlp_w_s", "gmlp_b_s"]

_LAY_MLA = dict(up=0, down=1024, out=2048, gate=2304, wdn=2560, wuq=2736, wukv=2880, proj=3008, rows=3072)
_LAY_GMLP = {"up": 0, "down": 1024, "in": 2048, "out": 3072, "gate": 3584, "proj": 3840, "ln": 3904, "rows": 4096}


def _layer_parts(i):
    j = i // 2
    if i % 2 == 0:
        lay = _LAY_MLA
        return lay, [("ffn_w_up", i, lay["up"]), ("ffn_w_down", i, lay["down"]), ("mla_w_out", j, lay["out"]),
                     ("ple_w_gate", i, lay["gate"]), ("mla_w_down", j, lay["wdn"]), ("mla_w_uq", j, lay["wuq"]),
                     ("mla_w_ukv", j, lay["wukv"]), ("ple_w_proj", i, lay["proj"])]
    lay = _LAY_GMLP
    return lay, [("ffn_w_up", i, lay["up"]), ("ffn_w_down", i, lay["down"]), ("gmlp_w_in", j, lay["in"]),
                 ("gmlp_w_out", j, lay["out"]), ("ple_w_gate", i, lay["gate"]), ("ple_w_proj", i, lay["proj"])]


def _pack_rows(parts, dtype, pad_to=None):
    flat = jnp.concatenate([p.reshape(-1).astype(dtype) for p in parts])
    if pad_to is not None:
        flat = jnp.pad(flat, (0, pad_to * D - flat.size))
    return flat.reshape(-1, D)


def _odd(allw, row0, a, b):
    return allw[:, row0:row0 + a * b // D].reshape(N_CHIPS, a, b)


def _cols_joined(s):
    return jnp.transpose(s, (1, 0, 2)).reshape(s.shape[1], N_CHIPS * s.shape[2])


def _col_shards(full):
    a, bb = full.shape
    return jnp.transpose(full.reshape(a, N_CHIPS, bb // N_CHIPS), (1, 0, 2)).reshape(N_CHIPS, -1, D)


def _pad_lanes(g):
    return jnp.pad(g, ((0, 0), (0, LANES - g.shape[1])))


def _split_uq(wuq):
    l = wuq.shape[0]
    w = wuq.reshape(l, QL, HEADS, DN + DR)
    nope = w[..., :DN].reshape(l, QL, HEADS * DN)
    rope = jnp.pad(w[..., DN:], ((0, 0), (0, 0), (0, 0), (0, LANES - DR))).reshape(l, QL, HEADS * LANES)
    return jnp.concatenate([nope, rope], axis=-1)


def _merge_uq(d):
    nope = d[:, :HEADS * DN].reshape(QL, HEADS, DN)
    rope = d[:, HEADS * DN:].reshape(QL, HEADS, LANES)[..., :DR]
    return jnp.concatenate([nope, rope], axis=-1).reshape(QL, HEADS * (DN + DR))


def _rope_tables(positions):
    inv_freq = ROPE_BASE ** (-(jnp.arange(0, DR, 2, dtype=F32) / DR))
    ang = positions.reshape(-1).astype(F32)[:, None] * inv_freq
    z = jnp.zeros((ang.shape[0], LANES - DR), F32)
    return (jnp.concatenate([jnp.cos(ang), jnp.cos(ang), z], axis=1),
            jnp.concatenate([jnp.sin(ang), jnp.sin(ang), z], axis=1))


def kernel(x, p, positions, norm_mix, norm_ffn, norm_ple, mla_w_down, mla_q_lora_g, mla_kv_lora_g, mla_w_uq, mla_w_ukv, mla_q_nope_g, mla_q_rope_g, mla_k_nope_g, mla_k_rope_g, mla_w_out, gmlp_w_in, gmlp_ln_g, gmlp_ln_b, gmlp_w_s, gmlp_b_s, gmlp_w_out, ffn_w_up, ffn_w_down, ple_w_gate, ple_w_proj, loss_target, m_norm_mix, m_norm_ffn, m_norm_ple, m_mla_w_down, m_mla_q_lora_g, m_mla_kv_lora_g, m_mla_w_uq, m_mla_w_ukv, m_mla_q_nope_g, m_mla_q_rope_g, m_mla_k_nope_g, m_mla_k_rope_g, m_mla_w_out, m_gmlp_w_in, m_gmlp_ln_g, m_gmlp_ln_b, m_gmlp_w_s, m_gmlp_b_s, m_gmlp_w_out, m_ffn_w_up, m_ffn_w_down, m_ple_w_gate, m_ple_w_proj, v_norm_mix, v_norm_ffn, v_norm_ple, v_mla_w_down, v_mla_q_lora_g, v_mla_kv_lora_g, v_mla_w_uq, v_mla_w_ukv, v_mla_q_nope_g, v_mla_q_rope_g, v_mla_k_nope_g, v_mla_k_rope_g, v_mla_w_out, v_gmlp_w_in, v_gmlp_ln_g, v_gmlp_ln_b, v_gmlp_w_s, v_gmlp_b_s, v_gmlp_w_out, v_ffn_w_up, v_ffn_w_down, v_ple_w_gate, v_ple_w_proj):
    args = dict(locals())
    weights = {n: args[n] for n in _BIG + _SMALL}
    depth = norm_mix.shape[0]
    nb, seq, _ = x.shape
    t = nb * seq
    assert seq % TQ == 0 and seq % TM == 0 and t % 512 == 0, (nb, seq)
    cx = lax.axis_index("x")
    cy = lax.axis_index("y")
    cc = lax.axis_index("c")
    chip = 2 * cx + cy

    gathers = []
    token = None
    for i in range(depth):
        lay, parts = _layer_parts(i)
        rows = [weights[n][l] for n, l, _ in parts]
        if token is not None:
            rows[0] = rows[0] + token[0, 0]
        if i % 2 == 1:
            ln = jnp.stack([gmlp_ln_g[i // 2], gmlp_ln_b[i // 2]]).astype(F32)
            rows.append(lax.bitcast_convert_type(ln, BF16))
        mine = _pack_rows(rows, BF16, pad_to=lay["rows"])
        land = lax.dynamic_update_slice(lax.empty((N_CHIPS, lay["rows"], D), BF16), mine[None], (chip, 0, 0))
        sems, src, land, token = gather_start(land, i)
        gathers.append((sems, src, land))
    allw = [None] * depth

    tril = jnp.tril(jnp.ones((GC, GC), F32))
    wm = (gmlp_w_s * tril).astype(BF16)
    wmt = jnp.swapaxes(wm, -1, -2)
    bfull = jnp.repeat(jnp.swapaxes(gmlp_b_s, -1, -2), GD, axis=-1)
    cos, sin = _rope_tables(positions)
    row = lambda g: g.reshape(1, -1)
    gqr = _pad_lanes(mla_q_rope_g)
    gkr = _pad_lanes(mla_k_rope_g)

    h = x.reshape(t, D)
    pt = p.reshape(depth, t, PLE)
    saved = []

    def arrive(i, after):
        sems, src, land = gathers[i]
        _, land = gather_wait(sems, src, land, after, i)
        return pass_start(land, i)

    passing = arrive(0, token)
    for i in range(depth):
        j = i // 2
        lay, _ = _layer_parts(i)
        sems, src, land, token = passing
        _, aw = pass_wait(sems, src, land, token if i == 0 else h, i)
        allw[i] = aw
        s = dict(h=h)
        if i % 2 == 0:
            wdn = jnp.pad(_odd(aw, lay["wdn"], D // N_CHIPS, LAT).reshape(D, LAT), ((0, 0), (0, LATP - LAT)))
            wuq = _split_uq(_cols_joined(_odd(aw, lay["wuq"], QL, 384))[None])[0]
            wukv = _cols_joined(_odd(aw, lay["wukv"], KVL, 512))
            mla_args = (row(norm_mix[i]), wdn, row(mla_q_lora_g[j]), row(mla_kv_lora_g[j]), wuq, wukv,
                        row(mla_q_nope_g[j]), gqr[j:j + 1], row(mla_k_nope_g[j]), gkr[j:j + 1], cos, sin)
            q, k, v = mla_pre_fwd(h, *mla_args)
            y, lse = flash_fwd(q, k, v, seq)
            s.update(q=q, k=k, v=v, lse=lse, mla_args=mla_args)
        else:
            ln = lax.bitcast_convert_type(aw[:, lay["ln"]:lay["ln"] + 2].reshape(N_CHIPS, 2, GH // N_CHIPS, 2), F32)
            ln = jnp.transpose(ln, (1, 0, 2)).reshape(2, 1, GH)
            y, pre = gmlp_fwd(h, row(norm_mix[i]), aw, lay, ln[0], ln[1], wm[j], bfull[j])
            s.update(pre=pre, ln=ln)
        wp = _cols_joined(_odd(aw, lay["proj"], PLE, 256))
        g2 = row(norm_ffn[i])
        if i + 1 < depth:
            passing = arrive(i + 1, y)
            g2 = g2 + passing[3][0:1, 0:1]
        h1, h2, hn2, r = mixffn_fwd(h, y, aw, lay, g2)
        h, hn3 = ple_fwd(h2, pt[i], row(norm_ple[i]), aw, lay, wp)
        s.update(y=y, wp=wp, h1=h1, h2=h2, hn2=hn2, r=r, hn3=hn3)
        saved.append(s)

    dh, loss_part = loss_head(h, loss_target.reshape(t, D))
    loss = lax.psum(loss_part[0, 0], ("x", "y", "c"))

    gs = {n: [None] * weights[n].shape[0] for n in _SMALL}
    gw = {n: [None] * weights[n].shape[0] for n in _BIG}
    place = jnp.stack([cc, chip]).astype(jnp.int32)
    scatters = []
    swapping = None
    token = None

    def put(b, row0, shards):
        return lax.dynamic_update_slice(b, shards.reshape(N_CHIPS, -1, D), (0, row0, 0))

    def swapped(after):
        ii, sems, g, got = swapping
        g, got = swap_wait(sems, g, got, after, ii)
        own, sums = chip_sum(place, g, got)
        sems, sums, land, tok = scatter_start(sums, ii)
        scatters.append((ii, own, sems, sums, land))
        return tok

    for i in reversed(range(depth)):
        j = i // 2
        lay, parts = _layer_parts(i)
        aw = allw[i]
        s = saved[i]

        g3 = row(norm_ple[i])
        if token is not None:
            g3 = g3 + token[0:1, 0:1]
        dh2, dh2b, dgt, dpp, dg3 = ple_bwd(dh, s["h2"], pt[i], g3, aw, lay, s["wp"])
        gs["norm_ple"][i] = dg3[0]
        buf = mm_tn_into(lay["rows"], s["hn3"], dgt, D // N_CHIPS, lay["gate"], False)
        tail = lay.get("ln", lay["rows"])
        if tail < lay["rows"]:
            buf = put(buf, tail, jnp.zeros((N_CHIPS, lay["rows"] - tail, D), F32))
        buf = put(buf, lay["proj"], _col_shards(mm_tn(pt[i], dpp)))
        dh1, dh1b, du, a, dg2 = ffn_bwd(dh2, dh2b, s["h1"], s["r"], row(norm_ffn[i]), aw, lay)
        gs["norm_ffn"][i] = dg2[0]
        buf = mm_tn_into(buf, a, dh2b, D, lay["down"], False)
        buf = mm_tn_into(buf, s["hn2"], du, D, lay["up"], True)
        buf = mm_tn_into(buf, s["y"], dh1b, s["y"].shape[1] // N_CHIPS, lay["out"], False)
        g1 = row(norm_mix[i])
        if swapping is not None:
            g1 = g1 + swapped(dh1)[0:1, 0:1]
        if i % 2 == 0:
            do = linear_nt(dh1b, aw, D // N_CHIPS, lay["out"])
            dq, dk, dv = flash_bwd(s["q"], s["k"], s["v"], s["y"], do, s["lse"], seq)
            (dh, hn1, cq, ckv, dqp, dkvp, dlat, dg1, dgq, dgkv, dgqn, dgqr, dgkn, dgkr) = mla_pre_bwd(
                dq, dk, dv, dh1, s["h"], g1, *s["mla_args"][1:])
            gs["norm_mix"][i] = dg1[0]
            gs["mla_q_lora_g"][j] = dgq[0]
            gs["mla_kv_lora_g"][j] = dgkv[0]
            gs["mla_q_nope_g"][j] = dgqn[0]
            gs["mla_q_rope_g"][j] = dgqr[0, :DR]
            gs["mla_k_nope_g"][j] = dgkn[0]
            gs["mla_k_rope_g"][j] = dgkr[0, :DR]
            buf = put(buf, lay["wdn"], mm_tn(hn1, dlat)[:, :LAT])
            buf = put(buf, lay["wuq"], _col_shards(_merge_uq(mm_tn(cq, dqp))))
            buf = put(buf, lay["wukv"], _col_shards(mm_tn(ckv, dkvp)))
        else:
            dh, hn1, dpre, dws, dbs, dlng, dlnb, dg1 = gmlp_bwd(
                dh1, dh1b, s["h"], s["pre"], g1, aw, lay, s["ln"][0], s["ln"][1], wm[j], wmt[j], bfull[j], tril)
            gs["norm_mix"][i] = dg1[0]
            gs["gmlp_ln_g"][j] = dlng[0]
            gs["gmlp_ln_b"][j] = dlnb[0]
            gs["gmlp_w_s"][j] = dws
            gs["gmlp_b_s"][j] = jnp.sum(dbs.reshape(GC, GG, GD), axis=-1).T
            buf = mm_tn_into(buf, hn1, dpre, D, lay["in"], True)

        sems, buf, got, token = swap_start(buf, i)
        swapping = (i, sems, buf, got)
    swapped(dh)
    grad_x = dh.reshape(x.shape)

    small_sizes = [weights[n].size if n not in ("gmlp_ln_g", "gmlp_ln_b") else weights[n].shape[0] * GH
                   for n in _SMALL]
    small_rows = -(-sum(small_sizes) // (56 * D)) * 56
    part = _pack_rows([jnp.stack(gs[n]) for n in _SMALL], F32, pad_to=small_rows)
    land = lax.dynamic_update_slice(lax.empty((N_DEV, small_rows, D), F32), part[None], (2 * chip + cc, 0, 0))
    small = small_start(land)

    after = small[3]
    for i, own, sems, sums, land in scatters:
        _, got = scatter_wait(sems, sums, land, after, i)
        after = reduced = share_with_sibling(final_sum(place, own, got))
        for n, l, row0 in _layer_parts(i)[1]:
            gw[n][l] = reduced[row0:row0 + weights[n][l].size // D].reshape(weights[n].shape[1:])
    grads = {n: jnp.stack(gw[n]) for n in _BIG}

    tot = sum_devices(small_wait(small[0], small[1], small[2], after)[1]).reshape(-1)
    off = 0
    for n, sz in zip(_SMALL, small_sizes):
        gsum = tot[off:off + sz]
        off += sz
        if n in ("gmlp_ln_g", "gmlp_ln_b"):
            gsum = lax.dynamic_slice_in_dim(gsum.reshape(-1, GH), chip * (GH // N_CHIPS), GH // N_CHIPS, axis=1)
        grads[n] = gsum.reshape(weights[n].shape)

    delta, new_m, new_v = {}, {}, {}
    for n in _BIG:
        w2 = weights[n].reshape(-1, weights[n].shape[-1])
        d, mn, vn = adamw(w2, grads[n].reshape(w2.shape), args["m_" + n].reshape(w2.shape),
                          args["v_" + n].reshape(w2.shape))
        delta[n], new_m[n], new_v[n] = (a.reshape(weights[n].shape) for a in (d, mn, vn))
    own_sizes = [weights[n].size for n in _SMALL]
    own_rows = -(-sum(own_sizes) // (8 * D)) * 8
    packed = [_pack_rows([src[n] for n in _SMALL], F32, pad_to=own_rows)
              for src in (weights, grads, {n: args["m_" + n] for n in _SMALL}, {n: args["v_" + n] for n in _SMALL})]
    outs = adamw(*packed)
    off = 0
    for n, sz in zip(_SMALL, own_sizes):
        for dst, o in zip((delta, new_m, new_v), outs):
            dst[n] = o.reshape(-1)[off:off + sz].reshape(weights[n].shape)
        off += sz

    order = ["norm_mix", "norm_ffn", "norm_ple", "mla_w_down", "mla_q_lora_g", "mla_kv_lora_g", "mla_w_uq",
             "mla_w_ukv", "mla_q_nope_g", "mla_q_rope_g", "mla_k_nope_g", "mla_k_rope_g", "mla_w_out", "gmlp_w_in",
             "gmlp_ln_g", "gmlp_ln_b", "gmlp_w_s", "gmlp_b_s", "gmlp_w_out", "ffn_w_up", "ffn_w_down", "ple_w_gate",
             "ple_w_proj"]
    return (loss, grad_x, *[grads[n] for n in order], *[delta[n] for n in order], *[new_m[n] for n in order],
            *[new_v[n] for n in order])
```

```python
import functools

import jax
import jax.numpy as jnp
from jax import lax
from jax.experimental import pallas as pl
from jax.experimental.pallas import tpu as pltpu

F32 = jnp.float32
BF16 = jnp.bfloat16
MESH = pl.DeviceIdType.MESH

D = 1024
HEADS = 8
DN = 128
DR = 64
QL = 384
KVL = 256
LAT = 704
LATP = 768
DFF = 4096
GH = 2048
GC = 128
GG = 8
GD = 256
PLE = 256
EPS = 1e-6
ROPE_BASE = 10000.0
SM_SCALE = (DN + DR) ** -0.5
N_CHIPS = 4
LANES = 128

ADAM_LR = 0.001
ADAM_B1 = 0.9
ADAM_B2 = 0.999
ADAM_EPS = 1e-08
ADAM_WD = 0.01
ADAM_STEP = 10

TM = 256
TMB = 512
TQ = 512
TQ_FWD = 512
FWD_HEADS = 1
SUM_ROWS = 256
VMEM_LIMIT = 56 * 1024 * 1024


def _cp(*sem):
    return pltpu.CompilerParams(dimension_semantics=sem, vmem_limit_bytes=VMEM_LIMIT)


def _dot(a, b):
    return jnp.dot(a, b, preferred_element_type=F32)


def _dot_nt(a, b):
    return lax.dot_general(a, b, (((1,), (1,)), ((), ())), preferred_element_type=F32)


def _dot_tn(a, b):
    return lax.dot_general(a, b, (((0,), (0,)), ((), ())), preferred_element_type=F32)


def _rms(x, g, n):
    r = lax.rsqrt(jnp.sum(x * x, axis=-1, keepdims=True) * (1.0 / n) + EPS)
    xhat = x * r
    return xhat * g, xhat, r


def _rms_bwd(dy, g, xhat, r, n):
    dxhat = dy * g
    return r * (dxhat - xhat * (jnp.sum(dxhat * xhat, axis=-1, keepdims=True) * (1.0 / n)))


def _rope(x, c, s):
    return x * c + (pltpu.roll(x, 32, 1) - pltpu.roll(x, 96, 1)) * s


def _rope_t(dy, c, s):
    w = dy * s
    return dy * c + pltpu.roll(w, 96, 1) - pltpu.roll(w, 32, 1)


def _sigmoid(x):
    return 1.0 / (1.0 + jnp.exp(-x))


_GELU_K = 0.7978845608028654
_GELU_C = 0.044715


def _gelu(x):
    return 0.5 * x * (1.0 + jnp.tanh(_GELU_K * (x + _GELU_C * x * x * x)))


def _gelu_and_grad(x):
    x2 = x * x
    t = jnp.tanh(_GELU_K * (x + _GELU_C * x2 * x))
    half = 0.5 * (1.0 + t)
    return x * half, half + 0.5 * x * (1.0 - t * t) * (_GELU_K * (1.0 + 3.0 * _GELU_C * x2))


def _acc_rows(ref, val):
    ref[...] += jnp.broadcast_to(jnp.sum(val, axis=0, keepdims=True), ref.shape)


def _row(tm, c):
    return pl.BlockSpec((tm, c), lambda i: (i, 0))


def _const(shape):
    nd = len(shape)
    return pl.BlockSpec(shape, lambda i: (0,) * nd, pipeline_mode=pl.Buffered(1))


def _wblk(rows, row0):
    assert row0 % rows == 0, (rows, row0)
    return pl.BlockSpec((N_CHIPS, rows, D), lambda i: (0, row0 // rows, 0), pipeline_mode=pl.Buffered(1))


def _rows_joined(w_ref):
    return w_ref[...].reshape(N_CHIPS * w_ref.shape[1], D)


def _sds(shape, dtype):
    return jax.ShapeDtypeStruct(shape, dtype)


def mixffn_fwd(h, y, allw, lay, g2):
    t, k = y.shape

    def body(h_ref, y_ref, wo_ref, g_ref, wu_ref, wd_ref, h1_ref, h2_ref, hn_ref, r_ref):
        h1 = h_ref[...] + _dot(y_ref[...], _rows_joined(wo_ref))
        h1_ref[...] = h1
        yn, _, _ = _rms(h1, g_ref[...], D)
        hn = yn.astype(BF16)
        hn_ref[...] = hn
        f = jnp.zeros((TMB, D), F32)
        for c in range(N_CHIPS):
            r = jnp.maximum(_dot(hn, wu_ref[c]), 0.0)
            r_ref[:, c * D:(c + 1) * D] = r.astype(BF16)
            f = f + _dot((r * r).astype(BF16), wd_ref[c])
        h2_ref[...] = h1 + f

    return pl.pallas_call(
        body, name="mixffn_fwd", grid=(t // TMB,),
        in_specs=[_row(TMB, D), _row(TMB, k), _wblk(k // N_CHIPS, lay["out"]), _const((1, D)), _wblk(D, lay["up"]),
                  _wblk(D, lay["down"])],
        out_specs=[_row(TMB, D), _row(TMB, D), _row(TMB, D), _row(TMB, DFF)],
        out_shape=[_sds((t, D), F32), _sds((t, D), F32), _sds((t, D), BF16), _sds((t, DFF), BF16)],
        compiler_params=_cp("parallel"),
    )(h, y, allw, g2, allw, allw)


def ple_fwd(h2, p, g3, allw, lay, wp):
    t = h2.shape[0]

    def body(h_ref, p_ref, g_ref, wg_ref, wp_ref, h3_ref, hn_ref, sg_ref, pp_ref):
        x = h_ref[...]
        yn, _, _ = _rms(x, g_ref[...], D)
        hn = yn.astype(BF16)
        hn_ref[...] = hn
        sg = _sigmoid(_dot(hn, _rows_joined(wg_ref)))
        pp = _dot(p_ref[...].astype(BF16), wp_ref[...])
        sg_ref[...] = sg.astype(BF16)
        pp_ref[...] = pp.astype(BF16)
        h3_ref[...] = x + sg * pp

    return pl.pallas_call(
        body, name="ple_fwd", grid=(t // TMB,),
        in_specs=[_row(TMB, D), _row(TMB, PLE), _const((1, D)), _wblk(D // N_CHIPS, lay["gate"]), _const((PLE, D))],
        out_specs=[_row(TMB, D)] * 4,
        out_shape=[_sds((t, D), F32), _sds((t, D), BF16), _sds((t, D), BF16), _sds((t, D), BF16)],
        compiler_params=_cp("parallel"),
    )(h2, p, g3, allw, wp)


def _mla_project(h_ref, g1_ref, wdn_ref, gq_ref, gkv_ref, wuq_ref, wukv_ref):
    x = h_ref[...]
    yn, xhat, rx = _rms(x, g1_ref[...], D)
    hn = yn.astype(BF16)
    lat = _dot(hn, wdn_ref[...])
    cq, cqhat, rq = _rms(lat[:, :QL], gq_ref[...], QL)
    ckv, ckvhat, rkv = _rms(lat[:, QL:QL + KVL], gkv_ref[...], KVL)
    kr_raw = lat[:, QL + KVL:]
    cqb = cq.astype(BF16)
    ckvb = ckv.astype(BF16)
    qp = _dot(cqb, wuq_ref[...])
    kvp = _dot(ckvb, wukv_ref[...])
    return dict(xhat=xhat, rx=rx, hn=hn, cqhat=cqhat, rq=rq, ckvhat=ckvhat, rkv=rkv, kr_raw=kr_raw,
                cqb=cqb, ckvb=ckvb, qp=qp, kvp=kvp)


def mla_pre_fwd(h, g1, wdn, gq, gkv, wuq, wukv, gqn, gqr, gkn, gkr, cos, sin):
    t = h.shape[0]

    def body(h_ref, g1_ref, wdn_ref, gq_ref, gkv_ref, wuq_ref, wukv_ref, gqn_ref, gqr_ref, gkn_ref, gkr_ref,
             c_ref, s_ref, q_ref, k_ref, v_ref):
        m = _mla_project(h_ref, g1_ref, wdn_ref, gq_ref, gkv_ref, wuq_ref, wukv_ref)
        c = c_ref[...]
        s = s_ref[...]
        kr, _, _ = _rms(m["kr_raw"], gkr_ref[...], DR)
        krb = _rope(kr, c, s).astype(BF16)
        for hd in range(HEADS):
            qn, _, _ = _rms(m["qp"][:, hd * DN:(hd + 1) * DN], gqn_ref[...], DN)
            qr, _, _ = _rms(m["qp"][:, D + hd * LANES:D + (hd + 1) * LANES], gqr_ref[...], DR)
            q_ref[hd, :, 0:DN] = (qn * SM_SCALE).astype(BF16)
            q_ref[hd, :, DN:2 * DN] = (_rope(qr, c, s) * SM_SCALE).astype(BF16)
            kn, _, _ = _rms(m["kvp"][:, hd * 2 * DN:hd * 2 * DN + DN], gkn_ref[...], DN)
            k_ref[hd, :, 0:DN] = kn.astype(BF16)
            k_ref[hd, :, DN:2 * DN] = krb
            v_ref[hd] = m["kvp"][:, hd * 2 * DN + DN:(hd + 1) * 2 * DN].astype(BF16)

    hb = lambda w: pl.BlockSpec((HEADS, TM, w), lambda i: (0, i, 0))
    return pl.pallas_call(
        body, name="mla_pre_fwd", grid=(t // TM,),
        in_specs=[_row(TM, D), _const((1, D)), _const((D, LATP)), _const((1, QL)), _const((1, KVL)),
                  _const((QL, 2 * D)), _const((KVL, 2 * D)), _const((1, LANES)), _const((1, LANES)),
                  _const((1, LANES)), _const((1, LANES)), _row(TM, LANES), _row(TM, LANES)],
        out_specs=[hb(2 * DN), hb(2 * DN), hb(DN)],
        out_shape=[_sds((HEADS, t, 2 * DN), BF16), _sds((HEADS, t, 2 * DN), BF16), _sds((HEADS, t, DN), BF16)],
        compiler_params=_cp("parallel"),
    )(h, g1, wdn, gq, gkv, wuq, wukv, gqn, gqr, gkn, gkr, cos, sin)


def _diagonal_mask(n=TQ):
    return lax.broadcasted_iota(jnp.int32, (n, n), 1) <= lax.broadcasted_iota(jnp.int32, (n, n), 0)


def flash_fwd(q, k, v, seq):
    t = q.shape[1]
    nb = t // seq
    tq = TQ_FWD
    nq = seq // tq
    hp = FWD_HEADS

    def body(q_ref, k_ref, v_ref, o_ref, lse_ref):
        qi = pl.program_id(2)
        qs = [q_ref[a] for a in range(hp)]

        def step(j, carry, diagonal=False):
            rows = pl.ds(pl.multiple_of(j * tq, tq), tq)
            out = []
            for a in range(hp):
                m, l, acc = carry[a]
                s = _dot_nt(qs[a], k_ref[a, rows, :])
                if diagonal:
                    s = jnp.where(_diagonal_mask(tq), s, -1e30)
                m_new = jnp.maximum(m, jnp.max(s, axis=-1, keepdims=True))
                p = jnp.exp(s - m_new)
                alpha = jnp.exp(m - m_new)
                l = alpha * l + jnp.sum(p, axis=-1, keepdims=True)
                acc = alpha * acc + _dot(p.astype(BF16), v_ref[a, rows, :])
                out.append((m_new, l, acc))
            return tuple(out)

        one = (jnp.full((tq, 1), -1e30, F32), jnp.zeros((tq, 1), F32), jnp.zeros((tq, DN), F32))
        done = step(qi, lax.fori_loop(0, qi, step, (one,) * hp), diagonal=True)
        for a, (m, l, acc) in enumerate(done):
            o_ref[:, a * DN:(a + 1) * DN] = (acc / l).astype(BF16)
            lse_ref[a] = m + jnp.log(l)

    return pl.pallas_call(
        body, name="flash_fwd", grid=(nb, HEADS // hp, nq),
        in_specs=[pl.BlockSpec((hp, tq, 2 * DN), lambda b, h, i: (h, b * nq + i, 0)),
                  pl.BlockSpec((hp, seq, 2 * DN), lambda b, h, i: (h, b, 0)),
                  pl.BlockSpec((hp, seq, DN), lambda b, h, i: (h, b, 0))],
        out_specs=[pl.BlockSpec((tq, hp * DN), lambda b, h, i: (b * nq + i, h)),
                   pl.BlockSpec((hp, tq, 1), lambda b, h, i: (h, b * nq + i, 0))],
        out_shape=[_sds((t, HEADS * DN), BF16), _sds((HEADS, t, 1), F32)],
        compiler_params=_cp("parallel", "parallel", "arbitrary"),
    )(q, k, v)


def _gmlp_in(hn, win_ref):
    pre = [_dot(hn, win_ref[c]) for c in range(N_CHIPS)]
    return jnp.concatenate(pre[:2], axis=1), jnp.concatenate(pre[2:], axis=1)


def gmlp_fwd(h, g1, allw, lay, lng, lnb, wm, bfull):
    t = h.shape[0]

    def body(h_ref, g1_ref, win_ref, lng_ref, lnb_ref, wm_ref, b_ref, y_ref, pre_ref):
        yn, _, _ = _rms(h_ref[...], g1_ref[...], D)
        pre_u, pre_v = _gmlp_in(yn.astype(BF16), win_ref)
        pre_ref[:, :GH] = pre_u.astype(BF16)
        pre_ref[:, GH:] = pre_v.astype(BF16)
        u = _gelu(pre_u)
        v = _gelu(pre_v)
        xc = v - jnp.mean(v, axis=-1, keepdims=True)
        rs = lax.rsqrt(jnp.mean(xc * xc, axis=-1, keepdims=True) + EPS)
        vnb = (xc * rs * lng_ref[...] + lnb_ref[...]).astype(BF16)
        for ch in range(TM // GC):
            rows = slice(ch * GC, (ch + 1) * GC)
            for g in range(GG):
                cols = slice(g * GD, (g + 1) * GD)
                sv = _dot(wm_ref[g], vnb[rows, cols]) + b_ref[:, cols]
                y_ref[rows, cols] = (u[rows, cols] * sv).astype(BF16)

    return pl.pallas_call(
        body, name="gmlp_fwd", grid=(t // TM,),
        in_specs=[_row(TM, D), _const((1, D)), _wblk(D, lay["in"]), _const((1, GH)), _const((1, GH)),
                  _const((GG, GC, GC)), _const((GC, GH))],
        out_specs=[_row(TM, GH), _row(TM, 2 * GH)],
        out_shape=[_sds((t, GH), BF16), _sds((t, 2 * GH), BF16)],
        compiler_params=_cp("parallel"),
    )(h, g1, allw, lng, lnb, wm, bfull)


def loss_head(h, tgt):
    t = h.shape[0]

    def body(h_ref, t_ref, dh_ref, loss_ref):
        @pl.when(pl.program_id(0) == 0)
        def _():
            loss_ref[...] = jnp.zeros_like(loss_ref)

        e = h_ref[...] - t_ref[...]
        dh_ref[...] = e * (1.0 / D)
        part = jnp.sum(jnp.sum(e * e, axis=-1, keepdims=True), axis=0, keepdims=True) * (0.5 / D)
        loss_ref[...] += jnp.broadcast_to(part, loss_ref.shape)

    return pl.pallas_call(
        body, name="loss_head", grid=(t // TMB,),
        in_specs=[_row(TMB, D), _row(TMB, D)],
        out_specs=[_row(TMB, D), _const((8, LANES))],
        out_shape=[_sds((t, D), F32), _sds((8, LANES), F32)],
        compiler_params=_cp("arbitrary"),
    )(h, tgt)


def _zero_at_first_step(*refs):
    @pl.when(pl.program_id(0) == 0)
    def _():
        for r in refs:
            r[...] = jnp.zeros_like(r)


def ple_bwd(dh3, h2, sg, pp, g3, allw, lay):
    t = h2.shape[0]

    def body(dh_ref, h_ref, sg_ref, pp_ref, g_ref, wg_ref, dh2_ref, dh2b_ref, dgt_ref, dpp_ref, dg_ref):
        _zero_at_first_step(dg_ref)
        dh3v = dh_ref[...]
        g = g_ref[...]
        _, xhat, r = _rms(h_ref[...], g, D)
        sg = sg_ref[...].astype(F32)
        dgt = (dh3v * pp_ref[...].astype(F32) * sg * (1.0 - sg)).astype(BF16)
        dgt_ref[...] = dgt
        dpp_ref[...] = (dh3v * sg).astype(BF16)
        dhn = _dot_nt(dgt, _rows_joined(wg_ref))
        _acc_rows(dg_ref, dhn * xhat)
        dh2 = dh3v + _rms_bwd(dhn, g, xhat, r, D)
        dh2_ref[...] = dh2
        dh2b_ref[...] = dh2.astype(BF16)

    return pl.pallas_call(
        body, name="ple_bwd", grid=(t // TMB,),
        in_specs=[_row(TMB, D), _row(TMB, D), _row(TMB, D), _row(TMB, D), _const((1, D)),
                  _wblk(D // N_CHIPS, lay["gate"])],
        out_specs=[_row(TMB, D), _row(TMB, D), _row(TMB, D), _row(TMB, D), _const((8, D))],
        out_shape=[_sds((t, D), F32), _sds((t, D), BF16), _sds((t, D), BF16), _sds((t, D), BF16), _sds((8, D), F32)],
        compiler_params=_cp("arbitrary"),
    )(dh3, h2, sg, pp, g3, allw)


def ffn_bwd(dh2, dh2b, h1, r, g2, allw, lay):
    t = h1.shape[0]

    def body(dh_ref, dhb_ref, h_ref, r_ref, g_ref, wu_ref, wd_ref, dh1_ref, dh1b_ref, du_ref, a_ref, dg_ref):
        _zero_at_first_step(dg_ref)
        dhb = dhb_ref[...]
        g = g_ref[...]
        _, xhat, rr = _rms(h_ref[...], g, D)
        dhn = jnp.zeros((TM, D), F32)
        for c in range(N_CHIPS):
            cs = slice(c * D, (c + 1) * D)
            rc = r_ref[:, cs].astype(F32)
            a_ref[:, cs] = (rc * rc).astype(BF16)
            da = _dot_nt(dhb, wd_ref[c])
            du = (da * (2.0 * rc)).astype(BF16)
            du_ref[:, cs] = du
            dhn = dhn + _dot_nt(du, wu_ref[c])
        _acc_rows(dg_ref, dhn * xhat)
        dh1 = dh_ref[...] + _rms_bwd(dhn, g, xhat, rr, D)
        dh1_ref[...] = dh1
        dh1b_ref[...] = dh1.astype(BF16)

    return pl.pallas_call(
        body, name="ffn_bwd", grid=(t // TM,),
        in_specs=[_row(TM, D), _row(TM, D), _row(TM, D), _row(TM, DFF), _const((1, D)), _wblk(D, lay["up"]),
                  _wblk(D, lay["down"])],
        out_specs=[_row(TM, D), _row(TM, D), _row(TM, DFF), _row(TM, DFF), _const((8, D))],
        out_shape=[_sds((t, D), F32), _sds((t, D), BF16), _sds((t, DFF), BF16), _sds((t, DFF), BF16),
                   _sds((8, D), F32)],
        compiler_params=_cp("arbitrary"),
    )(dh2, dh2b, h1, r, g2, allw, allw)


def linear_nt(a, allw, rows, row0):
    t = a.shape[0]
    k = N_CHIPS * rows

    def body(a_ref, w_ref, o_ref):
        o_ref[...] = _dot_nt(a_ref[...], _rows_joined(w_ref)).astype(BF16)

    return pl.pallas_call(
        body, name="linear_nt", grid=(t // TMB,),
        in_specs=[_row(TMB, D), _wblk(rows, row0)],
        out_specs=_row(TMB, k),
        out_shape=_sds((t, k), BF16),
        compiler_params=_cp("parallel"),
    )(a, allw)


def flash_bwd(q, k, v, o, do, lse, seq):
    t = q.shape[1]
    nb = t // seq
    nq = seq // TQ

    def body(q_ref, k_ref, v_ref, o_ref, do_ref, lse_ref, dq_ref, dk_ref, dv_ref):
        kj = pl.program_id(2)

        @pl.when(kj == 0)
        def _():
            dq_ref[...] = jnp.zeros_like(dq_ref)

        kv = k_ref[0]
        vv = v_ref[0]

        def step(i, carry, diagonal=False):
            dk, dv = carry
            rows = pl.ds(pl.multiple_of(i * TQ, TQ), TQ)
            qv = q_ref[0, rows, :]
            dov = do_ref[rows, :]
            delta = jnp.sum(dov.astype(F32) * o_ref[rows, :].astype(F32), axis=-1, keepdims=True)
            s = _dot_nt(qv, kv)
            if diagonal:
                s = jnp.where(_diagonal_mask(), s, -1e30)
            p = jnp.exp(s - lse_ref[0, rows, :])
            dp = _dot_nt(dov, vv)
            ds = (p * (dp - delta)).astype(BF16)
            dv = dv + _dot_tn(p.astype(BF16), dov)
            dk = dk + _dot_tn(ds, qv)
            dq_ref[0, rows, :] += _dot(ds, kv)
            return dk, dv

        init = (jnp.zeros((TQ, 2 * DN), F32), jnp.zeros((TQ, DN), F32))
        dk, dv = lax.fori_loop(kj + 1, nq, step, step(kj, init, diagonal=True))
        dk_ref[0] = dk
        dv_ref[0] = dv

    return pl.pallas_call(
        body, name="flash_bwd", grid=(nb, HEADS, nq),
        in_specs=[pl.BlockSpec((1, seq, 2 * DN), lambda b, h, j: (h, b, 0)),
                  pl.BlockSpec((1, TQ, 2 * DN), lambda b, h, j: (h, b * nq + j, 0)),
                  pl.BlockSpec((1, TQ, DN), lambda b, h, j: (h, b * nq + j, 0)),
                  pl.BlockSpec((seq, DN), lambda b, h, j: (b, h)),
                  pl.BlockSpec((seq, DN), lambda b, h, j: (b, h)),
                  pl.BlockSpec((1, seq, 1), lambda b, h, j: (h, b, 0))],
        out_specs=[pl.BlockSpec((1, seq, 2 * DN), lambda b, h, j: (h, b, 0)),
                   pl.BlockSpec((1, TQ, 2 * DN), lambda b, h, j: (h, b * nq + j, 0)),
                   pl.BlockSpec((1, TQ, DN), lambda b, h, j: (h, b * nq + j, 0))],
        out_shape=[_sds((HEADS, t, 2 * DN), F32), _sds((HEADS, t, 2 * DN), F32), _sds((HEADS, t, DN), F32)],
        compiler_params=_cp("parallel", "parallel", "arbitrary"),
    )(q, k, v, o, do, lse)


def mla_pre_bwd(dq, dk, dv, dh1, h, g1, wdn, gq, gkv, wuq, wukv, gqn, gqr, gkn, gkr, cos, sin):
    t = h.shape[0]

    def body(dq_ref, dk_ref, dv_ref, dh1_ref, h_ref, g1_ref, wdn_ref, gq_ref, gkv_ref, wuq_ref, wukv_ref,
             gqn_ref, gqr_ref, gkn_ref, gkr_ref, c_ref, s_ref,
             dh_ref, hn_ref, cq_ref, ckv_ref, dqp_ref, dkvp_ref, dlat_ref,
             dg1_ref, dgq_ref, dgkv_ref, dgqn_ref, dgqr_ref, dgkn_ref, dgkr_ref):
        _zero_at_first_step(dg1_ref, dgq_ref, dgkv_ref, dgqn_ref, dgqr_ref, dgkn_ref, dgkr_ref)
        m = _mla_project(h_ref, g1_ref, wdn_ref, gq_ref, gkv_ref, wuq_ref, wukv_ref)
        hn_ref[...] = m["hn"]
        cq_ref[...] = m["cqb"]
        ckv_ref[...] = m["ckvb"]
        c = c_ref[...]
        s = s_ref[...]
        gqn = gqn_ref[...]
        gqr = gqr_ref[...]
        gkn = gkn_ref[...]
        gkr = gkr_ref[...]

        dkr = dk_ref[0, :, DN:2 * DN]
        for hd in range(1, HEADS):
            dkr = dkr + dk_ref[hd, :, DN:2 * DN]
        dkr = _rope_t(dkr, c, s)
        _, krhat, rkr = _rms(m["kr_raw"], gkr, DR)
        _acc_rows(dgkr_ref, dkr * krhat)
        dkr_raw = _rms_bwd(dkr, gkr, krhat, rkr, DR)

        for hd in range(HEADS):
            ncols = slice(hd * DN, (hd + 1) * DN)
            _, xh, r = _rms(m["qp"][:, ncols], gqn, DN)
            dqn = dq_ref[hd, :, 0:DN] * SM_SCALE
            _acc_rows(dgqn_ref, dqn * xh)
            dqp_ref[:, ncols] = _rms_bwd(dqn, gqn, xh, r, DN).astype(BF16)

            rcols = slice(D + hd * LANES, D + (hd + 1) * LANES)
            _, xh, r = _rms(m["qp"][:, rcols], gqr, DR)
            dqr = _rope_t(dq_ref[hd, :, DN:2 * DN] * SM_SCALE, c, s)
            _acc_rows(dgqr_ref, dqr * xh)
            dqp_ref[:, rcols] = _rms_bwd(dqr, gqr, xh, r, DR).astype(BF16)

            kcols = slice(hd * 2 * DN, hd * 2 * DN + DN)
            _, xh, r = _rms(m["kvp"][:, kcols], gkn, DN)
            dkn = dk_ref[hd, :, 0:DN]
            _acc_rows(dgkn_ref, dkn * xh)
            dkvp_ref[:, kcols] = _rms_bwd(dkn, gkn, xh, r, DN).astype(BF16)
            dkvp_ref[:, hd * 2 * DN + DN:(hd + 1) * 2 * DN] = dv_ref[hd].astype(BF16)

        dcq = _dot_nt(dqp_ref[...], wuq_ref[...])
        _acc_rows(dgq_ref, dcq * m["cqhat"])
        dlat_q = _rms_bwd(dcq, gq_ref[...], m["cqhat"], m["rq"], QL)
        dckv = _dot_nt(dkvp_ref[...], wukv_ref[...])
        _acc_rows(dgkv_ref, dckv * m["ckvhat"])
        dlat_kv = _rms_bwd(dckv, gkv_ref[...], m["ckvhat"], m["rkv"], KVL)
        dlat = jnp.concatenate([dlat_q, dlat_kv, dkr_raw], axis=1).astype(BF16)
        dlat_ref[...] = dlat
        dhn = _dot_nt(dlat, wdn_ref[...])
        _acc_rows(dg1_ref, dhn * m["xhat"])
        dh_ref[...] = dh1_ref[...] + _rms_bwd(dhn, g1_ref[...], m["xhat"], m["rx"], D)

    hb = lambda w: pl.BlockSpec((HEADS, TM, w), lambda i: (0, i, 0))
    return pl.pallas_call(
        body, name="mla_pre_bwd", grid=(t // TM,),
        in_specs=[hb(2 * DN), hb(2 * DN), hb(DN), _row(TM, D), _row(TM, D), _const((1, D)), _const((D, LATP)),
                  _const((1, QL)), _const((1, KVL)), _const((QL, 2 * D)), _const((KVL, 2 * D)),
                  _const((1, LANES)), _const((1, LANES)), _const((1, LANES)), _const((1, LANES)),
                  _row(TM, LANES), _row(TM, LANES)],
        out_specs=[_row(TM, D), _row(TM, D), _row(TM, QL), _row(TM, KVL), _row(TM, 2 * D), _row(TM, 2 * D),
                   _row(TM, LATP), _const((8, D)), _const((8, QL)), _const((8, KVL)), _const((8, LANES)),
                   _const((8, LANES)), _const((8, LANES)), _const((8, LANES))],
        out_shape=[_sds((t, D), F32), _sds((t, D), BF16), _sds((t, QL), BF16), _sds((t, KVL), BF16),
                   _sds((t, 2 * D), BF16), _sds((t, 2 * D), BF16), _sds((t, LATP), BF16),
                   _sds((8, D), F32), _sds((8, QL), F32), _sds((8, KVL), F32), _sds((8, LANES), F32),
                   _sds((8, LANES), F32), _sds((8, LANES), F32), _sds((8, LANES), F32)],
        compiler_params=_cp("arbitrary"),
    )(dq, dk, dv, dh1, h, g1, wdn, gq, gkv, wuq, wukv, gqn, gqr, gkn, gkr, cos, sin)


def gmlp_bwd(dh1, dh1b, h, pre, g1, allw, lay, lng, lnb, wm, wmt, bfull, tril):
    t = h.shape[0]

    def body(dh1_ref, dh1b_ref, h_ref, pre_ref, g1_ref, win_ref, lng_ref, lnb_ref, wm_ref, wmt_ref, b_ref,
             wout_ref, tril_ref, dh_ref, hn_ref, dpre_ref, dws_ref, dbs_ref, dlng_ref, dlnb_ref, dg1_ref,
             dvn_s):
        _zero_at_first_step(dws_ref, dbs_ref, dlng_ref, dlnb_ref, dg1_ref)
        g1 = g1_ref[...]
        yn, xhat, rx = _rms(h_ref[...], g1, D)
        hn_ref[...] = yn.astype(BF16)
        dy = _dot_nt(dh1b_ref[...], _rows_joined(wout_ref))
        pre_u = pre_ref[:, :GH].astype(F32)
        pre_v = pre_ref[:, GH:].astype(F32)
        u, gg_u = _gelu_and_grad(pre_u)
        v, gg_v = _gelu_and_grad(pre_v)
        xc = v - jnp.mean(v, axis=-1, keepdims=True)
        rs = lax.rsqrt(jnp.mean(xc * xc, axis=-1, keepdims=True) + EPS)
        vhat = xc * rs
        lng = lng_ref[...]
        vnb = (vhat * lng + lnb_ref[...]).astype(BF16)
        dsv = dy * u
        dsvb = dsv.astype(BF16)
        tril_m = tril_ref[...]
        for ch in range(TM // GC):
            rows = slice(ch * GC, (ch + 1) * GC)
            dbs_ref[...] += dsv[rows, :]
            for g in range(GG):
                cols = slice(g * GD, (g + 1) * GD)
                sv = _dot(wm_ref[g], vnb[rows, cols]) + b_ref[:, cols]
                dpre_ref[rows, cols] = (dy[rows, cols] * sv * gg_u[rows, cols]).astype(BF16)
                dvn_s[rows, cols] = _dot(wmt_ref[g], dsvb[rows, cols])
                dws_ref[g] += _dot_nt(dsvb[rows, cols], vnb[rows, cols]) * tril_m
        dvn = dvn_s[...]
        _acc_rows(dlng_ref, dvn * vhat)
        _acc_rows(dlnb_ref, dvn)
        dvhat = dvn * lng
        dv = rs * (dvhat - jnp.mean(dvhat, axis=-1, keepdims=True)
                   - vhat * jnp.mean(dvhat * vhat, axis=-1, keepdims=True))
        dpre_v = (dv * gg_v).astype(BF16)
        dpre_ref[:, GH:] = dpre_v
        dhn = _dot_nt(dpre_ref[:, 0:D], win_ref[0])
        for c in range(1, N_CHIPS):
            dhn = dhn + _dot_nt(dpre_ref[:, c * D:(c + 1) * D], win_ref[c])
        _acc_rows(dg1_ref, dhn * xhat)
        dh_ref[...] = dh1_ref[...] + _rms_bwd(dhn, g1, xhat, rx, D)

    return pl.pallas_call(
        body, name="gmlp_bwd", grid=(t // TM,),
        in_specs=[_row(TM, D), _row(TM, D), _row(TM, D), _row(TM, 2 * GH), _const((1, D)), _wblk(D, lay["in"]),
                  _const((1, GH)), _const((1, GH)), _const((GG, GC, GC)), _const((GG, GC, GC)), _const((GC, GH)),
                  _wblk(GH // N_CHIPS, lay["out"]), _const((GC, GC))],
        out_specs=[_row(TM, D), _row(TM, D), _row(TM, 2 * GH), _const((GG, GC, GC)), _const((GC, GH)),
                   _const((8, GH)), _const((8, GH)), _const((8, D))],
        out_shape=[_sds((t, D), F32), _sds((t, D), BF16), _sds((t, 2 * GH), BF16), _sds((GG, GC, GC), F32),
                   _sds((GC, GH), F32), _sds((8, GH), F32), _sds((8, GH), F32), _sds((8, D), F32)],
        scratch_shapes=[pltpu.VMEM((TM, GH), F32)],
        compiler_params=_cp("arbitrary"),
    )(dh1, dh1b, h, pre, g1, allw, lng, lnb, wm, wmt, bfull, allw, tril)


def _token_step(t):
    return 1024 if t % 1024 == 0 else 512


def mm_tn(a, b):
    t, k = a.shape
    n = b.shape[1]
    tk = min(k, 1024)
    tn = min(n, 1024)
    tt = _token_step(t)

    def body(a_ref, b_ref, o_ref):
        @pl.when(pl.program_id(2) == 0)
        def _():
            o_ref[...] = jnp.zeros_like(o_ref)

        o_ref[...] += _dot_tn(a_ref[...].astype(BF16), b_ref[...].astype(BF16))

    return pl.pallas_call(
        body, name="mm_tn", grid=(k // tk, n // tn, t // tt),
        in_specs=[pl.BlockSpec((tt, tk), lambda i, j, s: (s, i)), pl.BlockSpec((tt, tn), lambda i, j, s: (s, j))],
        out_specs=pl.BlockSpec((tk, tn), lambda i, j, s: (i, j)), out_shape=_sds((k, n), F32),
        compiler_params=_cp("parallel", "parallel", "arbitrary"),
    )(a, b)


def mm_tn_into(buf, a, b, rows, row0, col_sharded):
    t = a.shape[0]
    tt = _token_step(t)
    assert row0 % rows == 0 and a.shape[1] == (rows if col_sharded else N_CHIPS * rows), (rows, row0, a.shape)
    assert b.shape[1] == (N_CHIPS * D if col_sharded else D), b.shape
    grid = (1, N_CHIPS, t // tt) if col_sharded else (N_CHIPS, 1, t // tt)
    fresh = isinstance(buf, int)

    def body(*refs):
        a_ref, b_ref, o_ref = refs[-3:]

        @pl.when(pl.program_id(2) == 0)
        def _():
            o_ref[...] = jnp.zeros_like(o_ref)

        o_ref[...] += _dot_tn(a_ref[...].astype(BF16), b_ref[...].astype(BF16))

    specs = [pl.BlockSpec((tt, rows), lambda i, j, s: (s, i)), pl.BlockSpec((tt, D), lambda i, j, s: (s, j))]
    return pl.pallas_call(
        body, name="mm_tn_into", grid=grid,
        in_specs=specs if fresh else [_ANY] + specs,
        out_specs=pl.BlockSpec((None, rows, D), lambda i, j, s: (i + j, row0 // rows, 0)),
        out_shape=_sds((N_CHIPS, buf, D) if fresh else buf.shape, F32),
        input_output_aliases={} if fresh else {0: 0},
        compiler_params=_cp("parallel", "parallel", "arbitrary"),
    )(*((a, b) if fresh else (buf, a, b)))


def adamw(w, g, m, v):
    rows, cols = w.shape
    tr = rows if rows <= 512 else next(r for r in (512, 384, 256, 128) if rows % r == 0)
    c1 = 1.0 - ADAM_B1 ** ADAM_STEP
    c2 = 1.0 - ADAM_B2 ** ADAM_STEP

    def body(w_ref, g_ref, m_ref, v_ref, d_ref, mo_ref, vo_ref):
        gv = g_ref[...]
        mn = ADAM_B1 * m_ref[...] + (1.0 - ADAM_B1) * gv
        vn = ADAM_B2 * v_ref[...] + (1.0 - ADAM_B2) * (gv * gv)
        mo_ref[...] = mn
        vo_ref[...] = vn
        d_ref[...] = -ADAM_LR * ((mn / c1) / (jnp.sqrt(vn / c2) + ADAM_EPS) + ADAM_WD * w_ref[...])

    spec = pl.BlockSpec((tr, cols), lambda i: (i, 0))
    return pl.pallas_call(
        body, name="adamw", grid=(rows // tr,),
        in_specs=[spec] * 4, out_specs=[spec] * 3, out_shape=[_sds((rows, cols), F32)] * 3,
        compiler_params=_cp("parallel"),
    )(w, g, m, v)


def _place():
    return lax.axis_index("x"), lax.axis_index("y"), lax.axis_index("c")


def _other_chips(x, y):
    return [(1 - x, y), (x, 1 - y), (1 - x, 1 - y)]


_ANY = pl.BlockSpec(memory_space=pl.ANY)


_HBM = pl.BlockSpec(memory_space=pltpu.HBM)
_SEM = pl.BlockSpec(memory_space=pltpu.SEMAPHORE)
_EFFECT = pltpu.SideEffectType.DATAFLOW_SIDE_EFFECTING
N_ICI = 3


def _exchange_start(name, src, land, copies, n):
    fresh = isinstance(land, jax.ShapeDtypeStruct)

    def body(*refs):
        src_ref = refs[0]
        outs = refs[1:] if fresh else refs[2:]
        sems, land_ref, token = outs[:2 * n], outs[2 * n + 1] if fresh else refs[1], outs[-1]
        for j, (s, d, to) in enumerate(copies(src_ref, land_ref, _place())):
            pltpu.make_async_remote_copy(src_ref=s, dst_ref=d, send_sem=sems[j], recv_sem=sems[n + j],
                                         device_id=to, device_id_type=MESH).start()
        token[...] = jnp.zeros_like(token)

    sem = pltpu.SemaphoreType.DMA(())
    operands = [pltpu.with_memory_space_constraint(src, pltpu.HBM)]
    aliases = {0: 2 * n}
    if not fresh:
        operands.append(pltpu.with_memory_space_constraint(land, pltpu.HBM))
        aliases[1] = 2 * n + 1
    outs = pl.pallas_call(
        body, name=name,
        out_shape=(sem,) * (2 * n) + (pltpu.HBM(src.shape, src.dtype), pltpu.HBM(land.shape, land.dtype),
                                      _sds((8, LANES), F32)),
        in_specs=(_HBM,) * len(operands),
        out_specs=(_SEM,) * (2 * n) + (_HBM, _HBM, pl.BlockSpec(memory_space=pltpu.VMEM)),
        input_output_aliases=aliases,
        compiler_params=pltpu.CompilerParams(has_side_effects=_EFFECT),
    )(*operands)
    return outs[:2 * n], outs[2 * n], outs[2 * n + 1], outs[-1]


def _exchange_wait(name, sems, src, land, after, arrivals):
    n = len(sems) // 2

    def body(src_ref, land_ref, *rest):
        sems = rest[:2 * n]
        for j, (s, d) in enumerate(arrivals(src_ref, land_ref, _place())):
            cp = pltpu.make_async_remote_copy(src_ref=s, dst_ref=d, send_sem=sems[j], recv_sem=sems[n + j],
                                              device_id=_place(), device_id_type=MESH)
            cp.wait_send()
            cp.wait_recv()

    return pl.pallas_call(
        body, name=name, out_shape=(pltpu.HBM(src.shape, src.dtype), pltpu.HBM(land.shape, land.dtype)),
        in_specs=(_HBM, _HBM) + (_SEM,) * (2 * n) + (_ANY,), out_specs=(_HBM, _HBM),
        input_output_aliases={0: 0, 1: 1},
        compiler_params=pltpu.CompilerParams(has_side_effects=_EFFECT),
    )(src, land, *sems, after)


def _halves(c, hh):
    return pl.ds(pl.multiple_of(c * hh, 16), hh), pl.ds(pl.multiple_of((1 - c) * hh, 16), hh)


def gather_start(land, tag):
    _, rr, _ = land.shape
    assert rr % 32 == 0, rr

    def copies(_, land_ref, place):
        x, y, c = place
        mine = land_ref.at[2 * x + y, _halves(c, rr // 2)[0]]
        return [(mine, mine, (cx, cy, c)) for cx, cy in _other_chips(x, y)]

    return _exchange_start(f"gather_start_{tag}", jnp.zeros((8, LANES), F32), land, copies, N_ICI)


def gather_wait(sems, src, land, after, tag):
    def arrivals(_, land_ref, place):
        x, y, c = place
        half = _halves(c, land.shape[1] // 2)[0]
        return [(land_ref.at[2 * x + y, half], land_ref.at[2 * cx + cy, half]) for cx, cy in _other_chips(x, y)]

    return _exchange_wait(f"gather_wait_{tag}", sems, src, land, after, arrivals)


def pass_start(land, tag):
    def copies(_, land_ref, place):
        x, y, c = place
        half = _halves(c, land.shape[1] // 2)[0]
        return [(land_ref.at[2 * cx + cy, half], land_ref.at[2 * cx + cy, half], (x, y, 1 - c))
                for cx, cy in _other_chips(x, y)]

    return _exchange_start(f"pass_start_{tag}", jnp.zeros((8, LANES), F32), land, copies, N_ICI)


def pass_wait(sems, src, land, after, tag):
    def arrivals(_, land_ref, place):
        x, y, c = place
        mine, other = _halves(c, land.shape[1] // 2)
        return [(land_ref.at[2 * cx + cy, mine], land_ref.at[2 * cx + cy, other]) for cx, cy in _other_chips(x, y)]

    return _exchange_wait(f"pass_wait_{tag}", sems, src, land, after, arrivals)


def swap_start(g, tag):
    _, rr, cc = g.shape

    def copies(g_ref, got_ref, place):
        x, y, c = place
        other = _halves(c, rr // 2)[1]
        return [(g_ref.at[k, other], got_ref.at[k], (x, y, 1 - c)) for k in range(N_CHIPS)]

    return _exchange_start(f"swap_start_{tag}", g, _sds((N_CHIPS, rr // 2, cc), g.dtype), copies, N_CHIPS)


def swap_wait(sems, g, got, after, tag):
    def arrivals(g_ref, got_ref, place):
        other = _halves(place[2], g.shape[1] // 2)[1]
        return [(g_ref.at[k, other], got_ref.at[k]) for k in range(N_CHIPS)]

    return _exchange_wait(f"swap_wait_{tag}", sems, g, got, after, arrivals)


def chip_sum(place, g32, got):
    _, rr, cc = g32.shape
    hh = rr // 2
    tr = SUM_ROWS
    assert rr % 2 == 0 and hh % tr == 0, (rr, tr)
    nb = hh // tr

    def body(place_ref, g_ref, got_ref, own_ref, all_ref):
        s = g_ref[...] + got_ref[...].astype(F32)
        all_ref[...] = s.astype(BF16)
        own_ref[...] = g_ref[place_ref[1]] + got_ref[place_ref[1]].astype(F32)

    return pl.pallas_call(
        body, name="chip_sum",
        grid_spec=pltpu.PrefetchScalarGridSpec(
            num_scalar_prefetch=1, grid=(nb,),
            in_specs=[pl.BlockSpec((N_CHIPS, tr, cc), lambda i, pr: (0, pr[0] * nb + i, 0)),
                      pl.BlockSpec((N_CHIPS, tr, cc), lambda i, pr: (0, i, 0))],
            out_specs=[pl.BlockSpec((tr, cc), lambda i, pr: (i, 0)),
                       pl.BlockSpec((N_CHIPS, tr, cc), lambda i, pr: (0, i, 0))]),
        out_shape=[_sds((hh, cc), F32), _sds((N_CHIPS, hh, cc), BF16)],
        compiler_params=_cp("parallel"),
    )(place, g32, got)


def _scatter_copies(s_ref, land_ref, place):
    x, y, c = place
    return [(s_ref.at[2 * cx + cy], land_ref.at[j], (cx, cy, c)) for j, (cx, cy) in enumerate(_other_chips(x, y))]


def scatter_start(s, tag):
    return _exchange_start(f"scatter_start_{tag}", s, _sds((N_ICI,) + s.shape[1:], s.dtype), _scatter_copies, N_ICI)


def scatter_wait(sems, s, land, after, tag):
    return _exchange_wait(f"scatter_wait_{tag}", sems, s, land, after,
                          lambda s_ref, land_ref, place: [(a, b) for a, b, _ in _scatter_copies(s_ref, land_ref, place)])


def final_sum(place, own, got):
    hh, cc = own.shape
    tr = SUM_ROWS
    assert hh % tr == 0, (hh, tr)
    nb = hh // tr

    def body(place_ref, own_ref, got_ref, o_ref):
        del place_ref
        o_ref[...] = ((own_ref[...] + got_ref[0].astype(F32)) + got_ref[1].astype(F32)) + got_ref[2].astype(F32)

    return pl.pallas_call(
        body, name="final_sum",
        grid_spec=pltpu.PrefetchScalarGridSpec(
            num_scalar_prefetch=1, grid=(nb,),
            in_specs=[pl.BlockSpec((tr, cc), lambda i, pr: (i, 0)), pl.BlockSpec((3, tr, cc), lambda i, pr: (0, i, 0))],
            out_specs=pl.BlockSpec((tr, cc), lambda i, pr: (pr[0] * nb + i, 0))),
        out_shape=_sds((2 * hh, cc), F32),
        compiler_params=_cp("parallel"),
    )(place, own, got)


def share_with_sibling(f):
    rr, cc = f.shape
    hh = rr // 2

    def body(f_ref, o_ref, send_sem, recv_sem):
        del o_ref
        x, y, c = _place()
        mine_half = f_ref.at[pl.ds(pl.multiple_of(c * hh, 8), hh)]
        cp = pltpu.make_async_remote_copy(src_ref=mine_half, dst_ref=mine_half, send_sem=send_sem,
                                          recv_sem=recv_sem, device_id=(x, y, 1 - c), device_id_type=MESH)
        cp.start()
        cp.wait()

    return pl.pallas_call(
        body, name="share_with_sibling", in_specs=[_ANY], out_specs=_ANY, out_shape=_sds(f.shape, f.dtype),
        input_output_aliases={0: 0},
        scratch_shapes=[pltpu.SemaphoreType.DMA, pltpu.SemaphoreType.DMA],
    )(f)


N_DEV = 8


def _peers(place):
    x, y, c = place
    return [((1 - x) if r & 4 else x, (1 - y) if r & 2 else y, (1 - c) if r & 1 else c) for r in range(1, N_DEV)]


def _device_index(place):
    x, y, c = place
    return 4 * x + 2 * y + c


def small_start(land):
    def copies(_, land_ref, place):
        mine = land_ref.at[_device_index(place)]
        return [(mine, mine, to) for to in _peers(place)]

    return _exchange_start("small_start", jnp.zeros((8, LANES), F32), land, copies, N_DEV - 1)


def small_wait(sems, src, land, after):
    def arrivals(_, land_ref, place):
        return [(land_ref.at[_device_index(place)], land_ref.at[_device_index(peer)]) for peer in _peers(place)]

    return _exchange_wait("small_wait", sems, src, land, after, arrivals)


def sum_devices(land):
    _, rr, cc = land.shape
    tr = 56
    assert rr % tr == 0, rr

    def body(l_ref, o_ref):
        acc = l_ref[0]
        for d in range(1, N_DEV):
            acc = acc + l_ref[d]
        o_ref[...] = acc

    return pl.pallas_call(
        body, name="sum_devices", grid=(rr // tr,),
        in_specs=[pl.BlockSpec((N_DEV, tr, cc), lambda i: (0, i, 0))],
        out_specs=pl.BlockSpec((tr, cc), lambda i: (i, 0)), out_shape=_sds((rr, cc), F32),
        compiler_params=_cp("parallel"),
    )(land)


_BIG = ["mla_w_down", "mla_w_uq", "mla_w_ukv", "mla_w_out", "gmlp_w_in", "gmlp_w_out", "ffn_w_up", "ffn_w_down",
        "ple_w_gate", "ple_w_proj"]
_SMALL = ["norm_mix", "norm_ffn", "norm_ple", "mla_q_lora_g", "mla_kv_lora_g", "mla_q_nope_g", "mla_q_rope_g",
          "mla_k_nope_g", "mla_k_rope_g", "gmlp_ln_g", "gmlp_ln_b", "gmlp_w_s", "gmlp_b_s"]

_LAY_MLA = dict(up=0, down=1024, out=2048, gate=2304, wdn=2560, wuq=2736, wukv=2880, proj=3008, rows=3072)
_LAY_GMLP = {"up": 0, "down": 1024, "in": 2048, "out": 3072, "gate": 3584, "proj": 3840, "ln": 3904, "rows": 4096}


def _layer_parts(i):
    j = i // 2
    if i % 2 == 0:
        lay = _LAY_MLA
        return lay, [("ffn_w_up", i, lay["up"]), ("ffn_w_down", i, lay["down"]), ("mla_w_out", j, lay["out"]),
                     ("ple_w_gate", i, lay["gate"]), ("mla_w_down", j, lay["wdn"]), ("mla_w_uq", j, lay["wuq"]),
                     ("mla_w_ukv", j, lay["wukv"]), ("ple_w_proj", i, lay["proj"])]
    lay = _LAY_GMLP
    return lay, [("ffn_w_up", i, lay["up"]), ("ffn_w_down", i, lay["down"]), ("gmlp_w_in", j, lay["in"]),
                 ("gmlp_w_out", j, lay["out"]), ("ple_w_gate", i, lay["gate"]), ("ple_w_proj", i, lay["proj"])]


def _pack_rows(parts, dtype, pad_to=None):
    flat = jnp.concatenate([p.reshape(-1).astype(dtype) for p in parts])
    if pad_to is not None:
        flat = jnp.pad(flat, (0, pad_to * D - flat.size))
    return flat.reshape(-1, D)


def _odd(allw, row0, a, b):
    return allw[:, row0:row0 + a * b // D].reshape(N_CHIPS, a, b)


def _cols_joined(s):
    return jnp.transpose(s, (1, 0, 2)).reshape(s.shape[1], N_CHIPS * s.shape[2])


def _col_shards(full):
    a, bb = full.shape
    return jnp.transpose(full.reshape(a, N_CHIPS, bb // N_CHIPS), (1, 0, 2)).reshape(N_CHIPS, -1, D)


def _pad_lanes(g):
    return jnp.pad(g, ((0, 0), (0, LANES - g.shape[1])))


def _split_uq(wuq):
    l = wuq.shape[0]
    w = wuq.reshape(l, QL, HEADS, DN + DR)
    nope = w[..., :DN].reshape(l, QL, HEADS * DN)
    rope = jnp.pad(w[..., DN:], ((0, 0), (0, 0), (0, 0), (0, LANES - DR))).reshape(l, QL, HEADS * LANES)
    return jnp.concatenate([nope, rope], axis=-1)


def _merge_uq(d):
    nope = d[:, :HEADS * DN].reshape(QL, HEADS, DN)
    rope = d[:, HEADS * DN:].reshape(QL, HEADS, LANES)[..., :DR]
    return jnp.concatenate([nope, rope], axis=-1).reshape(QL, HEADS * (DN + DR))


def _rope_tables(positions):
    inv_freq = ROPE_BASE ** (-(jnp.arange(0, DR, 2, dtype=F32) / DR))
    ang = positions.reshape(-1).astype(F32)[:, None] * inv_freq
    z = jnp.zeros((ang.shape[0], LANES - DR), F32)
    return (jnp.concatenate([jnp.cos(ang), jnp.cos(ang), z], axis=1),
            jnp.concatenate([jnp.sin(ang), jnp.sin(ang), z], axis=1))


def kernel(x, p, positions, norm_mix, norm_ffn, norm_ple, mla_w_down, mla_q_lora_g, mla_kv_lora_g, mla_w_uq, mla_w_ukv, mla_q_nope_g, mla_q_rope_g, mla_k_nope_g, mla_k_rope_g, mla_w_out, gmlp_w_in, gmlp_ln_g, gmlp_ln_b, gmlp_w_s, gmlp_b_s, gmlp_w_out, ffn_w_up, ffn_w_down, ple_w_gate, ple_w_proj, loss_target, m_norm_mix, m_norm_ffn, m_norm_ple, m_mla_w_down, m_mla_q_lora_g, m_mla_kv_lora_g, m_mla_w_uq, m_mla_w_ukv, m_mla_q_nope_g, m_mla_q_rope_g, m_mla_k_nope_g, m_mla_k_rope_g, m_mla_w_out, m_gmlp_w_in, m_gmlp_ln_g, m_gmlp_ln_b, m_gmlp_w_s, m_gmlp_b_s, m_gmlp_w_out, m_ffn_w_up, m_ffn_w_down, m_ple_w_gate, m_ple_w_proj, v_norm_mix, v_norm_ffn, v_norm_ple, v_mla_w_down, v_mla_q_lora_g, v_mla_kv_lora_g, v_mla_w_uq, v_mla_w_ukv, v_mla_q_nope_g, v_mla_q_rope_g, v_mla_k_nope_g, v_mla_k_rope_g, v_mla_w_out, v_gmlp_w_in, v_gmlp_ln_g, v_gmlp_ln_b, v_gmlp_w_s, v_gmlp_b_s, v_gmlp_w_out, v_ffn_w_up, v_ffn_w_down, v_ple_w_gate, v_ple_w_proj):
    args = dict(locals())
    weights = {n: args[n] for n in _BIG + _SMALL}
    depth = norm_mix.shape[0]
    nb, seq, _ = x.shape
    t = nb * seq
    assert seq % TQ == 0 and seq % TM == 0 and t % 512 == 0, (nb, seq)
    cx = lax.axis_index("x")
    cy = lax.axis_index("y")
    cc = lax.axis_index("c")
    chip = 2 * cx + cy

    gathers = []
    token = None
    for i in range(depth):
        lay, parts = _layer_parts(i)
        rows = [weights[n][l] for n, l, _ in parts]
        if token is not None:
            rows[0] = rows[0] + token[0, 0]
        if i % 2 == 1:
            ln = jnp.stack([gmlp_ln_g[i // 2], gmlp_ln_b[i // 2]]).astype(F32)
            rows.append(lax.bitcast_convert_type(ln, BF16))
        mine = _pack_rows(rows, BF16, pad_to=lay["rows"])
        land = lax.dynamic_update_slice(lax.empty((N_CHIPS, lay["rows"], D), BF16), mine[None], (chip, 0, 0))
        sems, src, land, token = gather_start(land, i)
        gathers.append((sems, src, land))
    allw = [None] * depth

    tril = jnp.tril(jnp.ones((GC, GC), F32))
    wm = (gmlp_w_s * tril).astype(BF16)
    wmt = jnp.swapaxes(wm, -1, -2)
    bfull = jnp.repeat(jnp.swapaxes(gmlp_b_s, -1, -2), GD, axis=-1)
    cos, sin = _rope_tables(positions)
    row = lambda g: g.reshape(1, -1)
    gqr = _pad_lanes(mla_q_rope_g)
    gkr = _pad_lanes(mla_k_rope_g)

    h = x.reshape(t, D)
    pt = p.reshape(depth, t, PLE)
    saved = []

    def arrive(i, after):
        sems, src, land = gathers[i]
        _, land = gather_wait(sems, src, land, after, i)
        return pass_start(land, i)

    passing = arrive(0, token)
    for i in range(depth):
        j = i // 2
        lay, _ = _layer_parts(i)
        sems, src, land, token = passing
        _, aw = pass_wait(sems, src, land, token if i == 0 else h, i)
        allw[i] = aw
        s = dict(h=h)
        if i % 2 == 0:
            wdn = jnp.pad(_odd(aw, lay["wdn"], D // N_CHIPS, LAT).reshape(D, LAT), ((0, 0), (0, LATP - LAT)))
            wuq = _split_uq(_cols_joined(_odd(aw, lay["wuq"], QL, 384))[None])[0]
            wukv = _cols_joined(_odd(aw, lay["wukv"], KVL, 512))
            mla_args = (row(norm_mix[i]), wdn, row(mla_q_lora_g[j]), row(mla_kv_lora_g[j]), wuq, wukv,
                        row(mla_q_nope_g[j]), gqr[j:j + 1], row(mla_k_nope_g[j]), gkr[j:j + 1], cos, sin)
            q, k, v = mla_pre_fwd(h, *mla_args)
            y, lse = flash_fwd(q, k, v, seq)
            s.update(q=q, k=k, v=v, lse=lse, mla_args=mla_args)
        else:
            ln = lax.bitcast_convert_type(aw[:, lay["ln"]:lay["ln"] + 2].reshape(N_CHIPS, 2, GH // N_CHIPS, 2), F32)
            ln = jnp.transpose(ln, (1, 0, 2)).reshape(2, 1, GH)
            y, pre = gmlp_fwd(h, row(norm_mix[i]), aw, lay, ln[0], ln[1], wm[j], bfull[j])
            s.update(pre=pre, ln=ln)
        wp = _cols_joined(_odd(aw, lay["proj"], PLE, 256))
        g2 = row(norm_ffn[i])
        if i + 1 < depth:
            passing = arrive(i + 1, y)
            g2 = g2 + passing[3][0:1, 0:1]
        h1, h2, hn2, r = mixffn_fwd(h, y, aw, lay, g2)
        h, hn3, sg, pp = ple_fwd(h2, pt[i], row(norm_ple[i]), aw, lay, wp)
        s.update(y=y, h1=h1, h2=h2, hn2=hn2, r=r, hn3=hn3, sg=sg, pp=pp)
        saved.append(s)

    dh, loss_part = loss_head(h, loss_target.reshape(t, D))
    loss = lax.psum(loss_part[0, 0], ("x", "y", "c"))

    gs = {n: [None] * weights[n].shape[0] for n in _SMALL}
    gw = {n: [None] * weights[n].shape[0] for n in _BIG}
    place = jnp.stack([cc, chip]).astype(jnp.int32)
    scatters = []
    swapping = None
    token = None

    def put(b, row0, shards):
        return lax.dynamic_update_slice(b, shards.reshape(N_CHIPS, -1, D), (0, row0, 0))

    def swapped(after):
        ii, sems, g, got = swapping
        g, got = swap_wait(sems, g, got, after, ii)
        own, sums = chip_sum(place, g, got)
        sems, sums, land, tok = scatter_start(sums, ii)
        scatters.append((ii, own, sems, sums, land))
        return tok

    for i in reversed(range(depth)):
        j = i // 2
        lay, parts = _layer_parts(i)
        aw = allw[i]
        s = saved[i]

        g3 = row(norm_ple[i])
        if token is not None:
            g3 = g3 + token[0:1, 0:1]
        dh2, dh2b, dgt, dpp, dg3 = ple_bwd(dh, s["h2"], s["sg"], s["pp"], g3, aw, lay)
        gs["norm_ple"][i] = dg3[0]
        buf = mm_tn_into(lay["rows"], s["hn3"], dgt, D // N_CHIPS, lay["gate"], False)
        tail = lay.get("ln", lay["rows"])
        if tail < lay["rows"]:
            buf = put(buf, tail, jnp.zeros((N_CHIPS, lay["rows"] - tail, D), F32))
        buf = put(buf, lay["proj"], _col_shards(mm_tn(pt[i], dpp)))
        dh1, dh1b, du, a, dg2 = ffn_bwd(dh2, dh2b, s["h1"], s["r"], row(norm_ffn[i]), aw, lay)
        gs["norm_ffn"][i] = dg2[0]
        buf = mm_tn_into(buf, a, dh2b, D, lay["down"], False)
        buf = mm_tn_into(buf, s["hn2"], du, D, lay["up"], True)
        buf = mm_tn_into(buf, s["y"], dh1b, s["y"].shape[1] // N_CHIPS, lay["out"], False)
        g1 = row(norm_mix[i])
        if swapping is not None:
            g1 = g1 + swapped(dh1)[0:1, 0:1]
        if i % 2 == 0:
            do = linear_nt(dh1b, aw, D // N_CHIPS, lay["out"])
            dq, dk, dv = flash_bwd(s["q"], s["k"], s["v"], s["y"], do, s["lse"], seq)
            (dh, hn1, cq, ckv, dqp, dkvp, dlat, dg1, dgq, dgkv, dgqn, dgqr, dgkn, dgkr) = mla_pre_bwd(
                dq, dk, dv, dh1, s["h"], g1, *s["mla_args"][1:])
            gs["norm_mix"][i] = dg1[0]
            gs["mla_q_lora_g"][j] = dgq[0]
            gs["mla_kv_lora_g"][j] = dgkv[0]
            gs["mla_q_nope_g"][j] = dgqn[0]
            gs["mla_q_rope_g"][j] = dgqr[0, :DR]
            gs["mla_k_nope_g"][j] = dgkn[0]
            gs["mla_k_rope_g"][j] = dgkr[0, :DR]
            buf = put(buf, lay["wdn"], mm_tn(hn1, dlat)[:, :LAT])
            buf = put(buf, lay["wuq"], _col_shards(_merge_uq(mm_tn(cq, dqp))))
            buf = put(buf, lay["wukv"], _col_shards(mm_tn(ckv, dkvp)))
        else:
            dh, hn1, dpre, dws, dbs, dlng, dlnb, dg1 = gmlp_bwd(
                dh1, dh1b, s["h"], s["pre"], g1, aw, lay, s["ln"][0], s["ln"][1], wm[j], wmt[j], bfull[j], tril)
            gs["norm_mix"][i] = dg1[0]
            gs["gmlp_ln_g"][j] = dlng[0]
            gs["gmlp_ln_b"][j] = dlnb[0]
            gs["gmlp_w_s"][j] = dws
            gs["gmlp_b_s"][j] = jnp.sum(dbs.reshape(GC, GG, GD), axis=-1).T
            buf = mm_tn_into(buf, hn1, dpre, D, lay["in"], True)

        sems, buf, got, token = swap_start(buf, i)
        swapping = (i, sems, buf, got)
    swapped(dh)
    grad_x = dh.reshape(x.shape)

    small_sizes = [weights[n].size if n not in ("gmlp_ln_g", "gmlp_ln_b") else weights[n].shape[0] * GH
                   for n in _SMALL]
    small_rows = -(-sum(small_sizes) // (56 * D)) * 56
    part = _pack_rows([jnp.stack(gs[n]) for n in _SMALL], F32, pad_to=small_rows)
    land = lax.dynamic_update_slice(lax.empty((N_DEV, small_rows, D), F32), part[None], (2 * chip + cc, 0, 0))
    small = small_start(land)

    after = small[3]
    for i, own, sems, sums, land in scatters:
        _, got = scatter_wait(sems, sums, land, after, i)
        after = reduced = share_with_sibling(final_sum(place, own, got))
        for n, l, row0 in _layer_parts(i)[1]:
            gw[n][l] = reduced[row0:row0 + weights[n][l].size // D].reshape(weights[n].shape[1:])
    grads = {n: jnp.stack(gw[n]) for n in _BIG}

    tot = sum_devices(small_wait(small[0], small[1], small[2], after)[1]).reshape(-1)
    off = 0
    for n, sz in zip(_SMALL, small_sizes):
        gsum = tot[off:off + sz]
        off += sz
        if n in ("gmlp_ln_g", "gmlp_ln_b"):
            gsum = lax.dynamic_slice_in_dim(gsum.reshape(-1, GH), chip * (GH // N_CHIPS), GH // N_CHIPS, axis=1)
        grads[n] = gsum.reshape(weights[n].shape)

    delta, new_m, new_v = {}, {}, {}
    for n in _BIG:
        w2 = weights[n].reshape(-1, weights[n].shape[-1])
        d, mn, vn = adamw(w2, grads[n].reshape(w2.shape), args["m_" + n].reshape(w2.shape),
                          args["v_" + n].reshape(w2.shape))
        delta[n], new_m[n], new_v[n] = (a.reshape(weights[n].shape) for a in (d, mn, vn))
    own_sizes = [weights[n].size for n in _SMALL]
    own_rows = -(-sum(own_sizes) // (8 * D)) * 8
    packed = [_pack_rows([src[n] for n in _SMALL], F32, pad_to=own_rows)
              for src in (weights, grads, {n: args["m_" + n] for n in _SMALL}, {n: args["v_" + n] for n in _SMALL})]
    outs = adamw(*packed)
    off = 0
    for n, sz in zip(_SMALL, own_sizes):
        for dst, o in zip((delta, new_m, new_v), outs):
            dst[n] = o.reshape(-1)[off:off + sz].reshape(weights[n].shape)
        off += sz

    order = ["norm_mix", "norm_ffn", "norm_ple", "mla_w_down", "mla_q_lora_g", "mla_kv_lora_g", "mla_w_uq",
             "mla_w_ukv", "mla_q_nope_g", "mla_q_rope_g", "mla_k_nope_g", "mla_k_rope_g", "mla_w_out", "gmlp_w_in",
             "gmlp_ln_g", "gmlp_ln_b", "gmlp_w_s", "gmlp_b_s", "gmlp_w_out", "ffn_w_up", "ffn_w_down", "ple_w_gate",
             "ple_w_proj"]
    return (loss, grad_x, *[grads[n] for n in order], *[delta[n] for n in order], *[new_m[n] for n in order],
            *[new_v[n] for n in order])
```

```python
import functools

import jax
import jax.numpy as jnp
from jax import lax
from jax.experimental import pallas as pl
from jax.experimental.pallas import tpu as pltpu

F32 = jnp.float32
BF16 = jnp.bfloat16
MESH = pl.DeviceIdType.MESH

D = 1024
HEADS = 8
DN = 128
DR = 64
QL = 384
KVL = 256
LAT = 704
LATP = 768
DFF = 4096
GH = 2048
GC = 128
GG = 8
GD = 256
PLE = 256
EPS = 1e-6
ROPE_BASE = 10000.0
SM_SCALE = (DN + DR) ** -0.5
N_CHIPS = 4
LANES = 128

ADAM_LR = 0.001
ADAM_B1 = 0.9
ADAM_B2 = 0.999
ADAM_EPS = 1e-08
ADAM_WD = 0.01
ADAM_STEP = 10

TM = 256
TMB = 512
TQ = 512
TQ_FWD = 512
FWD_HEADS = 2
BWD_HEADS = 2
SUM_ROWS = 256
VMEM_LIMIT = 56 * 1024 * 1024


def _cp(*sem):
    return pltpu.CompilerParams(dimension_semantics=sem, vmem_limit_bytes=VMEM_LIMIT)


def _dot(a, b):
    return jnp.dot(a, b, preferred_element_type=F32)


def _dot_nt(a, b):
    return lax.dot_general(a, b, (((1,), (1,)), ((), ())), preferred_element_type=F32)


def _dot_tn(a, b):
    return lax.dot_general(a, b, (((0,), (0,)), ((), ())), preferred_element_type=F32)


def _rms(x, g, n):
    r = lax.rsqrt(jnp.sum(x * x, axis=-1, keepdims=True) * (1.0 / n) + EPS)
    xhat = x * r
    return xhat * g, xhat, r


def _rms_bwd(dy, g, xhat, r, n):
    dxhat = dy * g
    return r * (dxhat - xhat * (jnp.sum(dxhat * xhat, axis=-1, keepdims=True) * (1.0 / n)))


def _rope(x, c, s):
    return x * c + (pltpu.roll(x, 32, 1) - pltpu.roll(x, 96, 1)) * s


def _rope_t(dy, c, s):
    w = dy * s
    return dy * c + pltpu.roll(w, 96, 1) - pltpu.roll(w, 32, 1)


def _sigmoid(x):
    return 1.0 / (1.0 + jnp.exp(-x))


_GELU_K = 0.7978845608028654
_GELU_C = 0.044715


def _gelu(x):
    return 0.5 * x * (1.0 + jnp.tanh(_GELU_K * (x + _GELU_C * x * x * x)))


def _gelu_and_grad(x):
    x2 = x * x
    t = jnp.tanh(_GELU_K * (x + _GELU_C * x2 * x))
    half = 0.5 * (1.0 + t)
    return x * half, half + 0.5 * x * (1.0 - t * t) * (_GELU_K * (1.0 + 3.0 * _GELU_C * x2))


def _acc_rows(ref, val):
    ref[...] += jnp.broadcast_to(jnp.sum(val, axis=0, keepdims=True), ref.shape)


def _row(tm, c):
    return pl.BlockSpec((tm, c), lambda i: (i, 0))


def _const(shape):
    nd = len(shape)
    return pl.BlockSpec(shape, lambda i: (0,) * nd, pipeline_mode=pl.Buffered(1))


def _wblk(rows, row0):
    assert row0 % rows == 0, (rows, row0)
    return pl.BlockSpec((N_CHIPS, rows, D), lambda i: (0, row0 // rows, 0), pipeline_mode=pl.Buffered(1))


def _rows_joined(w_ref):
    return w_ref[...].reshape(N_CHIPS * w_ref.shape[1], D)


def _sds(shape, dtype):
    return jax.ShapeDtypeStruct(shape, dtype)


def mixffn_fwd(h, y, allw, lay, g2):
    t, k = y.shape

    def body(h_ref, y_ref, wo_ref, g_ref, wu_ref, wd_ref, h1_ref, h2_ref, hn_ref, r_ref):
        h1 = h_ref[...] + _dot(y_ref[...], _rows_joined(wo_ref))
        h1_ref[...] = h1
        yn, _, _ = _rms(h1, g_ref[...], D)
        hn = yn.astype(BF16)
        hn_ref[...] = hn
        f = jnp.zeros((TMB, D), F32)
        for c in range(N_CHIPS):
            r = jnp.maximum(_dot(hn, wu_ref[c]), 0.0)
            r_ref[:, c * D:(c + 1) * D] = r.astype(BF16)
            f = f + _dot((r * r).astype(BF16), wd_ref[c])
        h2_ref[...] = h1 + f

    return pl.pallas_call(
        body, name="mixffn_fwd", grid=(t // TMB,),
        in_specs=[_row(TMB, D), _row(TMB, k), _wblk(k // N_CHIPS, lay["out"]), _const((1, D)), _wblk(D, lay["up"]),
                  _wblk(D, lay["down"])],
        out_specs=[_row(TMB, D), _row(TMB, D), _row(TMB, D), _row(TMB, DFF)],
        out_shape=[_sds((t, D), F32), _sds((t, D), F32), _sds((t, D), BF16), _sds((t, DFF), BF16)],
        compiler_params=_cp("parallel"),
    )(h, y, allw, g2, allw, allw)


def ple_fwd(h2, p, g3, allw, lay, wp):
    t = h2.shape[0]

    def body(h_ref, p_ref, g_ref, wg_ref, wp_ref, h3_ref, hn_ref):
        x = h_ref[...]
        yn, _, _ = _rms(x, g_ref[...], D)
        hn = yn.astype(BF16)
        hn_ref[...] = hn
        gt = _dot(hn, _rows_joined(wg_ref))
        pp = _dot(p_ref[...].astype(BF16), wp_ref[...])
        h3_ref[...] = x + _sigmoid(gt) * pp

    return pl.pallas_call(
        body, name="ple_fwd", grid=(t // TMB,),
        in_specs=[_row(TMB, D), _row(TMB, PLE), _const((1, D)), _wblk(D // N_CHIPS, lay["gate"]), _const((PLE, D))],
        out_specs=[_row(TMB, D), _row(TMB, D)],
        out_shape=[_sds((t, D), F32), _sds((t, D), BF16)],
        compiler_params=_cp("parallel"),
    )(h2, p, g3, allw, wp)


def _mla_project(h_ref, g1_ref, wdn_ref, gq_ref, gkv_ref, wuq_ref, wukv_ref):
    x = h_ref[...]
    yn, xhat, rx = _rms(x, g1_ref[...], D)
    hn = yn.astype(BF16)
    lat = _dot(hn, wdn_ref[...])
    cq, cqhat, rq = _rms(lat[:, :QL], gq_ref[...], QL)
    ckv, ckvhat, rkv = _rms(lat[:, QL:QL + KVL], gkv_ref[...], KVL)
    kr_raw = lat[:, QL + KVL:]
    cqb = cq.astype(BF16)
    ckvb = ckv.astype(BF16)
    qp = _dot(cqb, wuq_ref[...])
    kvp = _dot(ckvb, wukv_ref[...])
    return dict(xhat=xhat, rx=rx, hn=hn, cqhat=cqhat, rq=rq, ckvhat=ckvhat, rkv=rkv, kr_raw=kr_raw,
                cqb=cqb, ckvb=ckvb, qp=qp, kvp=kvp)


def mla_pre_fwd(h, g1, wdn, gq, gkv, wuq, wukv, gqn, gqr, gkn, gkr, cos, sin):
    t = h.shape[0]

    def body(h_ref, g1_ref, wdn_ref, gq_ref, gkv_ref, wuq_ref, wukv_ref, gqn_ref, gqr_ref, gkn_ref, gkr_ref,
             c_ref, s_ref, q_ref, k_ref, v_ref):
        m = _mla_project(h_ref, g1_ref, wdn_ref, gq_ref, gkv_ref, wuq_ref, wukv_ref)
        c = c_ref[...]
        s = s_ref[...]
        kr, _, _ = _rms(m["kr_raw"], gkr_ref[...], DR)
        krb = _rope(kr, c, s).astype(BF16)
        for hd in range(HEADS):
            qn, _, _ = _rms(m["qp"][:, hd * DN:(hd + 1) * DN], gqn_ref[...], DN)
            qr, _, _ = _rms(m["qp"][:, D + hd * LANES:D + (hd + 1) * LANES], gqr_ref[...], DR)
            q_ref[hd, :, 0:DN] = (qn * SM_SCALE).astype(BF16)
            q_ref[hd, :, DN:2 * DN] = (_rope(qr, c, s) * SM_SCALE).astype(BF16)
            kn, _, _ = _rms(m["kvp"][:, hd * 2 * DN:hd * 2 * DN + DN], gkn_ref[...], DN)
            k_ref[hd, :, 0:DN] = kn.astype(BF16)
            k_ref[hd, :, DN:2 * DN] = krb
            v_ref[hd] = m["kvp"][:, hd * 2 * DN + DN:(hd + 1) * 2 * DN].astype(BF16)

    hb = lambda w: pl.BlockSpec((HEADS, TM, w), lambda i: (0, i, 0))
    return pl.pallas_call(
        body, name="mla_pre_fwd", grid=(t // TM,),
        in_specs=[_row(TM, D), _const((1, D)), _const((D, LATP)), _const((1, QL)), _const((1, KVL)),
                  _const((QL, 2 * D)), _const((KVL, 2 * D)), _const((1, LANES)), _const((1, LANES)),
                  _const((1, LANES)), _const((1, LANES)), _row(TM, LANES), _row(TM, LANES)],
        out_specs=[hb(2 * DN), hb(2 * DN), hb(DN)],
        out_shape=[_sds((HEADS, t, 2 * DN), BF16), _sds((HEADS, t, 2 * DN), BF16), _sds((HEADS, t, DN), BF16)],
        compiler_params=_cp("parallel"),
    )(h, g1, wdn, gq, gkv, wuq, wukv, gqn, gqr, gkn, gkr, cos, sin)


def _diagonal_mask(n=TQ):
    return lax.broadcasted_iota(jnp.int32, (n, n), 1) <= lax.broadcasted_iota(jnp.int32, (n, n), 0)


def flash_fwd(q, k, v, seq):
    t = q.shape[1]
    nb = t // seq
    tq = TQ_FWD
    nq = seq // tq
    hp = FWD_HEADS

    def body(q_ref, k_ref, v_ref, o_ref, lse_ref):
        qi = pl.program_id(2)
        qs = [q_ref[a] for a in range(hp)]

        def step(j, carry, diagonal=False):
            rows = pl.ds(pl.multiple_of(j * tq, tq), tq)
            out = []
            for a in range(hp):
                m, l, acc = carry[a]
                s = _dot_nt(qs[a], k_ref[a, rows, :])
                if diagonal:
                    s = jnp.where(_diagonal_mask(tq), s, -1e30)
                m_new = jnp.maximum(m, jnp.max(s, axis=-1, keepdims=True))
                p = jnp.exp(s - m_new)
                alpha = jnp.exp(m - m_new)
                l = alpha * l + jnp.sum(p, axis=-1, keepdims=True)
                acc = alpha * acc + _dot(p.astype(BF16), v_ref[a, rows, :])
                out.append((m_new, l, acc))
            return tuple(out)

        one = (jnp.full((tq, 1), -1e30, F32), jnp.zeros((tq, 1), F32), jnp.zeros((tq, DN), F32))
        done = step(qi, lax.fori_loop(0, qi, step, (one,) * hp), diagonal=True)
        for a, (m, l, acc) in enumerate(done):
            o_ref[:, a * DN:(a + 1) * DN] = (acc / l).astype(BF16)
            lse_ref[a] = m + jnp.log(l)

    return pl.pallas_call(
        body, name="flash_fwd", grid=(nb, HEADS // hp, nq),
        in_specs=[pl.BlockSpec((hp, tq, 2 * DN), lambda b, h, i: (h, b * nq + i, 0)),
                  pl.BlockSpec((hp, seq, 2 * DN), lambda b, h, i: (h, b, 0)),
                  pl.BlockSpec((hp, seq, DN), lambda b, h, i: (h, b, 0))],
        out_specs=[pl.BlockSpec((tq, hp * DN), lambda b, h, i: (b * nq + i, h)),
                   pl.BlockSpec((hp, tq, 1), lambda b, h, i: (h, b * nq + i, 0))],
        out_shape=[_sds((t, HEADS * DN), BF16), _sds((HEADS, t, 1), F32)],
        compiler_params=_cp("parallel", "parallel", "arbitrary"),
    )(q, k, v)


def _gmlp_in(hn, win_ref):
    pre = [_dot(hn, win_ref[c]) for c in range(N_CHIPS)]
    return jnp.concatenate(pre[:2], axis=1), jnp.concatenate(pre[2:], axis=1)


def gmlp_fwd(h, g1, allw, lay, lng, lnb, wm, bfull):
    t = h.shape[0]

    def body(h_ref, g1_ref, win_ref, lng_ref, lnb_ref, wm_ref, b_ref, y_ref, pre_ref):
        yn, _, _ = _rms(h_ref[...], g1_ref[...], D)
        pre_u, pre_v = _gmlp_in(yn.astype(BF16), win_ref)
        pre_ref[:, :GH] = pre_u.astype(BF16)
        pre_ref[:, GH:] = pre_v.astype(BF16)
        u = _gelu(pre_u)
        v = _gelu(pre_v)
        xc = v - jnp.mean(v, axis=-1, keepdims=True)
        rs = lax.rsqrt(jnp.mean(xc * xc, axis=-1, keepdims=True) + EPS)
        vnb = (xc * rs * lng_ref[...] + lnb_ref[...]).astype(BF16)
        for ch in range(TM // GC):
            rows = slice(ch * GC, (ch + 1) * GC)
            for g in range(GG):
                cols = slice(g * GD, (g + 1) * GD)
                sv = _dot(wm_ref[g], vnb[rows, cols]) + b_ref[:, cols]
                y_ref[rows, cols] = (u[rows, cols] * sv).astype(BF16)

    return pl.pallas_call(
        body, name="gmlp_fwd", grid=(t // TM,),
        in_specs=[_row(TM, D), _const((1, D)), _wblk(D, lay["in"]), _const((1, GH)), _const((1, GH)),
                  _const((GG, GC, GC)), _const((GC, GH))],
        out_specs=[_row(TM, GH), _row(TM, 2 * GH)],
        out_shape=[_sds((t, GH), BF16), _sds((t, 2 * GH), BF16)],
        compiler_params=_cp("parallel"),
    )(h, g1, allw, lng, lnb, wm, bfull)


def loss_head(h, tgt):
    t = h.shape[0]

    def body(h_ref, t_ref, dh_ref, loss_ref):
        @pl.when(pl.program_id(0) == 0)
        def _():
            loss_ref[...] = jnp.zeros_like(loss_ref)

        e = h_ref[...] - t_ref[...]
        dh_ref[...] = e * (1.0 / D)
        part = jnp.sum(jnp.sum(e * e, axis=-1, keepdims=True), axis=0, keepdims=True) * (0.5 / D)
        loss_ref[...] += jnp.broadcast_to(part, loss_ref.shape)

    return pl.pallas_call(
        body, name="loss_head", grid=(t // TMB,),
        in_specs=[_row(TMB, D), _row(TMB, D)],
        out_specs=[_row(TMB, D), _const((8, LANES))],
        out_shape=[_sds((t, D), F32), _sds((8, LANES), F32)],
        compiler_params=_cp("arbitrary"),
    )(h, tgt)


def _zero_at_first_step(*refs):
    @pl.when(pl.program_id(0) == 0)
    def _():
        for r in refs:
            r[...] = jnp.zeros_like(r)


def ple_bwd(dh3, h2, p, g3, allw, lay, wp):
    t = h2.shape[0]

    def body(dh_ref, h_ref, p_ref, g_ref, wg_ref, wp_ref, dh2_ref, dh2b_ref, dgt_ref, dpp_ref, dg_ref):
        _zero_at_first_step(dg_ref)
        dh3v = dh_ref[...]
        x = h_ref[...]
        g = g_ref[...]
        wg = _rows_joined(wg_ref)
        yn, xhat, r = _rms(x, g, D)
        gt = _dot(yn.astype(BF16), wg)
        pp = _dot(p_ref[...].astype(BF16), wp_ref[...])
        sg = _sigmoid(gt)
        dgt = (dh3v * pp * sg * (1.0 - sg)).astype(BF16)
        dgt_ref[...] = dgt
        dpp_ref[...] = (dh3v * sg).astype(BF16)
        dhn = _dot_nt(dgt, wg)
        _acc_rows(dg_ref, dhn * xhat)
        dh2 = dh3v + _rms_bwd(dhn, g, xhat, r, D)
        dh2_ref[...] = dh2
        dh2b_ref[...] = dh2.astype(BF16)

    return pl.pallas_call(
        body, name="ple_bwd", grid=(t // TMB,),
        in_specs=[_row(TMB, D), _row(TMB, D), _row(TMB, PLE), _const((1, D)), _wblk(D // N_CHIPS, lay["gate"]),
                  _const((PLE, D))],
        out_specs=[_row(TMB, D), _row(TMB, D), _row(TMB, D), _row(TMB, D), _const((8, D))],
        out_shape=[_sds((t, D), F32), _sds((t, D), BF16), _sds((t, D), BF16), _sds((t, D), BF16), _sds((8, D), F32)],
        compiler_params=_cp("arbitrary"),
    )(dh3, h2, p, g3, allw, wp)


def ffn_bwd(dh2, dh2b, h1, r, g2, allw, lay):
    t = h1.shape[0]

    def body(dh_ref, dhb_ref, h_ref, r_ref, g_ref, wu_ref, wd_ref, dh1_ref, dh1b_ref, du_ref, a_ref, dg_ref):
        _zero_at_first_step(dg_ref)
        dhb = dhb_ref[...]
        g = g_ref[...]
        _, xhat, rr = _rms(h_ref[...], g, D)
        dhn = jnp.zeros((TM, D), F32)
        for c in range(N_CHIPS):
            cs = slice(c * D, (c + 1) * D)
            rc = r_ref[:, cs].astype(F32)
            a_ref[:, cs] = (rc * rc).astype(BF16)
            da = _dot_nt(dhb, wd_ref[c])
            du = (da * (2.0 * rc)).astype(BF16)
            du_ref[:, cs] = du
            dhn = dhn + _dot_nt(du, wu_ref[c])
        _acc_rows(dg_ref, dhn * xhat)
        dh1 = dh_ref[...] + _rms_bwd(dhn, g, xhat, rr, D)
        dh1_ref[...] = dh1
        dh1b_ref[...] = dh1.astype(BF16)

    return pl.pallas_call(
        body, name="ffn_bwd", grid=(t // TM,),
        in_specs=[_row(TM, D), _row(TM, D), _row(TM, D), _row(TM, DFF), _const((1, D)), _wblk(D, lay["up"]),
                  _wblk(D, lay["down"])],
        out_specs=[_row(TM, D), _row(TM, D), _row(TM, DFF), _row(TM, DFF), _const((8, D))],
        out_shape=[_sds((t, D), F32), _sds((t, D), BF16), _sds((t, DFF), BF16), _sds((t, DFF), BF16),
                   _sds((8, D), F32)],
        compiler_params=_cp("arbitrary"),
    )(dh2, dh2b, h1, r, g2, allw, allw)


def linear_nt(a, allw, rows, row0):
    t = a.shape[0]
    k = N_CHIPS * rows

    def body(a_ref, w_ref, o_ref):
        o_ref[...] = _dot_nt(a_ref[...], _rows_joined(w_ref)).astype(BF16)

    return pl.pallas_call(
        body, name="linear_nt", grid=(t // TMB,),
        in_specs=[_row(TMB, D), _wblk(rows, row0)],
        out_specs=_row(TMB, k),
        out_shape=_sds((t, k), BF16),
        compiler_params=_cp("parallel"),
    )(a, allw)


def flash_bwd(q, k, v, o, do, lse, seq):
    t = q.shape[1]
    nb = t // seq
    nq = seq // TQ
    hp = BWD_HEADS

    def body(q_ref, k_ref, v_ref, o_ref, do_ref, lse_ref, dq_ref, dk_ref, dv_ref):
        kj = pl.program_id(2)

        @pl.when(kj == 0)
        def _():
            dq_ref[...] = jnp.zeros_like(dq_ref)

        def step(i, carry, diagonal=False):
            rows = pl.ds(pl.multiple_of(i * TQ, TQ), TQ)
            out = []
            for a in range(hp):
                dk, dv = carry[a]
                kv = k_ref[a]
                qv = q_ref[a, rows, :]
                dov = do_ref[rows, a * DN:(a + 1) * DN]
                ov = o_ref[rows, a * DN:(a + 1) * DN]
                delta = jnp.sum(dov.astype(F32) * ov.astype(F32), axis=-1, keepdims=True)
                s = _dot_nt(qv, kv)
                if diagonal:
                    s = jnp.where(_diagonal_mask(), s, -1e30)
                p = jnp.exp(s - lse_ref[a, rows, :])
                dp = _dot_nt(dov, v_ref[a])
                ds = (p * (dp - delta)).astype(BF16)
                dv = dv + _dot_tn(p.astype(BF16), dov)
                dk = dk + _dot_tn(ds, qv)
                dq_ref[a, rows, :] += _dot(ds, kv)
                out.append((dk, dv))
            return tuple(out)

        one = (jnp.zeros((TQ, 2 * DN), F32), jnp.zeros((TQ, DN), F32))
        done = lax.fori_loop(kj + 1, nq, step, step(kj, (one,) * hp, diagonal=True))
        for a, (dk, dv) in enumerate(done):
            dk_ref[a] = dk
            dv_ref[a] = dv

    return pl.pallas_call(
        body, name="flash_bwd", grid=(nb, HEADS // hp, nq),
        in_specs=[pl.BlockSpec((hp, seq, 2 * DN), lambda b, h, j: (h, b, 0)),
                  pl.BlockSpec((hp, TQ, 2 * DN), lambda b, h, j: (h, b * nq + j, 0)),
                  pl.BlockSpec((hp, TQ, DN), lambda b, h, j: (h, b * nq + j, 0)),
                  pl.BlockSpec((seq, hp * DN), lambda b, h, j: (b, h)),
                  pl.BlockSpec((seq, hp * DN), lambda b, h, j: (b, h)),
                  pl.BlockSpec((hp, seq, 1), lambda b, h, j: (h, b, 0))],
        out_specs=[pl.BlockSpec((hp, seq, 2 * DN), lambda b, h, j: (h, b, 0)),
                   pl.BlockSpec((hp, TQ, 2 * DN), lambda b, h, j: (h, b * nq + j, 0)),
                   pl.BlockSpec((hp, TQ, DN), lambda b, h, j: (h, b * nq + j, 0))],
        out_shape=[_sds((HEADS, t, 2 * DN), F32), _sds((HEADS, t, 2 * DN), F32), _sds((HEADS, t, DN), F32)],
        compiler_params=_cp("parallel", "parallel", "arbitrary"),
    )(q, k, v, o, do, lse)


def mla_pre_bwd(dq, dk, dv, dh1, h, g1, wdn, gq, gkv, wuq, wukv, gqn, gqr, gkn, gkr, cos, sin):
    t = h.shape[0]

    def body(dq_ref, dk_ref, dv_ref, dh1_ref, h_ref, g1_ref, wdn_ref, gq_ref, gkv_ref, wuq_ref, wukv_ref,
             gqn_ref, gqr_ref, gkn_ref, gkr_ref, c_ref, s_ref,
             dh_ref, hn_ref, cq_ref, ckv_ref, dqp_ref, dkvp_ref, dlat_ref,
             dg1_ref, dgq_ref, dgkv_ref, dgqn_ref, dgqr_ref, dgkn_ref, dgkr_ref):
        _zero_at_first_step(dg1_ref, dgq_ref, dgkv_ref, dgqn_ref, dgqr_ref, dgkn_ref, dgkr_ref)
        m = _mla_project(h_ref, g1_ref, wdn_ref, gq_ref, gkv_ref, wuq_ref, wukv_ref)
        hn_ref[...] = m["hn"]
        cq_ref[...] = m["cqb"]
        ckv_ref[...] = m["ckvb"]
        c = c_ref[...]
        s = s_ref[...]
        gqn = gqn_ref[...]
        gqr = gqr_ref[...]
        gkn = gkn_ref[...]
        gkr = gkr_ref[...]

        dkr = dk_ref[0, :, DN:2 * DN]
        for hd in range(1, HEADS):
            dkr = dkr + dk_ref[hd, :, DN:2 * DN]
        dkr = _rope_t(dkr, c, s)
        _, krhat, rkr = _rms(m["kr_raw"], gkr, DR)
        _acc_rows(dgkr_ref, dkr * krhat)
        dkr_raw = _rms_bwd(dkr, gkr, krhat, rkr, DR)

        for hd in range(HEADS):
            ncols = slice(hd * DN, (hd + 1) * DN)
            _, xh, r = _rms(m["qp"][:, ncols], gqn, DN)
            dqn = dq_ref[hd, :, 0:DN] * SM_SCALE
            _acc_rows(dgqn_ref, dqn * xh)
            dqp_ref[:, ncols] = _rms_bwd(dqn, gqn, xh, r, DN).astype(BF16)

            rcols = slice(D + hd * LANES, D + (hd + 1) * LANES)
            _, xh, r = _rms(m["qp"][:, rcols], gqr, DR)
            dqr = _rope_t(dq_ref[hd, :, DN:2 * DN] * SM_SCALE, c, s)
            _acc_rows(dgqr_ref, dqr * xh)
            dqp_ref[:, rcols] = _rms_bwd(dqr, gqr, xh, r, DR).astype(BF16)

            kcols = slice(hd * 2 * DN, hd * 2 * DN + DN)
            _, xh, r = _rms(m["kvp"][:, kcols], gkn, DN)
            dkn = dk_ref[hd, :, 0:DN]
            _acc_rows(dgkn_ref, dkn * xh)
            dkvp_ref[:, kcols] = _rms_bwd(dkn, gkn, xh, r, DN).astype(BF16)
            dkvp_ref[:, hd * 2 * DN + DN:(hd + 1) * 2 * DN] = dv_ref[hd].astype(BF16)

        dcq = _dot_nt(dqp_ref[...], wuq_ref[...])
        _acc_rows(dgq_ref, dcq * m["cqhat"])
        dlat_q = _rms_bwd(dcq, gq_ref[...], m["cqhat"], m["rq"], QL)
        dckv = _dot_nt(dkvp_ref[...], wukv_ref[...])
        _acc_rows(dgkv_ref, dckv * m["ckvhat"])
        dlat_kv = _rms_bwd(dckv, gkv_ref[...], m["ckvhat"], m["rkv"], KVL)
        dlat = jnp.concatenate([dlat_q, dlat_kv, dkr_raw], axis=1).astype(BF16)
        dlat_ref[...] = dlat
        dhn = _dot_nt(dlat, wdn_ref[...])
        _acc_rows(dg1_ref, dhn * m["xhat"])
        dh_ref[...] = dh1_ref[...] + _rms_bwd(dhn, g1_ref[...], m["xhat"], m["rx"], D)

    hb = lambda w: pl.BlockSpec((HEADS, TM, w), lambda i: (0, i, 0))
    return pl.pallas_call(
        body, name="mla_pre_bwd", grid=(t // TM,),
        in_specs=[hb(2 * DN), hb(2 * DN), hb(DN), _row(TM, D), _row(TM, D), _const((1, D)), _const((D, LATP)),
                  _const((1, QL)), _const((1, KVL)), _const((QL, 2 * D)), _const((KVL, 2 * D)),
                  _const((1, LANES)), _const((1, LANES)), _const((1, LANES)), _const((1, LANES)),
                  _row(TM, LANES), _row(TM, LANES)],
        out_specs=[_row(TM, D), _row(TM, D), _row(TM, QL), _row(TM, KVL), _row(TM, 2 * D), _row(TM, 2 * D),
                   _row(TM, LATP), _const((8, D)), _const((8, QL)), _const((8, KVL)), _const((8, LANES)),
                   _const((8, LANES)), _const((8, LANES)), _const((8, LANES))],
        out_shape=[_sds((t, D), F32), _sds((t, D), BF16), _sds((t, QL), BF16), _sds((t, KVL), BF16),
                   _sds((t, 2 * D), BF16), _sds((t, 2 * D), BF16), _sds((t, LATP), BF16),
                   _sds((8, D), F32), _sds((8, QL), F32), _sds((8, KVL), F32), _sds((8, LANES), F32),
                   _sds((8, LANES), F32), _sds((8, LANES), F32), _sds((8, LANES), F32)],
        compiler_params=_cp("arbitrary"),
    )(dq, dk, dv, dh1, h, g1, wdn, gq, gkv, wuq, wukv, gqn, gqr, gkn, gkr, cos, sin)


def gmlp_bwd(dh1, dh1b, h, pre, g1, allw, lay, lng, lnb, wm, wmt, bfull, tril):
    t = h.shape[0]

    def body(dh1_ref, dh1b_ref, h_ref, pre_ref, g1_ref, win_ref, lng_ref, lnb_ref, wm_ref, wmt_ref, b_ref,
             wout_ref, tril_ref, dh_ref, hn_ref, dpre_ref, dws_ref, dbs_ref, dlng_ref, dlnb_ref, dg1_ref,
             dvn_s):
        _zero_at_first_step(dws_ref, dbs_ref, dlng_ref, dlnb_ref, dg1_ref)
        g1 = g1_ref[...]
        yn, xhat, rx = _rms(h_ref[...], g1, D)
        hn_ref[...] = yn.astype(BF16)
        dy = _dot_nt(dh1b_ref[...], _rows_joined(wout_ref))
        pre_u = pre_ref[:, :GH].astype(F32)
        pre_v = pre_ref[:, GH:].astype(F32)
        u, gg_u = _gelu_and_grad(pre_u)
        v, gg_v = _gelu_and_grad(pre_v)
        xc = v - jnp.mean(v, axis=-1, keepdims=True)
        rs = lax.rsqrt(jnp.mean(xc * xc, axis=-1, keepdims=True) + EPS)
        vhat = xc * rs
        lng = lng_ref[...]
        vnb = (vhat * lng + lnb_ref[...]).astype(BF16)
        dsv = dy * u
        dsvb = dsv.astype(BF16)
        tril_m = tril_ref[...]
        for ch in range(TM // GC):
            rows = slice(ch * GC, (ch + 1) * GC)
            dbs_ref[...] += dsv[rows, :]
            for g in range(GG):
                cols = slice(g * GD, (g + 1) * GD)
                sv = _dot(wm_ref[g], vnb[rows, cols]) + b_ref[:, cols]
                dpre_ref[rows, cols] = (dy[rows, cols] * sv * gg_u[rows, cols]).astype(BF16)
                dvn_s[rows, cols] = _dot(wmt_ref[g], dsvb[rows, cols])
                dws_ref[g] += _dot_nt(dsvb[rows, cols], vnb[rows, cols]) * tril_m
        dvn = dvn_s[...]
        _acc_rows(dlng_ref, dvn * vhat)
        _acc_rows(dlnb_ref, dvn)
        dvhat = dvn * lng
        dv = rs * (dvhat - jnp.mean(dvhat, axis=-1, keepdims=True)
                   - vhat * jnp.mean(dvhat * vhat, axis=-1, keepdims=True))
        dpre_v = (dv * gg_v).astype(BF16)
        dpre_ref[:, GH:] = dpre_v
        dhn = _dot_nt(dpre_ref[:, 0:D], win_ref[0])
        for c in range(1, N_CHIPS):
            dhn = dhn + _dot_nt(dpre_ref[:, c * D:(c + 1) * D], win_ref[c])
        _acc_rows(dg1_ref, dhn * xhat)
        dh_ref[...] = dh1_ref[...] + _rms_bwd(dhn, g1, xhat, rx, D)

    return pl.pallas_call(
        body, name="gmlp_bwd", grid=(t // TM,),
        in_specs=[_row(TM, D), _row(TM, D), _row(TM, D), _row(TM, 2 * GH), _const((1, D)), _wblk(D, lay["in"]),
                  _const((1, GH)), _const((1, GH)), _const((GG, GC, GC)), _const((GG, GC, GC)), _const((GC, GH)),
                  _wblk(GH // N_CHIPS, lay["out"]), _const((GC, GC))],
        out_specs=[_row(TM, D), _row(TM, D), _row(TM, 2 * GH), _const((GG, GC, GC)), _const((GC, GH)),
                   _const((8, GH)), _const((8, GH)), _const((8, D))],
        out_shape=[_sds((t, D), F32), _sds((t, D), BF16), _sds((t, 2 * GH), BF16), _sds((GG, GC, GC), F32),
                   _sds((GC, GH), F32), _sds((8, GH), F32), _sds((8, GH), F32), _sds((8, D), F32)],
        scratch_shapes=[pltpu.VMEM((TM, GH), F32)],
        compiler_params=_cp("arbitrary"),
    )(dh1, dh1b, h, pre, g1, allw, lng, lnb, wm, wmt, bfull, allw, tril)


def _token_step(t):
    return 1024 if t % 1024 == 0 else 512


def mm_tn(a, b):
    t, k = a.shape
    n = b.shape[1]
    tk = min(k, 1024)
    tn = min(n, 1024)
    tt = _token_step(t)

    def body(a_ref, b_ref, o_ref):
        @pl.when(pl.program_id(2) == 0)
        def _():
            o_ref[...] = jnp.zeros_like(o_ref)

        o_ref[...] += _dot_tn(a_ref[...].astype(BF16), b_ref[...].astype(BF16))

    return pl.pallas_call(
        body, name="mm_tn", grid=(k // tk, n // tn, t // tt),
        in_specs=[pl.BlockSpec((tt, tk), lambda i, j, s: (s, i)), pl.BlockSpec((tt, tn), lambda i, j, s: (s, j))],
        out_specs=pl.BlockSpec((tk, tn), lambda i, j, s: (i, j)), out_shape=_sds((k, n), F32),
        compiler_params=_cp("parallel", "parallel", "arbitrary"),
    )(a, b)


def mm_tn_into(buf, a, b, rows, row0, col_sharded):
    t = a.shape[0]
    tt = _token_step(t)
    assert row0 % rows == 0 and a.shape[1] == (rows if col_sharded else N_CHIPS * rows), (rows, row0, a.shape)
    assert b.shape[1] == (N_CHIPS * D if col_sharded else D), b.shape
    grid = (1, N_CHIPS, t // tt) if col_sharded else (N_CHIPS, 1, t // tt)
    fresh = isinstance(buf, int)

    def body(*refs):
        a_ref, b_ref, o_ref = refs[-3:]

        @pl.when(pl.program_id(2) == 0)
        def _():
            o_ref[...] = jnp.zeros_like(o_ref)

        o_ref[...] += _dot_tn(a_ref[...].astype(BF16), b_ref[...].astype(BF16))

    specs = [pl.BlockSpec((tt, rows), lambda i, j, s: (s, i)), pl.BlockSpec((tt, D), lambda i, j, s: (s, j))]
    return pl.pallas_call(
        body, name="mm_tn_into", grid=grid,
        in_specs=specs if fresh else [_ANY] + specs,
        out_specs=pl.BlockSpec((None, rows, D), lambda i, j, s: (i + j, row0 // rows, 0)),
        out_shape=_sds((N_CHIPS, buf, D) if fresh else buf.shape, F32),
        input_output_aliases={} if fresh else {0: 0},
        compiler_params=_cp("parallel", "parallel", "arbitrary"),
    )(*((a, b) if fresh else (buf, a, b)))


def adamw(w, g, m, v):
    rows, cols = w.shape
    tr = rows if rows <= 512 else next(r for r in (512, 384, 256, 128) if rows % r == 0)
    c1 = 1.0 - ADAM_B1 ** ADAM_STEP
    c2 = 1.0 - ADAM_B2 ** ADAM_STEP

    def body(w_ref, g_ref, m_ref, v_ref, d_ref, mo_ref, vo_ref):
        gv = g_ref[...]
        mn = ADAM_B1 * m_ref[...] + (1.0 - ADAM_B1) * gv
        vn = ADAM_B2 * v_ref[...] + (1.0 - ADAM_B2) * (gv * gv)
        mo_ref[...] = mn
        vo_ref[...] = vn
        d_ref[...] = -ADAM_LR * ((mn / c1) / (jnp.sqrt(vn / c2) + ADAM_EPS) + ADAM_WD * w_ref[...])

    spec = pl.BlockSpec((tr, cols), lambda i: (i, 0))
    return pl.pallas_call(
        body, name="adamw", grid=(rows // tr,),
        in_specs=[spec] * 4, out_specs=[spec] * 3, out_shape=[_sds((rows, cols), F32)] * 3,
        compiler_params=_cp("parallel"),
    )(w, g, m, v)


def _place():
    return lax.axis_index("x"), lax.axis_index("y"), lax.axis_index("c")


def _other_chips(x, y):
    return [(1 - x, y), (x, 1 - y), (1 - x, 1 - y)]


_ANY = pl.BlockSpec(memory_space=pl.ANY)


_HBM = pl.BlockSpec(memory_space=pltpu.HBM)
_SEM = pl.BlockSpec(memory_space=pltpu.SEMAPHORE)
_EFFECT = pltpu.SideEffectType.DATAFLOW_SIDE_EFFECTING
N_ICI = 3


def _exchange_start(name, src, land, copies, n):
    def body(src_ref, land_ref, *outs):
        sems, token = outs[:2 * n], outs[-1]
        for j, (s, d, to) in enumerate(copies(src_ref, land_ref, _place())):
            pltpu.make_async_remote_copy(src_ref=s, dst_ref=d, send_sem=sems[j], recv_sem=sems[n + j],
                                         device_id=to, device_id_type=MESH).start()
        token[...] = jnp.zeros_like(token)

    sem = pltpu.SemaphoreType.DMA(())
    outs = pl.pallas_call(
        body, name=name,
        out_shape=(sem,) * (2 * n) + (pltpu.HBM(src.shape, src.dtype), pltpu.HBM(land.shape, land.dtype),
                                      _sds((8, LANES), F32)),
        in_specs=(_HBM, _HBM),
        out_specs=(_SEM,) * (2 * n) + (_HBM, _HBM, pl.BlockSpec(memory_space=pltpu.VMEM)),
        input_output_aliases={0: 2 * n, 1: 2 * n + 1},
        compiler_params=pltpu.CompilerParams(has_side_effects=_EFFECT),
    )(pltpu.with_memory_space_constraint(src, pltpu.HBM), pltpu.with_memory_space_constraint(land, pltpu.HBM))
    return outs[:2 * n], outs[2 * n], outs[2 * n + 1], outs[-1]


def _exchange_wait(name, sems, src, land, after, arrivals):
    n = len(sems) // 2

    def body(src_ref, land_ref, *rest):
        sems = rest[:2 * n]
        for j, (s, d) in enumerate(arrivals(src_ref, land_ref, _place())):
            cp = pltpu.make_async_remote_copy(src_ref=s, dst_ref=d, send_sem=sems[j], recv_sem=sems[n + j],
                                              device_id=_place(), device_id_type=MESH)
            cp.wait_send()
            cp.wait_recv()

    return pl.pallas_call(
        body, name=name, out_shape=(pltpu.HBM(src.shape, src.dtype), pltpu.HBM(land.shape, land.dtype)),
        in_specs=(_HBM, _HBM) + (_SEM,) * (2 * n) + (_ANY,), out_specs=(_HBM, _HBM),
        input_output_aliases={0: 0, 1: 1},
        compiler_params=pltpu.CompilerParams(has_side_effects=_EFFECT),
    )(src, land, *sems, after)


def _halves(c, hh):
    return pl.ds(pl.multiple_of(c * hh, 16), hh), pl.ds(pl.multiple_of((1 - c) * hh, 16), hh)


def gather_start(land, tag):
    _, rr, _ = land.shape
    assert rr % 32 == 0, rr

    def copies(_, land_ref, place):
        x, y, c = place
        mine = land_ref.at[2 * x + y, _halves(c, rr // 2)[0]]
        return [(mine, mine, (cx, cy, c)) for cx, cy in _other_chips(x, y)]

    return _exchange_start(f"gather_start_{tag}", jnp.zeros((8, LANES), F32), land, copies, N_ICI)


def gather_wait(sems, src, land, after, tag):
    def arrivals(_, land_ref, place):
        x, y, c = place
        half = _halves(c, land.shape[1] // 2)[0]
        return [(land_ref.at[2 * x + y, half], land_ref.at[2 * cx + cy, half]) for cx, cy in _other_chips(x, y)]

    return _exchange_wait(f"gather_wait_{tag}", sems, src, land, after, arrivals)


def pass_start(land, tag):
    def copies(_, land_ref, place):
        x, y, c = place
        half = _halves(c, land.shape[1] // 2)[0]
        return [(land_ref.at[2 * cx + cy, half], land_ref.at[2 * cx + cy, half], (x, y, 1 - c))
                for cx, cy in _other_chips(x, y)]

    return _exchange_start(f"pass_start_{tag}", jnp.zeros((8, LANES), F32), land, copies, N_ICI)


def pass_wait(sems, src, land, after, tag):
    def arrivals(_, land_ref, place):
        x, y, c = place
        mine, other = _halves(c, land.shape[1] // 2)
        return [(land_ref.at[2 * cx + cy, mine], land_ref.at[2 * cx + cy, other]) for cx, cy in _other_chips(x, y)]

    return _exchange_wait(f"pass_wait_{tag}", sems, src, land, after, arrivals)


def swap_start(g, tag):
    _, rr, cc = g.shape

    def copies(g_ref, got_ref, place):
        x, y, c = place
        other = _halves(c, rr // 2)[1]
        return [(g_ref.at[k, other], got_ref.at[k], (x, y, 1 - c)) for k in range(N_CHIPS)]

    return _exchange_start(f"swap_start_{tag}", g, lax.empty((N_CHIPS, rr // 2, cc), g.dtype), copies, N_CHIPS)


def swap_wait(sems, g, got, after, tag):
    def arrivals(g_ref, got_ref, place):
        other = _halves(place[2], g.shape[1] // 2)[1]
        return [(g_ref.at[k, other], got_ref.at[k]) for k in range(N_CHIPS)]

    return _exchange_wait(f"swap_wait_{tag}", sems, g, got, after, arrivals)


def chip_sum(place, g32, got):
    _, rr, cc = g32.shape
    hh = rr // 2
    tr = SUM_ROWS
    assert rr % 2 == 0 and hh % tr == 0, (rr, tr)
    nb = hh // tr

    def body(place_ref, g_ref, got_ref, own_ref, all_ref):
        s = g_ref[...] + got_ref[...].astype(F32)
        all_ref[...] = s.astype(BF16)
        own_ref[...] = g_ref[place_ref[1]] + got_ref[place_ref[1]].astype(F32)

    return pl.pallas_call(
        body, name="chip_sum",
        grid_spec=pltpu.PrefetchScalarGridSpec(
            num_scalar_prefetch=1, grid=(nb,),
            in_specs=[pl.BlockSpec((N_CHIPS, tr, cc), lambda i, pr: (0, pr[0] * nb + i, 0)),
                      pl.BlockSpec((N_CHIPS, tr, cc), lambda i, pr: (0, i, 0))],
            out_specs=[pl.BlockSpec((tr, cc), lambda i, pr: (i, 0)),
                       pl.BlockSpec((N_CHIPS, tr, cc), lambda i, pr: (0, i, 0))]),
        out_shape=[_sds((hh, cc), F32), _sds((N_CHIPS, hh, cc), BF16)],
        compiler_params=_cp("parallel"),
    )(place, g32, got)


def _scatter_copies(s_ref, land_ref, place):
    x, y, c = place
    return [(s_ref.at[2 * cx + cy], land_ref.at[j], (cx, cy, c)) for j, (cx, cy) in enumerate(_other_chips(x, y))]


def scatter_start(s, tag):
    return _exchange_start(f"scatter_start_{tag}", s, lax.empty((N_ICI,) + s.shape[1:], s.dtype), _scatter_copies, N_ICI)


def scatter_wait(sems, s, land, after, tag):
    return _exchange_wait(f"scatter_wait_{tag}", sems, s, land, after,
                          lambda s_ref, land_ref, place: [(a, b) for a, b, _ in _scatter_copies(s_ref, land_ref, place)])


def final_sum(place, own, got):
    hh, cc = own.shape
    tr = SUM_ROWS
    assert hh % tr == 0, (hh, tr)
    nb = hh // tr

    def body(place_ref, own_ref, got_ref, o_ref):
        del place_ref
        o_ref[...] = ((own_ref[...] + got_ref[0].astype(F32)) + got_ref[1].astype(F32)) + got_ref[2].astype(F32)

    return pl.pallas_call(
        body, name="final_sum",
        grid_spec=pltpu.PrefetchScalarGridSpec(
            num_scalar_prefetch=1, grid=(nb,),
            in_specs=[pl.BlockSpec((tr, cc), lambda i, pr: (i, 0)), pl.BlockSpec((3, tr, cc), lambda i, pr: (0, i, 0))],
            out_specs=pl.BlockSpec((tr, cc), lambda i, pr: (pr[0] * nb + i, 0))),
        out_shape=_sds((2 * hh, cc), F32),
        compiler_params=_cp("parallel"),
    )(place, own, got)


def share_with_sibling(f):
    rr, cc = f.shape
    hh = rr // 2

    def body(f_ref, o_ref, send_sem, recv_sem):
        del o_ref
        x, y, c = _place()
        mine_half = f_ref.at[pl.ds(pl.multiple_of(c * hh, 8), hh)]
        cp = pltpu.make_async_remote_copy(src_ref=mine_half, dst_ref=mine_half, send_sem=send_sem,
                                          recv_sem=recv_sem, device_id=(x, y, 1 - c), device_id_type=MESH)
        cp.start()
        cp.wait()

    return pl.pallas_call(
        body, name="share_with_sibling", in_specs=[_ANY], out_specs=_ANY, out_shape=_sds(f.shape, f.dtype),
        input_output_aliases={0: 0},
        scratch_shapes=[pltpu.SemaphoreType.DMA, pltpu.SemaphoreType.DMA],
    )(f)


N_DEV = 8


def _peers(place):
    x, y, c = place
    return [((1 - x) if r & 4 else x, (1 - y) if r & 2 else y, (1 - c) if r & 1 else c) for r in range(1, N_DEV)]


def _device_index(place):
    x, y, c = place
    return 4 * x + 2 * y + c


def small_start(land):
    def copies(_, land_ref, place):
        mine = land_ref.at[_device_index(place)]
        return [(mine, mine, to) for to in _peers(place)]

    return _exchange_start("small_start", jnp.zeros((8, LANES), F32), land, copies, N_DEV - 1)


def small_wait(sems, src, land, after):
    def arrivals(_, land_ref, place):
        return [(land_ref.at[_device_index(place)], land_ref.at[_device_index(peer)]) for peer in _peers(place)]

    return _exchange_wait("small_wait", sems, src, land, after, arrivals)


def sum_devices(land):
    _, rr, cc = land.shape
    tr = 56
    assert rr % tr == 0, rr

    def body(l_ref, o_ref):
        acc = l_ref[0]
        for d in range(1, N_DEV):
            acc = acc + l_ref[d]
        o_ref[...] = acc

    return pl.pallas_call(
        body, name="sum_devices", grid=(rr // tr,),
        in_specs=[pl.BlockSpec((N_DEV, tr, cc), lambda i: (0, i, 0))],
        out_specs=pl.BlockSpec((tr, cc), lambda i: (i, 0)), out_shape=_sds((rr, cc), F32),
        compiler_params=_cp("parallel"),
    )(land)


_BIG = ["mla_w_down", "mla_w_uq", "mla_w_ukv", "mla_w_out", "gmlp_w_in", "gmlp_w_out", "ffn_w_up", "ffn_w_down",
        "ple_w_gate", "ple_w_proj"]
_SMALL = ["norm_mix", "norm_ffn", "norm_ple", "mla_q_lora_g", "mla_kv_lora_g", "mla_q_nope_g", "mla_q_rope_g",
          "mla_k_nope_g", "mla_k_rope_g", "gmlp_ln_g", "gmlp_ln_b", "gmlp_w_s", "gmlp_b_s"]

_LAY_MLA = dict(up=0, down=1024, out=2048, gate=2304, wdn=2560, wuq=2736, wukv=2880, proj=3008, rows=3072)
_LAY_GMLP = {"up": 0, "down": 1024, "in": 2048, "out": 3072, "gate": 3584, "proj": 3840, "ln": 3904, "rows": 4096}


def _layer_parts(i):
    j = i // 2
    if i % 2 == 0:
        lay = _LAY_MLA
        return lay, [("ffn_w_up", i, lay["up"]), ("ffn_w_down", i, lay["down"]), ("mla_w_out", j, lay["out"]),
                     ("ple_w_gate", i, lay["gate"]), ("mla_w_down", j, lay["wdn"]), ("mla_w_uq", j, lay["wuq"]),
                     ("mla_w_ukv", j, lay["wukv"]), ("ple_w_proj", i, lay["proj"])]
    lay = _LAY_GMLP
    return lay, [("ffn_w_up", i, lay["up"]), ("ffn_w_down", i, lay["down"]), ("gmlp_w_in", j, lay["in"]),
                 ("gmlp_w_out", j, lay["out"]), ("ple_w_gate", i, lay["gate"]), ("ple_w_proj", i, lay["proj"])]


def _pack_rows(parts, dtype, pad_to=None, slot=False):
    flat = [p.reshape(-1).astype(dtype) for p in parts]
    if pad_to is not None:
        flat.append(jnp.zeros((pad_to * D - sum(f.size for f in flat),), dtype))
    flat = jnp.concatenate(flat)
    return flat.reshape(1, -1, D) if slot else flat.reshape(-1, D)


def _odd(allw, row0, a, b):
    return allw[:, row0:row0 + a * b // D].reshape(N_CHIPS, a, b)


def _cols_joined(s):
    return jnp.transpose(s, (1, 0, 2)).reshape(s.shape[1], N_CHIPS * s.shape[2])


def _col_shards(full):
    a, bb = full.shape
    return jnp.transpose(full.reshape(a, N_CHIPS, bb // N_CHIPS), (1, 0, 2)).reshape(N_CHIPS, -1, D)


def _pad_lanes(g):
    return jnp.pad(g, ((0, 0), (0, LANES - g.shape[1])))


def _split_uq(wuq):
    l = wuq.shape[0]
    w = wuq.reshape(l, QL, HEADS, DN + DR)
    nope = w[..., :DN].reshape(l, QL, HEADS * DN)
    rope = jnp.pad(w[..., DN:], ((0, 0), (0, 0), (0, 0), (0, LANES - DR))).reshape(l, QL, HEADS * LANES)
    return jnp.concatenate([nope, rope], axis=-1)


def _merge_uq(d):
    nope = d[:, :HEADS * DN].reshape(QL, HEADS, DN)
    rope = d[:, HEADS * DN:].reshape(QL, HEADS, LANES)[..., :DR]
    return jnp.concatenate([nope, rope], axis=-1).reshape(QL, HEADS * (DN + DR))


def _rope_tables(positions):
    inv_freq = ROPE_BASE ** (-(jnp.arange(0, DR, 2, dtype=F32) / DR))
    ang = positions.reshape(-1).astype(F32)[:, None] * inv_freq
    z = jnp.zeros((ang.shape[0], LANES - DR), F32)
    return (jnp.concatenate([jnp.cos(ang), jnp.cos(ang), z], axis=1),
            jnp.concatenate([jnp.sin(ang), jnp.sin(ang), z], axis=1))


def kernel(x, p, positions, norm_mix, norm_ffn, norm_ple, mla_w_down, mla_q_lora_g, mla_kv_lora_g, mla_w_uq, mla_w_ukv, mla_q_nope_g, mla_q_rope_g, mla_k_nope_g, mla_k_rope_g, mla_w_out, gmlp_w_in, gmlp_ln_g, gmlp_ln_b, gmlp_w_s, gmlp_b_s, gmlp_w_out, ffn_w_up, ffn_w_down, ple_w_gate, ple_w_proj, loss_target, m_norm_mix, m_norm_ffn, m_norm_ple, m_mla_w_down, m_mla_q_lora_g, m_mla_kv_lora_g, m_mla_w_uq, m_mla_w_ukv, m_mla_q_nope_g, m_mla_q_rope_g, m_mla_k_nope_g, m_mla_k_rope_g, m_mla_w_out, m_gmlp_w_in, m_gmlp_ln_g, m_gmlp_ln_b, m_gmlp_w_s, m_gmlp_b_s, m_gmlp_w_out, m_ffn_w_up, m_ffn_w_down, m_ple_w_gate, m_ple_w_proj, v_norm_mix, v_norm_ffn, v_norm_ple, v_mla_w_down, v_mla_q_lora_g, v_mla_kv_lora_g, v_mla_w_uq, v_mla_w_ukv, v_mla_q_nope_g, v_mla_q_rope_g, v_mla_k_nope_g, v_mla_k_rope_g, v_mla_w_out, v_gmlp_w_in, v_gmlp_ln_g, v_gmlp_ln_b, v_gmlp_w_s, v_gmlp_b_s, v_gmlp_w_out, v_ffn_w_up, v_ffn_w_down, v_ple_w_gate, v_ple_w_proj):
    args = dict(locals())
    weights = {n: args[n] for n in _BIG + _SMALL}
    depth = norm_mix.shape[0]
    nb, seq, _ = x.shape
    t = nb * seq
    assert seq % TQ == 0 and seq % TM == 0 and t % 512 == 0, (nb, seq)
    cx = lax.axis_index("x")
    cy = lax.axis_index("y")
    cc = lax.axis_index("c")
    chip = 2 * cx + cy

    gathers = []
    token = None
    for i in range(depth):
        lay, parts = _layer_parts(i)
        rows = [weights[n][l] for n, l, _ in parts]
        if token is not None:
            rows[0] = rows[0] + token[0, 0]
        if i % 2 == 1:
            ln = jnp.stack([gmlp_ln_g[i // 2], gmlp_ln_b[i // 2]]).astype(F32)
            rows.append(lax.bitcast_convert_type(ln, BF16))
        mine = _pack_rows(rows, BF16, pad_to=lay["rows"], slot=True)
        land = lax.dynamic_update_slice(lax.empty((N_CHIPS, lay["rows"], D), BF16), mine, (chip, 0, 0))
        sems, src, land, token = gather_start(land, i)
        gathers.append((sems, src, land))
    allw = [None] * depth

    tril = jnp.tril(jnp.ones((GC, GC), F32))
    wm = (gmlp_w_s * tril).astype(BF16)
    wmt = jnp.swapaxes(wm, -1, -2)
    bfull = jnp.repeat(jnp.swapaxes(gmlp_b_s, -1, -2), GD, axis=-1)
    cos, sin = _rope_tables(positions)
    row = lambda g: g.reshape(1, -1)
    gqr = _pad_lanes(mla_q_rope_g)
    gkr = _pad_lanes(mla_k_rope_g)

    h = x.reshape(t, D)
    pt = p.reshape(depth, t, PLE)
    saved = []

    def arrive(i, after):
        sems, src, land = gathers[i]
        _, land = gather_wait(sems, src, land, after, i)
        return pass_start(land, i)

    passing = arrive(0, token)
    for i in range(depth):
        j = i // 2
        lay, _ = _layer_parts(i)
        sems, src, land, token = passing
        _, aw = pass_wait(sems, src, land, token if i == 0 else h, i)
        allw[i] = aw
        s = dict(h=h)
        if i % 2 == 0:
            wdn = jnp.pad(_odd(aw, lay["wdn"], D // N_CHIPS, LAT).reshape(D, LAT), ((0, 0), (0, LATP - LAT)))
            wuq = _split_uq(_cols_joined(_odd(aw, lay["wuq"], QL, 384))[None])[0]
            wukv = _cols_joined(_odd(aw, lay["wukv"], KVL, 512))
            mla_args = (row(norm_mix[i]), wdn, row(mla_q_lora_g[j]), row(mla_kv_lora_g[j]), wuq, wukv,
                        row(mla_q_nope_g[j]), gqr[j:j + 1], row(mla_k_nope_g[j]), gkr[j:j + 1], cos, sin)
            q, k, v = mla_pre_fwd(h, *mla_args)
            y, lse = flash_fwd(q, k, v, seq)
            s.update(q=q, k=k, v=v, lse=lse, mla_args=mla_args)
        else:
            ln = lax.bitcast_convert_type(aw[:, lay["ln"]:lay["ln"] + 2].reshape(N_CHIPS, 2, GH // N_CHIPS, 2), F32)
            ln = jnp.transpose(ln, (1, 0, 2)).reshape(2, 1, GH)
            y, pre = gmlp_fwd(h, row(norm_mix[i]), aw, lay, ln[0], ln[1], wm[j], bfull[j])
            s.update(pre=pre, ln=ln)
        wp = _cols_joined(_odd(aw, lay["proj"], PLE, 256))
        g2 = row(norm_ffn[i])
        if i + 1 < depth:
            passing = arrive(i + 1, y)
            g2 = g2 + passing[3][0:1, 0:1]
        h1, h2, hn2, r = mixffn_fwd(h, y, aw, lay, g2)
        h, hn3 = ple_fwd(h2, pt[i], row(norm_ple[i]), aw, lay, wp)
        s.update(y=y, wp=wp, h1=h1, h2=h2, hn2=hn2, r=r, hn3=hn3)
        saved.append(s)

    dh, loss_part = loss_head(h, loss_target.reshape(t, D))
    loss = lax.psum(loss_part[0, 0], ("x", "y", "c"))

    gs = {n: [None] * weights[n].shape[0] for n in _SMALL}
    gw = {n: [None] * weights[n].shape[0] for n in _BIG}
    place = jnp.stack([cc, chip]).astype(jnp.int32)
    scatters = []
    swapping = None
    token = None

    def put(b, row0, shards):
        return lax.dynamic_update_slice(b, shards.reshape(N_CHIPS, -1, D), (0, row0, 0))

    def swapped(after):
        ii, sems, g, got = swapping
        g, got = swap_wait(sems, g, got, after, ii)
        own, sums = chip_sum(place, g, got)
        sems, sums, land, tok = scatter_start(sums, ii)
        scatters.append((ii, own, sems, sums, land))
        return tok

    for i in reversed(range(depth)):
        j = i // 2
        lay, parts = _layer_parts(i)
        aw = allw[i]
        s = saved[i]

        g3 = row(norm_ple[i])
        if token is not None:
            g3 = g3 + token[0:1, 0:1]
        dh2, dh2b, dgt, dpp, dg3 = ple_bwd(dh, s["h2"], pt[i], g3, aw, lay, s["wp"])
        gs["norm_ple"][i] = dg3[0]
        buf = mm_tn_into(lay["rows"], s["hn3"], dgt, D // N_CHIPS, lay["gate"], False)
        tail = lay.get("ln", lay["rows"])
        if tail < lay["rows"]:
            buf = put(buf, tail, jnp.zeros((N_CHIPS, lay["rows"] - tail, D), F32))
        buf = put(buf, lay["proj"], _col_shards(mm_tn(pt[i], dpp)))
        dh1, dh1b, du, a, dg2 = ffn_bwd(dh2, dh2b, s["h1"], s["r"], row(norm_ffn[i]), aw, lay)
        gs["norm_ffn"][i] = dg2[0]
        buf = mm_tn_into(buf, a, dh2b, D, lay["down"], False)
        buf = mm_tn_into(buf, s["hn2"], du, D, lay["up"], True)
        buf = mm_tn_into(buf, s["y"], dh1b, s["y"].shape[1] // N_CHIPS, lay["out"], False)
        g1 = row(norm_mix[i])
        if swapping is not None:
            g1 = g1 + swapped(dh1)[0:1, 0:1]
        if i % 2 == 0:
            do = linear_nt(dh1b, aw, D // N_CHIPS, lay["out"])
            dq, dk, dv = flash_bwd(s["q"], s["k"], s["v"], s["y"], do, s["lse"], seq)
            (dh, hn1, cq, ckv, dqp, dkvp, dlat, dg1, dgq, dgkv, dgqn, dgqr, dgkn, dgkr) = mla_pre_bwd(
                dq, dk, dv, dh1, s["h"], g1, *s["mla_args"][1:])
            gs["norm_mix"][i] = dg1[0]
            gs["mla_q_lora_g"][j] = dgq[0]
            gs["mla_kv_lora_g"][j] = dgkv[0]
            gs["mla_q_nope_g"][j] = dgqn[0]
            gs["mla_q_rope_g"][j] = dgqr[0, :DR]
            gs["mla_k_nope_g"][j] = dgkn[0]
            gs["mla_k_rope_g"][j] = dgkr[0, :DR]
            buf = put(buf, lay["wdn"], mm_tn(hn1, dlat)[:, :LAT])
            buf = put(buf, lay["wuq"], _col_shards(_merge_uq(mm_tn(cq, dqp))))
            buf = put(buf, lay["wukv"], _col_shards(mm_tn(ckv, dkvp)))
        else:
            dh, hn1, dpre, dws, dbs, dlng, dlnb, dg1 = gmlp_bwd(
                dh1, dh1b, s["h"], s["pre"], g1, aw, lay, s["ln"][0], s["ln"][1], wm[j], wmt[j], bfull[j], tril)
            gs["norm_mix"][i] = dg1[0]
            gs["gmlp_ln_g"][j] = dlng[0]
            gs["gmlp_ln_b"][j] = dlnb[0]
            gs["gmlp_w_s"][j] = dws
            gs["gmlp_b_s"][j] = jnp.sum(dbs.reshape(GC, GG, GD), axis=-1).T
            buf = mm_tn_into(buf, hn1, dpre, D, lay["in"], True)

        sems, buf, got, token = swap_start(buf, i)
        swapping = (i, sems, buf, got)
    swapped(dh)
    grad_x = dh.reshape(x.shape)

    small_sizes = [weights[n].size if n not in ("gmlp_ln_g", "gmlp_ln_b") else weights[n].shape[0] * GH
                   for n in _SMALL]
    small_rows = -(-sum(small_sizes) // (56 * D)) * 56
    part = _pack_rows([jnp.stack(gs[n]) for n in _SMALL], F32, pad_to=small_rows, slot=True)
    land = lax.dynamic_update_slice(lax.empty((N_DEV, small_rows, D), F32), part, (2 * chip + cc, 0, 0))
    small = small_start(land)

    after = small[3]
    for i, own, sems, sums, land in scatters:
        _, got = scatter_wait(sems, sums, land, after, i)
        after = reduced = share_with_sibling(final_sum(place, own, got))
        for n, l, row0 in _layer_parts(i)[1]:
            gw[n][l] = reduced[row0:row0 + weights[n][l].size // D].reshape(weights[n].shape[1:])
    grads = {n: jnp.stack(gw[n]) for n in _BIG}

    tot = sum_devices(small_wait(small[0], small[1], small[2], after)[1]).reshape(-1)
    off = 0
    for n, sz in zip(_SMALL, small_sizes):
        gsum = tot[off:off + sz]
        off += sz
        if n in ("gmlp_ln_g", "gmlp_ln_b"):
            gsum = lax.dynamic_slice_in_dim(gsum.reshape(-1, GH), chip * (GH // N_CHIPS), GH // N_CHIPS, axis=1)
        grads[n] = gsum.reshape(weights[n].shape)

    delta, new_m, new_v = {}, {}, {}
    for n in _BIG:
        w2 = weights[n].reshape(-1, weights[n].shape[-1])
        d, mn, vn = adamw(w2, grads[n].reshape(w2.shape), args["m_" + n].reshape(w2.shape),
                          args["v_" + n].reshape(w2.shape))
        delta[n], new_m[n], new_v[n] = (a.reshape(weights[n].shape) for a in (d, mn, vn))
    own_sizes = [weights[n].size for n in _SMALL]
    own_rows = -(-sum(own_sizes) // (8 * D)) * 8
    packed = [_pack_rows([src[n] for n in _SMALL], F32, pad_to=own_rows)
              for src in (weights, grads, {n: args["m_" + n] for n in _SMALL}, {n: args["v_" + n] for n in _SMALL})]
    outs = adamw(*packed)
    off = 0
    for n, sz in zip(_SMALL, own_sizes):
        for dst, o in zip((delta, new_m, new_v), outs):
            dst[n] = o.reshape(-1)[off:off + sz].reshape(weights[n].shape)
        off += sz

    order = ["norm_mix", "norm_ffn", "norm_ple", "mla_w_down", "mla_q_lora_g", "mla_kv_lora_g", "mla_w_uq",
             "mla_w_ukv", "mla_q_nope_g", "mla_q_rope_g", "mla_k_nope_g", "mla_k_rope_g", "mla_w_out", "gmlp_w_in",
             "gmlp_ln_g", "gmlp_ln_b", "gmlp_w_s", "gmlp_b_s", "gmlp_w_out", "ffn_w_up", "ffn_w_down", "ple_w_gate",
             "ple_w_proj"]
    return (loss, grad_x, *[grads[n] for n in order], *[delta[n] for n in order], *[new_m[n] for n in order],
            *[new_v[n] for n in order])
```

```python
import functools

import jax
import jax.numpy as jnp
from jax import lax
from jax.experimental import pallas as pl
from jax.experimental.pallas import tpu as pltpu

F32 = jnp.float32
BF16 = jnp.bfloat16
MESH = pl.DeviceIdType.MESH

D = 1024
HEADS = 8
DN = 128
DR = 64
QL = 384
KVL = 256
LAT = 704
LATP = 768
DFF = 4096
GH = 2048
GC = 128
GG = 8
GD = 256
PLE = 256
EPS = 1e-6
ROPE_BASE = 10000.0
SM_SCALE = (DN + DR) ** -0.5
N_CHIPS = 4
LANES = 128

ADAM_LR = 0.001
ADAM_B1 = 0.9
ADAM_B2 = 0.999
ADAM_EPS = 1e-08
ADAM_WD = 0.01
ADAM_STEP = 10

TM = 256
TMB = 512
TQ = 512
TQ_FWD = 512
FWD_HEADS = 2
BWD_HEADS = 2
SUM_ROWS = 256
VMEM_LIMIT = 56 * 1024 * 1024


def _cp(*sem):
    return pltpu.CompilerParams(dimension_semantics=sem, vmem_limit_bytes=VMEM_LIMIT)


def _dot(a, b):
    return jnp.dot(a, b, preferred_element_type=F32)


def _dot_nt(a, b):
    return lax.dot_general(a, b, (((1,), (1,)), ((), ())), preferred_element_type=F32)


def _dot_tn(a, b):
    return lax.dot_general(a, b, (((0,), (0,)), ((), ())), preferred_element_type=F32)


def _rms(x, g, n):
    r = lax.rsqrt(jnp.sum(x * x, axis=-1, keepdims=True) * (1.0 / n) + EPS)
    xhat = x * r
    return xhat * g, xhat, r


def _rms_bwd(dy, g, xhat, r, n):
    dxhat = dy * g
    return r * (dxhat - xhat * (jnp.sum(dxhat * xhat, axis=-1, keepdims=True) * (1.0 / n)))


def _rope(x, c, s):
    return x * c + (pltpu.roll(x, 32, 1) - pltpu.roll(x, 96, 1)) * s


def _rope_t(dy, c, s):
    w = dy * s
    return dy * c + pltpu.roll(w, 96, 1) - pltpu.roll(w, 32, 1)


def _sigmoid(x):
    return 1.0 / (1.0 + jnp.exp(-x))


_GELU_K = 0.7978845608028654
_GELU_C = 0.044715


def _gelu(x):
    return 0.5 * x * (1.0 + jnp.tanh(_GELU_K * (x + _GELU_C * x * x * x)))


def _gelu_and_grad(x):
    x2 = x * x
    t = jnp.tanh(_GELU_K * (x + _GELU_C * x2 * x))
    half = 0.5 * (1.0 + t)
    return x * half, half + 0.5 * x * (1.0 - t * t) * (_GELU_K * (1.0 + 3.0 * _GELU_C * x2))


def _acc_rows(ref, val):
    ref[...] += jnp.broadcast_to(jnp.sum(val, axis=0, keepdims=True), ref.shape)


def _row(tm, c):
    return pl.BlockSpec((tm, c), lambda i: (i, 0))


def _const(shape):
    nd = len(shape)
    return pl.BlockSpec(shape, lambda i: (0,) * nd, pipeline_mode=pl.Buffered(1))


def _wblk(rows, row0):
    assert row0 % rows == 0, (rows, row0)
    return pl.BlockSpec((N_CHIPS, rows, D), lambda i: (0, row0 // rows, 0), pipeline_mode=pl.Buffered(1))


def _rows_joined(w_ref):
    return w_ref[...].reshape(N_CHIPS * w_ref.shape[1], D)


def _sds(shape, dtype):
    return jax.ShapeDtypeStruct(shape, dtype)


def mixffn_fwd(h, y, allw, lay, g2):
    t, k = y.shape

    def body(h_ref, y_ref, wo_ref, g_ref, wu_ref, wd_ref, h1_ref, h2_ref, hn_ref, r_ref):
        h1 = h_ref[...] + _dot(y_ref[...], _rows_joined(wo_ref))
        h1_ref[...] = h1
        yn, _, _ = _rms(h1, g_ref[...], D)
        hn = yn.astype(BF16)
        hn_ref[...] = hn
        f = jnp.zeros((TMB, D), F32)
        for c in range(N_CHIPS):
            r = jnp.maximum(_dot(hn, wu_ref[c]), 0.0)
            r_ref[:, c * D:(c + 1) * D] = r.astype(BF16)
            f = f + _dot((r * r).astype(BF16), wd_ref[c])
        h2_ref[...] = h1 + f

    return pl.pallas_call(
        body, name="mixffn_fwd", grid=(t // TMB,),
        in_specs=[_row(TMB, D), _row(TMB, k), _wblk(k // N_CHIPS, lay["out"]), _const((1, D)), _wblk(D, lay["up"]),
                  _wblk(D, lay["down"])],
        out_specs=[_row(TMB, D), _row(TMB, D), _row(TMB, D), _row(TMB, DFF)],
        out_shape=[_sds((t, D), F32), _sds((t, D), F32), _sds((t, D), BF16), _sds((t, DFF), BF16)],
        compiler_params=_cp("parallel"),
    )(h, y, allw, g2, allw, allw)


def ple_fwd(h2, p, g3, allw, lay, wp):
    t = h2.shape[0]

    def body(h_ref, p_ref, g_ref, wg_ref, wp_ref, h3_ref, hn_ref):
        x = h_ref[...]
        yn, _, _ = _rms(x, g_ref[...], D)
        hn = yn.astype(BF16)
        hn_ref[...] = hn
        gt = _dot(hn, _rows_joined(wg_ref))
        pp = _dot(p_ref[...].astype(BF16), wp_ref[...])
        h3_ref[...] = x + _sigmoid(gt) * pp

    return pl.pallas_call(
        body, name="ple_fwd", grid=(t // TMB,),
        in_specs=[_row(TMB, D), _row(TMB, PLE), _const((1, D)), _wblk(D // N_CHIPS, lay["gate"]), _const((PLE, D))],
        out_specs=[_row(TMB, D), _row(TMB, D)],
        out_shape=[_sds((t, D), F32), _sds((t, D), BF16)],
        compiler_params=_cp("parallel"),
    )(h2, p, g3, allw, wp)


def _mla_project(h_ref, g1_ref, wdn_ref, gq_ref, gkv_ref, wuq_ref, wukv_ref):
    x = h_ref[...]
    yn, xhat, rx = _rms(x, g1_ref[...], D)
    hn = yn.astype(BF16)
    lat = _dot(hn, wdn_ref[...])
    cq, cqhat, rq = _rms(lat[:, :QL], gq_ref[...], QL)
    ckv, ckvhat, rkv = _rms(lat[:, QL:QL + KVL], gkv_ref[...], KVL)
    kr_raw = lat[:, QL + KVL:]
    cqb = cq.astype(BF16)
    ckvb = ckv.astype(BF16)
    qp = _dot(cqb, wuq_ref[...])
    kvp = _dot(ckvb, wukv_ref[...])
    return dict(xhat=xhat, rx=rx, hn=hn, cqhat=cqhat, rq=rq, ckvhat=ckvhat, rkv=rkv, kr_raw=kr_raw,
                cqb=cqb, ckvb=ckvb, qp=qp, kvp=kvp)


def mla_pre_fwd(h, g1, wdn, gq, gkv, wuq, wukv, gqn, gqr, gkn, gkr, cos, sin):
    t = h.shape[0]

    def body(h_ref, g1_ref, wdn_ref, gq_ref, gkv_ref, wuq_ref, wukv_ref, gqn_ref, gqr_ref, gkn_ref, gkr_ref,
             c_ref, s_ref, q_ref, k_ref, v_ref):
        m = _mla_project(h_ref, g1_ref, wdn_ref, gq_ref, gkv_ref, wuq_ref, wukv_ref)
        c = c_ref[...]
        s = s_ref[...]
        kr, _, _ = _rms(m["kr_raw"], gkr_ref[...], DR)
        krb = _rope(kr, c, s).astype(BF16)
        for hd in range(HEADS):
            qn, _, _ = _rms(m["qp"][:, hd * DN:(hd + 1) * DN], gqn_ref[...], DN)
            qr, _, _ = _rms(m["qp"][:, D + hd * LANES:D + (hd + 1) * LANES], gqr_ref[...], DR)
            q_ref[hd, :, 0:DN] = (qn * SM_SCALE).astype(BF16)
            q_ref[hd, :, DN:2 * DN] = (_rope(qr, c, s) * SM_SCALE).astype(BF16)
            kn, _, _ = _rms(m["kvp"][:, hd * 2 * DN:hd * 2 * DN + DN], gkn_ref[...], DN)
            k_ref[hd, :, 0:DN] = kn.astype(BF16)
            k_ref[hd, :, DN:2 * DN] = krb
            v_ref[hd] = m["kvp"][:, hd * 2 * DN + DN:(hd + 1) * 2 * DN].astype(BF16)

    hb = lambda w: pl.BlockSpec((HEADS, TM, w), lambda i: (0, i, 0))
    return pl.pallas_call(
        body, name="mla_pre_fwd", grid=(t // TM,),
        in_specs=[_row(TM, D), _const((1, D)), _const((D, LATP)), _const((1, QL)), _const((1, KVL)),
                  _const((QL, 2 * D)), _const((KVL, 2 * D)), _const((1, LANES)), _const((1, LANES)),
                  _const((1, LANES)), _const((1, LANES)), _row(TM, LANES), _row(TM, LANES)],
        out_specs=[hb(2 * DN), hb(2 * DN), hb(DN)],
        out_shape=[_sds((HEADS, t, 2 * DN), BF16), _sds((HEADS, t, 2 * DN), BF16), _sds((HEADS, t, DN), BF16)],
        compiler_params=_cp("parallel"),
    )(h, g1, wdn, gq, gkv, wuq, wukv, gqn, gqr, gkn, gkr, cos, sin)


def _diagonal_mask(n=TQ):
    return lax.broadcasted_iota(jnp.int32, (n, n), 1) <= lax.broadcasted_iota(jnp.int32, (n, n), 0)


def flash_fwd(q, k, v, seq):
    t = q.shape[1]
    nb = t // seq
    tq = TQ_FWD
    nq = seq // tq
    hp = FWD_HEADS

    def body(q_ref, k_ref, v_ref, o_ref, lse_ref):
        qi = pl.program_id(2)
        qs = [q_ref[a] for a in range(hp)]

        def step(j, carry, diagonal=False):
            rows = pl.ds(pl.multiple_of(j * tq, tq), tq)
            out = []
            for a in range(hp):
                m, l, acc = carry[a]
                s = _dot_nt(qs[a], k_ref[a, rows, :])
                if diagonal:
                    s = jnp.where(_diagonal_mask(tq), s, -1e30)
                m_new = jnp.maximum(m, jnp.max(s, axis=-1, keepdims=True))
                p = jnp.exp(s - m_new)
                alpha = jnp.exp(m - m_new)
                l = alpha * l + jnp.sum(p, axis=-1, keepdims=True)
                acc = alpha * acc + _dot(p.astype(BF16), v_ref[a, rows, :])
                out.append((m_new, l, acc))
            return tuple(out)

        one = (jnp.full((tq, 1), -1e30, F32), jnp.zeros((tq, 1), F32), jnp.zeros((tq, DN), F32))
        done = step(qi, lax.fori_loop(0, qi, step, (one,) * hp), diagonal=True)
        for a, (m, l, acc) in enumerate(done):
            o_ref[:, a * DN:(a + 1) * DN] = (acc / l).astype(BF16)
            lse_ref[a] = m + jnp.log(l)

    return pl.pallas_call(
        body, name="flash_fwd", grid=(nb, HEADS // hp, nq),
        in_specs=[pl.BlockSpec((hp, tq, 2 * DN), lambda b, h, i: (h, b * nq + i, 0)),
                  pl.BlockSpec((hp, seq, 2 * DN), lambda b, h, i: (h, b, 0)),
                  pl.BlockSpec((hp, seq, DN), lambda b, h, i: (h, b, 0))],
        out_specs=[pl.BlockSpec((tq, hp * DN), lambda b, h, i: (b * nq + i, h)),
                   pl.BlockSpec((hp, tq, 1), lambda b, h, i: (h, b * nq + i, 0))],
        out_shape=[_sds((t, HEADS * DN), BF16), _sds((HEADS, t, 1), F32)],
        compiler_params=_cp("parallel", "parallel", "arbitrary"),
    )(q, k, v)


def _gmlp_in(hn, win_ref):
    pre = [_dot(hn, win_ref[c]) for c in range(N_CHIPS)]
    return jnp.concatenate(pre[:2], axis=1), jnp.concatenate(pre[2:], axis=1)


def gmlp_fwd(h, g1, allw, lay, lng, lnb, wm, bfull):
    t = h.shape[0]

    def body(h_ref, g1_ref, win_ref, lng_ref, lnb_ref, wm_ref, b_ref, y_ref, pre_ref):
        yn, _, _ = _rms(h_ref[...], g1_ref[...], D)
        pre_u, pre_v = _gmlp_in(yn.astype(BF16), win_ref)
        pre_ref[:, :GH] = pre_u.astype(BF16)
        pre_ref[:, GH:] = pre_v.astype(BF16)
        u = _gelu(pre_u)
        v = _gelu(pre_v)
        xc = v - jnp.mean(v, axis=-1, keepdims=True)
        rs = lax.rsqrt(jnp.mean(xc * xc, axis=-1, keepdims=True) + EPS)
        vnb = (xc * rs * lng_ref[...] + lnb_ref[...]).astype(BF16)
        for ch in range(TM // GC):
            rows = slice(ch * GC, (ch + 1) * GC)
            for g in range(GG):
                cols = slice(g * GD, (g + 1) * GD)
                sv = _dot(wm_ref[g], vnb[rows, cols]) + b_ref[:, cols]
                y_ref[rows, cols] = (u[rows, cols] * sv).astype(BF16)

    return pl.pallas_call(
        body, name="gmlp_fwd", grid=(t // TM,),
        in_specs=[_row(TM, D), _const((1, D)), _wblk(D, lay["in"]), _const((1, GH)), _const((1, GH)),
                  _const((GG, GC, GC)), _const((GC, GH))],
        out_specs=[_row(TM, GH), _row(TM, 2 * GH)],
        out_shape=[_sds((t, GH), BF16), _sds((t, 2 * GH), BF16)],
        compiler_params=_cp("parallel"),
    )(h, g1, allw, lng, lnb, wm, bfull)


def loss_head(h, tgt):
    t = h.shape[0]

    def body(h_ref, t_ref, dh_ref, loss_ref):
        @pl.when(pl.program_id(0) == 0)
        def _():
            loss_ref[...] = jnp.zeros_like(loss_ref)

        e = h_ref[...] - t_ref[...]
        dh_ref[...] = e * (1.0 / D)
        part = jnp.sum(jnp.sum(e * e, axis=-1, keepdims=True), axis=0, keepdims=True) * (0.5 / D)
        loss_ref[...] += jnp.broadcast_to(part, loss_ref.shape)

    return pl.pallas_call(
        body, name="loss_head", grid=(t // TMB,),
        in_specs=[_row(TMB, D), _row(TMB, D)],
        out_specs=[_row(TMB, D), _const((8, LANES))],
        out_shape=[_sds((t, D), F32), _sds((8, LANES), F32)],
        compiler_params=_cp("arbitrary"),
    )(h, tgt)


def _zero_at_first_step(*refs):
    @pl.when(pl.program_id(0) == 0)
    def _():
        for r in refs:
            r[...] = jnp.zeros_like(r)


def ple_bwd(dh3, h2, p, g3, allw, lay, wp):
    t = h2.shape[0]

    def body(dh_ref, h_ref, p_ref, g_ref, wg_ref, wp_ref, dh2_ref, dh2b_ref, dgt_ref, dpp_ref, dg_ref):
        _zero_at_first_step(dg_ref)
        dh3v = dh_ref[...]
        x = h_ref[...]
        g = g_ref[...]
        wg = _rows_joined(wg_ref)
        yn, xhat, r = _rms(x, g, D)
        gt = _dot(yn.astype(BF16), wg)
        pp = _dot(p_ref[...].astype(BF16), wp_ref[...])
        sg = _sigmoid(gt)
        dgt = (dh3v * pp * sg * (1.0 - sg)).astype(BF16)
        dgt_ref[...] = dgt
        dpp_ref[...] = (dh3v * sg).astype(BF16)
        dhn = _dot_nt(dgt, wg)
        _acc_rows(dg_ref, dhn * xhat)
        dh2 = dh3v + _rms_bwd(dhn, g, xhat, r, D)
        dh2_ref[...] = dh2
        dh2b_ref[...] = dh2.astype(BF16)

    return pl.pallas_call(
        body, name="ple_bwd", grid=(t // TMB,),
        in_specs=[_row(TMB, D), _row(TMB, D), _row(TMB, PLE), _const((1, D)), _wblk(D // N_CHIPS, lay["gate"]),
                  _const((PLE, D))],
        out_specs=[_row(TMB, D), _row(TMB, D), _row(TMB, D), _row(TMB, D), _const((8, D))],
        out_shape=[_sds((t, D), F32), _sds((t, D), BF16), _sds((t, D), BF16), _sds((t, D), BF16), _sds((8, D), F32)],
        compiler_params=_cp("arbitrary"),
    )(dh3, h2, p, g3, allw, wp)


def ffn_bwd(dh2, dh2b, h1, r, g2, allw, lay):
    t = h1.shape[0]

    def body(dh_ref, dhb_ref, h_ref, r_ref, g_ref, wu_ref, wd_ref, dh1_ref, dh1b_ref, du_ref, a_ref, dg_ref):
        _zero_at_first_step(dg_ref)
        dhb = dhb_ref[...]
        g = g_ref[...]
        _, xhat, rr = _rms(h_ref[...], g, D)
        dhn = jnp.zeros((TM, D), F32)
        for c in range(N_CHIPS):
            cs = slice(c * D, (c + 1) * D)
            rc = r_ref[:, cs].astype(F32)
            a_ref[:, cs] = (rc * rc).astype(BF16)
            da = _dot_nt(dhb, wd_ref[c])
            du = (da * (2.0 * rc)).astype(BF16)
            du_ref[:, cs] = du
            dhn = dhn + _dot_nt(du, wu_ref[c])
        _acc_rows(dg_ref, dhn * xhat)
        dh1 = dh_ref[...] + _rms_bwd(dhn, g, xhat, rr, D)
        dh1_ref[...] = dh1
        dh1b_ref[...] = dh1.astype(BF16)

    return pl.pallas_call(
        body, name="ffn_bwd", grid=(t // TM,),
        in_specs=[_row(TM, D), _row(TM, D), _row(TM, D), _row(TM, DFF), _const((1, D)), _wblk(D, lay["up"]),
                  _wblk(D, lay["down"])],
        out_specs=[_row(TM, D), _row(TM, D), _row(TM, DFF), _row(TM, DFF), _const((8, D))],
        out_shape=[_sds((t, D), F32), _sds((t, D), BF16), _sds((t, DFF), BF16), _sds((t, DFF), BF16),
                   _sds((8, D), F32)],
        compiler_params=_cp("arbitrary"),
    )(dh2, dh2b, h1, r, g2, allw, allw)


def linear_nt(a, allw, rows, row0):
    t = a.shape[0]
    k = N_CHIPS * rows

    def body(a_ref, w_ref, o_ref):
        o_ref[...] = _dot_nt(a_ref[...], _rows_joined(w_ref)).astype(BF16)

    return pl.pallas_call(
        body, name="linear_nt", grid=(t // TMB,),
        in_specs=[_row(TMB, D), _wblk(rows, row0)],
        out_specs=_row(TMB, k),
        out_shape=_sds((t, k), BF16),
        compiler_params=_cp("parallel"),
    )(a, allw)


def flash_bwd(q, k, v, o, do, lse, seq):
    t = q.shape[1]
    nb = t // seq
    nq = seq // TQ
    hp = BWD_HEADS

    def body(q_ref, k_ref, v_ref, o_ref, do_ref, lse_ref, dq_ref, dk_ref, dv_ref):
        kj = pl.program_id(2)

        @pl.when(kj == 0)
        def _():
            dq_ref[...] = jnp.zeros_like(dq_ref)

        def step(i, carry, diagonal=False):
            rows = pl.ds(pl.multiple_of(i * TQ, TQ), TQ)
            out = []
            for a in range(hp):
                dk, dv = carry[a]
                kv = k_ref[a]
                qv = q_ref[a, rows, :]
                dov = do_ref[rows, a * DN:(a + 1) * DN]
                ov = o_ref[rows, a * DN:(a + 1) * DN]
                delta = jnp.sum(dov.astype(F32) * ov.astype(F32), axis=-1, keepdims=True)
                s = _dot_nt(qv, kv)
                if diagonal:
                    s = jnp.where(_diagonal_mask(), s, -1e30)
                p = jnp.exp(s - lse_ref[a, rows, :])
                dp = _dot_nt(dov, v_ref[a])
                ds = (p * (dp - delta)).astype(BF16)
                dv = dv + _dot_tn(p.astype(BF16), dov)
                dk = dk + _dot_tn(ds, qv)
                dq_ref[a, rows, :] += _dot(ds, kv)
                out.append((dk, dv))
            return tuple(out)

        one = (jnp.zeros((TQ, 2 * DN), F32), jnp.zeros((TQ, DN), F32))
        done = lax.fori_loop(kj + 1, nq, step, step(kj, (one,) * hp, diagonal=True))
        for a, (dk, dv) in enumerate(done):
            dk_ref[a] = dk
            dv_ref[a] = dv

    return pl.pallas_call(
        body, name="flash_bwd", grid=(nb, HEADS // hp, nq),
        in_specs=[pl.BlockSpec((hp, seq, 2 * DN), lambda b, h, j: (h, b, 0)),
                  pl.BlockSpec((hp, TQ, 2 * DN), lambda b, h, j: (h, b * nq + j, 0)),
                  pl.BlockSpec((hp, TQ, DN), lambda b, h, j: (h, b * nq + j, 0)),
                  pl.BlockSpec((seq, hp * DN), lambda b, h, j: (b, h)),
                  pl.BlockSpec((seq, hp * DN), lambda b, h, j: (b, h)),
                  pl.BlockSpec((hp, seq, 1), lambda b, h, j: (h, b, 0))],
        out_specs=[pl.BlockSpec((hp, seq, 2 * DN), lambda b, h, j: (h, b, 0)),
                   pl.BlockSpec((hp, TQ, 2 * DN), lambda b, h, j: (h, b * nq + j, 0)),
                   pl.BlockSpec((hp, TQ, DN), lambda b, h, j: (h, b * nq + j, 0))],
        out_shape=[_sds((HEADS, t, 2 * DN), F32), _sds((HEADS, t, 2 * DN), F32), _sds((HEADS, t, DN), F32)],
        compiler_params=_cp("parallel", "parallel", "arbitrary"),
    )(q, k, v, o, do, lse)


def mla_pre_bwd(dq, dk, dv, dh1, h, g1, wdn, gq, gkv, wuq, wukv, gqn, gqr, gkn, gkr, cos, sin):
    t = h.shape[0]

    def body(dq_ref, dk_ref, dv_ref, dh1_ref, h_ref, g1_ref, wdn_ref, gq_ref, gkv_ref, wuq_ref, wukv_ref,
             gqn_ref, gqr_ref, gkn_ref, gkr_ref, c_ref, s_ref,
             dh_ref, hn_ref, cq_ref, ckv_ref, dqp_ref, dkvp_ref, dlat_ref,
             dg1_ref, dgq_ref, dgkv_ref, dgqn_ref, dgqr_ref, dgkn_ref, dgkr_ref):
        _zero_at_first_step(dg1_ref, dgq_ref, dgkv_ref, dgqn_ref, dgqr_ref, dgkn_ref, dgkr_ref)
        m = _mla_project(h_ref, g1_ref, wdn_ref, gq_ref, gkv_ref, wuq_ref, wukv_ref)
        hn_ref[...] = m["hn"]
        cq_ref[...] = m["cqb"]
        ckv_ref[...] = m["ckvb"]
        c = c_ref[...]
        s = s_ref[...]
        gqn = gqn_ref[...]
        gqr = gqr_ref[...]
        gkn = gkn_ref[...]
        gkr = gkr_ref[...]

        dkr = dk_ref[0, :, DN:2 * DN]
        for hd in range(1, HEADS):
            dkr = dkr + dk_ref[hd, :, DN:2 * DN]
        dkr = _rope_t(dkr, c, s)
        _, krhat, rkr = _rms(m["kr_raw"], gkr, DR)
        _acc_rows(dgkr_ref, dkr * krhat)
        dkr_raw = _rms_bwd(dkr, gkr, krhat, rkr, DR)

        for hd in range(HEADS):
            ncols = slice(hd * DN, (hd + 1) * DN)
            _, xh, r = _rms(m["qp"][:, ncols], gqn, DN)
            dqn = dq_ref[hd, :, 0:DN] * SM_SCALE
            _acc_rows(dgqn_ref, dqn * xh)
            dqp_ref[:, ncols] = _rms_bwd(dqn, gqn, xh, r, DN).astype(BF16)

            rcols = slice(D + hd * LANES, D + (hd + 1) * LANES)
            _, xh, r = _rms(m["qp"][:, rcols], gqr, DR)
            dqr = _rope_t(dq_ref[hd, :, DN:2 * DN] * SM_SCALE, c, s)
            _acc_rows(dgqr_ref, dqr * xh)
            dqp_ref[:, rcols] = _rms_bwd(dqr, gqr, xh, r, DR).astype(BF16)

            kcols = slice(hd * 2 * DN, hd * 2 * DN + DN)
            _, xh, r = _rms(m["kvp"][:, kcols], gkn, DN)
            dkn = dk_ref[hd, :, 0:DN]
            _acc_rows(dgkn_ref, dkn * xh)
            dkvp_ref[:, kcols] = _rms_bwd(dkn, gkn, xh, r, DN).astype(BF16)
            dkvp_ref[:, hd * 2 * DN + DN:(hd + 1) * 2 * DN] = dv_ref[hd].astype(BF16)

        dcq = _dot_nt(dqp_ref[...], wuq_ref[...])
        _acc_rows(dgq_ref, dcq * m["cqhat"])
        dlat_q = _rms_bwd(dcq, gq_ref[...], m["cqhat"], m["rq"], QL)
        dckv = _dot_nt(dkvp_ref[...], wukv_ref[...])
        _acc_rows(dgkv_ref, dckv * m["ckvhat"])
        dlat_kv = _rms_bwd(dckv, gkv_ref[...], m["ckvhat"], m["rkv"], KVL)
        dlat = jnp.concatenate([dlat_q, dlat_kv, dkr_raw], axis=1).astype(BF16)
        dlat_ref[...] = dlat
        dhn = _dot_nt(dlat, wdn_ref[...])
        _acc_rows(dg1_ref, dhn * m["xhat"])
        dh_ref[...] = dh1_ref[...] + _rms_bwd(dhn, g1_ref[...], m["xhat"], m["rx"], D)

    hb = lambda w: pl.BlockSpec((HEADS, TM, w), lambda i: (0, i, 0))
    return pl.pallas_call(
        body, name="mla_pre_bwd", grid=(t // TM,),
        in_specs=[hb(2 * DN), hb(2 * DN), hb(DN), _row(TM, D), _row(TM, D), _const((1, D)), _const((D, LATP)),
                  _const((1, QL)), _const((1, KVL)), _const((QL, 2 * D)), _const((KVL, 2 * D)),
                  _const((1, LANES)), _const((1, LANES)), _const((1, LANES)), _const((1, LANES)),
                  _row(TM, LANES), _row(TM, LANES)],
        out_specs=[_row(TM, D), _row(TM, D), _row(TM, QL), _row(TM, KVL), _row(TM, 2 * D), _row(TM, 2 * D),
                   _row(TM, LATP), _const((8, D)), _const((8, QL)), _const((8, KVL)), _const((8, LANES)),
                   _const((8, LANES)), _const((8, LANES)), _const((8, LANES))],
        out_shape=[_sds((t, D), F32), _sds((t, D), BF16), _sds((t, QL), BF16), _sds((t, KVL), BF16),
                   _sds((t, 2 * D), BF16), _sds((t, 2 * D), BF16), _sds((t, LATP), BF16),
                   _sds((8, D), F32), _sds((8, QL), F32), _sds((8, KVL), F32), _sds((8, LANES), F32),
                   _sds((8, LANES), F32), _sds((8, LANES), F32), _sds((8, LANES), F32)],
        compiler_params=_cp("arbitrary"),
    )(dq, dk, dv, dh1, h, g1, wdn, gq, gkv, wuq, wukv, gqn, gqr, gkn, gkr, cos, sin)


def gmlp_bwd(dh1, dh1b, h, pre, g1, allw, lay, lng, lnb, wm, wmt, bfull, tril):
    t = h.shape[0]

    def body(dh1_ref, dh1b_ref, h_ref, pre_ref, g1_ref, win_ref, lng_ref, lnb_ref, wm_ref, wmt_ref, b_ref,
             wout_ref, tril_ref, dh_ref, hn_ref, dpre_ref, dws_ref, dbs_ref, dlng_ref, dlnb_ref, dg1_ref,
             dvn_s):
        _zero_at_first_step(dws_ref, dbs_ref, dlng_ref, dlnb_ref, dg1_ref)
        g1 = g1_ref[...]
        yn, xhat, rx = _rms(h_ref[...], g1, D)
        hn_ref[...] = yn.astype(BF16)
        dy = _dot_nt(dh1b_ref[...], _rows_joined(wout_ref))
        pre_u = pre_ref[:, :GH].astype(F32)
        pre_v = pre_ref[:, GH:].astype(F32)
        u, gg_u = _gelu_and_grad(pre_u)
        v, gg_v = _gelu_and_grad(pre_v)
        xc = v - jnp.mean(v, axis=-1, keepdims=True)
        rs = lax.rsqrt(jnp.mean(xc * xc, axis=-1, keepdims=True) + EPS)
        vhat = xc * rs
        lng = lng_ref[...]
        vnb = (vhat * lng + lnb_ref[...]).astype(BF16)
        dsv = dy * u
        dsvb = dsv.astype(BF16)
        tril_m = tril_ref[...]
        for ch in range(TM // GC):
            rows = slice(ch * GC, (ch + 1) * GC)
            dbs_ref[...] += dsv[rows, :]
            for g in range(GG):
                cols = slice(g * GD, (g + 1) * GD)
                sv = _dot(wm_ref[g], vnb[rows, cols]) + b_ref[:, cols]
                dpre_ref[rows, cols] = (dy[rows, cols] * sv * gg_u[rows, cols]).astype(BF16)
                dvn_s[rows, cols] = _dot(wmt_ref[g], dsvb[rows, cols])
                dws_ref[g] += _dot_nt(dsvb[rows, cols], vnb[rows, cols]) * tril_m
        dvn = dvn_s[...]
        _acc_rows(dlng_ref, dvn * vhat)
        _acc_rows(dlnb_ref, dvn)
        dvhat = dvn * lng
        dv = rs * (dvhat - jnp.mean(dvhat, axis=-1, keepdims=True)
                   - vhat * jnp.mean(dvhat * vhat, axis=-1, keepdims=True))
        dpre_v = (dv * gg_v).astype(BF16)
        dpre_ref[:, GH:] = dpre_v
        dhn = _dot_nt(dpre_ref[:, 0:D], win_ref[0])
        for c in range(1, N_CHIPS):
            dhn = dhn + _dot_nt(dpre_ref[:, c * D:(c + 1) * D], win_ref[c])
        _acc_rows(dg1_ref, dhn * xhat)
        dh_ref[...] = dh1_ref[...] + _rms_bwd(dhn, g1, xhat, rx, D)

    return pl.pallas_call(
        body, name="gmlp_bwd", grid=(t // TM,),
        in_specs=[_row(TM, D), _row(TM, D), _row(TM, D), _row(TM, 2 * GH), _const((1, D)), _wblk(D, lay["in"]),
                  _const((1, GH)), _const((1, GH)), _const((GG, GC, GC)), _const((GG, GC, GC)), _const((GC, GH)),
                  _wblk(GH // N_CHIPS, lay["out"]), _const((GC, GC))],
        out_specs=[_row(TM, D), _row(TM, D), _row(TM, 2 * GH), _const((GG, GC, GC)), _const((GC, GH)),
                   _const((8, GH)), _const((8, GH)), _const((8, D))],
        out_shape=[_sds((t, D), F32), _sds((t, D), BF16), _sds((t, 2 * GH), BF16), _sds((GG, GC, GC), F32),
                   _sds((GC, GH), F32), _sds((8, GH), F32), _sds((8, GH), F32), _sds((8, D), F32)],
        scratch_shapes=[pltpu.VMEM((TM, GH), F32)],
        compiler_params=_cp("arbitrary"),
    )(dh1, dh1b, h, pre, g1, allw, lng, lnb, wm, wmt, bfull, allw, tril)


def _token_step(t):
    return 1024 if t % 1024 == 0 else 512


def mm_tn(a, b):
    t, k = a.shape
    n = b.shape[1]
    tk = min(k, 1024)
    tn = min(n, 1024)
    tt = _token_step(t)

    def body(a_ref, b_ref, o_ref):
        @pl.when(pl.program_id(2) == 0)
        def _():
            o_ref[...] = jnp.zeros_like(o_ref)

        o_ref[...] += _dot_tn(a_ref[...].astype(BF16), b_ref[...].astype(BF16))

    return pl.pallas_call(
        body, name="mm_tn", grid=(k // tk, n // tn, t // tt),
        in_specs=[pl.BlockSpec((tt, tk), lambda i, j, s: (s, i)), pl.BlockSpec((tt, tn), lambda i, j, s: (s, j))],
        out_specs=pl.BlockSpec((tk, tn), lambda i, j, s: (i, j)), out_shape=_sds((k, n), F32),
        compiler_params=_cp("parallel", "parallel", "arbitrary"),
    )(a, b)


def mm_tn_into(buf, a, b, rows, row0, col_sharded):
    t = a.shape[0]
    tt = _token_step(t)
    assert row0 % rows == 0 and a.shape[1] == (rows if col_sharded else N_CHIPS * rows), (rows, row0, a.shape)
    assert b.shape[1] == (N_CHIPS * D if col_sharded else D), b.shape
    grid = (1, N_CHIPS, t // tt) if col_sharded else (N_CHIPS, 1, t // tt)
    fresh = isinstance(buf, int)

    def body(*refs):
        a_ref, b_ref, o_ref = refs[-3:]

        @pl.when(pl.program_id(2) == 0)
        def _():
            o_ref[...] = jnp.zeros_like(o_ref)

        o_ref[...] += _dot_tn(a_ref[...].astype(BF16), b_ref[...].astype(BF16))

    specs = [pl.BlockSpec((tt, rows), lambda i, j, s: (s, i)), pl.BlockSpec((tt, D), lambda i, j, s: (s, j))]
    return pl.pallas_call(
        body, name="mm_tn_into", grid=grid,
        in_specs=specs if fresh else [_ANY] + specs,
        out_specs=pl.BlockSpec((None, rows, D), lambda i, j, s: (i + j, row0 // rows, 0)),
        out_shape=_sds((N_CHIPS, buf, D) if fresh else buf.shape, F32),
        input_output_aliases={} if fresh else {0: 0},
        compiler_params=_cp("parallel", "parallel", "arbitrary"),
    )(*((a, b) if fresh else (buf, a, b)))


def adamw(w, g, m, v):
    rows, cols = w.shape
    tr = rows if rows <= 512 else next(r for r in (512, 384, 256, 128) if rows % r == 0)
    c1 = 1.0 - ADAM_B1 ** ADAM_STEP
    c2 = 1.0 - ADAM_B2 ** ADAM_STEP

    def body(w_ref, g_ref, m_ref, v_ref, d_ref, mo_ref, vo_ref):
        gv = g_ref[...]
        mn = ADAM_B1 * m_ref[...] + (1.0 - ADAM_B1) * gv
        vn = ADAM_B2 * v_ref[...] + (1.0 - ADAM_B2) * (gv * gv)
        mo_ref[...] = mn
        vo_ref[...] = vn
        d_ref[...] = -ADAM_LR * ((mn / c1) / (jnp.sqrt(vn / c2) + ADAM_EPS) + ADAM_WD * w_ref[...])

    spec = pl.BlockSpec((tr, cols), lambda i: (i, 0))
    return pl.pallas_call(
        body, name="adamw", grid=(rows // tr,),
        in_specs=[spec] * 4, out_specs=[spec] * 3, out_shape=[_sds((rows, cols), F32)] * 3,
        compiler_params=_cp("parallel"),
    )(w, g, m, v)


def _place():
    return lax.axis_index("x"), lax.axis_index("y"), lax.axis_index("c")


def _other_chips(x, y):
    return [(1 - x, y), (x, 1 - y), (1 - x, 1 - y)]


_ANY = pl.BlockSpec(memory_space=pl.ANY)


_HBM = pl.BlockSpec(memory_space=pltpu.HBM)
_SEM = pl.BlockSpec(memory_space=pltpu.SEMAPHORE)
_EFFECT = pltpu.SideEffectType.DATAFLOW_SIDE_EFFECTING
N_ICI = 3


def _exchange_start(name, src, land, copies, n):
    def body(src_ref, land_ref, *outs):
        sems, token = outs[:2 * n], outs[-1]
        for j, (s, d, to) in enumerate(copies(src_ref, land_ref, _place())):
            pltpu.make_async_remote_copy(src_ref=s, dst_ref=d, send_sem=sems[j], recv_sem=sems[n + j],
                                         device_id=to, device_id_type=MESH).start()
        token[...] = jnp.zeros_like(token)

    sem = pltpu.SemaphoreType.DMA(())
    outs = pl.pallas_call(
        body, name=name,
        out_shape=(sem,) * (2 * n) + (pltpu.HBM(src.shape, src.dtype), pltpu.HBM(land.shape, land.dtype),
                                      _sds((8, LANES), F32)),
        in_specs=(_HBM, _HBM),
        out_specs=(_SEM,) * (2 * n) + (_HBM, _HBM, pl.BlockSpec(memory_space=pltpu.VMEM)),
        input_output_aliases={0: 2 * n, 1: 2 * n + 1},
        compiler_params=pltpu.CompilerParams(has_side_effects=_EFFECT),
    )(pltpu.with_memory_space_constraint(src, pltpu.HBM), pltpu.with_memory_space_constraint(land, pltpu.HBM))
    return outs[:2 * n], outs[2 * n], outs[2 * n + 1], outs[-1]


def _exchange_wait(name, sems, src, land, after, arrivals):
    n = len(sems) // 2

    def body(src_ref, land_ref, *rest):
        sems = rest[:2 * n]
        for j, (s, d) in enumerate(arrivals(src_ref, land_ref, _place())):
            cp = pltpu.make_async_remote_copy(src_ref=s, dst_ref=d, send_sem=sems[j], recv_sem=sems[n + j],
                                              device_id=_place(), device_id_type=MESH)
            cp.wait_send()
            cp.wait_recv()

    return pl.pallas_call(
        body, name=name, out_shape=(pltpu.HBM(src.shape, src.dtype), pltpu.HBM(land.shape, land.dtype)),
        in_specs=(_HBM, _HBM) + (_SEM,) * (2 * n) + (_ANY,), out_specs=(_HBM, _HBM),
        input_output_aliases={0: 0, 1: 1},
        compiler_params=pltpu.CompilerParams(has_side_effects=_EFFECT),
    )(src, land, *sems, after)


def _halves(c, hh):
    return pl.ds(pl.multiple_of(c * hh, 16), hh), pl.ds(pl.multiple_of((1 - c) * hh, 16), hh)


def gather_start(land, tag):
    _, rr, _ = land.shape
    assert rr % 32 == 0, rr

    def copies(_, land_ref, place):
        x, y, c = place
        mine = land_ref.at[2 * x + y, _halves(c, rr // 2)[0]]
        return [(mine, mine, (cx, cy, c)) for cx, cy in _other_chips(x, y)]

    return _exchange_start(f"gather_start_{tag}", jnp.zeros((8, LANES), F32), land, copies, N_ICI)


def gather_wait(sems, src, land, after, tag):
    def arrivals(_, land_ref, place):
        x, y, c = place
        half = _halves(c, land.shape[1] // 2)[0]
        return [(land_ref.at[2 * x + y, half], land_ref.at[2 * cx + cy, half]) for cx, cy in _other_chips(x, y)]

    return _exchange_wait(f"gather_wait_{tag}", sems, src, land, after, arrivals)


def pass_start(land, tag):
    def copies(_, land_ref, place):
        x, y, c = place
        half = _halves(c, land.shape[1] // 2)[0]
        return [(land_ref.at[2 * cx + cy, half], land_ref.at[2 * cx + cy, half], (x, y, 1 - c))
                for cx, cy in _other_chips(x, y)]

    return _exchange_start(f"pass_start_{tag}", jnp.zeros((8, LANES), F32), land, copies, N_ICI)


def pass_wait(sems, src, land, after, tag):
    def arrivals(_, land_ref, place):
        x, y, c = place
        mine, other = _halves(c, land.shape[1] // 2)
        return [(land_ref.at[2 * cx + cy, mine], land_ref.at[2 * cx + cy, other]) for cx, cy in _other_chips(x, y)]

    return _exchange_wait(f"pass_wait_{tag}", sems, src, land, after, arrivals)


def swap_start(g, tag):
    _, rr, cc = g.shape

    def copies(g_ref, got_ref, place):
        x, y, c = place
        other = _halves(c, rr // 2)[1]
        return [(g_ref.at[k, other], got_ref.at[k], (x, y, 1 - c)) for k in range(N_CHIPS)]

    return _exchange_start(f"swap_start_{tag}", g, lax.empty((N_CHIPS, rr // 2, cc), g.dtype), copies, N_CHIPS)


def swap_wait(sems, g, got, after, tag):
    def arrivals(g_ref, got_ref, place):
        other = _halves(place[2], g.shape[1] // 2)[1]
        return [(g_ref.at[k, other], got_ref.at[k]) for k in range(N_CHIPS)]

    return _exchange_wait(f"swap_wait_{tag}", sems, g, got, after, arrivals)


def chip_sum(place, g32, got):
    _, rr, cc = g32.shape
    hh = rr // 2
    tr = SUM_ROWS
    assert rr % 2 == 0 and hh % tr == 0, (rr, tr)
    nb = hh // tr

    def body(place_ref, g_ref, got_ref, own_ref, all_ref):
        s = g_ref[...] + got_ref[...].astype(F32)
        all_ref[...] = s.astype(BF16)
        own_ref[...] = g_ref[place_ref[1]] + got_ref[place_ref[1]].astype(F32)

    return pl.pallas_call(
        body, name="chip_sum",
        grid_spec=pltpu.PrefetchScalarGridSpec(
            num_scalar_prefetch=1, grid=(nb,),
            in_specs=[pl.BlockSpec((N_CHIPS, tr, cc), lambda i, pr: (0, pr[0] * nb + i, 0)),
                      pl.BlockSpec((N_CHIPS, tr, cc), lambda i, pr: (0, i, 0))],
            out_specs=[pl.BlockSpec((tr, cc), lambda i, pr: (i, 0)),
                       pl.BlockSpec((N_CHIPS, tr, cc), lambda i, pr: (0, i, 0))]),
        out_shape=[_sds((hh, cc), F32), _sds((N_CHIPS, hh, cc), BF16)],
        compiler_params=_cp("parallel"),
    )(place, g32, got)


def _scatter_copies(s_ref, land_ref, place):
    x, y, c = place
    return [(s_ref.at[2 * cx + cy], land_ref.at[j], (cx, cy, c)) for j, (cx, cy) in enumerate(_other_chips(x, y))]


def scatter_start(s, tag):
    return _exchange_start(f"scatter_start_{tag}", s, lax.empty((N_ICI,) + s.shape[1:], s.dtype), _scatter_copies, N_ICI)


def scatter_wait(sems, s, land, after, tag):
    return _exchange_wait(f"scatter_wait_{tag}", sems, s, land, after,
                          lambda s_ref, land_ref, place: [(a, b) for a, b, _ in _scatter_copies(s_ref, land_ref, place)])


def final_sum(place, own, got):
    hh, cc = own.shape
    tr = SUM_ROWS
    assert hh % tr == 0, (hh, tr)
    nb = hh // tr

    def body(place_ref, own_ref, got_ref, o_ref):
        del place_ref
        o_ref[...] = ((own_ref[...] + got_ref[0].astype(F32)) + got_ref[1].astype(F32)) + got_ref[2].astype(F32)

    return pl.pallas_call(
        body, name="final_sum",
        grid_spec=pltpu.PrefetchScalarGridSpec(
            num_scalar_prefetch=1, grid=(nb,),
            in_specs=[pl.BlockSpec((tr, cc), lambda i, pr: (i, 0)), pl.BlockSpec((3, tr, cc), lambda i, pr: (0, i, 0))],
            out_specs=pl.BlockSpec((tr, cc), lambda i, pr: (pr[0] * nb + i, 0))),
        out_shape=_sds((2 * hh, cc), F32),
        compiler_params=_cp("parallel"),
    )(place, own, got)


def share_start(f, tag):
    def copies(_, f_ref, place):
        x, y, c = place
        mine = f_ref.at[_halves(c, f.shape[0] // 2)[0]]
        return [(mine, mine, (x, y, 1 - c))]

    return _exchange_start(f"share_start_{tag}", jnp.zeros((8, LANES), F32), f, copies, 1)


def share_wait(sems, src, f, after, tag):
    def arrivals(_, f_ref, place):
        mine, other = _halves(place[2], f.shape[0] // 2)
        return [(f_ref.at[mine], f_ref.at[other])]

    return _exchange_wait(f"share_wait_{tag}", sems, src, f, after, arrivals)


N_DEV = 8


def _peers(place):
    x, y, c = place
    return [((1 - x) if r & 4 else x, (1 - y) if r & 2 else y, (1 - c) if r & 1 else c) for r in range(1, N_DEV)]


def _device_index(place):
    x, y, c = place
    return 4 * x + 2 * y + c


def small_start(land):
    def copies(_, land_ref, place):
        mine = land_ref.at[_device_index(place)]
        return [(mine, mine, to) for to in _peers(place)]

    return _exchange_start("small_start", jnp.zeros((8, LANES), F32), land, copies, N_DEV - 1)


def small_wait(sems, src, land, after):
    def arrivals(_, land_ref, place):
        return [(land_ref.at[_device_index(place)], land_ref.at[_device_index(peer)]) for peer in _peers(place)]

    return _exchange_wait("small_wait", sems, src, land, after, arrivals)


def sum_devices(land):
    _, rr, cc = land.shape
    tr = 56
    assert rr % tr == 0, rr

    def body(l_ref, o_ref):
        acc = l_ref[0]
        for d in range(1, N_DEV):
            acc = acc + l_ref[d]
        o_ref[...] = acc

    return pl.pallas_call(
        body, name="sum_devices", grid=(rr // tr,),
        in_specs=[pl.BlockSpec((N_DEV, tr, cc), lambda i: (0, i, 0))],
        out_specs=pl.BlockSpec((tr, cc), lambda i: (i, 0)), out_shape=_sds((rr, cc), F32),
        compiler_params=_cp("parallel"),
    )(land)


_BIG = ["mla_w_down", "mla_w_uq", "mla_w_ukv", "mla_w_out", "gmlp_w_in", "gmlp_w_out", "ffn_w_up", "ffn_w_down",
        "ple_w_gate", "ple_w_proj"]
_SMALL = ["norm_mix", "norm_ffn", "norm_ple", "mla_q_lora_g", "mla_kv_lora_g", "mla_q_nope_g", "mla_q_rope_g",
          "mla_k_nope_g", "mla_k_rope_g", "gmlp_ln_g", "gmlp_ln_b", "gmlp_w_s", "gmlp_b_s"]

_LAY_MLA = dict(up=0, down=1024, out=2048, gate=2304, wdn=2560, wuq=2736, wukv=2880, proj=3008, rows=3072)
_LAY_GMLP = {"up": 0, "down": 1024, "in": 2048, "out": 3072, "gate": 3584, "proj": 3840, "ln": 3904, "rows": 4096}


def _layer_parts(i):
    j = i // 2
    if i % 2 == 0:
        lay = _LAY_MLA
        return lay, [("ffn_w_up", i, lay["up"]), ("ffn_w_down", i, lay["down"]), ("mla_w_out", j, lay["out"]),
                     ("ple_w_gate", i, lay["gate"]), ("mla_w_down", j, lay["wdn"]), ("mla_w_uq", j, lay["wuq"]),
                     ("mla_w_ukv", j, lay["wukv"]), ("ple_w_proj", i, lay["proj"])]
    lay = _LAY_GMLP
    return lay, [("ffn_w_up", i, lay["up"]), ("ffn_w_down", i, lay["down"]), ("gmlp_w_in", j, lay["in"]),
                 ("gmlp_w_out", j, lay["out"]), ("ple_w_gate", i, lay["gate"]), ("ple_w_proj", i, lay["proj"])]


def _pack_rows(parts, dtype, pad_to=None, slot=False):
    size = sum(p.size for p in parts)
    tail = [] if pad_to is None or pad_to * D == size else [jnp.zeros((pad_to * D - size,), dtype)]
    shape = (1, -1, D) if slot else (-1, D)
    if all(p.size % D == 0 for p in parts + tail):
        return jnp.concatenate([p.astype(dtype).reshape(shape) for p in parts + tail], axis=len(shape) - 2)
    return jnp.concatenate([p.astype(dtype).reshape(-1) for p in parts + tail]).reshape(shape)


def _odd(allw, row0, a, b):
    return allw[:, row0:row0 + a * b // D].reshape(N_CHIPS, a, b)


def _cols_joined(s):
    return jnp.transpose(s, (1, 0, 2)).reshape(s.shape[1], N_CHIPS * s.shape[2])


def _col_shards(full):
    a, bb = full.shape
    return jnp.transpose(full.reshape(a, N_CHIPS, bb // N_CHIPS), (1, 0, 2)).reshape(N_CHIPS, -1, D)


def _pad_lanes(g):
    return jnp.pad(g, ((0, 0), (0, LANES - g.shape[1])))


def _split_uq(wuq):
    l = wuq.shape[0]
    w = wuq.reshape(l, QL, HEADS, DN + DR)
    nope = w[..., :DN].reshape(l, QL, HEADS * DN)
    rope = jnp.pad(w[..., DN:], ((0, 0), (0, 0), (0, 0), (0, LANES - DR))).reshape(l, QL, HEADS * LANES)
    return jnp.concatenate([nope, rope], axis=-1)


def _merge_uq(d):
    nope = d[:, :HEADS * DN].reshape(QL, HEADS, DN)
    rope = d[:, HEADS * DN:].reshape(QL, HEADS, LANES)[..., :DR]
    return jnp.concatenate([nope, rope], axis=-1).reshape(QL, HEADS * (DN + DR))


def _rope_tables(positions):
    inv_freq = ROPE_BASE ** (-(jnp.arange(0, DR, 2, dtype=F32) / DR))
    ang = positions.reshape(-1).astype(F32)[:, None] * inv_freq
    z = jnp.zeros((ang.shape[0], LANES - DR), F32)
    return (jnp.concatenate([jnp.cos(ang), jnp.cos(ang), z], axis=1),
            jnp.concatenate([jnp.sin(ang), jnp.sin(ang), z], axis=1))


def kernel(x, p, positions, norm_mix, norm_ffn, norm_ple, mla_w_down, mla_q_lora_g, mla_kv_lora_g, mla_w_uq, mla_w_ukv, mla_q_nope_g, mla_q_rope_g, mla_k_nope_g, mla_k_rope_g, mla_w_out, gmlp_w_in, gmlp_ln_g, gmlp_ln_b, gmlp_w_s, gmlp_b_s, gmlp_w_out, ffn_w_up, ffn_w_down, ple_w_gate, ple_w_proj, loss_target, m_norm_mix, m_norm_ffn, m_norm_ple, m_mla_w_down, m_mla_q_lora_g, m_mla_kv_lora_g, m_mla_w_uq, m_mla_w_ukv, m_mla_q_nope_g, m_mla_q_rope_g, m_mla_k_nope_g, m_mla_k_rope_g, m_mla_w_out, m_gmlp_w_in, m_gmlp_ln_g, m_gmlp_ln_b, m_gmlp_w_s, m_gmlp_b_s, m_gmlp_w_out, m_ffn_w_up, m_ffn_w_down, m_ple_w_gate, m_ple_w_proj, v_norm_mix, v_norm_ffn, v_norm_ple, v_mla_w_down, v_mla_q_lora_g, v_mla_kv_lora_g, v_mla_w_uq, v_mla_w_ukv, v_mla_q_nope_g, v_mla_q_rope_g, v_mla_k_nope_g, v_mla_k_rope_g, v_mla_w_out, v_gmlp_w_in, v_gmlp_ln_g, v_gmlp_ln_b, v_gmlp_w_s, v_gmlp_b_s, v_gmlp_w_out, v_ffn_w_up, v_ffn_w_down, v_ple_w_gate, v_ple_w_proj):
    args = dict(locals())
    weights = {n: args[n] for n in _BIG + _SMALL}
    depth = norm_mix.shape[0]
    nb, seq, _ = x.shape
    t = nb * seq
    assert seq % TQ == 0 and seq % TM == 0 and t % 512 == 0, (nb, seq)
    cx = lax.axis_index("x")
    cy = lax.axis_index("y")
    cc = lax.axis_index("c")
    chip = 2 * cx + cy

    gathers = []
    token = None
    for i in range(depth):
        lay, parts = _layer_parts(i)
        rows = [weights[n][l] for n, l, _ in parts]
        if token is not None:
            rows[0] = rows[0] + token[0, 0]
        if i % 2 == 1:
            ln = jnp.stack([gmlp_ln_g[i // 2], gmlp_ln_b[i // 2]]).astype(F32)
            bits = lax.bitcast_convert_type(ln, BF16).reshape(-1)
            rows.append(jnp.pad(bits, (0, 16 * D - bits.size)).reshape(16, D))
        mine = _pack_rows(rows, BF16, pad_to=lay["rows"], slot=True)
        land = lax.dynamic_update_slice(lax.empty((N_CHIPS, lay["rows"], D), BF16), mine, (chip, 0, 0))
        sems, src, land, token = gather_start(land, i)
        gathers.append((sems, src, land))
    allw = [None] * depth

    tril = jnp.tril(jnp.ones((GC, GC), F32))
    wm = (gmlp_w_s * tril).astype(BF16)
    wmt = jnp.swapaxes(wm, -1, -2)
    bfull = jnp.repeat(jnp.swapaxes(gmlp_b_s, -1, -2), GD, axis=-1)
    cos, sin = _rope_tables(positions)
    row = lambda g: g.reshape(1, -1)
    gqr = _pad_lanes(mla_q_rope_g)
    gkr = _pad_lanes(mla_k_rope_g)

    h = x.reshape(t, D)
    pt = p.reshape(depth, t, PLE)
    saved = []

    def arrive(i, after):
        sems, src, land = gathers[i]
        _, land = gather_wait(sems, src, land, after, i)
        return pass_start(land, i)

    passing = arrive(0, token)
    for i in range(depth):
        j = i // 2
        lay, _ = _layer_parts(i)
        sems, src, land, token = passing
        _, aw = pass_wait(sems, src, land, token if i == 0 else h, i)
        allw[i] = aw
        s = dict(h=h)
        if i % 2 == 0:
            wdn = jnp.pad(_odd(aw, lay["wdn"], D // N_CHIPS, LAT).reshape(D, LAT), ((0, 0), (0, LATP - LAT)))
            wuq = _split_uq(_cols_joined(_odd(aw, lay["wuq"], QL, 384))[None])[0]
            wukv = _cols_joined(_odd(aw, lay["wukv"], KVL, 512))
            mla_args = (row(norm_mix[i]), wdn, row(mla_q_lora_g[j]), row(mla_kv_lora_g[j]), wuq, wukv,
                        row(mla_q_nope_g[j]), gqr[j:j + 1], row(mla_k_nope_g[j]), gkr[j:j + 1], cos, sin)
            q, k, v = mla_pre_fwd(h, *mla_args)
            y, lse = flash_fwd(q, k, v, seq)
            s.update(q=q, k=k, v=v, lse=lse, mla_args=mla_args)
        else:
            ln = lax.bitcast_convert_type(aw[:, lay["ln"]:lay["ln"] + 2].reshape(N_CHIPS, 2, GH // N_CHIPS, 2), F32)
            ln = jnp.transpose(ln, (1, 0, 2)).reshape(2, 1, GH)
            y, pre = gmlp_fwd(h, row(norm_mix[i]), aw, lay, ln[0], ln[1], wm[j], bfull[j])
            s.update(pre=pre, ln=ln)
        wp = _cols_joined(_odd(aw, lay["proj"], PLE, 256))
        g2 = row(norm_ffn[i])
        if i + 1 < depth:
            passing = arrive(i + 1, y)
            g2 = g2 + passing[3][0:1, 0:1]
        h1, h2, hn2, r = mixffn_fwd(h, y, aw, lay, g2)
        h, hn3 = ple_fwd(h2, pt[i], row(norm_ple[i]), aw, lay, wp)
        s.update(y=y, wp=wp, h1=h1, h2=h2, hn2=hn2, r=r, hn3=hn3)
        saved.append(s)

    dh, loss_part = loss_head(h, loss_target.reshape(t, D))
    loss = lax.psum(loss_part[0, 0], ("x", "y", "c"))

    gs = {n: [None] * weights[n].shape[0] for n in _SMALL}
    gw = {n: [None] * weights[n].shape[0] for n in _BIG}
    place = jnp.stack([cc, chip]).astype(jnp.int32)
    scatters = []
    swapping = None
    token = None

    def put(b, row0, shards):
        return lax.dynamic_update_slice(b, shards.reshape(N_CHIPS, -1, D), (0, row0, 0))

    def swapped(after):
        ii, sems, g, got = swapping
        g, got = swap_wait(sems, g, got, after, ii)
        own, sums = chip_sum(place, g, got)
        sems, sums, land, tok = scatter_start(sums, ii)
        scatters.append((ii, own, sems, sums, land))
        return tok

    for i in reversed(range(depth)):
        j = i // 2
        lay, parts = _layer_parts(i)
        aw = allw[i]
        s = saved[i]

        g3 = row(norm_ple[i])
        if token is not None:
            g3 = g3 + token[0:1, 0:1]
        dh2, dh2b, dgt, dpp, dg3 = ple_bwd(dh, s["h2"], pt[i], g3, aw, lay, s["wp"])
        gs["norm_ple"][i] = dg3[0]
        buf = mm_tn_into(lay["rows"], s["hn3"], dgt, D // N_CHIPS, lay["gate"], False)
        tail = lay.get("ln", lay["rows"])
        if tail < lay["rows"]:
            buf = put(buf, tail, jnp.zeros((N_CHIPS, lay["rows"] - tail, D), F32))
        buf = put(buf, lay["proj"], _col_shards(mm_tn(pt[i], dpp)))
        dh1, dh1b, du, a, dg2 = ffn_bwd(dh2, dh2b, s["h1"], s["r"], row(norm_ffn[i]), aw, lay)
        gs["norm_ffn"][i] = dg2[0]
        buf = mm_tn_into(buf, a, dh2b, D, lay["down"], False)
        buf = mm_tn_into(buf, s["hn2"], du, D, lay["up"], True)
        buf = mm_tn_into(buf, s["y"], dh1b, s["y"].shape[1] // N_CHIPS, lay["out"], False)
        g1 = row(norm_mix[i])
        if swapping is not None:
            g1 = g1 + swapped(dh1)[0:1, 0:1]
        if i % 2 == 0:
            do = linear_nt(dh1b, aw, D // N_CHIPS, lay["out"])
            dq, dk, dv = flash_bwd(s["q"], s["k"], s["v"], s["y"], do, s["lse"], seq)
            (dh, hn1, cq, ckv, dqp, dkvp, dlat, dg1, dgq, dgkv, dgqn, dgqr, dgkn, dgkr) = mla_pre_bwd(
                dq, dk, dv, dh1, s["h"], g1, *s["mla_args"][1:])
            gs["norm_mix"][i] = dg1[0]
            gs["mla_q_lora_g"][j] = dgq[0]
            gs["mla_kv_lora_g"][j] = dgkv[0]
            gs["mla_q_nope_g"][j] = dgqn[0]
            gs["mla_q_rope_g"][j] = dgqr[0, :DR]
            gs["mla_k_nope_g"][j] = dgkn[0]
            gs["mla_k_rope_g"][j] = dgkr[0, :DR]
            buf = put(buf, lay["wdn"], mm_tn(hn1, dlat)[:, :LAT])
            buf = put(buf, lay["wuq"], _col_shards(_merge_uq(mm_tn(cq, dqp))))
            buf = put(buf, lay["wukv"], _col_shards(mm_tn(ckv, dkvp)))
        else:
            dh, hn1, dpre, dws, dbs, dlng, dlnb, dg1 = gmlp_bwd(
                dh1, dh1b, s["h"], s["pre"], g1, aw, lay, s["ln"][0], s["ln"][1], wm[j], wmt[j], bfull[j], tril)
            gs["norm_mix"][i] = dg1[0]
            gs["gmlp_ln_g"][j] = dlng[0]
            gs["gmlp_ln_b"][j] = dlnb[0]
            gs["gmlp_w_s"][j] = dws
            gs["gmlp_b_s"][j] = jnp.sum(dbs.reshape(GC, GG, GD), axis=-1).T
            buf = mm_tn_into(buf, hn1, dpre, D, lay["in"], True)

        sems, buf, got, token = swap_start(buf, i)
        swapping = (i, sems, buf, got)
    swapped(dh)
    grad_x = dh.reshape(x.shape)

    small_sizes = [weights[n].size if n not in ("gmlp_ln_g", "gmlp_ln_b") else weights[n].shape[0] * GH
                   for n in _SMALL]
    small_rows = -(-sum(small_sizes) // (56 * D)) * 56
    part = _pack_rows([jnp.stack(gs[n]) for n in _SMALL], F32, pad_to=small_rows, slot=True)
    land = lax.dynamic_update_slice(lax.empty((N_DEV, small_rows, D), F32), part, (2 * chip + cc, 0, 0))
    small = small_start(land)

    after = small[3]
    shares = []
    for i, own, sems, sums, land in scatters:
        if i == 0:
            after = tot = sum_devices(small_wait(small[0], small[1], small[2], after)[1]).reshape(-1)
        _, got = scatter_wait(sems, sums, land, after, i)
        sems, src, full, after = share_start(final_sum(place, own, got), i)
        shares.append((i, sems, src, full))
    for i, sems, src, full in shares:
        _, after = share_wait(sems, src, full, after, i)
        for n, l, row0 in _layer_parts(i)[1]:
            gw[n][l] = after[row0:row0 + weights[n][l].size // D].reshape(weights[n].shape[1:])
    grads = {n: jnp.stack(gw[n]) for n in _BIG}

    off = 0
    for n, sz in zip(_SMALL, small_sizes):
        gsum = tot[off:off + sz]
        off += sz
        if n in ("gmlp_ln_g", "gmlp_ln_b"):
            gsum = lax.dynamic_slice_in_dim(gsum.reshape(-1, GH), chip * (GH // N_CHIPS), GH // N_CHIPS, axis=1)
        grads[n] = gsum.reshape(weights[n].shape)

    delta, new_m, new_v = {}, {}, {}
    for n in _BIG:
        w2 = weights[n].reshape(-1, weights[n].shape[-1])
        d, mn, vn = adamw(w2, grads[n].reshape(w2.shape), args["m_" + n].reshape(w2.shape),
                          args["v_" + n].reshape(w2.shape))
        delta[n], new_m[n], new_v[n] = (a.reshape(weights[n].shape) for a in (d, mn, vn))
    own_sizes = [weights[n].size for n in _SMALL]
    own_rows = -(-sum(own_sizes) // (8 * D)) * 8
    packed = [_pack_rows([src[n] for n in _SMALL], F32, pad_to=own_rows)
              for src in (weights, grads, {n: args["m_" + n] for n in _SMALL}, {n: args["v_" + n] for n in _SMALL})]
    outs = adamw(*packed)
    off = 0
    for n, sz in zip(_SMALL, own_sizes):
        for dst, o in zip((delta, new_m, new_v), outs):
            dst[n] = o.reshape(-1)[off:off + sz].reshape(weights[n].shape)
        off += sz

    order = ["norm_mix", "norm_ffn", "norm_ple", "mla_w_down", "mla_q_lora_g", "mla_kv_lora_g", "mla_w_uq",
             "mla_w_ukv", "mla_q_nope_g", "mla_q_rope_g", "mla_k_nope_g", "mla_k_rope_g", "mla_w_out", "gmlp_w_in",
             "gmlp_ln_g", "gmlp_ln_b", "gmlp_w_s", "gmlp_b_s", "gmlp_w_out", "ffn_w_up", "ffn_w_down", "ple_w_gate",
             "ple_w_proj"]
    return (loss, grad_x, *[grads[n] for n in order], *[delta[n] for n in order], *[new_m[n] for n in order],
            *[new_v[n] for n in order])
```

```python
import functools

import jax
import jax.numpy as jnp
from jax import lax
from jax.experimental import pallas as pl
from jax.experimental.pallas import tpu as pltpu

F32 = jnp.float32
BF16 = jnp.bfloat16
MESH = pl.DeviceIdType.MESH

D = 1024
HEADS = 8
DN = 128
DR = 64
QL = 384
KVL = 256
LAT = 704
LATP = 768
DFF = 4096
GH = 2048
GC = 128
GG = 8
GD = 256
PLE = 256
EPS = 1e-6
ROPE_BASE = 10000.0
SM_SCALE = (DN + DR) ** -0.5
N_CHIPS = 4
LANES = 128

ADAM_LR = 0.001
ADAM_B1 = 0.9
ADAM_B2 = 0.999
ADAM_EPS = 1e-08
ADAM_WD = 0.01
ADAM_STEP = 10

TM = 256
TMB = 512
TQ = 512
TQ_FWD = 512
FWD_HEADS = 2
BWD_HEADS = 2
SUM_ROWS = 256
VMEM_LIMIT = 56 * 1024 * 1024


def _cp(*sem):
    return pltpu.CompilerParams(dimension_semantics=sem, vmem_limit_bytes=VMEM_LIMIT)


def _dot(a, b):
    return jnp.dot(a, b, preferred_element_type=F32)


def _dot_nt(a, b):
    return lax.dot_general(a, b, (((1,), (1,)), ((), ())), preferred_element_type=F32)


def _dot_tn(a, b):
    return lax.dot_general(a, b, (((0,), (0,)), ((), ())), preferred_element_type=F32)


def _rms(x, g, n):
    r = lax.rsqrt(jnp.sum(x * x, axis=-1, keepdims=True) * (1.0 / n) + EPS)
    xhat = x * r
    return xhat * g, xhat, r


def _rms_bwd(dy, g, xhat, r, n):
    dxhat = dy * g
    return r * (dxhat - xhat * (jnp.sum(dxhat * xhat, axis=-1, keepdims=True) * (1.0 / n)))


def _rope(x, c, s):
    return x * c + (pltpu.roll(x, 32, 1) - pltpu.roll(x, 96, 1)) * s


def _rope_t(dy, c, s):
    w = dy * s
    return dy * c + pltpu.roll(w, 96, 1) - pltpu.roll(w, 32, 1)


def _sigmoid(x):
    return 1.0 / (1.0 + jnp.exp(-x))


_GELU_K = 0.7978845608028654
_GELU_C = 0.044715


def _gelu(x):
    return 0.5 * x * (1.0 + jnp.tanh(_GELU_K * (x + _GELU_C * x * x * x)))


def _gelu_and_grad(x):
    x2 = x * x
    t = jnp.tanh(_GELU_K * (x + _GELU_C * x2 * x))
    half = 0.5 * (1.0 + t)
    return x * half, half + 0.5 * x * (1.0 - t * t) * (_GELU_K * (1.0 + 3.0 * _GELU_C * x2))


def _acc_rows(ref, val):
    ref[...] += jnp.broadcast_to(jnp.sum(val, axis=0, keepdims=True), ref.shape)


def _row(tm, c):
    return pl.BlockSpec((tm, c), lambda i: (i, 0))


def _const(shape):
    nd = len(shape)
    return pl.BlockSpec(shape, lambda i: (0,) * nd, pipeline_mode=pl.Buffered(1))


def _wblk(rows, row0):
    assert row0 % rows == 0, (rows, row0)
    return pl.BlockSpec((N_CHIPS, rows, D), lambda i: (0, row0 // rows, 0), pipeline_mode=pl.Buffered(1))


def _rows_joined(w_ref):
    return w_ref[...].reshape(N_CHIPS * w_ref.shape[1], D)


def _sds(shape, dtype):
    return jax.ShapeDtypeStruct(shape, dtype)


def mixffn_fwd(h, y, allw, lay, g2):
    t, k = y.shape

    def body(h_ref, y_ref, wo_ref, g_ref, wu_ref, wd_ref, h1_ref, h2_ref, hn_ref, r_ref):
        h1 = h_ref[...] + _dot(y_ref[...], _rows_joined(wo_ref))
        h1_ref[...] = h1
        yn, _, _ = _rms(h1, g_ref[...], D)
        hn = yn.astype(BF16)
        hn_ref[...] = hn
        f = jnp.zeros((TMB, D), F32)
        for c in range(N_CHIPS):
            r = jnp.maximum(_dot(hn, wu_ref[c]), 0.0)
            r_ref[:, c * D:(c + 1) * D] = r.astype(BF16)
            f = f + _dot((r * r).astype(BF16), wd_ref[c])
        h2_ref[...] = h1 + f

    return pl.pallas_call(
        body, name="mixffn_fwd", grid=(t // TMB,),
        in_specs=[_row(TMB, D), _row(TMB, k), _wblk(k // N_CHIPS, lay["out"]), _const((1, D)), _wblk(D, lay["up"]),
                  _wblk(D, lay["down"])],
        out_specs=[_row(TMB, D), _row(TMB, D), _row(TMB, D), _row(TMB, DFF)],
        out_shape=[_sds((t, D), F32), _sds((t, D), F32), _sds((t, D), BF16), _sds((t, DFF), BF16)],
        compiler_params=_cp("parallel"),
    )(h, y, allw, g2, allw, allw)


def ple_fwd(h2, p, g3, allw, lay, wp):
    t = h2.shape[0]

    def body(h_ref, p_ref, g_ref, wg_ref, wp_ref, h3_ref, hn_ref):
        x = h_ref[...]
        yn, _, _ = _rms(x, g_ref[...], D)
        hn = yn.astype(BF16)
        hn_ref[...] = hn
        gt = _dot(hn, _rows_joined(wg_ref))
        pp = _dot(p_ref[...].astype(BF16), wp_ref[...])
        h3_ref[...] = x + _sigmoid(gt) * pp

    return pl.pallas_call(
        body, name="ple_fwd", grid=(t // TMB,),
        in_specs=[_row(TMB, D), _row(TMB, PLE), _const((1, D)), _wblk(D // N_CHIPS, lay["gate"]), _const((PLE, D))],
        out_specs=[_row(TMB, D), _row(TMB, D)],
        out_shape=[_sds((t, D), F32), _sds((t, D), BF16)],
        compiler_params=_cp("parallel"),
    )(h2, p, g3, allw, wp)


def _mla_project(h_ref, g1_ref, wdn_ref, gq_ref, gkv_ref, wuq_ref, wukv_ref):
    x = h_ref[...]
    yn, xhat, rx = _rms(x, g1_ref[...], D)
    hn = yn.astype(BF16)
    lat = _dot(hn, wdn_ref[...])
    cq, cqhat, rq = _rms(lat[:, :QL], gq_ref[...], QL)
    ckv, ckvhat, rkv = _rms(lat[:, QL:QL + KVL], gkv_ref[...], KVL)
    kr_raw = lat[:, QL + KVL:]
    cqb = cq.astype(BF16)
    ckvb = ckv.astype(BF16)
    qp = _dot(cqb, wuq_ref[...])
    kvp = _dot(ckvb, wukv_ref[...])
    return dict(xhat=xhat, rx=rx, hn=hn, cqhat=cqhat, rq=rq, ckvhat=ckvhat, rkv=rkv, kr_raw=kr_raw,
                cqb=cqb, ckvb=ckvb, qp=qp, kvp=kvp)


def mla_pre_fwd(h, g1, wdn, gq, gkv, wuq, wukv, gqn, gqr, gkn, gkr, cos, sin):
    t = h.shape[0]

    def body(h_ref, g1_ref, wdn_ref, gq_ref, gkv_ref, wuq_ref, wukv_ref, gqn_ref, gqr_ref, gkn_ref, gkr_ref,
             c_ref, s_ref, q_ref, k_ref, v_ref):
        m = _mla_project(h_ref, g1_ref, wdn_ref, gq_ref, gkv_ref, wuq_ref, wukv_ref)
        c = c_ref[...]
        s = s_ref[...]
        kr, _, _ = _rms(m["kr_raw"], gkr_ref[...], DR)
        krb = _rope(kr, c, s).astype(BF16)
        for hd in range(HEADS):
            qn, _, _ = _rms(m["qp"][:, hd * DN:(hd + 1) * DN], gqn_ref[...], DN)
            qr, _, _ = _rms(m["qp"][:, D + hd * LANES:D + (hd + 1) * LANES], gqr_ref[...], DR)
            q_ref[hd, :, 0:DN] = (qn * SM_SCALE).astype(BF16)
            q_ref[hd, :, DN:2 * DN] = (_rope(qr, c, s) * SM_SCALE).astype(BF16)
            kn, _, _ = _rms(m["kvp"][:, hd * 2 * DN:hd * 2 * DN + DN], gkn_ref[...], DN)
            k_ref[hd, :, 0:DN] = kn.astype(BF16)
            k_ref[hd, :, DN:2 * DN] = krb
            v_ref[hd] = m["kvp"][:, hd * 2 * DN + DN:(hd + 1) * 2 * DN].astype(BF16)

    hb = lambda w: pl.BlockSpec((HEADS, TM, w), lambda i: (0, i, 0))
    return pl.pallas_call(
        body, name="mla_pre_fwd", grid=(t // TM,),
        in_specs=[_row(TM, D), _const((1, D)), _const((D, LATP)), _const((1, QL)), _const((1, KVL)),
                  _const((QL, 2 * D)), _const((KVL, 2 * D)), _const((1, LANES)), _const((1, LANES)),
                  _const((1, LANES)), _const((1, LANES)), _row(TM, LANES), _row(TM, LANES)],
        out_specs=[hb(2 * DN), hb(2 * DN), hb(DN)],
        out_shape=[_sds((HEADS, t, 2 * DN), BF16), _sds((HEADS, t, 2 * DN), BF16), _sds((HEADS, t, DN), BF16)],
        compiler_params=_cp("parallel"),
    )(h, g1, wdn, gq, gkv, wuq, wukv, gqn, gqr, gkn, gkr, cos, sin)


def _diagonal_mask(n=TQ):
    return lax.broadcasted_iota(jnp.int32, (n, n), 1) <= lax.broadcasted_iota(jnp.int32, (n, n), 0)


def flash_fwd(q, k, v, seq):
    t = q.shape[1]
    nb = t // seq
    tq = TQ_FWD
    nq = seq // tq
    hp = FWD_HEADS

    def body(q_ref, k_ref, v_ref, o_ref, lse_ref):
        qi = pl.program_id(2)
        qs = [q_ref[a] for a in range(hp)]

        def step(j, carry, diagonal=False):
            rows = pl.ds(pl.multiple_of(j * tq, tq), tq)
            out = []
            for a in range(hp):
                m, l, acc = carry[a]
                s = _dot_nt(qs[a], k_ref[a, rows, :])
                if diagonal:
                    s = jnp.where(_diagonal_mask(tq), s, -1e30)
                m_new = jnp.maximum(m, jnp.max(s, axis=-1, keepdims=True))
                p = jnp.exp(s - m_new)
                alpha = jnp.exp(m - m_new)
                l = alpha * l + jnp.sum(p, axis=-1, keepdims=True)
                acc = alpha * acc + _dot(p.astype(BF16), v_ref[a, rows, :])
                out.append((m_new, l, acc))
            return tuple(out)

        one = (jnp.full((tq, 1), -1e30, F32), jnp.zeros((tq, 1), F32), jnp.zeros((tq, DN), F32))
        done = step(qi, lax.fori_loop(0, qi, step, (one,) * hp), diagonal=True)
        for a, (m, l, acc) in enumerate(done):
            o_ref[:, a * DN:(a + 1) * DN] = (acc / l).astype(BF16)
            lse_ref[a] = m + jnp.log(l)

    return pl.pallas_call(
        body, name="flash_fwd", grid=(nb, HEADS // hp, nq),
        in_specs=[pl.BlockSpec((hp, tq, 2 * DN), lambda b, h, i: (h, b * nq + i, 0)),
                  pl.BlockSpec((hp, seq, 2 * DN), lambda b, h, i: (h, b, 0)),
                  pl.BlockSpec((hp, seq, DN), lambda b, h, i: (h, b, 0))],
        out_specs=[pl.BlockSpec((tq, hp * DN), lambda b, h, i: (b * nq + i, h)),
                   pl.BlockSpec((hp, tq, 1), lambda b, h, i: (h, b * nq + i, 0))],
        out_shape=[_sds((t, HEADS * DN), BF16), _sds((HEADS, t, 1), F32)],
        compiler_params=_cp("parallel", "parallel", "arbitrary"),
    )(q, k, v)


def _gmlp_in(hn, win_ref):
    pre = [_dot(hn, win_ref[c]) for c in range(N_CHIPS)]
    return jnp.concatenate(pre[:2], axis=1), jnp.concatenate(pre[2:], axis=1)


def gmlp_fwd(h, g1, allw, lay, lng, lnb, wm, bfull):
    t = h.shape[0]

    def body(h_ref, g1_ref, win_ref, lng_ref, lnb_ref, wm_ref, b_ref, y_ref, pre_ref):
        yn, _, _ = _rms(h_ref[...], g1_ref[...], D)
        pre_u, pre_v = _gmlp_in(yn.astype(BF16), win_ref)
        pre_ref[:, :GH] = pre_u.astype(BF16)
        pre_ref[:, GH:] = pre_v.astype(BF16)
        u = _gelu(pre_u)
        v = _gelu(pre_v)
        xc = v - jnp.mean(v, axis=-1, keepdims=True)
        rs = lax.rsqrt(jnp.mean(xc * xc, axis=-1, keepdims=True) + EPS)
        vnb = (xc * rs * lng_ref[...] + lnb_ref[...]).astype(BF16)
        for ch in range(TM // GC):
            rows = slice(ch * GC, (ch + 1) * GC)
            for g in range(GG):
                cols = slice(g * GD, (g + 1) * GD)
                sv = _dot(wm_ref[g], vnb[rows, cols]) + b_ref[:, cols]
                y_ref[rows, cols] = (u[rows, cols] * sv).astype(BF16)

    return pl.pallas_call(
        body, name="gmlp_fwd", grid=(t // TM,),
        in_specs=[_row(TM, D), _const((1, D)), _wblk(D, lay["in"]), _const((1, GH)), _const((1, GH)),
                  _const((GG, GC, GC)), _const((GC, GH))],
        out_specs=[_row(TM, GH), _row(TM, 2 * GH)],
        out_shape=[_sds((t, GH), BF16), _sds((t, 2 * GH), BF16)],
        compiler_params=_cp("parallel"),
    )(h, g1, allw, lng, lnb, wm, bfull)


def loss_head(h, tgt):
    t = h.shape[0]

    def body(h_ref, t_ref, dh_ref, loss_ref):
        @pl.when(pl.program_id(0) == 0)
        def _():
            loss_ref[...] = jnp.zeros_like(loss_ref)

        e = h_ref[...] - t_ref[...]
        dh_ref[...] = e * (1.0 / D)
        part = jnp.sum(jnp.sum(e * e, axis=-1, keepdims=True), axis=0, keepdims=True) * (0.5 / D)
        loss_ref[...] += jnp.broadcast_to(part, loss_ref.shape)

    return pl.pallas_call(
        body, name="loss_head", grid=(t // TMB,),
        in_specs=[_row(TMB, D), _row(TMB, D)],
        out_specs=[_row(TMB, D), _const((8, LANES))],
        out_shape=[_sds((t, D), F32), _sds((8, LANES), F32)],
        compiler_params=_cp("arbitrary"),
    )(h, tgt)


def _zero_at_first_step(*refs):
    @pl.when(pl.program_id(0) == 0)
    def _():
        for r in refs:
            r[...] = jnp.zeros_like(r)


def ple_bwd(dh3, h2, p, g3, allw, lay, wp):
    t = h2.shape[0]

    def body(dh_ref, h_ref, p_ref, g_ref, wg_ref, wp_ref, dh2_ref, dh2b_ref, dgt_ref, dpp_ref, dg_ref):
        _zero_at_first_step(dg_ref)
        dh3v = dh_ref[...]
        x = h_ref[...]
        g = g_ref[...]
        wg = _rows_joined(wg_ref)
        yn, xhat, r = _rms(x, g, D)
        gt = _dot(yn.astype(BF16), wg)
        pp = _dot(p_ref[...].astype(BF16), wp_ref[...])
        sg = _sigmoid(gt)
        dgt = (dh3v * pp * sg * (1.0 - sg)).astype(BF16)
        dgt_ref[...] = dgt
        dpp_ref[...] = (dh3v * sg).astype(BF16)
        dhn = _dot_nt(dgt, wg)
        _acc_rows(dg_ref, dhn * xhat)
        dh2 = dh3v + _rms_bwd(dhn, g, xhat, r, D)
        dh2_ref[...] = dh2
        dh2b_ref[...] = dh2.astype(BF16)

    return pl.pallas_call(
        body, name="ple_bwd", grid=(t // TMB,),
        in_specs=[_row(TMB, D), _row(TMB, D), _row(TMB, PLE), _const((1, D)), _wblk(D // N_CHIPS, lay["gate"]),
                  _const((PLE, D))],
        out_specs=[_row(TMB, D), _row(TMB, D), _row(TMB, D), _row(TMB, D), _const((8, D))],
        out_shape=[_sds((t, D), F32), _sds((t, D), BF16), _sds((t, D), BF16), _sds((t, D), BF16), _sds((8, D), F32)],
        compiler_params=_cp("arbitrary"),
    )(dh3, h2, p, g3, allw, wp)


def ffn_bwd(dh2, dh2b, h1, r, g2, allw, lay):
    t = h1.shape[0]

    def body(dh_ref, dhb_ref, h_ref, r_ref, g_ref, wu_ref, wd_ref, dh1_ref, dh1b_ref, du_ref, a_ref, dg_ref):
        _zero_at_first_step(dg_ref)
        dhb = dhb_ref[...]
        g = g_ref[...]
        _, xhat, rr = _rms(h_ref[...], g, D)
        dhn = jnp.zeros((TM, D), F32)
        for c in range(N_CHIPS):
            cs = slice(c * D, (c + 1) * D)
            rc = r_ref[:, cs].astype(F32)
            a_ref[:, cs] = (rc * rc).astype(BF16)
            da = _dot_nt(dhb, wd_ref[c])
            du = (da * (2.0 * rc)).astype(BF16)
            du_ref[:, cs] = du
            dhn = dhn + _dot_nt(du, wu_ref[c])
        _acc_rows(dg_ref, dhn * xhat)
        dh1 = dh_ref[...] + _rms_bwd(dhn, g, xhat, rr, D)
        dh1_ref[...] = dh1
        dh1b_ref[...] = dh1.astype(BF16)

    return pl.pallas_call(
        body, name="ffn_bwd", grid=(t // TM,),
        in_specs=[_row(TM, D), _row(TM, D), _row(TM, D), _row(TM, DFF), _const((1, D)), _wblk(D, lay["up"]),
                  _wblk(D, lay["down"])],
        out_specs=[_row(TM, D), _row(TM, D), _row(TM, DFF), _row(TM, DFF), _const((8, D))],
        out_shape=[_sds((t, D), F32), _sds((t, D), BF16), _sds((t, DFF), BF16), _sds((t, DFF), BF16),
                   _sds((8, D), F32)],
        compiler_params=_cp("arbitrary"),
    )(dh2, dh2b, h1, r, g2, allw, allw)


def linear_nt(a, allw, rows, row0):
    t = a.shape[0]
    k = N_CHIPS * rows

    def body(a_ref, w_ref, o_ref):
        o_ref[...] = _dot_nt(a_ref[...], _rows_joined(w_ref)).astype(BF16)

    return pl.pallas_call(
        body, name="linear_nt", grid=(t // TMB,),
        in_specs=[_row(TMB, D), _wblk(rows, row0)],
        out_specs=_row(TMB, k),
        out_shape=_sds((t, k), BF16),
        compiler_params=_cp("parallel"),
    )(a, allw)


def flash_bwd(q, k, v, o, do, lse, seq):
    t = q.shape[1]
    nb = t // seq
    nq = seq // TQ
    hp = BWD_HEADS

    def body(q_ref, k_ref, v_ref, o_ref, do_ref, lse_ref, dq_ref, dk_ref, dv_ref):
        kj = pl.program_id(2)

        @pl.when(kj == 0)
        def _():
            dq_ref[...] = jnp.zeros_like(dq_ref)

        def step(i, carry, diagonal=False):
            rows = pl.ds(pl.multiple_of(i * TQ, TQ), TQ)
            out = []
            for a in range(hp):
                dk, dv = carry[a]
                kv = k_ref[a]
                qv = q_ref[a, rows, :]
                dov = do_ref[rows, a * DN:(a + 1) * DN]
                ov = o_ref[rows, a * DN:(a + 1) * DN]
                delta = jnp.sum(dov.astype(F32) * ov.astype(F32), axis=-1, keepdims=True)
                s = _dot_nt(qv, kv)
                if diagonal:
                    s = jnp.where(_diagonal_mask(), s, -1e30)
                p = jnp.exp(s - lse_ref[a, rows, :])
                dp = _dot_nt(dov, v_ref[a])
                ds = (p * (dp - delta)).astype(BF16)
                dv = dv + _dot_tn(p.astype(BF16), dov)
                dk = dk + _dot_tn(ds, qv)
                dq_ref[a, rows, :] += _dot(ds, kv)
                out.append((dk, dv))
            return tuple(out)

        one = (jnp.zeros((TQ, 2 * DN), F32), jnp.zeros((TQ, DN), F32))
        done = lax.fori_loop(kj + 1, nq, step, step(kj, (one,) * hp, diagonal=True))
        for a, (dk, dv) in enumerate(done):
            dk_ref[a] = dk
            dv_ref[a] = dv

    return pl.pallas_call(
        body, name="flash_bwd", grid=(nb, HEADS // hp, nq),
        in_specs=[pl.BlockSpec((hp, seq, 2 * DN), lambda b, h, j: (h, b, 0)),
                  pl.BlockSpec((hp, TQ, 2 * DN), lambda b, h, j: (h, b * nq + j, 0)),
                  pl.BlockSpec((hp, TQ, DN), lambda b, h, j: (h, b * nq + j, 0)),
                  pl.BlockSpec((seq, hp * DN), lambda b, h, j: (b, h)),
                  pl.BlockSpec((seq, hp * DN), lambda b, h, j: (b, h)),
                  pl.BlockSpec((hp, seq, 1), lambda b, h, j: (h, b, 0))],
        out_specs=[pl.BlockSpec((hp, seq, 2 * DN), lambda b, h, j: (h, b, 0)),
                   pl.BlockSpec((hp, TQ, 2 * DN), lambda b, h, j: (h, b * nq + j, 0)),
                   pl.BlockSpec((hp, TQ, DN), lambda b, h, j: (h, b * nq + j, 0))],
        out_shape=[_sds((HEADS, t, 2 * DN), F32), _sds((HEADS, t, 2 * DN), F32), _sds((HEADS, t, DN), F32)],
        compiler_params=_cp("parallel", "parallel", "arbitrary"),
    )(q, k, v, o, do, lse)


def mla_pre_bwd(dq, dk, dv, dh1, h, g1, wdn, gq, gkv, wuq, wukv, gqn, gqr, gkn, gkr, cos, sin):
    t = h.shape[0]

    def body(dq_ref, dk_ref, dv_ref, dh1_ref, h_ref, g1_ref, wdn_ref, gq_ref, gkv_ref, wuq_ref, wukv_ref,
             gqn_ref, gqr_ref, gkn_ref, gkr_ref, c_ref, s_ref,
             dh_ref, hn_ref, cq_ref, ckv_ref, dqp_ref, dkvp_ref, dlat_ref,
             dg1_ref, dgq_ref, dgkv_ref, dgqn_ref, dgqr_ref, dgkn_ref, dgkr_ref):
        _zero_at_first_step(dg1_ref, dgq_ref, dgkv_ref, dgqn_ref, dgqr_ref, dgkn_ref, dgkr_ref)
        m = _mla_project(h_ref, g1_ref, wdn_ref, gq_ref, gkv_ref, wuq_ref, wukv_ref)
        hn_ref[...] = m["hn"]
        cq_ref[...] = m["cqb"]
        ckv_ref[...] = m["ckvb"]
        c = c_ref[...]
        s = s_ref[...]
        gqn = gqn_ref[...]
        gqr = gqr_ref[...]
        gkn = gkn_ref[...]
        gkr = gkr_ref[...]

        dkr = dk_ref[0, :, DN:2 * DN]
        for hd in range(1, HEADS):
            dkr = dkr + dk_ref[hd, :, DN:2 * DN]
        dkr = _rope_t(dkr, c, s)
        _, krhat, rkr = _rms(m["kr_raw"], gkr, DR)
        _acc_rows(dgkr_ref, dkr * krhat)
        dkr_raw = _rms_bwd(dkr, gkr, krhat, rkr, DR)

        for hd in range(HEADS):
            ncols = slice(hd * DN, (hd + 1) * DN)
            _, xh, r = _rms(m["qp"][:, ncols], gqn, DN)
            dqn = dq_ref[hd, :, 0:DN] * SM_SCALE
            _acc_rows(dgqn_ref, dqn * xh)
            dqp_ref[:, ncols] = _rms_bwd(dqn, gqn, xh, r, DN).astype(BF16)

            rcols = slice(D + hd * LANES, D + (hd + 1) * LANES)
            _, xh, r = _rms(m["qp"][:, rcols], gqr, DR)
            dqr = _rope_t(dq_ref[hd, :, DN:2 * DN] * SM_SCALE, c, s)
            _acc_rows(dgqr_ref, dqr * xh)
            dqp_ref[:, rcols] = _rms_bwd(dqr, gqr, xh, r, DR).astype(BF16)

            kcols = slice(hd * 2 * DN, hd * 2 * DN + DN)
            _, xh, r = _rms(m["kvp"][:, kcols], gkn, DN)
            dkn = dk_ref[hd, :, 0:DN]
            _acc_rows(dgkn_ref, dkn * xh)
            dkvp_ref[:, kcols] = _rms_bwd(dkn, gkn, xh, r, DN).astype(BF16)
            dkvp_ref[:, hd * 2 * DN + DN:(hd + 1) * 2 * DN] = dv_ref[hd].astype(BF16)

        dcq = _dot_nt(dqp_ref[...], wuq_ref[...])
        _acc_rows(dgq_ref, dcq * m["cqhat"])
        dlat_q = _rms_bwd(dcq, gq_ref[...], m["cqhat"], m["rq"], QL)
        dckv = _dot_nt(dkvp_ref[...], wukv_ref[...])
        _acc_rows(dgkv_ref, dckv * m["ckvhat"])
        dlat_kv = _rms_bwd(dckv, gkv_ref[...], m["ckvhat"], m["rkv"], KVL)
        dlat = jnp.concatenate([dlat_q, dlat_kv, dkr_raw], axis=1).astype(BF16)
        dlat_ref[...] = dlat
        dhn = _dot_nt(dlat, wdn_ref[...])
        _acc_rows(dg1_ref, dhn * m["xhat"])
        dh_ref[...] = dh1_ref[...] + _rms_bwd(dhn, g1_ref[...], m["xhat"], m["rx"], D)

    hb = lambda w: pl.BlockSpec((HEADS, TM, w), lambda i: (0, i, 0))
    return pl.pallas_call(
        body, name="mla_pre_bwd", grid=(t // TM,),
        in_specs=[hb(2 * DN), hb(2 * DN), hb(DN), _row(TM, D), _row(TM, D), _const((1, D)), _const((D, LATP)),
                  _const((1, QL)), _const((1, KVL)), _const((QL, 2 * D)), _const((KVL, 2 * D)),
                  _const((1, LANES)), _const((1, LANES)), _const((1, LANES)), _const((1, LANES)),
                  _row(TM, LANES), _row(TM, LANES)],
        out_specs=[_row(TM, D), _row(TM, D), _row(TM, QL), _row(TM, KVL), _row(TM, 2 * D), _row(TM, 2 * D),
                   _row(TM, LATP), _const((8, D)), _const((8, QL)), _const((8, KVL)), _const((8, LANES)),
                   _const((8, LANES)), _const((8, LANES)), _const((8, LANES))],
        out_shape=[_sds((t, D), F32), _sds((t, D), BF16), _sds((t, QL), BF16), _sds((t, KVL), BF16),
                   _sds((t, 2 * D), BF16), _sds((t, 2 * D), BF16), _sds((t, LATP), BF16),
                   _sds((8, D), F32), _sds((8, QL), F32), _sds((8, KVL), F32), _sds((8, LANES), F32),
                   _sds((8, LANES), F32), _sds((8, LANES), F32), _sds((8, LANES), F32)],
        compiler_params=_cp("arbitrary"),
    )(dq, dk, dv, dh1, h, g1, wdn, gq, gkv, wuq, wukv, gqn, gqr, gkn, gkr, cos, sin)


def gmlp_bwd(dh1, dh1b, h, pre, g1, allw, lay, lng, lnb, wm, wmt, bfull, tril):
    t = h.shape[0]

    def body(dh1_ref, dh1b_ref, h_ref, pre_ref, g1_ref, win_ref, lng_ref, lnb_ref, wm_ref, wmt_ref, b_ref,
             wout_ref, tril_ref, dh_ref, hn_ref, dpre_ref, dws_ref, dbs_ref, dlng_ref, dlnb_ref, dg1_ref,
             dvn_s):
        _zero_at_first_step(dws_ref, dbs_ref, dlng_ref, dlnb_ref, dg1_ref)
        g1 = g1_ref[...]
        yn, xhat, rx = _rms(h_ref[...], g1, D)
        hn_ref[...] = yn.astype(BF16)
        dy = _dot_nt(dh1b_ref[...], _rows_joined(wout_ref))
        pre_u = pre_ref[:, :GH].astype(F32)
        pre_v = pre_ref[:, GH:].astype(F32)
        u, gg_u = _gelu_and_grad(pre_u)
        v, gg_v = _gelu_and_grad(pre_v)
        xc = v - jnp.mean(v, axis=-1, keepdims=True)
        rs = lax.rsqrt(jnp.mean(xc * xc, axis=-1, keepdims=True) + EPS)
        vhat = xc * rs
        lng = lng_ref[...]
        vnb = (vhat * lng + lnb_ref[...]).astype(BF16)
        dsv = dy * u
        dsvb = dsv.astype(BF16)
        tril_m = tril_ref[...]
        for ch in range(TM // GC):
            rows = slice(ch * GC, (ch + 1) * GC)
            dbs_ref[...] += dsv[rows, :]
            for g in range(GG):
                cols = slice(g * GD, (g + 1) * GD)
                sv = _dot(wm_ref[g], vnb[rows, cols]) + b_ref[:, cols]
                dpre_ref[rows, cols] = (dy[rows, cols] * sv * gg_u[rows, cols]).astype(BF16)
                dvn_s[rows, cols] = _dot(wmt_ref[g], dsvb[rows, cols])
                dws_ref[g] += _dot_nt(dsvb[rows, cols], vnb[rows, cols]) * tril_m
        dvn = dvn_s[...]
        _acc_rows(dlng_ref, dvn * vhat)
        _acc_rows(dlnb_ref, dvn)
        dvhat = dvn * lng
        dv = rs * (dvhat - jnp.mean(dvhat, axis=-1, keepdims=True)
                   - vhat * jnp.mean(dvhat * vhat, axis=-1, keepdims=True))
        dpre_v = (dv * gg_v).astype(BF16)
        dpre_ref[:, GH:] = dpre_v
        dhn = _dot_nt(dpre_ref[:, 0:D], win_ref[0])
        for c in range(1, N_CHIPS):
            dhn = dhn + _dot_nt(dpre_ref[:, c * D:(c + 1) * D], win_ref[c])
        _acc_rows(dg1_ref, dhn * xhat)
        dh_ref[...] = dh1_ref[...] + _rms_bwd(dhn, g1, xhat, rx, D)

    return pl.pallas_call(
        body, name="gmlp_bwd", grid=(t // TM,),
        in_specs=[_row(TM, D), _row(TM, D), _row(TM, D), _row(TM, 2 * GH), _const((1, D)), _wblk(D, lay["in"]),
                  _const((1, GH)), _const((1, GH)), _const((GG, GC, GC)), _const((GG, GC, GC)), _const((GC, GH)),
                  _wblk(GH // N_CHIPS, lay["out"]), _const((GC, GC))],
        out_specs=[_row(TM, D), _row(TM, D), _row(TM, 2 * GH), _const((GG, GC, GC)), _const((GC, GH)),
                   _const((8, GH)), _const((8, GH)), _const((8, D))],
        out_shape=[_sds((t, D), F32), _sds((t, D), BF16), _sds((t, 2 * GH), BF16), _sds((GG, GC, GC), F32),
                   _sds((GC, GH), F32), _sds((8, GH), F32), _sds((8, GH), F32), _sds((8, D), F32)],
        scratch_shapes=[pltpu.VMEM((TM, GH), F32)],
        compiler_params=_cp("arbitrary"),
    )(dh1, dh1b, h, pre, g1, allw, lng, lnb, wm, wmt, bfull, allw, tril)


def _token_step(t):
    return 1024 if t % 1024 == 0 else 512


def mm_tn(a, b):
    t, k = a.shape
    n = b.shape[1]
    tk = min(k, 1024)
    tn = min(n, 1024)
    tt = _token_step(t)

    def body(a_ref, b_ref, o_ref):
        @pl.when(pl.program_id(2) == 0)
        def _():
            o_ref[...] = jnp.zeros_like(o_ref)

        o_ref[...] += _dot_tn(a_ref[...].astype(BF16), b_ref[...].astype(BF16))

    return pl.pallas_call(
        body, name="mm_tn", grid=(k // tk, n // tn, t // tt),
        in_specs=[pl.BlockSpec((tt, tk), lambda i, j, s: (s, i)), pl.BlockSpec((tt, tn), lambda i, j, s: (s, j))],
        out_specs=pl.BlockSpec((tk, tn), lambda i, j, s: (i, j)), out_shape=_sds((k, n), F32),
        compiler_params=_cp("parallel", "parallel", "arbitrary"),
    )(a, b)


def mm_tn_into(buf, a, b, rows, row0, col_sharded):
    t = a.shape[0]
    tt = _token_step(t)
    assert row0 % rows == 0 and a.shape[1] == (rows if col_sharded else N_CHIPS * rows), (rows, row0, a.shape)
    assert b.shape[1] == (N_CHIPS * D if col_sharded else D), b.shape
    grid = (1, N_CHIPS, t // tt) if col_sharded else (N_CHIPS, 1, t // tt)
    fresh = isinstance(buf, int)

    def body(*refs):
        a_ref, b_ref, o_ref = refs[-3:]

        @pl.when(pl.program_id(2) == 0)
        def _():
            o_ref[...] = jnp.zeros_like(o_ref)

        o_ref[...] += _dot_tn(a_ref[...].astype(BF16), b_ref[...].astype(BF16))

    specs = [pl.BlockSpec((tt, rows), lambda i, j, s: (s, i)), pl.BlockSpec((tt, D), lambda i, j, s: (s, j))]
    return pl.pallas_call(
        body, name="mm_tn_into", grid=grid,
        in_specs=specs if fresh else [_ANY] + specs,
        out_specs=pl.BlockSpec((None, rows, D), lambda i, j, s: (i + j, row0 // rows, 0)),
        out_shape=_sds((N_CHIPS, buf, D) if fresh else buf.shape, F32),
        input_output_aliases={} if fresh else {0: 0},
        compiler_params=_cp("parallel", "parallel", "arbitrary"),
    )(*((a, b) if fresh else (buf, a, b)))


def adamw(w, g, m, v):
    rows, cols = w.shape
    tr = rows if rows <= 512 else next(r for r in (512, 384, 256, 128) if rows % r == 0)
    c1 = 1.0 - ADAM_B1 ** ADAM_STEP
    c2 = 1.0 - ADAM_B2 ** ADAM_STEP

    def body(w_ref, g_ref, m_ref, v_ref, d_ref, mo_ref, vo_ref):
        gv = g_ref[...]
        mn = ADAM_B1 * m_ref[...] + (1.0 - ADAM_B1) * gv
        vn = ADAM_B2 * v_ref[...] + (1.0 - ADAM_B2) * (gv * gv)
        mo_ref[...] = mn
        vo_ref[...] = vn
        d_ref[...] = -ADAM_LR * ((mn / c1) / (jnp.sqrt(vn / c2) + ADAM_EPS) + ADAM_WD * w_ref[...])

    spec = pl.BlockSpec((tr, cols), lambda i: (i, 0))
    return pl.pallas_call(
        body, name="adamw", grid=(rows // tr,),
        in_specs=[spec] * 4, out_specs=[spec] * 3, out_shape=[_sds((rows, cols), F32)] * 3,
        compiler_params=_cp("parallel"),
    )(w, g, m, v)


def _place():
    return lax.axis_index("x"), lax.axis_index("y"), lax.axis_index("c")


def _other_chips(x, y):
    return [(1 - x, y), (x, 1 - y), (1 - x, 1 - y)]


_ANY = pl.BlockSpec(memory_space=pl.ANY)


_HBM = pl.BlockSpec(memory_space=pltpu.HBM)
_SEM = pl.BlockSpec(memory_space=pltpu.SEMAPHORE)
_EFFECT = pltpu.SideEffectType.DATAFLOW_SIDE_EFFECTING
N_ICI = 3


def _exchange_start(name, src, land, copies, n):
    def body(src_ref, land_ref, *outs):
        sems, token = outs[:2 * n], outs[-1]
        for j, (s, d, to) in enumerate(copies(src_ref, land_ref, _place())):
            pltpu.make_async_remote_copy(src_ref=s, dst_ref=d, send_sem=sems[j], recv_sem=sems[n + j],
                                         device_id=to, device_id_type=MESH).start()
        token[...] = jnp.zeros_like(token)

    sem = pltpu.SemaphoreType.DMA(())
    outs = pl.pallas_call(
        body, name=name,
        out_shape=(sem,) * (2 * n) + (pltpu.HBM(src.shape, src.dtype), pltpu.HBM(land.shape, land.dtype),
                                      _sds((8, LANES), F32)),
        in_specs=(_HBM, _HBM),
        out_specs=(_SEM,) * (2 * n) + (_HBM, _HBM, pl.BlockSpec(memory_space=pltpu.VMEM)),
        input_output_aliases={0: 2 * n, 1: 2 * n + 1},
        compiler_params=pltpu.CompilerParams(has_side_effects=_EFFECT),
    )(pltpu.with_memory_space_constraint(src, pltpu.HBM), pltpu.with_memory_space_constraint(land, pltpu.HBM))
    return outs[:2 * n], outs[2 * n], outs[2 * n + 1], outs[-1]


def _exchange_wait(name, sems, src, land, after, arrivals):
    n = len(sems) // 2

    def body(src_ref, land_ref, *rest):
        sems = rest[:2 * n]
        for j, (s, d) in enumerate(arrivals(src_ref, land_ref, _place())):
            cp = pltpu.make_async_remote_copy(src_ref=s, dst_ref=d, send_sem=sems[j], recv_sem=sems[n + j],
                                              device_id=_place(), device_id_type=MESH)
            cp.wait_send()
            cp.wait_recv()

    return pl.pallas_call(
        body, name=name, out_shape=(pltpu.HBM(src.shape, src.dtype), pltpu.HBM(land.shape, land.dtype)),
        in_specs=(_HBM, _HBM) + (_SEM,) * (2 * n) + (_ANY,), out_specs=(_HBM, _HBM),
        input_output_aliases={0: 0, 1: 1},
        compiler_params=pltpu.CompilerParams(has_side_effects=_EFFECT),
    )(src, land, *sems, after)


def _halves(c, hh):
    return pl.ds(pl.multiple_of(c * hh, 16), hh), pl.ds(pl.multiple_of((1 - c) * hh, 16), hh)


def gather_start(land, tag):
    _, rr, _ = land.shape
    assert rr % 32 == 0, rr

    def copies(_, land_ref, place):
        x, y, c = place
        mine = land_ref.at[2 * x + y, _halves(c, rr // 2)[0]]
        return [(mine, mine, (cx, cy, c)) for cx, cy in _other_chips(x, y)]

    return _exchange_start(f"gather_start_{tag}", jnp.zeros((8, LANES), F32), land, copies, N_ICI)


def gather_wait(sems, src, land, after, tag):
    def arrivals(_, land_ref, place):
        x, y, c = place
        half = _halves(c, land.shape[1] // 2)[0]
        return [(land_ref.at[2 * x + y, half], land_ref.at[2 * cx + cy, half]) for cx, cy in _other_chips(x, y)]

    return _exchange_wait(f"gather_wait_{tag}", sems, src, land, after, arrivals)


def pass_start(land, tag):
    def copies(_, land_ref, place):
        x, y, c = place
        half = _halves(c, land.shape[1] // 2)[0]
        return [(land_ref.at[2 * cx + cy, half], land_ref.at[2 * cx + cy, half], (x, y, 1 - c))
                for cx, cy in _other_chips(x, y)]

    return _exchange_start(f"pass_start_{tag}", jnp.zeros((8, LANES), F32), land, copies, N_ICI)


def pass_wait(sems, src, land, after, tag):
    def arrivals(_, land_ref, place):
        x, y, c = place
        mine, other = _halves(c, land.shape[1] // 2)
        return [(land_ref.at[2 * cx + cy, mine], land_ref.at[2 * cx + cy, other]) for cx, cy in _other_chips(x, y)]

    return _exchange_wait(f"pass_wait_{tag}", sems, src, land, after, arrivals)


def swap_start(g, tag):
    _, rr, cc = g.shape

    def copies(g_ref, got_ref, place):
        x, y, c = place
        other = _halves(c, rr // 2)[1]
        return [(g_ref.at[k, other], got_ref.at[k], (x, y, 1 - c)) for k in range(N_CHIPS)]

    return _exchange_start(f"swap_start_{tag}", g, lax.empty((N_CHIPS, rr // 2, cc), g.dtype), copies, N_CHIPS)


def swap_wait(sems, g, got, after, tag):
    def arrivals(g_ref, got_ref, place):
        other = _halves(place[2], g.shape[1] // 2)[1]
        return [(g_ref.at[k, other], got_ref.at[k]) for k in range(N_CHIPS)]

    return _exchange_wait(f"swap_wait_{tag}", sems, g, got, after, arrivals)


def chip_sum(place, g32, got):
    _, rr, cc = g32.shape
    hh = rr // 2
    tr = SUM_ROWS
    assert rr % 2 == 0 and hh % tr == 0, (rr, tr)
    nb = hh // tr

    def body(place_ref, g_ref, got_ref, own_ref, all_ref):
        s = g_ref[...] + got_ref[...].astype(F32)
        all_ref[...] = s.astype(BF16)
        own_ref[...] = g_ref[place_ref[1]] + got_ref[place_ref[1]].astype(F32)

    return pl.pallas_call(
        body, name="chip_sum",
        grid_spec=pltpu.PrefetchScalarGridSpec(
            num_scalar_prefetch=1, grid=(nb,),
            in_specs=[pl.BlockSpec((N_CHIPS, tr, cc), lambda i, pr: (0, pr[0] * nb + i, 0)),
                      pl.BlockSpec((N_CHIPS, tr, cc), lambda i, pr: (0, i, 0))],
            out_specs=[pl.BlockSpec((tr, cc), lambda i, pr: (i, 0)),
                       pl.BlockSpec((N_CHIPS, tr, cc), lambda i, pr: (0, i, 0))]),
        out_shape=[_sds((hh, cc), F32), _sds((N_CHIPS, hh, cc), BF16)],
        compiler_params=_cp("parallel"),
    )(place, g32, got)


def _scatter_copies(s_ref, land_ref, place):
    x, y, c = place
    return [(s_ref.at[2 * cx + cy], land_ref.at[j], (cx, cy, c)) for j, (cx, cy) in enumerate(_other_chips(x, y))]


def scatter_start(s, tag):
    return _exchange_start(f"scatter_start_{tag}", s, lax.empty((N_ICI,) + s.shape[1:], s.dtype), _scatter_copies, N_ICI)


def scatter_wait(sems, s, land, after, tag):
    return _exchange_wait(f"scatter_wait_{tag}", sems, s, land, after,
                          lambda s_ref, land_ref, place: [(a, b) for a, b, _ in _scatter_copies(s_ref, land_ref, place)])


def final_sum(place, own, got):
    hh, cc = own.shape
    tr = SUM_ROWS
    assert hh % tr == 0, (hh, tr)
    nb = hh // tr

    def body(place_ref, own_ref, got_ref, o_ref):
        del place_ref
        o_ref[...] = ((own_ref[...] + got_ref[0].astype(F32)) + got_ref[1].astype(F32)) + got_ref[2].astype(F32)

    return pl.pallas_call(
        body, name="final_sum",
        grid_spec=pltpu.PrefetchScalarGridSpec(
            num_scalar_prefetch=1, grid=(nb,),
            in_specs=[pl.BlockSpec((tr, cc), lambda i, pr: (i, 0)), pl.BlockSpec((3, tr, cc), lambda i, pr: (0, i, 0))],
            out_specs=pl.BlockSpec((tr, cc), lambda i, pr: (pr[0] * nb + i, 0))),
        out_shape=_sds((2 * hh, cc), F32),
        compiler_params=_cp("parallel"),
    )(place, own, got)


def share_start(f, tag):
    def copies(_, f_ref, place):
        x, y, c = place
        mine = f_ref.at[_halves(c, f.shape[0] // 2)[0]]
        return [(mine, mine, (x, y, 1 - c))]

    return _exchange_start(f"share_start_{tag}", jnp.zeros((8, LANES), F32), f, copies, 1)


def share_wait(sems, src, f, after, tag):
    def arrivals(_, f_ref, place):
        mine, other = _halves(place[2], f.shape[0] // 2)
        return [(f_ref.at[mine], f_ref.at[other])]

    return _exchange_wait(f"share_wait_{tag}", sems, src, f, after, arrivals)


N_DEV = 8


def _peers(place):
    x, y, c = place
    return [((1 - x) if r & 4 else x, (1 - y) if r & 2 else y, (1 - c) if r & 1 else c) for r in range(1, N_DEV)]


def _device_index(place):
    x, y, c = place
    return 4 * x + 2 * y + c


def small_start(land):
    def copies(_, land_ref, place):
        mine = land_ref.at[_device_index(place)]
        return [(mine, mine, to) for to in _peers(place)]

    return _exchange_start("small_start", jnp.zeros((8, LANES), F32), land, copies, N_DEV - 1)


def small_wait(sems, src, land, after):
    def arrivals(_, land_ref, place):
        return [(land_ref.at[_device_index(place)], land_ref.at[_device_index(peer)]) for peer in _peers(place)]

    return _exchange_wait("small_wait", sems, src, land, after, arrivals)


def sum_devices(land):
    _, rr, cc = land.shape
    tr = 56
    assert rr % tr == 0, rr

    def body(l_ref, o_ref):
        acc = l_ref[0]
        for d in range(1, N_DEV):
            acc = acc + l_ref[d]
        o_ref[...] = acc

    return pl.pallas_call(
        body, name="sum_devices", grid=(rr // tr,),
        in_specs=[pl.BlockSpec((N_DEV, tr, cc), lambda i: (0, i, 0))],
        out_specs=pl.BlockSpec((tr, cc), lambda i: (i, 0)), out_shape=_sds((rr, cc), F32),
        compiler_params=_cp("parallel"),
    )(land)


_BIG = ["mla_w_down", "mla_w_uq", "mla_w_ukv", "mla_w_out", "gmlp_w_in", "gmlp_w_out", "ffn_w_up", "ffn_w_down",
        "ple_w_gate", "ple_w_proj"]
_SMALL = ["norm_mix", "norm_ffn", "norm_ple", "mla_q_lora_g", "mla_kv_lora_g", "mla_q_nope_g", "mla_q_rope_g",
          "mla_k_nope_g", "mla_k_rope_g", "gmlp_ln_g", "gmlp_ln_b", "gmlp_w_s", "gmlp_b_s"]

_LAY_MLA = dict(up=0, down=1024, out=2048, gate=2304, wdn=2560, wuq=2736, wukv=2880, proj=3008, rows=3072)
_LAY_GMLP = {"up": 0, "down": 1024, "in": 2048, "out": 3072, "gate": 3584, "proj": 3840, "ln": 3904, "rows": 4096}


def _layer_parts(i):
    j = i // 2
    if i % 2 == 0:
        lay = _LAY_MLA
        return lay, [("ffn_w_up", i, lay["up"]), ("ffn_w_down", i, lay["down"]), ("mla_w_out", j, lay["out"]),
                     ("ple_w_gate", i, lay["gate"]), ("mla_w_down", j, lay["wdn"]), ("mla_w_uq", j, lay["wuq"]),
                     ("mla_w_ukv", j, lay["wukv"]), ("ple_w_proj", i, lay["proj"])]
    lay = _LAY_GMLP
    return lay, [("ffn_w_up", i, lay["up"]), ("ffn_w_down", i, lay["down"]), ("gmlp_w_in", j, lay["in"]),
                 ("gmlp_w_out", j, lay["out"]), ("ple_w_gate", i, lay["gate"]), ("ple_w_proj", i, lay["proj"])]


def _pack_rows(parts, dtype, pad_to=None, slot=False):
    size = sum(p.size for p in parts)
    tail = [] if pad_to is None or pad_to * D == size else [jnp.zeros((pad_to * D - size,), dtype)]
    shape = (1, -1, D) if slot else (-1, D)
    if all(p.size % D == 0 for p in parts + tail):
        return jnp.concatenate([p.astype(dtype).reshape(shape) for p in parts + tail], axis=len(shape) - 2)
    return jnp.concatenate([p.astype(dtype).reshape(-1) for p in parts + tail]).reshape(shape)


def _odd(allw, row0, a, b):
    return allw[:, row0:row0 + a * b // D].reshape(N_CHIPS, a, b)


def _cols_joined(s):
    return jnp.transpose(s, (1, 0, 2)).reshape(s.shape[1], N_CHIPS * s.shape[2])


def _col_shards(full):
    a, bb = full.shape
    return jnp.transpose(full.reshape(a, N_CHIPS, bb // N_CHIPS), (1, 0, 2)).reshape(N_CHIPS, -1, D)


def _pad_lanes(g):
    return jnp.pad(g, ((0, 0), (0, LANES - g.shape[1])))


def _split_uq(wuq):
    l = wuq.shape[0]
    w = wuq.reshape(l, QL, HEADS, DN + DR)
    nope = w[..., :DN].reshape(l, QL, HEADS * DN)
    rope = jnp.pad(w[..., DN:], ((0, 0), (0, 0), (0, 0), (0, LANES - DR))).reshape(l, QL, HEADS * LANES)
    return jnp.concatenate([nope, rope], axis=-1)


def _merge_uq(d):
    nope = d[:, :HEADS * DN].reshape(QL, HEADS, DN)
    rope = d[:, HEADS * DN:].reshape(QL, HEADS, LANES)[..., :DR]
    return jnp.concatenate([nope, rope], axis=-1).reshape(QL, HEADS * (DN + DR))


def _rope_tables(positions):
    inv_freq = ROPE_BASE ** (-(jnp.arange(0, DR, 2, dtype=F32) / DR))
    ang = positions.reshape(-1).astype(F32)[:, None] * inv_freq
    z = jnp.zeros((ang.shape[0], LANES - DR), F32)
    return (jnp.concatenate([jnp.cos(ang), jnp.cos(ang), z], axis=1),
            jnp.concatenate([jnp.sin(ang), jnp.sin(ang), z], axis=1))


def kernel(x, p, positions, norm_mix, norm_ffn, norm_ple, mla_w_down, mla_q_lora_g, mla_kv_lora_g, mla_w_uq, mla_w_ukv, mla_q_nope_g, mla_q_rope_g, mla_k_nope_g, mla_k_rope_g, mla_w_out, gmlp_w_in, gmlp_ln_g, gmlp_ln_b, gmlp_w_s, gmlp_b_s, gmlp_w_out, ffn_w_up, ffn_w_down, ple_w_gate, ple_w_proj, loss_target, m_norm_mix, m_norm_ffn, m_norm_ple, m_mla_w_down, m_mla_q_lora_g, m_mla_kv_lora_g, m_mla_w_uq, m_mla_w_ukv, m_mla_q_nope_g, m_mla_q_rope_g, m_mla_k_nope_g, m_mla_k_rope_g, m_mla_w_out, m_gmlp_w_in, m_gmlp_ln_g, m_gmlp_ln_b, m_gmlp_w_s, m_gmlp_b_s, m_gmlp_w_out, m_ffn_w_up, m_ffn_w_down, m_ple_w_gate, m_ple_w_proj, v_norm_mix, v_norm_ffn, v_norm_ple, v_mla_w_down, v_mla_q_lora_g, v_mla_kv_lora_g, v_mla_w_uq, v_mla_w_ukv, v_mla_q_nope_g, v_mla_q_rope_g, v_mla_k_nope_g, v_mla_k_rope_g, v_mla_w_out, v_gmlp_w_in, v_gmlp_ln_g, v_gmlp_ln_b, v_gmlp_w_s, v_gmlp_b_s, v_gmlp_w_out, v_ffn_w_up, v_ffn_w_down, v_ple_w_gate, v_ple_w_proj):
    args = dict(locals())
    weights = {n: args[n] for n in _BIG + _SMALL}
    depth = norm_mix.shape[0]
    nb, seq, _ = x.shape
    t = nb * seq
    assert seq % TQ == 0 and seq % TM == 0 and t % 512 == 0, (nb, seq)
    cx = lax.axis_index("x")
    cy = lax.axis_index("y")
    cc = lax.axis_index("c")
    chip = 2 * cx + cy

    gathers = []
    token = None
    for i in range(depth):
        lay, parts = _layer_parts(i)
        rows = [weights[n][l] for n, l, _ in parts]
        if token is not None:
            rows[0] = rows[0] + token[0, 0]
        if i % 2 == 1:
            ln = jnp.stack([gmlp_ln_g[i // 2], gmlp_ln_b[i // 2]]).astype(F32)
            bits = lax.bitcast_convert_type(ln, BF16).reshape(-1)
            rows.append(jnp.pad(bits, (0, 16 * D - bits.size)).reshape(16, D))
        mine = _pack_rows(rows, BF16, pad_to=lay["rows"], slot=True)
        land = lax.dynamic_update_slice(lax.empty((N_CHIPS, lay["rows"], D), BF16), mine, (chip, 0, 0))
        sems, src, land, token = gather_start(land, i)
        gathers.append((sems, src, land))
    allw = [None] * depth

    tril = jnp.tril(jnp.ones((GC, GC), F32))
    wm = (gmlp_w_s * tril).astype(BF16)
    wmt = jnp.swapaxes(wm, -1, -2)
    bfull = jnp.repeat(jnp.swapaxes(gmlp_b_s, -1, -2), GD, axis=-1)
    cos, sin = _rope_tables(positions)
    row = lambda g: g.reshape(1, -1)
    gqr = _pad_lanes(mla_q_rope_g)
    gkr = _pad_lanes(mla_k_rope_g)

    h = x.reshape(t, D)
    pt = p.reshape(depth, t, PLE)
    saved = []

    def arrive(i, after):
        sems, src, land = gathers[i]
        _, land = gather_wait(sems, src, land, after, i)
        return pass_start(land, i)

    passing = arrive(0, token)
    for i in range(depth):
        j = i // 2
        lay, _ = _layer_parts(i)
        sems, src, land, token = passing
        _, aw = pass_wait(sems, src, land, token if i == 0 else h, i)
        allw[i] = aw
        s = dict(h=h)
        if i % 2 == 0:
            wdn = jnp.pad(_odd(aw, lay["wdn"], D // N_CHIPS, LAT).reshape(D, LAT), ((0, 0), (0, LATP - LAT)))
            wuq = _split_uq(_cols_joined(_odd(aw, lay["wuq"], QL, 384))[None])[0]
            wukv = _cols_joined(_odd(aw, lay["wukv"], KVL, 512))
            mla_args = (row(norm_mix[i]), wdn, row(mla_q_lora_g[j]), row(mla_kv_lora_g[j]), wuq, wukv,
                        row(mla_q_nope_g[j]), gqr[j:j + 1], row(mla_k_nope_g[j]), gkr[j:j + 1], cos, sin)
            q, k, v = mla_pre_fwd(h, *mla_args)
            y, lse = flash_fwd(q, k, v, seq)
            s.update(q=q, k=k, v=v, lse=lse, mla_args=mla_args)
        else:
            ln = lax.bitcast_convert_type(aw[:, lay["ln"]:lay["ln"] + 2].reshape(N_CHIPS, 2, GH // N_CHIPS, 2), F32)
            ln = jnp.transpose(ln, (1, 0, 2)).reshape(2, 1, GH)
            y, pre = gmlp_fwd(h, row(norm_mix[i]), aw, lay, ln[0], ln[1], wm[j], bfull[j])
            s.update(pre=pre, ln=ln)
        wp = _cols_joined(_odd(aw, lay["proj"], PLE, 256))
        g2 = row(norm_ffn[i])
        if i + 1 < depth:
            passing = arrive(i + 1, y)
            g2 = g2 + passing[3][0:1, 0:1]
        h1, h2, hn2, r = mixffn_fwd(h, y, aw, lay, g2)
        h, hn3 = ple_fwd(h2, pt[i], row(norm_ple[i]), aw, lay, wp)
        s.update(y=y, wp=wp, h1=h1, h2=h2, hn2=hn2, r=r, hn3=hn3)
        saved.append(s)

    dh, loss_part = loss_head(h, loss_target.reshape(t, D))
    loss = lax.psum(loss_part[0, 0], ("x", "y", "c"))

    gs = {n: [None] * weights[n].shape[0] for n in _SMALL}
    gw = {n: [None] * weights[n].shape[0] for n in _BIG}
    place = jnp.stack([cc, chip]).astype(jnp.int32)
    scatters = []
    swapping = None
    token = None

    def put(b, row0, shards):
        return lax.dynamic_update_slice(b, shards.reshape(N_CHIPS, -1, D), (0, row0, 0))

    def swapped(after):
        ii, sems, g, got = swapping
        g, got = swap_wait(sems, g, got, after, ii)
        own, sums = chip_sum(place, g, got)
        sems, sums, land, tok = scatter_start(sums, ii)
        scatters.append((ii, own, sems, sums, land))
        return tok

    for i in reversed(range(depth)):
        j = i // 2
        lay, parts = _layer_parts(i)
        aw = allw[i]
        s = saved[i]

        g3 = row(norm_ple[i])
        if token is not None:
            g3 = g3 + token[0:1, 0:1]
        dh2, dh2b, dgt, dpp, dg3 = ple_bwd(dh, s["h2"], pt[i], g3, aw, lay, s["wp"])
        gs["norm_ple"][i] = dg3[0]
        buf = mm_tn_into(lay["rows"], s["hn3"], dgt, D // N_CHIPS, lay["gate"], False)
        tail = lay.get("ln", lay["rows"])
        if tail < lay["rows"]:
            buf = put(buf, tail, jnp.zeros((N_CHIPS, lay["rows"] - tail, D), F32))
        buf = put(buf, lay["proj"], _col_shards(mm_tn(pt[i], dpp)))
        dh1, dh1b, du, a, dg2 = ffn_bwd(dh2, dh2b, s["h1"], s["r"], row(norm_ffn[i]), aw, lay)
        gs["norm_ffn"][i] = dg2[0]
        buf = mm_tn_into(buf, a, dh2b, D, lay["down"], False)
        buf = mm_tn_into(buf, s["hn2"], du, D, lay["up"], True)
        buf = mm_tn_into(buf, s["y"], dh1b, s["y"].shape[1] // N_CHIPS, lay["out"], False)
        g1 = row(norm_mix[i])
        if swapping is not None:
            g1 = g1 + swapped(dh1)[0:1, 0:1]
        if i % 2 == 0:
            do = linear_nt(dh1b, aw, D // N_CHIPS, lay["out"])
            dq, dk, dv = flash_bwd(s["q"], s["k"], s["v"], s["y"], do, s["lse"], seq)
            (dh, hn1, cq, ckv, dqp, dkvp, dlat, dg1, dgq, dgkv, dgqn, dgqr, dgkn, dgkr) = mla_pre_bwd(
                dq, dk, dv, dh1, s["h"], g1, *s["mla_args"][1:])
            gs["norm_mix"][i] = dg1[0]
            gs["mla_q_lora_g"][j] = dgq[0]
            gs["mla_kv_lora_g"][j] = dgkv[0]
            gs["mla_q_nope_g"][j] = dgqn[0]
            gs["mla_q_rope_g"][j] = dgqr[0, :DR]
            gs["mla_k_nope_g"][j] = dgkn[0]
            gs["mla_k_rope_g"][j] = dgkr[0, :DR]
            buf = put(buf, lay["wdn"], mm_tn(hn1, dlat)[:, :LAT])
            buf = put(buf, lay["wuq"], _col_shards(_merge_uq(mm_tn(cq, dqp))))
            buf = put(buf, lay["wukv"], _col_shards(mm_tn(ckv, dkvp)))
        else:
            dh, hn1, dpre, dws, dbs, dlng, dlnb, dg1 = gmlp_bwd(
                dh1, dh1b, s["h"], s["pre"], g1, aw, lay, s["ln"][0], s["ln"][1], wm[j], wmt[j], bfull[j], tril)
            gs["norm_mix"][i] = dg1[0]
            gs["gmlp_ln_g"][j] = dlng[0]
            gs["gmlp_ln_b"][j] = dlnb[0]
            gs["gmlp_w_s"][j] = dws
            gs["gmlp_b_s"][j] = jnp.sum(dbs.reshape(GC, GG, GD), axis=-1).T
            buf = mm_tn_into(buf, hn1, dpre, D, lay["in"], True)

        sems, buf, got, token = swap_start(buf, i)
        swapping = (i, sems, buf, got)
    swapped(dh)
    grad_x = dh.reshape(x.shape)

    small_sizes = [weights[n].size if n not in ("gmlp_ln_g", "gmlp_ln_b") else weights[n].shape[0] * GH
                   for n in _SMALL]
    small_rows = -(-sum(small_sizes) // (56 * D)) * 56
    part = _pack_rows([jnp.stack(gs[n]) for n in _SMALL], F32, pad_to=small_rows, slot=True)
    land = lax.dynamic_update_slice(lax.empty((N_DEV, small_rows, D), F32), part, (2 * chip + cc, 0, 0))
    small = small_start(land)

    after = small[3]
    shares = []
    for i, own, sems, sums, land in scatters:
        _, got = scatter_wait(sems, sums, land, after, i)
        sems, src, full, after = share_start(final_sum(place, own, got), i)
        shares.append((i, sems, src, full))
    for i, sems, src, full in shares:
        _, after = share_wait(sems, src, full, after, i)
        for n, l, row0 in _layer_parts(i)[1]:
            gw[n][l] = after[row0:row0 + weights[n][l].size // D].reshape(weights[n].shape[1:])
    grads = {n: jnp.stack(gw[n]) for n in _BIG}

    tot = sum_devices(small_wait(small[0], small[1], small[2], after)[1]).reshape(-1)
    off = 0
    for n, sz in zip(_SMALL, small_sizes):
        gsum = tot[off:off + sz]
        off += sz
        if n in ("gmlp_ln_g", "gmlp_ln_b"):
            gsum = lax.dynamic_slice_in_dim(gsum.reshape(-1, GH), chip * (GH // N_CHIPS), GH // N_CHIPS, axis=1)
        grads[n] = gsum.reshape(weights[n].shape)

    delta, new_m, new_v = {}, {}, {}
    for n in _BIG:
        w2 = weights[n].reshape(-1, weights[n].shape[-1])
        d, mn, vn = adamw(w2, grads[n].reshape(w2.shape), args["m_" + n].reshape(w2.shape),
                          args["v_" + n].reshape(w2.shape))
        delta[n], new_m[n], new_v[n] = (a.reshape(weights[n].shape) for a in (d, mn, vn))
    own_sizes = [weights[n].size for n in _SMALL]
    own_rows = -(-sum(own_sizes) // (8 * D)) * 8
    packed = [_pack_rows([src[n] for n in _SMALL], F32, pad_to=own_rows)
              for src in (weights, grads, {n: args["m_" + n] for n in _SMALL}, {n: args["v_" + n] for n in _SMALL})]
    outs = adamw(*packed)
    off = 0
    for n, sz in zip(_SMALL, own_sizes):
        for dst, o in zip((delta, new_m, new_v), outs):
            dst[n] = o.reshape(-1)[off:off + sz].reshape(weights[n].shape)
        off += sz

    order = ["norm_mix", "norm_ffn", "norm_ple", "mla_w_down", "mla_q_lora_g", "mla_kv_lora_g", "mla_w_uq",
             "mla_w_ukv", "mla_q_nope_g", "mla_q_rope_g", "mla_k_nope_g", "mla_k_rope_g", "mla_w_out", "gmlp_w_in",
             "gmlp_ln_g", "gmlp_ln_b", "gmlp_w_s", "gmlp_b_s", "gmlp_w_out", "ffn_w_up", "ffn_w_down", "ple_w_gate",
             "ple_w_proj"]
    return (loss, grad_x, *[grads[n] for n in order], *[delta[n] for n in order], *[new_m[n] for n in order],
            *[new_v[n] for n in order])
```

```python
import functools

import jax
import jax.numpy as jnp
from jax import lax
from jax.experimental import pallas as pl
from jax.experimental.pallas import tpu as pltpu

F32 = jnp.float32
BF16 = jnp.bfloat16
MESH = pl.DeviceIdType.MESH

D = 1024
HEADS = 8
DN = 128
DR = 64
QL = 384
KVL = 256
LAT = 704
LATP = 768
DFF = 4096
GH = 2048
GC = 128
GG = 8
GD = 256
PLE = 256
EPS = 1e-6
ROPE_BASE = 10000.0
SM_SCALE = (DN + DR) ** -0.5
N_CHIPS = 4
LANES = 128

ADAM_LR = 0.001
ADAM_B1 = 0.9
ADAM_B2 = 0.999
ADAM_EPS = 1e-08
ADAM_WD = 0.01
ADAM_STEP = 10

TM = 256
TMB = 512
TQ = 512
TQ_FWD = 512
FWD_HEADS = 2
BWD_HEADS = 2
SUM_ROWS = 256
VMEM_LIMIT = 56 * 1024 * 1024


def _cp(*sem):
    return pltpu.CompilerParams(dimension_semantics=sem, vmem_limit_bytes=VMEM_LIMIT)


def _dot(a, b):
    return jnp.dot(a, b, preferred_element_type=F32)


def _dot_nt(a, b):
    return lax.dot_general(a, b, (((1,), (1,)), ((), ())), preferred_element_type=F32)


def _dot_tn(a, b):
    return lax.dot_general(a, b, (((0,), (0,)), ((), ())), preferred_element_type=F32)


def _rms(x, g, n):
    r = lax.rsqrt(jnp.sum(x * x, axis=-1, keepdims=True) * (1.0 / n) + EPS)
    xhat = x * r
    return xhat * g, xhat, r


def _rms_bwd(dy, g, xhat, r, n):
    dxhat = dy * g
    return r * (dxhat - xhat * (jnp.sum(dxhat * xhat, axis=-1, keepdims=True) * (1.0 / n)))


def _rope(x, c, s):
    return x * c + (pltpu.roll(x, 32, 1) - pltpu.roll(x, 96, 1)) * s


def _rope_t(dy, c, s):
    w = dy * s
    return dy * c + pltpu.roll(w, 96, 1) - pltpu.roll(w, 32, 1)


def _sigmoid(x):
    return 1.0 / (1.0 + jnp.exp(-x))


_GELU_K = 0.7978845608028654
_GELU_C = 0.044715


def _gelu(x):
    return 0.5 * x * (1.0 + jnp.tanh(_GELU_K * (x + _GELU_C * x * x * x)))


def _gelu_and_grad(x):
    x2 = x * x
    t = jnp.tanh(_GELU_K * (x + _GELU_C * x2 * x))
    half = 0.5 * (1.0 + t)
    return x * half, half + 0.5 * x * (1.0 - t * t) * (_GELU_K * (1.0 + 3.0 * _GELU_C * x2))


def _acc_rows(ref, val):
    ref[...] += jnp.broadcast_to(jnp.sum(val, axis=0, keepdims=True), ref.shape)


def _row(tm, c):
    return pl.BlockSpec((tm, c), lambda i: (i, 0))


def _const(shape):
    nd = len(shape)
    return pl.BlockSpec(shape, lambda i: (0,) * nd, pipeline_mode=pl.Buffered(1))


def _wblk(rows, row0):
    assert row0 % rows == 0, (rows, row0)
    return pl.BlockSpec((N_CHIPS, rows, D), lambda i: (0, row0 // rows, 0), pipeline_mode=pl.Buffered(1))


def _rows_joined(w_ref):
    return w_ref[...].reshape(N_CHIPS * w_ref.shape[1], D)


def _sds(shape, dtype):
    return jax.ShapeDtypeStruct(shape, dtype)


def mixffn_fwd(h, y, allw, lay, g2):
    t, k = y.shape

    def body(h_ref, y_ref, wo_ref, g_ref, wu_ref, wd_ref, h1_ref, h2_ref, hn_ref, r_ref):
        h1 = h_ref[...] + _dot(y_ref[...], _rows_joined(wo_ref))
        h1_ref[...] = h1
        yn, _, _ = _rms(h1, g_ref[...], D)
        hn = yn.astype(BF16)
        hn_ref[...] = hn
        f = jnp.zeros((TMB, D), F32)
        for c in range(N_CHIPS):
            r = jnp.maximum(_dot(hn, wu_ref[c]), 0.0)
            r_ref[:, c * D:(c + 1) * D] = r.astype(BF16)
            f = f + _dot((r * r).astype(BF16), wd_ref[c])
        h2_ref[...] = h1 + f

    return pl.pallas_call(
        body, name="mixffn_fwd", grid=(t // TMB,),
        in_specs=[_row(TMB, D), _row(TMB, k), _wblk(k // N_CHIPS, lay["out"]), _const((1, D)), _wblk(D, lay["up"]),
                  _wblk(D, lay["down"])],
        out_specs=[_row(TMB, D), _row(TMB, D), _row(TMB, D), _row(TMB, DFF)],
        out_shape=[_sds((t, D), F32), _sds((t, D), F32), _sds((t, D), BF16), _sds((t, DFF), BF16)],
        compiler_params=_cp("parallel"),
    )(h, y, allw, g2, allw, allw)


def ple_fwd(h2, p, g3, allw, lay, wp):
    t = h2.shape[0]

    def body(h_ref, p_ref, g_ref, wg_ref, wp_ref, h3_ref, hn_ref):
        x = h_ref[...]
        yn, _, _ = _rms(x, g_ref[...], D)
        hn = yn.astype(BF16)
        hn_ref[...] = hn
        gt = _dot(hn, _rows_joined(wg_ref))
        pp = _dot(p_ref[...].astype(BF16), wp_ref[...])
        h3_ref[...] = x + _sigmoid(gt) * pp

    return pl.pallas_call(
        body, name="ple_fwd", grid=(t // TMB,),
        in_specs=[_row(TMB, D), _row(TMB, PLE), _const((1, D)), _wblk(D // N_CHIPS, lay["gate"]), _const((PLE, D))],
        out_specs=[_row(TMB, D), _row(TMB, D)],
        out_shape=[_sds((t, D), F32), _sds((t, D), BF16)],
        compiler_params=_cp("parallel"),
    )(h2, p, g3, allw, wp)


def _mla_project(h_ref, g1_ref, wdn_ref, gq_ref, gkv_ref, wuq_ref, wukv_ref):
    x = h_ref[...]
    yn, xhat, rx = _rms(x, g1_ref[...], D)
    hn = yn.astype(BF16)
    lat = _dot(hn, wdn_ref[...])
    cq, cqhat, rq = _rms(lat[:, :QL], gq_ref[...], QL)
    ckv, ckvhat, rkv = _rms(lat[:, QL:QL + KVL], gkv_ref[...], KVL)
    kr_raw = lat[:, QL + KVL:]
    cqb = cq.astype(BF16)
    ckvb = ckv.astype(BF16)
    qp = _dot(cqb, wuq_ref[...])
    kvp = _dot(ckvb, wukv_ref[...])
    return dict(xhat=xhat, rx=rx, hn=hn, cqhat=cqhat, rq=rq, ckvhat=ckvhat, rkv=rkv, kr_raw=kr_raw,
                cqb=cqb, ckvb=ckvb, qp=qp, kvp=kvp)


def mla_pre_fwd(h, g1, wdn, gq, gkv, wuq, wukv, gqn, gqr, gkn, gkr, cos, sin):
    t = h.shape[0]

    def body(h_ref, g1_ref, wdn_ref, gq_ref, gkv_ref, wuq_ref, wukv_ref, gqn_ref, gqr_ref, gkn_ref, gkr_ref,
             c_ref, s_ref, q_ref, k_ref, v_ref):
        m = _mla_project(h_ref, g1_ref, wdn_ref, gq_ref, gkv_ref, wuq_ref, wukv_ref)
        c = c_ref[...]
        s = s_ref[...]
        kr, _, _ = _rms(m["kr_raw"], gkr_ref[...], DR)
        krb = _rope(kr, c, s).astype(BF16)
        for hd in range(HEADS):
            qn, _, _ = _rms(m["qp"][:, hd * DN:(hd + 1) * DN], gqn_ref[...], DN)
            qr, _, _ = _rms(m["qp"][:, D + hd * LANES:D + (hd + 1) * LANES], gqr_ref[...], DR)
            q_ref[hd, :, 0:DN] = (qn * SM_SCALE).astype(BF16)
            q_ref[hd, :, DN:2 * DN] = (_rope(qr, c, s) * SM_SCALE).astype(BF16)
            kn, _, _ = _rms(m["kvp"][:, hd * 2 * DN:hd * 2 * DN + DN], gkn_ref[...], DN)
            k_ref[hd, :, 0:DN] = kn.astype(BF16)
            k_ref[hd, :, DN:2 * DN] = krb
            v_ref[hd] = m["kvp"][:, hd * 2 * DN + DN:(hd + 1) * 2 * DN].astype(BF16)

    hb = lambda w: pl.BlockSpec((HEADS, TM, w), lambda i: (0, i, 0))
    return pl.pallas_call(
        body, name="mla_pre_fwd", grid=(t // TM,),
        in_specs=[_row(TM, D), _const((1, D)), _const((D, LATP)), _const((1, QL)), _const((1, KVL)),
                  _const((QL, 2 * D)), _const((KVL, 2 * D)), _const((1, LANES)), _const((1, LANES)),
                  _const((1, LANES)), _const((1, LANES)), _row(TM, LANES), _row(TM, LANES)],
        out_specs=[hb(2 * DN), hb(2 * DN), hb(DN)],
        out_shape=[_sds((HEADS, t, 2 * DN), BF16), _sds((HEADS, t, 2 * DN), BF16), _sds((HEADS, t, DN), BF16)],
        compiler_params=_cp("parallel"),
    )(h, g1, wdn, gq, gkv, wuq, wukv, gqn, gqr, gkn, gkr, cos, sin)


def _diagonal_mask(n=TQ):
    return lax.broadcasted_iota(jnp.int32, (n, n), 1) <= lax.broadcasted_iota(jnp.int32, (n, n), 0)


def flash_fwd(q, k, v, seq):
    t = q.shape[1]
    nb = t // seq
    tq = TQ_FWD
    nq = seq // tq
    hp = FWD_HEADS

    def body(q_ref, k_ref, v_ref, o_ref, lse_ref):
        qi = pl.program_id(2)
        qs = [q_ref[a] for a in range(hp)]

        def step(j, carry, diagonal=False):
            rows = pl.ds(pl.multiple_of(j * tq, tq), tq)
            out = []
            for a in range(hp):
                m, l, acc = carry[a]
                s = _dot_nt(qs[a], k_ref[a, rows, :])
                if diagonal:
                    s = jnp.where(_diagonal_mask(tq), s, -1e30)
                m_new = jnp.maximum(m, jnp.max(s, axis=-1, keepdims=True))
                p = jnp.exp(s - m_new)
                alpha = jnp.exp(m - m_new)
                l = alpha * l + jnp.sum(p, axis=-1, keepdims=True)
                acc = alpha * acc + _dot(p.astype(BF16), v_ref[a, rows, :])
                out.append((m_new, l, acc))
            return tuple(out)

        one = (jnp.full((tq, 1), -1e30, F32), jnp.zeros((tq, 1), F32), jnp.zeros((tq, DN), F32))
        done = step(qi, lax.fori_loop(0, qi, step, (one,) * hp), diagonal=True)
        for a, (m, l, acc) in enumerate(done):
            o_ref[:, a * DN:(a + 1) * DN] = (acc / l).astype(BF16)
            lse_ref[a] = m + jnp.log(l)

    return pl.pallas_call(
        body, name="flash_fwd", grid=(nb, HEADS // hp, nq),
        in_specs=[pl.BlockSpec((hp, tq, 2 * DN), lambda b, h, i: (h, b * nq + i, 0)),
                  pl.BlockSpec((hp, seq, 2 * DN), lambda b, h, i: (h, b, 0)),
                  pl.BlockSpec((hp, seq, DN), lambda b, h, i: (h, b, 0))],
        out_specs=[pl.BlockSpec((tq, hp * DN), lambda b, h, i: (b * nq + i, h)),
                   pl.BlockSpec((hp, tq, 1), lambda b, h, i: (h, b * nq + i, 0))],
        out_shape=[_sds((t, HEADS * DN), BF16), _sds((HEADS, t, 1), F32)],
        compiler_params=_cp("parallel", "parallel", "arbitrary"),
    )(q, k, v)


def _gmlp_in(hn, win_ref):
    pre = [_dot(hn, win_ref[c]) for c in range(N_CHIPS)]
    return jnp.concatenate(pre[:2], axis=1), jnp.concatenate(pre[2:], axis=1)


def gmlp_fwd(h, g1, allw, lay, lng, lnb, wm, bfull):
    t = h.shape[0]

    def body(h_ref, g1_ref, win_ref, lng_ref, lnb_ref, wm_ref, b_ref, y_ref, pre_ref):
        yn, _, _ = _rms(h_ref[...], g1_ref[...], D)
        pre_u, pre_v = _gmlp_in(yn.astype(BF16), win_ref)
        pre_ref[:, :GH] = pre_u.astype(BF16)
        pre_ref[:, GH:] = pre_v.astype(BF16)
        u = _gelu(pre_u)
        v = _gelu(pre_v)
        xc = v - jnp.mean(v, axis=-1, keepdims=True)
        rs = lax.rsqrt(jnp.mean(xc * xc, axis=-1, keepdims=True) + EPS)
        vnb = (xc * rs * lng_ref[...] + lnb_ref[...]).astype(BF16)
        for ch in range(TM // GC):
            rows = slice(ch * GC, (ch + 1) * GC)
            for g in range(GG):
                cols = slice(g * GD, (g + 1) * GD)
                sv = _dot(wm_ref[g], vnb[rows, cols]) + b_ref[:, cols]
                y_ref[rows, cols] = (u[rows, cols] * sv).astype(BF16)

    return pl.pallas_call(
        body, name="gmlp_fwd", grid=(t // TM,),
        in_specs=[_row(TM, D), _const((1, D)), _wblk(D, lay["in"]), _const((1, GH)), _const((1, GH)),
                  _const((GG, GC, GC)), _const((GC, GH))],
        out_specs=[_row(TM, GH), _row(TM, 2 * GH)],
        out_shape=[_sds((t, GH), BF16), _sds((t, 2 * GH), BF16)],
        compiler_params=_cp("parallel"),
    )(h, g1, allw, lng, lnb, wm, bfull)


def loss_head(h, tgt):
    t = h.shape[0]

    def body(h_ref, t_ref, dh_ref, loss_ref):
        @pl.when(pl.program_id(0) == 0)
        def _():
            loss_ref[...] = jnp.zeros_like(loss_ref)

        e = h_ref[...] - t_ref[...]
        dh_ref[...] = e * (1.0 / D)
        part = jnp.sum(jnp.sum(e * e, axis=-1, keepdims=True), axis=0, keepdims=True) * (0.5 / D)
        loss_ref[...] += jnp.broadcast_to(part, loss_ref.shape)

    return pl.pallas_call(
        body, name="loss_head", grid=(t // TMB,),
        in_specs=[_row(TMB, D), _row(TMB, D)],
        out_specs=[_row(TMB, D), _const((8, LANES))],
        out_shape=[_sds((t, D), F32), _sds((8, LANES), F32)],
        compiler_params=_cp("arbitrary"),
    )(h, tgt)


def _zero_at_first_step(*refs):
    @pl.when(pl.program_id(0) == 0)
    def _():
        for r in refs:
            r[...] = jnp.zeros_like(r)


def ple_bwd(dh3, h2, p, g3, allw, lay, wp):
    t = h2.shape[0]

    def body(dh_ref, h_ref, p_ref, g_ref, wg_ref, wp_ref, dh2_ref, dh2b_ref, dgt_ref, dpp_ref, dg_ref):
        _zero_at_first_step(dg_ref)
        dh3v = dh_ref[...]
        x = h_ref[...]
        g = g_ref[...]
        wg = _rows_joined(wg_ref)
        yn, xhat, r = _rms(x, g, D)
        gt = _dot(yn.astype(BF16), wg)
        pp = _dot(p_ref[...].astype(BF16), wp_ref[...])
        sg = _sigmoid(gt)
        dgt = (dh3v * pp * sg * (1.0 - sg)).astype(BF16)
        dgt_ref[...] = dgt
        dpp_ref[...] = (dh3v * sg).astype(BF16)
        dhn = _dot_nt(dgt, wg)
        _acc_rows(dg_ref, dhn * xhat)
        dh2 = dh3v + _rms_bwd(dhn, g, xhat, r, D)
        dh2_ref[...] = dh2
        dh2b_ref[...] = dh2.astype(BF16)

    return pl.pallas_call(
        body, name="ple_bwd", grid=(t // TMB,),
        in_specs=[_row(TMB, D), _row(TMB, D), _row(TMB, PLE), _const((1, D)), _wblk(D // N_CHIPS, lay["gate"]),
                  _const((PLE, D))],
        out_specs=[_row(TMB, D), _row(TMB, D), _row(TMB, D), _row(TMB, D), _const((8, D))],
        out_shape=[_sds((t, D), F32), _sds((t, D), BF16), _sds((t, D), BF16), _sds((t, D), BF16), _sds((8, D), F32)],
        compiler_params=_cp("arbitrary"),
    )(dh3, h2, p, g3, allw, wp)


def ffn_bwd(dh2, dh2b, h1, r, g2, allw, lay):
    t = h1.shape[0]

    def body(dh_ref, dhb_ref, h_ref, r_ref, g_ref, wu_ref, wd_ref, dh1_ref, dh1b_ref, du_ref, a_ref, dg_ref):
        _zero_at_first_step(dg_ref)
        dhb = dhb_ref[...]
        g = g_ref[...]
        _, xhat, rr = _rms(h_ref[...], g, D)
        dhn = jnp.zeros((TM, D), F32)
        for c in range(N_CHIPS):
            cs = slice(c * D, (c + 1) * D)
            rc = r_ref[:, cs].astype(F32)
            a_ref[:, cs] = (rc * rc).astype(BF16)
            da = _dot_nt(dhb, wd_ref[c])
            du = (da * (2.0 * rc)).astype(BF16)
            du_ref[:, cs] = du
            dhn = dhn + _dot_nt(du, wu_ref[c])
        _acc_rows(dg_ref, dhn * xhat)
        dh1 = dh_ref[...] + _rms_bwd(dhn, g, xhat, rr, D)
        dh1_ref[...] = dh1
        dh1b_ref[...] = dh1.astype(BF16)

    return pl.pallas_call(
        body, name="ffn_bwd", grid=(t // TM,),
        in_specs=[_row(TM, D), _row(TM, D), _row(TM, D), _row(TM, DFF), _const((1, D)), _wblk(D, lay["up"]),
                  _wblk(D, lay["down"])],
        out_specs=[_row(TM, D), _row(TM, D), _row(TM, DFF), _row(TM, DFF), _const((8, D))],
        out_shape=[_sds((t, D), F32), _sds((t, D), BF16), _sds((t, DFF), BF16), _sds((t, DFF), BF16),
                   _sds((8, D), F32)],
        compiler_params=_cp("arbitrary"),
    )(dh2, dh2b, h1, r, g2, allw, allw)


def linear_nt(a, allw, rows, row0):
    t = a.shape[0]
    k = N_CHIPS * rows

    def body(a_ref, w_ref, o_ref):
        o_ref[...] = _dot_nt(a_ref[...], _rows_joined(w_ref)).astype(BF16)

    return pl.pallas_call(
        body, name="linear_nt", grid=(t // TMB,),
        in_specs=[_row(TMB, D), _wblk(rows, row0)],
        out_specs=_row(TMB, k),
        out_shape=_sds((t, k), BF16),
        compiler_params=_cp("parallel"),
    )(a, allw)


def flash_bwd(q, k, v, o, do, lse, seq):
    t = q.shape[1]
    nb = t // seq
    nq = seq // TQ
    hp = BWD_HEADS

    def body(q_ref, k_ref, v_ref, o_ref, do_ref, lse_ref, dq_ref, dk_ref, dv_ref):
        kj = pl.program_id(2)

        @pl.when(kj == 0)
        def _():
            dq_ref[...] = jnp.zeros_like(dq_ref)

        def step(i, carry, diagonal=False):
            rows = pl.ds(pl.multiple_of(i * TQ, TQ), TQ)
            out = []
            for a in range(hp):
                dk, dv = carry[a]
                kv = k_ref[a]
                qv = q_ref[a, rows, :]
                dov = do_ref[rows, a * DN:(a + 1) * DN]
                ov = o_ref[rows, a * DN:(a + 1) * DN]
                delta = jnp.sum(dov.astype(F32) * ov.astype(F32), axis=-1, keepdims=True)
                s = _dot_nt(qv, kv)
                if diagonal:
                    s = jnp.where(_diagonal_mask(), s, -1e30)
                p = jnp.exp(s - lse_ref[a, rows, :])
                dp = _dot_nt(dov, v_ref[a])
                ds = (p * (dp - delta)).astype(BF16)
                dv = dv + _dot_tn(p.astype(BF16), dov)
                dk = dk + _dot_tn(ds, qv)
                dq_ref[a, rows, :] += _dot(ds, kv)
                out.append((dk, dv))
            return tuple(out)

        one = (jnp.zeros((TQ, 2 * DN), F32), jnp.zeros((TQ, DN), F32))
        done = lax.fori_loop(kj + 1, nq, step, step(kj, (one,) * hp, diagonal=True))
        for a, (dk, dv) in enumerate(done):
            dk_ref[a] = dk
            dv_ref[a] = dv

    return pl.pallas_call(
        body, name="flash_bwd", grid=(nb, HEADS // hp, nq),
        in_specs=[pl.BlockSpec((hp, seq, 2 * DN), lambda b, h, j: (h, b, 0)),
                  pl.BlockSpec((hp, TQ, 2 * DN), lambda b, h, j: (h, b * nq + j, 0)),
                  pl.BlockSpec((hp, TQ, DN), lambda b, h, j: (h, b * nq + j, 0)),
                  pl.BlockSpec((seq, hp * DN), lambda b, h, j: (b, h)),
                  pl.BlockSpec((seq, hp * DN), lambda b, h, j: (b, h)),
                  pl.BlockSpec((hp, seq, 1), lambda b, h, j: (h, b, 0))],
        out_specs=[pl.BlockSpec((hp, seq, 2 * DN), lambda b, h, j: (h, b, 0)),
                   pl.BlockSpec((hp, TQ, 2 * DN), lambda b, h, j: (h, b * nq + j, 0)),
                   pl.BlockSpec((hp, TQ, DN), lambda b, h, j: (h, b * nq + j, 0))],
        out_shape=[_sds((HEADS, t, 2 * DN), F32), _sds((HEADS, t, 2 * DN), F32), _sds((HEADS, t, DN), F32)],
        compiler_params=_cp("parallel", "parallel", "arbitrary"),
    )(q, k, v, o, do, lse)


def mla_pre_bwd(dq, dk, dv, dh1, h, g1, wdn, gq, gkv, wuq, wukv, gqn, gqr, gkn, gkr, cos, sin):
    t = h.shape[0]

    def body(dq_ref, dk_ref, dv_ref, dh1_ref, h_ref, g1_ref, wdn_ref, gq_ref, gkv_ref, wuq_ref, wukv_ref,
             gqn_ref, gqr_ref, gkn_ref, gkr_ref, c_ref, s_ref,
             dh_ref, hn_ref, cq_ref, ckv_ref, dqp_ref, dkvp_ref, dlat_ref,
             dg1_ref, dgq_ref, dgkv_ref, dgqn_ref, dgqr_ref, dgkn_ref, dgkr_ref):
        _zero_at_first_step(dg1_ref, dgq_ref, dgkv_ref, dgqn_ref, dgqr_ref, dgkn_ref, dgkr_ref)
        m = _mla_project(h_ref, g1_ref, wdn_ref, gq_ref, gkv_ref, wuq_ref, wukv_ref)
        hn_ref[...] = m["hn"]
        cq_ref[...] = m["cqb"]
        ckv_ref[...] = m["ckvb"]
        c = c_ref[...]
        s = s_ref[...]
        gqn = gqn_ref[...]
        gqr = gqr_ref[...]
        gkn = gkn_ref[...]
        gkr = gkr_ref[...]

        dkr = dk_ref[0, :, DN:2 * DN]
        for hd in range(1, HEADS):
            dkr = dkr + dk_ref[hd, :, DN:2 * DN]
        dkr = _rope_t(dkr, c, s)
        _, krhat, rkr = _rms(m["kr_raw"], gkr, DR)
        _acc_rows(dgkr_ref, dkr * krhat)
        dkr_raw = _rms_bwd(dkr, gkr, krhat, rkr, DR)

        for hd in range(HEADS):
            ncols = slice(hd * DN, (hd + 1) * DN)
            _, xh, r = _rms(m["qp"][:, ncols], gqn, DN)
            dqn = dq_ref[hd, :, 0:DN] * SM_SCALE
            _acc_rows(dgqn_ref, dqn * xh)
            dqp_ref[:, ncols] = _rms_bwd(dqn, gqn, xh, r, DN).astype(BF16)

            rcols = slice(D + hd * LANES, D + (hd + 1) * LANES)
            _, xh, r = _rms(m["qp"][:, rcols], gqr, DR)
            dqr = _rope_t(dq_ref[hd, :, DN:2 * DN] * SM_SCALE, c, s)
            _acc_rows(dgqr_ref, dqr * xh)
            dqp_ref[:, rcols] = _rms_bwd(dqr, gqr, xh, r, DR).astype(BF16)

            kcols = slice(hd * 2 * DN, hd * 2 * DN + DN)
            _, xh, r = _rms(m["kvp"][:, kcols], gkn, DN)
            dkn = dk_ref[hd, :, 0:DN]
            _acc_rows(dgkn_ref, dkn * xh)
            dkvp_ref[:, kcols] = _rms_bwd(dkn, gkn, xh, r, DN).astype(BF16)
            dkvp_ref[:, hd * 2 * DN + DN:(hd + 1) * 2 * DN] = dv_ref[hd].astype(BF16)

        dcq = _dot_nt(dqp_ref[...], wuq_ref[...])
        _acc_rows(dgq_ref, dcq * m["cqhat"])
        dlat_q = _rms_bwd(dcq, gq_ref[...], m["cqhat"], m["rq"], QL)
        dckv = _dot_nt(dkvp_ref[...], wukv_ref[...])
        _acc_rows(dgkv_ref, dckv * m["ckvhat"])
        dlat_kv = _rms_bwd(dckv, gkv_ref[...], m["ckvhat"], m["rkv"], KVL)
        dlat = jnp.concatenate([dlat_q, dlat_kv, dkr_raw], axis=1).astype(BF16)
        dlat_ref[...] = dlat
        dhn = _dot_nt(dlat, wdn_ref[...])
        _acc_rows(dg1_ref, dhn * m["xhat"])
        dh_ref[...] = dh1_ref[...] + _rms_bwd(dhn, g1_ref[...], m["xhat"], m["rx"], D)

    hb = lambda w: pl.BlockSpec((HEADS, TM, w), lambda i: (0, i, 0))
    return pl.pallas_call(
        body, name="mla_pre_bwd", grid=(t // TM,),
        in_specs=[hb(2 * DN), hb(2 * DN), hb(DN), _row(TM, D), _row(TM, D), _const((1, D)), _const((D, LATP)),
                  _const((1, QL)), _const((1, KVL)), _const((QL, 2 * D)), _const((KVL, 2 * D)),
                  _const((1, LANES)), _const((1, LANES)), _const((1, LANES)), _const((1, LANES)),
                  _row(TM, LANES), _row(TM, LANES)],
        out_specs=[_row(TM, D), _row(TM, D), _row(TM, QL), _row(TM, KVL), _row(TM, 2 * D), _row(TM, 2 * D),
                   _row(TM, LATP), _const((8, D)), _const((8, QL)), _const((8, KVL)), _const((8, LANES)),
                   _const((8, LANES)), _const((8, LANES)), _const((8, LANES))],
        out_shape=[_sds((t, D), F32), _sds((t, D), BF16), _sds((t, QL), BF16), _sds((t, KVL), BF16),
                   _sds((t, 2 * D), BF16), _sds((t, 2 * D), BF16), _sds((t, LATP), BF16),
                   _sds((8, D), F32), _sds((8, QL), F32), _sds((8, KVL), F32), _sds((8, LANES), F32),
                   _sds((8, LANES), F32), _sds((8, LANES), F32), _sds((8, LANES), F32)],
        compiler_params=_cp("arbitrary"),
    )(dq, dk, dv, dh1, h, g1, wdn, gq, gkv, wuq, wukv, gqn, gqr, gkn, gkr, cos, sin)


def gmlp_bwd(dh1, dh1b, h, pre, g1, allw, lay, lng, lnb, wm, wmt, bfull, tril):
    t = h.shape[0]

    def body(dh1_ref, dh1b_ref, h_ref, pre_ref, g1_ref, win_ref, lng_ref, lnb_ref, wm_ref, wmt_ref, b_ref,
             wout_ref, tril_ref, dh_ref, hn_ref, dpre_ref, dws_ref, dbs_ref, dlng_ref, dlnb_ref, dg1_ref,
             dvn_s):
        _zero_at_first_step(dws_ref, dbs_ref, dlng_ref, dlnb_ref, dg1_ref)
        g1 = g1_ref[...]
        yn, xhat, rx = _rms(h_ref[...], g1, D)
        hn_ref[...] = yn.astype(BF16)
        dy = _dot_nt(dh1b_ref[...], _rows_joined(wout_ref))
        pre_u = pre_ref[:, :GH].astype(F32)
        pre_v = pre_ref[:, GH:].astype(F32)
        u, gg_u = _gelu_and_grad(pre_u)
        v, gg_v = _gelu_and_grad(pre_v)
        xc = v - jnp.mean(v, axis=-1, keepdims=True)
        rs = lax.rsqrt(jnp.mean(xc * xc, axis=-1, keepdims=True) + EPS)
        vhat = xc * rs
        lng = lng_ref[...]
        vnb = (vhat * lng + lnb_ref[...]).astype(BF16)
        dsv = dy * u
        dsvb = dsv.astype(BF16)
        tril_m = tril_ref[...]
        for ch in range(TM // GC):
            rows = slice(ch * GC, (ch + 1) * GC)
            dbs_ref[...] += dsv[rows, :]
            for g in range(GG):
                cols = slice(g * GD, (g + 1) * GD)
                sv = _dot(wm_ref[g], vnb[rows, cols]) + b_ref[:, cols]
                dpre_ref[rows, cols] = (dy[rows, cols] * sv * gg_u[rows, cols]).astype(BF16)
                dvn_s[rows, cols] = _dot(wmt_ref[g], dsvb[rows, cols])
                dws_ref[g] += _dot_nt(dsvb[rows, cols], vnb[rows, cols]) * tril_m
        dvn = dvn_s[...]
        _acc_rows(dlng_ref, dvn * vhat)
        _acc_rows(dlnb_ref, dvn)
        dvhat = dvn * lng
        dv = rs * (dvhat - jnp.mean(dvhat, axis=-1, keepdims=True)
                   - vhat * jnp.mean(dvhat * vhat, axis=-1, keepdims=True))
        dpre_v = (dv * gg_v).astype(BF16)
        dpre_ref[:, GH:] = dpre_v
        dhn = _dot_nt(dpre_ref[:, 0:D], win_ref[0])
        for c in range(1, N_CHIPS):
            dhn = dhn + _dot_nt(dpre_ref[:, c * D:(c + 1) * D], win_ref[c])
        _acc_rows(dg1_ref, dhn * xhat)
        dh_ref[...] = dh1_ref[...] + _rms_bwd(dhn, g1, xhat, rx, D)

    return pl.pallas_call(
        body, name="gmlp_bwd", grid=(t // TM,),
        in_specs=[_row(TM, D), _row(TM, D), _row(TM, D), _row(TM, 2 * GH), _const((1, D)), _wblk(D, lay["in"]),
                  _const((1, GH)), _const((1, GH)), _const((GG, GC, GC)), _const((GG, GC, GC)), _const((GC, GH)),
                  _wblk(GH // N_CHIPS, lay["out"]), _const((GC, GC))],
        out_specs=[_row(TM, D), _row(TM, D), _row(TM, 2 * GH), _const((GG, GC, GC)), _const((GC, GH)),
                   _const((8, GH)), _const((8, GH)), _const((8, D))],
        out_shape=[_sds((t, D), F32), _sds((t, D), BF16), _sds((t, 2 * GH), BF16), _sds((GG, GC, GC), F32),
                   _sds((GC, GH), F32), _sds((8, GH), F32), _sds((8, GH), F32), _sds((8, D), F32)],
        scratch_shapes=[pltpu.VMEM((TM, GH), F32)],
        compiler_params=_cp("arbitrary"),
    )(dh1, dh1b, h, pre, g1, allw, lng, lnb, wm, wmt, bfull, allw, tril)


def _token_step(t):
    return 1024 if t % 1024 == 0 else 512


def mm_tn(a, b):
    t, k = a.shape
    n = b.shape[1]
    tk = min(k, 1024)
    tn = min(n, 1024)
    tt = _token_step(t)

    def body(a_ref, b_ref, o_ref):
        @pl.when(pl.program_id(2) == 0)
        def _():
            o_ref[...] = jnp.zeros_like(o_ref)

        o_ref[...] += _dot_tn(a_ref[...].astype(BF16), b_ref[...].astype(BF16))

    return pl.pallas_call(
        body, name="mm_tn", grid=(k // tk, n // tn, t // tt),
        in_specs=[pl.BlockSpec((tt, tk), lambda i, j, s: (s, i)), pl.BlockSpec((tt, tn), lambda i, j, s: (s, j))],
        out_specs=pl.BlockSpec((tk, tn), lambda i, j, s: (i, j)), out_shape=_sds((k, n), F32),
        compiler_params=_cp("parallel", "parallel", "arbitrary"),
    )(a, b)


def mm_tn_into(buf, a, b, rows, row0, col_sharded):
    t = a.shape[0]
    tt = _token_step(t)
    assert row0 % rows == 0 and a.shape[1] == (rows if col_sharded else N_CHIPS * rows), (rows, row0, a.shape)
    assert b.shape[1] == (N_CHIPS * D if col_sharded else D), b.shape
    grid = (1, N_CHIPS, t // tt) if col_sharded else (N_CHIPS, 1, t // tt)
    fresh = isinstance(buf, int)

    def body(*refs):
        a_ref, b_ref, o_ref = refs[-3:]

        @pl.when(pl.program_id(2) == 0)
        def _():
            o_ref[...] = jnp.zeros_like(o_ref)

        o_ref[...] += _dot_tn(a_ref[...].astype(BF16), b_ref[...].astype(BF16))

    specs = [pl.BlockSpec((tt, rows), lambda i, j, s: (s, i)), pl.BlockSpec((tt, D), lambda i, j, s: (s, j))]
    return pl.pallas_call(
        body, name="mm_tn_into", grid=grid,
        in_specs=specs if fresh else [_ANY] + specs,
        out_specs=pl.BlockSpec((None, rows, D), lambda i, j, s: (i + j, row0 // rows, 0)),
        out_shape=_sds((N_CHIPS, buf, D) if fresh else buf.shape, F32),
        input_output_aliases={} if fresh else {0: 0},
        compiler_params=_cp("parallel", "parallel", "arbitrary"),
    )(*((a, b) if fresh else (buf, a, b)))


def adamw(w, g, m, v):
    rows, cols = w.shape
    tr = rows if rows <= 512 else next(r for r in (512, 384, 256, 128) if rows % r == 0)
    c1 = 1.0 - ADAM_B1 ** ADAM_STEP
    c2 = 1.0 - ADAM_B2 ** ADAM_STEP

    def body(w_ref, g_ref, m_ref, v_ref, d_ref, mo_ref, vo_ref):
        gv = g_ref[...]
        mn = ADAM_B1 * m_ref[...] + (1.0 - ADAM_B1) * gv
        vn = ADAM_B2 * v_ref[...] + (1.0 - ADAM_B2) * (gv * gv)
        mo_ref[...] = mn
        vo_ref[...] = vn
        d_ref[...] = -ADAM_LR * ((mn / c1) / (jnp.sqrt(vn / c2) + ADAM_EPS) + ADAM_WD * w_ref[...])

    spec = pl.BlockSpec((tr, cols), lambda i: (i, 0))
    return pl.pallas_call(
        body, name="adamw", grid=(rows // tr,),
        in_specs=[spec] * 4, out_specs=[spec] * 3, out_shape=[_sds((rows, cols), F32)] * 3,
        compiler_params=_cp("parallel"),
    )(w, g, m, v)


def _place():
    return lax.axis_index("x"), lax.axis_index("y"), lax.axis_index("c")


def _other_chips(x, y):
    return [(1 - x, y), (x, 1 - y), (1 - x, 1 - y)]


_ANY = pl.BlockSpec(memory_space=pl.ANY)


_HBM = pl.BlockSpec(memory_space=pltpu.HBM)
_SEM = pl.BlockSpec(memory_space=pltpu.SEMAPHORE)
_EFFECT = pltpu.SideEffectType.DATAFLOW_SIDE_EFFECTING
N_ICI = 3


def _exchange_start(name, src, land, copies, n):
    def body(src_ref, land_ref, *outs):
        sems, token = outs[:2 * n], outs[-1]
        for j, (s, d, to) in enumerate(copies(src_ref, land_ref, _place())):
            pltpu.make_async_remote_copy(src_ref=s, dst_ref=d, send_sem=sems[j], recv_sem=sems[n + j],
                                         device_id=to, device_id_type=MESH).start()
        token[...] = jnp.zeros_like(token)

    sem = pltpu.SemaphoreType.DMA(())
    outs = pl.pallas_call(
        body, name=name,
        out_shape=(sem,) * (2 * n) + (pltpu.HBM(src.shape, src.dtype), pltpu.HBM(land.shape, land.dtype),
                                      _sds((8, LANES), F32)),
        in_specs=(_HBM, _HBM),
        out_specs=(_SEM,) * (2 * n) + (_HBM, _HBM, pl.BlockSpec(memory_space=pltpu.VMEM)),
        input_output_aliases={0: 2 * n, 1: 2 * n + 1},
        compiler_params=pltpu.CompilerParams(has_side_effects=_EFFECT),
    )(pltpu.with_memory_space_constraint(src, pltpu.HBM), pltpu.with_memory_space_constraint(land, pltpu.HBM))
    return outs[:2 * n], outs[2 * n], outs[2 * n + 1], outs[-1]


def _exchange_wait(name, sems, src, land, after, arrivals):
    n = len(sems) // 2

    def body(src_ref, land_ref, *rest):
        sems = rest[:2 * n]
        for j, (s, d) in enumerate(arrivals(src_ref, land_ref, _place())):
            cp = pltpu.make_async_remote_copy(src_ref=s, dst_ref=d, send_sem=sems[j], recv_sem=sems[n + j],
                                              device_id=_place(), device_id_type=MESH)
            cp.wait_send()
            cp.wait_recv()

    return pl.pallas_call(
        body, name=name, out_shape=(pltpu.HBM(src.shape, src.dtype), pltpu.HBM(land.shape, land.dtype)),
        in_specs=(_HBM, _HBM) + (_SEM,) * (2 * n) + (_ANY,), out_specs=(_HBM, _HBM),
        input_output_aliases={0: 0, 1: 1},
        compiler_params=pltpu.CompilerParams(has_side_effects=_EFFECT),
    )(src, land, *sems, after)


def _halves(c, hh):
    return pl.ds(pl.multiple_of(c * hh, 16), hh), pl.ds(pl.multiple_of((1 - c) * hh, 16), hh)


def gather_start(land, tag):
    _, rr, _ = land.shape
    assert rr % 32 == 0, rr

    def copies(_, land_ref, place):
        x, y, c = place
        mine = land_ref.at[2 * x + y, _halves(c, rr // 2)[0]]
        return [(mine, mine, (cx, cy, c)) for cx, cy in _other_chips(x, y)]

    return _exchange_start(f"gather_start_{tag}", jnp.zeros((8, LANES), F32), land, copies, N_ICI)


def gather_wait(sems, src, land, after, tag):
    def arrivals(_, land_ref, place):
        x, y, c = place
        half = _halves(c, land.shape[1] // 2)[0]
        return [(land_ref.at[2 * x + y, half], land_ref.at[2 * cx + cy, half]) for cx, cy in _other_chips(x, y)]

    return _exchange_wait(f"gather_wait_{tag}", sems, src, land, after, arrivals)


def pass_start(land, tag):
    def copies(_, land_ref, place):
        x, y, c = place
        half = _halves(c, land.shape[1] // 2)[0]
        return [(land_ref.at[2 * cx + cy, half], land_ref.at[2 * cx + cy, half], (x, y, 1 - c))
                for cx, cy in _other_chips(x, y)]

    return _exchange_start(f"pass_start_{tag}", jnp.zeros((8, LANES), F32), land, copies, N_ICI)


def pass_wait(sems, src, land, after, tag):
    def arrivals(_, land_ref, place):
        x, y, c = place
        mine, other = _halves(c, land.shape[1] // 2)
        return [(land_ref.at[2 * cx + cy, mine], land_ref.at[2 * cx + cy, other]) for cx, cy in _other_chips(x, y)]

    return _exchange_wait(f"pass_wait_{tag}", sems, src, land, after, arrivals)


def swap_start(g, tag):
    _, rr, cc = g.shape

    def copies(g_ref, got_ref, place):
        x, y, c = place
        other = _halves(c, rr // 2)[1]
        return [(g_ref.at[k, other], got_ref.at[k], (x, y, 1 - c)) for k in range(N_CHIPS)]

    return _exchange_start(f"swap_start_{tag}", g, lax.empty((N_CHIPS, rr // 2, cc), g.dtype), copies, N_CHIPS)


def swap_wait(sems, g, got, after, tag):
    def arrivals(g_ref, got_ref, place):
        other = _halves(place[2], g.shape[1] // 2)[1]
        return [(g_ref.at[k, other], got_ref.at[k]) for k in range(N_CHIPS)]

    return _exchange_wait(f"swap_wait_{tag}", sems, g, got, after, arrivals)


def chip_sum(place, g32, got):
    _, rr, cc = g32.shape
    hh = rr // 2
    tr = SUM_ROWS
    assert rr % 2 == 0 and hh % tr == 0, (rr, tr)
    nb = hh // tr

    def body(place_ref, g_ref, got_ref, own_ref, all_ref):
        s = g_ref[...] + got_ref[...].astype(F32)
        all_ref[...] = s.astype(BF16)
        own_ref[...] = g_ref[place_ref[1]] + got_ref[place_ref[1]].astype(F32)

    return pl.pallas_call(
        body, name="chip_sum",
        grid_spec=pltpu.PrefetchScalarGridSpec(
            num_scalar_prefetch=1, grid=(nb,),
            in_specs=[pl.BlockSpec((N_CHIPS, tr, cc), lambda i, pr: (0, pr[0] * nb + i, 0)),
                      pl.BlockSpec((N_CHIPS, tr, cc), lambda i, pr: (0, i, 0))],
            out_specs=[pl.BlockSpec((tr, cc), lambda i, pr: (i, 0)),
                       pl.BlockSpec((N_CHIPS, tr, cc), lambda i, pr: (0, i, 0))]),
        out_shape=[_sds((hh, cc), F32), _sds((N_CHIPS, hh, cc), BF16)],
        compiler_params=_cp("parallel"),
    )(place, g32, got)


def _scatter_copies(s_ref, land_ref, place):
    x, y, c = place
    return [(s_ref.at[2 * cx + cy], land_ref.at[j], (cx, cy, c)) for j, (cx, cy) in enumerate(_other_chips(x, y))]


def scatter_start(s, tag):
    return _exchange_start(f"scatter_start_{tag}", s, lax.empty((N_ICI,) + s.shape[1:], s.dtype), _scatter_copies, N_ICI)


def scatter_wait(sems, s, land, after, tag):
    return _exchange_wait(f"scatter_wait_{tag}", sems, s, land, after,
                          lambda s_ref, land_ref, place: [(a, b) for a, b, _ in _scatter_copies(s_ref, land_ref, place)])


def final_sum(place, own, got):
    hh, cc = own.shape
    tr = SUM_ROWS
    assert hh % tr == 0, (hh, tr)
    nb = hh // tr

    def body(place_ref, own_ref, got_ref, o_ref):
        del place_ref
        o_ref[...] = ((own_ref[...] + got_ref[0].astype(F32)) + got_ref[1].astype(F32)) + got_ref[2].astype(F32)

    return pl.pallas_call(
        body, name="final_sum",
        grid_spec=pltpu.PrefetchScalarGridSpec(
            num_scalar_prefetch=1, grid=(nb,),
            in_specs=[pl.BlockSpec((tr, cc), lambda i, pr: (i, 0)), pl.BlockSpec((3, tr, cc), lambda i, pr: (0, i, 0))],
            out_specs=pl.BlockSpec((tr, cc), lambda i, pr: (pr[0] * nb + i, 0))),
        out_shape=_sds((2 * hh, cc), F32),
        compiler_params=_cp("parallel"),
    )(place, own, got)


def share_start(f, tag):
    def copies(_, f_ref, place):
        x, y, c = place
        mine = f_ref.at[_halves(c, f.shape[0] // 2)[0]]
        return [(mine, mine, (x, y, 1 - c))]

    return _exchange_start(f"share_start_{tag}", jnp.zeros((8, LANES), F32), f, copies, 1)


def share_wait(sems, src, f, after, tag):
    def arrivals(_, f_ref, place):
        mine, other = _halves(place[2], f.shape[0] // 2)
        return [(f_ref.at[mine], f_ref.at[other])]

    return _exchange_wait(f"share_wait_{tag}", sems, src, f, after, arrivals)


N_DEV = 8


def _peers(place):
    x, y, c = place
    return [((1 - x) if r & 4 else x, (1 - y) if r & 2 else y, (1 - c) if r & 1 else c) for r in range(1, N_DEV)]


def _device_index(place):
    x, y, c = place
    return 4 * x + 2 * y + c


def small_start(land):
    def copies(_, land_ref, place):
        mine = land_ref.at[_device_index(place)]
        return [(mine, mine, to) for to in _peers(place)]

    return _exchange_start("small_start", jnp.zeros((8, LANES), F32), land, copies, N_DEV - 1)


def small_wait(sems, src, land, after):
    def arrivals(_, land_ref, place):
        return [(land_ref.at[_device_index(place)], land_ref.at[_device_index(peer)]) for peer in _peers(place)]

    return _exchange_wait("small_wait", sems, src, land, after, arrivals)


def sum_devices(land):
    _, rr, cc = land.shape
    tr = 56
    assert rr % tr == 0, rr

    def body(l_ref, o_ref):
        acc = l_ref[0]
        for d in range(1, N_DEV):
            acc = acc + l_ref[d]
        o_ref[...] = acc

    return pl.pallas_call(
        body, name="sum_devices", grid=(rr // tr,),
        in_specs=[pl.BlockSpec((N_DEV, tr, cc), lambda i: (0, i, 0))],
        out_specs=pl.BlockSpec((tr, cc), lambda i: (i, 0)), out_shape=_sds((rr, cc), F32),
        compiler_params=_cp("parallel"),
    )(land)


_BIG = ["mla_w_down", "mla_w_uq", "mla_w_ukv", "mla_w_out", "gmlp_w_in", "gmlp_w_out", "ffn_w_up", "ffn_w_down",
        "ple_w_gate", "ple_w_proj"]
_SMALL = ["norm_mix", "norm_ffn", "norm_ple", "mla_q_lora_g", "mla_kv_lora_g", "mla_q_nope_g", "mla_q_rope_g",
          "mla_k_nope_g", "mla_k_rope_g", "gmlp_ln_g", "gmlp_ln_b", "gmlp_w_s", "gmlp_b_s"]

_LAY_MLA = dict(up=0, down=1024, out=2048, gate=2304, rows=2560)
_LAY_MLA_ODD = dict(wdn=0, wuq=176, wukv=320, proj=448, rows=512)
_LAY_GMLP = {"up": 0, "down": 1024, "in": 2048, "out": 3072, "gate": 3584, "proj": 3840, "ln": 3904, "rows": 4096}


def _layer_units(i):
    j = i // 2
    if i % 2 == 0:
        odd, lay = _LAY_MLA_ODD, _LAY_MLA
        return [("odd", odd, [("mla_w_down", j, odd["wdn"]), ("mla_w_uq", j, odd["wuq"]), ("mla_w_ukv", j, odd["wukv"]),
                              ("ple_w_proj", i, odd["proj"])]),
                ("main", lay, [("ffn_w_up", i, lay["up"]), ("ffn_w_down", i, lay["down"]), ("mla_w_out", j, lay["out"]),
                               ("ple_w_gate", i, lay["gate"])])]
    lay = _LAY_GMLP
    return [("main", lay, [("ffn_w_up", i, lay["up"]), ("ffn_w_down", i, lay["down"]), ("gmlp_w_in", j, lay["in"]),
                           ("gmlp_w_out", j, lay["out"]), ("ple_w_gate", i, lay["gate"]),
                           ("ple_w_proj", i, lay["proj"])])]


def _pack_rows(parts, dtype, pad_to=None, slot=False):
    size = sum(p.size for p in parts)
    tail = [] if pad_to is None or pad_to * D == size else [jnp.zeros((pad_to * D - size,), dtype)]
    shape = (1, -1, D) if slot else (-1, D)
    if all(p.size % D == 0 for p in parts + tail):
        return jnp.concatenate([p.astype(dtype).reshape(shape) for p in parts + tail], axis=len(shape) - 2)
    return jnp.concatenate([p.astype(dtype).reshape(-1) for p in parts + tail]).reshape(shape)


def _odd(allw, row0, a, b):
    return allw[:, row0:row0 + a * b // D].reshape(N_CHIPS, a, b)


def _cols_joined(s):
    return jnp.transpose(s, (1, 0, 2)).reshape(s.shape[1], N_CHIPS * s.shape[2])


def _col_shards(full):
    a, bb = full.shape
    return jnp.transpose(full.reshape(a, N_CHIPS, bb // N_CHIPS), (1, 0, 2)).reshape(N_CHIPS, -1, D)


def _pad_lanes(g):
    return jnp.pad(g, ((0, 0), (0, LANES - g.shape[1])))


def _split_uq(wuq):
    l = wuq.shape[0]
    w = wuq.reshape(l, QL, HEADS, DN + DR)
    nope = w[..., :DN].reshape(l, QL, HEADS * DN)
    rope = jnp.pad(w[..., DN:], ((0, 0), (0, 0), (0, 0), (0, LANES - DR))).reshape(l, QL, HEADS * LANES)
    return jnp.concatenate([nope, rope], axis=-1)


def _merge_uq(d):
    nope = d[:, :HEADS * DN].reshape(QL, HEADS, DN)
    rope = d[:, HEADS * DN:].reshape(QL, HEADS, LANES)[..., :DR]
    return jnp.concatenate([nope, rope], axis=-1).reshape(QL, HEADS * (DN + DR))


def _rope_tables(positions):
    inv_freq = ROPE_BASE ** (-(jnp.arange(0, DR, 2, dtype=F32) / DR))
    ang = positions.reshape(-1).astype(F32)[:, None] * inv_freq
    z = jnp.zeros((ang.shape[0], LANES - DR), F32)
    return (jnp.concatenate([jnp.cos(ang), jnp.cos(ang), z], axis=1),
            jnp.concatenate([jnp.sin(ang), jnp.sin(ang), z], axis=1))


def kernel(x, p, positions, norm_mix, norm_ffn, norm_ple, mla_w_down, mla_q_lora_g, mla_kv_lora_g, mla_w_uq, mla_w_ukv, mla_q_nope_g, mla_q_rope_g, mla_k_nope_g, mla_k_rope_g, mla_w_out, gmlp_w_in, gmlp_ln_g, gmlp_ln_b, gmlp_w_s, gmlp_b_s, gmlp_w_out, ffn_w_up, ffn_w_down, ple_w_gate, ple_w_proj, loss_target, m_norm_mix, m_norm_ffn, m_norm_ple, m_mla_w_down, m_mla_q_lora_g, m_mla_kv_lora_g, m_mla_w_uq, m_mla_w_ukv, m_mla_q_nope_g, m_mla_q_rope_g, m_mla_k_nope_g, m_mla_k_rope_g, m_mla_w_out, m_gmlp_w_in, m_gmlp_ln_g, m_gmlp_ln_b, m_gmlp_w_s, m_gmlp_b_s, m_gmlp_w_out, m_ffn_w_up, m_ffn_w_down, m_ple_w_gate, m_ple_w_proj, v_norm_mix, v_norm_ffn, v_norm_ple, v_mla_w_down, v_mla_q_lora_g, v_mla_kv_lora_g, v_mla_w_uq, v_mla_w_ukv, v_mla_q_nope_g, v_mla_q_rope_g, v_mla_k_nope_g, v_mla_k_rope_g, v_mla_w_out, v_gmlp_w_in, v_gmlp_ln_g, v_gmlp_ln_b, v_gmlp_w_s, v_gmlp_b_s, v_gmlp_w_out, v_ffn_w_up, v_ffn_w_down, v_ple_w_gate, v_ple_w_proj):
    args = dict(locals())
    weights = {n: args[n] for n in _BIG + _SMALL}
    depth = norm_mix.shape[0]
    nb, seq, _ = x.shape
    t = nb * seq
    assert seq % TQ == 0 and seq % TM == 0 and t % 512 == 0, (nb, seq)
    cx = lax.axis_index("x")
    cy = lax.axis_index("y")
    cc = lax.axis_index("c")
    chip = 2 * cx + cy

    gathers = {}
    token = None
    for i in range(depth):
        for key, lay, parts in _layer_units(i):
            rows = [weights[n][l] for n, l, _ in parts]
            if token is not None:
                rows[0] = rows[0] + token[0, 0]
            if "ln" in lay:
                ln = jnp.stack([gmlp_ln_g[i // 2], gmlp_ln_b[i // 2]]).astype(F32)
                bits = lax.bitcast_convert_type(ln, BF16).reshape(-1)
                rows.append(jnp.pad(bits, (0, 16 * D - bits.size)).reshape(16, D))
            mine = _pack_rows(rows, BF16, pad_to=lay["rows"], slot=True)
            land = lax.dynamic_update_slice(lax.empty((N_CHIPS, lay["rows"], D), BF16), mine, (chip, 0, 0))
            sems, src, land, token = gather_start(land, f"{i}{key}")
            gathers[i, key] = (sems, src, land)
    allw = [None] * depth

    tril = jnp.tril(jnp.ones((GC, GC), F32))
    wm = (gmlp_w_s * tril).astype(BF16)
    wmt = jnp.swapaxes(wm, -1, -2)
    bfull = jnp.repeat(jnp.swapaxes(gmlp_b_s, -1, -2), GD, axis=-1)
    cos, sin = _rope_tables(positions)
    row = lambda g: g.reshape(1, -1)
    gqr = _pad_lanes(mla_q_rope_g)
    gkr = _pad_lanes(mla_k_rope_g)

    h = x.reshape(t, D)
    pt = p.reshape(depth, t, PLE)
    saved = []

    passing = {}

    def arrive(i, key, after):
        sems, src, land = gathers[i, key]
        _, land = gather_wait(sems, src, land, after, f"{i}{key}")
        passing[i, key] = pass_start(land, f"{i}{key}")
        return passing[i, key][3]

    def needed(i, key, after=None):
        sems, src, land, tok = passing.pop((i, key))
        return pass_wait(sems, src, land, tok if after is None else after, f"{i}{key}")[1]

    arrive(0, _layer_units(0)[0][0], token)
    for i in range(depth):
        j = i // 2
        lay = _layer_units(i)[-1][1]
        s = dict(h=h)
        if i % 2 == 0:
            odd = needed(i, "odd", None if i == 0 else h)
            wdn = jnp.pad(_odd(odd, _LAY_MLA_ODD["wdn"], D // N_CHIPS, LAT).reshape(D, LAT), ((0, 0), (0, LATP - LAT)))
            wuq = _split_uq(_cols_joined(_odd(odd, _LAY_MLA_ODD["wuq"], QL, 384))[None])[0]
            wukv = _cols_joined(_odd(odd, _LAY_MLA_ODD["wukv"], KVL, 512))
            wp = _cols_joined(_odd(odd, _LAY_MLA_ODD["proj"], PLE, 256))
            mla_args = (row(norm_mix[i]), wdn, row(mla_q_lora_g[j]), row(mla_kv_lora_g[j]), wuq, wukv,
                        row(mla_q_nope_g[j]), gqr[j:j + 1], row(mla_k_nope_g[j]), gkr[j:j + 1], cos, sin)
            q, k, v = mla_pre_fwd(h, *mla_args)
            y, lse = flash_fwd(q, k, v, seq)
            if i == 0:
                arrive(i, "main", y)
            aw = needed(i, "main", y)
            s.update(q=q, k=k, v=v, lse=lse, mla_args=mla_args)
        else:
            aw = needed(i, "main", h)
            ln = lax.bitcast_convert_type(aw[:, lay["ln"]:lay["ln"] + 2].reshape(N_CHIPS, 2, GH // N_CHIPS, 2), F32)
            ln = jnp.transpose(ln, (1, 0, 2)).reshape(2, 1, GH)
            wp = _cols_joined(_odd(aw, lay["proj"], PLE, 256))
            y, pre = gmlp_fwd(h, row(norm_mix[i]), aw, lay, ln[0], ln[1], wm[j], bfull[j])
            s.update(pre=pre, ln=ln)
        allw[i] = aw
        g2 = row(norm_ffn[i])
        if i + 1 < depth:
            for key, _, _ in _layer_units(i + 1):
                g2 = g2 + arrive(i + 1, key, y)[0:1, 0:1]
        h1, h2, hn2, r = mixffn_fwd(h, y, aw, lay, g2)
        h, hn3 = ple_fwd(h2, pt[i], row(norm_ple[i]), aw, lay, wp)
        s.update(y=y, wp=wp, h1=h1, h2=h2, hn2=hn2, r=r, hn3=hn3)
        saved.append(s)

    dh, loss_part = loss_head(h, loss_target.reshape(t, D))
    loss = lax.psum(loss_part[0, 0], ("x", "y", "c"))

    gs = {n: [None] * weights[n].shape[0] for n in _SMALL}
    gw = {n: [None] * weights[n].shape[0] for n in _BIG}
    place = jnp.stack([cc, chip]).astype(jnp.int32)
    scatters = []
    swaps = []
    token = None

    def put(b, row0, shards):
        return lax.dynamic_update_slice(b, shards.reshape(N_CHIPS, -1, D), (0, row0, 0))

    def swap(i, key, buf):
        sems, buf, got, tok = swap_start(buf, f"{i}{key}")
        swaps.append((i, key, sems, buf, got))
        return tok

    def swapped(after, zero):
        while swaps:
            i, key, sems, g, got = swaps.pop(0)
            g, got = swap_wait(sems, g, got, after, f"{i}{key}")
            own, sums = chip_sum(place, g, got)
            sems, sums, land, tok = scatter_start(sums, f"{i}{key}")
            scatters.append((i, key, own, sems, sums, land))
            zero = zero + tok[0:1, 0:1]
        return zero

    for i in reversed(range(depth)):
        j = i // 2
        lay = _layer_units(i)[-1][1]
        aw = allw[i]
        s = saved[i]

        g3 = row(norm_ple[i])
        if token is not None:
            g3 = g3 + token[0:1, 0:1]
        dh2, dh2b, dgt, dpp, dg3 = ple_bwd(dh, s["h2"], pt[i], g3, aw, lay, s["wp"])
        gs["norm_ple"][i] = dg3[0]
        buf = mm_tn_into(lay["rows"], s["hn3"], dgt, D // N_CHIPS, lay["gate"], False)
        dproj = _col_shards(mm_tn(pt[i], dpp))
        if "ln" in lay:
            buf = put(buf, lay["ln"], jnp.zeros((N_CHIPS, lay["rows"] - lay["ln"], D), F32))
            buf = put(buf, lay["proj"], dproj)
        dh1, dh1b, du, a, dg2 = ffn_bwd(dh2, dh2b, s["h1"], s["r"], row(norm_ffn[i]), aw, lay)
        gs["norm_ffn"][i] = dg2[0]
        buf = mm_tn_into(buf, a, dh2b, D, lay["down"], False)
        buf = mm_tn_into(buf, s["hn2"], du, D, lay["up"], True)
        buf = mm_tn_into(buf, s["y"], dh1b, s["y"].shape[1] // N_CHIPS, lay["out"], False)
        g1 = swapped(dh1, row(norm_mix[i]))
        if i % 2 == 0:
            g1 = g1 + swap(i, "main", buf)[0:1, 0:1]
            do = linear_nt(dh1b, aw, D // N_CHIPS, lay["out"])
            dq, dk, dv = flash_bwd(s["q"], s["k"], s["v"], s["y"], do, s["lse"], seq)
            g1 = swapped(dq, g1)
            (dh, hn1, cq, ckv, dqp, dkvp, dlat, dg1, dgq, dgkv, dgqn, dgqr, dgkn, dgkr) = mla_pre_bwd(
                dq, dk, dv, dh1, s["h"], g1, *s["mla_args"][1:])
            gs["norm_mix"][i] = dg1[0]
            gs["mla_q_lora_g"][j] = dgq[0]
            gs["mla_kv_lora_g"][j] = dgkv[0]
            gs["mla_q_nope_g"][j] = dgqn[0]
            gs["mla_q_rope_g"][j] = dgqr[0, :DR]
            gs["mla_k_nope_g"][j] = dgkn[0]
            gs["mla_k_rope_g"][j] = dgkr[0, :DR]
            buf = jnp.concatenate([mm_tn(hn1, dlat)[:, :LAT].reshape(N_CHIPS, -1, D), _col_shards(_merge_uq(mm_tn(cq, dqp))),
                                   _col_shards(mm_tn(ckv, dkvp)), dproj], axis=1)
            key = "odd"
        else:
            dh, hn1, dpre, dws, dbs, dlng, dlnb, dg1 = gmlp_bwd(
                dh1, dh1b, s["h"], s["pre"], g1, aw, lay, s["ln"][0], s["ln"][1], wm[j], wmt[j], bfull[j], tril)
            gs["norm_mix"][i] = dg1[0]
            gs["gmlp_ln_g"][j] = dlng[0]
            gs["gmlp_ln_b"][j] = dlnb[0]
            gs["gmlp_w_s"][j] = dws
            gs["gmlp_b_s"][j] = jnp.sum(dbs.reshape(GC, GG, GD), axis=-1).T
            buf = mm_tn_into(buf, hn1, dpre, D, lay["in"], True)
            key = "main"
        token = swap(i, key, buf)
    swapped(dh, jnp.zeros((1, 1), F32))
    grad_x = dh.reshape(x.shape)

    small_sizes = [weights[n].size if n not in ("gmlp_ln_g", "gmlp_ln_b") else weights[n].shape[0] * GH
                   for n in _SMALL]
    small_rows = -(-sum(small_sizes) // (56 * D)) * 56
    part = _pack_rows([jnp.stack(gs[n]) for n in _SMALL], F32, pad_to=small_rows, slot=True)
    land = lax.dynamic_update_slice(lax.empty((N_DEV, small_rows, D), F32), part, (2 * chip + cc, 0, 0))
    small = small_start(land)

    after = small[3]
    shares = []
    for i, key, own, sems, sums, land in scatters:
        _, got = scatter_wait(sems, sums, land, after, f"{i}{key}")
        sems, src, full, after = share_start(final_sum(place, own, got), f"{i}{key}")
        shares.append((i, key, sems, src, full))
    for i, key, sems, src, full in shares:
        _, after = share_wait(sems, src, full, after, f"{i}{key}")
        for n, l, row0 in dict((k, parts) for k, _, parts in _layer_units(i))[key]:
            gw[n][l] = after[row0:row0 + weights[n][l].size // D].reshape(weights[n].shape[1:])
    grads = {n: jnp.stack(gw[n]) for n in _BIG}

    tot = sum_devices(small_wait(small[0], small[1], small[2], after)[1]).reshape(-1)
    off = 0
    for n, sz in zip(_SMALL, small_sizes):
        gsum = tot[off:off + sz]
        off += sz
        if n in ("gmlp_ln_g", "gmlp_ln_b"):
            gsum = lax.dynamic_slice_in_dim(gsum.reshape(-1, GH), chip * (GH // N_CHIPS), GH // N_CHIPS, axis=1)
        grads[n] = gsum.reshape(weights[n].shape)

    delta, new_m, new_v = {}, {}, {}
    for n in _BIG:
        w2 = weights[n].reshape(-1, weights[n].shape[-1])
        d, mn, vn = adamw(w2, grads[n].reshape(w2.shape), args["m_" + n].reshape(w2.shape),
                          args["v_" + n].reshape(w2.shape))
        delta[n], new_m[n], new_v[n] = (a.reshape(weights[n].shape) for a in (d, mn, vn))
    own_sizes = [weights[n].size for n in _SMALL]
    own_rows = -(-sum(own_sizes) // (8 * D)) * 8
    packed = [_pack_rows([src[n] for n in _SMALL], F32, pad_to=own_rows)
              for src in (weights, grads, {n: args["m_" + n] for n in _SMALL}, {n: args["v_" + n] for n in _SMALL})]
    outs = adamw(*packed)
    off = 0
    for n, sz in zip(_SMALL, own_sizes):
        for dst, o in zip((delta, new_m, new_v), outs):
            dst[n] = o.reshape(-1)[off:off + sz].reshape(weights[n].shape)
        off += sz

    order = ["norm_mix", "norm_ffn", "norm_ple", "mla_w_down", "mla_q_lora_g", "mla_kv_lora_g", "mla_w_uq",
             "mla_w_ukv", "mla_q_nope_g", "mla_q_rope_g", "mla_k_nope_g", "mla_k_rope_g", "mla_w_out", "gmlp_w_in",
             "gmlp_ln_g", "gmlp_ln_b", "gmlp_w_s", "gmlp_b_s", "gmlp_w_out", "ffn_w_up", "ffn_w_down", "ple_w_gate",
             "ple_w_proj"]
    return (loss, grad_x, *[grads[n] for n in order], *[delta[n] for n in order], *[new_m[n] for n in order],
            *[new_v[n] for n in order])
```

```python
import functools

import jax
import jax.numpy as jnp
from jax import lax
from jax.experimental import pallas as pl
from jax.experimental.pallas import tpu as pltpu

F32 = jnp.float32
BF16 = jnp.bfloat16
MESH = pl.DeviceIdType.MESH

D = 1024
HEADS = 8
DN = 128
DR = 64
QL = 384
KVL = 256
LAT = 704
LATP = 768
DFF = 4096
GH = 2048
GC = 128
GG = 8
GD = 256
PLE = 256
EPS = 1e-6
ROPE_BASE = 10000.0
SM_SCALE = (DN + DR) ** -0.5
N_CHIPS = 4
LANES = 128

ADAM_LR = 0.001
ADAM_B1 = 0.9
ADAM_B2 = 0.999
ADAM_EPS = 1e-08
ADAM_WD = 0.01
ADAM_STEP = 10

TM = 256
TMB = 512
TQ = 512
TQ_FWD = 512
FWD_HEADS = 2
BWD_HEADS = 2
SUM_ROWS = 256
VMEM_LIMIT = 56 * 1024 * 1024


def _cp(*sem):
    return pltpu.CompilerParams(dimension_semantics=sem, vmem_limit_bytes=VMEM_LIMIT)


def _dot(a, b):
    return jnp.dot(a, b, preferred_element_type=F32)


def _dot_nt(a, b):
    return lax.dot_general(a, b, (((1,), (1,)), ((), ())), preferred_element_type=F32)


def _dot_tn(a, b):
    return lax.dot_general(a, b, (((0,), (0,)), ((), ())), preferred_element_type=F32)


def _rms(x, g, n):
    r = lax.rsqrt(jnp.sum(x * x, axis=-1, keepdims=True) * (1.0 / n) + EPS)
    xhat = x * r
    return xhat * g, xhat, r


def _rms_bwd(dy, g, xhat, r, n):
    dxhat = dy * g
    return r * (dxhat - xhat * (jnp.sum(dxhat * xhat, axis=-1, keepdims=True) * (1.0 / n)))


def _rope(x, c, s):
    return x * c + (pltpu.roll(x, 32, 1) - pltpu.roll(x, 96, 1)) * s


def _rope_t(dy, c, s):
    w = dy * s
    return dy * c + pltpu.roll(w, 96, 1) - pltpu.roll(w, 32, 1)


def _sigmoid(x):
    return 1.0 / (1.0 + jnp.exp(-x))


_GELU_K = 0.7978845608028654
_GELU_C = 0.044715


def _gelu(x):
    return 0.5 * x * (1.0 + jnp.tanh(_GELU_K * (x + _GELU_C * x * x * x)))


def _gelu_and_grad(x):
    x2 = x * x
    t = jnp.tanh(_GELU_K * (x + _GELU_C * x2 * x))
    half = 0.5 * (1.0 + t)
    return x * half, half + 0.5 * x * (1.0 - t * t) * (_GELU_K * (1.0 + 3.0 * _GELU_C * x2))


def _acc_rows(ref, val):
    ref[...] += jnp.broadcast_to(jnp.sum(val, axis=0, keepdims=True), ref.shape)


def _row(tm, c):
    return pl.BlockSpec((tm, c), lambda i: (i, 0))


def _const(shape):
    nd = len(shape)
    return pl.BlockSpec(shape, lambda i: (0,) * nd, pipeline_mode=pl.Buffered(1))


def _wblk(rows, row0):
    assert row0 % rows == 0, (rows, row0)
    return pl.BlockSpec((N_CHIPS, rows, D), lambda i: (0, row0 // rows, 0), pipeline_mode=pl.Buffered(1))


def _rows_joined(w_ref):
    return w_ref[...].reshape(N_CHIPS * w_ref.shape[1], D)


def _sds(shape, dtype):
    return jax.ShapeDtypeStruct(shape, dtype)


def mixffn_fwd(h, y, allw, lay, g2):
    t, k = y.shape

    def body(h_ref, y_ref, wo_ref, g_ref, wu_ref, wd_ref, h1_ref, h2_ref, hn_ref, r_ref):
        h1 = h_ref[...] + _dot(y_ref[...], _rows_joined(wo_ref))
        h1_ref[...] = h1
        yn, _, _ = _rms(h1, g_ref[...], D)
        hn = yn.astype(BF16)
        hn_ref[...] = hn
        f = jnp.zeros((TMB, D), F32)
        for c in range(N_CHIPS):
            r = jnp.maximum(_dot(hn, wu_ref[c]), 0.0)
            r_ref[:, c * D:(c + 1) * D] = r.astype(BF16)
            f = f + _dot((r * r).astype(BF16), wd_ref[c])
        h2_ref[...] = h1 + f

    return pl.pallas_call(
        body, name="mixffn_fwd", grid=(t // TMB,),
        in_specs=[_row(TMB, D), _row(TMB, k), _wblk(k // N_CHIPS, lay["out"]), _const((1, D)), _wblk(D, lay["up"]),
                  _wblk(D, lay["down"])],
        out_specs=[_row(TMB, D), _row(TMB, D), _row(TMB, D), _row(TMB, DFF)],
        out_shape=[_sds((t, D), F32), _sds((t, D), F32), _sds((t, D), BF16), _sds((t, DFF), BF16)],
        compiler_params=_cp("parallel"),
    )(h, y, allw, g2, allw, allw)


def ple_fwd(h2, p, g3, allw, lay, wp):
    t = h2.shape[0]

    def body(h_ref, p_ref, g_ref, wg_ref, wp_ref, h3_ref, hn_ref):
        x = h_ref[...]
        yn, _, _ = _rms(x, g_ref[...], D)
        hn = yn.astype(BF16)
        hn_ref[...] = hn
        gt = _dot(hn, _rows_joined(wg_ref))
        pp = _dot(p_ref[...].astype(BF16), wp_ref[...])
        h3_ref[...] = x + _sigmoid(gt) * pp

    return pl.pallas_call(
        body, name="ple_fwd", grid=(t // TMB,),
        in_specs=[_row(TMB, D), _row(TMB, PLE), _const((1, D)), _wblk(D // N_CHIPS, lay["gate"]), _const((PLE, D))],
        out_specs=[_row(TMB, D), _row(TMB, D)],
        out_shape=[_sds((t, D), F32), _sds((t, D), BF16)],
        compiler_params=_cp("parallel"),
    )(h2, p, g3, allw, wp)


def _mla_project(h_ref, g1_ref, wdn_ref, gq_ref, gkv_ref, wuq_ref, wukv_ref):
    x = h_ref[...]
    yn, xhat, rx = _rms(x, g1_ref[...], D)
    hn = yn.astype(BF16)
    lat = _dot(hn, wdn_ref[...])
    cq, cqhat, rq = _rms(lat[:, :QL], gq_ref[...], QL)
    ckv, ckvhat, rkv = _rms(lat[:, QL:QL + KVL], gkv_ref[...], KVL)
    kr_raw = lat[:, QL + KVL:]
    cqb = cq.astype(BF16)
    ckvb = ckv.astype(BF16)
    qp = _dot(cqb, wuq_ref[...])
    kvp = _dot(ckvb, wukv_ref[...])
    return dict(xhat=xhat, rx=rx, hn=hn, cqhat=cqhat, rq=rq, ckvhat=ckvhat, rkv=rkv, kr_raw=kr_raw,
                cqb=cqb, ckvb=ckvb, qp=qp, kvp=kvp)


def mla_pre_fwd(h, g1, wdn, gq, gkv, wuq, wukv, gqn, gqr, gkn, gkr, cos, sin):
    t = h.shape[0]

    def body(h_ref, g1_ref, wdn_ref, gq_ref, gkv_ref, wuq_ref, wukv_ref, gqn_ref, gqr_ref, gkn_ref, gkr_ref,
             c_ref, s_ref, q_ref, k_ref, v_ref):
        m = _mla_project(h_ref, g1_ref, wdn_ref, gq_ref, gkv_ref, wuq_ref, wukv_ref)
        c = c_ref[...]
        s = s_ref[...]
        kr, _, _ = _rms(m["kr_raw"], gkr_ref[...], DR)
        krb = _rope(kr, c, s).astype(BF16)
        for hd in range(HEADS):
            qn, _, _ = _rms(m["qp"][:, hd * DN:(hd + 1) * DN], gqn_ref[...], DN)
            qr, _, _ = _rms(m["qp"][:, D + hd * LANES:D + (hd + 1) * LANES], gqr_ref[...], DR)
            q_ref[hd, :, 0:DN] = (qn * SM_SCALE).astype(BF16)
            q_ref[hd, :, DN:2 * DN] = (_rope(qr, c, s) * SM_SCALE).astype(BF16)
            kn, _, _ = _rms(m["kvp"][:, hd * 2 * DN:hd * 2 * DN + DN], gkn_ref[...], DN)
            k_ref[hd, :, 0:DN] = kn.astype(BF16)
            k_ref[hd, :, DN:2 * DN] = krb
            v_ref[hd] = m["kvp"][:, hd * 2 * DN + DN:(hd + 1) * 2 * DN].astype(BF16)

    hb = lambda w: pl.BlockSpec((HEADS, TM, w), lambda i: (0, i, 0))
    return pl.pallas_call(
        body, name="mla_pre_fwd", grid=(t // TM,),
        in_specs=[_row(TM, D), _const((1, D)), _const((D, LATP)), _const((1, QL)), _const((1, KVL)),
                  _const((QL, 2 * D)), _const((KVL, 2 * D)), _const((1, LANES)), _const((1, LANES)),
                  _const((1, LANES)), _const((1, LANES)), _row(TM, LANES), _row(TM, LANES)],
        out_specs=[hb(2 * DN), hb(2 * DN), hb(DN)],
        out_shape=[_sds((HEADS, t, 2 * DN), BF16), _sds((HEADS, t, 2 * DN), BF16), _sds((HEADS, t, DN), BF16)],
        compiler_params=_cp("parallel"),
    )(h, g1, wdn, gq, gkv, wuq, wukv, gqn, gqr, gkn, gkr, cos, sin)


def _diagonal_mask(n=TQ):
    return lax.broadcasted_iota(jnp.int32, (n, n), 1) <= lax.broadcasted_iota(jnp.int32, (n, n), 0)


def flash_fwd(q, k, v, seq):
    t = q.shape[1]
    nb = t // seq
    tq = TQ_FWD
    nq = seq // tq
    hp = FWD_HEADS

    def body(q_ref, k_ref, v_ref, o_ref, lse_ref):
        qi = pl.program_id(2)
        qs = [q_ref[a] for a in range(hp)]

        def step(j, carry, diagonal=False):
            rows = pl.ds(pl.multiple_of(j * tq, tq), tq)
            out = []
            for a in range(hp):
                m, l, acc = carry[a]
                s = _dot_nt(qs[a], k_ref[a, rows, :])
                if diagonal:
                    s = jnp.where(_diagonal_mask(tq), s, -1e30)
                m_new = jnp.maximum(m, jnp.max(s, axis=-1, keepdims=True))
                p = jnp.exp(s - m_new)
                alpha = jnp.exp(m - m_new)
                l = alpha * l + jnp.sum(p, axis=-1, keepdims=True)
                acc = alpha * acc + _dot(p.astype(BF16), v_ref[a, rows, :])
                out.append((m_new, l, acc))
            return tuple(out)

        one = (jnp.full((tq, 1), -1e30, F32), jnp.zeros((tq, 1), F32), jnp.zeros((tq, DN), F32))
        done = step(qi, lax.fori_loop(0, qi, step, (one,) * hp), diagonal=True)
        for a, (m, l, acc) in enumerate(done):
            o_ref[:, a * DN:(a + 1) * DN] = (acc / l).astype(BF16)
            lse_ref[a] = m + jnp.log(l)

    return pl.pallas_call(
        body, name="flash_fwd", grid=(nb, HEADS // hp, nq),
        in_specs=[pl.BlockSpec((hp, tq, 2 * DN), lambda b, h, i: (h, b * nq + i, 0)),
                  pl.BlockSpec((hp, seq, 2 * DN), lambda b, h, i: (h, b, 0)),
                  pl.BlockSpec((hp, seq, DN), lambda b, h, i: (h, b, 0))],
        out_specs=[pl.BlockSpec((tq, hp * DN), lambda b, h, i: (b * nq + i, h)),
                   pl.BlockSpec((hp, tq, 1), lambda b, h, i: (h, b * nq + i, 0))],
        out_shape=[_sds((t, HEADS * DN), BF16), _sds((HEADS, t, 1), F32)],
        compiler_params=_cp("parallel", "parallel", "arbitrary"),
    )(q, k, v)


def _gmlp_in(hn, win_ref):
    pre = [_dot(hn, win_ref[c]) for c in range(N_CHIPS)]
    return jnp.concatenate(pre[:2], axis=1), jnp.concatenate(pre[2:], axis=1)


def gmlp_fwd(h, g1, allw, lay, lng, lnb, wm, bfull):
    t = h.shape[0]

    def body(h_ref, g1_ref, win_ref, lng_ref, lnb_ref, wm_ref, b_ref, y_ref, pre_ref):
        yn, _, _ = _rms(h_ref[...], g1_ref[...], D)
        pre_u, pre_v = _gmlp_in(yn.astype(BF16), win_ref)
        pre_ref[:, :GH] = pre_u.astype(BF16)
        pre_ref[:, GH:] = pre_v.astype(BF16)
        u = _gelu(pre_u)
        v = _gelu(pre_v)
        xc = v - jnp.mean(v, axis=-1, keepdims=True)
        rs = lax.rsqrt(jnp.mean(xc * xc, axis=-1, keepdims=True) + EPS)
        vnb = (xc * rs * lng_ref[...] + lnb_ref[...]).astype(BF16)
        for ch in range(TM // GC):
            rows = slice(ch * GC, (ch + 1) * GC)
            for g in range(GG):
                cols = slice(g * GD, (g + 1) * GD)
                sv = _dot(wm_ref[g], vnb[rows, cols]) + b_ref[:, cols]
                y_ref[rows, cols] = (u[rows, cols] * sv).astype(BF16)

    return pl.pallas_call(
        body, name="gmlp_fwd", grid=(t // TM,),
        in_specs=[_row(TM, D), _const((1, D)), _wblk(D, lay["in"]), _const((1, GH)), _const((1, GH)),
                  _const((GG, GC, GC)), _const((GC, GH))],
        out_specs=[_row(TM, GH), _row(TM, 2 * GH)],
        out_shape=[_sds((t, GH), BF16), _sds((t, 2 * GH), BF16)],
        compiler_params=_cp("parallel"),
    )(h, g1, allw, lng, lnb, wm, bfull)


def loss_head(h, tgt):
    t = h.shape[0]

    def body(h_ref, t_ref, dh_ref, loss_ref):
        @pl.when(pl.program_id(0) == 0)
        def _():
            loss_ref[...] = jnp.zeros_like(loss_ref)

        e = h_ref[...] - t_ref[...]
        dh_ref[...] = e * (1.0 / D)
        part = jnp.sum(jnp.sum(e * e, axis=-1, keepdims=True), axis=0, keepdims=True) * (0.5 / D)
        loss_ref[...] += jnp.broadcast_to(part, loss_ref.shape)

    return pl.pallas_call(
        body, name="loss_head", grid=(t // TMB,),
        in_specs=[_row(TMB, D), _row(TMB, D)],
        out_specs=[_row(TMB, D), _const((8, LANES))],
        out_shape=[_sds((t, D), F32), _sds((8, LANES), F32)],
        compiler_params=_cp("arbitrary"),
    )(h, tgt)


def _zero_at_first_step(*refs):
    @pl.when(pl.program_id(0) == 0)
    def _():
        for r in refs:
            r[...] = jnp.zeros_like(r)


def ple_bwd(dh3, h2, p, g3, allw, lay, wp):
    t = h2.shape[0]

    def body(dh_ref, h_ref, p_ref, g_ref, wg_ref, wp_ref, dh2_ref, dh2b_ref, dgt_ref, dpp_ref, dg_ref):
        _zero_at_first_step(dg_ref)
        dh3v = dh_ref[...]
        x = h_ref[...]
        g = g_ref[...]
        wg = _rows_joined(wg_ref)
        yn, xhat, r = _rms(x, g, D)
        gt = _dot(yn.astype(BF16), wg)
        pp = _dot(p_ref[...].astype(BF16), wp_ref[...])
        sg = _sigmoid(gt)
        dgt = (dh3v * pp * sg * (1.0 - sg)).astype(BF16)
        dgt_ref[...] = dgt
        dpp_ref[...] = (dh3v * sg).astype(BF16)
        dhn = _dot_nt(dgt, wg)
        _acc_rows(dg_ref, dhn * xhat)
        dh2 = dh3v + _rms_bwd(dhn, g, xhat, r, D)
        dh2_ref[...] = dh2
        dh2b_ref[...] = dh2.astype(BF16)

    return pl.pallas_call(
        body, name="ple_bwd", grid=(t // TMB,),
        in_specs=[_row(TMB, D), _row(TMB, D), _row(TMB, PLE), _const((1, D)), _wblk(D // N_CHIPS, lay["gate"]),
                  _const((PLE, D))],
        out_specs=[_row(TMB, D), _row(TMB, D), _row(TMB, D), _row(TMB, D), _const((8, D))],
        out_shape=[_sds((t, D), F32), _sds((t, D), BF16), _sds((t, D), BF16), _sds((t, D), BF16), _sds((8, D), F32)],
        compiler_params=_cp("arbitrary"),
    )(dh3, h2, p, g3, allw, wp)


def ffn_bwd(dh2, dh2b, h1, r, g2, allw, lay):
    t = h1.shape[0]

    def body(dh_ref, dhb_ref, h_ref, r_ref, g_ref, wu_ref, wd_ref, dh1_ref, dh1b_ref, du_ref, a_ref, dg_ref):
        _zero_at_first_step(dg_ref)
        dhb = dhb_ref[...]
        g = g_ref[...]
        _, xhat, rr = _rms(h_ref[...], g, D)
        dhn = jnp.zeros((TM, D), F32)
        for c in range(N_CHIPS):
            cs = slice(c * D, (c + 1) * D)
            rc = r_ref[:, cs].astype(F32)
            a_ref[:, cs] = (rc * rc).astype(BF16)
            da = _dot_nt(dhb, wd_ref[c])
            du = (da * (2.0 * rc)).astype(BF16)
            du_ref[:, cs] = du
            dhn = dhn + _dot_nt(du, wu_ref[c])
        _acc_rows(dg_ref, dhn * xhat)
        dh1 = dh_ref[...] + _rms_bwd(dhn, g, xhat, rr, D)
        dh1_ref[...] = dh1
        dh1b_ref[...] = dh1.astype(BF16)

    return pl.pallas_call(
        body, name="ffn_bwd", grid=(t // TM,),
        in_specs=[_row(TM, D), _row(TM, D), _row(TM, D), _row(TM, DFF), _const((1, D)), _wblk(D, lay["up"]),
                  _wblk(D, lay["down"])],
        out_specs=[_row(TM, D), _row(TM, D), _row(TM, DFF), _row(TM, DFF), _const((8, D))],
        out_shape=[_sds((t, D), F32), _sds((t, D), BF16), _sds((t, DFF), BF16), _sds((t, DFF), BF16),
                   _sds((8, D), F32)],
        compiler_params=_cp("arbitrary"),
    )(dh2, dh2b, h1, r, g2, allw, allw)


def linear_nt(a, allw, rows, row0):
    t = a.shape[0]
    k = N_CHIPS * rows

    def body(a_ref, w_ref, o_ref):
        o_ref[...] = _dot_nt(a_ref[...], _rows_joined(w_ref)).astype(BF16)

    return pl.pallas_call(
        body, name="linear_nt", grid=(t // TMB,),
        in_specs=[_row(TMB, D), _wblk(rows, row0)],
        out_specs=_row(TMB, k),
        out_shape=_sds((t, k), BF16),
        compiler_params=_cp("parallel"),
    )(a, allw)


def flash_bwd(q, k, v, o, do, lse, seq):
    t = q.shape[1]
    nb = t // seq
    nq = seq // TQ
    hp = BWD_HEADS

    def body(q_ref, k_ref, v_ref, o_ref, do_ref, lse_ref, dq_ref, dk_ref, dv_ref):
        kj = pl.program_id(2)

        @pl.when(kj == 0)
        def _():
            dq_ref[...] = jnp.zeros_like(dq_ref)

        def step(i, carry, diagonal=False):
            rows = pl.ds(pl.multiple_of(i * TQ, TQ), TQ)
            out = []
            for a in range(hp):
                dk, dv = carry[a]
                kv = k_ref[a]
                qv = q_ref[a, rows, :]
                dov = do_ref[rows, a * DN:(a + 1) * DN]
                ov = o_ref[rows, a * DN:(a + 1) * DN]
                delta = jnp.sum(dov.astype(F32) * ov.astype(F32), axis=-1, keepdims=True)
                s = _dot_nt(qv, kv)
                if diagonal:
                    s = jnp.where(_diagonal_mask(), s, -1e30)
                p = jnp.exp(s - lse_ref[a, rows, :])
                dp = _dot_nt(dov, v_ref[a])
                ds = (p * (dp - delta)).astype(BF16)
                dv = dv + _dot_tn(p.astype(BF16), dov)
                dk = dk + _dot_tn(ds, qv)
                dq_ref[a, rows, :] += _dot(ds, kv)
                out.append((dk, dv))
            return tuple(out)

        one = (jnp.zeros((TQ, 2 * DN), F32), jnp.zeros((TQ, DN), F32))
        done = lax.fori_loop(kj + 1, nq, step, step(kj, (one,) * hp, diagonal=True))
        for a, (dk, dv) in enumerate(done):
            dk_ref[a] = dk
            dv_ref[a] = dv

    return pl.pallas_call(
        body, name="flash_bwd", grid=(nb, HEADS // hp, nq),
        in_specs=[pl.BlockSpec((hp, seq, 2 * DN), lambda b, h, j: (h, b, 0)),
                  pl.BlockSpec((hp, TQ, 2 * DN), lambda b, h, j: (h, b * nq + j, 0)),
                  pl.BlockSpec((hp, TQ, DN), lambda b, h, j: (h, b * nq + j, 0)),
                  pl.BlockSpec((seq, hp * DN), lambda b, h, j: (b, h)),
                  pl.BlockSpec((seq, hp * DN), lambda b, h, j: (b, h)),
                  pl.BlockSpec((hp, seq, 1), lambda b, h, j: (h, b, 0))],
        out_specs=[pl.BlockSpec((hp, seq, 2 * DN), lambda b, h, j: (h, b, 0)),
                   pl.BlockSpec((hp, TQ, 2 * DN), lambda b, h, j: (h, b * nq + j, 0)),
                   pl.BlockSpec((hp, TQ, DN), lambda b, h, j: (h, b * nq + j, 0))],
        out_shape=[_sds((HEADS, t, 2 * DN), F32), _sds((HEADS, t, 2 * DN), F32), _sds((HEADS, t, DN), F32)],
        compiler_params=_cp("parallel", "parallel", "arbitrary"),
    )(q, k, v, o, do, lse)


def mla_pre_bwd(dq, dk, dv, dh1, h, g1, wdn, gq, gkv, wuq, wukv, gqn, gqr, gkn, gkr, cos, sin):
    t = h.shape[0]

    def body(dq_ref, dk_ref, dv_ref, dh1_ref, h_ref, g1_ref, wdn_ref, gq_ref, gkv_ref, wuq_ref, wukv_ref,
             gqn_ref, gqr_ref, gkn_ref, gkr_ref, c_ref, s_ref,
             dh_ref, hn_ref, cq_ref, ckv_ref, dqp_ref, dkvp_ref, dlat_ref,
             dg1_ref, dgq_ref, dgkv_ref, dgqn_ref, dgqr_ref, dgkn_ref, dgkr_ref):
        _zero_at_first_step(dg1_ref, dgq_ref, dgkv_ref, dgqn_ref, dgqr_ref, dgkn_ref, dgkr_ref)
        m = _mla_project(h_ref, g1_ref, wdn_ref, gq_ref, gkv_ref, wuq_ref, wukv_ref)
        hn_ref[...] = m["hn"]
        cq_ref[...] = m["cqb"]
        ckv_ref[...] = m["ckvb"]
        c = c_ref[...]
        s = s_ref[...]
        gqn = gqn_ref[...]
        gqr = gqr_ref[...]
        gkn = gkn_ref[...]
        gkr = gkr_ref[...]

        dkr = dk_ref[0, :, DN:2 * DN]
        for hd in range(1, HEADS):
            dkr = dkr + dk_ref[hd, :, DN:2 * DN]
        dkr = _rope_t(dkr, c, s)
        _, krhat, rkr = _rms(m["kr_raw"], gkr, DR)
        _acc_rows(dgkr_ref, dkr * krhat)
        dkr_raw = _rms_bwd(dkr, gkr, krhat, rkr, DR)

        for hd in range(HEADS):
            ncols = slice(hd * DN, (hd + 1) * DN)
            _, xh, r = _rms(m["qp"][:, ncols], gqn, DN)
            dqn = dq_ref[hd, :, 0:DN] * SM_SCALE
            _acc_rows(dgqn_ref, dqn * xh)
            dqp_ref[:, ncols] = _rms_bwd(dqn, gqn, xh, r, DN).astype(BF16)

            rcols = slice(D + hd * LANES, D + (hd + 1) * LANES)
            _, xh, r = _rms(m["qp"][:, rcols], gqr, DR)
            dqr = _rope_t(dq_ref[hd, :, DN:2 * DN] * SM_SCALE, c, s)
            _acc_rows(dgqr_ref, dqr * xh)
            dqp_ref[:, rcols] = _rms_bwd(dqr, gqr, xh, r, DR).astype(BF16)

            kcols = slice(hd * 2 * DN, hd * 2 * DN + DN)
            _, xh, r = _rms(m["kvp"][:, kcols], gkn, DN)
            dkn = dk_ref[hd, :, 0:DN]
            _acc_rows(dgkn_ref, dkn * xh)
            dkvp_ref[:, kcols] = _rms_bwd(dkn, gkn, xh, r, DN).astype(BF16)
            dkvp_ref[:, hd * 2 * DN + DN:(hd + 1) * 2 * DN] = dv_ref[hd].astype(BF16)

        dcq = _dot_nt(dqp_ref[...], wuq_ref[...])
        _acc_rows(dgq_ref, dcq * m["cqhat"])
        dlat_q = _rms_bwd(dcq, gq_ref[...], m["cqhat"], m["rq"], QL)
        dckv = _dot_nt(dkvp_ref[...], wukv_ref[...])
        _acc_rows(dgkv_ref, dckv * m["ckvhat"])
        dlat_kv = _rms_bwd(dckv, gkv_ref[...], m["ckvhat"], m["rkv"], KVL)
        dlat = jnp.concatenate([dlat_q, dlat_kv, dkr_raw], axis=1).astype(BF16)
        dlat_ref[...] = dlat
        dhn = _dot_nt(dlat, wdn_ref[...])
        _acc_rows(dg1_ref, dhn * m["xhat"])
        dh_ref[...] = dh1_ref[...] + _rms_bwd(dhn, g1_ref[...], m["xhat"], m["rx"], D)

    hb = lambda w: pl.BlockSpec((HEADS, TM, w), lambda i: (0, i, 0))
    return pl.pallas_call(
        body, name="mla_pre_bwd", grid=(t // TM,),
        in_specs=[hb(2 * DN), hb(2 * DN), hb(DN), _row(TM, D), _row(TM, D), _const((1, D)), _const((D, LATP)),
                  _const((1, QL)), _const((1, KVL)), _const((QL, 2 * D)), _const((KVL, 2 * D)),
                  _const((1, LANES)), _const((1, LANES)), _const((1, LANES)), _const((1, LANES)),
                  _row(TM, LANES), _row(TM, LANES)],
        out_specs=[_row(TM, D), _row(TM, D), _row(TM, QL), _row(TM, KVL), _row(TM, 2 * D), _row(TM, 2 * D),
                   _row(TM, LATP), _const((8, D)), _const((8, QL)), _const((8, KVL)), _const((8, LANES)),
                   _const((8, LANES)), _const((8, LANES)), _const((8, LANES))],
        out_shape=[_sds((t, D), F32), _sds((t, D), BF16), _sds((t, QL), BF16), _sds((t, KVL), BF16),
                   _sds((t, 2 * D), BF16), _sds((t, 2 * D), BF16), _sds((t, LATP), BF16),
                   _sds((8, D), F32), _sds((8, QL), F32), _sds((8, KVL), F32), _sds((8, LANES), F32),
                   _sds((8, LANES), F32), _sds((8, LANES), F32), _sds((8, LANES), F32)],
        compiler_params=_cp("arbitrary"),
    )(dq, dk, dv, dh1, h, g1, wdn, gq, gkv, wuq, wukv, gqn, gqr, gkn, gkr, cos, sin)


def gmlp_bwd(dh1, dh1b, h, pre, g1, allw, lay, lng, lnb, wm, wmt, bfull, tril):
    t = h.shape[0]

    def body(dh1_ref, dh1b_ref, h_ref, pre_ref, g1_ref, win_ref, lng_ref, lnb_ref, wm_ref, wmt_ref, b_ref,
             wout_ref, tril_ref, dh_ref, hn_ref, dpre_ref, dws_ref, dbs_ref, dlng_ref, dlnb_ref, dg1_ref,
             dvn_s):
        _zero_at_first_step(dws_ref, dbs_ref, dlng_ref, dlnb_ref, dg1_ref)
        g1 = g1_ref[...]
        yn, xhat, rx = _rms(h_ref[...], g1, D)
        hn_ref[...] = yn.astype(BF16)
        dy = _dot_nt(dh1b_ref[...], _rows_joined(wout_ref))
        pre_u = pre_ref[:, :GH].astype(F32)
        pre_v = pre_ref[:, GH:].astype(F32)
        u, gg_u = _gelu_and_grad(pre_u)
        v, gg_v = _gelu_and_grad(pre_v)
        xc = v - jnp.mean(v, axis=-1, keepdims=True)
        rs = lax.rsqrt(jnp.mean(xc * xc, axis=-1, keepdims=True) + EPS)
        vhat = xc * rs
        lng = lng_ref[...]
        vnb = (vhat * lng + lnb_ref[...]).astype(BF16)
        dsv = dy * u
        dsvb = dsv.astype(BF16)
        tril_m = tril_ref[...]
        for ch in range(TM // GC):
            rows = slice(ch * GC, (ch + 1) * GC)
            dbs_ref[...] += dsv[rows, :]
            for g in range(GG):
                cols = slice(g * GD, (g + 1) * GD)
                sv = _dot(wm_ref[g], vnb[rows, cols]) + b_ref[:, cols]
                dpre_ref[rows, cols] = (dy[rows, cols] * sv * gg_u[rows, cols]).astype(BF16)
                dvn_s[rows, cols] = _dot(wmt_ref[g], dsvb[rows, cols])
                dws_ref[g] += _dot_nt(dsvb[rows, cols], vnb[rows, cols]) * tril_m
        dvn = dvn_s[...]
        _acc_rows(dlng_ref, dvn * vhat)
        _acc_rows(dlnb_ref, dvn)
        dvhat = dvn * lng
        dv = rs * (dvhat - jnp.mean(dvhat, axis=-1, keepdims=True)
                   - vhat * jnp.mean(dvhat * vhat, axis=-1, keepdims=True))
        dpre_v = (dv * gg_v).astype(BF16)
        dpre_ref[:, GH:] = dpre_v
        dhn = _dot_nt(dpre_ref[:, 0:D], win_ref[0])
        for c in range(1, N_CHIPS):
            dhn = dhn + _dot_nt(dpre_ref[:, c * D:(c + 1) * D], win_ref[c])
        _acc_rows(dg1_ref, dhn * xhat)
        dh_ref[...] = dh1_ref[...] + _rms_bwd(dhn, g1, xhat, rx, D)

    return pl.pallas_call(
        body, name="gmlp_bwd", grid=(t // TM,),
        in_specs=[_row(TM, D), _row(TM, D), _row(TM, D), _row(TM, 2 * GH), _const((1, D)), _wblk(D, lay["in"]),
                  _const((1, GH)), _const((1, GH)), _const((GG, GC, GC)), _const((GG, GC, GC)), _const((GC, GH)),
                  _wblk(GH // N_CHIPS, lay["out"]), _const((GC, GC))],
        out_specs=[_row(TM, D), _row(TM, D), _row(TM, 2 * GH), _const((GG, GC, GC)), _const((GC, GH)),
                   _const((8, GH)), _const((8, GH)), _const((8, D))],
        out_shape=[_sds((t, D), F32), _sds((t, D), BF16), _sds((t, 2 * GH), BF16), _sds((GG, GC, GC), F32),
                   _sds((GC, GH), F32), _sds((8, GH), F32), _sds((8, GH), F32), _sds((8, D), F32)],
        scratch_shapes=[pltpu.VMEM((TM, GH), F32)],
        compiler_params=_cp("arbitrary"),
    )(dh1, dh1b, h, pre, g1, allw, lng, lnb, wm, wmt, bfull, allw, tril)


def _token_step(t):
    return 1024 if t % 1024 == 0 else 512


def mm_tn(a, b):
    t, k = a.shape
    n = b.shape[1]
    tk = min(k, 1024)
    tn = min(n, 1024)
    tt = _token_step(t)

    def body(a_ref, b_ref, o_ref):
        @pl.when(pl.program_id(2) == 0)
        def _():
            o_ref[...] = jnp.zeros_like(o_ref)

        o_ref[...] += _dot_tn(a_ref[...].astype(BF16), b_ref[...].astype(BF16))

    return pl.pallas_call(
        body, name="mm_tn", grid=(k // tk, n // tn, t // tt),
        in_specs=[pl.BlockSpec((tt, tk), lambda i, j, s: (s, i)), pl.BlockSpec((tt, tn), lambda i, j, s: (s, j))],
        out_specs=pl.BlockSpec((tk, tn), lambda i, j, s: (i, j)), out_shape=_sds((k, n), F32),
        compiler_params=_cp("parallel", "parallel", "arbitrary"),
    )(a, b)


def mm_tn_into(buf, a, b, rows, row0, col_sharded):
    t = a.shape[0]
    tt = _token_step(t)
    assert row0 % rows == 0 and a.shape[1] == (rows if col_sharded else N_CHIPS * rows), (rows, row0, a.shape)
    assert b.shape[1] == (N_CHIPS * D if col_sharded else D), b.shape
    grid = (1, N_CHIPS, t // tt) if col_sharded else (N_CHIPS, 1, t // tt)
    fresh = isinstance(buf, int)

    def body(*refs):
        a_ref, b_ref, o_ref = refs[-3:]

        @pl.when(pl.program_id(2) == 0)
        def _():
            o_ref[...] = jnp.zeros_like(o_ref)

        o_ref[...] += _dot_tn(a_ref[...].astype(BF16), b_ref[...].astype(BF16))

    specs = [pl.BlockSpec((tt, rows), lambda i, j, s: (s, i)), pl.BlockSpec((tt, D), lambda i, j, s: (s, j))]
    return pl.pallas_call(
        body, name="mm_tn_into", grid=grid,
        in_specs=specs if fresh else [_ANY] + specs,
        out_specs=pl.BlockSpec((None, rows, D), lambda i, j, s: (i + j, row0 // rows, 0)),
        out_shape=_sds((N_CHIPS, buf, D) if fresh else buf.shape, F32),
        input_output_aliases={} if fresh else {0: 0},
        compiler_params=_cp("parallel", "parallel", "arbitrary"),
    )(*((a, b) if fresh else (buf, a, b)))


def adamw(w, g, m, v):
    rows, cols = w.shape
    tr = rows if rows <= 512 else next(r for r in (512, 384, 256, 128) if rows % r == 0)
    c1 = 1.0 - ADAM_B1 ** ADAM_STEP
    c2 = 1.0 - ADAM_B2 ** ADAM_STEP

    def body(w_ref, g_ref, m_ref, v_ref, d_ref, mo_ref, vo_ref):
        gv = g_ref[...]
        mn = ADAM_B1 * m_ref[...] + (1.0 - ADAM_B1) * gv
        vn = ADAM_B2 * v_ref[...] + (1.0 - ADAM_B2) * (gv * gv)
        mo_ref[...] = mn
        vo_ref[...] = vn
        d_ref[...] = -ADAM_LR * ((mn / c1) / (jnp.sqrt(vn / c2) + ADAM_EPS) + ADAM_WD * w_ref[...])

    spec = pl.BlockSpec((tr, cols), lambda i: (i, 0))
    return pl.pallas_call(
        body, name="adamw", grid=(rows // tr,),
        in_specs=[spec] * 4, out_specs=[spec] * 3, out_shape=[_sds((rows, cols), F32)] * 3,
        compiler_params=_cp("parallel"),
    )(w, g, m, v)


def _place():
    return lax.axis_index("x"), lax.axis_index("y"), lax.axis_index("c")


def _other_chips(x, y):
    return [(1 - x, y), (x, 1 - y), (1 - x, 1 - y)]


_ANY = pl.BlockSpec(memory_space=pl.ANY)


_HBM = pl.BlockSpec(memory_space=pltpu.HBM)
_SEM = pl.BlockSpec(memory_space=pltpu.SEMAPHORE)
_EFFECT = pltpu.SideEffectType.DATAFLOW_SIDE_EFFECTING
N_ICI = 3


def _exchange_start(name, src, land, copies, n):
    def body(src_ref, land_ref, *outs):
        sems, token = outs[:2 * n], outs[-1]
        for j, (s, d, to) in enumerate(copies(src_ref, land_ref, _place())):
            pltpu.make_async_remote_copy(src_ref=s, dst_ref=d, send_sem=sems[j], recv_sem=sems[n + j],
                                         device_id=to, device_id_type=MESH).start()
        token[...] = jnp.zeros_like(token)

    sem = pltpu.SemaphoreType.DMA(())
    outs = pl.pallas_call(
        body, name=name,
        out_shape=(sem,) * (2 * n) + (pltpu.HBM(src.shape, src.dtype), pltpu.HBM(land.shape, land.dtype),
                                      _sds((8, LANES), F32)),
        in_specs=(_HBM, _HBM),
        out_specs=(_SEM,) * (2 * n) + (_HBM, _HBM, pl.BlockSpec(memory_space=pltpu.VMEM)),
        input_output_aliases={0: 2 * n, 1: 2 * n + 1},
        compiler_params=pltpu.CompilerParams(has_side_effects=_EFFECT),
    )(pltpu.with_memory_space_constraint(src, pltpu.HBM), pltpu.with_memory_space_constraint(land, pltpu.HBM))
    return outs[:2 * n], outs[2 * n], outs[2 * n + 1], outs[-1]


def _exchange_wait(name, sems, src, land, after, arrivals):
    n = len(sems) // 2

    def body(src_ref, land_ref, *rest):
        sems = rest[:2 * n]
        for j, (s, d) in enumerate(arrivals(src_ref, land_ref, _place())):
            cp = pltpu.make_async_remote_copy(src_ref=s, dst_ref=d, send_sem=sems[j], recv_sem=sems[n + j],
                                              device_id=_place(), device_id_type=MESH)
            cp.wait_send()
            cp.wait_recv()

    return pl.pallas_call(
        body, name=name, out_shape=(pltpu.HBM(src.shape, src.dtype), pltpu.HBM(land.shape, land.dtype)),
        in_specs=(_HBM, _HBM) + (_SEM,) * (2 * n) + (_ANY,), out_specs=(_HBM, _HBM),
        input_output_aliases={0: 0, 1: 1},
        compiler_params=pltpu.CompilerParams(has_side_effects=_EFFECT),
    )(src, land, *sems, after)


def _halves(c, hh):
    return pl.ds(pl.multiple_of(c * hh, 16), hh), pl.ds(pl.multiple_of((1 - c) * hh, 16), hh)


def gather_start(land, tag):
    _, rr, _ = land.shape
    assert rr % 32 == 0, rr

    def copies(_, land_ref, place):
        x, y, c = place
        mine = land_ref.at[2 * x + y, _halves(c, rr // 2)[0]]
        return [(mine, mine, (cx, cy, c)) for cx, cy in _other_chips(x, y)]

    return _exchange_start(f"gather_start_{tag}", jnp.zeros((8, LANES), F32), land, copies, N_ICI)


def gather_wait(sems, src, land, after, tag):
    def arrivals(_, land_ref, place):
        x, y, c = place
        half = _halves(c, land.shape[1] // 2)[0]
        return [(land_ref.at[2 * x + y, half], land_ref.at[2 * cx + cy, half]) for cx, cy in _other_chips(x, y)]

    return _exchange_wait(f"gather_wait_{tag}", sems, src, land, after, arrivals)


def pass_start(land, tag):
    def copies(_, land_ref, place):
        x, y, c = place
        half = _halves(c, land.shape[1] // 2)[0]
        return [(land_ref.at[2 * cx + cy, half], land_ref.at[2 * cx + cy, half], (x, y, 1 - c))
                for cx, cy in _other_chips(x, y)]

    return _exchange_start(f"pass_start_{tag}", jnp.zeros((8, LANES), F32), land, copies, N_ICI)


def pass_wait(sems, src, land, after, tag):
    def arrivals(_, land_ref, place):
        x, y, c = place
        mine, other = _halves(c, land.shape[1] // 2)
        return [(land_ref.at[2 * cx + cy, mine], land_ref.at[2 * cx + cy, other]) for cx, cy in _other_chips(x, y)]

    return _exchange_wait(f"pass_wait_{tag}", sems, src, land, after, arrivals)


def swap_start(g, tag):
    _, rr, cc = g.shape

    def copies(g_ref, got_ref, place):
        x, y, c = place
        other = _halves(c, rr // 2)[1]
        return [(g_ref.at[k, other], got_ref.at[k], (x, y, 1 - c)) for k in range(N_CHIPS)]

    return _exchange_start(f"swap_start_{tag}", g, lax.empty((N_CHIPS, rr // 2, cc), g.dtype), copies, N_CHIPS)


def swap_wait(sems, g, got, after, tag):
    def arrivals(g_ref, got_ref, place):
        other = _halves(place[2], g.shape[1] // 2)[1]
        return [(g_ref.at[k, other], got_ref.at[k]) for k in range(N_CHIPS)]

    return _exchange_wait(f"swap_wait_{tag}", sems, g, got, after, arrivals)


def chip_sum(place, g32, got):
    _, rr, cc = g32.shape
    hh = rr // 2
    tr = SUM_ROWS
    assert rr % 2 == 0 and hh % tr == 0, (rr, tr)
    nb = hh // tr

    def body(place_ref, g_ref, got_ref, own_ref, all_ref):
        s = g_ref[...] + got_ref[...].astype(F32)
        all_ref[...] = s.astype(BF16)
        own_ref[...] = g_ref[place_ref[1]] + got_ref[place_ref[1]].astype(F32)

    return pl.pallas_call(
        body, name="chip_sum",
        grid_spec=pltpu.PrefetchScalarGridSpec(
            num_scalar_prefetch=1, grid=(nb,),
            in_specs=[pl.BlockSpec((N_CHIPS, tr, cc), lambda i, pr: (0, pr[0] * nb + i, 0)),
                      pl.BlockSpec((N_CHIPS, tr, cc), lambda i, pr: (0, i, 0))],
            out_specs=[pl.BlockSpec((tr, cc), lambda i, pr: (i, 0)),
                       pl.BlockSpec((N_CHIPS, tr, cc), lambda i, pr: (0, i, 0))]),
        out_shape=[_sds((hh, cc), F32), _sds((N_CHIPS, hh, cc), BF16)],
        compiler_params=_cp("parallel"),
    )(place, g32, got)


def _scatter_copies(s_ref, land_ref, place):
    x, y, c = place
    return [(s_ref.at[2 * cx + cy], land_ref.at[j], (cx, cy, c)) for j, (cx, cy) in enumerate(_other_chips(x, y))]


def scatter_start(s, tag):
    return _exchange_start(f"scatter_start_{tag}", s, lax.empty((N_ICI,) + s.shape[1:], s.dtype), _scatter_copies, N_ICI)


def scatter_wait(sems, s, land, after, tag):
    return _exchange_wait(f"scatter_wait_{tag}", sems, s, land, after,
                          lambda s_ref, land_ref, place: [(a, b) for a, b, _ in _scatter_copies(s_ref, land_ref, place)])


def final_sum(place, own, got):
    hh, cc = own.shape
    tr = SUM_ROWS
    assert hh % tr == 0, (hh, tr)
    nb = hh // tr

    def body(place_ref, own_ref, got_ref, o_ref):
        del place_ref
        o_ref[...] = ((own_ref[...] + got_ref[0].astype(F32)) + got_ref[1].astype(F32)) + got_ref[2].astype(F32)

    return pl.pallas_call(
        body, name="final_sum",
        grid_spec=pltpu.PrefetchScalarGridSpec(
            num_scalar_prefetch=1, grid=(nb,),
            in_specs=[pl.BlockSpec((tr, cc), lambda i, pr: (i, 0)), pl.BlockSpec((3, tr, cc), lambda i, pr: (0, i, 0))],
            out_specs=pl.BlockSpec((tr, cc), lambda i, pr: (pr[0] * nb + i, 0))),
        out_shape=_sds((2 * hh, cc), F32),
        compiler_params=_cp("parallel"),
    )(place, own, got)


def share_start(f, tag):
    def copies(_, f_ref, place):
        x, y, c = place
        mine = f_ref.at[_halves(c, f.shape[0] // 2)[0]]
        return [(mine, mine, (x, y, 1 - c))]

    return _exchange_start(f"share_start_{tag}", jnp.zeros((8, LANES), F32), f, copies, 1)


def share_wait(sems, src, f, after, tag):
    def arrivals(_, f_ref, place):
        mine, other = _halves(place[2], f.shape[0] // 2)
        return [(f_ref.at[mine], f_ref.at[other])]

    return _exchange_wait(f"share_wait_{tag}", sems, src, f, after, arrivals)


N_DEV = 8


def _peers(place):
    x, y, c = place
    return [((1 - x) if r & 4 else x, (1 - y) if r & 2 else y, (1 - c) if r & 1 else c) for r in range(1, N_DEV)]


def _device_index(place):
    x, y, c = place
    return 4 * x + 2 * y + c


def small_start(land):
    def copies(_, land_ref, place):
        mine = land_ref.at[_device_index(place)]
        return [(mine, mine, to) for to in _peers(place)]

    return _exchange_start("small_start", jnp.zeros((8, LANES), F32), land, copies, N_DEV - 1)


def small_wait(sems, src, land, after):
    def arrivals(_, land_ref, place):
        return [(land_ref.at[_device_index(place)], land_ref.at[_device_index(peer)]) for peer in _peers(place)]

    return _exchange_wait("small_wait", sems, src, land, after, arrivals)


def sum_devices(land):
    _, rr, cc = land.shape
    tr = 56
    assert rr % tr == 0, rr

    def body(l_ref, o_ref):
        acc = l_ref[0]
        for d in range(1, N_DEV):
            acc = acc + l_ref[d]
        o_ref[...] = acc

    return pl.pallas_call(
        body, name="sum_devices", grid=(rr // tr,),
        in_specs=[pl.BlockSpec((N_DEV, tr, cc), lambda i: (0, i, 0))],
        out_specs=pl.BlockSpec((tr, cc), lambda i: (i, 0)), out_shape=_sds((rr, cc), F32),
        compiler_params=_cp("parallel"),
    )(land)


_BIG = ["mla_w_down", "mla_w_uq", "mla_w_ukv", "mla_w_out", "gmlp_w_in", "gmlp_w_out", "ffn_w_up", "ffn_w_down",
        "ple_w_gate", "ple_w_proj"]
_SMALL = ["norm_mix", "norm_ffn", "norm_ple", "mla_q_lora_g", "mla_kv_lora_g", "mla_q_nope_g", "mla_q_rope_g",
          "mla_k_nope_g", "mla_k_rope_g", "gmlp_ln_g", "gmlp_ln_b", "gmlp_w_s", "gmlp_b_s"]

_LAY_MLA = dict(up=0, down=1024, out=2048, gate=2304, wdn=2560, wuq=2736, wukv=2880, proj=3008, rows=3072)
_LAY_MLA_MAIN = dict(up=0, down=1024, out=2048, gate=2304, rows=2560)
_LAY_MLA_ODD = dict(wdn=0, wuq=176, wukv=320, proj=448, rows=512)
_LAY_GMLP = {"up": 0, "down": 1024, "in": 2048, "out": 3072, "gate": 3584, "proj": 3840, "ln": 3904, "rows": 4096}
SPLIT_LAYERS = (0,)


def _layer_units(i):
    j = i // 2
    if i % 2 == 0:
        odd, lay = (_LAY_MLA_ODD, _LAY_MLA_MAIN) if i in SPLIT_LAYERS else (_LAY_MLA, _LAY_MLA)
        small = [("mla_w_down", j, odd["wdn"]), ("mla_w_uq", j, odd["wuq"]), ("mla_w_ukv", j, odd["wukv"]),
                 ("ple_w_proj", i, odd["proj"])]
        large = [("ffn_w_up", i, lay["up"]), ("ffn_w_down", i, lay["down"]), ("mla_w_out", j, lay["out"]),
                 ("ple_w_gate", i, lay["gate"])]
        return [("odd", odd, small), ("main", lay, large)] if i in SPLIT_LAYERS else [("main", lay, large + small)]
    lay = _LAY_GMLP
    return [("main", lay, [("ffn_w_up", i, lay["up"]), ("ffn_w_down", i, lay["down"]), ("gmlp_w_in", j, lay["in"]),
                           ("gmlp_w_out", j, lay["out"]), ("ple_w_gate", i, lay["gate"]),
                           ("ple_w_proj", i, lay["proj"])])]


def _pack_rows(parts, dtype, pad_to=None, slot=False):
    size = sum(p.size for p in parts)
    tail = [] if pad_to is None or pad_to * D == size else [jnp.zeros((pad_to * D - size,), dtype)]
    shape = (1, -1, D) if slot else (-1, D)
    if all(p.size % D == 0 for p in parts + tail):
        return jnp.concatenate([p.astype(dtype).reshape(shape) for p in parts + tail], axis=len(shape) - 2)
    return jnp.concatenate([p.astype(dtype).reshape(-1) for p in parts + tail]).reshape(shape)


def _odd(allw, row0, a, b):
    return allw[:, row0:row0 + a * b // D].reshape(N_CHIPS, a, b)


def _cols_joined(s):
    return jnp.transpose(s, (1, 0, 2)).reshape(s.shape[1], N_CHIPS * s.shape[2])


def _col_shards(full):
    a, bb = full.shape
    return jnp.transpose(full.reshape(a, N_CHIPS, bb // N_CHIPS), (1, 0, 2)).reshape(N_CHIPS, -1, D)


def _pad_lanes(g):
    return jnp.pad(g, ((0, 0), (0, LANES - g.shape[1])))


def _split_uq(wuq):
    l = wuq.shape[0]
    w = wuq.reshape(l, QL, HEADS, DN + DR)
    nope = w[..., :DN].reshape(l, QL, HEADS * DN)
    rope = jnp.pad(w[..., DN:], ((0, 0), (0, 0), (0, 0), (0, LANES - DR))).reshape(l, QL, HEADS * LANES)
    return jnp.concatenate([nope, rope], axis=-1)


def _merge_uq(d):
    nope = d[:, :HEADS * DN].reshape(QL, HEADS, DN)
    rope = d[:, HEADS * DN:].reshape(QL, HEADS, LANES)[..., :DR]
    return jnp.concatenate([nope, rope], axis=-1).reshape(QL, HEADS * (DN + DR))


def _rope_tables(positions):
    inv_freq = ROPE_BASE ** (-(jnp.arange(0, DR, 2, dtype=F32) / DR))
    ang = positions.reshape(-1).astype(F32)[:, None] * inv_freq
    z = jnp.zeros((ang.shape[0], LANES - DR), F32)
    return (jnp.concatenate([jnp.cos(ang), jnp.cos(ang), z], axis=1),
            jnp.concatenate([jnp.sin(ang), jnp.sin(ang), z], axis=1))


def kernel(x, p, positions, norm_mix, norm_ffn, norm_ple, mla_w_down, mla_q_lora_g, mla_kv_lora_g, mla_w_uq, mla_w_ukv, mla_q_nope_g, mla_q_rope_g, mla_k_nope_g, mla_k_rope_g, mla_w_out, gmlp_w_in, gmlp_ln_g, gmlp_ln_b, gmlp_w_s, gmlp_b_s, gmlp_w_out, ffn_w_up, ffn_w_down, ple_w_gate, ple_w_proj, loss_target, m_norm_mix, m_norm_ffn, m_norm_ple, m_mla_w_down, m_mla_q_lora_g, m_mla_kv_lora_g, m_mla_w_uq, m_mla_w_ukv, m_mla_q_nope_g, m_mla_q_rope_g, m_mla_k_nope_g, m_mla_k_rope_g, m_mla_w_out, m_gmlp_w_in, m_gmlp_ln_g, m_gmlp_ln_b, m_gmlp_w_s, m_gmlp_b_s, m_gmlp_w_out, m_ffn_w_up, m_ffn_w_down, m_ple_w_gate, m_ple_w_proj, v_norm_mix, v_norm_ffn, v_norm_ple, v_mla_w_down, v_mla_q_lora_g, v_mla_kv_lora_g, v_mla_w_uq, v_mla_w_ukv, v_mla_q_nope_g, v_mla_q_rope_g, v_mla_k_nope_g, v_mla_k_rope_g, v_mla_w_out, v_gmlp_w_in, v_gmlp_ln_g, v_gmlp_ln_b, v_gmlp_w_s, v_gmlp_b_s, v_gmlp_w_out, v_ffn_w_up, v_ffn_w_down, v_ple_w_gate, v_ple_w_proj):
    args = dict(locals())
    weights = {n: args[n] for n in _BIG + _SMALL}
    depth = norm_mix.shape[0]
    nb, seq, _ = x.shape
    t = nb * seq
    assert seq % TQ == 0 and seq % TM == 0 and t % 512 == 0, (nb, seq)
    cx = lax.axis_index("x")
    cy = lax.axis_index("y")
    cc = lax.axis_index("c")
    chip = 2 * cx + cy

    gathers = {}
    token = None
    for i in range(depth):
        for key, lay, parts in _layer_units(i):
            rows = [weights[n][l] for n, l, _ in parts]
            if token is not None:
                rows[0] = rows[0] + token[0, 0]
            if "ln" in lay:
                ln = jnp.stack([gmlp_ln_g[i // 2], gmlp_ln_b[i // 2]]).astype(F32)
                bits = lax.bitcast_convert_type(ln, BF16).reshape(-1)
                rows.append(jnp.pad(bits, (0, 16 * D - bits.size)).reshape(16, D))
            mine = _pack_rows(rows, BF16, pad_to=lay["rows"], slot=True)
            land = lax.dynamic_update_slice(lax.empty((N_CHIPS, lay["rows"], D), BF16), mine, (chip, 0, 0))
            sems, src, land, token = gather_start(land, f"{i}{key}")
            gathers[i, key] = (sems, src, land)
    allw = [None] * depth

    tril = jnp.tril(jnp.ones((GC, GC), F32))
    wm = (gmlp_w_s * tril).astype(BF16)
    wmt = jnp.swapaxes(wm, -1, -2)
    bfull = jnp.repeat(jnp.swapaxes(gmlp_b_s, -1, -2), GD, axis=-1)
    cos, sin = _rope_tables(positions)
    row = lambda g: g.reshape(1, -1)
    gqr = _pad_lanes(mla_q_rope_g)
    gkr = _pad_lanes(mla_k_rope_g)

    h = x.reshape(t, D)
    pt = p.reshape(depth, t, PLE)
    saved = []

    passing = {}

    def arrive(i, key, after):
        sems, src, land = gathers[i, key]
        _, land = gather_wait(sems, src, land, after, f"{i}{key}")
        passing[i, key] = pass_start(land, f"{i}{key}")
        return passing[i, key][3]

    def needed(i, key, after=None):
        sems, src, land, tok = passing.pop((i, key))
        return pass_wait(sems, src, land, tok if after is None else after, f"{i}{key}")[1]

    arrive(0, _layer_units(0)[0][0], token)
    for i in range(depth):
        j = i // 2
        lay = _layer_units(i)[-1][1]
        s = dict(h=h)
        if i % 2 == 0:
            split = i in SPLIT_LAYERS
            olay = _layer_units(i)[0][1]
            odd = needed(i, "odd" if split else "main", None if i == 0 else h)
            wdn = jnp.pad(_odd(odd, olay["wdn"], D // N_CHIPS, LAT).reshape(D, LAT), ((0, 0), (0, LATP - LAT)))
            wuq = _split_uq(_cols_joined(_odd(odd, olay["wuq"], QL, 384))[None])[0]
            wukv = _cols_joined(_odd(odd, olay["wukv"], KVL, 512))
            wp = _cols_joined(_odd(odd, olay["proj"], PLE, 256))
            mla_args = (row(norm_mix[i]), wdn, row(mla_q_lora_g[j]), row(mla_kv_lora_g[j]), wuq, wukv,
                        row(mla_q_nope_g[j]), gqr[j:j + 1], row(mla_k_nope_g[j]), gkr[j:j + 1], cos, sin)
            q, k, v = mla_pre_fwd(h, *mla_args)
            y, lse = flash_fwd(q, k, v, seq)
            if split and i == 0:
                arrive(i, "main", y)
            aw = needed(i, "main", y) if split else odd
            s.update(q=q, k=k, v=v, lse=lse, mla_args=mla_args)
        else:
            aw = needed(i, "main", h)
            ln = lax.bitcast_convert_type(aw[:, lay["ln"]:lay["ln"] + 2].reshape(N_CHIPS, 2, GH // N_CHIPS, 2), F32)
            ln = jnp.transpose(ln, (1, 0, 2)).reshape(2, 1, GH)
            wp = _cols_joined(_odd(aw, lay["proj"], PLE, 256))
            y, pre = gmlp_fwd(h, row(norm_mix[i]), aw, lay, ln[0], ln[1], wm[j], bfull[j])
            s.update(pre=pre, ln=ln)
        allw[i] = aw
        g2 = row(norm_ffn[i])
        if i + 1 < depth:
            for key, _, _ in _layer_units(i + 1):
                g2 = g2 + arrive(i + 1, key, y)[0:1, 0:1]
        h1, h2, hn2, r = mixffn_fwd(h, y, aw, lay, g2)
        h, hn3 = ple_fwd(h2, pt[i], row(norm_ple[i]), aw, lay, wp)
        s.update(y=y, wp=wp, h1=h1, h2=h2, hn2=hn2, r=r, hn3=hn3)
        saved.append(s)

    dh, loss_part = loss_head(h, loss_target.reshape(t, D))
    loss = lax.psum(loss_part[0, 0], ("x", "y", "c"))

    gs = {n: [None] * weights[n].shape[0] for n in _SMALL}
    gw = {n: [None] * weights[n].shape[0] for n in _BIG}
    place = jnp.stack([cc, chip]).astype(jnp.int32)
    scatters = []
    swaps = []
    token = None

    def put(b, row0, shards):
        return lax.dynamic_update_slice(b, shards.reshape(N_CHIPS, -1, D), (0, row0, 0))

    def swap(i, key, buf):
        sems, buf, got, tok = swap_start(buf, f"{i}{key}")
        swaps.append((i, key, sems, buf, got))
        return tok

    def swapped(after, zero):
        while swaps:
            i, key, sems, g, got = swaps.pop(0)
            g, got = swap_wait(sems, g, got, after, f"{i}{key}")
            own, sums = chip_sum(place, g, got)
            sems, sums, land, tok = scatter_start(sums, f"{i}{key}")
            scatters.append((i, key, own, sems, sums, land))
            zero = zero + tok[0:1, 0:1]
        return zero

    for i in reversed(range(depth)):
        j = i // 2
        lay = _layer_units(i)[-1][1]
        aw = allw[i]
        s = saved[i]

        g3 = row(norm_ple[i])
        if token is not None:
            g3 = g3 + token[0:1, 0:1]
        dh2, dh2b, dgt, dpp, dg3 = ple_bwd(dh, s["h2"], pt[i], g3, aw, lay, s["wp"])
        gs["norm_ple"][i] = dg3[0]
        buf = mm_tn_into(lay["rows"], s["hn3"], dgt, D // N_CHIPS, lay["gate"], False)
        dproj = _col_shards(mm_tn(pt[i], dpp))
        if "ln" in lay:
            buf = put(buf, lay["ln"], jnp.zeros((N_CHIPS, lay["rows"] - lay["ln"], D), F32))
            buf = put(buf, lay["proj"], dproj)
        dh1, dh1b, du, a, dg2 = ffn_bwd(dh2, dh2b, s["h1"], s["r"], row(norm_ffn[i]), aw, lay)
        gs["norm_ffn"][i] = dg2[0]
        buf = mm_tn_into(buf, a, dh2b, D, lay["down"], False)
        buf = mm_tn_into(buf, s["hn2"], du, D, lay["up"], True)
        buf = mm_tn_into(buf, s["y"], dh1b, s["y"].shape[1] // N_CHIPS, lay["out"], False)
        g1 = swapped(dh1, row(norm_mix[i]))
        if i % 2 == 0:
            split = i in SPLIT_LAYERS
            lse = s["lse"]
            if split:
                lse = lse + swap(i, "main", buf)[0, 0]
            do = linear_nt(dh1b, aw, D // N_CHIPS, lay["out"])
            dq, dk, dv = flash_bwd(s["q"], s["k"], s["v"], s["y"], do, lse, seq)
            g1 = swapped(dq, g1)
            (dh, hn1, cq, ckv, dqp, dkvp, dlat, dg1, dgq, dgkv, dgqn, dgqr, dgkn, dgkr) = mla_pre_bwd(
                dq, dk, dv, dh1, s["h"], g1, *s["mla_args"][1:])
            gs["norm_mix"][i] = dg1[0]
            gs["mla_q_lora_g"][j] = dgq[0]
            gs["mla_kv_lora_g"][j] = dgkv[0]
            gs["mla_q_nope_g"][j] = dgqn[0]
            gs["mla_q_rope_g"][j] = dgqr[0, :DR]
            gs["mla_k_nope_g"][j] = dgkn[0]
            gs["mla_k_rope_g"][j] = dgkr[0, :DR]
            small = [mm_tn(hn1, dlat)[:, :LAT].reshape(N_CHIPS, -1, D), _col_shards(_merge_uq(mm_tn(cq, dqp))),
                     _col_shards(mm_tn(ckv, dkvp)), dproj]
            if split:
                buf = jnp.concatenate(small, axis=1)
            else:
                buf = put(buf, lay["wdn"], jnp.concatenate(small, axis=1))
            key = "odd" if split else "main"
        else:
            dh, hn1, dpre, dws, dbs, dlng, dlnb, dg1 = gmlp_bwd(
                dh1, dh1b, s["h"], s["pre"], g1, aw, lay, s["ln"][0], s["ln"][1], wm[j], wmt[j], bfull[j], tril)
            gs["norm_mix"][i] = dg1[0]
            gs["gmlp_ln_g"][j] = dlng[0]
            gs["gmlp_ln_b"][j] = dlnb[0]
            gs["gmlp_w_s"][j] = dws
            gs["gmlp_b_s"][j] = jnp.sum(dbs.reshape(GC, GG, GD), axis=-1).T
            buf = mm_tn_into(buf, hn1, dpre, D, lay["in"], True)
            key = "main"
        token = swap(i, key, buf)
    swapped(dh, jnp.zeros((1, 1), F32))
    grad_x = dh.reshape(x.shape)

    small_sizes = [weights[n].size if n not in ("gmlp_ln_g", "gmlp_ln_b") else weights[n].shape[0] * GH
                   for n in _SMALL]
    small_rows = -(-sum(small_sizes) // (56 * D)) * 56
    part = _pack_rows([jnp.stack(gs[n]) for n in _SMALL], F32, pad_to=small_rows, slot=True)
    land = lax.dynamic_update_slice(lax.empty((N_DEV, small_rows, D), F32), part, (2 * chip + cc, 0, 0))
    small = small_start(land)

    after = small[3]
    shares = []
    for i, key, own, sems, sums, land in scatters:
        _, got = scatter_wait(sems, sums, land, after, f"{i}{key}")
        sems, src, full, after = share_start(final_sum(place, own, got), f"{i}{key}")
        shares.append((i, key, sems, src, full))
    for i, key, sems, src, full in shares:
        _, after = share_wait(sems, src, full, after, f"{i}{key}")
        for n, l, row0 in dict((k, parts) for k, _, parts in _layer_units(i))[key]:
            gw[n][l] = after[row0:row0 + weights[n][l].size // D].reshape(weights[n].shape[1:])
    grads = {n: jnp.stack(gw[n]) for n in _BIG}

    tot = sum_devices(small_wait(small[0], small[1], small[2], after)[1]).reshape(-1)
    off = 0
    for n, sz in zip(_SMALL, small_sizes):
        gsum = tot[off:off + sz]
        off += sz
        if n in ("gmlp_ln_g", "gmlp_ln_b"):
            gsum = lax.dynamic_slice_in_dim(gsum.reshape(-1, GH), chip * (GH // N_CHIPS), GH // N_CHIPS, axis=1)
        grads[n] = gsum.reshape(weights[n].shape)

    delta, new_m, new_v = {}, {}, {}
    for n in _BIG:
        w2 = weights[n].reshape(-1, weights[n].shape[-1])
        d, mn, vn = adamw(w2, grads[n].reshape(w2.shape), args["m_" + n].reshape(w2.shape),
                          args["v_" + n].reshape(w2.shape))
        delta[n], new_m[n], new_v[n] = (a.reshape(weights[n].shape) for a in (d, mn, vn))
    own_sizes = [weights[n].size for n in _SMALL]
    own_rows = -(-sum(own_sizes) // (8 * D)) * 8
    packed = [_pack_rows([src[n] for n in _SMALL], F32, pad_to=own_rows)
              for src in (weights, grads, {n: args["m_" + n] for n in _SMALL}, {n: args["v_" + n] for n in _SMALL})]
    outs = adamw(*packed)
    off = 0
    for n, sz in zip(_SMALL, own_sizes):
        for dst, o in zip((delta, new_m, new_v), outs):
            dst[n] = o.reshape(-1)[off:off + sz].reshape(weights[n].shape)
        off += sz

    order = ["norm_mix", "norm_ffn", "norm_ple", "mla_w_down", "mla_q_lora_g", "mla_kv_lora_g", "mla_w_uq",
             "mla_w_ukv", "mla_q_nope_g", "mla_q_rope_g", "mla_k_nope_g", "mla_k_rope_g", "mla_w_out", "gmlp_w_in",
             "gmlp_ln_g", "gmlp_ln_b", "gmlp_w_s", "gmlp_b_s", "gmlp_w_out", "ffn_w_up", "ffn_w_down", "ple_w_gate",
             "ple_w_proj"]
    return (loss, grad_x, *[grads[n] for n in order], *[delta[n] for n in order], *[new_m[n] for n in order],
            *[new_v[n] for n in order])
```

```python
import functools

import jax
import jax.numpy as jnp
from jax import lax
from jax.experimental import pallas as pl
from jax.experimental.pallas import tpu as pltpu

F32 = jnp.float32
BF16 = jnp.bfloat16
MESH = pl.DeviceIdType.MESH

D = 1024
HEADS = 8
DN = 128
DR = 64
QL = 384
KVL = 256
LAT = 704
LATP = 768
DFF = 4096
GH = 2048
GC = 128
GG = 8
GD = 256
PLE = 256
EPS = 1e-6
ROPE_BASE = 10000.0
SM_SCALE = (DN + DR) ** -0.5
N_CHIPS = 4
LANES = 128

ADAM_LR = 0.001
ADAM_B1 = 0.9
ADAM_B2 = 0.999
ADAM_EPS = 1e-08
ADAM_WD = 0.01
ADAM_STEP = 10

TM = 256
TMB = 512
TQ = 512
TQ_FWD = 512
FWD_HEADS = 2
BWD_HEADS = 2
SUM_ROWS = 256
VMEM_LIMIT = 56 * 1024 * 1024


def _cp(*sem):
    return pltpu.CompilerParams(dimension_semantics=sem, vmem_limit_bytes=VMEM_LIMIT)


def _dot(a, b):
    return jnp.dot(a, b, preferred_element_type=F32)


def _dot_nt(a, b):
    return lax.dot_general(a, b, (((1,), (1,)), ((), ())), preferred_element_type=F32)


def _dot_tn(a, b):
    return lax.dot_general(a, b, (((0,), (0,)), ((), ())), preferred_element_type=F32)


def _rms(x, g, n):
    r = lax.rsqrt(jnp.sum(x * x, axis=-1, keepdims=True) * (1.0 / n) + EPS)
    xhat = x * r
    return xhat * g, xhat, r


def _rms_bwd(dy, g, xhat, r, n):
    dxhat = dy * g
    return r * (dxhat - xhat * (jnp.sum(dxhat * xhat, axis=-1, keepdims=True) * (1.0 / n)))


def _rope(x, c, s):
    return x * c + (pltpu.roll(x, 32, 1) - pltpu.roll(x, 96, 1)) * s


def _rope_t(dy, c, s):
    w = dy * s
    return dy * c + pltpu.roll(w, 96, 1) - pltpu.roll(w, 32, 1)


def _sigmoid(x):
    return 1.0 / (1.0 + jnp.exp(-x))


_GELU_K = 0.7978845608028654
_GELU_C = 0.044715


def _gelu(x):
    return 0.5 * x * (1.0 + jnp.tanh(_GELU_K * (x + _GELU_C * x * x * x)))


def _gelu_and_grad(x):
    x2 = x * x
    t = jnp.tanh(_GELU_K * (x + _GELU_C * x2 * x))
    half = 0.5 * (1.0 + t)
    return x * half, half + 0.5 * x * (1.0 - t * t) * (_GELU_K * (1.0 + 3.0 * _GELU_C * x2))


def _acc_rows(ref, val):
    ref[...] += jnp.broadcast_to(jnp.sum(val, axis=0, keepdims=True), ref.shape)


def _row(tm, c):
    return pl.BlockSpec((tm, c), lambda i: (i, 0))


def _const(shape):
    nd = len(shape)
    return pl.BlockSpec(shape, lambda i: (0,) * nd, pipeline_mode=pl.Buffered(1))


def _wblk(rows, row0):
    assert row0 % rows == 0, (rows, row0)
    return pl.BlockSpec((N_CHIPS, rows, D), lambda i: (0, row0 // rows, 0), pipeline_mode=pl.Buffered(1))


def _rows_joined(w_ref):
    return w_ref[...].reshape(N_CHIPS * w_ref.shape[1], D)


def _sds(shape, dtype):
    return jax.ShapeDtypeStruct(shape, dtype)


def mixffn_fwd(h, y, allw, lay, g2):
    t, k = y.shape

    def body(h_ref, y_ref, wo_ref, g_ref, wu_ref, wd_ref, h1_ref, h2_ref, hn_ref, r_ref):
        h1 = h_ref[...] + _dot(y_ref[...], _rows_joined(wo_ref))
        h1_ref[...] = h1
        yn, _, _ = _rms(h1, g_ref[...], D)
        hn = yn.astype(BF16)
        hn_ref[...] = hn
        f = jnp.zeros((TMB, D), F32)
        for c in range(N_CHIPS):
            r = jnp.maximum(_dot(hn, wu_ref[c]), 0.0)
            r_ref[:, c * D:(c + 1) * D] = r.astype(BF16)
            f = f + _dot((r * r).astype(BF16), wd_ref[c])
        h2_ref[...] = h1 + f

    return pl.pallas_call(
        body, name="mixffn_fwd", grid=(t // TMB,),
        in_specs=[_row(TMB, D), _row(TMB, k), _wblk(k // N_CHIPS, lay["out"]), _const((1, D)), _wblk(D, lay["up"]),
                  _wblk(D, lay["down"])],
        out_specs=[_row(TMB, D), _row(TMB, D), _row(TMB, D), _row(TMB, DFF)],
        out_shape=[_sds((t, D), F32), _sds((t, D), F32), _sds((t, D), BF16), _sds((t, DFF), BF16)],
        compiler_params=_cp("parallel"),
    )(h, y, allw, g2, allw, allw)


def ple_fwd(h2, p, g3, allw, lay, wp):
    t = h2.shape[0]

    def body(h_ref, p_ref, g_ref, wg_ref, wp_ref, h3_ref, hn_ref):
        x = h_ref[...]
        yn, _, _ = _rms(x, g_ref[...], D)
        hn = yn.astype(BF16)
        hn_ref[...] = hn
        gt = _dot(hn, _rows_joined(wg_ref))
        pp = _dot(p_ref[...].astype(BF16), wp_ref[...])
        h3_ref[...] = x + _sigmoid(gt) * pp

    return pl.pallas_call(
        body, name="ple_fwd", grid=(t // TMB,),
        in_specs=[_row(TMB, D), _row(TMB, PLE), _const((1, D)), _wblk(D // N_CHIPS, lay["gate"]), _const((PLE, D))],
        out_specs=[_row(TMB, D), _row(TMB, D)],
        out_shape=[_sds((t, D), F32), _sds((t, D), BF16)],
        compiler_params=_cp("parallel"),
    )(h2, p, g3, allw, wp)


def _mla_project(h_ref, g1_ref, wdn_ref, gq_ref, gkv_ref, wuq_ref, wukv_ref):
    x = h_ref[...]
    yn, xhat, rx = _rms(x, g1_ref[...], D)
    hn = yn.astype(BF16)
    lat = _dot(hn, wdn_ref[...])
    cq, cqhat, rq = _rms(lat[:, :QL], gq_ref[...], QL)
    ckv, ckvhat, rkv = _rms(lat[:, QL:QL + KVL], gkv_ref[...], KVL)
    kr_raw = lat[:, QL + KVL:]
    cqb = cq.astype(BF16)
    ckvb = ckv.astype(BF16)
    qp = _dot(cqb, wuq_ref[...])
    kvp = _dot(ckvb, wukv_ref[...])
    return dict(xhat=xhat, rx=rx, hn=hn, cqhat=cqhat, rq=rq, ckvhat=ckvhat, rkv=rkv, kr_raw=kr_raw,
                cqb=cqb, ckvb=ckvb, qp=qp, kvp=kvp)


def mla_pre_fwd(h, g1, wdn, gq, gkv, wuq, wukv, gqn, gqr, gkn, gkr, cos, sin):
    t = h.shape[0]

    def body(h_ref, g1_ref, wdn_ref, gq_ref, gkv_ref, wuq_ref, wukv_ref, gqn_ref, gqr_ref, gkn_ref, gkr_ref,
             c_ref, s_ref, q_ref, k_ref, v_ref):
        m = _mla_project(h_ref, g1_ref, wdn_ref, gq_ref, gkv_ref, wuq_ref, wukv_ref)
        c = c_ref[...]
        s = s_ref[...]
        kr, _, _ = _rms(m["kr_raw"], gkr_ref[...], DR)
        krb = _rope(kr, c, s).astype(BF16)
        for hd in range(HEADS):
            qn, _, _ = _rms(m["qp"][:, hd * DN:(hd + 1) * DN], gqn_ref[...], DN)
            qr, _, _ = _rms(m["qp"][:, D + hd * LANES:D + (hd + 1) * LANES], gqr_ref[...], DR)
            q_ref[hd, :, 0:DN] = (qn * SM_SCALE).astype(BF16)
            q_ref[hd, :, DN:2 * DN] = (_rope(qr, c, s) * SM_SCALE).astype(BF16)
            kn, _, _ = _rms(m["kvp"][:, hd * 2 * DN:hd * 2 * DN + DN], gkn_ref[...], DN)
            k_ref[hd, :, 0:DN] = kn.astype(BF16)
            k_ref[hd, :, DN:2 * DN] = krb
            v_ref[hd] = m["kvp"][:, hd * 2 * DN + DN:(hd + 1) * 2 * DN].astype(BF16)

    hb = lambda w: pl.BlockSpec((HEADS, TM, w), lambda i: (0, i, 0))
    return pl.pallas_call(
        body, name="mla_pre_fwd", grid=(t // TM,),
        in_specs=[_row(TM, D), _const((1, D)), _const((D, LATP)), _const((1, QL)), _const((1, KVL)),
                  _const((QL, 2 * D)), _const((KVL, 2 * D)), _const((1, LANES)), _const((1, LANES)),
                  _const((1, LANES)), _const((1, LANES)), _row(TM, LANES), _row(TM, LANES)],
        out_specs=[hb(2 * DN), hb(2 * DN), hb(DN)],
        out_shape=[_sds((HEADS, t, 2 * DN), BF16), _sds((HEADS, t, 2 * DN), BF16), _sds((HEADS, t, DN), BF16)],
        compiler_params=_cp("parallel"),
    )(h, g1, wdn, gq, gkv, wuq, wukv, gqn, gqr, gkn, gkr, cos, sin)


def _diagonal_mask(n=TQ):
    return lax.broadcasted_iota(jnp.int32, (n, n), 1) <= lax.broadcasted_iota(jnp.int32, (n, n), 0)


def flash_fwd(q, k, v, seq):
    t = q.shape[1]
    nb = t // seq
    tq = TQ_FWD
    nq = seq // tq
    hp = FWD_HEADS

    def body(q_ref, k_ref, v_ref, o_ref, lse_ref):
        qi = pl.program_id(2)
        qs = [q_ref[a] for a in range(hp)]

        def step(j, carry, diagonal=False):
            rows = pl.ds(pl.multiple_of(j * tq, tq), tq)
            out = []
            for a in range(hp):
                m, l, acc = carry[a]
                s = _dot_nt(qs[a], k_ref[a, rows, :])
                if diagonal:
                    s = jnp.where(_diagonal_mask(tq), s, -1e30)
                m_new = jnp.maximum(m, jnp.max(s, axis=-1, keepdims=True))
                p = jnp.exp(s - m_new)
                alpha = jnp.exp(m - m_new)
                l = alpha * l + jnp.sum(p, axis=-1, keepdims=True)
                acc = alpha * acc + _dot(p.astype(BF16), v_ref[a, rows, :])
                out.append((m_new, l, acc))
            return tuple(out)

        one = (jnp.full((tq, 1), -1e30, F32), jnp.zeros((tq, 1), F32), jnp.zeros((tq, DN), F32))
        done = step(qi, lax.fori_loop(0, qi, step, (one,) * hp), diagonal=True)
        for a, (m, l, acc) in enumerate(done):
            o_ref[:, a * DN:(a + 1) * DN] = (acc / l).astype(BF16)
            lse_ref[a] = m + jnp.log(l)

    return pl.pallas_call(
        body, name="flash_fwd", grid=(nb, HEADS // hp, nq),
        in_specs=[pl.BlockSpec((hp, tq, 2 * DN), lambda b, h, i: (h, b * nq + i, 0)),
                  pl.BlockSpec((hp, seq, 2 * DN), lambda b, h, i: (h, b, 0)),
                  pl.BlockSpec((hp, seq, DN), lambda b, h, i: (h, b, 0))],
        out_specs=[pl.BlockSpec((tq, hp * DN), lambda b, h, i: (b * nq + i, h)),
                   pl.BlockSpec((hp, tq, 1), lambda b, h, i: (h, b * nq + i, 0))],
        out_shape=[_sds((t, HEADS * DN), BF16), _sds((HEADS, t, 1), F32)],
        compiler_params=_cp("parallel", "parallel", "arbitrary"),
    )(q, k, v)


def _gmlp_in(hn, win_ref):
    pre = [_dot(hn, win_ref[c]) for c in range(N_CHIPS)]
    return jnp.concatenate(pre[:2], axis=1), jnp.concatenate(pre[2:], axis=1)


def gmlp_fwd(h, g1, allw, lay, lng, lnb, wm, bfull):
    t = h.shape[0]

    def body(h_ref, g1_ref, win_ref, lng_ref, lnb_ref, wm_ref, b_ref, y_ref, pre_ref):
        yn, _, _ = _rms(h_ref[...], g1_ref[...], D)
        pre_u, pre_v = _gmlp_in(yn.astype(BF16), win_ref)
        pre_ref[:, :GH] = pre_u.astype(BF16)
        pre_ref[:, GH:] = pre_v.astype(BF16)
        u = _gelu(pre_u)
        v = _gelu(pre_v)
        xc = v - jnp.mean(v, axis=-1, keepdims=True)
        rs = lax.rsqrt(jnp.mean(xc * xc, axis=-1, keepdims=True) + EPS)
        vnb = (xc * rs * lng_ref[...] + lnb_ref[...]).astype(BF16)
        for ch in range(TM // GC):
            rows = slice(ch * GC, (ch + 1) * GC)
            for g in range(GG):
                cols = slice(g * GD, (g + 1) * GD)
                sv = _dot(wm_ref[g], vnb[rows, cols]) + b_ref[:, cols]
                y_ref[rows, cols] = (u[rows, cols] * sv).astype(BF16)

    return pl.pallas_call(
        body, name="gmlp_fwd", grid=(t // TM,),
        in_specs=[_row(TM, D), _const((1, D)), _wblk(D, lay["in"]), _const((1, GH)), _const((1, GH)),
                  _const((GG, GC, GC)), _const((GC, GH))],
        out_specs=[_row(TM, GH), _row(TM, 2 * GH)],
        out_shape=[_sds((t, GH), BF16), _sds((t, 2 * GH), BF16)],
        compiler_params=_cp("parallel"),
    )(h, g1, allw, lng, lnb, wm, bfull)


def loss_head(h, tgt):
    t = h.shape[0]

    def body(h_ref, t_ref, dh_ref, loss_ref):
        @pl.when(pl.program_id(0) == 0)
        def _():
            loss_ref[...] = jnp.zeros_like(loss_ref)

        e = h_ref[...] - t_ref[...]
        dh_ref[...] = e * (1.0 / D)
        part = jnp.sum(jnp.sum(e * e, axis=-1, keepdims=True), axis=0, keepdims=True) * (0.5 / D)
        loss_ref[...] += jnp.broadcast_to(part, loss_ref.shape)

    return pl.pallas_call(
        body, name="loss_head", grid=(t // TMB,),
        in_specs=[_row(TMB, D), _row(TMB, D)],
        out_specs=[_row(TMB, D), _const((8, LANES))],
        out_shape=[_sds((t, D), F32), _sds((8, LANES), F32)],
        compiler_params=_cp("arbitrary"),
    )(h, tgt)


def _zero_at_first_step(*refs):
    @pl.when(pl.program_id(0) == 0)
    def _():
        for r in refs:
            r[...] = jnp.zeros_like(r)


def ple_bwd(dh3, h2, p, g3, allw, lay, wp):
    t = h2.shape[0]

    def body(dh_ref, h_ref, p_ref, g_ref, wg_ref, wp_ref, dh2_ref, dh2b_ref, dgt_ref, dpp_ref, dg_ref):
        _zero_at_first_step(dg_ref)
        dh3v = dh_ref[...]
        x = h_ref[...]
        g = g_ref[...]
        wg = _rows_joined(wg_ref)
        yn, xhat, r = _rms(x, g, D)
        gt = _dot(yn.astype(BF16), wg)
        pp = _dot(p_ref[...].astype(BF16), wp_ref[...])
        sg = _sigmoid(gt)
        dgt = (dh3v * pp * sg * (1.0 - sg)).astype(BF16)
        dgt_ref[...] = dgt
        dpp_ref[...] = (dh3v * sg).astype(BF16)
        dhn = _dot_nt(dgt, wg)
        _acc_rows(dg_ref, dhn * xhat)
        dh2 = dh3v + _rms_bwd(dhn, g, xhat, r, D)
        dh2_ref[...] = dh2
        dh2b_ref[...] = dh2.astype(BF16)

    return pl.pallas_call(
        body, name="ple_bwd", grid=(t // TMB,),
        in_specs=[_row(TMB, D), _row(TMB, D), _row(TMB, PLE), _const((1, D)), _wblk(D // N_CHIPS, lay["gate"]),
                  _const((PLE, D))],
        out_specs=[_row(TMB, D), _row(TMB, D), _row(TMB, D), _row(TMB, D), _const((8, D))],
        out_shape=[_sds((t, D), F32), _sds((t, D), BF16), _sds((t, D), BF16), _sds((t, D), BF16), _sds((8, D), F32)],
        compiler_params=_cp("arbitrary"),
    )(dh3, h2, p, g3, allw, wp)


def ffn_bwd(dh2, dh2b, h1, r, g2, allw, lay):
    t = h1.shape[0]

    def body(dh_ref, dhb_ref, h_ref, r_ref, g_ref, wu_ref, wd_ref, dh1_ref, dh1b_ref, du_ref, a_ref, dg_ref):
        _zero_at_first_step(dg_ref)
        dhb = dhb_ref[...]
        g = g_ref[...]
        _, xhat, rr = _rms(h_ref[...], g, D)
        dhn = jnp.zeros((TM, D), F32)
        for c in range(N_CHIPS):
            cs = slice(c * D, (c + 1) * D)
            rc = r_ref[:, cs].astype(F32)
            a_ref[:, cs] = (rc * rc).astype(BF16)
            da = _dot_nt(dhb, wd_ref[c])
            du = (da * (2.0 * rc)).astype(BF16)
            du_ref[:, cs] = du
            dhn = dhn + _dot_nt(du, wu_ref[c])
        _acc_rows(dg_ref, dhn * xhat)
        dh1 = dh_ref[...] + _rms_bwd(dhn, g, xhat, rr, D)
        dh1_ref[...] = dh1
        dh1b_ref[...] = dh1.astype(BF16)

    return pl.pallas_call(
        body, name="ffn_bwd", grid=(t // TM,),
        in_specs=[_row(TM, D), _row(TM, D), _row(TM, D), _row(TM, DFF), _const((1, D)), _wblk(D, lay["up"]),
                  _wblk(D, lay["down"])],
        out_specs=[_row(TM, D), _row(TM, D), _row(TM, DFF), _row(TM, DFF), _const((8, D))],
        out_shape=[_sds((t, D), F32), _sds((t, D), BF16), _sds((t, DFF), BF16), _sds((t, DFF), BF16),
                   _sds((8, D), F32)],
        compiler_params=_cp("arbitrary"),
    )(dh2, dh2b, h1, r, g2, allw, allw)


def linear_nt(a, allw, rows, row0):
    t = a.shape[0]
    k = N_CHIPS * rows

    def body(a_ref, w_ref, o_ref):
        o_ref[...] = _dot_nt(a_ref[...], _rows_joined(w_ref)).astype(BF16)

    return pl.pallas_call(
        body, name="linear_nt", grid=(t // TMB,),
        in_specs=[_row(TMB, D), _wblk(rows, row0)],
        out_specs=_row(TMB, k),
        out_shape=_sds((t, k), BF16),
        compiler_params=_cp("parallel"),
    )(a, allw)


def flash_bwd(q, k, v, o, do, lse, seq, after):
    t = q.shape[1]
    nb = t // seq
    nq = seq // TQ
    hp = BWD_HEADS

    def body(q_ref, k_ref, v_ref, o_ref, do_ref, lse_ref, after_ref, dq_ref, dk_ref, dv_ref):
        del after_ref
        kj = pl.program_id(2)

        @pl.when(kj == 0)
        def _():
            dq_ref[...] = jnp.zeros_like(dq_ref)

        def step(i, carry, diagonal=False):
            rows = pl.ds(pl.multiple_of(i * TQ, TQ), TQ)
            out = []
            for a in range(hp):
                dk, dv = carry[a]
                kv = k_ref[a]
                qv = q_ref[a, rows, :]
                dov = do_ref[rows, a * DN:(a + 1) * DN]
                ov = o_ref[rows, a * DN:(a + 1) * DN]
                delta = jnp.sum(dov.astype(F32) * ov.astype(F32), axis=-1, keepdims=True)
                s = _dot_nt(qv, kv)
                if diagonal:
                    s = jnp.where(_diagonal_mask(), s, -1e30)
                p = jnp.exp(s - lse_ref[a, rows, :])
                dp = _dot_nt(dov, v_ref[a])
                ds = (p * (dp - delta)).astype(BF16)
                dv = dv + _dot_tn(p.astype(BF16), dov)
                dk = dk + _dot_tn(ds, qv)
                dq_ref[a, rows, :] += _dot(ds, kv)
                out.append((dk, dv))
            return tuple(out)

        one = (jnp.zeros((TQ, 2 * DN), F32), jnp.zeros((TQ, DN), F32))
        done = lax.fori_loop(kj + 1, nq, step, step(kj, (one,) * hp, diagonal=True))
        for a, (dk, dv) in enumerate(done):
            dk_ref[a] = dk
            dv_ref[a] = dv

    return pl.pallas_call(
        body, name="flash_bwd", grid=(nb, HEADS // hp, nq),
        in_specs=[pl.BlockSpec((hp, seq, 2 * DN), lambda b, h, j: (h, b, 0)),
                  pl.BlockSpec((hp, TQ, 2 * DN), lambda b, h, j: (h, b * nq + j, 0)),
                  pl.BlockSpec((hp, TQ, DN), lambda b, h, j: (h, b * nq + j, 0)),
                  pl.BlockSpec((seq, hp * DN), lambda b, h, j: (b, h)),
                  pl.BlockSpec((seq, hp * DN), lambda b, h, j: (b, h)),
                  pl.BlockSpec((hp, seq, 1), lambda b, h, j: (h, b, 0)), _ANY],
        out_specs=[pl.BlockSpec((hp, seq, 2 * DN), lambda b, h, j: (h, b, 0)),
                   pl.BlockSpec((hp, TQ, 2 * DN), lambda b, h, j: (h, b * nq + j, 0)),
                   pl.BlockSpec((hp, TQ, DN), lambda b, h, j: (h, b * nq + j, 0))],
        out_shape=[_sds((HEADS, t, 2 * DN), F32), _sds((HEADS, t, 2 * DN), F32), _sds((HEADS, t, DN), F32)],
        compiler_params=_cp("parallel", "parallel", "arbitrary"),
    )(q, k, v, o, do, lse, after)


def mla_pre_bwd(dq, dk, dv, dh1, h, g1, wdn, gq, gkv, wuq, wukv, gqn, gqr, gkn, gkr, cos, sin):
    t = h.shape[0]

    def body(dq_ref, dk_ref, dv_ref, dh1_ref, h_ref, g1_ref, wdn_ref, gq_ref, gkv_ref, wuq_ref, wukv_ref,
             gqn_ref, gqr_ref, gkn_ref, gkr_ref, c_ref, s_ref,
             dh_ref, hn_ref, cq_ref, ckv_ref, dqp_ref, dkvp_ref, dlat_ref,
             dg1_ref, dgq_ref, dgkv_ref, dgqn_ref, dgqr_ref, dgkn_ref, dgkr_ref):
        _zero_at_first_step(dg1_ref, dgq_ref, dgkv_ref, dgqn_ref, dgqr_ref, dgkn_ref, dgkr_ref)
        m = _mla_project(h_ref, g1_ref, wdn_ref, gq_ref, gkv_ref, wuq_ref, wukv_ref)
        hn_ref[...] = m["hn"]
        cq_ref[...] = m["cqb"]
        ckv_ref[...] = m["ckvb"]
        c = c_ref[...]
        s = s_ref[...]
        gqn = gqn_ref[...]
        gqr = gqr_ref[...]
        gkn = gkn_ref[...]
        gkr = gkr_ref[...]

        dkr = dk_ref[0, :, DN:2 * DN]
        for hd in range(1, HEADS):
            dkr = dkr + dk_ref[hd, :, DN:2 * DN]
        dkr = _rope_t(dkr, c, s)
        _, krhat, rkr = _rms(m["kr_raw"], gkr, DR)
        _acc_rows(dgkr_ref, dkr * krhat)
        dkr_raw = _rms_bwd(dkr, gkr, krhat, rkr, DR)

        for hd in range(HEADS):
            ncols = slice(hd * DN, (hd + 1) * DN)
            _, xh, r = _rms(m["qp"][:, ncols], gqn, DN)
            dqn = dq_ref[hd, :, 0:DN] * SM_SCALE
            _acc_rows(dgqn_ref, dqn * xh)
            dqp_ref[:, ncols] = _rms_bwd(dqn, gqn, xh, r, DN).astype(BF16)

            rcols = slice(D + hd * LANES, D + (hd + 1) * LANES)
            _, xh, r = _rms(m["qp"][:, rcols], gqr, DR)
            dqr = _rope_t(dq_ref[hd, :, DN:2 * DN] * SM_SCALE, c, s)
            _acc_rows(dgqr_ref, dqr * xh)
            dqp_ref[:, rcols] = _rms_bwd(dqr, gqr, xh, r, DR).astype(BF16)

            kcols = slice(hd * 2 * DN, hd * 2 * DN + DN)
            _, xh, r = _rms(m["kvp"][:, kcols], gkn, DN)
            dkn = dk_ref[hd, :, 0:DN]
            _acc_rows(dgkn_ref, dkn * xh)
            dkvp_ref[:, kcols] = _rms_bwd(dkn, gkn, xh, r, DN).astype(BF16)
            dkvp_ref[:, hd * 2 * DN + DN:(hd + 1) * 2 * DN] = dv_ref[hd].astype(BF16)

        dcq = _dot_nt(dqp_ref[...], wuq_ref[...])
        _acc_rows(dgq_ref, dcq * m["cqhat"])
        dlat_q = _rms_bwd(dcq, gq_ref[...], m["cqhat"], m["rq"], QL)
        dckv = _dot_nt(dkvp_ref[...], wukv_ref[...])
        _acc_rows(dgkv_ref, dckv * m["ckvhat"])
        dlat_kv = _rms_bwd(dckv, gkv_ref[...], m["ckvhat"], m["rkv"], KVL)
        dlat = jnp.concatenate([dlat_q, dlat_kv, dkr_raw], axis=1).astype(BF16)
        dlat_ref[...] = dlat
        dhn = _dot_nt(dlat, wdn_ref[...])
        _acc_rows(dg1_ref, dhn * m["xhat"])
        dh_ref[...] = dh1_ref[...] + _rms_bwd(dhn, g1_ref[...], m["xhat"], m["rx"], D)

    hb = lambda w: pl.BlockSpec((HEADS, TM, w), lambda i: (0, i, 0))
    return pl.pallas_call(
        body, name="mla_pre_bwd", grid=(t // TM,),
        in_specs=[hb(2 * DN), hb(2 * DN), hb(DN), _row(TM, D), _row(TM, D), _const((1, D)), _const((D, LATP)),
                  _const((1, QL)), _const((1, KVL)), _const((QL, 2 * D)), _const((KVL, 2 * D)),
                  _const((1, LANES)), _const((1, LANES)), _const((1, LANES)), _const((1, LANES)),
                  _row(TM, LANES), _row(TM, LANES)],
        out_specs=[_row(TM, D), _row(TM, D), _row(TM, QL), _row(TM, KVL), _row(TM, 2 * D), _row(TM, 2 * D),
                   _row(TM, LATP), _const((8, D)), _const((8, QL)), _const((8, KVL)), _const((8, LANES)),
                   _const((8, LANES)), _const((8, LANES)), _const((8, LANES))],
        out_shape=[_sds((t, D), F32), _sds((t, D), BF16), _sds((t, QL), BF16), _sds((t, KVL), BF16),
                   _sds((t, 2 * D), BF16), _sds((t, 2 * D), BF16), _sds((t, LATP), BF16),
                   _sds((8, D), F32), _sds((8, QL), F32), _sds((8, KVL), F32), _sds((8, LANES), F32),
                   _sds((8, LANES), F32), _sds((8, LANES), F32), _sds((8, LANES), F32)],
        compiler_params=_cp("arbitrary"),
    )(dq, dk, dv, dh1, h, g1, wdn, gq, gkv, wuq, wukv, gqn, gqr, gkn, gkr, cos, sin)


def gmlp_bwd(dh1, dh1b, h, pre, g1, allw, lay, lng, lnb, wm, wmt, bfull, tril):
    t = h.shape[0]

    def body(dh1_ref, dh1b_ref, h_ref, pre_ref, g1_ref, win_ref, lng_ref, lnb_ref, wm_ref, wmt_ref, b_ref,
             wout_ref, tril_ref, dh_ref, hn_ref, dpre_ref, dws_ref, dbs_ref, dlng_ref, dlnb_ref, dg1_ref,
             dvn_s):
        _zero_at_first_step(dws_ref, dbs_ref, dlng_ref, dlnb_ref, dg1_ref)
        g1 = g1_ref[...]
        yn, xhat, rx = _rms(h_ref[...], g1, D)
        hn_ref[...] = yn.astype(BF16)
        dy = _dot_nt(dh1b_ref[...], _rows_joined(wout_ref))
        pre_u = pre_ref[:, :GH].astype(F32)
        pre_v = pre_ref[:, GH:].astype(F32)
        u, gg_u = _gelu_and_grad(pre_u)
        v, gg_v = _gelu_and_grad(pre_v)
        xc = v - jnp.mean(v, axis=-1, keepdims=True)
        rs = lax.rsqrt(jnp.mean(xc * xc, axis=-1, keepdims=True) + EPS)
        vhat = xc * rs
        lng = lng_ref[...]
        vnb = (vhat * lng + lnb_ref[...]).astype(BF16)
        dsv = dy * u
        dsvb = dsv.astype(BF16)
        tril_m = tril_ref[...]
        for ch in range(TM // GC):
            rows = slice(ch * GC, (ch + 1) * GC)
            dbs_ref[...] += dsv[rows, :]
            for g in range(GG):
                cols = slice(g * GD, (g + 1) * GD)
                sv = _dot(wm_ref[g], vnb[rows, cols]) + b_ref[:, cols]
                dpre_ref[rows, cols] = (dy[rows, cols] * sv * gg_u[rows, cols]).astype(BF16)
                dvn_s[rows, cols] = _dot(wmt_ref[g], dsvb[rows, cols])
                dws_ref[g] += _dot_nt(dsvb[rows, cols], vnb[rows, cols]) * tril_m
        dvn = dvn_s[...]
        _acc_rows(dlng_ref, dvn * vhat)
        _acc_rows(dlnb_ref, dvn)
        dvhat = dvn * lng
        dv = rs * (dvhat - jnp.mean(dvhat, axis=-1, keepdims=True)
                   - vhat * jnp.mean(dvhat * vhat, axis=-1, keepdims=True))
        dpre_v = (dv * gg_v).astype(BF16)
        dpre_ref[:, GH:] = dpre_v
        dhn = _dot_nt(dpre_ref[:, 0:D], win_ref[0])
        for c in range(1, N_CHIPS):
            dhn = dhn + _dot_nt(dpre_ref[:, c * D:(c + 1) * D], win_ref[c])
        _acc_rows(dg1_ref, dhn * xhat)
        dh_ref[...] = dh1_ref[...] + _rms_bwd(dhn, g1, xhat, rx, D)

    return pl.pallas_call(
        body, name="gmlp_bwd", grid=(t // TM,),
        in_specs=[_row(TM, D), _row(TM, D), _row(TM, D), _row(TM, 2 * GH), _const((1, D)), _wblk(D, lay["in"]),
                  _const((1, GH)), _const((1, GH)), _const((GG, GC, GC)), _const((GG, GC, GC)), _const((GC, GH)),
                  _wblk(GH // N_CHIPS, lay["out"]), _const((GC, GC))],
        out_specs=[_row(TM, D), _row(TM, D), _row(TM, 2 * GH), _const((GG, GC, GC)), _const((GC, GH)),
                   _const((8, GH)), _const((8, GH)), _const((8, D))],
        out_shape=[_sds((t, D), F32), _sds((t, D), BF16), _sds((t, 2 * GH), BF16), _sds((GG, GC, GC), F32),
                   _sds((GC, GH), F32), _sds((8, GH), F32), _sds((8, GH), F32), _sds((8, D), F32)],
        scratch_shapes=[pltpu.VMEM((TM, GH), F32)],
        compiler_params=_cp("arbitrary"),
    )(dh1, dh1b, h, pre, g1, allw, lng, lnb, wm, wmt, bfull, allw, tril)


def _token_step(t):
    return 1024 if t % 1024 == 0 else 512


def mm_tn(a, b):
    t, k = a.shape
    n = b.shape[1]
    tk = min(k, 1024)
    tn = min(n, 1024)
    tt = _token_step(t)

    def body(a_ref, b_ref, o_ref):
        @pl.when(pl.program_id(2) == 0)
        def _():
            o_ref[...] = jnp.zeros_like(o_ref)

        o_ref[...] += _dot_tn(a_ref[...].astype(BF16), b_ref[...].astype(BF16))

    return pl.pallas_call(
        body, name="mm_tn", grid=(k // tk, n // tn, t // tt),
        in_specs=[pl.BlockSpec((tt, tk), lambda i, j, s: (s, i)), pl.BlockSpec((tt, tn), lambda i, j, s: (s, j))],
        out_specs=pl.BlockSpec((tk, tn), lambda i, j, s: (i, j)), out_shape=_sds((k, n), F32),
        compiler_params=_cp("parallel", "parallel", "arbitrary"),
    )(a, b)


def mm_tn_into(buf, a, b, rows, row0, col_sharded):
    t = a.shape[0]
    tt = _token_step(t)
    assert row0 % rows == 0 and a.shape[1] == (rows if col_sharded else N_CHIPS * rows), (rows, row0, a.shape)
    assert b.shape[1] == (N_CHIPS * D if col_sharded else D), b.shape
    grid = (1, N_CHIPS, t // tt) if col_sharded else (N_CHIPS, 1, t // tt)
    fresh = isinstance(buf, int)

    def body(*refs):
        a_ref, b_ref, o_ref = refs[-3:]

        @pl.when(pl.program_id(2) == 0)
        def _():
            o_ref[...] = jnp.zeros_like(o_ref)

        o_ref[...] += _dot_tn(a_ref[...].astype(BF16), b_ref[...].astype(BF16))

    specs = [pl.BlockSpec((tt, rows), lambda i, j, s: (s, i)), pl.BlockSpec((tt, D), lambda i, j, s: (s, j))]
    return pl.pallas_call(
        body, name="mm_tn_into", grid=grid,
        in_specs=specs if fresh else [_ANY] + specs,
        out_specs=pl.BlockSpec((None, rows, D), lambda i, j, s: (i + j, row0 // rows, 0)),
        out_shape=_sds((N_CHIPS, buf, D) if fresh else buf.shape, F32),
        input_output_aliases={} if fresh else {0: 0},
        compiler_params=_cp("parallel", "parallel", "arbitrary"),
    )(*((a, b) if fresh else (buf, a, b)))


def adamw(w, g, m, v):
    rows, cols = w.shape
    tr = rows if rows <= 512 else next(r for r in (512, 384, 256, 128) if rows % r == 0)
    c1 = 1.0 - ADAM_B1 ** ADAM_STEP
    c2 = 1.0 - ADAM_B2 ** ADAM_STEP

    def body(w_ref, g_ref, m_ref, v_ref, d_ref, mo_ref, vo_ref):
        gv = g_ref[...]
        mn = ADAM_B1 * m_ref[...] + (1.0 - ADAM_B1) * gv
        vn = ADAM_B2 * v_ref[...] + (1.0 - ADAM_B2) * (gv * gv)
        mo_ref[...] = mn
        vo_ref[...] = vn
        d_ref[...] = -ADAM_LR * ((mn / c1) / (jnp.sqrt(vn / c2) + ADAM_EPS) + ADAM_WD * w_ref[...])

    spec = pl.BlockSpec((tr, cols), lambda i: (i, 0))
    return pl.pallas_call(
        body, name="adamw", grid=(rows // tr,),
        in_specs=[spec] * 4, out_specs=[spec] * 3, out_shape=[_sds((rows, cols), F32)] * 3,
        compiler_params=_cp("parallel"),
    )(w, g, m, v)


def _place():
    return lax.axis_index("x"), lax.axis_index("y"), lax.axis_index("c")


def _other_chips(x, y):
    return [(1 - x, y), (x, 1 - y), (1 - x, 1 - y)]


_ANY = pl.BlockSpec(memory_space=pl.ANY)


_HBM = pl.BlockSpec(memory_space=pltpu.HBM)
_SEM = pl.BlockSpec(memory_space=pltpu.SEMAPHORE)
_EFFECT = pltpu.SideEffectType.DATAFLOW_SIDE_EFFECTING
N_ICI = 3


def _exchange_start(name, src, land, copies, n):
    def body(src_ref, land_ref, *outs):
        sems, token = outs[:2 * n], outs[-1]
        for j, (s, d, to) in enumerate(copies(src_ref, land_ref, _place())):
            pltpu.make_async_remote_copy(src_ref=s, dst_ref=d, send_sem=sems[j], recv_sem=sems[n + j],
                                         device_id=to, device_id_type=MESH).start()
        token[...] = jnp.zeros_like(token)

    sem = pltpu.SemaphoreType.DMA(())
    outs = pl.pallas_call(
        body, name=name,
        out_shape=(sem,) * (2 * n) + (pltpu.HBM(src.shape, src.dtype), pltpu.HBM(land.shape, land.dtype),
                                      _sds((8, LANES), F32)),
        in_specs=(_HBM, _HBM),
        out_specs=(_SEM,) * (2 * n) + (_HBM, _HBM, pl.BlockSpec(memory_space=pltpu.VMEM)),
        input_output_aliases={0: 2 * n, 1: 2 * n + 1},
        compiler_params=pltpu.CompilerParams(has_side_effects=_EFFECT),
    )(pltpu.with_memory_space_constraint(src, pltpu.HBM), pltpu.with_memory_space_constraint(land, pltpu.HBM))
    return outs[:2 * n], outs[2 * n], outs[2 * n + 1], outs[-1]


def _exchange_wait(name, sems, src, land, after, arrivals):
    n = len(sems) // 2

    def body(src_ref, land_ref, *rest):
        sems = rest[:2 * n]
        for j, (s, d) in enumerate(arrivals(src_ref, land_ref, _place())):
            cp = pltpu.make_async_remote_copy(src_ref=s, dst_ref=d, send_sem=sems[j], recv_sem=sems[n + j],
                                              device_id=_place(), device_id_type=MESH)
            cp.wait_send()
            cp.wait_recv()

    return pl.pallas_call(
        body, name=name, out_shape=(pltpu.HBM(src.shape, src.dtype), pltpu.HBM(land.shape, land.dtype)),
        in_specs=(_HBM, _HBM) + (_SEM,) * (2 * n) + (_ANY,), out_specs=(_HBM, _HBM),
        input_output_aliases={0: 0, 1: 1},
        compiler_params=pltpu.CompilerParams(has_side_effects=_EFFECT),
    )(src, land, *sems, after)


def _halves(c, hh):
    return pl.ds(pl.multiple_of(c * hh, 16), hh), pl.ds(pl.multiple_of((1 - c) * hh, 16), hh)


def gather_start(land, tag):
    _, rr, _ = land.shape
    assert rr % 32 == 0, rr

    def copies(_, land_ref, place):
        x, y, c = place
        mine = land_ref.at[2 * x + y, _halves(c, rr // 2)[0]]
        return [(mine, mine, (cx, cy, c)) for cx, cy in _other_chips(x, y)]

    return _exchange_start(f"gather_start_{tag}", jnp.zeros((8, LANES), F32), land, copies, N_ICI)


def gather_wait(sems, src, land, after, tag):
    def arrivals(_, land_ref, place):
        x, y, c = place
        half = _halves(c, land.shape[1] // 2)[0]
        return [(land_ref.at[2 * x + y, half], land_ref.at[2 * cx + cy, half]) for cx, cy in _other_chips(x, y)]

    return _exchange_wait(f"gather_wait_{tag}", sems, src, land, after, arrivals)


def pass_start(land, tag):
    def copies(_, land_ref, place):
        x, y, c = place
        half = _halves(c, land.shape[1] // 2)[0]
        return [(land_ref.at[2 * cx + cy, half], land_ref.at[2 * cx + cy, half], (x, y, 1 - c))
                for cx, cy in _other_chips(x, y)]

    return _exchange_start(f"pass_start_{tag}", jnp.zeros((8, LANES), F32), land, copies, N_ICI)


def pass_wait(sems, src, land, after, tag):
    def arrivals(_, land_ref, place):
        x, y, c = place
        mine, other = _halves(c, land.shape[1] // 2)
        return [(land_ref.at[2 * cx + cy, mine], land_ref.at[2 * cx + cy, other]) for cx, cy in _other_chips(x, y)]

    return _exchange_wait(f"pass_wait_{tag}", sems, src, land, after, arrivals)


def swap_start(g, tag):
    _, rr, cc = g.shape

    def copies(g_ref, got_ref, place):
        x, y, c = place
        other = _halves(c, rr // 2)[1]
        return [(g_ref.at[k, other], got_ref.at[k], (x, y, 1 - c)) for k in range(N_CHIPS)]

    return _exchange_start(f"swap_start_{tag}", g, lax.empty((N_CHIPS, rr // 2, cc), g.dtype), copies, N_CHIPS)


def swap_wait(sems, g, got, after, tag):
    def arrivals(g_ref, got_ref, place):
        other = _halves(place[2], g.shape[1] // 2)[1]
        return [(g_ref.at[k, other], got_ref.at[k]) for k in range(N_CHIPS)]

    return _exchange_wait(f"swap_wait_{tag}", sems, g, got, after, arrivals)


def chip_sum(place, g32, got):
    _, rr, cc = g32.shape
    hh = rr // 2
    tr = SUM_ROWS
    assert rr % 2 == 0 and hh % tr == 0, (rr, tr)
    nb = hh // tr

    def body(place_ref, g_ref, got_ref, own_ref, all_ref):
        s = g_ref[...] + got_ref[...].astype(F32)
        all_ref[...] = s.astype(BF16)
        own_ref[...] = g_ref[place_ref[1]] + got_ref[place_ref[1]].astype(F32)

    return pl.pallas_call(
        body, name="chip_sum",
        grid_spec=pltpu.PrefetchScalarGridSpec(
            num_scalar_prefetch=1, grid=(nb,),
            in_specs=[pl.BlockSpec((N_CHIPS, tr, cc), lambda i, pr: (0, pr[0] * nb + i, 0)),
                      pl.BlockSpec((N_CHIPS, tr, cc), lambda i, pr: (0, i, 0))],
            out_specs=[pl.BlockSpec((tr, cc), lambda i, pr: (i, 0)),
                       pl.BlockSpec((N_CHIPS, tr, cc), lambda i, pr: (0, i, 0))]),
        out_shape=[_sds((hh, cc), F32), _sds((N_CHIPS, hh, cc), BF16)],
        compiler_params=_cp("parallel"),
    )(place, g32, got)


def _scatter_copies(s_ref, land_ref, place):
    x, y, c = place
    return [(s_ref.at[2 * cx + cy], land_ref.at[j], (cx, cy, c)) for j, (cx, cy) in enumerate(_other_chips(x, y))]


def scatter_start(s, tag):
    return _exchange_start(f"scatter_start_{tag}", s, lax.empty((N_ICI,) + s.shape[1:], s.dtype), _scatter_copies, N_ICI)


def scatter_wait(sems, s, land, after, tag):
    return _exchange_wait(f"scatter_wait_{tag}", sems, s, land, after,
                          lambda s_ref, land_ref, place: [(a, b) for a, b, _ in _scatter_copies(s_ref, land_ref, place)])


def final_sum(place, own, got):
    hh, cc = own.shape
    tr = SUM_ROWS
    assert hh % tr == 0, (hh, tr)
    nb = hh // tr

    def body(place_ref, own_ref, got_ref, o_ref):
        del place_ref
        o_ref[...] = ((own_ref[...] + got_ref[0].astype(F32)) + got_ref[1].astype(F32)) + got_ref[2].astype(F32)

    return pl.pallas_call(
        body, name="final_sum",
        grid_spec=pltpu.PrefetchScalarGridSpec(
            num_scalar_prefetch=1, grid=(nb,),
            in_specs=[pl.BlockSpec((tr, cc), lambda i, pr: (i, 0)), pl.BlockSpec((3, tr, cc), lambda i, pr: (0, i, 0))],
            out_specs=pl.BlockSpec((tr, cc), lambda i, pr: (pr[0] * nb + i, 0))),
        out_shape=_sds((2 * hh, cc), F32),
        compiler_params=_cp("parallel"),
    )(place, own, got)


def share_start(f, tag):
    def copies(_, f_ref, place):
        x, y, c = place
        mine = f_ref.at[_halves(c, f.shape[0] // 2)[0]]
        return [(mine, mine, (x, y, 1 - c))]

    return _exchange_start(f"share_start_{tag}", jnp.zeros((8, LANES), F32), f, copies, 1)


def share_wait(sems, src, f, after, tag):
    def arrivals(_, f_ref, place):
        mine, other = _halves(place[2], f.shape[0] // 2)
        return [(f_ref.at[mine], f_ref.at[other])]

    return _exchange_wait(f"share_wait_{tag}", sems, src, f, after, arrivals)


N_DEV = 8


def _peers(place):
    x, y, c = place
    return [((1 - x) if r & 4 else x, (1 - y) if r & 2 else y, (1 - c) if r & 1 else c) for r in range(1, N_DEV)]


def _device_index(place):
    x, y, c = place
    return 4 * x + 2 * y + c


def small_start(land):
    def copies(_, land_ref, place):
        mine = land_ref.at[_device_index(place)]
        return [(mine, mine, to) for to in _peers(place)]

    return _exchange_start("small_start", jnp.zeros((8, LANES), F32), land, copies, N_DEV - 1)


def small_wait(sems, src, land, after):
    def arrivals(_, land_ref, place):
        return [(land_ref.at[_device_index(place)], land_ref.at[_device_index(peer)]) for peer in _peers(place)]

    return _exchange_wait("small_wait", sems, src, land, after, arrivals)


def sum_devices(land):
    _, rr, cc = land.shape
    tr = 56
    assert rr % tr == 0, rr

    def body(l_ref, o_ref):
        acc = l_ref[0]
        for d in range(1, N_DEV):
            acc = acc + l_ref[d]
        o_ref[...] = acc

    return pl.pallas_call(
        body, name="sum_devices", grid=(rr // tr,),
        in_specs=[pl.BlockSpec((N_DEV, tr, cc), lambda i: (0, i, 0))],
        out_specs=pl.BlockSpec((tr, cc), lambda i: (i, 0)), out_shape=_sds((rr, cc), F32),
        compiler_params=_cp("parallel"),
    )(land)


_BIG = ["mla_w_down", "mla_w_uq", "mla_w_ukv", "mla_w_out", "gmlp_w_in", "gmlp_w_out", "ffn_w_up", "ffn_w_down",
        "ple_w_gate", "ple_w_proj"]
_SMALL = ["norm_mix", "norm_ffn", "norm_ple", "mla_q_lora_g", "mla_kv_lora_g", "mla_q_nope_g", "mla_q_rope_g",
          "mla_k_nope_g", "mla_k_rope_g", "gmlp_ln_g", "gmlp_ln_b", "gmlp_w_s", "gmlp_b_s"]

_LAY_MLA = dict(up=0, down=1024, out=2048, gate=2304, wdn=2560, wuq=2736, wukv=2880, proj=3008, rows=3072)
_LAY_MLA_MAIN = dict(up=0, down=1024, out=2048, gate=2304, rows=2560)
_LAY_MLA_ODD = dict(wdn=0, wuq=176, wukv=320, proj=448, rows=512)
_LAY_GMLP = {"up": 0, "down": 1024, "in": 2048, "out": 3072, "gate": 3584, "proj": 3840, "ln": 3904, "rows": 4096}
SPLIT_LAYERS = (0,)


def _layer_units(i):
    j = i // 2
    if i % 2 == 0:
        odd, lay = (_LAY_MLA_ODD, _LAY_MLA_MAIN) if i in SPLIT_LAYERS else (_LAY_MLA, _LAY_MLA)
        small = [("mla_w_down", j, odd["wdn"]), ("mla_w_uq", j, odd["wuq"]), ("mla_w_ukv", j, odd["wukv"]),
                 ("ple_w_proj", i, odd["proj"])]
        large = [("ffn_w_up", i, lay["up"]), ("ffn_w_down", i, lay["down"]), ("mla_w_out", j, lay["out"]),
                 ("ple_w_gate", i, lay["gate"])]
        return [("odd", odd, small), ("main", lay, large)] if i in SPLIT_LAYERS else [("main", lay, large + small)]
    lay = _LAY_GMLP
    return [("main", lay, [("ffn_w_up", i, lay["up"]), ("ffn_w_down", i, lay["down"]), ("gmlp_w_in", j, lay["in"]),
                           ("gmlp_w_out", j, lay["out"]), ("ple_w_gate", i, lay["gate"]),
                           ("ple_w_proj", i, lay["proj"])])]


def _pack_rows(parts, dtype, pad_to=None, slot=False):
    size = sum(p.size for p in parts)
    tail = [] if pad_to is None or pad_to * D == size else [jnp.zeros((pad_to * D - size,), dtype)]
    shape = (1, -1, D) if slot else (-1, D)
    if all(p.size % D == 0 for p in parts + tail):
        return jnp.concatenate([p.astype(dtype).reshape(shape) for p in parts + tail], axis=len(shape) - 2)
    return jnp.concatenate([p.astype(dtype).reshape(-1) for p in parts + tail]).reshape(shape)


def _odd(allw, row0, a, b):
    return allw[:, row0:row0 + a * b // D].reshape(N_CHIPS, a, b)


def _cols_joined(s):
    return jnp.transpose(s, (1, 0, 2)).reshape(s.shape[1], N_CHIPS * s.shape[2])


def _col_shards(full):
    a, bb = full.shape
    return jnp.transpose(full.reshape(a, N_CHIPS, bb // N_CHIPS), (1, 0, 2)).reshape(N_CHIPS, -1, D)


def _pad_lanes(g):
    return jnp.pad(g, ((0, 0), (0, LANES - g.shape[1])))


def _split_uq(wuq):
    l = wuq.shape[0]
    w = wuq.reshape(l, QL, HEADS, DN + DR)
    nope = w[..., :DN].reshape(l, QL, HEADS * DN)
    rope = jnp.pad(w[..., DN:], ((0, 0), (0, 0), (0, 0), (0, LANES - DR))).reshape(l, QL, HEADS * LANES)
    return jnp.concatenate([nope, rope], axis=-1)


def _merge_uq(d):
    nope = d[:, :HEADS * DN].reshape(QL, HEADS, DN)
    rope = d[:, HEADS * DN:].reshape(QL, HEADS, LANES)[..., :DR]
    return jnp.concatenate([nope, rope], axis=-1).reshape(QL, HEADS * (DN + DR))


def _rope_tables(positions):
    inv_freq = ROPE_BASE ** (-(jnp.arange(0, DR, 2, dtype=F32) / DR))
    ang = positions.reshape(-1).astype(F32)[:, None] * inv_freq
    z = jnp.zeros((ang.shape[0], LANES - DR), F32)
    return (jnp.concatenate([jnp.cos(ang), jnp.cos(ang), z], axis=1),
            jnp.concatenate([jnp.sin(ang), jnp.sin(ang), z], axis=1))


def kernel(x, p, positions, norm_mix, norm_ffn, norm_ple, mla_w_down, mla_q_lora_g, mla_kv_lora_g, mla_w_uq, mla_w_ukv, mla_q_nope_g, mla_q_rope_g, mla_k_nope_g, mla_k_rope_g, mla_w_out, gmlp_w_in, gmlp_ln_g, gmlp_ln_b, gmlp_w_s, gmlp_b_s, gmlp_w_out, ffn_w_up, ffn_w_down, ple_w_gate, ple_w_proj, loss_target, m_norm_mix, m_norm_ffn, m_norm_ple, m_mla_w_down, m_mla_q_lora_g, m_mla_kv_lora_g, m_mla_w_uq, m_mla_w_ukv, m_mla_q_nope_g, m_mla_q_rope_g, m_mla_k_nope_g, m_mla_k_rope_g, m_mla_w_out, m_gmlp_w_in, m_gmlp_ln_g, m_gmlp_ln_b, m_gmlp_w_s, m_gmlp_b_s, m_gmlp_w_out, m_ffn_w_up, m_ffn_w_down, m_ple_w_gate, m_ple_w_proj, v_norm_mix, v_norm_ffn, v_norm_ple, v_mla_w_down, v_mla_q_lora_g, v_mla_kv_lora_g, v_mla_w_uq, v_mla_w_ukv, v_mla_q_nope_g, v_mla_q_rope_g, v_mla_k_nope_g, v_mla_k_rope_g, v_mla_w_out, v_gmlp_w_in, v_gmlp_ln_g, v_gmlp_ln_b, v_gmlp_w_s, v_gmlp_b_s, v_gmlp_w_out, v_ffn_w_up, v_ffn_w_down, v_ple_w_gate, v_ple_w_proj):
    args = dict(locals())
    weights = {n: args[n] for n in _BIG + _SMALL}
    depth = norm_mix.shape[0]
    nb, seq, _ = x.shape
    t = nb * seq
    assert seq % TQ == 0 and seq % TM == 0 and t % 512 == 0, (nb, seq)
    cx = lax.axis_index("x")
    cy = lax.axis_index("y")
    cc = lax.axis_index("c")
    chip = 2 * cx + cy

    gathers = {}
    token = None
    for i in range(depth):
        for key, lay, parts in _layer_units(i):
            rows = [weights[n][l] for n, l, _ in parts]
            if token is not None:
                rows[0] = rows[0] + token[0, 0]
            if "ln" in lay:
                ln = jnp.stack([gmlp_ln_g[i // 2], gmlp_ln_b[i // 2]]).astype(F32)
                bits = lax.bitcast_convert_type(ln, BF16).reshape(-1)
                rows.append(jnp.pad(bits, (0, 16 * D - bits.size)).reshape(16, D))
            mine = _pack_rows(rows, BF16, pad_to=lay["rows"], slot=True)
            land = lax.dynamic_update_slice(lax.empty((N_CHIPS, lay["rows"], D), BF16), mine, (chip, 0, 0))
            sems, src, land, token = gather_start(land, f"{i}{key}")
            gathers[i, key] = (sems, src, land)
    allw = [None] * depth

    tril = jnp.tril(jnp.ones((GC, GC), F32))
    wm = (gmlp_w_s * tril).astype(BF16)
    wmt = jnp.swapaxes(wm, -1, -2)
    bfull = jnp.repeat(jnp.swapaxes(gmlp_b_s, -1, -2), GD, axis=-1)
    cos, sin = _rope_tables(positions)
    row = lambda g: g.reshape(1, -1)
    gqr = _pad_lanes(mla_q_rope_g)
    gkr = _pad_lanes(mla_k_rope_g)

    h = x.reshape(t, D)
    pt = p.reshape(depth, t, PLE)
    saved = []

    passing = {}

    def arrive(i, key, after):
        sems, src, land = gathers[i, key]
        _, land = gather_wait(sems, src, land, after, f"{i}{key}")
        passing[i, key] = pass_start(land, f"{i}{key}")
        return passing[i, key][3]

    def needed(i, key, after=None):
        sems, src, land, tok = passing.pop((i, key))
        return pass_wait(sems, src, land, tok if after is None else after, f"{i}{key}")[1]

    arrive(0, _layer_units(0)[0][0], token)
    for i in range(depth):
        j = i // 2
        lay = _layer_units(i)[-1][1]
        s = dict(h=h)
        if i % 2 == 0:
            split = i in SPLIT_LAYERS
            olay = _layer_units(i)[0][1]
            odd = needed(i, "odd" if split else "main", None if i == 0 else h)
            wdn = jnp.pad(_odd(odd, olay["wdn"], D // N_CHIPS, LAT).reshape(D, LAT), ((0, 0), (0, LATP - LAT)))
            wuq = _split_uq(_cols_joined(_odd(odd, olay["wuq"], QL, 384))[None])[0]
            wukv = _cols_joined(_odd(odd, olay["wukv"], KVL, 512))
            wp = _cols_joined(_odd(odd, olay["proj"], PLE, 256))
            mla_args = (row(norm_mix[i]), wdn, row(mla_q_lora_g[j]), row(mla_kv_lora_g[j]), wuq, wukv,
                        row(mla_q_nope_g[j]), gqr[j:j + 1], row(mla_k_nope_g[j]), gkr[j:j + 1], cos, sin)
            q, k, v = mla_pre_fwd(h, *mla_args)
            y, lse = flash_fwd(q, k, v, seq)
            if split and i == 0:
                arrive(i, "main", y)
            aw = needed(i, "main", y) if split else odd
            s.update(q=q, k=k, v=v, lse=lse, mla_args=mla_args)
        else:
            aw = needed(i, "main", h)
            ln = lax.bitcast_convert_type(aw[:, lay["ln"]:lay["ln"] + 2].reshape(N_CHIPS, 2, GH // N_CHIPS, 2), F32)
            ln = jnp.transpose(ln, (1, 0, 2)).reshape(2, 1, GH)
            wp = _cols_joined(_odd(aw, lay["proj"], PLE, 256))
            y, pre = gmlp_fwd(h, row(norm_mix[i]), aw, lay, ln[0], ln[1], wm[j], bfull[j])
            s.update(pre=pre, ln=ln)
        allw[i] = aw
        g2 = row(norm_ffn[i])
        if i + 1 < depth:
            for key, _, _ in _layer_units(i + 1):
                g2 = g2 + arrive(i + 1, key, y)[0:1, 0:1]
        h1, h2, hn2, r = mixffn_fwd(h, y, aw, lay, g2)
        h, hn3 = ple_fwd(h2, pt[i], row(norm_ple[i]), aw, lay, wp)
        s.update(y=y, wp=wp, h1=h1, h2=h2, hn2=hn2, r=r, hn3=hn3)
        saved.append(s)

    dh, loss_part = loss_head(h, loss_target.reshape(t, D))
    loss = lax.psum(loss_part[0, 0], ("x", "y", "c"))

    gs = {n: [None] * weights[n].shape[0] for n in _SMALL}
    gw = {n: [None] * weights[n].shape[0] for n in _BIG}
    place = jnp.stack([cc, chip]).astype(jnp.int32)
    scatters = []
    swaps = []
    token = None

    def put(b, row0, shards):
        return lax.dynamic_update_slice(b, shards.reshape(N_CHIPS, -1, D), (0, row0, 0))

    def swap(i, key, buf):
        sems, buf, got, tok = swap_start(buf, f"{i}{key}")
        swaps.append((i, key, sems, buf, got))
        return tok

    def swapped(after, zero):
        while swaps:
            i, key, sems, g, got = swaps.pop(0)
            g, got = swap_wait(sems, g, got, after, f"{i}{key}")
            own, sums = chip_sum(place, g, got)
            sems, sums, land, tok = scatter_start(sums, f"{i}{key}")
            scatters.append((i, key, own, sems, sums, land))
            zero = zero + tok[0:1, 0:1]
        return zero

    for i in reversed(range(depth)):
        j = i // 2
        lay = _layer_units(i)[-1][1]
        aw = allw[i]
        s = saved[i]

        g3 = row(norm_ple[i])
        if token is not None:
            g3 = g3 + token[0:1, 0:1]
        dh2, dh2b, dgt, dpp, dg3 = ple_bwd(dh, s["h2"], pt[i], g3, aw, lay, s["wp"])
        gs["norm_ple"][i] = dg3[0]
        buf = mm_tn_into(lay["rows"], s["hn3"], dgt, D // N_CHIPS, lay["gate"], False)
        dproj = _col_shards(mm_tn(pt[i], dpp))
        if "ln" in lay:
            buf = put(buf, lay["ln"], jnp.zeros((N_CHIPS, lay["rows"] - lay["ln"], D), F32))
            buf = put(buf, lay["proj"], dproj)
        dh1, dh1b, du, a, dg2 = ffn_bwd(dh2, dh2b, s["h1"], s["r"], row(norm_ffn[i]), aw, lay)
        gs["norm_ffn"][i] = dg2[0]
        buf = mm_tn_into(buf, a, dh2b, D, lay["down"], False)
        buf = mm_tn_into(buf, s["hn2"], du, D, lay["up"], True)
        buf = mm_tn_into(buf, s["y"], dh1b, s["y"].shape[1] // N_CHIPS, lay["out"], False)
        g1 = swapped(dh1, row(norm_mix[i]))
        if i % 2 == 0:
            split = i in SPLIT_LAYERS
            do = linear_nt(dh1b, aw, D // N_CHIPS, lay["out"])
            dq, dk, dv = flash_bwd(s["q"], s["k"], s["v"], s["y"], do, s["lse"], seq,
                                   after=swap(i, "main", buf) if split else dh1b)
            g1 = swapped(dq, g1)
            (dh, hn1, cq, ckv, dqp, dkvp, dlat, dg1, dgq, dgkv, dgqn, dgqr, dgkn, dgkr) = mla_pre_bwd(
                dq, dk, dv, dh1, s["h"], g1, *s["mla_args"][1:])
            gs["norm_mix"][i] = dg1[0]
            gs["mla_q_lora_g"][j] = dgq[0]
            gs["mla_kv_lora_g"][j] = dgkv[0]
            gs["mla_q_nope_g"][j] = dgqn[0]
            gs["mla_q_rope_g"][j] = dgqr[0, :DR]
            gs["mla_k_nope_g"][j] = dgkn[0]
            gs["mla_k_rope_g"][j] = dgkr[0, :DR]
            small = [mm_tn(hn1, dlat)[:, :LAT].reshape(N_CHIPS, -1, D), _col_shards(_merge_uq(mm_tn(cq, dqp))),
                     _col_shards(mm_tn(ckv, dkvp)), dproj]
            if split:
                buf = jnp.concatenate(small, axis=1)
            else:
                buf = put(buf, lay["wdn"], jnp.concatenate(small, axis=1))
            key = "odd" if split else "main"
        else:
            dh, hn1, dpre, dws, dbs, dlng, dlnb, dg1 = gmlp_bwd(
                dh1, dh1b, s["h"], s["pre"], g1, aw, lay, s["ln"][0], s["ln"][1], wm[j], wmt[j], bfull[j], tril)
            gs["norm_mix"][i] = dg1[0]
            gs["gmlp_ln_g"][j] = dlng[0]
            gs["gmlp_ln_b"][j] = dlnb[0]
            gs["gmlp_w_s"][j] = dws
            gs["gmlp_b_s"][j] = jnp.sum(dbs.reshape(GC, GG, GD), axis=-1).T
            buf = mm_tn_into(buf, hn1, dpre, D, lay["in"], True)
            key = "main"
        token = swap(i, key, buf)
    last = swapped(dh, jnp.zeros((1, 1), F32))
    grad_x = dh.reshape(x.shape)

    small_sizes = [weights[n].size if n not in ("gmlp_ln_g", "gmlp_ln_b") else weights[n].shape[0] * GH
                   for n in _SMALL]
    small_rows = -(-sum(small_sizes) // (56 * D)) * 56
    part = [jnp.stack(gs[n]) for n in _SMALL]
    part = _pack_rows([part[0] + last[0, 0]] + part[1:], F32, pad_to=small_rows, slot=True)
    land = lax.dynamic_update_slice(lax.empty((N_DEV, small_rows, D), F32), part, (2 * chip + cc, 0, 0))
    small = small_start(land)

    after = small[3]
    shares = []
    for i, key, own, sems, sums, land in scatters:
        _, got = scatter_wait(sems, sums, land, after, f"{i}{key}")
        sems, src, full, after = share_start(final_sum(place, own, got), f"{i}{key}")
        shares.append((i, key, sems, src, full))
    for i, key, sems, src, full in shares:
        _, after = share_wait(sems, src, full, after, f"{i}{key}")
        for n, l, row0 in dict((k, parts) for k, _, parts in _layer_units(i))[key]:
            gw[n][l] = after[row0:row0 + weights[n][l].size // D].reshape(weights[n].shape[1:])
    grads = {n: jnp.stack(gw[n]) for n in _BIG}

    tot = sum_devices(small_wait(small[0], small[1], small[2], after)[1]).reshape(-1)
    off = 0
    for n, sz in zip(_SMALL, small_sizes):
        gsum = tot[off:off + sz]
        off += sz
        if n in ("gmlp_ln_g", "gmlp_ln_b"):
            gsum = lax.dynamic_slice_in_dim(gsum.reshape(-1, GH), chip * (GH // N_CHIPS), GH // N_CHIPS, axis=1)
        grads[n] = gsum.reshape(weights[n].shape)

    delta, new_m, new_v = {}, {}, {}
    for n in _BIG:
        w2 = weights[n].reshape(-1, weights[n].shape[-1])
        d, mn, vn = adamw(w2, grads[n].reshape(w2.shape), args["m_" + n].reshape(w2.shape),
                          args["v_" + n].reshape(w2.shape))
        delta[n], new_m[n], new_v[n] = (a.reshape(weights[n].shape) for a in (d, mn, vn))
    own_sizes = [weights[n].size for n in _SMALL]
    own_rows = -(-sum(own_sizes) // (8 * D)) * 8
    packed = [_pack_rows([src[n] for n in _SMALL], F32, pad_to=own_rows)
              for src in (weights, grads, {n: args["m_" + n] for n in _SMALL}, {n: args["v_" + n] for n in _SMALL})]
    outs = adamw(*packed)
    off = 0
    for n, sz in zip(_SMALL, own_sizes):
        for dst, o in zip((delta, new_m, new_v), outs):
            dst[n] = o.reshape(-1)[off:off + sz].reshape(weights[n].shape)
        off += sz

    order = ["norm_mix", "norm_ffn", "norm_ple", "mla_w_down", "mla_q_lora_g", "mla_kv_lora_g", "mla_w_uq",
             "mla_w_ukv", "mla_q_nope_g", "mla_q_rope_g", "mla_k_nope_g", "mla_k_rope_g", "mla_w_out", "gmlp_w_in",
             "gmlp_ln_g", "gmlp_ln_b", "gmlp_w_s", "gmlp_b_s", "gmlp_w_out", "ffn_w_up", "ffn_w_down", "ple_w_gate",
             "ple_w_proj"]
    return (loss, grad_x, *[grads[n] for n in order], *[delta[n] for n in order], *[new_m[n] for n in order],
            *[new_v[n] for n in order])
```

```python
import functools

import jax
import jax.numpy as jnp
from jax import lax
from jax.experimental import pallas as pl
from jax.experimental.pallas import tpu as pltpu

F32 = jnp.float32
BF16 = jnp.bfloat16
MESH = pl.DeviceIdType.MESH

D = 1024
HEADS = 8
DN = 128
DR = 64
QL = 384
KVL = 256
LAT = 704
LATP = 768
DFF = 4096
GH = 2048
GC = 128
GG = 8
GD = 256
PLE = 256
EPS = 1e-6
ROPE_BASE = 10000.0
SM_SCALE = (DN + DR) ** -0.5
N_CHIPS = 4
LANES = 128

ADAM_LR = 0.001
ADAM_B1 = 0.9
ADAM_B2 = 0.999
ADAM_EPS = 1e-08
ADAM_WD = 0.01
ADAM_STEP = 10

TM = 256
TMB = 512
TQ = 512
TQ_FWD = 512
FWD_HEADS = 2
BWD_HEADS = 2
SUM_ROWS = 256
VMEM_LIMIT = 56 * 1024 * 1024


def _cp(*sem):
    return pltpu.CompilerParams(dimension_semantics=sem, vmem_limit_bytes=VMEM_LIMIT)


def _dot(a, b):
    return jnp.dot(a, b, preferred_element_type=F32)


def _dot_nt(a, b):
    return lax.dot_general(a, b, (((1,), (1,)), ((), ())), preferred_element_type=F32)


def _dot_tn(a, b):
    return lax.dot_general(a, b, (((0,), (0,)), ((), ())), preferred_element_type=F32)


def _rms(x, g, n):
    r = lax.rsqrt(jnp.sum(x * x, axis=-1, keepdims=True) * (1.0 / n) + EPS)
    xhat = x * r
    return xhat * g, xhat, r


def _rms_bwd(dy, g, xhat, r, n):
    dxhat = dy * g
    return r * (dxhat - xhat * (jnp.sum(dxhat * xhat, axis=-1, keepdims=True) * (1.0 / n)))


def _rope(x, c, s):
    return x * c + (pltpu.roll(x, 32, 1) - pltpu.roll(x, 96, 1)) * s


def _rope_t(dy, c, s):
    w = dy * s
    return dy * c + pltpu.roll(w, 96, 1) - pltpu.roll(w, 32, 1)


def _sigmoid(x):
    return 1.0 / (1.0 + jnp.exp(-x))


_GELU_K = 0.7978845608028654
_GELU_C = 0.044715


def _gelu(x):
    return 0.5 * x * (1.0 + jnp.tanh(_GELU_K * (x + _GELU_C * x * x * x)))


def _gelu_and_grad(x):
    x2 = x * x
    t = jnp.tanh(_GELU_K * (x + _GELU_C * x2 * x))
    half = 0.5 * (1.0 + t)
    return x * half, half + 0.5 * x * (1.0 - t * t) * (_GELU_K * (1.0 + 3.0 * _GELU_C * x2))


def _acc_rows(ref, val):
    ref[...] += jnp.broadcast_to(jnp.sum(val, axis=0, keepdims=True), ref.shape)


def _row(tm, c):
    return pl.BlockSpec((tm, c), lambda i: (i, 0))


def _const(shape):
    nd = len(shape)
    return pl.BlockSpec(shape, lambda i: (0,) * nd, pipeline_mode=pl.Buffered(1))


def _wblk(rows, row0):
    assert row0 % rows == 0, (rows, row0)
    return pl.BlockSpec((N_CHIPS, rows, D), lambda i: (0, row0 // rows, 0), pipeline_mode=pl.Buffered(1))


def _rows_joined(w_ref):
    return w_ref[...].reshape(N_CHIPS * w_ref.shape[1], D)


def _sds(shape, dtype):
    return jax.ShapeDtypeStruct(shape, dtype)


def mixffn_fwd(h, y, allw, lay, g2):
    t, k = y.shape

    def body(h_ref, y_ref, wo_ref, g_ref, wu_ref, wd_ref, h1_ref, h2_ref, hn_ref, r_ref):
        h1 = h_ref[...] + _dot(y_ref[...], _rows_joined(wo_ref))
        h1_ref[...] = h1
        yn, _, _ = _rms(h1, g_ref[...], D)
        hn = yn.astype(BF16)
        hn_ref[...] = hn
        f = jnp.zeros((TMB, D), F32)
        for c in range(N_CHIPS):
            r = jnp.maximum(_dot(hn, wu_ref[c]), 0.0)
            r_ref[:, c * D:(c + 1) * D] = r.astype(BF16)
            f = f + _dot((r * r).astype(BF16), wd_ref[c])
        h2_ref[...] = h1 + f

    return pl.pallas_call(
        body, name="mixffn_fwd", grid=(t // TMB,),
        in_specs=[_row(TMB, D), _row(TMB, k), _wblk(k // N_CHIPS, lay["out"]), _const((1, D)), _wblk(D, lay["up"]),
                  _wblk(D, lay["down"])],
        out_specs=[_row(TMB, D), _row(TMB, D), _row(TMB, D), _row(TMB, DFF)],
        out_shape=[_sds((t, D), F32), _sds((t, D), F32), _sds((t, D), BF16), _sds((t, DFF), BF16)],
        compiler_params=_cp("parallel"),
    )(h, y, allw, g2, allw, allw)


def ple_fwd(h2, p, g3, allw, lay, wp):
    t = h2.shape[0]

    def body(h_ref, p_ref, g_ref, wg_ref, wp_ref, h3_ref, hn_ref):
        x = h_ref[...]
        yn, _, _ = _rms(x, g_ref[...], D)
        hn = yn.astype(BF16)
        hn_ref[...] = hn
        gt = _dot(hn, _rows_joined(wg_ref))
        pp = _dot(p_ref[...].astype(BF16), wp_ref[...])
        h3_ref[...] = x + _sigmoid(gt) * pp

    return pl.pallas_call(
        body, name="ple_fwd", grid=(t // TMB,),
        in_specs=[_row(TMB, D), _row(TMB, PLE), _const((1, D)), _wblk(D // N_CHIPS, lay["gate"]), _const((PLE, D))],
        out_specs=[_row(TMB, D), _row(TMB, D)],
        out_shape=[_sds((t, D), F32), _sds((t, D), BF16)],
        compiler_params=_cp("parallel"),
    )(h2, p, g3, allw, wp)


def _mla_project(h_ref, g1_ref, wdn_ref, gq_ref, gkv_ref, wuq_ref, wukv_ref):
    x = h_ref[...]
    yn, xhat, rx = _rms(x, g1_ref[...], D)
    hn = yn.astype(BF16)
    lat = _dot(hn, wdn_ref[...])
    cq, cqhat, rq = _rms(lat[:, :QL], gq_ref[...], QL)
    ckv, ckvhat, rkv = _rms(lat[:, QL:QL + KVL], gkv_ref[...], KVL)
    kr_raw = lat[:, QL + KVL:]
    cqb = cq.astype(BF16)
    ckvb = ckv.astype(BF16)
    qp = _dot(cqb, wuq_ref[...])
    kvp = _dot(ckvb, wukv_ref[...])
    return dict(xhat=xhat, rx=rx, hn=hn, cqhat=cqhat, rq=rq, ckvhat=ckvhat, rkv=rkv, kr_raw=kr_raw,
                cqb=cqb, ckvb=ckvb, qp=qp, kvp=kvp)


def mla_pre_fwd(h, g1, wdn, gq, gkv, wuq, wukv, gqn, gqr, gkn, gkr, cos, sin):
    t = h.shape[0]

    def body(h_ref, g1_ref, wdn_ref, gq_ref, gkv_ref, wuq_ref, wukv_ref, gqn_ref, gqr_ref, gkn_ref, gkr_ref,
             c_ref, s_ref, q_ref, k_ref, v_ref):
        m = _mla_project(h_ref, g1_ref, wdn_ref, gq_ref, gkv_ref, wuq_ref, wukv_ref)
        c = c_ref[...]
        s = s_ref[...]
        kr, _, _ = _rms(m["kr_raw"], gkr_ref[...], DR)
        krb = _rope(kr, c, s).astype(BF16)
        for hd in range(HEADS):
            qn, _, _ = _rms(m["qp"][:, hd * DN:(hd + 1) * DN], gqn_ref[...], DN)
            qr, _, _ = _rms(m["qp"][:, D + hd * LANES:D + (hd + 1) * LANES], gqr_ref[...], DR)
            q_ref[hd, :, 0:DN] = (qn * SM_SCALE).astype(BF16)
            q_ref[hd, :, DN:2 * DN] = (_rope(qr, c, s) * SM_SCALE).astype(BF16)
            kn, _, _ = _rms(m["kvp"][:, hd * 2 * DN:hd * 2 * DN + DN], gkn_ref[...], DN)
            k_ref[hd, :, 0:DN] = kn.astype(BF16)
            k_ref[hd, :, DN:2 * DN] = krb
            v_ref[hd] = m["kvp"][:, hd * 2 * DN + DN:(hd + 1) * 2 * DN].astype(BF16)

    hb = lambda w: pl.BlockSpec((HEADS, TM, w), lambda i: (0, i, 0))
    return pl.pallas_call(
        body, name="mla_pre_fwd", grid=(t // TM,),
        in_specs=[_row(TM, D), _const((1, D)), _const((D, LATP)), _const((1, QL)), _const((1, KVL)),
                  _const((QL, 2 * D)), _const((KVL, 2 * D)), _const((1, LANES)), _const((1, LANES)),
                  _const((1, LANES)), _const((1, LANES)), _row(TM, LANES), _row(TM, LANES)],
        out_specs=[hb(2 * DN), hb(2 * DN), hb(DN)],
        out_shape=[_sds((HEADS, t, 2 * DN), BF16), _sds((HEADS, t, 2 * DN), BF16), _sds((HEADS, t, DN), BF16)],
        compiler_params=_cp("parallel"),
    )(h, g1, wdn, gq, gkv, wuq, wukv, gqn, gqr, gkn, gkr, cos, sin)


def _diagonal_mask(n=TQ):
    return lax.broadcasted_iota(jnp.int32, (n, n), 1) <= lax.broadcasted_iota(jnp.int32, (n, n), 0)


def flash_fwd(q, k, v, seq):
    t = q.shape[1]
    nb = t // seq
    tq = TQ_FWD
    nq = seq // tq
    hp = FWD_HEADS

    def body(q_ref, k_ref, v_ref, o_ref, lse_ref):
        qi = pl.program_id(2)
        qs = [q_ref[a] for a in range(hp)]

        def step(j, carry, diagonal=False):
            rows = pl.ds(pl.multiple_of(j * tq, tq), tq)
            out = []
            for a in range(hp):
                m, l, acc = carry[a]
                s = _dot_nt(qs[a], k_ref[a, rows, :])
                if diagonal:
                    s = jnp.where(_diagonal_mask(tq), s, -1e30)
                m_new = jnp.maximum(m, jnp.max(s, axis=-1, keepdims=True))
                p = jnp.exp(s - m_new)
                alpha = jnp.exp(m - m_new)
                l = alpha * l + jnp.sum(p, axis=-1, keepdims=True)
                acc = alpha * acc + _dot(p.astype(BF16), v_ref[a, rows, :])
                out.append((m_new, l, acc))
            return tuple(out)

        one = (jnp.full((tq, 1), -1e30, F32), jnp.zeros((tq, 1), F32), jnp.zeros((tq, DN), F32))
        done = step(qi, lax.fori_loop(0, qi, step, (one,) * hp), diagonal=True)
        for a, (m, l, acc) in enumerate(done):
            o_ref[:, a * DN:(a + 1) * DN] = (acc / l).astype(BF16)
            lse_ref[a] = m + jnp.log(l)

    return pl.pallas_call(
        body, name="flash_fwd", grid=(nb, HEADS // hp, nq),
        in_specs=[pl.BlockSpec((hp, tq, 2 * DN), lambda b, h, i: (h, b * nq + i, 0)),
                  pl.BlockSpec((hp, seq, 2 * DN), lambda b, h, i: (h, b, 0)),
                  pl.BlockSpec((hp, seq, DN), lambda b, h, i: (h, b, 0))],
        out_specs=[pl.BlockSpec((tq, hp * DN), lambda b, h, i: (b * nq + i, h)),
                   pl.BlockSpec((hp, tq, 1), lambda b, h, i: (h, b * nq + i, 0))],
        out_shape=[_sds((t, HEADS * DN), BF16), _sds((HEADS, t, 1), F32)],
        compiler_params=_cp("parallel", "parallel", "arbitrary"),
    )(q, k, v)


def _gmlp_in(hn, win_ref):
    pre = [_dot(hn, win_ref[c]) for c in range(N_CHIPS)]
    return jnp.concatenate(pre[:2], axis=1), jnp.concatenate(pre[2:], axis=1)


def gmlp_fwd(h, g1, allw, lay, lng, lnb, wm, bfull):
    t = h.shape[0]

    def body(h_ref, g1_ref, win_ref, lng_ref, lnb_ref, wm_ref, b_ref, y_ref, pre_ref):
        yn, _, _ = _rms(h_ref[...], g1_ref[...], D)
        pre_u, pre_v = _gmlp_in(yn.astype(BF16), win_ref)
        pre_ref[:, :GH] = pre_u.astype(BF16)
        pre_ref[:, GH:] = pre_v.astype(BF16)
        u = _gelu(pre_u)
        v = _gelu(pre_v)
        xc = v - jnp.mean(v, axis=-1, keepdims=True)
        rs = lax.rsqrt(jnp.mean(xc * xc, axis=-1, keepdims=True) + EPS)
        vnb = (xc * rs * lng_ref[...] + lnb_ref[...]).astype(BF16)
        for ch in range(TM // GC):
            rows = slice(ch * GC, (ch + 1) * GC)
            for g in range(GG):
                cols = slice(g * GD, (g + 1) * GD)
                sv = _dot(wm_ref[g], vnb[rows, cols]) + b_ref[:, cols]
                y_ref[rows, cols] = (u[rows, cols] * sv).astype(BF16)

    return pl.pallas_call(
        body, name="gmlp_fwd", grid=(t // TM,),
        in_specs=[_row(TM, D), _const((1, D)), _wblk(D, lay["in"]), _const((1, GH)), _const((1, GH)),
                  _const((GG, GC, GC)), _const((GC, GH))],
        out_specs=[_row(TM, GH), _row(TM, 2 * GH)],
        out_shape=[_sds((t, GH), BF16), _sds((t, 2 * GH), BF16)],
        compiler_params=_cp("parallel"),
    )(h, g1, allw, lng, lnb, wm, bfull)


def loss_head(h, tgt):
    t = h.shape[0]

    def body(h_ref, t_ref, dh_ref, loss_ref):
        @pl.when(pl.program_id(0) == 0)
        def _():
            loss_ref[...] = jnp.zeros_like(loss_ref)

        e = h_ref[...] - t_ref[...]
        dh_ref[...] = e * (1.0 / D)
        part = jnp.sum(jnp.sum(e * e, axis=-1, keepdims=True), axis=0, keepdims=True) * (0.5 / D)
        loss_ref[...] += jnp.broadcast_to(part, loss_ref.shape)

    return pl.pallas_call(
        body, name="loss_head", grid=(t // TMB,),
        in_specs=[_row(TMB, D), _row(TMB, D)],
        out_specs=[_row(TMB, D), _const((8, LANES))],
        out_shape=[_sds((t, D), F32), _sds((8, LANES), F32)],
        compiler_params=_cp("arbitrary"),
    )(h, tgt)


def _zero_at_first_step(*refs):
    @pl.when(pl.program_id(0) == 0)
    def _():
        for r in refs:
            r[...] = jnp.zeros_like(r)


def ple_bwd(dh3, h2, p, g3, allw, lay, wp):
    t = h2.shape[0]

    def body(dh_ref, h_ref, p_ref, g_ref, wg_ref, wp_ref, dh2_ref, dh2b_ref, dgt_ref, dpp_ref, dg_ref):
        _zero_at_first_step(dg_ref)
        dh3v = dh_ref[...]
        x = h_ref[...]
        g = g_ref[...]
        wg = _rows_joined(wg_ref)
        yn, xhat, r = _rms(x, g, D)
        gt = _dot(yn.astype(BF16), wg)
        pp = _dot(p_ref[...].astype(BF16), wp_ref[...])
        sg = _sigmoid(gt)
        dgt = (dh3v * pp * sg * (1.0 - sg)).astype(BF16)
        dgt_ref[...] = dgt
        dpp_ref[...] = (dh3v * sg).astype(BF16)
        dhn = _dot_nt(dgt, wg)
        _acc_rows(dg_ref, dhn * xhat)
        dh2 = dh3v + _rms_bwd(dhn, g, xhat, r, D)
        dh2_ref[...] = dh2
        dh2b_ref[...] = dh2.astype(BF16)

    return pl.pallas_call(
        body, name="ple_bwd", grid=(t // TMB,),
        in_specs=[_row(TMB, D), _row(TMB, D), _row(TMB, PLE), _const((1, D)), _wblk(D // N_CHIPS, lay["gate"]),
                  _const((PLE, D))],
        out_specs=[_row(TMB, D), _row(TMB, D), _row(TMB, D), _row(TMB, D), _const((8, D))],
        out_shape=[_sds((t, D), F32), _sds((t, D), BF16), _sds((t, D), BF16), _sds((t, D), BF16), _sds((8, D), F32)],
        compiler_params=_cp("arbitrary"),
    )(dh3, h2, p, g3, allw, wp)


def ffn_bwd(dh2, dh2b, h1, r, g2, allw, lay):
    t = h1.shape[0]

    def body(dh_ref, dhb_ref, h_ref, r_ref, g_ref, wu_ref, wd_ref, dh1_ref, dh1b_ref, du_ref, a_ref, dg_ref):
        _zero_at_first_step(dg_ref)
        dhb = dhb_ref[...]
        g = g_ref[...]
        _, xhat, rr = _rms(h_ref[...], g, D)
        dhn = jnp.zeros((TM, D), F32)
        for c in range(N_CHIPS):
            cs = slice(c * D, (c + 1) * D)
            rc = r_ref[:, cs].astype(F32)
            a_ref[:, cs] = (rc * rc).astype(BF16)
            da = _dot_nt(dhb, wd_ref[c])
            du = (da * (2.0 * rc)).astype(BF16)
            du_ref[:, cs] = du
            dhn = dhn + _dot_nt(du, wu_ref[c])
        _acc_rows(dg_ref, dhn * xhat)
        dh1 = dh_ref[...] + _rms_bwd(dhn, g, xhat, rr, D)
        dh1_ref[...] = dh1
        dh1b_ref[...] = dh1.astype(BF16)

    return pl.pallas_call(
        body, name="ffn_bwd", grid=(t // TM,),
        in_specs=[_row(TM, D), _row(TM, D), _row(TM, D), _row(TM, DFF), _const((1, D)), _wblk(D, lay["up"]),
                  _wblk(D, lay["down"])],
        out_specs=[_row(TM, D), _row(TM, D), _row(TM, DFF), _row(TM, DFF), _const((8, D))],
        out_shape=[_sds((t, D), F32), _sds((t, D), BF16), _sds((t, DFF), BF16), _sds((t, DFF), BF16),
                   _sds((8, D), F32)],
        compiler_params=_cp("arbitrary"),
    )(dh2, dh2b, h1, r, g2, allw, allw)


def linear_nt(a, allw, rows, row0):
    t = a.shape[0]
    k = N_CHIPS * rows

    def body(a_ref, w_ref, o_ref):
        o_ref[...] = _dot_nt(a_ref[...], _rows_joined(w_ref)).astype(BF16)

    return pl.pallas_call(
        body, name="linear_nt", grid=(t // TMB,),
        in_specs=[_row(TMB, D), _wblk(rows, row0)],
        out_specs=_row(TMB, k),
        out_shape=_sds((t, k), BF16),
        compiler_params=_cp("parallel"),
    )(a, allw)


def flash_bwd(q, k, v, o, do, lse, seq, after):
    t = q.shape[1]
    nb = t // seq
    nq = seq // TQ
    hp = BWD_HEADS

    def body(q_ref, k_ref, v_ref, o_ref, do_ref, lse_ref, after_ref, dq_ref, dk_ref, dv_ref):
        del after_ref
        kj = pl.program_id(2)

        @pl.when(kj == 0)
        def _():
            dq_ref[...] = jnp.zeros_like(dq_ref)

        def step(i, carry, diagonal=False):
            rows = pl.ds(pl.multiple_of(i * TQ, TQ), TQ)
            out = []
            for a in range(hp):
                dk, dv = carry[a]
                kv = k_ref[a]
                qv = q_ref[a, rows, :]
                dov = do_ref[rows, a * DN:(a + 1) * DN]
                ov = o_ref[rows, a * DN:(a + 1) * DN]
                delta = jnp.sum(dov.astype(F32) * ov.astype(F32), axis=-1, keepdims=True)
                s = _dot_nt(qv, kv)
                if diagonal:
                    s = jnp.where(_diagonal_mask(), s, -1e30)
                p = jnp.exp(s - lse_ref[a, rows, :])
                dp = _dot_nt(dov, v_ref[a])
                ds = (p * (dp - delta)).astype(BF16)
                dv = dv + _dot_tn(p.astype(BF16), dov)
                dk = dk + _dot_tn(ds, qv)
                dq_ref[a, rows, :] += _dot(ds, kv)
                out.append((dk, dv))
            return tuple(out)

        one = (jnp.zeros((TQ, 2 * DN), F32), jnp.zeros((TQ, DN), F32))
        done = lax.fori_loop(kj + 1, nq, step, step(kj, (one,) * hp, diagonal=True))
        for a, (dk, dv) in enumerate(done):
            dk_ref[a] = dk
            dv_ref[a] = dv

    return pl.pallas_call(
        body, name="flash_bwd", grid=(nb, HEADS // hp, nq),
        in_specs=[pl.BlockSpec((hp, seq, 2 * DN), lambda b, h, j: (h, b, 0)),
                  pl.BlockSpec((hp, TQ, 2 * DN), lambda b, h, j: (h, b * nq + j, 0)),
                  pl.BlockSpec((hp, TQ, DN), lambda b, h, j: (h, b * nq + j, 0)),
                  pl.BlockSpec((seq, hp * DN), lambda b, h, j: (b, h)),
                  pl.BlockSpec((seq, hp * DN), lambda b, h, j: (b, h)),
                  pl.BlockSpec((hp, seq, 1), lambda b, h, j: (h, b, 0)), _ANY],
        out_specs=[pl.BlockSpec((hp, seq, 2 * DN), lambda b, h, j: (h, b, 0)),
                   pl.BlockSpec((hp, TQ, 2 * DN), lambda b, h, j: (h, b * nq + j, 0)),
                   pl.BlockSpec((hp, TQ, DN), lambda b, h, j: (h, b * nq + j, 0))],
        out_shape=[_sds((HEADS, t, 2 * DN), F32), _sds((HEADS, t, 2 * DN), F32), _sds((HEADS, t, DN), F32)],
        compiler_params=_cp("parallel", "parallel", "arbitrary"),
    )(q, k, v, o, do, lse, after)


def mla_pre_bwd(dq, dk, dv, dh1, h, g1, wdn, gq, gkv, wuq, wukv, gqn, gqr, gkn, gkr, cos, sin):
    t = h.shape[0]

    def body(dq_ref, dk_ref, dv_ref, dh1_ref, h_ref, g1_ref, wdn_ref, gq_ref, gkv_ref, wuq_ref, wukv_ref,
             gqn_ref, gqr_ref, gkn_ref, gkr_ref, c_ref, s_ref,
             dh_ref, hn_ref, cq_ref, ckv_ref, dqp_ref, dkvp_ref, dlat_ref,
             dg1_ref, dgq_ref, dgkv_ref, dgqn_ref, dgqr_ref, dgkn_ref, dgkr_ref):
        _zero_at_first_step(dg1_ref, dgq_ref, dgkv_ref, dgqn_ref, dgqr_ref, dgkn_ref, dgkr_ref)
        m = _mla_project(h_ref, g1_ref, wdn_ref, gq_ref, gkv_ref, wuq_ref, wukv_ref)
        hn_ref[...] = m["hn"]
        cq_ref[...] = m["cqb"]
        ckv_ref[...] = m["ckvb"]
        c = c_ref[...]
        s = s_ref[...]
        gqn = gqn_ref[...]
        gqr = gqr_ref[...]
        gkn = gkn_ref[...]
        gkr = gkr_ref[...]

        dkr = dk_ref[0, :, DN:2 * DN]
        for hd in range(1, HEADS):
            dkr = dkr + dk_ref[hd, :, DN:2 * DN]
        dkr = _rope_t(dkr, c, s)
        _, krhat, rkr = _rms(m["kr_raw"], gkr, DR)
        _acc_rows(dgkr_ref, dkr * krhat)
        dkr_raw = _rms_bwd(dkr, gkr, krhat, rkr, DR)

        for hd in range(HEADS):
            ncols = slice(hd * DN, (hd + 1) * DN)
            _, xh, r = _rms(m["qp"][:, ncols], gqn, DN)
            dqn = dq_ref[hd, :, 0:DN] * SM_SCALE
            _acc_rows(dgqn_ref, dqn * xh)
            dqp_ref[:, ncols] = _rms_bwd(dqn, gqn, xh, r, DN).astype(BF16)

            rcols = slice(D + hd * LANES, D + (hd + 1) * LANES)
            _, xh, r = _rms(m["qp"][:, rcols], gqr, DR)
            dqr = _rope_t(dq_ref[hd, :, DN:2 * DN] * SM_SCALE, c, s)
            _acc_rows(dgqr_ref, dqr * xh)
            dqp_ref[:, rcols] = _rms_bwd(dqr, gqr, xh, r, DR).astype(BF16)

            kcols = slice(hd * 2 * DN, hd * 2 * DN + DN)
            _, xh, r = _rms(m["kvp"][:, kcols], gkn, DN)
            dkn = dk_ref[hd, :, 0:DN]
            _acc_rows(dgkn_ref, dkn * xh)
            dkvp_ref[:, kcols] = _rms_bwd(dkn, gkn, xh, r, DN).astype(BF16)
            dkvp_ref[:, hd * 2 * DN + DN:(hd + 1) * 2 * DN] = dv_ref[hd].astype(BF16)

        dcq = _dot_nt(dqp_ref[...], wuq_ref[...])
        _acc_rows(dgq_ref, dcq * m["cqhat"])
        dlat_q = _rms_bwd(dcq, gq_ref[...], m["cqhat"], m["rq"], QL)
        dckv = _dot_nt(dkvp_ref[...], wukv_ref[...])
        _acc_rows(dgkv_ref, dckv * m["ckvhat"])
        dlat_kv = _rms_bwd(dckv, gkv_ref[...], m["ckvhat"], m["rkv"], KVL)
        dlat = jnp.concatenate([dlat_q, dlat_kv, dkr_raw], axis=1).astype(BF16)
        dlat_ref[...] = dlat
        dhn = _dot_nt(dlat, wdn_ref[...])
        _acc_rows(dg1_ref, dhn * m["xhat"])
        dh_ref[...] = dh1_ref[...] + _rms_bwd(dhn, g1_ref[...], m["xhat"], m["rx"], D)

    hb = lambda w: pl.BlockSpec((HEADS, TM, w), lambda i: (0, i, 0))
    return pl.pallas_call(
        body, name="mla_pre_bwd", grid=(t // TM,),
        in_specs=[hb(2 * DN), hb(2 * DN), hb(DN), _row(TM, D), _row(TM, D), _const((1, D)), _const((D, LATP)),
                  _const((1, QL)), _const((1, KVL)), _const((QL, 2 * D)), _const((KVL, 2 * D)),
                  _const((1, LANES)), _const((1, LANES)), _const((1, LANES)), _const((1, LANES)),
                  _row(TM, LANES), _row(TM, LANES)],
        out_specs=[_row(TM, D), _row(TM, D), _row(TM, QL), _row(TM, KVL), _row(TM, 2 * D), _row(TM, 2 * D),
                   _row(TM, LATP), _const((8, D)), _const((8, QL)), _const((8, KVL)), _const((8, LANES)),
                   _const((8, LANES)), _const((8, LANES)), _const((8, LANES))],
        out_shape=[_sds((t, D), F32), _sds((t, D), BF16), _sds((t, QL), BF16), _sds((t, KVL), BF16),
                   _sds((t, 2 * D), BF16), _sds((t, 2 * D), BF16), _sds((t, LATP), BF16),
                   _sds((8, D), F32), _sds((8, QL), F32), _sds((8, KVL), F32), _sds((8, LANES), F32),
                   _sds((8, LANES), F32), _sds((8, LANES), F32), _sds((8, LANES), F32)],
        compiler_params=_cp("arbitrary"),
    )(dq, dk, dv, dh1, h, g1, wdn, gq, gkv, wuq, wukv, gqn, gqr, gkn, gkr, cos, sin)


def gmlp_bwd(dh1, dh1b, h, pre, g1, allw, lay, lng, lnb, wm, wmt, bfull, tril):
    t = h.shape[0]

    def body(dh1_ref, dh1b_ref, h_ref, pre_ref, g1_ref, win_ref, lng_ref, lnb_ref, wm_ref, wmt_ref, b_ref,
             wout_ref, tril_ref, dh_ref, hn_ref, dpre_ref, dws_ref, dbs_ref, dlng_ref, dlnb_ref, dg1_ref,
             dvn_s):
        _zero_at_first_step(dws_ref, dbs_ref, dlng_ref, dlnb_ref, dg1_ref)
        g1 = g1_ref[...]
        yn, xhat, rx = _rms(h_ref[...], g1, D)
        hn_ref[...] = yn.astype(BF16)
        dy = _dot_nt(dh1b_ref[...], _rows_joined(wout_ref))
        pre_u = pre_ref[:, :GH].astype(F32)
        pre_v = pre_ref[:, GH:].astype(F32)
        u, gg_u = _gelu_and_grad(pre_u)
        v, gg_v = _gelu_and_grad(pre_v)
        xc = v - jnp.mean(v, axis=-1, keepdims=True)
        rs = lax.rsqrt(jnp.mean(xc * xc, axis=-1, keepdims=True) + EPS)
        vhat = xc * rs
        lng = lng_ref[...]
        vnb = (vhat * lng + lnb_ref[...]).astype(BF16)
        dsv = dy * u
        dsvb = dsv.astype(BF16)
        tril_m = tril_ref[...]
        for ch in range(TM // GC):
            rows = slice(ch * GC, (ch + 1) * GC)
            dbs_ref[...] += dsv[rows, :]
            for g in range(GG):
                cols = slice(g * GD, (g + 1) * GD)
                sv = _dot(wm_ref[g], vnb[rows, cols]) + b_ref[:, cols]
                dpre_ref[rows, cols] = (dy[rows, cols] * sv * gg_u[rows, cols]).astype(BF16)
                dvn_s[rows, cols] = _dot(wmt_ref[g], dsvb[rows, cols])
                dws_ref[g] += _dot_nt(dsvb[rows, cols], vnb[rows, cols]) * tril_m
        dvn = dvn_s[...]
        _acc_rows(dlng_ref, dvn * vhat)
        _acc_rows(dlnb_ref, dvn)
        dvhat = dvn * lng
        dv = rs * (dvhat - jnp.mean(dvhat, axis=-1, keepdims=True)
                   - vhat * jnp.mean(dvhat * vhat, axis=-1, keepdims=True))
        dpre_v = (dv * gg_v).astype(BF16)
        dpre_ref[:, GH:] = dpre_v
        dhn = _dot_nt(dpre_ref[:, 0:D], win_ref[0])
        for c in range(1, N_CHIPS):
            dhn = dhn + _dot_nt(dpre_ref[:, c * D:(c + 1) * D], win_ref[c])
        _acc_rows(dg1_ref, dhn * xhat)
        dh_ref[...] = dh1_ref[...] + _rms_bwd(dhn, g1, xhat, rx, D)

    return pl.pallas_call(
        body, name="gmlp_bwd", grid=(t // TM,),
        in_specs=[_row(TM, D), _row(TM, D), _row(TM, D), _row(TM, 2 * GH), _const((1, D)), _wblk(D, lay["in"]),
                  _const((1, GH)), _const((1, GH)), _const((GG, GC, GC)), _const((GG, GC, GC)), _const((GC, GH)),
                  _wblk(GH // N_CHIPS, lay["out"]), _const((GC, GC))],
        out_specs=[_row(TM, D), _row(TM, D), _row(TM, 2 * GH), _const((GG, GC, GC)), _const((GC, GH)),
                   _const((8, GH)), _const((8, GH)), _const((8, D))],
        out_shape=[_sds((t, D), F32), _sds((t, D), BF16), _sds((t, 2 * GH), BF16), _sds((GG, GC, GC), F32),
                   _sds((GC, GH), F32), _sds((8, GH), F32), _sds((8, GH), F32), _sds((8, D), F32)],
        scratch_shapes=[pltpu.VMEM((TM, GH), F32)],
        compiler_params=_cp("arbitrary"),
    )(dh1, dh1b, h, pre, g1, allw, lng, lnb, wm, wmt, bfull, allw, tril)


def _token_step(t):
    return 1024 if t % 1024 == 0 else 512


def mm_tn(a, b):
    t, k = a.shape
    n = b.shape[1]
    tk = min(k, 1024)
    tn = min(n, 1024)
    tt = _token_step(t)

    def body(a_ref, b_ref, o_ref):
        @pl.when(pl.program_id(2) == 0)
        def _():
            o_ref[...] = jnp.zeros_like(o_ref)

        o_ref[...] += _dot_tn(a_ref[...].astype(BF16), b_ref[...].astype(BF16))

    return pl.pallas_call(
        body, name="mm_tn", grid=(k // tk, n // tn, t // tt),
        in_specs=[pl.BlockSpec((tt, tk), lambda i, j, s: (s, i)), pl.BlockSpec((tt, tn), lambda i, j, s: (s, j))],
        out_specs=pl.BlockSpec((tk, tn), lambda i, j, s: (i, j)), out_shape=_sds((k, n), F32),
        compiler_params=_cp("parallel", "parallel", "arbitrary"),
    )(a, b)


def mm_tn_into(buf, a, b, rows, row0, col_sharded):
    t = a.shape[0]
    tt = _token_step(t)
    assert row0 % rows == 0 and a.shape[1] == (rows if col_sharded else N_CHIPS * rows), (rows, row0, a.shape)
    assert b.shape[1] == (N_CHIPS * D if col_sharded else D), b.shape
    grid = (1, N_CHIPS, t // tt) if col_sharded else (N_CHIPS, 1, t // tt)
    fresh = isinstance(buf, int)

    def body(*refs):
        a_ref, b_ref, o_ref = refs[-3:]

        @pl.when(pl.program_id(2) == 0)
        def _():
            o_ref[...] = jnp.zeros_like(o_ref)

        o_ref[...] += _dot_tn(a_ref[...].astype(BF16), b_ref[...].astype(BF16))

    specs = [pl.BlockSpec((tt, rows), lambda i, j, s: (s, i)), pl.BlockSpec((tt, D), lambda i, j, s: (s, j))]
    return pl.pallas_call(
        body, name="mm_tn_into", grid=grid,
        in_specs=specs if fresh else [_ANY] + specs,
        out_specs=pl.BlockSpec((None, rows, D), lambda i, j, s: (i + j, row0 // rows, 0)),
        out_shape=_sds((N_CHIPS, buf, D) if fresh else buf.shape, F32),
        input_output_aliases={} if fresh else {0: 0},
        compiler_params=_cp("parallel", "parallel", "arbitrary"),
    )(*((a, b) if fresh else (buf, a, b)))


def adamw(w, g, m, v):
    rows, cols = w.shape
    tr = rows if rows <= 512 else next(r for r in (512, 384, 256, 128) if rows % r == 0)
    c1 = 1.0 - ADAM_B1 ** ADAM_STEP
    c2 = 1.0 - ADAM_B2 ** ADAM_STEP

    def body(w_ref, g_ref, m_ref, v_ref, d_ref, mo_ref, vo_ref):
        gv = g_ref[...]
        mn = ADAM_B1 * m_ref[...] + (1.0 - ADAM_B1) * gv
        vn = ADAM_B2 * v_ref[...] + (1.0 - ADAM_B2) * (gv * gv)
        mo_ref[...] = mn
        vo_ref[...] = vn
        d_ref[...] = -ADAM_LR * ((mn / c1) / (jnp.sqrt(vn / c2) + ADAM_EPS) + ADAM_WD * w_ref[...])

    spec = pl.BlockSpec((tr, cols), lambda i: (i, 0))
    return pl.pallas_call(
        body, name="adamw", grid=(rows // tr,),
        in_specs=[spec] * 4, out_specs=[spec] * 3, out_shape=[_sds((rows, cols), F32)] * 3,
        compiler_params=_cp("parallel"),
    )(w, g, m, v)


def _place():
    return lax.axis_index("x"), lax.axis_index("y"), lax.axis_index("c")


def _other_chips(x, y):
    return [(1 - x, y), (x, 1 - y), (1 - x, 1 - y)]


_ANY = pl.BlockSpec(memory_space=pl.ANY)


_HBM = pl.BlockSpec(memory_space=pltpu.HBM)
_SEM = pl.BlockSpec(memory_space=pltpu.SEMAPHORE)
_EFFECT = pltpu.SideEffectType.DATAFLOW_SIDE_EFFECTING
N_ICI = 3


def _exchange_start(name, src, land, copies, n):
    def body(src_ref, land_ref, *outs):
        sems, token = outs[:2 * n], outs[-1]
        for j, (s, d, to) in enumerate(copies(src_ref, land_ref, _place())):
            pltpu.make_async_remote_copy(src_ref=s, dst_ref=d, send_sem=sems[j], recv_sem=sems[n + j],
                                         device_id=to, device_id_type=MESH).start()
        token[...] = jnp.zeros_like(token)

    sem = pltpu.SemaphoreType.DMA(())
    outs = pl.pallas_call(
        body, name=name,
        out_shape=(sem,) * (2 * n) + (pltpu.HBM(src.shape, src.dtype), pltpu.HBM(land.shape, land.dtype),
                                      _sds((8, LANES), F32)),
        in_specs=(_HBM, _HBM),
        out_specs=(_SEM,) * (2 * n) + (_HBM, _HBM, pl.BlockSpec(memory_space=pltpu.VMEM)),
        input_output_aliases={0: 2 * n, 1: 2 * n + 1},
        compiler_params=pltpu.CompilerParams(has_side_effects=_EFFECT),
    )(pltpu.with_memory_space_constraint(src, pltpu.HBM), pltpu.with_memory_space_constraint(land, pltpu.HBM))
    return outs[:2 * n], outs[2 * n], outs[2 * n + 1], outs[-1]


def _exchange_wait(name, sems, src, land, after, arrivals):
    n = len(sems) // 2

    def body(src_ref, land_ref, *rest):
        sems = rest[:2 * n]
        for j, (s, d) in enumerate(arrivals(src_ref, land_ref, _place())):
            cp = pltpu.make_async_remote_copy(src_ref=s, dst_ref=d, send_sem=sems[j], recv_sem=sems[n + j],
                                              device_id=_place(), device_id_type=MESH)
            cp.wait_send()
            cp.wait_recv()

    return pl.pallas_call(
        body, name=name, out_shape=(pltpu.HBM(src.shape, src.dtype), pltpu.HBM(land.shape, land.dtype)),
        in_specs=(_HBM, _HBM) + (_SEM,) * (2 * n) + (_ANY,), out_specs=(_HBM, _HBM),
        input_output_aliases={0: 0, 1: 1},
        compiler_params=pltpu.CompilerParams(has_side_effects=_EFFECT),
    )(src, land, *sems, after)


def _halves(c, hh):
    return pl.ds(pl.multiple_of(c * hh, 16), hh), pl.ds(pl.multiple_of((1 - c) * hh, 16), hh)


def gather_start(land, tag):
    _, rr, _ = land.shape
    assert rr % 32 == 0, rr

    def copies(_, land_ref, place):
        x, y, c = place
        mine = land_ref.at[2 * x + y, _halves(c, rr // 2)[0]]
        return [(mine, mine, (cx, cy, c)) for cx, cy in _other_chips(x, y)]

    return _exchange_start(f"gather_start_{tag}", jnp.zeros((8, LANES), F32), land, copies, N_ICI)


def gather_wait(sems, src, land, after, tag):
    def arrivals(_, land_ref, place):
        x, y, c = place
        half = _halves(c, land.shape[1] // 2)[0]
        return [(land_ref.at[2 * x + y, half], land_ref.at[2 * cx + cy, half]) for cx, cy in _other_chips(x, y)]

    return _exchange_wait(f"gather_wait_{tag}", sems, src, land, after, arrivals)


def pass_start(land, tag):
    def copies(_, land_ref, place):
        x, y, c = place
        half = _halves(c, land.shape[1] // 2)[0]
        return [(land_ref.at[2 * cx + cy, half], land_ref.at[2 * cx + cy, half], (x, y, 1 - c))
                for cx, cy in _other_chips(x, y)]

    return _exchange_start(f"pass_start_{tag}", jnp.zeros((8, LANES), F32), land, copies, N_ICI)


def pass_wait(sems, src, land, after, tag):
    def arrivals(_, land_ref, place):
        x, y, c = place
        mine, other = _halves(c, land.shape[1] // 2)
        return [(land_ref.at[2 * cx + cy, mine], land_ref.at[2 * cx + cy, other]) for cx, cy in _other_chips(x, y)]

    return _exchange_wait(f"pass_wait_{tag}", sems, src, land, after, arrivals)


def swap_start(g, tag):
    _, rr, cc = g.shape

    def copies(g_ref, got_ref, place):
        x, y, c = place
        other = _halves(c, rr // 2)[1]
        return [(g_ref.at[k, other], got_ref.at[k], (x, y, 1 - c)) for k in range(N_CHIPS)]

    return _exchange_start(f"swap_start_{tag}", g, lax.empty((N_CHIPS, rr // 2, cc), g.dtype), copies, N_CHIPS)


def swap_wait(sems, g, got, after, tag):
    def arrivals(g_ref, got_ref, place):
        other = _halves(place[2], g.shape[1] // 2)[1]
        return [(g_ref.at[k, other], got_ref.at[k]) for k in range(N_CHIPS)]

    return _exchange_wait(f"swap_wait_{tag}", sems, g, got, after, arrivals)


def chip_sum(place, g32, got):
    _, rr, cc = g32.shape
    hh = rr // 2
    tr = SUM_ROWS
    assert rr % 2 == 0 and hh % tr == 0, (rr, tr)
    nb = hh // tr

    def body(place_ref, g_ref, got_ref, own_ref, all_ref):
        s = g_ref[...] + got_ref[...].astype(F32)
        all_ref[...] = s.astype(BF16)
        own_ref[...] = g_ref[place_ref[1]] + got_ref[place_ref[1]].astype(F32)

    return pl.pallas_call(
        body, name="chip_sum",
        grid_spec=pltpu.PrefetchScalarGridSpec(
            num_scalar_prefetch=1, grid=(nb,),
            in_specs=[pl.BlockSpec((N_CHIPS, tr, cc), lambda i, pr: (0, pr[0] * nb + i, 0)),
                      pl.BlockSpec((N_CHIPS, tr, cc), lambda i, pr: (0, i, 0))],
            out_specs=[pl.BlockSpec((tr, cc), lambda i, pr: (i, 0)),
                       pl.BlockSpec((N_CHIPS, tr, cc), lambda i, pr: (0, i, 0))]),
        out_shape=[_sds((hh, cc), F32), _sds((N_CHIPS, hh, cc), BF16)],
        compiler_params=_cp("parallel"),
    )(place, g32, got)


def _scatter_copies(s_ref, land_ref, place):
    x, y, c = place
    return [(s_ref.at[2 * cx + cy], land_ref.at[j], (cx, cy, c)) for j, (cx, cy) in enumerate(_other_chips(x, y))]


def scatter_start(s, tag):
    return _exchange_start(f"scatter_start_{tag}", s, lax.empty((N_ICI,) + s.shape[1:], s.dtype), _scatter_copies, N_ICI)


def scatter_wait(sems, s, land, after, tag):
    return _exchange_wait(f"scatter_wait_{tag}", sems, s, land, after,
                          lambda s_ref, land_ref, place: [(a, b) for a, b, _ in _scatter_copies(s_ref, land_ref, place)])


def final_sum(place, own, got):
    hh, cc = own.shape
    tr = SUM_ROWS
    assert hh % tr == 0, (hh, tr)
    nb = hh // tr

    def body(place_ref, own_ref, got_ref, o_ref):
        del place_ref
        o_ref[...] = ((own_ref[...] + got_ref[0].astype(F32)) + got_ref[1].astype(F32)) + got_ref[2].astype(F32)

    return pl.pallas_call(
        body, name="final_sum",
        grid_spec=pltpu.PrefetchScalarGridSpec(
            num_scalar_prefetch=1, grid=(nb,),
            in_specs=[pl.BlockSpec((tr, cc), lambda i, pr: (i, 0)), pl.BlockSpec((3, tr, cc), lambda i, pr: (0, i, 0))],
            out_specs=pl.BlockSpec((tr, cc), lambda i, pr: (pr[0] * nb + i, 0))),
        out_shape=_sds((2 * hh, cc), F32),
        compiler_params=_cp("parallel"),
    )(place, own, got)


def share_start(f, tag):
    def copies(_, f_ref, place):
        x, y, c = place
        mine = f_ref.at[_halves(c, f.shape[0] // 2)[0]]
        return [(mine, mine, (x, y, 1 - c))]

    return _exchange_start(f"share_start_{tag}", jnp.zeros((8, LANES), F32), f, copies, 1)


def share_wait(sems, src, f, after, tag):
    def arrivals(_, f_ref, place):
        mine, other = _halves(place[2], f.shape[0] // 2)
        return [(f_ref.at[mine], f_ref.at[other])]

    return _exchange_wait(f"share_wait_{tag}", sems, src, f, after, arrivals)


N_DEV = 8


def _peers(place):
    x, y, c = place
    return [((1 - x) if r & 4 else x, (1 - y) if r & 2 else y, (1 - c) if r & 1 else c) for r in range(1, N_DEV)]


def _device_index(place):
    x, y, c = place
    return 4 * x + 2 * y + c


def small_start(land, tag):
    def copies(_, land_ref, place):
        mine = land_ref.at[_device_index(place)]
        return [(mine, mine, to) for to in _peers(place)]

    return _exchange_start(f"small_start_{tag}", jnp.zeros((8, LANES), F32), land, copies, N_DEV - 1)


def small_wait(sems, src, land, after, tag):
    def arrivals(_, land_ref, place):
        return [(land_ref.at[_device_index(place)], land_ref.at[_device_index(peer)]) for peer in _peers(place)]

    return _exchange_wait(f"small_wait_{tag}", sems, src, land, after, arrivals)


def sum_devices(land):
    _, rr, cc = land.shape
    tr = 56
    assert rr % tr == 0, rr

    def body(l_ref, o_ref):
        acc = l_ref[0]
        for d in range(1, N_DEV):
            acc = acc + l_ref[d]
        o_ref[...] = acc

    return pl.pallas_call(
        body, name="sum_devices", grid=(rr // tr,),
        in_specs=[pl.BlockSpec((N_DEV, tr, cc), lambda i: (0, i, 0))],
        out_specs=pl.BlockSpec((tr, cc), lambda i: (i, 0)), out_shape=_sds((rr, cc), F32),
        compiler_params=_cp("parallel"),
    )(land)


_BIG = ["mla_w_down", "mla_w_uq", "mla_w_ukv", "mla_w_out", "gmlp_w_in", "gmlp_w_out", "ffn_w_up", "ffn_w_down",
        "ple_w_gate", "ple_w_proj"]
_SMALL_REST = ["norm_mix", "norm_ffn", "norm_ple", "mla_q_lora_g", "mla_kv_lora_g", "mla_q_nope_g", "mla_q_rope_g",
               "mla_k_nope_g", "mla_k_rope_g"]
_SMALL_GMLP = ["gmlp_ln_g", "gmlp_ln_b", "gmlp_w_s", "gmlp_b_s"]
_SMALL = _SMALL_REST + _SMALL_GMLP

_LAY_MLA = dict(up=0, down=1024, out=2048, gate=2304, wdn=2560, wuq=2736, wukv=2880, proj=3008, rows=3072)
_LAY_MLA_MAIN = dict(up=0, down=1024, out=2048, gate=2304, rows=2560)
_LAY_MLA_ODD = dict(wdn=0, wuq=176, wukv=320, proj=448, rows=512)
_LAY_GMLP = {"up": 0, "down": 1024, "in": 2048, "out": 3072, "gate": 3584, "proj": 3840, "ln": 3904, "rows": 4096}
SPLIT_LAYERS = (0,)


def _layer_units(i):
    j = i // 2
    if i % 2 == 0:
        odd, lay = (_LAY_MLA_ODD, _LAY_MLA_MAIN) if i in SPLIT_LAYERS else (_LAY_MLA, _LAY_MLA)
        small = [("mla_w_down", j, odd["wdn"]), ("mla_w_uq", j, odd["wuq"]), ("mla_w_ukv", j, odd["wukv"]),
                 ("ple_w_proj", i, odd["proj"])]
        large = [("ffn_w_up", i, lay["up"]), ("ffn_w_down", i, lay["down"]), ("mla_w_out", j, lay["out"]),
                 ("ple_w_gate", i, lay["gate"])]
        return [("odd", odd, small), ("main", lay, large)] if i in SPLIT_LAYERS else [("main", lay, large + small)]
    lay = _LAY_GMLP
    return [("main", lay, [("ffn_w_up", i, lay["up"]), ("ffn_w_down", i, lay["down"]), ("gmlp_w_in", j, lay["in"]),
                           ("gmlp_w_out", j, lay["out"]), ("ple_w_gate", i, lay["gate"]),
                           ("ple_w_proj", i, lay["proj"])])]


def _pack_rows(parts, dtype, pad_to=None, slot=False):
    size = sum(p.size for p in parts)
    tail = [] if pad_to is None or pad_to * D == size else [jnp.zeros((pad_to * D - size,), dtype)]
    shape = (1, -1, D) if slot else (-1, D)
    if all(p.size % D == 0 for p in parts + tail):
        return jnp.concatenate([p.astype(dtype).reshape(shape) for p in parts + tail], axis=len(shape) - 2)
    return jnp.concatenate([p.astype(dtype).reshape(-1) for p in parts + tail]).reshape(shape)


def _odd(allw, row0, a, b):
    return allw[:, row0:row0 + a * b // D].reshape(N_CHIPS, a, b)


def _cols_joined(s):
    return jnp.transpose(s, (1, 0, 2)).reshape(s.shape[1], N_CHIPS * s.shape[2])


def _col_shards(full):
    a, bb = full.shape
    return jnp.transpose(full.reshape(a, N_CHIPS, bb // N_CHIPS), (1, 0, 2)).reshape(N_CHIPS, -1, D)


def _pad_lanes(g):
    return jnp.pad(g, ((0, 0), (0, LANES - g.shape[1])))


def _split_uq(wuq):
    l = wuq.shape[0]
    w = wuq.reshape(l, QL, HEADS, DN + DR)
    nope = w[..., :DN].reshape(l, QL, HEADS * DN)
    rope = jnp.pad(w[..., DN:], ((0, 0), (0, 0), (0, 0), (0, LANES - DR))).reshape(l, QL, HEADS * LANES)
    return jnp.concatenate([nope, rope], axis=-1)


def _merge_uq(d):
    nope = d[:, :HEADS * DN].reshape(QL, HEADS, DN)
    rope = d[:, HEADS * DN:].reshape(QL, HEADS, LANES)[..., :DR]
    return jnp.concatenate([nope, rope], axis=-1).reshape(QL, HEADS * (DN + DR))


def _rope_tables(positions):
    inv_freq = ROPE_BASE ** (-(jnp.arange(0, DR, 2, dtype=F32) / DR))
    ang = positions.reshape(-1).astype(F32)[:, None] * inv_freq
    z = jnp.zeros((ang.shape[0], LANES - DR), F32)
    return (jnp.concatenate([jnp.cos(ang), jnp.cos(ang), z], axis=1),
            jnp.concatenate([jnp.sin(ang), jnp.sin(ang), z], axis=1))


def kernel(x, p, positions, norm_mix, norm_ffn, norm_ple, mla_w_down, mla_q_lora_g, mla_kv_lora_g, mla_w_uq, mla_w_ukv, mla_q_nope_g, mla_q_rope_g, mla_k_nope_g, mla_k_rope_g, mla_w_out, gmlp_w_in, gmlp_ln_g, gmlp_ln_b, gmlp_w_s, gmlp_b_s, gmlp_w_out, ffn_w_up, ffn_w_down, ple_w_gate, ple_w_proj, loss_target, m_norm_mix, m_norm_ffn, m_norm_ple, m_mla_w_down, m_mla_q_lora_g, m_mla_kv_lora_g, m_mla_w_uq, m_mla_w_ukv, m_mla_q_nope_g, m_mla_q_rope_g, m_mla_k_nope_g, m_mla_k_rope_g, m_mla_w_out, m_gmlp_w_in, m_gmlp_ln_g, m_gmlp_ln_b, m_gmlp_w_s, m_gmlp_b_s, m_gmlp_w_out, m_ffn_w_up, m_ffn_w_down, m_ple_w_gate, m_ple_w_proj, v_norm_mix, v_norm_ffn, v_norm_ple, v_mla_w_down, v_mla_q_lora_g, v_mla_kv_lora_g, v_mla_w_uq, v_mla_w_ukv, v_mla_q_nope_g, v_mla_q_rope_g, v_mla_k_nope_g, v_mla_k_rope_g, v_mla_w_out, v_gmlp_w_in, v_gmlp_ln_g, v_gmlp_ln_b, v_gmlp_w_s, v_gmlp_b_s, v_gmlp_w_out, v_ffn_w_up, v_ffn_w_down, v_ple_w_gate, v_ple_w_proj):
    args = dict(locals())
    weights = {n: args[n] for n in _BIG + _SMALL}
    depth = norm_mix.shape[0]
    nb, seq, _ = x.shape
    t = nb * seq
    assert seq % TQ == 0 and seq % TM == 0 and t % 512 == 0, (nb, seq)
    cx = lax.axis_index("x")
    cy = lax.axis_index("y")
    cc = lax.axis_index("c")
    chip = 2 * cx + cy

    gathers = {}
    token = None
    for i in range(depth):
        for key, lay, parts in _layer_units(i):
            rows = [weights[n][l] for n, l, _ in parts]
            if token is not None:
                rows[0] = rows[0] + token[0, 0]
            if "ln" in lay:
                ln = jnp.stack([gmlp_ln_g[i // 2], gmlp_ln_b[i // 2]]).astype(F32)
                bits = lax.bitcast_convert_type(ln, BF16).reshape(-1)
                rows.append(jnp.pad(bits, (0, 16 * D - bits.size)).reshape(16, D))
            mine = _pack_rows(rows, BF16, pad_to=lay["rows"], slot=True)
            land = lax.dynamic_update_slice(lax.empty((N_CHIPS, lay["rows"], D), BF16), mine, (chip, 0, 0))
            sems, src, land, token = gather_start(land, f"{i}{key}")
            gathers[i, key] = (sems, src, land)
    allw = [None] * depth

    tril = jnp.tril(jnp.ones((GC, GC), F32))
    wm = (gmlp_w_s * tril).astype(BF16)
    wmt = jnp.swapaxes(wm, -1, -2)
    bfull = jnp.repeat(jnp.swapaxes(gmlp_b_s, -1, -2), GD, axis=-1)
    cos, sin = _rope_tables(positions)
    row = lambda g: g.reshape(1, -1)
    gqr = _pad_lanes(mla_q_rope_g)
    gkr = _pad_lanes(mla_k_rope_g)

    h = x.reshape(t, D)
    pt = p.reshape(depth, t, PLE)
    saved = []

    passing = {}

    def arrive(i, key, after):
        sems, src, land = gathers[i, key]
        _, land = gather_wait(sems, src, land, after, f"{i}{key}")
        passing[i, key] = pass_start(land, f"{i}{key}")
        return passing[i, key][3]

    def needed(i, key, after=None):
        sems, src, land, tok = passing.pop((i, key))
        return pass_wait(sems, src, land, tok if after is None else after, f"{i}{key}")[1]

    arrive(0, _layer_units(0)[0][0], token)
    for i in range(depth):
        j = i // 2
        lay = _layer_units(i)[-1][1]
        s = dict(h=h)
        if i % 2 == 0:
            split = i in SPLIT_LAYERS
            olay = _layer_units(i)[0][1]
            odd = needed(i, "odd" if split else "main", None if i == 0 else h)
            wdn = jnp.pad(_odd(odd, olay["wdn"], D // N_CHIPS, LAT).reshape(D, LAT), ((0, 0), (0, LATP - LAT)))
            wuq = _split_uq(_cols_joined(_odd(odd, olay["wuq"], QL, 384))[None])[0]
            wukv = _cols_joined(_odd(odd, olay["wukv"], KVL, 512))
            wp = _cols_joined(_odd(odd, olay["proj"], PLE, 256))
            mla_args = (row(norm_mix[i]), wdn, row(mla_q_lora_g[j]), row(mla_kv_lora_g[j]), wuq, wukv,
                        row(mla_q_nope_g[j]), gqr[j:j + 1], row(mla_k_nope_g[j]), gkr[j:j + 1], cos, sin)
            q, k, v = mla_pre_fwd(h, *mla_args)
            y, lse = flash_fwd(q, k, v, seq)
            if split and i == 0:
                arrive(i, "main", y)
            aw = needed(i, "main", y) if split else odd
            s.update(q=q, k=k, v=v, lse=lse, mla_args=mla_args)
        else:
            aw = needed(i, "main", h)
            ln = lax.bitcast_convert_type(aw[:, lay["ln"]:lay["ln"] + 2].reshape(N_CHIPS, 2, GH // N_CHIPS, 2), F32)
            ln = jnp.transpose(ln, (1, 0, 2)).reshape(2, 1, GH)
            wp = _cols_joined(_odd(aw, lay["proj"], PLE, 256))
            y, pre = gmlp_fwd(h, row(norm_mix[i]), aw, lay, ln[0], ln[1], wm[j], bfull[j])
            s.update(pre=pre, ln=ln)
        allw[i] = aw
        g2 = row(norm_ffn[i])
        if i + 1 < depth:
            for key, _, _ in _layer_units(i + 1):
                g2 = g2 + arrive(i + 1, key, y)[0:1, 0:1]
        h1, h2, hn2, r = mixffn_fwd(h, y, aw, lay, g2)
        h, hn3 = ple_fwd(h2, pt[i], row(norm_ple[i]), aw, lay, wp)
        s.update(y=y, wp=wp, h1=h1, h2=h2, hn2=hn2, r=r, hn3=hn3)
        saved.append(s)

    dh, loss_part = loss_head(h, loss_target.reshape(t, D))
    loss = lax.psum(loss_part[0, 0], ("x", "y", "c"))

    gs = {n: [None] * weights[n].shape[0] for n in _SMALL}
    gw = {n: [None] * weights[n].shape[0] for n in _BIG}
    place = jnp.stack([cc, chip]).astype(jnp.int32)
    scatters = []
    swaps = []
    token = None

    def put(b, row0, shards):
        return lax.dynamic_update_slice(b, shards.reshape(N_CHIPS, -1, D), (0, row0, 0))

    def small_size(n):
        return weights[n].shape[0] * GH if n in ("gmlp_ln_g", "gmlp_ln_b") else weights[n].size

    def small_exchange(names, zero, tag):
        rows = -(-sum(small_size(n) for n in names) // (56 * D)) * 56
        part = [jnp.stack(gs[n]) for n in names]
        part = _pack_rows([part[0] + zero] + part[1:], F32, pad_to=rows, slot=True)
        land = lax.dynamic_update_slice(lax.empty((N_DEV, rows, D), F32), part, (2 * chip + cc, 0, 0))
        return small_start(land, tag)

    def swap(i, key, buf):
        sems, buf, got, tok = swap_start(buf, f"{i}{key}")
        swaps.append((i, key, sems, buf, got))
        return tok

    def swapped(after, zero):
        while swaps:
            i, key, sems, g, got = swaps.pop(0)
            g, got = swap_wait(sems, g, got, after, f"{i}{key}")
            own, sums = chip_sum(place, g, got)
            sems, sums, land, tok = scatter_start(sums, f"{i}{key}")
            scatters.append((i, key, own, sems, sums, land))
            zero = zero + tok[0:1, 0:1]
        return zero

    for i in reversed(range(depth)):
        j = i // 2
        lay = _layer_units(i)[-1][1]
        aw = allw[i]
        s = saved[i]

        g3 = row(norm_ple[i])
        if token is not None:
            g3 = g3 + token[0:1, 0:1]
        dh2, dh2b, dgt, dpp, dg3 = ple_bwd(dh, s["h2"], pt[i], g3, aw, lay, s["wp"])
        gs["norm_ple"][i] = dg3[0]
        buf = mm_tn_into(lay["rows"], s["hn3"], dgt, D // N_CHIPS, lay["gate"], False)
        dproj = _col_shards(mm_tn(pt[i], dpp))
        if "ln" in lay:
            buf = put(buf, lay["ln"], jnp.zeros((N_CHIPS, lay["rows"] - lay["ln"], D), F32))
            buf = put(buf, lay["proj"], dproj)
        dh1, dh1b, du, a, dg2 = ffn_bwd(dh2, dh2b, s["h1"], s["r"], row(norm_ffn[i]), aw, lay)
        gs["norm_ffn"][i] = dg2[0]
        buf = mm_tn_into(buf, a, dh2b, D, lay["down"], False)
        buf = mm_tn_into(buf, s["hn2"], du, D, lay["up"], True)
        buf = mm_tn_into(buf, s["y"], dh1b, s["y"].shape[1] // N_CHIPS, lay["out"], False)
        g1 = swapped(dh1, row(norm_mix[i]))
        if i % 2 == 0:
            split = i in SPLIT_LAYERS
            do = linear_nt(dh1b, aw, D // N_CHIPS, lay["out"])
            dq, dk, dv = flash_bwd(s["q"], s["k"], s["v"], s["y"], do, s["lse"], seq,
                                   after=swap(i, "main", buf) if split else dh1b)
            g1 = swapped(dq, g1)
            (dh, hn1, cq, ckv, dqp, dkvp, dlat, dg1, dgq, dgkv, dgqn, dgqr, dgkn, dgkr) = mla_pre_bwd(
                dq, dk, dv, dh1, s["h"], g1, *s["mla_args"][1:])
            gs["norm_mix"][i] = dg1[0]
            gs["mla_q_lora_g"][j] = dgq[0]
            gs["mla_kv_lora_g"][j] = dgkv[0]
            gs["mla_q_nope_g"][j] = dgqn[0]
            gs["mla_q_rope_g"][j] = dgqr[0, :DR]
            gs["mla_k_nope_g"][j] = dgkn[0]
            gs["mla_k_rope_g"][j] = dgkr[0, :DR]
            small = [mm_tn(hn1, dlat)[:, :LAT].reshape(N_CHIPS, -1, D), _col_shards(_merge_uq(mm_tn(cq, dqp))),
                     _col_shards(mm_tn(ckv, dkvp)), dproj]
            if split:
                buf = jnp.concatenate(small, axis=1)
            else:
                buf = put(buf, lay["wdn"], jnp.concatenate(small, axis=1))
            key = "odd" if split else "main"
        else:
            dh, hn1, dpre, dws, dbs, dlng, dlnb, dg1 = gmlp_bwd(
                dh1, dh1b, s["h"], s["pre"], g1, aw, lay, s["ln"][0], s["ln"][1], wm[j], wmt[j], bfull[j], tril)
            gs["norm_mix"][i] = dg1[0]
            gs["gmlp_ln_g"][j] = dlng[0]
            gs["gmlp_ln_b"][j] = dlnb[0]
            gs["gmlp_w_s"][j] = dws
            gs["gmlp_b_s"][j] = jnp.sum(dbs.reshape(GC, GG, GD), axis=-1).T
            buf = mm_tn_into(buf, hn1, dpre, D, lay["in"], True)
            key = "main"
        token = swap(i, key, buf)
        if i == 1:
            small_gmlp = small_exchange(_SMALL_GMLP, token[0, 0], "gmlp")
            token = token + small_gmlp[3]
    last = swapped(dh, jnp.zeros((1, 1), F32))
    grad_x = dh.reshape(x.shape)
    small_rest = small_exchange(_SMALL_REST, last[0, 0], "rest")

    after = small_rest[3]
    shares = []
    for i, key, own, sems, sums, land in scatters:
        _, got = scatter_wait(sems, sums, land, after, f"{i}{key}")
        sems, src, full, after = share_start(final_sum(place, own, got), f"{i}{key}")
        shares.append((i, key, sems, src, full))
    for i, key, sems, src, full in shares:
        _, after = share_wait(sems, src, full, after, f"{i}{key}")
        for n, l, row0 in dict((k, parts) for k, _, parts in _layer_units(i))[key]:
            gw[n][l] = after[row0:row0 + weights[n][l].size // D].reshape(weights[n].shape[1:])
    grads = {n: jnp.stack(gw[n]) for n in _BIG}

    tot = []
    for names, (sems, src, land, _), tag in ((_SMALL_REST, small_rest, "rest"), (_SMALL_GMLP, small_gmlp, "gmlp")):
        summed = sum_devices(small_wait(sems, src, land, after, tag)[1]).reshape(-1)
        tot.append(summed[:sum(small_size(n) for n in names)])
    tot = jnp.concatenate(tot)
    off = 0
    for n, sz in ((n, small_size(n)) for n in _SMALL_REST + _SMALL_GMLP):
        gsum = tot[off:off + sz]
        off += sz
        if n in ("gmlp_ln_g", "gmlp_ln_b"):
            gsum = lax.dynamic_slice_in_dim(gsum.reshape(-1, GH), chip * (GH // N_CHIPS), GH // N_CHIPS, axis=1)
        grads[n] = gsum.reshape(weights[n].shape)

    delta, new_m, new_v = {}, {}, {}
    for n in _BIG:
        w2 = weights[n].reshape(-1, weights[n].shape[-1])
        d, mn, vn = adamw(w2, grads[n].reshape(w2.shape), args["m_" + n].reshape(w2.shape),
                          args["v_" + n].reshape(w2.shape))
        delta[n], new_m[n], new_v[n] = (a.reshape(weights[n].shape) for a in (d, mn, vn))
    own_sizes = [weights[n].size for n in _SMALL]
    own_rows = -(-sum(own_sizes) // (8 * D)) * 8
    packed = [_pack_rows([src[n] for n in _SMALL], F32, pad_to=own_rows)
              for src in (weights, grads, {n: args["m_" + n] for n in _SMALL}, {n: args["v_" + n] for n in _SMALL})]
    outs = adamw(*packed)
    off = 0
    for n, sz in zip(_SMALL, own_sizes):
        for dst, o in zip((delta, new_m, new_v), outs):
            dst[n] = o.reshape(-1)[off:off + sz].reshape(weights[n].shape)
        off += sz

    order = ["norm_mix", "norm_ffn", "norm_ple", "mla_w_down", "mla_q_lora_g", "mla_kv_lora_g", "mla_w_uq",
             "mla_w_ukv", "mla_q_nope_g", "mla_q_rope_g", "mla_k_nope_g", "mla_k_rope_g", "mla_w_out", "gmlp_w_in",
             "gmlp_ln_g", "gmlp_ln_b", "gmlp_w_s", "gmlp_b_s", "gmlp_w_out", "ffn_w_up", "ffn_w_down", "ple_w_gate",
             "ple_w_proj"]
    return (loss, grad_x, *[grads[n] for n in order], *[delta[n] for n in order], *[new_m[n] for n in order],
            *[new_v[n] for n in order])
```

```python
import functools

import jax
import jax.numpy as jnp
from jax import lax
from jax.experimental import pallas as pl
from jax.experimental.pallas import tpu as pltpu

F32 = jnp.float32
BF16 = jnp.bfloat16
MESH = pl.DeviceIdType.MESH

D = 1024
HEADS = 8
DN = 128
DR = 64
QL = 384
KVL = 256
LAT = 704
LATP = 768
DFF = 4096
GH = 2048
GC = 128
GG = 8
GD = 256
PLE = 256
EPS = 1e-6
ROPE_BASE = 10000.0
SM_SCALE = (DN + DR) ** -0.5
N_CHIPS = 4
LANES = 128

ADAM_LR = 0.001
ADAM_B1 = 0.9
ADAM_B2 = 0.999
ADAM_EPS = 1e-08
ADAM_WD = 0.01
ADAM_STEP = 10

TM = 256
TMB = 512
TQ = 512
TQ_FWD = 512
FWD_HEADS = 2
BWD_HEADS = 2
SUM_ROWS = 256
VMEM_LIMIT = 56 * 1024 * 1024


def _cp(*sem):
    return pltpu.CompilerParams(dimension_semantics=sem, vmem_limit_bytes=VMEM_LIMIT)


def _dot(a, b):
    return jnp.dot(a, b, preferred_element_type=F32)


def _dot_nt(a, b):
    return lax.dot_general(a, b, (((1,), (1,)), ((), ())), preferred_element_type=F32)


def _dot_tn(a, b):
    return lax.dot_general(a, b, (((0,), (0,)), ((), ())), preferred_element_type=F32)


def _rms(x, g, n):
    r = lax.rsqrt(jnp.sum(x * x, axis=-1, keepdims=True) * (1.0 / n) + EPS)
    xhat = x * r
    return xhat * g, xhat, r


def _rms_bwd(dy, g, xhat, r, n):
    dxhat = dy * g
    return r * (dxhat - xhat * (jnp.sum(dxhat * xhat, axis=-1, keepdims=True) * (1.0 / n)))


def _rope(x, c, s):
    return x * c + (pltpu.roll(x, 32, 1) - pltpu.roll(x, 96, 1)) * s


def _rope_t(dy, c, s):
    w = dy * s
    return dy * c + pltpu.roll(w, 96, 1) - pltpu.roll(w, 32, 1)


def _sigmoid(x):
    return 1.0 / (1.0 + jnp.exp(-x))


_GELU_K = 0.7978845608028654
_GELU_C = 0.044715


def _gelu(x):
    return 0.5 * x * (1.0 + jnp.tanh(_GELU_K * (x + _GELU_C * x * x * x)))


def _gelu_and_grad(x):
    x2 = x * x
    t = jnp.tanh(_GELU_K * (x + _GELU_C * x2 * x))
    half = 0.5 * (1.0 + t)
    return x * half, half + 0.5 * x * (1.0 - t * t) * (_GELU_K * (1.0 + 3.0 * _GELU_C * x2))


def _acc_rows(ref, val):
    ref[...] += jnp.broadcast_to(jnp.sum(val, axis=0, keepdims=True), ref.shape)


def _row(tm, c):
    return pl.BlockSpec((tm, c), lambda i: (i, 0))


def _const(shape):
    nd = len(shape)
    return pl.BlockSpec(shape, lambda i: (0,) * nd, pipeline_mode=pl.Buffered(1))


def _wblk(rows, row0):
    assert row0 % rows == 0, (rows, row0)
    return pl.BlockSpec((N_CHIPS, rows, D), lambda i: (0, row0 // rows, 0), pipeline_mode=pl.Buffered(1))


def _rows_joined(w_ref):
    return w_ref[...].reshape(N_CHIPS * w_ref.shape[1], D)


def _sds(shape, dtype):
    return jax.ShapeDtypeStruct(shape, dtype)


def mixffn_fwd(h, y, allw, lay, g2):
    t, k = y.shape

    def body(h_ref, y_ref, wo_ref, g_ref, wu_ref, wd_ref, h1_ref, h2_ref, hn_ref, r_ref):
        h1 = h_ref[...] + _dot(y_ref[...], _rows_joined(wo_ref))
        h1_ref[...] = h1
        yn, _, _ = _rms(h1, g_ref[...], D)
        hn = yn.astype(BF16)
        hn_ref[...] = hn
        f = jnp.zeros((TMB, D), F32)
        for c in range(N_CHIPS):
            r = jnp.maximum(_dot(hn, wu_ref[c]), 0.0)
            r_ref[:, c * D:(c + 1) * D] = r.astype(BF16)
            f = f + _dot((r * r).astype(BF16), wd_ref[c])
        h2_ref[...] = h1 + f

    return pl.pallas_call(
        body, name="mixffn_fwd", grid=(t // TMB,),
        in_specs=[_row(TMB, D), _row(TMB, k), _wblk(k // N_CHIPS, lay["out"]), _const((1, D)), _wblk(D, lay["up"]),
                  _wblk(D, lay["down"])],
        out_specs=[_row(TMB, D), _row(TMB, D), _row(TMB, D), _row(TMB, DFF)],
        out_shape=[_sds((t, D), F32), _sds((t, D), F32), _sds((t, D), BF16), _sds((t, DFF), BF16)],
        compiler_params=_cp("parallel"),
    )(h, y, allw, g2, allw, allw)


def ple_fwd(h2, p, g3, allw, lay, wp):
    t = h2.shape[0]

    def body(h_ref, p_ref, g_ref, wg_ref, wp_ref, h3_ref, hn_ref):
        x = h_ref[...]
        yn, _, _ = _rms(x, g_ref[...], D)
        hn = yn.astype(BF16)
        hn_ref[...] = hn
        gt = _dot(hn, _rows_joined(wg_ref))
        pp = _dot(p_ref[...].astype(BF16), wp_ref[...])
        h3_ref[...] = x + _sigmoid(gt) * pp

    return pl.pallas_call(
        body, name="ple_fwd", grid=(t // TMB,),
        in_specs=[_row(TMB, D), _row(TMB, PLE), _const((1, D)), _wblk(D // N_CHIPS, lay["gate"]), _const((PLE, D))],
        out_specs=[_row(TMB, D), _row(TMB, D)],
        out_shape=[_sds((t, D), F32), _sds((t, D), BF16)],
        compiler_params=_cp("parallel"),
    )(h2, p, g3, allw, wp)


def _mla_project(h_ref, g1_ref, wdn_ref, gq_ref, gkv_ref, wuq_ref, wukv_ref):
    x = h_ref[...]
    yn, xhat, rx = _rms(x, g1_ref[...], D)
    hn = yn.astype(BF16)
    lat = _dot(hn, wdn_ref[...])
    cq, cqhat, rq = _rms(lat[:, :QL], gq_ref[...], QL)
    ckv, ckvhat, rkv = _rms(lat[:, QL:QL + KVL], gkv_ref[...], KVL)
    kr_raw = lat[:, QL + KVL:]
    cqb = cq.astype(BF16)
    ckvb = ckv.astype(BF16)
    qp = _dot(cqb, wuq_ref[...])
    kvp = _dot(ckvb, wukv_ref[...])
    return dict(xhat=xhat, rx=rx, hn=hn, cqhat=cqhat, rq=rq, ckvhat=ckvhat, rkv=rkv, kr_raw=kr_raw,
                cqb=cqb, ckvb=ckvb, qp=qp, kvp=kvp)


def mla_pre_fwd(h, g1, wdn, gq, gkv, wuq, wukv, gqn, gqr, gkn, gkr, cos, sin):
    t = h.shape[0]

    def body(h_ref, g1_ref, wdn_ref, gq_ref, gkv_ref, wuq_ref, wukv_ref, gqn_ref, gqr_ref, gkn_ref, gkr_ref,
             c_ref, s_ref, q_ref, k_ref, v_ref):
        m = _mla_project(h_ref, g1_ref, wdn_ref, gq_ref, gkv_ref, wuq_ref, wukv_ref)
        c = c_ref[...]
        s = s_ref[...]
        kr, _, _ = _rms(m["kr_raw"], gkr_ref[...], DR)
        krb = _rope(kr, c, s).astype(BF16)
        for hd in range(HEADS):
            qn, _, _ = _rms(m["qp"][:, hd * DN:(hd + 1) * DN], gqn_ref[...], DN)
            qr, _, _ = _rms(m["qp"][:, D + hd * LANES:D + (hd + 1) * LANES], gqr_ref[...], DR)
            q_ref[hd, :, 0:DN] = (qn * SM_SCALE).astype(BF16)
            q_ref[hd, :, DN:2 * DN] = (_rope(qr, c, s) * SM_SCALE).astype(BF16)
            kn, _, _ = _rms(m["kvp"][:, hd * 2 * DN:hd * 2 * DN + DN], gkn_ref[...], DN)
            k_ref[hd, :, 0:DN] = kn.astype(BF16)
            k_ref[hd, :, DN:2 * DN] = krb
            v_ref[hd] = m["kvp"][:, hd * 2 * DN + DN:(hd + 1) * 2 * DN].astype(BF16)

    hb = lambda w: pl.BlockSpec((HEADS, TM, w), lambda i: (0, i, 0))
    return pl.pallas_call(
        body, name="mla_pre_fwd", grid=(t // TM,),
        in_specs=[_row(TM, D), _const((1, D)), _const((D, LATP)), _const((1, QL)), _const((1, KVL)),
                  _const((QL, 2 * D)), _const((KVL, 2 * D)), _const((1, LANES)), _const((1, LANES)),
                  _const((1, LANES)), _const((1, LANES)), _row(TM, LANES), _row(TM, LANES)],
        out_specs=[hb(2 * DN), hb(2 * DN), hb(DN)],
        out_shape=[_sds((HEADS, t, 2 * DN), BF16), _sds((HEADS, t, 2 * DN), BF16), _sds((HEADS, t, DN), BF16)],
        compiler_params=_cp("parallel"),
    )(h, g1, wdn, gq, gkv, wuq, wukv, gqn, gqr, gkn, gkr, cos, sin)


def _diagonal_mask(n=TQ):
    return lax.broadcasted_iota(jnp.int32, (n, n), 1) <= lax.broadcasted_iota(jnp.int32, (n, n), 0)


def flash_fwd(q, k, v, seq):
    t = q.shape[1]
    nb = t // seq
    tq = TQ_FWD
    nq = seq // tq
    hp = FWD_HEADS

    def body(q_ref, k_ref, v_ref, o_ref, lse_ref):
        qi = pl.program_id(2)
        qs = [q_ref[a] for a in range(hp)]

        def step(j, carry, diagonal=False):
            rows = pl.ds(pl.multiple_of(j * tq, tq), tq)
            out = []
            for a in range(hp):
                m, l, acc = carry[a]
                s = _dot_nt(qs[a], k_ref[a, rows, :])
                if diagonal:
                    s = jnp.where(_diagonal_mask(tq), s, -1e30)
                m_new = jnp.maximum(m, jnp.max(s, axis=-1, keepdims=True))
                p = jnp.exp(s - m_new)
                alpha = jnp.exp(m - m_new)
                l = alpha * l + jnp.sum(p, axis=-1, keepdims=True)
                acc = alpha * acc + _dot(p.astype(BF16), v_ref[a, rows, :])
                out.append((m_new, l, acc))
            return tuple(out)

        one = (jnp.full((tq, 1), -1e30, F32), jnp.zeros((tq, 1), F32), jnp.zeros((tq, DN), F32))
        done = step(qi, lax.fori_loop(0, qi, step, (one,) * hp), diagonal=True)
        for a, (m, l, acc) in enumerate(done):
            o_ref[:, a * DN:(a + 1) * DN] = (acc / l).astype(BF16)
            lse_ref[a] = m + jnp.log(l)

    return pl.pallas_call(
        body, name="flash_fwd", grid=(nb, HEADS // hp, nq),
        in_specs=[pl.BlockSpec((hp, tq, 2 * DN), lambda b, h, i: (h, b * nq + i, 0)),
                  pl.BlockSpec((hp, seq, 2 * DN), lambda b, h, i: (h, b, 0)),
                  pl.BlockSpec((hp, seq, DN), lambda b, h, i: (h, b, 0))],
        out_specs=[pl.BlockSpec((tq, hp * DN), lambda b, h, i: (b * nq + i, h)),
                   pl.BlockSpec((hp, tq, 1), lambda b, h, i: (h, b * nq + i, 0))],
        out_shape=[_sds((t, HEADS * DN), BF16), _sds((HEADS, t, 1), F32)],
        compiler_params=_cp("parallel", "parallel", "arbitrary"),
    )(q, k, v)


def _gmlp_in(hn, win_ref):
    pre = [_dot(hn, win_ref[c]) for c in range(N_CHIPS)]
    return jnp.concatenate(pre[:2], axis=1), jnp.concatenate(pre[2:], axis=1)


def gmlp_fwd(h, g1, allw, lay, lng, lnb, wm, bfull):
    t = h.shape[0]

    def body(h_ref, g1_ref, win_ref, lng_ref, lnb_ref, wm_ref, b_ref, y_ref, pre_ref):
        yn, _, _ = _rms(h_ref[...], g1_ref[...], D)
        pre_u, pre_v = _gmlp_in(yn.astype(BF16), win_ref)
        pre_ref[:, :GH] = pre_u.astype(BF16)
        pre_ref[:, GH:] = pre_v.astype(BF16)
        u = _gelu(pre_u)
        v = _gelu(pre_v)
        xc = v - jnp.mean(v, axis=-1, keepdims=True)
        rs = lax.rsqrt(jnp.mean(xc * xc, axis=-1, keepdims=True) + EPS)
        vnb = (xc * rs * lng_ref[...] + lnb_ref[...]).astype(BF16)
        for ch in range(TM // GC):
            rows = slice(ch * GC, (ch + 1) * GC)
            for g in range(GG):
                cols = slice(g * GD, (g + 1) * GD)
                sv = _dot(wm_ref[g], vnb[rows, cols]) + b_ref[:, cols]
                y_ref[rows, cols] = (u[rows, cols] * sv).astype(BF16)

    return pl.pallas_call(
        body, name="gmlp_fwd", grid=(t // TM,),
        in_specs=[_row(TM, D), _const((1, D)), _wblk(D, lay["in"]), _const((1, GH)), _const((1, GH)),
                  _const((GG, GC, GC)), _const((GC, GH))],
        out_specs=[_row(TM, GH), _row(TM, 2 * GH)],
        out_shape=[_sds((t, GH), BF16), _sds((t, 2 * GH), BF16)],
        compiler_params=_cp("parallel"),
    )(h, g1, allw, lng, lnb, wm, bfull)


def loss_head(h, tgt):
    t = h.shape[0]

    def body(h_ref, t_ref, dh_ref, loss_ref):
        @pl.when(pl.program_id(0) == 0)
        def _():
            loss_ref[...] = jnp.zeros_like(loss_ref)

        e = h_ref[...] - t_ref[...]
        dh_ref[...] = e * (1.0 / D)
        part = jnp.sum(jnp.sum(e * e, axis=-1, keepdims=True), axis=0, keepdims=True) * (0.5 / D)
        loss_ref[...] += jnp.broadcast_to(part, loss_ref.shape)

    return pl.pallas_call(
        body, name="loss_head", grid=(t // TMB,),
        in_specs=[_row(TMB, D), _row(TMB, D)],
        out_specs=[_row(TMB, D), _const((8, LANES))],
        out_shape=[_sds((t, D), F32), _sds((8, LANES), F32)],
        compiler_params=_cp("arbitrary"),
    )(h, tgt)


def _zero_at_first_step(*refs):
    @pl.when(pl.program_id(0) == 0)
    def _():
        for r in refs:
            r[...] = jnp.zeros_like(r)


def ple_bwd(dh3, h2, p, g3, allw, lay, wp):
    t = h2.shape[0]

    def body(dh_ref, h_ref, p_ref, g_ref, wg_ref, wp_ref, dh2_ref, dh2b_ref, dgt_ref, dpp_ref, dg_ref):
        _zero_at_first_step(dg_ref)
        dh3v = dh_ref[...]
        x = h_ref[...]
        g = g_ref[...]
        wg = _rows_joined(wg_ref)
        yn, xhat, r = _rms(x, g, D)
        gt = _dot(yn.astype(BF16), wg)
        pp = _dot(p_ref[...].astype(BF16), wp_ref[...])
        sg = _sigmoid(gt)
        dgt = (dh3v * pp * sg * (1.0 - sg)).astype(BF16)
        dgt_ref[...] = dgt
        dpp_ref[...] = (dh3v * sg).astype(BF16)
        dhn = _dot_nt(dgt, wg)
        _acc_rows(dg_ref, dhn * xhat)
        dh2 = dh3v + _rms_bwd(dhn, g, xhat, r, D)
        dh2_ref[...] = dh2
        dh2b_ref[...] = dh2.astype(BF16)

    return pl.pallas_call(
        body, name="ple_bwd", grid=(t // TMB,),
        in_specs=[_row(TMB, D), _row(TMB, D), _row(TMB, PLE), _const((1, D)), _wblk(D // N_CHIPS, lay["gate"]),
                  _const((PLE, D))],
        out_specs=[_row(TMB, D), _row(TMB, D), _row(TMB, D), _row(TMB, D), _const((8, D))],
        out_shape=[_sds((t, D), F32), _sds((t, D), BF16), _sds((t, D), BF16), _sds((t, D), BF16), _sds((8, D), F32)],
        compiler_params=_cp("arbitrary"),
    )(dh3, h2, p, g3, allw, wp)


def ffn_bwd(dh2, dh2b, h1, r, g2, allw, lay):
    t = h1.shape[0]

    def body(dh_ref, dhb_ref, h_ref, r_ref, g_ref, wu_ref, wd_ref, dh1_ref, dh1b_ref, du_ref, a_ref, dg_ref):
        _zero_at_first_step(dg_ref)
        dhb = dhb_ref[...]
        g = g_ref[...]
        _, xhat, rr = _rms(h_ref[...], g, D)
        dhn = jnp.zeros((TM, D), F32)
        for c in range(N_CHIPS):
            cs = slice(c * D, (c + 1) * D)
            rc = r_ref[:, cs].astype(F32)
            a_ref[:, cs] = (rc * rc).astype(BF16)
            da = _dot_nt(dhb, wd_ref[c])
            du = (da * (2.0 * rc)).astype(BF16)
            du_ref[:, cs] = du
            dhn = dhn + _dot_nt(du, wu_ref[c])
        _acc_rows(dg_ref, dhn * xhat)
        dh1 = dh_ref[...] + _rms_bwd(dhn, g, xhat, rr, D)
        dh1_ref[...] = dh1
        dh1b_ref[...] = dh1.astype(BF16)

    return pl.pallas_call(
        body, name="ffn_bwd", grid=(t // TM,),
        in_specs=[_row(TM, D), _row(TM, D), _row(TM, D), _row(TM, DFF), _const((1, D)), _wblk(D, lay["up"]),
                  _wblk(D, lay["down"])],
        out_specs=[_row(TM, D), _row(TM, D), _row(TM, DFF), _row(TM, DFF), _const((8, D))],
        out_shape=[_sds((t, D), F32), _sds((t, D), BF16), _sds((t, DFF), BF16), _sds((t, DFF), BF16),
                   _sds((8, D), F32)],
        compiler_params=_cp("arbitrary"),
    )(dh2, dh2b, h1, r, g2, allw, allw)


def linear_nt(a, allw, rows, row0):
    t = a.shape[0]
    k = N_CHIPS * rows

    def body(a_ref, w_ref, o_ref):
        o_ref[...] = _dot_nt(a_ref[...], _rows_joined(w_ref)).astype(BF16)

    return pl.pallas_call(
        body, name="linear_nt", grid=(t // TMB,),
        in_specs=[_row(TMB, D), _wblk(rows, row0)],
        out_specs=_row(TMB, k),
        out_shape=_sds((t, k), BF16),
        compiler_params=_cp("parallel"),
    )(a, allw)


def flash_bwd(q, k, v, o, do, lse, seq, after):
    t = q.shape[1]
    nb = t // seq
    nq = seq // TQ
    hp = BWD_HEADS

    def body(q_ref, k_ref, v_ref, o_ref, do_ref, lse_ref, after_ref, dq_ref, dk_ref, dv_ref):
        del after_ref
        kj = pl.program_id(2)

        @pl.when(kj == 0)
        def _():
            dq_ref[...] = jnp.zeros_like(dq_ref)

        def step(i, carry, diagonal=False):
            rows = pl.ds(pl.multiple_of(i * TQ, TQ), TQ)
            out = []
            for a in range(hp):
                dk, dv = carry[a]
                kv = k_ref[a]
                qv = q_ref[a, rows, :]
                dov = do_ref[rows, a * DN:(a + 1) * DN]
                ov = o_ref[rows, a * DN:(a + 1) * DN]
                delta = jnp.sum(dov.astype(F32) * ov.astype(F32), axis=-1, keepdims=True)
                s = _dot_nt(qv, kv)
                if diagonal:
                    s = jnp.where(_diagonal_mask(), s, -1e30)
                p = jnp.exp(s - lse_ref[a, rows, :])
                dp = _dot_nt(dov, v_ref[a])
                ds = (p * (dp - delta)).astype(BF16)
                dv = dv + _dot_tn(p.astype(BF16), dov)
                dk = dk + _dot_tn(ds, qv)
                dq_ref[a, rows, :] += _dot(ds, kv)
                out.append((dk, dv))
            return tuple(out)

        one = (jnp.zeros((TQ, 2 * DN), F32), jnp.zeros((TQ, DN), F32))
        done = lax.fori_loop(kj + 1, nq, step, step(kj, (one,) * hp, diagonal=True))
        for a, (dk, dv) in enumerate(done):
            dk_ref[a] = dk
            dv_ref[a] = dv

    return pl.pallas_call(
        body, name="flash_bwd", grid=(nb, HEADS // hp, nq),
        in_specs=[pl.BlockSpec((hp, seq, 2 * DN), lambda b, h, j: (h, b, 0)),
                  pl.BlockSpec((hp, TQ, 2 * DN), lambda b, h, j: (h, b * nq + j, 0)),
                  pl.BlockSpec((hp, TQ, DN), lambda b, h, j: (h, b * nq + j, 0)),
                  pl.BlockSpec((seq, hp * DN), lambda b, h, j: (b, h)),
                  pl.BlockSpec((seq, hp * DN), lambda b, h, j: (b, h)),
                  pl.BlockSpec((hp, seq, 1), lambda b, h, j: (h, b, 0)), _ANY],
        out_specs=[pl.BlockSpec((hp, seq, 2 * DN), lambda b, h, j: (h, b, 0)),
                   pl.BlockSpec((hp, TQ, 2 * DN), lambda b, h, j: (h, b * nq + j, 0)),
                   pl.BlockSpec((hp, TQ, DN), lambda b, h, j: (h, b * nq + j, 0))],
        out_shape=[_sds((HEADS, t, 2 * DN), F32), _sds((HEADS, t, 2 * DN), F32), _sds((HEADS, t, DN), F32)],
        compiler_params=_cp("parallel", "parallel", "arbitrary"),
    )(q, k, v, o, do, lse, after)


def mla_pre_bwd(dq, dk, dv, dh1, h, g1, wdn, gq, gkv, wuq, wukv, gqn, gqr, gkn, gkr, cos, sin):
    t = h.shape[0]

    def body(dq_ref, dk_ref, dv_ref, dh1_ref, h_ref, g1_ref, wdn_ref, gq_ref, gkv_ref, wuq_ref, wukv_ref,
             gqn_ref, gqr_ref, gkn_ref, gkr_ref, c_ref, s_ref,
             dh_ref, hn_ref, cq_ref, ckv_ref, dqp_ref, dkvp_ref, dlat_ref,
             dg1_ref, dgq_ref, dgkv_ref, dgqn_ref, dgqr_ref, dgkn_ref, dgkr_ref):
        _zero_at_first_step(dg1_ref, dgq_ref, dgkv_ref, dgqn_ref, dgqr_ref, dgkn_ref, dgkr_ref)
        m = _mla_project(h_ref, g1_ref, wdn_ref, gq_ref, gkv_ref, wuq_ref, wukv_ref)
        hn_ref[...] = m["hn"]
        cq_ref[...] = m["cqb"]
        ckv_ref[...] = m["ckvb"]
        c = c_ref[...]
        s = s_ref[...]
        gqn = gqn_ref[...]
        gqr = gqr_ref[...]
        gkn = gkn_ref[...]
        gkr = gkr_ref[...]

        dkr = dk_ref[0, :, DN:2 * DN]
        for hd in range(1, HEADS):
            dkr = dkr + dk_ref[hd, :, DN:2 * DN]
        dkr = _rope_t(dkr, c, s)
        _, krhat, rkr = _rms(m["kr_raw"], gkr, DR)
        _acc_rows(dgkr_ref, dkr * krhat)
        dkr_raw = _rms_bwd(dkr, gkr, krhat, rkr, DR)

        for hd in range(HEADS):
            ncols = slice(hd * DN, (hd + 1) * DN)
            _, xh, r = _rms(m["qp"][:, ncols], gqn, DN)
            dqn = dq_ref[hd, :, 0:DN] * SM_SCALE
            _acc_rows(dgqn_ref, dqn * xh)
            dqp_ref[:, ncols] = _rms_bwd(dqn, gqn, xh, r, DN).astype(BF16)

            rcols = slice(D + hd * LANES, D + (hd + 1) * LANES)
            _, xh, r = _rms(m["qp"][:, rcols], gqr, DR)
            dqr = _rope_t(dq_ref[hd, :, DN:2 * DN] * SM_SCALE, c, s)
            _acc_rows(dgqr_ref, dqr * xh)
            dqp_ref[:, rcols] = _rms_bwd(dqr, gqr, xh, r, DR).astype(BF16)

            kcols = slice(hd * 2 * DN, hd * 2 * DN + DN)
            _, xh, r = _rms(m["kvp"][:, kcols], gkn, DN)
            dkn = dk_ref[hd, :, 0:DN]
            _acc_rows(dgkn_ref, dkn * xh)
            dkvp_ref[:, kcols] = _rms_bwd(dkn, gkn, xh, r, DN).astype(BF16)
            dkvp_ref[:, hd * 2 * DN + DN:(hd + 1) * 2 * DN] = dv_ref[hd].astype(BF16)

        dcq = _dot_nt(dqp_ref[...], wuq_ref[...])
        _acc_rows(dgq_ref, dcq * m["cqhat"])
        dlat_q = _rms_bwd(dcq, gq_ref[...], m["cqhat"], m["rq"], QL)
        dckv = _dot_nt(dkvp_ref[...], wukv_ref[...])
        _acc_rows(dgkv_ref, dckv * m["ckvhat"])
        dlat_kv = _rms_bwd(dckv, gkv_ref[...], m["ckvhat"], m["rkv"], KVL)
        dlat = jnp.concatenate([dlat_q, dlat_kv, dkr_raw], axis=1).astype(BF16)
        dlat_ref[...] = dlat
        dhn = _dot_nt(dlat, wdn_ref[...])
        _acc_rows(dg1_ref, dhn * m["xhat"])
        dh_ref[...] = dh1_ref[...] + _rms_bwd(dhn, g1_ref[...], m["xhat"], m["rx"], D)

    hb = lambda w: pl.BlockSpec((HEADS, TM, w), lambda i: (0, i, 0))
    return pl.pallas_call(
        body, name="mla_pre_bwd", grid=(t // TM,),
        in_specs=[hb(2 * DN), hb(2 * DN), hb(DN), _row(TM, D), _row(TM, D), _const((1, D)), _const((D, LATP)),
                  _const((1, QL)), _const((1, KVL)), _const((QL, 2 * D)), _const((KVL, 2 * D)),
                  _const((1, LANES)), _const((1, LANES)), _const((1, LANES)), _const((1, LANES)),
                  _row(TM, LANES), _row(TM, LANES)],
        out_specs=[_row(TM, D), _row(TM, D), _row(TM, QL), _row(TM, KVL), _row(TM, 2 * D), _row(TM, 2 * D),
                   _row(TM, LATP), _const((8, D)), _const((8, QL)), _const((8, KVL)), _const((8, LANES)),
                   _const((8, LANES)), _const((8, LANES)), _const((8, LANES))],
        out_shape=[_sds((t, D), F32), _sds((t, D), BF16), _sds((t, QL), BF16), _sds((t, KVL), BF16),
                   _sds((t, 2 * D), BF16), _sds((t, 2 * D), BF16), _sds((t, LATP), BF16),
                   _sds((8, D), F32), _sds((8, QL), F32), _sds((8, KVL), F32), _sds((8, LANES), F32),
                   _sds((8, LANES), F32), _sds((8, LANES), F32), _sds((8, LANES), F32)],
        compiler_params=_cp("arbitrary"),
    )(dq, dk, dv, dh1, h, g1, wdn, gq, gkv, wuq, wukv, gqn, gqr, gkn, gkr, cos, sin)


def gmlp_bwd(dh1, dh1b, h, pre, g1, allw, lay, lng, lnb, wm, wmt, bfull, tril):
    t = h.shape[0]

    def body(dh1_ref, dh1b_ref, h_ref, pre_ref, g1_ref, win_ref, lng_ref, lnb_ref, wm_ref, wmt_ref, b_ref,
             wout_ref, tril_ref, dh_ref, hn_ref, dpre_ref, dws_ref, dbs_ref, dlng_ref, dlnb_ref, dg1_ref,
             dvn_s):
        _zero_at_first_step(dws_ref, dbs_ref, dlng_ref, dlnb_ref, dg1_ref)
        g1 = g1_ref[...]
        yn, xhat, rx = _rms(h_ref[...], g1, D)
        hn_ref[...] = yn.astype(BF16)
        dy = _dot_nt(dh1b_ref[...], _rows_joined(wout_ref))
        pre_u = pre_ref[:, :GH].astype(F32)
        pre_v = pre_ref[:, GH:].astype(F32)
        u, gg_u = _gelu_and_grad(pre_u)
        v, gg_v = _gelu_and_grad(pre_v)
        xc = v - jnp.mean(v, axis=-1, keepdims=True)
        rs = lax.rsqrt(jnp.mean(xc * xc, axis=-1, keepdims=True) + EPS)
        vhat = xc * rs
        lng = lng_ref[...]
        vnb = (vhat * lng + lnb_ref[...]).astype(BF16)
        dsv = dy * u
        dsvb = dsv.astype(BF16)
        tril_m = tril_ref[...]
        for ch in range(TM // GC):
            rows = slice(ch * GC, (ch + 1) * GC)
            dbs_ref[...] += dsv[rows, :]
            for g in range(GG):
                cols = slice(g * GD, (g + 1) * GD)
                sv = _dot(wm_ref[g], vnb[rows, cols]) + b_ref[:, cols]
                dpre_ref[rows, cols] = (dy[rows, cols] * sv * gg_u[rows, cols]).astype(BF16)
                dvn_s[rows, cols] = _dot(wmt_ref[g], dsvb[rows, cols])
                dws_ref[g] += _dot_nt(dsvb[rows, cols], vnb[rows, cols]) * tril_m
        dvn = dvn_s[...]
        _acc_rows(dlng_ref, dvn * vhat)
        _acc_rows(dlnb_ref, dvn)
        dvhat = dvn * lng
        dv = rs * (dvhat - jnp.mean(dvhat, axis=-1, keepdims=True)
                   - vhat * jnp.mean(dvhat * vhat, axis=-1, keepdims=True))
        dpre_v = (dv * gg_v).astype(BF16)
        dpre_ref[:, GH:] = dpre_v
        dhn = _dot_nt(dpre_ref[:, 0:D], win_ref[0])
        for c in range(1, N_CHIPS):
            dhn = dhn + _dot_nt(dpre_ref[:, c * D:(c + 1) * D], win_ref[c])
        _acc_rows(dg1_ref, dhn * xhat)
        dh_ref[...] = dh1_ref[...] + _rms_bwd(dhn, g1, xhat, rx, D)

    return pl.pallas_call(
        body, name="gmlp_bwd", grid=(t // TM,),
        in_specs=[_row(TM, D), _row(TM, D), _row(TM, D), _row(TM, 2 * GH), _const((1, D)), _wblk(D, lay["in"]),
                  _const((1, GH)), _const((1, GH)), _const((GG, GC, GC)), _const((GG, GC, GC)), _const((GC, GH)),
                  _wblk(GH // N_CHIPS, lay["out"]), _const((GC, GC))],
        out_specs=[_row(TM, D), _row(TM, D), _row(TM, 2 * GH), _const((GG, GC, GC)), _const((GC, GH)),
                   _const((8, GH)), _const((8, GH)), _const((8, D))],
        out_shape=[_sds((t, D), F32), _sds((t, D), BF16), _sds((t, 2 * GH), BF16), _sds((GG, GC, GC), F32),
                   _sds((GC, GH), F32), _sds((8, GH), F32), _sds((8, GH), F32), _sds((8, D), F32)],
        scratch_shapes=[pltpu.VMEM((TM, GH), F32)],
        compiler_params=_cp("arbitrary"),
    )(dh1, dh1b, h, pre, g1, allw, lng, lnb, wm, wmt, bfull, allw, tril)


def _token_step(t):
    return 1024 if t % 1024 == 0 else 512


def mm_tn(a, b):
    t, k = a.shape
    n = b.shape[1]
    tk = min(k, 1024)
    tn = min(n, 1024)
    tt = _token_step(t)

    def body(a_ref, b_ref, o_ref):
        @pl.when(pl.program_id(2) == 0)
        def _():
            o_ref[...] = jnp.zeros_like(o_ref)

        o_ref[...] += _dot_tn(a_ref[...].astype(BF16), b_ref[...].astype(BF16))

    return pl.pallas_call(
        body, name="mm_tn", grid=(k // tk, n // tn, t // tt),
        in_specs=[pl.BlockSpec((tt, tk), lambda i, j, s: (s, i)), pl.BlockSpec((tt, tn), lambda i, j, s: (s, j))],
        out_specs=pl.BlockSpec((tk, tn), lambda i, j, s: (i, j)), out_shape=_sds((k, n), F32),
        compiler_params=_cp("parallel", "parallel", "arbitrary"),
    )(a, b)


def mm_tn_into(buf, a, b, rows, row0, col_sharded):
    t = a.shape[0]
    tt = _token_step(t)
    assert row0 % rows == 0 and a.shape[1] == (rows if col_sharded else N_CHIPS * rows), (rows, row0, a.shape)
    assert b.shape[1] == (N_CHIPS * D if col_sharded else D), b.shape
    grid = (1, N_CHIPS, t // tt) if col_sharded else (N_CHIPS, 1, t // tt)
    fresh = isinstance(buf, int)

    def body(*refs):
        a_ref, b_ref, o_ref = refs[-3:]

        @pl.when(pl.program_id(2) == 0)
        def _():
            o_ref[...] = jnp.zeros_like(o_ref)

        o_ref[...] += _dot_tn(a_ref[...].astype(BF16), b_ref[...].astype(BF16))

    specs = [pl.BlockSpec((tt, rows), lambda i, j, s: (s, i)), pl.BlockSpec((tt, D), lambda i, j, s: (s, j))]
    return pl.pallas_call(
        body, name="mm_tn_into", grid=grid,
        in_specs=specs if fresh else [_ANY] + specs,
        out_specs=pl.BlockSpec((None, rows, D), lambda i, j, s: (i + j, row0 // rows, 0)),
        out_shape=_sds((N_CHIPS, buf, D) if fresh else buf.shape, F32),
        input_output_aliases={} if fresh else {0: 0},
        compiler_params=_cp("parallel", "parallel", "arbitrary"),
    )(*((a, b) if fresh else (buf, a, b)))


def adamw(w, g, m, v):
    rows, cols = w.shape
    tr = rows if rows <= 512 else next(r for r in (512, 384, 256, 128) if rows % r == 0)
    c1 = 1.0 - ADAM_B1 ** ADAM_STEP
    c2 = 1.0 - ADAM_B2 ** ADAM_STEP

    def body(w_ref, g_ref, m_ref, v_ref, d_ref, mo_ref, vo_ref):
        gv = g_ref[...]
        mn = ADAM_B1 * m_ref[...] + (1.0 - ADAM_B1) * gv
        vn = ADAM_B2 * v_ref[...] + (1.0 - ADAM_B2) * (gv * gv)
        mo_ref[...] = mn
        vo_ref[...] = vn
        d_ref[...] = -ADAM_LR * ((mn / c1) / (jnp.sqrt(vn / c2) + ADAM_EPS) + ADAM_WD * w_ref[...])

    spec = pl.BlockSpec((tr, cols), lambda i: (i, 0))
    return pl.pallas_call(
        body, name="adamw", grid=(rows // tr,),
        in_specs=[spec] * 4, out_specs=[spec] * 3, out_shape=[_sds((rows, cols), F32)] * 3,
        compiler_params=_cp("parallel"),
    )(w, g, m, v)


def adamw_layers(w, m, v, bufs, row0s):
    nl, a, _ = w.shape
    tr = min(a, 256)
    c1 = 1.0 - ADAM_B1 ** ADAM_STEP
    c2 = 1.0 - ADAM_B2 ** ADAM_STEP
    assert all(r % tr == 0 for r in row0s) and a % tr == 0, (row0s, a)

    def body(w_ref, m_ref, v_ref, *rest):
        g_refs, (g_ref, d_ref, mo_ref, vo_ref) = rest[:nl], rest[nl:]
        for l in range(nl):
            @pl.when(pl.program_id(0) == l)
            def _(l=l):
                gv = g_refs[l][...]
                g_ref[...] = gv
                mn = ADAM_B1 * m_ref[...] + (1.0 - ADAM_B1) * gv
                vn = ADAM_B2 * v_ref[...] + (1.0 - ADAM_B2) * (gv * gv)
                mo_ref[...] = mn
                vo_ref[...] = vn
                d_ref[...] = -ADAM_LR * ((mn / c1) / (jnp.sqrt(vn / c2) + ADAM_EPS) + ADAM_WD * w_ref[...])

    def rows_of(l, row0):
        return pl.BlockSpec((tr, D), lambda li, i: (jnp.where(li == l, row0 // tr + i, row0 // tr), 0))

    spec = pl.BlockSpec((None, tr, D), lambda li, i: (li, i, 0))
    return pl.pallas_call(
        body, name="adamw_layers", grid=(nl, a // tr),
        in_specs=[spec] * 3 + [rows_of(l, r) for l, r in enumerate(row0s)],
        out_specs=[spec] * 4, out_shape=[_sds(w.shape, F32)] * 4,
        compiler_params=_cp("arbitrary", "arbitrary"),
    )(w, m, v, *bufs)


def _place():
    return lax.axis_index("x"), lax.axis_index("y"), lax.axis_index("c")


def _other_chips(x, y):
    return [(1 - x, y), (x, 1 - y), (1 - x, 1 - y)]


_ANY = pl.BlockSpec(memory_space=pl.ANY)


_HBM = pl.BlockSpec(memory_space=pltpu.HBM)
_SEM = pl.BlockSpec(memory_space=pltpu.SEMAPHORE)
_EFFECT = pltpu.SideEffectType.DATAFLOW_SIDE_EFFECTING
N_ICI = 3


def _exchange_start(name, src, land, copies, n):
    def body(src_ref, land_ref, *outs):
        sems, token = outs[:2 * n], outs[-1]
        for j, (s, d, to) in enumerate(copies(src_ref, land_ref, _place())):
            pltpu.make_async_remote_copy(src_ref=s, dst_ref=d, send_sem=sems[j], recv_sem=sems[n + j],
                                         device_id=to, device_id_type=MESH).start()
        token[...] = jnp.zeros_like(token)

    sem = pltpu.SemaphoreType.DMA(())
    outs = pl.pallas_call(
        body, name=name,
        out_shape=(sem,) * (2 * n) + (pltpu.HBM(src.shape, src.dtype), pltpu.HBM(land.shape, land.dtype),
                                      _sds((8, LANES), F32)),
        in_specs=(_HBM, _HBM),
        out_specs=(_SEM,) * (2 * n) + (_HBM, _HBM, pl.BlockSpec(memory_space=pltpu.VMEM)),
        input_output_aliases={0: 2 * n, 1: 2 * n + 1},
        compiler_params=pltpu.CompilerParams(has_side_effects=_EFFECT),
    )(pltpu.with_memory_space_constraint(src, pltpu.HBM), pltpu.with_memory_space_constraint(land, pltpu.HBM))
    return outs[:2 * n], outs[2 * n], outs[2 * n + 1], outs[-1]


def _exchange_wait(name, sems, src, land, after, arrivals):
    n = len(sems) // 2

    def body(src_ref, land_ref, *rest):
        sems = rest[:2 * n]
        for j, (s, d) in enumerate(arrivals(src_ref, land_ref, _place())):
            cp = pltpu.make_async_remote_copy(src_ref=s, dst_ref=d, send_sem=sems[j], recv_sem=sems[n + j],
                                              device_id=_place(), device_id_type=MESH)
            cp.wait_send()
            cp.wait_recv()

    return pl.pallas_call(
        body, name=name, out_shape=(pltpu.HBM(src.shape, src.dtype), pltpu.HBM(land.shape, land.dtype)),
        in_specs=(_HBM, _HBM) + (_SEM,) * (2 * n) + (_ANY,), out_specs=(_HBM, _HBM),
        input_output_aliases={0: 0, 1: 1},
        compiler_params=pltpu.CompilerParams(has_side_effects=_EFFECT),
    )(src, land, *sems, after)


def _halves(c, hh):
    return pl.ds(pl.multiple_of(c * hh, 16), hh), pl.ds(pl.multiple_of((1 - c) * hh, 16), hh)


def gather_start(land, tag):
    _, rr, _ = land.shape
    assert rr % 32 == 0, rr

    def copies(_, land_ref, place):
        x, y, c = place
        mine = land_ref.at[2 * x + y, _halves(c, rr // 2)[0]]
        return [(mine, mine, (cx, cy, c)) for cx, cy in _other_chips(x, y)]

    return _exchange_start(f"gather_start_{tag}", jnp.zeros((8, LANES), F32), land, copies, N_ICI)


def gather_wait(sems, src, land, after, tag):
    def arrivals(_, land_ref, place):
        x, y, c = place
        half = _halves(c, land.shape[1] // 2)[0]
        return [(land_ref.at[2 * x + y, half], land_ref.at[2 * cx + cy, half]) for cx, cy in _other_chips(x, y)]

    return _exchange_wait(f"gather_wait_{tag}", sems, src, land, after, arrivals)


def pass_start(land, tag):
    def copies(_, land_ref, place):
        x, y, c = place
        half = _halves(c, land.shape[1] // 2)[0]
        return [(land_ref.at[2 * cx + cy, half], land_ref.at[2 * cx + cy, half], (x, y, 1 - c))
                for cx, cy in _other_chips(x, y)]

    return _exchange_start(f"pass_start_{tag}", jnp.zeros((8, LANES), F32), land, copies, N_ICI)


def pass_wait(sems, src, land, after, tag):
    def arrivals(_, land_ref, place):
        x, y, c = place
        mine, other = _halves(c, land.shape[1] // 2)
        return [(land_ref.at[2 * cx + cy, mine], land_ref.at[2 * cx + cy, other]) for cx, cy in _other_chips(x, y)]

    return _exchange_wait(f"pass_wait_{tag}", sems, src, land, after, arrivals)


def swap_start(g, tag):
    _, rr, cc = g.shape

    def copies(g_ref, got_ref, place):
        x, y, c = place
        other = _halves(c, rr // 2)[1]
        return [(g_ref.at[k, other], got_ref.at[k], (x, y, 1 - c)) for k in range(N_CHIPS)]

    return _exchange_start(f"swap_start_{tag}", g, lax.empty((N_CHIPS, rr // 2, cc), g.dtype), copies, N_CHIPS)


def swap_wait(sems, g, got, after, tag):
    def arrivals(g_ref, got_ref, place):
        other = _halves(place[2], g.shape[1] // 2)[1]
        return [(g_ref.at[k, other], got_ref.at[k]) for k in range(N_CHIPS)]

    return _exchange_wait(f"swap_wait_{tag}", sems, g, got, after, arrivals)


def chip_sum(place, g32, got):
    _, rr, cc = g32.shape
    hh = rr // 2
    tr = SUM_ROWS
    assert rr % 2 == 0 and hh % tr == 0, (rr, tr)
    nb = hh // tr

    def body(place_ref, g_ref, got_ref, own_ref, all_ref):
        s = g_ref[...] + got_ref[...].astype(F32)
        all_ref[...] = s.astype(BF16)
        own_ref[...] = g_ref[place_ref[1]] + got_ref[place_ref[1]].astype(F32)

    return pl.pallas_call(
        body, name="chip_sum",
        grid_spec=pltpu.PrefetchScalarGridSpec(
            num_scalar_prefetch=1, grid=(nb,),
            in_specs=[pl.BlockSpec((N_CHIPS, tr, cc), lambda i, pr: (0, pr[0] * nb + i, 0)),
                      pl.BlockSpec((N_CHIPS, tr, cc), lambda i, pr: (0, i, 0))],
            out_specs=[pl.BlockSpec((tr, cc), lambda i, pr: (i, 0)),
                       pl.BlockSpec((N_CHIPS, tr, cc), lambda i, pr: (0, i, 0))]),
        out_shape=[_sds((hh, cc), F32), _sds((N_CHIPS, hh, cc), BF16)],
        compiler_params=_cp("parallel"),
    )(place, g32, got)


def _scatter_copies(s_ref, land_ref, place):
    x, y, c = place
    return [(s_ref.at[2 * cx + cy], land_ref.at[j], (cx, cy, c)) for j, (cx, cy) in enumerate(_other_chips(x, y))]


def scatter_start(s, tag):
    return _exchange_start(f"scatter_start_{tag}", s, lax.empty((N_ICI,) + s.shape[1:], s.dtype), _scatter_copies, N_ICI)


def scatter_wait(sems, s, land, after, tag):
    return _exchange_wait(f"scatter_wait_{tag}", sems, s, land, after,
                          lambda s_ref, land_ref, place: [(a, b) for a, b, _ in _scatter_copies(s_ref, land_ref, place)])


def final_sum(place, own, got):
    hh, cc = own.shape
    tr = SUM_ROWS
    assert hh % tr == 0, (hh, tr)
    nb = hh // tr

    def body(place_ref, own_ref, got_ref, o_ref):
        del place_ref
        o_ref[...] = ((own_ref[...] + got_ref[0].astype(F32)) + got_ref[1].astype(F32)) + got_ref[2].astype(F32)

    return pl.pallas_call(
        body, name="final_sum",
        grid_spec=pltpu.PrefetchScalarGridSpec(
            num_scalar_prefetch=1, grid=(nb,),
            in_specs=[pl.BlockSpec((tr, cc), lambda i, pr: (i, 0)), pl.BlockSpec((3, tr, cc), lambda i, pr: (0, i, 0))],
            out_specs=pl.BlockSpec((tr, cc), lambda i, pr: (pr[0] * nb + i, 0))),
        out_shape=_sds((2 * hh, cc), F32),
        compiler_params=_cp("parallel"),
    )(place, own, got)


def share_start(f, tag):
    def copies(_, f_ref, place):
        x, y, c = place
        mine = f_ref.at[_halves(c, f.shape[0] // 2)[0]]
        return [(mine, mine, (x, y, 1 - c))]

    return _exchange_start(f"share_start_{tag}", jnp.zeros((8, LANES), F32), f, copies, 1)


def share_wait(sems, src, f, after, tag):
    def arrivals(_, f_ref, place):
        mine, other = _halves(place[2], f.shape[0] // 2)
        return [(f_ref.at[mine], f_ref.at[other])]

    return _exchange_wait(f"share_wait_{tag}", sems, src, f, after, arrivals)


N_DEV = 8


def _peers(place):
    x, y, c = place
    return [((1 - x) if r & 4 else x, (1 - y) if r & 2 else y, (1 - c) if r & 1 else c) for r in range(1, N_DEV)]


def _device_index(place):
    x, y, c = place
    return 4 * x + 2 * y + c


def small_start(land, tag):
    def copies(_, land_ref, place):
        mine = land_ref.at[_device_index(place)]
        return [(mine, mine, to) for to in _peers(place)]

    return _exchange_start(f"small_start_{tag}", jnp.zeros((8, LANES), F32), land, copies, N_DEV - 1)


def small_wait(sems, src, land, after, tag):
    def arrivals(_, land_ref, place):
        return [(land_ref.at[_device_index(place)], land_ref.at[_device_index(peer)]) for peer in _peers(place)]

    return _exchange_wait(f"small_wait_{tag}", sems, src, land, after, arrivals)


def sum_devices(land):
    _, rr, cc = land.shape
    tr = 56
    assert rr % tr == 0, rr

    def body(l_ref, o_ref):
        acc = l_ref[0]
        for d in range(1, N_DEV):
            acc = acc + l_ref[d]
        o_ref[...] = acc

    return pl.pallas_call(
        body, name="sum_devices", grid=(rr // tr,),
        in_specs=[pl.BlockSpec((N_DEV, tr, cc), lambda i: (0, i, 0))],
        out_specs=pl.BlockSpec((tr, cc), lambda i: (i, 0)), out_shape=_sds((rr, cc), F32),
        compiler_params=_cp("parallel"),
    )(land)


_BIG = ["mla_w_down", "mla_w_uq", "mla_w_ukv", "mla_w_out", "gmlp_w_in", "gmlp_w_out", "ffn_w_up", "ffn_w_down",
        "ple_w_gate", "ple_w_proj"]
_SMALL_REST = ["norm_mix", "norm_ffn", "norm_ple", "mla_q_lora_g", "mla_kv_lora_g", "mla_q_nope_g", "mla_q_rope_g",
               "mla_k_nope_g", "mla_k_rope_g"]
_SMALL_GMLP = ["gmlp_ln_g", "gmlp_ln_b", "gmlp_w_s", "gmlp_b_s"]
_SMALL = _SMALL_REST + _SMALL_GMLP

_LAY_MLA = dict(up=0, down=1024, out=2048, gate=2304, wdn=2560, wuq=2736, wukv=2880, proj=3008, rows=3072)
_LAY_MLA_MAIN = dict(up=0, down=1024, out=2048, gate=2304, rows=2560)
_LAY_MLA_ODD = dict(wdn=0, wuq=176, wukv=320, proj=448, rows=512)
_LAY_GMLP = {"up": 0, "down": 1024, "in": 2048, "out": 3072, "gate": 3584, "proj": 3840, "ln": 3904, "rows": 4096}
SPLIT_LAYERS = (0,)


def _layer_units(i):
    j = i // 2
    if i % 2 == 0:
        odd, lay = (_LAY_MLA_ODD, _LAY_MLA_MAIN) if i in SPLIT_LAYERS else (_LAY_MLA, _LAY_MLA)
        small = [("mla_w_down", j, odd["wdn"]), ("mla_w_uq", j, odd["wuq"]), ("mla_w_ukv", j, odd["wukv"]),
                 ("ple_w_proj", i, odd["proj"])]
        large = [("ffn_w_up", i, lay["up"]), ("ffn_w_down", i, lay["down"]), ("mla_w_out", j, lay["out"]),
                 ("ple_w_gate", i, lay["gate"])]
        return [("odd", odd, small), ("main", lay, large)] if i in SPLIT_LAYERS else [("main", lay, large + small)]
    lay = _LAY_GMLP
    return [("main", lay, [("ffn_w_up", i, lay["up"]), ("ffn_w_down", i, lay["down"]), ("gmlp_w_in", j, lay["in"]),
                           ("gmlp_w_out", j, lay["out"]), ("ple_w_gate", i, lay["gate"]),
                           ("ple_w_proj", i, lay["proj"])])]


def _pack_rows(parts, dtype, pad_to=None, slot=False):
    size = sum(p.size for p in parts)
    tail = [] if pad_to is None or pad_to * D == size else [jnp.zeros((pad_to * D - size,), dtype)]
    shape = (1, -1, D) if slot else (-1, D)
    if all(p.size % D == 0 for p in parts + tail):
        return jnp.concatenate([p.astype(dtype).reshape(shape) for p in parts + tail], axis=len(shape) - 2)
    return jnp.concatenate([p.astype(dtype).reshape(-1) for p in parts + tail]).reshape(shape)


def _odd(allw, row0, a, b):
    return allw[:, row0:row0 + a * b // D].reshape(N_CHIPS, a, b)


def _cols_joined(s):
    return jnp.transpose(s, (1, 0, 2)).reshape(s.shape[1], N_CHIPS * s.shape[2])


def _col_shards(full):
    a, bb = full.shape
    return jnp.transpose(full.reshape(a, N_CHIPS, bb // N_CHIPS), (1, 0, 2)).reshape(N_CHIPS, -1, D)


def _pad_lanes(g):
    return jnp.pad(g, ((0, 0), (0, LANES - g.shape[1])))


def _split_uq(wuq):
    l = wuq.shape[0]
    w = wuq.reshape(l, QL, HEADS, DN + DR)
    nope = w[..., :DN].reshape(l, QL, HEADS * DN)
    rope = jnp.pad(w[..., DN:], ((0, 0), (0, 0), (0, 0), (0, LANES - DR))).reshape(l, QL, HEADS * LANES)
    return jnp.concatenate([nope, rope], axis=-1)


def _merge_uq(d):
    nope = d[:, :HEADS * DN].reshape(QL, HEADS, DN)
    rope = d[:, HEADS * DN:].reshape(QL, HEADS, LANES)[..., :DR]
    return jnp.concatenate([nope, rope], axis=-1).reshape(QL, HEADS * (DN + DR))


def _rope_tables(positions):
    inv_freq = ROPE_BASE ** (-(jnp.arange(0, DR, 2, dtype=F32) / DR))
    ang = positions.reshape(-1).astype(F32)[:, None] * inv_freq
    z = jnp.zeros((ang.shape[0], LANES - DR), F32)
    return (jnp.concatenate([jnp.cos(ang), jnp.cos(ang), z], axis=1),
            jnp.concatenate([jnp.sin(ang), jnp.sin(ang), z], axis=1))


def kernel(x, p, positions, norm_mix, norm_ffn, norm_ple, mla_w_down, mla_q_lora_g, mla_kv_lora_g, mla_w_uq, mla_w_ukv, mla_q_nope_g, mla_q_rope_g, mla_k_nope_g, mla_k_rope_g, mla_w_out, gmlp_w_in, gmlp_ln_g, gmlp_ln_b, gmlp_w_s, gmlp_b_s, gmlp_w_out, ffn_w_up, ffn_w_down, ple_w_gate, ple_w_proj, loss_target, m_norm_mix, m_norm_ffn, m_norm_ple, m_mla_w_down, m_mla_q_lora_g, m_mla_kv_lora_g, m_mla_w_uq, m_mla_w_ukv, m_mla_q_nope_g, m_mla_q_rope_g, m_mla_k_nope_g, m_mla_k_rope_g, m_mla_w_out, m_gmlp_w_in, m_gmlp_ln_g, m_gmlp_ln_b, m_gmlp_w_s, m_gmlp_b_s, m_gmlp_w_out, m_ffn_w_up, m_ffn_w_down, m_ple_w_gate, m_ple_w_proj, v_norm_mix, v_norm_ffn, v_norm_ple, v_mla_w_down, v_mla_q_lora_g, v_mla_kv_lora_g, v_mla_w_uq, v_mla_w_ukv, v_mla_q_nope_g, v_mla_q_rope_g, v_mla_k_nope_g, v_mla_k_rope_g, v_mla_w_out, v_gmlp_w_in, v_gmlp_ln_g, v_gmlp_ln_b, v_gmlp_w_s, v_gmlp_b_s, v_gmlp_w_out, v_ffn_w_up, v_ffn_w_down, v_ple_w_gate, v_ple_w_proj):
    args = dict(locals())
    weights = {n: args[n] for n in _BIG + _SMALL}
    depth = norm_mix.shape[0]
    nb, seq, _ = x.shape
    t = nb * seq
    assert seq % TQ == 0 and seq % TM == 0 and t % 512 == 0, (nb, seq)
    cx = lax.axis_index("x")
    cy = lax.axis_index("y")
    cc = lax.axis_index("c")
    chip = 2 * cx + cy

    gathers = {}
    token = None
    for i in range(depth):
        for key, lay, parts in _layer_units(i):
            rows = [weights[n][l] for n, l, _ in parts]
            if token is not None:
                rows[0] = rows[0] + token[0, 0]
            if "ln" in lay:
                ln = jnp.stack([gmlp_ln_g[i // 2], gmlp_ln_b[i // 2]]).astype(F32)
                bits = lax.bitcast_convert_type(ln, BF16).reshape(-1)
                rows.append(jnp.pad(bits, (0, 16 * D - bits.size)).reshape(16, D))
            mine = _pack_rows(rows, BF16, pad_to=lay["rows"], slot=True)
            land = lax.dynamic_update_slice(lax.empty((N_CHIPS, lay["rows"], D), BF16), mine, (chip, 0, 0))
            sems, src, land, token = gather_start(land, f"{i}{key}")
            gathers[i, key] = (sems, src, land)
    allw = [None] * depth

    tril = jnp.tril(jnp.ones((GC, GC), F32))
    wm = (gmlp_w_s * tril).astype(BF16)
    wmt = jnp.swapaxes(wm, -1, -2)
    bfull = jnp.repeat(jnp.swapaxes(gmlp_b_s, -1, -2), GD, axis=-1)
    cos, sin = _rope_tables(positions)
    row = lambda g: g.reshape(1, -1)
    gqr = _pad_lanes(mla_q_rope_g)
    gkr = _pad_lanes(mla_k_rope_g)

    h = x.reshape(t, D)
    pt = p.reshape(depth, t, PLE)
    saved = []

    passing = {}

    def arrive(i, key, after):
        sems, src, land = gathers[i, key]
        _, land = gather_wait(sems, src, land, after, f"{i}{key}")
        passing[i, key] = pass_start(land, f"{i}{key}")
        return passing[i, key][3]

    def needed(i, key, after=None):
        sems, src, land, tok = passing.pop((i, key))
        return pass_wait(sems, src, land, tok if after is None else after, f"{i}{key}")[1]

    arrive(0, _layer_units(0)[0][0], token)
    for i in range(depth):
        j = i // 2
        lay = _layer_units(i)[-1][1]
        s = dict(h=h)
        if i % 2 == 0:
            split = i in SPLIT_LAYERS
            olay = _layer_units(i)[0][1]
            odd = needed(i, "odd" if split else "main", None if i == 0 else h)
            wdn = jnp.pad(_odd(odd, olay["wdn"], D // N_CHIPS, LAT).reshape(D, LAT), ((0, 0), (0, LATP - LAT)))
            wuq = _split_uq(_cols_joined(_odd(odd, olay["wuq"], QL, 384))[None])[0]
            wukv = _cols_joined(_odd(odd, olay["wukv"], KVL, 512))
            wp = _cols_joined(_odd(odd, olay["proj"], PLE, 256))
            mla_args = (row(norm_mix[i]), wdn, row(mla_q_lora_g[j]), row(mla_kv_lora_g[j]), wuq, wukv,
                        row(mla_q_nope_g[j]), gqr[j:j + 1], row(mla_k_nope_g[j]), gkr[j:j + 1], cos, sin)
            q, k, v = mla_pre_fwd(h, *mla_args)
            y, lse = flash_fwd(q, k, v, seq)
            if split and i == 0:
                arrive(i, "main", y)
            aw = needed(i, "main", y) if split else odd
            s.update(q=q, k=k, v=v, lse=lse, mla_args=mla_args)
        else:
            aw = needed(i, "main", h)
            ln = lax.bitcast_convert_type(aw[:, lay["ln"]:lay["ln"] + 2].reshape(N_CHIPS, 2, GH // N_CHIPS, 2), F32)
            ln = jnp.transpose(ln, (1, 0, 2)).reshape(2, 1, GH)
            wp = _cols_joined(_odd(aw, lay["proj"], PLE, 256))
            y, pre = gmlp_fwd(h, row(norm_mix[i]), aw, lay, ln[0], ln[1], wm[j], bfull[j])
            s.update(pre=pre, ln=ln)
        allw[i] = aw
        g2 = row(norm_ffn[i])
        if i + 1 < depth:
            for key, _, _ in _layer_units(i + 1):
                g2 = g2 + arrive(i + 1, key, y)[0:1, 0:1]
        h1, h2, hn2, r = mixffn_fwd(h, y, aw, lay, g2)
        h, hn3 = ple_fwd(h2, pt[i], row(norm_ple[i]), aw, lay, wp)
        s.update(y=y, wp=wp, h1=h1, h2=h2, hn2=hn2, r=r, hn3=hn3)
        saved.append(s)

    dh, loss_part = loss_head(h, loss_target.reshape(t, D))
    loss = lax.psum(loss_part[0, 0], ("x", "y", "c"))

    gs = {n: [None] * weights[n].shape[0] for n in _SMALL}
    gw = {n: [None] * weights[n].shape[0] for n in _BIG}
    place = jnp.stack([cc, chip]).astype(jnp.int32)
    scatters = []
    swaps = []
    token = None

    def put(b, row0, shards):
        return lax.dynamic_update_slice(b, shards.reshape(N_CHIPS, -1, D), (0, row0, 0))

    def small_size(n):
        return weights[n].shape[0] * GH if n in ("gmlp_ln_g", "gmlp_ln_b") else weights[n].size

    def small_exchange(names, zero, tag):
        rows = -(-sum(small_size(n) for n in names) // (56 * D)) * 56
        part = [jnp.stack(gs[n]) for n in names]
        part = _pack_rows([part[0] + zero] + part[1:], F32, pad_to=rows, slot=True)
        land = lax.dynamic_update_slice(lax.empty((N_DEV, rows, D), F32), part, (2 * chip + cc, 0, 0))
        return small_start(land, tag)

    def swap(i, key, buf):
        sems, buf, got, tok = swap_start(buf, f"{i}{key}")
        swaps.append((i, key, sems, buf, got))
        return tok

    def swapped(after, zero):
        while swaps:
            i, key, sems, g, got = swaps.pop(0)
            g, got = swap_wait(sems, g, got, after, f"{i}{key}")
            own, sums = chip_sum(place, g, got)
            sems, sums, land, tok = scatter_start(sums, f"{i}{key}")
            scatters.append((i, key, own, sems, sums, land))
            zero = zero + tok[0:1, 0:1]
        return zero

    for i in reversed(range(depth)):
        j = i // 2
        lay = _layer_units(i)[-1][1]
        aw = allw[i]
        s = saved[i]

        g3 = row(norm_ple[i])
        if token is not None:
            g3 = g3 + token[0:1, 0:1]
        dh2, dh2b, dgt, dpp, dg3 = ple_bwd(dh, s["h2"], pt[i], g3, aw, lay, s["wp"])
        gs["norm_ple"][i] = dg3[0]
        buf = mm_tn_into(lay["rows"], s["hn3"], dgt, D // N_CHIPS, lay["gate"], False)
        dproj = _col_shards(mm_tn(pt[i], dpp))
        if "ln" in lay:
            buf = put(buf, lay["ln"], jnp.zeros((N_CHIPS, lay["rows"] - lay["ln"], D), F32))
            buf = put(buf, lay["proj"], dproj)
        dh1, dh1b, du, a, dg2 = ffn_bwd(dh2, dh2b, s["h1"], s["r"], row(norm_ffn[i]), aw, lay)
        gs["norm_ffn"][i] = dg2[0]
        buf = mm_tn_into(buf, a, dh2b, D, lay["down"], False)
        buf = mm_tn_into(buf, s["hn2"], du, D, lay["up"], True)
        buf = mm_tn_into(buf, s["y"], dh1b, s["y"].shape[1] // N_CHIPS, lay["out"], False)
        g1 = swapped(dh1, row(norm_mix[i]))
        if i % 2 == 0:
            split = i in SPLIT_LAYERS
            do = linear_nt(dh1b, aw, D // N_CHIPS, lay["out"])
            dq, dk, dv = flash_bwd(s["q"], s["k"], s["v"], s["y"], do, s["lse"], seq,
                                   after=swap(i, "main", buf) if split else dh1b)
            g1 = swapped(dq, g1)
            (dh, hn1, cq, ckv, dqp, dkvp, dlat, dg1, dgq, dgkv, dgqn, dgqr, dgkn, dgkr) = mla_pre_bwd(
                dq, dk, dv, dh1, s["h"], g1, *s["mla_args"][1:])
            gs["norm_mix"][i] = dg1[0]
            gs["mla_q_lora_g"][j] = dgq[0]
            gs["mla_kv_lora_g"][j] = dgkv[0]
            gs["mla_q_nope_g"][j] = dgqn[0]
            gs["mla_q_rope_g"][j] = dgqr[0, :DR]
            gs["mla_k_nope_g"][j] = dgkn[0]
            gs["mla_k_rope_g"][j] = dgkr[0, :DR]
            small = [mm_tn(hn1, dlat)[:, :LAT].reshape(N_CHIPS, -1, D), _col_shards(_merge_uq(mm_tn(cq, dqp))),
                     _col_shards(mm_tn(ckv, dkvp)), dproj]
            if split:
                buf = jnp.concatenate(small, axis=1)
            else:
                buf = put(buf, lay["wdn"], jnp.concatenate(small, axis=1))
            key = "odd" if split else "main"
        else:
            dh, hn1, dpre, dws, dbs, dlng, dlnb, dg1 = gmlp_bwd(
                dh1, dh1b, s["h"], s["pre"], g1, aw, lay, s["ln"][0], s["ln"][1], wm[j], wmt[j], bfull[j], tril)
            gs["norm_mix"][i] = dg1[0]
            gs["gmlp_ln_g"][j] = dlng[0]
            gs["gmlp_ln_b"][j] = dlnb[0]
            gs["gmlp_w_s"][j] = dws
            gs["gmlp_b_s"][j] = jnp.sum(dbs.reshape(GC, GG, GD), axis=-1).T
            buf = mm_tn_into(buf, hn1, dpre, D, lay["in"], True)
            key = "main"
        token = swap(i, key, buf)
        if i == 1:
            small_gmlp = small_exchange(_SMALL_GMLP, token[0, 0], "gmlp")
            token = token + small_gmlp[3]
    last = swapped(dh, jnp.zeros((1, 1), F32))
    grad_x = dh.reshape(x.shape)
    small_rest = small_exchange(_SMALL_REST, last[0, 0], "rest")

    after = small_rest[3]
    shares = []
    for i, key, own, sems, sums, land in scatters:
        _, got = scatter_wait(sems, sums, land, after, f"{i}{key}")
        sems, src, full, after = share_start(final_sum(place, own, got), f"{i}{key}")
        shares.append((i, key, sems, src, full))
    where = {n: [None] * weights[n].shape[0] for n in _BIG}
    for i, key, sems, src, full in shares:
        _, after = share_wait(sems, src, full, after, f"{i}{key}")
        for n, l, row0 in dict((k, parts) for k, _, parts in _layer_units(i))[key]:
            where[n][l] = (after, row0)
            if weights[n].shape[-1] != D:
                gw[n][l] = after[row0:row0 + weights[n][l].size // D].reshape(weights[n].shape[1:])
    grads = {n: jnp.stack(gw[n]) for n in _BIG if weights[n].shape[-1] != D}

    tot = []
    for names, (sems, src, land, _), tag in ((_SMALL_REST, small_rest, "rest"), (_SMALL_GMLP, small_gmlp, "gmlp")):
        summed = sum_devices(small_wait(sems, src, land, after, tag)[1]).reshape(-1)
        tot.append(summed[:sum(small_size(n) for n in names)])
    tot = jnp.concatenate(tot)
    off = 0
    for n, sz in ((n, small_size(n)) for n in _SMALL_REST + _SMALL_GMLP):
        gsum = tot[off:off + sz]
        off += sz
        if n in ("gmlp_ln_g", "gmlp_ln_b"):
            gsum = lax.dynamic_slice_in_dim(gsum.reshape(-1, GH), chip * (GH // N_CHIPS), GH // N_CHIPS, axis=1)
        grads[n] = gsum.reshape(weights[n].shape)

    delta, new_m, new_v = {}, {}, {}
    for n in _BIG:
        if weights[n].shape[-1] == D:
            grads[n], delta[n], new_m[n], new_v[n] = adamw_layers(
                weights[n], args["m_" + n], args["v_" + n], [b for b, _ in where[n]], [r for _, r in where[n]])
            continue
        w2 = weights[n].reshape(-1, weights[n].shape[-1])
        d, mn, vn = adamw(w2, grads[n].reshape(w2.shape), args["m_" + n].reshape(w2.shape),
                          args["v_" + n].reshape(w2.shape))
        delta[n], new_m[n], new_v[n] = (a.reshape(weights[n].shape) for a in (d, mn, vn))
    own_sizes = [weights[n].size for n in _SMALL]
    own_rows = -(-sum(own_sizes) // (8 * D)) * 8
    packed = [_pack_rows([src[n] for n in _SMALL], F32, pad_to=own_rows)
              for src in (weights, grads, {n: args["m_" + n] for n in _SMALL}, {n: args["v_" + n] for n in _SMALL})]
    outs = adamw(*packed)
    off = 0
    for n, sz in zip(_SMALL, own_sizes):
        for dst, o in zip((delta, new_m, new_v), outs):
            dst[n] = o.reshape(-1)[off:off + sz].reshape(weights[n].shape)
        off += sz

    order = ["norm_mix", "norm_ffn", "norm_ple", "mla_w_down", "mla_q_lora_g", "mla_kv_lora_g", "mla_w_uq",
             "mla_w_ukv", "mla_q_nope_g", "mla_q_rope_g", "mla_k_nope_g", "mla_k_rope_g", "mla_w_out", "gmlp_w_in",
             "gmlp_ln_g", "gmlp_ln_b", "gmlp_w_s", "gmlp_b_s", "gmlp_w_out", "ffn_w_up", "ffn_w_down", "ple_w_gate",
             "ple_w_proj"]
    return (loss, grad_x, *[grads[n] for n in order], *[delta[n] for n in order], *[new_m[n] for n in order],
            *[new_v[n] for n in order])
```

```python
import functools

import jax
import jax.numpy as jnp
from jax import lax
from jax.experimental import pallas as pl
from jax.experimental.pallas import tpu as pltpu

F32 = jnp.float32
BF16 = jnp.bfloat16
MESH = pl.DeviceIdType.MESH

D = 1024
HEADS = 8
DN = 128
DR = 64
QL = 384
KVL = 256
LAT = 704
LATP = 768
DFF = 4096
GH = 2048
GC = 128
GG = 8
GD = 256
PLE = 256
EPS = 1e-6
ROPE_BASE = 10000.0
SM_SCALE = (DN + DR) ** -0.5
N_CHIPS = 4
LANES = 128

ADAM_LR = 0.001
ADAM_B1 = 0.9
ADAM_B2 = 0.999
ADAM_EPS = 1e-08
ADAM_WD = 0.01
ADAM_STEP = 10

TM = 256
TMB = 512
TQ = 512
TQ_FWD = 512
FWD_HEADS = 2
BWD_HEADS = 2
SUM_ROWS = 256
VMEM_LIMIT = 56 * 1024 * 1024


def _cp(*sem):
    return pltpu.CompilerParams(dimension_semantics=sem, vmem_limit_bytes=VMEM_LIMIT)


def _dot(a, b):
    return jnp.dot(a, b, preferred_element_type=F32)


def _dot_nt(a, b):
    return lax.dot_general(a, b, (((1,), (1,)), ((), ())), preferred_element_type=F32)


def _dot_tn(a, b):
    return lax.dot_general(a, b, (((0,), (0,)), ((), ())), preferred_element_type=F32)


def _rms(x, g, n):
    r = lax.rsqrt(jnp.sum(x * x, axis=-1, keepdims=True) * (1.0 / n) + EPS)
    xhat = x * r
    return xhat * g, xhat, r


def _rms_bwd(dy, g, xhat, r, n):
    dxhat = dy * g
    return r * (dxhat - xhat * (jnp.sum(dxhat * xhat, axis=-1, keepdims=True) * (1.0 / n)))


def _rope(x, c, s):
    return x * c + (pltpu.roll(x, 32, 1) - pltpu.roll(x, 96, 1)) * s


def _rope_t(dy, c, s):
    w = dy * s
    return dy * c + pltpu.roll(w, 96, 1) - pltpu.roll(w, 32, 1)


def _sigmoid(x):
    return 1.0 / (1.0 + jnp.exp(-x))


_GELU_K = 0.7978845608028654
_GELU_C = 0.044715


def _gelu(x):
    return 0.5 * x * (1.0 + jnp.tanh(_GELU_K * (x + _GELU_C * x * x * x)))


def _gelu_and_grad(x):
    x2 = x * x
    t = jnp.tanh(_GELU_K * (x + _GELU_C * x2 * x))
    half = 0.5 * (1.0 + t)
    return x * half, half + 0.5 * x * (1.0 - t * t) * (_GELU_K * (1.0 + 3.0 * _GELU_C * x2))


def _acc_rows(ref, val):
    ref[...] += jnp.broadcast_to(jnp.sum(val, axis=0, keepdims=True), ref.shape)


def _row(tm, c):
    return pl.BlockSpec((tm, c), lambda i: (i, 0))


def _const(shape):
    nd = len(shape)
    return pl.BlockSpec(shape, lambda i: (0,) * nd, pipeline_mode=pl.Buffered(1))


def _wblk(rows, row0):
    assert row0 % rows == 0, (rows, row0)
    return pl.BlockSpec((N_CHIPS, rows, D), lambda i: (0, row0 // rows, 0), pipeline_mode=pl.Buffered(1))


def _rows_joined(w_ref):
    return w_ref[...].reshape(N_CHIPS * w_ref.shape[1], D)


def _sds(shape, dtype):
    return jax.ShapeDtypeStruct(shape, dtype)


def mixffn_fwd(h, y, allw, lay, g2):
    t, k = y.shape

    def body(h_ref, y_ref, wo_ref, g_ref, wu_ref, wd_ref, h1_ref, h2_ref, hn_ref, r_ref):
        h1 = h_ref[...] + _dot(y_ref[...], _rows_joined(wo_ref))
        h1_ref[...] = h1
        yn, _, _ = _rms(h1, g_ref[...], D)
        hn = yn.astype(BF16)
        hn_ref[...] = hn
        f = jnp.zeros((TMB, D), F32)
        for c in range(N_CHIPS):
            r = jnp.maximum(_dot(hn, wu_ref[c]), 0.0)
            r_ref[:, c * D:(c + 1) * D] = r.astype(BF16)
            f = f + _dot((r * r).astype(BF16), wd_ref[c])
        h2_ref[...] = h1 + f

    return pl.pallas_call(
        body, name="mixffn_fwd", grid=(t // TMB,),
        in_specs=[_row(TMB, D), _row(TMB, k), _wblk(k // N_CHIPS, lay["out"]), _const((1, D)), _wblk(D, lay["up"]),
                  _wblk(D, lay["down"])],
        out_specs=[_row(TMB, D), _row(TMB, D), _row(TMB, D), _row(TMB, DFF)],
        out_shape=[_sds((t, D), F32), _sds((t, D), F32), _sds((t, D), BF16), _sds((t, DFF), BF16)],
        compiler_params=_cp("parallel"),
    )(h, y, allw, g2, allw, allw)


def _layer_rows(tm, c, layer):
    return pl.BlockSpec((None, tm, c), lambda i: (layer, i, 0))


def ple_fwd(h2, p, layer, g3, allw, lay, wp):
    t = h2.shape[0]

    def body(h_ref, p_ref, g_ref, wg_ref, wp_ref, h3_ref, hn_ref):
        x = h_ref[...]
        yn, _, _ = _rms(x, g_ref[...], D)
        hn = yn.astype(BF16)
        hn_ref[...] = hn
        gt = _dot(hn, _rows_joined(wg_ref))
        pp = _dot(p_ref[...].astype(BF16), wp_ref[...])
        h3_ref[...] = x + _sigmoid(gt) * pp

    return pl.pallas_call(
        body, name="ple_fwd", grid=(t // TMB,),
        in_specs=[_row(TMB, D), _layer_rows(TMB, PLE, layer), _const((1, D)), _wblk(D // N_CHIPS, lay["gate"]),
                  _const((PLE, D))],
        out_specs=[_row(TMB, D), _row(TMB, D)],
        out_shape=[_sds((t, D), F32), _sds((t, D), BF16)],
        compiler_params=_cp("parallel"),
    )(h2, p, g3, allw, wp)


def _mla_project(h_ref, g1_ref, wdn_ref, gq_ref, gkv_ref, wuq_ref, wukv_ref):
    x = h_ref[...]
    yn, xhat, rx = _rms(x, g1_ref[...], D)
    hn = yn.astype(BF16)
    lat = _dot(hn, wdn_ref[...])
    cq, cqhat, rq = _rms(lat[:, :QL], gq_ref[...], QL)
    ckv, ckvhat, rkv = _rms(lat[:, QL:QL + KVL], gkv_ref[...], KVL)
    kr_raw = lat[:, QL + KVL:]
    cqb = cq.astype(BF16)
    ckvb = ckv.astype(BF16)
    qp = _dot(cqb, wuq_ref[...])
    kvp = _dot(ckvb, wukv_ref[...])
    return dict(xhat=xhat, rx=rx, hn=hn, cqhat=cqhat, rq=rq, ckvhat=ckvhat, rkv=rkv, kr_raw=kr_raw,
                cqb=cqb, ckvb=ckvb, qp=qp, kvp=kvp)


def mla_pre_fwd(h, g1, wdn, gq, gkv, wuq, wukv, gqn, gqr, gkn, gkr, cos, sin):
    t = h.shape[0]

    def body(h_ref, g1_ref, wdn_ref, gq_ref, gkv_ref, wuq_ref, wukv_ref, gqn_ref, gqr_ref, gkn_ref, gkr_ref,
             c_ref, s_ref, q_ref, k_ref, v_ref):
        m = _mla_project(h_ref, g1_ref, wdn_ref, gq_ref, gkv_ref, wuq_ref, wukv_ref)
        c = c_ref[...]
        s = s_ref[...]
        kr, _, _ = _rms(m["kr_raw"], gkr_ref[...], DR)
        krb = _rope(kr, c, s).astype(BF16)
        for hd in range(HEADS):
            qn, _, _ = _rms(m["qp"][:, hd * DN:(hd + 1) * DN], gqn_ref[...], DN)
            qr, _, _ = _rms(m["qp"][:, D + hd * LANES:D + (hd + 1) * LANES], gqr_ref[...], DR)
            q_ref[hd, :, 0:DN] = (qn * SM_SCALE).astype(BF16)
            q_ref[hd, :, DN:2 * DN] = (_rope(qr, c, s) * SM_SCALE).astype(BF16)
            kn, _, _ = _rms(m["kvp"][:, hd * 2 * DN:hd * 2 * DN + DN], gkn_ref[...], DN)
            k_ref[hd, :, 0:DN] = kn.astype(BF16)
            k_ref[hd, :, DN:2 * DN] = krb
            v_ref[hd] = m["kvp"][:, hd * 2 * DN + DN:(hd + 1) * 2 * DN].astype(BF16)

    hb = lambda w: pl.BlockSpec((HEADS, TM, w), lambda i: (0, i, 0))
    return pl.pallas_call(
        body, name="mla_pre_fwd", grid=(t // TM,),
        in_specs=[_row(TM, D), _const((1, D)), _const((D, LATP)), _const((1, QL)), _const((1, KVL)),
                  _const((QL, 2 * D)), _const((KVL, 2 * D)), _const((1, LANES)), _const((1, LANES)),
                  _const((1, LANES)), _const((1, LANES)), _row(TM, LANES), _row(TM, LANES)],
        out_specs=[hb(2 * DN), hb(2 * DN), hb(DN)],
        out_shape=[_sds((HEADS, t, 2 * DN), BF16), _sds((HEADS, t, 2 * DN), BF16), _sds((HEADS, t, DN), BF16)],
        compiler_params=_cp("parallel"),
    )(h, g1, wdn, gq, gkv, wuq, wukv, gqn, gqr, gkn, gkr, cos, sin)


def _diagonal_mask(n=TQ):
    return lax.broadcasted_iota(jnp.int32, (n, n), 1) <= lax.broadcasted_iota(jnp.int32, (n, n), 0)


def flash_fwd(q, k, v, seq):
    t = q.shape[1]
    nb = t // seq
    tq = TQ_FWD
    nq = seq // tq
    hp = FWD_HEADS

    def body(q_ref, k_ref, v_ref, o_ref, lse_ref):
        qi = pl.program_id(2)
        qs = [q_ref[a] for a in range(hp)]

        def step(j, carry, diagonal=False):
            rows = pl.ds(pl.multiple_of(j * tq, tq), tq)
            out = []
            for a in range(hp):
                m, l, acc = carry[a]
                s = _dot_nt(qs[a], k_ref[a, rows, :])
                if diagonal:
                    s = jnp.where(_diagonal_mask(tq), s, -1e30)
                m_new = jnp.maximum(m, jnp.max(s, axis=-1, keepdims=True))
                p = jnp.exp(s - m_new)
                alpha = jnp.exp(m - m_new)
                l = alpha * l + jnp.sum(p, axis=-1, keepdims=True)
                acc = alpha * acc + _dot(p.astype(BF16), v_ref[a, rows, :])
                out.append((m_new, l, acc))
            return tuple(out)

        one = (jnp.full((tq, 1), -1e30, F32), jnp.zeros((tq, 1), F32), jnp.zeros((tq, DN), F32))
        done = step(qi, lax.fori_loop(0, qi, step, (one,) * hp), diagonal=True)
        for a, (m, l, acc) in enumerate(done):
            o_ref[:, a * DN:(a + 1) * DN] = (acc / l).astype(BF16)
            lse_ref[a] = m + jnp.log(l)

    return pl.pallas_call(
        body, name="flash_fwd", grid=(nb, HEADS // hp, nq),
        in_specs=[pl.BlockSpec((hp, tq, 2 * DN), lambda b, h, i: (h, b * nq + i, 0)),
                  pl.BlockSpec((hp, seq, 2 * DN), lambda b, h, i: (h, b, 0)),
                  pl.BlockSpec((hp, seq, DN), lambda b, h, i: (h, b, 0))],
        out_specs=[pl.BlockSpec((tq, hp * DN), lambda b, h, i: (b * nq + i, h)),
                   pl.BlockSpec((hp, tq, 1), lambda b, h, i: (h, b * nq + i, 0))],
        out_shape=[_sds((t, HEADS * DN), BF16), _sds((HEADS, t, 1), F32)],
        compiler_params=_cp("parallel", "parallel", "arbitrary"),
    )(q, k, v)


def _gmlp_in(hn, win_ref):
    pre = [_dot(hn, win_ref[c]) for c in range(N_CHIPS)]
    return jnp.concatenate(pre[:2], axis=1), jnp.concatenate(pre[2:], axis=1)


def gmlp_fwd(h, g1, allw, lay, lng, lnb, wm, bfull):
    t = h.shape[0]

    def body(h_ref, g1_ref, win_ref, lng_ref, lnb_ref, wm_ref, b_ref, y_ref, pre_ref):
        yn, _, _ = _rms(h_ref[...], g1_ref[...], D)
        pre_u, pre_v = _gmlp_in(yn.astype(BF16), win_ref)
        pre_ref[:, :GH] = pre_u.astype(BF16)
        pre_ref[:, GH:] = pre_v.astype(BF16)
        u = _gelu(pre_u)
        v = _gelu(pre_v)
        xc = v - jnp.mean(v, axis=-1, keepdims=True)
        rs = lax.rsqrt(jnp.mean(xc * xc, axis=-1, keepdims=True) + EPS)
        vnb = (xc * rs * lng_ref[...] + lnb_ref[...]).astype(BF16)
        for ch in range(TM // GC):
            rows = slice(ch * GC, (ch + 1) * GC)
            for g in range(GG):
                cols = slice(g * GD, (g + 1) * GD)
                sv = _dot(wm_ref[g], vnb[rows, cols]) + b_ref[:, cols]
                y_ref[rows, cols] = (u[rows, cols] * sv).astype(BF16)

    return pl.pallas_call(
        body, name="gmlp_fwd", grid=(t // TM,),
        in_specs=[_row(TM, D), _const((1, D)), _wblk(D, lay["in"]), _const((1, GH)), _const((1, GH)),
                  _const((GG, GC, GC)), _const((GC, GH))],
        out_specs=[_row(TM, GH), _row(TM, 2 * GH)],
        out_shape=[_sds((t, GH), BF16), _sds((t, 2 * GH), BF16)],
        compiler_params=_cp("parallel"),
    )(h, g1, allw, lng, lnb, wm, bfull)


def loss_head(h, tgt):
    t = h.shape[0]

    def body(h_ref, t_ref, dh_ref, loss_ref):
        @pl.when(pl.program_id(0) == 0)
        def _():
            loss_ref[...] = jnp.zeros_like(loss_ref)

        e = h_ref[...] - t_ref[...]
        dh_ref[...] = e * (1.0 / D)
        part = jnp.sum(jnp.sum(e * e, axis=-1, keepdims=True), axis=0, keepdims=True) * (0.5 / D)
        loss_ref[...] += jnp.broadcast_to(part, loss_ref.shape)

    return pl.pallas_call(
        body, name="loss_head", grid=(t // TMB,),
        in_specs=[_row(TMB, D), _row(TMB, D)],
        out_specs=[_row(TMB, D), _const((8, LANES))],
        out_shape=[_sds((t, D), F32), _sds((8, LANES), F32)],
        compiler_params=_cp("arbitrary"),
    )(h, tgt)


def _zero_at_first_step(*refs):
    @pl.when(pl.program_id(0) == 0)
    def _():
        for r in refs:
            r[...] = jnp.zeros_like(r)


def ple_bwd(dh3, h2, p, layer, g3, allw, lay, wp):
    t = h2.shape[0]

    def body(dh_ref, h_ref, p_ref, g_ref, wg_ref, wp_ref, dh2_ref, dh2b_ref, dgt_ref, dpp_ref, dg_ref):
        _zero_at_first_step(dg_ref)
        dh3v = dh_ref[...]
        x = h_ref[...]
        g = g_ref[...]
        wg = _rows_joined(wg_ref)
        yn, xhat, r = _rms(x, g, D)
        gt = _dot(yn.astype(BF16), wg)
        pp = _dot(p_ref[...].astype(BF16), wp_ref[...])
        sg = _sigmoid(gt)
        dgt = (dh3v * pp * sg * (1.0 - sg)).astype(BF16)
        dgt_ref[...] = dgt
        dpp_ref[...] = (dh3v * sg).astype(BF16)
        dhn = _dot_nt(dgt, wg)
        _acc_rows(dg_ref, dhn * xhat)
        dh2 = dh3v + _rms_bwd(dhn, g, xhat, r, D)
        dh2_ref[...] = dh2
        dh2b_ref[...] = dh2.astype(BF16)

    return pl.pallas_call(
        body, name="ple_bwd", grid=(t // TMB,),
        in_specs=[_row(TMB, D), _row(TMB, D), _layer_rows(TMB, PLE, layer), _const((1, D)),
                  _wblk(D // N_CHIPS, lay["gate"]),
                  _const((PLE, D))],
        out_specs=[_row(TMB, D), _row(TMB, D), _row(TMB, D), _row(TMB, D), _const((8, D))],
        out_shape=[_sds((t, D), F32), _sds((t, D), BF16), _sds((t, D), BF16), _sds((t, D), BF16), _sds((8, D), F32)],
        compiler_params=_cp("arbitrary"),
    )(dh3, h2, p, g3, allw, wp)


def ffn_bwd(dh2, dh2b, h1, r, g2, allw, lay):
    t = h1.shape[0]

    def body(dh_ref, dhb_ref, h_ref, r_ref, g_ref, wu_ref, wd_ref, dh1_ref, dh1b_ref, du_ref, a_ref, dg_ref):
        _zero_at_first_step(dg_ref)
        dhb = dhb_ref[...]
        g = g_ref[...]
        _, xhat, rr = _rms(h_ref[...], g, D)
        dhn = jnp.zeros((TM, D), F32)
        for c in range(N_CHIPS):
            cs = slice(c * D, (c + 1) * D)
            rc = r_ref[:, cs].astype(F32)
            a_ref[:, cs] = (rc * rc).astype(BF16)
            da = _dot_nt(dhb, wd_ref[c])
            du = (da * (2.0 * rc)).astype(BF16)
            du_ref[:, cs] = du
            dhn = dhn + _dot_nt(du, wu_ref[c])
        _acc_rows(dg_ref, dhn * xhat)
        dh1 = dh_ref[...] + _rms_bwd(dhn, g, xhat, rr, D)
        dh1_ref[...] = dh1
        dh1b_ref[...] = dh1.astype(BF16)

    return pl.pallas_call(
        body, name="ffn_bwd", grid=(t // TM,),
        in_specs=[_row(TM, D), _row(TM, D), _row(TM, D), _row(TM, DFF), _const((1, D)), _wblk(D, lay["up"]),
                  _wblk(D, lay["down"])],
        out_specs=[_row(TM, D), _row(TM, D), _row(TM, DFF), _row(TM, DFF), _const((8, D))],
        out_shape=[_sds((t, D), F32), _sds((t, D), BF16), _sds((t, DFF), BF16), _sds((t, DFF), BF16),
                   _sds((8, D), F32)],
        compiler_params=_cp("arbitrary"),
    )(dh2, dh2b, h1, r, g2, allw, allw)


def linear_nt(a, allw, rows, row0):
    t = a.shape[0]
    k = N_CHIPS * rows

    def body(a_ref, w_ref, o_ref):
        o_ref[...] = _dot_nt(a_ref[...], _rows_joined(w_ref)).astype(BF16)

    return pl.pallas_call(
        body, name="linear_nt", grid=(t // TMB,),
        in_specs=[_row(TMB, D), _wblk(rows, row0)],
        out_specs=_row(TMB, k),
        out_shape=_sds((t, k), BF16),
        compiler_params=_cp("parallel"),
    )(a, allw)


def flash_bwd(q, k, v, o, do, lse, seq, after):
    t = q.shape[1]
    nb = t // seq
    nq = seq // TQ
    hp = BWD_HEADS

    def body(q_ref, k_ref, v_ref, o_ref, do_ref, lse_ref, after_ref, dq_ref, dk_ref, dv_ref):
        del after_ref
        kj = pl.program_id(2)

        @pl.when(kj == 0)
        def _():
            dq_ref[...] = jnp.zeros_like(dq_ref)

        def step(i, carry, diagonal=False):
            rows = pl.ds(pl.multiple_of(i * TQ, TQ), TQ)
            out = []
            for a in range(hp):
                dk, dv = carry[a]
                kv = k_ref[a]
                qv = q_ref[a, rows, :]
                dov = do_ref[rows, a * DN:(a + 1) * DN]
                ov = o_ref[rows, a * DN:(a + 1) * DN]
                delta = jnp.sum(dov.astype(F32) * ov.astype(F32), axis=-1, keepdims=True)
                s = _dot_nt(qv, kv)
                if diagonal:
                    s = jnp.where(_diagonal_mask(), s, -1e30)
                p = jnp.exp(s - lse_ref[a, rows, :])
                dp = _dot_nt(dov, v_ref[a])
                ds = (p * (dp - delta)).astype(BF16)
                dv = dv + _dot_tn(p.astype(BF16), dov)
                dk = dk + _dot_tn(ds, qv)
                dq_ref[a, rows, :] += _dot(ds, kv)
                out.append((dk, dv))
            return tuple(out)

        one = (jnp.zeros((TQ, 2 * DN), F32), jnp.zeros((TQ, DN), F32))
        done = lax.fori_loop(kj + 1, nq, step, step(kj, (one,) * hp, diagonal=True))
        for a, (dk, dv) in enumerate(done):
            dk_ref[a] = dk
            dv_ref[a] = dv

    return pl.pallas_call(
        body, name="flash_bwd", grid=(nb, HEADS // hp, nq),
        in_specs=[pl.BlockSpec((hp, seq, 2 * DN), lambda b, h, j: (h, b, 0)),
                  pl.BlockSpec((hp, TQ, 2 * DN), lambda b, h, j: (h, b * nq + j, 0)),
                  pl.BlockSpec((hp, TQ, DN), lambda b, h, j: (h, b * nq + j, 0)),
                  pl.BlockSpec((seq, hp * DN), lambda b, h, j: (b, h)),
                  pl.BlockSpec((seq, hp * DN), lambda b, h, j: (b, h)),
                  pl.BlockSpec((hp, seq, 1), lambda b, h, j: (h, b, 0)), _ANY],
        out_specs=[pl.BlockSpec((hp, seq, 2 * DN), lambda b, h, j: (h, b, 0)),
                   pl.BlockSpec((hp, TQ, 2 * DN), lambda b, h, j: (h, b * nq + j, 0)),
                   pl.BlockSpec((hp, TQ, DN), lambda b, h, j: (h, b * nq + j, 0))],
        out_shape=[_sds((HEADS, t, 2 * DN), F32), _sds((HEADS, t, 2 * DN), F32), _sds((HEADS, t, DN), F32)],
        compiler_params=_cp("parallel", "parallel", "arbitrary"),
    )(q, k, v, o, do, lse, after)


def mla_pre_bwd(dq, dk, dv, dh1, h, g1, wdn, gq, gkv, wuq, wukv, gqn, gqr, gkn, gkr, cos, sin):
    t = h.shape[0]

    def body(dq_ref, dk_ref, dv_ref, dh1_ref, h_ref, g1_ref, wdn_ref, gq_ref, gkv_ref, wuq_ref, wukv_ref,
             gqn_ref, gqr_ref, gkn_ref, gkr_ref, c_ref, s_ref,
             dh_ref, hn_ref, cq_ref, ckv_ref, dqp_ref, dkvp_ref, dlat_ref,
             dg1_ref, dgq_ref, dgkv_ref, dgqn_ref, dgqr_ref, dgkn_ref, dgkr_ref):
        _zero_at_first_step(dg1_ref, dgq_ref, dgkv_ref, dgqn_ref, dgqr_ref, dgkn_ref, dgkr_ref)
        m = _mla_project(h_ref, g1_ref, wdn_ref, gq_ref, gkv_ref, wuq_ref, wukv_ref)
        hn_ref[...] = m["hn"]
        cq_ref[...] = m["cqb"]
        ckv_ref[...] = m["ckvb"]
        c = c_ref[...]
        s = s_ref[...]
        gqn = gqn_ref[...]
        gqr = gqr_ref[...]
        gkn = gkn_ref[...]
        gkr = gkr_ref[...]

        dkr = dk_ref[0, :, DN:2 * DN]
        for hd in range(1, HEADS):
            dkr = dkr + dk_ref[hd, :, DN:2 * DN]
        dkr = _rope_t(dkr, c, s)
        _, krhat, rkr = _rms(m["kr_raw"], gkr, DR)
        _acc_rows(dgkr_ref, dkr * krhat)
        dkr_raw = _rms_bwd(dkr, gkr, krhat, rkr, DR)

        for hd in range(HEADS):
            ncols = slice(hd * DN, (hd + 1) * DN)
            _, xh, r = _rms(m["qp"][:, ncols], gqn, DN)
            dqn = dq_ref[hd, :, 0:DN] * SM_SCALE
            _acc_rows(dgqn_ref, dqn * xh)
            dqp_ref[:, ncols] = _rms_bwd(dqn, gqn, xh, r, DN).astype(BF16)

            rcols = slice(D + hd * LANES, D + (hd + 1) * LANES)
            _, xh, r = _rms(m["qp"][:, rcols], gqr, DR)
            dqr = _rope_t(dq_ref[hd, :, DN:2 * DN] * SM_SCALE, c, s)
            _acc_rows(dgqr_ref, dqr * xh)
            dqp_ref[:, rcols] = _rms_bwd(dqr, gqr, xh, r, DR).astype(BF16)

            kcols = slice(hd * 2 * DN, hd * 2 * DN + DN)
            _, xh, r = _rms(m["kvp"][:, kcols], gkn, DN)
            dkn = dk_ref[hd, :, 0:DN]
            _acc_rows(dgkn_ref, dkn * xh)
            dkvp_ref[:, kcols] = _rms_bwd(dkn, gkn, xh, r, DN).astype(BF16)
            dkvp_ref[:, hd * 2 * DN + DN:(hd + 1) * 2 * DN] = dv_ref[hd].astype(BF16)

        dcq = _dot_nt(dqp_ref[...], wuq_ref[...])
        _acc_rows(dgq_ref, dcq * m["cqhat"])
        dlat_q = _rms_bwd(dcq, gq_ref[...], m["cqhat"], m["rq"], QL)
        dckv = _dot_nt(dkvp_ref[...], wukv_ref[...])
        _acc_rows(dgkv_ref, dckv * m["ckvhat"])
        dlat_kv = _rms_bwd(dckv, gkv_ref[...], m["ckvhat"], m["rkv"], KVL)
        dlat = jnp.concatenate([dlat_q, dlat_kv, dkr_raw], axis=1).astype(BF16)
        dlat_ref[...] = dlat
        dhn = _dot_nt(dlat, wdn_ref[...])
        _acc_rows(dg1_ref, dhn * m["xhat"])
        dh_ref[...] = dh1_ref[...] + _rms_bwd(dhn, g1_ref[...], m["xhat"], m["rx"], D)

    hb = lambda w: pl.BlockSpec((HEADS, TM, w), lambda i: (0, i, 0))
    return pl.pallas_call(
        body, name="mla_pre_bwd", grid=(t // TM,),
        in_specs=[hb(2 * DN), hb(2 * DN), hb(DN), _row(TM, D), _row(TM, D), _const((1, D)), _const((D, LATP)),
                  _const((1, QL)), _const((1, KVL)), _const((QL, 2 * D)), _const((KVL, 2 * D)),
                  _const((1, LANES)), _const((1, LANES)), _const((1, LANES)), _const((1, LANES)),
                  _row(TM, LANES), _row(TM, LANES)],
        out_specs=[_row(TM, D), _row(TM, D), _row(TM, QL), _row(TM, KVL), _row(TM, 2 * D), _row(TM, 2 * D),
                   _row(TM, LATP), _const((8, D)), _const((8, QL)), _const((8, KVL)), _const((8, LANES)),
                   _const((8, LANES)), _const((8, LANES)), _const((8, LANES))],
        out_shape=[_sds((t, D), F32), _sds((t, D), BF16), _sds((t, QL), BF16), _sds((t, KVL), BF16),
                   _sds((t, 2 * D), BF16), _sds((t, 2 * D), BF16), _sds((t, LATP), BF16),
                   _sds((8, D), F32), _sds((8, QL), F32), _sds((8, KVL), F32), _sds((8, LANES), F32),
                   _sds((8, LANES), F32), _sds((8, LANES), F32), _sds((8, LANES), F32)],
        compiler_params=_cp("arbitrary"),
    )(dq, dk, dv, dh1, h, g1, wdn, gq, gkv, wuq, wukv, gqn, gqr, gkn, gkr, cos, sin)


def gmlp_bwd(dh1, dh1b, h, pre, g1, allw, lay, lng, lnb, wm, wmt, bfull, tril):
    t = h.shape[0]

    def body(dh1_ref, dh1b_ref, h_ref, pre_ref, g1_ref, win_ref, lng_ref, lnb_ref, wm_ref, wmt_ref, b_ref,
             wout_ref, tril_ref, dh_ref, hn_ref, dpre_ref, dws_ref, dbs_ref, dlng_ref, dlnb_ref, dg1_ref,
             dvn_s):
        _zero_at_first_step(dws_ref, dbs_ref, dlng_ref, dlnb_ref, dg1_ref)
        g1 = g1_ref[...]
        yn, xhat, rx = _rms(h_ref[...], g1, D)
        hn_ref[...] = yn.astype(BF16)
        dy = _dot_nt(dh1b_ref[...], _rows_joined(wout_ref))
        pre_u = pre_ref[:, :GH].astype(F32)
        pre_v = pre_ref[:, GH:].astype(F32)
        u, gg_u = _gelu_and_grad(pre_u)
        v, gg_v = _gelu_and_grad(pre_v)
        xc = v - jnp.mean(v, axis=-1, keepdims=True)
        rs = lax.rsqrt(jnp.mean(xc * xc, axis=-1, keepdims=True) + EPS)
        vhat = xc * rs
        lng = lng_ref[...]
        vnb = (vhat * lng + lnb_ref[...]).astype(BF16)
        dsv = dy * u
        dsvb = dsv.astype(BF16)
        tril_m = tril_ref[...]
        for ch in range(TM // GC):
            rows = slice(ch * GC, (ch + 1) * GC)
            dbs_ref[...] += dsv[rows, :]
            for g in range(GG):
                cols = slice(g * GD, (g + 1) * GD)
                sv = _dot(wm_ref[g], vnb[rows, cols]) + b_ref[:, cols]
                dpre_ref[rows, cols] = (dy[rows, cols] * sv * gg_u[rows, cols]).astype(BF16)
                dvn_s[rows, cols] = _dot(wmt_ref[g], dsvb[rows, cols])
                dws_ref[g] += _dot_nt(dsvb[rows, cols], vnb[rows, cols]) * tril_m
        dvn = dvn_s[...]
        _acc_rows(dlng_ref, dvn * vhat)
        _acc_rows(dlnb_ref, dvn)
        dvhat = dvn * lng
        dv = rs * (dvhat - jnp.mean(dvhat, axis=-1, keepdims=True)
                   - vhat * jnp.mean(dvhat * vhat, axis=-1, keepdims=True))
        dpre_v = (dv * gg_v).astype(BF16)
        dpre_ref[:, GH:] = dpre_v
        dhn = _dot_nt(dpre_ref[:, 0:D], win_ref[0])
        for c in range(1, N_CHIPS):
            dhn = dhn + _dot_nt(dpre_ref[:, c * D:(c + 1) * D], win_ref[c])
        _acc_rows(dg1_ref, dhn * xhat)
        dh_ref[...] = dh1_ref[...] + _rms_bwd(dhn, g1, xhat, rx, D)

    return pl.pallas_call(
        body, name="gmlp_bwd", grid=(t // TM,),
        in_specs=[_row(TM, D), _row(TM, D), _row(TM, D), _row(TM, 2 * GH), _const((1, D)), _wblk(D, lay["in"]),
                  _const((1, GH)), _const((1, GH)), _const((GG, GC, GC)), _const((GG, GC, GC)), _const((GC, GH)),
                  _wblk(GH // N_CHIPS, lay["out"]), _const((GC, GC))],
        out_specs=[_row(TM, D), _row(TM, D), _row(TM, 2 * GH), _const((GG, GC, GC)), _const((GC, GH)),
                   _const((8, GH)), _const((8, GH)), _const((8, D))],
        out_shape=[_sds((t, D), F32), _sds((t, D), BF16), _sds((t, 2 * GH), BF16), _sds((GG, GC, GC), F32),
                   _sds((GC, GH), F32), _sds((8, GH), F32), _sds((8, GH), F32), _sds((8, D), F32)],
        scratch_shapes=[pltpu.VMEM((TM, GH), F32)],
        compiler_params=_cp("arbitrary"),
    )(dh1, dh1b, h, pre, g1, allw, lng, lnb, wm, wmt, bfull, allw, tril)


def _token_step(t):
    return next(s for s in (2048, 1024, 512) if t % s == 0)


def mm_tn(a, b, layer=None):
    t, k = a.shape[-2:]
    n = b.shape[1]
    tk = min(k, 1024)
    tn = min(n, 1024)
    tt = _token_step(t)
    a_spec = (pl.BlockSpec((tt, tk), lambda i, j, s: (s, i)) if layer is None else
              pl.BlockSpec((None, tt, tk), lambda i, j, s: (layer, s, i)))

    def body(a_ref, b_ref, o_ref):
        @pl.when(pl.program_id(2) == 0)
        def _():
            o_ref[...] = jnp.zeros_like(o_ref)

        o_ref[...] += _dot_tn(a_ref[...].astype(BF16), b_ref[...].astype(BF16))

    return pl.pallas_call(
        body, name="mm_tn", grid=(k // tk, n // tn, t // tt),
        in_specs=[a_spec, pl.BlockSpec((tt, tn), lambda i, j, s: (s, j))],
        out_specs=pl.BlockSpec((tk, tn), lambda i, j, s: (i, j)), out_shape=_sds((k, n), F32),
        compiler_params=_cp("parallel", "parallel", "arbitrary"),
    )(a, b)


def mm_tn_into(buf, a, b, rows, row0, col_sharded):
    t = a.shape[0]
    tt = _token_step(t)
    assert row0 % rows == 0 and a.shape[1] == (rows if col_sharded else N_CHIPS * rows), (rows, row0, a.shape)
    assert b.shape[1] == (N_CHIPS * D if col_sharded else D), b.shape
    grid = (1, N_CHIPS, t // tt) if col_sharded else (N_CHIPS, 1, t // tt)
    fresh = isinstance(buf, int)

    def body(*refs):
        a_ref, b_ref, o_ref = refs[-3:]

        @pl.when(pl.program_id(2) == 0)
        def _():
            o_ref[...] = jnp.zeros_like(o_ref)

        o_ref[...] += _dot_tn(a_ref[...].astype(BF16), b_ref[...].astype(BF16))

    specs = [pl.BlockSpec((tt, rows), lambda i, j, s: (s, i)), pl.BlockSpec((tt, D), lambda i, j, s: (s, j))]
    return pl.pallas_call(
        body, name="mm_tn_into", grid=grid,
        in_specs=specs if fresh else [_ANY] + specs,
        out_specs=pl.BlockSpec((None, rows, D), lambda i, j, s: (i + j, row0 // rows, 0)),
        out_shape=_sds((N_CHIPS, buf, D) if fresh else buf.shape, F32),
        input_output_aliases={} if fresh else {0: 0},
        compiler_params=_cp("parallel", "parallel", "arbitrary"),
    )(*((a, b) if fresh else (buf, a, b)))


def adamw(w, g, m, v):
    rows, cols = w.shape
    tr = rows if rows <= 512 else next(r for r in (512, 384, 256, 128) if rows % r == 0)
    c1 = 1.0 - ADAM_B1 ** ADAM_STEP
    c2 = 1.0 - ADAM_B2 ** ADAM_STEP

    def body(w_ref, g_ref, m_ref, v_ref, d_ref, mo_ref, vo_ref):
        gv = g_ref[...]
        mn = ADAM_B1 * m_ref[...] + (1.0 - ADAM_B1) * gv
        vn = ADAM_B2 * v_ref[...] + (1.0 - ADAM_B2) * (gv * gv)
        mo_ref[...] = mn
        vo_ref[...] = vn
        d_ref[...] = -ADAM_LR * ((mn / c1) / (jnp.sqrt(vn / c2) + ADAM_EPS) + ADAM_WD * w_ref[...])

    spec = pl.BlockSpec((tr, cols), lambda i: (i, 0))
    return pl.pallas_call(
        body, name="adamw", grid=(rows // tr,),
        in_specs=[spec] * 4, out_specs=[spec] * 3, out_shape=[_sds((rows, cols), F32)] * 3,
        compiler_params=_cp("parallel"),
    )(w, g, m, v)


def adamw_layers(w, m, v, bufs, row0s):
    nl, a, _ = w.shape
    tr = min(a, 256)
    c1 = 1.0 - ADAM_B1 ** ADAM_STEP
    c2 = 1.0 - ADAM_B2 ** ADAM_STEP
    assert all(r % tr == 0 for r in row0s) and a % tr == 0, (row0s, a)

    def body(w_ref, m_ref, v_ref, *rest):
        g_refs, (g_ref, d_ref, mo_ref, vo_ref) = rest[:nl], rest[nl:]
        for l in range(nl):
            @pl.when(pl.program_id(0) == l)
            def _(l=l):
                gv = g_refs[l][...]
                g_ref[...] = gv
                mn = ADAM_B1 * m_ref[...] + (1.0 - ADAM_B1) * gv
                vn = ADAM_B2 * v_ref[...] + (1.0 - ADAM_B2) * (gv * gv)
                mo_ref[...] = mn
                vo_ref[...] = vn
                d_ref[...] = -ADAM_LR * ((mn / c1) / (jnp.sqrt(vn / c2) + ADAM_EPS) + ADAM_WD * w_ref[...])

    def rows_of(l, row0):
        return pl.BlockSpec((tr, D), lambda li, i: (jnp.where(li == l, row0 // tr + i, row0 // tr), 0))

    spec = pl.BlockSpec((None, tr, D), lambda li, i: (li, i, 0))
    return pl.pallas_call(
        body, name="adamw_layers", grid=(nl, a // tr),
        in_specs=[spec] * 3 + [rows_of(l, r) for l, r in enumerate(row0s)],
        out_specs=[spec] * 4, out_shape=[_sds(w.shape, F32)] * 4,
        compiler_params=_cp("arbitrary", "arbitrary"),
    )(w, m, v, *bufs)


def _place():
    return lax.axis_index("x"), lax.axis_index("y"), lax.axis_index("c")


def _other_chips(x, y):
    return [(1 - x, y), (x, 1 - y), (1 - x, 1 - y)]


_ANY = pl.BlockSpec(memory_space=pl.ANY)


_HBM = pl.BlockSpec(memory_space=pltpu.HBM)
_SEM = pl.BlockSpec(memory_space=pltpu.SEMAPHORE)
_EFFECT = pltpu.SideEffectType.DATAFLOW_SIDE_EFFECTING
N_ICI = 3


def _exchange_start(name, src, land, copies, n):
    def body(src_ref, land_ref, *outs):
        sems, token = outs[:2 * n], outs[-1]
        for j, (s, d, to) in enumerate(copies(src_ref, land_ref, _place())):
            pltpu.make_async_remote_copy(src_ref=s, dst_ref=d, send_sem=sems[j], recv_sem=sems[n + j],
                                         device_id=to, device_id_type=MESH).start()
        token[...] = jnp.zeros_like(token)

    sem = pltpu.SemaphoreType.DMA(())
    outs = pl.pallas_call(
        body, name=name,
        out_shape=(sem,) * (2 * n) + (pltpu.HBM(src.shape, src.dtype), pltpu.HBM(land.shape, land.dtype),
                                      _sds((8, LANES), F32)),
        in_specs=(_HBM, _HBM),
        out_specs=(_SEM,) * (2 * n) + (_HBM, _HBM, pl.BlockSpec(memory_space=pltpu.VMEM)),
        input_output_aliases={0: 2 * n, 1: 2 * n + 1},
        compiler_params=pltpu.CompilerParams(has_side_effects=_EFFECT),
    )(pltpu.with_memory_space_constraint(src, pltpu.HBM), pltpu.with_memory_space_constraint(land, pltpu.HBM))
    return outs[:2 * n], outs[2 * n], outs[2 * n + 1], outs[-1]


def _exchange_wait(name, sems, src, land, after, arrivals):
    n = len(sems) // 2

    def body(src_ref, land_ref, *rest):
        sems = rest[:2 * n]
        for j, (s, d) in enumerate(arrivals(src_ref, land_ref, _place())):
            cp = pltpu.make_async_remote_copy(src_ref=s, dst_ref=d, send_sem=sems[j], recv_sem=sems[n + j],
                                              device_id=_place(), device_id_type=MESH)
            cp.wait_send()
            cp.wait_recv()

    return pl.pallas_call(
        body, name=name, out_shape=(pltpu.HBM(src.shape, src.dtype), pltpu.HBM(land.shape, land.dtype)),
        in_specs=(_HBM, _HBM) + (_SEM,) * (2 * n) + (_ANY,), out_specs=(_HBM, _HBM),
        input_output_aliases={0: 0, 1: 1},
        compiler_params=pltpu.CompilerParams(has_side_effects=_EFFECT),
    )(src, land, *sems, after)


def _halves(c, hh):
    return pl.ds(pl.multiple_of(c * hh, 16), hh), pl.ds(pl.multiple_of((1 - c) * hh, 16), hh)


def gather_start(land, tag):
    _, rr, _ = land.shape
    assert rr % 32 == 0, rr

    def copies(_, land_ref, place):
        x, y, c = place
        mine = land_ref.at[2 * x + y, _halves(c, rr // 2)[0]]
        return [(mine, mine, (cx, cy, c)) for cx, cy in _other_chips(x, y)]

    return _exchange_start(f"gather_start_{tag}", jnp.zeros((8, LANES), F32), land, copies, N_ICI)


def gather_wait(sems, src, land, after, tag):
    def arrivals(_, land_ref, place):
        x, y, c = place
        half = _halves(c, land.shape[1] // 2)[0]
        return [(land_ref.at[2 * x + y, half], land_ref.at[2 * cx + cy, half]) for cx, cy in _other_chips(x, y)]

    return _exchange_wait(f"gather_wait_{tag}", sems, src, land, after, arrivals)


def pass_start(land, tag):
    def copies(_, land_ref, place):
        x, y, c = place
        half = _halves(c, land.shape[1] // 2)[0]
        return [(land_ref.at[2 * cx + cy, half], land_ref.at[2 * cx + cy, half], (x, y, 1 - c))
                for cx, cy in _other_chips(x, y)]

    return _exchange_start(f"pass_start_{tag}", jnp.zeros((8, LANES), F32), land, copies, N_ICI)


def pass_wait(sems, src, land, after, tag):
    def arrivals(_, land_ref, place):
        x, y, c = place
        mine, other = _halves(c, land.shape[1] // 2)
        return [(land_ref.at[2 * cx + cy, mine], land_ref.at[2 * cx + cy, other]) for cx, cy in _other_chips(x, y)]

    return _exchange_wait(f"pass_wait_{tag}", sems, src, land, after, arrivals)


def swap_start(g, tag):
    _, rr, cc = g.shape

    def copies(g_ref, got_ref, place):
        x, y, c = place
        other = _halves(c, rr // 2)[1]
        return [(g_ref.at[k, other], got_ref.at[k], (x, y, 1 - c)) for k in range(N_CHIPS)]

    return _exchange_start(f"swap_start_{tag}", g, lax.empty((N_CHIPS, rr // 2, cc), g.dtype), copies, N_CHIPS)


def swap_wait(sems, g, got, after, tag):
    def arrivals(g_ref, got_ref, place):
        other = _halves(place[2], g.shape[1] // 2)[1]
        return [(g_ref.at[k, other], got_ref.at[k]) for k in range(N_CHIPS)]

    return _exchange_wait(f"swap_wait_{tag}", sems, g, got, after, arrivals)


def chip_sum(place, g32, got):
    _, rr, cc = g32.shape
    hh = rr // 2
    tr = SUM_ROWS
    assert rr % 2 == 0 and hh % tr == 0, (rr, tr)
    nb = hh // tr

    def body(place_ref, g_ref, got_ref, own_ref, all_ref):
        s = g_ref[...] + got_ref[...].astype(F32)
        all_ref[...] = s.astype(BF16)
        own_ref[...] = g_ref[place_ref[1]] + got_ref[place_ref[1]].astype(F32)

    return pl.pallas_call(
        body, name="chip_sum",
        grid_spec=pltpu.PrefetchScalarGridSpec(
            num_scalar_prefetch=1, grid=(nb,),
            in_specs=[pl.BlockSpec((N_CHIPS, tr, cc), lambda i, pr: (0, pr[0] * nb + i, 0)),
                      pl.BlockSpec((N_CHIPS, tr, cc), lambda i, pr: (0, i, 0))],
            out_specs=[pl.BlockSpec((tr, cc), lambda i, pr: (i, 0)),
                       pl.BlockSpec((N_CHIPS, tr, cc), lambda i, pr: (0, i, 0))]),
        out_shape=[_sds((hh, cc), F32), _sds((N_CHIPS, hh, cc), BF16)],
        compiler_params=_cp("parallel"),
    )(place, g32, got)


def _scatter_copies(s_ref, land_ref, place):
    x, y, c = place
    return [(s_ref.at[2 * cx + cy], land_ref.at[j], (cx, cy, c)) for j, (cx, cy) in enumerate(_other_chips(x, y))]


def scatter_start(s, tag):
    return _exchange_start(f"scatter_start_{tag}", s, lax.empty((N_ICI,) + s.shape[1:], s.dtype), _scatter_copies, N_ICI)


def scatter_wait(sems, s, land, after, tag):
    return _exchange_wait(f"scatter_wait_{tag}", sems, s, land, after,
                          lambda s_ref, land_ref, place: [(a, b) for a, b, _ in _scatter_copies(s_ref, land_ref, place)])


def final_sum(place, own, got):
    hh, cc = own.shape
    tr = SUM_ROWS
    assert hh % tr == 0, (hh, tr)
    nb = hh // tr

    def body(place_ref, own_ref, got_ref, o_ref):
        del place_ref
        o_ref[...] = ((own_ref[...] + got_ref[0].astype(F32)) + got_ref[1].astype(F32)) + got_ref[2].astype(F32)

    return pl.pallas_call(
        body, name="final_sum",
        grid_spec=pltpu.PrefetchScalarGridSpec(
            num_scalar_prefetch=1, grid=(nb,),
            in_specs=[pl.BlockSpec((tr, cc), lambda i, pr: (i, 0)), pl.BlockSpec((3, tr, cc), lambda i, pr: (0, i, 0))],
            out_specs=pl.BlockSpec((tr, cc), lambda i, pr: (pr[0] * nb + i, 0))),
        out_shape=_sds((2 * hh, cc), F32),
        compiler_params=_cp("parallel"),
    )(place, own, got)


def share_start(f, tag):
    def copies(_, f_ref, place):
        x, y, c = place
        mine = f_ref.at[_halves(c, f.shape[0] // 2)[0]]
        return [(mine, mine, (x, y, 1 - c))]

    return _exchange_start(f"share_start_{tag}", jnp.zeros((8, LANES), F32), f, copies, 1)


def share_wait(sems, src, f, after, tag):
    def arrivals(_, f_ref, place):
        mine, other = _halves(place[2], f.shape[0] // 2)
        return [(f_ref.at[mine], f_ref.at[other])]

    return _exchange_wait(f"share_wait_{tag}", sems, src, f, after, arrivals)


N_DEV = 8


def _peers(place):
    x, y, c = place
    return [((1 - x) if r & 4 else x, (1 - y) if r & 2 else y, (1 - c) if r & 1 else c) for r in range(1, N_DEV)]


def _device_index(place):
    x, y, c = place
    return 4 * x + 2 * y + c


def small_start(land, tag):
    def copies(_, land_ref, place):
        mine = land_ref.at[_device_index(place)]
        return [(mine, mine, to) for to in _peers(place)]

    return _exchange_start(f"small_start_{tag}", jnp.zeros((8, LANES), F32), land, copies, N_DEV - 1)


def small_wait(sems, src, land, after, tag):
    def arrivals(_, land_ref, place):
        return [(land_ref.at[_device_index(place)], land_ref.at[_device_index(peer)]) for peer in _peers(place)]

    return _exchange_wait(f"small_wait_{tag}", sems, src, land, after, arrivals)


def sum_devices(land):
    _, rr, cc = land.shape
    tr = 56
    assert rr % tr == 0, rr

    def body(l_ref, o_ref):
        acc = l_ref[0]
        for d in range(1, N_DEV):
            acc = acc + l_ref[d]
        o_ref[...] = acc

    return pl.pallas_call(
        body, name="sum_devices", grid=(rr // tr,),
        in_specs=[pl.BlockSpec((N_DEV, tr, cc), lambda i: (0, i, 0))],
        out_specs=pl.BlockSpec((tr, cc), lambda i: (i, 0)), out_shape=_sds((rr, cc), F32),
        compiler_params=_cp("parallel"),
    )(land)


_BIG = ["mla_w_down", "mla_w_uq", "mla_w_ukv", "mla_w_out", "gmlp_w_in", "gmlp_w_out", "ffn_w_up", "ffn_w_down",
        "ple_w_gate", "ple_w_proj"]
_SMALL_REST = ["norm_mix", "norm_ffn", "norm_ple", "mla_q_lora_g", "mla_kv_lora_g", "mla_q_nope_g", "mla_q_rope_g",
               "mla_k_nope_g", "mla_k_rope_g"]
_SMALL_GMLP = ["gmlp_ln_g", "gmlp_ln_b", "gmlp_w_s", "gmlp_b_s"]
_SMALL = _SMALL_REST + _SMALL_GMLP

_LAY_MLA = dict(up=0, down=1024, out=2048, gate=2304, wdn=2560, wuq=2736, wukv=2880, proj=3008, rows=3072)
_LAY_MLA_MAIN = dict(up=0, down=1024, out=2048, gate=2304, rows=2560)
_LAY_MLA_ODD = dict(wdn=0, wuq=176, wukv=320, proj=448, rows=512)
_LAY_GMLP = {"up": 0, "down": 1024, "in": 2048, "out": 3072, "gate": 3584, "proj": 3840, "ln": 3904, "rows": 4096}
SPLIT_LAYERS = (0,)


def _layer_units(i):
    j = i // 2
    if i % 2 == 0:
        odd, lay = (_LAY_MLA_ODD, _LAY_MLA_MAIN) if i in SPLIT_LAYERS else (_LAY_MLA, _LAY_MLA)
        small = [("mla_w_down", j, odd["wdn"]), ("mla_w_uq", j, odd["wuq"]), ("mla_w_ukv", j, odd["wukv"]),
                 ("ple_w_proj", i, odd["proj"])]
        large = [("ffn_w_up", i, lay["up"]), ("ffn_w_down", i, lay["down"]), ("mla_w_out", j, lay["out"]),
                 ("ple_w_gate", i, lay["gate"])]
        return [("odd", odd, small), ("main", lay, large)] if i in SPLIT_LAYERS else [("main", lay, large + small)]
    lay = _LAY_GMLP
    return [("main", lay, [("ffn_w_up", i, lay["up"]), ("ffn_w_down", i, lay["down"]), ("gmlp_w_in", j, lay["in"]),
                           ("gmlp_w_out", j, lay["out"]), ("ple_w_gate", i, lay["gate"]),
                           ("ple_w_proj", i, lay["proj"])])]


def _pack_rows(parts, dtype, pad_to=None, slot=False):
    size = sum(p.size for p in parts)
    tail = [] if pad_to is None or pad_to * D == size else [jnp.zeros((pad_to * D - size,), dtype)]
    shape = (1, -1, D) if slot else (-1, D)
    if all(p.size % D == 0 for p in parts + tail):
        return jnp.concatenate([p.astype(dtype).reshape(shape) for p in parts + tail], axis=len(shape) - 2)
    return jnp.concatenate([p.astype(dtype).reshape(-1) for p in parts + tail]).reshape(shape)


def _odd(allw, row0, a, b):
    return allw[:, row0:row0 + a * b // D].reshape(N_CHIPS, a, b)


def _cols_joined(s):
    return jnp.transpose(s, (1, 0, 2)).reshape(s.shape[1], N_CHIPS * s.shape[2])


def _col_shards(full):
    a, bb = full.shape
    return jnp.transpose(full.reshape(a, N_CHIPS, bb // N_CHIPS), (1, 0, 2)).reshape(N_CHIPS, -1, D)


def _pad_lanes(g):
    return jnp.pad(g, ((0, 0), (0, LANES - g.shape[1])))


def _split_uq(wuq):
    l = wuq.shape[0]
    w = wuq.reshape(l, QL, HEADS, DN + DR)
    nope = w[..., :DN].reshape(l, QL, HEADS * DN)
    rope = jnp.pad(w[..., DN:], ((0, 0), (0, 0), (0, 0), (0, LANES - DR))).reshape(l, QL, HEADS * LANES)
    return jnp.concatenate([nope, rope], axis=-1)


def _merge_uq(d):
    nope = d[:, :HEADS * DN].reshape(QL, HEADS, DN)
    rope = d[:, HEADS * DN:].reshape(QL, HEADS, LANES)[..., :DR]
    return jnp.concatenate([nope, rope], axis=-1).reshape(QL, HEADS * (DN + DR))


def _rope_tables(positions):
    inv_freq = ROPE_BASE ** (-(jnp.arange(0, DR, 2, dtype=F32) / DR))
    ang = positions.reshape(-1).astype(F32)[:, None] * inv_freq
    z = jnp.zeros((ang.shape[0], LANES - DR), F32)
    return (jnp.concatenate([jnp.cos(ang), jnp.cos(ang), z], axis=1),
            jnp.concatenate([jnp.sin(ang), jnp.sin(ang), z], axis=1))


def kernel(x, p, positions, norm_mix, norm_ffn, norm_ple, mla_w_down, mla_q_lora_g, mla_kv_lora_g, mla_w_uq, mla_w_ukv, mla_q_nope_g, mla_q_rope_g, mla_k_nope_g, mla_k_rope_g, mla_w_out, gmlp_w_in, gmlp_ln_g, gmlp_ln_b, gmlp_w_s, gmlp_b_s, gmlp_w_out, ffn_w_up, ffn_w_down, ple_w_gate, ple_w_proj, loss_target, m_norm_mix, m_norm_ffn, m_norm_ple, m_mla_w_down, m_mla_q_lora_g, m_mla_kv_lora_g, m_mla_w_uq, m_mla_w_ukv, m_mla_q_nope_g, m_mla_q_rope_g, m_mla_k_nope_g, m_mla_k_rope_g, m_mla_w_out, m_gmlp_w_in, m_gmlp_ln_g, m_gmlp_ln_b, m_gmlp_w_s, m_gmlp_b_s, m_gmlp_w_out, m_ffn_w_up, m_ffn_w_down, m_ple_w_gate, m_ple_w_proj, v_norm_mix, v_norm_ffn, v_norm_ple, v_mla_w_down, v_mla_q_lora_g, v_mla_kv_lora_g, v_mla_w_uq, v_mla_w_ukv, v_mla_q_nope_g, v_mla_q_rope_g, v_mla_k_nope_g, v_mla_k_rope_g, v_mla_w_out, v_gmlp_w_in, v_gmlp_ln_g, v_gmlp_ln_b, v_gmlp_w_s, v_gmlp_b_s, v_gmlp_w_out, v_ffn_w_up, v_ffn_w_down, v_ple_w_gate, v_ple_w_proj):
    args = dict(locals())
    weights = {n: args[n] for n in _BIG + _SMALL}
    depth = norm_mix.shape[0]
    nb, seq, _ = x.shape
    t = nb * seq
    assert seq % TQ == 0 and seq % TM == 0 and t % 512 == 0, (nb, seq)
    cx = lax.axis_index("x")
    cy = lax.axis_index("y")
    cc = lax.axis_index("c")
    chip = 2 * cx + cy

    gathers = {}
    token = None
    for i in range(depth):
        for key, lay, parts in _layer_units(i):
            rows = [weights[n][l] for n, l, _ in parts]
            if token is not None:
                rows[0] = rows[0] + token[0, 0]
            if "ln" in lay:
                ln = jnp.stack([gmlp_ln_g[i // 2], gmlp_ln_b[i // 2]]).astype(F32)
                bits = lax.bitcast_convert_type(ln, BF16).reshape(-1)
                rows.append(jnp.pad(bits, (0, 16 * D - bits.size)).reshape(16, D))
            mine = _pack_rows(rows, BF16, pad_to=lay["rows"], slot=True)
            land = lax.dynamic_update_slice(lax.empty((N_CHIPS, lay["rows"], D), BF16), mine, (chip, 0, 0))
            sems, src, land, token = gather_start(land, f"{i}{key}")
            gathers[i, key] = (sems, src, land)
    allw = [None] * depth

    tril = jnp.tril(jnp.ones((GC, GC), F32))
    wm = (gmlp_w_s * tril).astype(BF16)
    wmt = jnp.swapaxes(wm, -1, -2)
    bfull = jnp.repeat(jnp.swapaxes(gmlp_b_s, -1, -2), GD, axis=-1)
    cos, sin = _rope_tables(positions)
    row = lambda g: g.reshape(1, -1)
    gqr = _pad_lanes(mla_q_rope_g)
    gkr = _pad_lanes(mla_k_rope_g)

    h = x.reshape(t, D)
    pt = p.reshape(depth, t, PLE)
    saved = []

    passing = {}

    def arrive(i, key, after):
        sems, src, land = gathers[i, key]
        _, land = gather_wait(sems, src, land, after, f"{i}{key}")
        passing[i, key] = pass_start(land, f"{i}{key}")
        return passing[i, key][3]

    def needed(i, key, after=None):
        sems, src, land, tok = passing.pop((i, key))
        return pass_wait(sems, src, land, tok if after is None else after, f"{i}{key}")[1]

    arrive(0, _layer_units(0)[0][0], token)
    for i in range(depth):
        j = i // 2
        lay = _layer_units(i)[-1][1]
        s = dict(h=h)
        if i % 2 == 0:
            split = i in SPLIT_LAYERS
            olay = _layer_units(i)[0][1]
            odd = needed(i, "odd" if split else "main", None if i == 0 else h)
            wdn = jnp.pad(_odd(odd, olay["wdn"], D // N_CHIPS, LAT).reshape(D, LAT), ((0, 0), (0, LATP - LAT)))
            wuq = _split_uq(_cols_joined(_odd(odd, olay["wuq"], QL, 384))[None])[0]
            wukv = _cols_joined(_odd(odd, olay["wukv"], KVL, 512))
            wp = _cols_joined(_odd(odd, olay["proj"], PLE, 256))
            mla_args = (row(norm_mix[i]), wdn, row(mla_q_lora_g[j]), row(mla_kv_lora_g[j]), wuq, wukv,
                        row(mla_q_nope_g[j]), gqr[j:j + 1], row(mla_k_nope_g[j]), gkr[j:j + 1], cos, sin)
            q, k, v = mla_pre_fwd(h, *mla_args)
            y, lse = flash_fwd(q, k, v, seq)
            if split and i == 0:
                arrive(i, "main", y)
            aw = needed(i, "main", y) if split else odd
            s.update(q=q, k=k, v=v, lse=lse, mla_args=mla_args)
        else:
            aw = needed(i, "main", h)
            ln = lax.bitcast_convert_type(aw[:, lay["ln"]:lay["ln"] + 2].reshape(N_CHIPS, 2, GH // N_CHIPS, 2), F32)
            ln = jnp.transpose(ln, (1, 0, 2)).reshape(2, 1, GH)
            wp = _cols_joined(_odd(aw, lay["proj"], PLE, 256))
            y, pre = gmlp_fwd(h, row(norm_mix[i]), aw, lay, ln[0], ln[1], wm[j], bfull[j])
            s.update(pre=pre, ln=ln)
        allw[i] = aw
        g2 = row(norm_ffn[i])
        if i + 1 < depth:
            for key, _, _ in _layer_units(i + 1):
                g2 = g2 + arrive(i + 1, key, y)[0:1, 0:1]
        h1, h2, hn2, r = mixffn_fwd(h, y, aw, lay, g2)
        h, hn3 = ple_fwd(h2, pt, i, row(norm_ple[i]), aw, lay, wp)
        s.update(y=y, wp=wp, h1=h1, h2=h2, hn2=hn2, r=r, hn3=hn3)
        saved.append(s)

    dh, loss_part = loss_head(h, loss_target.reshape(t, D))
    loss = lax.psum(loss_part[0, 0], ("x", "y", "c"))

    gs = {n: [None] * weights[n].shape[0] for n in _SMALL}
    gw = {n: [None] * weights[n].shape[0] for n in _BIG}
    place = jnp.stack([cc, chip]).astype(jnp.int32)
    scatters = []
    swaps = []
    token = None

    def put(b, row0, shards):
        return lax.dynamic_update_slice(b, shards.reshape(N_CHIPS, -1, D), (0, row0, 0))

    def small_size(n):
        return weights[n].shape[0] * GH if n in ("gmlp_ln_g", "gmlp_ln_b") else weights[n].size

    def small_exchange(names, zero, tag):
        rows = -(-sum(small_size(n) for n in names) // (56 * D)) * 56
        part = [jnp.stack(gs[n]) for n in names]
        part = _pack_rows([part[0] + zero] + part[1:], F32, pad_to=rows, slot=True)
        land = lax.dynamic_update_slice(lax.empty((N_DEV, rows, D), F32), part, (2 * chip + cc, 0, 0))
        return small_start(land, tag)

    def swap(i, key, buf):
        sems, buf, got, tok = swap_start(buf, f"{i}{key}")
        swaps.append((i, key, sems, buf, got))
        return tok

    def swapped(after, zero):
        while swaps:
            i, key, sems, g, got = swaps.pop(0)
            g, got = swap_wait(sems, g, got, after, f"{i}{key}")
            own, sums = chip_sum(place, g, got)
            sems, sums, land, tok = scatter_start(sums, f"{i}{key}")
            scatters.append((i, key, own, sems, sums, land))
            zero = zero + tok[0:1, 0:1]
        return zero

    for i in reversed(range(depth)):
        j = i // 2
        lay = _layer_units(i)[-1][1]
        aw = allw[i]
        s = saved[i]

        g3 = row(norm_ple[i])
        if token is not None:
            g3 = g3 + token[0:1, 0:1]
        dh2, dh2b, dgt, dpp, dg3 = ple_bwd(dh, s["h2"], pt, i, g3, aw, lay, s["wp"])
        gs["norm_ple"][i] = dg3[0]
        buf = mm_tn_into(lay["rows"], s["hn3"], dgt, D // N_CHIPS, lay["gate"], False)
        dproj = _col_shards(mm_tn(pt, dpp, layer=i))
        if "ln" in lay:
            buf = put(buf, lay["ln"], jnp.zeros((N_CHIPS, lay["rows"] - lay["ln"], D), F32))
            buf = put(buf, lay["proj"], dproj)
        dh1, dh1b, du, a, dg2 = ffn_bwd(dh2, dh2b, s["h1"], s["r"], row(norm_ffn[i]), aw, lay)
        gs["norm_ffn"][i] = dg2[0]
        buf = mm_tn_into(buf, a, dh2b, D, lay["down"], False)
        buf = mm_tn_into(buf, s["hn2"], du, D, lay["up"], True)
        buf = mm_tn_into(buf, s["y"], dh1b, s["y"].shape[1] // N_CHIPS, lay["out"], False)
        g1 = swapped(dh1, row(norm_mix[i]))
        if i % 2 == 0:
            split = i in SPLIT_LAYERS
            do = linear_nt(dh1b, aw, D // N_CHIPS, lay["out"])
            dq, dk, dv = flash_bwd(s["q"], s["k"], s["v"], s["y"], do, s["lse"], seq,
                                   after=swap(i, "main", buf) if split else dh1b)
            g1 = swapped(dq, g1)
            (dh, hn1, cq, ckv, dqp, dkvp, dlat, dg1, dgq, dgkv, dgqn, dgqr, dgkn, dgkr) = mla_pre_bwd(
                dq, dk, dv, dh1, s["h"], g1, *s["mla_args"][1:])
            gs["norm_mix"][i] = dg1[0]
            gs["mla_q_lora_g"][j] = dgq[0]
            gs["mla_kv_lora_g"][j] = dgkv[0]
            gs["mla_q_nope_g"][j] = dgqn[0]
            gs["mla_q_rope_g"][j] = dgqr[0, :DR]
            gs["mla_k_nope_g"][j] = dgkn[0]
            gs["mla_k_rope_g"][j] = dgkr[0, :DR]
            small = [mm_tn(hn1, dlat)[:, :LAT].reshape(N_CHIPS, -1, D), _col_shards(_merge_uq(mm_tn(cq, dqp))),
                     _col_shards(mm_tn(ckv, dkvp)), dproj]
            if split:
                buf = jnp.concatenate(small, axis=1)
            else:
                buf = put(buf, lay["wdn"], jnp.concatenate(small, axis=1))
            key = "odd" if split else "main"
        else:
            dh, hn1, dpre, dws, dbs, dlng, dlnb, dg1 = gmlp_bwd(
                dh1, dh1b, s["h"], s["pre"], g1, aw, lay, s["ln"][0], s["ln"][1], wm[j], wmt[j], bfull[j], tril)
            gs["norm_mix"][i] = dg1[0]
            gs["gmlp_ln_g"][j] = dlng[0]
            gs["gmlp_ln_b"][j] = dlnb[0]
            gs["gmlp_w_s"][j] = dws
            gs["gmlp_b_s"][j] = jnp.sum(dbs.reshape(GC, GG, GD), axis=-1).T
            buf = mm_tn_into(buf, hn1, dpre, D, lay["in"], True)
            key = "main"
        token = swap(i, key, buf)
        if i == 1:
            small_gmlp = small_exchange(_SMALL_GMLP, token[0, 0], "gmlp")
            token = token + small_gmlp[3]
    last = swapped(dh, jnp.zeros((1, 1), F32))
    grad_x = dh.reshape(x.shape)
    small_rest = small_exchange(_SMALL_REST, last[0, 0], "rest")

    after = small_rest[3]
    shares = []
    for i, key, own, sems, sums, land in scatters:
        _, got = scatter_wait(sems, sums, land, after, f"{i}{key}")
        sems, src, full, after = share_start(final_sum(place, own, got), f"{i}{key}")
        shares.append((i, key, sems, src, full))
    where = {n: [None] * weights[n].shape[0] for n in _BIG}
    for i, key, sems, src, full in shares:
        _, after = share_wait(sems, src, full, after, f"{i}{key}")
        for n, l, row0 in dict((k, parts) for k, _, parts in _layer_units(i))[key]:
            where[n][l] = (after, row0)
            if weights[n].shape[-1] != D:
                gw[n][l] = after[row0:row0 + weights[n][l].size // D].reshape(weights[n].shape[1:])
    grads = {n: jnp.stack(gw[n]) for n in _BIG if weights[n].shape[-1] != D}

    tot = []
    for names, (sems, src, land, _), tag in ((_SMALL_REST, small_rest, "rest"), (_SMALL_GMLP, small_gmlp, "gmlp")):
        summed = sum_devices(small_wait(sems, src, land, after, tag)[1]).reshape(-1)
        tot.append(summed[:sum(small_size(n) for n in names)])
    tot = jnp.concatenate(tot)
    off = 0
    for n, sz in ((n, small_size(n)) for n in _SMALL_REST + _SMALL_GMLP):
        gsum = tot[off:off + sz]
        off += sz
        if n in ("gmlp_ln_g", "gmlp_ln_b"):
            gsum = lax.dynamic_slice_in_dim(gsum.reshape(-1, GH), chip * (GH // N_CHIPS), GH // N_CHIPS, axis=1)
        grads[n] = gsum.reshape(weights[n].shape)

    delta, new_m, new_v = {}, {}, {}
    for n in _BIG:
        if weights[n].shape[-1] == D:
            grads[n], delta[n], new_m[n], new_v[n] = adamw_layers(
                weights[n], args["m_" + n], args["v_" + n], [b for b, _ in where[n]], [r for _, r in where[n]])
            continue
        w2 = weights[n].reshape(-1, weights[n].shape[-1])
        d, mn, vn = adamw(w2, grads[n].reshape(w2.shape), args["m_" + n].reshape(w2.shape),
                          args["v_" + n].reshape(w2.shape))
        delta[n], new_m[n], new_v[n] = (a.reshape(weights[n].shape) for a in (d, mn, vn))
    own_sizes = [weights[n].size for n in _SMALL]
    own_rows = -(-sum(own_sizes) // (8 * D)) * 8
    packed = [_pack_rows([src[n] for n in _SMALL], F32, pad_to=own_rows)
              for src in (weights, grads, {n: args["m_" + n] for n in _SMALL}, {n: args["v_" + n] for n in _SMALL})]
    outs = adamw(*packed)
    off = 0
    for n, sz in zip(_SMALL, own_sizes):
        for dst, o in zip((delta, new_m, new_v), outs):
            dst[n] = o.reshape(-1)[off:off + sz].reshape(weights[n].shape)
        off += sz

    order = ["norm_mix", "norm_ffn", "norm_ple", "mla_w_down", "mla_q_lora_g", "mla_kv_lora_g", "mla_w_uq",
             "mla_w_ukv", "mla_q_nope_g", "mla_q_rope_g", "mla_k_nope_g", "mla_k_rope_g", "mla_w_out", "gmlp_w_in",
             "gmlp_ln_g", "gmlp_ln_b", "gmlp_w_s", "gmlp_b_s", "gmlp_w_out", "ffn_w_up", "ffn_w_down", "ple_w_gate",
             "ple_w_proj"]
    return (loss, grad_x, *[grads[n] for n in order], *[delta[n] for n in order], *[new_m[n] for n in order],
            *[new_v[n] for n in order])
```

```python
import functools

import jax
import jax.numpy as jnp
from jax import lax
from jax.experimental import pallas as pl
from jax.experimental.pallas import tpu as pltpu

F32 = jnp.float32
BF16 = jnp.bfloat16
MESH = pl.DeviceIdType.MESH

D = 1024
HEADS = 8
DN = 128
DR = 64
QL = 384
KVL = 256
LAT = 704
LATP = 768
DFF = 4096
GH = 2048
GC = 128
GG = 8
GD = 256
PLE = 256
EPS = 1e-6
ROPE_BASE = 10000.0
SM_SCALE = (DN + DR) ** -0.5
N_CHIPS = 4
LANES = 128

ADAM_LR = 0.001
ADAM_B1 = 0.9
ADAM_B2 = 0.999
ADAM_EPS = 1e-08
ADAM_WD = 0.01
ADAM_STEP = 10

TM = 256
TMB = 512
TQ = 512
TQ_FWD = 512
FWD_HEADS = 2
BWD_HEADS = 2
SUM_ROWS = 256
VMEM_LIMIT = 56 * 1024 * 1024


def _cp(*sem):
    return pltpu.CompilerParams(dimension_semantics=sem, vmem_limit_bytes=VMEM_LIMIT)


def _dot(a, b):
    return jnp.dot(a, b, preferred_element_type=F32)


def _dot_nt(a, b):
    return lax.dot_general(a, b, (((1,), (1,)), ((), ())), preferred_element_type=F32)


def _dot_tn(a, b):
    return lax.dot_general(a, b, (((0,), (0,)), ((), ())), preferred_element_type=F32)


def _rms(x, g, n):
    r = lax.rsqrt(jnp.sum(x * x, axis=-1, keepdims=True) * (1.0 / n) + EPS)
    xhat = x * r
    return xhat * g, xhat, r


def _rms_bwd(dy, g, xhat, r, n):
    dxhat = dy * g
    return r * (dxhat - xhat * (jnp.sum(dxhat * xhat, axis=-1, keepdims=True) * (1.0 / n)))


def _rope(x, c, s):
    return x * c + (pltpu.roll(x, 32, 1) - pltpu.roll(x, 96, 1)) * s


def _rope_t(dy, c, s):
    w = dy * s
    return dy * c + pltpu.roll(w, 96, 1) - pltpu.roll(w, 32, 1)


def _sigmoid(x):
    return 1.0 / (1.0 + jnp.exp(-x))


_GELU_K = 0.7978845608028654
_GELU_C = 0.044715


def _gelu(x):
    return 0.5 * x * (1.0 + jnp.tanh(_GELU_K * (x + _GELU_C * x * x * x)))


def _gelu_and_grad(x):
    x2 = x * x
    t = jnp.tanh(_GELU_K * (x + _GELU_C * x2 * x))
    half = 0.5 * (1.0 + t)
    return x * half, half + 0.5 * x * (1.0 - t * t) * (_GELU_K * (1.0 + 3.0 * _GELU_C * x2))


def _acc_rows(ref, val):
    ref[...] += jnp.broadcast_to(jnp.sum(val, axis=0, keepdims=True), ref.shape)


def _row(tm, c):
    return pl.BlockSpec((tm, c), lambda i: (i, 0))


def _const(shape):
    nd = len(shape)
    return pl.BlockSpec(shape, lambda i: (0,) * nd, pipeline_mode=pl.Buffered(1))


def _wblk(rows, row0):
    assert row0 % rows == 0, (rows, row0)
    return pl.BlockSpec((N_CHIPS, rows, D), lambda i: (0, row0 // rows, 0), pipeline_mode=pl.Buffered(1))


def _rows_joined(w_ref):
    return w_ref[...].reshape(N_CHIPS * w_ref.shape[1], D)


def _sds(shape, dtype):
    return jax.ShapeDtypeStruct(shape, dtype)


def mixffn_fwd(h, y, allw, lay, g2):
    t, k = y.shape

    def body(h_ref, y_ref, wo_ref, g_ref, wu_ref, wd_ref, h1_ref, h2_ref, hn_ref, a_ref):
        h1 = h_ref[...] + _dot(y_ref[...], _rows_joined(wo_ref))
        h1_ref[...] = h1
        yn, _, _ = _rms(h1, g_ref[...], D)
        hn = yn.astype(BF16)
        hn_ref[...] = hn
        f = jnp.zeros((TMB, D), F32)
        for c in range(N_CHIPS):
            r = jnp.maximum(_dot(hn, wu_ref[c]), 0.0)
            a = (r * r).astype(BF16)
            a_ref[:, c * D:(c + 1) * D] = a
            f = f + _dot(a, wd_ref[c])
        h2_ref[...] = h1 + f

    return pl.pallas_call(
        body, name="mixffn_fwd", grid=(t // TMB,),
        in_specs=[_row(TMB, D), _row(TMB, k), _wblk(k // N_CHIPS, lay["out"]), _const((1, D)), _wblk(D, lay["up"]),
                  _wblk(D, lay["down"])],
        out_specs=[_row(TMB, D), _row(TMB, D), _row(TMB, D), _row(TMB, DFF)],
        out_shape=[_sds((t, D), F32), _sds((t, D), F32), _sds((t, D), BF16), _sds((t, DFF), BF16)],
        compiler_params=_cp("parallel"),
    )(h, y, allw, g2, allw, allw)


def _layer_rows(tm, c, layer):
    return pl.BlockSpec((None, tm, c), lambda i: (layer, i, 0))


def ple_fwd(h2, p, layer, g3, allw, lay, wp):
    t = h2.shape[0]

    def body(h_ref, p_ref, g_ref, wg_ref, wp_ref, h3_ref, hn_ref):
        x = h_ref[...]
        yn, _, _ = _rms(x, g_ref[...], D)
        hn = yn.astype(BF16)
        hn_ref[...] = hn
        gt = _dot(hn, _rows_joined(wg_ref))
        pp = _dot(p_ref[...].astype(BF16), wp_ref[...])
        h3_ref[...] = x + _sigmoid(gt) * pp

    return pl.pallas_call(
        body, name="ple_fwd", grid=(t // TMB,),
        in_specs=[_row(TMB, D), _layer_rows(TMB, PLE, layer), _const((1, D)), _wblk(D // N_CHIPS, lay["gate"]),
                  _const((PLE, D))],
        out_specs=[_row(TMB, D), _row(TMB, D)],
        out_shape=[_sds((t, D), F32), _sds((t, D), BF16)],
        compiler_params=_cp("parallel"),
    )(h2, p, g3, allw, wp)


def _mla_project(h_ref, g1_ref, wdn_ref, gq_ref, gkv_ref, wuq_ref, wukv_ref):
    x = h_ref[...]
    yn, xhat, rx = _rms(x, g1_ref[...], D)
    hn = yn.astype(BF16)
    lat = _dot(hn, wdn_ref[...])
    cq, cqhat, rq = _rms(lat[:, :QL], gq_ref[...], QL)
    ckv, ckvhat, rkv = _rms(lat[:, QL:QL + KVL], gkv_ref[...], KVL)
    kr_raw = lat[:, QL + KVL:]
    cqb = cq.astype(BF16)
    ckvb = ckv.astype(BF16)
    qp = _dot(cqb, wuq_ref[...])
    kvp = _dot(ckvb, wukv_ref[...])
    return dict(xhat=xhat, rx=rx, hn=hn, cqhat=cqhat, rq=rq, ckvhat=ckvhat, rkv=rkv, kr_raw=kr_raw,
                cqb=cqb, ckvb=ckvb, qp=qp, kvp=kvp)


def mla_pre_fwd(h, g1, wdn, gq, gkv, wuq, wukv, gqn, gqr, gkn, gkr, cos, sin):
    t = h.shape[0]

    def body(h_ref, g1_ref, wdn_ref, gq_ref, gkv_ref, wuq_ref, wukv_ref, gqn_ref, gqr_ref, gkn_ref, gkr_ref,
             c_ref, s_ref, q_ref, k_ref, v_ref):
        m = _mla_project(h_ref, g1_ref, wdn_ref, gq_ref, gkv_ref, wuq_ref, wukv_ref)
        c = c_ref[...]
        s = s_ref[...]
        kr, _, _ = _rms(m["kr_raw"], gkr_ref[...], DR)
        krb = _rope(kr, c, s).astype(BF16)
        for hd in range(HEADS):
            qn, _, _ = _rms(m["qp"][:, hd * DN:(hd + 1) * DN], gqn_ref[...], DN)
            qr, _, _ = _rms(m["qp"][:, D + hd * LANES:D + (hd + 1) * LANES], gqr_ref[...], DR)
            q_ref[hd, :, 0:DN] = (qn * SM_SCALE).astype(BF16)
            q_ref[hd, :, DN:2 * DN] = (_rope(qr, c, s) * SM_SCALE).astype(BF16)
            kn, _, _ = _rms(m["kvp"][:, hd * 2 * DN:hd * 2 * DN + DN], gkn_ref[...], DN)
            k_ref[hd, :, 0:DN] = kn.astype(BF16)
            k_ref[hd, :, DN:2 * DN] = krb
            v_ref[hd] = m["kvp"][:, hd * 2 * DN + DN:(hd + 1) * 2 * DN].astype(BF16)

    hb = lambda w: pl.BlockSpec((HEADS, TM, w), lambda i: (0, i, 0))
    return pl.pallas_call(
        body, name="mla_pre_fwd", grid=(t // TM,),
        in_specs=[_row(TM, D), _const((1, D)), _const((D, LATP)), _const((1, QL)), _const((1, KVL)),
                  _const((QL, 2 * D)), _const((KVL, 2 * D)), _const((1, LANES)), _const((1, LANES)),
                  _const((1, LANES)), _const((1, LANES)), _row(TM, LANES), _row(TM, LANES)],
        out_specs=[hb(2 * DN), hb(2 * DN), hb(DN)],
        out_shape=[_sds((HEADS, t, 2 * DN), BF16), _sds((HEADS, t, 2 * DN), BF16), _sds((HEADS, t, DN), BF16)],
        compiler_params=_cp("parallel"),
    )(h, g1, wdn, gq, gkv, wuq, wukv, gqn, gqr, gkn, gkr, cos, sin)


def _diagonal_mask(n=TQ):
    return lax.broadcasted_iota(jnp.int32, (n, n), 1) <= lax.broadcasted_iota(jnp.int32, (n, n), 0)


def flash_fwd(q, k, v, seq):
    t = q.shape[1]
    nb = t // seq
    tq = TQ_FWD
    nq = seq // tq
    hp = FWD_HEADS

    def body(q_ref, k_ref, v_ref, o_ref, lse_ref):
        qi = pl.program_id(2)
        qs = [q_ref[a] for a in range(hp)]

        def step(j, carry, diagonal=False):
            rows = pl.ds(pl.multiple_of(j * tq, tq), tq)
            out = []
            for a in range(hp):
                m, l, acc = carry[a]
                s = _dot_nt(qs[a], k_ref[a, rows, :])
                if diagonal:
                    s = jnp.where(_diagonal_mask(tq), s, -1e30)
                m_new = jnp.maximum(m, jnp.max(s, axis=-1, keepdims=True))
                p = jnp.exp(s - m_new)
                alpha = jnp.exp(m - m_new)
                l = alpha * l + jnp.sum(p, axis=-1, keepdims=True)
                acc = alpha * acc + _dot(p.astype(BF16), v_ref[a, rows, :])
                out.append((m_new, l, acc))
            return tuple(out)

        one = (jnp.full((tq, 1), -1e30, F32), jnp.zeros((tq, 1), F32), jnp.zeros((tq, DN), F32))
        done = step(qi, lax.fori_loop(0, qi, step, (one,) * hp), diagonal=True)
        for a, (m, l, acc) in enumerate(done):
            o_ref[:, a * DN:(a + 1) * DN] = (acc / l).astype(BF16)
            lse_ref[a] = m + jnp.log(l)

    return pl.pallas_call(
        body, name="flash_fwd", grid=(nb, HEADS // hp, nq),
        in_specs=[pl.BlockSpec((hp, tq, 2 * DN), lambda b, h, i: (h, b * nq + i, 0)),
                  pl.BlockSpec((hp, seq, 2 * DN), lambda b, h, i: (h, b, 0)),
                  pl.BlockSpec((hp, seq, DN), lambda b, h, i: (h, b, 0))],
        out_specs=[pl.BlockSpec((tq, hp * DN), lambda b, h, i: (b * nq + i, h)),
                   pl.BlockSpec((hp, tq, 1), lambda b, h, i: (h, b * nq + i, 0))],
        out_shape=[_sds((t, HEADS * DN), BF16), _sds((HEADS, t, 1), F32)],
        compiler_params=_cp("parallel", "parallel", "arbitrary"),
    )(q, k, v)


def _gmlp_in(hn, win_ref):
    pre = [_dot(hn, win_ref[c]) for c in range(N_CHIPS)]
    return jnp.concatenate(pre[:2], axis=1), jnp.concatenate(pre[2:], axis=1)


def gmlp_fwd(h, g1, allw, lay, lng, lnb, wm, bfull):
    t = h.shape[0]

    def body(h_ref, g1_ref, win_ref, lng_ref, lnb_ref, wm_ref, b_ref, y_ref, pre_ref):
        yn, _, _ = _rms(h_ref[...], g1_ref[...], D)
        pre_u, pre_v = _gmlp_in(yn.astype(BF16), win_ref)
        pre_ref[:, :GH] = pre_u.astype(BF16)
        pre_ref[:, GH:] = pre_v.astype(BF16)
        u = _gelu(pre_u)
        v = _gelu(pre_v)
        xc = v - jnp.mean(v, axis=-1, keepdims=True)
        rs = lax.rsqrt(jnp.mean(xc * xc, axis=-1, keepdims=True) + EPS)
        vnb = (xc * rs * lng_ref[...] + lnb_ref[...]).astype(BF16)
        for ch in range(TM // GC):
            rows = slice(ch * GC, (ch + 1) * GC)
            for g in range(GG):
                cols = slice(g * GD, (g + 1) * GD)
                sv = _dot(wm_ref[g], vnb[rows, cols]) + b_ref[:, cols]
                y_ref[rows, cols] = (u[rows, cols] * sv).astype(BF16)

    return pl.pallas_call(
        body, name="gmlp_fwd", grid=(t // TM,),
        in_specs=[_row(TM, D), _const((1, D)), _wblk(D, lay["in"]), _const((1, GH)), _const((1, GH)),
                  _const((GG, GC, GC)), _const((GC, GH))],
        out_specs=[_row(TM, GH), _row(TM, 2 * GH)],
        out_shape=[_sds((t, GH), BF16), _sds((t, 2 * GH), BF16)],
        compiler_params=_cp("parallel"),
    )(h, g1, allw, lng, lnb, wm, bfull)


def loss_head(h, tgt):
    t = h.shape[0]

    def body(h_ref, t_ref, dh_ref, loss_ref):
        @pl.when(pl.program_id(0) == 0)
        def _():
            loss_ref[...] = jnp.zeros_like(loss_ref)

        e = h_ref[...] - t_ref[...]
        dh_ref[...] = e * (1.0 / D)
        part = jnp.sum(jnp.sum(e * e, axis=-1, keepdims=True), axis=0, keepdims=True) * (0.5 / D)
        loss_ref[...] += jnp.broadcast_to(part, loss_ref.shape)

    return pl.pallas_call(
        body, name="loss_head", grid=(t // TMB,),
        in_specs=[_row(TMB, D), _row(TMB, D)],
        out_specs=[_row(TMB, D), _const((8, LANES))],
        out_shape=[_sds((t, D), F32), _sds((8, LANES), F32)],
        compiler_params=_cp("arbitrary"),
    )(h, tgt)


def _zero_at_first_step(*refs):
    @pl.when(pl.program_id(0) == 0)
    def _():
        for r in refs:
            r[...] = jnp.zeros_like(r)


def ple_bwd(dh3, h2, p, layer, g3, allw, lay, wp):
    t = h2.shape[0]

    def body(dh_ref, h_ref, p_ref, g_ref, wg_ref, wp_ref, dh2_ref, dh2b_ref, dgt_ref, dpp_ref, dg_ref):
        _zero_at_first_step(dg_ref)
        dh3v = dh_ref[...]
        x = h_ref[...]
        g = g_ref[...]
        wg = _rows_joined(wg_ref)
        yn, xhat, r = _rms(x, g, D)
        gt = _dot(yn.astype(BF16), wg)
        pp = _dot(p_ref[...].astype(BF16), wp_ref[...])
        sg = _sigmoid(gt)
        dgt = (dh3v * pp * sg * (1.0 - sg)).astype(BF16)
        dgt_ref[...] = dgt
        dpp_ref[...] = (dh3v * sg).astype(BF16)
        dhn = _dot_nt(dgt, wg)
        _acc_rows(dg_ref, dhn * xhat)
        dh2 = dh3v + _rms_bwd(dhn, g, xhat, r, D)
        dh2_ref[...] = dh2
        dh2b_ref[...] = dh2.astype(BF16)

    return pl.pallas_call(
        body, name="ple_bwd", grid=(t // TMB,),
        in_specs=[_row(TMB, D), _row(TMB, D), _layer_rows(TMB, PLE, layer), _const((1, D)),
                  _wblk(D // N_CHIPS, lay["gate"]),
                  _const((PLE, D))],
        out_specs=[_row(TMB, D), _row(TMB, D), _row(TMB, D), _row(TMB, D), _const((8, D))],
        out_shape=[_sds((t, D), F32), _sds((t, D), BF16), _sds((t, D), BF16), _sds((t, D), BF16), _sds((8, D), F32)],
        compiler_params=_cp("arbitrary"),
    )(dh3, h2, p, g3, allw, wp)


def ffn_bwd(dh2, dh2b, h1, a, g2, allw, lay):
    t = h1.shape[0]

    def body(dh_ref, dhb_ref, h_ref, a_ref, g_ref, wu_ref, wd_ref, dh1_ref, dh1b_ref, du_ref, dg_ref):
        _zero_at_first_step(dg_ref)
        dhb = dhb_ref[...]
        g = g_ref[...]
        _, xhat, rr = _rms(h_ref[...], g, D)
        dhn = jnp.zeros((TMB, D), F32)
        for c in range(N_CHIPS):
            cs = slice(c * D, (c + 1) * D)
            da = _dot_nt(dhb, wd_ref[c])
            du = (da * (2.0 * jnp.sqrt(a_ref[:, cs].astype(F32)))).astype(BF16)
            du_ref[:, cs] = du
            dhn = dhn + _dot_nt(du, wu_ref[c])
        _acc_rows(dg_ref, dhn * xhat)
        dh1 = dh_ref[...] + _rms_bwd(dhn, g, xhat, rr, D)
        dh1_ref[...] = dh1
        dh1b_ref[...] = dh1.astype(BF16)

    return pl.pallas_call(
        body, name="ffn_bwd", grid=(t // TMB,),
        in_specs=[_row(TMB, D), _row(TMB, D), _row(TMB, D), _row(TMB, DFF), _const((1, D)), _wblk(D, lay["up"]),
                  _wblk(D, lay["down"])],
        out_specs=[_row(TMB, D), _row(TMB, D), _row(TMB, DFF), _const((8, D))],
        out_shape=[_sds((t, D), F32), _sds((t, D), BF16), _sds((t, DFF), BF16), _sds((8, D), F32)],
        compiler_params=_cp("arbitrary"),
    )(dh2, dh2b, h1, a, g2, allw, allw)


def linear_nt(a, allw, rows, row0):
    t = a.shape[0]
    k = N_CHIPS * rows

    def body(a_ref, w_ref, o_ref):
        o_ref[...] = _dot_nt(a_ref[...], _rows_joined(w_ref)).astype(BF16)

    return pl.pallas_call(
        body, name="linear_nt", grid=(t // TMB,),
        in_specs=[_row(TMB, D), _wblk(rows, row0)],
        out_specs=_row(TMB, k),
        out_shape=_sds((t, k), BF16),
        compiler_params=_cp("parallel"),
    )(a, allw)


def flash_bwd(q, k, v, o, do, lse, seq, after):
    t = q.shape[1]
    nb = t // seq
    nq = seq // TQ
    hp = BWD_HEADS

    def body(q_ref, k_ref, v_ref, o_ref, do_ref, lse_ref, after_ref, dq_ref, dk_ref, dv_ref):
        del after_ref
        kj = pl.program_id(2)

        @pl.when(kj == 0)
        def _():
            dq_ref[...] = jnp.zeros_like(dq_ref)

        def step(i, carry, diagonal=False):
            rows = pl.ds(pl.multiple_of(i * TQ, TQ), TQ)
            out = []
            for a in range(hp):
                dk, dv = carry[a]
                kv = k_ref[a]
                qv = q_ref[a, rows, :]
                dov = do_ref[rows, a * DN:(a + 1) * DN]
                ov = o_ref[rows, a * DN:(a + 1) * DN]
                delta = jnp.sum(dov.astype(F32) * ov.astype(F32), axis=-1, keepdims=True)
                s = _dot_nt(qv, kv)
                if diagonal:
                    s = jnp.where(_diagonal_mask(), s, -1e30)
                p = jnp.exp(s - lse_ref[a, rows, :])
                dp = _dot_nt(dov, v_ref[a])
                ds = (p * (dp - delta)).astype(BF16)
                dv = dv + _dot_tn(p.astype(BF16), dov)
                dk = dk + _dot_tn(ds, qv)
                dq_ref[a, rows, :] += _dot(ds, kv)
                out.append((dk, dv))
            return tuple(out)

        one = (jnp.zeros((TQ, 2 * DN), F32), jnp.zeros((TQ, DN), F32))
        done = lax.fori_loop(kj + 1, nq, step, step(kj, (one,) * hp, diagonal=True))
        for a, (dk, dv) in enumerate(done):
            dk_ref[a] = dk
            dv_ref[a] = dv

    return pl.pallas_call(
        body, name="flash_bwd", grid=(nb, HEADS // hp, nq),
        in_specs=[pl.BlockSpec((hp, seq, 2 * DN), lambda b, h, j: (h, b, 0)),
                  pl.BlockSpec((hp, TQ, 2 * DN), lambda b, h, j: (h, b * nq + j, 0)),
                  pl.BlockSpec((hp, TQ, DN), lambda b, h, j: (h, b * nq + j, 0)),
                  pl.BlockSpec((seq, hp * DN), lambda b, h, j: (b, h)),
                  pl.BlockSpec((seq, hp * DN), lambda b, h, j: (b, h)),
                  pl.BlockSpec((hp, seq, 1), lambda b, h, j: (h, b, 0)), _ANY],
        out_specs=[pl.BlockSpec((hp, seq, 2 * DN), lambda b, h, j: (h, b, 0)),
                   pl.BlockSpec((hp, TQ, 2 * DN), lambda b, h, j: (h, b * nq + j, 0)),
                   pl.BlockSpec((hp, TQ, DN), lambda b, h, j: (h, b * nq + j, 0))],
        out_shape=[_sds((HEADS, t, 2 * DN), F32), _sds((HEADS, t, 2 * DN), F32), _sds((HEADS, t, DN), F32)],
        compiler_params=_cp("parallel", "parallel", "arbitrary"),
    )(q, k, v, o, do, lse, after)


def mla_pre_bwd(dq, dk, dv, dh1, h, g1, wdn, gq, gkv, wuq, wukv, gqn, gqr, gkn, gkr, cos, sin):
    t = h.shape[0]

    def body(dq_ref, dk_ref, dv_ref, dh1_ref, h_ref, g1_ref, wdn_ref, gq_ref, gkv_ref, wuq_ref, wukv_ref,
             gqn_ref, gqr_ref, gkn_ref, gkr_ref, c_ref, s_ref,
             dh_ref, hn_ref, cq_ref, ckv_ref, dqp_ref, dkvp_ref, dlat_ref,
             dg1_ref, dgq_ref, dgkv_ref, dgqn_ref, dgqr_ref, dgkn_ref, dgkr_ref):
        _zero_at_first_step(dg1_ref, dgq_ref, dgkv_ref, dgqn_ref, dgqr_ref, dgkn_ref, dgkr_ref)
        m = _mla_project(h_ref, g1_ref, wdn_ref, gq_ref, gkv_ref, wuq_ref, wukv_ref)
        hn_ref[...] = m["hn"]
        cq_ref[...] = m["cqb"]
        ckv_ref[...] = m["ckvb"]
        c = c_ref[...]
        s = s_ref[...]
        gqn = gqn_ref[...]
        gqr = gqr_ref[...]
        gkn = gkn_ref[...]
        gkr = gkr_ref[...]

        dkr = dk_ref[0, :, DN:2 * DN]
        for hd in range(1, HEADS):
            dkr = dkr + dk_ref[hd, :, DN:2 * DN]
        dkr = _rope_t(dkr, c, s)
        _, krhat, rkr = _rms(m["kr_raw"], gkr, DR)
        _acc_rows(dgkr_ref, dkr * krhat)
        dkr_raw = _rms_bwd(dkr, gkr, krhat, rkr, DR)

        for hd in range(HEADS):
            ncols = slice(hd * DN, (hd + 1) * DN)
            _, xh, r = _rms(m["qp"][:, ncols], gqn, DN)
            dqn = dq_ref[hd, :, 0:DN] * SM_SCALE
            _acc_rows(dgqn_ref, dqn * xh)
            dqp_ref[:, ncols] = _rms_bwd(dqn, gqn, xh, r, DN).astype(BF16)

            rcols = slice(D + hd * LANES, D + (hd + 1) * LANES)
            _, xh, r = _rms(m["qp"][:, rcols], gqr, DR)
            dqr = _rope_t(dq_ref[hd, :, DN:2 * DN] * SM_SCALE, c, s)
            _acc_rows(dgqr_ref, dqr * xh)
            dqp_ref[:, rcols] = _rms_bwd(dqr, gqr, xh, r, DR).astype(BF16)

            kcols = slice(hd * 2 * DN, hd * 2 * DN + DN)
            _, xh, r = _rms(m["kvp"][:, kcols], gkn, DN)
            dkn = dk_ref[hd, :, 0:DN]
            _acc_rows(dgkn_ref, dkn * xh)
            dkvp_ref[:, kcols] = _rms_bwd(dkn, gkn, xh, r, DN).astype(BF16)
            dkvp_ref[:, hd * 2 * DN + DN:(hd + 1) * 2 * DN] = dv_ref[hd].astype(BF16)

        dcq = _dot_nt(dqp_ref[...], wuq_ref[...])
        _acc_rows(dgq_ref, dcq * m["cqhat"])
        dlat_q = _rms_bwd(dcq, gq_ref[...], m["cqhat"], m["rq"], QL)
        dckv = _dot_nt(dkvp_ref[...], wukv_ref[...])
        _acc_rows(dgkv_ref, dckv * m["ckvhat"])
        dlat_kv = _rms_bwd(dckv, gkv_ref[...], m["ckvhat"], m["rkv"], KVL)
        dlat = jnp.concatenate([dlat_q, dlat_kv, dkr_raw], axis=1).astype(BF16)
        dlat_ref[...] = dlat
        dhn = _dot_nt(dlat, wdn_ref[...])
        _acc_rows(dg1_ref, dhn * m["xhat"])
        dh_ref[...] = dh1_ref[...] + _rms_bwd(dhn, g1_ref[...], m["xhat"], m["rx"], D)

    hb = lambda w: pl.BlockSpec((HEADS, TM, w), lambda i: (0, i, 0))
    return pl.pallas_call(
        body, name="mla_pre_bwd", grid=(t // TM,),
        in_specs=[hb(2 * DN), hb(2 * DN), hb(DN), _row(TM, D), _row(TM, D), _const((1, D)), _const((D, LATP)),
                  _const((1, QL)), _const((1, KVL)), _const((QL, 2 * D)), _const((KVL, 2 * D)),
                  _const((1, LANES)), _const((1, LANES)), _const((1, LANES)), _const((1, LANES)),
                  _row(TM, LANES), _row(TM, LANES)],
        out_specs=[_row(TM, D), _row(TM, D), _row(TM, QL), _row(TM, KVL), _row(TM, 2 * D), _row(TM, 2 * D),
                   _row(TM, LATP), _const((8, D)), _const((8, QL)), _const((8, KVL)), _const((8, LANES)),
                   _const((8, LANES)), _const((8, LANES)), _const((8, LANES))],
        out_shape=[_sds((t, D), F32), _sds((t, D), BF16), _sds((t, QL), BF16), _sds((t, KVL), BF16),
                   _sds((t, 2 * D), BF16), _sds((t, 2 * D), BF16), _sds((t, LATP), BF16),
                   _sds((8, D), F32), _sds((8, QL), F32), _sds((8, KVL), F32), _sds((8, LANES), F32),
                   _sds((8, LANES), F32), _sds((8, LANES), F32), _sds((8, LANES), F32)],
        compiler_params=_cp("arbitrary"),
    )(dq, dk, dv, dh1, h, g1, wdn, gq, gkv, wuq, wukv, gqn, gqr, gkn, gkr, cos, sin)


def gmlp_bwd(dh1, dh1b, h, pre, g1, allw, lay, lng, lnb, wm, wmt, bfull, tril):
    t = h.shape[0]

    def body(dh1_ref, dh1b_ref, h_ref, pre_ref, g1_ref, win_ref, lng_ref, lnb_ref, wm_ref, wmt_ref, b_ref,
             wout_ref, tril_ref, dh_ref, hn_ref, dpre_ref, dws_ref, dbs_ref, dlng_ref, dlnb_ref, dg1_ref,
             dvn_s):
        _zero_at_first_step(dws_ref, dbs_ref, dlng_ref, dlnb_ref, dg1_ref)
        g1 = g1_ref[...]
        yn, xhat, rx = _rms(h_ref[...], g1, D)
        hn_ref[...] = yn.astype(BF16)
        dy = _dot_nt(dh1b_ref[...], _rows_joined(wout_ref))
        pre_u = pre_ref[:, :GH].astype(F32)
        pre_v = pre_ref[:, GH:].astype(F32)
        u, gg_u = _gelu_and_grad(pre_u)
        v, gg_v = _gelu_and_grad(pre_v)
        xc = v - jnp.mean(v, axis=-1, keepdims=True)
        rs = lax.rsqrt(jnp.mean(xc * xc, axis=-1, keepdims=True) + EPS)
        vhat = xc * rs
        lng = lng_ref[...]
        vnb = (vhat * lng + lnb_ref[...]).astype(BF16)
        dsv = dy * u
        dsvb = dsv.astype(BF16)
        tril_m = tril_ref[...]
        for ch in range(TM // GC):
            rows = slice(ch * GC, (ch + 1) * GC)
            dbs_ref[...] += dsv[rows, :]
            for g in range(GG):
                cols = slice(g * GD, (g + 1) * GD)
                sv = _dot(wm_ref[g], vnb[rows, cols]) + b_ref[:, cols]
                dpre_ref[rows, cols] = (dy[rows, cols] * sv * gg_u[rows, cols]).astype(BF16)
                dvn_s[rows, cols] = _dot(wmt_ref[g], dsvb[rows, cols])
                dws_ref[g] += _dot_nt(dsvb[rows, cols], vnb[rows, cols]) * tril_m
        dvn = dvn_s[...]
        _acc_rows(dlng_ref, dvn * vhat)
        _acc_rows(dlnb_ref, dvn)
        dvhat = dvn * lng
        dv = rs * (dvhat - jnp.mean(dvhat, axis=-1, keepdims=True)
                   - vhat * jnp.mean(dvhat * vhat, axis=-1, keepdims=True))
        dpre_v = (dv * gg_v).astype(BF16)
        dpre_ref[:, GH:] = dpre_v
        dhn = _dot_nt(dpre_ref[:, 0:D], win_ref[0])
        for c in range(1, N_CHIPS):
            dhn = dhn + _dot_nt(dpre_ref[:, c * D:(c + 1) * D], win_ref[c])
        _acc_rows(dg1_ref, dhn * xhat)
        dh_ref[...] = dh1_ref[...] + _rms_bwd(dhn, g1, xhat, rx, D)

    return pl.pallas_call(
        body, name="gmlp_bwd", grid=(t // TM,),
        in_specs=[_row(TM, D), _row(TM, D), _row(TM, D), _row(TM, 2 * GH), _const((1, D)), _wblk(D, lay["in"]),
                  _const((1, GH)), _const((1, GH)), _const((GG, GC, GC)), _const((GG, GC, GC)), _const((GC, GH)),
                  _wblk(GH // N_CHIPS, lay["out"]), _const((GC, GC))],
        out_specs=[_row(TM, D), _row(TM, D), _row(TM, 2 * GH), _const((GG, GC, GC)), _const((GC, GH)),
                   _const((8, GH)), _const((8, GH)), _const((8, D))],
        out_shape=[_sds((t, D), F32), _sds((t, D), BF16), _sds((t, 2 * GH), BF16), _sds((GG, GC, GC), F32),
                   _sds((GC, GH), F32), _sds((8, GH), F32), _sds((8, GH), F32), _sds((8, D), F32)],
        scratch_shapes=[pltpu.VMEM((TM, GH), F32)],
        compiler_params=_cp("arbitrary"),
    )(dh1, dh1b, h, pre, g1, allw, lng, lnb, wm, wmt, bfull, allw, tril)


def _token_step(t):
    return next(s for s in (2048, 1024, 512) if t % s == 0)


def mm_tn(a, b, layer=None):
    t, k = a.shape[-2:]
    n = b.shape[1]
    tk = min(k, 1024)
    tn = min(n, 1024)
    tt = _token_step(t)
    a_spec = (pl.BlockSpec((tt, tk), lambda i, j, s: (s, i)) if layer is None else
              pl.BlockSpec((None, tt, tk), lambda i, j, s: (layer, s, i)))

    def body(a_ref, b_ref, o_ref):
        @pl.when(pl.program_id(2) == 0)
        def _():
            o_ref[...] = jnp.zeros_like(o_ref)

        o_ref[...] += _dot_tn(a_ref[...].astype(BF16), b_ref[...].astype(BF16))

    return pl.pallas_call(
        body, name="mm_tn", grid=(k // tk, n // tn, t // tt),
        in_specs=[a_spec, pl.BlockSpec((tt, tn), lambda i, j, s: (s, j))],
        out_specs=pl.BlockSpec((tk, tn), lambda i, j, s: (i, j)), out_shape=_sds((k, n), F32),
        compiler_params=_cp("parallel", "parallel", "arbitrary"),
    )(a, b)


def mm_tn_into(buf, a, b, rows, row0, col_sharded):
    t = a.shape[0]
    tt = _token_step(t)
    assert row0 % rows == 0 and a.shape[1] == (rows if col_sharded else N_CHIPS * rows), (rows, row0, a.shape)
    assert b.shape[1] == (N_CHIPS * D if col_sharded else D), b.shape
    grid = (1, N_CHIPS, t // tt) if col_sharded else (N_CHIPS, 1, t // tt)
    fresh = isinstance(buf, int)

    def body(*refs):
        a_ref, b_ref, o_ref = refs[-3:]

        @pl.when(pl.program_id(2) == 0)
        def _():
            o_ref[...] = jnp.zeros_like(o_ref)

        o_ref[...] += _dot_tn(a_ref[...].astype(BF16), b_ref[...].astype(BF16))

    specs = [pl.BlockSpec((tt, rows), lambda i, j, s: (s, i)), pl.BlockSpec((tt, D), lambda i, j, s: (s, j))]
    return pl.pallas_call(
        body, name="mm_tn_into", grid=grid,
        in_specs=specs if fresh else [_ANY] + specs,
        out_specs=pl.BlockSpec((None, rows, D), lambda i, j, s: (i + j, row0 // rows, 0)),
        out_shape=_sds((N_CHIPS, buf, D) if fresh else buf.shape, F32),
        input_output_aliases={} if fresh else {0: 0},
        compiler_params=_cp("parallel", "parallel", "arbitrary"),
    )(*((a, b) if fresh else (buf, a, b)))


def adamw(w, g, m, v):
    rows, cols = w.shape
    tr = rows if rows <= 512 else next(r for r in (512, 384, 256, 128) if rows % r == 0)
    c1 = 1.0 - ADAM_B1 ** ADAM_STEP
    c2 = 1.0 - ADAM_B2 ** ADAM_STEP

    def body(w_ref, g_ref, m_ref, v_ref, d_ref, mo_ref, vo_ref):
        gv = g_ref[...]
        mn = ADAM_B1 * m_ref[...] + (1.0 - ADAM_B1) * gv
        vn = ADAM_B2 * v_ref[...] + (1.0 - ADAM_B2) * (gv * gv)
        mo_ref[...] = mn
        vo_ref[...] = vn
        d_ref[...] = -ADAM_LR * ((mn / c1) / (jnp.sqrt(vn / c2) + ADAM_EPS) + ADAM_WD * w_ref[...])

    spec = pl.BlockSpec((tr, cols), lambda i: (i, 0))
    return pl.pallas_call(
        body, name="adamw", grid=(rows // tr,),
        in_specs=[spec] * 4, out_specs=[spec] * 3, out_shape=[_sds((rows, cols), F32)] * 3,
        compiler_params=_cp("parallel"),
    )(w, g, m, v)


def adamw_layers(w, m, v, bufs, row0s):
    nl, a, _ = w.shape
    tr = min(a, 256)
    c1 = 1.0 - ADAM_B1 ** ADAM_STEP
    c2 = 1.0 - ADAM_B2 ** ADAM_STEP
    assert all(r % tr == 0 for r in row0s) and a % tr == 0, (row0s, a)

    def body(w_ref, m_ref, v_ref, *rest):
        g_refs, (g_ref, d_ref, mo_ref, vo_ref) = rest[:nl], rest[nl:]
        for l in range(nl):
            @pl.when(pl.program_id(0) == l)
            def _(l=l):
                gv = g_refs[l][...]
                g_ref[...] = gv
                mn = ADAM_B1 * m_ref[...] + (1.0 - ADAM_B1) * gv
                vn = ADAM_B2 * v_ref[...] + (1.0 - ADAM_B2) * (gv * gv)
                mo_ref[...] = mn
                vo_ref[...] = vn
                d_ref[...] = -ADAM_LR * ((mn / c1) / (jnp.sqrt(vn / c2) + ADAM_EPS) + ADAM_WD * w_ref[...])

    def rows_of(l, row0):
        return pl.BlockSpec((tr, D), lambda li, i: (jnp.where(li == l, row0 // tr + i, row0 // tr), 0))

    spec = pl.BlockSpec((None, tr, D), lambda li, i: (li, i, 0))
    return pl.pallas_call(
        body, name="adamw_layers", grid=(nl, a // tr),
        in_specs=[spec] * 3 + [rows_of(l, r) for l, r in enumerate(row0s)],
        out_specs=[spec] * 4, out_shape=[_sds(w.shape, F32)] * 4,
        compiler_params=_cp("arbitrary", "arbitrary"),
    )(w, m, v, *bufs)


def _place():
    return lax.axis_index("x"), lax.axis_index("y"), lax.axis_index("c")


def _other_chips(x, y):
    return [(1 - x, y), (x, 1 - y), (1 - x, 1 - y)]


_ANY = pl.BlockSpec(memory_space=pl.ANY)


_HBM = pl.BlockSpec(memory_space=pltpu.HBM)
_SEM = pl.BlockSpec(memory_space=pltpu.SEMAPHORE)
_EFFECT = pltpu.SideEffectType.DATAFLOW_SIDE_EFFECTING
N_ICI = 3


def _exchange_start(name, src, land, copies, n):
    def body(src_ref, land_ref, *outs):
        sems, token = outs[:2 * n], outs[-1]
        for j, (s, d, to) in enumerate(copies(src_ref, land_ref, _place())):
            pltpu.make_async_remote_copy(src_ref=s, dst_ref=d, send_sem=sems[j], recv_sem=sems[n + j],
                                         device_id=to, device_id_type=MESH).start()
        token[...] = jnp.zeros_like(token)

    sem = pltpu.SemaphoreType.DMA(())
    outs = pl.pallas_call(
        body, name=name,
        out_shape=(sem,) * (2 * n) + (pltpu.HBM(src.shape, src.dtype), pltpu.HBM(land.shape, land.dtype),
                                      _sds((8, LANES), F32)),
        in_specs=(_HBM, _HBM),
        out_specs=(_SEM,) * (2 * n) + (_HBM, _HBM, pl.BlockSpec(memory_space=pltpu.VMEM)),
        input_output_aliases={0: 2 * n, 1: 2 * n + 1},
        compiler_params=pltpu.CompilerParams(has_side_effects=_EFFECT),
    )(pltpu.with_memory_space_constraint(src, pltpu.HBM), pltpu.with_memory_space_constraint(land, pltpu.HBM))
    return outs[:2 * n], outs[2 * n], outs[2 * n + 1], outs[-1]


def _exchange_wait(name, sems, src, land, after, arrivals):
    n = len(sems) // 2

    def body(src_ref, land_ref, *rest):
        sems = rest[:2 * n]
        for j, (s, d) in enumerate(arrivals(src_ref, land_ref, _place())):
            cp = pltpu.make_async_remote_copy(src_ref=s, dst_ref=d, send_sem=sems[j], recv_sem=sems[n + j],
                                              device_id=_place(), device_id_type=MESH)
            cp.wait_send()
            cp.wait_recv()

    return pl.pallas_call(
        body, name=name, out_shape=(pltpu.HBM(src.shape, src.dtype), pltpu.HBM(land.shape, land.dtype)),
        in_specs=(_HBM, _HBM) + (_SEM,) * (2 * n) + (_ANY,), out_specs=(_HBM, _HBM),
        input_output_aliases={0: 0, 1: 1},
        compiler_params=pltpu.CompilerParams(has_side_effects=_EFFECT),
    )(src, land, *sems, after)


def _halves(c, hh):
    return pl.ds(pl.multiple_of(c * hh, 16), hh), pl.ds(pl.multiple_of((1 - c) * hh, 16), hh)


def gather_start(land, tag):
    _, rr, _ = land.shape
    assert rr % 32 == 0, rr

    def copies(_, land_ref, place):
        x, y, c = place
        mine = land_ref.at[2 * x + y, _halves(c, rr // 2)[0]]
        return [(mine, mine, (cx, cy, c)) for cx, cy in _other_chips(x, y)]

    return _exchange_start(f"gather_start_{tag}", jnp.zeros((8, LANES), F32), land, copies, N_ICI)


def gather_wait(sems, src, land, after, tag):
    def arrivals(_, land_ref, place):
        x, y, c = place
        half = _halves(c, land.shape[1] // 2)[0]
        return [(land_ref.at[2 * x + y, half], land_ref.at[2 * cx + cy, half]) for cx, cy in _other_chips(x, y)]

    return _exchange_wait(f"gather_wait_{tag}", sems, src, land, after, arrivals)


def pass_start(land, tag):
    def copies(_, land_ref, place):
        x, y, c = place
        half = _halves(c, land.shape[1] // 2)[0]
        return [(land_ref.at[2 * cx + cy, half], land_ref.at[2 * cx + cy, half], (x, y, 1 - c))
                for cx, cy in _other_chips(x, y)]

    return _exchange_start(f"pass_start_{tag}", jnp.zeros((8, LANES), F32), land, copies, N_ICI)


def pass_wait(sems, src, land, after, tag):
    def arrivals(_, land_ref, place):
        x, y, c = place
        mine, other = _halves(c, land.shape[1] // 2)
        return [(land_ref.at[2 * cx + cy, mine], land_ref.at[2 * cx + cy, other]) for cx, cy in _other_chips(x, y)]

    return _exchange_wait(f"pass_wait_{tag}", sems, src, land, after, arrivals)


def swap_start(g, tag):
    _, rr, cc = g.shape

    def copies(g_ref, got_ref, place):
        x, y, c = place
        other = _halves(c, rr // 2)[1]
        return [(g_ref.at[k, other], got_ref.at[k], (x, y, 1 - c)) for k in range(N_CHIPS)]

    return _exchange_start(f"swap_start_{tag}", g, lax.empty((N_CHIPS, rr // 2, cc), g.dtype), copies, N_CHIPS)


def swap_wait(sems, g, got, after, tag):
    def arrivals(g_ref, got_ref, place):
        other = _halves(place[2], g.shape[1] // 2)[1]
        return [(g_ref.at[k, other], got_ref.at[k]) for k in range(N_CHIPS)]

    return _exchange_wait(f"swap_wait_{tag}", sems, g, got, after, arrivals)


def chip_sum(place, g32, got):
    _, rr, cc = g32.shape
    hh = rr // 2
    tr = SUM_ROWS
    assert rr % 2 == 0 and hh % tr == 0, (rr, tr)
    nb = hh // tr

    def body(place_ref, g_ref, got_ref, own_ref, all_ref):
        s = g_ref[...] + got_ref[...].astype(F32)
        all_ref[...] = s.astype(BF16)
        own_ref[...] = g_ref[place_ref[1]] + got_ref[place_ref[1]].astype(F32)

    return pl.pallas_call(
        body, name="chip_sum",
        grid_spec=pltpu.PrefetchScalarGridSpec(
            num_scalar_prefetch=1, grid=(nb,),
            in_specs=[pl.BlockSpec((N_CHIPS, tr, cc), lambda i, pr: (0, pr[0] * nb + i, 0)),
                      pl.BlockSpec((N_CHIPS, tr, cc), lambda i, pr: (0, i, 0))],
            out_specs=[pl.BlockSpec((tr, cc), lambda i, pr: (i, 0)),
                       pl.BlockSpec((N_CHIPS, tr, cc), lambda i, pr: (0, i, 0))]),
        out_shape=[_sds((hh, cc), F32), _sds((N_CHIPS, hh, cc), BF16)],
        compiler_params=_cp("parallel"),
    )(place, g32, got)


def _scatter_copies(s_ref, land_ref, place):
    x, y, c = place
    return [(s_ref.at[2 * cx + cy], land_ref.at[j], (cx, cy, c)) for j, (cx, cy) in enumerate(_other_chips(x, y))]


def scatter_start(s, tag):
    return _exchange_start(f"scatter_start_{tag}", s, lax.empty((N_ICI,) + s.shape[1:], s.dtype), _scatter_copies, N_ICI)


def scatter_wait(sems, s, land, after, tag):
    return _exchange_wait(f"scatter_wait_{tag}", sems, s, land, after,
                          lambda s_ref, land_ref, place: [(a, b) for a, b, _ in _scatter_copies(s_ref, land_ref, place)])


def final_sum(place, own, got):
    hh, cc = own.shape
    tr = SUM_ROWS
    assert hh % tr == 0, (hh, tr)
    nb = hh // tr

    def body(place_ref, own_ref, got_ref, o_ref):
        del place_ref
        o_ref[...] = ((own_ref[...] + got_ref[0].astype(F32)) + got_ref[1].astype(F32)) + got_ref[2].astype(F32)

    return pl.pallas_call(
        body, name="final_sum",
        grid_spec=pltpu.PrefetchScalarGridSpec(
            num_scalar_prefetch=1, grid=(nb,),
            in_specs=[pl.BlockSpec((tr, cc), lambda i, pr: (i, 0)), pl.BlockSpec((3, tr, cc), lambda i, pr: (0, i, 0))],
            out_specs=pl.BlockSpec((tr, cc), lambda i, pr: (pr[0] * nb + i, 0))),
        out_shape=_sds((2 * hh, cc), F32),
        compiler_params=_cp("parallel"),
    )(place, own, got)


def share_start(f, tag):
    def copies(_, f_ref, place):
        x, y, c = place
        mine = f_ref.at[_halves(c, f.shape[0] // 2)[0]]
        return [(mine, mine, (x, y, 1 - c))]

    return _exchange_start(f"share_start_{tag}", jnp.zeros((8, LANES), F32), f, copies, 1)


def share_wait(sems, src, f, after, tag):
    def arrivals(_, f_ref, place):
        mine, other = _halves(place[2], f.shape[0] // 2)
        return [(f_ref.at[mine], f_ref.at[other])]

    return _exchange_wait(f"share_wait_{tag}", sems, src, f, after, arrivals)


N_DEV = 8


def _peers(place):
    x, y, c = place
    return [((1 - x) if r & 4 else x, (1 - y) if r & 2 else y, (1 - c) if r & 1 else c) for r in range(1, N_DEV)]


def _device_index(place):
    x, y, c = place
    return 4 * x + 2 * y + c


def small_start(land, tag):
    def copies(_, land_ref, place):
        mine = land_ref.at[_device_index(place)]
        return [(mine, mine, to) for to in _peers(place)]

    return _exchange_start(f"small_start_{tag}", jnp.zeros((8, LANES), F32), land, copies, N_DEV - 1)


def small_wait(sems, src, land, after, tag):
    def arrivals(_, land_ref, place):
        return [(land_ref.at[_device_index(place)], land_ref.at[_device_index(peer)]) for peer in _peers(place)]

    return _exchange_wait(f"small_wait_{tag}", sems, src, land, after, arrivals)


def sum_devices(land):
    _, rr, cc = land.shape
    tr = 56
    assert rr % tr == 0, rr

    def body(l_ref, o_ref):
        acc = l_ref[0]
        for d in range(1, N_DEV):
            acc = acc + l_ref[d]
        o_ref[...] = acc

    return pl.pallas_call(
        body, name="sum_devices", grid=(rr // tr,),
        in_specs=[pl.BlockSpec((N_DEV, tr, cc), lambda i: (0, i, 0))],
        out_specs=pl.BlockSpec((tr, cc), lambda i: (i, 0)), out_shape=_sds((rr, cc), F32),
        compiler_params=_cp("parallel"),
    )(land)


_BIG = ["mla_w_down", "mla_w_uq", "mla_w_ukv", "mla_w_out", "gmlp_w_in", "gmlp_w_out", "ffn_w_up", "ffn_w_down",
        "ple_w_gate", "ple_w_proj"]
_SMALL_REST = ["norm_mix", "norm_ffn", "norm_ple", "mla_q_lora_g", "mla_kv_lora_g", "mla_q_nope_g", "mla_q_rope_g",
               "mla_k_nope_g", "mla_k_rope_g"]
_SMALL_GMLP = ["gmlp_ln_g", "gmlp_ln_b", "gmlp_w_s", "gmlp_b_s"]
_SMALL = _SMALL_REST + _SMALL_GMLP

_LAY_MLA = dict(up=0, down=1024, out=2048, gate=2304, wdn=2560, wuq=2736, wukv=2880, proj=3008, rows=3072)
_LAY_MLA_MAIN = dict(up=0, down=1024, out=2048, gate=2304, rows=2560)
_LAY_MLA_ODD = dict(wdn=0, wuq=176, wukv=320, proj=448, rows=512)
_LAY_GMLP = {"up": 0, "down": 1024, "in": 2048, "out": 3072, "gate": 3584, "proj": 3840, "ln": 3904, "rows": 4096}
SPLIT_LAYERS = (0,)


def _layer_units(i):
    j = i // 2
    if i % 2 == 0:
        odd, lay = (_LAY_MLA_ODD, _LAY_MLA_MAIN) if i in SPLIT_LAYERS else (_LAY_MLA, _LAY_MLA)
        small = [("mla_w_down", j, odd["wdn"]), ("mla_w_uq", j, odd["wuq"]), ("mla_w_ukv", j, odd["wukv"]),
                 ("ple_w_proj", i, odd["proj"])]
        large = [("ffn_w_up", i, lay["up"]), ("ffn_w_down", i, lay["down"]), ("mla_w_out", j, lay["out"]),
                 ("ple_w_gate", i, lay["gate"])]
        return [("odd", odd, small), ("main", lay, large)] if i in SPLIT_LAYERS else [("main", lay, large + small)]
    lay = _LAY_GMLP
    return [("main", lay, [("ffn_w_up", i, lay["up"]), ("ffn_w_down", i, lay["down"]), ("gmlp_w_in", j, lay["in"]),
                           ("gmlp_w_out", j, lay["out"]), ("ple_w_gate", i, lay["gate"]),
                           ("ple_w_proj", i, lay["proj"])])]


def _pack_rows(parts, dtype, pad_to=None, slot=False):
    size = sum(p.size for p in parts)
    tail = [] if pad_to is None or pad_to * D == size else [jnp.zeros((pad_to * D - size,), dtype)]
    shape = (1, -1, D) if slot else (-1, D)
    if all(p.size % D == 0 for p in parts + tail):
        return jnp.concatenate([p.astype(dtype).reshape(shape) for p in parts + tail], axis=len(shape) - 2)
    return jnp.concatenate([p.astype(dtype).reshape(-1) for p in parts + tail]).reshape(shape)


def _odd(allw, row0, a, b):
    return allw[:, row0:row0 + a * b // D].reshape(N_CHIPS, a, b)


def _cols_joined(s):
    return jnp.transpose(s, (1, 0, 2)).reshape(s.shape[1], N_CHIPS * s.shape[2])


def _col_shards(full):
    a, bb = full.shape
    return jnp.transpose(full.reshape(a, N_CHIPS, bb // N_CHIPS), (1, 0, 2)).reshape(N_CHIPS, -1, D)


def _pad_lanes(g):
    return jnp.pad(g, ((0, 0), (0, LANES - g.shape[1])))


def _split_uq(wuq):
    l = wuq.shape[0]
    w = wuq.reshape(l, QL, HEADS, DN + DR)
    nope = w[..., :DN].reshape(l, QL, HEADS * DN)
    rope = jnp.pad(w[..., DN:], ((0, 0), (0, 0), (0, 0), (0, LANES - DR))).reshape(l, QL, HEADS * LANES)
    return jnp.concatenate([nope, rope], axis=-1)


def _merge_uq(d):
    nope = d[:, :HEADS * DN].reshape(QL, HEADS, DN)
    rope = d[:, HEADS * DN:].reshape(QL, HEADS, LANES)[..., :DR]
    return jnp.concatenate([nope, rope], axis=-1).reshape(QL, HEADS * (DN + DR))


def _rope_tables(positions):
    inv_freq = ROPE_BASE ** (-(jnp.arange(0, DR, 2, dtype=F32) / DR))
    ang = positions.reshape(-1).astype(F32)[:, None] * inv_freq
    z = jnp.zeros((ang.shape[0], LANES - DR), F32)
    return (jnp.concatenate([jnp.cos(ang), jnp.cos(ang), z], axis=1),
            jnp.concatenate([jnp.sin(ang), jnp.sin(ang), z], axis=1))


def kernel(x, p, positions, norm_mix, norm_ffn, norm_ple, mla_w_down, mla_q_lora_g, mla_kv_lora_g, mla_w_uq, mla_w_ukv, mla_q_nope_g, mla_q_rope_g, mla_k_nope_g, mla_k_rope_g, mla_w_out, gmlp_w_in, gmlp_ln_g, gmlp_ln_b, gmlp_w_s, gmlp_b_s, gmlp_w_out, ffn_w_up, ffn_w_down, ple_w_gate, ple_w_proj, loss_target, m_norm_mix, m_norm_ffn, m_norm_ple, m_mla_w_down, m_mla_q_lora_g, m_mla_kv_lora_g, m_mla_w_uq, m_mla_w_ukv, m_mla_q_nope_g, m_mla_q_rope_g, m_mla_k_nope_g, m_mla_k_rope_g, m_mla_w_out, m_gmlp_w_in, m_gmlp_ln_g, m_gmlp_ln_b, m_gmlp_w_s, m_gmlp_b_s, m_gmlp_w_out, m_ffn_w_up, m_ffn_w_down, m_ple_w_gate, m_ple_w_proj, v_norm_mix, v_norm_ffn, v_norm_ple, v_mla_w_down, v_mla_q_lora_g, v_mla_kv_lora_g, v_mla_w_uq, v_mla_w_ukv, v_mla_q_nope_g, v_mla_q_rope_g, v_mla_k_nope_g, v_mla_k_rope_g, v_mla_w_out, v_gmlp_w_in, v_gmlp_ln_g, v_gmlp_ln_b, v_gmlp_w_s, v_gmlp_b_s, v_gmlp_w_out, v_ffn_w_up, v_ffn_w_down, v_ple_w_gate, v_ple_w_proj):
    args = dict(locals())
    weights = {n: args[n] for n in _BIG + _SMALL}
    depth = norm_mix.shape[0]
    nb, seq, _ = x.shape
    t = nb * seq
    assert seq % TQ == 0 and seq % TM == 0 and t % 512 == 0, (nb, seq)
    cx = lax.axis_index("x")
    cy = lax.axis_index("y")
    cc = lax.axis_index("c")
    chip = 2 * cx + cy

    gathers = {}
    token = None
    for i in range(depth):
        for key, lay, parts in _layer_units(i):
            rows = [weights[n][l] for n, l, _ in parts]
            if token is not None:
                rows[0] = rows[0] + token[0, 0]
            if "ln" in lay:
                ln = jnp.stack([gmlp_ln_g[i // 2], gmlp_ln_b[i // 2]]).astype(F32)
                bits = lax.bitcast_convert_type(ln, BF16).reshape(-1)
                rows.append(jnp.pad(bits, (0, 16 * D - bits.size)).reshape(16, D))
            mine = _pack_rows(rows, BF16, pad_to=lay["rows"], slot=True)
            land = lax.dynamic_update_slice(lax.empty((N_CHIPS, lay["rows"], D), BF16), mine, (chip, 0, 0))
            sems, src, land, token = gather_start(land, f"{i}{key}")
            gathers[i, key] = (sems, src, land)
    allw = [None] * depth

    tril = jnp.tril(jnp.ones((GC, GC), F32))
    wm = (gmlp_w_s * tril).astype(BF16)
    wmt = jnp.swapaxes(wm, -1, -2)
    bfull = jnp.repeat(jnp.swapaxes(gmlp_b_s, -1, -2), GD, axis=-1)
    cos, sin = _rope_tables(positions)
    row = lambda g: g.reshape(1, -1)
    gqr = _pad_lanes(mla_q_rope_g)
    gkr = _pad_lanes(mla_k_rope_g)

    h = x.reshape(t, D)
    pt = p.reshape(depth, t, PLE)
    saved = []

    passing = {}

    def arrive(i, key, after):
        sems, src, land = gathers[i, key]
        _, land = gather_wait(sems, src, land, after, f"{i}{key}")
        passing[i, key] = pass_start(land, f"{i}{key}")
        return passing[i, key][3]

    def needed(i, key, after=None):
        sems, src, land, tok = passing.pop((i, key))
        return pass_wait(sems, src, land, tok if after is None else after, f"{i}{key}")[1]

    arrive(0, _layer_units(0)[0][0], token)
    for i in range(depth):
        j = i // 2
        lay = _layer_units(i)[-1][1]
        s = dict(h=h)
        if i % 2 == 0:
            split = i in SPLIT_LAYERS
            olay = _layer_units(i)[0][1]
            odd = needed(i, "odd" if split else "main", None if i == 0 else h)
            wdn = jnp.pad(_odd(odd, olay["wdn"], D // N_CHIPS, LAT).reshape(D, LAT), ((0, 0), (0, LATP - LAT)))
            wuq = _split_uq(_cols_joined(_odd(odd, olay["wuq"], QL, 384))[None])[0]
            wukv = _cols_joined(_odd(odd, olay["wukv"], KVL, 512))
            wp = _cols_joined(_odd(odd, olay["proj"], PLE, 256))
            mla_args = (row(norm_mix[i]), wdn, row(mla_q_lora_g[j]), row(mla_kv_lora_g[j]), wuq, wukv,
                        row(mla_q_nope_g[j]), gqr[j:j + 1], row(mla_k_nope_g[j]), gkr[j:j + 1], cos, sin)
            q, k, v = mla_pre_fwd(h, *mla_args)
            y, lse = flash_fwd(q, k, v, seq)
            if split and i == 0:
                arrive(i, "main", y)
            aw = needed(i, "main", y) if split else odd
            s.update(q=q, k=k, v=v, lse=lse, mla_args=mla_args)
        else:
            aw = needed(i, "main", h)
            ln = lax.bitcast_convert_type(aw[:, lay["ln"]:lay["ln"] + 2].reshape(N_CHIPS, 2, GH // N_CHIPS, 2), F32)
            ln = jnp.transpose(ln, (1, 0, 2)).reshape(2, 1, GH)
            wp = _cols_joined(_odd(aw, lay["proj"], PLE, 256))
            y, pre = gmlp_fwd(h, row(norm_mix[i]), aw, lay, ln[0], ln[1], wm[j], bfull[j])
            s.update(pre=pre, ln=ln)
        allw[i] = aw
        g2 = row(norm_ffn[i])
        if i + 1 < depth:
            for key, _, _ in _layer_units(i + 1):
                g2 = g2 + arrive(i + 1, key, y)[0:1, 0:1]
        h1, h2, hn2, a = mixffn_fwd(h, y, aw, lay, g2)
        h, hn3 = ple_fwd(h2, pt, i, row(norm_ple[i]), aw, lay, wp)
        s.update(y=y, wp=wp, h1=h1, h2=h2, hn2=hn2, a=a, hn3=hn3)
        saved.append(s)

    dh, loss_part = loss_head(h, loss_target.reshape(t, D))
    loss = lax.psum(loss_part[0, 0], ("x", "y", "c"))

    gs = {n: [None] * weights[n].shape[0] for n in _SMALL}
    gw = {n: [None] * weights[n].shape[0] for n in _BIG}
    place = jnp.stack([cc, chip]).astype(jnp.int32)
    scatters = []
    swaps = []
    token = None

    def put(b, row0, shards):
        return lax.dynamic_update_slice(b, shards.reshape(N_CHIPS, -1, D), (0, row0, 0))

    def small_size(n):
        return weights[n].shape[0] * GH if n in ("gmlp_ln_g", "gmlp_ln_b") else weights[n].size

    def small_exchange(names, zero, tag):
        rows = -(-sum(small_size(n) for n in names) // (56 * D)) * 56
        part = [jnp.stack(gs[n]) for n in names]
        part = _pack_rows([part[0] + zero] + part[1:], F32, pad_to=rows, slot=True)
        land = lax.dynamic_update_slice(lax.empty((N_DEV, rows, D), F32), part, (2 * chip + cc, 0, 0))
        return small_start(land, tag)

    def swap(i, key, buf):
        sems, buf, got, tok = swap_start(buf, f"{i}{key}")
        swaps.append((i, key, sems, buf, got))
        return tok

    def swapped(after, zero):
        while swaps:
            i, key, sems, g, got = swaps.pop(0)
            g, got = swap_wait(sems, g, got, after, f"{i}{key}")
            own, sums = chip_sum(place, g, got)
            sems, sums, land, tok = scatter_start(sums, f"{i}{key}")
            scatters.append((i, key, own, sems, sums, land))
            zero = zero + tok[0:1, 0:1]
        return zero

    for i in reversed(range(depth)):
        j = i // 2
        lay = _layer_units(i)[-1][1]
        aw = allw[i]
        s = saved[i]

        g3 = row(norm_ple[i])
        if token is not None:
            g3 = g3 + token[0:1, 0:1]
        dh2, dh2b, dgt, dpp, dg3 = ple_bwd(dh, s["h2"], pt, i, g3, aw, lay, s["wp"])
        gs["norm_ple"][i] = dg3[0]
        buf = mm_tn_into(lay["rows"], s["hn3"], dgt, D // N_CHIPS, lay["gate"], False)
        dproj = _col_shards(mm_tn(pt, dpp, layer=i))
        if "ln" in lay:
            buf = put(buf, lay["ln"], jnp.zeros((N_CHIPS, lay["rows"] - lay["ln"], D), F32))
            buf = put(buf, lay["proj"], dproj)
        dh1, dh1b, du, dg2 = ffn_bwd(dh2, dh2b, s["h1"], s["a"], row(norm_ffn[i]), aw, lay)
        gs["norm_ffn"][i] = dg2[0]
        buf = mm_tn_into(buf, s["a"], dh2b, D, lay["down"], False)
        buf = mm_tn_into(buf, s["hn2"], du, D, lay["up"], True)
        buf = mm_tn_into(buf, s["y"], dh1b, s["y"].shape[1] // N_CHIPS, lay["out"], False)
        g1 = swapped(dh1, row(norm_mix[i]))
        if i % 2 == 0:
            split = i in SPLIT_LAYERS
            do = linear_nt(dh1b, aw, D // N_CHIPS, lay["out"])
            dq, dk, dv = flash_bwd(s["q"], s["k"], s["v"], s["y"], do, s["lse"], seq,
                                   after=swap(i, "main", buf) if split else dh1b)
            g1 = swapped(dq, g1)
            (dh, hn1, cq, ckv, dqp, dkvp, dlat, dg1, dgq, dgkv, dgqn, dgqr, dgkn, dgkr) = mla_pre_bwd(
                dq, dk, dv, dh1, s["h"], g1, *s["mla_args"][1:])
            gs["norm_mix"][i] = dg1[0]
            gs["mla_q_lora_g"][j] = dgq[0]
            gs["mla_kv_lora_g"][j] = dgkv[0]
            gs["mla_q_nope_g"][j] = dgqn[0]
            gs["mla_q_rope_g"][j] = dgqr[0, :DR]
            gs["mla_k_nope_g"][j] = dgkn[0]
            gs["mla_k_rope_g"][j] = dgkr[0, :DR]
            small = [mm_tn(hn1, dlat)[:, :LAT].reshape(N_CHIPS, -1, D), _col_shards(_merge_uq(mm_tn(cq, dqp))),
                     _col_shards(mm_tn(ckv, dkvp)), dproj]
            if split:
                buf = jnp.concatenate(small, axis=1)
            else:
                buf = put(buf, lay["wdn"], jnp.concatenate(small, axis=1))
            key = "odd" if split else "main"
        else:
            dh, hn1, dpre, dws, dbs, dlng, dlnb, dg1 = gmlp_bwd(
                dh1, dh1b, s["h"], s["pre"], g1, aw, lay, s["ln"][0], s["ln"][1], wm[j], wmt[j], bfull[j], tril)
            gs["norm_mix"][i] = dg1[0]
            gs["gmlp_ln_g"][j] = dlng[0]
            gs["gmlp_ln_b"][j] = dlnb[0]
            gs["gmlp_w_s"][j] = dws
            gs["gmlp_b_s"][j] = jnp.sum(dbs.reshape(GC, GG, GD), axis=-1).T
            buf = mm_tn_into(buf, hn1, dpre, D, lay["in"], True)
            key = "main"
        token = swap(i, key, buf)
        if i == 1:
            small_gmlp = small_exchange(_SMALL_GMLP, token[0, 0], "gmlp")
            token = token + small_gmlp[3]
    last = swapped(dh, jnp.zeros((1, 1), F32))
    grad_x = dh.reshape(x.shape)
    small_rest = small_exchange(_SMALL_REST, last[0, 0], "rest")

    after = small_rest[3]
    shares = []
    for i, key, own, sems, sums, land in scatters:
        _, got = scatter_wait(sems, sums, land, after, f"{i}{key}")
        sems, src, full, after = share_start(final_sum(place, own, got), f"{i}{key}")
        shares.append((i, key, sems, src, full))
    where = {n: [None] * weights[n].shape[0] for n in _BIG}
    for i, key, sems, src, full in shares:
        _, after = share_wait(sems, src, full, after, f"{i}{key}")
        for n, l, row0 in dict((k, parts) for k, _, parts in _layer_units(i))[key]:
            where[n][l] = (after, row0)
            if weights[n].shape[-1] != D:
                gw[n][l] = after[row0:row0 + weights[n][l].size // D].reshape(weights[n].shape[1:])
    grads = {n: jnp.stack(gw[n]) for n in _BIG if weights[n].shape[-1] != D}

    tot = []
    for names, (sems, src, land, _), tag in ((_SMALL_REST, small_rest, "rest"), (_SMALL_GMLP, small_gmlp, "gmlp")):
        summed = sum_devices(small_wait(sems, src, land, after, tag)[1]).reshape(-1)
        tot.append(summed[:sum(small_size(n) for n in names)])
    tot = jnp.concatenate(tot)
    off = 0
    for n, sz in ((n, small_size(n)) for n in _SMALL_REST + _SMALL_GMLP):
        gsum = tot[off:off + sz]
        off += sz
        if n in ("gmlp_ln_g", "gmlp_ln_b"):
            gsum = lax.dynamic_slice_in_dim(gsum.reshape(-1, GH), chip * (GH // N_CHIPS), GH // N_CHIPS, axis=1)
        grads[n] = gsum.reshape(weights[n].shape)

    delta, new_m, new_v = {}, {}, {}
    for n in _BIG:
        if weights[n].shape[-1] == D:
            grads[n], delta[n], new_m[n], new_v[n] = adamw_layers(
                weights[n], args["m_" + n], args["v_" + n], [b for b, _ in where[n]], [r for _, r in where[n]])
            continue
        w2 = weights[n].reshape(-1, weights[n].shape[-1])
        d, mn, vn = adamw(w2, grads[n].reshape(w2.shape), args["m_" + n].reshape(w2.shape),
                          args["v_" + n].reshape(w2.shape))
        delta[n], new_m[n], new_v[n] = (a.reshape(weights[n].shape) for a in (d, mn, vn))
    own_sizes = [weights[n].size for n in _SMALL]
    own_rows = -(-sum(own_sizes) // (8 * D)) * 8
    packed = [_pack_rows([src[n] for n in _SMALL], F32, pad_to=own_rows)
              for src in (weights, grads, {n: args["m_" + n] for n in _SMALL}, {n: args["v_" + n] for n in _SMALL})]
    outs = adamw(*packed)
    off = 0
    for n, sz in zip(_SMALL, own_sizes):
        for dst, o in zip((delta, new_m, new_v), outs):
            dst[n] = o.reshape(-1)[off:off + sz].reshape(weights[n].shape)
        off += sz

    order = ["norm_mix", "norm_ffn", "norm_ple", "mla_w_down", "mla_q_lora_g", "mla_kv_lora_g", "mla_w_uq",
             "mla_w_ukv", "mla_q_nope_g", "mla_q_rope_g", "mla_k_nope_g", "mla_k_rope_g", "mla_w_out", "gmlp_w_in",
             "gmlp_ln_g", "gmlp_ln_b", "gmlp_w_s", "gmlp_b_s", "gmlp_w_out", "ffn_w_up", "ffn_w_down", "ple_w_gate",
             "ple_w_proj"]
    return (loss, grad_x, *[grads[n] for n in order], *[delta[n] for n in order], *[new_m[n] for n in order],
            *[new_v[n] for n in order])
```

```python
import functools

import jax
import jax.numpy as jnp
from jax import lax
from jax.experimental import pallas as pl
from jax.experimental.pallas import tpu as pltpu

F32 = jnp.float32
BF16 = jnp.bfloat16
MESH = pl.DeviceIdType.MESH

D = 1024
HEADS = 8
DN = 128
DR = 64
QL = 384
KVL = 256
LAT = 704
LATP = 768
DFF = 4096
GH = 2048
GC = 128
GG = 8
GD = 256
PLE = 256
EPS = 1e-6
ROPE_BASE = 10000.0
SM_SCALE = (DN + DR) ** -0.5
N_CHIPS = 4
LANES = 128

ADAM_LR = 0.001
ADAM_B1 = 0.9
ADAM_B2 = 0.999
ADAM_EPS = 1e-08
ADAM_WD = 0.01
ADAM_STEP = 10

TM = 256
TMB = 512
TQ = 512
TQ_FWD = 512
FWD_HEADS = 2
BWD_HEADS = 2
SUM_ROWS = 256
VMEM_LIMIT = 56 * 1024 * 1024


def _cp(*sem):
    return pltpu.CompilerParams(dimension_semantics=sem, vmem_limit_bytes=VMEM_LIMIT)


def _dot(a, b):
    return jnp.dot(a, b, preferred_element_type=F32)


def _dot_nt(a, b):
    return lax.dot_general(a, b, (((1,), (1,)), ((), ())), preferred_element_type=F32)


def _dot_tn(a, b):
    return lax.dot_general(a, b, (((0,), (0,)), ((), ())), preferred_element_type=F32)


def _rms(x, g, n):
    r = lax.rsqrt(jnp.sum(x * x, axis=-1, keepdims=True) * (1.0 / n) + EPS)
    xhat = x * r
    return xhat * g, xhat, r


def _rms_bwd(dy, g, xhat, r, n):
    dxhat = dy * g
    return r * (dxhat - xhat * (jnp.sum(dxhat * xhat, axis=-1, keepdims=True) * (1.0 / n)))


def _rope(x, c, s):
    return x * c + (pltpu.roll(x, 32, 1) - pltpu.roll(x, 96, 1)) * s


def _rope_t(dy, c, s):
    w = dy * s
    return dy * c + pltpu.roll(w, 96, 1) - pltpu.roll(w, 32, 1)


def _sigmoid(x):
    return 1.0 / (1.0 + jnp.exp(-x))


_GELU_K = 0.7978845608028654
_GELU_C = 0.044715


def _gelu(x):
    return 0.5 * x * (1.0 + jnp.tanh(_GELU_K * (x + _GELU_C * x * x * x)))


def _gelu_and_grad(x):
    x2 = x * x
    t = jnp.tanh(_GELU_K * (x + _GELU_C * x2 * x))
    half = 0.5 * (1.0 + t)
    return x * half, half + 0.5 * x * (1.0 - t * t) * (_GELU_K * (1.0 + 3.0 * _GELU_C * x2))


def _acc_rows(ref, val):
    ref[...] += jnp.broadcast_to(jnp.sum(val, axis=0, keepdims=True), ref.shape)


def _row(tm, c):
    return pl.BlockSpec((tm, c), lambda i: (i, 0))


def _const(shape):
    nd = len(shape)
    return pl.BlockSpec(shape, lambda i: (0,) * nd, pipeline_mode=pl.Buffered(1))


def _wblk(rows, row0):
    assert row0 % rows == 0, (rows, row0)
    return pl.BlockSpec((N_CHIPS, rows, D), lambda i: (0, row0 // rows, 0), pipeline_mode=pl.Buffered(1))


def _rows_joined(w_ref):
    return w_ref[...].reshape(N_CHIPS * w_ref.shape[1], D)


def _sds(shape, dtype):
    return jax.ShapeDtypeStruct(shape, dtype)


def mixffn_fwd(h, y, allw, lay, g2):
    t, k = y.shape

    def body(h_ref, y_ref, wo_ref, g_ref, wu_ref, wd_ref, h1_ref, h2_ref, hn_ref, r_ref):
        h1 = h_ref[...] + _dot(y_ref[...], _rows_joined(wo_ref))
        h1_ref[...] = h1
        yn, _, _ = _rms(h1, g_ref[...], D)
        hn = yn.astype(BF16)
        hn_ref[...] = hn
        f = jnp.zeros((TMB, D), F32)
        for c in range(N_CHIPS):
            r = jnp.maximum(_dot(hn, wu_ref[c]), 0.0)
            r_ref[:, c * D:(c + 1) * D] = r.astype(BF16)
            f = f + _dot((r * r).astype(BF16), wd_ref[c])
        h2_ref[...] = h1 + f

    return pl.pallas_call(
        body, name="mixffn_fwd", grid=(t // TMB,),
        in_specs=[_row(TMB, D), _row(TMB, k), _wblk(k // N_CHIPS, lay["out"]), _const((1, D)), _wblk(D, lay["up"]),
                  _wblk(D, lay["down"])],
        out_specs=[_row(TMB, D), _row(TMB, D), _row(TMB, D), _row(TMB, DFF)],
        out_shape=[_sds((t, D), F32), _sds((t, D), F32), _sds((t, D), BF16), _sds((t, DFF), BF16)],
        compiler_params=_cp("parallel"),
    )(h, y, allw, g2, allw, allw)


def _layer_rows(tm, c, layer):
    return pl.BlockSpec((None, tm, c), lambda i: (layer, i, 0))


def ple_fwd(h2, p, layer, g3, allw, lay, wp):
    t = h2.shape[0]

    def body(h_ref, p_ref, g_ref, wg_ref, wp_ref, h3_ref, hn_ref):
        x = h_ref[...]
        yn, _, _ = _rms(x, g_ref[...], D)
        hn = yn.astype(BF16)
        hn_ref[...] = hn
        gt = _dot(hn, _rows_joined(wg_ref))
        pp = _dot(p_ref[...].astype(BF16), wp_ref[...])
        h3_ref[...] = x + _sigmoid(gt) * pp

    return pl.pallas_call(
        body, name="ple_fwd", grid=(t // TMB,),
        in_specs=[_row(TMB, D), _layer_rows(TMB, PLE, layer), _const((1, D)), _wblk(D // N_CHIPS, lay["gate"]),
                  _const((PLE, D))],
        out_specs=[_row(TMB, D), _row(TMB, D)],
        out_shape=[_sds((t, D), F32), _sds((t, D), BF16)],
        compiler_params=_cp("parallel"),
    )(h2, p, g3, allw, wp)


def _mla_project(h_ref, g1_ref, wdn_ref, gq_ref, gkv_ref, wuq_ref, wukv_ref):
    x = h_ref[...]
    yn, xhat, rx = _rms(x, g1_ref[...], D)
    hn = yn.astype(BF16)
    lat = _dot(hn, wdn_ref[...])
    cq, cqhat, rq = _rms(lat[:, :QL], gq_ref[...], QL)
    ckv, ckvhat, rkv = _rms(lat[:, QL:QL + KVL], gkv_ref[...], KVL)
    kr_raw = lat[:, QL + KVL:]
    cqb = cq.astype(BF16)
    ckvb = ckv.astype(BF16)
    qp = _dot(cqb, wuq_ref[...])
    kvp = _dot(ckvb, wukv_ref[...])
    return dict(xhat=xhat, rx=rx, hn=hn, cqhat=cqhat, rq=rq, ckvhat=ckvhat, rkv=rkv, kr_raw=kr_raw,
                cqb=cqb, ckvb=ckvb, qp=qp, kvp=kvp)


def mla_pre_fwd(h, g1, wdn, gq, gkv, wuq, wukv, gqn, gqr, gkn, gkr, cos, sin):
    t = h.shape[0]

    def body(h_ref, g1_ref, wdn_ref, gq_ref, gkv_ref, wuq_ref, wukv_ref, gqn_ref, gqr_ref, gkn_ref, gkr_ref,
             c_ref, s_ref, q_ref, k_ref, v_ref):
        m = _mla_project(h_ref, g1_ref, wdn_ref, gq_ref, gkv_ref, wuq_ref, wukv_ref)
        c = c_ref[...]
        s = s_ref[...]
        kr, _, _ = _rms(m["kr_raw"], gkr_ref[...], DR)
        krb = _rope(kr, c, s).astype(BF16)
        for hd in range(HEADS):
            qn, _, _ = _rms(m["qp"][:, hd * DN:(hd + 1) * DN], gqn_ref[...], DN)
            qr, _, _ = _rms(m["qp"][:, D + hd * LANES:D + (hd + 1) * LANES], gqr_ref[...], DR)
            q_ref[hd, :, 0:DN] = (qn * SM_SCALE).astype(BF16)
            q_ref[hd, :, DN:2 * DN] = (_rope(qr, c, s) * SM_SCALE).astype(BF16)
            kn, _, _ = _rms(m["kvp"][:, hd * 2 * DN:hd * 2 * DN + DN], gkn_ref[...], DN)
            k_ref[hd, :, 0:DN] = kn.astype(BF16)
            k_ref[hd, :, DN:2 * DN] = krb
            v_ref[hd] = m["kvp"][:, hd * 2 * DN + DN:(hd + 1) * 2 * DN].astype(BF16)

    hb = lambda w: pl.BlockSpec((HEADS, TM, w), lambda i: (0, i, 0))
    return pl.pallas_call(
        body, name="mla_pre_fwd", grid=(t // TM,),
        in_specs=[_row(TM, D), _const((1, D)), _const((D, LATP)), _const((1, QL)), _const((1, KVL)),
                  _const((QL, 2 * D)), _const((KVL, 2 * D)), _const((1, LANES)), _const((1, LANES)),
                  _const((1, LANES)), _const((1, LANES)), _row(TM, LANES), _row(TM, LANES)],
        out_specs=[hb(2 * DN), hb(2 * DN), hb(DN)],
        out_shape=[_sds((HEADS, t, 2 * DN), BF16), _sds((HEADS, t, 2 * DN), BF16), _sds((HEADS, t, DN), BF16)],
        compiler_params=_cp("parallel"),
    )(h, g1, wdn, gq, gkv, wuq, wukv, gqn, gqr, gkn, gkr, cos, sin)


def _diagonal_mask(n=TQ):
    return lax.broadcasted_iota(jnp.int32, (n, n), 1) <= lax.broadcasted_iota(jnp.int32, (n, n), 0)


def flash_fwd(q, k, v, seq):
    t = q.shape[1]
    nb = t // seq
    tq = TQ_FWD
    nq = seq // tq
    hp = FWD_HEADS

    def body(q_ref, k_ref, v_ref, o_ref, lse_ref):
        qi = pl.program_id(2)
        qs = [q_ref[a] for a in range(hp)]

        def step(j, carry, diagonal=False):
            rows = pl.ds(pl.multiple_of(j * tq, tq), tq)
            out = []
            for a in range(hp):
                m, l, acc = carry[a]
                s = _dot_nt(qs[a], k_ref[a, rows, :])
                if diagonal:
                    s = jnp.where(_diagonal_mask(tq), s, -1e30)
                m_new = jnp.maximum(m, jnp.max(s, axis=-1, keepdims=True))
                p = jnp.exp(s - m_new)
                alpha = jnp.exp(m - m_new)
                l = alpha * l + jnp.sum(p, axis=-1, keepdims=True)
                acc = alpha * acc + _dot(p.astype(BF16), v_ref[a, rows, :])
                out.append((m_new, l, acc))
            return tuple(out)

        one = (jnp.full((tq, 1), -1e30, F32), jnp.zeros((tq, 1), F32), jnp.zeros((tq, DN), F32))
        done = step(qi, lax.fori_loop(0, qi, step, (one,) * hp), diagonal=True)
        for a, (m, l, acc) in enumerate(done):
            o_ref[:, a * DN:(a + 1) * DN] = (acc / l).astype(BF16)
            lse_ref[a] = m + jnp.log(l)

    return pl.pallas_call(
        body, name="flash_fwd", grid=(nb, HEADS // hp, nq),
        in_specs=[pl.BlockSpec((hp, tq, 2 * DN), lambda b, h, i: (h, b * nq + i, 0)),
                  pl.BlockSpec((hp, seq, 2 * DN), lambda b, h, i: (h, b, 0)),
                  pl.BlockSpec((hp, seq, DN), lambda b, h, i: (h, b, 0))],
        out_specs=[pl.BlockSpec((tq, hp * DN), lambda b, h, i: (b * nq + i, h)),
                   pl.BlockSpec((hp, tq, 1), lambda b, h, i: (h, b * nq + i, 0))],
        out_shape=[_sds((t, HEADS * DN), BF16), _sds((HEADS, t, 1), F32)],
        compiler_params=_cp("parallel", "parallel", "arbitrary"),
    )(q, k, v)


def _gmlp_in(hn, win_ref):
    pre = [_dot(hn, win_ref[c]) for c in range(N_CHIPS)]
    return jnp.concatenate(pre[:2], axis=1), jnp.concatenate(pre[2:], axis=1)


def gmlp_fwd(h, g1, allw, lay, lng, lnb, wm, bfull):
    t = h.shape[0]

    def body(h_ref, g1_ref, win_ref, lng_ref, lnb_ref, wm_ref, b_ref, y_ref, pre_ref):
        yn, _, _ = _rms(h_ref[...], g1_ref[...], D)
        pre_u, pre_v = _gmlp_in(yn.astype(BF16), win_ref)
        pre_ref[:, :GH] = pre_u.astype(BF16)
        pre_ref[:, GH:] = pre_v.astype(BF16)
        u = _gelu(pre_u)
        v = _gelu(pre_v)
        xc = v - jnp.mean(v, axis=-1, keepdims=True)
        rs = lax.rsqrt(jnp.mean(xc * xc, axis=-1, keepdims=True) + EPS)
        vnb = (xc * rs * lng_ref[...] + lnb_ref[...]).astype(BF16)
        for ch in range(TM // GC):
            rows = slice(ch * GC, (ch + 1) * GC)
            for g in range(GG):
                cols = slice(g * GD, (g + 1) * GD)
                sv = _dot(wm_ref[g], vnb[rows, cols]) + b_ref[:, cols]
                y_ref[rows, cols] = (u[rows, cols] * sv).astype(BF16)

    return pl.pallas_call(
        body, name="gmlp_fwd", grid=(t // TM,),
        in_specs=[_row(TM, D), _const((1, D)), _wblk(D, lay["in"]), _const((1, GH)), _const((1, GH)),
                  _const((GG, GC, GC)), _const((GC, GH))],
        out_specs=[_row(TM, GH), _row(TM, 2 * GH)],
        out_shape=[_sds((t, GH), BF16), _sds((t, 2 * GH), BF16)],
        compiler_params=_cp("parallel"),
    )(h, g1, allw, lng, lnb, wm, bfull)


def loss_head(h, tgt):
    t = h.shape[0]

    def body(h_ref, t_ref, dh_ref, loss_ref):
        @pl.when(pl.program_id(0) == 0)
        def _():
            loss_ref[...] = jnp.zeros_like(loss_ref)

        e = h_ref[...] - t_ref[...]
        dh_ref[...] = e * (1.0 / D)
        part = jnp.sum(jnp.sum(e * e, axis=-1, keepdims=True), axis=0, keepdims=True) * (0.5 / D)
        loss_ref[...] += jnp.broadcast_to(part, loss_ref.shape)

    return pl.pallas_call(
        body, name="loss_head", grid=(t // TMB,),
        in_specs=[_row(TMB, D), _row(TMB, D)],
        out_specs=[_row(TMB, D), _const((8, LANES))],
        out_shape=[_sds((t, D), F32), _sds((8, LANES), F32)],
        compiler_params=_cp("arbitrary"),
    )(h, tgt)


def _zero_at_first_step(*refs):
    @pl.when(pl.program_id(0) == 0)
    def _():
        for r in refs:
            r[...] = jnp.zeros_like(r)


def ple_bwd(dh3, h2, p, layer, g3, allw, lay, wp):
    t = h2.shape[0]

    def body(dh_ref, h_ref, p_ref, g_ref, wg_ref, wp_ref, dh2_ref, dh2b_ref, dgt_ref, dpp_ref, dg_ref):
        _zero_at_first_step(dg_ref)
        dh3v = dh_ref[...]
        x = h_ref[...]
        g = g_ref[...]
        wg = _rows_joined(wg_ref)
        yn, xhat, r = _rms(x, g, D)
        gt = _dot(yn.astype(BF16), wg)
        pp = _dot(p_ref[...].astype(BF16), wp_ref[...])
        sg = _sigmoid(gt)
        dgt = (dh3v * pp * sg * (1.0 - sg)).astype(BF16)
        dgt_ref[...] = dgt
        dpp_ref[...] = (dh3v * sg).astype(BF16)
        dhn = _dot_nt(dgt, wg)
        _acc_rows(dg_ref, dhn * xhat)
        dh2 = dh3v + _rms_bwd(dhn, g, xhat, r, D)
        dh2_ref[...] = dh2
        dh2b_ref[...] = dh2.astype(BF16)

    return pl.pallas_call(
        body, name="ple_bwd", grid=(t // TMB,),
        in_specs=[_row(TMB, D), _row(TMB, D), _layer_rows(TMB, PLE, layer), _const((1, D)),
                  _wblk(D // N_CHIPS, lay["gate"]),
                  _const((PLE, D))],
        out_specs=[_row(TMB, D), _row(TMB, D), _row(TMB, D), _row(TMB, D), _const((8, D))],
        out_shape=[_sds((t, D), F32), _sds((t, D), BF16), _sds((t, D), BF16), _sds((t, D), BF16), _sds((8, D), F32)],
        compiler_params=_cp("arbitrary"),
    )(dh3, h2, p, g3, allw, wp)


def ffn_bwd(dh2, dh2b, h1, r, g2, allw, lay):
    t = h1.shape[0]

    def body(dh_ref, dhb_ref, h_ref, r_ref, g_ref, wu_ref, wd_ref, dh1_ref, dh1b_ref, du_ref, a_ref, dg_ref):
        _zero_at_first_step(dg_ref)
        dhb = dhb_ref[...]
        g = g_ref[...]
        _, xhat, rr = _rms(h_ref[...], g, D)
        dhn = jnp.zeros((TM, D), F32)
        for c in range(N_CHIPS):
            cs = slice(c * D, (c + 1) * D)
            rc = r_ref[:, cs].astype(F32)
            a_ref[:, cs] = (rc * rc).astype(BF16)
            da = _dot_nt(dhb, wd_ref[c])
            du = (da * (2.0 * rc)).astype(BF16)
            du_ref[:, cs] = du
            dhn = dhn + _dot_nt(du, wu_ref[c])
        _acc_rows(dg_ref, dhn * xhat)
        dh1 = dh_ref[...] + _rms_bwd(dhn, g, xhat, rr, D)
        dh1_ref[...] = dh1
        dh1b_ref[...] = dh1.astype(BF16)

    return pl.pallas_call(
        body, name="ffn_bwd", grid=(t // TM,),
        in_specs=[_row(TM, D), _row(TM, D), _row(TM, D), _row(TM, DFF), _const((1, D)), _wblk(D, lay["up"]),
                  _wblk(D, lay["down"])],
        out_specs=[_row(TM, D), _row(TM, D), _row(TM, DFF), _row(TM, DFF), _const((8, D))],
        out_shape=[_sds((t, D), F32), _sds((t, D), BF16), _sds((t, DFF), BF16), _sds((t, DFF), BF16),
                   _sds((8, D), F32)],
        compiler_params=_cp("arbitrary"),
    )(dh2, dh2b, h1, r, g2, allw, allw)


def linear_nt(a, allw, rows, row0):
    t = a.shape[0]
    k = N_CHIPS * rows

    def body(a_ref, w_ref, o_ref):
        o_ref[...] = _dot_nt(a_ref[...], _rows_joined(w_ref)).astype(BF16)

    return pl.pallas_call(
        body, name="linear_nt", grid=(t // TMB,),
        in_specs=[_row(TMB, D), _wblk(rows, row0)],
        out_specs=_row(TMB, k),
        out_shape=_sds((t, k), BF16),
        compiler_params=_cp("parallel"),
    )(a, allw)


def flash_bwd(q, k, v, o, do, lse, seq, after):
    t = q.shape[1]
    nb = t // seq
    nq = seq // TQ
    hp = BWD_HEADS

    def body(q_ref, k_ref, v_ref, o_ref, do_ref, lse_ref, after_ref, dq_ref, dk_ref, dv_ref):
        del after_ref
        kj = pl.program_id(2)

        @pl.when(kj == 0)
        def _():
            dq_ref[...] = jnp.zeros_like(dq_ref)

        def step(i, carry, diagonal=False):
            rows = pl.ds(pl.multiple_of(i * TQ, TQ), TQ)
            out = []
            for a in range(hp):
                dk, dv = carry[a]
                kv = k_ref[a]
                qv = q_ref[a, rows, :]
                dov = do_ref[rows, a * DN:(a + 1) * DN]
                ov = o_ref[rows, a * DN:(a + 1) * DN]
                delta = jnp.sum(dov.astype(F32) * ov.astype(F32), axis=-1, keepdims=True)
                s = _dot_nt(qv, kv)
                if diagonal:
                    s = jnp.where(_diagonal_mask(), s, -1e30)
                p = jnp.exp(s - lse_ref[a, rows, :])
                dp = _dot_nt(dov, v_ref[a])
                ds = (p * (dp - delta)).astype(BF16)
                dv = dv + _dot_tn(p.astype(BF16), dov)
                dk = dk + _dot_tn(ds, qv)
                dq_ref[a, rows, :] += _dot(ds, kv)
                out.append((dk, dv))
            return tuple(out)

        one = (jnp.zeros((TQ, 2 * DN), F32), jnp.zeros((TQ, DN), F32))
        done = lax.fori_loop(kj + 1, nq, step, step(kj, (one,) * hp, diagonal=True))
        for a, (dk, dv) in enumerate(done):
            dk_ref[a] = dk
            dv_ref[a] = dv

    return pl.pallas_call(
        body, name="flash_bwd", grid=(nb, HEADS // hp, nq),
        in_specs=[pl.BlockSpec((hp, seq, 2 * DN), lambda b, h, j: (h, b, 0)),
                  pl.BlockSpec((hp, TQ, 2 * DN), lambda b, h, j: (h, b * nq + j, 0)),
                  pl.BlockSpec((hp, TQ, DN), lambda b, h, j: (h, b * nq + j, 0)),
                  pl.BlockSpec((seq, hp * DN), lambda b, h, j: (b, h)),
                  pl.BlockSpec((seq, hp * DN), lambda b, h, j: (b, h)),
                  pl.BlockSpec((hp, seq, 1), lambda b, h, j: (h, b, 0)), _ANY],
        out_specs=[pl.BlockSpec((hp, seq, 2 * DN), lambda b, h, j: (h, b, 0)),
                   pl.BlockSpec((hp, TQ, 2 * DN), lambda b, h, j: (h, b * nq + j, 0)),
                   pl.BlockSpec((hp, TQ, DN), lambda b, h, j: (h, b * nq + j, 0))],
        out_shape=[_sds((HEADS, t, 2 * DN), F32), _sds((HEADS, t, 2 * DN), F32), _sds((HEADS, t, DN), F32)],
        compiler_params=_cp("parallel", "parallel", "arbitrary"),
    )(q, k, v, o, do, lse, after)


def mla_pre_bwd(dq, dk, dv, dh1, h, g1, wdn, gq, gkv, wuq, wukv, gqn, gqr, gkn, gkr, cos, sin):
    t = h.shape[0]

    def body(dq_ref, dk_ref, dv_ref, dh1_ref, h_ref, g1_ref, wdn_ref, gq_ref, gkv_ref, wuq_ref, wukv_ref,
             gqn_ref, gqr_ref, gkn_ref, gkr_ref, c_ref, s_ref,
             dh_ref, hn_ref, cq_ref, ckv_ref, dqp_ref, dkvp_ref, dlat_ref,
             dg1_ref, dgq_ref, dgkv_ref, dgqn_ref, dgqr_ref, dgkn_ref, dgkr_ref):
        _zero_at_first_step(dg1_ref, dgq_ref, dgkv_ref, dgqn_ref, dgqr_ref, dgkn_ref, dgkr_ref)
        m = _mla_project(h_ref, g1_ref, wdn_ref, gq_ref, gkv_ref, wuq_ref, wukv_ref)
        hn_ref[...] = m["hn"]
        cq_ref[...] = m["cqb"]
        ckv_ref[...] = m["ckvb"]
        c = c_ref[...]
        s = s_ref[...]
        gqn = gqn_ref[...]
        gqr = gqr_ref[...]
        gkn = gkn_ref[...]
        gkr = gkr_ref[...]

        dkr = dk_ref[0, :, DN:2 * DN]
        for hd in range(1, HEADS):
            dkr = dkr + dk_ref[hd, :, DN:2 * DN]
        dkr = _rope_t(dkr, c, s)
        _, krhat, rkr = _rms(m["kr_raw"], gkr, DR)
        _acc_rows(dgkr_ref, dkr * krhat)
        dkr_raw = _rms_bwd(dkr, gkr, krhat, rkr, DR)

        for hd in range(HEADS):
            ncols = slice(hd * DN, (hd + 1) * DN)
            _, xh, r = _rms(m["qp"][:, ncols], gqn, DN)
            dqn = dq_ref[hd, :, 0:DN] * SM_SCALE
            _acc_rows(dgqn_ref, dqn * xh)
            dqp_ref[:, ncols] = _rms_bwd(dqn, gqn, xh, r, DN).astype(BF16)

            rcols = slice(D + hd * LANES, D + (hd + 1) * LANES)
            _, xh, r = _rms(m["qp"][:, rcols], gqr, DR)
            dqr = _rope_t(dq_ref[hd, :, DN:2 * DN] * SM_SCALE, c, s)
            _acc_rows(dgqr_ref, dqr * xh)
            dqp_ref[:, rcols] = _rms_bwd(dqr, gqr, xh, r, DR).astype(BF16)

            kcols = slice(hd * 2 * DN, hd * 2 * DN + DN)
            _, xh, r = _rms(m["kvp"][:, kcols], gkn, DN)
            dkn = dk_ref[hd, :, 0:DN]
            _acc_rows(dgkn_ref, dkn * xh)
            dkvp_ref[:, kcols] = _rms_bwd(dkn, gkn, xh, r, DN).astype(BF16)
            dkvp_ref[:, hd * 2 * DN + DN:(hd + 1) * 2 * DN] = dv_ref[hd].astype(BF16)

        dcq = _dot_nt(dqp_ref[...], wuq_ref[...])
        _acc_rows(dgq_ref, dcq * m["cqhat"])
        dlat_q = _rms_bwd(dcq, gq_ref[...], m["cqhat"], m["rq"], QL)
        dckv = _dot_nt(dkvp_ref[...], wukv_ref[...])
        _acc_rows(dgkv_ref, dckv * m["ckvhat"])
        dlat_kv = _rms_bwd(dckv, gkv_ref[...], m["ckvhat"], m["rkv"], KVL)
        dlat = jnp.concatenate([dlat_q, dlat_kv, dkr_raw], axis=1).astype(BF16)
        dlat_ref[...] = dlat
        dhn = _dot_nt(dlat, wdn_ref[...])
        _acc_rows(dg1_ref, dhn * m["xhat"])
        dh_ref[...] = dh1_ref[...] + _rms_bwd(dhn, g1_ref[...], m["xhat"], m["rx"], D)

    hb = lambda w: pl.BlockSpec((HEADS, TM, w), lambda i: (0, i, 0))
    return pl.pallas_call(
        body, name="mla_pre_bwd", grid=(t // TM,),
        in_specs=[hb(2 * DN), hb(2 * DN), hb(DN), _row(TM, D), _row(TM, D), _const((1, D)), _const((D, LATP)),
                  _const((1, QL)), _const((1, KVL)), _const((QL, 2 * D)), _const((KVL, 2 * D)),
                  _const((1, LANES)), _const((1, LANES)), _const((1, LANES)), _const((1, LANES)),
                  _row(TM, LANES), _row(TM, LANES)],
        out_specs=[_row(TM, D), _row(TM, D), _row(TM, QL), _row(TM, KVL), _row(TM, 2 * D), _row(TM, 2 * D),
                   _row(TM, LATP), _const((8, D)), _const((8, QL)), _const((8, KVL)), _const((8, LANES)),
                   _const((8, LANES)), _const((8, LANES)), _const((8, LANES))],
        out_shape=[_sds((t, D), F32), _sds((t, D), BF16), _sds((t, QL), BF16), _sds((t, KVL), BF16),
                   _sds((t, 2 * D), BF16), _sds((t, 2 * D), BF16), _sds((t, LATP), BF16),
                   _sds((8, D), F32), _sds((8, QL), F32), _sds((8, KVL), F32), _sds((8, LANES), F32),
                   _sds((8, LANES), F32), _sds((8, LANES), F32), _sds((8, LANES), F32)],
        compiler_params=_cp("arbitrary"),
    )(dq, dk, dv, dh1, h, g1, wdn, gq, gkv, wuq, wukv, gqn, gqr, gkn, gkr, cos, sin)


def gmlp_bwd(dh1, dh1b, h, pre, g1, allw, lay, lng, lnb, wm, wmt, bfull, tril):
    t = h.shape[0]

    def body(dh1_ref, dh1b_ref, h_ref, pre_ref, g1_ref, win_ref, lng_ref, lnb_ref, wm_ref, wmt_ref, b_ref,
             wout_ref, tril_ref, dh_ref, hn_ref, dpre_ref, dws_ref, dbs_ref, dlng_ref, dlnb_ref, dg1_ref,
             dvn_s):
        _zero_at_first_step(dws_ref, dbs_ref, dlng_ref, dlnb_ref, dg1_ref)
        g1 = g1_ref[...]
        yn, xhat, rx = _rms(h_ref[...], g1, D)
        hn_ref[...] = yn.astype(BF16)
        dy = _dot_nt(dh1b_ref[...], _rows_joined(wout_ref))
        pre_u = pre_ref[:, :GH].astype(F32)
        pre_v = pre_ref[:, GH:].astype(F32)
        u, gg_u = _gelu_and_grad(pre_u)
        v, gg_v = _gelu_and_grad(pre_v)
        xc = v - jnp.mean(v, axis=-1, keepdims=True)
        rs = lax.rsqrt(jnp.mean(xc * xc, axis=-1, keepdims=True) + EPS)
        vhat = xc * rs
        lng = lng_ref[...]
        vnb = (vhat * lng + lnb_ref[...]).astype(BF16)
        dsv = dy * u
        dsvb = dsv.astype(BF16)
        tril_m = tril_ref[...]
        for ch in range(TM // GC):
            rows = slice(ch * GC, (ch + 1) * GC)
            dbs_ref[...] += dsv[rows, :]
            for g in range(GG):
                cols = slice(g * GD, (g + 1) * GD)
                sv = _dot(wm_ref[g], vnb[rows, cols]) + b_ref[:, cols]
                dpre_ref[rows, cols] = (dy[rows, cols] * sv * gg_u[rows, cols]).astype(BF16)
                dvn_s[rows, cols] = _dot(wmt_ref[g], dsvb[rows, cols])
                dws_ref[g] += _dot_nt(dsvb[rows, cols], vnb[rows, cols]) * tril_m
        dvn = dvn_s[...]
        _acc_rows(dlng_ref, dvn * vhat)
        _acc_rows(dlnb_ref, dvn)
        dvhat = dvn * lng
        dv = rs * (dvhat - jnp.mean(dvhat, axis=-1, keepdims=True)
                   - vhat * jnp.mean(dvhat * vhat, axis=-1, keepdims=True))
        dpre_v = (dv * gg_v).astype(BF16)
        dpre_ref[:, GH:] = dpre_v
        dhn = _dot_nt(dpre_ref[:, 0:D], win_ref[0])
        for c in range(1, N_CHIPS):
            dhn = dhn + _dot_nt(dpre_ref[:, c * D:(c + 1) * D], win_ref[c])
        _acc_rows(dg1_ref, dhn * xhat)
        dh_ref[...] = dh1_ref[...] + _rms_bwd(dhn, g1, xhat, rx, D)

    return pl.pallas_call(
        body, name="gmlp_bwd", grid=(t // TM,),
        in_specs=[_row(TM, D), _row(TM, D), _row(TM, D), _row(TM, 2 * GH), _const((1, D)), _wblk(D, lay["in"]),
                  _const((1, GH)), _const((1, GH)), _const((GG, GC, GC)), _const((GG, GC, GC)), _const((GC, GH)),
                  _wblk(GH // N_CHIPS, lay["out"]), _const((GC, GC))],
        out_specs=[_row(TM, D), _row(TM, D), _row(TM, 2 * GH), _const((GG, GC, GC)), _const((GC, GH)),
                   _const((8, GH)), _const((8, GH)), _const((8, D))],
        out_shape=[_sds((t, D), F32), _sds((t, D), BF16), _sds((t, 2 * GH), BF16), _sds((GG, GC, GC), F32),
                   _sds((GC, GH), F32), _sds((8, GH), F32), _sds((8, GH), F32), _sds((8, D), F32)],
        scratch_shapes=[pltpu.VMEM((TM, GH), F32)],
        compiler_params=_cp("arbitrary"),
    )(dh1, dh1b, h, pre, g1, allw, lng, lnb, wm, wmt, bfull, allw, tril)


def _token_step(t):
    return next(s for s in (2048, 1024, 512) if t % s == 0)


def mm_tn(a, b, layer=None):
    t, k = a.shape[-2:]
    n = b.shape[1]
    tk = min(k, 1024)
    tn = min(n, 1024)
    tt = _token_step(t)
    a_spec = (pl.BlockSpec((tt, tk), lambda i, j, s: (s, i)) if layer is None else
              pl.BlockSpec((None, tt, tk), lambda i, j, s: (layer, s, i)))

    def body(a_ref, b_ref, o_ref):
        @pl.when(pl.program_id(2) == 0)
        def _():
            o_ref[...] = jnp.zeros_like(o_ref)

        o_ref[...] += _dot_tn(a_ref[...].astype(BF16), b_ref[...].astype(BF16))

    return pl.pallas_call(
        body, name="mm_tn", grid=(k // tk, n // tn, t // tt),
        in_specs=[a_spec, pl.BlockSpec((tt, tn), lambda i, j, s: (s, j))],
        out_specs=pl.BlockSpec((tk, tn), lambda i, j, s: (i, j)), out_shape=_sds((k, n), F32),
        compiler_params=_cp("parallel", "parallel", "arbitrary"),
    )(a, b)


def mm_tn_into(buf, a, b, rows, row0, col_sharded):
    t = a.shape[0]
    tt = _token_step(t)
    assert row0 % rows == 0 and a.shape[1] == (rows if col_sharded else N_CHIPS * rows), (rows, row0, a.shape)
    assert b.shape[1] == (N_CHIPS * D if col_sharded else D), b.shape
    grid = (1, N_CHIPS, t // tt) if col_sharded else (N_CHIPS, 1, t // tt)
    fresh = isinstance(buf, int)

    def body(*refs):
        a_ref, b_ref, o_ref = refs[-3:]

        @pl.when(pl.program_id(2) == 0)
        def _():
            o_ref[...] = jnp.zeros_like(o_ref)

        o_ref[...] += _dot_tn(a_ref[...].astype(BF16), b_ref[...].astype(BF16))

    specs = [pl.BlockSpec((tt, rows), lambda i, j, s: (s, i)), pl.BlockSpec((tt, D), lambda i, j, s: (s, j))]
    return pl.pallas_call(
        body, name="mm_tn_into", grid=grid,
        in_specs=specs if fresh else [_ANY] + specs,
        out_specs=pl.BlockSpec((None, rows, D), lambda i, j, s: (i + j, row0 // rows, 0)),
        out_shape=_sds((N_CHIPS, buf, D) if fresh else buf.shape, F32),
        input_output_aliases={} if fresh else {0: 0},
        compiler_params=_cp("parallel", "parallel", "arbitrary"),
    )(*((a, b) if fresh else (buf, a, b)))


def adamw(w, g, m, v):
    rows, cols = w.shape
    tr = rows if rows <= 512 else next(r for r in (512, 384, 256, 128) if rows % r == 0)
    c1 = 1.0 - ADAM_B1 ** ADAM_STEP
    c2 = 1.0 - ADAM_B2 ** ADAM_STEP

    def body(w_ref, g_ref, m_ref, v_ref, d_ref, mo_ref, vo_ref):
        gv = g_ref[...]
        mn = ADAM_B1 * m_ref[...] + (1.0 - ADAM_B1) * gv
        vn = ADAM_B2 * v_ref[...] + (1.0 - ADAM_B2) * (gv * gv)
        mo_ref[...] = mn
        vo_ref[...] = vn
        d_ref[...] = -ADAM_LR * ((mn / c1) / (jnp.sqrt(vn / c2) + ADAM_EPS) + ADAM_WD * w_ref[...])

    spec = pl.BlockSpec((tr, cols), lambda i: (i, 0))
    return pl.pallas_call(
        body, name="adamw", grid=(rows // tr,),
        in_specs=[spec] * 4, out_specs=[spec] * 3, out_shape=[_sds((rows, cols), F32)] * 3,
        compiler_params=_cp("parallel"),
    )(w, g, m, v)


def adamw_layers(w, m, v, bufs, row0s):
    nl, a, _ = w.shape
    tr = min(a, 256)
    c1 = 1.0 - ADAM_B1 ** ADAM_STEP
    c2 = 1.0 - ADAM_B2 ** ADAM_STEP
    assert all(r % tr == 0 for r in row0s) and a % tr == 0, (row0s, a)

    def body(w_ref, m_ref, v_ref, *rest):
        g_refs, (g_ref, d_ref, mo_ref, vo_ref) = rest[:nl], rest[nl:]
        for l in range(nl):
            @pl.when(pl.program_id(0) == l)
            def _(l=l):
                gv = g_refs[l][...]
                g_ref[...] = gv
                mn = ADAM_B1 * m_ref[...] + (1.0 - ADAM_B1) * gv
                vn = ADAM_B2 * v_ref[...] + (1.0 - ADAM_B2) * (gv * gv)
                mo_ref[...] = mn
                vo_ref[...] = vn
                d_ref[...] = -ADAM_LR * ((mn / c1) / (jnp.sqrt(vn / c2) + ADAM_EPS) + ADAM_WD * w_ref[...])

    def rows_of(l, row0):
        return pl.BlockSpec((tr, D), lambda li, i: (jnp.where(li == l, row0 // tr + i, row0 // tr), 0))

    spec = pl.BlockSpec((None, tr, D), lambda li, i: (li, i, 0))
    return pl.pallas_call(
        body, name="adamw_layers", grid=(nl, a // tr),
        in_specs=[spec] * 3 + [rows_of(l, r) for l, r in enumerate(row0s)],
        out_specs=[spec] * 4, out_shape=[_sds(w.shape, F32)] * 4,
        compiler_params=_cp("arbitrary", "arbitrary"),
    )(w, m, v, *bufs)


def _place():
    return lax.axis_index("x"), lax.axis_index("y"), lax.axis_index("c")


def _other_chips(x, y):
    return [(1 - x, y), (x, 1 - y), (1 - x, 1 - y)]


_ANY = pl.BlockSpec(memory_space=pl.ANY)


_HBM = pl.BlockSpec(memory_space=pltpu.HBM)
_SEM = pl.BlockSpec(memory_space=pltpu.SEMAPHORE)
_EFFECT = pltpu.SideEffectType.DATAFLOW_SIDE_EFFECTING
N_ICI = 3


def _exchange_start(name, src, land, copies, n):
    def body(src_ref, land_ref, *outs):
        sems, token = outs[:2 * n], outs[-1]
        for j, (s, d, to) in enumerate(copies(src_ref, land_ref, _place())):
            pltpu.make_async_remote_copy(src_ref=s, dst_ref=d, send_sem=sems[j], recv_sem=sems[n + j],
                                         device_id=to, device_id_type=MESH).start()
        token[...] = jnp.zeros_like(token)

    sem = pltpu.SemaphoreType.DMA(())
    outs = pl.pallas_call(
        body, name=name,
        out_shape=(sem,) * (2 * n) + (pltpu.HBM(src.shape, src.dtype), pltpu.HBM(land.shape, land.dtype),
                                      _sds((8, LANES), F32)),
        in_specs=(_HBM, _HBM),
        out_specs=(_SEM,) * (2 * n) + (_HBM, _HBM, pl.BlockSpec(memory_space=pltpu.VMEM)),
        input_output_aliases={0: 2 * n, 1: 2 * n + 1},
        compiler_params=pltpu.CompilerParams(has_side_effects=_EFFECT),
    )(pltpu.with_memory_space_constraint(src, pltpu.HBM), pltpu.with_memory_space_constraint(land, pltpu.HBM))
    return outs[:2 * n], outs[2 * n], outs[2 * n + 1], outs[-1]


def _exchange_wait(name, sems, src, land, after, arrivals):
    n = len(sems) // 2

    def body(src_ref, land_ref, *rest):
        sems = rest[:2 * n]
        for j, (s, d) in enumerate(arrivals(src_ref, land_ref, _place())):
            cp = pltpu.make_async_remote_copy(src_ref=s, dst_ref=d, send_sem=sems[j], recv_sem=sems[n + j],
                                              device_id=_place(), device_id_type=MESH)
            cp.wait_send()
            cp.wait_recv()

    return pl.pallas_call(
        body, name=name, out_shape=(pltpu.HBM(src.shape, src.dtype), pltpu.HBM(land.shape, land.dtype)),
        in_specs=(_HBM, _HBM) + (_SEM,) * (2 * n) + (_ANY,), out_specs=(_HBM, _HBM),
        input_output_aliases={0: 0, 1: 1},
        compiler_params=pltpu.CompilerParams(has_side_effects=_EFFECT),
    )(src, land, *sems, after)


def _halves(c, hh):
    return pl.ds(pl.multiple_of(c * hh, 16), hh), pl.ds(pl.multiple_of((1 - c) * hh, 16), hh)


def gather_start(land, tag):
    _, rr, _ = land.shape
    assert rr % 32 == 0, rr

    def copies(_, land_ref, place):
        x, y, c = place
        mine = land_ref.at[2 * x + y, _halves(c, rr // 2)[0]]
        return [(mine, mine, (cx, cy, c)) for cx, cy in _other_chips(x, y)]

    return _exchange_start(f"gather_start_{tag}", jnp.zeros((8, LANES), F32), land, copies, N_ICI)


def gather_wait(sems, src, land, after, tag):
    def arrivals(_, land_ref, place):
        x, y, c = place
        half = _halves(c, land.shape[1] // 2)[0]
        return [(land_ref.at[2 * x + y, half], land_ref.at[2 * cx + cy, half]) for cx, cy in _other_chips(x, y)]

    return _exchange_wait(f"gather_wait_{tag}", sems, src, land, after, arrivals)


def pass_start(land, tag):
    def copies(_, land_ref, place):
        x, y, c = place
        half = _halves(c, land.shape[1] // 2)[0]
        return [(land_ref.at[2 * cx + cy, half], land_ref.at[2 * cx + cy, half], (x, y, 1 - c))
                for cx, cy in _other_chips(x, y)]

    return _exchange_start(f"pass_start_{tag}", jnp.zeros((8, LANES), F32), land, copies, N_ICI)


def pass_wait(sems, src, land, after, tag):
    def arrivals(_, land_ref, place):
        x, y, c = place
        mine, other = _halves(c, land.shape[1] // 2)
        return [(land_ref.at[2 * cx + cy, mine], land_ref.at[2 * cx + cy, other]) for cx, cy in _other_chips(x, y)]

    return _exchange_wait(f"pass_wait_{tag}", sems, src, land, after, arrivals)


def swap_start(g, tag):
    _, rr, cc = g.shape

    def copies(g_ref, got_ref, place):
        x, y, c = place
        other = _halves(c, rr // 2)[1]
        return [(g_ref.at[k, other], got_ref.at[k], (x, y, 1 - c)) for k in range(N_CHIPS)]

    return _exchange_start(f"swap_start_{tag}", g, lax.empty((N_CHIPS, rr // 2, cc), g.dtype), copies, N_CHIPS)


def swap_wait(sems, g, got, after, tag):
    def arrivals(g_ref, got_ref, place):
        other = _halves(place[2], g.shape[1] // 2)[1]
        return [(g_ref.at[k, other], got_ref.at[k]) for k in range(N_CHIPS)]

    return _exchange_wait(f"swap_wait_{tag}", sems, g, got, after, arrivals)


def chip_sum(place, g32, got):
    _, rr, cc = g32.shape
    hh = rr // 2
    tr = SUM_ROWS
    assert rr % 2 == 0 and hh % tr == 0, (rr, tr)
    nb = hh // tr

    def body(place_ref, g_ref, got_ref, own_ref, all_ref):
        s = g_ref[...] + got_ref[...].astype(F32)
        all_ref[...] = s.astype(BF16)
        own_ref[...] = g_ref[place_ref[1]] + got_ref[place_ref[1]].astype(F32)

    return pl.pallas_call(
        body, name="chip_sum",
        grid_spec=pltpu.PrefetchScalarGridSpec(
            num_scalar_prefetch=1, grid=(nb,),
            in_specs=[pl.BlockSpec((N_CHIPS, tr, cc), lambda i, pr: (0, pr[0] * nb + i, 0)),
                      pl.BlockSpec((N_CHIPS, tr, cc), lambda i, pr: (0, i, 0))],
            out_specs=[pl.BlockSpec((tr, cc), lambda i, pr: (i, 0)),
                       pl.BlockSpec((N_CHIPS, tr, cc), lambda i, pr: (0, i, 0))]),
        out_shape=[_sds((hh, cc), F32), _sds((N_CHIPS, hh, cc), BF16)],
        compiler_params=_cp("parallel"),
    )(place, g32, got)


def _scatter_copies(s_ref, land_ref, place):
    x, y, c = place
    return [(s_ref.at[2 * cx + cy], land_ref.at[j], (cx, cy, c)) for j, (cx, cy) in enumerate(_other_chips(x, y))]


def scatter_start(s, tag):
    return _exchange_start(f"scatter_start_{tag}", s, lax.empty((N_ICI,) + s.shape[1:], s.dtype), _scatter_copies, N_ICI)


def scatter_wait(sems, s, land, after, tag):
    return _exchange_wait(f"scatter_wait_{tag}", sems, s, land, after,
                          lambda s_ref, land_ref, place: [(a, b) for a, b, _ in _scatter_copies(s_ref, land_ref, place)])


def final_sum(place, own, got):
    hh, cc = own.shape
    tr = SUM_ROWS
    assert hh % tr == 0, (hh, tr)
    nb = hh // tr

    def body(place_ref, own_ref, got_ref, o_ref):
        del place_ref
        o_ref[...] = ((own_ref[...] + got_ref[0].astype(F32)) + got_ref[1].astype(F32)) + got_ref[2].astype(F32)

    return pl.pallas_call(
        body, name="final_sum",
        grid_spec=pltpu.PrefetchScalarGridSpec(
            num_scalar_prefetch=1, grid=(nb,),
            in_specs=[pl.BlockSpec((tr, cc), lambda i, pr: (i, 0)), pl.BlockSpec((3, tr, cc), lambda i, pr: (0, i, 0))],
            out_specs=pl.BlockSpec((tr, cc), lambda i, pr: (pr[0] * nb + i, 0))),
        out_shape=_sds((2 * hh, cc), F32),
        compiler_params=_cp("parallel"),
    )(place, own, got)


def share_start(f, tag):
    def copies(_, f_ref, place):
        x, y, c = place
        mine = f_ref.at[_halves(c, f.shape[0] // 2)[0]]
        return [(mine, mine, (x, y, 1 - c))]

    return _exchange_start(f"share_start_{tag}", jnp.zeros((8, LANES), F32), f, copies, 1)


def share_wait(sems, src, f, after, tag):
    def arrivals(_, f_ref, place):
        mine, other = _halves(place[2], f.shape[0] // 2)
        return [(f_ref.at[mine], f_ref.at[other])]

    return _exchange_wait(f"share_wait_{tag}", sems, src, f, after, arrivals)


N_DEV = 8


def _peers(place):
    x, y, c = place
    return [((1 - x) if r & 4 else x, (1 - y) if r & 2 else y, (1 - c) if r & 1 else c) for r in range(1, N_DEV)]


def _device_index(place):
    x, y, c = place
    return 4 * x + 2 * y + c


def small_start(land, tag):
    def copies(_, land_ref, place):
        mine = land_ref.at[_device_index(place)]
        return [(mine, mine, to) for to in _peers(place)]

    return _exchange_start(f"small_start_{tag}", jnp.zeros((8, LANES), F32), land, copies, N_DEV - 1)


def small_wait(sems, src, land, after, tag):
    def arrivals(_, land_ref, place):
        return [(land_ref.at[_device_index(place)], land_ref.at[_device_index(peer)]) for peer in _peers(place)]

    return _exchange_wait(f"small_wait_{tag}", sems, src, land, after, arrivals)


def sum_devices(land):
    _, rr, cc = land.shape
    tr = 56
    assert rr % tr == 0, rr

    def body(l_ref, o_ref):
        acc = l_ref[0]
        for d in range(1, N_DEV):
            acc = acc + l_ref[d]
        o_ref[...] = acc

    return pl.pallas_call(
        body, name="sum_devices", grid=(rr // tr,),
        in_specs=[pl.BlockSpec((N_DEV, tr, cc), lambda i: (0, i, 0))],
        out_specs=pl.BlockSpec((tr, cc), lambda i: (i, 0)), out_shape=_sds((rr, cc), F32),
        compiler_params=_cp("parallel"),
    )(land)


_BIG = ["mla_w_down", "mla_w_uq", "mla_w_ukv", "mla_w_out", "gmlp_w_in", "gmlp_w_out", "ffn_w_up", "ffn_w_down",
        "ple_w_gate", "ple_w_proj"]
_SMALL_REST = ["norm_mix", "norm_ffn", "norm_ple", "mla_q_lora_g", "mla_kv_lora_g", "mla_q_nope_g", "mla_q_rope_g",
               "mla_k_nope_g", "mla_k_rope_g"]
_SMALL_GMLP = ["gmlp_ln_g", "gmlp_ln_b", "gmlp_w_s", "gmlp_b_s"]
_SMALL = _SMALL_REST + _SMALL_GMLP

_LAY_MLA = dict(up=0, down=1024, out=2048, gate=2304, wdn=2560, wuq=2736, wukv=2880, proj=3008, rows=3072)
_LAY_MLA_MAIN = dict(up=0, down=1024, out=2048, gate=2304, rows=2560)
_LAY_MLA_ODD = dict(wdn=0, wuq=176, wukv=320, proj=448, rows=512)
_LAY_GMLP = {"up": 0, "down": 1024, "in": 2048, "out": 3072, "gate": 3584, "proj": 3840, "ln": 3904, "rows": 4096}
SPLIT_LAYERS = (0,)


def _layer_units(i):
    j = i // 2
    if i % 2 == 0:
        odd, lay = (_LAY_MLA_ODD, _LAY_MLA_MAIN) if i in SPLIT_LAYERS else (_LAY_MLA, _LAY_MLA)
        small = [("mla_w_down", j, odd["wdn"]), ("mla_w_uq", j, odd["wuq"]), ("mla_w_ukv", j, odd["wukv"]),
                 ("ple_w_proj", i, odd["proj"])]
        large = [("ffn_w_up", i, lay["up"]), ("ffn_w_down", i, lay["down"]), ("mla_w_out", j, lay["out"]),
                 ("ple_w_gate", i, lay["gate"])]
        return [("odd", odd, small), ("main", lay, large)] if i in SPLIT_LAYERS else [("main", lay, large + small)]
    lay = _LAY_GMLP
    return [("main", lay, [("ffn_w_up", i, lay["up"]), ("ffn_w_down", i, lay["down"]), ("gmlp_w_in", j, lay["in"]),
                           ("gmlp_w_out", j, lay["out"]), ("ple_w_gate", i, lay["gate"]),
                           ("ple_w_proj", i, lay["proj"])])]


def _pack_rows(parts, dtype, pad_to=None, slot=False):
    size = sum(p.size for p in parts)
    tail = [] if pad_to is None or pad_to * D == size else [jnp.zeros((pad_to * D - size,), dtype)]
    shape = (1, -1, D) if slot else (-1, D)
    if all(p.size % D == 0 for p in parts + tail):
        return jnp.concatenate([p.astype(dtype).reshape(shape) for p in parts + tail], axis=len(shape) - 2)
    return jnp.concatenate([p.astype(dtype).reshape(-1) for p in parts + tail]).reshape(shape)


def _odd(allw, row0, a, b):
    return allw[:, row0:row0 + a * b // D].reshape(N_CHIPS, a, b)


def _cols_joined(s):
    return jnp.transpose(s, (1, 0, 2)).reshape(s.shape[1], N_CHIPS * s.shape[2])


def _col_shards(full):
    a, bb = full.shape
    return jnp.transpose(full.reshape(a, N_CHIPS, bb // N_CHIPS), (1, 0, 2)).reshape(N_CHIPS, -1, D)


def _pad_lanes(g):
    return jnp.pad(g, ((0, 0), (0, LANES - g.shape[1])))


def _split_uq(wuq):
    l = wuq.shape[0]
    w = wuq.reshape(l, QL, HEADS, DN + DR)
    nope = w[..., :DN].reshape(l, QL, HEADS * DN)
    rope = jnp.pad(w[..., DN:], ((0, 0), (0, 0), (0, 0), (0, LANES - DR))).reshape(l, QL, HEADS * LANES)
    return jnp.concatenate([nope, rope], axis=-1)


def _merge_uq(d):
    nope = d[:, :HEADS * DN].reshape(QL, HEADS, DN)
    rope = d[:, HEADS * DN:].reshape(QL, HEADS, LANES)[..., :DR]
    return jnp.concatenate([nope, rope], axis=-1).reshape(QL, HEADS * (DN + DR))


def _rope_tables(positions):
    inv_freq = ROPE_BASE ** (-(jnp.arange(0, DR, 2, dtype=F32) / DR))
    ang = positions.reshape(-1).astype(F32)[:, None] * inv_freq
    z = jnp.zeros((ang.shape[0], LANES - DR), F32)
    return (jnp.concatenate([jnp.cos(ang), jnp.cos(ang), z], axis=1),
            jnp.concatenate([jnp.sin(ang), jnp.sin(ang), z], axis=1))


def kernel(x, p, positions, norm_mix, norm_ffn, norm_ple, mla_w_down, mla_q_lora_g, mla_kv_lora_g, mla_w_uq, mla_w_ukv, mla_q_nope_g, mla_q_rope_g, mla_k_nope_g, mla_k_rope_g, mla_w_out, gmlp_w_in, gmlp_ln_g, gmlp_ln_b, gmlp_w_s, gmlp_b_s, gmlp_w_out, ffn_w_up, ffn_w_down, ple_w_gate, ple_w_proj, loss_target, m_norm_mix, m_norm_ffn, m_norm_ple, m_mla_w_down, m_mla_q_lora_g, m_mla_kv_lora_g, m_mla_w_uq, m_mla_w_ukv, m_mla_q_nope_g, m_mla_q_rope_g, m_mla_k_nope_g, m_mla_k_rope_g, m_mla_w_out, m_gmlp_w_in, m_gmlp_ln_g, m_gmlp_ln_b, m_gmlp_w_s, m_gmlp_b_s, m_gmlp_w_out, m_ffn_w_up, m_ffn_w_down, m_ple_w_gate, m_ple_w_proj, v_norm_mix, v_norm_ffn, v_norm_ple, v_mla_w_down, v_mla_q_lora_g, v_mla_kv_lora_g, v_mla_w_uq, v_mla_w_ukv, v_mla_q_nope_g, v_mla_q_rope_g, v_mla_k_nope_g, v_mla_k_rope_g, v_mla_w_out, v_gmlp_w_in, v_gmlp_ln_g, v_gmlp_ln_b, v_gmlp_w_s, v_gmlp_b_s, v_gmlp_w_out, v_ffn_w_up, v_ffn_w_down, v_ple_w_gate, v_ple_w_proj):
    args = dict(locals())
    weights = {n: args[n] for n in _BIG + _SMALL}
    depth = norm_mix.shape[0]
    nb, seq, _ = x.shape
    t = nb * seq
    assert seq % TQ == 0 and seq % TM == 0 and t % 512 == 0, (nb, seq)
    cx = lax.axis_index("x")
    cy = lax.axis_index("y")
    cc = lax.axis_index("c")
    chip = 2 * cx + cy

    gathers = {}
    token = None
    for i in range(depth):
        for key, lay, parts in _layer_units(i):
            rows = [weights[n][l] for n, l, _ in parts]
            if token is not None:
                rows[0] = rows[0] + token[0, 0]
            if "ln" in lay:
                ln = jnp.stack([gmlp_ln_g[i // 2], gmlp_ln_b[i // 2]]).astype(F32)
                bits = lax.bitcast_convert_type(ln, BF16).reshape(-1)
                rows.append(jnp.pad(bits, (0, 16 * D - bits.size)).reshape(16, D))
            mine = _pack_rows(rows, BF16, pad_to=lay["rows"], slot=True)
            land = lax.dynamic_update_slice(lax.empty((N_CHIPS, lay["rows"], D), BF16), mine, (chip, 0, 0))
            sems, src, land, token = gather_start(land, f"{i}{key}")
            gathers[i, key] = (sems, src, land)
    allw = [None] * depth

    tril = jnp.tril(jnp.ones((GC, GC), F32))
    wm = (gmlp_w_s * tril).astype(BF16)
    wmt = jnp.swapaxes(wm, -1, -2)
    bfull = jnp.repeat(jnp.swapaxes(gmlp_b_s, -1, -2), GD, axis=-1)
    cos, sin = _rope_tables(positions)
    row = lambda g: g.reshape(1, -1)
    gqr = _pad_lanes(mla_q_rope_g)
    gkr = _pad_lanes(mla_k_rope_g)

    h = x.reshape(t, D)
    pt = p.reshape(depth, t, PLE)
    saved = []

    passing = {}

    def arrive(i, key, after):
        sems, src, land = gathers[i, key]
        _, land = gather_wait(sems, src, land, after, f"{i}{key}")
        passing[i, key] = pass_start(land, f"{i}{key}")
        return passing[i, key][3]

    def needed(i, key, after=None):
        sems, src, land, tok = passing.pop((i, key))
        return pass_wait(sems, src, land, tok if after is None else after, f"{i}{key}")[1]

    arrive(0, _layer_units(0)[0][0], token)
    for i in range(depth):
        j = i // 2
        lay = _layer_units(i)[-1][1]
        s = dict(h=h)
        if i % 2 == 0:
            split = i in SPLIT_LAYERS
            olay = _layer_units(i)[0][1]
            odd = needed(i, "odd" if split else "main", None if i == 0 else h)
            wdn = jnp.pad(_odd(odd, olay["wdn"], D // N_CHIPS, LAT).reshape(D, LAT), ((0, 0), (0, LATP - LAT)))
            wuq = _split_uq(_cols_joined(_odd(odd, olay["wuq"], QL, 384))[None])[0]
            wukv = _cols_joined(_odd(odd, olay["wukv"], KVL, 512))
            wp = _cols_joined(_odd(odd, olay["proj"], PLE, 256))
            mla_args = (row(norm_mix[i]), wdn, row(mla_q_lora_g[j]), row(mla_kv_lora_g[j]), wuq, wukv,
                        row(mla_q_nope_g[j]), gqr[j:j + 1], row(mla_k_nope_g[j]), gkr[j:j + 1], cos, sin)
            q, k, v = mla_pre_fwd(h, *mla_args)
            y, lse = flash_fwd(q, k, v, seq)
            if split and i == 0:
                arrive(i, "main", y)
            aw = needed(i, "main", y) if split else odd
            s.update(q=q, k=k, v=v, lse=lse, mla_args=mla_args)
        else:
            aw = needed(i, "main", h)
            ln = lax.bitcast_convert_type(aw[:, lay["ln"]:lay["ln"] + 2].reshape(N_CHIPS, 2, GH // N_CHIPS, 2), F32)
            ln = jnp.transpose(ln, (1, 0, 2)).reshape(2, 1, GH)
            wp = _cols_joined(_odd(aw, lay["proj"], PLE, 256))
            y, pre = gmlp_fwd(h, row(norm_mix[i]), aw, lay, ln[0], ln[1], wm[j], bfull[j])
            s.update(pre=pre, ln=ln)
        allw[i] = aw
        g2 = row(norm_ffn[i])
        if i + 1 < depth:
            for key, _, _ in _layer_units(i + 1):
                g2 = g2 + arrive(i + 1, key, y)[0:1, 0:1]
        h1, h2, hn2, r = mixffn_fwd(h, y, aw, lay, g2)
        h, hn3 = ple_fwd(h2, pt, i, row(norm_ple[i]), aw, lay, wp)
        s.update(y=y, wp=wp, h1=h1, h2=h2, hn2=hn2, r=r, hn3=hn3)
        saved.append(s)

    dh, loss_part = loss_head(h, loss_target.reshape(t, D))

    gs = {n: [None] * weights[n].shape[0] for n in _SMALL}
    gw = {n: [None] * weights[n].shape[0] for n in _BIG}
    place = jnp.stack([cc, chip]).astype(jnp.int32)
    scatters = []
    swaps = []
    token = None

    def put(b, row0, shards):
        return lax.dynamic_update_slice(b, shards.reshape(N_CHIPS, -1, D), (0, row0, 0))

    def small_size(n):
        return weights[n].shape[0] * GH if n in ("gmlp_ln_g", "gmlp_ln_b") else weights[n].size

    def small_exchange(names, zero, tag, extra=()):
        rows = -(-(sum(small_size(n) for n in names) + sum(e.size for e in extra)) // (56 * D)) * 56
        part = [jnp.stack(gs[n]) for n in names] + list(extra)
        part = _pack_rows([part[0] + zero] + part[1:], F32, pad_to=rows, slot=True)
        land = lax.dynamic_update_slice(lax.empty((N_DEV, rows, D), F32), part, (2 * chip + cc, 0, 0))
        return small_start(land, tag)

    def swap(i, key, buf):
        sems, buf, got, tok = swap_start(buf, f"{i}{key}")
        swaps.append((i, key, sems, buf, got))
        return tok

    def swapped(after, zero):
        while swaps:
            i, key, sems, g, got = swaps.pop(0)
            g, got = swap_wait(sems, g, got, after, f"{i}{key}")
            own, sums = chip_sum(place, g, got)
            sems, sums, land, tok = scatter_start(sums, f"{i}{key}")
            scatters.append((i, key, own, sems, sums, land))
            zero = zero + tok[0:1, 0:1]
        return zero

    for i in reversed(range(depth)):
        j = i // 2
        lay = _layer_units(i)[-1][1]
        aw = allw[i]
        s = saved[i]

        g3 = row(norm_ple[i])
        if token is not None:
            g3 = g3 + token[0:1, 0:1]
        dh2, dh2b, dgt, dpp, dg3 = ple_bwd(dh, s["h2"], pt, i, g3, aw, lay, s["wp"])
        gs["norm_ple"][i] = dg3[0]
        buf = mm_tn_into(lay["rows"], s["hn3"], dgt, D // N_CHIPS, lay["gate"], False)
        dproj = _col_shards(mm_tn(pt, dpp, layer=i))
        if "ln" in lay:
            buf = put(buf, lay["ln"], jnp.zeros((N_CHIPS, lay["rows"] - lay["ln"], D), F32))
            buf = put(buf, lay["proj"], dproj)
        dh1, dh1b, du, a, dg2 = ffn_bwd(dh2, dh2b, s["h1"], s["r"], row(norm_ffn[i]), aw, lay)
        gs["norm_ffn"][i] = dg2[0]
        buf = mm_tn_into(buf, a, dh2b, D, lay["down"], False)
        buf = mm_tn_into(buf, s["hn2"], du, D, lay["up"], True)
        buf = mm_tn_into(buf, s["y"], dh1b, s["y"].shape[1] // N_CHIPS, lay["out"], False)
        g1 = swapped(dh1, row(norm_mix[i]))
        if i % 2 == 0:
            split = i in SPLIT_LAYERS
            do = linear_nt(dh1b, aw, D // N_CHIPS, lay["out"])
            dq, dk, dv = flash_bwd(s["q"], s["k"], s["v"], s["y"], do, s["lse"], seq,
                                   after=swap(i, "main", buf) if split else dh1b)
            g1 = swapped(dq, g1)
            (dh, hn1, cq, ckv, dqp, dkvp, dlat, dg1, dgq, dgkv, dgqn, dgqr, dgkn, dgkr) = mla_pre_bwd(
                dq, dk, dv, dh1, s["h"], g1, *s["mla_args"][1:])
            gs["norm_mix"][i] = dg1[0]
            gs["mla_q_lora_g"][j] = dgq[0]
            gs["mla_kv_lora_g"][j] = dgkv[0]
            gs["mla_q_nope_g"][j] = dgqn[0]
            gs["mla_q_rope_g"][j] = dgqr[0, :DR]
            gs["mla_k_nope_g"][j] = dgkn[0]
            gs["mla_k_rope_g"][j] = dgkr[0, :DR]
            small = [mm_tn(hn1, dlat)[:, :LAT].reshape(N_CHIPS, -1, D), _col_shards(_merge_uq(mm_tn(cq, dqp))),
                     _col_shards(mm_tn(ckv, dkvp)), dproj]
            if split:
                buf = jnp.concatenate(small, axis=1)
            else:
                buf = put(buf, lay["wdn"], jnp.concatenate(small, axis=1))
            key = "odd" if split else "main"
        else:
            dh, hn1, dpre, dws, dbs, dlng, dlnb, dg1 = gmlp_bwd(
                dh1, dh1b, s["h"], s["pre"], g1, aw, lay, s["ln"][0], s["ln"][1], wm[j], wmt[j], bfull[j], tril)
            gs["norm_mix"][i] = dg1[0]
            gs["gmlp_ln_g"][j] = dlng[0]
            gs["gmlp_ln_b"][j] = dlnb[0]
            gs["gmlp_w_s"][j] = dws
            gs["gmlp_b_s"][j] = jnp.sum(dbs.reshape(GC, GG, GD), axis=-1).T
            buf = mm_tn_into(buf, hn1, dpre, D, lay["in"], True)
            key = "main"
        token = swap(i, key, buf)
        if i == 1:
            small_gmlp = small_exchange(_SMALL_GMLP, token[0, 0], "gmlp")
            token = token + small_gmlp[3]
    last = swapped(dh, jnp.zeros((1, 1), F32))
    grad_x = dh.reshape(x.shape)
    small_rest = small_exchange(_SMALL_REST, last[0, 0], "rest", extra=[loss_part[0, 0:1]])

    after = small_rest[3]
    shares = []
    for i, key, own, sems, sums, land in scatters:
        _, got = scatter_wait(sems, sums, land, after, f"{i}{key}")
        sems, src, full, after = share_start(final_sum(place, own, got), f"{i}{key}")
        shares.append((i, key, sems, src, full))
    where = {n: [None] * weights[n].shape[0] for n in _BIG}
    for i, key, sems, src, full in shares:
        _, after = share_wait(sems, src, full, after, f"{i}{key}")
        for n, l, row0 in dict((k, parts) for k, _, parts in _layer_units(i))[key]:
            where[n][l] = (after, row0)
            if weights[n].shape[-1] != D:
                gw[n][l] = after[row0:row0 + weights[n][l].size // D].reshape(weights[n].shape[1:])
    grads = {n: jnp.stack(gw[n]) for n in _BIG if weights[n].shape[-1] != D}

    tot = []
    for names, (sems, src, land, _), tag in ((_SMALL_REST, small_rest, "rest"), (_SMALL_GMLP, small_gmlp, "gmlp")):
        summed = sum_devices(small_wait(sems, src, land, after, tag)[1]).reshape(-1)
        tot.append(summed[:sum(small_size(n) for n in names)])
        if tag == "rest":
            loss = summed[tot[-1].size]
    tot = jnp.concatenate(tot)
    off = 0
    for n, sz in ((n, small_size(n)) for n in _SMALL_REST + _SMALL_GMLP):
        gsum = tot[off:off + sz]
        off += sz
        if n in ("gmlp_ln_g", "gmlp_ln_b"):
            gsum = lax.dynamic_slice_in_dim(gsum.reshape(-1, GH), chip * (GH // N_CHIPS), GH // N_CHIPS, axis=1)
        grads[n] = gsum.reshape(weights[n].shape)

    delta, new_m, new_v = {}, {}, {}
    for n in _BIG:
        if weights[n].shape[-1] == D:
            grads[n], delta[n], new_m[n], new_v[n] = adamw_layers(
                weights[n], args["m_" + n], args["v_" + n], [b for b, _ in where[n]], [r for _, r in where[n]])
            continue
        w2 = weights[n].reshape(-1, weights[n].shape[-1])
        d, mn, vn = adamw(w2, grads[n].reshape(w2.shape), args["m_" + n].reshape(w2.shape),
                          args["v_" + n].reshape(w2.shape))
        delta[n], new_m[n], new_v[n] = (a.reshape(weights[n].shape) for a in (d, mn, vn))
    own_sizes = [weights[n].size for n in _SMALL]
    own_rows = -(-sum(own_sizes) // (8 * D)) * 8
    packed = [_pack_rows([src[n] for n in _SMALL], F32, pad_to=own_rows)
              for src in (weights, grads, {n: args["m_" + n] for n in _SMALL}, {n: args["v_" + n] for n in _SMALL})]
    outs = adamw(*packed)
    off = 0
    for n, sz in zip(_SMALL, own_sizes):
        for dst, o in zip((delta, new_m, new_v), outs):
            dst[n] = o.reshape(-1)[off:off + sz].reshape(weights[n].shape)
        off += sz

    order = ["norm_mix", "norm_ffn", "norm_ple", "mla_w_down", "mla_q_lora_g", "mla_kv_lora_g", "mla_w_uq",
             "mla_w_ukv", "mla_q_nope_g", "mla_q_rope_g", "mla_k_nope_g", "mla_k_rope_g", "mla_w_out", "gmlp_w_in",
             "gmlp_ln_g", "gmlp_ln_b", "gmlp_w_s", "gmlp_b_s", "gmlp_w_out", "ffn_w_up", "ffn_w_down", "ple_w_gate",
             "ple_w_proj"]
    return (loss, grad_x, *[grads[n] for n in order], *[delta[n] for n in order], *[new_m[n] for n in order],
            *[new_v[n] for n in order])
```

```python
import functools

import jax
import jax.numpy as jnp
from jax import lax
from jax.experimental import pallas as pl
from jax.experimental.pallas import tpu as pltpu

F32 = jnp.float32
BF16 = jnp.bfloat16
MESH = pl.DeviceIdType.MESH

D = 1024
HEADS = 8
DN = 128
DR = 64
QL = 384
KVL = 256
LAT = 704
LATP = 768
DFF = 4096
GH = 2048
GC = 128
GG = 8
GD = 256
PLE = 256
EPS = 1e-6
ROPE_BASE = 10000.0
SM_SCALE = (DN + DR) ** -0.5
N_CHIPS = 4
LANES = 128

ADAM_LR = 0.001
ADAM_B1 = 0.9
ADAM_B2 = 0.999
ADAM_EPS = 1e-08
ADAM_WD = 0.01
ADAM_STEP = 10

TM = 256
TMB = 512
TQ = 512
TQ_FWD = 512
FWD_HEADS = 2
BWD_HEADS = 2
SUM_ROWS = 256
VMEM_LIMIT = 56 * 1024 * 1024


def _cp(*sem):
    return pltpu.CompilerParams(dimension_semantics=sem, vmem_limit_bytes=VMEM_LIMIT)


def _dot(a, b):
    return jnp.dot(a, b, preferred_element_type=F32)


def _dot_nt(a, b):
    return lax.dot_general(a, b, (((1,), (1,)), ((), ())), preferred_element_type=F32)


def _dot_tn(a, b):
    return lax.dot_general(a, b, (((0,), (0,)), ((), ())), preferred_element_type=F32)


def _rms(x, g, n):
    r = lax.rsqrt(jnp.sum(x * x, axis=-1, keepdims=True) * (1.0 / n) + EPS)
    xhat = x * r
    return xhat * g, xhat, r


def _rms_bwd(dy, g, xhat, r, n):
    dxhat = dy * g
    return r * (dxhat - xhat * (jnp.sum(dxhat * xhat, axis=-1, keepdims=True) * (1.0 / n)))


def _rope(x, c, s):
    return x * c + (pltpu.roll(x, 32, 1) - pltpu.roll(x, 96, 1)) * s


def _rope_t(dy, c, s):
    w = dy * s
    return dy * c + pltpu.roll(w, 96, 1) - pltpu.roll(w, 32, 1)


def _sigmoid(x):
    return 1.0 / (1.0 + jnp.exp(-x))


_GELU_K = 0.7978845608028654
_GELU_C = 0.044715


def _gelu(x):
    return 0.5 * x * (1.0 + jnp.tanh(_GELU_K * (x + _GELU_C * x * x * x)))


def _gelu_and_grad(x):
    x2 = x * x
    t = jnp.tanh(_GELU_K * (x + _GELU_C * x2 * x))
    half = 0.5 * (1.0 + t)
    return x * half, half + 0.5 * x * (1.0 - t * t) * (_GELU_K * (1.0 + 3.0 * _GELU_C * x2))


def _acc_rows(ref, val):
    ref[...] += jnp.broadcast_to(jnp.sum(val, axis=0, keepdims=True), ref.shape)


def _row(tm, c):
    return pl.BlockSpec((tm, c), lambda i: (i, 0))


def _const(shape):
    nd = len(shape)
    return pl.BlockSpec(shape, lambda i: (0,) * nd, pipeline_mode=pl.Buffered(1))


def _wblk(rows, row0):
    assert row0 % rows == 0, (rows, row0)
    return pl.BlockSpec((N_CHIPS, rows, D), lambda i: (0, row0 // rows, 0), pipeline_mode=pl.Buffered(1))


def _rows_joined(w_ref):
    return w_ref[...].reshape(N_CHIPS * w_ref.shape[1], D)


def _sds(shape, dtype):
    return jax.ShapeDtypeStruct(shape, dtype)


def mixffn_fwd(h, y, allw, lay, g2):
    t, k = y.shape

    def body(h_ref, y_ref, wo_ref, g_ref, wu_ref, wd_ref, h1_ref, h2_ref, hn_ref, r_ref):
        h1 = h_ref[...] + _dot(y_ref[...], _rows_joined(wo_ref))
        h1_ref[...] = h1
        yn, _, _ = _rms(h1, g_ref[...], D)
        hn = yn.astype(BF16)
        hn_ref[...] = hn
        f = jnp.zeros((TMB, D), F32)
        for c in range(N_CHIPS):
            r = jnp.maximum(_dot(hn, wu_ref[c]), 0.0)
            r_ref[:, c * D:(c + 1) * D] = r.astype(BF16)
            f = f + _dot((r * r).astype(BF16), wd_ref[c])
        h2_ref[...] = h1 + f

    return pl.pallas_call(
        body, name="mixffn_fwd", grid=(t // TMB,),
        in_specs=[_row(TMB, D), _row(TMB, k), _wblk(k // N_CHIPS, lay["out"]), _const((1, D)), _wblk(D, lay["up"]),
                  _wblk(D, lay["down"])],
        out_specs=[_row(TMB, D), _row(TMB, D), _row(TMB, D), _row(TMB, DFF)],
        out_shape=[_sds((t, D), F32), _sds((t, D), F32), _sds((t, D), BF16), _sds((t, DFF), BF16)],
        compiler_params=_cp("parallel"),
    )(h, y, allw, g2, allw, allw)


def _layer_rows(tm, c, layer):
    return pl.BlockSpec((None, tm, c), lambda i: (layer, i, 0))


def ple_fwd(h2, p, layer, g3, allw, lay, wp):
    t = h2.shape[0]

    def body(h_ref, p_ref, g_ref, wg_ref, wp_ref, h3_ref, hn_ref):
        x = h_ref[...]
        yn, _, _ = _rms(x, g_ref[...], D)
        hn = yn.astype(BF16)
        hn_ref[...] = hn
        gt = _dot(hn, _rows_joined(wg_ref))
        pp = _dot(p_ref[...].astype(BF16), wp_ref[...])
        h3_ref[...] = x + _sigmoid(gt) * pp

    return pl.pallas_call(
        body, name="ple_fwd", grid=(t // TMB,),
        in_specs=[_row(TMB, D), _layer_rows(TMB, PLE, layer), _const((1, D)), _wblk(D // N_CHIPS, lay["gate"]),
                  _const((PLE, D))],
        out_specs=[_row(TMB, D), _row(TMB, D)],
        out_shape=[_sds((t, D), F32), _sds((t, D), BF16)],
        compiler_params=_cp("parallel"),
    )(h2, p, g3, allw, wp)


def _mla_project(h_ref, g1_ref, wdn_ref, gq_ref, gkv_ref, wuq_ref, wukv_ref):
    x = h_ref[...]
    yn, xhat, rx = _rms(x, g1_ref[...], D)
    hn = yn.astype(BF16)
    lat = _dot(hn, wdn_ref[...])
    cq, cqhat, rq = _rms(lat[:, :QL], gq_ref[...], QL)
    ckv, ckvhat, rkv = _rms(lat[:, QL:QL + KVL], gkv_ref[...], KVL)
    kr_raw = lat[:, QL + KVL:]
    cqb = cq.astype(BF16)
    ckvb = ckv.astype(BF16)
    qp = _dot(cqb, wuq_ref[...])
    kvp = _dot(ckvb, wukv_ref[...])
    return dict(xhat=xhat, rx=rx, hn=hn, cqhat=cqhat, rq=rq, ckvhat=ckvhat, rkv=rkv, kr_raw=kr_raw,
                cqb=cqb, ckvb=ckvb, qp=qp, kvp=kvp)


def mla_pre_fwd(h, g1, wdn, gq, gkv, wuq, wukv, gqn, gqr, gkn, gkr, cos, sin):
    t = h.shape[0]

    def body(h_ref, g1_ref, wdn_ref, gq_ref, gkv_ref, wuq_ref, wukv_ref, gqn_ref, gqr_ref, gkn_ref, gkr_ref,
             c_ref, s_ref, q_ref, k_ref, v_ref):
        m = _mla_project(h_ref, g1_ref, wdn_ref, gq_ref, gkv_ref, wuq_ref, wukv_ref)
        c = c_ref[...]
        s = s_ref[...]
        kr, _, _ = _rms(m["kr_raw"], gkr_ref[...], DR)
        krb = _rope(kr, c, s).astype(BF16)
        for hd in range(HEADS):
            qn, _, _ = _rms(m["qp"][:, hd * DN:(hd + 1) * DN], gqn_ref[...], DN)
            qr, _, _ = _rms(m["qp"][:, D + hd * LANES:D + (hd + 1) * LANES], gqr_ref[...], DR)
            q_ref[hd, :, 0:DN] = (qn * SM_SCALE).astype(BF16)
            q_ref[hd, :, DN:2 * DN] = (_rope(qr, c, s) * SM_SCALE).astype(BF16)
            kn, _, _ = _rms(m["kvp"][:, hd * 2 * DN:hd * 2 * DN + DN], gkn_ref[...], DN)
            k_ref[hd, :, 0:DN] = kn.astype(BF16)
            k_ref[hd, :, DN:2 * DN] = krb
            v_ref[hd] = m["kvp"][:, hd * 2 * DN + DN:(hd + 1) * 2 * DN].astype(BF16)

    hb = lambda w: pl.BlockSpec((HEADS, TM, w), lambda i: (0, i, 0))
    return pl.pallas_call(
        body, name="mla_pre_fwd", grid=(t // TM,),
        in_specs=[_row(TM, D), _const((1, D)), _const((D, LATP)), _const((1, QL)), _const((1, KVL)),
                  _const((QL, 2 * D)), _const((KVL, 2 * D)), _const((1, LANES)), _const((1, LANES)),
                  _const((1, LANES)), _const((1, LANES)), _row(TM, LANES), _row(TM, LANES)],
        out_specs=[hb(2 * DN), hb(2 * DN), hb(DN)],
        out_shape=[_sds((HEADS, t, 2 * DN), BF16), _sds((HEADS, t, 2 * DN), BF16), _sds((HEADS, t, DN), BF16)],
        compiler_params=_cp("parallel"),
    )(h, g1, wdn, gq, gkv, wuq, wukv, gqn, gqr, gkn, gkr, cos, sin)


def _diagonal_mask(n=TQ):
    return lax.broadcasted_iota(jnp.int32, (n, n), 1) <= lax.broadcasted_iota(jnp.int32, (n, n), 0)


def flash_fwd(q, k, v, seq):
    t = q.shape[1]
    nb = t // seq
    tq = TQ_FWD
    nq = seq // tq
    hp = FWD_HEADS

    def body(q_ref, k_ref, v_ref, o_ref, lse_ref):
        qi = pl.program_id(2)
        qs = [q_ref[a] for a in range(hp)]

        def step(j, carry, diagonal=False):
            rows = pl.ds(pl.multiple_of(j * tq, tq), tq)
            out = []
            for a in range(hp):
                m, l, acc = carry[a]
                s = _dot_nt(qs[a], k_ref[a, rows, :])
                if diagonal:
                    s = jnp.where(_diagonal_mask(tq), s, -1e30)
                m_new = jnp.maximum(m, jnp.max(s, axis=-1, keepdims=True))
                p = jnp.exp(s - m_new)
                alpha = jnp.exp(m - m_new)
                l = alpha * l + jnp.sum(p, axis=-1, keepdims=True)
                acc = alpha * acc + _dot(p.astype(BF16), v_ref[a, rows, :])
                out.append((m_new, l, acc))
            return tuple(out)

        one = (jnp.full((tq, 1), -1e30, F32), jnp.zeros((tq, 1), F32), jnp.zeros((tq, DN), F32))
        done = step(qi, lax.fori_loop(0, qi, step, (one,) * hp), diagonal=True)
        for a, (m, l, acc) in enumerate(done):
            o_ref[:, a * DN:(a + 1) * DN] = (acc / l).astype(BF16)
            lse_ref[a] = m + jnp.log(l)

    return pl.pallas_call(
        body, name="flash_fwd", grid=(nb, HEADS // hp, nq),
        in_specs=[pl.BlockSpec((hp, tq, 2 * DN), lambda b, h, i: (h, b * nq + i, 0)),
                  pl.BlockSpec((hp, seq, 2 * DN), lambda b, h, i: (h, b, 0)),
                  pl.BlockSpec((hp, seq, DN), lambda b, h, i: (h, b, 0))],
        out_specs=[pl.BlockSpec((tq, hp * DN), lambda b, h, i: (b * nq + i, h)),
                   pl.BlockSpec((hp, tq, 1), lambda b, h, i: (h, b * nq + i, 0))],
        out_shape=[_sds((t, HEADS * DN), BF16), _sds((HEADS, t, 1), F32)],
        compiler_params=_cp("parallel", "parallel", "arbitrary"),
    )(q, k, v)


def _gmlp_in(hn, win_ref):
    pre = [_dot(hn, win_ref[c]) for c in range(N_CHIPS)]
    return jnp.concatenate(pre[:2], axis=1), jnp.concatenate(pre[2:], axis=1)


def gmlp_fwd(h, g1, allw, lay, lng, lnb, wm, bfull):
    t = h.shape[0]

    def body(h_ref, g1_ref, win_ref, lng_ref, lnb_ref, wm_ref, b_ref, y_ref, pre_ref):
        yn, _, _ = _rms(h_ref[...], g1_ref[...], D)
        pre_u, pre_v = _gmlp_in(yn.astype(BF16), win_ref)
        pre_ref[:, :GH] = pre_u.astype(BF16)
        pre_ref[:, GH:] = pre_v.astype(BF16)
        u = _gelu(pre_u)
        v = _gelu(pre_v)
        xc = v - jnp.mean(v, axis=-1, keepdims=True)
        rs = lax.rsqrt(jnp.mean(xc * xc, axis=-1, keepdims=True) + EPS)
        vnb = (xc * rs * lng_ref[...] + lnb_ref[...]).astype(BF16)
        for ch in range(TM // GC):
            rows = slice(ch * GC, (ch + 1) * GC)
            for g in range(GG):
                cols = slice(g * GD, (g + 1) * GD)
                sv = _dot(wm_ref[g], vnb[rows, cols]) + b_ref[:, cols]
                y_ref[rows, cols] = (u[rows, cols] * sv).astype(BF16)

    return pl.pallas_call(
        body, name="gmlp_fwd", grid=(t // TM,),
        in_specs=[_row(TM, D), _const((1, D)), _wblk(D, lay["in"]), _const((1, GH)), _const((1, GH)),
                  _const((GG, GC, GC)), _const((GC, GH))],
        out_specs=[_row(TM, GH), _row(TM, 2 * GH)],
        out_shape=[_sds((t, GH), BF16), _sds((t, 2 * GH), BF16)],
        compiler_params=_cp("parallel"),
    )(h, g1, allw, lng, lnb, wm, bfull)


def loss_head(h, tgt):
    t = h.shape[0]

    def body(h_ref, t_ref, dh_ref, loss_ref):
        @pl.when(pl.program_id(0) == 0)
        def _():
            loss_ref[...] = jnp.zeros_like(loss_ref)

        e = h_ref[...] - t_ref[...]
        dh_ref[...] = e * (1.0 / D)
        part = jnp.sum(jnp.sum(e * e, axis=-1, keepdims=True), axis=0, keepdims=True) * (0.5 / D)
        loss_ref[...] += jnp.broadcast_to(part, loss_ref.shape)

    return pl.pallas_call(
        body, name="loss_head", grid=(t // TMB,),
        in_specs=[_row(TMB, D), _row(TMB, D)],
        out_specs=[_row(TMB, D), _const((8, LANES))],
        out_shape=[_sds((t, D), F32), _sds((8, LANES), F32)],
        compiler_params=_cp("arbitrary"),
    )(h, tgt)


def _zero_at_first_step(*refs):
    @pl.when(pl.program_id(0) == 0)
    def _():
        for r in refs:
            r[...] = jnp.zeros_like(r)


def ple_bwd(dh3, h2, p, layer, g3, allw, lay, wp):
    t = h2.shape[0]

    def body(dh_ref, h_ref, p_ref, g_ref, wg_ref, wp_ref, dh2_ref, dh2b_ref, dgt_ref, dpp_ref, dg_ref):
        _zero_at_first_step(dg_ref)
        dh3v = dh_ref[...]
        x = h_ref[...]
        g = g_ref[...]
        wg = _rows_joined(wg_ref)
        yn, xhat, r = _rms(x, g, D)
        gt = _dot(yn.astype(BF16), wg)
        pp = _dot(p_ref[...].astype(BF16), wp_ref[...])
        sg = _sigmoid(gt)
        dgt = (dh3v * pp * sg * (1.0 - sg)).astype(BF16)
        dgt_ref[...] = dgt
        dpp_ref[...] = (dh3v * sg).astype(BF16)
        dhn = _dot_nt(dgt, wg)
        _acc_rows(dg_ref, dhn * xhat)
        dh2 = dh3v + _rms_bwd(dhn, g, xhat, r, D)
        dh2_ref[...] = dh2
        dh2b_ref[...] = dh2.astype(BF16)

    return pl.pallas_call(
        body, name="ple_bwd", grid=(t // TMB,),
        in_specs=[_row(TMB, D), _row(TMB, D), _layer_rows(TMB, PLE, layer), _const((1, D)),
                  _wblk(D // N_CHIPS, lay["gate"]),
                  _const((PLE, D))],
        out_specs=[_row(TMB, D), _row(TMB, D), _row(TMB, D), _row(TMB, D), _const((8, D))],
        out_shape=[_sds((t, D), F32), _sds((t, D), BF16), _sds((t, D), BF16), _sds((t, D), BF16), _sds((8, D), F32)],
        compiler_params=_cp("arbitrary"),
    )(dh3, h2, p, g3, allw, wp)


def ffn_bwd(dh2, dh2b, h1, r, g2, allw, lay):
    t = h1.shape[0]

    def body(dh_ref, dhb_ref, h_ref, r_ref, g_ref, wu_ref, wd_ref, dh1_ref, dh1b_ref, du_ref, a_ref, dg_ref):
        _zero_at_first_step(dg_ref)
        dhb = dhb_ref[...]
        g = g_ref[...]
        _, xhat, rr = _rms(h_ref[...], g, D)
        dhn = jnp.zeros((TM, D), F32)
        for c in range(N_CHIPS):
            cs = slice(c * D, (c + 1) * D)
            rc = r_ref[:, cs].astype(F32)
            a_ref[:, cs] = (rc * rc).astype(BF16)
            da = _dot_nt(dhb, wd_ref[c])
            du = (da * (2.0 * rc)).astype(BF16)
            du_ref[:, cs] = du
            dhn = dhn + _dot_nt(du, wu_ref[c])
        _acc_rows(dg_ref, dhn * xhat)
        dh1 = dh_ref[...] + _rms_bwd(dhn, g, xhat, rr, D)
        dh1_ref[...] = dh1
        dh1b_ref[...] = dh1.astype(BF16)

    return pl.pallas_call(
        body, name="ffn_bwd", grid=(t // TM,),
        in_specs=[_row(TM, D), _row(TM, D), _row(TM, D), _row(TM, DFF), _const((1, D)), _wblk(D, lay["up"]),
                  _wblk(D, lay["down"])],
        out_specs=[_row(TM, D), _row(TM, D), _row(TM, DFF), _row(TM, DFF), _const((8, D))],
        out_shape=[_sds((t, D), F32), _sds((t, D), BF16), _sds((t, DFF), BF16), _sds((t, DFF), BF16),
                   _sds((8, D), F32)],
        compiler_params=_cp("arbitrary"),
    )(dh2, dh2b, h1, r, g2, allw, allw)


def linear_nt(a, allw, rows, row0):
    t = a.shape[0]
    k = N_CHIPS * rows

    def body(a_ref, w_ref, o_ref):
        o_ref[...] = _dot_nt(a_ref[...], _rows_joined(w_ref)).astype(BF16)

    return pl.pallas_call(
        body, name="linear_nt", grid=(t // TMB,),
        in_specs=[_row(TMB, D), _wblk(rows, row0)],
        out_specs=_row(TMB, k),
        out_shape=_sds((t, k), BF16),
        compiler_params=_cp("parallel"),
    )(a, allw)


def flash_bwd(q, k, v, o, do, lse, seq, after):
    t = q.shape[1]
    nb = t // seq
    nq = seq // TQ
    hp = BWD_HEADS

    def body(q_ref, k_ref, v_ref, o_ref, do_ref, lse_ref, after_ref, dq_ref, dk_ref, dv_ref):
        del after_ref
        kj = pl.program_id(2)

        @pl.when(kj == 0)
        def _():
            dq_ref[...] = jnp.zeros_like(dq_ref)

        def step(i, carry, diagonal=False):
            rows = pl.ds(pl.multiple_of(i * TQ, TQ), TQ)
            out = []
            for a in range(hp):
                dk, dv = carry[a]
                kv = k_ref[a]
                qv = q_ref[a, rows, :]
                dov = do_ref[rows, a * DN:(a + 1) * DN]
                ov = o_ref[rows, a * DN:(a + 1) * DN]
                delta = jnp.sum(dov.astype(F32) * ov.astype(F32), axis=-1, keepdims=True)
                s = _dot_nt(qv, kv)
                if diagonal:
                    s = jnp.where(_diagonal_mask(), s, -1e30)
                p = jnp.exp(s - lse_ref[a, rows, :])
                dp = _dot_nt(dov, v_ref[a])
                ds = (p * (dp - delta)).astype(BF16)
                dv = dv + _dot_tn(p.astype(BF16), dov)
                dk = dk + _dot_tn(ds, qv)
                dq_ref[a, rows, :] += _dot(ds, kv)
                out.append((dk, dv))
            return tuple(out)

        one = (jnp.zeros((TQ, 2 * DN), F32), jnp.zeros((TQ, DN), F32))
        done = lax.fori_loop(kj + 1, nq, step, step(kj, (one,) * hp, diagonal=True))
        for a, (dk, dv) in enumerate(done):
            dk_ref[a] = dk
            dv_ref[a] = dv

    return pl.pallas_call(
        body, name="flash_bwd", grid=(nb, HEADS // hp, nq),
        in_specs=[pl.BlockSpec((hp, seq, 2 * DN), lambda b, h, j: (h, b, 0)),
                  pl.BlockSpec((hp, TQ, 2 * DN), lambda b, h, j: (h, b * nq + j, 0)),
                  pl.BlockSpec((hp, TQ, DN), lambda b, h, j: (h, b * nq + j, 0)),
                  pl.BlockSpec((seq, hp * DN), lambda b, h, j: (b, h)),
                  pl.BlockSpec((seq, hp * DN), lambda b, h, j: (b, h)),
                  pl.BlockSpec((hp, seq, 1), lambda b, h, j: (h, b, 0)), _ANY],
        out_specs=[pl.BlockSpec((hp, seq, 2 * DN), lambda b, h, j: (h, b, 0)),
                   pl.BlockSpec((hp, TQ, 2 * DN), lambda b, h, j: (h, b * nq + j, 0)),
                   pl.BlockSpec((hp, TQ, DN), lambda b, h, j: (h, b * nq + j, 0))],
        out_shape=[_sds((HEADS, t, 2 * DN), F32), _sds((HEADS, t, 2 * DN), F32), _sds((HEADS, t, DN), F32)],
        compiler_params=_cp("parallel", "parallel", "arbitrary"),
    )(q, k, v, o, do, lse, after)


def mla_pre_bwd(dq, dk, dv, dh1, h, g1, wdn, gq, gkv, wuq, wukv, gqn, gqr, gkn, gkr, cos, sin):
    t = h.shape[0]

    def body(dq_ref, dk_ref, dv_ref, dh1_ref, h_ref, g1_ref, wdn_ref, gq_ref, gkv_ref, wuq_ref, wukv_ref,
             gqn_ref, gqr_ref, gkn_ref, gkr_ref, c_ref, s_ref,
             dh_ref, hn_ref, cq_ref, ckv_ref, dqp_ref, dkvp_ref, dlat_ref,
             dg1_ref, dgq_ref, dgkv_ref, dgqn_ref, dgqr_ref, dgkn_ref, dgkr_ref):
        _zero_at_first_step(dg1_ref, dgq_ref, dgkv_ref, dgqn_ref, dgqr_ref, dgkn_ref, dgkr_ref)
        m = _mla_project(h_ref, g1_ref, wdn_ref, gq_ref, gkv_ref, wuq_ref, wukv_ref)
        hn_ref[...] = m["hn"]
        cq_ref[...] = m["cqb"]
        ckv_ref[...] = m["ckvb"]
        c = c_ref[...]
        s = s_ref[...]
        gqn = gqn_ref[...]
        gqr = gqr_ref[...]
        gkn = gkn_ref[...]
        gkr = gkr_ref[...]

        dkr = dk_ref[0, :, DN:2 * DN]
        for hd in range(1, HEADS):
            dkr = dkr + dk_ref[hd, :, DN:2 * DN]
        dkr = _rope_t(dkr, c, s)
        _, krhat, rkr = _rms(m["kr_raw"], gkr, DR)
        _acc_rows(dgkr_ref, dkr * krhat)
        dkr_raw = _rms_bwd(dkr, gkr, krhat, rkr, DR)

        for hd in range(HEADS):
            ncols = slice(hd * DN, (hd + 1) * DN)
            _, xh, r = _rms(m["qp"][:, ncols], gqn, DN)
            dqn = dq_ref[hd, :, 0:DN] * SM_SCALE
            _acc_rows(dgqn_ref, dqn * xh)
            dqp_ref[:, ncols] = _rms_bwd(dqn, gqn, xh, r, DN).astype(BF16)

            rcols = slice(D + hd * LANES, D + (hd + 1) * LANES)
            _, xh, r = _rms(m["qp"][:, rcols], gqr, DR)
            dqr = _rope_t(dq_ref[hd, :, DN:2 * DN] * SM_SCALE, c, s)
            _acc_rows(dgqr_ref, dqr * xh)
            dqp_ref[:, rcols] = _rms_bwd(dqr, gqr, xh, r, DR).astype(BF16)

            kcols = slice(hd * 2 * DN, hd * 2 * DN + DN)
            _, xh, r = _rms(m["kvp"][:, kcols], gkn, DN)
            dkn = dk_ref[hd, :, 0:DN]
            _acc_rows(dgkn_ref, dkn * xh)
            dkvp_ref[:, kcols] = _rms_bwd(dkn, gkn, xh, r, DN).astype(BF16)
            dkvp_ref[:, hd * 2 * DN + DN:(hd + 1) * 2 * DN] = dv_ref[hd].astype(BF16)

        dcq = _dot_nt(dqp_ref[...], wuq_ref[...])
        _acc_rows(dgq_ref, dcq * m["cqhat"])
        dlat_q = _rms_bwd(dcq, gq_ref[...], m["cqhat"], m["rq"], QL)
        dckv = _dot_nt(dkvp_ref[...], wukv_ref[...])
        _acc_rows(dgkv_ref, dckv * m["ckvhat"])
        dlat_kv = _rms_bwd(dckv, gkv_ref[...], m["ckvhat"], m["rkv"], KVL)
        dlat = jnp.concatenate([dlat_q, dlat_kv, dkr_raw], axis=1).astype(BF16)
        dlat_ref[...] = dlat
        dhn = _dot_nt(dlat, wdn_ref[...])
        _acc_rows(dg1_ref, dhn * m["xhat"])
        dh_ref[...] = dh1_ref[...] + _rms_bwd(dhn, g1_ref[...], m["xhat"], m["rx"], D)

    hb = lambda w: pl.BlockSpec((HEADS, TM, w), lambda i: (0, i, 0))
    return pl.pallas_call(
        body, name="mla_pre_bwd", grid=(t // TM,),
        in_specs=[hb(2 * DN), hb(2 * DN), hb(DN), _row(TM, D), _row(TM, D), _const((1, D)), _const((D, LATP)),
                  _const((1, QL)), _const((1, KVL)), _const((QL, 2 * D)), _const((KVL, 2 * D)),
                  _const((1, LANES)), _const((1, LANES)), _const((1, LANES)), _const((1, LANES)),
                  _row(TM, LANES), _row(TM, LANES)],
        out_specs=[_row(TM, D), _row(TM, D), _row(TM, QL), _row(TM, KVL), _row(TM, 2 * D), _row(TM, 2 * D),
                   _row(TM, LATP), _const((8, D)), _const((8, QL)), _const((8, KVL)), _const((8, LANES)),
                   _const((8, LANES)), _const((8, LANES)), _const((8, LANES))],
        out_shape=[_sds((t, D), F32), _sds((t, D), BF16), _sds((t, QL), BF16), _sds((t, KVL), BF16),
                   _sds((t, 2 * D), BF16), _sds((t, 2 * D), BF16), _sds((t, LATP), BF16),
                   _sds((8, D), F32), _sds((8, QL), F32), _sds((8, KVL), F32), _sds((8, LANES), F32),
                   _sds((8, LANES), F32), _sds((8, LANES), F32), _sds((8, LANES), F32)],
        compiler_params=_cp("arbitrary"),
    )(dq, dk, dv, dh1, h, g1, wdn, gq, gkv, wuq, wukv, gqn, gqr, gkn, gkr, cos, sin)


def gmlp_bwd(dh1, dh1b, h, pre, g1, allw, lay, lng, lnb, wm, wmt, bfull, tril):
    t = h.shape[0]

    def body(dh1_ref, dh1b_ref, h_ref, pre_ref, g1_ref, win_ref, lng_ref, lnb_ref, wm_ref, wmt_ref, b_ref,
             wout_ref, tril_ref, dh_ref, hn_ref, dpre_ref, dws_ref, dbs_ref, dlng_ref, dlnb_ref, dg1_ref,
             dvn_s):
        _zero_at_first_step(dws_ref, dbs_ref, dlng_ref, dlnb_ref, dg1_ref)
        g1 = g1_ref[...]
        yn, xhat, rx = _rms(h_ref[...], g1, D)
        hn_ref[...] = yn.astype(BF16)
        dy = _dot_nt(dh1b_ref[...], _rows_joined(wout_ref))
        pre_u = pre_ref[:, :GH].astype(F32)
        pre_v = pre_ref[:, GH:].astype(F32)
        u, gg_u = _gelu_and_grad(pre_u)
        v, gg_v = _gelu_and_grad(pre_v)
        xc = v - jnp.mean(v, axis=-1, keepdims=True)
        rs = lax.rsqrt(jnp.mean(xc * xc, axis=-1, keepdims=True) + EPS)
        vhat = xc * rs
        lng = lng_ref[...]
        vnb = (vhat * lng + lnb_ref[...]).astype(BF16)
        dsv = dy * u
        dsvb = dsv.astype(BF16)
        tril_m = tril_ref[...]
        for ch in range(TM // GC):
            rows = slice(ch * GC, (ch + 1) * GC)
            dbs_ref[...] += dsv[rows, :]
            for g in range(GG):
                cols = slice(g * GD, (g + 1) * GD)
                sv = _dot(wm_ref[g], vnb[rows, cols]) + b_ref[:, cols]
                dpre_ref[rows, cols] = (dy[rows, cols] * sv * gg_u[rows, cols]).astype(BF16)
                dvn_s[rows, cols] = _dot(wmt_ref[g], dsvb[rows, cols])
                dws_ref[g] += _dot_nt(dsvb[rows, cols], vnb[rows, cols]) * tril_m
        dvn = dvn_s[...]
        _acc_rows(dlng_ref, dvn * vhat)
        _acc_rows(dlnb_ref, dvn)
        dvhat = dvn * lng
        dv = rs * (dvhat - jnp.mean(dvhat, axis=-1, keepdims=True)
                   - vhat * jnp.mean(dvhat * vhat, axis=-1, keepdims=True))
        dpre_v = (dv * gg_v).astype(BF16)
        dpre_ref[:, GH:] = dpre_v
        dhn = _dot_nt(dpre_ref[:, 0:D], win_ref[0])
        for c in range(1, N_CHIPS):
            dhn = dhn + _dot_nt(dpre_ref[:, c * D:(c + 1) * D], win_ref[c])
        _acc_rows(dg1_ref, dhn * xhat)
        dh_ref[...] = dh1_ref[...] + _rms_bwd(dhn, g1, xhat, rx, D)

    return pl.pallas_call(
        body, name="gmlp_bwd", grid=(t // TM,),
        in_specs=[_row(TM, D), _row(TM, D), _row(TM, D), _row(TM, 2 * GH), _const((1, D)), _wblk(D, lay["in"]),
                  _const((1, GH)), _const((1, GH)), _const((GG, GC, GC)), _const((GG, GC, GC)), _const((GC, GH)),
                  _wblk(GH // N_CHIPS, lay["out"]), _const((GC, GC))],
        out_specs=[_row(TM, D), _row(TM, D), _row(TM, 2 * GH), _const((GG, GC, GC)), _const((GC, GH)),
                   _const((8, GH)), _const((8, GH)), _const((8, D))],
        out_shape=[_sds((t, D), F32), _sds((t, D), BF16), _sds((t, 2 * GH), BF16), _sds((GG, GC, GC), F32),
                   _sds((GC, GH), F32), _sds((8, GH), F32), _sds((8, GH), F32), _sds((8, D), F32)],
        scratch_shapes=[pltpu.VMEM((TM, GH), F32)],
        compiler_params=_cp("arbitrary"),
    )(dh1, dh1b, h, pre, g1, allw, lng, lnb, wm, wmt, bfull, allw, tril)


def _token_step(t):
    return next(s for s in (2048, 1024, 512) if t % s == 0)


def mm_tn(a, b, layer=None):
    t, k = a.shape[-2:]
    n = b.shape[1]
    tk = min(k, 1024)
    tn = min(n, 1024)
    tt = _token_step(t)
    a_spec = (pl.BlockSpec((tt, tk), lambda i, j, s: (s, i)) if layer is None else
              pl.BlockSpec((None, tt, tk), lambda i, j, s: (layer, s, i)))

    def body(a_ref, b_ref, o_ref):
        @pl.when(pl.program_id(2) == 0)
        def _():
            o_ref[...] = jnp.zeros_like(o_ref)

        o_ref[...] += _dot_tn(a_ref[...].astype(BF16), b_ref[...].astype(BF16))

    return pl.pallas_call(
        body, name="mm_tn", grid=(k // tk, n // tn, t // tt),
        in_specs=[a_spec, pl.BlockSpec((tt, tn), lambda i, j, s: (s, j))],
        out_specs=pl.BlockSpec((tk, tn), lambda i, j, s: (i, j)), out_shape=_sds((k, n), F32),
        compiler_params=_cp("parallel", "parallel", "arbitrary"),
    )(a, b)


def mm_tn_into(buf, a, b, rows, row0, col_sharded):
    t = a.shape[0]
    tt = _token_step(t)
    assert row0 % rows == 0 and a.shape[1] == (rows if col_sharded else N_CHIPS * rows), (rows, row0, a.shape)
    assert b.shape[1] == (N_CHIPS * D if col_sharded else D), b.shape
    joint = not col_sharded and N_CHIPS * rows <= 2048
    grid = (1, 1, t // tt) if joint else (1, N_CHIPS, t // tt) if col_sharded else (N_CHIPS, 1, t // tt)
    fresh = isinstance(buf, int)

    def body(*refs):
        a_ref, b_ref, o_ref = refs[-3:]

        @pl.when(pl.program_id(2) == 0)
        def _():
            o_ref[...] = jnp.zeros_like(o_ref)

        o_ref[...] += _dot_tn(a_ref[...].astype(BF16), b_ref[...].astype(BF16)).reshape(o_ref.shape)

    specs = [pl.BlockSpec((tt, N_CHIPS * rows if joint else rows), lambda i, j, s: (s, i)),
             pl.BlockSpec((tt, D), lambda i, j, s: (s, j))]
    return pl.pallas_call(
        body, name="mm_tn_into", grid=grid,
        in_specs=specs if fresh else [_ANY] + specs,
        out_specs=pl.BlockSpec((N_CHIPS if joint else None, rows, D), lambda i, j, s: (i + j, row0 // rows, 0)),
        out_shape=_sds((N_CHIPS, buf, D) if fresh else buf.shape, F32),
        input_output_aliases={} if fresh else {0: 0},
        compiler_params=_cp("parallel", "parallel", "arbitrary"),
    )(*((a, b) if fresh else (buf, a, b)))


def adamw(w, g, m, v):
    rows, cols = w.shape
    tr = rows if rows <= 512 else next(r for r in (512, 384, 256, 128) if rows % r == 0)
    c1 = 1.0 - ADAM_B1 ** ADAM_STEP
    c2 = 1.0 - ADAM_B2 ** ADAM_STEP

    def body(w_ref, g_ref, m_ref, v_ref, d_ref, mo_ref, vo_ref):
        gv = g_ref[...]
        mn = ADAM_B1 * m_ref[...] + (1.0 - ADAM_B1) * gv
        vn = ADAM_B2 * v_ref[...] + (1.0 - ADAM_B2) * (gv * gv)
        mo_ref[...] = mn
        vo_ref[...] = vn
        d_ref[...] = -ADAM_LR * ((mn / c1) / (jnp.sqrt(vn / c2) + ADAM_EPS) + ADAM_WD * w_ref[...])

    spec = pl.BlockSpec((tr, cols), lambda i: (i, 0))
    return pl.pallas_call(
        body, name="adamw", grid=(rows // tr,),
        in_specs=[spec] * 4, out_specs=[spec] * 3, out_shape=[_sds((rows, cols), F32)] * 3,
        compiler_params=_cp("parallel"),
    )(w, g, m, v)


def adamw_layers(w, m, v, bufs, row0s):
    nl, a, _ = w.shape
    tr = min(a, 256)
    c1 = 1.0 - ADAM_B1 ** ADAM_STEP
    c2 = 1.0 - ADAM_B2 ** ADAM_STEP
    assert all(r % tr == 0 for r in row0s) and a % tr == 0, (row0s, a)

    def body(w_ref, m_ref, v_ref, *rest):
        g_refs, (g_ref, d_ref, mo_ref, vo_ref) = rest[:nl], rest[nl:]
        for l in range(nl):
            @pl.when(pl.program_id(0) == l)
            def _(l=l):
                gv = g_refs[l][...]
                g_ref[...] = gv
                mn = ADAM_B1 * m_ref[...] + (1.0 - ADAM_B1) * gv
                vn = ADAM_B2 * v_ref[...] + (1.0 - ADAM_B2) * (gv * gv)
                mo_ref[...] = mn
                vo_ref[...] = vn
                d_ref[...] = -ADAM_LR * ((mn / c1) / (jnp.sqrt(vn / c2) + ADAM_EPS) + ADAM_WD * w_ref[...])

    def rows_of(l, row0):
        return pl.BlockSpec((tr, D), lambda li, i: (jnp.where(li == l, row0 // tr + i, row0 // tr), 0))

    spec = pl.BlockSpec((None, tr, D), lambda li, i: (li, i, 0))
    return pl.pallas_call(
        body, name="adamw_layers", grid=(nl, a // tr),
        in_specs=[spec] * 3 + [rows_of(l, r) for l, r in enumerate(row0s)],
        out_specs=[spec] * 4, out_shape=[_sds(w.shape, F32)] * 4,
        compiler_params=_cp("arbitrary", "arbitrary"),
    )(w, m, v, *bufs)


def _place():
    return lax.axis_index("x"), lax.axis_index("y"), lax.axis_index("c")


def _other_chips(x, y):
    return [(1 - x, y), (x, 1 - y), (1 - x, 1 - y)]


_ANY = pl.BlockSpec(memory_space=pl.ANY)


_HBM = pl.BlockSpec(memory_space=pltpu.HBM)
_SEM = pl.BlockSpec(memory_space=pltpu.SEMAPHORE)
_EFFECT = pltpu.SideEffectType.DATAFLOW_SIDE_EFFECTING
N_ICI = 3


def _exchange_start(name, src, land, copies, n):
    def body(src_ref, land_ref, *outs):
        sems, token = outs[:2 * n], outs[-1]
        for j, (s, d, to) in enumerate(copies(src_ref, land_ref, _place())):
            pltpu.make_async_remote_copy(src_ref=s, dst_ref=d, send_sem=sems[j], recv_sem=sems[n + j],
                                         device_id=to, device_id_type=MESH).start()
        token[...] = jnp.zeros_like(token)

    sem = pltpu.SemaphoreType.DMA(())
    outs = pl.pallas_call(
        body, name=name,
        out_shape=(sem,) * (2 * n) + (pltpu.HBM(src.shape, src.dtype), pltpu.HBM(land.shape, land.dtype),
                                      _sds((8, LANES), F32)),
        in_specs=(_HBM, _HBM),
        out_specs=(_SEM,) * (2 * n) + (_HBM, _HBM, pl.BlockSpec(memory_space=pltpu.VMEM)),
        input_output_aliases={0: 2 * n, 1: 2 * n + 1},
        compiler_params=pltpu.CompilerParams(has_side_effects=_EFFECT),
    )(pltpu.with_memory_space_constraint(src, pltpu.HBM), pltpu.with_memory_space_constraint(land, pltpu.HBM))
    return outs[:2 * n], outs[2 * n], outs[2 * n + 1], outs[-1]


def _exchange_wait(name, sems, src, land, after, arrivals):
    n = len(sems) // 2

    def body(src_ref, land_ref, *rest):
        sems = rest[:2 * n]
        for j, (s, d) in enumerate(arrivals(src_ref, land_ref, _place())):
            cp = pltpu.make_async_remote_copy(src_ref=s, dst_ref=d, send_sem=sems[j], recv_sem=sems[n + j],
                                              device_id=_place(), device_id_type=MESH)
            cp.wait_send()
            cp.wait_recv()

    return pl.pallas_call(
        body, name=name, out_shape=(pltpu.HBM(src.shape, src.dtype), pltpu.HBM(land.shape, land.dtype)),
        in_specs=(_HBM, _HBM) + (_SEM,) * (2 * n) + (_ANY,), out_specs=(_HBM, _HBM),
        input_output_aliases={0: 0, 1: 1},
        compiler_params=pltpu.CompilerParams(has_side_effects=_EFFECT),
    )(src, land, *sems, after)


def _halves(c, hh):
    return pl.ds(pl.multiple_of(c * hh, 16), hh), pl.ds(pl.multiple_of((1 - c) * hh, 16), hh)


def gather_start(land, tag):
    _, rr, _ = land.shape
    assert rr % 32 == 0, rr

    def copies(_, land_ref, place):
        x, y, c = place
        mine = land_ref.at[2 * x + y, _halves(c, rr // 2)[0]]
        return [(mine, mine, (cx, cy, c)) for cx, cy in _other_chips(x, y)]

    return _exchange_start(f"gather_start_{tag}", jnp.zeros((8, LANES), F32), land, copies, N_ICI)


def gather_wait(sems, src, land, after, tag):
    def arrivals(_, land_ref, place):
        x, y, c = place
        half = _halves(c, land.shape[1] // 2)[0]
        return [(land_ref.at[2 * x + y, half], land_ref.at[2 * cx + cy, half]) for cx, cy in _other_chips(x, y)]

    return _exchange_wait(f"gather_wait_{tag}", sems, src, land, after, arrivals)


def pass_start(land, tag):
    def copies(_, land_ref, place):
        x, y, c = place
        half = _halves(c, land.shape[1] // 2)[0]
        return [(land_ref.at[2 * cx + cy, half], land_ref.at[2 * cx + cy, half], (x, y, 1 - c))
                for cx, cy in _other_chips(x, y)]

    return _exchange_start(f"pass_start_{tag}", jnp.zeros((8, LANES), F32), land, copies, N_ICI)


def pass_wait(sems, src, land, after, tag):
    def arrivals(_, land_ref, place):
        x, y, c = place
        mine, other = _halves(c, land.shape[1] // 2)
        return [(land_ref.at[2 * cx + cy, mine], land_ref.at[2 * cx + cy, other]) for cx, cy in _other_chips(x, y)]

    return _exchange_wait(f"pass_wait_{tag}", sems, src, land, after, arrivals)


def swap_start(g, tag):
    _, rr, cc = g.shape

    def copies(g_ref, got_ref, place):
        x, y, c = place
        other = _halves(c, rr // 2)[1]
        return [(g_ref.at[k, other], got_ref.at[k], (x, y, 1 - c)) for k in range(N_CHIPS)]

    return _exchange_start(f"swap_start_{tag}", g, lax.empty((N_CHIPS, rr // 2, cc), g.dtype), copies, N_CHIPS)


def swap_wait(sems, g, got, after, tag):
    def arrivals(g_ref, got_ref, place):
        other = _halves(place[2], g.shape[1] // 2)[1]
        return [(g_ref.at[k, other], got_ref.at[k]) for k in range(N_CHIPS)]

    return _exchange_wait(f"swap_wait_{tag}", sems, g, got, after, arrivals)


def chip_sum(place, g32, got):
    _, rr, cc = g32.shape
    hh = rr // 2
    tr = SUM_ROWS
    assert rr % 2 == 0 and hh % tr == 0, (rr, tr)
    nb = hh // tr

    def body(place_ref, g_ref, got_ref, own_ref, all_ref):
        s = g_ref[...] + got_ref[...].astype(F32)
        all_ref[...] = s.astype(BF16)
        own_ref[...] = g_ref[place_ref[1]] + got_ref[place_ref[1]].astype(F32)

    return pl.pallas_call(
        body, name="chip_sum",
        grid_spec=pltpu.PrefetchScalarGridSpec(
            num_scalar_prefetch=1, grid=(nb,),
            in_specs=[pl.BlockSpec((N_CHIPS, tr, cc), lambda i, pr: (0, pr[0] * nb + i, 0)),
                      pl.BlockSpec((N_CHIPS, tr, cc), lambda i, pr: (0, i, 0))],
            out_specs=[pl.BlockSpec((tr, cc), lambda i, pr: (i, 0)),
                       pl.BlockSpec((N_CHIPS, tr, cc), lambda i, pr: (0, i, 0))]),
        out_shape=[_sds((hh, cc), F32), _sds((N_CHIPS, hh, cc), BF16)],
        compiler_params=_cp("parallel"),
    )(place, g32, got)


def _scatter_copies(s_ref, land_ref, place):
    x, y, c = place
    return [(s_ref.at[2 * cx + cy], land_ref.at[j], (cx, cy, c)) for j, (cx, cy) in enumerate(_other_chips(x, y))]


def scatter_start(s, tag):
    return _exchange_start(f"scatter_start_{tag}", s, lax.empty((N_ICI,) + s.shape[1:], s.dtype), _scatter_copies, N_ICI)


def scatter_wait(sems, s, land, after, tag):
    return _exchange_wait(f"scatter_wait_{tag}", sems, s, land, after,
                          lambda s_ref, land_ref, place: [(a, b) for a, b, _ in _scatter_copies(s_ref, land_ref, place)])


def final_sum(place, own, got):
    hh, cc = own.shape
    tr = SUM_ROWS
    assert hh % tr == 0, (hh, tr)
    nb = hh // tr

    def body(place_ref, own_ref, got_ref, o_ref):
        del place_ref
        o_ref[...] = ((own_ref[...] + got_ref[0].astype(F32)) + got_ref[1].astype(F32)) + got_ref[2].astype(F32)

    return pl.pallas_call(
        body, name="final_sum",
        grid_spec=pltpu.PrefetchScalarGridSpec(
            num_scalar_prefetch=1, grid=(nb,),
            in_specs=[pl.BlockSpec((tr, cc), lambda i, pr: (i, 0)), pl.BlockSpec((3, tr, cc), lambda i, pr: (0, i, 0))],
            out_specs=pl.BlockSpec((tr, cc), lambda i, pr: (pr[0] * nb + i, 0))),
        out_shape=_sds((2 * hh, cc), F32),
        compiler_params=_cp("parallel"),
    )(place, own, got)


def share_start(f, tag):
    def copies(_, f_ref, place):
        x, y, c = place
        mine = f_ref.at[_halves(c, f.shape[0] // 2)[0]]
        return [(mine, mine, (x, y, 1 - c))]

    return _exchange_start(f"share_start_{tag}", jnp.zeros((8, LANES), F32), f, copies, 1)


def share_wait(sems, src, f, after, tag):
    def arrivals(_, f_ref, place):
        mine, other = _halves(place[2], f.shape[0] // 2)
        return [(f_ref.at[mine], f_ref.at[other])]

    return _exchange_wait(f"share_wait_{tag}", sems, src, f, after, arrivals)


N_DEV = 8


def _peers(place):
    x, y, c = place
    return [((1 - x) if r & 4 else x, (1 - y) if r & 2 else y, (1 - c) if r & 1 else c) for r in range(1, N_DEV)]


def _device_index(place):
    x, y, c = place
    return 4 * x + 2 * y + c


def small_start(land, tag):
    def copies(_, land_ref, place):
        mine = land_ref.at[_device_index(place)]
        return [(mine, mine, to) for to in _peers(place)]

    return _exchange_start(f"small_start_{tag}", jnp.zeros((8, LANES), F32), land, copies, N_DEV - 1)


def small_wait(sems, src, land, after, tag):
    def arrivals(_, land_ref, place):
        return [(land_ref.at[_device_index(place)], land_ref.at[_device_index(peer)]) for peer in _peers(place)]

    return _exchange_wait(f"small_wait_{tag}", sems, src, land, after, arrivals)


def sum_devices(land):
    _, rr, cc = land.shape
    tr = 56
    assert rr % tr == 0, rr

    def body(l_ref, o_ref):
        acc = l_ref[0]
        for d in range(1, N_DEV):
            acc = acc + l_ref[d]
        o_ref[...] = acc

    return pl.pallas_call(
        body, name="sum_devices", grid=(rr // tr,),
        in_specs=[pl.BlockSpec((N_DEV, tr, cc), lambda i: (0, i, 0))],
        out_specs=pl.BlockSpec((tr, cc), lambda i: (i, 0)), out_shape=_sds((rr, cc), F32),
        compiler_params=_cp("parallel"),
    )(land)


_BIG = ["mla_w_down", "mla_w_uq", "mla_w_ukv", "mla_w_out", "gmlp_w_in", "gmlp_w_out", "ffn_w_up", "ffn_w_down",
        "ple_w_gate", "ple_w_proj"]
_SMALL_REST = ["norm_mix", "norm_ffn", "norm_ple", "mla_q_lora_g", "mla_kv_lora_g", "mla_q_nope_g", "mla_q_rope_g",
               "mla_k_nope_g", "mla_k_rope_g"]
_SMALL_GMLP = ["gmlp_ln_g", "gmlp_ln_b", "gmlp_w_s", "gmlp_b_s"]
_SMALL = _SMALL_REST + _SMALL_GMLP

_LAY_MLA = dict(up=0, down=1024, out=2048, gate=2304, wdn=2560, wuq=2736, wukv=2880, proj=3008, rows=3072)
_LAY_MLA_MAIN = dict(up=0, down=1024, out=2048, gate=2304, rows=2560)
_LAY_MLA_ODD = dict(wdn=0, wuq=176, wukv=320, proj=448, rows=512)
_LAY_GMLP = {"up": 0, "down": 1024, "in": 2048, "out": 3072, "gate": 3584, "proj": 3840, "ln": 3904, "rows": 4096}
SPLIT_LAYERS = (0,)


def _layer_units(i):
    j = i // 2
    if i % 2 == 0:
        odd, lay = (_LAY_MLA_ODD, _LAY_MLA_MAIN) if i in SPLIT_LAYERS else (_LAY_MLA, _LAY_MLA)
        small = [("mla_w_down", j, odd["wdn"]), ("mla_w_uq", j, odd["wuq"]), ("mla_w_ukv", j, odd["wukv"]),
                 ("ple_w_proj", i, odd["proj"])]
        large = [("ffn_w_up", i, lay["up"]), ("ffn_w_down", i, lay["down"]), ("mla_w_out", j, lay["out"]),
                 ("ple_w_gate", i, lay["gate"])]
        return [("odd", odd, small), ("main", lay, large)] if i in SPLIT_LAYERS else [("main", lay, large + small)]
    lay = _LAY_GMLP
    return [("main", lay, [("ffn_w_up", i, lay["up"]), ("ffn_w_down", i, lay["down"]), ("gmlp_w_in", j, lay["in"]),
                           ("gmlp_w_out", j, lay["out"]), ("ple_w_gate", i, lay["gate"]),
                           ("ple_w_proj", i, lay["proj"])])]


def _pack_rows(parts, dtype, pad_to=None, slot=False):
    size = sum(p.size for p in parts)
    tail = [] if pad_to is None or pad_to * D == size else [jnp.zeros((pad_to * D - size,), dtype)]
    shape = (1, -1, D) if slot else (-1, D)
    if all(p.size % D == 0 for p in parts + tail):
        return jnp.concatenate([p.astype(dtype).reshape(shape) for p in parts + tail], axis=len(shape) - 2)
    return jnp.concatenate([p.astype(dtype).reshape(-1) for p in parts + tail]).reshape(shape)


def _odd(allw, row0, a, b):
    return allw[:, row0:row0 + a * b // D].reshape(N_CHIPS, a, b)


def _cols_joined(s):
    return jnp.transpose(s, (1, 0, 2)).reshape(s.shape[1], N_CHIPS * s.shape[2])


def _col_shards(full):
    a, bb = full.shape
    return jnp.transpose(full.reshape(a, N_CHIPS, bb // N_CHIPS), (1, 0, 2)).reshape(N_CHIPS, -1, D)


def _pad_lanes(g):
    return jnp.pad(g, ((0, 0), (0, LANES - g.shape[1])))


def _split_uq(wuq):
    l = wuq.shape[0]
    w = wuq.reshape(l, QL, HEADS, DN + DR)
    nope = w[..., :DN].reshape(l, QL, HEADS * DN)
    rope = jnp.pad(w[..., DN:], ((0, 0), (0, 0), (0, 0), (0, LANES - DR))).reshape(l, QL, HEADS * LANES)
    return jnp.concatenate([nope, rope], axis=-1)


def _merge_uq(d):
    nope = d[:, :HEADS * DN].reshape(QL, HEADS, DN)
    rope = d[:, HEADS * DN:].reshape(QL, HEADS, LANES)[..., :DR]
    return jnp.concatenate([nope, rope], axis=-1).reshape(QL, HEADS * (DN + DR))


def _rope_tables(positions):
    inv_freq = ROPE_BASE ** (-(jnp.arange(0, DR, 2, dtype=F32) / DR))
    ang = positions.reshape(-1).astype(F32)[:, None] * inv_freq
    z = jnp.zeros((ang.shape[0], LANES - DR), F32)
    return (jnp.concatenate([jnp.cos(ang), jnp.cos(ang), z], axis=1),
            jnp.concatenate([jnp.sin(ang), jnp.sin(ang), z], axis=1))


def kernel(x, p, positions, norm_mix, norm_ffn, norm_ple, mla_w_down, mla_q_lora_g, mla_kv_lora_g, mla_w_uq, mla_w_ukv, mla_q_nope_g, mla_q_rope_g, mla_k_nope_g, mla_k_rope_g, mla_w_out, gmlp_w_in, gmlp_ln_g, gmlp_ln_b, gmlp_w_s, gmlp_b_s, gmlp_w_out, ffn_w_up, ffn_w_down, ple_w_gate, ple_w_proj, loss_target, m_norm_mix, m_norm_ffn, m_norm_ple, m_mla_w_down, m_mla_q_lora_g, m_mla_kv_lora_g, m_mla_w_uq, m_mla_w_ukv, m_mla_q_nope_g, m_mla_q_rope_g, m_mla_k_nope_g, m_mla_k_rope_g, m_mla_w_out, m_gmlp_w_in, m_gmlp_ln_g, m_gmlp_ln_b, m_gmlp_w_s, m_gmlp_b_s, m_gmlp_w_out, m_ffn_w_up, m_ffn_w_down, m_ple_w_gate, m_ple_w_proj, v_norm_mix, v_norm_ffn, v_norm_ple, v_mla_w_down, v_mla_q_lora_g, v_mla_kv_lora_g, v_mla_w_uq, v_mla_w_ukv, v_mla_q_nope_g, v_mla_q_rope_g, v_mla_k_nope_g, v_mla_k_rope_g, v_mla_w_out, v_gmlp_w_in, v_gmlp_ln_g, v_gmlp_ln_b, v_gmlp_w_s, v_gmlp_b_s, v_gmlp_w_out, v_ffn_w_up, v_ffn_w_down, v_ple_w_gate, v_ple_w_proj):
    args = dict(locals())
    weights = {n: args[n] for n in _BIG + _SMALL}
    depth = norm_mix.shape[0]
    nb, seq, _ = x.shape
    t = nb * seq
    assert seq % TQ == 0 and seq % TM == 0 and t % 512 == 0, (nb, seq)
    cx = lax.axis_index("x")
    cy = lax.axis_index("y")
    cc = lax.axis_index("c")
    chip = 2 * cx + cy

    gathers = {}
    token = None
    for i in range(depth):
        for key, lay, parts in _layer_units(i):
            rows = [weights[n][l] for n, l, _ in parts]
            if token is not None:
                rows[0] = rows[0] + token[0, 0]
            if "ln" in lay:
                ln = jnp.stack([gmlp_ln_g[i // 2], gmlp_ln_b[i // 2]]).astype(F32)
                bits = lax.bitcast_convert_type(ln, BF16).reshape(-1)
                rows.append(jnp.pad(bits, (0, 16 * D - bits.size)).reshape(16, D))
            mine = _pack_rows(rows, BF16, pad_to=lay["rows"], slot=True)
            land = lax.dynamic_update_slice(lax.empty((N_CHIPS, lay["rows"], D), BF16), mine, (chip, 0, 0))
            sems, src, land, token = gather_start(land, f"{i}{key}")
            gathers[i, key] = (sems, src, land)
    allw = [None] * depth

    tril = jnp.tril(jnp.ones((GC, GC), F32))
    wm = (gmlp_w_s * tril).astype(BF16)
    wmt = jnp.swapaxes(wm, -1, -2)
    bfull = jnp.repeat(jnp.swapaxes(gmlp_b_s, -1, -2), GD, axis=-1)
    cos, sin = _rope_tables(positions)
    row = lambda g: g.reshape(1, -1)
    gqr = _pad_lanes(mla_q_rope_g)
    gkr = _pad_lanes(mla_k_rope_g)

    h = x.reshape(t, D)
    pt = p.reshape(depth, t, PLE)
    saved = []

    passing = {}

    def arrive(i, key, after):
        sems, src, land = gathers[i, key]
        _, land = gather_wait(sems, src, land, after, f"{i}{key}")
        passing[i, key] = pass_start(land, f"{i}{key}")
        return passing[i, key][3]

    def needed(i, key, after=None):
        sems, src, land, tok = passing.pop((i, key))
        return pass_wait(sems, src, land, tok if after is None else after, f"{i}{key}")[1]

    arrive(0, _layer_units(0)[0][0], token)
    for i in range(depth):
        j = i // 2
        lay = _layer_units(i)[-1][1]
        s = dict(h=h)
        if i % 2 == 0:
            split = i in SPLIT_LAYERS
            olay = _layer_units(i)[0][1]
            odd = needed(i, "odd" if split else "main", None if i == 0 else h)
            wdn = jnp.pad(_odd(odd, olay["wdn"], D // N_CHIPS, LAT).reshape(D, LAT), ((0, 0), (0, LATP - LAT)))
            wuq = _split_uq(_cols_joined(_odd(odd, olay["wuq"], QL, 384))[None])[0]
            wukv = _cols_joined(_odd(odd, olay["wukv"], KVL, 512))
            wp = _cols_joined(_odd(odd, olay["proj"], PLE, 256))
            mla_args = (row(norm_mix[i]), wdn, row(mla_q_lora_g[j]), row(mla_kv_lora_g[j]), wuq, wukv,
                        row(mla_q_nope_g[j]), gqr[j:j + 1], row(mla_k_nope_g[j]), gkr[j:j + 1], cos, sin)
            q, k, v = mla_pre_fwd(h, *mla_args)
            y, lse = flash_fwd(q, k, v, seq)
            if split and i == 0:
                arrive(i, "main", y)
            aw = needed(i, "main", y) if split else odd
            s.update(q=q, k=k, v=v, lse=lse, mla_args=mla_args)
        else:
            aw = needed(i, "main", h)
            ln = lax.bitcast_convert_type(aw[:, lay["ln"]:lay["ln"] + 2].reshape(N_CHIPS, 2, GH // N_CHIPS, 2), F32)
            ln = jnp.transpose(ln, (1, 0, 2)).reshape(2, 1, GH)
            wp = _cols_joined(_odd(aw, lay["proj"], PLE, 256))
            y, pre = gmlp_fwd(h, row(norm_mix[i]), aw, lay, ln[0], ln[1], wm[j], bfull[j])
            s.update(pre=pre, ln=ln)
        allw[i] = aw
        g2 = row(norm_ffn[i])
        if i + 1 < depth:
            for key, _, _ in _layer_units(i + 1):
                g2 = g2 + arrive(i + 1, key, y)[0:1, 0:1]
        h1, h2, hn2, r = mixffn_fwd(h, y, aw, lay, g2)
        h, hn3 = ple_fwd(h2, pt, i, row(norm_ple[i]), aw, lay, wp)
        s.update(y=y, wp=wp, h1=h1, h2=h2, hn2=hn2, r=r, hn3=hn3)
        saved.append(s)

    dh, loss_part = loss_head(h, loss_target.reshape(t, D))

    gs = {n: [None] * weights[n].shape[0] for n in _SMALL}
    gw = {n: [None] * weights[n].shape[0] for n in _BIG}
    place = jnp.stack([cc, chip]).astype(jnp.int32)
    scatters = []
    swaps = []
    token = None

    def put(b, row0, shards):
        return lax.dynamic_update_slice(b, shards.reshape(N_CHIPS, -1, D), (0, row0, 0))

    def small_size(n):
        return weights[n].shape[0] * GH if n in ("gmlp_ln_g", "gmlp_ln_b") else weights[n].size

    def small_exchange(names, zero, tag, extra=()):
        rows = -(-(sum(small_size(n) for n in names) + sum(e.size for e in extra)) // (56 * D)) * 56
        part = [jnp.stack(gs[n]) for n in names] + list(extra)
        part = _pack_rows([part[0] + zero] + part[1:], F32, pad_to=rows, slot=True)
        land = lax.dynamic_update_slice(lax.empty((N_DEV, rows, D), F32), part, (2 * chip + cc, 0, 0))
        return small_start(land, tag)

    def swap(i, key, buf):
        sems, buf, got, tok = swap_start(buf, f"{i}{key}")
        swaps.append((i, key, sems, buf, got))
        return tok

    def swapped(after, zero):
        while swaps:
            i, key, sems, g, got = swaps.pop(0)
            g, got = swap_wait(sems, g, got, after, f"{i}{key}")
            own, sums = chip_sum(place, g, got)
            sems, sums, land, tok = scatter_start(sums, f"{i}{key}")
            scatters.append((i, key, own, sems, sums, land))
            zero = zero + tok[0:1, 0:1]
        return zero

    for i in reversed(range(depth)):
        j = i // 2
        lay = _layer_units(i)[-1][1]
        aw = allw[i]
        s = saved[i]

        g3 = row(norm_ple[i])
        if token is not None:
            g3 = g3 + token[0:1, 0:1]
        dh2, dh2b, dgt, dpp, dg3 = ple_bwd(dh, s["h2"], pt, i, g3, aw, lay, s["wp"])
        gs["norm_ple"][i] = dg3[0]
        buf = mm_tn_into(lay["rows"], s["hn3"], dgt, D // N_CHIPS, lay["gate"], False)
        dproj = _col_shards(mm_tn(pt, dpp, layer=i))
        if "ln" in lay:
            buf = put(buf, lay["ln"], jnp.zeros((N_CHIPS, lay["rows"] - lay["ln"], D), F32))
            buf = put(buf, lay["proj"], dproj)
        dh1, dh1b, du, a, dg2 = ffn_bwd(dh2, dh2b, s["h1"], s["r"], row(norm_ffn[i]), aw, lay)
        gs["norm_ffn"][i] = dg2[0]
        buf = mm_tn_into(buf, a, dh2b, D, lay["down"], False)
        buf = mm_tn_into(buf, s["hn2"], du, D, lay["up"], True)
        buf = mm_tn_into(buf, s["y"], dh1b, s["y"].shape[1] // N_CHIPS, lay["out"], False)
        g1 = swapped(dh1, row(norm_mix[i]))
        if i % 2 == 0:
            split = i in SPLIT_LAYERS
            do = linear_nt(dh1b, aw, D // N_CHIPS, lay["out"])
            dq, dk, dv = flash_bwd(s["q"], s["k"], s["v"], s["y"], do, s["lse"], seq,
                                   after=swap(i, "main", buf) if split else dh1b)
            g1 = swapped(dq, g1)
            (dh, hn1, cq, ckv, dqp, dkvp, dlat, dg1, dgq, dgkv, dgqn, dgqr, dgkn, dgkr) = mla_pre_bwd(
                dq, dk, dv, dh1, s["h"], g1, *s["mla_args"][1:])
            gs["norm_mix"][i] = dg1[0]
            gs["mla_q_lora_g"][j] = dgq[0]
            gs["mla_kv_lora_g"][j] = dgkv[0]
            gs["mla_q_nope_g"][j] = dgqn[0]
            gs["mla_q_rope_g"][j] = dgqr[0, :DR]
            gs["mla_k_nope_g"][j] = dgkn[0]
            gs["mla_k_rope_g"][j] = dgkr[0, :DR]
            small = [mm_tn(hn1, dlat)[:, :LAT].reshape(N_CHIPS, -1, D), _col_shards(_merge_uq(mm_tn(cq, dqp))),
                     _col_shards(mm_tn(ckv, dkvp)), dproj]
            if split:
                buf = jnp.concatenate(small, axis=1)
            else:
                buf = put(buf, lay["wdn"], jnp.concatenate(small, axis=1))
            key = "odd" if split else "main"
        else:
            dh, hn1, dpre, dws, dbs, dlng, dlnb, dg1 = gmlp_bwd(
                dh1, dh1b, s["h"], s["pre"], g1, aw, lay, s["ln"][0], s["ln"][1], wm[j], wmt[j], bfull[j], tril)
            gs["norm_mix"][i] = dg1[0]
            gs["gmlp_ln_g"][j] = dlng[0]
            gs["gmlp_ln_b"][j] = dlnb[0]
            gs["gmlp_w_s"][j] = dws
            gs["gmlp_b_s"][j] = jnp.sum(dbs.reshape(GC, GG, GD), axis=-1).T
            buf = mm_tn_into(buf, hn1, dpre, D, lay["in"], True)
            key = "main"
        token = swap(i, key, buf)
        if i == 1:
            small_gmlp = small_exchange(_SMALL_GMLP, token[0, 0], "gmlp")
            token = token + small_gmlp[3]
    last = swapped(dh, jnp.zeros((1, 1), F32))
    grad_x = dh.reshape(x.shape)
    small_rest = small_exchange(_SMALL_REST, last[0, 0], "rest", extra=[loss_part[0, 0:1]])

    after = small_rest[3]
    shares = []
    for i, key, own, sems, sums, land in scatters:
        _, got = scatter_wait(sems, sums, land, after, f"{i}{key}")
        sems, src, full, after = share_start(final_sum(place, own, got), f"{i}{key}")
        shares.append((i, key, sems, src, full))
    where = {n: [None] * weights[n].shape[0] for n in _BIG}
    for i, key, sems, src, full in shares:
        _, after = share_wait(sems, src, full, after, f"{i}{key}")
        for n, l, row0 in dict((k, parts) for k, _, parts in _layer_units(i))[key]:
            where[n][l] = (after, row0)
            if weights[n].shape[-1] != D:
                gw[n][l] = after[row0:row0 + weights[n][l].size // D].reshape(weights[n].shape[1:])
    grads = {n: jnp.stack(gw[n]) for n in _BIG if weights[n].shape[-1] != D}

    tot = []
    for names, (sems, src, land, _), tag in ((_SMALL_REST, small_rest, "rest"), (_SMALL_GMLP, small_gmlp, "gmlp")):
        summed = sum_devices(small_wait(sems, src, land, after, tag)[1]).reshape(-1)
        tot.append(summed[:sum(small_size(n) for n in names)])
        if tag == "rest":
            loss = summed[tot[-1].size]
    tot = jnp.concatenate(tot)
    off = 0
    for n, sz in ((n, small_size(n)) for n in _SMALL_REST + _SMALL_GMLP):
        gsum = tot[off:off + sz]
        off += sz
        if n in ("gmlp_ln_g", "gmlp_ln_b"):
            gsum = lax.dynamic_slice_in_dim(gsum.reshape(-1, GH), chip * (GH // N_CHIPS), GH // N_CHIPS, axis=1)
        grads[n] = gsum.reshape(weights[n].shape)

    delta, new_m, new_v = {}, {}, {}
    for n in _BIG:
        if weights[n].shape[-1] == D:
            grads[n], delta[n], new_m[n], new_v[n] = adamw_layers(
                weights[n], args["m_" + n], args["v_" + n], [b for b, _ in where[n]], [r for _, r in where[n]])
            continue
        w2 = weights[n].reshape(-1, weights[n].shape[-1])
        d, mn, vn = adamw(w2, grads[n].reshape(w2.shape), args["m_" + n].reshape(w2.shape),
                          args["v_" + n].reshape(w2.shape))
        delta[n], new_m[n], new_v[n] = (a.reshape(weights[n].shape) for a in (d, mn, vn))
    own_sizes = [weights[n].size for n in _SMALL]
    own_rows = -(-sum(own_sizes) // (8 * D)) * 8
    packed = [_pack_rows([src[n] for n in _SMALL], F32, pad_to=own_rows)
              for src in (weights, grads, {n: args["m_" + n] for n in _SMALL}, {n: args["v_" + n] for n in _SMALL})]
    outs = adamw(*packed)
    off = 0
    for n, sz in zip(_SMALL, own_sizes):
        for dst, o in zip((delta, new_m, new_v), outs):
            dst[n] = o.reshape(-1)[off:off + sz].reshape(weights[n].shape)
        off += sz

    order = ["norm_mix", "norm_ffn", "norm_ple", "mla_w_down", "mla_q_lora_g", "mla_kv_lora_g", "mla_w_uq",
             "mla_w_ukv", "mla_q_nope_g", "mla_q_rope_g", "mla_k_nope_g", "mla_k_rope_g", "mla_w_out", "gmlp_w_in",
             "gmlp_ln_g", "gmlp_ln_b", "gmlp_w_s", "gmlp_b_s", "gmlp_w_out", "ffn_w_up", "ffn_w_down", "ple_w_gate",
             "ple_w_proj"]
    return (loss, grad_x, *[grads[n] for n in order], *[delta[n] for n in order], *[new_m[n] for n in order],
            *[new_v[n] for n in order])
```

```python
import functools

import jax
import jax.numpy as jnp
from jax import lax
from jax.experimental import pallas as pl
from jax.experimental.pallas import tpu as pltpu

F32 = jnp.float32
BF16 = jnp.bfloat16
MESH = pl.DeviceIdType.MESH

D = 1024
HEADS = 8
DN = 128
DR = 64
QL = 384
KVL = 256
LAT = 704
LATP = 768
DFF = 4096
GH = 2048
GC = 128
GG = 8
GD = 256
PLE = 256
EPS = 1e-6
ROPE_BASE = 10000.0
SM_SCALE = (DN + DR) ** -0.5
N_CHIPS = 4
LANES = 128

ADAM_LR = 0.001
ADAM_B1 = 0.9
ADAM_B2 = 0.999
ADAM_EPS = 1e-08
ADAM_WD = 0.01
ADAM_STEP = 10

TM = 256
TMB = 512
TQ = 512
TQ_FWD = 512
FWD_HEADS = 2
BWD_HEADS = 2
SUM_ROWS = 256
VMEM_LIMIT = 56 * 1024 * 1024


def _cp(*sem):
    return pltpu.CompilerParams(dimension_semantics=sem, vmem_limit_bytes=VMEM_LIMIT)


def _dot(a, b):
    return jnp.dot(a, b, preferred_element_type=F32)


def _dot_nt(a, b):
    return lax.dot_general(a, b, (((1,), (1,)), ((), ())), preferred_element_type=F32)


def _dot_tn(a, b):
    return lax.dot_general(a, b, (((0,), (0,)), ((), ())), preferred_element_type=F32)


def _rms(x, g, n):
    r = lax.rsqrt(jnp.sum(x * x, axis=-1, keepdims=True) * (1.0 / n) + EPS)
    xhat = x * r
    return xhat * g, xhat, r


def _rms_bwd(dy, g, xhat, r, n):
    dxhat = dy * g
    return r * (dxhat - xhat * (jnp.sum(dxhat * xhat, axis=-1, keepdims=True) * (1.0 / n)))


def _rope(x, c, s):
    return x * c + (pltpu.roll(x, 32, 1) - pltpu.roll(x, 96, 1)) * s


def _rope_t(dy, c, s):
    w = dy * s
    return dy * c + pltpu.roll(w, 96, 1) - pltpu.roll(w, 32, 1)


def _sigmoid(x):
    return 1.0 / (1.0 + jnp.exp(-x))


_GELU_K = 0.7978845608028654
_GELU_C = 0.044715


def _gelu(x):
    return 0.5 * x * (1.0 + jnp.tanh(_GELU_K * (x + _GELU_C * x * x * x)))


def _gelu_and_grad(x):
    x2 = x * x
    t = jnp.tanh(_GELU_K * (x + _GELU_C * x2 * x))
    half = 0.5 * (1.0 + t)
    return x * half, half + 0.5 * x * (1.0 - t * t) * (_GELU_K * (1.0 + 3.0 * _GELU_C * x2))


def _acc_rows(ref, val):
    ref[...] += jnp.broadcast_to(jnp.sum(val, axis=0, keepdims=True), ref.shape)


def _row(tm, c):
    return pl.BlockSpec((tm, c), lambda i: (i, 0))


def _const(shape):
    nd = len(shape)
    return pl.BlockSpec(shape, lambda i: (0,) * nd, pipeline_mode=pl.Buffered(1))


def _wblk(rows, row0):
    assert row0 % rows == 0, (rows, row0)
    return pl.BlockSpec((N_CHIPS, rows, D), lambda i: (0, row0 // rows, 0), pipeline_mode=pl.Buffered(1))


def _rows_joined(w_ref):
    return w_ref[...].reshape(N_CHIPS * w_ref.shape[1], D)


def _sds(shape, dtype):
    return jax.ShapeDtypeStruct(shape, dtype)


def mixffn_fwd(h, y, allw, lay, g2):
    t, k = y.shape

    def body(h_ref, y_ref, wo_ref, g_ref, wu_ref, wd_ref, h1_ref, h2_ref, hn_ref, r_ref):
        h1 = h_ref[...] + _dot(y_ref[...], _rows_joined(wo_ref))
        h1_ref[...] = h1
        yn, _, _ = _rms(h1, g_ref[...], D)
        hn = yn.astype(BF16)
        hn_ref[...] = hn
        f = jnp.zeros((TMB, D), F32)
        for c in range(N_CHIPS):
            r = jnp.maximum(_dot(hn, wu_ref[c]), 0.0)
            r_ref[:, c * D:(c + 1) * D] = r.astype(BF16)
            f = f + _dot((r * r).astype(BF16), wd_ref[c])
        h2_ref[...] = h1 + f

    return pl.pallas_call(
        body, name="mixffn_fwd", grid=(t // TMB,),
        in_specs=[_row(TMB, D), _row(TMB, k), _wblk(k // N_CHIPS, lay["out"]), _const((1, D)), _wblk(D, lay["up"]),
                  _wblk(D, lay["down"])],
        out_specs=[_row(TMB, D), _row(TMB, D), _row(TMB, D), _row(TMB, DFF)],
        out_shape=[_sds((t, D), F32), _sds((t, D), F32), _sds((t, D), BF16), _sds((t, DFF), BF16)],
        compiler_params=_cp("parallel"),
    )(h, y, allw, g2, allw, allw)


def _layer_rows(tm, c, layer):
    return pl.BlockSpec((None, tm, c), lambda i: (layer, i, 0))


def ple_fwd(h2, p, layer, g3, allw, lay, wp):
    t = h2.shape[0]

    def body(h_ref, p_ref, g_ref, wg_ref, wp_ref, h3_ref, hn_ref):
        x = h_ref[...]
        yn, _, _ = _rms(x, g_ref[...], D)
        hn = yn.astype(BF16)
        hn_ref[...] = hn
        gt = _dot(hn, _rows_joined(wg_ref))
        pp = _dot(p_ref[...].astype(BF16), wp_ref[...])
        h3_ref[...] = x + _sigmoid(gt) * pp

    return pl.pallas_call(
        body, name="ple_fwd", grid=(t // TMB,),
        in_specs=[_row(TMB, D), _layer_rows(TMB, PLE, layer), _const((1, D)), _wblk(D // N_CHIPS, lay["gate"]),
                  _const((PLE, D))],
        out_specs=[_row(TMB, D), _row(TMB, D)],
        out_shape=[_sds((t, D), F32), _sds((t, D), BF16)],
        compiler_params=_cp("parallel"),
    )(h2, p, g3, allw, wp)


def _mla_project(h_ref, g1_ref, wdn_ref, gq_ref, gkv_ref, wuq_ref, wukv_ref):
    x = h_ref[...]
    yn, xhat, rx = _rms(x, g1_ref[...], D)
    hn = yn.astype(BF16)
    lat = _dot(hn, wdn_ref[...])
    cq, cqhat, rq = _rms(lat[:, :QL], gq_ref[...], QL)
    ckv, ckvhat, rkv = _rms(lat[:, QL:QL + KVL], gkv_ref[...], KVL)
    kr_raw = lat[:, QL + KVL:]
    cqb = cq.astype(BF16)
    ckvb = ckv.astype(BF16)
    qp = _dot(cqb, wuq_ref[...])
    kvp = _dot(ckvb, wukv_ref[...])
    return dict(xhat=xhat, rx=rx, hn=hn, cqhat=cqhat, rq=rq, ckvhat=ckvhat, rkv=rkv, kr_raw=kr_raw,
                cqb=cqb, ckvb=ckvb, qp=qp, kvp=kvp)


def mla_pre_fwd(h, g1, wdn, gq, gkv, wuq, wukv, gqn, gqr, gkn, gkr, cos, sin):
    t = h.shape[0]

    def body(h_ref, g1_ref, wdn_ref, gq_ref, gkv_ref, wuq_ref, wukv_ref, gqn_ref, gqr_ref, gkn_ref, gkr_ref,
             c_ref, s_ref, q_ref, k_ref, v_ref):
        m = _mla_project(h_ref, g1_ref, wdn_ref, gq_ref, gkv_ref, wuq_ref, wukv_ref)
        c = c_ref[...]
        s = s_ref[...]
        kr, _, _ = _rms(m["kr_raw"], gkr_ref[...], DR)
        krb = _rope(kr, c, s).astype(BF16)
        for hd in range(HEADS):
            qn, _, _ = _rms(m["qp"][:, hd * DN:(hd + 1) * DN], gqn_ref[...], DN)
            qr, _, _ = _rms(m["qp"][:, D + hd * LANES:D + (hd + 1) * LANES], gqr_ref[...], DR)
            q_ref[hd, :, 0:DN] = (qn * SM_SCALE).astype(BF16)
            q_ref[hd, :, DN:2 * DN] = (_rope(qr, c, s) * SM_SCALE).astype(BF16)
            kn, _, _ = _rms(m["kvp"][:, hd * 2 * DN:hd * 2 * DN + DN], gkn_ref[...], DN)
            k_ref[hd, :, 0:DN] = kn.astype(BF16)
            k_ref[hd, :, DN:2 * DN] = krb
            v_ref[hd] = m["kvp"][:, hd * 2 * DN + DN:(hd + 1) * 2 * DN].astype(BF16)

    hb = lambda w: pl.BlockSpec((HEADS, TM, w), lambda i: (0, i, 0))
    return pl.pallas_call(
        body, name="mla_pre_fwd", grid=(t // TM,),
        in_specs=[_row(TM, D), _const((1, D)), _const((D, LATP)), _const((1, QL)), _const((1, KVL)),
                  _const((QL, 2 * D)), _const((KVL, 2 * D)), _const((1, LANES)), _const((1, LANES)),
                  _const((1, LANES)), _const((1, LANES)), _row(TM, LANES), _row(TM, LANES)],
        out_specs=[hb(2 * DN), hb(2 * DN), hb(DN)],
        out_shape=[_sds((HEADS, t, 2 * DN), BF16), _sds((HEADS, t, 2 * DN), BF16), _sds((HEADS, t, DN), BF16)],
        compiler_params=_cp("parallel"),
    )(h, g1, wdn, gq, gkv, wuq, wukv, gqn, gqr, gkn, gkr, cos, sin)


def _diagonal_mask(n=TQ):
    return lax.broadcasted_iota(jnp.int32, (n, n), 1) <= lax.broadcasted_iota(jnp.int32, (n, n), 0)


def flash_fwd(q, k, v, seq):
    t = q.shape[1]
    nb = t // seq
    tq = TQ_FWD
    nq = seq // tq
    hp = FWD_HEADS

    def body(q_ref, k_ref, v_ref, o_ref, lse_ref):
        qi = pl.program_id(2)
        qs = [q_ref[a] for a in range(hp)]

        def step(j, carry, diagonal=False):
            rows = pl.ds(pl.multiple_of(j * tq, tq), tq)
            out = []
            for a in range(hp):
                m, l, acc = carry[a]
                s = _dot_nt(qs[a], k_ref[a, rows, :])
                if diagonal:
                    s = jnp.where(_diagonal_mask(tq), s, -1e30)
                m_new = jnp.maximum(m, jnp.max(s, axis=-1, keepdims=True))
                p = jnp.exp(s - m_new)
                alpha = jnp.exp(m - m_new)
                l = alpha * l + jnp.sum(p, axis=-1, keepdims=True)
                acc = alpha * acc + _dot(p.astype(BF16), v_ref[a, rows, :])
                out.append((m_new, l, acc))
            return tuple(out)

        one = (jnp.full((tq, 1), -1e30, F32), jnp.zeros((tq, 1), F32), jnp.zeros((tq, DN), F32))
        done = step(qi, lax.fori_loop(0, qi, step, (one,) * hp), diagonal=True)
        for a, (m, l, acc) in enumerate(done):
            o_ref[:, a * DN:(a + 1) * DN] = (acc / l).astype(BF16)
            lse_ref[a] = m + jnp.log(l)

    return pl.pallas_call(
        body, name="flash_fwd", grid=(nb, HEADS // hp, nq),
        in_specs=[pl.BlockSpec((hp, tq, 2 * DN), lambda b, h, i: (h, b * nq + i, 0)),
                  pl.BlockSpec((hp, seq, 2 * DN), lambda b, h, i: (h, b, 0)),
                  pl.BlockSpec((hp, seq, DN), lambda b, h, i: (h, b, 0))],
        out_specs=[pl.BlockSpec((tq, hp * DN), lambda b, h, i: (b * nq + i, h)),
                   pl.BlockSpec((hp, tq, 1), lambda b, h, i: (h, b * nq + i, 0))],
        out_shape=[_sds((t, HEADS * DN), BF16), _sds((HEADS, t, 1), F32)],
        compiler_params=_cp("parallel", "parallel", "arbitrary"),
    )(q, k, v)


def _gmlp_in(hn, win_ref):
    pre = [_dot(hn, win_ref[c]) for c in range(N_CHIPS)]
    return jnp.concatenate(pre[:2], axis=1), jnp.concatenate(pre[2:], axis=1)


def gmlp_fwd(h, g1, allw, lay, lng, lnb, wm, bfull):
    t = h.shape[0]

    def body(h_ref, g1_ref, win_ref, lng_ref, lnb_ref, wm_ref, b_ref, y_ref, pre_ref):
        yn, _, _ = _rms(h_ref[...], g1_ref[...], D)
        pre_u, pre_v = _gmlp_in(yn.astype(BF16), win_ref)
        pre_ref[:, :GH] = pre_u.astype(BF16)
        pre_ref[:, GH:] = pre_v.astype(BF16)
        u = _gelu(pre_u)
        v = _gelu(pre_v)
        xc = v - jnp.mean(v, axis=-1, keepdims=True)
        rs = lax.rsqrt(jnp.mean(xc * xc, axis=-1, keepdims=True) + EPS)
        vnb = (xc * rs * lng_ref[...] + lnb_ref[...]).astype(BF16)
        for ch in range(TM // GC):
            rows = slice(ch * GC, (ch + 1) * GC)
            for g in range(GG):
                cols = slice(g * GD, (g + 1) * GD)
                sv = _dot(wm_ref[g], vnb[rows, cols]) + b_ref[:, cols]
                y_ref[rows, cols] = (u[rows, cols] * sv).astype(BF16)

    return pl.pallas_call(
        body, name="gmlp_fwd", grid=(t // TM,),
        in_specs=[_row(TM, D), _const((1, D)), _wblk(D, lay["in"]), _const((1, GH)), _const((1, GH)),
                  _const((GG, GC, GC)), _const((GC, GH))],
        out_specs=[_row(TM, GH), _row(TM, 2 * GH)],
        out_shape=[_sds((t, GH), BF16), _sds((t, 2 * GH), BF16)],
        compiler_params=_cp("parallel"),
    )(h, g1, allw, lng, lnb, wm, bfull)


def loss_head(h, tgt):
    t = h.shape[0]

    def body(h_ref, t_ref, dh_ref, loss_ref):
        @pl.when(pl.program_id(0) == 0)
        def _():
            loss_ref[...] = jnp.zeros_like(loss_ref)

        e = h_ref[...] - t_ref[...]
        dh_ref[...] = e * (1.0 / D)
        part = jnp.sum(jnp.sum(e * e, axis=-1, keepdims=True), axis=0, keepdims=True) * (0.5 / D)
        loss_ref[...] += jnp.broadcast_to(part, loss_ref.shape)

    return pl.pallas_call(
        body, name="loss_head", grid=(t // TMB,),
        in_specs=[_row(TMB, D), _row(TMB, D)],
        out_specs=[_row(TMB, D), _const((8, LANES))],
        out_shape=[_sds((t, D), F32), _sds((8, LANES), F32)],
        compiler_params=_cp("arbitrary"),
    )(h, tgt)


def _zero_at_first_step(*refs):
    @pl.when(pl.program_id(0) == 0)
    def _():
        for r in refs:
            r[...] = jnp.zeros_like(r)


def ple_bwd(dh3, h2, p, layer, g3, allw, lay, wp):
    t = h2.shape[0]

    def body(dh_ref, h_ref, p_ref, g_ref, wg_ref, wp_ref, dh2_ref, dh2b_ref, dgt_ref, dpp_ref, dg_ref):
        _zero_at_first_step(dg_ref)
        dh3v = dh_ref[...]
        x = h_ref[...]
        g = g_ref[...]
        wg = _rows_joined(wg_ref)
        yn, xhat, r = _rms(x, g, D)
        gt = _dot(yn.astype(BF16), wg)
        pp = _dot(p_ref[...].astype(BF16), wp_ref[...])
        sg = _sigmoid(gt)
        dgt = (dh3v * pp * sg * (1.0 - sg)).astype(BF16)
        dgt_ref[...] = dgt
        dpp_ref[...] = (dh3v * sg).astype(BF16)
        dhn = _dot_nt(dgt, wg)
        _acc_rows(dg_ref, dhn * xhat)
        dh2 = dh3v + _rms_bwd(dhn, g, xhat, r, D)
        dh2_ref[...] = dh2
        dh2b_ref[...] = dh2.astype(BF16)

    return pl.pallas_call(
        body, name="ple_bwd", grid=(t // TMB,),
        in_specs=[_row(TMB, D), _row(TMB, D), _layer_rows(TMB, PLE, layer), _const((1, D)),
                  _wblk(D // N_CHIPS, lay["gate"]),
                  _const((PLE, D))],
        out_specs=[_row(TMB, D), _row(TMB, D), _row(TMB, D), _row(TMB, D), _const((8, D))],
        out_shape=[_sds((t, D), F32), _sds((t, D), BF16), _sds((t, D), BF16), _sds((t, D), BF16), _sds((8, D), F32)],
        compiler_params=_cp("arbitrary"),
    )(dh3, h2, p, g3, allw, wp)


def ffn_bwd(dh2, dh2b, h1, r, g2, allw, lay, mixer_rows=None):
    t = h1.shape[0]

    def body(dh_ref, dhb_ref, h_ref, r_ref, g_ref, wu_ref, wd_ref, *rest):
        wo_ref = rest[0] if mixer_rows else None
        dh1_ref, dh1b_ref, du_ref, a_ref, dg_ref = rest[-5:] if not mixer_rows else rest[1:6]
        _zero_at_first_step(dg_ref)
        dhb = dhb_ref[...]
        g = g_ref[...]
        _, xhat, rr = _rms(h_ref[...], g, D)
        dhn = jnp.zeros((TM, D), F32)
        for c in range(N_CHIPS):
            cs = slice(c * D, (c + 1) * D)
            rc = r_ref[:, cs].astype(F32)
            a_ref[:, cs] = (rc * rc).astype(BF16)
            da = _dot_nt(dhb, wd_ref[c])
            du = (da * (2.0 * rc)).astype(BF16)
            du_ref[:, cs] = du
            dhn = dhn + _dot_nt(du, wu_ref[c])
        _acc_rows(dg_ref, dhn * xhat)
        dh1 = dh_ref[...] + _rms_bwd(dhn, g, xhat, rr, D)
        dh1_ref[...] = dh1
        dh1b = dh1.astype(BF16)
        dh1b_ref[...] = dh1b
        if mixer_rows:
            rest[6][...] = _dot_nt(dh1b, _rows_joined(wo_ref)).astype(BF16)

    k = N_CHIPS * mixer_rows if mixer_rows else 0
    return pl.pallas_call(
        body, name="ffn_bwd", grid=(t // TM,),
        in_specs=[_row(TM, D), _row(TM, D), _row(TM, D), _row(TM, DFF), _const((1, D)), _wblk(D, lay["up"]),
                  _wblk(D, lay["down"])] + ([_wblk(mixer_rows, lay["out"])] if mixer_rows else []),
        out_specs=[_row(TM, D), _row(TM, D), _row(TM, DFF), _row(TM, DFF), _const((8, D))]
        + ([_row(TM, k)] if mixer_rows else []),
        out_shape=[_sds((t, D), F32), _sds((t, D), BF16), _sds((t, DFF), BF16), _sds((t, DFF), BF16),
                   _sds((8, D), F32)] + ([_sds((t, k), BF16)] if mixer_rows else []),
        compiler_params=_cp("arbitrary"),
    )(dh2, dh2b, h1, r, g2, allw, allw, *([allw] if mixer_rows else []))


def flash_bwd(q, k, v, o, do, lse, seq, after):
    t = q.shape[1]
    nb = t // seq
    nq = seq // TQ
    hp = BWD_HEADS

    def body(q_ref, k_ref, v_ref, o_ref, do_ref, lse_ref, after_ref, dq_ref, dk_ref, dv_ref):
        del after_ref
        kj = pl.program_id(2)

        @pl.when(kj == 0)
        def _():
            dq_ref[...] = jnp.zeros_like(dq_ref)

        def step(i, carry, diagonal=False):
            rows = pl.ds(pl.multiple_of(i * TQ, TQ), TQ)
            out = []
            for a in range(hp):
                dk, dv = carry[a]
                kv = k_ref[a]
                qv = q_ref[a, rows, :]
                dov = do_ref[rows, a * DN:(a + 1) * DN]
                ov = o_ref[rows, a * DN:(a + 1) * DN]
                delta = jnp.sum(dov.astype(F32) * ov.astype(F32), axis=-1, keepdims=True)
                s = _dot_nt(qv, kv)
                if diagonal:
                    s = jnp.where(_diagonal_mask(), s, -1e30)
                p = jnp.exp(s - lse_ref[a, rows, :])
                dp = _dot_nt(dov, v_ref[a])
                ds = (p * (dp - delta)).astype(BF16)
                dv = dv + _dot_tn(p.astype(BF16), dov)
                dk = dk + _dot_tn(ds, qv)
                dq_ref[a, rows, :] += _dot(ds, kv)
                out.append((dk, dv))
            return tuple(out)

        one = (jnp.zeros((TQ, 2 * DN), F32), jnp.zeros((TQ, DN), F32))
        done = lax.fori_loop(kj + 1, nq, step, step(kj, (one,) * hp, diagonal=True))
        for a, (dk, dv) in enumerate(done):
            dk_ref[a] = dk
            dv_ref[a] = dv

    return pl.pallas_call(
        body, name="flash_bwd", grid=(nb, HEADS // hp, nq),
        in_specs=[pl.BlockSpec((hp, seq, 2 * DN), lambda b, h, j: (h, b, 0)),
                  pl.BlockSpec((hp, TQ, 2 * DN), lambda b, h, j: (h, b * nq + j, 0)),
                  pl.BlockSpec((hp, TQ, DN), lambda b, h, j: (h, b * nq + j, 0)),
                  pl.BlockSpec((seq, hp * DN), lambda b, h, j: (b, h)),
                  pl.BlockSpec((seq, hp * DN), lambda b, h, j: (b, h)),
                  pl.BlockSpec((hp, seq, 1), lambda b, h, j: (h, b, 0)), _ANY],
        out_specs=[pl.BlockSpec((hp, seq, 2 * DN), lambda b, h, j: (h, b, 0)),
                   pl.BlockSpec((hp, TQ, 2 * DN), lambda b, h, j: (h, b * nq + j, 0)),
                   pl.BlockSpec((hp, TQ, DN), lambda b, h, j: (h, b * nq + j, 0))],
        out_shape=[_sds((HEADS, t, 2 * DN), F32), _sds((HEADS, t, 2 * DN), F32), _sds((HEADS, t, DN), F32)],
        compiler_params=_cp("parallel", "parallel", "arbitrary"),
    )(q, k, v, o, do, lse, after)


def mla_pre_bwd(dq, dk, dv, dh1, h, g1, wdn, gq, gkv, wuq, wukv, gqn, gqr, gkn, gkr, cos, sin):
    t = h.shape[0]

    def body(dq_ref, dk_ref, dv_ref, dh1_ref, h_ref, g1_ref, wdn_ref, gq_ref, gkv_ref, wuq_ref, wukv_ref,
             gqn_ref, gqr_ref, gkn_ref, gkr_ref, c_ref, s_ref,
             dh_ref, hn_ref, cq_ref, ckv_ref, dqp_ref, dkvp_ref, dlat_ref,
             dg1_ref, dgq_ref, dgkv_ref, dgqn_ref, dgqr_ref, dgkn_ref, dgkr_ref):
        _zero_at_first_step(dg1_ref, dgq_ref, dgkv_ref, dgqn_ref, dgqr_ref, dgkn_ref, dgkr_ref)
        m = _mla_project(h_ref, g1_ref, wdn_ref, gq_ref, gkv_ref, wuq_ref, wukv_ref)
        hn_ref[...] = m["hn"]
        cq_ref[...] = m["cqb"]
        ckv_ref[...] = m["ckvb"]
        c = c_ref[...]
        s = s_ref[...]
        gqn = gqn_ref[...]
        gqr = gqr_ref[...]
        gkn = gkn_ref[...]
        gkr = gkr_ref[...]

        dkr = dk_ref[0, :, DN:2 * DN]
        for hd in range(1, HEADS):
            dkr = dkr + dk_ref[hd, :, DN:2 * DN]
        dkr = _rope_t(dkr, c, s)
        _, krhat, rkr = _rms(m["kr_raw"], gkr, DR)
        _acc_rows(dgkr_ref, dkr * krhat)
        dkr_raw = _rms_bwd(dkr, gkr, krhat, rkr, DR)

        for hd in range(HEADS):
            ncols = slice(hd * DN, (hd + 1) * DN)
            _, xh, r = _rms(m["qp"][:, ncols], gqn, DN)
            dqn = dq_ref[hd, :, 0:DN] * SM_SCALE
            _acc_rows(dgqn_ref, dqn * xh)
            dqp_ref[:, ncols] = _rms_bwd(dqn, gqn, xh, r, DN).astype(BF16)

            rcols = slice(D + hd * LANES, D + (hd + 1) * LANES)
            _, xh, r = _rms(m["qp"][:, rcols], gqr, DR)
            dqr = _rope_t(dq_ref[hd, :, DN:2 * DN] * SM_SCALE, c, s)
            _acc_rows(dgqr_ref, dqr * xh)
            dqp_ref[:, rcols] = _rms_bwd(dqr, gqr, xh, r, DR).astype(BF16)

            kcols = slice(hd * 2 * DN, hd * 2 * DN + DN)
            _, xh, r = _rms(m["kvp"][:, kcols], gkn, DN)
            dkn = dk_ref[hd, :, 0:DN]
            _acc_rows(dgkn_ref, dkn * xh)
            dkvp_ref[:, kcols] = _rms_bwd(dkn, gkn, xh, r, DN).astype(BF16)
            dkvp_ref[:, hd * 2 * DN + DN:(hd + 1) * 2 * DN] = dv_ref[hd].astype(BF16)

        dcq = _dot_nt(dqp_ref[...], wuq_ref[...])
        _acc_rows(dgq_ref, dcq * m["cqhat"])
        dlat_q = _rms_bwd(dcq, gq_ref[...], m["cqhat"], m["rq"], QL)
        dckv = _dot_nt(dkvp_ref[...], wukv_ref[...])
        _acc_rows(dgkv_ref, dckv * m["ckvhat"])
        dlat_kv = _rms_bwd(dckv, gkv_ref[...], m["ckvhat"], m["rkv"], KVL)
        dlat = jnp.concatenate([dlat_q, dlat_kv, dkr_raw], axis=1).astype(BF16)
        dlat_ref[...] = dlat
        dhn = _dot_nt(dlat, wdn_ref[...])
        _acc_rows(dg1_ref, dhn * m["xhat"])
        dh_ref[...] = dh1_ref[...] + _rms_bwd(dhn, g1_ref[...], m["xhat"], m["rx"], D)

    hb = lambda w: pl.BlockSpec((HEADS, TM, w), lambda i: (0, i, 0))
    return pl.pallas_call(
        body, name="mla_pre_bwd", grid=(t // TM,),
        in_specs=[hb(2 * DN), hb(2 * DN), hb(DN), _row(TM, D), _row(TM, D), _const((1, D)), _const((D, LATP)),
                  _const((1, QL)), _const((1, KVL)), _const((QL, 2 * D)), _const((KVL, 2 * D)),
                  _const((1, LANES)), _const((1, LANES)), _const((1, LANES)), _const((1, LANES)),
                  _row(TM, LANES), _row(TM, LANES)],
        out_specs=[_row(TM, D), _row(TM, D), _row(TM, QL), _row(TM, KVL), _row(TM, 2 * D), _row(TM, 2 * D),
                   _row(TM, LATP), _const((8, D)), _const((8, QL)), _const((8, KVL)), _const((8, LANES)),
                   _const((8, LANES)), _const((8, LANES)), _const((8, LANES))],
        out_shape=[_sds((t, D), F32), _sds((t, D), BF16), _sds((t, QL), BF16), _sds((t, KVL), BF16),
                   _sds((t, 2 * D), BF16), _sds((t, 2 * D), BF16), _sds((t, LATP), BF16),
                   _sds((8, D), F32), _sds((8, QL), F32), _sds((8, KVL), F32), _sds((8, LANES), F32),
                   _sds((8, LANES), F32), _sds((8, LANES), F32), _sds((8, LANES), F32)],
        compiler_params=_cp("arbitrary"),
    )(dq, dk, dv, dh1, h, g1, wdn, gq, gkv, wuq, wukv, gqn, gqr, gkn, gkr, cos, sin)


def gmlp_bwd(dh1, dh1b, h, pre, g1, allw, lay, lng, lnb, wm, wmt, bfull, tril):
    t = h.shape[0]

    def body(dh1_ref, dh1b_ref, h_ref, pre_ref, g1_ref, win_ref, lng_ref, lnb_ref, wm_ref, wmt_ref, b_ref,
             wout_ref, tril_ref, dh_ref, hn_ref, dpre_ref, dws_ref, dbs_ref, dlng_ref, dlnb_ref, dg1_ref,
             dvn_s):
        _zero_at_first_step(dws_ref, dbs_ref, dlng_ref, dlnb_ref, dg1_ref)
        g1 = g1_ref[...]
        yn, xhat, rx = _rms(h_ref[...], g1, D)
        hn_ref[...] = yn.astype(BF16)
        dy = _dot_nt(dh1b_ref[...], _rows_joined(wout_ref))
        pre_u = pre_ref[:, :GH].astype(F32)
        pre_v = pre_ref[:, GH:].astype(F32)
        u, gg_u = _gelu_and_grad(pre_u)
        v, gg_v = _gelu_and_grad(pre_v)
        xc = v - jnp.mean(v, axis=-1, keepdims=True)
        rs = lax.rsqrt(jnp.mean(xc * xc, axis=-1, keepdims=True) + EPS)
        vhat = xc * rs
        lng = lng_ref[...]
        vnb = (vhat * lng + lnb_ref[...]).astype(BF16)
        dsv = dy * u
        dsvb = dsv.astype(BF16)
        tril_m = tril_ref[...]
        for ch in range(TM // GC):
            rows = slice(ch * GC, (ch + 1) * GC)
            dbs_ref[...] += dsv[rows, :]
            for g in range(GG):
                cols = slice(g * GD, (g + 1) * GD)
                sv = _dot(wm_ref[g], vnb[rows, cols]) + b_ref[:, cols]
                dpre_ref[rows, cols] = (dy[rows, cols] * sv * gg_u[rows, cols]).astype(BF16)
                dvn_s[rows, cols] = _dot(wmt_ref[g], dsvb[rows, cols])
                dws_ref[g] += _dot_nt(dsvb[rows, cols], vnb[rows, cols]) * tril_m
        dvn = dvn_s[...]
        _acc_rows(dlng_ref, dvn * vhat)
        _acc_rows(dlnb_ref, dvn)
        dvhat = dvn * lng
        dv = rs * (dvhat - jnp.mean(dvhat, axis=-1, keepdims=True)
                   - vhat * jnp.mean(dvhat * vhat, axis=-1, keepdims=True))
        dpre_v = (dv * gg_v).astype(BF16)
        dpre_ref[:, GH:] = dpre_v
        dhn = _dot_nt(dpre_ref[:, 0:D], win_ref[0])
        for c in range(1, N_CHIPS):
            dhn = dhn + _dot_nt(dpre_ref[:, c * D:(c + 1) * D], win_ref[c])
        _acc_rows(dg1_ref, dhn * xhat)
        dh_ref[...] = dh1_ref[...] + _rms_bwd(dhn, g1, xhat, rx, D)

    return pl.pallas_call(
        body, name="gmlp_bwd", grid=(t // TM,),
        in_specs=[_row(TM, D), _row(TM, D), _row(TM, D), _row(TM, 2 * GH), _const((1, D)), _wblk(D, lay["in"]),
                  _const((1, GH)), _const((1, GH)), _const((GG, GC, GC)), _const((GG, GC, GC)), _const((GC, GH)),
                  _wblk(GH // N_CHIPS, lay["out"]), _const((GC, GC))],
        out_specs=[_row(TM, D), _row(TM, D), _row(TM, 2 * GH), _const((GG, GC, GC)), _const((GC, GH)),
                   _const((8, GH)), _const((8, GH)), _const((8, D))],
        out_shape=[_sds((t, D), F32), _sds((t, D), BF16), _sds((t, 2 * GH), BF16), _sds((GG, GC, GC), F32),
                   _sds((GC, GH), F32), _sds((8, GH), F32), _sds((8, GH), F32), _sds((8, D), F32)],
        scratch_shapes=[pltpu.VMEM((TM, GH), F32)],
        compiler_params=_cp("arbitrary"),
    )(dh1, dh1b, h, pre, g1, allw, lng, lnb, wm, wmt, bfull, allw, tril)


def _token_step(t):
    return next(s for s in (2048, 1024, 512) if t % s == 0)


def mm_tn(a, b, layer=None):
    t, k = a.shape[-2:]
    n = b.shape[1]
    tk = min(k, 1024)
    tn = min(n, 1024)
    tt = _token_step(t)
    a_spec = (pl.BlockSpec((tt, tk), lambda i, j, s: (s, i)) if layer is None else
              pl.BlockSpec((None, tt, tk), lambda i, j, s: (layer, s, i)))

    def body(a_ref, b_ref, o_ref):
        @pl.when(pl.program_id(2) == 0)
        def _():
            o_ref[...] = jnp.zeros_like(o_ref)

        o_ref[...] += _dot_tn(a_ref[...].astype(BF16), b_ref[...].astype(BF16))

    return pl.pallas_call(
        body, name="mm_tn", grid=(k // tk, n // tn, t // tt),
        in_specs=[a_spec, pl.BlockSpec((tt, tn), lambda i, j, s: (s, j))],
        out_specs=pl.BlockSpec((tk, tn), lambda i, j, s: (i, j)), out_shape=_sds((k, n), F32),
        compiler_params=_cp("parallel", "parallel", "arbitrary"),
    )(a, b)


def mm_tn_into(buf, a, b, rows, row0, col_sharded):
    t = a.shape[0]
    tt = _token_step(t)
    assert row0 % rows == 0 and a.shape[1] == (rows if col_sharded else N_CHIPS * rows), (rows, row0, a.shape)
    assert b.shape[1] == (N_CHIPS * D if col_sharded else D), b.shape
    joint = not col_sharded and N_CHIPS * rows <= 2048
    grid = (1, 1, t // tt) if joint else (1, N_CHIPS, t // tt) if col_sharded else (N_CHIPS, 1, t // tt)
    fresh = isinstance(buf, int)

    def body(*refs):
        a_ref, b_ref, o_ref = refs[-3:]

        @pl.when(pl.program_id(2) == 0)
        def _():
            o_ref[...] = jnp.zeros_like(o_ref)

        o_ref[...] += _dot_tn(a_ref[...].astype(BF16), b_ref[...].astype(BF16)).reshape(o_ref.shape)

    specs = [pl.BlockSpec((tt, N_CHIPS * rows if joint else rows), lambda i, j, s: (s, i)),
             pl.BlockSpec((tt, D), lambda i, j, s: (s, j))]
    return pl.pallas_call(
        body, name="mm_tn_into", grid=grid,
        in_specs=specs if fresh else [_ANY] + specs,
        out_specs=pl.BlockSpec((N_CHIPS if joint else None, rows, D), lambda i, j, s: (i + j, row0 // rows, 0)),
        out_shape=_sds((N_CHIPS, buf, D) if fresh else buf.shape, F32),
        input_output_aliases={} if fresh else {0: 0},
        compiler_params=_cp("parallel", "parallel", "arbitrary"),
    )(*((a, b) if fresh else (buf, a, b)))


def adamw(w, g, m, v):
    rows, cols = w.shape
    tr = rows if rows <= 512 else next(r for r in (512, 384, 256, 128) if rows % r == 0)
    c1 = 1.0 - ADAM_B1 ** ADAM_STEP
    c2 = 1.0 - ADAM_B2 ** ADAM_STEP

    def body(w_ref, g_ref, m_ref, v_ref, d_ref, mo_ref, vo_ref):
        gv = g_ref[...]
        mn = ADAM_B1 * m_ref[...] + (1.0 - ADAM_B1) * gv
        vn = ADAM_B2 * v_ref[...] + (1.0 - ADAM_B2) * (gv * gv)
        mo_ref[...] = mn
        vo_ref[...] = vn
        d_ref[...] = -ADAM_LR * ((mn / c1) / (jnp.sqrt(vn / c2) + ADAM_EPS) + ADAM_WD * w_ref[...])

    spec = pl.BlockSpec((tr, cols), lambda i: (i, 0))
    return pl.pallas_call(
        body, name="adamw", grid=(rows // tr,),
        in_specs=[spec] * 4, out_specs=[spec] * 3, out_shape=[_sds((rows, cols), F32)] * 3,
        compiler_params=_cp("parallel"),
    )(w, g, m, v)


def adamw_layers(w, m, v, bufs, row0s):
    nl, a, _ = w.shape
    tr = min(a, 256)
    c1 = 1.0 - ADAM_B1 ** ADAM_STEP
    c2 = 1.0 - ADAM_B2 ** ADAM_STEP
    assert all(r % tr == 0 for r in row0s) and a % tr == 0, (row0s, a)

    def body(w_ref, m_ref, v_ref, *rest):
        g_refs, (g_ref, d_ref, mo_ref, vo_ref) = rest[:nl], rest[nl:]
        for l in range(nl):
            @pl.when(pl.program_id(0) == l)
            def _(l=l):
                gv = g_refs[l][...]
                g_ref[...] = gv
                mn = ADAM_B1 * m_ref[...] + (1.0 - ADAM_B1) * gv
                vn = ADAM_B2 * v_ref[...] + (1.0 - ADAM_B2) * (gv * gv)
                mo_ref[...] = mn
                vo_ref[...] = vn
                d_ref[...] = -ADAM_LR * ((mn / c1) / (jnp.sqrt(vn / c2) + ADAM_EPS) + ADAM_WD * w_ref[...])

    def rows_of(l, row0):
        return pl.BlockSpec((tr, D), lambda li, i: (jnp.where(li == l, row0 // tr + i, row0 // tr), 0))

    spec = pl.BlockSpec((None, tr, D), lambda li, i: (li, i, 0))
    return pl.pallas_call(
        body, name="adamw_layers", grid=(nl, a // tr),
        in_specs=[spec] * 3 + [rows_of(l, r) for l, r in enumerate(row0s)],
        out_specs=[spec] * 4, out_shape=[_sds(w.shape, F32)] * 4,
        compiler_params=_cp("arbitrary", "arbitrary"),
    )(w, m, v, *bufs)


def _place():
    return lax.axis_index("x"), lax.axis_index("y"), lax.axis_index("c")


def _other_chips(x, y):
    return [(1 - x, y), (x, 1 - y), (1 - x, 1 - y)]


_ANY = pl.BlockSpec(memory_space=pl.ANY)


_HBM = pl.BlockSpec(memory_space=pltpu.HBM)
_SEM = pl.BlockSpec(memory_space=pltpu.SEMAPHORE)
_EFFECT = pltpu.SideEffectType.DATAFLOW_SIDE_EFFECTING
N_ICI = 3


def _exchange_start(name, src, land, copies, n):
    def body(src_ref, land_ref, *outs):
        sems, token = outs[:2 * n], outs[-1]
        for j, (s, d, to) in enumerate(copies(src_ref, land_ref, _place())):
            pltpu.make_async_remote_copy(src_ref=s, dst_ref=d, send_sem=sems[j], recv_sem=sems[n + j],
                                         device_id=to, device_id_type=MESH).start()
        token[...] = jnp.zeros_like(token)

    sem = pltpu.SemaphoreType.DMA(())
    outs = pl.pallas_call(
        body, name=name,
        out_shape=(sem,) * (2 * n) + (pltpu.HBM(src.shape, src.dtype), pltpu.HBM(land.shape, land.dtype),
                                      _sds((8, LANES), F32)),
        in_specs=(_HBM, _HBM),
        out_specs=(_SEM,) * (2 * n) + (_HBM, _HBM, pl.BlockSpec(memory_space=pltpu.VMEM)),
        input_output_aliases={0: 2 * n, 1: 2 * n + 1},
        compiler_params=pltpu.CompilerParams(has_side_effects=_EFFECT),
    )(pltpu.with_memory_space_constraint(src, pltpu.HBM), pltpu.with_memory_space_constraint(land, pltpu.HBM))
    return outs[:2 * n], outs[2 * n], outs[2 * n + 1], outs[-1]


def _exchange_wait(name, sems, src, land, after, arrivals):
    n = len(sems) // 2

    def body(src_ref, land_ref, *rest):
        sems = rest[:2 * n]
        for j, (s, d) in enumerate(arrivals(src_ref, land_ref, _place())):
            cp = pltpu.make_async_remote_copy(src_ref=s, dst_ref=d, send_sem=sems[j], recv_sem=sems[n + j],
                                              device_id=_place(), device_id_type=MESH)
            cp.wait_send()
            cp.wait_recv()

    return pl.pallas_call(
        body, name=name, out_shape=(pltpu.HBM(src.shape, src.dtype), pltpu.HBM(land.shape, land.dtype)),
        in_specs=(_HBM, _HBM) + (_SEM,) * (2 * n) + (_ANY,), out_specs=(_HBM, _HBM),
        input_output_aliases={0: 0, 1: 1},
        compiler_params=pltpu.CompilerParams(has_side_effects=_EFFECT),
    )(src, land, *sems, after)


def _halves(c, hh):
    return pl.ds(pl.multiple_of(c * hh, 16), hh), pl.ds(pl.multiple_of((1 - c) * hh, 16), hh)


def gather_start(land, tag):
    _, rr, _ = land.shape
    assert rr % 32 == 0, rr

    def copies(_, land_ref, place):
        x, y, c = place
        mine = land_ref.at[2 * x + y, _halves(c, rr // 2)[0]]
        return [(mine, mine, (cx, cy, c)) for cx, cy in _other_chips(x, y)]

    return _exchange_start(f"gather_start_{tag}", jnp.zeros((8, LANES), F32), land, copies, N_ICI)


def gather_wait(sems, src, land, after, tag):
    def arrivals(_, land_ref, place):
        x, y, c = place
        half = _halves(c, land.shape[1] // 2)[0]
        return [(land_ref.at[2 * x + y, half], land_ref.at[2 * cx + cy, half]) for cx, cy in _other_chips(x, y)]

    return _exchange_wait(f"gather_wait_{tag}", sems, src, land, after, arrivals)


def pass_start(land, tag):
    def copies(_, land_ref, place):
        x, y, c = place
        half = _halves(c, land.shape[1] // 2)[0]
        return [(land_ref.at[2 * cx + cy, half], land_ref.at[2 * cx + cy, half], (x, y, 1 - c))
                for cx, cy in _other_chips(x, y)]

    return _exchange_start(f"pass_start_{tag}", jnp.zeros((8, LANES), F32), land, copies, N_ICI)


def pass_wait(sems, src, land, after, tag):
    def arrivals(_, land_ref, place):
        x, y, c = place
        mine, other = _halves(c, land.shape[1] // 2)
        return [(land_ref.at[2 * cx + cy, mine], land_ref.at[2 * cx + cy, other]) for cx, cy in _other_chips(x, y)]

    return _exchange_wait(f"pass_wait_{tag}", sems, src, land, after, arrivals)


def swap_start(g, tag):
    _, rr, cc = g.shape

    def copies(g_ref, got_ref, place):
        x, y, c = place
        other = _halves(c, rr // 2)[1]
        return [(g_ref.at[k, other], got_ref.at[k], (x, y, 1 - c)) for k in range(N_CHIPS)]

    return _exchange_start(f"swap_start_{tag}", g, lax.empty((N_CHIPS, rr // 2, cc), g.dtype), copies, N_CHIPS)


def swap_wait(sems, g, got, after, tag):
    def arrivals(g_ref, got_ref, place):
        other = _halves(place[2], g.shape[1] // 2)[1]
        return [(g_ref.at[k, other], got_ref.at[k]) for k in range(N_CHIPS)]

    return _exchange_wait(f"swap_wait_{tag}", sems, g, got, after, arrivals)


def chip_sum(place, g32, got):
    _, rr, cc = g32.shape
    hh = rr // 2
    tr = SUM_ROWS
    assert rr % 2 == 0 and hh % tr == 0, (rr, tr)
    nb = hh // tr

    def body(place_ref, g_ref, got_ref, own_ref, all_ref):
        s = g_ref[...] + got_ref[...].astype(F32)
        all_ref[...] = s.astype(BF16)
        own_ref[...] = g_ref[place_ref[1]] + got_ref[place_ref[1]].astype(F32)

    return pl.pallas_call(
        body, name="chip_sum",
        grid_spec=pltpu.PrefetchScalarGridSpec(
            num_scalar_prefetch=1, grid=(nb,),
            in_specs=[pl.BlockSpec((N_CHIPS, tr, cc), lambda i, pr: (0, pr[0] * nb + i, 0)),
                      pl.BlockSpec((N_CHIPS, tr, cc), lambda i, pr: (0, i, 0))],
            out_specs=[pl.BlockSpec((tr, cc), lambda i, pr: (i, 0)),
                       pl.BlockSpec((N_CHIPS, tr, cc), lambda i, pr: (0, i, 0))]),
        out_shape=[_sds((hh, cc), F32), _sds((N_CHIPS, hh, cc), BF16)],
        compiler_params=_cp("parallel"),
    )(place, g32, got)


def _scatter_copies(s_ref, land_ref, place):
    x, y, c = place
    return [(s_ref.at[2 * cx + cy], land_ref.at[j], (cx, cy, c)) for j, (cx, cy) in enumerate(_other_chips(x, y))]


def scatter_start(s, tag):
    return _exchange_start(f"scatter_start_{tag}", s, lax.empty((N_ICI,) + s.shape[1:], s.dtype), _scatter_copies, N_ICI)


def scatter_wait(sems, s, land, after, tag):
    return _exchange_wait(f"scatter_wait_{tag}", sems, s, land, after,
                          lambda s_ref, land_ref, place: [(a, b) for a, b, _ in _scatter_copies(s_ref, land_ref, place)])


def final_sum(place, own, got):
    hh, cc = own.shape
    tr = SUM_ROWS
    assert hh % tr == 0, (hh, tr)
    nb = hh // tr

    def body(place_ref, own_ref, got_ref, o_ref):
        del place_ref
        o_ref[...] = ((own_ref[...] + got_ref[0].astype(F32)) + got_ref[1].astype(F32)) + got_ref[2].astype(F32)

    return pl.pallas_call(
        body, name="final_sum",
        grid_spec=pltpu.PrefetchScalarGridSpec(
            num_scalar_prefetch=1, grid=(nb,),
            in_specs=[pl.BlockSpec((tr, cc), lambda i, pr: (i, 0)), pl.BlockSpec((3, tr, cc), lambda i, pr: (0, i, 0))],
            out_specs=pl.BlockSpec((tr, cc), lambda i, pr: (pr[0] * nb + i, 0))),
        out_shape=_sds((2 * hh, cc), F32),
        compiler_params=_cp("parallel"),
    )(place, own, got)


def share_start(f, tag):
    def copies(_, f_ref, place):
        x, y, c = place
        mine = f_ref.at[_halves(c, f.shape[0] // 2)[0]]
        return [(mine, mine, (x, y, 1 - c))]

    return _exchange_start(f"share_start_{tag}", jnp.zeros((8, LANES), F32), f, copies, 1)


def share_wait(sems, src, f, after, tag):
    def arrivals(_, f_ref, place):
        mine, other = _halves(place[2], f.shape[0] // 2)
        return [(f_ref.at[mine], f_ref.at[other])]

    return _exchange_wait(f"share_wait_{tag}", sems, src, f, after, arrivals)


N_DEV = 8


def _peers(place):
    x, y, c = place
    return [((1 - x) if r & 4 else x, (1 - y) if r & 2 else y, (1 - c) if r & 1 else c) for r in range(1, N_DEV)]


def _device_index(place):
    x, y, c = place
    return 4 * x + 2 * y + c


def small_start(land, tag):
    def copies(_, land_ref, place):
        mine = land_ref.at[_device_index(place)]
        return [(mine, mine, to) for to in _peers(place)]

    return _exchange_start(f"small_start_{tag}", jnp.zeros((8, LANES), F32), land, copies, N_DEV - 1)


def small_wait(sems, src, land, after, tag):
    def arrivals(_, land_ref, place):
        return [(land_ref.at[_device_index(place)], land_ref.at[_device_index(peer)]) for peer in _peers(place)]

    return _exchange_wait(f"small_wait_{tag}", sems, src, land, after, arrivals)


def sum_devices(land):
    _, rr, cc = land.shape
    tr = 56
    assert rr % tr == 0, rr

    def body(l_ref, o_ref):
        acc = l_ref[0]
        for d in range(1, N_DEV):
            acc = acc + l_ref[d]
        o_ref[...] = acc

    return pl.pallas_call(
        body, name="sum_devices", grid=(rr // tr,),
        in_specs=[pl.BlockSpec((N_DEV, tr, cc), lambda i: (0, i, 0))],
        out_specs=pl.BlockSpec((tr, cc), lambda i: (i, 0)), out_shape=_sds((rr, cc), F32),
        compiler_params=_cp("parallel"),
    )(land)


_BIG = ["mla_w_down", "mla_w_uq", "mla_w_ukv", "mla_w_out", "gmlp_w_in", "gmlp_w_out", "ffn_w_up", "ffn_w_down",
        "ple_w_gate", "ple_w_proj"]
_SMALL_REST = ["norm_mix", "norm_ffn", "norm_ple", "mla_q_lora_g", "mla_kv_lora_g", "mla_q_nope_g", "mla_q_rope_g",
               "mla_k_nope_g", "mla_k_rope_g"]
_SMALL_GMLP = ["gmlp_ln_g", "gmlp_ln_b", "gmlp_w_s", "gmlp_b_s"]
_SMALL = _SMALL_REST + _SMALL_GMLP

_LAY_MLA = dict(up=0, down=1024, out=2048, gate=2304, wdn=2560, wuq=2736, wukv=2880, proj=3008, rows=3072)
_LAY_MLA_MAIN = dict(up=0, down=1024, out=2048, gate=2304, rows=2560)
_LAY_MLA_ODD = dict(wdn=0, wuq=176, wukv=320, proj=448, rows=512)
_LAY_GMLP = {"up": 0, "down": 1024, "in": 2048, "out": 3072, "gate": 3584, "proj": 3840, "ln": 3904, "rows": 4096}
SPLIT_LAYERS = (0,)


def _layer_units(i):
    j = i // 2
    if i % 2 == 0:
        odd, lay = (_LAY_MLA_ODD, _LAY_MLA_MAIN) if i in SPLIT_LAYERS else (_LAY_MLA, _LAY_MLA)
        small = [("mla_w_down", j, odd["wdn"]), ("mla_w_uq", j, odd["wuq"]), ("mla_w_ukv", j, odd["wukv"]),
                 ("ple_w_proj", i, odd["proj"])]
        large = [("ffn_w_up", i, lay["up"]), ("ffn_w_down", i, lay["down"]), ("mla_w_out", j, lay["out"]),
                 ("ple_w_gate", i, lay["gate"])]
        return [("odd", odd, small), ("main", lay, large)] if i in SPLIT_LAYERS else [("main", lay, large + small)]
    lay = _LAY_GMLP
    return [("main", lay, [("ffn_w_up", i, lay["up"]), ("ffn_w_down", i, lay["down"]), ("gmlp_w_in", j, lay["in"]),
                           ("gmlp_w_out", j, lay["out"]), ("ple_w_gate", i, lay["gate"]),
                           ("ple_w_proj", i, lay["proj"])])]


def _pack_rows(parts, dtype, pad_to=None, slot=False):
    size = sum(p.size for p in parts)
    tail = [] if pad_to is None or pad_to * D == size else [jnp.zeros((pad_to * D - size,), dtype)]
    shape = (1, -1, D) if slot else (-1, D)
    if all(p.size % D == 0 for p in parts + tail):
        return jnp.concatenate([p.astype(dtype).reshape(shape) for p in parts + tail], axis=len(shape) - 2)
    return jnp.concatenate([p.astype(dtype).reshape(-1) for p in parts + tail]).reshape(shape)


def _odd(allw, row0, a, b):
    return allw[:, row0:row0 + a * b // D].reshape(N_CHIPS, a, b)


def _cols_joined(s):
    return jnp.transpose(s, (1, 0, 2)).reshape(s.shape[1], N_CHIPS * s.shape[2])


def _col_shards(full):
    a, bb = full.shape
    return jnp.transpose(full.reshape(a, N_CHIPS, bb // N_CHIPS), (1, 0, 2)).reshape(N_CHIPS, -1, D)


def _pad_lanes(g):
    return jnp.pad(g, ((0, 0), (0, LANES - g.shape[1])))


def _split_uq(wuq):
    l = wuq.shape[0]
    w = wuq.reshape(l, QL, HEADS, DN + DR)
    nope = w[..., :DN].reshape(l, QL, HEADS * DN)
    rope = jnp.pad(w[..., DN:], ((0, 0), (0, 0), (0, 0), (0, LANES - DR))).reshape(l, QL, HEADS * LANES)
    return jnp.concatenate([nope, rope], axis=-1)


def _merge_uq(d):
    nope = d[:, :HEADS * DN].reshape(QL, HEADS, DN)
    rope = d[:, HEADS * DN:].reshape(QL, HEADS, LANES)[..., :DR]
    return jnp.concatenate([nope, rope], axis=-1).reshape(QL, HEADS * (DN + DR))


def _rope_tables(positions):
    inv_freq = ROPE_BASE ** (-(jnp.arange(0, DR, 2, dtype=F32) / DR))
    ang = positions.reshape(-1).astype(F32)[:, None] * inv_freq
    z = jnp.zeros((ang.shape[0], LANES - DR), F32)
    return (jnp.concatenate([jnp.cos(ang), jnp.cos(ang), z], axis=1),
            jnp.concatenate([jnp.sin(ang), jnp.sin(ang), z], axis=1))


def kernel(x, p, positions, norm_mix, norm_ffn, norm_ple, mla_w_down, mla_q_lora_g, mla_kv_lora_g, mla_w_uq, mla_w_ukv, mla_q_nope_g, mla_q_rope_g, mla_k_nope_g, mla_k_rope_g, mla_w_out, gmlp_w_in, gmlp_ln_g, gmlp_ln_b, gmlp_w_s, gmlp_b_s, gmlp_w_out, ffn_w_up, ffn_w_down, ple_w_gate, ple_w_proj, loss_target, m_norm_mix, m_norm_ffn, m_norm_ple, m_mla_w_down, m_mla_q_lora_g, m_mla_kv_lora_g, m_mla_w_uq, m_mla_w_ukv, m_mla_q_nope_g, m_mla_q_rope_g, m_mla_k_nope_g, m_mla_k_rope_g, m_mla_w_out, m_gmlp_w_in, m_gmlp_ln_g, m_gmlp_ln_b, m_gmlp_w_s, m_gmlp_b_s, m_gmlp_w_out, m_ffn_w_up, m_ffn_w_down, m_ple_w_gate, m_ple_w_proj, v_norm_mix, v_norm_ffn, v_norm_ple, v_mla_w_down, v_mla_q_lora_g, v_mla_kv_lora_g, v_mla_w_uq, v_mla_w_ukv, v_mla_q_nope_g, v_mla_q_rope_g, v_mla_k_nope_g, v_mla_k_rope_g, v_mla_w_out, v_gmlp_w_in, v_gmlp_ln_g, v_gmlp_ln_b, v_gmlp_w_s, v_gmlp_b_s, v_gmlp_w_out, v_ffn_w_up, v_ffn_w_down, v_ple_w_gate, v_ple_w_proj):
    args = dict(locals())
    weights = {n: args[n] for n in _BIG + _SMALL}
    depth = norm_mix.shape[0]
    nb, seq, _ = x.shape
    t = nb * seq
    assert seq % TQ == 0 and seq % TM == 0 and t % 512 == 0, (nb, seq)
    cx = lax.axis_index("x")
    cy = lax.axis_index("y")
    cc = lax.axis_index("c")
    chip = 2 * cx + cy

    gathers = {}
    token = None
    for i in range(depth):
        for key, lay, parts in _layer_units(i):
            rows = [weights[n][l] for n, l, _ in parts]
            if token is not None:
                rows[0] = rows[0] + token[0, 0]
            if "ln" in lay:
                ln = jnp.stack([gmlp_ln_g[i // 2], gmlp_ln_b[i // 2]]).astype(F32)
                bits = lax.bitcast_convert_type(ln, BF16).reshape(-1)
                rows.append(jnp.pad(bits, (0, 16 * D - bits.size)).reshape(16, D))
            mine = _pack_rows(rows, BF16, pad_to=lay["rows"], slot=True)
            land = lax.dynamic_update_slice(lax.empty((N_CHIPS, lay["rows"], D), BF16), mine, (chip, 0, 0))
            sems, src, land, token = gather_start(land, f"{i}{key}")
            gathers[i, key] = (sems, src, land)
    allw = [None] * depth

    tril = jnp.tril(jnp.ones((GC, GC), F32))
    wm = (gmlp_w_s * tril).astype(BF16)
    wmt = jnp.swapaxes(wm, -1, -2)
    bfull = jnp.repeat(jnp.swapaxes(gmlp_b_s, -1, -2), GD, axis=-1)
    cos, sin = _rope_tables(positions)
    row = lambda g: g.reshape(1, -1)
    gqr = _pad_lanes(mla_q_rope_g)
    gkr = _pad_lanes(mla_k_rope_g)

    h = x.reshape(t, D)
    pt = p.reshape(depth, t, PLE)
    saved = []

    passing = {}

    def arrive(i, key, after):
        sems, src, land = gathers[i, key]
        _, land = gather_wait(sems, src, land, after, f"{i}{key}")
        passing[i, key] = pass_start(land, f"{i}{key}")
        return passing[i, key][3]

    def needed(i, key, after=None):
        sems, src, land, tok = passing.pop((i, key))
        return pass_wait(sems, src, land, tok if after is None else after, f"{i}{key}")[1]

    arrive(0, _layer_units(0)[0][0], token)
    for i in range(depth):
        j = i // 2
        lay = _layer_units(i)[-1][1]
        s = dict(h=h)
        if i % 2 == 0:
            split = i in SPLIT_LAYERS
            olay = _layer_units(i)[0][1]
            odd = needed(i, "odd" if split else "main", None if i == 0 else h)
            wdn = jnp.pad(_odd(odd, olay["wdn"], D // N_CHIPS, LAT).reshape(D, LAT), ((0, 0), (0, LATP - LAT)))
            wuq = _split_uq(_cols_joined(_odd(odd, olay["wuq"], QL, 384))[None])[0]
            wukv = _cols_joined(_odd(odd, olay["wukv"], KVL, 512))
            wp = _cols_joined(_odd(odd, olay["proj"], PLE, 256))
            mla_args = (row(norm_mix[i]), wdn, row(mla_q_lora_g[j]), row(mla_kv_lora_g[j]), wuq, wukv,
                        row(mla_q_nope_g[j]), gqr[j:j + 1], row(mla_k_nope_g[j]), gkr[j:j + 1], cos, sin)
            q, k, v = mla_pre_fwd(h, *mla_args)
            y, lse = flash_fwd(q, k, v, seq)
            if split and i == 0:
                arrive(i, "main", y)
            aw = needed(i, "main", y) if split else odd
            s.update(q=q, k=k, v=v, lse=lse, mla_args=mla_args)
        else:
            aw = needed(i, "main", h)
            ln = lax.bitcast_convert_type(aw[:, lay["ln"]:lay["ln"] + 2].reshape(N_CHIPS, 2, GH // N_CHIPS, 2), F32)
            ln = jnp.transpose(ln, (1, 0, 2)).reshape(2, 1, GH)
            wp = _cols_joined(_odd(aw, lay["proj"], PLE, 256))
            y, pre = gmlp_fwd(h, row(norm_mix[i]), aw, lay, ln[0], ln[1], wm[j], bfull[j])
            s.update(pre=pre, ln=ln)
        allw[i] = aw
        g2 = row(norm_ffn[i])
        if i + 1 < depth:
            for key, _, _ in _layer_units(i + 1):
                g2 = g2 + arrive(i + 1, key, y)[0:1, 0:1]
        h1, h2, hn2, r = mixffn_fwd(h, y, aw, lay, g2)
        h, hn3 = ple_fwd(h2, pt, i, row(norm_ple[i]), aw, lay, wp)
        s.update(y=y, wp=wp, h1=h1, h2=h2, hn2=hn2, r=r, hn3=hn3)
        saved.append(s)

    dh, loss_part = loss_head(h, loss_target.reshape(t, D))

    gs = {n: [None] * weights[n].shape[0] for n in _SMALL}
    gw = {n: [None] * weights[n].shape[0] for n in _BIG}
    place = jnp.stack([cc, chip]).astype(jnp.int32)
    scatters = []
    swaps = []
    token = None

    def put(b, row0, shards):
        return lax.dynamic_update_slice(b, shards.reshape(N_CHIPS, -1, D), (0, row0, 0))

    def small_size(n):
        return weights[n].shape[0] * GH if n in ("gmlp_ln_g", "gmlp_ln_b") else weights[n].size

    def small_exchange(names, zero, tag, extra=()):
        rows = -(-(sum(small_size(n) for n in names) + sum(e.size for e in extra)) // (56 * D)) * 56
        part = [jnp.stack(gs[n]) for n in names] + list(extra)
        part = _pack_rows([part[0] + zero] + part[1:], F32, pad_to=rows, slot=True)
        land = lax.dynamic_update_slice(lax.empty((N_DEV, rows, D), F32), part, (2 * chip + cc, 0, 0))
        return small_start(land, tag)

    def swap(i, key, buf):
        sems, buf, got, tok = swap_start(buf, f"{i}{key}")
        swaps.append((i, key, sems, buf, got))
        return tok

    def swapped(after, zero):
        while swaps:
            i, key, sems, g, got = swaps.pop(0)
            g, got = swap_wait(sems, g, got, after, f"{i}{key}")
            own, sums = chip_sum(place, g, got)
            sems, sums, land, tok = scatter_start(sums, f"{i}{key}")
            scatters.append((i, key, own, sems, sums, land))
            zero = zero + tok[0:1, 0:1]
        return zero

    for i in reversed(range(depth)):
        j = i // 2
        lay = _layer_units(i)[-1][1]
        aw = allw[i]
        s = saved[i]

        g3 = row(norm_ple[i])
        if token is not None:
            g3 = g3 + token[0:1, 0:1]
        dh2, dh2b, dgt, dpp, dg3 = ple_bwd(dh, s["h2"], pt, i, g3, aw, lay, s["wp"])
        gs["norm_ple"][i] = dg3[0]
        buf = mm_tn_into(lay["rows"], s["hn3"], dgt, D // N_CHIPS, lay["gate"], False)
        dproj = _col_shards(mm_tn(pt, dpp, layer=i))
        if "ln" in lay:
            buf = put(buf, lay["ln"], jnp.zeros((N_CHIPS, lay["rows"] - lay["ln"], D), F32))
            buf = put(buf, lay["proj"], dproj)
        dh1, dh1b, du, a, dg2, *do = ffn_bwd(dh2, dh2b, s["h1"], s["r"], row(norm_ffn[i]), aw, lay,
                                             mixer_rows=D // N_CHIPS if i % 2 == 0 else None)
        gs["norm_ffn"][i] = dg2[0]
        buf = mm_tn_into(buf, a, dh2b, D, lay["down"], False)
        buf = mm_tn_into(buf, s["hn2"], du, D, lay["up"], True)
        buf = mm_tn_into(buf, s["y"], dh1b, s["y"].shape[1] // N_CHIPS, lay["out"], False)
        g1 = swapped(dh1, row(norm_mix[i]))
        if i % 2 == 0:
            split = i in SPLIT_LAYERS
            dq, dk, dv = flash_bwd(s["q"], s["k"], s["v"], s["y"], do[0], s["lse"], seq,
                                   after=swap(i, "main", buf) if split else dh1b)
            g1 = swapped(dq, g1)
            (dh, hn1, cq, ckv, dqp, dkvp, dlat, dg1, dgq, dgkv, dgqn, dgqr, dgkn, dgkr) = mla_pre_bwd(
                dq, dk, dv, dh1, s["h"], g1, *s["mla_args"][1:])
            gs["norm_mix"][i] = dg1[0]
            gs["mla_q_lora_g"][j] = dgq[0]
            gs["mla_kv_lora_g"][j] = dgkv[0]
            gs["mla_q_nope_g"][j] = dgqn[0]
            gs["mla_q_rope_g"][j] = dgqr[0, :DR]
            gs["mla_k_nope_g"][j] = dgkn[0]
            gs["mla_k_rope_g"][j] = dgkr[0, :DR]
            small = [mm_tn(hn1, dlat)[:, :LAT].reshape(N_CHIPS, -1, D), _col_shards(_merge_uq(mm_tn(cq, dqp))),
                     _col_shards(mm_tn(ckv, dkvp)), dproj]
            if split:
                buf = jnp.concatenate(small, axis=1)
            else:
                buf = put(buf, lay["wdn"], jnp.concatenate(small, axis=1))
            key = "odd" if split else "main"
        else:
            dh, hn1, dpre, dws, dbs, dlng, dlnb, dg1 = gmlp_bwd(
                dh1, dh1b, s["h"], s["pre"], g1, aw, lay, s["ln"][0], s["ln"][1], wm[j], wmt[j], bfull[j], tril)
            gs["norm_mix"][i] = dg1[0]
            gs["gmlp_ln_g"][j] = dlng[0]
            gs["gmlp_ln_b"][j] = dlnb[0]
            gs["gmlp_w_s"][j] = dws
            gs["gmlp_b_s"][j] = jnp.sum(dbs.reshape(GC, GG, GD), axis=-1).T
            buf = mm_tn_into(buf, hn1, dpre, D, lay["in"], True)
            key = "main"
        token = swap(i, key, buf)
        if i == 1:
            small_gmlp = small_exchange(_SMALL_GMLP, token[0, 0], "gmlp")
            token = token + small_gmlp[3]
    last = swapped(dh, jnp.zeros((1, 1), F32))
    grad_x = dh.reshape(x.shape)
    small_rest = small_exchange(_SMALL_REST, last[0, 0], "rest", extra=[loss_part[0, 0:1]])

    after = small_rest[3]
    shares = []
    for i, key, own, sems, sums, land in scatters:
        _, got = scatter_wait(sems, sums, land, after, f"{i}{key}")
        sems, src, full, after = share_start(final_sum(place, own, got), f"{i}{key}")
        shares.append((i, key, sems, src, full))
    where = {n: [None] * weights[n].shape[0] for n in _BIG}
    for i, key, sems, src, full in shares:
        _, after = share_wait(sems, src, full, after, f"{i}{key}")
        for n, l, row0 in dict((k, parts) for k, _, parts in _layer_units(i))[key]:
            where[n][l] = (after, row0)
            if weights[n].shape[-1] != D:
                gw[n][l] = after[row0:row0 + weights[n][l].size // D].reshape(weights[n].shape[1:])
    grads = {n: jnp.stack(gw[n]) for n in _BIG if weights[n].shape[-1] != D}

    tot = []
    for names, (sems, src, land, _), tag in ((_SMALL_REST, small_rest, "rest"), (_SMALL_GMLP, small_gmlp, "gmlp")):
        summed = sum_devices(small_wait(sems, src, land, after, tag)[1]).reshape(-1)
        tot.append(summed[:sum(small_size(n) for n in names)])
        if tag == "rest":
            loss = summed[tot[-1].size]
    tot = jnp.concatenate(tot)
    off = 0
    for n, sz in ((n, small_size(n)) for n in _SMALL_REST + _SMALL_GMLP):
        gsum = tot[off:off + sz]
        off += sz
        if n in ("gmlp_ln_g", "gmlp_ln_b"):
            gsum = lax.dynamic_slice_in_dim(gsum.reshape(-1, GH), chip * (GH // N_CHIPS), GH // N_CHIPS, axis=1)
        grads[n] = gsum.reshape(weights[n].shape)

    delta, new_m, new_v = {}, {}, {}
    for n in _BIG:
        if weights[n].shape[-1] == D:
            grads[n], delta[n], new_m[n], new_v[n] = adamw_layers(
                weights[n], args["m_" + n], args["v_" + n], [b for b, _ in where[n]], [r for _, r in where[n]])
            continue
        w2 = weights[n].reshape(-1, weights[n].shape[-1])
        d, mn, vn = adamw(w2, grads[n].reshape(w2.shape), args["m_" + n].reshape(w2.shape),
                          args["v_" + n].reshape(w2.shape))
        delta[n], new_m[n], new_v[n] = (a.reshape(weights[n].shape) for a in (d, mn, vn))
    own_sizes = [weights[n].size for n in _SMALL]
    own_rows = -(-sum(own_sizes) // (8 * D)) * 8
    packed = [_pack_rows([src[n] for n in _SMALL], F32, pad_to=own_rows)
              for src in (weights, grads, {n: args["m_" + n] for n in _SMALL}, {n: args["v_" + n] for n in _SMALL})]
    outs = adamw(*packed)
    off = 0
    for n, sz in zip(_SMALL, own_sizes):
        for dst, o in zip((delta, new_m, new_v), outs):
            dst[n] = o.reshape(-1)[off:off + sz].reshape(weights[n].shape)
        off += sz

    order = ["norm_mix", "norm_ffn", "norm_ple", "mla_w_down", "mla_q_lora_g", "mla_kv_lora_g", "mla_w_uq",
             "mla_w_ukv", "mla_q_nope_g", "mla_q_rope_g", "mla_k_nope_g", "mla_k_rope_g", "mla_w_out", "gmlp_w_in",
             "gmlp_ln_g", "gmlp_ln_b", "gmlp_w_s", "gmlp_b_s", "gmlp_w_out", "ffn_w_up", "ffn_w_down", "ple_w_gate",
             "ple_w_proj"]
    return (loss, grad_x, *[grads[n] for n in order], *[delta[n] for n in order], *[new_m[n] for n in order],
            *[new_v[n] for n in order])
```

```python
import functools

import jax
import jax.numpy as jnp
from jax import lax
from jax.experimental import pallas as pl
from jax.experimental.pallas import tpu as pltpu

F32 = jnp.float32
BF16 = jnp.bfloat16
MESH = pl.DeviceIdType.MESH

D = 1024
HEADS = 8
DN = 128
DR = 64
QL = 384
KVL = 256
LAT = 704
LATP = 768
DFF = 4096
GH = 2048
GC = 128
GG = 8
GD = 256
PLE = 256
EPS = 1e-6
ROPE_BASE = 10000.0
SM_SCALE = (DN + DR) ** -0.5
N_CHIPS = 4
LANES = 128

ADAM_LR = 0.001
ADAM_B1 = 0.9
ADAM_B2 = 0.999
ADAM_EPS = 1e-08
ADAM_WD = 0.01
ADAM_STEP = 10

TM = 256
TMB = 512
TQ = 512
TQ_FWD = 512
FWD_HEADS = 2
BWD_HEADS = 2
SUM_ROWS = 256
VMEM_LIMIT = 56 * 1024 * 1024


def _cp(*sem):
    return pltpu.CompilerParams(dimension_semantics=sem, vmem_limit_bytes=VMEM_LIMIT)


def _dot(a, b):
    return jnp.dot(a, b, preferred_element_type=F32)


def _dot_nt(a, b):
    return lax.dot_general(a, b, (((1,), (1,)), ((), ())), preferred_element_type=F32)


def _dot_tn(a, b):
    return lax.dot_general(a, b, (((0,), (0,)), ((), ())), preferred_element_type=F32)


def _rms(x, g, n):
    r = lax.rsqrt(jnp.sum(x * x, axis=-1, keepdims=True) * (1.0 / n) + EPS)
    xhat = x * r
    return xhat * g, xhat, r


def _rms_bwd(dy, g, xhat, r, n):
    dxhat = dy * g
    return r * (dxhat - xhat * (jnp.sum(dxhat * xhat, axis=-1, keepdims=True) * (1.0 / n)))


def _rope(x, c, s):
    return x * c + (pltpu.roll(x, 32, 1) - pltpu.roll(x, 96, 1)) * s


def _rope_t(dy, c, s):
    w = dy * s
    return dy * c + pltpu.roll(w, 96, 1) - pltpu.roll(w, 32, 1)


def _sigmoid(x):
    return 1.0 / (1.0 + jnp.exp(-x))


_GELU_K = 0.7978845608028654
_GELU_C = 0.044715


def _gelu(x):
    return 0.5 * x * (1.0 + jnp.tanh(_GELU_K * (x + _GELU_C * x * x * x)))


def _gelu_and_grad(x):
    x2 = x * x
    t = jnp.tanh(_GELU_K * (x + _GELU_C * x2 * x))
    half = 0.5 * (1.0 + t)
    return x * half, half + 0.5 * x * (1.0 - t * t) * (_GELU_K * (1.0 + 3.0 * _GELU_C * x2))


def _acc_rows(ref, val):
    ref[...] += jnp.broadcast_to(jnp.sum(val, axis=0, keepdims=True), ref.shape)


def _row(tm, c):
    return pl.BlockSpec((tm, c), lambda i: (i, 0))


def _const(shape):
    nd = len(shape)
    return pl.BlockSpec(shape, lambda i: (0,) * nd, pipeline_mode=pl.Buffered(1))


def _wblk(rows, row0):
    assert row0 % rows == 0, (rows, row0)
    return pl.BlockSpec((N_CHIPS, rows, D), lambda i: (0, row0 // rows, 0), pipeline_mode=pl.Buffered(1))


def _rows_joined(w_ref):
    return w_ref[...].reshape(N_CHIPS * w_ref.shape[1], D)


def _sds(shape, dtype):
    return jax.ShapeDtypeStruct(shape, dtype)


def mixffn_fwd(h, y, allw, lay, g2):
    t, k = y.shape

    def body(h_ref, y_ref, wo_ref, g_ref, wu_ref, wd_ref, h1_ref, h2_ref, hn_ref, r_ref):
        h1 = h_ref[...] + _dot(y_ref[...], _rows_joined(wo_ref))
        h1_ref[...] = h1
        yn, _, _ = _rms(h1, g_ref[...], D)
        hn = yn.astype(BF16)
        hn_ref[...] = hn
        f = jnp.zeros((TMB, D), F32)
        for c in range(N_CHIPS):
            r = jnp.maximum(_dot(hn, wu_ref[c]), 0.0)
            r_ref[:, c * D:(c + 1) * D] = r.astype(BF16)
            f = f + _dot((r * r).astype(BF16), wd_ref[c])
        h2_ref[...] = h1 + f

    return pl.pallas_call(
        body, name="mixffn_fwd", grid=(t // TMB,),
        in_specs=[_row(TMB, D), _row(TMB, k), _wblk(k // N_CHIPS, lay["out"]), _const((1, D)), _wblk(D, lay["up"]),
                  _wblk(D, lay["down"])],
        out_specs=[_row(TMB, D), _row(TMB, D), _row(TMB, D), _row(TMB, DFF)],
        out_shape=[_sds((t, D), F32), _sds((t, D), F32), _sds((t, D), BF16), _sds((t, DFF), BF16)],
        compiler_params=_cp("parallel"),
    )(h, y, allw, g2, allw, allw)


def _layer_rows(tm, c, layer):
    return pl.BlockSpec((None, tm, c), lambda i: (layer, i, 0))


def ple_fwd(h2, p, layer, g3, allw, lay, wp):
    t = h2.shape[0]

    def body(h_ref, p_ref, g_ref, wg_ref, wp_ref, h3_ref, hn_ref):
        x = h_ref[...]
        yn, _, _ = _rms(x, g_ref[...], D)
        hn = yn.astype(BF16)
        hn_ref[...] = hn
        gt = _dot(hn, _rows_joined(wg_ref))
        pp = _dot(p_ref[...].astype(BF16), wp_ref[...])
        h3_ref[...] = x + _sigmoid(gt) * pp

    return pl.pallas_call(
        body, name="ple_fwd", grid=(t // TMB,),
        in_specs=[_row(TMB, D), _layer_rows(TMB, PLE, layer), _const((1, D)), _wblk(D // N_CHIPS, lay["gate"]),
                  _const((PLE, D))],
        out_specs=[_row(TMB, D), _row(TMB, D)],
        out_shape=[_sds((t, D), F32), _sds((t, D), BF16)],
        compiler_params=_cp("parallel"),
    )(h2, p, g3, allw, wp)


def _mla_project(h_ref, g1_ref, wdn_ref, gq_ref, gkv_ref, wuq_ref, wukv_ref):
    x = h_ref[...]
    yn, xhat, rx = _rms(x, g1_ref[...], D)
    hn = yn.astype(BF16)
    lat = _dot(hn, wdn_ref[...])
    cq, cqhat, rq = _rms(lat[:, :QL], gq_ref[...], QL)
    ckv, ckvhat, rkv = _rms(lat[:, QL:QL + KVL], gkv_ref[...], KVL)
    kr_raw = lat[:, QL + KVL:]
    cqb = cq.astype(BF16)
    ckvb = ckv.astype(BF16)
    qp = _dot(cqb, wuq_ref[...])
    kvp = _dot(ckvb, wukv_ref[...])
    return dict(xhat=xhat, rx=rx, hn=hn, cqhat=cqhat, rq=rq, ckvhat=ckvhat, rkv=rkv, kr_raw=kr_raw,
                cqb=cqb, ckvb=ckvb, qp=qp, kvp=kvp)


def mla_pre_fwd(h, g1, wdn, gq, gkv, wuq, wukv, gqn, gqr, gkn, gkr, cos, sin):
    t = h.shape[0]

    def body(h_ref, g1_ref, wdn_ref, gq_ref, gkv_ref, wuq_ref, wukv_ref, gqn_ref, gqr_ref, gkn_ref, gkr_ref,
             c_ref, s_ref, q_ref, k_ref, v_ref):
        m = _mla_project(h_ref, g1_ref, wdn_ref, gq_ref, gkv_ref, wuq_ref, wukv_ref)
        c = c_ref[...]
        s = s_ref[...]
        kr, _, _ = _rms(m["kr_raw"], gkr_ref[...], DR)
        krb = _rope(kr, c, s).astype(BF16)
        for hd in range(HEADS):
            qn, _, _ = _rms(m["qp"][:, hd * DN:(hd + 1) * DN], gqn_ref[...], DN)
            qr, _, _ = _rms(m["qp"][:, D + hd * LANES:D + (hd + 1) * LANES], gqr_ref[...], DR)
            q_ref[hd, :, 0:DN] = (qn * SM_SCALE).astype(BF16)
            q_ref[hd, :, DN:2 * DN] = (_rope(qr, c, s) * SM_SCALE).astype(BF16)
            kn, _, _ = _rms(m["kvp"][:, hd * 2 * DN:hd * 2 * DN + DN], gkn_ref[...], DN)
            k_ref[hd, :, 0:DN] = kn.astype(BF16)
            k_ref[hd, :, DN:2 * DN] = krb
            v_ref[hd] = m["kvp"][:, hd * 2 * DN + DN:(hd + 1) * 2 * DN].astype(BF16)

    hb = lambda w: pl.BlockSpec((HEADS, TM, w), lambda i: (0, i, 0))
    return pl.pallas_call(
        body, name="mla_pre_fwd", grid=(t // TM,),
        in_specs=[_row(TM, D), _const((1, D)), _const((D, LATP)), _const((1, QL)), _const((1, KVL)),
                  _const((QL, 2 * D)), _const((KVL, 2 * D)), _const((1, LANES)), _const((1, LANES)),
                  _const((1, LANES)), _const((1, LANES)), _row(TM, LANES), _row(TM, LANES)],
        out_specs=[hb(2 * DN), hb(2 * DN), hb(DN)],
        out_shape=[_sds((HEADS, t, 2 * DN), BF16), _sds((HEADS, t, 2 * DN), BF16), _sds((HEADS, t, DN), BF16)],
        compiler_params=_cp("parallel"),
    )(h, g1, wdn, gq, gkv, wuq, wukv, gqn, gqr, gkn, gkr, cos, sin)


def _diagonal_mask(n=TQ):
    return lax.broadcasted_iota(jnp.int32, (n, n), 1) <= lax.broadcasted_iota(jnp.int32, (n, n), 0)


def flash_fwd(q, k, v, seq):
    t = q.shape[1]
    nb = t // seq
    tq = TQ_FWD
    nq = seq // tq
    hp = FWD_HEADS

    def body(q_ref, k_ref, v_ref, o_ref, lse_ref):
        qi = pl.program_id(2)
        qs = [q_ref[a] for a in range(hp)]

        def step(j, carry, diagonal=False):
            rows = pl.ds(pl.multiple_of(j * tq, tq), tq)
            out = []
            for a in range(hp):
                m, l, acc = carry[a]
                s = _dot_nt(qs[a], k_ref[a, rows, :])
                if diagonal:
                    s = jnp.where(_diagonal_mask(tq), s, -1e30)
                m_new = jnp.maximum(m, jnp.max(s, axis=-1, keepdims=True))
                p = jnp.exp(s - m_new)
                alpha = jnp.exp(m - m_new)
                l = alpha * l + jnp.sum(p, axis=-1, keepdims=True)
                acc = alpha * acc + _dot(p.astype(BF16), v_ref[a, rows, :])
                out.append((m_new, l, acc))
            return tuple(out)

        one = (jnp.full((tq, 1), -1e30, F32), jnp.zeros((tq, 1), F32), jnp.zeros((tq, DN), F32))
        done = step(qi, lax.fori_loop(0, qi, step, (one,) * hp), diagonal=True)
        for a, (m, l, acc) in enumerate(done):
            o_ref[:, a * DN:(a + 1) * DN] = (acc / l).astype(BF16)
            lse_ref[a] = m + jnp.log(l)

    return pl.pallas_call(
        body, name="flash_fwd", grid=(nb, HEADS // hp, nq),
        in_specs=[pl.BlockSpec((hp, tq, 2 * DN), lambda b, h, i: (h, b * nq + i, 0)),
                  pl.BlockSpec((hp, seq, 2 * DN), lambda b, h, i: (h, b, 0)),
                  pl.BlockSpec((hp, seq, DN), lambda b, h, i: (h, b, 0))],
        out_specs=[pl.BlockSpec((tq, hp * DN), lambda b, h, i: (b * nq + i, h)),
                   pl.BlockSpec((hp, tq, 1), lambda b, h, i: (h, b * nq + i, 0))],
        out_shape=[_sds((t, HEADS * DN), BF16), _sds((HEADS, t, 1), F32)],
        compiler_params=_cp("parallel", "parallel", "arbitrary"),
    )(q, k, v)


def _gmlp_in(hn, win_ref):
    pre = [_dot(hn, win_ref[c]) for c in range(N_CHIPS)]
    return jnp.concatenate(pre[:2], axis=1), jnp.concatenate(pre[2:], axis=1)


def gmlp_fwd(h, g1, allw, lay, lng, lnb, wm, bfull):
    t = h.shape[0]

    def body(h_ref, g1_ref, win_ref, lng_ref, lnb_ref, wm_ref, b_ref, y_ref, pre_ref):
        yn, _, _ = _rms(h_ref[...], g1_ref[...], D)
        pre_u, pre_v = _gmlp_in(yn.astype(BF16), win_ref)
        pre_ref[:, :GH] = pre_u.astype(BF16)
        pre_ref[:, GH:] = pre_v.astype(BF16)
        u = _gelu(pre_u)
        v = _gelu(pre_v)
        xc = v - jnp.mean(v, axis=-1, keepdims=True)
        rs = lax.rsqrt(jnp.mean(xc * xc, axis=-1, keepdims=True) + EPS)
        vnb = (xc * rs * lng_ref[...] + lnb_ref[...]).astype(BF16)
        for ch in range(TM // GC):
            rows = slice(ch * GC, (ch + 1) * GC)
            for g in range(GG):
                cols = slice(g * GD, (g + 1) * GD)
                sv = _dot(wm_ref[g], vnb[rows, cols]) + b_ref[:, cols]
                y_ref[rows, cols] = (u[rows, cols] * sv).astype(BF16)

    return pl.pallas_call(
        body, name="gmlp_fwd", grid=(t // TM,),
        in_specs=[_row(TM, D), _const((1, D)), _wblk(D, lay["in"]), _const((1, GH)), _const((1, GH)),
                  _const((GG, GC, GC)), _const((GC, GH))],
        out_specs=[_row(TM, GH), _row(TM, 2 * GH)],
        out_shape=[_sds((t, GH), BF16), _sds((t, 2 * GH), BF16)],
        compiler_params=_cp("parallel"),
    )(h, g1, allw, lng, lnb, wm, bfull)


def loss_head(h, tgt):
    t = h.shape[0]

    def body(h_ref, t_ref, dh_ref, loss_ref):
        @pl.when(pl.program_id(0) == 0)
        def _():
            loss_ref[...] = jnp.zeros_like(loss_ref)

        e = h_ref[...] - t_ref[...]
        dh_ref[...] = e * (1.0 / D)
        part = jnp.sum(jnp.sum(e * e, axis=-1, keepdims=True), axis=0, keepdims=True) * (0.5 / D)
        loss_ref[...] += jnp.broadcast_to(part, loss_ref.shape)

    return pl.pallas_call(
        body, name="loss_head", grid=(t // TMB,),
        in_specs=[_row(TMB, D), _row(TMB, D)],
        out_specs=[_row(TMB, D), _const((8, LANES))],
        out_shape=[_sds((t, D), F32), _sds((8, LANES), F32)],
        compiler_params=_cp("arbitrary"),
    )(h, tgt)


def _zero_at_first_step(*refs):
    @pl.when(pl.program_id(0) == 0)
    def _():
        for r in refs:
            r[...] = jnp.zeros_like(r)


def ple_bwd(dh3, h2, p, layer, g3, allw, lay, wp):
    t = h2.shape[0]

    def body(dh_ref, h_ref, p_ref, g_ref, wg_ref, wp_ref, dh2_ref, dgt_ref, dpp_ref, dg_ref):
        _zero_at_first_step(dg_ref)
        dh3v = dh_ref[...]
        x = h_ref[...]
        g = g_ref[...]
        wg = _rows_joined(wg_ref)
        yn, xhat, r = _rms(x, g, D)
        gt = _dot(yn.astype(BF16), wg)
        pp = _dot(p_ref[...].astype(BF16), wp_ref[...])
        sg = _sigmoid(gt)
        dgt = (dh3v * pp * sg * (1.0 - sg)).astype(BF16)
        dgt_ref[...] = dgt
        dpp_ref[...] = (dh3v * sg).astype(BF16)
        dhn = _dot_nt(dgt, wg)
        _acc_rows(dg_ref, dhn * xhat)
        dh2_ref[...] = dh3v + _rms_bwd(dhn, g, xhat, r, D)

    return pl.pallas_call(
        body, name="ple_bwd", grid=(t // TMB,),
        in_specs=[_row(TMB, D), _row(TMB, D), _layer_rows(TMB, PLE, layer), _const((1, D)),
                  _wblk(D // N_CHIPS, lay["gate"]),
                  _const((PLE, D))],
        out_specs=[_row(TMB, D), _row(TMB, D), _row(TMB, D), _const((8, D))],
        out_shape=[_sds((t, D), F32), _sds((t, D), BF16), _sds((t, D), BF16), _sds((8, D), F32)],
        compiler_params=_cp("arbitrary"),
    )(dh3, h2, p, g3, allw, wp)


def ffn_bwd(dh2, h1, r, g2, allw, lay, mixer_rows=None):
    t = h1.shape[0]

    def body(dh_ref, h_ref, r_ref, g_ref, wu_ref, wd_ref, *rest):
        wo_ref = rest[0] if mixer_rows else None
        dh1_ref, dh1b_ref, du_ref, a_ref, dg_ref, dhb_ref = rest[1:7] if mixer_rows else rest[:6]
        _zero_at_first_step(dg_ref)
        dhb = dh_ref[...].astype(BF16)
        dhb_ref[...] = dhb
        g = g_ref[...]
        _, xhat, rr = _rms(h_ref[...], g, D)
        dhn = jnp.zeros((TM, D), F32)
        for c in range(N_CHIPS):
            cs = slice(c * D, (c + 1) * D)
            rc = r_ref[:, cs].astype(F32)
            a_ref[:, cs] = (rc * rc).astype(BF16)
            da = _dot_nt(dhb, wd_ref[c])
            du = (da * (2.0 * rc)).astype(BF16)
            du_ref[:, cs] = du
            dhn = dhn + _dot_nt(du, wu_ref[c])
        _acc_rows(dg_ref, dhn * xhat)
        dh1 = dh_ref[...] + _rms_bwd(dhn, g, xhat, rr, D)
        dh1_ref[...] = dh1
        dh1b = dh1.astype(BF16)
        dh1b_ref[...] = dh1b
        if mixer_rows:
            rest[7][...] = _dot_nt(dh1b, _rows_joined(wo_ref)).astype(BF16)

    k = N_CHIPS * mixer_rows if mixer_rows else 0
    return pl.pallas_call(
        body, name="ffn_bwd", grid=(t // TM,),
        in_specs=[_row(TM, D), _row(TM, D), _row(TM, DFF), _const((1, D)), _wblk(D, lay["up"]),
                  _wblk(D, lay["down"])] + ([_wblk(mixer_rows, lay["out"])] if mixer_rows else []),
        out_specs=[_row(TM, D), _row(TM, D), _row(TM, DFF), _row(TM, DFF), _const((8, D)), _row(TM, D)]
        + ([_row(TM, k)] if mixer_rows else []),
        out_shape=[_sds((t, D), F32), _sds((t, D), BF16), _sds((t, DFF), BF16), _sds((t, DFF), BF16),
                   _sds((8, D), F32), _sds((t, D), BF16)] + ([_sds((t, k), BF16)] if mixer_rows else []),
        compiler_params=_cp("arbitrary"),
    )(dh2, h1, r, g2, allw, allw, *([allw] if mixer_rows else []))


def flash_bwd(q, k, v, o, do, lse, seq, after):
    t = q.shape[1]
    nb = t // seq
    nq = seq // TQ
    hp = BWD_HEADS

    def body(q_ref, k_ref, v_ref, o_ref, do_ref, lse_ref, after_ref, dq_ref, dk_ref, dv_ref):
        del after_ref
        kj = pl.program_id(2)

        @pl.when(kj == 0)
        def _():
            dq_ref[...] = jnp.zeros_like(dq_ref)

        def step(i, carry, diagonal=False):
            rows = pl.ds(pl.multiple_of(i * TQ, TQ), TQ)
            out = []
            for a in range(hp):
                dk, dv = carry[a]
                kv = k_ref[a]
                qv = q_ref[a, rows, :]
                dov = do_ref[rows, a * DN:(a + 1) * DN]
                ov = o_ref[rows, a * DN:(a + 1) * DN]
                delta = jnp.sum(dov.astype(F32) * ov.astype(F32), axis=-1, keepdims=True)
                s = _dot_nt(qv, kv)
                if diagonal:
                    s = jnp.where(_diagonal_mask(), s, -1e30)
                p = jnp.exp(s - lse_ref[a, rows, :])
                dp = _dot_nt(dov, v_ref[a])
                ds = (p * (dp - delta)).astype(BF16)
                dv = dv + _dot_tn(p.astype(BF16), dov)
                dk = dk + _dot_tn(ds, qv)
                dq_ref[a, rows, :] += _dot(ds, kv)
                out.append((dk, dv))
            return tuple(out)

        one = (jnp.zeros((TQ, 2 * DN), F32), jnp.zeros((TQ, DN), F32))
        done = lax.fori_loop(kj + 1, nq, step, step(kj, (one,) * hp, diagonal=True))
        for a, (dk, dv) in enumerate(done):
            dk_ref[a] = dk
            dv_ref[a] = dv

    return pl.pallas_call(
        body, name="flash_bwd", grid=(nb, HEADS // hp, nq),
        in_specs=[pl.BlockSpec((hp, seq, 2 * DN), lambda b, h, j: (h, b, 0)),
                  pl.BlockSpec((hp, TQ, 2 * DN), lambda b, h, j: (h, b * nq + j, 0)),
                  pl.BlockSpec((hp, TQ, DN), lambda b, h, j: (h, b * nq + j, 0)),
                  pl.BlockSpec((seq, hp * DN), lambda b, h, j: (b, h)),
                  pl.BlockSpec((seq, hp * DN), lambda b, h, j: (b, h)),
                  pl.BlockSpec((hp, seq, 1), lambda b, h, j: (h, b, 0)), _ANY],
        out_specs=[pl.BlockSpec((hp, seq, 2 * DN), lambda b, h, j: (h, b, 0)),
                   pl.BlockSpec((hp, TQ, 2 * DN), lambda b, h, j: (h, b * nq + j, 0)),
                   pl.BlockSpec((hp, TQ, DN), lambda b, h, j: (h, b * nq + j, 0))],
        out_shape=[_sds((HEADS, t, 2 * DN), F32), _sds((HEADS, t, 2 * DN), F32), _sds((HEADS, t, DN), F32)],
        compiler_params=_cp("parallel", "parallel", "arbitrary"),
    )(q, k, v, o, do, lse, after)


def mla_pre_bwd(dq, dk, dv, dh1, h, g1, wdn, gq, gkv, wuq, wukv, gqn, gqr, gkn, gkr, cos, sin):
    t = h.shape[0]

    def body(dq_ref, dk_ref, dv_ref, dh1_ref, h_ref, g1_ref, wdn_ref, gq_ref, gkv_ref, wuq_ref, wukv_ref,
             gqn_ref, gqr_ref, gkn_ref, gkr_ref, c_ref, s_ref,
             dh_ref, hn_ref, cq_ref, ckv_ref, dqp_ref, dkvp_ref, dlat_ref,
             dg1_ref, dgq_ref, dgkv_ref, dgqn_ref, dgqr_ref, dgkn_ref, dgkr_ref):
        _zero_at_first_step(dg1_ref, dgq_ref, dgkv_ref, dgqn_ref, dgqr_ref, dgkn_ref, dgkr_ref)
        m = _mla_project(h_ref, g1_ref, wdn_ref, gq_ref, gkv_ref, wuq_ref, wukv_ref)
        hn_ref[...] = m["hn"]
        cq_ref[...] = m["cqb"]
        ckv_ref[...] = m["ckvb"]
        c = c_ref[...]
        s = s_ref[...]
        gqn = gqn_ref[...]
        gqr = gqr_ref[...]
        gkn = gkn_ref[...]
        gkr = gkr_ref[...]

        dkr = dk_ref[0, :, DN:2 * DN]
        for hd in range(1, HEADS):
            dkr = dkr + dk_ref[hd, :, DN:2 * DN]
        dkr = _rope_t(dkr, c, s)
        _, krhat, rkr = _rms(m["kr_raw"], gkr, DR)
        _acc_rows(dgkr_ref, dkr * krhat)
        dkr_raw = _rms_bwd(dkr, gkr, krhat, rkr, DR)

        for hd in range(HEADS):
            ncols = slice(hd * DN, (hd + 1) * DN)
            _, xh, r = _rms(m["qp"][:, ncols], gqn, DN)
            dqn = dq_ref[hd, :, 0:DN] * SM_SCALE
            _acc_rows(dgqn_ref, dqn * xh)
            dqp_ref[:, ncols] = _rms_bwd(dqn, gqn, xh, r, DN).astype(BF16)

            rcols = slice(D + hd * LANES, D + (hd + 1) * LANES)
            _, xh, r = _rms(m["qp"][:, rcols], gqr, DR)
            dqr = _rope_t(dq_ref[hd, :, DN:2 * DN] * SM_SCALE, c, s)
            _acc_rows(dgqr_ref, dqr * xh)
            dqp_ref[:, rcols] = _rms_bwd(dqr, gqr, xh, r, DR).astype(BF16)

            kcols = slice(hd * 2 * DN, hd * 2 * DN + DN)
            _, xh, r = _rms(m["kvp"][:, kcols], gkn, DN)
            dkn = dk_ref[hd, :, 0:DN]
            _acc_rows(dgkn_ref, dkn * xh)
            dkvp_ref[:, kcols] = _rms_bwd(dkn, gkn, xh, r, DN).astype(BF16)
            dkvp_ref[:, hd * 2 * DN + DN:(hd + 1) * 2 * DN] = dv_ref[hd].astype(BF16)

        dcq = _dot_nt(dqp_ref[...], wuq_ref[...])
        _acc_rows(dgq_ref, dcq * m["cqhat"])
        dlat_q = _rms_bwd(dcq, gq_ref[...], m["cqhat"], m["rq"], QL)
        dckv = _dot_nt(dkvp_ref[...], wukv_ref[...])
        _acc_rows(dgkv_ref, dckv * m["ckvhat"])
        dlat_kv = _rms_bwd(dckv, gkv_ref[...], m["ckvhat"], m["rkv"], KVL)
        dlat = jnp.concatenate([dlat_q, dlat_kv, dkr_raw], axis=1).astype(BF16)
        dlat_ref[...] = dlat
        dhn = _dot_nt(dlat, wdn_ref[...])
        _acc_rows(dg1_ref, dhn * m["xhat"])
        dh_ref[...] = dh1_ref[...] + _rms_bwd(dhn, g1_ref[...], m["xhat"], m["rx"], D)

    hb = lambda w: pl.BlockSpec((HEADS, TM, w), lambda i: (0, i, 0))
    return pl.pallas_call(
        body, name="mla_pre_bwd", grid=(t // TM,),
        in_specs=[hb(2 * DN), hb(2 * DN), hb(DN), _row(TM, D), _row(TM, D), _const((1, D)), _const((D, LATP)),
                  _const((1, QL)), _const((1, KVL)), _const((QL, 2 * D)), _const((KVL, 2 * D)),
                  _const((1, LANES)), _const((1, LANES)), _const((1, LANES)), _const((1, LANES)),
                  _row(TM, LANES), _row(TM, LANES)],
        out_specs=[_row(TM, D), _row(TM, D), _row(TM, QL), _row(TM, KVL), _row(TM, 2 * D), _row(TM, 2 * D),
                   _row(TM, LATP), _const((8, D)), _const((8, QL)), _const((8, KVL)), _const((8, LANES)),
                   _const((8, LANES)), _const((8, LANES)), _const((8, LANES))],
        out_shape=[_sds((t, D), F32), _sds((t, D), BF16), _sds((t, QL), BF16), _sds((t, KVL), BF16),
                   _sds((t, 2 * D), BF16), _sds((t, 2 * D), BF16), _sds((t, LATP), BF16),
                   _sds((8, D), F32), _sds((8, QL), F32), _sds((8, KVL), F32), _sds((8, LANES), F32),
                   _sds((8, LANES), F32), _sds((8, LANES), F32), _sds((8, LANES), F32)],
        compiler_params=_cp("arbitrary"),
    )(dq, dk, dv, dh1, h, g1, wdn, gq, gkv, wuq, wukv, gqn, gqr, gkn, gkr, cos, sin)


def gmlp_bwd(dh1, dh1b, h, pre, g1, allw, lay, lng, lnb, wm, wmt, bfull, tril):
    t = h.shape[0]

    def body(dh1_ref, dh1b_ref, h_ref, pre_ref, g1_ref, win_ref, lng_ref, lnb_ref, wm_ref, wmt_ref, b_ref,
             wout_ref, tril_ref, dh_ref, hn_ref, dpre_ref, dws_ref, dbs_ref, dlng_ref, dlnb_ref, dg1_ref,
             dvn_s):
        _zero_at_first_step(dws_ref, dbs_ref, dlng_ref, dlnb_ref, dg1_ref)
        g1 = g1_ref[...]
        yn, xhat, rx = _rms(h_ref[...], g1, D)
        hn_ref[...] = yn.astype(BF16)
        dy = _dot_nt(dh1b_ref[...], _rows_joined(wout_ref))
        pre_u = pre_ref[:, :GH].astype(F32)
        pre_v = pre_ref[:, GH:].astype(F32)
        u, gg_u = _gelu_and_grad(pre_u)
        v, gg_v = _gelu_and_grad(pre_v)
        xc = v - jnp.mean(v, axis=-1, keepdims=True)
        rs = lax.rsqrt(jnp.mean(xc * xc, axis=-1, keepdims=True) + EPS)
        vhat = xc * rs
        lng = lng_ref[...]
        vnb = (vhat * lng + lnb_ref[...]).astype(BF16)
        dsv = dy * u
        dsvb = dsv.astype(BF16)
        tril_m = tril_ref[...]
        for ch in range(TM // GC):
            rows = slice(ch * GC, (ch + 1) * GC)
            dbs_ref[...] += dsv[rows, :]
            for g in range(GG):
                cols = slice(g * GD, (g + 1) * GD)
                sv = _dot(wm_ref[g], vnb[rows, cols]) + b_ref[:, cols]
                dpre_ref[rows, cols] = (dy[rows, cols] * sv * gg_u[rows, cols]).astype(BF16)
                dvn_s[rows, cols] = _dot(wmt_ref[g], dsvb[rows, cols])
                dws_ref[g] += _dot_nt(dsvb[rows, cols], vnb[rows, cols]) * tril_m
        dvn = dvn_s[...]
        _acc_rows(dlng_ref, dvn * vhat)
        _acc_rows(dlnb_ref, dvn)
        dvhat = dvn * lng
        dv = rs * (dvhat - jnp.mean(dvhat, axis=-1, keepdims=True)
                   - vhat * jnp.mean(dvhat * vhat, axis=-1, keepdims=True))
        dpre_v = (dv * gg_v).astype(BF16)
        dpre_ref[:, GH:] = dpre_v
        dhn = _dot_nt(dpre_ref[:, 0:D], win_ref[0])
        for c in range(1, N_CHIPS):
            dhn = dhn + _dot_nt(dpre_ref[:, c * D:(c + 1) * D], win_ref[c])
        _acc_rows(dg1_ref, dhn * xhat)
        dh_ref[...] = dh1_ref[...] + _rms_bwd(dhn, g1, xhat, rx, D)

    return pl.pallas_call(
        body, name="gmlp_bwd", grid=(t // TM,),
        in_specs=[_row(TM, D), _row(TM, D), _row(TM, D), _row(TM, 2 * GH), _const((1, D)), _wblk(D, lay["in"]),
                  _const((1, GH)), _const((1, GH)), _const((GG, GC, GC)), _const((GG, GC, GC)), _const((GC, GH)),
                  _wblk(GH // N_CHIPS, lay["out"]), _const((GC, GC))],
        out_specs=[_row(TM, D), _row(TM, D), _row(TM, 2 * GH), _const((GG, GC, GC)), _const((GC, GH)),
                   _const((8, GH)), _const((8, GH)), _const((8, D))],
        out_shape=[_sds((t, D), F32), _sds((t, D), BF16), _sds((t, 2 * GH), BF16), _sds((GG, GC, GC), F32),
                   _sds((GC, GH), F32), _sds((8, GH), F32), _sds((8, GH), F32), _sds((8, D), F32)],
        scratch_shapes=[pltpu.VMEM((TM, GH), F32)],
        compiler_params=_cp("arbitrary"),
    )(dh1, dh1b, h, pre, g1, allw, lng, lnb, wm, wmt, bfull, allw, tril)


def _token_step(t):
    return next(s for s in (2048, 1024, 512) if t % s == 0)


def mm_tn(a, b, layer=None):
    t, k = a.shape[-2:]
    n = b.shape[1]
    tk = min(k, 1024)
    tn = min(n, 1024)
    tt = _token_step(t)
    a_spec = (pl.BlockSpec((tt, tk), lambda i, j, s: (s, i)) if layer is None else
              pl.BlockSpec((None, tt, tk), lambda i, j, s: (layer, s, i)))

    def body(a_ref, b_ref, o_ref):
        @pl.when(pl.program_id(2) == 0)
        def _():
            o_ref[...] = jnp.zeros_like(o_ref)

        o_ref[...] += _dot_tn(a_ref[...].astype(BF16), b_ref[...].astype(BF16))

    return pl.pallas_call(
        body, name="mm_tn", grid=(k // tk, n // tn, t // tt),
        in_specs=[a_spec, pl.BlockSpec((tt, tn), lambda i, j, s: (s, j))],
        out_specs=pl.BlockSpec((tk, tn), lambda i, j, s: (i, j)), out_shape=_sds((k, n), F32),
        compiler_params=_cp("parallel", "parallel", "arbitrary"),
    )(a, b)


def mm_tn_into(buf, a, b, rows, row0, col_sharded):
    t = a.shape[0]
    tt = _token_step(t)
    assert row0 % rows == 0 and a.shape[1] == (rows if col_sharded else N_CHIPS * rows), (rows, row0, a.shape)
    assert b.shape[1] == (N_CHIPS * D if col_sharded else D), b.shape
    joint = not col_sharded and N_CHIPS * rows <= 2048
    grid = (1, 1, t // tt) if joint else (1, N_CHIPS, t // tt) if col_sharded else (N_CHIPS, 1, t // tt)
    fresh = isinstance(buf, int)

    def body(*refs):
        a_ref, b_ref, o_ref = refs[-3:]

        @pl.when(pl.program_id(2) == 0)
        def _():
            o_ref[...] = jnp.zeros_like(o_ref)

        o_ref[...] += _dot_tn(a_ref[...].astype(BF16), b_ref[...].astype(BF16)).reshape(o_ref.shape)

    specs = [pl.BlockSpec((tt, N_CHIPS * rows if joint else rows), lambda i, j, s: (s, i)),
             pl.BlockSpec((tt, D), lambda i, j, s: (s, j))]
    return pl.pallas_call(
        body, name="mm_tn_into", grid=grid,
        in_specs=specs if fresh else [_ANY] + specs,
        out_specs=pl.BlockSpec((N_CHIPS if joint else None, rows, D), lambda i, j, s: (i + j, row0 // rows, 0)),
        out_shape=_sds((N_CHIPS, buf, D) if fresh else buf.shape, F32),
        input_output_aliases={} if fresh else {0: 0},
        compiler_params=_cp("parallel", "parallel", "arbitrary"),
    )(*((a, b) if fresh else (buf, a, b)))


def adamw(w, g, m, v):
    rows, cols = w.shape
    tr = rows if rows <= 512 else next(r for r in (512, 384, 256, 128) if rows % r == 0)
    c1 = 1.0 - ADAM_B1 ** ADAM_STEP
    c2 = 1.0 - ADAM_B2 ** ADAM_STEP

    def body(w_ref, g_ref, m_ref, v_ref, d_ref, mo_ref, vo_ref):
        gv = g_ref[...]
        mn = ADAM_B1 * m_ref[...] + (1.0 - ADAM_B1) * gv
        vn = ADAM_B2 * v_ref[...] + (1.0 - ADAM_B2) * (gv * gv)
        mo_ref[...] = mn
        vo_ref[...] = vn
        d_ref[...] = -ADAM_LR * ((mn / c1) / (jnp.sqrt(vn / c2) + ADAM_EPS) + ADAM_WD * w_ref[...])

    spec = pl.BlockSpec((tr, cols), lambda i: (i, 0))
    return pl.pallas_call(
        body, name="adamw", grid=(rows // tr,),
        in_specs=[spec] * 4, out_specs=[spec] * 3, out_shape=[_sds((rows, cols), F32)] * 3,
        compiler_params=_cp("parallel"),
    )(w, g, m, v)


def adamw_layers(w, m, v, bufs, row0s):
    nl, a, _ = w.shape
    tr = min(a, 256)
    c1 = 1.0 - ADAM_B1 ** ADAM_STEP
    c2 = 1.0 - ADAM_B2 ** ADAM_STEP
    assert all(r % tr == 0 for r in row0s) and a % tr == 0, (row0s, a)

    def body(w_ref, m_ref, v_ref, *rest):
        g_refs, (g_ref, d_ref, mo_ref, vo_ref) = rest[:nl], rest[nl:]
        for l in range(nl):
            @pl.when(pl.program_id(0) == l)
            def _(l=l):
                gv = g_refs[l][...]
                g_ref[...] = gv
                mn = ADAM_B1 * m_ref[...] + (1.0 - ADAM_B1) * gv
                vn = ADAM_B2 * v_ref[...] + (1.0 - ADAM_B2) * (gv * gv)
                mo_ref[...] = mn
                vo_ref[...] = vn
                d_ref[...] = -ADAM_LR * ((mn / c1) / (jnp.sqrt(vn / c2) + ADAM_EPS) + ADAM_WD * w_ref[...])

    def rows_of(l, row0):
        return pl.BlockSpec((tr, D), lambda li, i: (jnp.where(li == l, row0 // tr + i, row0 // tr), 0))

    spec = pl.BlockSpec((None, tr, D), lambda li, i: (li, i, 0))
    return pl.pallas_call(
        body, name="adamw_layers", grid=(nl, a // tr),
        in_specs=[spec] * 3 + [rows_of(l, r) for l, r in enumerate(row0s)],
        out_specs=[spec] * 4, out_shape=[_sds(w.shape, F32)] * 4,
        compiler_params=_cp("arbitrary", "arbitrary"),
    )(w, m, v, *bufs)


def _place():
    return lax.axis_index("x"), lax.axis_index("y"), lax.axis_index("c")


def _other_chips(x, y):
    return [(1 - x, y), (x, 1 - y), (1 - x, 1 - y)]


_ANY = pl.BlockSpec(memory_space=pl.ANY)


_HBM = pl.BlockSpec(memory_space=pltpu.HBM)
_SEM = pl.BlockSpec(memory_space=pltpu.SEMAPHORE)
_EFFECT = pltpu.SideEffectType.DATAFLOW_SIDE_EFFECTING
N_ICI = 3


def _exchange_start(name, src, land, copies, n):
    def body(src_ref, land_ref, *outs):
        sems, token = outs[:2 * n], outs[-1]
        for j, (s, d, to) in enumerate(copies(src_ref, land_ref, _place())):
            pltpu.make_async_remote_copy(src_ref=s, dst_ref=d, send_sem=sems[j], recv_sem=sems[n + j],
                                         device_id=to, device_id_type=MESH).start()
        token[...] = jnp.zeros_like(token)

    sem = pltpu.SemaphoreType.DMA(())
    outs = pl.pallas_call(
        body, name=name,
        out_shape=(sem,) * (2 * n) + (pltpu.HBM(src.shape, src.dtype), pltpu.HBM(land.shape, land.dtype),
                                      _sds((8, LANES), F32)),
        in_specs=(_HBM, _HBM),
        out_specs=(_SEM,) * (2 * n) + (_HBM, _HBM, pl.BlockSpec(memory_space=pltpu.VMEM)),
        input_output_aliases={0: 2 * n, 1: 2 * n + 1},
        compiler_params=pltpu.CompilerParams(has_side_effects=_EFFECT),
    )(pltpu.with_memory_space_constraint(src, pltpu.HBM), pltpu.with_memory_space_constraint(land, pltpu.HBM))
    return outs[:2 * n], outs[2 * n], outs[2 * n + 1], outs[-1]


def _exchange_wait(name, sems, src, land, after, arrivals):
    n = len(sems) // 2

    def body(src_ref, land_ref, *rest):
        sems = rest[:2 * n]
        for j, (s, d) in enumerate(arrivals(src_ref, land_ref, _place())):
            cp = pltpu.make_async_remote_copy(src_ref=s, dst_ref=d, send_sem=sems[j], recv_sem=sems[n + j],
                                              device_id=_place(), device_id_type=MESH)
            cp.wait_send()
            cp.wait_recv()

    return pl.pallas_call(
        body, name=name, out_shape=(pltpu.HBM(src.shape, src.dtype), pltpu.HBM(land.shape, land.dtype)),
        in_specs=(_HBM, _HBM) + (_SEM,) * (2 * n) + (_ANY,), out_specs=(_HBM, _HBM),
        input_output_aliases={0: 0, 1: 1},
        compiler_params=pltpu.CompilerParams(has_side_effects=_EFFECT),
    )(src, land, *sems, after)


def _halves(c, hh):
    return pl.ds(pl.multiple_of(c * hh, 16), hh), pl.ds(pl.multiple_of((1 - c) * hh, 16), hh)


def gather_start(land, tag):
    _, rr, _ = land.shape
    assert rr % 32 == 0, rr

    def copies(_, land_ref, place):
        x, y, c = place
        mine = land_ref.at[2 * x + y, _halves(c, rr // 2)[0]]
        return [(mine, mine, (cx, cy, c)) for cx, cy in _other_chips(x, y)]

    return _exchange_start(f"gather_start_{tag}", jnp.zeros((8, LANES), F32), land, copies, N_ICI)


def gather_wait(sems, src, land, after, tag):
    def arrivals(_, land_ref, place):
        x, y, c = place
        half = _halves(c, land.shape[1] // 2)[0]
        return [(land_ref.at[2 * x + y, half], land_ref.at[2 * cx + cy, half]) for cx, cy in _other_chips(x, y)]

    return _exchange_wait(f"gather_wait_{tag}", sems, src, land, after, arrivals)


def pass_start(land, tag):
    def copies(_, land_ref, place):
        x, y, c = place
        half = _halves(c, land.shape[1] // 2)[0]
        return [(land_ref.at[2 * cx + cy, half], land_ref.at[2 * cx + cy, half], (x, y, 1 - c))
                for cx, cy in _other_chips(x, y)]

    return _exchange_start(f"pass_start_{tag}", jnp.zeros((8, LANES), F32), land, copies, N_ICI)


def pass_wait(sems, src, land, after, tag):
    def arrivals(_, land_ref, place):
        x, y, c = place
        mine, other = _halves(c, land.shape[1] // 2)
        return [(land_ref.at[2 * cx + cy, mine], land_ref.at[2 * cx + cy, other]) for cx, cy in _other_chips(x, y)]

    return _exchange_wait(f"pass_wait_{tag}", sems, src, land, after, arrivals)


def swap_start(g, tag):
    _, rr, cc = g.shape

    def copies(g_ref, got_ref, place):
        x, y, c = place
        other = _halves(c, rr // 2)[1]
        return [(g_ref.at[k, other], got_ref.at[k], (x, y, 1 - c)) for k in range(N_CHIPS)]

    return _exchange_start(f"swap_start_{tag}", g, lax.empty((N_CHIPS, rr // 2, cc), g.dtype), copies, N_CHIPS)


def swap_wait(sems, g, got, after, tag):
    def arrivals(g_ref, got_ref, place):
        other = _halves(place[2], g.shape[1] // 2)[1]
        return [(g_ref.at[k, other], got_ref.at[k]) for k in range(N_CHIPS)]

    return _exchange_wait(f"swap_wait_{tag}", sems, g, got, after, arrivals)


def chip_sum(place, g32, got):
    _, rr, cc = g32.shape
    hh = rr // 2
    tr = SUM_ROWS
    assert rr % 2 == 0 and hh % tr == 0, (rr, tr)
    nb = hh // tr

    def body(place_ref, g_ref, got_ref, own_ref, all_ref):
        s = g_ref[...] + got_ref[...].astype(F32)
        all_ref[...] = s.astype(BF16)
        own_ref[...] = g_ref[place_ref[1]] + got_ref[place_ref[1]].astype(F32)

    return pl.pallas_call(
        body, name="chip_sum",
        grid_spec=pltpu.PrefetchScalarGridSpec(
            num_scalar_prefetch=1, grid=(nb,),
            in_specs=[pl.BlockSpec((N_CHIPS, tr, cc), lambda i, pr: (0, pr[0] * nb + i, 0)),
                      pl.BlockSpec((N_CHIPS, tr, cc), lambda i, pr: (0, i, 0))],
            out_specs=[pl.BlockSpec((tr, cc), lambda i, pr: (i, 0)),
                       pl.BlockSpec((N_CHIPS, tr, cc), lambda i, pr: (0, i, 0))]),
        out_shape=[_sds((hh, cc), F32), _sds((N_CHIPS, hh, cc), BF16)],
        compiler_params=_cp("parallel"),
    )(place, g32, got)


def _scatter_copies(s_ref, land_ref, place):
    x, y, c = place
    return [(s_ref.at[2 * cx + cy], land_ref.at[j], (cx, cy, c)) for j, (cx, cy) in enumerate(_other_chips(x, y))]


def scatter_start(s, tag):
    return _exchange_start(f"scatter_start_{tag}", s, lax.empty((N_ICI,) + s.shape[1:], s.dtype), _scatter_copies, N_ICI)


def scatter_wait(sems, s, land, after, tag):
    return _exchange_wait(f"scatter_wait_{tag}", sems, s, land, after,
                          lambda s_ref, land_ref, place: [(a, b) for a, b, _ in _scatter_copies(s_ref, land_ref, place)])


def final_sum(place, own, got):
    hh, cc = own.shape
    tr = SUM_ROWS
    assert hh % tr == 0, (hh, tr)
    nb = hh // tr

    def body(place_ref, own_ref, got_ref, o_ref):
        del place_ref
        o_ref[...] = ((own_ref[...] + got_ref[0].astype(F32)) + got_ref[1].astype(F32)) + got_ref[2].astype(F32)

    return pl.pallas_call(
        body, name="final_sum",
        grid_spec=pltpu.PrefetchScalarGridSpec(
            num_scalar_prefetch=1, grid=(nb,),
            in_specs=[pl.BlockSpec((tr, cc), lambda i, pr: (i, 0)), pl.BlockSpec((3, tr, cc), lambda i, pr: (0, i, 0))],
            out_specs=pl.BlockSpec((tr, cc), lambda i, pr: (pr[0] * nb + i, 0))),
        out_shape=_sds((2 * hh, cc), F32),
        compiler_params=_cp("parallel"),
    )(place, own, got)


def share_start(f, tag):
    def copies(_, f_ref, place):
        x, y, c = place
        mine = f_ref.at[_halves(c, f.shape[0] // 2)[0]]
        return [(mine, mine, (x, y, 1 - c))]

    return _exchange_start(f"share_start_{tag}", jnp.zeros((8, LANES), F32), f, copies, 1)


def share_wait(sems, src, f, after, tag):
    def arrivals(_, f_ref, place):
        mine, other = _halves(place[2], f.shape[0] // 2)
        return [(f_ref.at[mine], f_ref.at[other])]

    return _exchange_wait(f"share_wait_{tag}", sems, src, f, after, arrivals)


N_DEV = 8


def _peers(place):
    x, y, c = place
    return [((1 - x) if r & 4 else x, (1 - y) if r & 2 else y, (1 - c) if r & 1 else c) for r in range(1, N_DEV)]


def _device_index(place):
    x, y, c = place
    return 4 * x + 2 * y + c


def small_start(land, tag):
    def copies(_, land_ref, place):
        mine = land_ref.at[_device_index(place)]
        return [(mine, mine, to) for to in _peers(place)]

    return _exchange_start(f"small_start_{tag}", jnp.zeros((8, LANES), F32), land, copies, N_DEV - 1)


def small_wait(sems, src, land, after, tag):
    def arrivals(_, land_ref, place):
        return [(land_ref.at[_device_index(place)], land_ref.at[_device_index(peer)]) for peer in _peers(place)]

    return _exchange_wait(f"small_wait_{tag}", sems, src, land, after, arrivals)


def sum_devices(land):
    _, rr, cc = land.shape
    tr = 56
    assert rr % tr == 0, rr

    def body(l_ref, o_ref):
        acc = l_ref[0]
        for d in range(1, N_DEV):
            acc = acc + l_ref[d]
        o_ref[...] = acc

    return pl.pallas_call(
        body, name="sum_devices", grid=(rr // tr,),
        in_specs=[pl.BlockSpec((N_DEV, tr, cc), lambda i: (0, i, 0))],
        out_specs=pl.BlockSpec((tr, cc), lambda i: (i, 0)), out_shape=_sds((rr, cc), F32),
        compiler_params=_cp("parallel"),
    )(land)


_BIG = ["mla_w_down", "mla_w_uq", "mla_w_ukv", "mla_w_out", "gmlp_w_in", "gmlp_w_out", "ffn_w_up", "ffn_w_down",
        "ple_w_gate", "ple_w_proj"]
_SMALL_REST = ["norm_mix", "norm_ffn", "norm_ple", "mla_q_lora_g", "mla_kv_lora_g", "mla_q_nope_g", "mla_q_rope_g",
               "mla_k_nope_g", "mla_k_rope_g"]
_SMALL_GMLP = ["gmlp_ln_g", "gmlp_ln_b", "gmlp_w_s", "gmlp_b_s"]
_SMALL = _SMALL_REST + _SMALL_GMLP

_LAY_MLA = dict(up=0, down=1024, out=2048, gate=2304, wdn=2560, wuq=2736, wukv=2880, proj=3008, rows=3072)
_LAY_MLA_MAIN = dict(up=0, down=1024, out=2048, gate=2304, rows=2560)
_LAY_MLA_ODD = dict(wdn=0, wuq=176, wukv=320, proj=448, rows=512)
_LAY_GMLP = {"up": 0, "down": 1024, "in": 2048, "out": 3072, "gate": 3584, "proj": 3840, "ln": 3904, "rows": 4096}
SPLIT_LAYERS = (0,)


def _layer_units(i):
    j = i // 2
    if i % 2 == 0:
        odd, lay = (_LAY_MLA_ODD, _LAY_MLA_MAIN) if i in SPLIT_LAYERS else (_LAY_MLA, _LAY_MLA)
        small = [("mla_w_down", j, odd["wdn"]), ("mla_w_uq", j, odd["wuq"]), ("mla_w_ukv", j, odd["wukv"]),
                 ("ple_w_proj", i, odd["proj"])]
        large = [("ffn_w_up", i, lay["up"]), ("ffn_w_down", i, lay["down"]), ("mla_w_out", j, lay["out"]),
                 ("ple_w_gate", i, lay["gate"])]
        return [("odd", odd, small), ("main", lay, large)] if i in SPLIT_LAYERS else [("main", lay, large + small)]
    lay = _LAY_GMLP
    return [("main", lay, [("ffn_w_up", i, lay["up"]), ("ffn_w_down", i, lay["down"]), ("gmlp_w_in", j, lay["in"]),
                           ("gmlp_w_out", j, lay["out"]), ("ple_w_gate", i, lay["gate"]),
                           ("ple_w_proj", i, lay["proj"])])]


def _pack_rows(parts, dtype, pad_to=None, slot=False):
    size = sum(p.size for p in parts)
    tail = [] if pad_to is None or pad_to * D == size else [jnp.zeros((pad_to * D - size,), dtype)]
    shape = (1, -1, D) if slot else (-1, D)
    if all(p.size % D == 0 for p in parts + tail):
        return jnp.concatenate([p.astype(dtype).reshape(shape) for p in parts + tail], axis=len(shape) - 2)
    return jnp.concatenate([p.astype(dtype).reshape(-1) for p in parts + tail]).reshape(shape)


def _odd(allw, row0, a, b):
    return allw[:, row0:row0 + a * b // D].reshape(N_CHIPS, a, b)


def _cols_joined(s):
    return jnp.transpose(s, (1, 0, 2)).reshape(s.shape[1], N_CHIPS * s.shape[2])


def _col_shards(full):
    a, bb = full.shape
    return jnp.transpose(full.reshape(a, N_CHIPS, bb // N_CHIPS), (1, 0, 2)).reshape(N_CHIPS, -1, D)


def _pad_lanes(g):
    return jnp.pad(g, ((0, 0), (0, LANES - g.shape[1])))


def _split_uq(wuq):
    l = wuq.shape[0]
    w = wuq.reshape(l, QL, HEADS, DN + DR)
    nope = w[..., :DN].reshape(l, QL, HEADS * DN)
    rope = jnp.pad(w[..., DN:], ((0, 0), (0, 0), (0, 0), (0, LANES - DR))).reshape(l, QL, HEADS * LANES)
    return jnp.concatenate([nope, rope], axis=-1)


def _merge_uq(d):
    nope = d[:, :HEADS * DN].reshape(QL, HEADS, DN)
    rope = d[:, HEADS * DN:].reshape(QL, HEADS, LANES)[..., :DR]
    return jnp.concatenate([nope, rope], axis=-1).reshape(QL, HEADS * (DN + DR))


def _rope_tables(positions):
    inv_freq = ROPE_BASE ** (-(jnp.arange(0, DR, 2, dtype=F32) / DR))
    ang = positions.reshape(-1).astype(F32)[:, None] * inv_freq
    z = jnp.zeros((ang.shape[0], LANES - DR), F32)
    return (jnp.concatenate([jnp.cos(ang), jnp.cos(ang), z], axis=1),
            jnp.concatenate([jnp.sin(ang), jnp.sin(ang), z], axis=1))


def kernel(x, p, positions, norm_mix, norm_ffn, norm_ple, mla_w_down, mla_q_lora_g, mla_kv_lora_g, mla_w_uq, mla_w_ukv, mla_q_nope_g, mla_q_rope_g, mla_k_nope_g, mla_k_rope_g, mla_w_out, gmlp_w_in, gmlp_ln_g, gmlp_ln_b, gmlp_w_s, gmlp_b_s, gmlp_w_out, ffn_w_up, ffn_w_down, ple_w_gate, ple_w_proj, loss_target, m_norm_mix, m_norm_ffn, m_norm_ple, m_mla_w_down, m_mla_q_lora_g, m_mla_kv_lora_g, m_mla_w_uq, m_mla_w_ukv, m_mla_q_nope_g, m_mla_q_rope_g, m_mla_k_nope_g, m_mla_k_rope_g, m_mla_w_out, m_gmlp_w_in, m_gmlp_ln_g, m_gmlp_ln_b, m_gmlp_w_s, m_gmlp_b_s, m_gmlp_w_out, m_ffn_w_up, m_ffn_w_down, m_ple_w_gate, m_ple_w_proj, v_norm_mix, v_norm_ffn, v_norm_ple, v_mla_w_down, v_mla_q_lora_g, v_mla_kv_lora_g, v_mla_w_uq, v_mla_w_ukv, v_mla_q_nope_g, v_mla_q_rope_g, v_mla_k_nope_g, v_mla_k_rope_g, v_mla_w_out, v_gmlp_w_in, v_gmlp_ln_g, v_gmlp_ln_b, v_gmlp_w_s, v_gmlp_b_s, v_gmlp_w_out, v_ffn_w_up, v_ffn_w_down, v_ple_w_gate, v_ple_w_proj):
    args = dict(locals())
    weights = {n: args[n] for n in _BIG + _SMALL}
    depth = norm_mix.shape[0]
    nb, seq, _ = x.shape
    t = nb * seq
    assert seq % TQ == 0 and seq % TM == 0 and t % 512 == 0, (nb, seq)
    cx = lax.axis_index("x")
    cy = lax.axis_index("y")
    cc = lax.axis_index("c")
    chip = 2 * cx + cy

    gathers = {}
    token = None
    for i in range(depth):
        for key, lay, parts in _layer_units(i):
            rows = [weights[n][l] for n, l, _ in parts]
            if token is not None:
                rows[0] = rows[0] + token[0, 0]
            if "ln" in lay:
                ln = jnp.stack([gmlp_ln_g[i // 2], gmlp_ln_b[i // 2]]).astype(F32)
                bits = lax.bitcast_convert_type(ln, BF16).reshape(-1)
                rows.append(jnp.pad(bits, (0, 16 * D - bits.size)).reshape(16, D))
            mine = _pack_rows(rows, BF16, pad_to=lay["rows"], slot=True)
            land = lax.dynamic_update_slice(lax.empty((N_CHIPS, lay["rows"], D), BF16), mine, (chip, 0, 0))
            sems, src, land, token = gather_start(land, f"{i}{key}")
            gathers[i, key] = (sems, src, land)
    allw = [None] * depth

    tril = jnp.tril(jnp.ones((GC, GC), F32))
    wm = (gmlp_w_s * tril).astype(BF16)
    wmt = jnp.swapaxes(wm, -1, -2)
    bfull = jnp.repeat(jnp.swapaxes(gmlp_b_s, -1, -2), GD, axis=-1)
    cos, sin = _rope_tables(positions)
    row = lambda g: g.reshape(1, -1)
    gqr = _pad_lanes(mla_q_rope_g)
    gkr = _pad_lanes(mla_k_rope_g)

    h = x.reshape(t, D)
    pt = p.reshape(depth, t, PLE)
    saved = []

    passing = {}

    def arrive(i, key, after):
        sems, src, land = gathers[i, key]
        _, land = gather_wait(sems, src, land, after, f"{i}{key}")
        passing[i, key] = pass_start(land, f"{i}{key}")
        return passing[i, key][3]

    def needed(i, key, after=None):
        sems, src, land, tok = passing.pop((i, key))
        return pass_wait(sems, src, land, tok if after is None else after, f"{i}{key}")[1]

    arrive(0, _layer_units(0)[0][0], token)
    for i in range(depth):
        j = i // 2
        lay = _layer_units(i)[-1][1]
        s = dict(h=h)
        if i % 2 == 0:
            split = i in SPLIT_LAYERS
            olay = _layer_units(i)[0][1]
            odd = needed(i, "odd" if split else "main", None if i == 0 else h)
            wdn = jnp.pad(_odd(odd, olay["wdn"], D // N_CHIPS, LAT).reshape(D, LAT), ((0, 0), (0, LATP - LAT)))
            wuq = _split_uq(_cols_joined(_odd(odd, olay["wuq"], QL, 384))[None])[0]
            wukv = _cols_joined(_odd(odd, olay["wukv"], KVL, 512))
            wp = _cols_joined(_odd(odd, olay["proj"], PLE, 256))
            mla_args = (row(norm_mix[i]), wdn, row(mla_q_lora_g[j]), row(mla_kv_lora_g[j]), wuq, wukv,
                        row(mla_q_nope_g[j]), gqr[j:j + 1], row(mla_k_nope_g[j]), gkr[j:j + 1], cos, sin)
            q, k, v = mla_pre_fwd(h, *mla_args)
            y, lse = flash_fwd(q, k, v, seq)
            if split and i == 0:
                arrive(i, "main", y)
            aw = needed(i, "main", y) if split else odd
            s.update(q=q, k=k, v=v, lse=lse, mla_args=mla_args)
        else:
            aw = needed(i, "main", h)
            ln = lax.bitcast_convert_type(aw[:, lay["ln"]:lay["ln"] + 2].reshape(N_CHIPS, 2, GH // N_CHIPS, 2), F32)
            ln = jnp.transpose(ln, (1, 0, 2)).reshape(2, 1, GH)
            wp = _cols_joined(_odd(aw, lay["proj"], PLE, 256))
            y, pre = gmlp_fwd(h, row(norm_mix[i]), aw, lay, ln[0], ln[1], wm[j], bfull[j])
            s.update(pre=pre, ln=ln)
        allw[i] = aw
        g2 = row(norm_ffn[i])
        if i + 1 < depth:
            for key, _, _ in _layer_units(i + 1):
                g2 = g2 + arrive(i + 1, key, y)[0:1, 0:1]
        h1, h2, hn2, r = mixffn_fwd(h, y, aw, lay, g2)
        h, hn3 = ple_fwd(h2, pt, i, row(norm_ple[i]), aw, lay, wp)
        s.update(y=y, wp=wp, h1=h1, h2=h2, hn2=hn2, r=r, hn3=hn3)
        saved.append(s)

    dh, loss_part = loss_head(h, loss_target.reshape(t, D))

    gs = {n: [None] * weights[n].shape[0] for n in _SMALL}
    gw = {n: [None] * weights[n].shape[0] for n in _BIG}
    place = jnp.stack([cc, chip]).astype(jnp.int32)
    scatters = []
    swaps = []
    token = None

    def put(b, row0, shards):
        return lax.dynamic_update_slice(b, shards.reshape(N_CHIPS, -1, D), (0, row0, 0))

    def small_size(n):
        return weights[n].shape[0] * GH if n in ("gmlp_ln_g", "gmlp_ln_b") else weights[n].size

    def small_exchange(names, zero, tag, extra=()):
        rows = -(-(sum(small_size(n) for n in names) + sum(e.size for e in extra)) // (56 * D)) * 56
        part = [jnp.stack(gs[n]) for n in names] + list(extra)
        part = _pack_rows([part[0] + zero] + part[1:], F32, pad_to=rows, slot=True)
        land = lax.dynamic_update_slice(lax.empty((N_DEV, rows, D), F32), part, (2 * chip + cc, 0, 0))
        return small_start(land, tag)

    def swap(i, key, buf):
        sems, buf, got, tok = swap_start(buf, f"{i}{key}")
        swaps.append((i, key, sems, buf, got))
        return tok

    def swapped(after, zero):
        while swaps:
            i, key, sems, g, got = swaps.pop(0)
            g, got = swap_wait(sems, g, got, after, f"{i}{key}")
            own, sums = chip_sum(place, g, got)
            sems, sums, land, tok = scatter_start(sums, f"{i}{key}")
            scatters.append((i, key, own, sems, sums, land))
            zero = zero + tok[0:1, 0:1]
        return zero

    for i in reversed(range(depth)):
        j = i // 2
        lay = _layer_units(i)[-1][1]
        aw = allw[i]
        s = saved[i]

        g3 = row(norm_ple[i])
        if token is not None:
            g3 = g3 + token[0:1, 0:1]
        dh2, dgt, dpp, dg3 = ple_bwd(dh, s["h2"], pt, i, g3, aw, lay, s["wp"])
        gs["norm_ple"][i] = dg3[0]
        buf = mm_tn_into(lay["rows"], s["hn3"], dgt, D // N_CHIPS, lay["gate"], False)
        dproj = _col_shards(mm_tn(pt, dpp, layer=i))
        if "ln" in lay:
            buf = put(buf, lay["ln"], jnp.zeros((N_CHIPS, lay["rows"] - lay["ln"], D), F32))
            buf = put(buf, lay["proj"], dproj)
        dh1, dh1b, du, a, dg2, dh2b, *do = ffn_bwd(dh2, s["h1"], s["r"], row(norm_ffn[i]), aw, lay,
                                                   mixer_rows=D // N_CHIPS if i % 2 == 0 else None)
        gs["norm_ffn"][i] = dg2[0]
        buf = mm_tn_into(buf, a, dh2b, D, lay["down"], False)
        buf = mm_tn_into(buf, s["hn2"], du, D, lay["up"], True)
        buf = mm_tn_into(buf, s["y"], dh1b, s["y"].shape[1] // N_CHIPS, lay["out"], False)
        g1 = swapped(dh1, row(norm_mix[i]))
        if i % 2 == 0:
            split = i in SPLIT_LAYERS
            dq, dk, dv = flash_bwd(s["q"], s["k"], s["v"], s["y"], do[0], s["lse"], seq,
                                   after=swap(i, "main", buf) if split else dh1b)
            g1 = swapped(dq, g1)
            (dh, hn1, cq, ckv, dqp, dkvp, dlat, dg1, dgq, dgkv, dgqn, dgqr, dgkn, dgkr) = mla_pre_bwd(
                dq, dk, dv, dh1, s["h"], g1, *s["mla_args"][1:])
            gs["norm_mix"][i] = dg1[0]
            gs["mla_q_lora_g"][j] = dgq[0]
            gs["mla_kv_lora_g"][j] = dgkv[0]
            gs["mla_q_nope_g"][j] = dgqn[0]
            gs["mla_q_rope_g"][j] = dgqr[0, :DR]
            gs["mla_k_nope_g"][j] = dgkn[0]
            gs["mla_k_rope_g"][j] = dgkr[0, :DR]
            small = [mm_tn(hn1, dlat)[:, :LAT].reshape(N_CHIPS, -1, D), _col_shards(_merge_uq(mm_tn(cq, dqp))),
                     _col_shards(mm_tn(ckv, dkvp)), dproj]
            if split:
                buf = jnp.concatenate(small, axis=1)
            else:
                buf = put(buf, lay["wdn"], jnp.concatenate(small, axis=1))
            key = "odd" if split else "main"
        else:
            dh, hn1, dpre, dws, dbs, dlng, dlnb, dg1 = gmlp_bwd(
                dh1, dh1b, s["h"], s["pre"], g1, aw, lay, s["ln"][0], s["ln"][1], wm[j], wmt[j], bfull[j], tril)
            gs["norm_mix"][i] = dg1[0]
            gs["gmlp_ln_g"][j] = dlng[0]
            gs["gmlp_ln_b"][j] = dlnb[0]
            gs["gmlp_w_s"][j] = dws
            gs["gmlp_b_s"][j] = jnp.sum(dbs.reshape(GC, GG, GD), axis=-1).T
            buf = mm_tn_into(buf, hn1, dpre, D, lay["in"], True)
            key = "main"
        token = swap(i, key, buf)
        if i == 1:
            small_gmlp = small_exchange(_SMALL_GMLP, token[0, 0], "gmlp")
            token = token + small_gmlp[3]
    last = swapped(dh, jnp.zeros((1, 1), F32))
    grad_x = dh.reshape(x.shape)
    small_rest = small_exchange(_SMALL_REST, last[0, 0], "rest", extra=[loss_part[0, 0:1]])

    after = small_rest[3]
    shares = []
    for i, key, own, sems, sums, land in scatters:
        _, got = scatter_wait(sems, sums, land, after, f"{i}{key}")
        sems, src, full, after = share_start(final_sum(place, own, got), f"{i}{key}")
        shares.append((i, key, sems, src, full))
    where = {n: [None] * weights[n].shape[0] for n in _BIG}
    for i, key, sems, src, full in shares:
        _, after = share_wait(sems, src, full, after, f"{i}{key}")
        for n, l, row0 in dict((k, parts) for k, _, parts in _layer_units(i))[key]:
            where[n][l] = (after, row0)
            if weights[n].shape[-1] != D:
                gw[n][l] = after[row0:row0 + weights[n][l].size // D].reshape(weights[n].shape[1:])
    grads = {n: jnp.stack(gw[n]) for n in _BIG if weights[n].shape[-1] != D}

    tot = []
    for names, (sems, src, land, _), tag in ((_SMALL_REST, small_rest, "rest"), (_SMALL_GMLP, small_gmlp, "gmlp")):
        summed = sum_devices(small_wait(sems, src, land, after, tag)[1]).reshape(-1)
        tot.append(summed[:sum(small_size(n) for n in names)])
        if tag == "rest":
            loss = summed[tot[-1].size]
    tot = jnp.concatenate(tot)
    off = 0
    for n, sz in ((n, small_size(n)) for n in _SMALL_REST + _SMALL_GMLP):
        gsum = tot[off:off + sz]
        off += sz
        if n in ("gmlp_ln_g", "gmlp_ln_b"):
            gsum = lax.dynamic_slice_in_dim(gsum.reshape(-1, GH), chip * (GH // N_CHIPS), GH // N_CHIPS, axis=1)
        grads[n] = gsum.reshape(weights[n].shape)

    delta, new_m, new_v = {}, {}, {}
    for n in _BIG:
        if weights[n].shape[-1] == D:
            grads[n], delta[n], new_m[n], new_v[n] = adamw_layers(
                weights[n], args["m_" + n], args["v_" + n], [b for b, _ in where[n]], [r for _, r in where[n]])
            continue
        w2 = weights[n].reshape(-1, weights[n].shape[-1])
        d, mn, vn = adamw(w2, grads[n].reshape(w2.shape), args["m_" + n].reshape(w2.shape),
                          args["v_" + n].reshape(w2.shape))
        delta[n], new_m[n], new_v[n] = (a.reshape(weights[n].shape) for a in (d, mn, vn))
    own_sizes = [weights[n].size for n in _SMALL]
    own_rows = -(-sum(own_sizes) // (8 * D)) * 8
    packed = [_pack_rows([src[n] for n in _SMALL], F32, pad_to=own_rows)
              for src in (weights, grads, {n: args["m_" + n] for n in _SMALL}, {n: args["v_" + n] for n in _SMALL})]
    outs = adamw(*packed)
    off = 0
    for n, sz in zip(_SMALL, own_sizes):
        for dst, o in zip((delta, new_m, new_v), outs):
            dst[n] = o.reshape(-1)[off:off + sz].reshape(weights[n].shape)
        off += sz

    order = ["norm_mix", "norm_ffn", "norm_ple", "mla_w_down", "mla_q_lora_g", "mla_kv_lora_g", "mla_w_uq",
             "mla_w_ukv", "mla_q_nope_g", "mla_q_rope_g", "mla_k_nope_g", "mla_k_rope_g", "mla_w_out", "gmlp_w_in",
             "gmlp_ln_g", "gmlp_ln_b", "gmlp_w_s", "gmlp_b_s", "gmlp_w_out", "ffn_w_up", "ffn_w_down", "ple_w_gate",
             "ple_w_proj"]
    return (loss, grad_x, *[grads[n] for n in order], *[delta[n] for n in order], *[new_m[n] for n in order],
            *[new_v[n] for n in order])
```

```python
import functools

import jax
import jax.numpy as jnp
from jax import lax
from jax.experimental import pallas as pl
from jax.experimental.pallas import tpu as pltpu

F32 = jnp.float32
BF16 = jnp.bfloat16
MESH = pl.DeviceIdType.MESH

D = 1024
HEADS = 8
DN = 128
DR = 64
QL = 384
KVL = 256
LAT = 704
LATP = 768
DFF = 4096
GH = 2048
GC = 128
GG = 8
GD = 256
PLE = 256
EPS = 1e-6
ROPE_BASE = 10000.0
SM_SCALE = (DN + DR) ** -0.5
N_CHIPS = 4
LANES = 128

ADAM_LR = 0.001
ADAM_B1 = 0.9
ADAM_B2 = 0.999
ADAM_EPS = 1e-08
ADAM_WD = 0.01
ADAM_STEP = 10

TM = 256
TMB = 512
TQ = 512
TQ_FWD = 512
FWD_HEADS = 4
BWD_HEADS = 2
SUM_ROWS = 256
VMEM_LIMIT = 56 * 1024 * 1024


def _cp(*sem):
    return pltpu.CompilerParams(dimension_semantics=sem, vmem_limit_bytes=VMEM_LIMIT)


def _dot(a, b):
    return jnp.dot(a, b, preferred_element_type=F32)


def _dot_nt(a, b):
    return lax.dot_general(a, b, (((1,), (1,)), ((), ())), preferred_element_type=F32)


def _dot_tn(a, b):
    return lax.dot_general(a, b, (((0,), (0,)), ((), ())), preferred_element_type=F32)


def _rms(x, g, n):
    r = lax.rsqrt(jnp.sum(x * x, axis=-1, keepdims=True) * (1.0 / n) + EPS)
    xhat = x * r
    return xhat * g, xhat, r


def _rms_bwd(dy, g, xhat, r, n):
    dxhat = dy * g
    return r * (dxhat - xhat * (jnp.sum(dxhat * xhat, axis=-1, keepdims=True) * (1.0 / n)))


def _rope(x, c, s):
    return x * c + (pltpu.roll(x, 32, 1) - pltpu.roll(x, 96, 1)) * s


def _rope_t(dy, c, s):
    w = dy * s
    return dy * c + pltpu.roll(w, 96, 1) - pltpu.roll(w, 32, 1)


def _sigmoid(x):
    return 1.0 / (1.0 + jnp.exp(-x))


_GELU_K = 0.7978845608028654
_GELU_C = 0.044715


def _gelu(x):
    return 0.5 * x * (1.0 + jnp.tanh(_GELU_K * (x + _GELU_C * x * x * x)))


def _gelu_and_grad(x):
    x2 = x * x
    t = jnp.tanh(_GELU_K * (x + _GELU_C * x2 * x))
    half = 0.5 * (1.0 + t)
    return x * half, half + 0.5 * x * (1.0 - t * t) * (_GELU_K * (1.0 + 3.0 * _GELU_C * x2))


def _acc_rows(ref, val):
    ref[...] += jnp.broadcast_to(jnp.sum(val, axis=0, keepdims=True), ref.shape)


def _row(tm, c):
    return pl.BlockSpec((tm, c), lambda i: (i, 0))


def _const(shape):
    nd = len(shape)
    return pl.BlockSpec(shape, lambda i: (0,) * nd, pipeline_mode=pl.Buffered(1))


def _wblk(rows, row0):
    assert row0 % rows == 0, (rows, row0)
    return pl.BlockSpec((N_CHIPS, rows, D), lambda i: (0, row0 // rows, 0), pipeline_mode=pl.Buffered(1))


def _rows_joined(w_ref):
    return w_ref[...].reshape(N_CHIPS * w_ref.shape[1], D)


def _sds(shape, dtype):
    return jax.ShapeDtypeStruct(shape, dtype)


def mixffn_fwd(h, y, allw, lay, g2):
    t, k = y.shape

    def body(h_ref, y_ref, wo_ref, g_ref, wu_ref, wd_ref, h1_ref, h2_ref, hn_ref, r_ref):
        h1 = h_ref[...] + _dot(y_ref[...], _rows_joined(wo_ref))
        h1_ref[...] = h1
        yn, _, _ = _rms(h1, g_ref[...], D)
        hn = yn.astype(BF16)
        hn_ref[...] = hn
        f = jnp.zeros((TMB, D), F32)
        for c in range(N_CHIPS):
            r = jnp.maximum(_dot(hn, wu_ref[c]), 0.0)
            r_ref[:, c * D:(c + 1) * D] = r.astype(BF16)
            f = f + _dot((r * r).astype(BF16), wd_ref[c])
        h2_ref[...] = h1 + f

    return pl.pallas_call(
        body, name="mixffn_fwd", grid=(t // TMB,),
        in_specs=[_row(TMB, D), _row(TMB, k), _wblk(k // N_CHIPS, lay["out"]), _const((1, D)), _wblk(D, lay["up"]),
                  _wblk(D, lay["down"])],
        out_specs=[_row(TMB, D), _row(TMB, D), _row(TMB, D), _row(TMB, DFF)],
        out_shape=[_sds((t, D), F32), _sds((t, D), F32), _sds((t, D), BF16), _sds((t, DFF), BF16)],
        compiler_params=_cp("parallel"),
    )(h, y, allw, g2, allw, allw)


def _layer_rows(tm, c, layer):
    return pl.BlockSpec((None, tm, c), lambda i: (layer, i, 0))


def ple_fwd(h2, p, layer, g3, allw, lay, wp):
    t = h2.shape[0]

    def body(h_ref, p_ref, g_ref, wg_ref, wp_ref, h3_ref, hn_ref):
        x = h_ref[...]
        yn, _, _ = _rms(x, g_ref[...], D)
        hn = yn.astype(BF16)
        hn_ref[...] = hn
        gt = _dot(hn, _rows_joined(wg_ref))
        pp = _dot(p_ref[...].astype(BF16), wp_ref[...])
        h3_ref[...] = x + _sigmoid(gt) * pp

    return pl.pallas_call(
        body, name="ple_fwd", grid=(t // TMB,),
        in_specs=[_row(TMB, D), _layer_rows(TMB, PLE, layer), _const((1, D)), _wblk(D // N_CHIPS, lay["gate"]),
                  _const((PLE, D))],
        out_specs=[_row(TMB, D), _row(TMB, D)],
        out_shape=[_sds((t, D), F32), _sds((t, D), BF16)],
        compiler_params=_cp("parallel"),
    )(h2, p, g3, allw, wp)


def _mla_project(h_ref, g1_ref, wdn_ref, gq_ref, gkv_ref, wuq_ref, wukv_ref):
    x = h_ref[...]
    yn, xhat, rx = _rms(x, g1_ref[...], D)
    hn = yn.astype(BF16)
    lat = _dot(hn, wdn_ref[...])
    cq, cqhat, rq = _rms(lat[:, :QL], gq_ref[...], QL)
    ckv, ckvhat, rkv = _rms(lat[:, QL:QL + KVL], gkv_ref[...], KVL)
    kr_raw = lat[:, QL + KVL:]
    cqb = cq.astype(BF16)
    ckvb = ckv.astype(BF16)
    qp = _dot(cqb, wuq_ref[...])
    kvp = _dot(ckvb, wukv_ref[...])
    return dict(xhat=xhat, rx=rx, hn=hn, cqhat=cqhat, rq=rq, ckvhat=ckvhat, rkv=rkv, kr_raw=kr_raw,
                cqb=cqb, ckvb=ckvb, qp=qp, kvp=kvp)


def mla_pre_fwd(h, g1, wdn, gq, gkv, wuq, wukv, gqn, gqr, gkn, gkr, cos, sin):
    t = h.shape[0]

    def body(h_ref, g1_ref, wdn_ref, gq_ref, gkv_ref, wuq_ref, wukv_ref, gqn_ref, gqr_ref, gkn_ref, gkr_ref,
             c_ref, s_ref, q_ref, k_ref, v_ref):
        m = _mla_project(h_ref, g1_ref, wdn_ref, gq_ref, gkv_ref, wuq_ref, wukv_ref)
        c = c_ref[...]
        s = s_ref[...]
        kr, _, _ = _rms(m["kr_raw"], gkr_ref[...], DR)
        krb = _rope(kr, c, s).astype(BF16)
        for hd in range(HEADS):
            qn, _, _ = _rms(m["qp"][:, hd * DN:(hd + 1) * DN], gqn_ref[...], DN)
            qr, _, _ = _rms(m["qp"][:, D + hd * LANES:D + (hd + 1) * LANES], gqr_ref[...], DR)
            q_ref[hd, :, 0:DN] = (qn * SM_SCALE).astype(BF16)
            q_ref[hd, :, DN:2 * DN] = (_rope(qr, c, s) * SM_SCALE).astype(BF16)
            kn, _, _ = _rms(m["kvp"][:, hd * 2 * DN:hd * 2 * DN + DN], gkn_ref[...], DN)
            k_ref[hd, :, 0:DN] = kn.astype(BF16)
            k_ref[hd, :, DN:2 * DN] = krb
            v_ref[hd] = m["kvp"][:, hd * 2 * DN + DN:(hd + 1) * 2 * DN].astype(BF16)

    hb = lambda w: pl.BlockSpec((HEADS, TM, w), lambda i: (0, i, 0))
    return pl.pallas_call(
        body, name="mla_pre_fwd", grid=(t // TM,),
        in_specs=[_row(TM, D), _const((1, D)), _const((D, LATP)), _const((1, QL)), _const((1, KVL)),
                  _const((QL, 2 * D)), _const((KVL, 2 * D)), _const((1, LANES)), _const((1, LANES)),
                  _const((1, LANES)), _const((1, LANES)), _row(TM, LANES), _row(TM, LANES)],
        out_specs=[hb(2 * DN), hb(2 * DN), hb(DN)],
        out_shape=[_sds((HEADS, t, 2 * DN), BF16), _sds((HEADS, t, 2 * DN), BF16), _sds((HEADS, t, DN), BF16)],
        compiler_params=_cp("parallel"),
    )(h, g1, wdn, gq, gkv, wuq, wukv, gqn, gqr, gkn, gkr, cos, sin)


def _diagonal_mask(n=TQ):
    return lax.broadcasted_iota(jnp.int32, (n, n), 1) <= lax.broadcasted_iota(jnp.int32, (n, n), 0)


def flash_fwd(q, k, v, seq):
    t = q.shape[1]
    nb = t // seq
    tq = TQ_FWD
    nq = seq // tq
    hp = FWD_HEADS

    def body(q_ref, k_ref, v_ref, o_ref, lse_ref):
        qi = pl.program_id(2)
        qs = [q_ref[a] for a in range(hp)]

        def step(j, carry, diagonal=False):
            rows = pl.ds(pl.multiple_of(j * tq, tq), tq)
            out = []
            for a in range(hp):
                m, l, acc = carry[a]
                s = _dot_nt(qs[a], k_ref[a, rows, :])
                if diagonal:
                    s = jnp.where(_diagonal_mask(tq), s, -1e30)
                m_new = jnp.maximum(m, jnp.max(s, axis=-1, keepdims=True))
                p = jnp.exp(s - m_new)
                alpha = jnp.exp(m - m_new)
                l = alpha * l + jnp.sum(p, axis=-1, keepdims=True)
                acc = alpha * acc + _dot(p.astype(BF16), v_ref[a, rows, :])
                out.append((m_new, l, acc))
            return tuple(out)

        one = (jnp.full((tq, 1), -1e30, F32), jnp.zeros((tq, 1), F32), jnp.zeros((tq, DN), F32))
        done = step(qi, lax.fori_loop(0, qi, step, (one,) * hp), diagonal=True)
        for a, (m, l, acc) in enumerate(done):
            o_ref[:, a * DN:(a + 1) * DN] = (acc / l).astype(BF16)
            lse_ref[a] = m + jnp.log(l)

    return pl.pallas_call(
        body, name="flash_fwd", grid=(nb, HEADS // hp, nq),
        in_specs=[pl.BlockSpec((hp, tq, 2 * DN), lambda b, h, i: (h, b * nq + i, 0)),
                  pl.BlockSpec((hp, seq, 2 * DN), lambda b, h, i: (h, b, 0)),
                  pl.BlockSpec((hp, seq, DN), lambda b, h, i: (h, b, 0))],
        out_specs=[pl.BlockSpec((tq, hp * DN), lambda b, h, i: (b * nq + i, h)),
                   pl.BlockSpec((hp, tq, 1), lambda b, h, i: (h, b * nq + i, 0))],
        out_shape=[_sds((t, HEADS * DN), BF16), _sds((HEADS, t, 1), F32)],
        compiler_params=_cp("parallel", "parallel", "arbitrary"),
    )(q, k, v)


def _gmlp_in(hn, win_ref):
    pre = [_dot(hn, win_ref[c]) for c in range(N_CHIPS)]
    return jnp.concatenate(pre[:2], axis=1), jnp.concatenate(pre[2:], axis=1)


def gmlp_fwd(h, g1, allw, lay, lng, lnb, wm, bfull):
    t = h.shape[0]

    def body(h_ref, g1_ref, win_ref, lng_ref, lnb_ref, wm_ref, b_ref, y_ref, pre_ref):
        yn, _, _ = _rms(h_ref[...], g1_ref[...], D)
        pre_u, pre_v = _gmlp_in(yn.astype(BF16), win_ref)
        pre_ref[:, :GH] = pre_u.astype(BF16)
        pre_ref[:, GH:] = pre_v.astype(BF16)
        u = _gelu(pre_u)
        v = _gelu(pre_v)
        xc = v - jnp.mean(v, axis=-1, keepdims=True)
        rs = lax.rsqrt(jnp.mean(xc * xc, axis=-1, keepdims=True) + EPS)
        vnb = (xc * rs * lng_ref[...] + lnb_ref[...]).astype(BF16)
        for ch in range(TM // GC):
            rows = slice(ch * GC, (ch + 1) * GC)
            for g in range(GG):
                cols = slice(g * GD, (g + 1) * GD)
                sv = _dot(wm_ref[g], vnb[rows, cols]) + b_ref[:, cols]
                y_ref[rows, cols] = (u[rows, cols] * sv).astype(BF16)

    return pl.pallas_call(
        body, name="gmlp_fwd", grid=(t // TM,),
        in_specs=[_row(TM, D), _const((1, D)), _wblk(D, lay["in"]), _const((1, GH)), _const((1, GH)),
                  _const((GG, GC, GC)), _const((GC, GH))],
        out_specs=[_row(TM, GH), _row(TM, 2 * GH)],
        out_shape=[_sds((t, GH), BF16), _sds((t, 2 * GH), BF16)],
        compiler_params=_cp("parallel"),
    )(h, g1, allw, lng, lnb, wm, bfull)


def loss_head(h, tgt):
    t = h.shape[0]

    def body(h_ref, t_ref, dh_ref, loss_ref):
        @pl.when(pl.program_id(0) == 0)
        def _():
            loss_ref[...] = jnp.zeros_like(loss_ref)

        e = h_ref[...] - t_ref[...]
        dh_ref[...] = e * (1.0 / D)
        part = jnp.sum(jnp.sum(e * e, axis=-1, keepdims=True), axis=0, keepdims=True) * (0.5 / D)
        loss_ref[...] += jnp.broadcast_to(part, loss_ref.shape)

    return pl.pallas_call(
        body, name="loss_head", grid=(t // TMB,),
        in_specs=[_row(TMB, D), _row(TMB, D)],
        out_specs=[_row(TMB, D), _const((8, LANES))],
        out_shape=[_sds((t, D), F32), _sds((8, LANES), F32)],
        compiler_params=_cp("arbitrary"),
    )(h, tgt)


def _zero_at_first_step(*refs):
    @pl.when(pl.program_id(0) == 0)
    def _():
        for r in refs:
            r[...] = jnp.zeros_like(r)


def ple_bwd(dh3, h2, p, layer, g3, allw, lay, wp):
    t = h2.shape[0]

    def body(dh_ref, h_ref, p_ref, g_ref, wg_ref, wp_ref, dh2_ref, dgt_ref, dpp_ref, dg_ref):
        _zero_at_first_step(dg_ref)
        dh3v = dh_ref[...]
        x = h_ref[...]
        g = g_ref[...]
        wg = _rows_joined(wg_ref)
        yn, xhat, r = _rms(x, g, D)
        gt = _dot(yn.astype(BF16), wg)
        pp = _dot(p_ref[...].astype(BF16), wp_ref[...])
        sg = _sigmoid(gt)
        dgt = (dh3v * pp * sg * (1.0 - sg)).astype(BF16)
        dgt_ref[...] = dgt
        dpp_ref[...] = (dh3v * sg).astype(BF16)
        dhn = _dot_nt(dgt, wg)
        _acc_rows(dg_ref, dhn * xhat)
        dh2_ref[...] = dh3v + _rms_bwd(dhn, g, xhat, r, D)

    return pl.pallas_call(
        body, name="ple_bwd", grid=(t // TMB,),
        in_specs=[_row(TMB, D), _row(TMB, D), _layer_rows(TMB, PLE, layer), _const((1, D)),
                  _wblk(D // N_CHIPS, lay["gate"]),
                  _const((PLE, D))],
        out_specs=[_row(TMB, D), _row(TMB, D), _row(TMB, D), _const((8, D))],
        out_shape=[_sds((t, D), F32), _sds((t, D), BF16), _sds((t, D), BF16), _sds((8, D), F32)],
        compiler_params=_cp("arbitrary"),
    )(dh3, h2, p, g3, allw, wp)


def ffn_bwd(dh2, h1, r, g2, allw, lay, mixer_rows=None):
    t = h1.shape[0]

    def body(dh_ref, h_ref, r_ref, g_ref, wu_ref, wd_ref, *rest):
        wo_ref = rest[0] if mixer_rows else None
        dh1_ref, dh1b_ref, du_ref, a_ref, dg_ref, dhb_ref = rest[1:7] if mixer_rows else rest[:6]
        _zero_at_first_step(dg_ref)
        dhb = dh_ref[...].astype(BF16)
        dhb_ref[...] = dhb
        g = g_ref[...]
        _, xhat, rr = _rms(h_ref[...], g, D)
        dhn = jnp.zeros((TM, D), F32)
        for c in range(N_CHIPS):
            cs = slice(c * D, (c + 1) * D)
            rc = r_ref[:, cs].astype(F32)
            a_ref[:, cs] = (rc * rc).astype(BF16)
            da = _dot_nt(dhb, wd_ref[c])
            du = (da * (2.0 * rc)).astype(BF16)
            du_ref[:, cs] = du
            dhn = dhn + _dot_nt(du, wu_ref[c])
        _acc_rows(dg_ref, dhn * xhat)
        dh1 = dh_ref[...] + _rms_bwd(dhn, g, xhat, rr, D)
        dh1_ref[...] = dh1
        dh1b = dh1.astype(BF16)
        dh1b_ref[...] = dh1b
        if mixer_rows:
            rest[7][...] = _dot_nt(dh1b, _rows_joined(wo_ref)).astype(BF16)

    k = N_CHIPS * mixer_rows if mixer_rows else 0
    return pl.pallas_call(
        body, name="ffn_bwd", grid=(t // TM,),
        in_specs=[_row(TM, D), _row(TM, D), _row(TM, DFF), _const((1, D)), _wblk(D, lay["up"]),
                  _wblk(D, lay["down"])] + ([_wblk(mixer_rows, lay["out"])] if mixer_rows else []),
        out_specs=[_row(TM, D), _row(TM, D), _row(TM, DFF), _row(TM, DFF), _const((8, D)), _row(TM, D)]
        + ([_row(TM, k)] if mixer_rows else []),
        out_shape=[_sds((t, D), F32), _sds((t, D), BF16), _sds((t, DFF), BF16), _sds((t, DFF), BF16),
                   _sds((8, D), F32), _sds((t, D), BF16)] + ([_sds((t, k), BF16)] if mixer_rows else []),
        compiler_params=_cp("arbitrary"),
    )(dh2, h1, r, g2, allw, allw, *([allw] if mixer_rows else []))


def flash_bwd(q, k, v, o, do, lse, seq, after):
    t = q.shape[1]
    nb = t // seq
    nq = seq // TQ
    hp = BWD_HEADS

    def body(q_ref, k_ref, v_ref, o_ref, do_ref, lse_ref, after_ref, dq_ref, dk_ref, dv_ref):
        del after_ref
        kj = pl.program_id(2)

        @pl.when(kj == 0)
        def _():
            dq_ref[...] = jnp.zeros_like(dq_ref)

        def step(i, carry, diagonal=False):
            rows = pl.ds(pl.multiple_of(i * TQ, TQ), TQ)
            out = []
            for a in range(hp):
                dk, dv = carry[a]
                kv = k_ref[a]
                qv = q_ref[a, rows, :]
                dov = do_ref[rows, a * DN:(a + 1) * DN]
                ov = o_ref[rows, a * DN:(a + 1) * DN]
                delta = jnp.sum(dov.astype(F32) * ov.astype(F32), axis=-1, keepdims=True)
                s = _dot_nt(qv, kv)
                if diagonal:
                    s = jnp.where(_diagonal_mask(), s, -1e30)
                p = jnp.exp(s - lse_ref[a, rows, :])
                dp = _dot_nt(dov, v_ref[a])
                ds = (p * (dp - delta)).astype(BF16)
                dv = dv + _dot_tn(p.astype(BF16), dov)
                dk = dk + _dot_tn(ds, qv)
                dq_ref[a, rows, :] += _dot(ds, kv)
                out.append((dk, dv))
            return tuple(out)

        one = (jnp.zeros((TQ, 2 * DN), F32), jnp.zeros((TQ, DN), F32))
        done = lax.fori_loop(kj + 1, nq, step, step(kj, (one,) * hp, diagonal=True))
        for a, (dk, dv) in enumerate(done):
            dk_ref[a] = dk
            dv_ref[a] = dv

    return pl.pallas_call(
        body, name="flash_bwd", grid=(nb, HEADS // hp, nq),
        in_specs=[pl.BlockSpec((hp, seq, 2 * DN), lambda b, h, j: (h, b, 0)),
                  pl.BlockSpec((hp, TQ, 2 * DN), lambda b, h, j: (h, b * nq + j, 0)),
                  pl.BlockSpec((hp, TQ, DN), lambda b, h, j: (h, b * nq + j, 0)),
                  pl.BlockSpec((seq, hp * DN), lambda b, h, j: (b, h)),
                  pl.BlockSpec((seq, hp * DN), lambda b, h, j: (b, h)),
                  pl.BlockSpec((hp, seq, 1), lambda b, h, j: (h, b, 0)), _ANY],
        out_specs=[pl.BlockSpec((hp, seq, 2 * DN), lambda b, h, j: (h, b, 0)),
                   pl.BlockSpec((hp, TQ, 2 * DN), lambda b, h, j: (h, b * nq + j, 0)),
                   pl.BlockSpec((hp, TQ, DN), lambda b, h, j: (h, b * nq + j, 0))],
        out_shape=[_sds((HEADS, t, 2 * DN), F32), _sds((HEADS, t, 2 * DN), F32), _sds((HEADS, t, DN), F32)],
        compiler_params=_cp("parallel", "parallel", "arbitrary"),
    )(q, k, v, o, do, lse, after)


def mla_pre_bwd(dq, dk, dv, dh1, h, g1, wdn, gq, gkv, wuq, wukv, gqn, gqr, gkn, gkr, cos, sin):
    t = h.shape[0]

    def body(dq_ref, dk_ref, dv_ref, dh1_ref, h_ref, g1_ref, wdn_ref, gq_ref, gkv_ref, wuq_ref, wukv_ref,
             gqn_ref, gqr_ref, gkn_ref, gkr_ref, c_ref, s_ref,
             dh_ref, hn_ref, cq_ref, ckv_ref, dqp_ref, dkvp_ref, dlat_ref,
             dg1_ref, dgq_ref, dgkv_ref, dgqn_ref, dgqr_ref, dgkn_ref, dgkr_ref):
        _zero_at_first_step(dg1_ref, dgq_ref, dgkv_ref, dgqn_ref, dgqr_ref, dgkn_ref, dgkr_ref)
        m = _mla_project(h_ref, g1_ref, wdn_ref, gq_ref, gkv_ref, wuq_ref, wukv_ref)
        hn_ref[...] = m["hn"]
        cq_ref[...] = m["cqb"]
        ckv_ref[...] = m["ckvb"]
        c = c_ref[...]
        s = s_ref[...]
        gqn = gqn_ref[...]
        gqr = gqr_ref[...]
        gkn = gkn_ref[...]
        gkr = gkr_ref[...]

        dkr = dk_ref[0, :, DN:2 * DN]
        for hd in range(1, HEADS):
            dkr = dkr + dk_ref[hd, :, DN:2 * DN]
        dkr = _rope_t(dkr, c, s)
        _, krhat, rkr = _rms(m["kr_raw"], gkr, DR)
        _acc_rows(dgkr_ref, dkr * krhat)
        dkr_raw = _rms_bwd(dkr, gkr, krhat, rkr, DR)

        for hd in range(HEADS):
            ncols = slice(hd * DN, (hd + 1) * DN)
            _, xh, r = _rms(m["qp"][:, ncols], gqn, DN)
            dqn = dq_ref[hd, :, 0:DN] * SM_SCALE
            _acc_rows(dgqn_ref, dqn * xh)
            dqp_ref[:, ncols] = _rms_bwd(dqn, gqn, xh, r, DN).astype(BF16)

            rcols = slice(D + hd * LANES, D + (hd + 1) * LANES)
            _, xh, r = _rms(m["qp"][:, rcols], gqr, DR)
            dqr = _rope_t(dq_ref[hd, :, DN:2 * DN] * SM_SCALE, c, s)
            _acc_rows(dgqr_ref, dqr * xh)
            dqp_ref[:, rcols] = _rms_bwd(dqr, gqr, xh, r, DR).astype(BF16)

            kcols = slice(hd * 2 * DN, hd * 2 * DN + DN)
            _, xh, r = _rms(m["kvp"][:, kcols], gkn, DN)
            dkn = dk_ref[hd, :, 0:DN]
            _acc_rows(dgkn_ref, dkn * xh)
            dkvp_ref[:, kcols] = _rms_bwd(dkn, gkn, xh, r, DN).astype(BF16)
            dkvp_ref[:, hd * 2 * DN + DN:(hd + 1) * 2 * DN] = dv_ref[hd].astype(BF16)

        dcq = _dot_nt(dqp_ref[...], wuq_ref[...])
        _acc_rows(dgq_ref, dcq * m["cqhat"])
        dlat_q = _rms_bwd(dcq, gq_ref[...], m["cqhat"], m["rq"], QL)
        dckv = _dot_nt(dkvp_ref[...], wukv_ref[...])
        _acc_rows(dgkv_ref, dckv * m["ckvhat"])
        dlat_kv = _rms_bwd(dckv, gkv_ref[...], m["ckvhat"], m["rkv"], KVL)
        dlat = jnp.concatenate([dlat_q, dlat_kv, dkr_raw], axis=1).astype(BF16)
        dlat_ref[...] = dlat
        dhn = _dot_nt(dlat, wdn_ref[...])
        _acc_rows(dg1_ref, dhn * m["xhat"])
        dh_ref[...] = dh1_ref[...] + _rms_bwd(dhn, g1_ref[...], m["xhat"], m["rx"], D)

    hb = lambda w: pl.BlockSpec((HEADS, TM, w), lambda i: (0, i, 0))
    return pl.pallas_call(
        body, name="mla_pre_bwd", grid=(t // TM,),
        in_specs=[hb(2 * DN), hb(2 * DN), hb(DN), _row(TM, D), _row(TM, D), _const((1, D)), _const((D, LATP)),
                  _const((1, QL)), _const((1, KVL)), _const((QL, 2 * D)), _const((KVL, 2 * D)),
                  _const((1, LANES)), _const((1, LANES)), _const((1, LANES)), _const((1, LANES)),
                  _row(TM, LANES), _row(TM, LANES)],
        out_specs=[_row(TM, D), _row(TM, D), _row(TM, QL), _row(TM, KVL), _row(TM, 2 * D), _row(TM, 2 * D),
                   _row(TM, LATP), _const((8, D)), _const((8, QL)), _const((8, KVL)), _const((8, LANES)),
                   _const((8, LANES)), _const((8, LANES)), _const((8, LANES))],
        out_shape=[_sds((t, D), F32), _sds((t, D), BF16), _sds((t, QL), BF16), _sds((t, KVL), BF16),
                   _sds((t, 2 * D), BF16), _sds((t, 2 * D), BF16), _sds((t, LATP), BF16),
                   _sds((8, D), F32), _sds((8, QL), F32), _sds((8, KVL), F32), _sds((8, LANES), F32),
                   _sds((8, LANES), F32), _sds((8, LANES), F32), _sds((8, LANES), F32)],
        compiler_params=_cp("arbitrary"),
    )(dq, dk, dv, dh1, h, g1, wdn, gq, gkv, wuq, wukv, gqn, gqr, gkn, gkr, cos, sin)


def gmlp_bwd(dh1, dh1b, h, pre, g1, allw, lay, lng, lnb, wm, wmt, bfull, tril):
    t = h.shape[0]

    def body(dh1_ref, dh1b_ref, h_ref, pre_ref, g1_ref, win_ref, lng_ref, lnb_ref, wm_ref, wmt_ref, b_ref,
             wout_ref, tril_ref, dh_ref, hn_ref, dpre_ref, dws_ref, dbs_ref, dlng_ref, dlnb_ref, dg1_ref,
             dvn_s):
        _zero_at_first_step(dws_ref, dbs_ref, dlng_ref, dlnb_ref, dg1_ref)
        g1 = g1_ref[...]
        yn, xhat, rx = _rms(h_ref[...], g1, D)
        hn_ref[...] = yn.astype(BF16)
        dy = _dot_nt(dh1b_ref[...], _rows_joined(wout_ref))
        pre_u = pre_ref[:, :GH].astype(F32)
        pre_v = pre_ref[:, GH:].astype(F32)
        u, gg_u = _gelu_and_grad(pre_u)
        v, gg_v = _gelu_and_grad(pre_v)
        xc = v - jnp.mean(v, axis=-1, keepdims=True)
        rs = lax.rsqrt(jnp.mean(xc * xc, axis=-1, keepdims=True) + EPS)
        vhat = xc * rs
        lng = lng_ref[...]
        vnb = (vhat * lng + lnb_ref[...]).astype(BF16)
        dsv = dy * u
        dsvb = dsv.astype(BF16)
        tril_m = tril_ref[...]
        for ch in range(TM // GC):
            rows = slice(ch * GC, (ch + 1) * GC)
            dbs_ref[...] += dsv[rows, :]
            for g in range(GG):
                cols = slice(g * GD, (g + 1) * GD)
                sv = _dot(wm_ref[g], vnb[rows, cols]) + b_ref[:, cols]
                dpre_ref[rows, cols] = (dy[rows, cols] * sv * gg_u[rows, cols]).astype(BF16)
                dvn_s[rows, cols] = _dot(wmt_ref[g], dsvb[rows, cols])
                dws_ref[g] += _dot_nt(dsvb[rows, cols], vnb[rows, cols]) * tril_m
        dvn = dvn_s[...]
        _acc_rows(dlng_ref, dvn * vhat)
        _acc_rows(dlnb_ref, dvn)
        dvhat = dvn * lng
        dv = rs * (dvhat - jnp.mean(dvhat, axis=-1, keepdims=True)
                   - vhat * jnp.mean(dvhat * vhat, axis=-1, keepdims=True))
        dpre_v = (dv * gg_v).astype(BF16)
        dpre_ref[:, GH:] = dpre_v
        dhn = _dot_nt(dpre_ref[:, 0:D], win_ref[0])
        for c in range(1, N_CHIPS):
            dhn = dhn + _dot_nt(dpre_ref[:, c * D:(c + 1) * D], win_ref[c])
        _acc_rows(dg1_ref, dhn * xhat)
        dh_ref[...] = dh1_ref[...] + _rms_bwd(dhn, g1, xhat, rx, D)

    return pl.pallas_call(
        body, name="gmlp_bwd", grid=(t // TM,),
        in_specs=[_row(TM, D), _row(TM, D), _row(TM, D), _row(TM, 2 * GH), _const((1, D)), _wblk(D, lay["in"]),
                  _const((1, GH)), _const((1, GH)), _const((GG, GC, GC)), _const((GG, GC, GC)), _const((GC, GH)),
                  _wblk(GH // N_CHIPS, lay["out"]), _const((GC, GC))],
        out_specs=[_row(TM, D), _row(TM, D), _row(TM, 2 * GH), _const((GG, GC, GC)), _const((GC, GH)),
                   _const((8, GH)), _const((8, GH)), _const((8, D))],
        out_shape=[_sds((t, D), F32), _sds((t, D), BF16), _sds((t, 2 * GH), BF16), _sds((GG, GC, GC), F32),
                   _sds((GC, GH), F32), _sds((8, GH), F32), _sds((8, GH), F32), _sds((8, D), F32)],
        scratch_shapes=[pltpu.VMEM((TM, GH), F32)],
        compiler_params=_cp("arbitrary"),
    )(dh1, dh1b, h, pre, g1, allw, lng, lnb, wm, wmt, bfull, allw, tril)


def _token_step(t):
    return next(s for s in (2048, 1024, 512) if t % s == 0)


def mm_tn(a, b, layer=None):
    t, k = a.shape[-2:]
    n = b.shape[1]
    tk = min(k, 1024)
    tn = min(n, 1024)
    tt = _token_step(t)
    a_spec = (pl.BlockSpec((tt, tk), lambda i, j, s: (s, i)) if layer is None else
              pl.BlockSpec((None, tt, tk), lambda i, j, s: (layer, s, i)))

    def body(a_ref, b_ref, o_ref):
        @pl.when(pl.program_id(2) == 0)
        def _():
            o_ref[...] = jnp.zeros_like(o_ref)

        o_ref[...] += _dot_tn(a_ref[...].astype(BF16), b_ref[...].astype(BF16))

    return pl.pallas_call(
        body, name="mm_tn", grid=(k // tk, n // tn, t // tt),
        in_specs=[a_spec, pl.BlockSpec((tt, tn), lambda i, j, s: (s, j))],
        out_specs=pl.BlockSpec((tk, tn), lambda i, j, s: (i, j)), out_shape=_sds((k, n), F32),
        compiler_params=_cp("parallel", "parallel", "arbitrary"),
    )(a, b)


def mm_tn_into(buf, a, b, rows, row0, col_sharded):
    t = a.shape[0]
    tt = _token_step(t)
    assert row0 % rows == 0 and a.shape[1] == (rows if col_sharded else N_CHIPS * rows), (rows, row0, a.shape)
    assert b.shape[1] == (N_CHIPS * D if col_sharded else D), b.shape
    joint = not col_sharded and N_CHIPS * rows <= 2048
    grid = (1, 1, t // tt) if joint else (1, N_CHIPS, t // tt) if col_sharded else (N_CHIPS, 1, t // tt)
    fresh = isinstance(buf, int)

    def body(*refs):
        a_ref, b_ref, o_ref = refs[-3:]

        @pl.when(pl.program_id(2) == 0)
        def _():
            o_ref[...] = jnp.zeros_like(o_ref)

        o_ref[...] += _dot_tn(a_ref[...].astype(BF16), b_ref[...].astype(BF16)).reshape(o_ref.shape)

    specs = [pl.BlockSpec((tt, N_CHIPS * rows if joint else rows), lambda i, j, s: (s, i)),
             pl.BlockSpec((tt, D), lambda i, j, s: (s, j))]
    return pl.pallas_call(
        body, name="mm_tn_into", grid=grid,
        in_specs=specs if fresh else [_ANY] + specs,
        out_specs=pl.BlockSpec((N_CHIPS if joint else None, rows, D), lambda i, j, s: (i + j, row0 // rows, 0)),
        out_shape=_sds((N_CHIPS, buf, D) if fresh else buf.shape, F32),
        input_output_aliases={} if fresh else {0: 0},
        compiler_params=_cp("parallel", "parallel", "arbitrary"),
    )(*((a, b) if fresh else (buf, a, b)))


def adamw(w, g, m, v):
    rows, cols = w.shape
    tr = rows if rows <= 512 else next(r for r in (512, 384, 256, 128) if rows % r == 0)
    c1 = 1.0 - ADAM_B1 ** ADAM_STEP
    c2 = 1.0 - ADAM_B2 ** ADAM_STEP

    def body(w_ref, g_ref, m_ref, v_ref, d_ref, mo_ref, vo_ref):
        gv = g_ref[...]
        mn = ADAM_B1 * m_ref[...] + (1.0 - ADAM_B1) * gv
        vn = ADAM_B2 * v_ref[...] + (1.0 - ADAM_B2) * (gv * gv)
        mo_ref[...] = mn
        vo_ref[...] = vn
        d_ref[...] = -ADAM_LR * ((mn / c1) / (jnp.sqrt(vn / c2) + ADAM_EPS) + ADAM_WD * w_ref[...])

    spec = pl.BlockSpec((tr, cols), lambda i: (i, 0))
    return pl.pallas_call(
        body, name="adamw", grid=(rows // tr,),
        in_specs=[spec] * 4, out_specs=[spec] * 3, out_shape=[_sds((rows, cols), F32)] * 3,
        compiler_params=_cp("parallel"),
    )(w, g, m, v)


def adamw_layers(w, m, v, bufs, row0s):
    nl, a, _ = w.shape
    tr = min(a, 256)
    c1 = 1.0 - ADAM_B1 ** ADAM_STEP
    c2 = 1.0 - ADAM_B2 ** ADAM_STEP
    assert all(r % tr == 0 for r in row0s) and a % tr == 0, (row0s, a)

    def body(w_ref, m_ref, v_ref, *rest):
        g_refs, (g_ref, d_ref, mo_ref, vo_ref) = rest[:nl], rest[nl:]
        for l in range(nl):
            @pl.when(pl.program_id(0) == l)
            def _(l=l):
                gv = g_refs[l][...]
                g_ref[...] = gv
                mn = ADAM_B1 * m_ref[...] + (1.0 - ADAM_B1) * gv
                vn = ADAM_B2 * v_ref[...] + (1.0 - ADAM_B2) * (gv * gv)
                mo_ref[...] = mn
                vo_ref[...] = vn
                d_ref[...] = -ADAM_LR * ((mn / c1) / (jnp.sqrt(vn / c2) + ADAM_EPS) + ADAM_WD * w_ref[...])

    def rows_of(l, row0):
        return pl.BlockSpec((tr, D), lambda li, i: (jnp.where(li == l, row0 // tr + i, row0 // tr), 0))

    spec = pl.BlockSpec((None, tr, D), lambda li, i: (li, i, 0))
    return pl.pallas_call(
        body, name="adamw_layers", grid=(nl, a // tr),
        in_specs=[spec] * 3 + [rows_of(l, r) for l, r in enumerate(row0s)],
        out_specs=[spec] * 4, out_shape=[_sds(w.shape, F32)] * 4,
        compiler_params=_cp("arbitrary", "arbitrary"),
    )(w, m, v, *bufs)


def _place():
    return lax.axis_index("x"), lax.axis_index("y"), lax.axis_index("c")


def _other_chips(x, y):
    return [(1 - x, y), (x, 1 - y), (1 - x, 1 - y)]


_ANY = pl.BlockSpec(memory_space=pl.ANY)


_HBM = pl.BlockSpec(memory_space=pltpu.HBM)
_SEM = pl.BlockSpec(memory_space=pltpu.SEMAPHORE)
_EFFECT = pltpu.SideEffectType.DATAFLOW_SIDE_EFFECTING
N_ICI = 3


def _exchange_start(name, src, land, copies, n):
    def body(src_ref, land_ref, *outs):
        sems, token = outs[:2 * n], outs[-1]
        for j, (s, d, to) in enumerate(copies(src_ref, land_ref, _place())):
            pltpu.make_async_remote_copy(src_ref=s, dst_ref=d, send_sem=sems[j], recv_sem=sems[n + j],
                                         device_id=to, device_id_type=MESH).start()
        token[...] = jnp.zeros_like(token)

    sem = pltpu.SemaphoreType.DMA(())
    outs = pl.pallas_call(
        body, name=name,
        out_shape=(sem,) * (2 * n) + (pltpu.HBM(src.shape, src.dtype), pltpu.HBM(land.shape, land.dtype),
                                      _sds((8, LANES), F32)),
        in_specs=(_HBM, _HBM),
        out_specs=(_SEM,) * (2 * n) + (_HBM, _HBM, pl.BlockSpec(memory_space=pltpu.VMEM)),
        input_output_aliases={0: 2 * n, 1: 2 * n + 1},
        compiler_params=pltpu.CompilerParams(has_side_effects=_EFFECT),
    )(pltpu.with_memory_space_constraint(src, pltpu.HBM), pltpu.with_memory_space_constraint(land, pltpu.HBM))
    return outs[:2 * n], outs[2 * n], outs[2 * n + 1], outs[-1]


def _exchange_wait(name, sems, src, land, after, arrivals):
    n = len(sems) // 2

    def body(src_ref, land_ref, *rest):
        sems = rest[:2 * n]
        for j, (s, d) in enumerate(arrivals(src_ref, land_ref, _place())):
            cp = pltpu.make_async_remote_copy(src_ref=s, dst_ref=d, send_sem=sems[j], recv_sem=sems[n + j],
                                              device_id=_place(), device_id_type=MESH)
            cp.wait_send()
            cp.wait_recv()

    return pl.pallas_call(
        body, name=name, out_shape=(pltpu.HBM(src.shape, src.dtype), pltpu.HBM(land.shape, land.dtype)),
        in_specs=(_HBM, _HBM) + (_SEM,) * (2 * n) + (_ANY,), out_specs=(_HBM, _HBM),
        input_output_aliases={0: 0, 1: 1},
        compiler_params=pltpu.CompilerParams(has_side_effects=_EFFECT),
    )(src, land, *sems, after)


def _halves(c, hh):
    return pl.ds(pl.multiple_of(c * hh, 16), hh), pl.ds(pl.multiple_of((1 - c) * hh, 16), hh)


def gather_start(land, tag):
    _, rr, _ = land.shape
    assert rr % 32 == 0, rr

    def copies(_, land_ref, place):
        x, y, c = place
        mine = land_ref.at[2 * x + y, _halves(c, rr // 2)[0]]
        return [(mine, mine, (cx, cy, c)) for cx, cy in _other_chips(x, y)]

    return _exchange_start(f"gather_start_{tag}", jnp.zeros((8, LANES), F32), land, copies, N_ICI)


def gather_wait(sems, src, land, after, tag):
    def arrivals(_, land_ref, place):
        x, y, c = place
        half = _halves(c, land.shape[1] // 2)[0]
        return [(land_ref.at[2 * x + y, half], land_ref.at[2 * cx + cy, half]) for cx, cy in _other_chips(x, y)]

    return _exchange_wait(f"gather_wait_{tag}", sems, src, land, after, arrivals)


def pass_start(land, tag):
    def copies(_, land_ref, place):
        x, y, c = place
        half = _halves(c, land.shape[1] // 2)[0]
        return [(land_ref.at[2 * cx + cy, half], land_ref.at[2 * cx + cy, half], (x, y, 1 - c))
                for cx, cy in _other_chips(x, y)]

    return _exchange_start(f"pass_start_{tag}", jnp.zeros((8, LANES), F32), land, copies, N_ICI)


def pass_wait(sems, src, land, after, tag):
    def arrivals(_, land_ref, place):
        x, y, c = place
        mine, other = _halves(c, land.shape[1] // 2)
        return [(land_ref.at[2 * cx + cy, mine], land_ref.at[2 * cx + cy, other]) for cx, cy in _other_chips(x, y)]

    return _exchange_wait(f"pass_wait_{tag}", sems, src, land, after, arrivals)


def swap_start(g, tag):
    _, rr, cc = g.shape

    def copies(g_ref, got_ref, place):
        x, y, c = place
        other = _halves(c, rr // 2)[1]
        return [(g_ref.at[k, other], got_ref.at[k], (x, y, 1 - c)) for k in range(N_CHIPS)]

    return _exchange_start(f"swap_start_{tag}", g, lax.empty((N_CHIPS, rr // 2, cc), g.dtype), copies, N_CHIPS)


def swap_wait(sems, g, got, after, tag):
    def arrivals(g_ref, got_ref, place):
        other = _halves(place[2], g.shape[1] // 2)[1]
        return [(g_ref.at[k, other], got_ref.at[k]) for k in range(N_CHIPS)]

    return _exchange_wait(f"swap_wait_{tag}", sems, g, got, after, arrivals)


def chip_sum(place, g32, got):
    _, rr, cc = g32.shape
    hh = rr // 2
    tr = SUM_ROWS
    assert rr % 2 == 0 and hh % tr == 0, (rr, tr)
    nb = hh // tr

    def body(place_ref, g_ref, got_ref, own_ref, all_ref):
        s = g_ref[...] + got_ref[...].astype(F32)
        all_ref[...] = s.astype(BF16)
        own_ref[...] = g_ref[place_ref[1]] + got_ref[place_ref[1]].astype(F32)

    return pl.pallas_call(
        body, name="chip_sum",
        grid_spec=pltpu.PrefetchScalarGridSpec(
            num_scalar_prefetch=1, grid=(nb,),
            in_specs=[pl.BlockSpec((N_CHIPS, tr, cc), lambda i, pr: (0, pr[0] * nb + i, 0)),
                      pl.BlockSpec((N_CHIPS, tr, cc), lambda i, pr: (0, i, 0))],
            out_specs=[pl.BlockSpec((tr, cc), lambda i, pr: (i, 0)),
                       pl.BlockSpec((N_CHIPS, tr, cc), lambda i, pr: (0, i, 0))]),
        out_shape=[_sds((hh, cc), F32), _sds((N_CHIPS, hh, cc), BF16)],
        compiler_params=_cp("parallel"),
    )(place, g32, got)


def _scatter_copies(s_ref, land_ref, place):
    x, y, c = place
    return [(s_ref.at[2 * cx + cy], land_ref.at[j], (cx, cy, c)) for j, (cx, cy) in enumerate(_other_chips(x, y))]


def scatter_start(s, tag):
    return _exchange_start(f"scatter_start_{tag}", s, lax.empty((N_ICI,) + s.shape[1:], s.dtype), _scatter_copies, N_ICI)


def scatter_wait(sems, s, land, after, tag):
    return _exchange_wait(f"scatter_wait_{tag}", sems, s, land, after,
                          lambda s_ref, land_ref, place: [(a, b) for a, b, _ in _scatter_copies(s_ref, land_ref, place)])


def final_sum(place, own, got):
    hh, cc = own.shape
    tr = SUM_ROWS
    assert hh % tr == 0, (hh, tr)
    nb = hh // tr

    def body(place_ref, own_ref, got_ref, o_ref):
        del place_ref
        o_ref[...] = ((own_ref[...] + got_ref[0].astype(F32)) + got_ref[1].astype(F32)) + got_ref[2].astype(F32)

    return pl.pallas_call(
        body, name="final_sum",
        grid_spec=pltpu.PrefetchScalarGridSpec(
            num_scalar_prefetch=1, grid=(nb,),
            in_specs=[pl.BlockSpec((tr, cc), lambda i, pr: (i, 0)), pl.BlockSpec((3, tr, cc), lambda i, pr: (0, i, 0))],
            out_specs=pl.BlockSpec((tr, cc), lambda i, pr: (pr[0] * nb + i, 0))),
        out_shape=_sds((2 * hh, cc), F32),
        compiler_params=_cp("parallel"),
    )(place, own, got)


def share_start(f, tag):
    def copies(_, f_ref, place):
        x, y, c = place
        mine = f_ref.at[_halves(c, f.shape[0] // 2)[0]]
        return [(mine, mine, (x, y, 1 - c))]

    return _exchange_start(f"share_start_{tag}", jnp.zeros((8, LANES), F32), f, copies, 1)


def share_wait(sems, src, f, after, tag):
    def arrivals(_, f_ref, place):
        mine, other = _halves(place[2], f.shape[0] // 2)
        return [(f_ref.at[mine], f_ref.at[other])]

    return _exchange_wait(f"share_wait_{tag}", sems, src, f, after, arrivals)


N_DEV = 8


def _peers(place):
    x, y, c = place
    return [((1 - x) if r & 4 else x, (1 - y) if r & 2 else y, (1 - c) if r & 1 else c) for r in range(1, N_DEV)]


def _device_index(place):
    x, y, c = place
    return 4 * x + 2 * y + c


def small_start(land, tag):
    def copies(_, land_ref, place):
        mine = land_ref.at[_device_index(place)]
        return [(mine, mine, to) for to in _peers(place)]

    return _exchange_start(f"small_start_{tag}", jnp.zeros((8, LANES), F32), land, copies, N_DEV - 1)


def small_wait(sems, src, land, after, tag):
    def arrivals(_, land_ref, place):
        return [(land_ref.at[_device_index(place)], land_ref.at[_device_index(peer)]) for peer in _peers(place)]

    return _exchange_wait(f"small_wait_{tag}", sems, src, land, after, arrivals)


def sum_devices(land):
    _, rr, cc = land.shape
    tr = 56
    assert rr % tr == 0, rr

    def body(l_ref, o_ref):
        acc = l_ref[0]
        for d in range(1, N_DEV):
            acc = acc + l_ref[d]
        o_ref[...] = acc

    return pl.pallas_call(
        body, name="sum_devices", grid=(rr // tr,),
        in_specs=[pl.BlockSpec((N_DEV, tr, cc), lambda i: (0, i, 0))],
        out_specs=pl.BlockSpec((tr, cc), lambda i: (i, 0)), out_shape=_sds((rr, cc), F32),
        compiler_params=_cp("parallel"),
    )(land)


_BIG = ["mla_w_down", "mla_w_uq", "mla_w_ukv", "mla_w_out", "gmlp_w_in", "gmlp_w_out", "ffn_w_up", "ffn_w_down",
        "ple_w_gate", "ple_w_proj"]
_SMALL_REST = ["norm_mix", "norm_ffn", "norm_ple", "mla_q_lora_g", "mla_kv_lora_g", "mla_q_nope_g", "mla_q_rope_g",
               "mla_k_nope_g", "mla_k_rope_g"]
_SMALL_GMLP = ["gmlp_ln_g", "gmlp_ln_b", "gmlp_w_s", "gmlp_b_s"]
_SMALL = _SMALL_REST + _SMALL_GMLP

_LAY_MLA = dict(up=0, down=1024, out=2048, gate=2304, wdn=2560, wuq=2736, wukv=2880, proj=3008, rows=3072)
_LAY_MLA_MAIN = dict(up=0, down=1024, out=2048, gate=2304, rows=2560)
_LAY_MLA_ODD = dict(wdn=0, wuq=176, wukv=320, proj=448, rows=512)
_LAY_GMLP = {"up": 0, "down": 1024, "in": 2048, "out": 3072, "gate": 3584, "proj": 3840, "ln": 3904, "rows": 4096}
SPLIT_LAYERS = (0,)


def _layer_units(i):
    j = i // 2
    if i % 2 == 0:
        odd, lay = (_LAY_MLA_ODD, _LAY_MLA_MAIN) if i in SPLIT_LAYERS else (_LAY_MLA, _LAY_MLA)
        small = [("mla_w_down", j, odd["wdn"]), ("mla_w_uq", j, odd["wuq"]), ("mla_w_ukv", j, odd["wukv"]),
                 ("ple_w_proj", i, odd["proj"])]
        large = [("ffn_w_up", i, lay["up"]), ("ffn_w_down", i, lay["down"]), ("mla_w_out", j, lay["out"]),
                 ("ple_w_gate", i, lay["gate"])]
        return [("odd", odd, small), ("main", lay, large)] if i in SPLIT_LAYERS else [("main", lay, large + small)]
    lay = _LAY_GMLP
    return [("main", lay, [("ffn_w_up", i, lay["up"]), ("ffn_w_down", i, lay["down"]), ("gmlp_w_in", j, lay["in"]),
                           ("gmlp_w_out", j, lay["out"]), ("ple_w_gate", i, lay["gate"]),
                           ("ple_w_proj", i, lay["proj"])])]


def _pack_rows(parts, dtype, pad_to=None, slot=False):
    size = sum(p.size for p in parts)
    tail = [] if pad_to is None or pad_to * D == size else [jnp.zeros((pad_to * D - size,), dtype)]
    shape = (1, -1, D) if slot else (-1, D)
    if all(p.size % D == 0 for p in parts + tail):
        return jnp.concatenate([p.astype(dtype).reshape(shape) for p in parts + tail], axis=len(shape) - 2)
    return jnp.concatenate([p.astype(dtype).reshape(-1) for p in parts + tail]).reshape(shape)


def _odd(allw, row0, a, b):
    return allw[:, row0:row0 + a * b // D].reshape(N_CHIPS, a, b)


def _cols_joined(s):
    return jnp.transpose(s, (1, 0, 2)).reshape(s.shape[1], N_CHIPS * s.shape[2])


def _col_shards(full):
    a, bb = full.shape
    return jnp.transpose(full.reshape(a, N_CHIPS, bb // N_CHIPS), (1, 0, 2)).reshape(N_CHIPS, -1, D)


def _pad_lanes(g):
    return jnp.pad(g, ((0, 0), (0, LANES - g.shape[1])))


def _split_uq(wuq):
    l = wuq.shape[0]
    w = wuq.reshape(l, QL, HEADS, DN + DR)
    nope = w[..., :DN].reshape(l, QL, HEADS * DN)
    rope = jnp.pad(w[..., DN:], ((0, 0), (0, 0), (0, 0), (0, LANES - DR))).reshape(l, QL, HEADS * LANES)
    return jnp.concatenate([nope, rope], axis=-1)


def _merge_uq(d):
    nope = d[:, :HEADS * DN].reshape(QL, HEADS, DN)
    rope = d[:, HEADS * DN:].reshape(QL, HEADS, LANES)[..., :DR]
    return jnp.concatenate([nope, rope], axis=-1).reshape(QL, HEADS * (DN + DR))


def _rope_tables(positions):
    inv_freq = ROPE_BASE ** (-(jnp.arange(0, DR, 2, dtype=F32) / DR))
    ang = positions.reshape(-1).astype(F32)[:, None] * inv_freq
    z = jnp.zeros((ang.shape[0], LANES - DR), F32)
    return (jnp.concatenate([jnp.cos(ang), jnp.cos(ang), z], axis=1),
            jnp.concatenate([jnp.sin(ang), jnp.sin(ang), z], axis=1))


def kernel(x, p, positions, norm_mix, norm_ffn, norm_ple, mla_w_down, mla_q_lora_g, mla_kv_lora_g, mla_w_uq, mla_w_ukv, mla_q_nope_g, mla_q_rope_g, mla_k_nope_g, mla_k_rope_g, mla_w_out, gmlp_w_in, gmlp_ln_g, gmlp_ln_b, gmlp_w_s, gmlp_b_s, gmlp_w_out, ffn_w_up, ffn_w_down, ple_w_gate, ple_w_proj, loss_target, m_norm_mix, m_norm_ffn, m_norm_ple, m_mla_w_down, m_mla_q_lora_g, m_mla_kv_lora_g, m_mla_w_uq, m_mla_w_ukv, m_mla_q_nope_g, m_mla_q_rope_g, m_mla_k_nope_g, m_mla_k_rope_g, m_mla_w_out, m_gmlp_w_in, m_gmlp_ln_g, m_gmlp_ln_b, m_gmlp_w_s, m_gmlp_b_s, m_gmlp_w_out, m_ffn_w_up, m_ffn_w_down, m_ple_w_gate, m_ple_w_proj, v_norm_mix, v_norm_ffn, v_norm_ple, v_mla_w_down, v_mla_q_lora_g, v_mla_kv_lora_g, v_mla_w_uq, v_mla_w_ukv, v_mla_q_nope_g, v_mla_q_rope_g, v_mla_k_nope_g, v_mla_k_rope_g, v_mla_w_out, v_gmlp_w_in, v_gmlp_ln_g, v_gmlp_ln_b, v_gmlp_w_s, v_gmlp_b_s, v_gmlp_w_out, v_ffn_w_up, v_ffn_w_down, v_ple_w_gate, v_ple_w_proj):
    args = dict(locals())
    weights = {n: args[n] for n in _BIG + _SMALL}
    depth = norm_mix.shape[0]
    nb, seq, _ = x.shape
    t = nb * seq
    assert seq % TQ == 0 and seq % TM == 0 and t % 512 == 0, (nb, seq)
    cx = lax.axis_index("x")
    cy = lax.axis_index("y")
    cc = lax.axis_index("c")
    chip = 2 * cx + cy

    gathers = {}
    token = None
    for i in range(depth):
        for key, lay, parts in _layer_units(i):
            rows = [weights[n][l] for n, l, _ in parts]
            if token is not None:
                rows[0] = rows[0] + token[0, 0]
            if "ln" in lay:
                ln = jnp.stack([gmlp_ln_g[i // 2], gmlp_ln_b[i // 2]]).astype(F32)
                bits = lax.bitcast_convert_type(ln, BF16).reshape(-1)
                rows.append(jnp.pad(bits, (0, 16 * D - bits.size)).reshape(16, D))
            mine = _pack_rows(rows, BF16, pad_to=lay["rows"], slot=True)
            land = lax.dynamic_update_slice(lax.empty((N_CHIPS, lay["rows"], D), BF16), mine, (chip, 0, 0))
            sems, src, land, token = gather_start(land, f"{i}{key}")
            gathers[i, key] = (sems, src, land)
    allw = [None] * depth

    tril = jnp.tril(jnp.ones((GC, GC), F32))
    wm = (gmlp_w_s * tril).astype(BF16)
    wmt = jnp.swapaxes(wm, -1, -2)
    bfull = jnp.repeat(jnp.swapaxes(gmlp_b_s, -1, -2), GD, axis=-1)
    cos, sin = _rope_tables(positions)
    row = lambda g: g.reshape(1, -1)
    gqr = _pad_lanes(mla_q_rope_g)
    gkr = _pad_lanes(mla_k_rope_g)

    h = x.reshape(t, D)
    pt = p.reshape(depth, t, PLE)
    saved = []

    passing = {}

    def arrive(i, key, after):
        sems, src, land = gathers[i, key]
        _, land = gather_wait(sems, src, land, after, f"{i}{key}")
        passing[i, key] = pass_start(land, f"{i}{key}")
        return passing[i, key][3]

    def needed(i, key, after=None):
        sems, src, land, tok = passing.pop((i, key))
        return pass_wait(sems, src, land, tok if after is None else after, f"{i}{key}")[1]

    arrive(0, _layer_units(0)[0][0], token)
    for i in range(depth):
        j = i // 2
        lay = _layer_units(i)[-1][1]
        s = dict(h=h)
        if i % 2 == 0:
            split = i in SPLIT_LAYERS
            olay = _layer_units(i)[0][1]
            odd = needed(i, "odd" if split else "main", None if i == 0 else h)
            wdn = jnp.pad(_odd(odd, olay["wdn"], D // N_CHIPS, LAT).reshape(D, LAT), ((0, 0), (0, LATP - LAT)))
            wuq = _split_uq(_cols_joined(_odd(odd, olay["wuq"], QL, 384))[None])[0]
            wukv = _cols_joined(_odd(odd, olay["wukv"], KVL, 512))
            wp = _cols_joined(_odd(odd, olay["proj"], PLE, 256))
            mla_args = (row(norm_mix[i]), wdn, row(mla_q_lora_g[j]), row(mla_kv_lora_g[j]), wuq, wukv,
                        row(mla_q_nope_g[j]), gqr[j:j + 1], row(mla_k_nope_g[j]), gkr[j:j + 1], cos, sin)
            q, k, v = mla_pre_fwd(h, *mla_args)
            y, lse = flash_fwd(q, k, v, seq)
            if split and i == 0:
                arrive(i, "main", y)
            aw = needed(i, "main", y) if split else odd
            s.update(q=q, k=k, v=v, lse=lse, mla_args=mla_args)
        else:
            aw = needed(i, "main", h)
            ln = lax.bitcast_convert_type(aw[:, lay["ln"]:lay["ln"] + 2].reshape(N_CHIPS, 2, GH // N_CHIPS, 2), F32)
            ln = jnp.transpose(ln, (1, 0, 2)).reshape(2, 1, GH)
            wp = _cols_joined(_odd(aw, lay["proj"], PLE, 256))
            y, pre = gmlp_fwd(h, row(norm_mix[i]), aw, lay, ln[0], ln[1], wm[j], bfull[j])
            s.update(pre=pre, ln=ln)
        allw[i] = aw
        g2 = row(norm_ffn[i])
        if i + 1 < depth:
            for key, _, _ in _layer_units(i + 1):
                g2 = g2 + arrive(i + 1, key, y)[0:1, 0:1]
        h1, h2, hn2, r = mixffn_fwd(h, y, aw, lay, g2)
        h, hn3 = ple_fwd(h2, pt, i, row(norm_ple[i]), aw, lay, wp)
        s.update(y=y, wp=wp, h1=h1, h2=h2, hn2=hn2, r=r, hn3=hn3)
        saved.append(s)

    dh, loss_part = loss_head(h, loss_target.reshape(t, D))

    gs = {n: [None] * weights[n].shape[0] for n in _SMALL}
    gw = {n: [None] * weights[n].shape[0] for n in _BIG}
    place = jnp.stack([cc, chip]).astype(jnp.int32)
    scatters = []
    swaps = []
    token = None

    def put(b, row0, shards):
        return lax.dynamic_update_slice(b, shards.reshape(N_CHIPS, -1, D), (0, row0, 0))

    def small_size(n):
        return weights[n].shape[0] * GH if n in ("gmlp_ln_g", "gmlp_ln_b") else weights[n].size

    def small_exchange(names, zero, tag, extra=()):
        rows = -(-(sum(small_size(n) for n in names) + sum(e.size for e in extra)) // (56 * D)) * 56
        part = [jnp.stack(gs[n]) for n in names] + list(extra)
        part = _pack_rows([part[0] + zero] + part[1:], F32, pad_to=rows, slot=True)
        land = lax.dynamic_update_slice(lax.empty((N_DEV, rows, D), F32), part, (2 * chip + cc, 0, 0))
        return small_start(land, tag)

    def swap(i, key, buf):
        sems, buf, got, tok = swap_start(buf, f"{i}{key}")
        swaps.append((i, key, sems, buf, got))
        return tok

    def swapped(after, zero):
        while swaps:
            i, key, sems, g, got = swaps.pop(0)
            g, got = swap_wait(sems, g, got, after, f"{i}{key}")
            own, sums = chip_sum(place, g, got)
            sems, sums, land, tok = scatter_start(sums, f"{i}{key}")
            scatters.append((i, key, own, sems, sums, land))
            zero = zero + tok[0:1, 0:1]
        return zero

    for i in reversed(range(depth)):
        j = i // 2
        lay = _layer_units(i)[-1][1]
        aw = allw[i]
        s = saved[i]

        g3 = row(norm_ple[i])
        if token is not None:
            g3 = g3 + token[0:1, 0:1]
        dh2, dgt, dpp, dg3 = ple_bwd(dh, s["h2"], pt, i, g3, aw, lay, s["wp"])
        gs["norm_ple"][i] = dg3[0]
        buf = mm_tn_into(lay["rows"], s["hn3"], dgt, D // N_CHIPS, lay["gate"], False)
        dproj = _col_shards(mm_tn(pt, dpp, layer=i))
        if "ln" in lay:
            buf = put(buf, lay["ln"], jnp.zeros((N_CHIPS, lay["rows"] - lay["ln"], D), F32))
            buf = put(buf, lay["proj"], dproj)
        dh1, dh1b, du, a, dg2, dh2b, *do = ffn_bwd(dh2, s["h1"], s["r"], row(norm_ffn[i]), aw, lay,
                                                   mixer_rows=D // N_CHIPS if i % 2 == 0 else None)
        gs["norm_ffn"][i] = dg2[0]
        buf = mm_tn_into(buf, a, dh2b, D, lay["down"], False)
        buf = mm_tn_into(buf, s["hn2"], du, D, lay["up"], True)
        buf = mm_tn_into(buf, s["y"], dh1b, s["y"].shape[1] // N_CHIPS, lay["out"], False)
        g1 = swapped(dh1, row(norm_mix[i]))
        if i % 2 == 0:
            split = i in SPLIT_LAYERS
            dq, dk, dv = flash_bwd(s["q"], s["k"], s["v"], s["y"], do[0], s["lse"], seq,
                                   after=swap(i, "main", buf) if split else dh1b)
            g1 = swapped(dq, g1)
            (dh, hn1, cq, ckv, dqp, dkvp, dlat, dg1, dgq, dgkv, dgqn, dgqr, dgkn, dgkr) = mla_pre_bwd(
                dq, dk, dv, dh1, s["h"], g1, *s["mla_args"][1:])
            gs["norm_mix"][i] = dg1[0]
            gs["mla_q_lora_g"][j] = dgq[0]
            gs["mla_kv_lora_g"][j] = dgkv[0]
            gs["mla_q_nope_g"][j] = dgqn[0]
            gs["mla_q_rope_g"][j] = dgqr[0, :DR]
            gs["mla_k_nope_g"][j] = dgkn[0]
            gs["mla_k_rope_g"][j] = dgkr[0, :DR]
            small = [mm_tn(hn1, dlat)[:, :LAT].reshape(N_CHIPS, -1, D), _col_shards(_merge_uq(mm_tn(cq, dqp))),
                     _col_shards(mm_tn(ckv, dkvp)), dproj]
            if split:
                buf = jnp.concatenate(small, axis=1)
            else:
                buf = put(buf, lay["wdn"], jnp.concatenate(small, axis=1))
            key = "odd" if split else "main"
        else:
            dh, hn1, dpre, dws, dbs, dlng, dlnb, dg1 = gmlp_bwd(
                dh1, dh1b, s["h"], s["pre"], g1, aw, lay, s["ln"][0], s["ln"][1], wm[j], wmt[j], bfull[j], tril)
            gs["norm_mix"][i] = dg1[0]
            gs["gmlp_ln_g"][j] = dlng[0]
            gs["gmlp_ln_b"][j] = dlnb[0]
            gs["gmlp_w_s"][j] = dws
            gs["gmlp_b_s"][j] = jnp.sum(dbs.reshape(GC, GG, GD), axis=-1).T
            buf = mm_tn_into(buf, hn1, dpre, D, lay["in"], True)
            key = "main"
        token = swap(i, key, buf)
        if i == 1:
            small_gmlp = small_exchange(_SMALL_GMLP, token[0, 0], "gmlp")
            token = token + small_gmlp[3]
    last = swapped(dh, jnp.zeros((1, 1), F32))
    grad_x = dh.reshape(x.shape)
    small_rest = small_exchange(_SMALL_REST, last[0, 0], "rest", extra=[loss_part[0, 0:1]])

    after = small_rest[3]
    shares = []
    for i, key, own, sems, sums, land in scatters:
        _, got = scatter_wait(sems, sums, land, after, f"{i}{key}")
        sems, src, full, after = share_start(final_sum(place, own, got), f"{i}{key}")
        shares.append((i, key, sems, src, full))
    where = {n: [None] * weights[n].shape[0] for n in _BIG}
    for i, key, sems, src, full in shares:
        _, after = share_wait(sems, src, full, after, f"{i}{key}")
        for n, l, row0 in dict((k, parts) for k, _, parts in _layer_units(i))[key]:
            where[n][l] = (after, row0)
            if weights[n].shape[-1] != D:
                gw[n][l] = after[row0:row0 + weights[n][l].size // D].reshape(weights[n].shape[1:])
    grads = {n: jnp.stack(gw[n]) for n in _BIG if weights[n].shape[-1] != D}

    tot = []
    for names, (sems, src, land, _), tag in ((_SMALL_REST, small_rest, "rest"), (_SMALL_GMLP, small_gmlp, "gmlp")):
        summed = sum_devices(small_wait(sems, src, land, after, tag)[1]).reshape(-1)
        tot.append(summed[:sum(small_size(n) for n in names)])
        if tag == "rest":
            loss = summed[tot[-1].size]
    tot = jnp.concatenate(tot)
    off = 0
    for n, sz in ((n, small_size(n)) for n in _SMALL_REST + _SMALL_GMLP):
        gsum = tot[off:off + sz]
        off += sz
        if n in ("gmlp_ln_g", "gmlp_ln_b"):
            gsum = lax.dynamic_slice_in_dim(gsum.reshape(-1, GH), chip * (GH // N_CHIPS), GH // N_CHIPS, axis=1)
        grads[n] = gsum.reshape(weights[n].shape)

    delta, new_m, new_v = {}, {}, {}
    for n in _BIG:
        if weights[n].shape[-1] == D:
            grads[n], delta[n], new_m[n], new_v[n] = adamw_layers(
                weights[n], args["m_" + n], args["v_" + n], [b for b, _ in where[n]], [r for _, r in where[n]])
            continue
        w2 = weights[n].reshape(-1, weights[n].shape[-1])
        d, mn, vn = adamw(w2, grads[n].reshape(w2.shape), args["m_" + n].reshape(w2.shape),
                          args["v_" + n].reshape(w2.shape))
        delta[n], new_m[n], new_v[n] = (a.reshape(weights[n].shape) for a in (d, mn, vn))
    own_sizes = [weights[n].size for n in _SMALL]
    own_rows = -(-sum(own_sizes) // (8 * D)) * 8
    packed = [_pack_rows([src[n] for n in _SMALL], F32, pad_to=own_rows)
              for src in (weights, grads, {n: args["m_" + n] for n in _SMALL}, {n: args["v_" + n] for n in _SMALL})]
    outs = adamw(*packed)
    off = 0
    for n, sz in zip(_SMALL, own_sizes):
        for dst, o in zip((delta, new_m, new_v), outs):
            dst[n] = o.reshape(-1)[off:off + sz].reshape(weights[n].shape)
        off += sz

    order = ["norm_mix", "norm_ffn", "norm_ple", "mla_w_down", "mla_q_lora_g", "mla_kv_lora_g", "mla_w_uq",
             "mla_w_ukv", "mla_q_nope_g", "mla_q_rope_g", "mla_k_nope_g", "mla_k_rope_g", "mla_w_out", "gmlp_w_in",
             "gmlp_ln_g", "gmlp_ln_b", "gmlp_w_s", "gmlp_b_s", "gmlp_w_out", "ffn_w_up", "ffn_w_down", "ple_w_gate",
             "ple_w_proj"]
    return (loss, grad_x, *[grads[n] for n in order], *[delta[n] for n in order], *[new_m[n] for n in order],
            *[new_v[n] for n in order])
```

```python
import functools

import jax
import jax.numpy as jnp
from jax import lax
from jax.experimental import pallas as pl
from jax.experimental.pallas import tpu as pltpu

F32 = jnp.float32
BF16 = jnp.bfloat16
MESH = pl.DeviceIdType.MESH

D = 1024
HEADS = 8
DN = 128
DR = 64
QL = 384
KVL = 256
LAT = 704
LATP = 768
DFF = 4096
GH = 2048
GC = 128
GG = 8
GD = 256
PLE = 256
EPS = 1e-6
ROPE_BASE = 10000.0
SM_SCALE = (DN + DR) ** -0.5
N_CHIPS = 4
LANES = 128

ADAM_LR = 0.001
ADAM_B1 = 0.9
ADAM_B2 = 0.999
ADAM_EPS = 1e-08
ADAM_WD = 0.01
ADAM_STEP = 10

TM = 256
TMB = 512
TQ = 512
TQ_FWD = 512
FWD_HEADS = 4
BWD_HEADS = 2
SUM_ROWS = 256
VMEM_LIMIT = 56 * 1024 * 1024


def _cp(*sem):
    return pltpu.CompilerParams(dimension_semantics=sem, vmem_limit_bytes=VMEM_LIMIT)


def _dot(a, b):
    return jnp.dot(a, b, preferred_element_type=F32)


def _dot_nt(a, b):
    return lax.dot_general(a, b, (((1,), (1,)), ((), ())), preferred_element_type=F32)


def _dot_tn(a, b):
    return lax.dot_general(a, b, (((0,), (0,)), ((), ())), preferred_element_type=F32)


def _rms(x, g, n):
    r = lax.rsqrt(jnp.sum(x * x, axis=-1, keepdims=True) * (1.0 / n) + EPS)
    xhat = x * r
    return xhat * g, xhat, r


def _rms_bwd(dy, g, xhat, r, n):
    dxhat = dy * g
    return r * (dxhat - xhat * (jnp.sum(dxhat * xhat, axis=-1, keepdims=True) * (1.0 / n)))


def _rope(x, c, s):
    return x * c + (pltpu.roll(x, 32, 1) - pltpu.roll(x, 96, 1)) * s


def _rope_t(dy, c, s):
    w = dy * s
    return dy * c + pltpu.roll(w, 96, 1) - pltpu.roll(w, 32, 1)


def _sigmoid(x):
    return 1.0 / (1.0 + jnp.exp(-x))


_GELU_K = 0.7978845608028654
_GELU_C = 0.044715


def _gelu(x):
    return 0.5 * x * (1.0 + jnp.tanh(_GELU_K * (x + _GELU_C * x * x * x)))


def _gelu_and_grad(x):
    x2 = x * x
    t = jnp.tanh(_GELU_K * (x + _GELU_C * x2 * x))
    half = 0.5 * (1.0 + t)
    return x * half, half + 0.5 * x * (1.0 - t * t) * (_GELU_K * (1.0 + 3.0 * _GELU_C * x2))


def _acc_rows(ref, val):
    ref[...] += jnp.broadcast_to(jnp.sum(val, axis=0, keepdims=True), ref.shape)


def _row(tm, c):
    return pl.BlockSpec((tm, c), lambda i: (i, 0))


def _const(shape):
    nd = len(shape)
    return pl.BlockSpec(shape, lambda i: (0,) * nd, pipeline_mode=pl.Buffered(1))


def _wblk(rows, row0):
    assert row0 % rows == 0, (rows, row0)
    return pl.BlockSpec((N_CHIPS, rows, D), lambda i: (0, row0 // rows, 0), pipeline_mode=pl.Buffered(1))


def _rows_joined(w_ref):
    return w_ref[...].reshape(N_CHIPS * w_ref.shape[1], D)


def _sds(shape, dtype):
    return jax.ShapeDtypeStruct(shape, dtype)


def mixffn_fwd(h, y, allw, lay, g2):
    t, k = y.shape

    def body(h_ref, y_ref, wo_ref, g_ref, wu_ref, wd_ref, h1_ref, h2_ref, hn_ref, r_ref):
        h1 = h_ref[...] + _dot(y_ref[...], _rows_joined(wo_ref))
        h1_ref[...] = h1
        yn, _, _ = _rms(h1, g_ref[...], D)
        hn = yn.astype(BF16)
        hn_ref[...] = hn
        f = jnp.zeros((TMB, D), F32)
        for c in range(N_CHIPS):
            r = jnp.maximum(_dot(hn, wu_ref[c]), 0.0)
            r_ref[:, c * D:(c + 1) * D] = r.astype(BF16)
            f = f + _dot((r * r).astype(BF16), wd_ref[c])
        h2_ref[...] = h1 + f

    return pl.pallas_call(
        body, name="mixffn_fwd", grid=(t // TMB,),
        in_specs=[_row(TMB, D), _row(TMB, k), _wblk(k // N_CHIPS, lay["out"]), _const((1, D)), _wblk(D, lay["up"]),
                  _wblk(D, lay["down"])],
        out_specs=[_row(TMB, D), _row(TMB, D), _row(TMB, D), _row(TMB, DFF)],
        out_shape=[_sds((t, D), F32), _sds((t, D), F32), _sds((t, D), BF16), _sds((t, DFF), BF16)],
        compiler_params=_cp("parallel"),
    )(h, y, allw, g2, allw, allw)


def _layer_rows(tm, c, layer):
    return pl.BlockSpec((None, tm, c), lambda i: (layer, i, 0))


def ple_fwd(h2, p, layer, g3, allw, lay, wp):
    t = h2.shape[0]

    def body(h_ref, p_ref, g_ref, wg_ref, wp_ref, h3_ref, hn_ref):
        x = h_ref[...]
        yn, _, _ = _rms(x, g_ref[...], D)
        hn = yn.astype(BF16)
        hn_ref[...] = hn
        gt = _dot(hn, _rows_joined(wg_ref))
        pp = _dot(p_ref[...].astype(BF16), wp_ref[...])
        h3_ref[...] = x + _sigmoid(gt) * pp

    return pl.pallas_call(
        body, name="ple_fwd", grid=(t // TMB,),
        in_specs=[_row(TMB, D), _layer_rows(TMB, PLE, layer), _const((1, D)), _wblk(D // N_CHIPS, lay["gate"]),
                  _const((PLE, D))],
        out_specs=[_row(TMB, D), _row(TMB, D)],
        out_shape=[_sds((t, D), F32), _sds((t, D), BF16)],
        compiler_params=_cp("parallel"),
    )(h2, p, g3, allw, wp)


def _mla_project(h_ref, g1_ref, wdn_ref, gq_ref, gkv_ref, wuq_ref, wukv_ref):
    x = h_ref[...]
    yn, xhat, rx = _rms(x, g1_ref[...], D)
    hn = yn.astype(BF16)
    lat = _dot(hn, wdn_ref[...])
    cq, cqhat, rq = _rms(lat[:, :QL], gq_ref[...], QL)
    ckv, ckvhat, rkv = _rms(lat[:, QL:QL + KVL], gkv_ref[...], KVL)
    kr_raw = lat[:, QL + KVL:]
    cqb = cq.astype(BF16)
    ckvb = ckv.astype(BF16)
    qp = _dot(cqb, wuq_ref[...])
    kvp = _dot(ckvb, wukv_ref[...])
    return dict(xhat=xhat, rx=rx, hn=hn, cqhat=cqhat, rq=rq, ckvhat=ckvhat, rkv=rkv, kr_raw=kr_raw,
                cqb=cqb, ckvb=ckvb, qp=qp, kvp=kvp)


def mla_pre_fwd(h, g1, wdn, gq, gkv, wuq, wukv, gqn, gqr, gkn, gkr, cos, sin):
    t = h.shape[0]

    def body(h_ref, g1_ref, wdn_ref, gq_ref, gkv_ref, wuq_ref, wukv_ref, gqn_ref, gqr_ref, gkn_ref, gkr_ref,
             c_ref, s_ref, q_ref, k_ref, v_ref):
        m = _mla_project(h_ref, g1_ref, wdn_ref, gq_ref, gkv_ref, wuq_ref, wukv_ref)
        c = c_ref[...]
        s = s_ref[...]
        kr, _, _ = _rms(m["kr_raw"], gkr_ref[...], DR)
        krb = _rope(kr, c, s).astype(BF16)
        for hd in range(HEADS):
            qn, _, _ = _rms(m["qp"][:, hd * DN:(hd + 1) * DN], gqn_ref[...], DN)
            qr, _, _ = _rms(m["qp"][:, D + hd * LANES:D + (hd + 1) * LANES], gqr_ref[...], DR)
            q_ref[hd, :, 0:DN] = (qn * SM_SCALE).astype(BF16)
            q_ref[hd, :, DN:2 * DN] = (_rope(qr, c, s) * SM_SCALE).astype(BF16)
            kn, _, _ = _rms(m["kvp"][:, hd * 2 * DN:hd * 2 * DN + DN], gkn_ref[...], DN)
            k_ref[hd, :, 0:DN] = kn.astype(BF16)
            k_ref[hd, :, DN:2 * DN] = krb
            v_ref[hd] = m["kvp"][:, hd * 2 * DN + DN:(hd + 1) * 2 * DN].astype(BF16)

    hb = lambda w: pl.BlockSpec((HEADS, TM, w), lambda i: (0, i, 0))
    return pl.pallas_call(
        body, name="mla_pre_fwd", grid=(t // TM,),
        in_specs=[_row(TM, D), _const((1, D)), _const((D, LATP)), _const((1, QL)), _const((1, KVL)),
                  _const((QL, 2 * D)), _const((KVL, 2 * D)), _const((1, LANES)), _const((1, LANES)),
                  _const((1, LANES)), _const((1, LANES)), _row(TM, LANES), _row(TM, LANES)],
        out_specs=[hb(2 * DN), hb(2 * DN), hb(DN)],
        out_shape=[_sds((HEADS, t, 2 * DN), BF16), _sds((HEADS, t, 2 * DN), BF16), _sds((HEADS, t, DN), BF16)],
        compiler_params=_cp("parallel"),
    )(h, g1, wdn, gq, gkv, wuq, wukv, gqn, gqr, gkn, gkr, cos, sin)


def _diagonal_mask(n=TQ):
    return lax.broadcasted_iota(jnp.int32, (n, n), 1) <= lax.broadcasted_iota(jnp.int32, (n, n), 0)


def flash_fwd(q, k, v, seq):
    t = q.shape[1]
    nb = t // seq
    tq = TQ_FWD
    nq = seq // tq
    hp = FWD_HEADS

    def body(q_ref, k_ref, v_ref, o_ref, lse_ref):
        qi = pl.program_id(2)
        qs = [q_ref[a] for a in range(hp)]

        def step(j, carry, diagonal=False):
            rows = pl.ds(pl.multiple_of(j * tq, tq), tq)
            out = []
            for a in range(hp):
                m, l, acc = carry[a]
                s = _dot_nt(qs[a], k_ref[a, rows, :])
                if diagonal:
                    s = jnp.where(_diagonal_mask(tq), s, -1e30)
                m_new = jnp.maximum(m, jnp.max(s, axis=-1, keepdims=True))
                p = jnp.exp(s - m_new)
                alpha = jnp.exp(m - m_new)
                l = alpha * l + jnp.sum(p, axis=-1, keepdims=True)
                acc = alpha * acc + _dot(p.astype(BF16), v_ref[a, rows, :])
                out.append((m_new, l, acc))
            return tuple(out)

        one = (jnp.full((tq, 1), -1e30, F32), jnp.zeros((tq, 1), F32), jnp.zeros((tq, DN), F32))
        done = step(qi, lax.fori_loop(0, qi, step, (one,) * hp), diagonal=True)
        for a, (m, l, acc) in enumerate(done):
            o_ref[:, a * DN:(a + 1) * DN] = (acc / l).astype(BF16)
            lse_ref[a] = m + jnp.log(l)

    return pl.pallas_call(
        body, name="flash_fwd", grid=(nb, HEADS // hp, nq),
        in_specs=[pl.BlockSpec((hp, tq, 2 * DN), lambda b, h, i: (h, b * nq + i, 0)),
                  pl.BlockSpec((hp, seq, 2 * DN), lambda b, h, i: (h, b, 0)),
                  pl.BlockSpec((hp, seq, DN), lambda b, h, i: (h, b, 0))],
        out_specs=[pl.BlockSpec((tq, hp * DN), lambda b, h, i: (b * nq + i, h)),
                   pl.BlockSpec((hp, tq, 1), lambda b, h, i: (h, b * nq + i, 0))],
        out_shape=[_sds((t, HEADS * DN), BF16), _sds((HEADS, t, 1), F32)],
        compiler_params=_cp("parallel", "parallel", "arbitrary"),
    )(q, k, v)


def _gmlp_in(hn, win_ref):
    pre = [_dot(hn, win_ref[c]) for c in range(N_CHIPS)]
    return jnp.concatenate(pre[:2], axis=1), jnp.concatenate(pre[2:], axis=1)


def gmlp_fwd(h, g1, allw, lay, lng, lnb, wm, bfull):
    t = h.shape[0]

    def body(h_ref, g1_ref, win_ref, lng_ref, lnb_ref, wm_ref, b_ref, y_ref, pre_ref):
        yn, _, _ = _rms(h_ref[...], g1_ref[...], D)
        pre_u, pre_v = _gmlp_in(yn.astype(BF16), win_ref)
        pre_ref[:, :GH] = pre_u.astype(BF16)
        pre_ref[:, GH:] = pre_v.astype(BF16)
        u = _gelu(pre_u)
        v = _gelu(pre_v)
        xc = v - jnp.mean(v, axis=-1, keepdims=True)
        rs = lax.rsqrt(jnp.mean(xc * xc, axis=-1, keepdims=True) + EPS)
        vnb = (xc * rs * lng_ref[...] + lnb_ref[...]).astype(BF16)
        for ch in range(TM // GC):
            rows = slice(ch * GC, (ch + 1) * GC)
            for g in range(GG):
                cols = slice(g * GD, (g + 1) * GD)
                sv = _dot(wm_ref[g], vnb[rows, cols]) + b_ref[:, cols]
                y_ref[rows, cols] = (u[rows, cols] * sv).astype(BF16)

    return pl.pallas_call(
        body, name="gmlp_fwd", grid=(t // TM,),
        in_specs=[_row(TM, D), _const((1, D)), _wblk(D, lay["in"]), _const((1, GH)), _const((1, GH)),
                  _const((GG, GC, GC)), _const((GC, GH))],
        out_specs=[_row(TM, GH), _row(TM, 2 * GH)],
        out_shape=[_sds((t, GH), BF16), _sds((t, 2 * GH), BF16)],
        compiler_params=_cp("parallel"),
    )(h, g1, allw, lng, lnb, wm, bfull)


def loss_head(h, tgt):
    t = h.shape[0]

    def body(h_ref, t_ref, dh_ref, loss_ref):
        @pl.when(pl.program_id(0) == 0)
        def _():
            loss_ref[...] = jnp.zeros_like(loss_ref)

        e = h_ref[...] - t_ref[...]
        dh_ref[...] = e * (1.0 / D)
        part = jnp.sum(jnp.sum(e * e, axis=-1, keepdims=True), axis=0, keepdims=True) * (0.5 / D)
        loss_ref[...] += jnp.broadcast_to(part, loss_ref.shape)

    return pl.pallas_call(
        body, name="loss_head", grid=(t // TMB,),
        in_specs=[_row(TMB, D), _row(TMB, D)],
        out_specs=[_row(TMB, D), _const((8, LANES))],
        out_shape=[_sds((t, D), F32), _sds((8, LANES), F32)],
        compiler_params=_cp("arbitrary"),
    )(h, tgt)


def _zero_at_first_step(*refs):
    @pl.when(pl.program_id(0) == 0)
    def _():
        for r in refs:
            r[...] = jnp.zeros_like(r)


def ple_bwd(dh3, h2, p, layer, g3, allw, lay, wp):
    t = h2.shape[0]

    def body(dh_ref, h_ref, p_ref, g_ref, wg_ref, wp_ref, dh2_ref, dgt_ref, dpp_ref, dg_ref):
        _zero_at_first_step(dg_ref)
        dh3v = dh_ref[...]
        x = h_ref[...]
        g = g_ref[...]
        wg = _rows_joined(wg_ref)
        yn, xhat, r = _rms(x, g, D)
        gt = _dot(yn.astype(BF16), wg)
        pp = _dot(p_ref[...].astype(BF16), wp_ref[...])
        sg = _sigmoid(gt)
        dgt = (dh3v * pp * sg * (1.0 - sg)).astype(BF16)
        dgt_ref[...] = dgt
        dpp_ref[...] = (dh3v * sg).astype(BF16)
        dhn = _dot_nt(dgt, wg)
        _acc_rows(dg_ref, dhn * xhat)
        dh2_ref[...] = dh3v + _rms_bwd(dhn, g, xhat, r, D)

    return pl.pallas_call(
        body, name="ple_bwd", grid=(t // TMB,),
        in_specs=[_row(TMB, D), _row(TMB, D), _layer_rows(TMB, PLE, layer), _const((1, D)),
                  _wblk(D // N_CHIPS, lay["gate"]),
                  _const((PLE, D))],
        out_specs=[_row(TMB, D), _row(TMB, D), _row(TMB, D), _const((8, D))],
        out_shape=[_sds((t, D), F32), _sds((t, D), BF16), _sds((t, D), BF16), _sds((8, D), F32)],
        compiler_params=_cp("arbitrary"),
    )(dh3, h2, p, g3, allw, wp)


def ffn_bwd(dh2, h1, r, g2, allw, lay, mixer_rows=None):
    t = h1.shape[0]

    def body(dh_ref, h_ref, r_ref, g_ref, wu_ref, wd_ref, *rest):
        wo_ref = rest[0] if mixer_rows else None
        dh1_ref, dh1b_ref, du_ref, a_ref, dg_ref, dhb_ref = rest[1:7] if mixer_rows else rest[:6]
        _zero_at_first_step(dg_ref)
        dhb = dh_ref[...].astype(BF16)
        dhb_ref[...] = dhb
        g = g_ref[...]
        _, xhat, rr = _rms(h_ref[...], g, D)
        dhn = jnp.zeros((TM, D), F32)
        for c in range(N_CHIPS):
            cs = slice(c * D, (c + 1) * D)
            rc = r_ref[:, cs].astype(F32)
            a_ref[:, cs] = (rc * rc).astype(BF16)
            da = _dot_nt(dhb, wd_ref[c])
            du = (da * (2.0 * rc)).astype(BF16)
            du_ref[:, cs] = du
            dhn = dhn + _dot_nt(du, wu_ref[c])
        _acc_rows(dg_ref, dhn * xhat)
        dh1 = dh_ref[...] + _rms_bwd(dhn, g, xhat, rr, D)
        dh1_ref[...] = dh1
        dh1b = dh1.astype(BF16)
        dh1b_ref[...] = dh1b
        if mixer_rows:
            rest[7][...] = _dot_nt(dh1b, _rows_joined(wo_ref)).astype(BF16)

    k = N_CHIPS * mixer_rows if mixer_rows else 0
    return pl.pallas_call(
        body, name="ffn_bwd", grid=(t // TM,),
        in_specs=[_row(TM, D), _row(TM, D), _row(TM, DFF), _const((1, D)), _wblk(D, lay["up"]),
                  _wblk(D, lay["down"])] + ([_wblk(mixer_rows, lay["out"])] if mixer_rows else []),
        out_specs=[_row(TM, D), _row(TM, D), _row(TM, DFF), _row(TM, DFF), _const((8, D)), _row(TM, D)]
        + ([_row(TM, k)] if mixer_rows else []),
        out_shape=[_sds((t, D), F32), _sds((t, D), BF16), _sds((t, DFF), BF16), _sds((t, DFF), BF16),
                   _sds((8, D), F32), _sds((t, D), BF16)] + ([_sds((t, k), BF16)] if mixer_rows else []),
        compiler_params=_cp("arbitrary"),
    )(dh2, h1, r, g2, allw, allw, *([allw] if mixer_rows else []))


def flash_bwd(q, k, v, o, do, lse, seq, after):
    t = q.shape[1]
    nb = t // seq
    nq = seq // TQ
    hp = BWD_HEADS

    def body(q_ref, k_ref, v_ref, o_ref, do_ref, lse_ref, after_ref, dq_ref, dk_ref, dv_ref):
        del after_ref
        kj = pl.program_id(2)

        @pl.when(kj == 0)
        def _():
            dq_ref[...] = jnp.zeros_like(dq_ref)

        def step(i, carry, diagonal=False):
            rows = pl.ds(pl.multiple_of(i * TQ, TQ), TQ)
            out = []
            for a in range(hp):
                dk, dv = carry[a]
                kv = k_ref[a]
                qv = q_ref[a, rows, :]
                dov = do_ref[rows, a * DN:(a + 1) * DN]
                ov = o_ref[rows, a * DN:(a + 1) * DN]
                delta = jnp.sum(dov.astype(F32) * ov.astype(F32), axis=-1, keepdims=True)
                s = _dot_nt(qv, kv)
                if diagonal:
                    s = jnp.where(_diagonal_mask(), s, -1e30)
                p = jnp.exp(s - lse_ref[a, rows, :])
                dp = _dot_nt(dov, v_ref[a])
                ds = (p * (dp - delta)).astype(BF16)
                dv = dv + _dot_tn(p.astype(BF16), dov)
                dk = dk + _dot_tn(ds, qv)
                dq_ref[a, rows, :] += _dot(ds, kv)
                out.append((dk, dv))
            return tuple(out)

        one = (jnp.zeros((TQ, 2 * DN), F32), jnp.zeros((TQ, DN), F32))
        done = lax.fori_loop(kj + 1, nq, step, step(kj, (one,) * hp, diagonal=True))
        for a, (dk, dv) in enumerate(done):
            dk_ref[a] = dk
            dv_ref[a] = dv

    return pl.pallas_call(
        body, name="flash_bwd", grid=(nb, HEADS // hp, nq),
        in_specs=[pl.BlockSpec((hp, seq, 2 * DN), lambda b, h, j: (h, b, 0)),
                  pl.BlockSpec((hp, TQ, 2 * DN), lambda b, h, j: (h, b * nq + j, 0)),
                  pl.BlockSpec((hp, TQ, DN), lambda b, h, j: (h, b * nq + j, 0)),
                  pl.BlockSpec((seq, hp * DN), lambda b, h, j: (b, h)),
                  pl.BlockSpec((seq, hp * DN), lambda b, h, j: (b, h)),
                  pl.BlockSpec((hp, seq, 1), lambda b, h, j: (h, b, 0)), _ANY],
        out_specs=[pl.BlockSpec((hp, seq, 2 * DN), lambda b, h, j: (h, b, 0)),
                   pl.BlockSpec((hp, TQ, 2 * DN), lambda b, h, j: (h, b * nq + j, 0)),
                   pl.BlockSpec((hp, TQ, DN), lambda b, h, j: (h, b * nq + j, 0))],
        out_shape=[_sds((HEADS, t, 2 * DN), F32), _sds((HEADS, t, 2 * DN), F32), _sds((HEADS, t, DN), F32)],
        compiler_params=_cp("parallel", "parallel", "arbitrary"),
    )(q, k, v, o, do, lse, after)


def mla_pre_bwd(dq, dk, dv, dh1, h, g1, wdn, gq, gkv, wuq, wukv, gqn, gqr, gkn, gkr, cos, sin):
    t = h.shape[0]

    def body(dq_ref, dk_ref, dv_ref, dh1_ref, h_ref, g1_ref, wdn_ref, gq_ref, gkv_ref, wuq_ref, wukv_ref,
             gqn_ref, gqr_ref, gkn_ref, gkr_ref, c_ref, s_ref,
             dh_ref, hn_ref, cq_ref, ckv_ref, dqp_ref, dkvp_ref, dlat_ref,
             dg1_ref, dgq_ref, dgkv_ref, dgqn_ref, dgqr_ref, dgkn_ref, dgkr_ref):
        _zero_at_first_step(dg1_ref, dgq_ref, dgkv_ref, dgqn_ref, dgqr_ref, dgkn_ref, dgkr_ref)
        m = _mla_project(h_ref, g1_ref, wdn_ref, gq_ref, gkv_ref, wuq_ref, wukv_ref)
        hn_ref[...] = m["hn"]
        cq_ref[...] = m["cqb"]
        ckv_ref[...] = m["ckvb"]
        c = c_ref[...]
        s = s_ref[...]
        gqn = gqn_ref[...]
        gqr = gqr_ref[...]
        gkn = gkn_ref[...]
        gkr = gkr_ref[...]

        dkr = dk_ref[0, :, DN:2 * DN]
        for hd in range(1, HEADS):
            dkr = dkr + dk_ref[hd, :, DN:2 * DN]
        dkr = _rope_t(dkr, c, s)
        _, krhat, rkr = _rms(m["kr_raw"], gkr, DR)
        _acc_rows(dgkr_ref, dkr * krhat)
        dkr_raw = _rms_bwd(dkr, gkr, krhat, rkr, DR)

        for hd in range(HEADS):
            ncols = slice(hd * DN, (hd + 1) * DN)
            _, xh, r = _rms(m["qp"][:, ncols], gqn, DN)
            dqn = dq_ref[hd, :, 0:DN] * SM_SCALE
            _acc_rows(dgqn_ref, dqn * xh)
            dqp_ref[:, ncols] = _rms_bwd(dqn, gqn, xh, r, DN).astype(BF16)

            rcols = slice(D + hd * LANES, D + (hd + 1) * LANES)
            _, xh, r = _rms(m["qp"][:, rcols], gqr, DR)
            dqr = _rope_t(dq_ref[hd, :, DN:2 * DN] * SM_SCALE, c, s)
            _acc_rows(dgqr_ref, dqr * xh)
            dqp_ref[:, rcols] = _rms_bwd(dqr, gqr, xh, r, DR).astype(BF16)

            kcols = slice(hd * 2 * DN, hd * 2 * DN + DN)
            _, xh, r = _rms(m["kvp"][:, kcols], gkn, DN)
            dkn = dk_ref[hd, :, 0:DN]
            _acc_rows(dgkn_ref, dkn * xh)
            dkvp_ref[:, kcols] = _rms_bwd(dkn, gkn, xh, r, DN).astype(BF16)
            dkvp_ref[:, hd * 2 * DN + DN:(hd + 1) * 2 * DN] = dv_ref[hd].astype(BF16)

        dcq = _dot_nt(dqp_ref[...], wuq_ref[...])
        _acc_rows(dgq_ref, dcq * m["cqhat"])
        dlat_q = _rms_bwd(dcq, gq_ref[...], m["cqhat"], m["rq"], QL)
        dckv = _dot_nt(dkvp_ref[...], wukv_ref[...])
        _acc_rows(dgkv_ref, dckv * m["ckvhat"])
        dlat_kv = _rms_bwd(dckv, gkv_ref[...], m["ckvhat"], m["rkv"], KVL)
        dlat = jnp.concatenate([dlat_q, dlat_kv, dkr_raw], axis=1).astype(BF16)
        dlat_ref[...] = dlat
        dhn = _dot_nt(dlat, wdn_ref[...])
        _acc_rows(dg1_ref, dhn * m["xhat"])
        dh_ref[...] = dh1_ref[...] + _rms_bwd(dhn, g1_ref[...], m["xhat"], m["rx"], D)

    hb = lambda w: pl.BlockSpec((HEADS, TM, w), lambda i: (0, i, 0))
    return pl.pallas_call(
        body, name="mla_pre_bwd", grid=(t // TM,),
        in_specs=[hb(2 * DN), hb(2 * DN), hb(DN), _row(TM, D), _row(TM, D), _const((1, D)), _const((D, LATP)),
                  _const((1, QL)), _const((1, KVL)), _const((QL, 2 * D)), _const((KVL, 2 * D)),
                  _const((1, LANES)), _const((1, LANES)), _const((1, LANES)), _const((1, LANES)),
                  _row(TM, LANES), _row(TM, LANES)],
        out_specs=[_row(TM, D), _row(TM, D), _row(TM, QL), _row(TM, KVL), _row(TM, 2 * D), _row(TM, 2 * D),
                   _row(TM, LATP), _const((8, D)), _const((8, QL)), _const((8, KVL)), _const((8, LANES)),
                   _const((8, LANES)), _const((8, LANES)), _const((8, LANES))],
        out_shape=[_sds((t, D), F32), _sds((t, D), BF16), _sds((t, QL), BF16), _sds((t, KVL), BF16),
                   _sds((t, 2 * D), BF16), _sds((t, 2 * D), BF16), _sds((t, LATP), BF16),
                   _sds((8, D), F32), _sds((8, QL), F32), _sds((8, KVL), F32), _sds((8, LANES), F32),
                   _sds((8, LANES), F32), _sds((8, LANES), F32), _sds((8, LANES), F32)],
        compiler_params=_cp("arbitrary"),
    )(dq, dk, dv, dh1, h, g1, wdn, gq, gkv, wuq, wukv, gqn, gqr, gkn, gkr, cos, sin)


def gmlp_bwd(dh1, dh1b, h, pre, g1, allw, lay, lng, lnb, wm, wmt, bfull, tril):
    t = h.shape[0]

    def body(dh1_ref, dh1b_ref, h_ref, pre_ref, g1_ref, win_ref, lng_ref, lnb_ref, wm_ref, wmt_ref, b_ref,
             wout_ref, tril_ref, dh_ref, hn_ref, dpre_ref, dws_ref, dbs_ref, dlng_ref, dlnb_ref, dg1_ref,
             dvn_s):
        _zero_at_first_step(dws_ref, dbs_ref, dlng_ref, dlnb_ref, dg1_ref)
        g1 = g1_ref[...]
        yn, xhat, rx = _rms(h_ref[...], g1, D)
        hn_ref[...] = yn.astype(BF16)
        dy = _dot_nt(dh1b_ref[...], _rows_joined(wout_ref))
        pre_u = pre_ref[:, :GH].astype(F32)
        pre_v = pre_ref[:, GH:].astype(F32)
        u, gg_u = _gelu_and_grad(pre_u)
        v, gg_v = _gelu_and_grad(pre_v)
        xc = v - jnp.mean(v, axis=-1, keepdims=True)
        rs = lax.rsqrt(jnp.mean(xc * xc, axis=-1, keepdims=True) + EPS)
        vhat = xc * rs
        lng = lng_ref[...]
        vnb = (vhat * lng + lnb_ref[...]).astype(BF16)
        dsv = dy * u
        dsvb = dsv.astype(BF16)
        tril_m = tril_ref[...]
        for ch in range(TM // GC):
            rows = slice(ch * GC, (ch + 1) * GC)
            dbs_ref[...] += dsv[rows, :]
            for g in range(GG):
                cols = slice(g * GD, (g + 1) * GD)
                sv = _dot(wm_ref[g], vnb[rows, cols]) + b_ref[:, cols]
                dpre_ref[rows, cols] = (dy[rows, cols] * sv * gg_u[rows, cols]).astype(BF16)
                dvn_s[rows, cols] = _dot(wmt_ref[g], dsvb[rows, cols])
                dws_ref[g] += _dot_nt(dsvb[rows, cols], vnb[rows, cols]) * tril_m
        dvn = dvn_s[...]
        _acc_rows(dlng_ref, dvn * vhat)
        _acc_rows(dlnb_ref, dvn)
        dvhat = dvn * lng
        dv = rs * (dvhat - jnp.mean(dvhat, axis=-1, keepdims=True)
                   - vhat * jnp.mean(dvhat * vhat, axis=-1, keepdims=True))
        dpre_v = (dv * gg_v).astype(BF16)
        dpre_ref[:, GH:] = dpre_v
        dhn = _dot_nt(dpre_ref[:, 0:D], win_ref[0])
        for c in range(1, N_CHIPS):
            dhn = dhn + _dot_nt(dpre_ref[:, c * D:(c + 1) * D], win_ref[c])
        _acc_rows(dg1_ref, dhn * xhat)
        dh_ref[...] = dh1_ref[...] + _rms_bwd(dhn, g1, xhat, rx, D)

    return pl.pallas_call(
        body, name="gmlp_bwd", grid=(t // TM,),
        in_specs=[_row(TM, D), _row(TM, D), _row(TM, D), _row(TM, 2 * GH), _const((1, D)), _wblk(D, lay["in"]),
                  _const((1, GH)), _const((1, GH)), _const((GG, GC, GC)), _const((GG, GC, GC)), _const((GC, GH)),
                  _wblk(GH // N_CHIPS, lay["out"]), _const((GC, GC))],
        out_specs=[_row(TM, D), _row(TM, D), _row(TM, 2 * GH), _const((GG, GC, GC)), _const((GC, GH)),
                   _const((8, GH)), _const((8, GH)), _const((8, D))],
        out_shape=[_sds((t, D), F32), _sds((t, D), BF16), _sds((t, 2 * GH), BF16), _sds((GG, GC, GC), F32),
                   _sds((GC, GH), F32), _sds((8, GH), F32), _sds((8, GH), F32), _sds((8, D), F32)],
        scratch_shapes=[pltpu.VMEM((TM, GH), F32)],
        compiler_params=_cp("arbitrary"),
    )(dh1, dh1b, h, pre, g1, allw, lng, lnb, wm, wmt, bfull, allw, tril)


def _token_step(t):
    return next(s for s in (2048, 1024, 512) if t % s == 0)


def mm_tn(a, b, layer=None):
    t, k = a.shape[-2:]
    n = b.shape[1]
    tk = min(k, 1024)
    tn = min(n, 1024)
    tt = _token_step(t)
    a_spec = (pl.BlockSpec((tt, tk), lambda i, j, s: (s, i)) if layer is None else
              pl.BlockSpec((None, tt, tk), lambda i, j, s: (layer, s, i)))

    def body(a_ref, b_ref, o_ref):
        @pl.when(pl.program_id(2) == 0)
        def _():
            o_ref[...] = jnp.zeros_like(o_ref)

        o_ref[...] += _dot_tn(a_ref[...].astype(BF16), b_ref[...].astype(BF16))

    return pl.pallas_call(
        body, name="mm_tn", grid=(k // tk, n // tn, t // tt),
        in_specs=[a_spec, pl.BlockSpec((tt, tn), lambda i, j, s: (s, j))],
        out_specs=pl.BlockSpec((tk, tn), lambda i, j, s: (i, j)), out_shape=_sds((k, n), F32),
        compiler_params=_cp("parallel", "parallel", "arbitrary"),
    )(a, b)


def mm_tn_into(buf, a, b, rows, row0, col_sharded):
    t = a.shape[0]
    tt = _token_step(t)
    assert row0 % rows == 0 and a.shape[1] == (rows if col_sharded else N_CHIPS * rows), (rows, row0, a.shape)
    assert b.shape[1] == (N_CHIPS * D if col_sharded else D), b.shape
    joint = not col_sharded and N_CHIPS * rows <= 2048
    grid = (1, 1, t // tt) if joint else (1, N_CHIPS, t // tt) if col_sharded else (N_CHIPS, 1, t // tt)
    fresh = isinstance(buf, int)

    def body(*refs):
        a_ref, b_ref, o_ref = refs[-3:]

        @pl.when(pl.program_id(2) == 0)
        def _():
            o_ref[...] = jnp.zeros_like(o_ref)

        o_ref[...] += _dot_tn(a_ref[...].astype(BF16), b_ref[...].astype(BF16)).reshape(o_ref.shape)

    specs = [pl.BlockSpec((tt, N_CHIPS * rows if joint else rows), lambda i, j, s: (s, i)),
             pl.BlockSpec((tt, D), lambda i, j, s: (s, j))]
    return pl.pallas_call(
        body, name="mm_tn_into", grid=grid,
        in_specs=specs if fresh else [_ANY] + specs,
        out_specs=pl.BlockSpec((N_CHIPS if joint else None, rows, D), lambda i, j, s: (i + j, row0 // rows, 0)),
        out_shape=_sds((N_CHIPS, buf, D) if fresh else buf.shape, F32),
        input_output_aliases={} if fresh else {0: 0},
        compiler_params=_cp("parallel", "parallel", "arbitrary"),
    )(*((a, b) if fresh else (buf, a, b)))


def adamw(w, g, m, v):
    rows, cols = w.shape
    tr = rows if rows <= 512 else next(r for r in (512, 384, 256, 128) if rows % r == 0)
    c1 = 1.0 - ADAM_B1 ** ADAM_STEP
    c2 = 1.0 - ADAM_B2 ** ADAM_STEP

    def body(w_ref, g_ref, m_ref, v_ref, d_ref, mo_ref, vo_ref):
        gv = g_ref[...]
        mn = ADAM_B1 * m_ref[...] + (1.0 - ADAM_B1) * gv
        vn = ADAM_B2 * v_ref[...] + (1.0 - ADAM_B2) * (gv * gv)
        mo_ref[...] = mn
        vo_ref[...] = vn
        d_ref[...] = -ADAM_LR * ((mn / c1) / (jnp.sqrt(vn / c2) + ADAM_EPS) + ADAM_WD * w_ref[...])

    spec = pl.BlockSpec((tr, cols), lambda i: (i, 0))
    return pl.pallas_call(
        body, name="adamw", grid=(rows // tr,),
        in_specs=[spec] * 4, out_specs=[spec] * 3, out_shape=[_sds((rows, cols), F32)] * 3,
        compiler_params=_cp("parallel"),
    )(w, g, m, v)


def adamw_layers(w, m, v, bufs, row0s):
    nl, a, _ = w.shape
    tr = min(a, 512)
    c1 = 1.0 - ADAM_B1 ** ADAM_STEP
    c2 = 1.0 - ADAM_B2 ** ADAM_STEP
    assert all(r % tr == 0 for r in row0s) and a % tr == 0, (row0s, a)

    def body(w_ref, m_ref, v_ref, *rest):
        g_refs, (g_ref, d_ref, mo_ref, vo_ref) = rest[:nl], rest[nl:]
        for l in range(nl):
            @pl.when(pl.program_id(0) == l)
            def _(l=l):
                gv = g_refs[l][...]
                g_ref[...] = gv
                mn = ADAM_B1 * m_ref[...] + (1.0 - ADAM_B1) * gv
                vn = ADAM_B2 * v_ref[...] + (1.0 - ADAM_B2) * (gv * gv)
                mo_ref[...] = mn
                vo_ref[...] = vn
                d_ref[...] = -ADAM_LR * ((mn / c1) / (jnp.sqrt(vn / c2) + ADAM_EPS) + ADAM_WD * w_ref[...])

    def rows_of(l, row0):
        return pl.BlockSpec((tr, D), lambda li, i: (jnp.where(li == l, row0 // tr + i, row0 // tr), 0))

    spec = pl.BlockSpec((None, tr, D), lambda li, i: (li, i, 0))
    return pl.pallas_call(
        body, name="adamw_layers", grid=(nl, a // tr),
        in_specs=[spec] * 3 + [rows_of(l, r) for l, r in enumerate(row0s)],
        out_specs=[spec] * 4, out_shape=[_sds(w.shape, F32)] * 4,
        compiler_params=_cp("arbitrary", "arbitrary"),
    )(w, m, v, *bufs)


def _place():
    return lax.axis_index("x"), lax.axis_index("y"), lax.axis_index("c")


def _other_chips(x, y):
    return [(1 - x, y), (x, 1 - y), (1 - x, 1 - y)]


_ANY = pl.BlockSpec(memory_space=pl.ANY)


_HBM = pl.BlockSpec(memory_space=pltpu.HBM)
_SEM = pl.BlockSpec(memory_space=pltpu.SEMAPHORE)
_EFFECT = pltpu.SideEffectType.DATAFLOW_SIDE_EFFECTING
N_ICI = 3


def _exchange_start(name, src, land, copies, n):
    def body(src_ref, land_ref, *outs):
        sems, token = outs[:2 * n], outs[-1]
        for j, (s, d, to) in enumerate(copies(src_ref, land_ref, _place())):
            pltpu.make_async_remote_copy(src_ref=s, dst_ref=d, send_sem=sems[j], recv_sem=sems[n + j],
                                         device_id=to, device_id_type=MESH).start()
        token[...] = jnp.zeros_like(token)

    sem = pltpu.SemaphoreType.DMA(())
    outs = pl.pallas_call(
        body, name=name,
        out_shape=(sem,) * (2 * n) + (pltpu.HBM(src.shape, src.dtype), pltpu.HBM(land.shape, land.dtype),
                                      _sds((8, LANES), F32)),
        in_specs=(_HBM, _HBM),
        out_specs=(_SEM,) * (2 * n) + (_HBM, _HBM, pl.BlockSpec(memory_space=pltpu.VMEM)),
        input_output_aliases={0: 2 * n, 1: 2 * n + 1},
        compiler_params=pltpu.CompilerParams(has_side_effects=_EFFECT),
    )(pltpu.with_memory_space_constraint(src, pltpu.HBM), pltpu.with_memory_space_constraint(land, pltpu.HBM))
    return outs[:2 * n], outs[2 * n], outs[2 * n + 1], outs[-1]


def _exchange_wait(name, sems, src, land, after, arrivals):
    n = len(sems) // 2

    def body(src_ref, land_ref, *rest):
        sems = rest[:2 * n]
        for j, (s, d) in enumerate(arrivals(src_ref, land_ref, _place())):
            cp = pltpu.make_async_remote_copy(src_ref=s, dst_ref=d, send_sem=sems[j], recv_sem=sems[n + j],
                                              device_id=_place(), device_id_type=MESH)
            cp.wait_send()
            cp.wait_recv()

    return pl.pallas_call(
        body, name=name, out_shape=(pltpu.HBM(src.shape, src.dtype), pltpu.HBM(land.shape, land.dtype)),
        in_specs=(_HBM, _HBM) + (_SEM,) * (2 * n) + (_ANY,), out_specs=(_HBM, _HBM),
        input_output_aliases={0: 0, 1: 1},
        compiler_params=pltpu.CompilerParams(has_side_effects=_EFFECT),
    )(src, land, *sems, after)


def _halves(c, hh):
    return pl.ds(pl.multiple_of(c * hh, 16), hh), pl.ds(pl.multiple_of((1 - c) * hh, 16), hh)


def gather_start(land, tag):
    _, rr, _ = land.shape
    assert rr % 32 == 0, rr

    def copies(_, land_ref, place):
        x, y, c = place
        mine = land_ref.at[2 * x + y, _halves(c, rr // 2)[0]]
        return [(mine, mine, (cx, cy, c)) for cx, cy in _other_chips(x, y)]

    return _exchange_start(f"gather_start_{tag}", jnp.zeros((8, LANES), F32), land, copies, N_ICI)


def gather_wait(sems, src, land, after, tag):
    def arrivals(_, land_ref, place):
        x, y, c = place
        half = _halves(c, land.shape[1] // 2)[0]
        return [(land_ref.at[2 * x + y, half], land_ref.at[2 * cx + cy, half]) for cx, cy in _other_chips(x, y)]

    return _exchange_wait(f"gather_wait_{tag}", sems, src, land, after, arrivals)


def pass_start(land, tag):
    def copies(_, land_ref, place):
        x, y, c = place
        half = _halves(c, land.shape[1] // 2)[0]
        return [(land_ref.at[2 * cx + cy, half], land_ref.at[2 * cx + cy, half], (x, y, 1 - c))
                for cx, cy in _other_chips(x, y)]

    return _exchange_start(f"pass_start_{tag}", jnp.zeros((8, LANES), F32), land, copies, N_ICI)


def pass_wait(sems, src, land, after, tag):
    def arrivals(_, land_ref, place):
        x, y, c = place
        mine, other = _halves(c, land.shape[1] // 2)
        return [(land_ref.at[2 * cx + cy, mine], land_ref.at[2 * cx + cy, other]) for cx, cy in _other_chips(x, y)]

    return _exchange_wait(f"pass_wait_{tag}", sems, src, land, after, arrivals)


def swap_start(g, tag):
    _, rr, cc = g.shape

    def copies(g_ref, got_ref, place):
        x, y, c = place
        other = _halves(c, rr // 2)[1]
        return [(g_ref.at[k, other], got_ref.at[k], (x, y, 1 - c)) for k in range(N_CHIPS)]

    return _exchange_start(f"swap_start_{tag}", g, lax.empty((N_CHIPS, rr // 2, cc), g.dtype), copies, N_CHIPS)


def swap_wait(sems, g, got, after, tag):
    def arrivals(g_ref, got_ref, place):
        other = _halves(place[2], g.shape[1] // 2)[1]
        return [(g_ref.at[k, other], got_ref.at[k]) for k in range(N_CHIPS)]

    return _exchange_wait(f"swap_wait_{tag}", sems, g, got, after, arrivals)


def chip_sum(place, g32, got):
    _, rr, cc = g32.shape
    hh = rr // 2
    tr = SUM_ROWS
    assert rr % 2 == 0 and hh % tr == 0, (rr, tr)
    nb = hh // tr

    def body(place_ref, g_ref, got_ref, own_ref, all_ref):
        s = g_ref[...] + got_ref[...].astype(F32)
        all_ref[...] = s.astype(BF16)
        own_ref[...] = g_ref[place_ref[1]] + got_ref[place_ref[1]].astype(F32)

    return pl.pallas_call(
        body, name="chip_sum",
        grid_spec=pltpu.PrefetchScalarGridSpec(
            num_scalar_prefetch=1, grid=(nb,),
            in_specs=[pl.BlockSpec((N_CHIPS, tr, cc), lambda i, pr: (0, pr[0] * nb + i, 0)),
                      pl.BlockSpec((N_CHIPS, tr, cc), lambda i, pr: (0, i, 0))],
            out_specs=[pl.BlockSpec((tr, cc), lambda i, pr: (i, 0)),
                       pl.BlockSpec((N_CHIPS, tr, cc), lambda i, pr: (0, i, 0))]),
        out_shape=[_sds((hh, cc), F32), _sds((N_CHIPS, hh, cc), BF16)],
        compiler_params=_cp("parallel"),
    )(place, g32, got)


def _scatter_copies(s_ref, land_ref, place):
    x, y, c = place
    return [(s_ref.at[2 * cx + cy], land_ref.at[j], (cx, cy, c)) for j, (cx, cy) in enumerate(_other_chips(x, y))]


def scatter_start(s, tag):
    return _exchange_start(f"scatter_start_{tag}", s, lax.empty((N_ICI,) + s.shape[1:], s.dtype), _scatter_copies, N_ICI)


def scatter_wait(sems, s, land, after, tag):
    return _exchange_wait(f"scatter_wait_{tag}", sems, s, land, after,
                          lambda s_ref, land_ref, place: [(a, b) for a, b, _ in _scatter_copies(s_ref, land_ref, place)])


def final_sum(place, own, got):
    hh, cc = own.shape
    tr = SUM_ROWS
    assert hh % tr == 0, (hh, tr)
    nb = hh // tr

    def body(place_ref, own_ref, got_ref, o_ref):
        del place_ref
        o_ref[...] = ((own_ref[...] + got_ref[0].astype(F32)) + got_ref[1].astype(F32)) + got_ref[2].astype(F32)

    return pl.pallas_call(
        body, name="final_sum",
        grid_spec=pltpu.PrefetchScalarGridSpec(
            num_scalar_prefetch=1, grid=(nb,),
            in_specs=[pl.BlockSpec((tr, cc), lambda i, pr: (i, 0)), pl.BlockSpec((3, tr, cc), lambda i, pr: (0, i, 0))],
            out_specs=pl.BlockSpec((tr, cc), lambda i, pr: (pr[0] * nb + i, 0))),
        out_shape=_sds((2 * hh, cc), F32),
        compiler_params=_cp("parallel"),
    )(place, own, got)


def share_start(f, tag):
    def copies(_, f_ref, place):
        x, y, c = place
        mine = f_ref.at[_halves(c, f.shape[0] // 2)[0]]
        return [(mine, mine, (x, y, 1 - c))]

    return _exchange_start(f"share_start_{tag}", jnp.zeros((8, LANES), F32), f, copies, 1)


def share_wait(sems, src, f, after, tag):
    def arrivals(_, f_ref, place):
        mine, other = _halves(place[2], f.shape[0] // 2)
        return [(f_ref.at[mine], f_ref.at[other])]

    return _exchange_wait(f"share_wait_{tag}", sems, src, f, after, arrivals)


N_DEV = 8


def _peers(place):
    x, y, c = place
    return [((1 - x) if r & 4 else x, (1 - y) if r & 2 else y, (1 - c) if r & 1 else c) for r in range(1, N_DEV)]


def _device_index(place):
    x, y, c = place
    return 4 * x + 2 * y + c


def small_start(land, tag):
    def copies(_, land_ref, place):
        mine = land_ref.at[_device_index(place)]
        return [(mine, mine, to) for to in _peers(place)]

    return _exchange_start(f"small_start_{tag}", jnp.zeros((8, LANES), F32), land, copies, N_DEV - 1)


def small_wait(sems, src, land, after, tag):
    def arrivals(_, land_ref, place):
        return [(land_ref.at[_device_index(place)], land_ref.at[_device_index(peer)]) for peer in _peers(place)]

    return _exchange_wait(f"small_wait_{tag}", sems, src, land, after, arrivals)


def sum_devices(land):
    _, rr, cc = land.shape
    tr = 56
    assert rr % tr == 0, rr

    def body(l_ref, o_ref):
        acc = l_ref[0]
        for d in range(1, N_DEV):
            acc = acc + l_ref[d]
        o_ref[...] = acc

    return pl.pallas_call(
        body, name="sum_devices", grid=(rr // tr,),
        in_specs=[pl.BlockSpec((N_DEV, tr, cc), lambda i: (0, i, 0))],
        out_specs=pl.BlockSpec((tr, cc), lambda i: (i, 0)), out_shape=_sds((rr, cc), F32),
        compiler_params=_cp("parallel"),
    )(land)


_BIG = ["mla_w_down", "mla_w_uq", "mla_w_ukv", "mla_w_out", "gmlp_w_in", "gmlp_w_out", "ffn_w_up", "ffn_w_down",
        "ple_w_gate", "ple_w_proj"]
_SMALL_REST = ["norm_mix", "norm_ffn", "norm_ple", "mla_q_lora_g", "mla_kv_lora_g", "mla_q_nope_g", "mla_q_rope_g",
               "mla_k_nope_g", "mla_k_rope_g"]
_SMALL_GMLP = ["gmlp_ln_g", "gmlp_ln_b", "gmlp_w_s", "gmlp_b_s"]
_SMALL = _SMALL_REST + _SMALL_GMLP

_LAY_MLA = dict(up=0, down=1024, out=2048, gate=2304, wdn=2560, wuq=2736, wukv=2880, proj=3008, rows=3072)
_LAY_MLA_MAIN = dict(up=0, down=1024, out=2048, gate=2304, rows=2560)
_LAY_MLA_ODD = dict(wdn=0, wuq=176, wukv=320, proj=448, rows=512)
_LAY_GMLP = {"up": 0, "down": 1024, "in": 2048, "out": 3072, "gate": 3584, "proj": 3840, "ln": 3904, "rows": 4096}
SPLIT_LAYERS = (0,)


def _layer_units(i):
    j = i // 2
    if i % 2 == 0:
        odd, lay = (_LAY_MLA_ODD, _LAY_MLA_MAIN) if i in SPLIT_LAYERS else (_LAY_MLA, _LAY_MLA)
        small = [("mla_w_down", j, odd["wdn"]), ("mla_w_uq", j, odd["wuq"]), ("mla_w_ukv", j, odd["wukv"]),
                 ("ple_w_proj", i, odd["proj"])]
        large = [("ffn_w_up", i, lay["up"]), ("ffn_w_down", i, lay["down"]), ("mla_w_out", j, lay["out"]),
                 ("ple_w_gate", i, lay["gate"])]
        return [("odd", odd, small), ("main", lay, large)] if i in SPLIT_LAYERS else [("main", lay, large + small)]
    lay = _LAY_GMLP
    return [("main", lay, [("ffn_w_up", i, lay["up"]), ("ffn_w_down", i, lay["down"]), ("gmlp_w_in", j, lay["in"]),
                           ("gmlp_w_out", j, lay["out"]), ("ple_w_gate", i, lay["gate"]),
                           ("ple_w_proj", i, lay["proj"])])]


def _pack_rows(parts, dtype, pad_to=None, slot=False):
    size = sum(p.size for p in parts)
    tail = [] if pad_to is None or pad_to * D == size else [jnp.zeros((pad_to * D - size,), dtype)]
    shape = (1, -1, D) if slot else (-1, D)
    if all(p.size % D == 0 for p in parts + tail):
        return jnp.concatenate([p.astype(dtype).reshape(shape) for p in parts + tail], axis=len(shape) - 2)
    return jnp.concatenate([p.astype(dtype).reshape(-1) for p in parts + tail]).reshape(shape)


def _odd(allw, row0, a, b):
    return allw[:, row0:row0 + a * b // D].reshape(N_CHIPS, a, b)


def _cols_joined(s):
    return jnp.transpose(s, (1, 0, 2)).reshape(s.shape[1], N_CHIPS * s.shape[2])


def _col_shards(full):
    a, bb = full.shape
    return jnp.transpose(full.reshape(a, N_CHIPS, bb // N_CHIPS), (1, 0, 2)).reshape(N_CHIPS, -1, D)


def _pad_lanes(g):
    return jnp.pad(g, ((0, 0), (0, LANES - g.shape[1])))


def _split_uq(wuq):
    l = wuq.shape[0]
    w = wuq.reshape(l, QL, HEADS, DN + DR)
    nope = w[..., :DN].reshape(l, QL, HEADS * DN)
    rope = jnp.pad(w[..., DN:], ((0, 0), (0, 0), (0, 0), (0, LANES - DR))).reshape(l, QL, HEADS * LANES)
    return jnp.concatenate([nope, rope], axis=-1)


def _merge_uq(d):
    nope = d[:, :HEADS * DN].reshape(QL, HEADS, DN)
    rope = d[:, HEADS * DN:].reshape(QL, HEADS, LANES)[..., :DR]
    return jnp.concatenate([nope, rope], axis=-1).reshape(QL, HEADS * (DN + DR))


def _rope_tables(positions):
    inv_freq = ROPE_BASE ** (-(jnp.arange(0, DR, 2, dtype=F32) / DR))
    ang = positions.reshape(-1).astype(F32)[:, None] * inv_freq
    z = jnp.zeros((ang.shape[0], LANES - DR), F32)
    return (jnp.concatenate([jnp.cos(ang), jnp.cos(ang), z], axis=1),
            jnp.concatenate([jnp.sin(ang), jnp.sin(ang), z], axis=1))


def kernel(x, p, positions, norm_mix, norm_ffn, norm_ple, mla_w_down, mla_q_lora_g, mla_kv_lora_g, mla_w_uq, mla_w_ukv, mla_q_nope_g, mla_q_rope_g, mla_k_nope_g, mla_k_rope_g, mla_w_out, gmlp_w_in, gmlp_ln_g, gmlp_ln_b, gmlp_w_s, gmlp_b_s, gmlp_w_out, ffn_w_up, ffn_w_down, ple_w_gate, ple_w_proj, loss_target, m_norm_mix, m_norm_ffn, m_norm_ple, m_mla_w_down, m_mla_q_lora_g, m_mla_kv_lora_g, m_mla_w_uq, m_mla_w_ukv, m_mla_q_nope_g, m_mla_q_rope_g, m_mla_k_nope_g, m_mla_k_rope_g, m_mla_w_out, m_gmlp_w_in, m_gmlp_ln_g, m_gmlp_ln_b, m_gmlp_w_s, m_gmlp_b_s, m_gmlp_w_out, m_ffn_w_up, m_ffn_w_down, m_ple_w_gate, m_ple_w_proj, v_norm_mix, v_norm_ffn, v_norm_ple, v_mla_w_down, v_mla_q_lora_g, v_mla_kv_lora_g, v_mla_w_uq, v_mla_w_ukv, v_mla_q_nope_g, v_mla_q_rope_g, v_mla_k_nope_g, v_mla_k_rope_g, v_mla_w_out, v_gmlp_w_in, v_gmlp_ln_g, v_gmlp_ln_b, v_gmlp_w_s, v_gmlp_b_s, v_gmlp_w_out, v_ffn_w_up, v_ffn_w_down, v_ple_w_gate, v_ple_w_proj):
    args = dict(locals())
    weights = {n: args[n] for n in _BIG + _SMALL}
    depth = norm_mix.shape[0]
    nb, seq, _ = x.shape
    t = nb * seq
    assert seq % TQ == 0 and seq % TM == 0 and t % 512 == 0, (nb, seq)
    cx = lax.axis_index("x")
    cy = lax.axis_index("y")
    cc = lax.axis_index("c")
    chip = 2 * cx + cy

    gathers = {}
    token = None
    for i in range(depth):
        for key, lay, parts in _layer_units(i):
            rows = [weights[n][l] for n, l, _ in parts]
            if token is not None:
                rows[0] = rows[0] + token[0, 0]
            if "ln" in lay:
                ln = jnp.stack([gmlp_ln_g[i // 2], gmlp_ln_b[i // 2]]).astype(F32)
                bits = lax.bitcast_convert_type(ln, BF16).reshape(-1)
                rows.append(jnp.pad(bits, (0, 16 * D - bits.size)).reshape(16, D))
            mine = _pack_rows(rows, BF16, pad_to=lay["rows"], slot=True)
            land = lax.dynamic_update_slice(lax.empty((N_CHIPS, lay["rows"], D), BF16), mine, (chip, 0, 0))
            sems, src, land, token = gather_start(land, f"{i}{key}")
            gathers[i, key] = (sems, src, land)
    allw = [None] * depth

    tril = jnp.tril(jnp.ones((GC, GC), F32))
    wm = (gmlp_w_s * tril).astype(BF16)
    wmt = jnp.swapaxes(wm, -1, -2)
    bfull = jnp.repeat(jnp.swapaxes(gmlp_b_s, -1, -2), GD, axis=-1)
    cos, sin = _rope_tables(positions)
    row = lambda g: g.reshape(1, -1)
    gqr = _pad_lanes(mla_q_rope_g)
    gkr = _pad_lanes(mla_k_rope_g)

    h = x.reshape(t, D)
    pt = p.reshape(depth, t, PLE)
    saved = []

    passing = {}

    def arrive(i, key, after):
        sems, src, land = gathers[i, key]
        _, land = gather_wait(sems, src, land, after, f"{i}{key}")
        passing[i, key] = pass_start(land, f"{i}{key}")
        return passing[i, key][3]

    def needed(i, key, after=None):
        sems, src, land, tok = passing.pop((i, key))
        return pass_wait(sems, src, land, tok if after is None else after, f"{i}{key}")[1]

    arrive(0, _layer_units(0)[0][0], token)
    for i in range(depth):
        j = i // 2
        lay = _layer_units(i)[-1][1]
        s = dict(h=h)
        if i % 2 == 0:
            split = i in SPLIT_LAYERS
            olay = _layer_units(i)[0][1]
            odd = needed(i, "odd" if split else "main", None if i == 0 else h)
            wdn = jnp.pad(_odd(odd, olay["wdn"], D // N_CHIPS, LAT).reshape(D, LAT), ((0, 0), (0, LATP - LAT)))
            wuq = _split_uq(_cols_joined(_odd(odd, olay["wuq"], QL, 384))[None])[0]
            wukv = _cols_joined(_odd(odd, olay["wukv"], KVL, 512))
            wp = _cols_joined(_odd(odd, olay["proj"], PLE, 256))
            mla_args = (row(norm_mix[i]), wdn, row(mla_q_lora_g[j]), row(mla_kv_lora_g[j]), wuq, wukv,
                        row(mla_q_nope_g[j]), gqr[j:j + 1], row(mla_k_nope_g[j]), gkr[j:j + 1], cos, sin)
            q, k, v = mla_pre_fwd(h, *mla_args)
            y, lse = flash_fwd(q, k, v, seq)
            if split and i == 0:
                arrive(i, "main", y)
            aw = needed(i, "main", y) if split else odd
            s.update(q=q, k=k, v=v, lse=lse, mla_args=mla_args)
        else:
            aw = needed(i, "main", h)
            ln = lax.bitcast_convert_type(aw[:, lay["ln"]:lay["ln"] + 2].reshape(N_CHIPS, 2, GH // N_CHIPS, 2), F32)
            ln = jnp.transpose(ln, (1, 0, 2)).reshape(2, 1, GH)
            wp = _cols_joined(_odd(aw, lay["proj"], PLE, 256))
            y, pre = gmlp_fwd(h, row(norm_mix[i]), aw, lay, ln[0], ln[1], wm[j], bfull[j])
            s.update(pre=pre, ln=ln)
        allw[i] = aw
        g2 = row(norm_ffn[i])
        if i + 1 < depth:
            for key, _, _ in _layer_units(i + 1):
                g2 = g2 + arrive(i + 1, key, y)[0:1, 0:1]
        h1, h2, hn2, r = mixffn_fwd(h, y, aw, lay, g2)
        h, hn3 = ple_fwd(h2, pt, i, row(norm_ple[i]), aw, lay, wp)
        s.update(y=y, wp=wp, h1=h1, h2=h2, hn2=hn2, r=r, hn3=hn3)
        saved.append(s)

    dh, loss_part = loss_head(h, loss_target.reshape(t, D))

    gs = {n: [None] * weights[n].shape[0] for n in _SMALL}
    gw = {n: [None] * weights[n].shape[0] for n in _BIG}
    place = jnp.stack([cc, chip]).astype(jnp.int32)
    scatters = []
    swaps = []
    token = None

    def put(b, row0, shards):
        return lax.dynamic_update_slice(b, shards.reshape(N_CHIPS, -1, D), (0, row0, 0))

    def small_size(n):
        return weights[n].shape[0] * GH if n in ("gmlp_ln_g", "gmlp_ln_b") else weights[n].size

    def small_exchange(names, zero, tag, extra=()):
        rows = -(-(sum(small_size(n) for n in names) + sum(e.size for e in extra)) // (56 * D)) * 56
        part = [jnp.stack(gs[n]) for n in names] + list(extra)
        part = _pack_rows([part[0] + zero] + part[1:], F32, pad_to=rows, slot=True)
        land = lax.dynamic_update_slice(lax.empty((N_DEV, rows, D), F32), part, (2 * chip + cc, 0, 0))
        return small_start(land, tag)

    def swap(i, key, buf):
        sems, buf, got, tok = swap_start(buf, f"{i}{key}")
        swaps.append((i, key, sems, buf, got))
        return tok

    def swapped(after, zero):
        while swaps:
            i, key, sems, g, got = swaps.pop(0)
            g, got = swap_wait(sems, g, got, after, f"{i}{key}")
            own, sums = chip_sum(place, g, got)
            sems, sums, land, tok = scatter_start(sums, f"{i}{key}")
            scatters.append((i, key, own, sems, sums, land))
            zero = zero + tok[0:1, 0:1]
        return zero

    for i in reversed(range(depth)):
        j = i // 2
        lay = _layer_units(i)[-1][1]
        aw = allw[i]
        s = saved[i]

        g3 = row(norm_ple[i])
        if token is not None:
            g3 = g3 + token[0:1, 0:1]
        dh2, dgt, dpp, dg3 = ple_bwd(dh, s["h2"], pt, i, g3, aw, lay, s["wp"])
        gs["norm_ple"][i] = dg3[0]
        buf = mm_tn_into(lay["rows"], s["hn3"], dgt, D // N_CHIPS, lay["gate"], False)
        dproj = _col_shards(mm_tn(pt, dpp, layer=i))
        if "ln" in lay:
            buf = put(buf, lay["ln"], jnp.zeros((N_CHIPS, lay["rows"] - lay["ln"], D), F32))
            buf = put(buf, lay["proj"], dproj)
        dh1, dh1b, du, a, dg2, dh2b, *do = ffn_bwd(dh2, s["h1"], s["r"], row(norm_ffn[i]), aw, lay,
                                                   mixer_rows=D // N_CHIPS if i % 2 == 0 else None)
        gs["norm_ffn"][i] = dg2[0]
        buf = mm_tn_into(buf, a, dh2b, D, lay["down"], False)
        buf = mm_tn_into(buf, s["hn2"], du, D, lay["up"], True)
        buf = mm_tn_into(buf, s["y"], dh1b, s["y"].shape[1] // N_CHIPS, lay["out"], False)
        g1 = swapped(dh1, row(norm_mix[i]))
        if i % 2 == 0:
            split = i in SPLIT_LAYERS
            dq, dk, dv = flash_bwd(s["q"], s["k"], s["v"], s["y"], do[0], s["lse"], seq,
                                   after=swap(i, "main", buf) if split else dh1b)
            g1 = swapped(dq, g1)
            (dh, hn1, cq, ckv, dqp, dkvp, dlat, dg1, dgq, dgkv, dgqn, dgqr, dgkn, dgkr) = mla_pre_bwd(
                dq, dk, dv, dh1, s["h"], g1, *s["mla_args"][1:])
            gs["norm_mix"][i] = dg1[0]
            gs["mla_q_lora_g"][j] = dgq[0]
            gs["mla_kv_lora_g"][j] = dgkv[0]
            gs["mla_q_nope_g"][j] = dgqn[0]
            gs["mla_q_rope_g"][j] = dgqr[0, :DR]
            gs["mla_k_nope_g"][j] = dgkn[0]
            gs["mla_k_rope_g"][j] = dgkr[0, :DR]
            small = [mm_tn(hn1, dlat)[:, :LAT].reshape(N_CHIPS, -1, D), _col_shards(_merge_uq(mm_tn(cq, dqp))),
                     _col_shards(mm_tn(ckv, dkvp)), dproj]
            if split:
                buf = jnp.concatenate(small, axis=1)
            else:
                buf = put(buf, lay["wdn"], jnp.concatenate(small, axis=1))
            key = "odd" if split else "main"
        else:
            dh, hn1, dpre, dws, dbs, dlng, dlnb, dg1 = gmlp_bwd(
                dh1, dh1b, s["h"], s["pre"], g1, aw, lay, s["ln"][0], s["ln"][1], wm[j], wmt[j], bfull[j], tril)
            gs["norm_mix"][i] = dg1[0]
            gs["gmlp_ln_g"][j] = dlng[0]
            gs["gmlp_ln_b"][j] = dlnb[0]
            gs["gmlp_w_s"][j] = dws
            gs["gmlp_b_s"][j] = jnp.sum(dbs.reshape(GC, GG, GD), axis=-1).T
            buf = mm_tn_into(buf, hn1, dpre, D, lay["in"], True)
            key = "main"
        token = swap(i, key, buf)
        if i == 1:
            small_gmlp = small_exchange(_SMALL_GMLP, token[0, 0], "gmlp")
            token = token + small_gmlp[3]
    last = swapped(dh, jnp.zeros((1, 1), F32))
    grad_x = dh.reshape(x.shape)
    small_rest = small_exchange(_SMALL_REST, last[0, 0], "rest", extra=[loss_part[0, 0:1]])

    after = small_rest[3]
    shares = []
    for i, key, own, sems, sums, land in scatters:
        _, got = scatter_wait(sems, sums, land, after, f"{i}{key}")
        sems, src, full, after = share_start(final_sum(place, own, got), f"{i}{key}")
        shares.append((i, key, sems, src, full))
    where = {n: [None] * weights[n].shape[0] for n in _BIG}
    for i, key, sems, src, full in shares:
        _, after = share_wait(sems, src, full, after, f"{i}{key}")
        for n, l, row0 in dict((k, parts) for k, _, parts in _layer_units(i))[key]:
            where[n][l] = (after, row0)
            if weights[n].shape[-1] != D:
                gw[n][l] = after[row0:row0 + weights[n][l].size // D].reshape(weights[n].shape[1:])
    grads = {n: jnp.stack(gw[n]) for n in _BIG if weights[n].shape[-1] != D}

    tot = []
    for names, (sems, src, land, _), tag in ((_SMALL_REST, small_rest, "rest"), (_SMALL_GMLP, small_gmlp, "gmlp")):
        summed = sum_devices(small_wait(sems, src, land, after, tag)[1]).reshape(-1)
        tot.append(summed[:sum(small_size(n) for n in names)])
        if tag == "rest":
            loss = summed[tot[-1].size]
    tot = jnp.concatenate(tot)
    off = 0
    for n, sz in ((n, small_size(n)) for n in _SMALL_REST + _SMALL_GMLP):
        gsum = tot[off:off + sz]
        off += sz
        if n in ("gmlp_ln_g", "gmlp_ln_b"):
            gsum = lax.dynamic_slice_in_dim(gsum.reshape(-1, GH), chip * (GH // N_CHIPS), GH // N_CHIPS, axis=1)
        grads[n] = gsum.reshape(weights[n].shape)

    delta, new_m, new_v = {}, {}, {}
    for n in _BIG:
        if weights[n].shape[-1] == D:
            grads[n], delta[n], new_m[n], new_v[n] = adamw_layers(
                weights[n], args["m_" + n], args["v_" + n], [b for b, _ in where[n]], [r for _, r in where[n]])
            continue
        w2 = weights[n].reshape(-1, weights[n].shape[-1])
        d, mn, vn = adamw(w2, grads[n].reshape(w2.shape), args["m_" + n].reshape(w2.shape),
                          args["v_" + n].reshape(w2.shape))
        delta[n], new_m[n], new_v[n] = (a.reshape(weights[n].shape) for a in (d, mn, vn))
    own_sizes = [weights[n].size for n in _SMALL]
    own_rows = -(-sum(own_sizes) // (8 * D)) * 8
    packed = [_pack_rows([src[n] for n in _SMALL], F32, pad_to=own_rows)
              for src in (weights, grads, {n: args["m_" + n] for n in _SMALL}, {n: args["v_" + n] for n in _SMALL})]
    outs = adamw(*packed)
    off = 0
    for n, sz in zip(_SMALL, own_sizes):
        for dst, o in zip((delta, new_m, new_v), outs):
            dst[n] = o.reshape(-1)[off:off + sz].reshape(weights[n].shape)
        off += sz

    order = ["norm_mix", "norm_ffn", "norm_ple", "mla_w_down", "mla_q_lora_g", "mla_kv_lora_g", "mla_w_uq",
             "mla_w_ukv", "mla_q_nope_g", "mla_q_rope_g", "mla_k_nope_g", "mla_k_rope_g", "mla_w_out", "gmlp_w_in",
             "gmlp_ln_g", "gmlp_ln_b", "gmlp_w_s", "gmlp_b_s", "gmlp_w_out", "ffn_w_up", "ffn_w_down", "ple_w_gate",
             "ple_w_proj"]
    return (loss, grad_x, *[grads[n] for n in order], *[delta[n] for n in order], *[new_m[n] for n in order],
            *[new_v[n] for n in order])
```
